```python
import math
import jax, jax.numpy as jnp
from jax import lax
import numpy as np

D_MODEL = 2048
BATCH = 8
SEQ = 4096
DEPTH = 1

HEAD_DIM = 64
N_Q_HEADS = D_MODEL // 128
N_KV_HEADS = 4
Q_PER_KV = N_Q_HEADS // N_KV_HEADS
ATTN_W = N_Q_HEADS * HEAD_DIM
KV_W = N_KV_HEADS * HEAD_DIM
WINDOW = 128
BLOCK = 128
SSM_W = D_MODEL // 2
GROUP = 16
N_GROUPS = SSM_W // GROUP
STATE = 64
IN_SIZES = (ATTN_W, KV_W, KV_W, ATTN_W, SSM_W, SSM_W, D_MODEL, D_MODEL)
IN_W = sum(IN_SIZES)
NORM_EPS = 1e-6

kernel_name = "hybrid_swa_sink_s5_gated_merge"


def rms_norm(x, w):
    xf = x.astype(jnp.float32)
    y = xf * lax.rsqrt(jnp.mean(xf * xf, axis=-1, keepdims=True) + NORM_EPS)
    return (y * w.astype(jnp.float32)).astype(x.dtype)


def sliding_window_attention(q, k, v, sinks):
    b, l = q.shape[0], q.shape[1]
    nb = l // BLOCK
    qb = q.reshape(b, nb, BLOCK, N_KV_HEADS, Q_PER_KV, HEAD_DIM)
    kb = k.reshape(b, nb, BLOCK, N_KV_HEADS, HEAD_DIM)
    vb = v.reshape(b, nb, BLOCK, N_KV_HEADS, HEAD_DIM)
    k_prev = jnp.concatenate([jnp.zeros_like(kb[:, :1]), kb[:, :-1]], axis=1)
    v_prev = jnp.concatenate([jnp.zeros_like(vb[:, :1]), vb[:, :-1]], axis=1)
    kk = jnp.concatenate([k_prev, kb], axis=2)
    vv = jnp.concatenate([v_prev, vb], axis=2)
    scale = 1.0 / math.sqrt(HEAD_DIM)
    scores = jnp.einsum('bnqgrd,bnsgd->bngrqs', qb, kk).astype(jnp.float32) * scale
    q_loc = jnp.arange(BLOCK)[:, None] + BLOCK
    k_loc = jnp.arange(2 * BLOCK)[None, :]
    diff = q_loc - k_loc
    k_abs = (jnp.arange(nb)[:, None, None] - 1) * BLOCK + k_loc[None]
    valid = (diff >= 0)[None] & (diff < WINDOW)[None] & (k_abs >= 0)
    scores = jnp.where(valid[None, :, None, None], scores, -1e30)
    sink = jnp.broadcast_to(
        sinks.astype(jnp.float32).reshape(1, 1, N_KV_HEADS, Q_PER_KV, 1, 1),
        scores.shape[:-1] + (1,))
    probs = jax.nn.softmax(jnp.concatenate([scores, sink], axis=-1), axis=-1)[..., :-1]
    out = jnp.einsum('bngrqs,bnsgd->bnqgrd', probs.astype(v.dtype), vv)
    return out.reshape(b, l, ATTN_W)


def s5_ssm(u, A_re, A_im, log_dt, B_re, B_im, C_re, C_im, D_skip):
    dt = jnp.exp(log_dt)[:, None]
    mag = jnp.exp(dt * A_re)
    ab_re = mag * jnp.cos(dt * A_im)
    ab_im = mag * jnp.sin(dt * A_im)
    num_re = ab_re - 1.0
    num_im = ab_im
    den = A_re * A_re + A_im * A_im
    cf_re = (num_re * A_re + num_im * A_im) / den
    cf_im = (num_im * A_re - num_re * A_im) / den
    bu_re = jnp.einsum('blgh,gph->blgp', u, B_re)
    bu_im = jnp.einsum('blgh,gph->blgp', u, B_im)
    b_re = cf_re * bu_re - cf_im * bu_im
    b_im = cf_re * bu_im + cf_im * bu_re
    a_re = jnp.broadcast_to(ab_re, b_re.shape)
    a_im = jnp.broadcast_to(ab_im, b_im.shape)

    def combine(e1, e2):
        a1r, a1i, b1r, b1i = e1
        a2r, a2i, b2r, b2i = e2
        return (a2r * a1r - a2i * a1i,
                a2r * a1i + a2i * a1r,
                a2r * b1r - a2i * b1i + b2r,
                a2r * b1i + a2i * b1r + b2i)

    _, _, s_re, s_im = lax.associative_scan(combine, (a_re, a_im, b_re, b_im), axis=1)
    y = (jnp.einsum('blgp,ghp->blgh', s_re, C_re)
         - jnp.einsum('blgp,ghp->blgh', s_im, C_im)
         + D_skip * u)
    return y


def _fwd_setup_inputs(seed: int = 0) -> dict:
    key = jax.random.key(seed)
    ks = jax.random.split(key, 20)
    f32 = jnp.float32
    n = jnp.arange(STATE, dtype=f32)
    x = jax.random.normal(ks[0], (BATCH, SEQ, D_MODEL), f32)
    norm_w = 1.0 + 0.02 * jax.random.normal(ks[1], (D_MODEL,), f32)
    w_in = jax.random.normal(ks[2], (D_MODEL, IN_W), f32) * D_MODEL ** -0.5
    q_norm_w = 1.0 + 0.02 * jax.random.normal(ks[3], (HEAD_DIM,), f32)
    k_norm_w = 1.0 + 0.02 * jax.random.normal(ks[4], (HEAD_DIM,), f32)
    sinks = jax.random.normal(ks[5], (N_Q_HEADS,), f32)
    w_attn_proj = jax.random.normal(ks[6], (ATTN_W, D_MODEL), f32) * ATTN_W ** -0.5
    A_re = -0.5 + 0.01 * jax.random.normal(ks[7], (N_GROUPS, STATE), f32)
    A_im = math.pi * n[None, :] + 0.01 * jax.random.normal(ks[8], (N_GROUPS, STATE), f32)
    log_dt = jax.random.uniform(ks[9], (N_GROUPS,), f32, math.log(1e-3), math.log(1e-1))
    b_scale = (2.0 * GROUP) ** -0.5
    B_re = jax.random.normal(ks[10], (N_GROUPS, STATE, GROUP), f32) * b_scale
    B_im = jax.random.normal(ks[11], (N_GROUPS, STATE, GROUP), f32) * b_scale
    c_scale = (2.0 * STATE) ** -0.5
    C_re = jax.random.normal(ks[12], (N_GROUPS, GROUP, STATE), f32) * c_scale
    C_im = jax.random.normal(ks[13], (N_GROUPS, GROUP, STATE), f32) * c_scale
    D_skip = jax.random.normal(ks[14], (N_GROUPS, GROUP), f32)
    w_glu = jax.random.normal(ks[15], (SSM_W, 2 * SSM_W), f32) * SSM_W ** -0.5
    b_glu = 0.01 * jax.random.normal(ks[16], (2 * SSM_W,), f32)
    w_ssm_proj = jax.random.normal(ks[17], (SSM_W, D_MODEL), f32) * SSM_W ** -0.5
    w_out = jax.random.normal(ks[18], (D_MODEL, D_MODEL), f32) * D_MODEL ** -0.5
    return {"x": x, "norm_w": norm_w, "w_in": w_in, "q_norm_w": q_norm_w,
            "k_norm_w": k_norm_w, "sinks": sinks, "w_attn_proj": w_attn_proj,
            "A_re": A_re, "A_im": A_im, "log_dt": log_dt, "B_re": B_re, "B_im": B_im,
            "C_re": C_re, "C_im": C_im, "D_skip": D_skip, "w_glu": w_glu,
            "b_glu": b_glu, "w_ssm_proj": w_ssm_proj, "w_out": w_out}


def _fwd_reference(x, norm_w, w_in, q_norm_w, k_norm_w, sinks, w_attn_proj, A_re, A_im,
              log_dt, B_re, B_im, C_re, C_im, D_skip, w_glu, b_glu, w_ssm_proj, w_out):
    b, l, _ = x.shape
    split_pts = list(np.cumsum(IN_SIZES)[:-1])
    f32 = jnp.float32
    for _layer in range(DEPTH):
        h = rms_norm(x, norm_w)
        proj = h @ w_in
        q, k, v, a_gate, u, z, g_a, g_s = jnp.split(proj, split_pts, axis=-1)
        q = rms_norm(q.reshape(b, l, N_Q_HEADS, HEAD_DIM), q_norm_w)
        k = rms_norm(k.reshape(b, l, N_KV_HEADS, HEAD_DIM), k_norm_w)
        v = v.reshape(b, l, N_KV_HEADS, HEAD_DIM)
        attn = sliding_window_attention(q, k, v, sinks)
        y_a = (attn * jax.nn.silu(a_gate)) @ w_attn_proj
        u_g = u.reshape(b, l, N_GROUPS, GROUP).astype(f32)
        y_ssm = s5_ssm(u_g, A_re.astype(f32), A_im.astype(f32), log_dt.astype(f32),
                       B_re.astype(f32), B_im.astype(f32), C_re.astype(f32),
                       C_im.astype(f32), D_skip.astype(f32))
        y_ssm = jax.nn.gelu(y_ssm.reshape(b, l, SSM_W)).astype(x.dtype)
        glu_a, glu_b = jnp.split(y_ssm @ w_glu + b_glu, 2, axis=-1)
        y_s = (glu_a * jax.nn.sigmoid(glu_b) * jax.nn.silu(z)) @ w_ssm_proj
        merged = jax.nn.sigmoid(g_a) * y_a + jax.nn.sigmoid(g_s) * y_s
        x = x + merged @ w_out
    return x


import jax as _jax
import jax.numpy as _jnp

TWIN_FORMAT = 'train_step'
FWD_PARAMS = ['x', 'norm_w', 'w_in', 'q_norm_w', 'k_norm_w', 'sinks', 'w_attn_proj', 'A_re', 'A_im', 'log_dt', 'B_re', 'B_im', 'C_re', 'C_im', 'D_skip', 'w_glu', 'b_glu', 'w_ssm_proj', 'w_out']
TWIN_WEIGHTS = ['norm_w', 'w_in', 'q_norm_w', 'k_norm_w', 'sinks', 'w_attn_proj', 'A_re', 'A_im', 'log_dt', 'B_re', 'B_im', 'C_re', 'C_im', 'D_skip', 'w_glu', 'b_glu', 'w_ssm_proj', 'w_out']
TWIN_DIFF_INPUT = 'x'
TWIN_INPUTS = ['x', 'norm_w', 'w_in', 'q_norm_w', 'k_norm_w', 'sinks', 'w_attn_proj', 'A_re', 'A_im', 'log_dt', 'B_re', 'B_im', 'C_re', 'C_im', 'D_skip', 'w_glu', 'b_glu', 'w_ssm_proj', 'w_out', 'loss_target', 'm_norm_w', 'm_w_in', 'm_q_norm_w', 'm_k_norm_w', 'm_sinks', 'm_w_attn_proj', 'm_A_re', 'm_A_im', 'm_log_dt', 'm_B_re', 'm_B_im', 'm_C_re', 'm_C_im', 'm_D_skip', 'm_w_glu', 'm_b_glu', 'm_w_ssm_proj', 'm_w_out', 'v_norm_w', 'v_w_in', 'v_q_norm_w', 'v_k_norm_w', 'v_sinks', 'v_w_attn_proj', 'v_A_re', 'v_A_im', 'v_log_dt', 'v_B_re', 'v_B_im', 'v_C_re', 'v_C_im', 'v_D_skip', 'v_w_glu', 'v_b_glu', 'v_w_ssm_proj', 'v_w_out']
TWIN_OUTPUTS = ['loss', 'grad_x', 'grad_norm_w', 'grad_w_in', 'grad_q_norm_w', 'grad_k_norm_w', 'grad_sinks', 'grad_w_attn_proj', 'grad_A_re', 'grad_A_im', 'grad_log_dt', 'grad_B_re', 'grad_B_im', 'grad_C_re', 'grad_C_im', 'grad_D_skip', 'grad_w_glu', 'grad_b_glu', 'grad_w_ssm_proj', 'grad_w_out', 'delta_norm_w', 'delta_w_in', 'delta_q_norm_w', 'delta_k_norm_w', 'delta_sinks', 'delta_w_attn_proj', 'delta_A_re', 'delta_A_im', 'delta_log_dt', 'delta_B_re', 'delta_B_im', 'delta_C_re', 'delta_C_im', 'delta_D_skip', 'delta_w_glu', 'delta_b_glu', 'delta_w_ssm_proj', 'delta_w_out', 'new_m_norm_w', 'new_m_w_in', 'new_m_q_norm_w', 'new_m_k_norm_w', 'new_m_sinks', 'new_m_w_attn_proj', 'new_m_A_re', 'new_m_A_im', 'new_m_log_dt', 'new_m_B_re', 'new_m_B_im', 'new_m_C_re', 'new_m_C_im', 'new_m_D_skip', 'new_m_w_glu', 'new_m_b_glu', 'new_m_w_ssm_proj', 'new_m_w_out', 'new_v_norm_w', 'new_v_w_in', 'new_v_q_norm_w', 'new_v_k_norm_w', 'new_v_sinks', 'new_v_w_attn_proj', 'new_v_A_re', 'new_v_A_im', 'new_v_log_dt', 'new_v_B_re', 'new_v_B_im', 'new_v_C_re', 'new_v_C_im', 'new_v_D_skip', 'new_v_w_glu', 'new_v_b_glu', 'new_v_w_ssm_proj', 'new_v_w_out']
TWIN_LEAF_KINDS = {'loss': 'loss', 'grad_x': 'grad_x', 'grad_norm_w': 'grad_w', 'grad_w_in': 'grad_w', 'grad_q_norm_w': 'grad_w', 'grad_k_norm_w': 'grad_w', 'grad_sinks': 'grad_w', 'grad_w_attn_proj': 'grad_w', 'grad_A_re': 'grad_w', 'grad_A_im': 'grad_w', 'grad_log_dt': 'grad_w', 'grad_B_re': 'grad_w', 'grad_B_im': 'grad_w', 'grad_C_re': 'grad_w', 'grad_C_im': 'grad_w', 'grad_D_skip': 'grad_w', 'grad_w_glu': 'grad_w', 'grad_b_glu': 'grad_w', 'grad_w_ssm_proj': 'grad_w', 'grad_w_out': 'grad_w', 'delta_norm_w': 'delta_w', 'delta_w_in': 'delta_w', 'delta_q_norm_w': 'delta_w', 'delta_k_norm_w': 'delta_w', 'delta_sinks': 'delta_w', 'delta_w_attn_proj': 'delta_w', 'delta_A_re': 'delta_w', 'delta_A_im': 'delta_w', 'delta_log_dt': 'delta_w', 'delta_B_re': 'delta_w', 'delta_B_im': 'delta_w', 'delta_C_re': 'delta_w', 'delta_C_im': 'delta_w', 'delta_D_skip': 'delta_w', 'delta_w_glu': 'delta_w', 'delta_b_glu': 'delta_w', 'delta_w_ssm_proj': 'delta_w', 'delta_w_out': 'delta_w', 'new_m_norm_w': 'new_m', 'new_m_w_in': 'new_m', 'new_m_q_norm_w': 'new_m', 'new_m_k_norm_w': 'new_m', 'new_m_sinks': 'new_m', 'new_m_w_attn_proj': 'new_m', 'new_m_A_re': 'new_m', 'new_m_A_im': 'new_m', 'new_m_log_dt': 'new_m', 'new_m_B_re': 'new_m', 'new_m_B_im': 'new_m', 'new_m_C_re': 'new_m', 'new_m_C_im': 'new_m', 'new_m_D_skip': 'new_m', 'new_m_w_glu': 'new_m', 'new_m_b_glu': 'new_m', 'new_m_w_ssm_proj': 'new_m', 'new_m_w_out': 'new_m', 'new_v_norm_w': 'new_v', 'new_v_w_in': 'new_v', 'new_v_q_norm_w': 'new_v', 'new_v_k_norm_w': 'new_v', 'new_v_sinks': 'new_v', 'new_v_w_attn_proj': 'new_v', 'new_v_A_re': 'new_v', 'new_v_A_im': 'new_v', 'new_v_log_dt': 'new_v', 'new_v_B_re': 'new_v', 'new_v_B_im': 'new_v', 'new_v_C_re': 'new_v', 'new_v_C_im': 'new_v', 'new_v_D_skip': 'new_v', 'new_v_w_glu': 'new_v', 'new_v_b_glu': 'new_v', 'new_v_w_ssm_proj': 'new_v', 'new_v_w_out': 'new_v'}


def _forward(args):
    return _fwd_reference(*[args[k] for k in FWD_PARAMS])


def _output_shape():
    def fwd():
        inp = _fwd_setup_inputs(0)
        return _fwd_reference(*[inp[k] for k in FWD_PARAMS])
    out = _jax.eval_shape(fwd)
    return out.shape, out.dtype

N_MICROBATCH = 1
ADAM_LR = 0.001
ADAM_B1 = 0.9
ADAM_B2 = 0.999
ADAM_EPS = 1e-08
ADAM_WD = 0.01
ADAM_STEP = 10
PER_EXAMPLE_BATCH_AXIS = {'x': 0, 'loss_target': 0}
SHARED_INPUTS = []
_WEIGHT_DTYPES = {'norm_w': _jnp.float32, 'w_in': _jnp.float32, 'q_norm_w': _jnp.float32, 'k_norm_w': _jnp.float32, 'sinks': _jnp.float32, 'w_attn_proj': _jnp.float32, 'A_re': _jnp.float32, 'A_im': _jnp.float32, 'log_dt': _jnp.float32, 'B_re': _jnp.float32, 'B_im': _jnp.float32, 'C_re': _jnp.float32, 'C_im': _jnp.float32, 'D_skip': _jnp.float32, 'w_glu': _jnp.float32, 'b_glu': _jnp.float32, 'w_ssm_proj': _jnp.float32, 'w_out': _jnp.float32}
MOMENT_SCALE = {'norm_w': 5.415652e-01, 'w_in': 1.454204e-02, 'q_norm_w': 7.166632e-01, 'k_norm_w': 7.176504e-01, 'sinks': 1.346698e-01, 'w_attn_proj': 8.135552e-03, 'A_re': 1.261739e-03, 'A_im': 1.021121e-03, 'log_dt': 9.567169e-01, 'B_re': 9.767128e-04, 'B_im': 9.783163e-04, 'C_re': 1.907451e-03, 'C_im': 1.941580e-03, 'D_skip': 5.021597e-01, 'w_glu': 7.813848e-02, 'b_glu': 2.550773e-01, 'w_ssm_proj': 2.072099e-02, 'w_out': 1.991139e-02}


def _to_microbatches(a, axis):
    t = _jnp.moveaxis(a, axis, 0)
    t = t.reshape((N_MICROBATCH, t.shape[0] // N_MICROBATCH) + t.shape[1:])
    return _jnp.moveaxis(t, 1, axis + 1)


def setup_inputs(seed: int = 0) -> dict:
    inp = _fwd_setup_inputs(seed)
    key = _jax.random.fold_in(_jax.random.key(seed), 7919)
    shape, _ = _output_shape()
    out = dict(inp)
    out["loss_target"] = _jax.random.normal(_jax.random.fold_in(key, 0), shape, _jnp.float32)
    for i, name in enumerate(TWIN_WEIGHTS):
        w = inp[name].astype(_jnp.float32)
        if MOMENT_SCALE is None:
            s = _jnp.sqrt(_jnp.mean(_jnp.square(w)) + 1e-30)
        else:
            s = MOMENT_SCALE[name]
        km, kv = _jax.random.split(_jax.random.fold_in(key, i + 1))
        out[name] = w
        out["m_" + name] = s * _jax.random.normal(km, w.shape, _jnp.float32)
        out["v_" + name] = (s * s) * _jax.random.uniform(kv, w.shape, _jnp.float32, 0.5, 1.5)
    if N_MICROBATCH > 1:
        for name, axis in PER_EXAMPLE_BATCH_AXIS.items():
            out[name] = _to_microbatches(out[name], axis)
    return {'x': out['x'], 'norm_w': out['norm_w'], 'w_in': out['w_in'], 'q_norm_w': out['q_norm_w'], 'k_norm_w': out['k_norm_w'], 'sinks': out['sinks'], 'w_attn_proj': out['w_attn_proj'], 'A_re': out['A_re'], 'A_im': out['A_im'], 'log_dt': out['log_dt'], 'B_re': out['B_re'], 'B_im': out['B_im'], 'C_re': out['C_re'], 'C_im': out['C_im'], 'D_skip': out['D_skip'], 'w_glu': out['w_glu'], 'b_glu': out['b_glu'], 'w_ssm_proj': out['w_ssm_proj'], 'w_out': out['w_out'], 'loss_target': out['loss_target'], 'm_norm_w': out['m_norm_w'], 'm_w_in': out['m_w_in'], 'm_q_norm_w': out['m_q_norm_w'], 'm_k_norm_w': out['m_k_norm_w'], 'm_sinks': out['m_sinks'], 'm_w_attn_proj': out['m_w_attn_proj'], 'm_A_re': out['m_A_re'], 'm_A_im': out['m_A_im'], 'm_log_dt': out['m_log_dt'], 'm_B_re': out['m_B_re'], 'm_B_im': out['m_B_im'], 'm_C_re': out['m_C_re'], 'm_C_im': out['m_C_im'], 'm_D_skip': out['m_D_skip'], 'm_w_glu': out['m_w_glu'], 'm_b_glu': out['m_b_glu'], 'm_w_ssm_proj': out['m_w_ssm_proj'], 'm_w_out': out['m_w_out'], 'v_norm_w': out['v_norm_w'], 'v_w_in': out['v_w_in'], 'v_q_norm_w': out['v_q_norm_w'], 'v_k_norm_w': out['v_k_norm_w'], 'v_sinks': out['v_sinks'], 'v_w_attn_proj': out['v_w_attn_proj'], 'v_A_re': out['v_A_re'], 'v_A_im': out['v_A_im'], 'v_log_dt': out['v_log_dt'], 'v_B_re': out['v_B_re'], 'v_B_im': out['v_B_im'], 'v_C_re': out['v_C_re'], 'v_C_im': out['v_C_im'], 'v_D_skip': out['v_D_skip'], 'v_w_glu': out['v_w_glu'], 'v_b_glu': out['v_b_glu'], 'v_w_ssm_proj': out['v_w_ssm_proj'], 'v_w_out': out['v_w_out']}


def _loss(weights, diff, rest, loss_target):
    with _jax.named_scope("forward"):
        args = {**rest, TWIN_DIFF_INPUT: diff, **{k: w.astype(_WEIGHT_DTYPES[k]) for k, w in weights.items()}}
        y = _forward(args)
    with _jax.named_scope("loss_head"):
        err = _jnp.square(y.astype(_jnp.float32) - loss_target)
        return 0.5 * _jnp.sum(_jnp.mean(err, axis=-1)) if err.ndim else 0.5 * err


def _adamw(w, g, m, v):
    m = ADAM_B1 * m + (1.0 - ADAM_B1) * g
    v = ADAM_B2 * v + (1.0 - ADAM_B2) * _jnp.square(g)
    m_hat = m / (1.0 - ADAM_B1 ** ADAM_STEP)
    v_hat = v / (1.0 - ADAM_B2 ** ADAM_STEP)
    delta = -ADAM_LR * (m_hat / (_jnp.sqrt(v_hat) + ADAM_EPS) + ADAM_WD * w)
    return delta, m, v


def reference(x, norm_w, w_in, q_norm_w, k_norm_w, sinks, w_attn_proj, A_re, A_im, log_dt, B_re, B_im, C_re, C_im, D_skip, w_glu, b_glu, w_ssm_proj, w_out, loss_target, m_norm_w, m_w_in, m_q_norm_w, m_k_norm_w, m_sinks, m_w_attn_proj, m_A_re, m_A_im, m_log_dt, m_B_re, m_B_im, m_C_re, m_C_im, m_D_skip, m_w_glu, m_b_glu, m_w_ssm_proj, m_w_out, v_norm_w, v_w_in, v_q_norm_w, v_k_norm_w, v_sinks, v_w_attn_proj, v_A_re, v_A_im, v_log_dt, v_B_re, v_B_im, v_C_re, v_C_im, v_D_skip, v_w_glu, v_b_glu, v_w_ssm_proj, v_w_out):
    given = dict(x=x, norm_w=norm_w, w_in=w_in, q_norm_w=q_norm_w, k_norm_w=k_norm_w, sinks=sinks, w_attn_proj=w_attn_proj, A_re=A_re, A_im=A_im, log_dt=log_dt, B_re=B_re, B_im=B_im, C_re=C_re, C_im=C_im, D_skip=D_skip, w_glu=w_glu, b_glu=b_glu, w_ssm_proj=w_ssm_proj, w_out=w_out, loss_target=loss_target, m_norm_w=m_norm_w, m_w_in=m_w_in, m_q_norm_w=m_q_norm_w, m_k_norm_w=m_k_norm_w, m_sinks=m_sinks, m_w_attn_proj=m_w_attn_proj, m_A_re=m_A_re, m_A_im=m_A_im, m_log_dt=m_log_dt, m_B_re=m_B_re, m_B_im=m_B_im, m_C_re=m_C_re, m_C_im=m_C_im, m_D_skip=m_D_skip, m_w_glu=m_w_glu, m_b_glu=m_b_glu, m_w_ssm_proj=m_w_ssm_proj, m_w_out=m_w_out, v_norm_w=v_norm_w, v_w_in=v_w_in, v_q_norm_w=v_q_norm_w, v_k_norm_w=v_k_norm_w, v_sinks=v_sinks, v_w_attn_proj=v_w_attn_proj, v_A_re=v_A_re, v_A_im=v_A_im, v_log_dt=v_log_dt, v_B_re=v_B_re, v_B_im=v_B_im, v_C_re=v_C_re, v_C_im=v_C_im, v_D_skip=v_D_skip, v_w_glu=v_w_glu, v_b_glu=v_b_glu, v_w_ssm_proj=v_w_ssm_proj, v_w_out=v_w_out)
    weights = {n: given[n] for n in TWIN_WEIGHTS}
    shared = {n: given[n] for n in SHARED_INPUTS}
    per_example = {n: given[n] for n in ['x']}
    grad_fn = _jax.value_and_grad(_loss, argnums=(0, 1))

    def one_microbatch(ex, loss_target):
        ex = dict(ex)
        diff = ex.pop(TWIN_DIFF_INPUT)
        return grad_fn(weights, diff, {**shared, **ex}, loss_target)

    if N_MICROBATCH == 1:
        loss, (grad_w, grad_x) = one_microbatch(per_example, given["loss_target"])
    else:
        def body(carry, xs):
            loss_sum, grad_sum = carry
            l_k, (gw_k, gx_k) = one_microbatch(xs[0], xs[1])
            with _jax.named_scope("update"):
                return (loss_sum + l_k, _jax.tree.map(_jnp.add, grad_sum, gw_k)), gx_k

        init = (_jnp.zeros((), _jnp.float32), _jax.tree.map(_jnp.zeros_like, weights))
        (loss, grad_w), grad_x = _jax.lax.scan(body, init, (per_example, given["loss_target"]))
    with _jax.named_scope("update"):
        delta_w, new_m, new_v = {}, {}, {}
        for n in TWIN_WEIGHTS:
            delta_w[n], new_m[n], new_v[n] = _adamw(weights[n], grad_w[n], given["m_" + n], given["v_" + n])
    return (loss, grad_x, *[grad_w[n] for n in TWIN_WEIGHTS], *[delta_w[n] for n in TWIN_WEIGHTS],
            *[new_m[n] for n in TWIN_WEIGHTS], *[new_v[n] for n in TWIN_WEIGHTS])
```

```python
import functools
import math

import jax
import jax.numpy as jnp
from jax import lax
from jax.experimental import pallas as pl
from jax.experimental.pallas import tpu as pltpu

F32 = jnp.float32
_MXU = jnp.bfloat16
_WIRE = jnp.bfloat16

LANES = 128
SUBLANES = 8
VMEM_LIMIT = 56 * 1024 * 1024

D_MODEL = 2048
HEAD_DIM = 64
N_Q_HEADS = 16
N_KV_HEADS = 4
Q_PER_KV = 4
ATTN_W = 1024
KV_W = 256
WINDOW = 128
SSM_W = 1024
GROUP = 16
N_GROUPS = 64
STATE = 64
N_STATES = N_GROUPS * STATE
IN_W = 8704
NORM_EPS = 1e-6
N_CHIPS = 4
CW = 512
OFF_AGATE, OFF_U, OFF_Z, OFF_GA, OFF_GS = 3, 5, 7, 9, 13

SSM_T = 256
SSM_L = SSM_T // SUBLANES
SSM_JB = 8
SSM_SB = N_STATES // SSM_JB

ADAM_LR, ADAM_B1, ADAM_B2, ADAM_EPS, ADAM_WD, ADAM_STEP = 0.001, 0.9, 0.999, 1e-08, 0.01, 10

MESH = pl.DeviceIdType.MESH
_ANY = pl.BlockSpec(memory_space=pl.ANY)


def _params(sem=None):
    return pltpu.CompilerParams(dimension_semantics=sem, vmem_limit_bytes=VMEM_LIMIT)


def _mm(a, b, *, mode, name, tm, tn, tk, out_dtype=F32, b_blocked=False, out_blocked=False):
    if mode == "tn":
        K, M = a.shape
    else:
        M, K = a.shape
    if mode == "nn":
        N = b.shape[0] * b.shape[2] if b_blocked else b.shape[1]
    elif mode == "nt":
        N = b.shape[1] if b_blocked else b.shape[0]
    else:
        N = b.shape[1]
    tm, tn, tk = min(tm, M), min(tn, N), min(tk, K)
    nj, ni, nk = N // tn, M // tm, K // tk
    assert nj * tn == N and ni * tm == M and nk * tk == K, (name, M, N, K)
    dims = {"nn": (((1,), (0,)), ((), ())), "nt": (((1,), (1,)), ((), ())), "tn": (((0,), (0,)), ((), ()))}[mode]

    if mode == "tn":
        a_spec = pl.BlockSpec((tk, tm), lambda j, i, k: (k, i))
    else:
        a_spec = pl.BlockSpec((tm, tk), lambda j, i, k: (i, k))
    if mode == "nn":
        if b_blocked:
            assert b.shape[0] == nj and b.shape[2] == tn
            b_spec = pl.BlockSpec((None, tk, tn), lambda j, i, k: (j, k, 0))
        else:
            b_spec = pl.BlockSpec((tk, tn), lambda j, i, k: (k, j))
    elif mode == "nt":
        if b_blocked:
            assert b.shape[0] == nk and b.shape[2] == tk
            b_spec = pl.BlockSpec((None, tn, tk), lambda j, i, k: (k, j, 0))
        else:
            b_spec = pl.BlockSpec((tn, tk), lambda j, i, k: (j, k))
    else:
        b_spec = pl.BlockSpec((tk, tn), lambda j, i, k: (k, j))
    if out_blocked:
        assert nj == N_CHIPS
        o_spec = pl.BlockSpec((None, tm, tn), lambda j, i, k: (j, i, 0))
        o_shape = jax.ShapeDtypeStruct((nj, M, tn), out_dtype)
    else:
        o_spec = pl.BlockSpec((tm, tn), lambda j, i, k: (i, j))
        o_shape = jax.ShapeDtypeStruct((M, N), out_dtype)
    use_acc = nk > 1 and out_dtype != F32

    def body(a_ref, b_ref, o_ref, *scratch):
        part = lax.dot_general(a_ref[...].astype(_MXU), b_ref[...].astype(_MXU), dims,
                               preferred_element_type=F32)
        if nk == 1:
            o_ref[...] = part.astype(o_ref.dtype)
            return
        k = pl.program_id(2)
        acc = scratch[0] if use_acc else o_ref

        @pl.when(k == 0)
        def _():
            acc[...] = part

        @pl.when(k > 0)
        def _():
            acc[...] += part

        if use_acc:
            @pl.when(k == nk - 1)
            def _():
                o_ref[...] = acc[...].astype(o_ref.dtype)

    return pl.pallas_call(
        body, name=name, grid=(nj, ni, nk), in_specs=[a_spec, b_spec], out_specs=o_spec, out_shape=o_shape,
        scratch_shapes=[pltpu.VMEM((tm, tn), F32)] if use_acc else [],
        compiler_params=_params(("parallel", "parallel", "arbitrary")),
    )(a, b)


def _ew(fn, ins, outs, *, rows, ncol, name, n_acc=0, tm=512):
    n_in, n_out = len(ins), len(outs)
    tm = min(tm, rows)
    in_specs = []
    for _, kind, col0 in ins:
        if kind == "mat":
            in_specs.append(pl.BlockSpec((tm, CW), lambda j, i, c0=col0: (i, c0 + j)))
        else:
            in_specs.append(pl.BlockSpec((1, CW), lambda j, i, c0=col0: (0, c0 + j)))
    out_specs = [pl.BlockSpec((tm, CW), lambda j, i: (i, j)) for _ in outs]
    out_shape = [jax.ShapeDtypeStruct((rows, w), dt) for w, dt in outs]
    for _ in range(n_acc):
        out_specs.append(pl.BlockSpec((1, CW), lambda j, i: (0, j)))
        out_shape.append(jax.ShapeDtypeStruct((1, ncol * CW), F32))

    def body(*refs):
        vals = fn(*[r[...] for r in refs[:n_in]])
        for r, v in zip(refs[n_in:n_in + n_out], vals[:n_out]):
            r[...] = v.astype(r.dtype)
        i = pl.program_id(1)
        for r, v in zip(refs[n_in + n_out:], vals[n_out:]):
            @pl.when(i == 0)
            def _(r=r, v=v):
                r[...] = v

            @pl.when(i > 0)
            def _(r=r, v=v):
                r[...] += v

    res = pl.pallas_call(
        body, name=name, grid=(ncol, rows // tm), in_specs=in_specs, out_specs=out_specs, out_shape=out_shape,
        compiler_params=_params(("parallel", "arbitrary")),
    )(*[a for a, _, _ in ins])
    return res


def _colsum(v):
    return jnp.sum(v, axis=0, keepdims=True)


def _sigmoid(v):
    return jax.nn.sigmoid(v)


def _silu_and_grad(v):
    s = _sigmoid(v)
    return v * s, s * (1.0 + v * (1.0 - s))


def _rms_fwd(x, w, *, tm=512):
    rows, d = x.shape

    def body(x_ref, w_ref, h_ref, r_ref):
        xv = x_ref[...]
        r = lax.rsqrt(jnp.mean(xv * xv, axis=-1, keepdims=True) + NORM_EPS)
        h_ref[...] = (xv * r * w_ref[...]).astype(h_ref.dtype)
        r_ref[...] = r

    return pl.pallas_call(
        body, name="rms_fwd", grid=(rows // tm,),
        in_specs=[pl.BlockSpec((tm, d), lambda i: (i, 0)), pl.BlockSpec((1, d), lambda i: (0, 0))],
        out_specs=[pl.BlockSpec((tm, d), lambda i: (i, 0)), pl.BlockSpec((tm, 1), lambda i: (i, 0))],
        out_shape=[jax.ShapeDtypeStruct((rows, d), _MXU), jax.ShapeDtypeStruct((rows, 1), F32)],
        compiler_params=_params(("arbitrary",)),
    )(x, w)


def _rms_bwd(dh, x, rstd, w, dout, *, tm=256):
    rows, d = x.shape

    def body(dh_ref, x_ref, r_ref, w_ref, do_ref, gx_ref, gw_ref):
        dhv, xv, r, wv = dh_ref[...], x_ref[...], r_ref[...], w_ref[...]
        xr = xv * r
        t = jnp.mean(dhv * wv * xr, axis=-1, keepdims=True)
        gx_ref[...] = do_ref[...] + r * (wv * dhv - xr * t)
        part = _colsum(dhv * xr)
        i = pl.program_id(0)

        @pl.when(i == 0)
        def _():
            gw_ref[...] = part

        @pl.when(i > 0)
        def _():
            gw_ref[...] += part

    return pl.pallas_call(
        body, name="rms_bwd", grid=(rows // tm,),
        in_specs=[pl.BlockSpec((tm, d), lambda i: (i, 0)), pl.BlockSpec((tm, d), lambda i: (i, 0)),
                  pl.BlockSpec((tm, 1), lambda i: (i, 0)), pl.BlockSpec((1, d), lambda i: (0, 0)),
                  pl.BlockSpec((tm, d), lambda i: (i, 0))],
        out_specs=[pl.BlockSpec((tm, d), lambda i: (i, 0)), pl.BlockSpec((1, d), lambda i: (0, 0))],
        out_shape=[jax.ShapeDtypeStruct((rows, d), F32), jax.ShapeDtypeStruct((1, d), F32)],
        compiler_params=_params(("arbitrary",)),
    )(dh, x, rstd, w, dout)


_NT = (((1,), (1,)), ((), ()))
_TN = (((0,), (0,)), ((), ()))


def _head_rstd(v):
    return lax.rsqrt(jnp.mean(v * v, axis=-1, keepdims=True) + NORM_EPS)


def _band_mask(n):
    qi = lax.broadcasted_iota(jnp.int32, (WINDOW, 2 * WINDOW), 0) + WINDOW
    kj = lax.broadcasted_iota(jnp.int32, (WINDOW, 2 * WINDOW), 1)
    diff = qi - kj
    first_key = jnp.where(n > 0, 0, WINDOW)
    return (diff >= 0) & (diff < WINDOW) & (kj >= first_key)


def _attn_specs(nblk, rev):
    def cur(g, n):
        return (nblk - 1 - n) if rev else n

    q_spec = pl.BlockSpec((Q_PER_KV, WINDOW, HEAD_DIM), lambda g, n: (g, cur(g, n), 0))
    kc_spec = pl.BlockSpec((1, WINDOW, HEAD_DIM), lambda g, n: (g, cur(g, n), 0))
    kp_spec = pl.BlockSpec((1, WINDOW, HEAD_DIM), lambda g, n: (g, jnp.maximum(cur(g, n) - 1, 0), 0))
    w_spec = pl.BlockSpec((1, HEAD_DIM), lambda g, n: (0, 0))
    l_spec = pl.BlockSpec((Q_PER_KV, WINDOW, 1), lambda g, n: (g, cur(g, n), 0))
    return q_spec, kc_spec, kp_spec, w_spec, l_spec


def _attn_fwd(q, k, v, qw, kw, sinks):
    seq = q.shape[1]
    nblk = seq // WINDOW
    scale = 1.0 / math.sqrt(HEAD_DIM)
    q_spec, kc_spec, kp_spec, w_spec, l_spec = _attn_specs(nblk, False)

    def body(sink_ref, q_ref, kc_ref, kp_ref, vc_ref, vp_ref, qw_ref, kw_ref, o_ref, lse_ref):
        g, n = pl.program_id(0), pl.program_id(1)
        kwv, qwv = kw_ref[...], qw_ref[...]
        kc, kp = kc_ref[0], kp_ref[0]
        kk = jnp.concatenate([kp * _head_rstd(kp) * kwv, kc * _head_rstd(kc) * kwv], axis=0).astype(_MXU)
        vv = jnp.concatenate([vp_ref[0], vc_ref[0]], axis=0).astype(_MXU)
        valid = _band_mask(n)
        for r in range(Q_PER_KV):
            qr = q_ref[r]
            qn = (qr * _head_rstd(qr) * qwv).astype(_MXU)
            s = lax.dot_general(qn, kk, _NT, preferred_element_type=F32) * scale
            s = jnp.where(valid, s, -1e30)
            sink = sink_ref[g * Q_PER_KV + r]
            m = jnp.maximum(jnp.max(s, axis=-1, keepdims=True), sink)
            e = jnp.exp(s - m)
            z = jnp.sum(e, axis=-1, keepdims=True) + jnp.exp(sink - m)
            p = e / z
            o_ref[r] = jnp.dot(p.astype(_MXU), vv, preferred_element_type=F32)
            lse_ref[r] = m + jnp.log(z)

    return pl.pallas_call(
        body, name="attn_fwd", grid=(N_KV_HEADS, nblk),
        in_specs=[pl.BlockSpec(memory_space=pltpu.SMEM), q_spec, kc_spec, kp_spec, kc_spec, kp_spec, w_spec, w_spec],
        out_specs=[q_spec, l_spec],
        out_shape=[jax.ShapeDtypeStruct((N_Q_HEADS, seq, HEAD_DIM), F32),
                   jax.ShapeDtypeStruct((N_Q_HEADS, seq, 1), F32)],
        compiler_params=_params(("arbitrary", "arbitrary")),
    )(sinks, q, k, k, v, v, qw, kw)


def _attn_bwd(q, k, v, qw, kw, sinks, lse, do):
    seq = q.shape[1]
    nblk = seq // WINDOW
    scale = 1.0 / math.sqrt(HEAD_DIM)
    q_spec, kc_spec, kp_spec, w_spec, l_spec = _attn_specs(nblk, True)
    s_spec = pl.BlockSpec((1, Q_PER_KV, LANES), lambda g, n: (g, 0, 0))

    def body(sink_ref, q_ref, kc_ref, kp_ref, vc_ref, vp_ref, qw_ref, kw_ref, lse_ref, do_ref,
             dq_ref, dk_ref, dv_ref, dqw_ref, dkw_ref, dsk_ref, cdk, cdv):
        g, step = pl.program_id(0), pl.program_id(1)
        n = nblk - 1 - step

        @pl.when(step == 0)
        def _():
            cdk[...] = jnp.zeros_like(cdk)
            cdv[...] = jnp.zeros_like(cdv)
            dsk_ref[...] = jnp.zeros_like(dsk_ref)

        @pl.when((step == 0) & (g == 0))
        def _():
            dqw_ref[...] = jnp.zeros_like(dqw_ref)
            dkw_ref[...] = jnp.zeros_like(dkw_ref)

        kwv, qwv = kw_ref[...], qw_ref[...]
        kc, kp = kc_ref[0], kp_ref[0]
        rc = _head_rstd(kc)
        kk = jnp.concatenate([kp * _head_rstd(kp) * kwv, kc * rc * kwv], axis=0).astype(_MXU)
        vv = jnp.concatenate([vp_ref[0], vc_ref[0]], axis=0).astype(_MXU)
        valid = _band_mask(n)
        dkk = jnp.zeros((2 * WINDOW, HEAD_DIM), F32)
        dvv = jnp.zeros((2 * WINDOW, HEAD_DIM), F32)
        dqw = jnp.zeros((1, HEAD_DIM), F32)
        for r in range(Q_PER_KV):
            qr = q_ref[r]
            rq = _head_rstd(qr)
            qn = (qr * rq * qwv).astype(_MXU)
            s = lax.dot_general(qn, kk, _NT, preferred_element_type=F32) * scale
            s = jnp.where(valid, s, -1e30)
            lse_r = lse_ref[r]
            p = jnp.exp(s - lse_r)
            dob = do_ref[r].astype(_MXU)
            dp = lax.dot_general(dob, vv, _NT, preferred_element_type=F32)
            dsum = jnp.sum(p * dp, axis=-1, keepdims=True)
            ds = (p * (dp - dsum) * scale).astype(_MXU)
            sink = sink_ref[g * Q_PER_KV + r]
            dsink = _colsum(-jnp.exp(sink - lse_r) * dsum)
            dsk_ref[0, r:r + 1, :] += jnp.broadcast_to(dsink, (1, LANES))
            dvv = dvv + lax.dot_general(p.astype(_MXU), dob, _TN, preferred_element_type=F32)
            dqn = jnp.dot(ds, kk, preferred_element_type=F32)
            dkk = dkk + lax.dot_general(ds, qn, _TN, preferred_element_type=F32)
            qx = qr * rq
            dq_ref[r] = rq * (qwv * dqn - qx * jnp.mean(dqn * qwv * qx, axis=-1, keepdims=True))
            dqw = dqw + _colsum(dqn * qx)
        dkn = dkk[WINDOW:] + cdk[...]
        kx = kc * rc
        dk_ref[0] = rc * (kwv * dkn - kx * jnp.mean(dkn * kwv * kx, axis=-1, keepdims=True))
        dv_ref[0] = dvv[WINDOW:] + cdv[...]
        cdk[...] = dkk[:WINDOW]
        cdv[...] = dvv[:WINDOW]
        dqw_ref[...] += dqw
        dkw_ref[...] += _colsum(dkn * kx)

    return pl.pallas_call(
        body, name="attn_bwd", grid=(N_KV_HEADS, nblk),
        in_specs=[pl.BlockSpec(memory_space=pltpu.SMEM), q_spec, kc_spec, kp_spec, kc_spec, kp_spec, w_spec, w_spec,
                  l_spec, q_spec],
        out_specs=[q_spec, kc_spec, kc_spec, w_spec, w_spec, s_spec],
        out_shape=[jax.ShapeDtypeStruct((N_Q_HEADS, seq, HEAD_DIM), F32),
                   jax.ShapeDtypeStruct((N_KV_HEADS, seq, HEAD_DIM), F32),
                   jax.ShapeDtypeStruct((N_KV_HEADS, seq, HEAD_DIM), F32),
                   jax.ShapeDtypeStruct((1, HEAD_DIM), F32), jax.ShapeDtypeStruct((1, HEAD_DIM), F32),
                   jax.ShapeDtypeStruct((N_KV_HEADS, Q_PER_KV, LANES), F32)],
        scratch_shapes=[pltpu.VMEM((WINDOW, HEAD_DIM), F32), pltpu.VMEM((WINDOW, HEAD_DIM), F32)],
        compiler_params=_params(("arbitrary", "arbitrary")),
    )(sinks, q, k, k, v, v, qw, kw, lse, do)


def _ssm_discretise(a_re, a_im, log_dt):
    dt = jnp.exp(log_dt)
    mag = jnp.exp(dt * a_re)
    ab_re = mag * jnp.cos(dt * a_im)
    ab_im = mag * jnp.sin(dt * a_im)
    num_re = ab_re - 1.0
    num_im = ab_im
    den = a_re * a_re + a_im * a_im
    cf_re = (num_re * a_re + num_im * a_im) / den
    cf_im = (num_im * a_re - num_re * a_im) / den
    return ab_re, ab_im, cf_re, cf_im


def _ssm_params_fwd(a_re, a_im, log_dt):
    shp = jax.ShapeDtypeStruct(a_re.shape, F32)

    def body(are_ref, aim_ref, ldt_ref, abr_ref, abi_ref, cfr_ref, cfi_ref, alr_ref, ali_ref):
        abr, abi, cfr, cfi = _ssm_discretise(are_ref[...], aim_ref[...], ldt_ref[...])
        abr_ref[...], abi_ref[...], cfr_ref[...], cfi_ref[...] = abr, abi, cfr, cfi
        pr, pi = abr, abi
        for _ in range(int(math.log2(SSM_L))):
            pr, pi = pr * pr - pi * pi, 2.0 * pr * pi
        alr_ref[...], ali_ref[...] = pr, pi

    return pl.pallas_call(body, name="ssm_params_fwd", out_shape=[shp] * 6)(a_re, a_im, log_dt)


def _ssm_params_bwd(a_re, a_im, log_dt, d_abr, d_abi, d_cfr, d_cfi):
    def body(are_ref, aim_ref, ldt_ref, g0, g1, g2, g3, dare_ref, daim_ref, dldt_ref):
        _, vjp = jax.vjp(_ssm_discretise, are_ref[...], aim_ref[...], ldt_ref[...])
        dare_ref[...], daim_ref[...], dldt_ref[...] = vjp((g0[...], g1[...], g2[...], g3[...]))

    return pl.pallas_call(
        body, name="ssm_params_bwd",
        out_shape=[jax.ShapeDtypeStruct(a_re.shape, F32), jax.ShapeDtypeStruct(a_im.shape, F32),
                   jax.ShapeDtypeStruct(log_dt.shape, F32)],
    )(a_re, a_im, log_dt, d_abr, d_abi, d_cfr, d_cfi)


def _scan_cols(j):
    return pl.ds(j * SSM_SB, SSM_SB)


def _rows8(r):
    return pl.ds(pl.multiple_of(r * SUBLANES, SUBLANES), SUBLANES)


def _bcast8(row):
    return jnp.broadcast_to(row, (SUBLANES, row.shape[-1]))


def _ssm_fwd(u, b_re, b_im, c_re, c_im, d_skip, coef):
    seq = u.shape[0]
    nc = seq // SSM_T
    T, L = SSM_T, SSM_L

    def body(u_ref, bre_ref, bim_ref, cre_ref, cim_ref, d_ref, are_ref, aim_ref, cfr_ref, cfi_ref, alr_ref, ali_ref,
             y_ref, sre_ref, sim_ref, ire_ref, iim_ref, car_re, car_im, end_re, end_im):
        c = pl.program_id(0)

        @pl.when(c == 0)
        def _():
            car_re[...] = jnp.zeros_like(car_re)
            car_im[...] = jnp.zeros_like(car_im)

        for j in range(SSM_JB):
            ub = u_ref[:, j * LANES:(j + 1) * LANES].astype(_MXU)
            bur = jnp.dot(ub, bre_ref[j], preferred_element_type=F32)
            bui = jnp.dot(ub, bim_ref[j], preferred_element_type=F32)
            cfr, cfi = cfr_ref[:, _scan_cols(j)], cfi_ref[:, _scan_cols(j)]
            sre_ref[:, _scan_cols(j)] = cfr * bur - cfi * bui
            sim_ref[:, _scan_cols(j)] = cfr * bui + cfi * bur

        for j in range(SSM_JB):
            cols = _scan_cols(j)
            ar, ai = _bcast8(are_ref[:, cols]), _bcast8(aim_ref[:, cols])

            def step1(r, s, cols=cols, ar=ar, ai=ai):
                sr, si = s
                rows = _rows8(r)
                return (ar * sr - ai * si + sre_ref[rows, cols], ar * si + ai * sr + sim_ref[rows, cols])

            zero = jnp.zeros((SUBLANES, SSM_SB), F32)
            er, ei = lax.fori_loop(0, L, step1, (zero, zero), unroll=4)
            end_re[:, cols] = er
            end_im[:, cols] = ei

        alr, ali = alr_ref[...], ali_ref[...]
        cr, ci = car_re[...], car_im[...]
        ire_ref[0:1, :] = cr
        iim_ref[0:1, :] = ci
        for i in range(1, SUBLANES):
            er, ei = end_re[i - 1:i, :], end_im[i - 1:i, :]
            cr, ci = alr * cr - ali * ci + er, alr * ci + ali * cr + ei
            ire_ref[i:i + 1, :] = cr
            iim_ref[i:i + 1, :] = ci

        for j in range(SSM_JB):
            cols = _scan_cols(j)
            ar, ai = _bcast8(are_ref[:, cols]), _bcast8(aim_ref[:, cols])

            def step2(r, s, cols=cols, ar=ar, ai=ai):
                sr, si = s
                rows = _rows8(r)
                nr = ar * sr - ai * si + sre_ref[rows, cols]
                ni = ar * si + ai * sr + sim_ref[rows, cols]
                sre_ref[rows, cols] = nr
                sim_ref[rows, cols] = ni
                return nr, ni

            lax.fori_loop(0, L, step2, (ire_ref[:, cols], iim_ref[:, cols]), unroll=4)

        car_re[...] = sre_ref[T - 1:T, :]
        car_im[...] = sim_ref[T - 1:T, :]

        for j in range(SSM_JB):
            cols = _scan_cols(j)
            ch = slice(j * LANES, (j + 1) * LANES)
            y = (jnp.dot(sre_ref[:, cols].astype(_MXU), cre_ref[j], preferred_element_type=F32)
                 - jnp.dot(sim_ref[:, cols].astype(_MXU), cim_ref[j], preferred_element_type=F32))
            y_ref[:, ch] = y + d_ref[:, ch] * u_ref[:, ch]

    tok = pl.BlockSpec((T, SSM_W), lambda c: (c, 0))
    st = pl.BlockSpec((T, N_STATES), lambda c: (c, 0))
    ini = pl.BlockSpec((None, SUBLANES, N_STATES), lambda c: (c, 0, 0))
    bsp = pl.BlockSpec((SSM_JB, LANES, SSM_SB), lambda c: (0, 0, 0))
    csp = pl.BlockSpec((SSM_JB, SSM_SB, LANES), lambda c: (0, 0, 0))
    row_w = pl.BlockSpec((1, SSM_W), lambda c: (0, 0))
    row_s = pl.BlockSpec((1, N_STATES), lambda c: (0, 0))
    return pl.pallas_call(
        body, name="ssm_fwd", grid=(nc,),
        in_specs=[tok, bsp, bsp, csp, csp, row_w] + [row_s] * 6,
        out_specs=[tok, st, st, ini, ini],
        out_shape=[jax.ShapeDtypeStruct((seq, SSM_W), F32),
                   jax.ShapeDtypeStruct((seq, N_STATES), F32), jax.ShapeDtypeStruct((seq, N_STATES), F32),
                   jax.ShapeDtypeStruct((nc, SUBLANES, N_STATES), F32),
                   jax.ShapeDtypeStruct((nc, SUBLANES, N_STATES), F32)],
        scratch_shapes=[pltpu.VMEM((1, N_STATES), F32), pltpu.VMEM((1, N_STATES), F32),
                        pltpu.VMEM((SUBLANES, N_STATES), F32), pltpu.VMEM((SUBLANES, N_STATES), F32)],
        compiler_params=_params(("arbitrary",)),
    )(u, b_re, b_im, c_re, c_im, d_skip, *coef)


def _ssm_bwd(dy, u, s_re, s_im, i_re, i_im, b_re, b_im, c_re, c_im, d_skip, coef):
    seq = u.shape[0]
    nc = seq // SSM_T
    T, L = SSM_T, SSM_L

    def body(dy_ref, u_ref, sre_ref, sim_ref, ire_ref, iim_ref, bre_ref, bim_ref, cre_ref, cim_ref, d_ref,
             are_ref, aim_ref, cfr_ref, cfi_ref, alr_ref, ali_ref,
             du_ref, dbre_out, dbim_out, dcre_out, dcim_out, dd_ref, dar_ref, dai_ref, dcfr_ref, dcfi_ref,
             lre, lim, car_re, car_im, end_re, end_im, ini_re, ini_im, dbre_ref, dbim_ref, dcre_ref, dcim_ref):
        step = pl.program_id(0)

        @pl.when(step == 0)
        def _():
            car_re[...] = jnp.zeros_like(car_re)
            car_im[...] = jnp.zeros_like(car_im)
            for ref in (dbre_ref, dbim_ref, dcre_ref, dcim_ref, dd_ref, dar_ref, dai_ref, dcfr_ref, dcfi_ref):
                ref[...] = jnp.zeros_like(ref)

        for j in range(SSM_JB):
            dyb = dy_ref[:, j * LANES:(j + 1) * LANES].astype(_MXU)
            lre[:, _scan_cols(j)] = lax.dot_general(dyb, cre_ref[j], _NT, preferred_element_type=F32)
            lim[:, _scan_cols(j)] = -lax.dot_general(dyb, cim_ref[j], _NT, preferred_element_type=F32)

        for j in range(SSM_JB):
            cols = _scan_cols(j)
            ar, ai = _bcast8(are_ref[:, cols]), _bcast8(aim_ref[:, cols])

            def step1(t, s, cols=cols, ar=ar, ai=ai):
                sr, si = s
                rows = _rows8(L - 1 - t)
                return (ar * sr + ai * si + lre[rows, cols], ar * si - ai * sr + lim[rows, cols])

            zero = jnp.zeros((SUBLANES, SSM_SB), F32)
            er, ei = lax.fori_loop(0, L, step1, (zero, zero), unroll=4)
            end_re[:, cols] = er
            end_im[:, cols] = ei

        alr, ali = alr_ref[...], ali_ref[...]
        cr, ci = car_re[...], car_im[...]
        ini_re[SUBLANES - 1:SUBLANES, :] = cr
        ini_im[SUBLANES - 1:SUBLANES, :] = ci
        for i in range(SUBLANES - 2, -1, -1):
            er, ei = end_re[i + 1:i + 2, :], end_im[i + 1:i + 2, :]
            cr, ci = alr * cr + ali * ci + er, alr * ci - ali * cr + ei
            ini_re[i:i + 1, :] = cr
            ini_im[i:i + 1, :] = ci

        for j in range(SSM_JB):
            cols = _scan_cols(j)
            ar, ai = _bcast8(are_ref[:, cols]), _bcast8(aim_ref[:, cols])

            def step2(t, s, cols=cols, ar=ar, ai=ai):
                sr, si = s
                rows = _rows8(L - 1 - t)
                nr = ar * sr + ai * si + lre[rows, cols]
                ni = ar * si - ai * sr + lim[rows, cols]
                lre[rows, cols] = nr
                lim[rows, cols] = ni
                return nr, ni

            lax.fori_loop(0, L, step2, (ini_re[:, cols], ini_im[:, cols]), unroll=4)

        car_re[...] = lre[0:1, :]
        car_im[...] = lim[0:1, :]

        head, tail, body_rows = slice(0, SUBLANES), slice(SUBLANES, T), slice(0, T - SUBLANES)
        for j in range(SSM_JB):
            cols = _scan_cols(j)
            ch = slice(j * LANES, (j + 1) * LANES)
            lr, li = lre[:, cols], lim[:, cols]
            dar_ref[:, cols] += (_colsum(lre[tail, cols] * sre_ref[body_rows, cols] + lim[tail, cols] * sim_ref[body_rows, cols])
                                 + _colsum(lre[head, cols] * ire_ref[:, cols] + lim[head, cols] * iim_ref[:, cols]))
            dai_ref[:, cols] += (_colsum(lim[tail, cols] * sre_ref[body_rows, cols] - lre[tail, cols] * sim_ref[body_rows, cols])
                                 + _colsum(lim[head, cols] * ire_ref[:, cols] - lre[head, cols] * iim_ref[:, cols]))
            uf = u_ref[:, ch]
            ub = uf.astype(_MXU)
            bur = jnp.dot(ub, bre_ref[j], preferred_element_type=F32)
            bui = jnp.dot(ub, bim_ref[j], preferred_element_type=F32)
            dcfr_ref[:, cols] += _colsum(lr * bur + li * bui)
            dcfi_ref[:, cols] += _colsum(li * bur - lr * bui)
            cfr, cfi = cfr_ref[:, cols], cfi_ref[:, cols]
            dbur = (cfr * lr + cfi * li).astype(_MXU)
            dbui = (cfr * li - cfi * lr).astype(_MXU)
            dyf = dy_ref[:, ch]
            dyb = dyf.astype(_MXU)
            du_ref[:, ch] = (lax.dot_general(dbur, bre_ref[j], _NT, preferred_element_type=F32)
                             + lax.dot_general(dbui, bim_ref[j], _NT, preferred_element_type=F32)
                             + d_ref[:, ch] * dyf)
            dbre_ref[j] += lax.dot_general(ub, dbur, _TN, preferred_element_type=F32)
            dbim_ref[j] += lax.dot_general(ub, dbui, _TN, preferred_element_type=F32)
            dcre_ref[j] += lax.dot_general(sre_ref[:, cols].astype(_MXU), dyb, _TN, preferred_element_type=F32)
            dcim_ref[j] -= lax.dot_general(sim_ref[:, cols].astype(_MXU), dyb, _TN, preferred_element_type=F32)
            dd_ref[:, ch] += _colsum(dyf * uf)

        @pl.when(step == nc - 1)
        def _():
            for acc, out in ((dbre_ref, dbre_out), (dbim_ref, dbim_out), (dcre_ref, dcre_out), (dcim_ref, dcim_out)):
                pltpu.sync_copy(acc, out)

    tok = pl.BlockSpec((T, SSM_W), lambda c: (nc - 1 - c, 0))
    st = pl.BlockSpec((T, N_STATES), lambda c: (nc - 1 - c, 0))
    ini = pl.BlockSpec((None, SUBLANES, N_STATES), lambda c: (nc - 1 - c, 0, 0))
    bsp = pl.BlockSpec((SSM_JB, LANES, SSM_SB), lambda c: (0, 0, 0))
    csp = pl.BlockSpec((SSM_JB, SSM_SB, LANES), lambda c: (0, 0, 0))
    row_w = pl.BlockSpec((1, SSM_W), lambda c: (0, 0))
    row_s = pl.BlockSpec((1, N_STATES), lambda c: (0, 0))
    big = pltpu.VMEM((T, N_STATES), F32)
    one = pltpu.VMEM((1, N_STATES), F32)
    eight = pltpu.VMEM((SUBLANES, N_STATES), F32)
    return pl.pallas_call(
        body, name="ssm_bwd", grid=(nc,),
        in_specs=[tok, tok, st, st, ini, ini, bsp, bsp, csp, csp, row_w] + [row_s] * 6,
        out_specs=[tok, _ANY, _ANY, _ANY, _ANY, row_w, row_s, row_s, row_s, row_s],
        out_shape=[jax.ShapeDtypeStruct((seq, SSM_W), F32),
                   jax.ShapeDtypeStruct((SSM_JB, LANES, SSM_SB), F32), jax.ShapeDtypeStruct((SSM_JB, LANES, SSM_SB), F32),
                   jax.ShapeDtypeStruct((SSM_JB, SSM_SB, LANES), F32), jax.ShapeDtypeStruct((SSM_JB, SSM_SB, LANES), F32),
                   jax.ShapeDtypeStruct((1, SSM_W), F32)] + [jax.ShapeDtypeStruct((1, N_STATES), F32)] * 4,
        scratch_shapes=[big, big, one, one, eight, eight, eight, eight,
                        pltpu.VMEM((SSM_JB, LANES, SSM_SB), F32), pltpu.VMEM((SSM_JB, LANES, SSM_SB), F32),
                        pltpu.VMEM((SSM_JB, SSM_SB, LANES), F32), pltpu.VMEM((SSM_JB, SSM_SB, LANES), F32)],
        compiler_params=_params(("arbitrary",)),
    )(dy, u, s_re, s_im, i_re, i_im, b_re, b_im, c_re, c_im, d_skip, *coef)


def _block_diag_b(b):
    t = b.reshape(SSM_JB, 8, STATE, GROUP).transpose(0, 1, 3, 2)
    eye = jnp.eye(8, dtype=b.dtype)
    return (t[:, :, :, None, :] * eye[None, :, None, :, None]).reshape(SSM_JB, LANES, SSM_SB)


def _block_diag_c(c):
    t = c.reshape(SSM_JB, 8, GROUP, STATE).transpose(0, 1, 3, 2)
    eye = jnp.eye(8, dtype=c.dtype)
    return (t[:, :, :, None, :] * eye[None, :, None, :, None]).reshape(SSM_JB, SSM_SB, LANES)


def _diag_of_b(blk):
    t = blk.reshape(SSM_JB, 8, GROUP, 8, STATE)
    d = jnp.stack([t[:, i, :, i, :] for i in range(8)], axis=1)
    return d.transpose(0, 1, 3, 2).reshape(N_GROUPS, STATE, GROUP)


def _diag_of_c(blk):
    t = blk.reshape(SSM_JB, 8, STATE, 8, GROUP)
    d = jnp.stack([t[:, i, :, i, :] for i in range(8)], axis=1)
    return d.transpose(0, 1, 3, 2).reshape(N_GROUPS, GROUP, STATE)


def _to_scan_order(v):
    seq, w = v.shape
    return v.reshape(seq // SSM_T, SUBLANES, SSM_L, w).transpose(0, 2, 1, 3).reshape(seq, w)


def _from_scan_order(v):
    seq, w = v.shape
    return v.reshape(seq // SSM_T, SSM_L, SUBLANES, w).transpose(0, 2, 1, 3).reshape(seq, w)


def _adamw(w, g, m, v, *, name, tm):
    rows, cols = w.shape

    def body(w_ref, g_ref, m_ref, v_ref, d_ref, nm_ref, nv_ref):
        gv = g_ref[...]
        nm = ADAM_B1 * m_ref[...] + (1.0 - ADAM_B1) * gv
        nv = ADAM_B2 * v_ref[...] + (1.0 - ADAM_B2) * jnp.square(gv)
        m_hat = nm / (1.0 - ADAM_B1 ** ADAM_STEP)
        v_hat = nv / (1.0 - ADAM_B2 ** ADAM_STEP)
        d_ref[...] = -ADAM_LR * (m_hat / (jnp.sqrt(v_hat) + ADAM_EPS) + ADAM_WD * w_ref[...])
        nm_ref[...] = nm
        nv_ref[...] = nv

    spec = pl.BlockSpec((tm, cols), lambda i: (i, 0))
    shp = jax.ShapeDtypeStruct((rows, cols), F32)
    return pl.pallas_call(body, name=name, grid=(rows // tm,), in_specs=[spec] * 4, out_specs=[spec] * 3,
                          out_shape=[shp] * 3, compiler_params=_params(("arbitrary",)))(w, g, m, v)


def _place():
    x, y, c = lax.axis_index("x"), lax.axis_index("y"), lax.axis_index("c")
    chips = [(1 - x, y), (x, 1 - y), (1 - x, 1 - y)]
    return x, y, c, chips


def _remote(src, dst, send_sem, recv_sem, dev):
    return pltpu.make_async_remote_copy(src_ref=src, dst_ref=dst, send_sem=send_sem, recv_sem=recv_sem,
                                        device_id=dev, device_id_type=MESH)


def _gather_weights(shards):
    n = len(shards)

    def body(*refs):
        ins, outs = refs[:n], refs[n:2 * n]
        send_sems, recv_sems, local_sems = refs[2 * n:]
        x, y, c, chips = _place()
        mine = 2 * x + y
        sibling = (x, y, 1 - c)
        local, sends = [], []
        for t in range(n):
            half = pl.ds(c * (ins[t].shape[0] // 2), ins[t].shape[0] // 2)
            cp = pltpu.make_async_copy(ins[t], outs[t].at[mine], local_sems.at[t])
            cp.start()
            local.append(cp)
            for k, chip in enumerate(chips):
                cp = _remote(ins[t].at[half], outs[t].at[mine, half], send_sems.at[6 * t + k],
                             recv_sems.at[6 * t + k], (*chip, c))
                cp.start()
                sends.append(cp)
        for t in range(n):
            half = pl.ds(c * (ins[t].shape[0] // 2), ins[t].shape[0] // 2)
            for k, chip in enumerate(chips):
                landed = outs[t].at[2 * chip[0] + chip[1], half]
                _remote(landed, landed, send_sems.at[6 * t + k], recv_sems.at[6 * t + k], (*chip, c)).wait_recv()
                cp = _remote(landed, landed, send_sems.at[6 * t + 3 + k], recv_sems.at[6 * t + 3 + k], sibling)
                cp.start()
                sends.append(cp)
        for t in range(n):
            other = pl.ds((1 - c) * (ins[t].shape[0] // 2), ins[t].shape[0] // 2)
            for k, chip in enumerate(chips):
                passed = outs[t].at[2 * chip[0] + chip[1], other]
                _remote(passed, passed, send_sems.at[6 * t + 3 + k], recv_sems.at[6 * t + 3 + k], sibling).wait_recv()
        for cp in sends:
            cp.wait_send()
        for cp in local:
            cp.wait()

    return pl.pallas_call(
        body, name="gather_weights", in_specs=[_ANY] * n, out_specs=[_ANY] * n,
        out_shape=[jax.ShapeDtypeStruct((N_CHIPS,) + s.shape, s.dtype) for s in shards],
        scratch_shapes=[pltpu.SemaphoreType.DMA((6 * n,)), pltpu.SemaphoreType.DMA((6 * n,)),
                        pltpu.SemaphoreType.DMA((n,))],
    )(*shards)


def _swap_halves(grads):
    n = len(grads)

    def body(*refs):
        ins, outs = refs[:n], refs[n:2 * n]
        send_sems, recv_sems = refs[2 * n:]
        x, y, c, _ = _place()
        cps = []
        for t in range(n):
            r2 = ins[t].shape[1] // 2
            cp = _remote(ins[t].at[:, pl.ds((1 - c) * r2, r2), :], outs[t], send_sems.at[t], recv_sems.at[t],
                         (x, y, 1 - c))
            cp.start()
            cps.append(cp)
        for cp in cps:
            cp.wait()

    return pl.pallas_call(
        body, name="grad_swap_halves", in_specs=[_ANY] * n, out_specs=[_ANY] * n,
        out_shape=[jax.ShapeDtypeStruct((N_CHIPS, g.shape[1] // 2, g.shape[2]), g.dtype) for g in grads],
        scratch_shapes=[pltpu.SemaphoreType.DMA((n,)), pltpu.SemaphoreType.DMA((n,))],
    )(*grads)


def _scatter_chips(halves):
    n = len(halves)

    def body(*refs):
        ins, outs = refs[:n], refs[n:2 * n]
        send_sems, recv_sems = refs[2 * n:]
        x, y, c, chips = _place()
        cps = []
        for t in range(n):
            for k, chip in enumerate(chips):
                cp = _remote(ins[t].at[2 * chip[0] + chip[1]], outs[t].at[k], send_sems.at[3 * t + k],
                             recv_sems.at[3 * t + k], (*chip, c))
                cp.start()
                cps.append(cp)
        for cp in cps:
            cp.wait()

    return pl.pallas_call(
        body, name="grad_scatter_chips", in_specs=[_ANY] * n, out_specs=[_ANY] * n,
        out_shape=[jax.ShapeDtypeStruct((3,) + h.shape[1:], h.dtype) for h in halves],
        scratch_shapes=[pltpu.SemaphoreType.DMA((3 * n,)), pltpu.SemaphoreType.DMA((3 * n,))],
    )(*halves)


def _join_halves(totals):
    n = len(totals)

    def body(*refs):
        ins, outs = refs[:n], refs[n:2 * n]
        send_sems, recv_sems, local_sems = refs[2 * n:]
        x, y, c, _ = _place()
        cps = []
        for t in range(n):
            r2 = ins[t].shape[0]
            mine = outs[t].at[pl.ds(c * r2, r2)]
            lc = pltpu.make_async_copy(ins[t], mine, local_sems.at[t])
            lc.start()
            cp = _remote(ins[t], mine, send_sems.at[t], recv_sems.at[t], (x, y, 1 - c))
            cp.start()
            cps.append((lc, cp))
        for t, (lc, cp) in enumerate(cps):
            r2 = ins[t].shape[0]
            theirs = outs[t].at[pl.ds((1 - c) * r2, r2)]
            _remote(ins[t], theirs, send_sems.at[t], recv_sems.at[t], (x, y, 1 - c)).wait_recv()
            cp.wait_send()
            lc.wait()

    return pl.pallas_call(
        body, name="grad_join_halves", in_specs=[_ANY] * n, out_specs=[_ANY] * n,
        out_shape=[jax.ShapeDtypeStruct((2 * t.shape[0], t.shape[1]), t.dtype) for t in totals],
        scratch_shapes=[pltpu.SemaphoreType.DMA((n,)), pltpu.SemaphoreType.DMA((n,)), pltpu.SemaphoreType.DMA((n,))],
    )(*totals)


def _add_sibling_half(g, got, c_arr, *, name, tm):
    _, rows, cols = g.shape
    r2 = rows // 2
    nb = r2 // tm

    def body(c_ref, g_ref, r_ref, o_ref):
        o_ref[...] = (g_ref[...].astype(F32) + r_ref[...].astype(F32)).astype(o_ref.dtype)

    return pl.pallas_call(
        body, name=name,
        grid_spec=pltpu.PrefetchScalarGridSpec(
            num_scalar_prefetch=1, grid=(N_CHIPS, nb),
            in_specs=[pl.BlockSpec((None, tm, cols), lambda b, i, c: (b, c[0] * nb + i, 0)),
                      pl.BlockSpec((None, tm, cols), lambda b, i, c: (b, i, 0))],
            out_specs=pl.BlockSpec((None, tm, cols), lambda b, i, c: (b, i, 0))),
        out_shape=jax.ShapeDtypeStruct((N_CHIPS, r2, cols), _WIRE),
        compiler_params=_params(("arbitrary", "arbitrary")),
    )(c_arr, g, got)


def _add_chips(h, got, mine_arr, *, name, tm):
    _, r2, cols = h.shape

    def body(m_ref, h_ref, r_ref, o_ref):
        o_ref[...] = ((h_ref[...].astype(F32) + r_ref[0].astype(F32)) + r_ref[1].astype(F32)) + r_ref[2].astype(F32)

    return pl.pallas_call(
        body, name=name,
        grid_spec=pltpu.PrefetchScalarGridSpec(
            num_scalar_prefetch=1, grid=(r2 // tm,),
            in_specs=[pl.BlockSpec((None, tm, cols), lambda i, m: (m[0], i, 0)),
                      pl.BlockSpec((3, tm, cols), lambda i, m: (0, i, 0))],
            out_specs=pl.BlockSpec((tm, cols), lambda i, m: (i, 0))),
        out_shape=jax.ShapeDtypeStruct((r2, cols), F32),
        compiler_params=_params(("arbitrary",)),
    )(mine_arr, h, got)


def _reduce_scatter(grads):
    c_arr = lax.axis_index("c").astype(jnp.int32).reshape(1)
    mine_arr = (2 * lax.axis_index("x") + lax.axis_index("y")).astype(jnp.int32).reshape(1)
    got = _swap_halves(grads)
    pair = [_add_sibling_half(g, r, c_arr, name=f"grad_add_sibling_{t}", tm=min(256, g.shape[1] // 2))
            for t, (g, r) in enumerate(zip(grads, got))]
    got = _scatter_chips(pair)
    total = [_add_chips(h, r, mine_arr, name=f"grad_add_chips_{t}", tm=min(256, h.shape[1]))
             for t, (h, r) in enumerate(zip(pair, got))]
    return _join_halves(total)


def _all_gather_small(v):
    m_per, n = v.shape

    def body(x_ref, out_ref, send_sems, recv_sems, local_sem):
        x, y, c, chips = _place()
        me, sibling = (x, y, c), (x, y, 1 - c)

        def rows(px, py, pc):
            return out_ref.at[4 * px + 2 * py + pc]

        def copy(k, block, to, src=None):
            return _remote(rows(*block) if src is None else src, rows(*block), send_sems.at[k], recv_sems.at[k], to)

        mine = pltpu.make_async_copy(x_ref, rows(*me), local_sem)
        mine.start()
        first = [copy(0, me, sibling, src=x_ref)]
        first += [copy(1 + j, me, (*chip, c), src=x_ref) for j, chip in enumerate(chips)]
        for cp in first:
            cp.start()
        passed = [copy(4 + j, (*chip, c), sibling) for j, chip in enumerate(chips)]
        for j, chip in enumerate(chips):
            copy(1 + j, (*chip, c), me).wait_recv()
            passed[j].start()
        copy(0, sibling, me).wait_recv()
        for j, chip in enumerate(chips):
            copy(4 + j, (*chip, 1 - c), me).wait_recv()
        for cp in first + passed:
            cp.wait_send()
        mine.wait()

    return pl.pallas_call(
        body, name="gather_small_grads",
        out_shape=jax.ShapeDtypeStruct((8, m_per, n), v.dtype),
        in_specs=[pl.BlockSpec(memory_space=pltpu.VMEM)], out_specs=pl.BlockSpec(memory_space=pltpu.VMEM),
        scratch_shapes=[pltpu.SemaphoreType.DMA((7,)), pltpu.SemaphoreType.DMA((7,)), pltpu.SemaphoreType.DMA],
        compiler_params=pltpu.CompilerParams(vmem_limit_bytes=VMEM_LIMIT),
    )(v)


def _sum8(v):
    _, m, n = v.shape

    def body(v_ref, o_ref):
        acc = v_ref[0]
        for d in range(1, 8):
            acc = acc + v_ref[d]
        o_ref[...] = acc

    return pl.pallas_call(body, name="sum_small_grads", out_shape=jax.ShapeDtypeStruct((m, n), F32),
                          compiler_params=pltpu.CompilerParams(vmem_limit_bytes=VMEM_LIMIT))(v)


def _heads(v, nh):
    return v.reshape(v.shape[0], nh, HEAD_DIM).transpose(1, 0, 2)


def _unheads(v):
    return v.transpose(1, 0, 2).reshape(v.shape[1], v.shape[0] * HEAD_DIM)


def _local_step(x, target, norm_w, w_in4, q_norm_w, k_norm_w, sinks, w_ap4, a_re, a_im, log_dt, b_re, b_im, c_re,
                c_im, d_skip, w_glu4, b_glu, w_sp4, w_out):
    seq = x.shape[0]
    qw, kw = q_norm_w.reshape(1, HEAD_DIM), k_norm_w.reshape(1, HEAD_DIM)
    nw, bg = norm_w.reshape(1, D_MODEL), b_glu.reshape(1, D_MODEL)
    dsk = d_skip.reshape(1, SSM_W)

    h, rstd = _rms_fwd(x, nw)
    proj = _mm(h, w_in4, mode="nn", name="mm_proj", tm=512, tn=IN_W // 4, tk=D_MODEL, b_blocked=True)
    q = _heads(proj[:, :ATTN_W], N_Q_HEADS)
    k = _heads(proj[:, ATTN_W:ATTN_W + KV_W], N_KV_HEADS)
    v = _heads(proj[:, ATTN_W + KV_W:ATTN_W + 2 * KV_W], N_KV_HEADS)
    attn_h, lse = _attn_fwd(q, k, v, qw, kw, sinks)
    attn = _unheads(attn_h)

    def gate_a(at, ag):
        return (at * (ag * _sigmoid(ag)),)

    (ya_in,) = _ew(gate_a, [(attn, "mat", 0), (proj, "mat", OFF_AGATE)], [(ATTN_W, _MXU)], rows=seq, ncol=2,
                   name="ew_attn_gate")
    y_a = _mm(ya_in, w_ap4, mode="nn", name="mm_attn_proj", tm=1024, tn=512, tk=ATTN_W, b_blocked=True)

    coef = [t.reshape(1, N_STATES) for t in _ssm_params_fwd(a_re, a_im, log_dt.reshape(N_GROUPS, 1))]
    bre_blk, bim_blk = _block_diag_b(b_re).astype(_MXU), _block_diag_b(b_im).astype(_MXU)
    cre_blk, cim_blk = _block_diag_c(c_re).astype(_MXU), _block_diag_c(c_im).astype(_MXU)
    u_scan = _to_scan_order(proj[:, OFF_U * CW:OFF_U * CW + SSM_W])
    y_scan, s_re, s_im, i_re, i_im = _ssm_fwd(u_scan, bre_blk, bim_blk, cre_blk, cim_blk, dsk, coef)
    y_ssm = _from_scan_order(y_scan)

    (yg,) = _ew(lambda yv: (jax.nn.gelu(yv),), [(y_ssm, "mat", 0)], [(SSM_W, _MXU)], rows=seq, ncol=2, name="ew_gelu")
    glu = _mm(yg, w_glu4, mode="nn", name="mm_glu", tm=1024, tn=512, tk=SSM_W, b_blocked=True)

    def gate_s(ga, gb, ba, bb, z):
        return ((ga + ba) * _sigmoid(gb + bb) * (z * _sigmoid(z)),)

    (ys_in,) = _ew(gate_s, [(glu, "mat", 0), (glu, "mat", 2), (bg, "row", 0), (bg, "row", 2), (proj, "mat", OFF_Z)],
                   [(SSM_W, _MXU)], rows=seq, ncol=2, name="ew_ssm_gate")
    y_s = _mm(ys_in, w_sp4, mode="nn", name="mm_ssm_proj", tm=1024, tn=512, tk=SSM_W, b_blocked=True)

    def merge(ga, gs, ya, ys):
        return (_sigmoid(ga) * ya + _sigmoid(gs) * ys,)

    (merged,) = _ew(merge, [(proj, "mat", OFF_GA), (proj, "mat", OFF_GS), (y_a, "mat", 0), (y_s, "mat", 0)],
                    [(D_MODEL, _MXU)], rows=seq, ncol=4, name="ew_merge")
    mo = _mm(merged, w_out, mode="nn", name="mm_out", tm=512, tn=D_MODEL, tk=D_MODEL)

    def loss_head(xv, mv, tv):
        err = (xv + mv) - tv
        dout = err * (1.0 / D_MODEL)
        return dout, dout, _colsum(err * err)

    dout, dout_b, sq = _ew(loss_head, [(x, "mat", 0), (mo, "mat", 0), (target, "mat", 0)],
                           [(D_MODEL, F32), (D_MODEL, _MXU)], rows=seq, ncol=4, n_acc=1, name="ew_loss")
    loss = 0.5 * jnp.sum(sq) / D_MODEL

    d_merged = _mm(dout_b, w_out, mode="nt", name="mm_d_merged", tm=512, tn=D_MODEL, tk=D_MODEL)
    g_w_out = _mm(merged, dout_b, mode="tn", name="mm_g_w_out", tm=512, tn=D_MODEL, tk=512, out_dtype=_WIRE)

    def merge_bwd(dm, ga, gs, ya, ys):
        sa, ss = _sigmoid(ga), _sigmoid(gs)
        return sa * dm, ss * dm, dm * ya * sa * (1.0 - sa), dm * ys * ss * (1.0 - ss)

    d_ya, d_ys, d_ga, d_gs = _ew(
        merge_bwd, [(d_merged, "mat", 0), (proj, "mat", OFF_GA), (proj, "mat", OFF_GS), (y_a, "mat", 0), (y_s, "mat", 0)],
        [(D_MODEL, _MXU)] * 4, rows=seq, ncol=4, name="ew_merge_bwd")

    d_ya_in = _mm(d_ya, w_ap4, mode="nt", name="mm_d_attn_gate", tm=1024, tn=ATTN_W, tk=512, b_blocked=True)
    g_w_ap = _mm(ya_in, d_ya, mode="tn", name="mm_g_w_attn_proj", tm=ATTN_W, tn=512, tk=1024, out_dtype=_WIRE,
                 out_blocked=True)

    def gate_a_bwd(dv, at, ag):
        f, df = _silu_and_grad(ag)
        return dv * f, dv * at * df

    d_attn, d_agate = _ew(gate_a_bwd, [(d_ya_in, "mat", 0), (attn, "mat", 0), (proj, "mat", OFF_AGATE)],
                          [(ATTN_W, F32), (ATTN_W, _MXU)], rows=seq, ncol=2, name="ew_attn_gate_bwd")
    dq_h, dk_h, dv_h, g_qw, g_kw, g_sk = _attn_bwd(q, k, v, qw, kw, sinks, lse, _heads(d_attn, N_Q_HEADS))

    d_ys_in = _mm(d_ys, w_sp4, mode="nt", name="mm_d_ssm_gate", tm=1024, tn=SSM_W, tk=512, b_blocked=True)
    g_w_sp = _mm(ys_in, d_ys, mode="tn", name="mm_g_w_ssm_proj", tm=SSM_W, tn=512, tk=1024, out_dtype=_WIRE,
                 out_blocked=True)

    def gate_s_bwd(dv, ga, gb, ba, bb, z):
        a, sb = ga + ba, _sigmoid(gb + bb)
        f, df = _silu_and_grad(z)
        dga = dv * sb * f
        dgb = dv * a * f * sb * (1.0 - sb)
        return dga, dgb, dv * a * sb * df, _colsum(dga), _colsum(dgb)

    d_glu_a, d_glu_b, d_z, g_bga, g_bgb = _ew(
        gate_s_bwd, [(d_ys_in, "mat", 0), (glu, "mat", 0), (glu, "mat", 2), (bg, "row", 0), (bg, "row", 2),
                     (proj, "mat", OFF_Z)],
        [(SSM_W, _MXU)] * 3, rows=seq, ncol=2, n_acc=2, name="ew_ssm_gate_bwd")
    d_glu = jnp.concatenate([d_glu_a, d_glu_b], axis=1)
    d_yg = _mm(d_glu, w_glu4, mode="nt", name="mm_d_gelu", tm=1024, tn=SSM_W, tk=512, b_blocked=True)
    g_w_glu = _mm(yg, d_glu, mode="tn", name="mm_g_w_glu", tm=SSM_W, tn=512, tk=1024, out_dtype=_WIRE, out_blocked=True)

    def gelu_bwd(dv, yv):
        return (jax.vjp(jax.nn.gelu, yv)[1](dv)[0],)

    (d_yssm,) = _ew(gelu_bwd, [(d_yg, "mat", 0), (y_ssm, "mat", 0)], [(SSM_W, F32)], rows=seq, ncol=2, name="ew_gelu_bwd")
    (du_scan, g_bre, g_bim, g_cre, g_cim, g_dsk, g_abr, g_abi, g_cfr, g_cfi) = _ssm_bwd(
        _to_scan_order(d_yssm), u_scan, s_re, s_im, i_re, i_im, bre_blk, bim_blk, cre_blk, cim_blk, dsk, coef)
    g_are, g_aim, g_ldt = _ssm_params_bwd(a_re, a_im, log_dt.reshape(N_GROUPS, 1),
                                          *[t.reshape(N_GROUPS, STATE) for t in (g_abr, g_abi, g_cfr, g_cfi)])
    d_u = _from_scan_order(du_scan)

    d_proj = jnp.concatenate(
        [_unheads(dq_h).astype(_MXU), _unheads(dk_h).astype(_MXU), _unheads(dv_h).astype(_MXU), d_agate,
         d_u.astype(_MXU), d_z, d_ga, d_gs], axis=1)
    d_h = _mm(d_proj, w_in4, mode="nt", name="mm_d_h", tm=512, tn=D_MODEL, tk=IN_W // 4, b_blocked=True)
    g_w_in = _mm(h, d_proj, mode="tn", name="mm_g_w_in", tm=1024, tn=IN_W // 4, tk=512, out_dtype=_WIRE,
                 out_blocked=True)
    grad_x, g_nw = _rms_bwd(d_h, x, rstd, nw, dout)

    big = [g_w_in, g_w_ap, g_w_glu, g_w_sp, g_w_out.reshape(N_CHIPS, D_MODEL // N_CHIPS, D_MODEL)]
    small = dict(
        norm_w=g_nw.reshape(D_MODEL), q_norm_w=g_qw.reshape(HEAD_DIM), k_norm_w=g_kw.reshape(HEAD_DIM),
        sinks=g_sk[:, :, 0].reshape(N_Q_HEADS), A_re=g_are, A_im=g_aim, log_dt=g_ldt.reshape(N_GROUPS),
        B_re=_diag_of_b(g_bre), B_im=_diag_of_b(g_bim), C_re=_diag_of_c(g_cre), C_im=_diag_of_c(g_cim),
        D_skip=g_dsk.reshape(N_GROUPS, GROUP), b_glu=jnp.concatenate([g_bga, g_bgb], axis=1).reshape(D_MODEL))
    return loss, grad_x, big, small


_SMALL = ["norm_w", "q_norm_w", "k_norm_w", "sinks", "A_re", "A_im", "log_dt", "B_re", "B_im", "C_re", "C_im",
          "D_skip", "b_glu"]
_BIG = ["w_in", "w_attn_proj", "w_glu", "w_ssm_proj", "w_out"]
_ORDER = ["norm_w", "w_in", "q_norm_w", "k_norm_w", "sinks", "w_attn_proj", "A_re", "A_im", "log_dt", "B_re", "B_im",
          "C_re", "C_im", "D_skip", "w_glu", "b_glu", "w_ssm_proj", "w_out"]
_PACK_W = 1024


def _pack_small(d):
    flat = jnp.concatenate([d[n].reshape(-1).astype(F32) for n in _SMALL])
    unit = SUBLANES * _PACK_W
    pad = (-flat.shape[0]) % unit
    return jnp.pad(flat, (0, pad)).reshape(-1, _PACK_W)


def _unpack_small(packed, like):
    flat = packed.reshape(-1)
    out, pos = {}, 0
    for n in _SMALL:
        size = like[n].size
        out[n] = flat[pos:pos + size].reshape(like[n].shape)
        pos += size
    return out


def kernel(x, norm_w, w_in, q_norm_w, k_norm_w, sinks, w_attn_proj, A_re, A_im, log_dt, B_re, B_im, C_re, C_im, D_skip, w_glu, b_glu, w_ssm_proj, w_out, loss_target, m_norm_w, m_w_in, m_q_norm_w, m_k_norm_w, m_sinks, m_w_attn_proj, m_A_re, m_A_im, m_log_dt, m_B_re, m_B_im, m_C_re, m_C_im, m_D_skip, m_w_glu, m_b_glu, m_w_ssm_proj, m_w_out, v_norm_w, v_w_in, v_q_norm_w, v_k_norm_w, v_sinks, v_w_attn_proj, v_A_re, v_A_im, v_log_dt, v_B_re, v_B_im, v_C_re, v_C_im, v_D_skip, v_w_glu, v_b_glu, v_w_ssm_proj, v_w_out):
    w = dict(norm_w=norm_w, w_in=w_in, q_norm_w=q_norm_w, k_norm_w=k_norm_w, sinks=sinks, w_attn_proj=w_attn_proj,
             A_re=A_re, A_im=A_im, log_dt=log_dt, B_re=B_re, B_im=B_im, C_re=C_re, C_im=C_im, D_skip=D_skip,
             w_glu=w_glu, b_glu=b_glu, w_ssm_proj=w_ssm_proj, w_out=w_out)
    m = dict(norm_w=m_norm_w, w_in=m_w_in, q_norm_w=m_q_norm_w, k_norm_w=m_k_norm_w, sinks=m_sinks,
             w_attn_proj=m_w_attn_proj, A_re=m_A_re, A_im=m_A_im, log_dt=m_log_dt, B_re=m_B_re, B_im=m_B_im,
             C_re=m_C_re, C_im=m_C_im, D_skip=m_D_skip, w_glu=m_w_glu, b_glu=m_b_glu, w_ssm_proj=m_w_ssm_proj,
             w_out=m_w_out)
    v = dict(norm_w=v_norm_w, w_in=v_w_in, q_norm_w=v_q_norm_w, k_norm_w=v_k_norm_w, sinks=v_sinks,
             w_attn_proj=v_w_attn_proj, A_re=v_A_re, A_im=v_A_im, log_dt=v_log_dt, B_re=v_B_re, B_im=v_B_im,
             C_re=v_C_re, C_im=v_C_im, D_skip=v_D_skip, w_glu=v_w_glu, b_glu=v_b_glu, w_ssm_proj=v_w_ssm_proj,
             w_out=v_w_out)

    full = _gather_weights([w[n].astype(_WIRE) for n in _BIG])
    w_in4, w_ap4, w_glu4, w_sp4, w_out4 = full
    loss, grad_x, big, small = _local_step(
        x[0], loss_target[0], norm_w, w_in4, q_norm_w, k_norm_w, sinks, w_ap4, A_re, A_im, log_dt, B_re, B_im, C_re,
        C_im, D_skip, w_glu4, b_glu, w_sp4, w_out4.reshape(D_MODEL, D_MODEL))
    loss = lax.psum(loss, ("x", "y", "c"))

    grads = dict(zip(_BIG, _reduce_scatter(big)))
    small_sum = _sum8(_all_gather_small(_pack_small(small)))
    grads.update(_unpack_small(small_sum, w))

    delta, new_m, new_v = {}, {}, {}
    for n in _BIG:
        delta[n], new_m[n], new_v[n] = _adamw(w[n], grads[n], m[n], v[n], name=f"adamw_{n}", tm=128)
    packed = _adamw(_pack_small(w), small_sum, _pack_small(m), _pack_small(v), name="adamw_small", tm=SUBLANES)
    for out, p in zip((delta, new_m, new_v), packed):
        out.update(_unpack_small(p, w))

    return (loss, grad_x[None], *[grads[n] for n in _ORDER], *[delta[n] for n in _ORDER],
            *[new_m[n] for n in _ORDER], *[new_v[n] for n in _ORDER])
```

```python
import functools
import math

import jax
import jax.numpy as jnp
from jax import lax
from jax.experimental import pallas as pl
from jax.experimental.pallas import tpu as pltpu

F32 = jnp.float32
_MXU = jnp.bfloat16
_WIRE = jnp.bfloat16

LANES = 128
SUBLANES = 8
VMEM_LIMIT = 56 * 1024 * 1024

D_MODEL = 2048
HEAD_DIM = 64
N_Q_HEADS = 16
N_KV_HEADS = 4
Q_PER_KV = 4
ATTN_W = 1024
KV_W = 256
WINDOW = 128
SSM_W = 1024
GROUP = 16
N_GROUPS = 64
STATE = 64
N_STATES = N_GROUPS * STATE
IN_W = 8704
NORM_EPS = 1e-6
N_CHIPS = 4
CW = 512
OFF_AGATE, OFF_U, OFF_Z, OFF_GA, OFF_GS = 3, 5, 7, 9, 13

SSM_T = 256
SSM_L = SSM_T // SUBLANES
SSM_JB = 8
SSM_SB = N_STATES // SSM_JB

ADAM_LR, ADAM_B1, ADAM_B2, ADAM_EPS, ADAM_WD, ADAM_STEP = 0.001, 0.9, 0.999, 1e-08, 0.01, 10

MESH = pl.DeviceIdType.MESH
_ANY = pl.BlockSpec(memory_space=pl.ANY)


def _params(sem=None):
    return pltpu.CompilerParams(dimension_semantics=sem, vmem_limit_bytes=VMEM_LIMIT)


def _mm(a, b, *, mode, name, tm, tn, tk, out_dtype=F32, b_blocked=False, out_blocked=False):
    if mode == "tn":
        K, M = a.shape
    else:
        M, K = a.shape
    if mode == "nn":
        N = b.shape[0] * b.shape[2] if b_blocked else b.shape[1]
    elif mode == "nt":
        N = b.shape[1] if b_blocked else b.shape[0]
    else:
        N = b.shape[1]
    tm, tn, tk = min(tm, M), min(tn, N), min(tk, K)
    nj, ni, nk = N // tn, M // tm, K // tk
    assert nj * tn == N and ni * tm == M and nk * tk == K, (name, M, N, K)
    dims = {"nn": (((1,), (0,)), ((), ())), "nt": (((1,), (1,)), ((), ())), "tn": (((0,), (0,)), ((), ()))}[mode]

    if mode == "tn":
        a_spec = pl.BlockSpec((tk, tm), lambda j, i, k: (k, i))
    else:
        a_spec = pl.BlockSpec((tm, tk), lambda j, i, k: (i, k))
    if mode == "nn":
        if b_blocked:
            assert b.shape[0] == nj and b.shape[2] == tn
            b_spec = pl.BlockSpec((None, tk, tn), lambda j, i, k: (j, k, 0))
        else:
            b_spec = pl.BlockSpec((tk, tn), lambda j, i, k: (k, j))
    elif mode == "nt":
        if b_blocked:
            assert b.shape[0] == nk and b.shape[2] == tk
            b_spec = pl.BlockSpec((None, tn, tk), lambda j, i, k: (k, j, 0))
        else:
            b_spec = pl.BlockSpec((tn, tk), lambda j, i, k: (j, k))
    else:
        b_spec = pl.BlockSpec((tk, tn), lambda j, i, k: (k, j))
    if out_blocked:
        assert nj == N_CHIPS
        o_spec = pl.BlockSpec((None, tm, tn), lambda j, i, k: (j, i, 0))
        o_shape = jax.ShapeDtypeStruct((nj, M, tn), out_dtype)
    else:
        o_spec = pl.BlockSpec((tm, tn), lambda j, i, k: (i, j))
        o_shape = jax.ShapeDtypeStruct((M, N), out_dtype)
    use_acc = nk > 1 and out_dtype != F32

    def body(a_ref, b_ref, o_ref, *scratch):
        part = lax.dot_general(a_ref[...].astype(_MXU), b_ref[...].astype(_MXU), dims,
                               preferred_element_type=F32)
        if nk == 1:
            o_ref[...] = part.astype(o_ref.dtype)
            return
        k = pl.program_id(2)
        acc = scratch[0] if use_acc else o_ref

        @pl.when(k == 0)
        def _():
            acc[...] = part

        @pl.when(k > 0)
        def _():
            acc[...] += part

        if use_acc:
            @pl.when(k == nk - 1)
            def _():
                o_ref[...] = acc[...].astype(o_ref.dtype)

    return pl.pallas_call(
        body, name=name, grid=(nj, ni, nk), in_specs=[a_spec, b_spec], out_specs=o_spec, out_shape=o_shape,
        scratch_shapes=[pltpu.VMEM((tm, tn), F32)] if use_acc else [],
        compiler_params=_params(("parallel", "parallel", "arbitrary")),
    )(a, b)


def _ew(fn, ins, outs, *, rows, ncol, name, n_acc=0, tm=512):
    n_in, n_out = len(ins), len(outs)
    tm = min(tm, rows)
    in_specs = []
    for _, kind, col0 in ins:
        if kind == "mat":
            in_specs.append(pl.BlockSpec((tm, CW), lambda j, i, c0=col0: (i, c0 + j)))
        else:
            in_specs.append(pl.BlockSpec((1, CW), lambda j, i, c0=col0: (0, c0 + j)))
    out_specs = [pl.BlockSpec((tm, CW), lambda j, i: (i, j)) for _ in outs]
    out_shape = [jax.ShapeDtypeStruct((rows, w), dt) for w, dt in outs]
    for _ in range(n_acc):
        out_specs.append(pl.BlockSpec((1, CW), lambda j, i: (0, j)))
        out_shape.append(jax.ShapeDtypeStruct((1, ncol * CW), F32))

    def body(*refs):
        vals = fn(*[r[...] for r in refs[:n_in]])
        for r, v in zip(refs[n_in:n_in + n_out], vals[:n_out]):
            r[...] = v.astype(r.dtype)
        i = pl.program_id(1)
        for r, v in zip(refs[n_in + n_out:], vals[n_out:]):
            @pl.when(i == 0)
            def _(r=r, v=v):
                r[...] = v

            @pl.when(i > 0)
            def _(r=r, v=v):
                r[...] += v

    res = pl.pallas_call(
        body, name=name, grid=(ncol, rows // tm), in_specs=in_specs, out_specs=out_specs, out_shape=out_shape,
        compiler_params=_params(("parallel", "arbitrary")),
    )(*[a for a, _, _ in ins])
    return res


def _colsum(v):
    return jnp.sum(v, axis=0, keepdims=True)


def _sigmoid(v):
    return jax.nn.sigmoid(v)


def _silu_and_grad(v):
    s = _sigmoid(v)
    return v * s, s * (1.0 + v * (1.0 - s))


def _rms_fwd(x, w, *, tm=512):
    rows, d = x.shape

    def body(x_ref, w_ref, h_ref, r_ref):
        xv = x_ref[...]
        r = lax.rsqrt(jnp.mean(xv * xv, axis=-1, keepdims=True) + NORM_EPS)
        h_ref[...] = (xv * r * w_ref[...]).astype(h_ref.dtype)
        r_ref[...] = r

    return pl.pallas_call(
        body, name="rms_fwd", grid=(rows // tm,),
        in_specs=[pl.BlockSpec((tm, d), lambda i: (i, 0)), pl.BlockSpec((1, d), lambda i: (0, 0))],
        out_specs=[pl.BlockSpec((tm, d), lambda i: (i, 0)), pl.BlockSpec((tm, 1), lambda i: (i, 0))],
        out_shape=[jax.ShapeDtypeStruct((rows, d), _MXU), jax.ShapeDtypeStruct((rows, 1), F32)],
        compiler_params=_params(("arbitrary",)),
    )(x, w)


def _rms_bwd(dh, x, rstd, w, dout, *, tm=256):
    rows, d = x.shape

    def body(dh_ref, x_ref, r_ref, w_ref, do_ref, gx_ref, gw_ref):
        dhv, xv, r, wv = dh_ref[...], x_ref[...], r_ref[...], w_ref[...]
        xr = xv * r
        t = jnp.mean(dhv * wv * xr, axis=-1, keepdims=True)
        gx_ref[...] = do_ref[...] + r * (wv * dhv - xr * t)
        part = _colsum(dhv * xr)
        i = pl.program_id(0)

        @pl.when(i == 0)
        def _():
            gw_ref[...] = part

        @pl.when(i > 0)
        def _():
            gw_ref[...] += part

    return pl.pallas_call(
        body, name="rms_bwd", grid=(rows // tm,),
        in_specs=[pl.BlockSpec((tm, d), lambda i: (i, 0)), pl.BlockSpec((tm, d), lambda i: (i, 0)),
                  pl.BlockSpec((tm, 1), lambda i: (i, 0)), pl.BlockSpec((1, d), lambda i: (0, 0)),
                  pl.BlockSpec((tm, d), lambda i: (i, 0))],
        out_specs=[pl.BlockSpec((tm, d), lambda i: (i, 0)), pl.BlockSpec((1, d), lambda i: (0, 0))],
        out_shape=[jax.ShapeDtypeStruct((rows, d), F32), jax.ShapeDtypeStruct((1, d), F32)],
        compiler_params=_params(("arbitrary",)),
    )(dh, x, rstd, w, dout)


_NT = (((1,), (1,)), ((), ()))
_TN = (((0,), (0,)), ((), ()))


def _head_rstd(v):
    return lax.rsqrt(jnp.mean(v * v, axis=-1, keepdims=True) + NORM_EPS)


def _band_mask(n):
    qi = lax.broadcasted_iota(jnp.int32, (WINDOW, 2 * WINDOW), 0) + WINDOW
    kj = lax.broadcasted_iota(jnp.int32, (WINDOW, 2 * WINDOW), 1)
    diff = qi - kj
    first_key = jnp.where(n > 0, 0, WINDOW)
    return (diff >= 0) & (diff < WINDOW) & (kj >= first_key)


def _attn_specs(nblk, rev):
    def cur(g, n):
        return (nblk - 1 - n) if rev else n

    q_spec = pl.BlockSpec((Q_PER_KV, WINDOW, HEAD_DIM), lambda g, n: (g, cur(g, n), 0))
    kc_spec = pl.BlockSpec((1, WINDOW, HEAD_DIM), lambda g, n: (g, cur(g, n), 0))
    kp_spec = pl.BlockSpec((1, WINDOW, HEAD_DIM), lambda g, n: (g, jnp.maximum(cur(g, n) - 1, 0), 0))
    w_spec = pl.BlockSpec((1, HEAD_DIM), lambda g, n: (0, 0))
    l_spec = pl.BlockSpec((Q_PER_KV, WINDOW, 1), lambda g, n: (g, cur(g, n), 0))
    return q_spec, kc_spec, kp_spec, w_spec, l_spec


def _attn_fwd(q, k, v, qw, kw, sinks):
    seq = q.shape[1]
    nblk = seq // WINDOW
    scale = 1.0 / math.sqrt(HEAD_DIM)
    q_spec, kc_spec, kp_spec, w_spec, l_spec = _attn_specs(nblk, False)

    def body(sink_ref, q_ref, kc_ref, kp_ref, vc_ref, vp_ref, qw_ref, kw_ref, o_ref, lse_ref):
        g, n = pl.program_id(0), pl.program_id(1)
        kwv, qwv = kw_ref[...], qw_ref[...]
        kc, kp = kc_ref[0], kp_ref[0]
        kk = jnp.concatenate([kp * _head_rstd(kp) * kwv, kc * _head_rstd(kc) * kwv], axis=0).astype(_MXU)
        vv = jnp.concatenate([vp_ref[0], vc_ref[0]], axis=0).astype(_MXU)
        valid = _band_mask(n)
        for r in range(Q_PER_KV):
            qr = q_ref[r]
            qn = (qr * _head_rstd(qr) * qwv).astype(_MXU)
            s = lax.dot_general(qn, kk, _NT, preferred_element_type=F32) * scale
            s = jnp.where(valid, s, -1e30)
            sink = sink_ref[g * Q_PER_KV + r]
            m = jnp.maximum(jnp.max(s, axis=-1, keepdims=True), sink)
            e = jnp.exp(s - m)
            z = jnp.sum(e, axis=-1, keepdims=True) + jnp.exp(sink - m)
            p = e / z
            o_ref[r] = jnp.dot(p.astype(_MXU), vv, preferred_element_type=F32)
            lse_ref[r] = m + jnp.log(z)

    return pl.pallas_call(
        body, name="attn_fwd", grid=(N_KV_HEADS, nblk),
        in_specs=[pl.BlockSpec(memory_space=pltpu.SMEM), q_spec, kc_spec, kp_spec, kc_spec, kp_spec, w_spec, w_spec],
        out_specs=[q_spec, l_spec],
        out_shape=[jax.ShapeDtypeStruct((N_Q_HEADS, seq, HEAD_DIM), F32),
                   jax.ShapeDtypeStruct((N_Q_HEADS, seq, 1), F32)],
        compiler_params=_params(("arbitrary", "arbitrary")),
    )(sinks, q, k, k, v, v, qw, kw)


def _attn_bwd(q, k, v, qw, kw, sinks, lse, do):
    seq = q.shape[1]
    nblk = seq // WINDOW
    scale = 1.0 / math.sqrt(HEAD_DIM)
    q_spec, kc_spec, kp_spec, w_spec, l_spec = _attn_specs(nblk, True)
    s_spec = pl.BlockSpec((1, Q_PER_KV, LANES), lambda g, n: (g, 0, 0))

    def body(sink_ref, q_ref, kc_ref, kp_ref, vc_ref, vp_ref, qw_ref, kw_ref, lse_ref, do_ref,
             dq_ref, dk_ref, dv_ref, dqw_ref, dkw_ref, dsk_ref, cdk, cdv):
        g, step = pl.program_id(0), pl.program_id(1)
        n = nblk - 1 - step

        @pl.when(step == 0)
        def _():
            cdk[...] = jnp.zeros_like(cdk)
            cdv[...] = jnp.zeros_like(cdv)
            dsk_ref[...] = jnp.zeros_like(dsk_ref)

        @pl.when((step == 0) & (g == 0))
        def _():
            dqw_ref[...] = jnp.zeros_like(dqw_ref)
            dkw_ref[...] = jnp.zeros_like(dkw_ref)

        kwv, qwv = kw_ref[...], qw_ref[...]
        kc, kp = kc_ref[0], kp_ref[0]
        rc = _head_rstd(kc)
        kk = jnp.concatenate([kp * _head_rstd(kp) * kwv, kc * rc * kwv], axis=0).astype(_MXU)
        vv = jnp.concatenate([vp_ref[0], vc_ref[0]], axis=0).astype(_MXU)
        valid = _band_mask(n)
        dkk = jnp.zeros((2 * WINDOW, HEAD_DIM), F32)
        dvv = jnp.zeros((2 * WINDOW, HEAD_DIM), F32)
        dqw = jnp.zeros((1, HEAD_DIM), F32)
        for r in range(Q_PER_KV):
            qr = q_ref[r]
            rq = _head_rstd(qr)
            qn = (qr * rq * qwv).astype(_MXU)
            s = lax.dot_general(qn, kk, _NT, preferred_element_type=F32) * scale
            s = jnp.where(valid, s, -1e30)
            lse_r = lse_ref[r]
            p = jnp.exp(s - lse_r)
            dob = do_ref[r].astype(_MXU)
            dp = lax.dot_general(dob, vv, _NT, preferred_element_type=F32)
            dsum = jnp.sum(p * dp, axis=-1, keepdims=True)
            ds = (p * (dp - dsum) * scale).astype(_MXU)
            sink = sink_ref[g * Q_PER_KV + r]
            dsink = _colsum(-jnp.exp(sink - lse_r) * dsum)
            dsk_ref[0, r:r + 1, :] += jnp.broadcast_to(dsink, (1, LANES))
            dvv = dvv + lax.dot_general(p.astype(_MXU), dob, _TN, preferred_element_type=F32)
            dqn = jnp.dot(ds, kk, preferred_element_type=F32)
            dkk = dkk + lax.dot_general(ds, qn, _TN, preferred_element_type=F32)
            qx = qr * rq
            dq_ref[r] = rq * (qwv * dqn - qx * jnp.mean(dqn * qwv * qx, axis=-1, keepdims=True))
            dqw = dqw + _colsum(dqn * qx)
        dkn = dkk[WINDOW:] + cdk[...]
        kx = kc * rc
        dk_ref[0] = rc * (kwv * dkn - kx * jnp.mean(dkn * kwv * kx, axis=-1, keepdims=True))
        dv_ref[0] = dvv[WINDOW:] + cdv[...]
        cdk[...] = dkk[:WINDOW]
        cdv[...] = dvv[:WINDOW]
        dqw_ref[...] += dqw
        dkw_ref[...] += _colsum(dkn * kx)

    return pl.pallas_call(
        body, name="attn_bwd", grid=(N_KV_HEADS, nblk),
        in_specs=[pl.BlockSpec(memory_space=pltpu.SMEM), q_spec, kc_spec, kp_spec, kc_spec, kp_spec, w_spec, w_spec,
                  l_spec, q_spec],
        out_specs=[q_spec, kc_spec, kc_spec, w_spec, w_spec, s_spec],
        out_shape=[jax.ShapeDtypeStruct((N_Q_HEADS, seq, HEAD_DIM), F32),
                   jax.ShapeDtypeStruct((N_KV_HEADS, seq, HEAD_DIM), F32),
                   jax.ShapeDtypeStruct((N_KV_HEADS, seq, HEAD_DIM), F32),
                   jax.ShapeDtypeStruct((1, HEAD_DIM), F32), jax.ShapeDtypeStruct((1, HEAD_DIM), F32),
                   jax.ShapeDtypeStruct((N_KV_HEADS, Q_PER_KV, LANES), F32)],
        scratch_shapes=[pltpu.VMEM((WINDOW, HEAD_DIM), F32), pltpu.VMEM((WINDOW, HEAD_DIM), F32)],
        compiler_params=_params(("arbitrary", "arbitrary")),
    )(sinks, q, k, k, v, v, qw, kw, lse, do)


def _ssm_discretise(a_re, a_im, log_dt):
    dt = jnp.exp(log_dt)
    mag = jnp.exp(dt * a_re)
    ab_re = mag * jnp.cos(dt * a_im)
    ab_im = mag * jnp.sin(dt * a_im)
    num_re = ab_re - 1.0
    num_im = ab_im
    den = a_re * a_re + a_im * a_im
    cf_re = (num_re * a_re + num_im * a_im) / den
    cf_im = (num_im * a_re - num_re * a_im) / den
    return ab_re, ab_im, cf_re, cf_im


def _ssm_params_fwd(a_re, a_im, log_dt):
    shp = jax.ShapeDtypeStruct(a_re.shape, F32)

    def body(are_ref, aim_ref, ldt_ref, abr_ref, abi_ref, cfr_ref, cfi_ref, alr_ref, ali_ref):
        abr, abi, cfr, cfi = _ssm_discretise(are_ref[...], aim_ref[...], ldt_ref[...])
        abr_ref[...], abi_ref[...], cfr_ref[...], cfi_ref[...] = abr, abi, cfr, cfi
        pr, pi = abr, abi
        for _ in range(int(math.log2(SSM_L))):
            pr, pi = pr * pr - pi * pi, 2.0 * pr * pi
        alr_ref[...], ali_ref[...] = pr, pi

    return pl.pallas_call(body, name="ssm_params_fwd", out_shape=[shp] * 6)(a_re, a_im, log_dt)


def _ssm_params_bwd(a_re, a_im, log_dt, d_abr, d_abi, d_cfr, d_cfi):
    def body(are_ref, aim_ref, ldt_ref, g0, g1, g2, g3, dare_ref, daim_ref, dldt_ref):
        _, vjp = jax.vjp(_ssm_discretise, are_ref[...], aim_ref[...], ldt_ref[...])
        dare_ref[...], daim_ref[...], dldt_ref[...] = vjp((g0[...], g1[...], g2[...], g3[...]))

    return pl.pallas_call(
        body, name="ssm_params_bwd",
        out_shape=[jax.ShapeDtypeStruct(a_re.shape, F32), jax.ShapeDtypeStruct(a_im.shape, F32),
                   jax.ShapeDtypeStruct(log_dt.shape, F32)],
    )(a_re, a_im, log_dt, d_abr, d_abi, d_cfr, d_cfi)


def _scan_cols(j):
    return pl.ds(j * SSM_SB, SSM_SB)


def _rows8(r):
    return pl.ds(pl.multiple_of(r * SUBLANES, SUBLANES), SUBLANES)


def _bcast8(row):
    return jnp.broadcast_to(row, (SUBLANES, row.shape[-1]))


def _ssm_fwd(u, b_re, b_im, c_re, c_im, d_skip, coef):
    seq = u.shape[0]
    nc = seq // SSM_T
    T, L = SSM_T, SSM_L

    def body(u_ref, bre_ref, bim_ref, cre_ref, cim_ref, d_ref, are_ref, aim_ref, cfr_ref, cfi_ref, alr_ref, ali_ref,
             y_ref, sre_ref, sim_ref, ire_ref, iim_ref, car_re, car_im, end_re, end_im):
        c = pl.program_id(0)

        @pl.when(c == 0)
        def _():
            car_re[...] = jnp.zeros_like(car_re)
            car_im[...] = jnp.zeros_like(car_im)

        for j in range(SSM_JB):
            ub = u_ref[:, j * LANES:(j + 1) * LANES].astype(_MXU)
            bur = jnp.dot(ub, bre_ref[j], preferred_element_type=F32)
            bui = jnp.dot(ub, bim_ref[j], preferred_element_type=F32)
            cfr, cfi = cfr_ref[:, _scan_cols(j)], cfi_ref[:, _scan_cols(j)]
            sre_ref[:, _scan_cols(j)] = cfr * bur - cfi * bui
            sim_ref[:, _scan_cols(j)] = cfr * bui + cfi * bur

        for j in range(SSM_JB):
            cols = _scan_cols(j)
            ar, ai = _bcast8(are_ref[:, cols]), _bcast8(aim_ref[:, cols])

            def step1(r, s, cols=cols, ar=ar, ai=ai):
                sr, si = s
                rows = _rows8(r)
                return (ar * sr - ai * si + sre_ref[rows, cols], ar * si + ai * sr + sim_ref[rows, cols])

            zero = jnp.zeros((SUBLANES, SSM_SB), F32)
            er, ei = lax.fori_loop(0, L, step1, (zero, zero), unroll=4)
            end_re[:, cols] = er
            end_im[:, cols] = ei

        alr, ali = alr_ref[...], ali_ref[...]
        cr, ci = car_re[...], car_im[...]
        ire_ref[0:1, :] = cr
        iim_ref[0:1, :] = ci
        for i in range(1, SUBLANES):
            er, ei = end_re[i - 1:i, :], end_im[i - 1:i, :]
            cr, ci = alr * cr - ali * ci + er, alr * ci + ali * cr + ei
            ire_ref[i:i + 1, :] = cr
            iim_ref[i:i + 1, :] = ci

        for j in range(SSM_JB):
            cols = _scan_cols(j)
            ar, ai = _bcast8(are_ref[:, cols]), _bcast8(aim_ref[:, cols])

            def step2(r, s, cols=cols, ar=ar, ai=ai):
                sr, si = s
                rows = _rows8(r)
                nr = ar * sr - ai * si + sre_ref[rows, cols]
                ni = ar * si + ai * sr + sim_ref[rows, cols]
                sre_ref[rows, cols] = nr
                sim_ref[rows, cols] = ni
                return nr, ni

            lax.fori_loop(0, L, step2, (ire_ref[:, cols], iim_ref[:, cols]), unroll=4)

        car_re[...] = sre_ref[T - 1:T, :]
        car_im[...] = sim_ref[T - 1:T, :]

        for j in range(SSM_JB):
            cols = _scan_cols(j)
            ch = slice(j * LANES, (j + 1) * LANES)
            y = (jnp.dot(sre_ref[:, cols].astype(_MXU), cre_ref[j], preferred_element_type=F32)
                 - jnp.dot(sim_ref[:, cols].astype(_MXU), cim_ref[j], preferred_element_type=F32))
            y_ref[:, ch] = y + d_ref[:, ch] * u_ref[:, ch]

    tok = pl.BlockSpec((T, SSM_W), lambda c: (c, 0))
    st = pl.BlockSpec((T, N_STATES), lambda c: (c, 0))
    ini = pl.BlockSpec((None, SUBLANES, N_STATES), lambda c: (c, 0, 0))
    bsp = pl.BlockSpec((SSM_JB, LANES, SSM_SB), lambda c: (0, 0, 0))
    csp = pl.BlockSpec((SSM_JB, SSM_SB, LANES), lambda c: (0, 0, 0))
    row_w = pl.BlockSpec((1, SSM_W), lambda c: (0, 0))
    row_s = pl.BlockSpec((1, N_STATES), lambda c: (0, 0))
    return pl.pallas_call(
        body, name="ssm_fwd", grid=(nc,),
        in_specs=[tok, bsp, bsp, csp, csp, row_w] + [row_s] * 6,
        out_specs=[tok, st, st, ini, ini],
        out_shape=[jax.ShapeDtypeStruct((seq, SSM_W), F32),
                   jax.ShapeDtypeStruct((seq, N_STATES), F32), jax.ShapeDtypeStruct((seq, N_STATES), F32),
                   jax.ShapeDtypeStruct((nc, SUBLANES, N_STATES), F32),
                   jax.ShapeDtypeStruct((nc, SUBLANES, N_STATES), F32)],
        scratch_shapes=[pltpu.VMEM((1, N_STATES), F32), pltpu.VMEM((1, N_STATES), F32),
                        pltpu.VMEM((SUBLANES, N_STATES), F32), pltpu.VMEM((SUBLANES, N_STATES), F32)],
        compiler_params=_params(("arbitrary",)),
    )(u, b_re, b_im, c_re, c_im, d_skip, *coef)


def _ssm_bwd(dy, u, s_re, s_im, i_re, i_im, b_re, b_im, c_re, c_im, d_skip, coef):
    seq = u.shape[0]
    nc = seq // SSM_T
    T, L = SSM_T, SSM_L

    def body(dy_ref, u_ref, sre_ref, sim_ref, ire_ref, iim_ref, bre_ref, bim_ref, cre_ref, cim_ref, d_ref,
             are_ref, aim_ref, cfr_ref, cfi_ref, alr_ref, ali_ref,
             du_ref, dbre_out, dbim_out, dcre_out, dcim_out, dd_ref, dar_ref, dai_ref, dcfr_ref, dcfi_ref,
             lre, lim, car_re, car_im, end_re, end_im, ini_re, ini_im, dbre_ref, dbim_ref, dcre_ref, dcim_ref):
        step = pl.program_id(0)

        @pl.when(step == 0)
        def _():
            car_re[...] = jnp.zeros_like(car_re)
            car_im[...] = jnp.zeros_like(car_im)
            for ref in (dbre_ref, dbim_ref, dcre_ref, dcim_ref, dd_ref, dar_ref, dai_ref, dcfr_ref, dcfi_ref):
                ref[...] = jnp.zeros_like(ref)

        for j in range(SSM_JB):
            dyb = dy_ref[:, j * LANES:(j + 1) * LANES].astype(_MXU)
            lre[:, _scan_cols(j)] = lax.dot_general(dyb, cre_ref[j], _NT, preferred_element_type=F32)
            lim[:, _scan_cols(j)] = -lax.dot_general(dyb, cim_ref[j], _NT, preferred_element_type=F32)

        for j in range(SSM_JB):
            cols = _scan_cols(j)
            ar, ai = _bcast8(are_ref[:, cols]), _bcast8(aim_ref[:, cols])

            def step1(t, s, cols=cols, ar=ar, ai=ai):
                sr, si = s
                rows = _rows8(L - 1 - t)
                return (ar * sr + ai * si + lre[rows, cols], ar * si - ai * sr + lim[rows, cols])

            zero = jnp.zeros((SUBLANES, SSM_SB), F32)
            er, ei = lax.fori_loop(0, L, step1, (zero, zero), unroll=4)
            end_re[:, cols] = er
            end_im[:, cols] = ei

        alr, ali = alr_ref[...], ali_ref[...]
        cr, ci = car_re[...], car_im[...]
        ini_re[SUBLANES - 1:SUBLANES, :] = cr
        ini_im[SUBLANES - 1:SUBLANES, :] = ci
        for i in range(SUBLANES - 2, -1, -1):
            er, ei = end_re[i + 1:i + 2, :], end_im[i + 1:i + 2, :]
            cr, ci = alr * cr + ali * ci + er, alr * ci - ali * cr + ei
            ini_re[i:i + 1, :] = cr
            ini_im[i:i + 1, :] = ci

        for j in range(SSM_JB):
            cols = _scan_cols(j)
            ar, ai = _bcast8(are_ref[:, cols]), _bcast8(aim_ref[:, cols])

            def step2(t, s, cols=cols, ar=ar, ai=ai):
                sr, si = s
                rows = _rows8(L - 1 - t)
                nr = ar * sr + ai * si + lre[rows, cols]
                ni = ar * si - ai * sr + lim[rows, cols]
                lre[rows, cols] = nr
                lim[rows, cols] = ni
                return nr, ni

            lax.fori_loop(0, L, step2, (ini_re[:, cols], ini_im[:, cols]), unroll=4)

        car_re[...] = lre[0:1, :]
        car_im[...] = lim[0:1, :]

        head, tail, body_rows = slice(0, SUBLANES), slice(SUBLANES, T), slice(0, T - SUBLANES)
        for j in range(SSM_JB):
            cols = _scan_cols(j)
            ch = slice(j * LANES, (j + 1) * LANES)
            lr, li = lre[:, cols], lim[:, cols]
            dar_ref[:, cols] += (_colsum(lre[tail, cols] * sre_ref[body_rows, cols] + lim[tail, cols] * sim_ref[body_rows, cols])
                                 + _colsum(lre[head, cols] * ire_ref[:, cols] + lim[head, cols] * iim_ref[:, cols]))
            dai_ref[:, cols] += (_colsum(lim[tail, cols] * sre_ref[body_rows, cols] - lre[tail, cols] * sim_ref[body_rows, cols])
                                 + _colsum(lim[head, cols] * ire_ref[:, cols] - lre[head, cols] * iim_ref[:, cols]))
            uf = u_ref[:, ch]
            ub = uf.astype(_MXU)
            bur = jnp.dot(ub, bre_ref[j], preferred_element_type=F32)
            bui = jnp.dot(ub, bim_ref[j], preferred_element_type=F32)
            dcfr_ref[:, cols] += _colsum(lr * bur + li * bui)
            dcfi_ref[:, cols] += _colsum(li * bur - lr * bui)
            cfr, cfi = cfr_ref[:, cols], cfi_ref[:, cols]
            dbur = (cfr * lr + cfi * li).astype(_MXU)
            dbui = (cfr * li - cfi * lr).astype(_MXU)
            dyf = dy_ref[:, ch]
            dyb = dyf.astype(_MXU)
            du_ref[:, ch] = (lax.dot_general(dbur, bre_ref[j], _NT, preferred_element_type=F32)
                             + lax.dot_general(dbui, bim_ref[j], _NT, preferred_element_type=F32)
                             + d_ref[:, ch] * dyf)
            dbre_ref[j] += lax.dot_general(ub, dbur, _TN, preferred_element_type=F32)
            dbim_ref[j] += lax.dot_general(ub, dbui, _TN, preferred_element_type=F32)
            dcre_ref[j] += lax.dot_general(sre_ref[:, cols].astype(_MXU), dyb, _TN, preferred_element_type=F32)
            dcim_ref[j] -= lax.dot_general(sim_ref[:, cols].astype(_MXU), dyb, _TN, preferred_element_type=F32)
            dd_ref[:, ch] += _colsum(dyf * uf)

        @pl.when(step == nc - 1)
        def _():
            for acc, out in ((dbre_ref, dbre_out), (dbim_ref, dbim_out), (dcre_ref, dcre_out), (dcim_ref, dcim_out)):
                pltpu.sync_copy(acc, out)

    tok = pl.BlockSpec((T, SSM_W), lambda c: (nc - 1 - c, 0))
    st = pl.BlockSpec((T, N_STATES), lambda c: (nc - 1 - c, 0))
    ini = pl.BlockSpec((None, SUBLANES, N_STATES), lambda c: (nc - 1 - c, 0, 0))
    bsp = pl.BlockSpec((SSM_JB, LANES, SSM_SB), lambda c: (0, 0, 0))
    csp = pl.BlockSpec((SSM_JB, SSM_SB, LANES), lambda c: (0, 0, 0))
    row_w = pl.BlockSpec((1, SSM_W), lambda c: (0, 0))
    row_s = pl.BlockSpec((1, N_STATES), lambda c: (0, 0))
    big = pltpu.VMEM((T, N_STATES), F32)
    one = pltpu.VMEM((1, N_STATES), F32)
    eight = pltpu.VMEM((SUBLANES, N_STATES), F32)
    return pl.pallas_call(
        body, name="ssm_bwd", grid=(nc,),
        in_specs=[tok, tok, st, st, ini, ini, bsp, bsp, csp, csp, row_w] + [row_s] * 6,
        out_specs=[tok, _ANY, _ANY, _ANY, _ANY, row_w, row_s, row_s, row_s, row_s],
        out_shape=[jax.ShapeDtypeStruct((seq, SSM_W), F32),
                   jax.ShapeDtypeStruct((SSM_JB, LANES, SSM_SB), F32), jax.ShapeDtypeStruct((SSM_JB, LANES, SSM_SB), F32),
                   jax.ShapeDtypeStruct((SSM_JB, SSM_SB, LANES), F32), jax.ShapeDtypeStruct((SSM_JB, SSM_SB, LANES), F32),
                   jax.ShapeDtypeStruct((1, SSM_W), F32)] + [jax.ShapeDtypeStruct((1, N_STATES), F32)] * 4,
        scratch_shapes=[big, big, one, one, eight, eight, eight, eight,
                        pltpu.VMEM((SSM_JB, LANES, SSM_SB), F32), pltpu.VMEM((SSM_JB, LANES, SSM_SB), F32),
                        pltpu.VMEM((SSM_JB, SSM_SB, LANES), F32), pltpu.VMEM((SSM_JB, SSM_SB, LANES), F32)],
        compiler_params=_params(("arbitrary",)),
    )(dy, u, s_re, s_im, i_re, i_im, b_re, b_im, c_re, c_im, d_skip, *coef)


def _block_diag_b(b):
    t = b.reshape(SSM_JB, 8, STATE, GROUP).transpose(0, 1, 3, 2)
    eye = jnp.eye(8, dtype=b.dtype)
    return (t[:, :, :, None, :] * eye[None, :, None, :, None]).reshape(SSM_JB, LANES, SSM_SB)


def _block_diag_c(c):
    t = c.reshape(SSM_JB, 8, GROUP, STATE).transpose(0, 1, 3, 2)
    eye = jnp.eye(8, dtype=c.dtype)
    return (t[:, :, :, None, :] * eye[None, :, None, :, None]).reshape(SSM_JB, SSM_SB, LANES)


def _diag_of_b(blk):
    t = blk.reshape(SSM_JB, 8, GROUP, 8, STATE)
    d = jnp.stack([t[:, i, :, i, :] for i in range(8)], axis=1)
    return d.transpose(0, 1, 3, 2).reshape(N_GROUPS, STATE, GROUP)


def _diag_of_c(blk):
    t = blk.reshape(SSM_JB, 8, STATE, 8, GROUP)
    d = jnp.stack([t[:, i, :, i, :] for i in range(8)], axis=1)
    return d.transpose(0, 1, 3, 2).reshape(N_GROUPS, GROUP, STATE)


def _to_scan_order(v):
    seq, w = v.shape
    return v.reshape(seq // SSM_T, SUBLANES, SSM_L, w).transpose(0, 2, 1, 3).reshape(seq, w)


def _from_scan_order(v):
    seq, w = v.shape
    return v.reshape(seq // SSM_T, SSM_L, SUBLANES, w).transpose(0, 2, 1, 3).reshape(seq, w)


def _adamw_math(w, g, m, v):
    nm = ADAM_B1 * m + (1.0 - ADAM_B1) * g
    nv = ADAM_B2 * v + (1.0 - ADAM_B2) * jnp.square(g)
    m_hat = nm / (1.0 - ADAM_B1 ** ADAM_STEP)
    v_hat = nv / (1.0 - ADAM_B2 ** ADAM_STEP)
    return -ADAM_LR * (m_hat / (jnp.sqrt(v_hat) + ADAM_EPS) + ADAM_WD * w), nm, nv


def _adamw(w, g, m, v, *, name, tm):
    rows, cols = w.shape

    def body(w_ref, g_ref, m_ref, v_ref, d_ref, nm_ref, nv_ref):
        d_ref[...], nm_ref[...], nv_ref[...] = _adamw_math(w_ref[...], g_ref[...], m_ref[...], v_ref[...])

    spec = pl.BlockSpec((tm, cols), lambda i: (i, 0))
    shp = jax.ShapeDtypeStruct((rows, cols), F32)
    return pl.pallas_call(body, name=name, grid=(rows // tm,), in_specs=[spec] * 4, out_specs=[spec] * 3,
                          out_shape=[shp] * 3, compiler_params=_params(("arbitrary",)))(w, g, m, v)


def _place():
    x, y, c = lax.axis_index("x"), lax.axis_index("y"), lax.axis_index("c")
    chips = [(1 - x, y), (x, 1 - y), (1 - x, 1 - y)]
    return x, y, c, chips


def _remote(src, dst, send_sem, recv_sem, dev):
    return pltpu.make_async_remote_copy(src_ref=src, dst_ref=dst, send_sem=send_sem, recv_sem=recv_sem,
                                        device_id=dev, device_id_type=MESH)


def _place_shard(w, mine_arr, *, name, tm=256):
    rows, cols = w.shape

    def body(m_ref, w_ref, o_ref):
        o_ref[...] = w_ref[...].astype(o_ref.dtype)

    return pl.pallas_call(
        body, name=name,
        grid_spec=pltpu.PrefetchScalarGridSpec(
            num_scalar_prefetch=1, grid=(rows // tm,),
            in_specs=[pl.BlockSpec((tm, cols), lambda i, m: (i, 0))],
            out_specs=pl.BlockSpec((None, tm, cols), lambda i, m: (m[0], i, 0))),
        out_shape=jax.ShapeDtypeStruct((N_CHIPS, rows, cols), _WIRE),
        compiler_params=_params(("arbitrary",)),
    )(mine_arr, w)


def _gather_weights(fulls):
    n = len(fulls)

    def body(*refs):
        outs = refs[n:2 * n]
        send_sems, recv_sems = refs[2 * n:]
        x, y, c, chips = _place()
        mine = 2 * x + y
        sibling = (x, y, 1 - c)
        sends = []
        for t in range(n):
            half = pl.ds(c * (outs[t].shape[1] // 2), outs[t].shape[1] // 2)
            own = outs[t].at[mine, half]
            for k, chip in enumerate(chips):
                cp = _remote(own, own, send_sems.at[6 * t + k], recv_sems.at[6 * t + k], (*chip, c))
                cp.start()
                sends.append(cp)
        for t in range(n):
            half = pl.ds(c * (outs[t].shape[1] // 2), outs[t].shape[1] // 2)
            for k, chip in enumerate(chips):
                landed = outs[t].at[2 * chip[0] + chip[1], half]
                _remote(landed, landed, send_sems.at[6 * t + k], recv_sems.at[6 * t + k], (*chip, c)).wait_recv()
                cp = _remote(landed, landed, send_sems.at[6 * t + 3 + k], recv_sems.at[6 * t + 3 + k], sibling)
                cp.start()
                sends.append(cp)
        for t in range(n):
            other = pl.ds((1 - c) * (outs[t].shape[1] // 2), outs[t].shape[1] // 2)
            for k, chip in enumerate(chips):
                passed = outs[t].at[2 * chip[0] + chip[1], other]
                _remote(passed, passed, send_sems.at[6 * t + 3 + k], recv_sems.at[6 * t + 3 + k], sibling).wait_recv()
        for cp in sends:
            cp.wait_send()

    return pl.pallas_call(
        body, name="gather_weights", in_specs=[_ANY] * n, out_specs=[_ANY] * n,
        out_shape=[jax.ShapeDtypeStruct(f.shape, f.dtype) for f in fulls],
        input_output_aliases={t: t for t in range(n)},
        scratch_shapes=[pltpu.SemaphoreType.DMA((6 * n,)), pltpu.SemaphoreType.DMA((6 * n,))],
    )(*fulls)


def _swap_halves(grads):
    n = len(grads)

    def body(*refs):
        ins, outs = refs[:n], refs[n:2 * n]
        send_sems, recv_sems = refs[2 * n:]
        x, y, c, _ = _place()
        cps = []
        for t in range(n):
            r2 = ins[t].shape[1] // 2
            cp = _remote(ins[t].at[:, pl.ds((1 - c) * r2, r2), :], outs[t], send_sems.at[t], recv_sems.at[t],
                         (x, y, 1 - c))
            cp.start()
            cps.append(cp)
        for cp in cps:
            cp.wait()

    return pl.pallas_call(
        body, name="grad_swap_halves", in_specs=[_ANY] * n, out_specs=[_ANY] * n,
        out_shape=[jax.ShapeDtypeStruct((N_CHIPS, g.shape[1] // 2, g.shape[2]), g.dtype) for g in grads],
        scratch_shapes=[pltpu.SemaphoreType.DMA((n,)), pltpu.SemaphoreType.DMA((n,))],
    )(*grads)


def _scatter_chips(halves):
    n = len(halves)

    def body(*refs):
        ins, outs = refs[:n], refs[n:2 * n]
        send_sems, recv_sems = refs[2 * n:]
        x, y, c, chips = _place()
        cps = []
        for t in range(n):
            for k, chip in enumerate(chips):
                cp = _remote(ins[t].at[2 * chip[0] + chip[1]], outs[t].at[k], send_sems.at[3 * t + k],
                             recv_sems.at[3 * t + k], (*chip, c))
                cp.start()
                cps.append(cp)
        for cp in cps:
            cp.wait()

    return pl.pallas_call(
        body, name="grad_scatter_chips", in_specs=[_ANY] * n, out_specs=[_ANY] * n,
        out_shape=[jax.ShapeDtypeStruct((3,) + h.shape[1:], h.dtype) for h in halves],
        scratch_shapes=[pltpu.SemaphoreType.DMA((3 * n,)), pltpu.SemaphoreType.DMA((3 * n,))],
    )(*halves)


def _join_halves(totals):
    n = len(totals)

    def body(*refs):
        outs = refs[n:2 * n]
        send_sems, recv_sems = refs[2 * n:]
        x, y, c, _ = _place()
        cps = []
        for t in range(n):
            r2 = outs[t].shape[0] // 2
            mine = outs[t].at[pl.ds(c * r2, r2)]
            cp = _remote(mine, mine, send_sems.at[t], recv_sems.at[t], (x, y, 1 - c))
            cp.start()
            cps.append(cp)
        for t, cp in enumerate(cps):
            r2 = outs[t].shape[0] // 2
            theirs = outs[t].at[pl.ds((1 - c) * r2, r2)]
            _remote(theirs, theirs, send_sems.at[t], recv_sems.at[t], (x, y, 1 - c)).wait_recv()
            cp.wait_send()

    return pl.pallas_call(
        body, name="grad_join_halves", in_specs=[_ANY] * n, out_specs=[_ANY] * n,
        out_shape=[jax.ShapeDtypeStruct(t.shape, t.dtype) for t in totals],
        input_output_aliases={t: t for t in range(n)},
        scratch_shapes=[pltpu.SemaphoreType.DMA((n,)), pltpu.SemaphoreType.DMA((n,))],
    )(*totals)


def _add_sibling_half(g, got, c_arr, *, name, tm):
    _, rows, cols = g.shape
    r2 = rows // 2
    nb = r2 // tm

    def body(c_ref, g_ref, r_ref, o_ref):
        o_ref[...] = (g_ref[...].astype(F32) + r_ref[...].astype(F32)).astype(o_ref.dtype)

    return pl.pallas_call(
        body, name=name,
        grid_spec=pltpu.PrefetchScalarGridSpec(
            num_scalar_prefetch=1, grid=(N_CHIPS, nb),
            in_specs=[pl.BlockSpec((None, tm, cols), lambda b, i, c: (b, c[0] * nb + i, 0)),
                      pl.BlockSpec((None, tm, cols), lambda b, i, c: (b, i, 0))],
            out_specs=pl.BlockSpec((None, tm, cols), lambda b, i, c: (b, i, 0))),
        out_shape=jax.ShapeDtypeStruct((N_CHIPS, r2, cols), _WIRE),
        compiler_params=_params(("arbitrary", "arbitrary")),
    )(c_arr, g, got)


def _add_chips(h, got, place_arr, *, name, tm):
    _, r2, cols = h.shape
    nb = r2 // tm

    def body(p_ref, h_ref, r_ref, o_ref):
        o_ref[...] = ((h_ref[...].astype(F32) + r_ref[0].astype(F32)) + r_ref[1].astype(F32)) + r_ref[2].astype(F32)

    return pl.pallas_call(
        body, name=name,
        grid_spec=pltpu.PrefetchScalarGridSpec(
            num_scalar_prefetch=1, grid=(nb,),
            in_specs=[pl.BlockSpec((None, tm, cols), lambda i, p: (p[0], i, 0)),
                      pl.BlockSpec((3, tm, cols), lambda i, p: (0, i, 0))],
            out_specs=pl.BlockSpec((tm, cols), lambda i, p: (p[1] * nb + i, 0))),
        out_shape=jax.ShapeDtypeStruct((2 * r2, cols), F32),
        compiler_params=_params(("arbitrary",)),
    )(place_arr, h, got)


def _reduce_scatter(grads):
    core = lax.axis_index("c").astype(jnp.int32)
    chip = (2 * lax.axis_index("x") + lax.axis_index("y")).astype(jnp.int32)
    c_arr, place_arr = core.reshape(1), jnp.stack([chip, core])
    got = _swap_halves(grads)
    pair = [_add_sibling_half(g, r, c_arr, name=f"grad_add_sibling_{t}", tm=min(256, g.shape[1] // 2))
            for t, (g, r) in enumerate(zip(grads, got))]
    got = _scatter_chips(pair)
    total = [_add_chips(h, r, place_arr, name=f"grad_add_chips_{t}", tm=min(256, h.shape[1]))
             for t, (h, r) in enumerate(zip(pair, got))]
    return _join_halves(total)


def _all_gather_small(v):
    m_per, n = v.shape

    def body(x_ref, out_ref, send_sems, recv_sems, local_sem):
        x, y, c, chips = _place()
        me, sibling = (x, y, c), (x, y, 1 - c)

        def rows(px, py, pc):
            return out_ref.at[4 * px + 2 * py + pc]

        def copy(k, block, to, src=None):
            return _remote(rows(*block) if src is None else src, rows(*block), send_sems.at[k], recv_sems.at[k], to)

        mine = pltpu.make_async_copy(x_ref, rows(*me), local_sem)
        mine.start()
        first = [copy(0, me, sibling, src=x_ref)]
        first += [copy(1 + j, me, (*chip, c), src=x_ref) for j, chip in enumerate(chips)]
        for cp in first:
            cp.start()
        passed = [copy(4 + j, (*chip, c), sibling) for j, chip in enumerate(chips)]
        for j, chip in enumerate(chips):
            copy(1 + j, (*chip, c), me).wait_recv()
            passed[j].start()
        copy(0, sibling, me).wait_recv()
        for j, chip in enumerate(chips):
            copy(4 + j, (*chip, 1 - c), me).wait_recv()
        for cp in first + passed:
            cp.wait_send()
        mine.wait()

    return pl.pallas_call(
        body, name="gather_small_grads",
        out_shape=jax.ShapeDtypeStruct((8, m_per, n), v.dtype),
        in_specs=[pl.BlockSpec(memory_space=pltpu.VMEM)], out_specs=pl.BlockSpec(memory_space=pltpu.VMEM),
        scratch_shapes=[pltpu.SemaphoreType.DMA((7,)), pltpu.SemaphoreType.DMA((7,)), pltpu.SemaphoreType.DMA],
        compiler_params=pltpu.CompilerParams(vmem_limit_bytes=VMEM_LIMIT),
    )(v)


def _sum8(v):
    _, m, n = v.shape

    def body(v_ref, o_ref):
        acc = v_ref[0]
        for d in range(1, 8):
            acc = acc + v_ref[d]
        o_ref[...] = acc

    return pl.pallas_call(body, name="sum_small_grads", out_shape=jax.ShapeDtypeStruct((m, n), F32),
                          compiler_params=pltpu.CompilerParams(vmem_limit_bytes=VMEM_LIMIT))(v)


def _heads(v, nh):
    return v.reshape(v.shape[0], nh, HEAD_DIM).transpose(1, 0, 2)


def _unheads(v):
    return v.transpose(1, 0, 2).reshape(v.shape[1], v.shape[0] * HEAD_DIM)


def _local_step(x, target, norm_w, w_in4, q_norm_w, k_norm_w, sinks, w_ap4, a_re, a_im, log_dt, b_re, b_im, c_re,
                c_im, d_skip, w_glu4, b_glu, w_sp4, w_out):
    seq = x.shape[0]
    qw, kw = q_norm_w.reshape(1, HEAD_DIM), k_norm_w.reshape(1, HEAD_DIM)
    nw, bg = norm_w.reshape(1, D_MODEL), b_glu.reshape(1, D_MODEL)
    dsk = d_skip.reshape(1, SSM_W)

    h, rstd = _rms_fwd(x, nw)
    proj = _mm(h, w_in4, mode="nn", name="mm_proj", tm=512, tn=IN_W // 4, tk=D_MODEL, b_blocked=True)
    q = _heads(proj[:, :ATTN_W], N_Q_HEADS)
    k = _heads(proj[:, ATTN_W:ATTN_W + KV_W], N_KV_HEADS)
    v = _heads(proj[:, ATTN_W + KV_W:ATTN_W + 2 * KV_W], N_KV_HEADS)
    attn_h, lse = _attn_fwd(q, k, v, qw, kw, sinks)
    attn = _unheads(attn_h)

    def gate_a(at, ag):
        return (at * (ag * _sigmoid(ag)),)

    (ya_in,) = _ew(gate_a, [(attn, "mat", 0), (proj, "mat", OFF_AGATE)], [(ATTN_W, _MXU)], rows=seq, ncol=2,
                   name="ew_attn_gate")
    y_a = _mm(ya_in, w_ap4, mode="nn", name="mm_attn_proj", tm=1024, tn=512, tk=ATTN_W, b_blocked=True)

    coef = [t.reshape(1, N_STATES) for t in _ssm_params_fwd(a_re, a_im, log_dt.reshape(N_GROUPS, 1))]
    bre_blk, bim_blk = _block_diag_b(b_re).astype(_MXU), _block_diag_b(b_im).astype(_MXU)
    cre_blk, cim_blk = _block_diag_c(c_re).astype(_MXU), _block_diag_c(c_im).astype(_MXU)
    u_scan = _to_scan_order(proj[:, OFF_U * CW:OFF_U * CW + SSM_W])
    y_scan, s_re, s_im, i_re, i_im = _ssm_fwd(u_scan, bre_blk, bim_blk, cre_blk, cim_blk, dsk, coef)
    y_ssm = _from_scan_order(y_scan)

    (yg,) = _ew(lambda yv: (jax.nn.gelu(yv),), [(y_ssm, "mat", 0)], [(SSM_W, _MXU)], rows=seq, ncol=2, name="ew_gelu")
    glu = _mm(yg, w_glu4, mode="nn", name="mm_glu", tm=1024, tn=512, tk=SSM_W, b_blocked=True)

    def gate_s(ga, gb, ba, bb, z):
        return ((ga + ba) * _sigmoid(gb + bb) * (z * _sigmoid(z)),)

    (ys_in,) = _ew(gate_s, [(glu, "mat", 0), (glu, "mat", 2), (bg, "row", 0), (bg, "row", 2), (proj, "mat", OFF_Z)],
                   [(SSM_W, _MXU)], rows=seq, ncol=2, name="ew_ssm_gate")
    y_s = _mm(ys_in, w_sp4, mode="nn", name="mm_ssm_proj", tm=1024, tn=512, tk=SSM_W, b_blocked=True)

    def merge(ga, gs, ya, ys):
        return (_sigmoid(ga) * ya + _sigmoid(gs) * ys,)

    (merged,) = _ew(merge, [(proj, "mat", OFF_GA), (proj, "mat", OFF_GS), (y_a, "mat", 0), (y_s, "mat", 0)],
                    [(D_MODEL, _MXU)], rows=seq, ncol=4, name="ew_merge")
    mo = _mm(merged, w_out, mode="nn", name="mm_out", tm=512, tn=D_MODEL, tk=D_MODEL)

    def loss_head(xv, mv, tv):
        err = (xv + mv) - tv
        dout = err * (1.0 / D_MODEL)
        return dout, dout, _colsum(err * err)

    dout, dout_b, sq = _ew(loss_head, [(x, "mat", 0), (mo, "mat", 0), (target, "mat", 0)],
                           [(D_MODEL, F32), (D_MODEL, _MXU)], rows=seq, ncol=4, n_acc=1, name="ew_loss")
    loss = 0.5 * jnp.sum(sq) / D_MODEL

    d_merged = _mm(dout_b, w_out, mode="nt", name="mm_d_merged", tm=512, tn=D_MODEL, tk=D_MODEL)
    g_w_out = _mm(merged, dout_b, mode="tn", name="mm_g_w_out", tm=512, tn=D_MODEL, tk=512, out_dtype=_WIRE)

    def merge_bwd(dm, ga, gs, ya, ys):
        sa, ss = _sigmoid(ga), _sigmoid(gs)
        return sa * dm, ss * dm, dm * ya * sa * (1.0 - sa), dm * ys * ss * (1.0 - ss)

    d_ya, d_ys, d_ga, d_gs = _ew(
        merge_bwd, [(d_merged, "mat", 0), (proj, "mat", OFF_GA), (proj, "mat", OFF_GS), (y_a, "mat", 0), (y_s, "mat", 0)],
        [(D_MODEL, _MXU)] * 4, rows=seq, ncol=4, name="ew_merge_bwd")

    d_ya_in = _mm(d_ya, w_ap4, mode="nt", name="mm_d_attn_gate", tm=1024, tn=ATTN_W, tk=512, b_blocked=True)
    g_w_ap = _mm(ya_in, d_ya, mode="tn", name="mm_g_w_attn_proj", tm=ATTN_W, tn=512, tk=1024, out_dtype=_WIRE,
                 out_blocked=True)

    def gate_a_bwd(dv, at, ag):
        f, df = _silu_and_grad(ag)
        return dv * f, dv * at * df

    d_attn, d_agate = _ew(gate_a_bwd, [(d_ya_in, "mat", 0), (attn, "mat", 0), (proj, "mat", OFF_AGATE)],
                          [(ATTN_W, F32), (ATTN_W, _MXU)], rows=seq, ncol=2, name="ew_attn_gate_bwd")
    dq_h, dk_h, dv_h, g_qw, g_kw, g_sk = _attn_bwd(q, k, v, qw, kw, sinks, lse, _heads(d_attn, N_Q_HEADS))

    d_ys_in = _mm(d_ys, w_sp4, mode="nt", name="mm_d_ssm_gate", tm=1024, tn=SSM_W, tk=512, b_blocked=True)
    g_w_sp = _mm(ys_in, d_ys, mode="tn", name="mm_g_w_ssm_proj", tm=SSM_W, tn=512, tk=1024, out_dtype=_WIRE,
                 out_blocked=True)

    def gate_s_bwd(dv, ga, gb, ba, bb, z):
        a, sb = ga + ba, _sigmoid(gb + bb)
        f, df = _silu_and_grad(z)
        dga = dv * sb * f
        dgb = dv * a * f * sb * (1.0 - sb)
        return dga, dgb, dv * a * sb * df, _colsum(dga), _colsum(dgb)

    d_glu_a, d_glu_b, d_z, g_bga, g_bgb = _ew(
        gate_s_bwd, [(d_ys_in, "mat", 0), (glu, "mat", 0), (glu, "mat", 2), (bg, "row", 0), (bg, "row", 2),
                     (proj, "mat", OFF_Z)],
        [(SSM_W, _MXU)] * 3, rows=seq, ncol=2, n_acc=2, name="ew_ssm_gate_bwd")
    d_glu = jnp.concatenate([d_glu_a, d_glu_b], axis=1)
    d_yg = _mm(d_glu, w_glu4, mode="nt", name="mm_d_gelu", tm=1024, tn=SSM_W, tk=512, b_blocked=True)
    g_w_glu = _mm(yg, d_glu, mode="tn", name="mm_g_w_glu", tm=SSM_W, tn=512, tk=1024, out_dtype=_WIRE, out_blocked=True)

    def gelu_bwd(dv, yv):
        return (jax.vjp(jax.nn.gelu, yv)[1](dv)[0],)

    (d_yssm,) = _ew(gelu_bwd, [(d_yg, "mat", 0), (y_ssm, "mat", 0)], [(SSM_W, F32)], rows=seq, ncol=2, name="ew_gelu_bwd")
    (du_scan, g_bre, g_bim, g_cre, g_cim, g_dsk, g_abr, g_abi, g_cfr, g_cfi) = _ssm_bwd(
        _to_scan_order(d_yssm), u_scan, s_re, s_im, i_re, i_im, bre_blk, bim_blk, cre_blk, cim_blk, dsk, coef)
    g_are, g_aim, g_ldt = _ssm_params_bwd(a_re, a_im, log_dt.reshape(N_GROUPS, 1),
                                          *[t.reshape(N_GROUPS, STATE) for t in (g_abr, g_abi, g_cfr, g_cfi)])
    d_u = _from_scan_order(du_scan)

    d_proj = jnp.concatenate(
        [_unheads(dq_h).astype(_MXU), _unheads(dk_h).astype(_MXU), _unheads(dv_h).astype(_MXU), d_agate,
         d_u.astype(_MXU), d_z, d_ga, d_gs], axis=1)
    d_h = _mm(d_proj, w_in4, mode="nt", name="mm_d_h", tm=512, tn=D_MODEL, tk=IN_W // 4, b_blocked=True)
    g_w_in = _mm(h, d_proj, mode="tn", name="mm_g_w_in", tm=1024, tn=IN_W // 4, tk=512, out_dtype=_WIRE,
                 out_blocked=True)
    grad_x, g_nw = _rms_bwd(d_h, x, rstd, nw, dout)

    big = [g_w_in, g_w_ap, g_w_glu, g_w_sp, g_w_out.reshape(N_CHIPS, D_MODEL // N_CHIPS, D_MODEL)]
    small = dict(
        norm_w=g_nw.reshape(D_MODEL), q_norm_w=g_qw.reshape(HEAD_DIM), k_norm_w=g_kw.reshape(HEAD_DIM),
        sinks=g_sk[:, :, 0].reshape(N_Q_HEADS), A_re=g_are, A_im=g_aim, log_dt=g_ldt.reshape(N_GROUPS),
        B_re=_diag_of_b(g_bre), B_im=_diag_of_b(g_bim), C_re=_diag_of_c(g_cre), C_im=_diag_of_c(g_cim),
        D_skip=g_dsk.reshape(N_GROUPS, GROUP), b_glu=jnp.concatenate([g_bga, g_bgb], axis=1).reshape(D_MODEL))
    return loss, grad_x, big, small


_SMALL = ["norm_w", "q_norm_w", "k_norm_w", "sinks", "A_re", "A_im", "log_dt", "B_re", "B_im", "C_re", "C_im",
          "D_skip", "b_glu"]
_BIG = ["w_in", "w_attn_proj", "w_glu", "w_ssm_proj", "w_out"]
_ORDER = ["norm_w", "w_in", "q_norm_w", "k_norm_w", "sinks", "w_attn_proj", "A_re", "A_im", "log_dt", "B_re", "B_im",
          "C_re", "C_im", "D_skip", "w_glu", "b_glu", "w_ssm_proj", "w_out"]
_PACK_W = 1024


def _packed_rows(size):
    unit = SUBLANES * _PACK_W
    return -(-size // unit) * SUBLANES


def _pack_small(d):
    parts = []
    for n in _SMALL:
        flat = d[n].reshape(-1).astype(F32)
        rows = _packed_rows(flat.shape[0])
        parts.append(jnp.pad(flat, (0, rows * _PACK_W - flat.shape[0])).reshape(rows, _PACK_W))
    return jnp.concatenate(parts, axis=0)


def _unpack_small(packed, like):
    out, pos = {}, 0
    for n in _SMALL:
        rows = _packed_rows(like[n].size)
        out[n] = packed[pos:pos + rows].reshape(-1)[:like[n].size].reshape(like[n].shape)
        pos += rows
    return out


def _as2d(a):
    return a.reshape(1, -1) if a.ndim == 1 else a


def _adamw_whole(w, g, m, v, *, name):
    shape = w.shape
    w, g, m, v = _as2d(w), _as2d(g), _as2d(m), _as2d(v)

    def body(w_ref, g_ref, m_ref, v_ref, d_ref, nm_ref, nv_ref):
        d_ref[...], nm_ref[...], nv_ref[...] = _adamw_math(w_ref[...], g_ref[...], m_ref[...], v_ref[...])

    outs = pl.pallas_call(body, name=name, out_shape=[jax.ShapeDtypeStruct(w.shape, F32)] * 3)(w, g, m, v)
    return [o.reshape(shape) for o in outs]


def kernel(x, norm_w, w_in, q_norm_w, k_norm_w, sinks, w_attn_proj, A_re, A_im, log_dt, B_re, B_im, C_re, C_im, D_skip, w_glu, b_glu, w_ssm_proj, w_out, loss_target, m_norm_w, m_w_in, m_q_norm_w, m_k_norm_w, m_sinks, m_w_attn_proj, m_A_re, m_A_im, m_log_dt, m_B_re, m_B_im, m_C_re, m_C_im, m_D_skip, m_w_glu, m_b_glu, m_w_ssm_proj, m_w_out, v_norm_w, v_w_in, v_q_norm_w, v_k_norm_w, v_sinks, v_w_attn_proj, v_A_re, v_A_im, v_log_dt, v_B_re, v_B_im, v_C_re, v_C_im, v_D_skip, v_w_glu, v_b_glu, v_w_ssm_proj, v_w_out):
    w = dict(norm_w=norm_w, w_in=w_in, q_norm_w=q_norm_w, k_norm_w=k_norm_w, sinks=sinks, w_attn_proj=w_attn_proj,
             A_re=A_re, A_im=A_im, log_dt=log_dt, B_re=B_re, B_im=B_im, C_re=C_re, C_im=C_im, D_skip=D_skip,
             w_glu=w_glu, b_glu=b_glu, w_ssm_proj=w_ssm_proj, w_out=w_out)
    m = dict(norm_w=m_norm_w, w_in=m_w_in, q_norm_w=m_q_norm_w, k_norm_w=m_k_norm_w, sinks=m_sinks,
             w_attn_proj=m_w_attn_proj, A_re=m_A_re, A_im=m_A_im, log_dt=m_log_dt, B_re=m_B_re, B_im=m_B_im,
             C_re=m_C_re, C_im=m_C_im, D_skip=m_D_skip, w_glu=m_w_glu, b_glu=m_b_glu, w_ssm_proj=m_w_ssm_proj,
             w_out=m_w_out)
    v = dict(norm_w=v_norm_w, w_in=v_w_in, q_norm_w=v_q_norm_w, k_norm_w=v_k_norm_w, sinks=v_sinks,
             w_attn_proj=v_w_attn_proj, A_re=v_A_re, A_im=v_A_im, log_dt=v_log_dt, B_re=v_B_re, B_im=v_B_im,
             C_re=v_C_re, C_im=v_C_im, D_skip=v_D_skip, w_glu=v_w_glu, b_glu=v_b_glu, w_ssm_proj=v_w_ssm_proj,
             w_out=v_w_out)

    chip = (2 * lax.axis_index("x") + lax.axis_index("y")).astype(jnp.int32).reshape(1)
    full = _gather_weights([_place_shard(w[n], chip, name=f"place_{n}") for n in _BIG])
    w_in4, w_ap4, w_glu4, w_sp4, w_out4 = full
    loss, grad_x, big, small = _local_step(
        x[0], loss_target[0], norm_w, w_in4, q_norm_w, k_norm_w, sinks, w_ap4, A_re, A_im, log_dt, B_re, B_im, C_re,
        C_im, D_skip, w_glu4, b_glu, w_sp4, w_out4.reshape(D_MODEL, D_MODEL))
    loss = lax.psum(loss, ("x", "y", "c"))

    grads = dict(zip(_BIG, _reduce_scatter(big)))
    small_sum = _sum8(_all_gather_small(_pack_small(small)))
    grads.update(_unpack_small(small_sum, w))

    delta, new_m, new_v = {}, {}, {}
    for n in _BIG:
        delta[n], new_m[n], new_v[n] = _adamw(w[n], grads[n], m[n], v[n], name=f"adamw_{n}", tm=128)
    for n in _SMALL:
        delta[n], new_m[n], new_v[n] = _adamw_whole(w[n], grads[n], m[n], v[n], name=f"adamw_{n}")

    return (loss, grad_x[None], *[grads[n] for n in _ORDER], *[delta[n] for n in _ORDER],
            *[new_m[n] for n in _ORDER], *[new_v[n] for n in _ORDER])
```

```python
import functools
import math

import jax
import jax.numpy as jnp
from jax import lax
from jax.experimental import pallas as pl
from jax.experimental.pallas import tpu as pltpu

F32 = jnp.float32
_MXU = jnp.bfloat16
_WIRE = jnp.bfloat16

LANES = 128
SUBLANES = 8
VMEM_LIMIT = 56 * 1024 * 1024

D_MODEL = 2048
HEAD_DIM = 64
N_Q_HEADS = 16
N_KV_HEADS = 4
Q_PER_KV = 4
ATTN_W = 1024
KV_W = 256
WINDOW = 128
SSM_W = 1024
GROUP = 16
N_GROUPS = 64
STATE = 64
N_STATES = N_GROUPS * STATE
IN_W = 8704
NORM_EPS = 1e-6
N_CHIPS = 4
CW = 512
OFF_AGATE, OFF_U, OFF_Z, OFF_GA, OFF_GS = 3, 5, 7, 9, 13

SSM_T = 256
SSM_L = SSM_T // SUBLANES
SSM_JB = 8
SSM_SB = N_STATES // SSM_JB

ADAM_LR, ADAM_B1, ADAM_B2, ADAM_EPS, ADAM_WD, ADAM_STEP = 0.001, 0.9, 0.999, 1e-08, 0.01, 10

MESH = pl.DeviceIdType.MESH
_ANY = pl.BlockSpec(memory_space=pl.ANY)


def _params(sem=None):
    return pltpu.CompilerParams(dimension_semantics=sem, vmem_limit_bytes=VMEM_LIMIT)


def _mm(a, b, *, mode, name, tm, tn, tk, out_dtype=F32, b_blocked=False, out_blocked=False, deps=()):
    nd = len(deps)
    if mode == "tn":
        K, M = a.shape
    else:
        M, K = a.shape
    if mode == "nn":
        N = b.shape[0] * b.shape[2] if b_blocked else b.shape[1]
    elif mode == "nt":
        N = b.shape[1] if b_blocked else b.shape[0]
    else:
        N = b.shape[1]
    tm, tn, tk = min(tm, M), min(tn, N), min(tk, K)
    nj, ni, nk = N // tn, M // tm, K // tk
    assert nj * tn == N and ni * tm == M and nk * tk == K, (name, M, N, K)
    dims = {"nn": (((1,), (0,)), ((), ())), "nt": (((1,), (1,)), ((), ())), "tn": (((0,), (0,)), ((), ()))}[mode]

    if mode == "tn":
        a_spec = pl.BlockSpec((tk, tm), lambda j, i, k: (k, i))
    else:
        a_spec = pl.BlockSpec((tm, tk), lambda j, i, k: (i, k))
    if mode == "nn":
        if b_blocked:
            assert b.shape[0] == nj and b.shape[2] == tn
            b_spec = pl.BlockSpec((None, tk, tn), lambda j, i, k: (j, k, 0))
        else:
            b_spec = pl.BlockSpec((tk, tn), lambda j, i, k: (k, j))
    elif mode == "nt":
        if b_blocked:
            assert b.shape[0] == nk and b.shape[2] == tk
            b_spec = pl.BlockSpec((None, tn, tk), lambda j, i, k: (k, j, 0))
        else:
            b_spec = pl.BlockSpec((tn, tk), lambda j, i, k: (j, k))
    else:
        b_spec = pl.BlockSpec((tk, tn), lambda j, i, k: (k, j))
    if out_blocked:
        assert nj == N_CHIPS
        o_spec = pl.BlockSpec((None, tm, tn), lambda j, i, k: (j, i, 0))
        o_shape = jax.ShapeDtypeStruct((nj, M, tn), out_dtype)
    else:
        o_spec = pl.BlockSpec((tm, tn), lambda j, i, k: (i, j))
        o_shape = jax.ShapeDtypeStruct((M, N), out_dtype)
    use_acc = nk > 1 and out_dtype != F32

    def body(a_ref, b_ref, *rest):
        o_ref, scratch = rest[nd], rest[nd + 1:]
        part = lax.dot_general(a_ref[...].astype(_MXU), b_ref[...].astype(_MXU), dims,
                               preferred_element_type=F32)
        if nk == 1:
            o_ref[...] = part.astype(o_ref.dtype)
            return
        k = pl.program_id(2)
        acc = scratch[0] if use_acc else o_ref

        @pl.when(k == 0)
        def _():
            acc[...] = part

        @pl.when(k > 0)
        def _():
            acc[...] += part

        if use_acc:
            @pl.when(k == nk - 1)
            def _():
                o_ref[...] = acc[...].astype(o_ref.dtype)

    return pl.pallas_call(
        body, name=name, grid=(nj, ni, nk), in_specs=[a_spec, b_spec] + [_ANY] * nd, out_specs=o_spec,
        out_shape=o_shape, scratch_shapes=[pltpu.VMEM((tm, tn), F32)] if use_acc else [],
        compiler_params=_params(("parallel", "parallel", "arbitrary")),
    )(a, b, *deps)


def _ew(fn, ins, outs, *, rows, ncol, name, n_acc=0, tm=512, deps=()):
    n_in, n_out, nd = len(ins), len(outs), len(deps)
    tm = min(tm, rows)
    in_specs = []
    for _, kind, col0 in ins:
        if kind == "mat":
            in_specs.append(pl.BlockSpec((tm, CW), lambda j, i, c0=col0: (i, c0 + j)))
        else:
            in_specs.append(pl.BlockSpec((1, CW), lambda j, i, c0=col0: (0, c0 + j)))
    out_specs = [pl.BlockSpec((tm, CW), lambda j, i: (i, j)) for _ in outs]
    out_shape = [jax.ShapeDtypeStruct((rows, w), dt) for w, dt in outs]
    for _ in range(n_acc):
        out_specs.append(pl.BlockSpec((1, CW), lambda j, i: (0, j)))
        out_shape.append(jax.ShapeDtypeStruct((1, ncol * CW), F32))

    def body(*refs):
        vals = fn(*[r[...] for r in refs[:n_in]])
        refs = refs[n_in + nd:]
        for r, v in zip(refs[:n_out], vals[:n_out]):
            r[...] = v.astype(r.dtype)
        i = pl.program_id(1)
        for r, v in zip(refs[n_out:], vals[n_out:]):
            @pl.when(i == 0)
            def _(r=r, v=v):
                r[...] = v

            @pl.when(i > 0)
            def _(r=r, v=v):
                r[...] += v

    res = pl.pallas_call(
        body, name=name, grid=(ncol, rows // tm), in_specs=in_specs + [_ANY] * nd, out_specs=out_specs,
        out_shape=out_shape, compiler_params=_params(("parallel", "arbitrary")),
    )(*[a for a, _, _ in ins], *deps)
    return res


def _colsum(v):
    return jnp.sum(v, axis=0, keepdims=True)


def _sigmoid(v):
    return jax.nn.sigmoid(v)


def _silu_and_grad(v):
    s = _sigmoid(v)
    return v * s, s * (1.0 + v * (1.0 - s))


def _rms_fwd(x, w, *, tm=512, deps=()):
    rows, d = x.shape
    nd = len(deps)

    def body(x_ref, w_ref, *rest):
        h_ref, r_ref = rest[nd:]
        xv = x_ref[...]
        r = lax.rsqrt(jnp.mean(xv * xv, axis=-1, keepdims=True) + NORM_EPS)
        h_ref[...] = (xv * r * w_ref[...]).astype(h_ref.dtype)
        r_ref[...] = r

    return pl.pallas_call(
        body, name="rms_fwd", grid=(rows // tm,),
        in_specs=[pl.BlockSpec((tm, d), lambda i: (i, 0)), pl.BlockSpec((1, d), lambda i: (0, 0))] + [_ANY] * nd,
        out_specs=[pl.BlockSpec((tm, d), lambda i: (i, 0)), pl.BlockSpec((tm, 1), lambda i: (i, 0))],
        out_shape=[jax.ShapeDtypeStruct((rows, d), _MXU), jax.ShapeDtypeStruct((rows, 1), F32)],
        compiler_params=_params(("arbitrary",)),
    )(x, w, *deps)


def _rms_bwd(dh, x, rstd, w, dout, *, tm=256):
    rows, d = x.shape

    def body(dh_ref, x_ref, r_ref, w_ref, do_ref, gx_ref, gw_ref):
        dhv, xv, r, wv = dh_ref[...], x_ref[...], r_ref[...], w_ref[...]
        xr = xv * r
        t = jnp.mean(dhv * wv * xr, axis=-1, keepdims=True)
        gx_ref[...] = do_ref[...] + r * (wv * dhv - xr * t)
        part = _colsum(dhv * xr)
        i = pl.program_id(0)

        @pl.when(i == 0)
        def _():
            gw_ref[...] = part

        @pl.when(i > 0)
        def _():
            gw_ref[...] += part

    return pl.pallas_call(
        body, name="rms_bwd", grid=(rows // tm,),
        in_specs=[pl.BlockSpec((tm, d), lambda i: (i, 0)), pl.BlockSpec((tm, d), lambda i: (i, 0)),
                  pl.BlockSpec((tm, 1), lambda i: (i, 0)), pl.BlockSpec((1, d), lambda i: (0, 0)),
                  pl.BlockSpec((tm, d), lambda i: (i, 0))],
        out_specs=[pl.BlockSpec((tm, d), lambda i: (i, 0)), pl.BlockSpec((1, d), lambda i: (0, 0))],
        out_shape=[jax.ShapeDtypeStruct((rows, d), F32), jax.ShapeDtypeStruct((1, d), F32)],
        compiler_params=_params(("arbitrary",)),
    )(dh, x, rstd, w, dout)


_NT = (((1,), (1,)), ((), ()))
_TN = (((0,), (0,)), ((), ()))


def _head_rstd(v):
    return lax.rsqrt(jnp.mean(v * v, axis=-1, keepdims=True) + NORM_EPS)


def _band_mask(n):
    qi = lax.broadcasted_iota(jnp.int32, (WINDOW, 2 * WINDOW), 0) + WINDOW
    kj = lax.broadcasted_iota(jnp.int32, (WINDOW, 2 * WINDOW), 1)
    diff = qi - kj
    first_key = jnp.where(n > 0, 0, WINDOW)
    return (diff >= 0) & (diff < WINDOW) & (kj >= first_key)


def _attn_specs(nblk, rev):
    def cur(g, n):
        return (nblk - 1 - n) if rev else n

    q_spec = pl.BlockSpec((Q_PER_KV, WINDOW, HEAD_DIM), lambda g, n: (g, cur(g, n), 0))
    kc_spec = pl.BlockSpec((1, WINDOW, HEAD_DIM), lambda g, n: (g, cur(g, n), 0))
    kp_spec = pl.BlockSpec((1, WINDOW, HEAD_DIM), lambda g, n: (g, jnp.maximum(cur(g, n) - 1, 0), 0))
    w_spec = pl.BlockSpec((1, HEAD_DIM), lambda g, n: (0, 0))
    l_spec = pl.BlockSpec((Q_PER_KV, WINDOW, 1), lambda g, n: (g, cur(g, n), 0))
    return q_spec, kc_spec, kp_spec, w_spec, l_spec


def _attn_fwd(q, k, v, qw, kw, sinks, deps=()):
    seq = q.shape[1]
    nblk = seq // WINDOW
    scale = 1.0 / math.sqrt(HEAD_DIM)
    q_spec, kc_spec, kp_spec, w_spec, l_spec = _attn_specs(nblk, False)
    nd = len(deps)

    def body(sink_ref, q_ref, kc_ref, kp_ref, vc_ref, vp_ref, qw_ref, kw_ref, *rest):
        o_ref, lse_ref = rest[nd:]
        g, n = pl.program_id(0), pl.program_id(1)
        kwv, qwv = kw_ref[...], qw_ref[...]
        kc, kp = kc_ref[0], kp_ref[0]
        kk = jnp.concatenate([kp * _head_rstd(kp) * kwv, kc * _head_rstd(kc) * kwv], axis=0).astype(_MXU)
        vv = jnp.concatenate([vp_ref[0], vc_ref[0]], axis=0).astype(_MXU)
        valid = _band_mask(n)
        for r in range(Q_PER_KV):
            qr = q_ref[r]
            qn = (qr * _head_rstd(qr) * qwv).astype(_MXU)
            s = lax.dot_general(qn, kk, _NT, preferred_element_type=F32) * scale
            s = jnp.where(valid, s, -1e30)
            sink = sink_ref[g * Q_PER_KV + r]
            m = jnp.maximum(jnp.max(s, axis=-1, keepdims=True), sink)
            e = jnp.exp(s - m)
            z = jnp.sum(e, axis=-1, keepdims=True) + jnp.exp(sink - m)
            p = e / z
            o_ref[r] = jnp.dot(p.astype(_MXU), vv, preferred_element_type=F32)
            lse_ref[r] = m + jnp.log(z)

    return pl.pallas_call(
        body, name="attn_fwd", grid=(N_KV_HEADS, nblk),
        in_specs=[pl.BlockSpec(memory_space=pltpu.SMEM), q_spec, kc_spec, kp_spec, kc_spec, kp_spec, w_spec, w_spec]
        + [_ANY] * nd,
        out_specs=[q_spec, l_spec],
        out_shape=[jax.ShapeDtypeStruct((N_Q_HEADS, seq, HEAD_DIM), F32),
                   jax.ShapeDtypeStruct((N_Q_HEADS, seq, 1), F32)],
        compiler_params=_params(("arbitrary", "arbitrary")),
    )(sinks, q, k, k, v, v, qw, kw, *deps)


def _attn_bwd(q, k, v, qw, kw, sinks, lse, do, deps=()):
    seq = q.shape[1]
    nblk = seq // WINDOW
    scale = 1.0 / math.sqrt(HEAD_DIM)
    q_spec, kc_spec, kp_spec, w_spec, l_spec = _attn_specs(nblk, True)
    s_spec = pl.BlockSpec((1, Q_PER_KV, LANES), lambda g, n: (g, 0, 0))
    nd = len(deps)

    def body(sink_ref, q_ref, kc_ref, kp_ref, vc_ref, vp_ref, qw_ref, kw_ref, lse_ref, do_ref, *rest):
        dq_ref, dk_ref, dv_ref, dqw_ref, dkw_ref, dsk_ref, cdk, cdv = rest[nd:]
        g, step = pl.program_id(0), pl.program_id(1)
        n = nblk - 1 - step

        @pl.when(step == 0)
        def _():
            cdk[...] = jnp.zeros_like(cdk)
            cdv[...] = jnp.zeros_like(cdv)
            dsk_ref[...] = jnp.zeros_like(dsk_ref)

        @pl.when((step == 0) & (g == 0))
        def _():
            dqw_ref[...] = jnp.zeros_like(dqw_ref)
            dkw_ref[...] = jnp.zeros_like(dkw_ref)

        kwv, qwv = kw_ref[...], qw_ref[...]
        kc, kp = kc_ref[0], kp_ref[0]
        rc = _head_rstd(kc)
        kk = jnp.concatenate([kp * _head_rstd(kp) * kwv, kc * rc * kwv], axis=0).astype(_MXU)
        vv = jnp.concatenate([vp_ref[0], vc_ref[0]], axis=0).astype(_MXU)
        valid = _band_mask(n)
        dkk = jnp.zeros((2 * WINDOW, HEAD_DIM), F32)
        dvv = jnp.zeros((2 * WINDOW, HEAD_DIM), F32)
        dqw = jnp.zeros((1, HEAD_DIM), F32)
        for r in range(Q_PER_KV):
            qr = q_ref[r]
            rq = _head_rstd(qr)
            qn = (qr * rq * qwv).astype(_MXU)
            s = lax.dot_general(qn, kk, _NT, preferred_element_type=F32) * scale
            s = jnp.where(valid, s, -1e30)
            lse_r = lse_ref[r]
            p = jnp.exp(s - lse_r)
            dob = do_ref[r].astype(_MXU)
            dp = lax.dot_general(dob, vv, _NT, preferred_element_type=F32)
            dsum = jnp.sum(p * dp, axis=-1, keepdims=True)
            ds = (p * (dp - dsum) * scale).astype(_MXU)
            sink = sink_ref[g * Q_PER_KV + r]
            dsink = _colsum(-jnp.exp(sink - lse_r) * dsum)
            dsk_ref[0, r:r + 1, :] += jnp.broadcast_to(dsink, (1, LANES))
            dvv = dvv + lax.dot_general(p.astype(_MXU), dob, _TN, preferred_element_type=F32)
            dqn = jnp.dot(ds, kk, preferred_element_type=F32)
            dkk = dkk + lax.dot_general(ds, qn, _TN, preferred_element_type=F32)
            qx = qr * rq
            dq_ref[r] = rq * (qwv * dqn - qx * jnp.mean(dqn * qwv * qx, axis=-1, keepdims=True))
            dqw = dqw + _colsum(dqn * qx)
        dkn = dkk[WINDOW:] + cdk[...]
        kx = kc * rc
        dk_ref[0] = rc * (kwv * dkn - kx * jnp.mean(dkn * kwv * kx, axis=-1, keepdims=True))
        dv_ref[0] = dvv[WINDOW:] + cdv[...]
        cdk[...] = dkk[:WINDOW]
        cdv[...] = dvv[:WINDOW]
        dqw_ref[...] += dqw
        dkw_ref[...] += _colsum(dkn * kx)

    return pl.pallas_call(
        body, name="attn_bwd", grid=(N_KV_HEADS, nblk),
        in_specs=[pl.BlockSpec(memory_space=pltpu.SMEM), q_spec, kc_spec, kp_spec, kc_spec, kp_spec, w_spec, w_spec,
                  l_spec, q_spec] + [_ANY] * nd,
        out_specs=[q_spec, kc_spec, kc_spec, w_spec, w_spec, s_spec],
        out_shape=[jax.ShapeDtypeStruct((N_Q_HEADS, seq, HEAD_DIM), F32),
                   jax.ShapeDtypeStruct((N_KV_HEADS, seq, HEAD_DIM), F32),
                   jax.ShapeDtypeStruct((N_KV_HEADS, seq, HEAD_DIM), F32),
                   jax.ShapeDtypeStruct((1, HEAD_DIM), F32), jax.ShapeDtypeStruct((1, HEAD_DIM), F32),
                   jax.ShapeDtypeStruct((N_KV_HEADS, Q_PER_KV, LANES), F32)],
        scratch_shapes=[pltpu.VMEM((WINDOW, HEAD_DIM), F32), pltpu.VMEM((WINDOW, HEAD_DIM), F32)],
        compiler_params=_params(("arbitrary", "arbitrary")),
    )(sinks, q, k, k, v, v, qw, kw, lse, do, *deps)


def _ssm_discretise(a_re, a_im, log_dt):
    dt = jnp.exp(log_dt)
    mag = jnp.exp(dt * a_re)
    ab_re = mag * jnp.cos(dt * a_im)
    ab_im = mag * jnp.sin(dt * a_im)
    num_re = ab_re - 1.0
    num_im = ab_im
    den = a_re * a_re + a_im * a_im
    cf_re = (num_re * a_re + num_im * a_im) / den
    cf_im = (num_im * a_re - num_re * a_im) / den
    return ab_re, ab_im, cf_re, cf_im


def _ssm_params_fwd(a_re, a_im, log_dt):
    shp = jax.ShapeDtypeStruct(a_re.shape, F32)

    def body(are_ref, aim_ref, ldt_ref, abr_ref, abi_ref, cfr_ref, cfi_ref, alr_ref, ali_ref):
        abr, abi, cfr, cfi = _ssm_discretise(are_ref[...], aim_ref[...], ldt_ref[...])
        abr_ref[...], abi_ref[...], cfr_ref[...], cfi_ref[...] = abr, abi, cfr, cfi
        pr, pi = abr, abi
        for _ in range(int(math.log2(SSM_L))):
            pr, pi = pr * pr - pi * pi, 2.0 * pr * pi
        alr_ref[...], ali_ref[...] = pr, pi

    return pl.pallas_call(body, name="ssm_params_fwd", out_shape=[shp] * 6)(a_re, a_im, log_dt)


def _ssm_params_bwd(a_re, a_im, log_dt, d_abr, d_abi, d_cfr, d_cfi):
    def body(are_ref, aim_ref, ldt_ref, g0, g1, g2, g3, dare_ref, daim_ref, dldt_ref):
        _, vjp = jax.vjp(_ssm_discretise, are_ref[...], aim_ref[...], ldt_ref[...])
        dare_ref[...], daim_ref[...], dldt_ref[...] = vjp((g0[...], g1[...], g2[...], g3[...]))

    return pl.pallas_call(
        body, name="ssm_params_bwd",
        out_shape=[jax.ShapeDtypeStruct(a_re.shape, F32), jax.ShapeDtypeStruct(a_im.shape, F32),
                   jax.ShapeDtypeStruct(log_dt.shape, F32)],
    )(a_re, a_im, log_dt, d_abr, d_abi, d_cfr, d_cfi)


def _scan_cols(j):
    return pl.ds(j * SSM_SB, SSM_SB)


def _rows8(r):
    return pl.ds(pl.multiple_of(r * SUBLANES, SUBLANES), SUBLANES)


def _bcast8(row):
    return jnp.broadcast_to(row, (SUBLANES, row.shape[-1]))


def _ssm_fwd(u, b_re, b_im, c_re, c_im, d_skip, coef):
    seq = u.shape[0]
    nc = seq // SSM_T
    T, L = SSM_T, SSM_L

    def body(u_ref, bre_ref, bim_ref, cre_ref, cim_ref, d_ref, are_ref, aim_ref, cfr_ref, cfi_ref, alr_ref, ali_ref,
             y_ref, sre_ref, sim_ref, ire_ref, iim_ref, car_re, car_im, end_re, end_im):
        c = pl.program_id(0)

        @pl.when(c == 0)
        def _():
            car_re[...] = jnp.zeros_like(car_re)
            car_im[...] = jnp.zeros_like(car_im)

        for j in range(SSM_JB):
            ub = u_ref[:, j * LANES:(j + 1) * LANES].astype(_MXU)
            bur = jnp.dot(ub, bre_ref[j], preferred_element_type=F32)
            bui = jnp.dot(ub, bim_ref[j], preferred_element_type=F32)
            cfr, cfi = cfr_ref[:, _scan_cols(j)], cfi_ref[:, _scan_cols(j)]
            sre_ref[:, _scan_cols(j)] = cfr * bur - cfi * bui
            sim_ref[:, _scan_cols(j)] = cfr * bui + cfi * bur

        for j in range(SSM_JB):
            cols = _scan_cols(j)
            ar, ai = _bcast8(are_ref[:, cols]), _bcast8(aim_ref[:, cols])

            def step1(r, s, cols=cols, ar=ar, ai=ai):
                sr, si = s
                rows = _rows8(r)
                return (ar * sr - ai * si + sre_ref[rows, cols], ar * si + ai * sr + sim_ref[rows, cols])

            zero = jnp.zeros((SUBLANES, SSM_SB), F32)
            er, ei = lax.fori_loop(0, L, step1, (zero, zero), unroll=4)
            end_re[:, cols] = er
            end_im[:, cols] = ei

        alr, ali = alr_ref[...], ali_ref[...]
        cr, ci = car_re[...], car_im[...]
        ire_ref[0:1, :] = cr
        iim_ref[0:1, :] = ci
        for i in range(1, SUBLANES):
            er, ei = end_re[i - 1:i, :], end_im[i - 1:i, :]
            cr, ci = alr * cr - ali * ci + er, alr * ci + ali * cr + ei
            ire_ref[i:i + 1, :] = cr
            iim_ref[i:i + 1, :] = ci

        for j in range(SSM_JB):
            cols = _scan_cols(j)
            ar, ai = _bcast8(are_ref[:, cols]), _bcast8(aim_ref[:, cols])

            def step2(r, s, cols=cols, ar=ar, ai=ai):
                sr, si = s
                rows = _rows8(r)
                nr = ar * sr - ai * si + sre_ref[rows, cols]
                ni = ar * si + ai * sr + sim_ref[rows, cols]
                sre_ref[rows, cols] = nr
                sim_ref[rows, cols] = ni
                return nr, ni

            lax.fori_loop(0, L, step2, (ire_ref[:, cols], iim_ref[:, cols]), unroll=4)

        car_re[...] = sre_ref[T - 1:T, :]
        car_im[...] = sim_ref[T - 1:T, :]

        for j in range(SSM_JB):
            cols = _scan_cols(j)
            ch = slice(j * LANES, (j + 1) * LANES)
            y = (jnp.dot(sre_ref[:, cols].astype(_MXU), cre_ref[j], preferred_element_type=F32)
                 - jnp.dot(sim_ref[:, cols].astype(_MXU), cim_ref[j], preferred_element_type=F32))
            y_ref[:, ch] = y + d_ref[:, ch] * u_ref[:, ch]

    tok = pl.BlockSpec((T, SSM_W), lambda c: (c, 0))
    st = pl.BlockSpec((T, N_STATES), lambda c: (c, 0))
    ini = pl.BlockSpec((None, SUBLANES, N_STATES), lambda c: (c, 0, 0))
    bsp = pl.BlockSpec((SSM_JB, LANES, SSM_SB), lambda c: (0, 0, 0))
    csp = pl.BlockSpec((SSM_JB, SSM_SB, LANES), lambda c: (0, 0, 0))
    row_w = pl.BlockSpec((1, SSM_W), lambda c: (0, 0))
    row_s = pl.BlockSpec((1, N_STATES), lambda c: (0, 0))
    return pl.pallas_call(
        body, name="ssm_fwd", grid=(nc,),
        in_specs=[tok, bsp, bsp, csp, csp, row_w] + [row_s] * 6,
        out_specs=[tok, st, st, ini, ini],
        out_shape=[jax.ShapeDtypeStruct((seq, SSM_W), F32),
                   jax.ShapeDtypeStruct((seq, N_STATES), F32), jax.ShapeDtypeStruct((seq, N_STATES), F32),
                   jax.ShapeDtypeStruct((nc, SUBLANES, N_STATES), F32),
                   jax.ShapeDtypeStruct((nc, SUBLANES, N_STATES), F32)],
        scratch_shapes=[pltpu.VMEM((1, N_STATES), F32), pltpu.VMEM((1, N_STATES), F32),
                        pltpu.VMEM((SUBLANES, N_STATES), F32), pltpu.VMEM((SUBLANES, N_STATES), F32)],
        compiler_params=_params(("arbitrary",)),
    )(u, b_re, b_im, c_re, c_im, d_skip, *coef)


def _ssm_bwd(dy, u, s_re, s_im, i_re, i_im, b_re, b_im, c_re, c_im, d_skip, coef):
    seq = u.shape[0]
    nc = seq // SSM_T
    T, L = SSM_T, SSM_L

    def body(dy_ref, u_ref, sre_ref, sim_ref, ire_ref, iim_ref, bre_ref, bim_ref, cre_ref, cim_ref, d_ref,
             are_ref, aim_ref, cfr_ref, cfi_ref, alr_ref, ali_ref,
             du_ref, dbre_out, dbim_out, dcre_out, dcim_out, dd_ref, dar_ref, dai_ref, dcfr_ref, dcfi_ref,
             lre, lim, car_re, car_im, end_re, end_im, ini_re, ini_im, dbre_ref, dbim_ref, dcre_ref, dcim_ref):
        step = pl.program_id(0)

        @pl.when(step == 0)
        def _():
            car_re[...] = jnp.zeros_like(car_re)
            car_im[...] = jnp.zeros_like(car_im)
            for ref in (dbre_ref, dbim_ref, dcre_ref, dcim_ref, dd_ref, dar_ref, dai_ref, dcfr_ref, dcfi_ref):
                ref[...] = jnp.zeros_like(ref)

        for j in range(SSM_JB):
            dyb = dy_ref[:, j * LANES:(j + 1) * LANES].astype(_MXU)
            lre[:, _scan_cols(j)] = lax.dot_general(dyb, cre_ref[j], _NT, preferred_element_type=F32)
            lim[:, _scan_cols(j)] = -lax.dot_general(dyb, cim_ref[j], _NT, preferred_element_type=F32)

        for j in range(SSM_JB):
            cols = _scan_cols(j)
            ar, ai = _bcast8(are_ref[:, cols]), _bcast8(aim_ref[:, cols])

            def step1(t, s, cols=cols, ar=ar, ai=ai):
                sr, si = s
                rows = _rows8(L - 1 - t)
                return (ar * sr + ai * si + lre[rows, cols], ar * si - ai * sr + lim[rows, cols])

            zero = jnp.zeros((SUBLANES, SSM_SB), F32)
            er, ei = lax.fori_loop(0, L, step1, (zero, zero), unroll=4)
            end_re[:, cols] = er
            end_im[:, cols] = ei

        alr, ali = alr_ref[...], ali_ref[...]
        cr, ci = car_re[...], car_im[...]
        ini_re[SUBLANES - 1:SUBLANES, :] = cr
        ini_im[SUBLANES - 1:SUBLANES, :] = ci
        for i in range(SUBLANES - 2, -1, -1):
            er, ei = end_re[i + 1:i + 2, :], end_im[i + 1:i + 2, :]
            cr, ci = alr * cr + ali * ci + er, alr * ci - ali * cr + ei
            ini_re[i:i + 1, :] = cr
            ini_im[i:i + 1, :] = ci

        for j in range(SSM_JB):
            cols = _scan_cols(j)
            ar, ai = _bcast8(are_ref[:, cols]), _bcast8(aim_ref[:, cols])

            def step2(t, s, cols=cols, ar=ar, ai=ai):
                sr, si = s
                rows = _rows8(L - 1 - t)
                nr = ar * sr + ai * si + lre[rows, cols]
                ni = ar * si - ai * sr + lim[rows, cols]
                lre[rows, cols] = nr
                lim[rows, cols] = ni
                return nr, ni

            lax.fori_loop(0, L, step2, (ini_re[:, cols], ini_im[:, cols]), unroll=4)

        car_re[...] = lre[0:1, :]
        car_im[...] = lim[0:1, :]

        head, tail, body_rows = slice(0, SUBLANES), slice(SUBLANES, T), slice(0, T - SUBLANES)
        for j in range(SSM_JB):
            cols = _scan_cols(j)
            ch = slice(j * LANES, (j + 1) * LANES)
            lr, li = lre[:, cols], lim[:, cols]
            dar_ref[:, cols] += (_colsum(lre[tail, cols] * sre_ref[body_rows, cols] + lim[tail, cols] * sim_ref[body_rows, cols])
                                 + _colsum(lre[head, cols] * ire_ref[:, cols] + lim[head, cols] * iim_ref[:, cols]))
            dai_ref[:, cols] += (_colsum(lim[tail, cols] * sre_ref[body_rows, cols] - lre[tail, cols] * sim_ref[body_rows, cols])
                                 + _colsum(lim[head, cols] * ire_ref[:, cols] - lre[head, cols] * iim_ref[:, cols]))
            uf = u_ref[:, ch]
            ub = uf.astype(_MXU)
            bur = jnp.dot(ub, bre_ref[j], preferred_element_type=F32)
            bui = jnp.dot(ub, bim_ref[j], preferred_element_type=F32)
            dcfr_ref[:, cols] += _colsum(lr * bur + li * bui)
            dcfi_ref[:, cols] += _colsum(li * bur - lr * bui)
            cfr, cfi = cfr_ref[:, cols], cfi_ref[:, cols]
            dbur = (cfr * lr + cfi * li).astype(_MXU)
            dbui = (cfr * li - cfi * lr).astype(_MXU)
            dyf = dy_ref[:, ch]
            dyb = dyf.astype(_MXU)
            du_ref[:, ch] = (lax.dot_general(dbur, bre_ref[j], _NT, preferred_element_type=F32)
                             + lax.dot_general(dbui, bim_ref[j], _NT, preferred_element_type=F32)
                             + d_ref[:, ch] * dyf)
            dbre_ref[j] += lax.dot_general(ub, dbur, _TN, preferred_element_type=F32)
            dbim_ref[j] += lax.dot_general(ub, dbui, _TN, preferred_element_type=F32)
            dcre_ref[j] += lax.dot_general(sre_ref[:, cols].astype(_MXU), dyb, _TN, preferred_element_type=F32)
            dcim_ref[j] -= lax.dot_general(sim_ref[:, cols].astype(_MXU), dyb, _TN, preferred_element_type=F32)
            dd_ref[:, ch] += _colsum(dyf * uf)

        @pl.when(step == nc - 1)
        def _():
            for acc, out in ((dbre_ref, dbre_out), (dbim_ref, dbim_out), (dcre_ref, dcre_out), (dcim_ref, dcim_out)):
                pltpu.sync_copy(acc, out)

    tok = pl.BlockSpec((T, SSM_W), lambda c: (nc - 1 - c, 0))
    st = pl.BlockSpec((T, N_STATES), lambda c: (nc - 1 - c, 0))
    ini = pl.BlockSpec((None, SUBLANES, N_STATES), lambda c: (nc - 1 - c, 0, 0))
    bsp = pl.BlockSpec((SSM_JB, LANES, SSM_SB), lambda c: (0, 0, 0))
    csp = pl.BlockSpec((SSM_JB, SSM_SB, LANES), lambda c: (0, 0, 0))
    row_w = pl.BlockSpec((1, SSM_W), lambda c: (0, 0))
    row_s = pl.BlockSpec((1, N_STATES), lambda c: (0, 0))
    big = pltpu.VMEM((T, N_STATES), F32)
    one = pltpu.VMEM((1, N_STATES), F32)
    eight = pltpu.VMEM((SUBLANES, N_STATES), F32)
    return pl.pallas_call(
        body, name="ssm_bwd", grid=(nc,),
        in_specs=[tok, tok, st, st, ini, ini, bsp, bsp, csp, csp, row_w] + [row_s] * 6,
        out_specs=[tok, _ANY, _ANY, _ANY, _ANY, row_w, row_s, row_s, row_s, row_s],
        out_shape=[jax.ShapeDtypeStruct((seq, SSM_W), F32),
                   jax.ShapeDtypeStruct((SSM_JB, LANES, SSM_SB), F32), jax.ShapeDtypeStruct((SSM_JB, LANES, SSM_SB), F32),
                   jax.ShapeDtypeStruct((SSM_JB, SSM_SB, LANES), F32), jax.ShapeDtypeStruct((SSM_JB, SSM_SB, LANES), F32),
                   jax.ShapeDtypeStruct((1, SSM_W), F32)] + [jax.ShapeDtypeStruct((1, N_STATES), F32)] * 4,
        scratch_shapes=[big, big, one, one, eight, eight, eight, eight,
                        pltpu.VMEM((SSM_JB, LANES, SSM_SB), F32), pltpu.VMEM((SSM_JB, LANES, SSM_SB), F32),
                        pltpu.VMEM((SSM_JB, SSM_SB, LANES), F32), pltpu.VMEM((SSM_JB, SSM_SB, LANES), F32)],
        compiler_params=_params(("arbitrary",)),
    )(dy, u, s_re, s_im, i_re, i_im, b_re, b_im, c_re, c_im, d_skip, *coef)


def _block_diag_b(b):
    t = b.reshape(SSM_JB, 8, STATE, GROUP).transpose(0, 1, 3, 2)
    eye = jnp.eye(8, dtype=b.dtype)
    return (t[:, :, :, None, :] * eye[None, :, None, :, None]).reshape(SSM_JB, LANES, SSM_SB)


def _block_diag_c(c):
    t = c.reshape(SSM_JB, 8, GROUP, STATE).transpose(0, 1, 3, 2)
    eye = jnp.eye(8, dtype=c.dtype)
    return (t[:, :, :, None, :] * eye[None, :, None, :, None]).reshape(SSM_JB, SSM_SB, LANES)


def _diag_of_b(blk):
    t = blk.reshape(SSM_JB, 8, GROUP, 8, STATE)
    d = jnp.stack([t[:, i, :, i, :] for i in range(8)], axis=1)
    return d.transpose(0, 1, 3, 2).reshape(N_GROUPS, STATE, GROUP)


def _diag_of_c(blk):
    t = blk.reshape(SSM_JB, 8, STATE, 8, GROUP)
    d = jnp.stack([t[:, i, :, i, :] for i in range(8)], axis=1)
    return d.transpose(0, 1, 3, 2).reshape(N_GROUPS, GROUP, STATE)


def _to_scan_order(v):
    seq, w = v.shape
    return v.reshape(seq // SSM_T, SUBLANES, SSM_L, w).transpose(0, 2, 1, 3).reshape(seq, w)


def _from_scan_order(v):
    seq, w = v.shape
    return v.reshape(seq // SSM_T, SSM_L, SUBLANES, w).transpose(0, 2, 1, 3).reshape(seq, w)


def _adamw_math(w, g, m, v):
    nm = ADAM_B1 * m + (1.0 - ADAM_B1) * g
    nv = ADAM_B2 * v + (1.0 - ADAM_B2) * jnp.square(g)
    m_hat = nm / (1.0 - ADAM_B1 ** ADAM_STEP)
    v_hat = nv / (1.0 - ADAM_B2 ** ADAM_STEP)
    return -ADAM_LR * (m_hat / (jnp.sqrt(v_hat) + ADAM_EPS) + ADAM_WD * w), nm, nv


def _adamw(w, g, m, v, *, name, tm, deps=()):
    rows, cols = w.shape
    nd = len(deps)

    def body(w_ref, g_ref, m_ref, v_ref, *rest):
        d_ref, nm_ref, nv_ref = rest[nd:]
        d_ref[...], nm_ref[...], nv_ref[...] = _adamw_math(w_ref[...], g_ref[...], m_ref[...], v_ref[...])

    spec = pl.BlockSpec((tm, cols), lambda i: (i, 0))
    shp = jax.ShapeDtypeStruct((rows, cols), F32)
    return pl.pallas_call(body, name=name, grid=(rows // tm,), in_specs=[spec] * 4 + [_ANY] * nd,
                          out_specs=[spec] * 3, out_shape=[shp] * 3,
                          compiler_params=_params(("arbitrary",)))(w, g, m, v, *deps)


def _place():
    x, y, c = lax.axis_index("x"), lax.axis_index("y"), lax.axis_index("c")
    chips = [(1 - x, y), (x, 1 - y), (1 - x, 1 - y)]
    return x, y, c, chips


def _remote(src, dst, send_sem, recv_sem, dev):
    return pltpu.make_async_remote_copy(src_ref=src, dst_ref=dst, send_sem=send_sem, recv_sem=recv_sem,
                                        device_id=dev, device_id_type=MESH)


def _place_shard(w, mine_arr, *, name, tm=256):
    rows, cols = w.shape

    def body(m_ref, w_ref, o_ref):
        o_ref[...] = w_ref[...].astype(o_ref.dtype)

    return pl.pallas_call(
        body, name=name,
        grid_spec=pltpu.PrefetchScalarGridSpec(
            num_scalar_prefetch=1, grid=(rows // tm,),
            in_specs=[pl.BlockSpec((tm, cols), lambda i, m: (i, 0))],
            out_specs=pl.BlockSpec((None, tm, cols), lambda i, m: (m[0], i, 0))),
        out_shape=jax.ShapeDtypeStruct((N_CHIPS, rows, cols), _WIRE),
        compiler_params=_params(("arbitrary",)),
    )(mine_arr, w)


_HBM = pl.BlockSpec(memory_space=pltpu.HBM)
_SEM = pl.BlockSpec(memory_space=pltpu.SEMAPHORE)
_EFFECT = pltpu.SideEffectType.DATAFLOW_SIDE_EFFECTING


def _copies_start(name, bufs, plan, count, after=()):
    nb, na = len(bufs), len(after)

    def body(*refs):
        send_sems, recv_sems, token = refs[nb + na], refs[nb + na + 1], refs[-1]
        copies = plan(refs[:nb])
        assert len(copies) == count
        for i, (src, dst, dev, _) in enumerate(copies):
            _remote(src, dst, send_sems.at[i], recv_sems.at[i], dev).start()
        token[...] = jnp.zeros_like(token)

    res = pl.pallas_call(
        body, name=name, in_specs=[_HBM] * nb + [_ANY] * na,
        out_specs=(_SEM, _SEM, *[_HBM] * nb, pl.BlockSpec(memory_space=pltpu.VMEM)),
        out_shape=(pltpu.SemaphoreType.DMA((count,)), pltpu.SemaphoreType.DMA((count,)),
                   *[pltpu.HBM(b.shape, b.dtype) for b in bufs], jax.ShapeDtypeStruct((SUBLANES, LANES), F32)),
        input_output_aliases={i: 2 + i for i in range(nb)},
        compiler_params=pltpu.CompilerParams(has_side_effects=_EFFECT),
    )(*[pltpu.with_memory_space_constraint(b, pltpu.HBM) for b in bufs], *after)
    return (res[0], res[1]), list(res[2:2 + nb]), res[-1]


def _copies_wait(name, bufs, sems, plan, after=()):
    nb, na = len(bufs), len(after)

    def body(*refs):
        send_sems, recv_sems = refs[nb], refs[nb + 1]
        for i, (src, _, dev, land) in enumerate(plan(refs[:nb])):
            cp = _remote(src, land, send_sems.at[i], recv_sems.at[i], dev)
            cp.wait_send()
            cp.wait_recv()

    res = pl.pallas_call(
        body, name=name, in_specs=[_HBM] * nb + [_SEM, _SEM] + [_ANY] * na, out_specs=[_HBM] * nb,
        out_shape=[pltpu.HBM(b.shape, b.dtype) for b in bufs],
        input_output_aliases={i: i for i in range(nb)},
        compiler_params=pltpu.CompilerParams(has_side_effects=_EFFECT),
    )(*bufs, *sems, *after)
    return list(res)


def _plan_gather_ici(fulls):
    x, y, c, chips = _place()
    copies = []
    for f in fulls:
        half = pl.ds(c * (f.shape[1] // 2), f.shape[1] // 2)
        own = f.at[2 * x + y, half]
        for chip in chips:
            copies.append((own, own, (*chip, c), f.at[2 * chip[0] + chip[1], half]))
    return copies


def _plan_gather_d2d(fulls):
    x, y, c, chips = _place()
    copies = []
    for f in fulls:
        r2 = f.shape[1] // 2
        for chip in chips:
            blk = 2 * chip[0] + chip[1]
            landed = f.at[blk, pl.ds(c * r2, r2)]
            copies.append((landed, landed, (x, y, 1 - c), f.at[blk, pl.ds((1 - c) * r2, r2)]))
    return copies


def _plan_swap_halves(refs):
    x, y, c, _ = _place()
    n = len(refs) // 2
    copies = []
    for g, land in zip(refs[:n], refs[n:]):
        r2 = g.shape[1] // 2
        copies.append((g.at[:, pl.ds((1 - c) * r2, r2), :], land, (x, y, 1 - c), land))
    return copies


def _plan_scatter_chips(refs):
    x, y, c, chips = _place()
    n = len(refs) // 2
    copies = []
    for h, land in zip(refs[:n], refs[n:]):
        for k, chip in enumerate(chips):
            copies.append((h.at[2 * chip[0] + chip[1]], land.at[k], (*chip, c), land.at[k]))
    return copies


def _plan_join_halves(totals):
    x, y, c, _ = _place()
    copies = []
    for t in totals:
        r2 = t.shape[0] // 2
        mine = t.at[pl.ds(c * r2, r2)]
        copies.append((mine, mine, (x, y, 1 - c), t.at[pl.ds((1 - c) * r2, r2)]))
    return copies


def _add_sibling_half(g, got, c_arr, *, name, tm):
    _, rows, cols = g.shape
    r2 = rows // 2
    nb = r2 // tm

    def body(c_ref, g_ref, r_ref, o_ref):
        o_ref[...] = (g_ref[...].astype(F32) + r_ref[...].astype(F32)).astype(o_ref.dtype)

    return pl.pallas_call(
        body, name=name,
        grid_spec=pltpu.PrefetchScalarGridSpec(
            num_scalar_prefetch=1, grid=(N_CHIPS, nb),
            in_specs=[pl.BlockSpec((None, tm, cols), lambda b, i, c: (b, c[0] * nb + i, 0)),
                      pl.BlockSpec((None, tm, cols), lambda b, i, c: (b, i, 0))],
            out_specs=pl.BlockSpec((None, tm, cols), lambda b, i, c: (b, i, 0))),
        out_shape=jax.ShapeDtypeStruct((N_CHIPS, r2, cols), _WIRE),
        compiler_params=_params(("arbitrary", "arbitrary")),
    )(c_arr, g, got)


def _add_chips(h, got, place_arr, *, name, tm):
    _, r2, cols = h.shape
    nb = r2 // tm

    def body(p_ref, h_ref, r_ref, o_ref):
        o_ref[...] = ((h_ref[...].astype(F32) + r_ref[0].astype(F32)) + r_ref[1].astype(F32)) + r_ref[2].astype(F32)

    return pl.pallas_call(
        body, name=name,
        grid_spec=pltpu.PrefetchScalarGridSpec(
            num_scalar_prefetch=1, grid=(nb,),
            in_specs=[pl.BlockSpec((None, tm, cols), lambda i, p: (p[0], i, 0)),
                      pl.BlockSpec((3, tm, cols), lambda i, p: (0, i, 0))],
            out_specs=pl.BlockSpec((tm, cols), lambda i, p: (p[1] * nb + i, 0))),
        out_shape=jax.ShapeDtypeStruct((2 * r2, cols), F32),
        compiler_params=_params(("arbitrary",)),
    )(place_arr, h, got)


class _ReduceScatter:
    def __init__(self, tag, names, grads):
        self.tag, self.names, self.n = tag, names, len(names)
        core = lax.axis_index("c").astype(jnp.int32)
        chip = (2 * lax.axis_index("x") + lax.axis_index("y")).astype(jnp.int32)
        self.c_arr, self.place_arr = core.reshape(1), jnp.stack([chip, core])
        self.bufs = list(grads)

    def _start(self, step, bufs, plan, count, after):
        self.plan = plan
        self.step = f"grad_{step}_{self.tag}"
        self.sems, self.bufs, token = _copies_start(self.step + "_start", bufs, plan, count, after)
        return [token]

    def _wait(self, after):
        self.bufs = _copies_wait(self.step + "_wait", self.bufs, self.sems, self.plan, after)
        return self.bufs

    def start_swap(self, after=()):
        lands = [lax.empty((N_CHIPS, g.shape[1] // 2, g.shape[2]), g.dtype) for g in self.bufs]
        return self._start("swap", self.bufs + lands, _plan_swap_halves, self.n, after)

    def start_scatter(self, after):
        bufs = self._wait(after)
        pair = [_add_sibling_half(g, r, self.c_arr, name=f"grad_add_sibling_{nm}", tm=min(256, g.shape[1] // 2))
                for nm, g, r in zip(self.names, bufs[:self.n], bufs[self.n:])]
        lands = [lax.empty((3,) + h.shape[1:], h.dtype) for h in pair]
        return self._start("scatter", pair + lands, _plan_scatter_chips, 3 * self.n, ())

    def start_join(self, after):
        bufs = self._wait(after)
        total = [_add_chips(h, r, self.place_arr, name=f"grad_add_chips_{nm}", tm=min(256, h.shape[1]))
                 for nm, h, r in zip(self.names, bufs[:self.n], bufs[self.n:])]
        return self._start("join", total, _plan_join_halves, self.n, ())

    def finish(self, after):
        return dict(zip(self.names, self._wait(after)))


def _all_gather_small(v):
    m_per, n = v.shape

    def body(x_ref, out_ref, send_sems, recv_sems, local_sem):
        x, y, c, chips = _place()
        me, sibling = (x, y, c), (x, y, 1 - c)

        def rows(px, py, pc):
            return out_ref.at[4 * px + 2 * py + pc]

        def copy(k, block, to, src=None):
            return _remote(rows(*block) if src is None else src, rows(*block), send_sems.at[k], recv_sems.at[k], to)

        mine = pltpu.make_async_copy(x_ref, rows(*me), local_sem)
        mine.start()
        first = [copy(0, me, sibling, src=x_ref)]
        first += [copy(1 + j, me, (*chip, c), src=x_ref) for j, chip in enumerate(chips)]
        for cp in first:
            cp.start()
        passed = [copy(4 + j, (*chip, c), sibling) for j, chip in enumerate(chips)]
        for j, chip in enumerate(chips):
            copy(1 + j, (*chip, c), me).wait_recv()
            passed[j].start()
        copy(0, sibling, me).wait_recv()
        for j, chip in enumerate(chips):
            copy(4 + j, (*chip, 1 - c), me).wait_recv()
        for cp in first + passed:
            cp.wait_send()
        mine.wait()

    return pl.pallas_call(
        body, name="gather_small_grads",
        out_shape=jax.ShapeDtypeStruct((8, m_per, n), v.dtype),
        in_specs=[pl.BlockSpec(memory_space=pltpu.VMEM)], out_specs=pl.BlockSpec(memory_space=pltpu.VMEM),
        scratch_shapes=[pltpu.SemaphoreType.DMA((7,)), pltpu.SemaphoreType.DMA((7,)), pltpu.SemaphoreType.DMA],
        compiler_params=pltpu.CompilerParams(vmem_limit_bytes=VMEM_LIMIT),
    )(v)


def _sum8(v):
    _, m, n = v.shape

    def body(v_ref, o_ref):
        acc = v_ref[0]
        for d in range(1, 8):
            acc = acc + v_ref[d]
        o_ref[...] = acc

    return pl.pallas_call(body, name="sum_small_grads", out_shape=jax.ShapeDtypeStruct((m, n), F32),
                          compiler_params=pltpu.CompilerParams(vmem_limit_bytes=VMEM_LIMIT))(v)


def _heads(v, nh):
    return v.reshape(v.shape[0], nh, HEAD_DIM).transpose(1, 0, 2)


def _unheads(v):
    return v.transpose(1, 0, 2).reshape(v.shape[1], v.shape[0] * HEAD_DIM)


def _local_step(x, target, norm_w, q_norm_w, k_norm_w, sinks, a_re, a_im, log_dt, b_re, b_im, c_re, c_im, d_skip,
                b_glu, io):
    seq = x.shape[0]
    qw, kw = q_norm_w.reshape(1, HEAD_DIM), k_norm_w.reshape(1, HEAD_DIM)
    nw, bg = norm_w.reshape(1, D_MODEL), b_glu.reshape(1, D_MODEL)
    dsk = d_skip.reshape(1, SSM_W)

    h, rstd = _rms_fwd(x, nw, deps=io.begin())
    w_in4 = io.weight("w_in", h)
    proj = _mm(h, w_in4, mode="nn", name="mm_proj", tm=512, tn=IN_W // 4, tk=D_MODEL, b_blocked=True)
    q = _heads(proj[:, :ATTN_W], N_Q_HEADS)
    k = _heads(proj[:, ATTN_W:ATTN_W + KV_W], N_KV_HEADS)
    v = _heads(proj[:, ATTN_W + KV_W:ATTN_W + 2 * KV_W], N_KV_HEADS)
    attn_h, lse = _attn_fwd(q, k, v, qw, kw, sinks, deps=io.after_proj(proj))
    attn = _unheads(attn_h)

    def gate_a(at, ag):
        return (at * (ag * _sigmoid(ag)),)

    (ya_in,) = _ew(gate_a, [(attn, "mat", 0), (proj, "mat", OFF_AGATE)], [(ATTN_W, _MXU)], rows=seq, ncol=2,
                   name="ew_attn_gate")
    w_ap4 = io.weight("w_attn_proj", ya_in)
    w_glu4, w_sp4, w_out = io.weight("w_glu", ya_in), io.weight("w_ssm_proj", ya_in), io.weight("w_out", ya_in)
    y_a = _mm(ya_in, w_ap4, mode="nn", name="mm_attn_proj", tm=1024, tn=512, tk=ATTN_W, b_blocked=True)

    coef = [t.reshape(1, N_STATES) for t in _ssm_params_fwd(a_re, a_im, log_dt.reshape(N_GROUPS, 1))]
    bre_blk, bim_blk = _block_diag_b(b_re).astype(_MXU), _block_diag_b(b_im).astype(_MXU)
    cre_blk, cim_blk = _block_diag_c(c_re).astype(_MXU), _block_diag_c(c_im).astype(_MXU)
    u_scan = _to_scan_order(proj[:, OFF_U * CW:OFF_U * CW + SSM_W])
    y_scan, s_re, s_im, i_re, i_im = _ssm_fwd(u_scan, bre_blk, bim_blk, cre_blk, cim_blk, dsk, coef)
    y_ssm = _from_scan_order(y_scan)

    (yg,) = _ew(lambda yv: (jax.nn.gelu(yv),), [(y_ssm, "mat", 0)], [(SSM_W, _MXU)], rows=seq, ncol=2, name="ew_gelu")
    glu = _mm(yg, w_glu4, mode="nn", name="mm_glu", tm=1024, tn=512, tk=SSM_W, b_blocked=True)

    def gate_s(ga, gb, ba, bb, z):
        return ((ga + ba) * _sigmoid(gb + bb) * (z * _sigmoid(z)),)

    (ys_in,) = _ew(gate_s, [(glu, "mat", 0), (glu, "mat", 2), (bg, "row", 0), (bg, "row", 2), (proj, "mat", OFF_Z)],
                   [(SSM_W, _MXU)], rows=seq, ncol=2, name="ew_ssm_gate")
    y_s = _mm(ys_in, w_sp4, mode="nn", name="mm_ssm_proj", tm=1024, tn=512, tk=SSM_W, b_blocked=True)

    def merge(ga, gs, ya, ys):
        return (_sigmoid(ga) * ya + _sigmoid(gs) * ys,)

    (merged,) = _ew(merge, [(proj, "mat", OFF_GA), (proj, "mat", OFF_GS), (y_a, "mat", 0), (y_s, "mat", 0)],
                    [(D_MODEL, _MXU)], rows=seq, ncol=4, name="ew_merge")
    mo = _mm(merged, w_out, mode="nn", name="mm_out", tm=512, tn=D_MODEL, tk=D_MODEL)

    def loss_head(xv, mv, tv):
        err = (xv + mv) - tv
        dout = err * (1.0 / D_MODEL)
        return dout, dout, _colsum(err * err)

    dout, dout_b, sq = _ew(loss_head, [(x, "mat", 0), (mo, "mat", 0), (target, "mat", 0)],
                           [(D_MODEL, F32), (D_MODEL, _MXU)], rows=seq, ncol=4, n_acc=1, name="ew_loss")
    loss = 0.5 * jnp.sum(sq) / D_MODEL

    d_merged = _mm(dout_b, w_out, mode="nt", name="mm_d_merged", tm=512, tn=D_MODEL, tk=D_MODEL)
    g_w_out = _mm(merged, dout_b, mode="tn", name="mm_g_w_out", tm=1024, tn=D_MODEL, tk=1024, out_dtype=_WIRE)

    def merge_bwd(dm, ga, gs, ya, ys):
        sa, ss = _sigmoid(ga), _sigmoid(gs)
        return sa * dm, ss * dm, dm * ya * sa * (1.0 - sa), dm * ys * ss * (1.0 - ss)

    d_ya, d_ys, d_ga, d_gs = _ew(
        merge_bwd, [(d_merged, "mat", 0), (proj, "mat", OFF_GA), (proj, "mat", OFF_GS), (y_a, "mat", 0), (y_s, "mat", 0)],
        [(D_MODEL, _MXU)] * 4, rows=seq, ncol=4, name="ew_merge_bwd")

    d_ya_in = _mm(d_ya, w_ap4, mode="nt", name="mm_d_attn_gate", tm=1024, tn=ATTN_W, tk=512, b_blocked=True)
    g_w_ap = _mm(ya_in, d_ya, mode="tn", name="mm_g_w_attn_proj", tm=ATTN_W, tn=512, tk=1024, out_dtype=_WIRE,
                 out_blocked=True)

    d_ys_in = _mm(d_ys, w_sp4, mode="nt", name="mm_d_ssm_gate", tm=1024, tn=SSM_W, tk=512, b_blocked=True)
    g_w_sp = _mm(ys_in, d_ys, mode="tn", name="mm_g_w_ssm_proj", tm=SSM_W, tn=512, tk=1024, out_dtype=_WIRE,
                 out_blocked=True)

    def gate_s_bwd(dv, ga, gb, ba, bb, z):
        a, sb = ga + ba, _sigmoid(gb + bb)
        f, df = _silu_and_grad(z)
        dga = dv * sb * f
        dgb = dv * a * f * sb * (1.0 - sb)
        return dga, dgb, dv * a * sb * df, _colsum(dga), _colsum(dgb)

    d_glu_a, d_glu_b, d_z, g_bga, g_bgb = _ew(
        gate_s_bwd, [(d_ys_in, "mat", 0), (glu, "mat", 0), (glu, "mat", 2), (bg, "row", 0), (bg, "row", 2),
                     (proj, "mat", OFF_Z)],
        [(SSM_W, _MXU)] * 3, rows=seq, ncol=2, n_acc=2, name="ew_ssm_gate_bwd")
    d_glu = jnp.concatenate([d_glu_a, d_glu_b], axis=1)
    d_yg = _mm(d_glu, w_glu4, mode="nt", name="mm_d_gelu", tm=1024, tn=SSM_W, tk=512, b_blocked=True)
    g_w_glu = _mm(yg, d_glu, mode="tn", name="mm_g_w_glu", tm=SSM_W, tn=512, tk=1024, out_dtype=_WIRE, out_blocked=True)
    dep = io.later_grads(dict(w_attn_proj=g_w_ap, w_glu=g_w_glu, w_ssm_proj=g_w_sp,
                              w_out=g_w_out.reshape(N_CHIPS, D_MODEL // N_CHIPS, D_MODEL)))

    def gate_a_bwd(dv, at, ag):
        f, df = _silu_and_grad(ag)
        return dv * f, dv * at * df

    d_attn, d_agate = _ew(gate_a_bwd, [(d_ya_in, "mat", 0), (attn, "mat", 0), (proj, "mat", OFF_AGATE)],
                          [(ATTN_W, F32), (ATTN_W, _MXU)], rows=seq, ncol=2, name="ew_attn_gate_bwd", deps=dep)

    def gelu_bwd(dv, yv):
        return (jax.vjp(jax.nn.gelu, yv)[1](dv)[0],)

    (d_yssm,) = _ew(gelu_bwd, [(d_yg, "mat", 0), (y_ssm, "mat", 0)], [(SSM_W, F32)], rows=seq, ncol=2, name="ew_gelu_bwd",
                    deps=dep)
    dep = io.before_attention_backward([d_attn, d_yssm])
    dq_h, dk_h, dv_h, g_qw, g_kw, g_sk = _attn_bwd(q, k, v, qw, kw, sinks, lse, _heads(d_attn, N_Q_HEADS), deps=dep)
    (du_scan, g_bre, g_bim, g_cre, g_cim, g_dsk, g_abr, g_abi, g_cfr, g_cfi) = _ssm_bwd(
        _to_scan_order(d_yssm), u_scan, s_re, s_im, i_re, i_im, bre_blk, bim_blk, cre_blk, cim_blk, dsk, coef)
    g_are, g_aim, g_ldt = _ssm_params_bwd(a_re, a_im, log_dt.reshape(N_GROUPS, 1),
                                          *[t.reshape(N_GROUPS, STATE) for t in (g_abr, g_abi, g_cfr, g_cfi)])
    d_u = _from_scan_order(du_scan)

    d_proj = jnp.concatenate(
        [_unheads(dq_h).astype(_MXU), _unheads(dk_h).astype(_MXU), _unheads(dv_h).astype(_MXU), d_agate,
         d_u.astype(_MXU), d_z, d_ga, d_gs], axis=1)
    g_w_in = _mm(h, d_proj, mode="tn", name="mm_g_w_in", tm=1024, tn=IN_W // 4, tk=1024, out_dtype=_WIRE,
                 out_blocked=True, deps=io.before_input_projection_grad([d_proj]))
    dep = io.input_projection_grad(g_w_in)
    d_h = _mm(d_proj, w_in4, mode="nt", name="mm_d_h", tm=512, tn=D_MODEL, tk=IN_W // 4, b_blocked=True, deps=dep)
    grad_x, g_nw = _rms_bwd(d_h, x, rstd, nw, dout)

    small = dict(
        norm_w=g_nw.reshape(D_MODEL), q_norm_w=g_qw.reshape(HEAD_DIM), k_norm_w=g_kw.reshape(HEAD_DIM),
        sinks=g_sk[:, :, 0].reshape(N_Q_HEADS), A_re=g_are, A_im=g_aim, log_dt=g_ldt.reshape(N_GROUPS),
        B_re=_diag_of_b(g_bre), B_im=_diag_of_b(g_bim), C_re=_diag_of_c(g_cre), C_im=_diag_of_c(g_cim),
        D_skip=g_dsk.reshape(N_GROUPS, GROUP), b_glu=jnp.concatenate([g_bga, g_bgb], axis=1).reshape(D_MODEL))
    return loss, grad_x, small


_SMALL = ["norm_w", "q_norm_w", "k_norm_w", "sinks", "A_re", "A_im", "log_dt", "B_re", "B_im", "C_re", "C_im",
          "D_skip", "b_glu"]
_BIG = ["w_in", "w_attn_proj", "w_glu", "w_ssm_proj", "w_out"]
_LATER = _BIG[1:]
_ORDER = ["norm_w", "w_in", "q_norm_w", "k_norm_w", "sinks", "w_attn_proj", "A_re", "A_im", "log_dt", "B_re", "B_im",
          "C_re", "C_im", "D_skip", "w_glu", "b_glu", "w_ssm_proj", "w_out"]
_PACK_W = 1024


def _packed_rows(size):
    unit = SUBLANES * _PACK_W
    return -(-size // unit) * SUBLANES


def _pack_small(d):
    parts = []
    for n in _SMALL:
        flat = d[n].reshape(-1).astype(F32)
        rows = _packed_rows(flat.shape[0])
        parts.append(jnp.pad(flat, (0, rows * _PACK_W - flat.shape[0])).reshape(rows, _PACK_W))
    return jnp.concatenate(parts, axis=0)


def _unpack_small(packed, like):
    out, pos = {}, 0
    for n in _SMALL:
        rows = _packed_rows(like[n].size)
        out[n] = packed[pos:pos + rows].reshape(-1)[:like[n].size].reshape(like[n].shape)
        pos += rows
    return out


def _as2d(a):
    return a.reshape(1, -1) if a.ndim == 1 else a


def _adamw_whole(w, g, m, v, *, name):
    shape = w.shape
    w, g, m, v = _as2d(w), _as2d(g), _as2d(m), _as2d(v)

    def body(w_ref, g_ref, m_ref, v_ref, d_ref, nm_ref, nv_ref):
        d_ref[...], nm_ref[...], nv_ref[...] = _adamw_math(w_ref[...], g_ref[...], m_ref[...], v_ref[...])

    outs = pl.pallas_call(body, name=name, out_shape=[jax.ShapeDtypeStruct(w.shape, F32)] * 3)(w, g, m, v)
    return [o.reshape(shape) for o in outs]


class _Exchanges:
    def __init__(self, w, m, v):
        self.w, self.m, self.v = w, m, v
        self.grads, self.delta, self.new_m, self.new_v = {}, {}, {}, {}

    def _adamw(self, names, deps):
        for n in names:
            self.delta[n], self.new_m[n], self.new_v[n] = _adamw(
                self.w[n], self.grads[n], self.m[n], self.v[n], name=f"adamw_{n}", tm=128, deps=deps)

    def begin(self):
        chip = (2 * lax.axis_index("x") + lax.axis_index("y")).astype(jnp.int32).reshape(1)
        full = {n: _place_shard(self.w[n], chip, name=f"place_{n}") for n in _BIG}
        self.first = _copies_start("gather_ici_w_in_start", [full["w_in"]], _plan_gather_ici, 3)
        self.rest = _copies_start("gather_ici_rest_start", [full[n] for n in _LATER], _plan_gather_ici,
                                  3 * len(_LATER), after=[self.first[2]])
        return [self.rest[2]]

    def weight(self, name, after):
        if name == "w_in":
            sems, bufs, _ = self.first
            bufs = _copies_wait("gather_ici_w_in_wait", bufs, sems, _plan_gather_ici, [after])
            sems, bufs, token = _copies_start("gather_d2d_w_in_start", bufs, _plan_gather_d2d, 3)
            return _copies_wait("gather_d2d_w_in_wait", bufs, sems, _plan_gather_d2d, [token])[0]
        if self.rest is not None:
            sems, bufs = self.rest
            later = dict(zip(_LATER, _copies_wait("gather_d2d_rest_wait", bufs, sems, _plan_gather_d2d, [after])))
            later["w_out"] = later["w_out"].reshape(D_MODEL, D_MODEL)
            self.later, self.rest = later, None
        return self.later[name]

    def after_proj(self, proj):
        sems, bufs, _ = self.rest
        bufs = _copies_wait("gather_ici_rest_wait", bufs, sems, _plan_gather_ici, [proj])
        sems, bufs, token = _copies_start("gather_d2d_rest_start", bufs, _plan_gather_d2d, 3 * len(_LATER))
        self.rest = (sems, bufs)
        return [token]

    def later_grads(self, grads):
        self.rs_later = _ReduceScatter("later", _LATER, [grads[n] for n in _LATER])
        return self.rs_later.start_swap()

    def before_attention_backward(self, after):
        return self.rs_later.start_scatter(after)

    def before_input_projection_grad(self, after):
        return self.rs_later.start_join(after)

    def input_projection_grad(self, g_w_in):
        self.grads.update(self.rs_later.finish([g_w_in]))
        self.rs_in = _ReduceScatter("w_in", ["w_in"], [g_w_in])
        self._adamw(_LATER, self.rs_in.start_swap())
        return self.rs_in.start_scatter([self.delta[n] for n in _LATER])

    def finish(self, after):
        self.grads.update(self.rs_in.finish(self.rs_in.start_join(after)))
        self._adamw(["w_in"], ())


def kernel(x, norm_w, w_in, q_norm_w, k_norm_w, sinks, w_attn_proj, A_re, A_im, log_dt, B_re, B_im, C_re, C_im, D_skip, w_glu, b_glu, w_ssm_proj, w_out, loss_target, m_norm_w, m_w_in, m_q_norm_w, m_k_norm_w, m_sinks, m_w_attn_proj, m_A_re, m_A_im, m_log_dt, m_B_re, m_B_im, m_C_re, m_C_im, m_D_skip, m_w_glu, m_b_glu, m_w_ssm_proj, m_w_out, v_norm_w, v_w_in, v_q_norm_w, v_k_norm_w, v_sinks, v_w_attn_proj, v_A_re, v_A_im, v_log_dt, v_B_re, v_B_im, v_C_re, v_C_im, v_D_skip, v_w_glu, v_b_glu, v_w_ssm_proj, v_w_out):
    w = dict(norm_w=norm_w, w_in=w_in, q_norm_w=q_norm_w, k_norm_w=k_norm_w, sinks=sinks, w_attn_proj=w_attn_proj,
             A_re=A_re, A_im=A_im, log_dt=log_dt, B_re=B_re, B_im=B_im, C_re=C_re, C_im=C_im, D_skip=D_skip,
             w_glu=w_glu, b_glu=b_glu, w_ssm_proj=w_ssm_proj, w_out=w_out)
    m = dict(norm_w=m_norm_w, w_in=m_w_in, q_norm_w=m_q_norm_w, k_norm_w=m_k_norm_w, sinks=m_sinks,
             w_attn_proj=m_w_attn_proj, A_re=m_A_re, A_im=m_A_im, log_dt=m_log_dt, B_re=m_B_re, B_im=m_B_im,
             C_re=m_C_re, C_im=m_C_im, D_skip=m_D_skip, w_glu=m_w_glu, b_glu=m_b_glu, w_ssm_proj=m_w_ssm_proj,
             w_out=m_w_out)
    v = dict(norm_w=v_norm_w, w_in=v_w_in, q_norm_w=v_q_norm_w, k_norm_w=v_k_norm_w, sinks=v_sinks,
             w_attn_proj=v_w_attn_proj, A_re=v_A_re, A_im=v_A_im, log_dt=v_log_dt, B_re=v_B_re, B_im=v_B_im,
             C_re=v_C_re, C_im=v_C_im, D_skip=v_D_skip, w_glu=v_w_glu, b_glu=v_b_glu, w_ssm_proj=v_w_ssm_proj,
             w_out=v_w_out)

    io = _Exchanges(w, m, v)
    loss, grad_x, small = _local_step(x[0], loss_target[0], norm_w, q_norm_w, k_norm_w, sinks, A_re, A_im, log_dt,
                                      B_re, B_im, C_re, C_im, D_skip, b_glu, io)
    loss = lax.psum(loss, ("x", "y", "c"))

    grads, delta, new_m, new_v = io.grads, io.delta, io.new_m, io.new_v
    small_sum = _sum8(_all_gather_small(_pack_small(small)))
    grads.update(_unpack_small(small_sum, w))
    for n in _SMALL:
        delta[n], new_m[n], new_v[n] = _adamw_whole(w[n], grads[n], m[n], v[n], name=f"adamw_{n}")
    io.finish([delta[_SMALL[-1]]])

    return (loss, grad_x[None], *[grads[n] for n in _ORDER], *[delta[n] for n in _ORDER],
            *[new_m[n] for n in _ORDER], *[new_v[n] for n in _ORDER])
```

```python
import functools
import math

import jax
import jax.numpy as jnp
from jax import lax
from jax.experimental import pallas as pl
from jax.experimental.pallas import tpu as pltpu

F32 = jnp.float32
_MXU = jnp.bfloat16
_WIRE = jnp.bfloat16

LANES = 128
SUBLANES = 8
VMEM_LIMIT = 56 * 1024 * 1024

D_MODEL = 2048
HEAD_DIM = 64
N_Q_HEADS = 16
N_KV_HEADS = 4
Q_PER_KV = 4
ATTN_W = 1024
KV_W = 256
WINDOW = 128
SSM_W = 1024
GROUP = 16
N_GROUPS = 64
STATE = 64
N_STATES = N_GROUPS * STATE
IN_W = 8704
NORM_EPS = 1e-6
N_CHIPS = 4
CW = 512
OFF_AGATE, OFF_U, OFF_Z, OFF_GA, OFF_GS = 3, 5, 7, 9, 13

SSM_T = 256
SSM_L = SSM_T // SUBLANES
SSM_JB = 8
SSM_SB = N_STATES // SSM_JB

ADAM_LR, ADAM_B1, ADAM_B2, ADAM_EPS, ADAM_WD, ADAM_STEP = 0.001, 0.9, 0.999, 1e-08, 0.01, 10

MESH = pl.DeviceIdType.MESH
_ANY = pl.BlockSpec(memory_space=pl.ANY)


def _params(sem=None):
    return pltpu.CompilerParams(dimension_semantics=sem, vmem_limit_bytes=VMEM_LIMIT)


def _mm(a, b, *, mode, name, tm, tn, tk, out_dtype=F32, b_blocked=False, out_blocked=False, deps=()):
    nd = len(deps)
    if mode == "tn":
        K, M = a.shape
    else:
        M, K = a.shape
    if mode == "nn":
        N = b.shape[0] * b.shape[2] if b_blocked else b.shape[1]
    elif mode == "nt":
        N = b.shape[1] if b_blocked else b.shape[0]
    else:
        N = b.shape[1]
    tm, tn, tk = min(tm, M), min(tn, N), min(tk, K)
    nj, ni, nk = N // tn, M // tm, K // tk
    assert nj * tn == N and ni * tm == M and nk * tk == K, (name, M, N, K)
    dims = {"nn": (((1,), (0,)), ((), ())), "nt": (((1,), (1,)), ((), ())), "tn": (((0,), (0,)), ((), ()))}[mode]

    if mode == "tn":
        a_spec = pl.BlockSpec((tk, tm), lambda j, i, k: (k, i))
    else:
        a_spec = pl.BlockSpec((tm, tk), lambda j, i, k: (i, k))
    if mode == "nn":
        if b_blocked:
            assert b.shape[0] == nj and b.shape[2] == tn
            b_spec = pl.BlockSpec((None, tk, tn), lambda j, i, k: (j, k, 0))
        else:
            b_spec = pl.BlockSpec((tk, tn), lambda j, i, k: (k, j))
    elif mode == "nt":
        if b_blocked:
            assert b.shape[0] == nk and b.shape[2] == tk
            b_spec = pl.BlockSpec((None, tn, tk), lambda j, i, k: (k, j, 0))
        else:
            b_spec = pl.BlockSpec((tn, tk), lambda j, i, k: (j, k))
    else:
        b_spec = pl.BlockSpec((tk, tn), lambda j, i, k: (k, j))
    if out_blocked:
        assert nj == N_CHIPS
        o_spec = pl.BlockSpec((None, tm, tn), lambda j, i, k: (j, i, 0))
        o_shape = jax.ShapeDtypeStruct((nj, M, tn), out_dtype)
    else:
        o_spec = pl.BlockSpec((tm, tn), lambda j, i, k: (i, j))
        o_shape = jax.ShapeDtypeStruct((M, N), out_dtype)
    use_acc = nk > 1 and out_dtype != F32

    def body(a_ref, b_ref, *rest):
        o_ref, scratch = rest[nd], rest[nd + 1:]
        part = lax.dot_general(a_ref[...].astype(_MXU), b_ref[...].astype(_MXU), dims,
                               preferred_element_type=F32)
        if nk == 1:
            o_ref[...] = part.astype(o_ref.dtype)
            return
        k = pl.program_id(2)
        acc = scratch[0] if use_acc else o_ref

        @pl.when(k == 0)
        def _():
            acc[...] = part

        @pl.when(k > 0)
        def _():
            acc[...] += part

        if use_acc:
            @pl.when(k == nk - 1)
            def _():
                o_ref[...] = acc[...].astype(o_ref.dtype)

    return pl.pallas_call(
        body, name=name, grid=(nj, ni, nk), in_specs=[a_spec, b_spec] + [_ANY] * nd, out_specs=o_spec,
        out_shape=o_shape, scratch_shapes=[pltpu.VMEM((tm, tn), F32)] if use_acc else [],
        compiler_params=_params(("parallel", "parallel", "arbitrary")),
    )(a, b, *deps)


def _ew(fn, ins, outs, *, rows, ncol, name, n_acc=0, tm=512, deps=()):
    n_in, n_out, nd = len(ins), len(outs), len(deps)
    tm = min(tm, rows)
    in_specs = []
    for _, kind, col0 in ins:
        if kind == "mat":
            in_specs.append(pl.BlockSpec((tm, CW), lambda j, i, c0=col0: (i, c0 + j)))
        else:
            in_specs.append(pl.BlockSpec((1, CW), lambda j, i, c0=col0: (0, c0 + j)))
    out_specs = [pl.BlockSpec((tm, CW), lambda j, i: (i, j)) for _ in outs]
    out_shape = [jax.ShapeDtypeStruct((rows, w), dt) for w, dt in outs]
    for _ in range(n_acc):
        out_specs.append(pl.BlockSpec((1, CW), lambda j, i: (0, j)))
        out_shape.append(jax.ShapeDtypeStruct((1, ncol * CW), F32))

    def body(*refs):
        vals = fn(*[r[...] for r in refs[:n_in]])
        refs = refs[n_in + nd:]
        for r, v in zip(refs[:n_out], vals[:n_out]):
            r[...] = v.astype(r.dtype)
        i = pl.program_id(1)
        for r, v in zip(refs[n_out:], vals[n_out:]):
            @pl.when(i == 0)
            def _(r=r, v=v):
                r[...] = v

            @pl.when(i > 0)
            def _(r=r, v=v):
                r[...] += v

    res = pl.pallas_call(
        body, name=name, grid=(ncol, rows // tm), in_specs=in_specs + [_ANY] * nd, out_specs=out_specs,
        out_shape=out_shape, compiler_params=_params(("parallel", "arbitrary")),
    )(*[a for a, _, _ in ins], *deps)
    return res


def _colsum(v):
    return jnp.sum(v, axis=0, keepdims=True)


def _sigmoid(v):
    return jax.nn.sigmoid(v)


def _silu_and_grad(v):
    s = _sigmoid(v)
    return v * s, s * (1.0 + v * (1.0 - s))


def _rms_fwd(x, w, *, tm=512, deps=()):
    rows, d = x.shape
    nd = len(deps)

    def body(x_ref, w_ref, *rest):
        h_ref, r_ref = rest[nd:]
        xv = x_ref[...]
        r = lax.rsqrt(jnp.mean(xv * xv, axis=-1, keepdims=True) + NORM_EPS)
        h_ref[...] = (xv * r * w_ref[...]).astype(h_ref.dtype)
        r_ref[...] = r

    return pl.pallas_call(
        body, name="rms_fwd", grid=(rows // tm,),
        in_specs=[pl.BlockSpec((tm, d), lambda i: (i, 0)), pl.BlockSpec((1, d), lambda i: (0, 0))] + [_ANY] * nd,
        out_specs=[pl.BlockSpec((tm, d), lambda i: (i, 0)), pl.BlockSpec((tm, 1), lambda i: (i, 0))],
        out_shape=[jax.ShapeDtypeStruct((rows, d), _MXU), jax.ShapeDtypeStruct((rows, 1), F32)],
        compiler_params=_params(("arbitrary",)),
    )(x, w, *deps)


def _rms_bwd(dh, x, rstd, w, dout, *, tm=256):
    rows, d = x.shape

    def body(dh_ref, x_ref, r_ref, w_ref, do_ref, gx_ref, gw_ref):
        dhv, xv, r, wv = dh_ref[...], x_ref[...], r_ref[...], w_ref[...]
        xr = xv * r
        t = jnp.mean(dhv * wv * xr, axis=-1, keepdims=True)
        gx_ref[...] = do_ref[...] + r * (wv * dhv - xr * t)
        part = _colsum(dhv * xr)
        i = pl.program_id(0)

        @pl.when(i == 0)
        def _():
            gw_ref[...] = part

        @pl.when(i > 0)
        def _():
            gw_ref[...] += part

    return pl.pallas_call(
        body, name="rms_bwd", grid=(rows // tm,),
        in_specs=[pl.BlockSpec((tm, d), lambda i: (i, 0)), pl.BlockSpec((tm, d), lambda i: (i, 0)),
                  pl.BlockSpec((tm, 1), lambda i: (i, 0)), pl.BlockSpec((1, d), lambda i: (0, 0)),
                  pl.BlockSpec((tm, d), lambda i: (i, 0))],
        out_specs=[pl.BlockSpec((tm, d), lambda i: (i, 0)), pl.BlockSpec((1, d), lambda i: (0, 0))],
        out_shape=[jax.ShapeDtypeStruct((rows, d), F32), jax.ShapeDtypeStruct((1, d), F32)],
        compiler_params=_params(("arbitrary",)),
    )(dh, x, rstd, w, dout)


_NT = (((1,), (1,)), ((), ()))
_TN = (((0,), (0,)), ((), ()))


QKV_W = ATTN_W + 2 * KV_W
HEADS_PER_TILE = LANES // HEAD_DIM


def _low_half(rows):
    return lax.broadcasted_iota(jnp.int32, (rows, LANES), 1) < HEAD_DIM


def _pair_mean(t, low):
    m_lo = jnp.sum(jnp.where(low, t, 0.0), axis=-1, keepdims=True)
    m_hi = jnp.sum(jnp.where(low, 0.0, t), axis=-1, keepdims=True)
    return jnp.where(low, m_lo, m_hi) * (1.0 / HEAD_DIM)


def _pair_rstd(t, low):
    return lax.rsqrt(_pair_mean(t * t, low) + NORM_EPS)


def _dup_half(t, hi, low):
    swapped = pltpu.roll(t, HEAD_DIM, 1)
    return jnp.where(low, swapped, t) if hi else jnp.where(low, t, swapped)


def _fold_halves(t):
    return t + pltpu.roll(t, HEAD_DIM, 1)


def _split_heads(t, low):
    return [jnp.where(low, t, 0.0), jnp.where(low, 0.0, t)]


def _stacked_band_mask(n):
    rows = Q_PER_KV * WINDOW
    qi = lax.broadcasted_iota(jnp.int32, (rows, 2 * WINDOW), 0) % WINDOW + WINDOW
    kj = lax.broadcasted_iota(jnp.int32, (rows, 2 * WINDOW), 1)
    diff = qi - kj
    first_key = jnp.where(n > 0, 0, WINDOW)
    return (diff >= 0) & (diff < WINDOW) & (kj >= first_key)


def _stacked_sinks(sink_ref, g):
    blk = lax.broadcasted_iota(jnp.int32, (Q_PER_KV * WINDOW, 1), 0) // WINDOW
    col = jnp.full((Q_PER_KV * WINDOW, 1), sink_ref[Q_PER_KV * g], F32)
    for r in range(1, Q_PER_KV):
        col = jnp.where(blk == r, sink_ref[Q_PER_KV * g + r], col)
    return col


def _attn_in_specs(nblk, rev):
    def cur(n):
        return (nblk - 1 - n) if rev else n

    q_spec = pl.BlockSpec((WINDOW, ATTN_W), lambda n: (cur(n), 0))
    kvc_spec = pl.BlockSpec((WINDOW, 2 * KV_W), lambda n: (cur(n), ATTN_W // (2 * KV_W)))
    kvp_spec = pl.BlockSpec((WINDOW, 2 * KV_W), lambda n: (jnp.maximum(cur(n) - 1, 0), ATTN_W // (2 * KV_W)))
    w_spec = pl.BlockSpec((1, LANES), lambda n: (0, 0))
    l_spec = pl.BlockSpec((WINDOW, N_Q_HEADS), lambda n: (cur(n), 0))
    return q_spec, kvc_spec, kvp_spec, w_spec, l_spec


def _attn2_fwd(proj, qw2, kw2, sinks, deps=()):
    seq = proj.shape[0]
    nblk = seq // WINDOW
    scale = 1.0 / math.sqrt(HEAD_DIM)
    q_spec, kvc_spec, kvp_spec, w_spec, l_spec = _attn_in_specs(nblk, False)
    nd = len(deps)

    def body(sink_ref, q_ref, kvc_ref, kvp_ref, qw_ref, kw_ref, *rest):
        o_ref, lse_ref = rest[nd:]
        n = pl.program_id(0)
        low, low2 = _low_half(WINDOW), _low_half(2 * WINDOW)
        valid = _stacked_band_mask(n)
        head_lane = lax.broadcasted_iota(jnp.int32, (WINDOW, N_Q_HEADS), 1)
        kv = jnp.concatenate([kvp_ref[...], kvc_ref[...]], axis=0)
        qwv, kwv = qw_ref[...], kw_ref[...]
        lse_blk = jnp.zeros((WINDOW, N_Q_HEADS), F32)
        for t in range(N_KV_HEADS // HEADS_PER_TILE):
            kt = kv[:, t * LANES:(t + 1) * LANES]
            vt = kv[:, KV_W + t * LANES:KV_W + (t + 1) * LANES]
            kn = kt * _pair_rstd(kt, low2) * kwv
            for hi in range(HEADS_PER_TILE):
                g = HEADS_PER_TILE * t + hi
                kdup = _dup_half(kn, hi, low2).astype(_MXU)
                vdup = _dup_half(vt, hi, low2).astype(_MXU)
                stack = []
                for tq in (2 * g, 2 * g + 1):
                    qt = q_ref[:, tq * LANES:(tq + 1) * LANES]
                    stack += _split_heads(qt * _pair_rstd(qt, low) * qwv, low)
                qs = jnp.concatenate(stack, axis=0).astype(_MXU)
                s = lax.dot_general(qs, kdup, _NT, preferred_element_type=F32) * scale
                s = jnp.where(valid, s, -1e30)
                sink = _stacked_sinks(sink_ref, g)
                m = jnp.maximum(jnp.max(s, axis=-1, keepdims=True), sink)
                e = jnp.exp(s - m)
                z = jnp.sum(e, axis=-1, keepdims=True) + jnp.exp(sink - m)
                o = jnp.dot((e / z).astype(_MXU), vdup, preferred_element_type=F32)
                for i, tq in enumerate((2 * g, 2 * g + 1)):
                    o_ref[:, tq * LANES:(tq + 1) * LANES] = jnp.where(
                        low, o[2 * i * WINDOW:(2 * i + 1) * WINDOW], o[(2 * i + 1) * WINDOW:(2 * i + 2) * WINDOW])
                lse = m + jnp.log(z)
                for r in range(Q_PER_KV):
                    lse_blk = jnp.where(head_lane == Q_PER_KV * g + r, lse[r * WINDOW:(r + 1) * WINDOW], lse_blk)
        lse_ref[...] = lse_blk

    return pl.pallas_call(
        body, name="attn_fwd", grid=(nblk,),
        in_specs=[pl.BlockSpec(memory_space=pltpu.SMEM), q_spec, kvc_spec, kvp_spec, w_spec, w_spec] + [_ANY] * nd,
        out_specs=[q_spec, l_spec],
        out_shape=[jax.ShapeDtypeStruct((seq, ATTN_W), F32), jax.ShapeDtypeStruct((seq, N_Q_HEADS), F32)],
        compiler_params=_params(("arbitrary",)),
    )(sinks, proj, proj, proj, qw2, kw2, *deps)


def _attn2_bwd(proj, qw2, kw2, sinks, lse, do, deps=()):
    seq = proj.shape[0]
    nblk = seq // WINDOW
    scale = 1.0 / math.sqrt(HEAD_DIM)
    q_spec, kvc_spec, kvp_spec, w_spec, l_spec = _attn_in_specs(nblk, True)
    s_spec = pl.BlockSpec((1, N_Q_HEADS), lambda n: (0, 0))
    d_spec = pl.BlockSpec((WINDOW, QKV_W), lambda n: (nblk - 1 - n, 0))
    nd = len(deps)

    def body(sink_ref, q_ref, kvc_ref, kvp_ref, qw_ref, kw_ref, lse_ref, do_ref, *rest):
        d_ref, dqw_ref, dkw_ref, dsk_ref, carry = rest[nd:]
        step = pl.program_id(0)
        n = nblk - 1 - step

        @pl.when(step == 0)
        def _():
            carry[...] = jnp.zeros_like(carry)
            dqw_ref[...] = jnp.zeros_like(dqw_ref)
            dkw_ref[...] = jnp.zeros_like(dkw_ref)
            dsk_ref[...] = jnp.zeros_like(dsk_ref)

        low, low2 = _low_half(WINDOW), _low_half(2 * WINDOW)
        valid = _stacked_band_mask(n)
        head_lane = lax.broadcasted_iota(jnp.int32, (WINDOW, N_Q_HEADS), 1)
        sink_lane = lax.broadcasted_iota(jnp.int32, (1, N_Q_HEADS), 1)
        kv = jnp.concatenate([kvp_ref[...], kvc_ref[...]], axis=0)
        qwv, kwv = qw_ref[...], kw_ref[...]
        lse_blk = lse_ref[...]
        dqw = jnp.zeros((1, LANES), F32)
        dkw = jnp.zeros((1, LANES), F32)
        dsk = jnp.zeros((1, N_Q_HEADS), F32)
        for t in range(N_KV_HEADS // HEADS_PER_TILE):
            kt = kv[:, t * LANES:(t + 1) * LANES]
            vt = kv[:, KV_W + t * LANES:KV_W + (t + 1) * LANES]
            rk = _pair_rstd(kt, low2)
            kn = kt * rk * kwv
            dkn_t = jnp.zeros((2 * WINDOW, LANES), F32)
            dv_t = jnp.zeros((2 * WINDOW, LANES), F32)
            for hi in range(HEADS_PER_TILE):
                g = HEADS_PER_TILE * t + hi
                kdup = _dup_half(kn, hi, low2).astype(_MXU)
                vdup = _dup_half(vt, hi, low2).astype(_MXU)
                tiles = (2 * g, 2 * g + 1)
                qx, rq, stack, dstack, lse_rows = [], [], [], [], []
                for tq in tiles:
                    qt = q_ref[:, tq * LANES:(tq + 1) * LANES]
                    r = _pair_rstd(qt, low)
                    rq.append(r)
                    qx.append(qt * r)
                    stack += _split_heads(qx[-1] * qwv, low)
                    dstack += _split_heads(do_ref[:, tq * LANES:(tq + 1) * LANES], low)
                for r in range(Q_PER_KV):
                    lse_rows.append(jnp.sum(jnp.where(head_lane == Q_PER_KV * g + r, lse_blk, 0.0), axis=-1, keepdims=True))
                qs = jnp.concatenate(stack, axis=0).astype(_MXU)
                dos = jnp.concatenate(dstack, axis=0).astype(_MXU)
                lse_col = jnp.concatenate(lse_rows, axis=0)
                s = lax.dot_general(qs, kdup, _NT, preferred_element_type=F32) * scale
                s = jnp.where(valid, s, -1e30)
                p = jnp.exp(s - lse_col)
                dp = lax.dot_general(dos, vdup, _NT, preferred_element_type=F32)
                dsum = jnp.sum(p * dp, axis=-1, keepdims=True)
                ds = (p * (dp - dsum) * scale).astype(_MXU)
                dsink = -jnp.exp(_stacked_sinks(sink_ref, g) - lse_col) * dsum
                for r in range(Q_PER_KV):
                    dsk = dsk + jnp.where(sink_lane == Q_PER_KV * g + r, _colsum(dsink[r * WINDOW:(r + 1) * WINDOW]), 0.0)
                dv_g = _fold_halves(lax.dot_general(p.astype(_MXU), dos, _TN, preferred_element_type=F32))
                dkn_g = _fold_halves(lax.dot_general(ds, qs, _TN, preferred_element_type=F32))
                dv_t = jnp.where(low2, dv_t, dv_g) if hi else jnp.where(low2, dv_g, dv_t)
                dkn_t = jnp.where(low2, dkn_t, dkn_g) if hi else jnp.where(low2, dkn_g, dkn_t)
                dqn = jnp.dot(ds, kdup, preferred_element_type=F32)
                for i, tq in enumerate(tiles):
                    dqn_t = jnp.where(low, dqn[2 * i * WINDOW:(2 * i + 1) * WINDOW],
                                      dqn[(2 * i + 1) * WINDOW:(2 * i + 2) * WINDOW])
                    dq = rq[i] * (qwv * dqn_t - qx[i] * _pair_mean(dqn_t * qwv * qx[i], low))
                    d_ref[:, tq * LANES:(tq + 1) * LANES] = dq.astype(d_ref.dtype)
                    dqw = dqw + _colsum(dqn_t * qx[i])
            k_cols = slice(t * LANES, (t + 1) * LANES)
            v_cols = slice(KV_W + t * LANES, KV_W + (t + 1) * LANES)
            dkn_c = dkn_t[WINDOW:] + carry[:, k_cols]
            rc = rk[WINDOW:]
            kx = kt[WINDOW:] * rc
            dk = rc * (kwv * dkn_c - kx * _pair_mean(dkn_c * kwv * kx, low))
            d_ref[:, ATTN_W + t * LANES:ATTN_W + (t + 1) * LANES] = dk.astype(d_ref.dtype)
            d_ref[:, ATTN_W + KV_W + t * LANES:ATTN_W + KV_W + (t + 1) * LANES] = (
                dv_t[WINDOW:] + carry[:, v_cols]).astype(d_ref.dtype)
            carry[:, k_cols] = dkn_t[:WINDOW]
            carry[:, v_cols] = dv_t[:WINDOW]
            dkw = dkw + _colsum(dkn_c * kx)
        dqw_ref[...] += dqw
        dkw_ref[...] += dkw
        dsk_ref[...] += dsk

    return pl.pallas_call(
        body, name="attn_bwd", grid=(nblk,),
        in_specs=[pl.BlockSpec(memory_space=pltpu.SMEM), q_spec, kvc_spec, kvp_spec, w_spec, w_spec, l_spec, q_spec]
        + [_ANY] * nd,
        out_specs=[d_spec, w_spec, w_spec, s_spec],
        out_shape=[jax.ShapeDtypeStruct((seq, QKV_W), _MXU), jax.ShapeDtypeStruct((1, LANES), F32),
                   jax.ShapeDtypeStruct((1, LANES), F32), jax.ShapeDtypeStruct((1, N_Q_HEADS), F32)],
        scratch_shapes=[pltpu.VMEM((WINDOW, 2 * KV_W), F32)],
        compiler_params=_params(("arbitrary",)),
    )(sinks, proj, proj, proj, qw2, kw2, lse, do, *deps)


def _ssm_discretise(a_re, a_im, log_dt):
    dt = jnp.exp(log_dt)
    mag = jnp.exp(dt * a_re)
    ab_re = mag * jnp.cos(dt * a_im)
    ab_im = mag * jnp.sin(dt * a_im)
    num_re = ab_re - 1.0
    num_im = ab_im
    den = a_re * a_re + a_im * a_im
    cf_re = (num_re * a_re + num_im * a_im) / den
    cf_im = (num_im * a_re - num_re * a_im) / den
    return ab_re, ab_im, cf_re, cf_im


def _ssm_params_fwd(a_re, a_im, log_dt):
    shp = jax.ShapeDtypeStruct(a_re.shape, F32)

    def body(are_ref, aim_ref, ldt_ref, abr_ref, abi_ref, cfr_ref, cfi_ref, alr_ref, ali_ref):
        abr, abi, cfr, cfi = _ssm_discretise(are_ref[...], aim_ref[...], ldt_ref[...])
        abr_ref[...], abi_ref[...], cfr_ref[...], cfi_ref[...] = abr, abi, cfr, cfi
        pr, pi = abr, abi
        for _ in range(int(math.log2(SSM_L))):
            pr, pi = pr * pr - pi * pi, 2.0 * pr * pi
        alr_ref[...], ali_ref[...] = pr, pi

    return pl.pallas_call(body, name="ssm_params_fwd", out_shape=[shp] * 6)(a_re, a_im, log_dt)


def _ssm_params_bwd(a_re, a_im, log_dt, d_abr, d_abi, d_cfr, d_cfi):
    def body(are_ref, aim_ref, ldt_ref, g0, g1, g2, g3, dare_ref, daim_ref, dldt_ref):
        _, vjp = jax.vjp(_ssm_discretise, are_ref[...], aim_ref[...], ldt_ref[...])
        dare_ref[...], daim_ref[...], dldt_ref[...] = vjp((g0[...], g1[...], g2[...], g3[...]))

    return pl.pallas_call(
        body, name="ssm_params_bwd",
        out_shape=[jax.ShapeDtypeStruct(a_re.shape, F32), jax.ShapeDtypeStruct(a_im.shape, F32),
                   jax.ShapeDtypeStruct(log_dt.shape, F32)],
    )(a_re, a_im, log_dt, d_abr, d_abi, d_cfr, d_cfi)


def _scan_cols(j):
    return pl.ds(j * SSM_SB, SSM_SB)


def _rows8(r):
    return pl.ds(pl.multiple_of(r * SUBLANES, SUBLANES), SUBLANES)


def _bcast8(row):
    return jnp.broadcast_to(row, (SUBLANES, row.shape[-1]))


def _ssm_fwd(u, b_re, b_im, c_re, c_im, d_skip, coef):
    seq = u.shape[0]
    nc = seq // SSM_T
    T, L = SSM_T, SSM_L

    def body(u_ref, bre_ref, bim_ref, cre_ref, cim_ref, d_ref, are_ref, aim_ref, cfr_ref, cfi_ref, alr_ref, ali_ref,
             y_ref, sre_ref, sim_ref, ire_ref, iim_ref, car_re, car_im, end_re, end_im):
        c = pl.program_id(0)

        @pl.when(c == 0)
        def _():
            car_re[...] = jnp.zeros_like(car_re)
            car_im[...] = jnp.zeros_like(car_im)

        for j in range(SSM_JB):
            ub = u_ref[:, j * LANES:(j + 1) * LANES].astype(_MXU)
            bur = jnp.dot(ub, bre_ref[j], preferred_element_type=F32)
            bui = jnp.dot(ub, bim_ref[j], preferred_element_type=F32)
            cfr, cfi = cfr_ref[:, _scan_cols(j)], cfi_ref[:, _scan_cols(j)]
            sre_ref[:, _scan_cols(j)] = cfr * bur - cfi * bui
            sim_ref[:, _scan_cols(j)] = cfr * bui + cfi * bur

        for j in range(SSM_JB):
            cols = _scan_cols(j)
            ar, ai = _bcast8(are_ref[:, cols]), _bcast8(aim_ref[:, cols])

            def step1(r, s, cols=cols, ar=ar, ai=ai):
                sr, si = s
                rows = _rows8(r)
                return (ar * sr - ai * si + sre_ref[rows, cols], ar * si + ai * sr + sim_ref[rows, cols])

            zero = jnp.zeros((SUBLANES, SSM_SB), F32)
            er, ei = lax.fori_loop(0, L, step1, (zero, zero), unroll=4)
            end_re[:, cols] = er
            end_im[:, cols] = ei

        alr, ali = alr_ref[...], ali_ref[...]
        cr, ci = car_re[...], car_im[...]
        ire_ref[0:1, :] = cr
        iim_ref[0:1, :] = ci
        for i in range(1, SUBLANES):
            er, ei = end_re[i - 1:i, :], end_im[i - 1:i, :]
            cr, ci = alr * cr - ali * ci + er, alr * ci + ali * cr + ei
            ire_ref[i:i + 1, :] = cr
            iim_ref[i:i + 1, :] = ci

        for j in range(SSM_JB):
            cols = _scan_cols(j)
            ar, ai = _bcast8(are_ref[:, cols]), _bcast8(aim_ref[:, cols])

            def step2(r, s, cols=cols, ar=ar, ai=ai):
                sr, si = s
                rows = _rows8(r)
                nr = ar * sr - ai * si + sre_ref[rows, cols]
                ni = ar * si + ai * sr + sim_ref[rows, cols]
                sre_ref[rows, cols] = nr
                sim_ref[rows, cols] = ni
                return nr, ni

            lax.fori_loop(0, L, step2, (ire_ref[:, cols], iim_ref[:, cols]), unroll=4)

        car_re[...] = sre_ref[T - 1:T, :]
        car_im[...] = sim_ref[T - 1:T, :]

        for j in range(SSM_JB):
            cols = _scan_cols(j)
            ch = slice(j * LANES, (j + 1) * LANES)
            y = (jnp.dot(sre_ref[:, cols].astype(_MXU), cre_ref[j], preferred_element_type=F32)
                 - jnp.dot(sim_ref[:, cols].astype(_MXU), cim_ref[j], preferred_element_type=F32))
            y_ref[:, ch] = y + d_ref[:, ch] * u_ref[:, ch]

    tok = pl.BlockSpec((T, SSM_W), lambda c: (c, 0))
    st = pl.BlockSpec((T, N_STATES), lambda c: (c, 0))
    ini = pl.BlockSpec((None, SUBLANES, N_STATES), lambda c: (c, 0, 0))
    bsp = pl.BlockSpec((SSM_JB, LANES, SSM_SB), lambda c: (0, 0, 0))
    csp = pl.BlockSpec((SSM_JB, SSM_SB, LANES), lambda c: (0, 0, 0))
    row_w = pl.BlockSpec((1, SSM_W), lambda c: (0, 0))
    row_s = pl.BlockSpec((1, N_STATES), lambda c: (0, 0))
    return pl.pallas_call(
        body, name="ssm_fwd", grid=(nc,),
        in_specs=[tok, bsp, bsp, csp, csp, row_w] + [row_s] * 6,
        out_specs=[tok, st, st, ini, ini],
        out_shape=[jax.ShapeDtypeStruct((seq, SSM_W), F32),
                   jax.ShapeDtypeStruct((seq, N_STATES), F32), jax.ShapeDtypeStruct((seq, N_STATES), F32),
                   jax.ShapeDtypeStruct((nc, SUBLANES, N_STATES), F32),
                   jax.ShapeDtypeStruct((nc, SUBLANES, N_STATES), F32)],
        scratch_shapes=[pltpu.VMEM((1, N_STATES), F32), pltpu.VMEM((1, N_STATES), F32),
                        pltpu.VMEM((SUBLANES, N_STATES), F32), pltpu.VMEM((SUBLANES, N_STATES), F32)],
        compiler_params=_params(("arbitrary",)),
    )(u, b_re, b_im, c_re, c_im, d_skip, *coef)


def _ssm_bwd(dy, u, s_re, s_im, i_re, i_im, b_re, b_im, c_re, c_im, d_skip, coef):
    seq = u.shape[0]
    nc = seq // SSM_T
    T, L = SSM_T, SSM_L

    def body(dy_ref, u_ref, sre_ref, sim_ref, ire_ref, iim_ref, bre_ref, bim_ref, cre_ref, cim_ref, d_ref,
             are_ref, aim_ref, cfr_ref, cfi_ref, alr_ref, ali_ref,
             du_ref, dbre_out, dbim_out, dcre_out, dcim_out, dd_ref, dar_ref, dai_ref, dcfr_ref, dcfi_ref,
             lre, lim, car_re, car_im, end_re, end_im, ini_re, ini_im, dbre_ref, dbim_ref, dcre_ref, dcim_ref):
        step = pl.program_id(0)

        @pl.when(step == 0)
        def _():
            car_re[...] = jnp.zeros_like(car_re)
            car_im[...] = jnp.zeros_like(car_im)
            for ref in (dbre_ref, dbim_ref, dcre_ref, dcim_ref, dd_ref, dar_ref, dai_ref, dcfr_ref, dcfi_ref):
                ref[...] = jnp.zeros_like(ref)

        for j in range(SSM_JB):
            dyb = dy_ref[:, j * LANES:(j + 1) * LANES].astype(_MXU)
            lre[:, _scan_cols(j)] = lax.dot_general(dyb, cre_ref[j], _NT, preferred_element_type=F32)
            lim[:, _scan_cols(j)] = -lax.dot_general(dyb, cim_ref[j], _NT, preferred_element_type=F32)

        for j in range(SSM_JB):
            cols = _scan_cols(j)
            ar, ai = _bcast8(are_ref[:, cols]), _bcast8(aim_ref[:, cols])

            def step1(t, s, cols=cols, ar=ar, ai=ai):
                sr, si = s
                rows = _rows8(L - 1 - t)
                return (ar * sr + ai * si + lre[rows, cols], ar * si - ai * sr + lim[rows, cols])

            zero = jnp.zeros((SUBLANES, SSM_SB), F32)
            er, ei = lax.fori_loop(0, L, step1, (zero, zero), unroll=4)
            end_re[:, cols] = er
            end_im[:, cols] = ei

        alr, ali = alr_ref[...], ali_ref[...]
        cr, ci = car_re[...], car_im[...]
        ini_re[SUBLANES - 1:SUBLANES, :] = cr
        ini_im[SUBLANES - 1:SUBLANES, :] = ci
        for i in range(SUBLANES - 2, -1, -1):
            er, ei = end_re[i + 1:i + 2, :], end_im[i + 1:i + 2, :]
            cr, ci = alr * cr + ali * ci + er, alr * ci - ali * cr + ei
            ini_re[i:i + 1, :] = cr
            ini_im[i:i + 1, :] = ci

        for j in range(SSM_JB):
            cols = _scan_cols(j)
            ar, ai = _bcast8(are_ref[:, cols]), _bcast8(aim_ref[:, cols])

            def step2(t, s, cols=cols, ar=ar, ai=ai):
                sr, si = s
                rows = _rows8(L - 1 - t)
                nr = ar * sr + ai * si + lre[rows, cols]
                ni = ar * si - ai * sr + lim[rows, cols]
                lre[rows, cols] = nr
                lim[rows, cols] = ni
                return nr, ni

            lax.fori_loop(0, L, step2, (ini_re[:, cols], ini_im[:, cols]), unroll=4)

        car_re[...] = lre[0:1, :]
        car_im[...] = lim[0:1, :]

        head, tail, body_rows = slice(0, SUBLANES), slice(SUBLANES, T), slice(0, T - SUBLANES)
        for j in range(SSM_JB):
            cols = _scan_cols(j)
            ch = slice(j * LANES, (j + 1) * LANES)
            lr, li = lre[:, cols], lim[:, cols]
            dar_ref[:, cols] += (_colsum(lre[tail, cols] * sre_ref[body_rows, cols] + lim[tail, cols] * sim_ref[body_rows, cols])
                                 + _colsum(lre[head, cols] * ire_ref[:, cols] + lim[head, cols] * iim_ref[:, cols]))
            dai_ref[:, cols] += (_colsum(lim[tail, cols] * sre_ref[body_rows, cols] - lre[tail, cols] * sim_ref[body_rows, cols])
                                 + _colsum(lim[head, cols] * ire_ref[:, cols] - lre[head, cols] * iim_ref[:, cols]))
            uf = u_ref[:, ch]
            ub = uf.astype(_MXU)
            bur = jnp.dot(ub, bre_ref[j], preferred_element_type=F32)
            bui = jnp.dot(ub, bim_ref[j], preferred_element_type=F32)
            dcfr_ref[:, cols] += _colsum(lr * bur + li * bui)
            dcfi_ref[:, cols] += _colsum(li * bur - lr * bui)
            cfr, cfi = cfr_ref[:, cols], cfi_ref[:, cols]
            dbur = (cfr * lr + cfi * li).astype(_MXU)
            dbui = (cfr * li - cfi * lr).astype(_MXU)
            dyf = dy_ref[:, ch]
            dyb = dyf.astype(_MXU)
            du_ref[:, ch] = (lax.dot_general(dbur, bre_ref[j], _NT, preferred_element_type=F32)
                             + lax.dot_general(dbui, bim_ref[j], _NT, preferred_element_type=F32)
                             + d_ref[:, ch] * dyf)
            dbre_ref[j] += lax.dot_general(ub, dbur, _TN, preferred_element_type=F32)
            dbim_ref[j] += lax.dot_general(ub, dbui, _TN, preferred_element_type=F32)
            dcre_ref[j] += lax.dot_general(sre_ref[:, cols].astype(_MXU), dyb, _TN, preferred_element_type=F32)
            dcim_ref[j] -= lax.dot_general(sim_ref[:, cols].astype(_MXU), dyb, _TN, preferred_element_type=F32)
            dd_ref[:, ch] += _colsum(dyf * uf)

        @pl.when(step == nc - 1)
        def _():
            for acc, out in ((dbre_ref, dbre_out), (dbim_ref, dbim_out), (dcre_ref, dcre_out), (dcim_ref, dcim_out)):
                pltpu.sync_copy(acc, out)

    tok = pl.BlockSpec((T, SSM_W), lambda c: (nc - 1 - c, 0))
    st = pl.BlockSpec((T, N_STATES), lambda c: (nc - 1 - c, 0))
    ini = pl.BlockSpec((None, SUBLANES, N_STATES), lambda c: (nc - 1 - c, 0, 0))
    bsp = pl.BlockSpec((SSM_JB, LANES, SSM_SB), lambda c: (0, 0, 0))
    csp = pl.BlockSpec((SSM_JB, SSM_SB, LANES), lambda c: (0, 0, 0))
    row_w = pl.BlockSpec((1, SSM_W), lambda c: (0, 0))
    row_s = pl.BlockSpec((1, N_STATES), lambda c: (0, 0))
    big = pltpu.VMEM((T, N_STATES), F32)
    one = pltpu.VMEM((1, N_STATES), F32)
    eight = pltpu.VMEM((SUBLANES, N_STATES), F32)
    return pl.pallas_call(
        body, name="ssm_bwd", grid=(nc,),
        in_specs=[tok, tok, st, st, ini, ini, bsp, bsp, csp, csp, row_w] + [row_s] * 6,
        out_specs=[tok, _ANY, _ANY, _ANY, _ANY, row_w, row_s, row_s, row_s, row_s],
        out_shape=[jax.ShapeDtypeStruct((seq, SSM_W), F32),
                   jax.ShapeDtypeStruct((SSM_JB, LANES, SSM_SB), F32), jax.ShapeDtypeStruct((SSM_JB, LANES, SSM_SB), F32),
                   jax.ShapeDtypeStruct((SSM_JB, SSM_SB, LANES), F32), jax.ShapeDtypeStruct((SSM_JB, SSM_SB, LANES), F32),
                   jax.ShapeDtypeStruct((1, SSM_W), F32)] + [jax.ShapeDtypeStruct((1, N_STATES), F32)] * 4,
        scratch_shapes=[big, big, one, one, eight, eight, eight, eight,
                        pltpu.VMEM((SSM_JB, LANES, SSM_SB), F32), pltpu.VMEM((SSM_JB, LANES, SSM_SB), F32),
                        pltpu.VMEM((SSM_JB, SSM_SB, LANES), F32), pltpu.VMEM((SSM_JB, SSM_SB, LANES), F32)],
        compiler_params=_params(("arbitrary",)),
    )(dy, u, s_re, s_im, i_re, i_im, b_re, b_im, c_re, c_im, d_skip, *coef)


def _block_diag_b(b):
    t = b.reshape(SSM_JB, 8, STATE, GROUP).transpose(0, 1, 3, 2)
    eye = jnp.eye(8, dtype=b.dtype)
    return (t[:, :, :, None, :] * eye[None, :, None, :, None]).reshape(SSM_JB, LANES, SSM_SB)


def _block_diag_c(c):
    t = c.reshape(SSM_JB, 8, GROUP, STATE).transpose(0, 1, 3, 2)
    eye = jnp.eye(8, dtype=c.dtype)
    return (t[:, :, :, None, :] * eye[None, :, None, :, None]).reshape(SSM_JB, SSM_SB, LANES)


def _diag_of_b(blk):
    t = blk.reshape(SSM_JB, 8, GROUP, 8, STATE)
    d = jnp.stack([t[:, i, :, i, :] for i in range(8)], axis=1)
    return d.transpose(0, 1, 3, 2).reshape(N_GROUPS, STATE, GROUP)


def _diag_of_c(blk):
    t = blk.reshape(SSM_JB, 8, STATE, 8, GROUP)
    d = jnp.stack([t[:, i, :, i, :] for i in range(8)], axis=1)
    return d.transpose(0, 1, 3, 2).reshape(N_GROUPS, GROUP, STATE)


def _to_scan_order(v):
    seq, w = v.shape
    return v.reshape(seq // SSM_T, SUBLANES, SSM_L, w).transpose(0, 2, 1, 3).reshape(seq, w)


def _from_scan_order(v):
    seq, w = v.shape
    return v.reshape(seq // SSM_T, SSM_L, SUBLANES, w).transpose(0, 2, 1, 3).reshape(seq, w)


def _adamw_math(w, g, m, v):
    nm = ADAM_B1 * m + (1.0 - ADAM_B1) * g
    nv = ADAM_B2 * v + (1.0 - ADAM_B2) * jnp.square(g)
    m_hat = nm / (1.0 - ADAM_B1 ** ADAM_STEP)
    v_hat = nv / (1.0 - ADAM_B2 ** ADAM_STEP)
    return -ADAM_LR * (m_hat / (jnp.sqrt(v_hat) + ADAM_EPS) + ADAM_WD * w), nm, nv


def _adamw(w, g, m, v, *, name, tm, deps=()):
    rows, cols = w.shape
    nd = len(deps)

    def body(w_ref, g_ref, m_ref, v_ref, *rest):
        d_ref, nm_ref, nv_ref = rest[nd:]
        d_ref[...], nm_ref[...], nv_ref[...] = _adamw_math(w_ref[...], g_ref[...], m_ref[...], v_ref[...])

    spec = pl.BlockSpec((tm, cols), lambda i: (i, 0))
    shp = jax.ShapeDtypeStruct((rows, cols), F32)
    return pl.pallas_call(body, name=name, grid=(rows // tm,), in_specs=[spec] * 4 + [_ANY] * nd,
                          out_specs=[spec] * 3, out_shape=[shp] * 3,
                          compiler_params=_params(("arbitrary",)))(w, g, m, v, *deps)


def _place():
    x, y, c = lax.axis_index("x"), lax.axis_index("y"), lax.axis_index("c")
    chips = [(1 - x, y), (x, 1 - y), (1 - x, 1 - y)]
    return x, y, c, chips


def _remote(src, dst, send_sem, recv_sem, dev):
    return pltpu.make_async_remote_copy(src_ref=src, dst_ref=dst, send_sem=send_sem, recv_sem=recv_sem,
                                        device_id=dev, device_id_type=MESH)


def _place_shard(w, mine_arr, *, name, tm=256):
    rows, cols = w.shape

    def body(m_ref, w_ref, o_ref):
        o_ref[...] = w_ref[...].astype(o_ref.dtype)

    return pl.pallas_call(
        body, name=name,
        grid_spec=pltpu.PrefetchScalarGridSpec(
            num_scalar_prefetch=1, grid=(rows // tm,),
            in_specs=[pl.BlockSpec((tm, cols), lambda i, m: (i, 0))],
            out_specs=pl.BlockSpec((None, tm, cols), lambda i, m: (m[0], i, 0))),
        out_shape=jax.ShapeDtypeStruct((N_CHIPS, rows, cols), _WIRE),
        compiler_params=_params(("arbitrary",)),
    )(mine_arr, w)


_HBM = pl.BlockSpec(memory_space=pltpu.HBM)
_SEM = pl.BlockSpec(memory_space=pltpu.SEMAPHORE)
_EFFECT = pltpu.SideEffectType.DATAFLOW_SIDE_EFFECTING


def _copies_start(name, bufs, plan, count, after=()):
    nb, na = len(bufs), len(after)

    def body(*refs):
        send_sems, recv_sems, token = refs[nb + na], refs[nb + na + 1], refs[-1]
        copies = plan(refs[:nb])
        assert len(copies) == count
        for i, (src, dst, dev, _) in enumerate(copies):
            _remote(src, dst, send_sems.at[i], recv_sems.at[i], dev).start()
        token[...] = jnp.zeros_like(token)

    res = pl.pallas_call(
        body, name=name, in_specs=[_HBM] * nb + [_ANY] * na,
        out_specs=(_SEM, _SEM, *[_HBM] * nb, pl.BlockSpec(memory_space=pltpu.VMEM)),
        out_shape=(pltpu.SemaphoreType.DMA((count,)), pltpu.SemaphoreType.DMA((count,)),
                   *[pltpu.HBM(b.shape, b.dtype) for b in bufs], jax.ShapeDtypeStruct((SUBLANES, LANES), F32)),
        input_output_aliases={i: 2 + i for i in range(nb)},
        compiler_params=pltpu.CompilerParams(has_side_effects=_EFFECT),
    )(*[pltpu.with_memory_space_constraint(b, pltpu.HBM) for b in bufs], *after)
    return (res[0], res[1]), list(res[2:2 + nb]), res[-1]


def _copies_wait(name, bufs, sems, plan, after=()):
    nb, na = len(bufs), len(after)

    def body(*refs):
        send_sems, recv_sems = refs[nb], refs[nb + 1]
        for i, (src, _, dev, land) in enumerate(plan(refs[:nb])):
            cp = _remote(src, land, send_sems.at[i], recv_sems.at[i], dev)
            cp.wait_send()
            cp.wait_recv()

    res = pl.pallas_call(
        body, name=name, in_specs=[_HBM] * nb + [_SEM, _SEM] + [_ANY] * na, out_specs=[_HBM] * nb,
        out_shape=[pltpu.HBM(b.shape, b.dtype) for b in bufs],
        input_output_aliases={i: i for i in range(nb)},
        compiler_params=pltpu.CompilerParams(has_side_effects=_EFFECT),
    )(*bufs, *sems, *after)
    return list(res)


def _plan_gather_ici(fulls):
    x, y, c, chips = _place()
    copies = []
    for f in fulls:
        half = pl.ds(c * (f.shape[1] // 2), f.shape[1] // 2)
        own = f.at[2 * x + y, half]
        for chip in chips:
            copies.append((own, own, (*chip, c), f.at[2 * chip[0] + chip[1], half]))
    return copies


def _plan_gather_d2d(fulls):
    x, y, c, chips = _place()
    copies = []
    for f in fulls:
        r2 = f.shape[1] // 2
        for chip in chips:
            blk = 2 * chip[0] + chip[1]
            landed = f.at[blk, pl.ds(c * r2, r2)]
            copies.append((landed, landed, (x, y, 1 - c), f.at[blk, pl.ds((1 - c) * r2, r2)]))
    return copies


def _plan_swap_halves(refs):
    x, y, c, _ = _place()
    n = len(refs) // 2
    copies = []
    for g, land in zip(refs[:n], refs[n:]):
        r2 = g.shape[1] // 2
        copies.append((g.at[:, pl.ds((1 - c) * r2, r2), :], land, (x, y, 1 - c), land))
    return copies


def _plan_scatter_chips(refs):
    x, y, c, chips = _place()
    n = len(refs) // 2
    copies = []
    for h, land in zip(refs[:n], refs[n:]):
        for k, chip in enumerate(chips):
            copies.append((h.at[2 * chip[0] + chip[1]], land.at[k], (*chip, c), land.at[k]))
    return copies


def _plan_join_halves(totals):
    x, y, c, _ = _place()
    copies = []
    for t in totals:
        r2 = t.shape[0] // 2
        mine = t.at[pl.ds(c * r2, r2)]
        copies.append((mine, mine, (x, y, 1 - c), t.at[pl.ds((1 - c) * r2, r2)]))
    return copies


def _add_sibling_half(g, got, c_arr, *, name, tm):
    _, rows, cols = g.shape
    r2 = rows // 2
    nb = r2 // tm

    def body(c_ref, g_ref, r_ref, o_ref):
        o_ref[...] = (g_ref[...].astype(F32) + r_ref[...].astype(F32)).astype(o_ref.dtype)

    return pl.pallas_call(
        body, name=name,
        grid_spec=pltpu.PrefetchScalarGridSpec(
            num_scalar_prefetch=1, grid=(N_CHIPS, nb),
            in_specs=[pl.BlockSpec((None, tm, cols), lambda b, i, c: (b, c[0] * nb + i, 0)),
                      pl.BlockSpec((None, tm, cols), lambda b, i, c: (b, i, 0))],
            out_specs=pl.BlockSpec((None, tm, cols), lambda b, i, c: (b, i, 0))),
        out_shape=jax.ShapeDtypeStruct((N_CHIPS, r2, cols), _WIRE),
        compiler_params=_params(("arbitrary", "arbitrary")),
    )(c_arr, g, got)


def _add_chips(h, got, place_arr, *, name, tm):
    _, r2, cols = h.shape
    nb = r2 // tm

    def body(p_ref, h_ref, r_ref, o_ref):
        o_ref[...] = ((h_ref[...].astype(F32) + r_ref[0].astype(F32)) + r_ref[1].astype(F32)) + r_ref[2].astype(F32)

    return pl.pallas_call(
        body, name=name,
        grid_spec=pltpu.PrefetchScalarGridSpec(
            num_scalar_prefetch=1, grid=(nb,),
            in_specs=[pl.BlockSpec((None, tm, cols), lambda i, p: (p[0], i, 0)),
                      pl.BlockSpec((3, tm, cols), lambda i, p: (0, i, 0))],
            out_specs=pl.BlockSpec((tm, cols), lambda i, p: (p[1] * nb + i, 0))),
        out_shape=jax.ShapeDtypeStruct((2 * r2, cols), F32),
        compiler_params=_params(("arbitrary",)),
    )(place_arr, h, got)


class _ReduceScatter:
    def __init__(self, tag, names, grads):
        self.tag, self.names, self.n = tag, names, len(names)
        core = lax.axis_index("c").astype(jnp.int32)
        chip = (2 * lax.axis_index("x") + lax.axis_index("y")).astype(jnp.int32)
        self.c_arr, self.place_arr = core.reshape(1), jnp.stack([chip, core])
        self.bufs = list(grads)

    def _start(self, step, bufs, plan, count, after):
        self.plan = plan
        self.step = f"grad_{step}_{self.tag}"
        self.sems, self.bufs, token = _copies_start(self.step + "_start", bufs, plan, count, after)
        return [token]

    def _wait(self, after):
        self.bufs = _copies_wait(self.step + "_wait", self.bufs, self.sems, self.plan, after)
        return self.bufs

    def start_swap(self, after=()):
        lands = [lax.empty((N_CHIPS, g.shape[1] // 2, g.shape[2]), g.dtype) for g in self.bufs]
        return self._start("swap", self.bufs + lands, _plan_swap_halves, self.n, after)

    def start_scatter(self, after):
        bufs = self._wait(after)
        pair = [_add_sibling_half(g, r, self.c_arr, name=f"grad_add_sibling_{nm}", tm=min(256, g.shape[1] // 2))
                for nm, g, r in zip(self.names, bufs[:self.n], bufs[self.n:])]
        lands = [lax.empty((3,) + h.shape[1:], h.dtype) for h in pair]
        return self._start("scatter", pair + lands, _plan_scatter_chips, 3 * self.n, ())

    def start_join(self, after):
        bufs = self._wait(after)
        total = [_add_chips(h, r, self.place_arr, name=f"grad_add_chips_{nm}", tm=min(256, h.shape[1]))
                 for nm, h, r in zip(self.names, bufs[:self.n], bufs[self.n:])]
        return self._start("join", total, _plan_join_halves, self.n, ())

    def finish(self, after):
        return dict(zip(self.names, self._wait(after)))


def _all_gather_small(v):
    m_per, n = v.shape

    def body(x_ref, out_ref, send_sems, recv_sems, local_sem):
        x, y, c, chips = _place()
        me, sibling = (x, y, c), (x, y, 1 - c)

        def rows(px, py, pc):
            return out_ref.at[4 * px + 2 * py + pc]

        def copy(k, block, to, src=None):
            return _remote(rows(*block) if src is None else src, rows(*block), send_sems.at[k], recv_sems.at[k], to)

        mine = pltpu.make_async_copy(x_ref, rows(*me), local_sem)
        mine.start()
        first = [copy(0, me, sibling, src=x_ref)]
        first += [copy(1 + j, me, (*chip, c), src=x_ref) for j, chip in enumerate(chips)]
        for cp in first:
            cp.start()
        passed = [copy(4 + j, (*chip, c), sibling) for j, chip in enumerate(chips)]
        for j, chip in enumerate(chips):
            copy(1 + j, (*chip, c), me).wait_recv()
            passed[j].start()
        copy(0, sibling, me).wait_recv()
        for j, chip in enumerate(chips):
            copy(4 + j, (*chip, 1 - c), me).wait_recv()
        for cp in first + passed:
            cp.wait_send()
        mine.wait()

    return pl.pallas_call(
        body, name="gather_small_grads",
        out_shape=jax.ShapeDtypeStruct((8, m_per, n), v.dtype),
        in_specs=[pl.BlockSpec(memory_space=pltpu.VMEM)], out_specs=pl.BlockSpec(memory_space=pltpu.VMEM),
        scratch_shapes=[pltpu.SemaphoreType.DMA((7,)), pltpu.SemaphoreType.DMA((7,)), pltpu.SemaphoreType.DMA],
        compiler_params=pltpu.CompilerParams(vmem_limit_bytes=VMEM_LIMIT),
    )(v)


def _sum8(v):
    _, m, n = v.shape

    def body(v_ref, o_ref):
        acc = v_ref[0]
        for d in range(1, 8):
            acc = acc + v_ref[d]
        o_ref[...] = acc

    return pl.pallas_call(body, name="sum_small_grads", out_shape=jax.ShapeDtypeStruct((m, n), F32),
                          compiler_params=pltpu.CompilerParams(vmem_limit_bytes=VMEM_LIMIT))(v)


def _local_step(x, target, norm_w, q_norm_w, k_norm_w, sinks, a_re, a_im, log_dt, b_re, b_im, c_re, c_im, d_skip,
                b_glu, io):
    seq = x.shape[0]
    qw2 = jnp.tile(q_norm_w.reshape(1, HEAD_DIM), (1, HEADS_PER_TILE))
    kw2 = jnp.tile(k_norm_w.reshape(1, HEAD_DIM), (1, HEADS_PER_TILE))
    nw, bg = norm_w.reshape(1, D_MODEL), b_glu.reshape(1, D_MODEL)
    dsk = d_skip.reshape(1, SSM_W)

    h, rstd = _rms_fwd(x, nw, deps=io.begin())
    w_in4 = io.weight("w_in", h)
    proj = _mm(h, w_in4, mode="nn", name="mm_proj", tm=512, tn=IN_W // 4, tk=D_MODEL, b_blocked=True)
    attn, lse = _attn2_fwd(proj, qw2, kw2, sinks, deps=io.after_proj(proj))

    def gate_a(at, ag):
        return (at * (ag * _sigmoid(ag)),)

    (ya_in,) = _ew(gate_a, [(attn, "mat", 0), (proj, "mat", OFF_AGATE)], [(ATTN_W, _MXU)], rows=seq, ncol=2,
                   name="ew_attn_gate")
    w_ap4 = io.weight("w_attn_proj", ya_in)
    w_glu4, w_sp4, w_out = io.weight("w_glu", ya_in), io.weight("w_ssm_proj", ya_in), io.weight("w_out", ya_in)
    y_a = _mm(ya_in, w_ap4, mode="nn", name="mm_attn_proj", tm=1024, tn=512, tk=ATTN_W, b_blocked=True)

    coef = [t.reshape(1, N_STATES) for t in _ssm_params_fwd(a_re, a_im, log_dt.reshape(N_GROUPS, 1))]
    bre_blk, bim_blk = _block_diag_b(b_re).astype(_MXU), _block_diag_b(b_im).astype(_MXU)
    cre_blk, cim_blk = _block_diag_c(c_re).astype(_MXU), _block_diag_c(c_im).astype(_MXU)
    u_scan = _to_scan_order(proj[:, OFF_U * CW:OFF_U * CW + SSM_W])
    y_scan, s_re, s_im, i_re, i_im = _ssm_fwd(u_scan, bre_blk, bim_blk, cre_blk, cim_blk, dsk, coef)
    y_ssm = _from_scan_order(y_scan)

    (yg,) = _ew(lambda yv: (jax.nn.gelu(yv),), [(y_ssm, "mat", 0)], [(SSM_W, _MXU)], rows=seq, ncol=2, name="ew_gelu")
    glu = _mm(yg, w_glu4, mode="nn", name="mm_glu", tm=1024, tn=512, tk=SSM_W, b_blocked=True)

    def gate_s(ga, gb, ba, bb, z):
        return ((ga + ba) * _sigmoid(gb + bb) * (z * _sigmoid(z)),)

    (ys_in,) = _ew(gate_s, [(glu, "mat", 0), (glu, "mat", 2), (bg, "row", 0), (bg, "row", 2), (proj, "mat", OFF_Z)],
                   [(SSM_W, _MXU)], rows=seq, ncol=2, name="ew_ssm_gate")
    y_s = _mm(ys_in, w_sp4, mode="nn", name="mm_ssm_proj", tm=1024, tn=512, tk=SSM_W, b_blocked=True)

    def merge(ga, gs, ya, ys):
        return (_sigmoid(ga) * ya + _sigmoid(gs) * ys,)

    (merged,) = _ew(merge, [(proj, "mat", OFF_GA), (proj, "mat", OFF_GS), (y_a, "mat", 0), (y_s, "mat", 0)],
                    [(D_MODEL, _MXU)], rows=seq, ncol=4, name="ew_merge")
    mo = _mm(merged, w_out, mode="nn", name="mm_out", tm=512, tn=D_MODEL, tk=D_MODEL)

    def loss_head(xv, mv, tv):
        err = (xv + mv) - tv
        dout = err * (1.0 / D_MODEL)
        return dout, dout, _colsum(err * err)

    dout, dout_b, sq = _ew(loss_head, [(x, "mat", 0), (mo, "mat", 0), (target, "mat", 0)],
                           [(D_MODEL, F32), (D_MODEL, _MXU)], rows=seq, ncol=4, n_acc=1, name="ew_loss")
    loss = 0.5 * jnp.sum(sq) / D_MODEL

    d_merged = _mm(dout_b, w_out, mode="nt", name="mm_d_merged", tm=512, tn=D_MODEL, tk=D_MODEL)
    g_w_out = _mm(merged, dout_b, mode="tn", name="mm_g_w_out", tm=1024, tn=D_MODEL, tk=1024, out_dtype=_WIRE)

    def merge_bwd(dm, ga, gs, ya, ys):
        sa, ss = _sigmoid(ga), _sigmoid(gs)
        return sa * dm, ss * dm, dm * ya * sa * (1.0 - sa), dm * ys * ss * (1.0 - ss)

    d_ya, d_ys, d_ga, d_gs = _ew(
        merge_bwd, [(d_merged, "mat", 0), (proj, "mat", OFF_GA), (proj, "mat", OFF_GS), (y_a, "mat", 0), (y_s, "mat", 0)],
        [(D_MODEL, _MXU)] * 4, rows=seq, ncol=4, name="ew_merge_bwd")

    d_ya_in = _mm(d_ya, w_ap4, mode="nt", name="mm_d_attn_gate", tm=1024, tn=ATTN_W, tk=512, b_blocked=True)
    g_w_ap = _mm(ya_in, d_ya, mode="tn", name="mm_g_w_attn_proj", tm=ATTN_W, tn=512, tk=1024, out_dtype=_WIRE,
                 out_blocked=True)

    d_ys_in = _mm(d_ys, w_sp4, mode="nt", name="mm_d_ssm_gate", tm=1024, tn=SSM_W, tk=512, b_blocked=True)
    g_w_sp = _mm(ys_in, d_ys, mode="tn", name="mm_g_w_ssm_proj", tm=SSM_W, tn=512, tk=1024, out_dtype=_WIRE,
                 out_blocked=True)

    def gate_s_bwd(dv, ga, gb, ba, bb, z):
        a, sb = ga + ba, _sigmoid(gb + bb)
        f, df = _silu_and_grad(z)
        dga = dv * sb * f
        dgb = dv * a * f * sb * (1.0 - sb)
        return dga, dgb, dv * a * sb * df, _colsum(dga), _colsum(dgb)

    d_glu_a, d_glu_b, d_z, g_bga, g_bgb = _ew(
        gate_s_bwd, [(d_ys_in, "mat", 0), (glu, "mat", 0), (glu, "mat", 2), (bg, "row", 0), (bg, "row", 2),
                     (proj, "mat", OFF_Z)],
        [(SSM_W, _MXU)] * 3, rows=seq, ncol=2, n_acc=2, name="ew_ssm_gate_bwd")
    d_glu = jnp.concatenate([d_glu_a, d_glu_b], axis=1)
    d_yg = _mm(d_glu, w_glu4, mode="nt", name="mm_d_gelu", tm=1024, tn=SSM_W, tk=512, b_blocked=True)
    g_w_glu = _mm(yg, d_glu, mode="tn", name="mm_g_w_glu", tm=SSM_W, tn=512, tk=1024, out_dtype=_WIRE, out_blocked=True)
    dep = io.later_grads(dict(w_attn_proj=g_w_ap, w_glu=g_w_glu, w_ssm_proj=g_w_sp,
                              w_out=g_w_out.reshape(N_CHIPS, D_MODEL // N_CHIPS, D_MODEL)))

    def gate_a_bwd(dv, at, ag):
        f, df = _silu_and_grad(ag)
        return dv * f, dv * at * df

    d_attn, d_agate = _ew(gate_a_bwd, [(d_ya_in, "mat", 0), (attn, "mat", 0), (proj, "mat", OFF_AGATE)],
                          [(ATTN_W, F32), (ATTN_W, _MXU)], rows=seq, ncol=2, name="ew_attn_gate_bwd", deps=dep)

    def gelu_bwd(dv, yv):
        return (jax.vjp(jax.nn.gelu, yv)[1](dv)[0],)

    (d_yssm,) = _ew(gelu_bwd, [(d_yg, "mat", 0), (y_ssm, "mat", 0)], [(SSM_W, F32)], rows=seq, ncol=2, name="ew_gelu_bwd",
                    deps=dep)
    dep = io.before_attention_backward([d_attn, d_yssm])
    d_qkv, g_qw2, g_kw2, g_sk = _attn2_bwd(proj, qw2, kw2, sinks, lse, d_attn, deps=dep)
    (du_scan, g_bre, g_bim, g_cre, g_cim, g_dsk, g_abr, g_abi, g_cfr, g_cfi) = _ssm_bwd(
        _to_scan_order(d_yssm), u_scan, s_re, s_im, i_re, i_im, bre_blk, bim_blk, cre_blk, cim_blk, dsk, coef)
    g_are, g_aim, g_ldt = _ssm_params_bwd(a_re, a_im, log_dt.reshape(N_GROUPS, 1),
                                          *[t.reshape(N_GROUPS, STATE) for t in (g_abr, g_abi, g_cfr, g_cfi)])
    d_u = _from_scan_order(du_scan)

    d_proj = jnp.concatenate([d_qkv, d_agate, d_u.astype(_MXU), d_z, d_ga, d_gs], axis=1)
    g_w_in = _mm(h, d_proj, mode="tn", name="mm_g_w_in", tm=1024, tn=IN_W // 4, tk=1024, out_dtype=_WIRE,
                 out_blocked=True, deps=io.before_input_projection_grad([d_proj]))
    dep = io.input_projection_grad(g_w_in)
    d_h = _mm(d_proj, w_in4, mode="nt", name="mm_d_h", tm=512, tn=D_MODEL, tk=IN_W // 4, b_blocked=True, deps=dep)
    grad_x, g_nw = _rms_bwd(d_h, x, rstd, nw, dout)

    small = dict(
        norm_w=g_nw.reshape(D_MODEL), q_norm_w=g_qw2[0, :HEAD_DIM] + g_qw2[0, HEAD_DIM:],
        k_norm_w=g_kw2[0, :HEAD_DIM] + g_kw2[0, HEAD_DIM:], sinks=g_sk.reshape(N_Q_HEADS), A_re=g_are, A_im=g_aim, log_dt=g_ldt.reshape(N_GROUPS),
        B_re=_diag_of_b(g_bre), B_im=_diag_of_b(g_bim), C_re=_diag_of_c(g_cre), C_im=_diag_of_c(g_cim),
        D_skip=g_dsk.reshape(N_GROUPS, GROUP), b_glu=jnp.concatenate([g_bga, g_bgb], axis=1).reshape(D_MODEL))
    return loss, grad_x, small


_SMALL = ["norm_w", "q_norm_w", "k_norm_w", "sinks", "A_re", "A_im", "log_dt", "B_re", "B_im", "C_re", "C_im",
          "D_skip", "b_glu"]
_BIG = ["w_in", "w_attn_proj", "w_glu", "w_ssm_proj", "w_out"]
_LATER = _BIG[1:]
_ORDER = ["norm_w", "w_in", "q_norm_w", "k_norm_w", "sinks", "w_attn_proj", "A_re", "A_im", "log_dt", "B_re", "B_im",
          "C_re", "C_im", "D_skip", "w_glu", "b_glu", "w_ssm_proj", "w_out"]
_PACK_W = 1024


def _packed_rows(size):
    unit = SUBLANES * _PACK_W
    return -(-size // unit) * SUBLANES


def _pack_small(d):
    parts = []
    for n in _SMALL:
        flat = d[n].reshape(-1).astype(F32)
        rows = _packed_rows(flat.shape[0])
        parts.append(jnp.pad(flat, (0, rows * _PACK_W - flat.shape[0])).reshape(rows, _PACK_W))
    return jnp.concatenate(parts, axis=0)


def _unpack_small(packed, like):
    out, pos = {}, 0
    for n in _SMALL:
        rows = _packed_rows(like[n].size)
        out[n] = packed[pos:pos + rows].reshape(-1)[:like[n].size].reshape(like[n].shape)
        pos += rows
    return out


def _as2d(a):
    return a.reshape(1, -1) if a.ndim == 1 else a


def _adamw_whole(w, g, m, v, *, name):
    shape = w.shape
    w, g, m, v = _as2d(w), _as2d(g), _as2d(m), _as2d(v)

    def body(w_ref, g_ref, m_ref, v_ref, d_ref, nm_ref, nv_ref):
        d_ref[...], nm_ref[...], nv_ref[...] = _adamw_math(w_ref[...], g_ref[...], m_ref[...], v_ref[...])

    outs = pl.pallas_call(body, name=name, out_shape=[jax.ShapeDtypeStruct(w.shape, F32)] * 3)(w, g, m, v)
    return [o.reshape(shape) for o in outs]


class _Exchanges:
    def __init__(self, w, m, v):
        self.w, self.m, self.v = w, m, v
        self.grads, self.delta, self.new_m, self.new_v = {}, {}, {}, {}

    def _adamw(self, names, deps):
        for n in names:
            self.delta[n], self.new_m[n], self.new_v[n] = _adamw(
                self.w[n], self.grads[n], self.m[n], self.v[n], name=f"adamw_{n}", tm=128, deps=deps)

    def begin(self):
        chip = (2 * lax.axis_index("x") + lax.axis_index("y")).astype(jnp.int32).reshape(1)
        full = {n: _place_shard(self.w[n], chip, name=f"place_{n}") for n in _BIG}
        self.first = _copies_start("gather_ici_w_in_start", [full["w_in"]], _plan_gather_ici, 3)
        self.rest = _copies_start("gather_ici_rest_start", [full[n] for n in _LATER], _plan_gather_ici,
                                  3 * len(_LATER), after=[self.first[2]])
        return [self.rest[2]]

    def weight(self, name, after):
        if name == "w_in":
            sems, bufs, _ = self.first
            bufs = _copies_wait("gather_ici_w_in_wait", bufs, sems, _plan_gather_ici, [after])
            sems, bufs, token = _copies_start("gather_d2d_w_in_start", bufs, _plan_gather_d2d, 3)
            return _copies_wait("gather_d2d_w_in_wait", bufs, sems, _plan_gather_d2d, [token])[0]
        if self.rest is not None:
            sems, bufs = self.rest
            later = dict(zip(_LATER, _copies_wait("gather_d2d_rest_wait", bufs, sems, _plan_gather_d2d, [after])))
            later["w_out"] = later["w_out"].reshape(D_MODEL, D_MODEL)
            self.later, self.rest = later, None
        return self.later[name]

    def after_proj(self, proj):
        sems, bufs, _ = self.rest
        bufs = _copies_wait("gather_ici_rest_wait", bufs, sems, _plan_gather_ici, [proj])
        sems, bufs, token = _copies_start("gather_d2d_rest_start", bufs, _plan_gather_d2d, 3 * len(_LATER))
        self.rest = (sems, bufs)
        return [token]

    def later_grads(self, grads):
        self.rs_later = _ReduceScatter("later", _LATER, [grads[n] for n in _LATER])
        return self.rs_later.start_swap()

    def before_attention_backward(self, after):
        return self.rs_later.start_scatter(after)

    def before_input_projection_grad(self, after):
        return self.rs_later.start_join(after)

    def input_projection_grad(self, g_w_in):
        self.grads.update(self.rs_later.finish([g_w_in]))
        self.rs_in = _ReduceScatter("w_in", ["w_in"], [g_w_in])
        self._adamw(_LATER, self.rs_in.start_swap())
        return self.rs_in.start_scatter([self.delta[n] for n in _LATER])

    def finish(self, after):
        self.grads.update(self.rs_in.finish(self.rs_in.start_join(after)))
        self._adamw(["w_in"], ())


def kernel(x, norm_w, w_in, q_norm_w, k_norm_w, sinks, w_attn_proj, A_re, A_im, log_dt, B_re, B_im, C_re, C_im, D_skip, w_glu, b_glu, w_ssm_proj, w_out, loss_target, m_norm_w, m_w_in, m_q_norm_w, m_k_norm_w, m_sinks, m_w_attn_proj, m_A_re, m_A_im, m_log_dt, m_B_re, m_B_im, m_C_re, m_C_im, m_D_skip, m_w_glu, m_b_glu, m_w_ssm_proj, m_w_out, v_norm_w, v_w_in, v_q_norm_w, v_k_norm_w, v_sinks, v_w_attn_proj, v_A_re, v_A_im, v_log_dt, v_B_re, v_B_im, v_C_re, v_C_im, v_D_skip, v_w_glu, v_b_glu, v_w_ssm_proj, v_w_out):
    w = dict(norm_w=norm_w, w_in=w_in, q_norm_w=q_norm_w, k_norm_w=k_norm_w, sinks=sinks, w_attn_proj=w_attn_proj,
             A_re=A_re, A_im=A_im, log_dt=log_dt, B_re=B_re, B_im=B_im, C_re=C_re, C_im=C_im, D_skip=D_skip,
             w_glu=w_glu, b_glu=b_glu, w_ssm_proj=w_ssm_proj, w_out=w_out)
    m = dict(norm_w=m_norm_w, w_in=m_w_in, q_norm_w=m_q_norm_w, k_norm_w=m_k_norm_w, sinks=m_sinks,
             w_attn_proj=m_w_attn_proj, A_re=m_A_re, A_im=m_A_im, log_dt=m_log_dt, B_re=m_B_re, B_im=m_B_im,
             C_re=m_C_re, C_im=m_C_im, D_skip=m_D_skip, w_glu=m_w_glu, b_glu=m_b_glu, w_ssm_proj=m_w_ssm_proj,
             w_out=m_w_out)
    v = dict(norm_w=v_norm_w, w_in=v_w_in, q_norm_w=v_q_norm_w, k_norm_w=v_k_norm_w, sinks=v_sinks,
             w_attn_proj=v_w_attn_proj, A_re=v_A_re, A_im=v_A_im, log_dt=v_log_dt, B_re=v_B_re, B_im=v_B_im,
             C_re=v_C_re, C_im=v_C_im, D_skip=v_D_skip, w_glu=v_w_glu, b_glu=v_b_glu, w_ssm_proj=v_w_ssm_proj,
             w_out=v_w_out)

    io = _Exchanges(w, m, v)
    loss, grad_x, small = _local_step(x[0], loss_target[0], norm_w, q_norm_w, k_norm_w, sinks, A_re, A_im, log_dt,
                                      B_re, B_im, C_re, C_im, D_skip, b_glu, io)
    loss = lax.psum(loss, ("x", "y", "c"))

    grads, delta, new_m, new_v = io.grads, io.delta, io.new_m, io.new_v
    small_sum = _sum8(_all_gather_small(_pack_small(small)))
    grads.update(_unpack_small(small_sum, w))
    for n in _SMALL:
        delta[n], new_m[n], new_v[n] = _adamw_whole(w[n], grads[n], m[n], v[n], name=f"adamw_{n}")
    io.finish([delta[_SMALL[-1]]])

    return (loss, grad_x[None], *[grads[n] for n in _ORDER], *[delta[n] for n in _ORDER],
            *[new_m[n] for n in _ORDER], *[new_v[n] for n in _ORDER])
```

```python
import functools
import math

import jax
import jax.numpy as jnp
from jax import lax
from jax.experimental import pallas as pl
from jax.experimental.pallas import tpu as pltpu

F32 = jnp.float32
_MXU = jnp.bfloat16
_WIRE = jnp.bfloat16

LANES = 128
SUBLANES = 8
VMEM_LIMIT = 56 * 1024 * 1024

D_MODEL = 2048
HEAD_DIM = 64
N_Q_HEADS = 16
N_KV_HEADS = 4
Q_PER_KV = 4
ATTN_W = 1024
KV_W = 256
WINDOW = 128
SSM_W = 1024
GROUP = 16
N_GROUPS = 64
STATE = 64
N_STATES = N_GROUPS * STATE
IN_W = 8704
NORM_EPS = 1e-6
N_CHIPS = 4
CW = 512
OFF_AGATE, OFF_U, OFF_Z, OFF_GA, OFF_GS = 3, 5, 7, 9, 13

SSM_T = 256
SSM_L = SSM_T // SUBLANES
SSM_JB = 8
SSM_SB = N_STATES // SSM_JB

ADAM_LR, ADAM_B1, ADAM_B2, ADAM_EPS, ADAM_WD, ADAM_STEP = 0.001, 0.9, 0.999, 1e-08, 0.01, 10

MESH = pl.DeviceIdType.MESH
_ANY = pl.BlockSpec(memory_space=pl.ANY)


def _params(sem=None):
    return pltpu.CompilerParams(dimension_semantics=sem, vmem_limit_bytes=VMEM_LIMIT)


def _mm(a, b, *, mode, name, tm, tn, tk, out_dtype=F32, b_blocked=False, out_blocked=False, deps=()):
    nd = len(deps)
    if mode == "tn":
        K, M = a.shape
    else:
        M, K = a.shape
    if mode == "nn":
        N = b.shape[0] * b.shape[2] if b_blocked else b.shape[1]
    elif mode == "nt":
        N = b.shape[1] if b_blocked else b.shape[0]
    else:
        N = b.shape[1]
    tm, tn, tk = min(tm, M), min(tn, N), min(tk, K)
    nj, ni, nk = N // tn, M // tm, K // tk
    assert nj * tn == N and ni * tm == M and nk * tk == K, (name, M, N, K)
    dims = {"nn": (((1,), (0,)), ((), ())), "nt": (((1,), (1,)), ((), ())), "tn": (((0,), (0,)), ((), ()))}[mode]

    if mode == "tn":
        a_spec = pl.BlockSpec((tk, tm), lambda j, i, k: (k, i))
    else:
        a_spec = pl.BlockSpec((tm, tk), lambda j, i, k: (i, k))
    if mode == "nn":
        if b_blocked:
            assert b.shape[0] == nj and b.shape[2] == tn
            b_spec = pl.BlockSpec((None, tk, tn), lambda j, i, k: (j, k, 0))
        else:
            b_spec = pl.BlockSpec((tk, tn), lambda j, i, k: (k, j))
    elif mode == "nt":
        if b_blocked:
            assert b.shape[0] == nk and b.shape[2] == tk
            b_spec = pl.BlockSpec((None, tn, tk), lambda j, i, k: (k, j, 0))
        else:
            b_spec = pl.BlockSpec((tn, tk), lambda j, i, k: (j, k))
    else:
        b_spec = pl.BlockSpec((tk, tn), lambda j, i, k: (k, j))
    if out_blocked:
        assert nj == N_CHIPS
        o_spec = pl.BlockSpec((None, tm, tn), lambda j, i, k: (j, i, 0))
        o_shape = jax.ShapeDtypeStruct((nj, M, tn), out_dtype)
    else:
        o_spec = pl.BlockSpec((tm, tn), lambda j, i, k: (i, j))
        o_shape = jax.ShapeDtypeStruct((M, N), out_dtype)
    use_acc = nk > 1 and out_dtype != F32

    def body(a_ref, b_ref, *rest):
        o_ref, scratch = rest[nd], rest[nd + 1:]
        part = lax.dot_general(a_ref[...].astype(_MXU), b_ref[...].astype(_MXU), dims,
                               preferred_element_type=F32)
        if nk == 1:
            o_ref[...] = part.astype(o_ref.dtype)
            return
        k = pl.program_id(2)
        acc = scratch[0] if use_acc else o_ref

        @pl.when(k == 0)
        def _():
            acc[...] = part

        @pl.when(k > 0)
        def _():
            acc[...] += part

        if use_acc:
            @pl.when(k == nk - 1)
            def _():
                o_ref[...] = acc[...].astype(o_ref.dtype)

    return pl.pallas_call(
        body, name=name, grid=(nj, ni, nk), in_specs=[a_spec, b_spec] + [_ANY] * nd, out_specs=o_spec,
        out_shape=o_shape, scratch_shapes=[pltpu.VMEM((tm, tn), F32)] if use_acc else [],
        compiler_params=_params(("parallel", "parallel", "arbitrary")),
    )(a, b, *deps)


def _ew(fn, ins, outs, *, rows, ncol, name, n_acc=0, tm=512, deps=()):
    n_in, n_out, nd = len(ins), len(outs), len(deps)
    tm = min(tm, rows)
    in_specs = []
    for _, kind, col0 in ins:
        if kind == "mat":
            in_specs.append(pl.BlockSpec((tm, CW), lambda j, i, c0=col0: (i, c0 + j)))
        else:
            in_specs.append(pl.BlockSpec((1, CW), lambda j, i, c0=col0: (0, c0 + j)))
    out_specs = [pl.BlockSpec((tm, CW), lambda j, i: (i, j)) for _ in outs]
    out_shape = [jax.ShapeDtypeStruct((rows, w), dt) for w, dt in outs]
    for _ in range(n_acc):
        out_specs.append(pl.BlockSpec((1, CW), lambda j, i: (0, j)))
        out_shape.append(jax.ShapeDtypeStruct((1, ncol * CW), F32))

    def body(*refs):
        vals = fn(*[r[...] for r in refs[:n_in]])
        refs = refs[n_in + nd:]
        for r, v in zip(refs[:n_out], vals[:n_out]):
            r[...] = v.astype(r.dtype)
        i = pl.program_id(1)
        for r, v in zip(refs[n_out:], vals[n_out:]):
            @pl.when(i == 0)
            def _(r=r, v=v):
                r[...] = v

            @pl.when(i > 0)
            def _(r=r, v=v):
                r[...] += v

    res = pl.pallas_call(
        body, name=name, grid=(ncol, rows // tm), in_specs=in_specs + [_ANY] * nd, out_specs=out_specs,
        out_shape=out_shape, compiler_params=_params(("parallel", "arbitrary")),
    )(*[a for a, _, _ in ins], *deps)
    return res


def _colsum(v):
    return jnp.sum(v, axis=0, keepdims=True)


def _sigmoid(v):
    return jax.nn.sigmoid(v)


def _silu_and_grad(v):
    s = _sigmoid(v)
    return v * s, s * (1.0 + v * (1.0 - s))


def _rms_fwd(x, w, *, tm=512, deps=()):
    rows, d = x.shape
    nd = len(deps)

    def body(x_ref, w_ref, *rest):
        h_ref, r_ref = rest[nd:]
        xv = x_ref[...]
        r = lax.rsqrt(jnp.mean(xv * xv, axis=-1, keepdims=True) + NORM_EPS)
        h_ref[...] = (xv * r * w_ref[...]).astype(h_ref.dtype)
        r_ref[...] = r

    return pl.pallas_call(
        body, name="rms_fwd", grid=(rows // tm,),
        in_specs=[pl.BlockSpec((tm, d), lambda i: (i, 0)), pl.BlockSpec((1, d), lambda i: (0, 0))] + [_ANY] * nd,
        out_specs=[pl.BlockSpec((tm, d), lambda i: (i, 0)), pl.BlockSpec((tm, 1), lambda i: (i, 0))],
        out_shape=[jax.ShapeDtypeStruct((rows, d), _MXU), jax.ShapeDtypeStruct((rows, 1), F32)],
        compiler_params=_params(("arbitrary",)),
    )(x, w, *deps)


def _rms_bwd(dh, x, rstd, w, dout, *, tm=256):
    rows, d = x.shape

    def body(dh_ref, x_ref, r_ref, w_ref, do_ref, gx_ref, gw_ref):
        dhv, xv, r, wv = dh_ref[...], x_ref[...], r_ref[...], w_ref[...]
        xr = xv * r
        t = jnp.mean(dhv * wv * xr, axis=-1, keepdims=True)
        gx_ref[...] = do_ref[...] + r * (wv * dhv - xr * t)
        part = _colsum(dhv * xr)
        i = pl.program_id(0)

        @pl.when(i == 0)
        def _():
            gw_ref[...] = part

        @pl.when(i > 0)
        def _():
            gw_ref[...] += part

    return pl.pallas_call(
        body, name="rms_bwd", grid=(rows // tm,),
        in_specs=[pl.BlockSpec((tm, d), lambda i: (i, 0)), pl.BlockSpec((tm, d), lambda i: (i, 0)),
                  pl.BlockSpec((tm, 1), lambda i: (i, 0)), pl.BlockSpec((1, d), lambda i: (0, 0)),
                  pl.BlockSpec((tm, d), lambda i: (i, 0))],
        out_specs=[pl.BlockSpec((tm, d), lambda i: (i, 0)), pl.BlockSpec((1, d), lambda i: (0, 0))],
        out_shape=[jax.ShapeDtypeStruct((rows, d), F32), jax.ShapeDtypeStruct((1, d), F32)],
        compiler_params=_params(("arbitrary",)),
    )(dh, x, rstd, w, dout)


_NT = (((1,), (1,)), ((), ()))
_TN = (((0,), (0,)), ((), ()))


QKV_W = ATTN_W + 2 * KV_W
HEADS_PER_TILE = LANES // HEAD_DIM


def _low_half(rows):
    return lax.broadcasted_iota(jnp.int32, (rows, LANES), 1) < HEAD_DIM


def _pair_mean(t, low):
    m_lo = jnp.sum(jnp.where(low, t, 0.0), axis=-1, keepdims=True)
    m_hi = jnp.sum(jnp.where(low, 0.0, t), axis=-1, keepdims=True)
    return jnp.where(low, m_lo, m_hi) * (1.0 / HEAD_DIM)


def _pair_rstd(t, low):
    return lax.rsqrt(_pair_mean(t * t, low) + NORM_EPS)


def _dup_half(t, hi, low):
    swapped = pltpu.roll(t, HEAD_DIM, 1)
    return jnp.where(low, swapped, t) if hi else jnp.where(low, t, swapped)


def _fold_halves(t):
    return t + pltpu.roll(t, HEAD_DIM, 1)


def _split_heads(t, low):
    return [jnp.where(low, t, 0.0), jnp.where(low, 0.0, t)]


def _stacked_band_mask(n):
    rows = Q_PER_KV * WINDOW
    qi = lax.broadcasted_iota(jnp.int32, (rows, 2 * WINDOW), 0) % WINDOW + WINDOW
    kj = lax.broadcasted_iota(jnp.int32, (rows, 2 * WINDOW), 1)
    diff = qi - kj
    first_key = jnp.where(n > 0, 0, WINDOW)
    return (diff >= 0) & (diff < WINDOW) & (kj >= first_key)


def _stacked_sinks(sink_ref, g):
    blk = lax.broadcasted_iota(jnp.int32, (Q_PER_KV * WINDOW, 1), 0) // WINDOW
    col = jnp.full((Q_PER_KV * WINDOW, 1), sink_ref[Q_PER_KV * g], F32)
    for r in range(1, Q_PER_KV):
        col = jnp.where(blk == r, sink_ref[Q_PER_KV * g + r], col)
    return col


def _attn_in_specs(nblk, rev):
    def cur(n):
        return (nblk - 1 - n) if rev else n

    q_spec = pl.BlockSpec((WINDOW, ATTN_W), lambda n: (cur(n), 0))
    kvc_spec = pl.BlockSpec((WINDOW, 2 * KV_W), lambda n: (cur(n), ATTN_W // (2 * KV_W)))
    kvp_spec = pl.BlockSpec((WINDOW, 2 * KV_W), lambda n: (jnp.maximum(cur(n) - 1, 0), ATTN_W // (2 * KV_W)))
    w_spec = pl.BlockSpec((1, LANES), lambda n: (0, 0))
    l_spec = pl.BlockSpec((WINDOW, N_Q_HEADS), lambda n: (cur(n), 0))
    return q_spec, kvc_spec, kvp_spec, w_spec, l_spec


def _attn2_fwd(proj, qw2, kw2, sinks, deps=()):
    seq = proj.shape[0]
    nblk = seq // WINDOW
    scale = 1.0 / math.sqrt(HEAD_DIM)
    q_spec, kvc_spec, kvp_spec, w_spec, l_spec = _attn_in_specs(nblk, False)
    nd = len(deps)

    def body(sink_ref, q_ref, kvc_ref, kvp_ref, qw_ref, kw_ref, *rest):
        o_ref, lse_ref = rest[nd:]
        n = pl.program_id(0)
        low, low2 = _low_half(WINDOW), _low_half(2 * WINDOW)
        valid = _stacked_band_mask(n)
        head_lane = lax.broadcasted_iota(jnp.int32, (WINDOW, N_Q_HEADS), 1)
        kv = jnp.concatenate([kvp_ref[...], kvc_ref[...]], axis=0)
        qwv, kwv = qw_ref[...], kw_ref[...]
        lse_blk = jnp.zeros((WINDOW, N_Q_HEADS), F32)
        for t in range(N_KV_HEADS // HEADS_PER_TILE):
            kt = kv[:, t * LANES:(t + 1) * LANES]
            vt = kv[:, KV_W + t * LANES:KV_W + (t + 1) * LANES]
            kn = kt * _pair_rstd(kt, low2) * kwv
            for hi in range(HEADS_PER_TILE):
                g = HEADS_PER_TILE * t + hi
                kdup = _dup_half(kn, hi, low2).astype(_MXU)
                vdup = _dup_half(vt, hi, low2).astype(_MXU)
                stack = []
                for tq in (2 * g, 2 * g + 1):
                    qt = q_ref[:, tq * LANES:(tq + 1) * LANES]
                    stack += _split_heads(qt * _pair_rstd(qt, low) * qwv, low)
                qs = jnp.concatenate(stack, axis=0).astype(_MXU)
                s = lax.dot_general(qs, kdup, _NT, preferred_element_type=F32) * scale
                s = jnp.where(valid, s, -1e30)
                sink = _stacked_sinks(sink_ref, g)
                m = jnp.maximum(jnp.max(s, axis=-1, keepdims=True), sink)
                e = jnp.exp(s - m)
                z = jnp.sum(e, axis=-1, keepdims=True) + jnp.exp(sink - m)
                o = jnp.dot((e / z).astype(_MXU), vdup, preferred_element_type=F32)
                for i, tq in enumerate((2 * g, 2 * g + 1)):
                    o_ref[:, tq * LANES:(tq + 1) * LANES] = jnp.where(
                        low, o[2 * i * WINDOW:(2 * i + 1) * WINDOW], o[(2 * i + 1) * WINDOW:(2 * i + 2) * WINDOW])
                lse = m + jnp.log(z)
                for r in range(Q_PER_KV):
                    lse_blk = jnp.where(head_lane == Q_PER_KV * g + r, lse[r * WINDOW:(r + 1) * WINDOW], lse_blk)
        lse_ref[...] = lse_blk

    return pl.pallas_call(
        body, name="attn_fwd", grid=(nblk,),
        in_specs=[pl.BlockSpec(memory_space=pltpu.SMEM), q_spec, kvc_spec, kvp_spec, w_spec, w_spec] + [_ANY] * nd,
        out_specs=[q_spec, l_spec],
        out_shape=[jax.ShapeDtypeStruct((seq, ATTN_W), F32), jax.ShapeDtypeStruct((seq, N_Q_HEADS), F32)],
        compiler_params=_params(("arbitrary",)),
    )(sinks, proj, proj, proj, qw2, kw2, *deps)


def _attn2_bwd(proj, qw2, kw2, sinks, lse, do, deps=()):
    seq = proj.shape[0]
    nblk = seq // WINDOW
    scale = 1.0 / math.sqrt(HEAD_DIM)
    q_spec, kvc_spec, kvp_spec, w_spec, l_spec = _attn_in_specs(nblk, True)
    s_spec = pl.BlockSpec((1, N_Q_HEADS), lambda n: (0, 0))
    d_spec = pl.BlockSpec((WINDOW, QKV_W), lambda n: (nblk - 1 - n, 0))
    nd = len(deps)

    def body(sink_ref, q_ref, kvc_ref, kvp_ref, qw_ref, kw_ref, lse_ref, do_ref, *rest):
        d_ref, dqw_ref, dkw_ref, dsk_ref, carry = rest[nd:]
        step = pl.program_id(0)
        n = nblk - 1 - step

        @pl.when(step == 0)
        def _():
            carry[...] = jnp.zeros_like(carry)
            dqw_ref[...] = jnp.zeros_like(dqw_ref)
            dkw_ref[...] = jnp.zeros_like(dkw_ref)
            dsk_ref[...] = jnp.zeros_like(dsk_ref)

        low, low2 = _low_half(WINDOW), _low_half(2 * WINDOW)
        valid = _stacked_band_mask(n)
        head_lane = lax.broadcasted_iota(jnp.int32, (WINDOW, N_Q_HEADS), 1)
        sink_lane = lax.broadcasted_iota(jnp.int32, (1, N_Q_HEADS), 1)
        kv = jnp.concatenate([kvp_ref[...], kvc_ref[...]], axis=0)
        qwv, kwv = qw_ref[...], kw_ref[...]
        lse_blk = lse_ref[...]
        dqw = jnp.zeros((1, LANES), F32)
        dkw = jnp.zeros((1, LANES), F32)
        dsk = jnp.zeros((1, N_Q_HEADS), F32)
        for t in range(N_KV_HEADS // HEADS_PER_TILE):
            kt = kv[:, t * LANES:(t + 1) * LANES]
            vt = kv[:, KV_W + t * LANES:KV_W + (t + 1) * LANES]
            rk = _pair_rstd(kt, low2)
            kn = kt * rk * kwv
            dkn_t = jnp.zeros((2 * WINDOW, LANES), F32)
            dv_t = jnp.zeros((2 * WINDOW, LANES), F32)
            for hi in range(HEADS_PER_TILE):
                g = HEADS_PER_TILE * t + hi
                kdup = _dup_half(kn, hi, low2).astype(_MXU)
                vdup = _dup_half(vt, hi, low2).astype(_MXU)
                tiles = (2 * g, 2 * g + 1)
                qx, rq, stack, dstack, lse_rows = [], [], [], [], []
                for tq in tiles:
                    qt = q_ref[:, tq * LANES:(tq + 1) * LANES]
                    r = _pair_rstd(qt, low)
                    rq.append(r)
                    qx.append(qt * r)
                    stack += _split_heads(qx[-1] * qwv, low)
                    dstack += _split_heads(do_ref[:, tq * LANES:(tq + 1) * LANES], low)
                for r in range(Q_PER_KV):
                    lse_rows.append(jnp.sum(jnp.where(head_lane == Q_PER_KV * g + r, lse_blk, 0.0), axis=-1, keepdims=True))
                qs = jnp.concatenate(stack, axis=0).astype(_MXU)
                dos = jnp.concatenate(dstack, axis=0).astype(_MXU)
                lse_col = jnp.concatenate(lse_rows, axis=0)
                s = lax.dot_general(qs, kdup, _NT, preferred_element_type=F32) * scale
                s = jnp.where(valid, s, -1e30)
                p = jnp.exp(s - lse_col)
                dp = lax.dot_general(dos, vdup, _NT, preferred_element_type=F32)
                dsum = jnp.sum(p * dp, axis=-1, keepdims=True)
                ds = (p * (dp - dsum) * scale).astype(_MXU)
                dsink = -jnp.exp(_stacked_sinks(sink_ref, g) - lse_col) * dsum
                for r in range(Q_PER_KV):
                    dsk = dsk + jnp.where(sink_lane == Q_PER_KV * g + r, _colsum(dsink[r * WINDOW:(r + 1) * WINDOW]), 0.0)
                dv_g = _fold_halves(lax.dot_general(p.astype(_MXU), dos, _TN, preferred_element_type=F32))
                dkn_g = _fold_halves(lax.dot_general(ds, qs, _TN, preferred_element_type=F32))
                dv_t = jnp.where(low2, dv_t, dv_g) if hi else jnp.where(low2, dv_g, dv_t)
                dkn_t = jnp.where(low2, dkn_t, dkn_g) if hi else jnp.where(low2, dkn_g, dkn_t)
                dqn = jnp.dot(ds, kdup, preferred_element_type=F32)
                for i, tq in enumerate(tiles):
                    dqn_t = jnp.where(low, dqn[2 * i * WINDOW:(2 * i + 1) * WINDOW],
                                      dqn[(2 * i + 1) * WINDOW:(2 * i + 2) * WINDOW])
                    dq = rq[i] * (qwv * dqn_t - qx[i] * _pair_mean(dqn_t * qwv * qx[i], low))
                    d_ref[:, tq * LANES:(tq + 1) * LANES] = dq.astype(d_ref.dtype)
                    dqw = dqw + _colsum(dqn_t * qx[i])
            k_cols = slice(t * LANES, (t + 1) * LANES)
            v_cols = slice(KV_W + t * LANES, KV_W + (t + 1) * LANES)
            dkn_c = dkn_t[WINDOW:] + carry[:, k_cols]
            rc = rk[WINDOW:]
            kx = kt[WINDOW:] * rc
            dk = rc * (kwv * dkn_c - kx * _pair_mean(dkn_c * kwv * kx, low))
            d_ref[:, ATTN_W + t * LANES:ATTN_W + (t + 1) * LANES] = dk.astype(d_ref.dtype)
            d_ref[:, ATTN_W + KV_W + t * LANES:ATTN_W + KV_W + (t + 1) * LANES] = (
                dv_t[WINDOW:] + carry[:, v_cols]).astype(d_ref.dtype)
            carry[:, k_cols] = dkn_t[:WINDOW]
            carry[:, v_cols] = dv_t[:WINDOW]
            dkw = dkw + _colsum(dkn_c * kx)
        dqw_ref[...] += dqw
        dkw_ref[...] += dkw
        dsk_ref[...] += dsk

    return pl.pallas_call(
        body, name="attn_bwd", grid=(nblk,),
        in_specs=[pl.BlockSpec(memory_space=pltpu.SMEM), q_spec, kvc_spec, kvp_spec, w_spec, w_spec, l_spec, q_spec]
        + [_ANY] * nd,
        out_specs=[d_spec, w_spec, w_spec, s_spec],
        out_shape=[jax.ShapeDtypeStruct((seq, QKV_W), _MXU), jax.ShapeDtypeStruct((1, LANES), F32),
                   jax.ShapeDtypeStruct((1, LANES), F32), jax.ShapeDtypeStruct((1, N_Q_HEADS), F32)],
        scratch_shapes=[pltpu.VMEM((WINDOW, 2 * KV_W), F32)],
        compiler_params=_params(("arbitrary",)),
    )(sinks, proj, proj, proj, qw2, kw2, lse, do, *deps)


def _ssm_discretise(a_re, a_im, log_dt):
    dt = jnp.exp(log_dt)
    mag = jnp.exp(dt * a_re)
    ab_re = mag * jnp.cos(dt * a_im)
    ab_im = mag * jnp.sin(dt * a_im)
    num_re = ab_re - 1.0
    num_im = ab_im
    den = a_re * a_re + a_im * a_im
    cf_re = (num_re * a_re + num_im * a_im) / den
    cf_im = (num_im * a_re - num_re * a_im) / den
    return ab_re, ab_im, cf_re, cf_im


def _ssm_params_fwd(a_re, a_im, log_dt):
    shp = jax.ShapeDtypeStruct(a_re.shape, F32)

    def body(are_ref, aim_ref, ldt_ref, abr_ref, abi_ref, cfr_ref, cfi_ref, alr_ref, ali_ref):
        abr, abi, cfr, cfi = _ssm_discretise(are_ref[...], aim_ref[...], ldt_ref[...])
        abr_ref[...], abi_ref[...], cfr_ref[...], cfi_ref[...] = abr, abi, cfr, cfi
        pr, pi = abr, abi
        for _ in range(int(math.log2(SSM_L))):
            pr, pi = pr * pr - pi * pi, 2.0 * pr * pi
        alr_ref[...], ali_ref[...] = pr, pi

    return pl.pallas_call(body, name="ssm_params_fwd", out_shape=[shp] * 6)(a_re, a_im, log_dt)


def _ssm_params_bwd(a_re, a_im, log_dt, d_abr, d_abi, d_cfr, d_cfi):
    def body(are_ref, aim_ref, ldt_ref, g0, g1, g2, g3, dare_ref, daim_ref, dldt_ref):
        _, vjp = jax.vjp(_ssm_discretise, are_ref[...], aim_ref[...], ldt_ref[...])
        dare_ref[...], daim_ref[...], dldt_ref[...] = vjp((g0[...], g1[...], g2[...], g3[...]))

    return pl.pallas_call(
        body, name="ssm_params_bwd",
        out_shape=[jax.ShapeDtypeStruct(a_re.shape, F32), jax.ShapeDtypeStruct(a_im.shape, F32),
                   jax.ShapeDtypeStruct(log_dt.shape, F32)],
    )(a_re, a_im, log_dt, d_abr, d_abi, d_cfr, d_cfi)


def _scan_cols(j):
    return pl.ds(j * SSM_SB, SSM_SB)


def _rows8(r):
    return pl.ds(pl.multiple_of(r * SUBLANES, SUBLANES), SUBLANES)


def _bcast8(row):
    return jnp.broadcast_to(row, (SUBLANES, row.shape[-1]))


def _ssm_fwd(u, b_re, b_im, c_re, c_im, d_skip, coef):
    seq = u.shape[0]
    nc = seq // SSM_T
    T, L = SSM_T, SSM_L

    def body(u_ref, bre_ref, bim_ref, cre_ref, cim_ref, d_ref, are_ref, aim_ref, cfr_ref, cfi_ref, alr_ref, ali_ref,
             y_ref, sre_ref, sim_ref, ire_ref, iim_ref, car_re, car_im, end_re, end_im):
        c = pl.program_id(0)

        @pl.when(c == 0)
        def _():
            car_re[...] = jnp.zeros_like(car_re)
            car_im[...] = jnp.zeros_like(car_im)

        for j in range(SSM_JB):
            ub = u_ref[:, j * LANES:(j + 1) * LANES].astype(_MXU)
            bur = jnp.dot(ub, bre_ref[j], preferred_element_type=F32)
            bui = jnp.dot(ub, bim_ref[j], preferred_element_type=F32)
            cfr, cfi = cfr_ref[:, _scan_cols(j)], cfi_ref[:, _scan_cols(j)]
            sre_ref[:, _scan_cols(j)] = cfr * bur - cfi * bui
            sim_ref[:, _scan_cols(j)] = cfr * bui + cfi * bur

        for j in range(SSM_JB):
            cols = _scan_cols(j)
            ar, ai = _bcast8(are_ref[:, cols]), _bcast8(aim_ref[:, cols])

            def step1(r, s, cols=cols, ar=ar, ai=ai):
                sr, si = s
                rows = _rows8(r)
                return (ar * sr - ai * si + sre_ref[rows, cols], ar * si + ai * sr + sim_ref[rows, cols])

            zero = jnp.zeros((SUBLANES, SSM_SB), F32)
            er, ei = lax.fori_loop(0, L, step1, (zero, zero), unroll=4)
            end_re[:, cols] = er
            end_im[:, cols] = ei

        alr, ali = alr_ref[...], ali_ref[...]
        cr, ci = car_re[...], car_im[...]
        ire_ref[0:1, :] = cr
        iim_ref[0:1, :] = ci
        for i in range(1, SUBLANES):
            er, ei = end_re[i - 1:i, :], end_im[i - 1:i, :]
            cr, ci = alr * cr - ali * ci + er, alr * ci + ali * cr + ei
            ire_ref[i:i + 1, :] = cr
            iim_ref[i:i + 1, :] = ci

        for j in range(SSM_JB):
            cols = _scan_cols(j)
            ar, ai = _bcast8(are_ref[:, cols]), _bcast8(aim_ref[:, cols])

            def step2(r, s, cols=cols, ar=ar, ai=ai):
                sr, si = s
                rows = _rows8(r)
                nr = ar * sr - ai * si + sre_ref[rows, cols]
                ni = ar * si + ai * sr + sim_ref[rows, cols]
                sre_ref[rows, cols] = nr
                sim_ref[rows, cols] = ni
                return nr, ni

            lax.fori_loop(0, L, step2, (ire_ref[:, cols], iim_ref[:, cols]), unroll=4)

        car_re[...] = sre_ref[T - 1:T, :]
        car_im[...] = sim_ref[T - 1:T, :]

        for j in range(SSM_JB):
            cols = _scan_cols(j)
            ch = slice(j * LANES, (j + 1) * LANES)
            y = (jnp.dot(sre_ref[:, cols].astype(_MXU), cre_ref[j], preferred_element_type=F32)
                 - jnp.dot(sim_ref[:, cols].astype(_MXU), cim_ref[j], preferred_element_type=F32))
            y_ref[:, ch] = y + d_ref[:, ch] * u_ref[:, ch]

    tok = pl.BlockSpec((T, SSM_W), lambda c: (c, 0))
    st = pl.BlockSpec((T, N_STATES), lambda c: (c, 0))
    ini = pl.BlockSpec((None, SUBLANES, N_STATES), lambda c: (c, 0, 0))
    bsp = pl.BlockSpec((SSM_JB, LANES, SSM_SB), lambda c: (0, 0, 0))
    csp = pl.BlockSpec((SSM_JB, SSM_SB, LANES), lambda c: (0, 0, 0))
    row_w = pl.BlockSpec((1, SSM_W), lambda c: (0, 0))
    row_s = pl.BlockSpec((1, N_STATES), lambda c: (0, 0))
    return pl.pallas_call(
        body, name="ssm_fwd", grid=(nc,),
        in_specs=[tok, bsp, bsp, csp, csp, row_w] + [row_s] * 6,
        out_specs=[tok, st, st, ini, ini],
        out_shape=[jax.ShapeDtypeStruct((seq, SSM_W), F32),
                   jax.ShapeDtypeStruct((seq, N_STATES), F32), jax.ShapeDtypeStruct((seq, N_STATES), F32),
                   jax.ShapeDtypeStruct((nc, SUBLANES, N_STATES), F32),
                   jax.ShapeDtypeStruct((nc, SUBLANES, N_STATES), F32)],
        scratch_shapes=[pltpu.VMEM((1, N_STATES), F32), pltpu.VMEM((1, N_STATES), F32),
                        pltpu.VMEM((SUBLANES, N_STATES), F32), pltpu.VMEM((SUBLANES, N_STATES), F32)],
        compiler_params=_params(("arbitrary",)),
    )(u, b_re, b_im, c_re, c_im, d_skip, *coef)


def _ssm_bwd(dy, u, s_re, s_im, i_re, i_im, b_re, b_im, c_re, c_im, d_skip, coef):
    seq = u.shape[0]
    nc = seq // SSM_T
    T, L = SSM_T, SSM_L

    def body(dy_ref, u_ref, sre_ref, sim_ref, ire_ref, iim_ref, bre_ref, bim_ref, cre_ref, cim_ref, d_ref,
             are_ref, aim_ref, cfr_ref, cfi_ref, alr_ref, ali_ref,
             du_ref, dbre_out, dbim_out, dcre_out, dcim_out, dd_ref, dar_ref, dai_ref, dcfr_ref, dcfi_ref,
             lre, lim, car_re, car_im, end_re, end_im, ini_re, ini_im, dbre_ref, dbim_ref, dcre_ref, dcim_ref):
        step = pl.program_id(0)

        @pl.when(step == 0)
        def _():
            car_re[...] = jnp.zeros_like(car_re)
            car_im[...] = jnp.zeros_like(car_im)
            for ref in (dbre_ref, dbim_ref, dcre_ref, dcim_ref, dd_ref, dar_ref, dai_ref, dcfr_ref, dcfi_ref):
                ref[...] = jnp.zeros_like(ref)

        for j in range(SSM_JB):
            dyb = dy_ref[:, j * LANES:(j + 1) * LANES].astype(_MXU)
            lre[:, _scan_cols(j)] = lax.dot_general(dyb, cre_ref[j], _NT, preferred_element_type=F32)
            lim[:, _scan_cols(j)] = -lax.dot_general(dyb, cim_ref[j], _NT, preferred_element_type=F32)

        for j in range(SSM_JB):
            cols = _scan_cols(j)
            ar, ai = _bcast8(are_ref[:, cols]), _bcast8(aim_ref[:, cols])

            def step1(t, s, cols=cols, ar=ar, ai=ai):
                sr, si = s
                rows = _rows8(L - 1 - t)
                return (ar * sr + ai * si + lre[rows, cols], ar * si - ai * sr + lim[rows, cols])

            zero = jnp.zeros((SUBLANES, SSM_SB), F32)
            er, ei = lax.fori_loop(0, L, step1, (zero, zero), unroll=4)
            end_re[:, cols] = er
            end_im[:, cols] = ei

        alr, ali = alr_ref[...], ali_ref[...]
        cr, ci = car_re[...], car_im[...]
        ini_re[SUBLANES - 1:SUBLANES, :] = cr
        ini_im[SUBLANES - 1:SUBLANES, :] = ci
        for i in range(SUBLANES - 2, -1, -1):
            er, ei = end_re[i + 1:i + 2, :], end_im[i + 1:i + 2, :]
            cr, ci = alr * cr + ali * ci + er, alr * ci - ali * cr + ei
            ini_re[i:i + 1, :] = cr
            ini_im[i:i + 1, :] = ci

        for j in range(SSM_JB):
            cols = _scan_cols(j)
            ar, ai = _bcast8(are_ref[:, cols]), _bcast8(aim_ref[:, cols])

            def step2(t, s, cols=cols, ar=ar, ai=ai):
                sr, si = s
                rows = _rows8(L - 1 - t)
                nr = ar * sr + ai * si + lre[rows, cols]
                ni = ar * si - ai * sr + lim[rows, cols]
                lre[rows, cols] = nr
                lim[rows, cols] = ni
                return nr, ni

            lax.fori_loop(0, L, step2, (ini_re[:, cols], ini_im[:, cols]), unroll=4)

        car_re[...] = lre[0:1, :]
        car_im[...] = lim[0:1, :]

        head, tail, body_rows = slice(0, SUBLANES), slice(SUBLANES, T), slice(0, T - SUBLANES)
        for j in range(SSM_JB):
            cols = _scan_cols(j)
            ch = slice(j * LANES, (j + 1) * LANES)
            lr, li = lre[:, cols], lim[:, cols]
            dar_ref[:, cols] += (_colsum(lre[tail, cols] * sre_ref[body_rows, cols] + lim[tail, cols] * sim_ref[body_rows, cols])
                                 + _colsum(lre[head, cols] * ire_ref[:, cols] + lim[head, cols] * iim_ref[:, cols]))
            dai_ref[:, cols] += (_colsum(lim[tail, cols] * sre_ref[body_rows, cols] - lre[tail, cols] * sim_ref[body_rows, cols])
                                 + _colsum(lim[head, cols] * ire_ref[:, cols] - lre[head, cols] * iim_ref[:, cols]))
            uf = u_ref[:, ch]
            ub = uf.astype(_MXU)
            bur = jnp.dot(ub, bre_ref[j], preferred_element_type=F32)
            bui = jnp.dot(ub, bim_ref[j], preferred_element_type=F32)
            dcfr_ref[:, cols] += _colsum(lr * bur + li * bui)
            dcfi_ref[:, cols] += _colsum(li * bur - lr * bui)
            cfr, cfi = cfr_ref[:, cols], cfi_ref[:, cols]
            dbur = (cfr * lr + cfi * li).astype(_MXU)
            dbui = (cfr * li - cfi * lr).astype(_MXU)
            dyf = dy_ref[:, ch]
            dyb = dyf.astype(_MXU)
            du_ref[:, ch] = (lax.dot_general(dbur, bre_ref[j], _NT, preferred_element_type=F32)
                             + lax.dot_general(dbui, bim_ref[j], _NT, preferred_element_type=F32)
                             + d_ref[:, ch] * dyf)
            dbre_ref[j] += lax.dot_general(ub, dbur, _TN, preferred_element_type=F32)
            dbim_ref[j] += lax.dot_general(ub, dbui, _TN, preferred_element_type=F32)
            dcre_ref[j] += lax.dot_general(sre_ref[:, cols].astype(_MXU), dyb, _TN, preferred_element_type=F32)
            dcim_ref[j] -= lax.dot_general(sim_ref[:, cols].astype(_MXU), dyb, _TN, preferred_element_type=F32)
            dd_ref[:, ch] += _colsum(dyf * uf)

        @pl.when(step == nc - 1)
        def _():
            for acc, out in ((dbre_ref, dbre_out), (dbim_ref, dbim_out), (dcre_ref, dcre_out), (dcim_ref, dcim_out)):
                pltpu.sync_copy(acc, out)

    tok = pl.BlockSpec((T, SSM_W), lambda c: (nc - 1 - c, 0))
    st = pl.BlockSpec((T, N_STATES), lambda c: (nc - 1 - c, 0))
    ini = pl.BlockSpec((None, SUBLANES, N_STATES), lambda c: (nc - 1 - c, 0, 0))
    bsp = pl.BlockSpec((SSM_JB, LANES, SSM_SB), lambda c: (0, 0, 0))
    csp = pl.BlockSpec((SSM_JB, SSM_SB, LANES), lambda c: (0, 0, 0))
    row_w = pl.BlockSpec((1, SSM_W), lambda c: (0, 0))
    row_s = pl.BlockSpec((1, N_STATES), lambda c: (0, 0))
    big = pltpu.VMEM((T, N_STATES), F32)
    one = pltpu.VMEM((1, N_STATES), F32)
    eight = pltpu.VMEM((SUBLANES, N_STATES), F32)
    return pl.pallas_call(
        body, name="ssm_bwd", grid=(nc,),
        in_specs=[tok, tok, st, st, ini, ini, bsp, bsp, csp, csp, row_w] + [row_s] * 6,
        out_specs=[tok, _ANY, _ANY, _ANY, _ANY, row_w, row_s, row_s, row_s, row_s],
        out_shape=[jax.ShapeDtypeStruct((seq, SSM_W), F32),
                   jax.ShapeDtypeStruct((SSM_JB, LANES, SSM_SB), F32), jax.ShapeDtypeStruct((SSM_JB, LANES, SSM_SB), F32),
                   jax.ShapeDtypeStruct((SSM_JB, SSM_SB, LANES), F32), jax.ShapeDtypeStruct((SSM_JB, SSM_SB, LANES), F32),
                   jax.ShapeDtypeStruct((1, SSM_W), F32)] + [jax.ShapeDtypeStruct((1, N_STATES), F32)] * 4,
        scratch_shapes=[big, big, one, one, eight, eight, eight, eight,
                        pltpu.VMEM((SSM_JB, LANES, SSM_SB), F32), pltpu.VMEM((SSM_JB, LANES, SSM_SB), F32),
                        pltpu.VMEM((SSM_JB, SSM_SB, LANES), F32), pltpu.VMEM((SSM_JB, SSM_SB, LANES), F32)],
        compiler_params=_params(("arbitrary",)),
    )(dy, u, s_re, s_im, i_re, i_im, b_re, b_im, c_re, c_im, d_skip, *coef)


def _block_diag_b(b):
    t = b.reshape(SSM_JB, 8, STATE, GROUP).transpose(0, 1, 3, 2)
    eye = jnp.eye(8, dtype=b.dtype)
    return (t[:, :, :, None, :] * eye[None, :, None, :, None]).reshape(SSM_JB, LANES, SSM_SB)


def _block_diag_c(c):
    t = c.reshape(SSM_JB, 8, GROUP, STATE).transpose(0, 1, 3, 2)
    eye = jnp.eye(8, dtype=c.dtype)
    return (t[:, :, :, None, :] * eye[None, :, None, :, None]).reshape(SSM_JB, SSM_SB, LANES)


def _diag_of_b(blk):
    t = blk.reshape(SSM_JB, 8, GROUP, 8, STATE)
    d = jnp.sum(t * jnp.eye(8, dtype=blk.dtype)[None, :, None, :, None], axis=3)
    return d.transpose(0, 1, 3, 2).reshape(N_GROUPS, STATE, GROUP)


def _diag_of_c(blk):
    t = blk.reshape(SSM_JB, 8, STATE, 8, GROUP)
    d = jnp.sum(t * jnp.eye(8, dtype=blk.dtype)[None, :, None, :, None], axis=3)
    return d.transpose(0, 1, 3, 2).reshape(N_GROUPS, GROUP, STATE)


def _to_scan_order(v):
    seq, w = v.shape
    return v.reshape(seq // SSM_T, SUBLANES, SSM_L, w).transpose(0, 2, 1, 3).reshape(seq, w)


def _from_scan_order(v):
    seq, w = v.shape
    return v.reshape(seq // SSM_T, SSM_L, SUBLANES, w).transpose(0, 2, 1, 3).reshape(seq, w)


def _adamw_math(w, g, m, v):
    nm = ADAM_B1 * m + (1.0 - ADAM_B1) * g
    nv = ADAM_B2 * v + (1.0 - ADAM_B2) * jnp.square(g)
    m_hat = nm / (1.0 - ADAM_B1 ** ADAM_STEP)
    v_hat = nv / (1.0 - ADAM_B2 ** ADAM_STEP)
    return -ADAM_LR * (m_hat / (jnp.sqrt(v_hat) + ADAM_EPS) + ADAM_WD * w), nm, nv


def _adamw(w, g, m, v, *, name, tm, deps=()):
    rows, cols = w.shape
    nd = len(deps)

    def body(w_ref, g_ref, m_ref, v_ref, *rest):
        d_ref, nm_ref, nv_ref = rest[nd:]
        d_ref[...], nm_ref[...], nv_ref[...] = _adamw_math(w_ref[...], g_ref[...], m_ref[...], v_ref[...])

    spec = pl.BlockSpec((tm, cols), lambda i: (i, 0))
    shp = jax.ShapeDtypeStruct((rows, cols), F32)
    return pl.pallas_call(body, name=name, grid=(rows // tm,), in_specs=[spec] * 4 + [_ANY] * nd,
                          out_specs=[spec] * 3, out_shape=[shp] * 3,
                          compiler_params=_params(("arbitrary",)))(w, g, m, v, *deps)


def _place():
    x, y, c = lax.axis_index("x"), lax.axis_index("y"), lax.axis_index("c")
    chips = [(1 - x, y), (x, 1 - y), (1 - x, 1 - y)]
    return x, y, c, chips


def _remote(src, dst, send_sem, recv_sem, dev):
    return pltpu.make_async_remote_copy(src_ref=src, dst_ref=dst, send_sem=send_sem, recv_sem=recv_sem,
                                        device_id=dev, device_id_type=MESH)


def _place_shard(w, mine_arr, *, name, tm=256):
    rows, cols = w.shape

    def body(m_ref, w_ref, o_ref):
        o_ref[...] = w_ref[...].astype(o_ref.dtype)

    return pl.pallas_call(
        body, name=name,
        grid_spec=pltpu.PrefetchScalarGridSpec(
            num_scalar_prefetch=1, grid=(rows // tm,),
            in_specs=[pl.BlockSpec((tm, cols), lambda i, m: (i, 0))],
            out_specs=pl.BlockSpec((None, tm, cols), lambda i, m: (m[0], i, 0))),
        out_shape=jax.ShapeDtypeStruct((N_CHIPS, rows, cols), _WIRE),
        compiler_params=_params(("arbitrary",)),
    )(mine_arr, w)


_HBM = pl.BlockSpec(memory_space=pltpu.HBM)
_SEM = pl.BlockSpec(memory_space=pltpu.SEMAPHORE)
_EFFECT = pltpu.SideEffectType.DATAFLOW_SIDE_EFFECTING


def _copies_start(name, bufs, plan, count, after=()):
    nb, na = len(bufs), len(after)

    def body(*refs):
        send_sems, recv_sems, token = refs[nb + na], refs[nb + na + 1], refs[-1]
        copies = plan(refs[:nb])
        assert len(copies) == count
        for i, (src, dst, dev, _) in enumerate(copies):
            _remote(src, dst, send_sems.at[i], recv_sems.at[i], dev).start()
        token[...] = jnp.zeros_like(token)

    res = pl.pallas_call(
        body, name=name, in_specs=[_HBM] * nb + [_ANY] * na,
        out_specs=(_SEM, _SEM, *[_HBM] * nb, pl.BlockSpec(memory_space=pltpu.VMEM)),
        out_shape=(pltpu.SemaphoreType.DMA((count,)), pltpu.SemaphoreType.DMA((count,)),
                   *[pltpu.HBM(b.shape, b.dtype) for b in bufs], jax.ShapeDtypeStruct((SUBLANES, LANES), F32)),
        input_output_aliases={i: 2 + i for i in range(nb)},
        compiler_params=pltpu.CompilerParams(has_side_effects=_EFFECT),
    )(*[pltpu.with_memory_space_constraint(b, pltpu.HBM) for b in bufs], *after)
    return (res[0], res[1]), list(res[2:2 + nb]), res[-1]


def _copies_wait(name, bufs, sems, plan, after=()):
    nb, na = len(bufs), len(after)

    def body(*refs):
        send_sems, recv_sems = refs[nb], refs[nb + 1]
        for i, (src, _, dev, land) in enumerate(plan(refs[:nb])):
            cp = _remote(src, land, send_sems.at[i], recv_sems.at[i], dev)
            cp.wait_send()
            cp.wait_recv()

    res = pl.pallas_call(
        body, name=name, in_specs=[_HBM] * nb + [_SEM, _SEM] + [_ANY] * na, out_specs=[_HBM] * nb,
        out_shape=[pltpu.HBM(b.shape, b.dtype) for b in bufs],
        input_output_aliases={i: i for i in range(nb)},
        compiler_params=pltpu.CompilerParams(has_side_effects=_EFFECT),
    )(*bufs, *sems, *after)
    return list(res)


def _plan_gather_ici(fulls):
    x, y, c, chips = _place()
    copies = []
    for f in fulls:
        half = pl.ds(c * (f.shape[1] // 2), f.shape[1] // 2)
        own = f.at[2 * x + y, half]
        for chip in chips:
            copies.append((own, own, (*chip, c), f.at[2 * chip[0] + chip[1], half]))
    return copies


def _plan_gather_d2d(fulls):
    x, y, c, chips = _place()
    copies = []
    for f in fulls:
        r2 = f.shape[1] // 2
        for chip in chips:
            blk = 2 * chip[0] + chip[1]
            landed = f.at[blk, pl.ds(c * r2, r2)]
            copies.append((landed, landed, (x, y, 1 - c), f.at[blk, pl.ds((1 - c) * r2, r2)]))
    return copies


def _plan_swap_halves(refs):
    x, y, c, _ = _place()
    n = len(refs) // 2
    copies = []
    for g, land in zip(refs[:n], refs[n:]):
        r2 = g.shape[1] // 2
        copies.append((g.at[:, pl.ds((1 - c) * r2, r2), :], land, (x, y, 1 - c), land))
    return copies


def _plan_scatter_chips(refs):
    x, y, c, chips = _place()
    n = len(refs) // 2
    copies = []
    for h, land in zip(refs[:n], refs[n:]):
        for k, chip in enumerate(chips):
            copies.append((h.at[2 * chip[0] + chip[1]], land.at[k], (*chip, c), land.at[k]))
    return copies


def _plan_join_halves(totals):
    x, y, c, _ = _place()
    copies = []
    for t in totals:
        r2 = t.shape[0] // 2
        mine = t.at[pl.ds(c * r2, r2)]
        copies.append((mine, mine, (x, y, 1 - c), t.at[pl.ds((1 - c) * r2, r2)]))
    return copies


def _add_sibling_half(g, got, c_arr, *, name, tm):
    _, rows, cols = g.shape
    r2 = rows // 2
    nb = r2 // tm

    def body(c_ref, g_ref, r_ref, o_ref):
        o_ref[...] = (g_ref[...].astype(F32) + r_ref[...].astype(F32)).astype(o_ref.dtype)

    return pl.pallas_call(
        body, name=name,
        grid_spec=pltpu.PrefetchScalarGridSpec(
            num_scalar_prefetch=1, grid=(N_CHIPS, nb),
            in_specs=[pl.BlockSpec((None, tm, cols), lambda b, i, c: (b, c[0] * nb + i, 0)),
                      pl.BlockSpec((None, tm, cols), lambda b, i, c: (b, i, 0))],
            out_specs=pl.BlockSpec((None, tm, cols), lambda b, i, c: (b, i, 0))),
        out_shape=jax.ShapeDtypeStruct((N_CHIPS, r2, cols), _WIRE),
        compiler_params=_params(("arbitrary", "arbitrary")),
    )(c_arr, g, got)


def _add_chips(h, got, place_arr, *, name, tm):
    _, r2, cols = h.shape
    nb = r2 // tm

    def body(p_ref, h_ref, r_ref, o_ref):
        o_ref[...] = ((h_ref[...].astype(F32) + r_ref[0].astype(F32)) + r_ref[1].astype(F32)) + r_ref[2].astype(F32)

    return pl.pallas_call(
        body, name=name,
        grid_spec=pltpu.PrefetchScalarGridSpec(
            num_scalar_prefetch=1, grid=(nb,),
            in_specs=[pl.BlockSpec((None, tm, cols), lambda i, p: (p[0], i, 0)),
                      pl.BlockSpec((3, tm, cols), lambda i, p: (0, i, 0))],
            out_specs=pl.BlockSpec((tm, cols), lambda i, p: (p[1] * nb + i, 0))),
        out_shape=jax.ShapeDtypeStruct((2 * r2, cols), F32),
        compiler_params=_params(("arbitrary",)),
    )(place_arr, h, got)


class _ReduceScatter:
    def __init__(self, tag, names, grads):
        self.tag, self.names, self.n = tag, names, len(names)
        core = lax.axis_index("c").astype(jnp.int32)
        chip = (2 * lax.axis_index("x") + lax.axis_index("y")).astype(jnp.int32)
        self.c_arr, self.place_arr = core.reshape(1), jnp.stack([chip, core])
        self.bufs = list(grads)

    def _start(self, step, bufs, plan, count, after):
        self.plan = plan
        self.step = f"grad_{step}_{self.tag}"
        self.sems, self.bufs, token = _copies_start(self.step + "_start", bufs, plan, count, after)
        return [token]

    def _wait(self, after):
        self.bufs = _copies_wait(self.step + "_wait", self.bufs, self.sems, self.plan, after)
        return self.bufs

    def start_swap(self, after=()):
        lands = [lax.empty((N_CHIPS, g.shape[1] // 2, g.shape[2]), g.dtype) for g in self.bufs]
        return self._start("swap", self.bufs + lands, _plan_swap_halves, self.n, after)

    def start_scatter(self, after):
        bufs = self._wait(after)
        pair = [_add_sibling_half(g, r, self.c_arr, name=f"grad_add_sibling_{nm}", tm=min(256, g.shape[1] // 2))
                for nm, g, r in zip(self.names, bufs[:self.n], bufs[self.n:])]
        lands = [lax.empty((3,) + h.shape[1:], h.dtype) for h in pair]
        return self._start("scatter", pair + lands, _plan_scatter_chips, 3 * self.n, ())

    def start_join(self, after):
        bufs = self._wait(after)
        total = [_add_chips(h, r, self.place_arr, name=f"grad_add_chips_{nm}", tm=min(256, h.shape[1]))
                 for nm, h, r in zip(self.names, bufs[:self.n], bufs[self.n:])]
        return self._start("join", total, _plan_join_halves, self.n, ())

    def finish(self, after):
        return dict(zip(self.names, self._wait(after)))


def _all_gather_small(v):
    m_per, n = v.shape

    def body(x_ref, out_ref, send_sems, recv_sems, local_sem):
        x, y, c, chips = _place()
        me, sibling = (x, y, c), (x, y, 1 - c)

        def rows(px, py, pc):
            return out_ref.at[4 * px + 2 * py + pc]

        def copy(k, block, to, src=None):
            return _remote(rows(*block) if src is None else src, rows(*block), send_sems.at[k], recv_sems.at[k], to)

        mine = pltpu.make_async_copy(x_ref, rows(*me), local_sem)
        mine.start()
        first = [copy(0, me, sibling, src=x_ref)]
        first += [copy(1 + j, me, (*chip, c), src=x_ref) for j, chip in enumerate(chips)]
        for cp in first:
            cp.start()
        passed = [copy(4 + j, (*chip, c), sibling) for j, chip in enumerate(chips)]
        for j, chip in enumerate(chips):
            copy(1 + j, (*chip, c), me).wait_recv()
            passed[j].start()
        copy(0, sibling, me).wait_recv()
        for j, chip in enumerate(chips):
            copy(4 + j, (*chip, 1 - c), me).wait_recv()
        for cp in first + passed:
            cp.wait_send()
        mine.wait()

    return pl.pallas_call(
        body, name="gather_small_grads",
        out_shape=jax.ShapeDtypeStruct((8, m_per, n), v.dtype),
        in_specs=[pl.BlockSpec(memory_space=pltpu.VMEM)], out_specs=pl.BlockSpec(memory_space=pltpu.VMEM),
        scratch_shapes=[pltpu.SemaphoreType.DMA((7,)), pltpu.SemaphoreType.DMA((7,)), pltpu.SemaphoreType.DMA],
        compiler_params=pltpu.CompilerParams(vmem_limit_bytes=VMEM_LIMIT),
    )(v)


def _sum8(v, *, name):
    _, m, n = v.shape

    def body(v_ref, o_ref):
        acc = v_ref[0]
        for d in range(1, 8):
            acc = acc + v_ref[d]
        o_ref[...] = acc

    return pl.pallas_call(body, name=name, out_shape=jax.ShapeDtypeStruct((m, n), F32),
                          compiler_params=pltpu.CompilerParams(vmem_limit_bytes=VMEM_LIMIT))(v)


def _local_step(x, target, norm_w, q_norm_w, k_norm_w, sinks, a_re, a_im, log_dt, b_re, b_im, c_re, c_im, d_skip,
                b_glu, io):
    seq = x.shape[0]
    qw2 = jnp.tile(q_norm_w.reshape(1, HEAD_DIM), (1, HEADS_PER_TILE))
    kw2 = jnp.tile(k_norm_w.reshape(1, HEAD_DIM), (1, HEADS_PER_TILE))
    nw, bg = norm_w.reshape(1, D_MODEL), b_glu.reshape(1, D_MODEL)
    dsk = d_skip.reshape(1, SSM_W)

    h, rstd = _rms_fwd(x, nw, deps=io.begin())
    w_in4 = io.weight("w_in", h)
    proj = _mm(h, w_in4, mode="nn", name="mm_proj", tm=512, tn=IN_W // 4, tk=D_MODEL, b_blocked=True)
    attn, lse = _attn2_fwd(proj, qw2, kw2, sinks, deps=io.after_proj(proj))

    def gate_a(at, ag):
        return (at * (ag * _sigmoid(ag)),)

    (ya_in,) = _ew(gate_a, [(attn, "mat", 0), (proj, "mat", OFF_AGATE)], [(ATTN_W, _MXU)], rows=seq, ncol=2,
                   name="ew_attn_gate")
    w_ap4 = io.weight("w_attn_proj", ya_in)
    w_glu4, w_sp4, w_out = io.weight("w_glu", ya_in), io.weight("w_ssm_proj", ya_in), io.weight("w_out", ya_in)
    y_a = _mm(ya_in, w_ap4, mode="nn", name="mm_attn_proj", tm=2048, tn=512, tk=ATTN_W, b_blocked=True)

    flat_a = (a_re.reshape(1, N_STATES), a_im.reshape(1, N_STATES), jnp.repeat(log_dt, STATE).reshape(1, N_STATES))
    coef = _ssm_params_fwd(*flat_a)
    bre_blk, bim_blk = _block_diag_b(b_re).astype(_MXU), _block_diag_b(b_im).astype(_MXU)
    cre_blk, cim_blk = _block_diag_c(c_re).astype(_MXU), _block_diag_c(c_im).astype(_MXU)
    u_scan = _to_scan_order(proj[:, OFF_U * CW:OFF_U * CW + SSM_W])
    y_scan, s_re, s_im, i_re, i_im = _ssm_fwd(u_scan, bre_blk, bim_blk, cre_blk, cim_blk, dsk, coef)
    y_ssm = _from_scan_order(y_scan)

    (yg,) = _ew(lambda yv: (jax.nn.gelu(yv),), [(y_ssm, "mat", 0)], [(SSM_W, _MXU)], rows=seq, ncol=2, name="ew_gelu")
    glu = _mm(yg, w_glu4, mode="nn", name="mm_glu", tm=2048, tn=512, tk=SSM_W, b_blocked=True)

    def gate_s(ga, gb, ba, bb, z):
        return ((ga + ba) * _sigmoid(gb + bb) * (z * _sigmoid(z)),)

    (ys_in,) = _ew(gate_s, [(glu, "mat", 0), (glu, "mat", 2), (bg, "row", 0), (bg, "row", 2), (proj, "mat", OFF_Z)],
                   [(SSM_W, _MXU)], rows=seq, ncol=2, name="ew_ssm_gate")
    y_s = _mm(ys_in, w_sp4, mode="nn", name="mm_ssm_proj", tm=2048, tn=512, tk=SSM_W, b_blocked=True)

    def merge(ga, gs, ya, ys):
        return (_sigmoid(ga) * ya + _sigmoid(gs) * ys,)

    (merged,) = _ew(merge, [(proj, "mat", OFF_GA), (proj, "mat", OFF_GS), (y_a, "mat", 0), (y_s, "mat", 0)],
                    [(D_MODEL, _MXU)], rows=seq, ncol=4, name="ew_merge")
    mo = _mm(merged, w_out, mode="nn", name="mm_out", tm=512, tn=D_MODEL, tk=D_MODEL)

    def loss_head(xv, mv, tv):
        err = (xv + mv) - tv
        dout = err * (1.0 / D_MODEL)
        return dout, dout, _colsum(err * err)

    dout, dout_b, sq = _ew(loss_head, [(x, "mat", 0), (mo, "mat", 0), (target, "mat", 0)],
                           [(D_MODEL, F32), (D_MODEL, _MXU)], rows=seq, ncol=4, n_acc=1, name="ew_loss")
    loss = 0.5 * jnp.sum(sq) / D_MODEL

    d_merged = _mm(dout_b, w_out, mode="nt", name="mm_d_merged", tm=512, tn=D_MODEL, tk=D_MODEL)
    g_w_out = _mm(merged, dout_b, mode="tn", name="mm_g_w_out", tm=1024, tn=D_MODEL, tk=1024, out_dtype=_WIRE)

    def merge_bwd(dm, ga, gs, ya, ys):
        sa, ss = _sigmoid(ga), _sigmoid(gs)
        return sa * dm, ss * dm, dm * ya * sa * (1.0 - sa), dm * ys * ss * (1.0 - ss)

    d_ya, d_ys, d_ga, d_gs = _ew(
        merge_bwd, [(d_merged, "mat", 0), (proj, "mat", OFF_GA), (proj, "mat", OFF_GS), (y_a, "mat", 0), (y_s, "mat", 0)],
        [(D_MODEL, _MXU)] * 4, rows=seq, ncol=4, name="ew_merge_bwd")

    d_ya_in = _mm(d_ya, w_ap4, mode="nt", name="mm_d_attn_gate", tm=2048, tn=ATTN_W, tk=512, b_blocked=True)
    g_w_ap = _mm(ya_in, d_ya, mode="tn", name="mm_g_w_attn_proj", tm=ATTN_W, tn=512, tk=2048, out_dtype=_WIRE,
                 out_blocked=True)

    d_ys_in = _mm(d_ys, w_sp4, mode="nt", name="mm_d_ssm_gate", tm=2048, tn=SSM_W, tk=512, b_blocked=True)
    g_w_sp = _mm(ys_in, d_ys, mode="tn", name="mm_g_w_ssm_proj", tm=SSM_W, tn=512, tk=2048, out_dtype=_WIRE,
                 out_blocked=True)

    def gate_s_bwd(dv, ga, gb, ba, bb, z):
        a, sb = ga + ba, _sigmoid(gb + bb)
        f, df = _silu_and_grad(z)
        dga = dv * sb * f
        dgb = dv * a * f * sb * (1.0 - sb)
        return dga, dgb, dv * a * sb * df, _colsum(dga), _colsum(dgb)

    d_glu_a, d_glu_b, d_z, g_bga, g_bgb = _ew(
        gate_s_bwd, [(d_ys_in, "mat", 0), (glu, "mat", 0), (glu, "mat", 2), (bg, "row", 0), (bg, "row", 2),
                     (proj, "mat", OFF_Z)],
        [(SSM_W, _MXU)] * 3, rows=seq, ncol=2, n_acc=2, name="ew_ssm_gate_bwd")
    d_glu = jnp.concatenate([d_glu_a, d_glu_b], axis=1)
    d_yg = _mm(d_glu, w_glu4, mode="nt", name="mm_d_gelu", tm=2048, tn=SSM_W, tk=512, b_blocked=True)
    g_w_glu = _mm(yg, d_glu, mode="tn", name="mm_g_w_glu", tm=SSM_W, tn=512, tk=2048, out_dtype=_WIRE, out_blocked=True)
    dep = io.later_grads(dict(w_attn_proj=g_w_ap, w_glu=g_w_glu, w_ssm_proj=g_w_sp,
                              w_out=g_w_out.reshape(N_CHIPS, D_MODEL // N_CHIPS, D_MODEL)))

    def gate_a_bwd(dv, at, ag):
        f, df = _silu_and_grad(ag)
        return dv * f, dv * at * df

    d_attn, d_agate = _ew(gate_a_bwd, [(d_ya_in, "mat", 0), (attn, "mat", 0), (proj, "mat", OFF_AGATE)],
                          [(ATTN_W, F32), (ATTN_W, _MXU)], rows=seq, ncol=2, name="ew_attn_gate_bwd", deps=dep)

    def gelu_bwd(dv, yv):
        return (jax.vjp(jax.nn.gelu, yv)[1](dv)[0],)

    (d_yssm,) = _ew(gelu_bwd, [(d_yg, "mat", 0), (y_ssm, "mat", 0)], [(SSM_W, F32)], rows=seq, ncol=2, name="ew_gelu_bwd",
                    deps=dep)
    dep = io.before_attention_backward([d_attn, d_yssm])
    d_qkv, g_qw2, g_kw2, g_sk = _attn2_bwd(proj, qw2, kw2, sinks, lse, d_attn, deps=dep)
    (du_scan, g_bre, g_bim, g_cre, g_cim, g_dsk, g_abr, g_abi, g_cfr, g_cfi) = _ssm_bwd(
        _to_scan_order(d_yssm), u_scan, s_re, s_im, i_re, i_im, bre_blk, bim_blk, cre_blk, cim_blk, dsk, coef)
    g_are, g_aim, g_ldt = _ssm_params_bwd(*flat_a, g_abr, g_abi, g_cfr, g_cfi)
    g_are, g_aim = g_are.reshape(N_GROUPS, STATE), g_aim.reshape(N_GROUPS, STATE)
    g_ldt = g_ldt.reshape(N_GROUPS, STATE).sum(axis=1)
    d_u = _from_scan_order(du_scan)

    d_proj = jnp.concatenate([d_qkv, d_agate, d_u.astype(_MXU), d_z, d_ga, d_gs], axis=1)
    dep = io.before_input_projection_grad([d_proj]) + io.small_grads(dict(
        q_norm_w=g_qw2[0, :HEAD_DIM] + g_qw2[0, HEAD_DIM:], k_norm_w=g_kw2[0, :HEAD_DIM] + g_kw2[0, HEAD_DIM:],
        sinks=g_sk.reshape(N_Q_HEADS), A_re=g_are, A_im=g_aim, log_dt=g_ldt,
        B_re=_diag_of_b(g_bre), B_im=_diag_of_b(g_bim), C_re=_diag_of_c(g_cre), C_im=_diag_of_c(g_cim),
        D_skip=g_dsk.reshape(N_GROUPS, GROUP), b_glu=jnp.concatenate([g_bga, g_bgb], axis=1).reshape(D_MODEL)))
    g_w_in = _mm(h, d_proj, mode="tn", name="mm_g_w_in", tm=1024, tn=IN_W // 4, tk=1024, out_dtype=_WIRE,
                 out_blocked=True, deps=dep)
    dep = io.input_projection_grad(g_w_in)
    d_h = _mm(d_proj, w_in4, mode="nt", name="mm_d_h", tm=512, tn=D_MODEL, tk=IN_W // 4, b_blocked=True, deps=dep)
    grad_x, g_nw = _rms_bwd(d_h, x, rstd, nw, dout)
    return loss, grad_x, g_nw.reshape(D_MODEL)


_SMALL = ["norm_w", "q_norm_w", "k_norm_w", "sinks", "A_re", "A_im", "log_dt", "B_re", "B_im", "C_re", "C_im",
          "D_skip", "b_glu"]
_BIG = ["w_in", "w_attn_proj", "w_glu", "w_ssm_proj", "w_out"]
_LATER = _BIG[1:]
_ORDER = ["norm_w", "w_in", "q_norm_w", "k_norm_w", "sinks", "w_attn_proj", "A_re", "A_im", "log_dt", "B_re", "B_im",
          "C_re", "C_im", "D_skip", "w_glu", "b_glu", "w_ssm_proj", "w_out"]
_PACK_W = 1024


def _packed_rows(size):
    unit = SUBLANES * _PACK_W
    return -(-size // unit) * SUBLANES


def _pack_small(d, names):
    parts = []
    for n in names:
        flat = d[n].reshape(-1).astype(F32)
        rows = _packed_rows(flat.shape[0])
        parts.append(jnp.pad(flat, (0, rows * _PACK_W - flat.shape[0])).reshape(rows, _PACK_W))
    return jnp.concatenate(parts, axis=0)


def _unpack_small(packed, like, names):
    out, pos = {}, 0
    for n in names:
        rows = _packed_rows(like[n].size)
        out[n] = packed[pos:pos + rows].reshape(-1)[:like[n].size].reshape(like[n].shape)
        pos += rows
    return out


def _place_block(v, index_arr, *, name):
    rows, cols = v.shape

    def body(i_ref, v_ref, o_ref):
        o_ref[...] = v_ref[...]

    return pl.pallas_call(
        body, name=name,
        grid_spec=pltpu.PrefetchScalarGridSpec(
            num_scalar_prefetch=1, grid=(1,),
            in_specs=[pl.BlockSpec((rows, cols), lambda i, d: (0, 0))],
            out_specs=pl.BlockSpec((None, rows, cols), lambda i, d: (d[0], 0, 0))),
        out_shape=jax.ShapeDtypeStruct((8, rows, cols), v.dtype),
        compiler_params=_params(("arbitrary",)),
    )(index_arr, v)


def _plan_all_to_all(refs):
    (land,) = refs
    x, y, c, _ = _place()
    own = land.at[4 * x + 2 * y + c]
    copies = []
    for fx, fy, fc in [(0, 0, 1), (0, 1, 0), (0, 1, 1), (1, 0, 0), (1, 0, 1), (1, 1, 0), (1, 1, 1)]:
        px, py, pc = (1 - x) if fx else x, (1 - y) if fy else y, (1 - c) if fc else c
        copies.append((own, own, (px, py, pc), land.at[4 * px + 2 * py + pc]))
    return copies


def _as2d(a):
    return a.reshape(1, -1) if a.ndim == 1 else a


def _adamw_whole(w, g, m, v, *, name):
    shape = w.shape
    w, g, m, v = _as2d(w), _as2d(g), _as2d(m), _as2d(v)

    def body(w_ref, g_ref, m_ref, v_ref, d_ref, nm_ref, nv_ref):
        d_ref[...], nm_ref[...], nv_ref[...] = _adamw_math(w_ref[...], g_ref[...], m_ref[...], v_ref[...])

    outs = pl.pallas_call(body, name=name, out_shape=[jax.ShapeDtypeStruct(w.shape, F32)] * 3)(w, g, m, v)
    return [o.reshape(shape) for o in outs]


class _Exchanges:
    def __init__(self, w, m, v):
        self.w, self.m, self.v = w, m, v
        self.grads, self.delta, self.new_m, self.new_v = {}, {}, {}, {}

    def _adamw(self, names, deps):
        for n in names:
            self.delta[n], self.new_m[n], self.new_v[n] = _adamw(
                self.w[n], self.grads[n], self.m[n], self.v[n], name=f"adamw_{n}", tm=128, deps=deps)

    def begin(self):
        chip = (2 * lax.axis_index("x") + lax.axis_index("y")).astype(jnp.int32).reshape(1)
        full = {n: _place_shard(self.w[n], chip, name=f"place_{n}") for n in _BIG}
        self.first = _copies_start("gather_ici_w_in_start", [full["w_in"]], _plan_gather_ici, 3)
        self.rest = _copies_start("gather_ici_rest_start", [full[n] for n in _LATER], _plan_gather_ici,
                                  3 * len(_LATER), after=[self.first[2]])
        return [self.rest[2]]

    def weight(self, name, after):
        if name == "w_in":
            sems, bufs, _ = self.first
            bufs = _copies_wait("gather_ici_w_in_wait", bufs, sems, _plan_gather_ici, [after])
            sems, bufs, token = _copies_start("gather_d2d_w_in_start", bufs, _plan_gather_d2d, 3)
            return _copies_wait("gather_d2d_w_in_wait", bufs, sems, _plan_gather_d2d, [token])[0]
        if self.rest is not None:
            sems, bufs = self.rest
            later = dict(zip(_LATER, _copies_wait("gather_d2d_rest_wait", bufs, sems, _plan_gather_d2d, [after])))
            later["w_out"] = later["w_out"].reshape(D_MODEL, D_MODEL)
            self.later, self.rest = later, None
        return self.later[name]

    def after_proj(self, proj):
        sems, bufs, _ = self.rest
        bufs = _copies_wait("gather_ici_rest_wait", bufs, sems, _plan_gather_ici, [proj])
        sems, bufs, token = _copies_start("gather_d2d_rest_start", bufs, _plan_gather_d2d, 3 * len(_LATER))
        self.rest = (sems, bufs)
        return [token]

    def later_grads(self, grads):
        self.rs_later = _ReduceScatter("later", _LATER, [grads[n] for n in _LATER])
        return self.rs_later.start_swap()

    def before_attention_backward(self, after):
        return self.rs_later.start_scatter(after)

    def before_input_projection_grad(self, after):
        return self.rs_later.start_join(after)

    def input_projection_grad(self, g_w_in):
        self.grads.update(self.rs_later.finish([g_w_in]))
        self.rs_in = _ReduceScatter("w_in", ["w_in"], [g_w_in])
        self._adamw(_LATER, self.rs_in.start_swap())
        return self.rs_in.start_scatter([self.delta[n] for n in _LATER])

    def _adamw_small(self, names):
        for n in names:
            self.delta[n], self.new_m[n], self.new_v[n] = _adamw_whole(
                self.w[n], self.grads[n], self.m[n], self.v[n], name=f"adamw_{n}")

    def small_grads(self, grads):
        me = (4 * lax.axis_index("x") + 2 * lax.axis_index("y") + lax.axis_index("c")).astype(jnp.int32).reshape(1)
        land = _place_block(_pack_small(grads, _SMALL[1:]), me, name="place_small_grads")
        self.small = _copies_start("gather_small_start", [land], _plan_all_to_all, 7)
        return [self.small[2]]

    def finish(self, g_norm_w, after):
        sems, bufs, _ = self.small
        (land,) = _copies_wait("gather_small_wait", bufs, sems, _plan_all_to_all, after)
        self.grads.update(_unpack_small(_sum8(land, name="sum_small_grads"), self.w, _SMALL[1:]))
        self._adamw_small(_SMALL[1:])
        late = _sum8(_all_gather_small(_pack_small(dict(norm_w=g_norm_w), _SMALL[:1])), name="sum_norm_w_grad")
        self.grads.update(_unpack_small(late, self.w, _SMALL[:1]))
        self._adamw_small(_SMALL[:1])
        self.grads.update(self.rs_in.finish(self.rs_in.start_join([self.delta[_SMALL[0]]])))
        self._adamw(["w_in"], ())


def kernel(x, norm_w, w_in, q_norm_w, k_norm_w, sinks, w_attn_proj, A_re, A_im, log_dt, B_re, B_im, C_re, C_im, D_skip, w_glu, b_glu, w_ssm_proj, w_out, loss_target, m_norm_w, m_w_in, m_q_norm_w, m_k_norm_w, m_sinks, m_w_attn_proj, m_A_re, m_A_im, m_log_dt, m_B_re, m_B_im, m_C_re, m_C_im, m_D_skip, m_w_glu, m_b_glu, m_w_ssm_proj, m_w_out, v_norm_w, v_w_in, v_q_norm_w, v_k_norm_w, v_sinks, v_w_attn_proj, v_A_re, v_A_im, v_log_dt, v_B_re, v_B_im, v_C_re, v_C_im, v_D_skip, v_w_glu, v_b_glu, v_w_ssm_proj, v_w_out):
    w = dict(norm_w=norm_w, w_in=w_in, q_norm_w=q_norm_w, k_norm_w=k_norm_w, sinks=sinks, w_attn_proj=w_attn_proj,
             A_re=A_re, A_im=A_im, log_dt=log_dt, B_re=B_re, B_im=B_im, C_re=C_re, C_im=C_im, D_skip=D_skip,
             w_glu=w_glu, b_glu=b_glu, w_ssm_proj=w_ssm_proj, w_out=w_out)
    m = dict(norm_w=m_norm_w, w_in=m_w_in, q_norm_w=m_q_norm_w, k_norm_w=m_k_norm_w, sinks=m_sinks,
             w_attn_proj=m_w_attn_proj, A_re=m_A_re, A_im=m_A_im, log_dt=m_log_dt, B_re=m_B_re, B_im=m_B_im,
             C_re=m_C_re, C_im=m_C_im, D_skip=m_D_skip, w_glu=m_w_glu, b_glu=m_b_glu, w_ssm_proj=m_w_ssm_proj,
             w_out=m_w_out)
    v = dict(norm_w=v_norm_w, w_in=v_w_in, q_norm_w=v_q_norm_w, k_norm_w=v_k_norm_w, sinks=v_sinks,
             w_attn_proj=v_w_attn_proj, A_re=v_A_re, A_im=v_A_im, log_dt=v_log_dt, B_re=v_B_re, B_im=v_B_im,
             C_re=v_C_re, C_im=v_C_im, D_skip=v_D_skip, w_glu=v_w_glu, b_glu=v_b_glu, w_ssm_proj=v_w_ssm_proj,
             w_out=v_w_out)

    io = _Exchanges(w, m, v)
    loss, grad_x, g_norm_w = _local_step(x[0], loss_target[0], norm_w, q_norm_w, k_norm_w, sinks, A_re, A_im, log_dt,
                                         B_re, B_im, C_re, C_im, D_skip, b_glu, io)
    loss = lax.psum(loss, ("x", "y", "c"))
    io.finish(g_norm_w, [grad_x])
    grads, delta, new_m, new_v = io.grads, io.delta, io.new_m, io.new_v

    return (loss, grad_x[None], *[grads[n] for n in _ORDER], *[delta[n] for n in _ORDER],
            *[new_m[n] for n in _ORDER], *[new_v[n] for n in _ORDER])
```

```python
import functools
import math

import jax
import jax.numpy as jnp
from jax import lax
from jax.experimental import pallas as pl
from jax.experimental.pallas import tpu as pltpu

F32 = jnp.float32
_MXU = jnp.bfloat16
_WIRE = jnp.bfloat16

LANES = 128
SUBLANES = 8
VMEM_LIMIT = 56 * 1024 * 1024

D_MODEL = 2048
HEAD_DIM = 64
N_Q_HEADS = 16
N_KV_HEADS = 4
Q_PER_KV = 4
ATTN_W = 1024
KV_W = 256
WINDOW = 128
SSM_W = 1024
GROUP = 16
N_GROUPS = 64
STATE = 64
N_STATES = N_GROUPS * STATE
IN_W = 8704
NORM_EPS = 1e-6
N_CHIPS = 4
CW = 512
OFF_AGATE, OFF_U, OFF_Z, OFF_GA, OFF_GS = 3, 5, 7, 9, 13

SSM_T = 256
SSM_L = SSM_T // SUBLANES
SSM_JB = 8
SSM_SB = N_STATES // SSM_JB

ADAM_LR, ADAM_B1, ADAM_B2, ADAM_EPS, ADAM_WD, ADAM_STEP = 0.001, 0.9, 0.999, 1e-08, 0.01, 10

MESH = pl.DeviceIdType.MESH
_ANY = pl.BlockSpec(memory_space=pl.ANY)


def _params(sem=None):
    return pltpu.CompilerParams(dimension_semantics=sem, vmem_limit_bytes=VMEM_LIMIT)


def _mm(a, b, *, mode, name, tm, tn, tk, out_dtype=F32, b_blocked=False, out_blocked=False, deps=()):
    nd = len(deps)
    if mode == "tn":
        K, M = a.shape
    else:
        M, K = a.shape
    if mode == "nn":
        N = b.shape[0] * b.shape[2] if b_blocked else b.shape[1]
    elif mode == "nt":
        N = b.shape[1] if b_blocked else b.shape[0]
    else:
        N = b.shape[1]
    tm, tn, tk = min(tm, M), min(tn, N), min(tk, K)
    nj, ni, nk = N // tn, M // tm, K // tk
    assert nj * tn == N and ni * tm == M and nk * tk == K, (name, M, N, K)
    dims = {"nn": (((1,), (0,)), ((), ())), "nt": (((1,), (1,)), ((), ())), "tn": (((0,), (0,)), ((), ()))}[mode]

    if mode == "tn":
        a_spec = pl.BlockSpec((tk, tm), lambda j, i, k: (k, i))
    else:
        a_spec = pl.BlockSpec((tm, tk), lambda j, i, k: (i, k))
    if mode == "nn":
        if b_blocked:
            assert b.shape[0] == nj and b.shape[2] == tn
            b_spec = pl.BlockSpec((None, tk, tn), lambda j, i, k: (j, k, 0))
        else:
            b_spec = pl.BlockSpec((tk, tn), lambda j, i, k: (k, j))
    elif mode == "nt":
        if b_blocked:
            assert b.shape[0] == nk and b.shape[2] == tk
            b_spec = pl.BlockSpec((None, tn, tk), lambda j, i, k: (k, j, 0))
        else:
            b_spec = pl.BlockSpec((tn, tk), lambda j, i, k: (j, k))
    else:
        b_spec = pl.BlockSpec((tk, tn), lambda j, i, k: (k, j))
    if out_blocked:
        assert nj == N_CHIPS
        o_spec = pl.BlockSpec((None, tm, tn), lambda j, i, k: (j, i, 0))
        o_shape = jax.ShapeDtypeStruct((nj, M, tn), out_dtype)
    else:
        o_spec = pl.BlockSpec((tm, tn), lambda j, i, k: (i, j))
        o_shape = jax.ShapeDtypeStruct((M, N), out_dtype)
    use_acc = nk > 1 and out_dtype != F32

    def body(a_ref, b_ref, *rest):
        o_ref, scratch = rest[nd], rest[nd + 1:]
        part = lax.dot_general(a_ref[...].astype(_MXU), b_ref[...].astype(_MXU), dims,
                               preferred_element_type=F32)
        if nk == 1:
            o_ref[...] = part.astype(o_ref.dtype)
            return
        k = pl.program_id(2)
        acc = scratch[0] if use_acc else o_ref

        @pl.when(k == 0)
        def _():
            acc[...] = part

        @pl.when(k > 0)
        def _():
            acc[...] += part

        if use_acc:
            @pl.when(k == nk - 1)
            def _():
                o_ref[...] = acc[...].astype(o_ref.dtype)

    return pl.pallas_call(
        body, name=name, grid=(nj, ni, nk), in_specs=[a_spec, b_spec] + [_ANY] * nd, out_specs=o_spec,
        out_shape=o_shape, scratch_shapes=[pltpu.VMEM((tm, tn), F32)] if use_acc else [],
        compiler_params=_params(("parallel", "parallel", "arbitrary")),
    )(a, b, *deps)


def _mm_chip_block(a, b4, blk, prev, *, name, tm=512, deps=()):
    M, K = a.shape
    nchip, _, C = b4.shape
    tm = min(tm, M)
    extra = ([] if prev is None else [prev]) + list(deps)

    def body(blk_ref, a_ref, b_ref, *rest):
        rest[-1][...] = jnp.dot(a_ref[...].astype(_MXU), b_ref[...].astype(_MXU), preferred_element_type=F32)

    return pl.pallas_call(
        body, name=name,
        grid_spec=pltpu.PrefetchScalarGridSpec(
            num_scalar_prefetch=1, grid=(M // tm,),
            in_specs=[pl.BlockSpec((tm, K), lambda i, c: (i, 0)), pl.BlockSpec((None, K, C), lambda i, c: (c[0], 0, 0))]
            + [_ANY] * len(extra),
            out_specs=pl.BlockSpec((tm, C), lambda i, c: (i, c[0]))),
        out_shape=jax.ShapeDtypeStruct((M, nchip * C), F32),
        input_output_aliases={} if prev is None else {3: 0},
        compiler_params=_params(("arbitrary",)),
    )(blk, a, b4, *extra)


def _ew(fn, ins, outs, *, rows, ncol, name, n_acc=0, tm=512, deps=()):
    n_in, n_out, nd = len(ins), len(outs), len(deps)
    tm = min(tm, rows)
    in_specs = []
    for _, kind, col0 in ins:
        if kind == "mat":
            in_specs.append(pl.BlockSpec((tm, CW), lambda j, i, c0=col0: (i, c0 + j)))
        else:
            in_specs.append(pl.BlockSpec((1, CW), lambda j, i, c0=col0: (0, c0 + j)))
    out_specs = [pl.BlockSpec((tm, CW), lambda j, i: (i, j)) for _ in outs]
    out_shape = [jax.ShapeDtypeStruct((rows, w), dt) for w, dt in outs]
    for _ in range(n_acc):
        out_specs.append(pl.BlockSpec((1, CW), lambda j, i: (0, j)))
        out_shape.append(jax.ShapeDtypeStruct((1, ncol * CW), F32))

    def body(*refs):
        vals = fn(*[r[...] for r in refs[:n_in]])
        refs = refs[n_in + nd:]
        for r, v in zip(refs[:n_out], vals[:n_out]):
            r[...] = v.astype(r.dtype)
        i = pl.program_id(1)
        for r, v in zip(refs[n_out:], vals[n_out:]):
            @pl.when(i == 0)
            def _(r=r, v=v):
                r[...] = v

            @pl.when(i > 0)
            def _(r=r, v=v):
                r[...] += v

    res = pl.pallas_call(
        body, name=name, grid=(ncol, rows // tm), in_specs=in_specs + [_ANY] * nd, out_specs=out_specs,
        out_shape=out_shape, compiler_params=_params(("parallel", "arbitrary")),
    )(*[a for a, _, _ in ins], *deps)
    return res


def _colsum(v):
    return jnp.sum(v, axis=0, keepdims=True)


def _sigmoid(v):
    return jax.nn.sigmoid(v)


def _silu_and_grad(v):
    s = _sigmoid(v)
    return v * s, s * (1.0 + v * (1.0 - s))


def _rms_fwd(x, w, *, tm=512, deps=()):
    rows, d = x.shape
    nd = len(deps)

    def body(x_ref, w_ref, *rest):
        h_ref, r_ref = rest[nd:]
        xv = x_ref[...]
        r = lax.rsqrt(jnp.mean(xv * xv, axis=-1, keepdims=True) + NORM_EPS)
        h_ref[...] = (xv * r * w_ref[...]).astype(h_ref.dtype)
        r_ref[...] = r

    return pl.pallas_call(
        body, name="rms_fwd", grid=(rows // tm,),
        in_specs=[pl.BlockSpec((tm, d), lambda i: (i, 0)), pl.BlockSpec((1, d), lambda i: (0, 0))] + [_ANY] * nd,
        out_specs=[pl.BlockSpec((tm, d), lambda i: (i, 0)), pl.BlockSpec((tm, 1), lambda i: (i, 0))],
        out_shape=[jax.ShapeDtypeStruct((rows, d), _MXU), jax.ShapeDtypeStruct((rows, 1), F32)],
        compiler_params=_params(("arbitrary",)),
    )(x, w, *deps)


def _rms_bwd(dh, x, rstd, w, dout, *, tm=256):
    rows, d = x.shape

    def body(dh_ref, x_ref, r_ref, w_ref, do_ref, gx_ref, gw_ref):
        dhv, xv, r, wv = dh_ref[...], x_ref[...], r_ref[...], w_ref[...]
        xr = xv * r
        t = jnp.mean(dhv * wv * xr, axis=-1, keepdims=True)
        gx_ref[...] = do_ref[...] + r * (wv * dhv - xr * t)
        part = _colsum(dhv * xr)
        i = pl.program_id(0)

        @pl.when(i == 0)
        def _():
            gw_ref[...] = part

        @pl.when(i > 0)
        def _():
            gw_ref[...] += part

    return pl.pallas_call(
        body, name="rms_bwd", grid=(rows // tm,),
        in_specs=[pl.BlockSpec((tm, d), lambda i: (i, 0)), pl.BlockSpec((tm, d), lambda i: (i, 0)),
                  pl.BlockSpec((tm, 1), lambda i: (i, 0)), pl.BlockSpec((1, d), lambda i: (0, 0)),
                  pl.BlockSpec((tm, d), lambda i: (i, 0))],
        out_specs=[pl.BlockSpec((tm, d), lambda i: (i, 0)), pl.BlockSpec((1, d), lambda i: (0, 0))],
        out_shape=[jax.ShapeDtypeStruct((rows, d), F32), jax.ShapeDtypeStruct((1, d), F32)],
        compiler_params=_params(("arbitrary",)),
    )(dh, x, rstd, w, dout)


_NT = (((1,), (1,)), ((), ()))
_TN = (((0,), (0,)), ((), ()))


QKV_W = ATTN_W + 2 * KV_W
HEADS_PER_TILE = LANES // HEAD_DIM


def _low_half(rows):
    return lax.broadcasted_iota(jnp.int32, (rows, LANES), 1) < HEAD_DIM


def _pair_mean(t, low):
    m_lo = jnp.sum(jnp.where(low, t, 0.0), axis=-1, keepdims=True)
    m_hi = jnp.sum(jnp.where(low, 0.0, t), axis=-1, keepdims=True)
    return jnp.where(low, m_lo, m_hi) * (1.0 / HEAD_DIM)


def _pair_rstd(t, low):
    return lax.rsqrt(_pair_mean(t * t, low) + NORM_EPS)


def _dup_half(t, hi, low):
    swapped = pltpu.roll(t, HEAD_DIM, 1)
    return jnp.where(low, swapped, t) if hi else jnp.where(low, t, swapped)


def _fold_halves(t):
    return t + pltpu.roll(t, HEAD_DIM, 1)


def _split_heads(t, low):
    return [jnp.where(low, t, 0.0), jnp.where(low, 0.0, t)]


def _stacked_band_mask(n):
    rows = Q_PER_KV * WINDOW
    qi = lax.broadcasted_iota(jnp.int32, (rows, 2 * WINDOW), 0) % WINDOW + WINDOW
    kj = lax.broadcasted_iota(jnp.int32, (rows, 2 * WINDOW), 1)
    diff = qi - kj
    first_key = jnp.where(n > 0, 0, WINDOW)
    return (diff >= 0) & (diff < WINDOW) & (kj >= first_key)


def _stacked_sinks(sink_ref, g):
    blk = lax.broadcasted_iota(jnp.int32, (Q_PER_KV * WINDOW, 1), 0) // WINDOW
    col = jnp.full((Q_PER_KV * WINDOW, 1), sink_ref[Q_PER_KV * g], F32)
    for r in range(1, Q_PER_KV):
        col = jnp.where(blk == r, sink_ref[Q_PER_KV * g + r], col)
    return col


def _attn_in_specs(nblk, rev):
    def cur(n):
        return (nblk - 1 - n) if rev else n

    q_spec = pl.BlockSpec((WINDOW, ATTN_W), lambda n: (cur(n), 0))
    kvc_spec = pl.BlockSpec((WINDOW, 2 * KV_W), lambda n: (cur(n), ATTN_W // (2 * KV_W)))
    kvp_spec = pl.BlockSpec((WINDOW, 2 * KV_W), lambda n: (jnp.maximum(cur(n) - 1, 0), ATTN_W // (2 * KV_W)))
    w_spec = pl.BlockSpec((1, LANES), lambda n: (0, 0))
    l_spec = pl.BlockSpec((WINDOW, N_Q_HEADS), lambda n: (cur(n), 0))
    return q_spec, kvc_spec, kvp_spec, w_spec, l_spec


def _attn2_fwd(proj, qw2, kw2, sinks, deps=()):
    seq = proj.shape[0]
    nblk = seq // WINDOW
    scale = 1.0 / math.sqrt(HEAD_DIM)
    q_spec, kvc_spec, kvp_spec, w_spec, l_spec = _attn_in_specs(nblk, False)
    nd = len(deps)

    def body(sink_ref, q_ref, kvc_ref, kvp_ref, qw_ref, kw_ref, *rest):
        o_ref, lse_ref = rest[nd:]
        n = pl.program_id(0)
        low, low2 = _low_half(WINDOW), _low_half(2 * WINDOW)
        valid = _stacked_band_mask(n)
        head_lane = lax.broadcasted_iota(jnp.int32, (WINDOW, N_Q_HEADS), 1)
        kv = jnp.concatenate([kvp_ref[...], kvc_ref[...]], axis=0)
        qwv, kwv = qw_ref[...], kw_ref[...]
        lse_blk = jnp.zeros((WINDOW, N_Q_HEADS), F32)
        for t in range(N_KV_HEADS // HEADS_PER_TILE):
            kt = kv[:, t * LANES:(t + 1) * LANES]
            vt = kv[:, KV_W + t * LANES:KV_W + (t + 1) * LANES]
            kn = kt * _pair_rstd(kt, low2) * kwv
            for hi in range(HEADS_PER_TILE):
                g = HEADS_PER_TILE * t + hi
                kdup = _dup_half(kn, hi, low2).astype(_MXU)
                vdup = _dup_half(vt, hi, low2).astype(_MXU)
                stack = []
                for tq in (2 * g, 2 * g + 1):
                    qt = q_ref[:, tq * LANES:(tq + 1) * LANES]
                    stack += _split_heads(qt * _pair_rstd(qt, low) * qwv, low)
                qs = jnp.concatenate(stack, axis=0).astype(_MXU)
                s = lax.dot_general(qs, kdup, _NT, preferred_element_type=F32) * scale
                s = jnp.where(valid, s, -1e30)
                sink = _stacked_sinks(sink_ref, g)
                m = jnp.maximum(jnp.max(s, axis=-1, keepdims=True), sink)
                e = jnp.exp(s - m)
                z = jnp.sum(e, axis=-1, keepdims=True) + jnp.exp(sink - m)
                o = jnp.dot((e / z).astype(_MXU), vdup, preferred_element_type=F32)
                for i, tq in enumerate((2 * g, 2 * g + 1)):
                    o_ref[:, tq * LANES:(tq + 1) * LANES] = jnp.where(
                        low, o[2 * i * WINDOW:(2 * i + 1) * WINDOW], o[(2 * i + 1) * WINDOW:(2 * i + 2) * WINDOW])
                lse = m + jnp.log(z)
                for r in range(Q_PER_KV):
                    lse_blk = jnp.where(head_lane == Q_PER_KV * g + r, lse[r * WINDOW:(r + 1) * WINDOW], lse_blk)
        lse_ref[...] = lse_blk

    return pl.pallas_call(
        body, name="attn_fwd", grid=(nblk,),
        in_specs=[pl.BlockSpec(memory_space=pltpu.SMEM), q_spec, kvc_spec, kvp_spec, w_spec, w_spec] + [_ANY] * nd,
        out_specs=[q_spec, l_spec],
        out_shape=[jax.ShapeDtypeStruct((seq, ATTN_W), F32), jax.ShapeDtypeStruct((seq, N_Q_HEADS), F32)],
        compiler_params=_params(("arbitrary",)),
    )(sinks, proj, proj, proj, qw2, kw2, *deps)


def _attn2_bwd(proj, qw2, kw2, sinks, lse, do, deps=()):
    seq = proj.shape[0]
    nblk = seq // WINDOW
    scale = 1.0 / math.sqrt(HEAD_DIM)
    q_spec, kvc_spec, kvp_spec, w_spec, l_spec = _attn_in_specs(nblk, True)
    s_spec = pl.BlockSpec((1, N_Q_HEADS), lambda n: (0, 0))
    d_spec = pl.BlockSpec((WINDOW, QKV_W), lambda n: (nblk - 1 - n, 0))
    nd = len(deps)

    def body(sink_ref, q_ref, kvc_ref, kvp_ref, qw_ref, kw_ref, lse_ref, do_ref, *rest):
        d_ref, dqw_ref, dkw_ref, dsk_ref, carry = rest[nd:]
        step = pl.program_id(0)
        n = nblk - 1 - step

        @pl.when(step == 0)
        def _():
            carry[...] = jnp.zeros_like(carry)
            dqw_ref[...] = jnp.zeros_like(dqw_ref)
            dkw_ref[...] = jnp.zeros_like(dkw_ref)
            dsk_ref[...] = jnp.zeros_like(dsk_ref)

        low, low2 = _low_half(WINDOW), _low_half(2 * WINDOW)
        valid = _stacked_band_mask(n)
        head_lane = lax.broadcasted_iota(jnp.int32, (WINDOW, N_Q_HEADS), 1)
        sink_lane = lax.broadcasted_iota(jnp.int32, (1, N_Q_HEADS), 1)
        kv = jnp.concatenate([kvp_ref[...], kvc_ref[...]], axis=0)
        qwv, kwv = qw_ref[...], kw_ref[...]
        lse_blk = lse_ref[...]
        dqw = jnp.zeros((1, LANES), F32)
        dkw = jnp.zeros((1, LANES), F32)
        dsk = jnp.zeros((1, N_Q_HEADS), F32)
        for t in range(N_KV_HEADS // HEADS_PER_TILE):
            kt = kv[:, t * LANES:(t + 1) * LANES]
            vt = kv[:, KV_W + t * LANES:KV_W + (t + 1) * LANES]
            rk = _pair_rstd(kt, low2)
            kn = kt * rk * kwv
            dkn_t = jnp.zeros((2 * WINDOW, LANES), F32)
            dv_t = jnp.zeros((2 * WINDOW, LANES), F32)
            for hi in range(HEADS_PER_TILE):
                g = HEADS_PER_TILE * t + hi
                kdup = _dup_half(kn, hi, low2).astype(_MXU)
                vdup = _dup_half(vt, hi, low2).astype(_MXU)
                tiles = (2 * g, 2 * g + 1)
                qx, rq, stack, dstack, lse_rows = [], [], [], [], []
                for tq in tiles:
                    qt = q_ref[:, tq * LANES:(tq + 1) * LANES]
                    r = _pair_rstd(qt, low)
                    rq.append(r)
                    qx.append(qt * r)
                    stack += _split_heads(qx[-1] * qwv, low)
                    dstack += _split_heads(do_ref[:, tq * LANES:(tq + 1) * LANES], low)
                for r in range(Q_PER_KV):
                    lse_rows.append(jnp.sum(jnp.where(head_lane == Q_PER_KV * g + r, lse_blk, 0.0), axis=-1, keepdims=True))
                qs = jnp.concatenate(stack, axis=0).astype(_MXU)
                dos = jnp.concatenate(dstack, axis=0).astype(_MXU)
                lse_col = jnp.concatenate(lse_rows, axis=0)
                s = lax.dot_general(qs, kdup, _NT, preferred_element_type=F32) * scale
                s = jnp.where(valid, s, -1e30)
                p = jnp.exp(s - lse_col)
                dp = lax.dot_general(dos, vdup, _NT, preferred_element_type=F32)
                dsum = jnp.sum(p * dp, axis=-1, keepdims=True)
                ds = (p * (dp - dsum) * scale).astype(_MXU)
                dsink = -jnp.exp(_stacked_sinks(sink_ref, g) - lse_col) * dsum
                for r in range(Q_PER_KV):
                    dsk = dsk + jnp.where(sink_lane == Q_PER_KV * g + r, _colsum(dsink[r * WINDOW:(r + 1) * WINDOW]), 0.0)
                dv_g = _fold_halves(lax.dot_general(p.astype(_MXU), dos, _TN, preferred_element_type=F32))
                dkn_g = _fold_halves(lax.dot_general(ds, qs, _TN, preferred_element_type=F32))
                dv_t = jnp.where(low2, dv_t, dv_g) if hi else jnp.where(low2, dv_g, dv_t)
                dkn_t = jnp.where(low2, dkn_t, dkn_g) if hi else jnp.where(low2, dkn_g, dkn_t)
                dqn = jnp.dot(ds, kdup, preferred_element_type=F32)
                for i, tq in enumerate(tiles):
                    dqn_t = jnp.where(low, dqn[2 * i * WINDOW:(2 * i + 1) * WINDOW],
                                      dqn[(2 * i + 1) * WINDOW:(2 * i + 2) * WINDOW])
                    dq = rq[i] * (qwv * dqn_t - qx[i] * _pair_mean(dqn_t * qwv * qx[i], low))
                    d_ref[:, tq * LANES:(tq + 1) * LANES] = dq.astype(d_ref.dtype)
                    dqw = dqw + _colsum(dqn_t * qx[i])
            k_cols = slice(t * LANES, (t + 1) * LANES)
            v_cols = slice(KV_W + t * LANES, KV_W + (t + 1) * LANES)
            dkn_c = dkn_t[WINDOW:] + carry[:, k_cols]
            rc = rk[WINDOW:]
            kx = kt[WINDOW:] * rc
            dk = rc * (kwv * dkn_c - kx * _pair_mean(dkn_c * kwv * kx, low))
            d_ref[:, ATTN_W + t * LANES:ATTN_W + (t + 1) * LANES] = dk.astype(d_ref.dtype)
            d_ref[:, ATTN_W + KV_W + t * LANES:ATTN_W + KV_W + (t + 1) * LANES] = (
                dv_t[WINDOW:] + carry[:, v_cols]).astype(d_ref.dtype)
            carry[:, k_cols] = dkn_t[:WINDOW]
            carry[:, v_cols] = dv_t[:WINDOW]
            dkw = dkw + _colsum(dkn_c * kx)
        dqw_ref[...] += dqw
        dkw_ref[...] += dkw
        dsk_ref[...] += dsk

    return pl.pallas_call(
        body, name="attn_bwd", grid=(nblk,),
        in_specs=[pl.BlockSpec(memory_space=pltpu.SMEM), q_spec, kvc_spec, kvp_spec, w_spec, w_spec, l_spec, q_spec]
        + [_ANY] * nd,
        out_specs=[d_spec, w_spec, w_spec, s_spec],
        out_shape=[jax.ShapeDtypeStruct((seq, QKV_W), _MXU), jax.ShapeDtypeStruct((1, LANES), F32),
                   jax.ShapeDtypeStruct((1, LANES), F32), jax.ShapeDtypeStruct((1, N_Q_HEADS), F32)],
        scratch_shapes=[pltpu.VMEM((WINDOW, 2 * KV_W), F32)],
        compiler_params=_params(("arbitrary",)),
    )(sinks, proj, proj, proj, qw2, kw2, lse, do, *deps)


def _ssm_discretise(a_re, a_im, log_dt):
    dt = jnp.exp(log_dt)
    mag = jnp.exp(dt * a_re)
    ab_re = mag * jnp.cos(dt * a_im)
    ab_im = mag * jnp.sin(dt * a_im)
    num_re = ab_re - 1.0
    num_im = ab_im
    den = a_re * a_re + a_im * a_im
    cf_re = (num_re * a_re + num_im * a_im) / den
    cf_im = (num_im * a_re - num_re * a_im) / den
    return ab_re, ab_im, cf_re, cf_im


def _ssm_params_fwd(a_re, a_im, log_dt):
    shp = jax.ShapeDtypeStruct(a_re.shape, F32)

    def body(are_ref, aim_ref, ldt_ref, abr_ref, abi_ref, cfr_ref, cfi_ref, alr_ref, ali_ref):
        abr, abi, cfr, cfi = _ssm_discretise(are_ref[...], aim_ref[...], ldt_ref[...])
        abr_ref[...], abi_ref[...], cfr_ref[...], cfi_ref[...] = abr, abi, cfr, cfi
        pr, pi = abr, abi
        for _ in range(int(math.log2(SSM_L))):
            pr, pi = pr * pr - pi * pi, 2.0 * pr * pi
        alr_ref[...], ali_ref[...] = pr, pi

    return pl.pallas_call(body, name="ssm_params_fwd", out_shape=[shp] * 6)(a_re, a_im, log_dt)


def _ssm_params_bwd(a_re, a_im, log_dt, d_abr, d_abi, d_cfr, d_cfi):
    def body(are_ref, aim_ref, ldt_ref, g0, g1, g2, g3, dare_ref, daim_ref, dldt_ref):
        _, vjp = jax.vjp(_ssm_discretise, are_ref[...], aim_ref[...], ldt_ref[...])
        dare_ref[...], daim_ref[...], dldt_ref[...] = vjp((g0[...], g1[...], g2[...], g3[...]))

    return pl.pallas_call(
        body, name="ssm_params_bwd",
        out_shape=[jax.ShapeDtypeStruct(a_re.shape, F32), jax.ShapeDtypeStruct(a_im.shape, F32),
                   jax.ShapeDtypeStruct(log_dt.shape, F32)],
    )(a_re, a_im, log_dt, d_abr, d_abi, d_cfr, d_cfi)


def _scan_cols(j):
    return pl.ds(j * SSM_SB, SSM_SB)


def _rows8(r):
    return pl.ds(pl.multiple_of(r * SUBLANES, SUBLANES), SUBLANES)


def _bcast8(row):
    return jnp.broadcast_to(row, (SUBLANES, row.shape[-1]))


def _ssm_fwd(u, b_re, b_im, c_re, c_im, d_skip, coef):
    seq = u.shape[0]
    nc = seq // SSM_T
    T, L = SSM_T, SSM_L

    def body(u_ref, bre_ref, bim_ref, cre_ref, cim_ref, d_ref, are_ref, aim_ref, cfr_ref, cfi_ref, alr_ref, ali_ref,
             y_ref, sre_ref, sim_ref, ire_ref, iim_ref, car_re, car_im, end_re, end_im):
        c = pl.program_id(0)

        @pl.when(c == 0)
        def _():
            car_re[...] = jnp.zeros_like(car_re)
            car_im[...] = jnp.zeros_like(car_im)

        for j in range(SSM_JB):
            ub = u_ref[:, j * LANES:(j + 1) * LANES].astype(_MXU)
            bur = jnp.dot(ub, bre_ref[j], preferred_element_type=F32)
            bui = jnp.dot(ub, bim_ref[j], preferred_element_type=F32)
            cfr, cfi = cfr_ref[:, _scan_cols(j)], cfi_ref[:, _scan_cols(j)]
            sre_ref[:, _scan_cols(j)] = cfr * bur - cfi * bui
            sim_ref[:, _scan_cols(j)] = cfr * bui + cfi * bur

        for j in range(SSM_JB):
            cols = _scan_cols(j)
            ar, ai = _bcast8(are_ref[:, cols]), _bcast8(aim_ref[:, cols])

            def step1(r, s, cols=cols, ar=ar, ai=ai):
                sr, si = s
                rows = _rows8(r)
                return (ar * sr - ai * si + sre_ref[rows, cols], ar * si + ai * sr + sim_ref[rows, cols])

            zero = jnp.zeros((SUBLANES, SSM_SB), F32)
            er, ei = lax.fori_loop(0, L, step1, (zero, zero), unroll=4)
            end_re[:, cols] = er
            end_im[:, cols] = ei

        alr, ali = alr_ref[...], ali_ref[...]
        cr, ci = car_re[...], car_im[...]
        ire_ref[0:1, :] = cr
        iim_ref[0:1, :] = ci
        for i in range(1, SUBLANES):
            er, ei = end_re[i - 1:i, :], end_im[i - 1:i, :]
            cr, ci = alr * cr - ali * ci + er, alr * ci + ali * cr + ei
            ire_ref[i:i + 1, :] = cr
            iim_ref[i:i + 1, :] = ci

        for j in range(SSM_JB):
            cols = _scan_cols(j)
            ar, ai = _bcast8(are_ref[:, cols]), _bcast8(aim_ref[:, cols])

            def step2(r, s, cols=cols, ar=ar, ai=ai):
                sr, si = s
                rows = _rows8(r)
                nr = ar * sr - ai * si + sre_ref[rows, cols]
                ni = ar * si + ai * sr + sim_ref[rows, cols]
                sre_ref[rows, cols] = nr
                sim_ref[rows, cols] = ni
                return nr, ni

            lax.fori_loop(0, L, step2, (ire_ref[:, cols], iim_ref[:, cols]), unroll=4)

        car_re[...] = sre_ref[T - 1:T, :]
        car_im[...] = sim_ref[T - 1:T, :]

        for j in range(SSM_JB):
            cols = _scan_cols(j)
            ch = slice(j * LANES, (j + 1) * LANES)
            y = (jnp.dot(sre_ref[:, cols].astype(_MXU), cre_ref[j], preferred_element_type=F32)
                 - jnp.dot(sim_ref[:, cols].astype(_MXU), cim_ref[j], preferred_element_type=F32))
            y_ref[:, ch] = y + d_ref[:, ch] * u_ref[:, ch]

    tok = pl.BlockSpec((T, SSM_W), lambda c: (c, 0))
    st = pl.BlockSpec((T, N_STATES), lambda c: (c, 0))
    ini = pl.BlockSpec((None, SUBLANES, N_STATES), lambda c: (c, 0, 0))
    bsp = pl.BlockSpec((SSM_JB, LANES, SSM_SB), lambda c: (0, 0, 0))
    csp = pl.BlockSpec((SSM_JB, SSM_SB, LANES), lambda c: (0, 0, 0))
    row_w = pl.BlockSpec((1, SSM_W), lambda c: (0, 0))
    row_s = pl.BlockSpec((1, N_STATES), lambda c: (0, 0))
    return pl.pallas_call(
        body, name="ssm_fwd", grid=(nc,),
        in_specs=[tok, bsp, bsp, csp, csp, row_w] + [row_s] * 6,
        out_specs=[tok, st, st, ini, ini],
        out_shape=[jax.ShapeDtypeStruct((seq, SSM_W), F32),
                   jax.ShapeDtypeStruct((seq, N_STATES), F32), jax.ShapeDtypeStruct((seq, N_STATES), F32),
                   jax.ShapeDtypeStruct((nc, SUBLANES, N_STATES), F32),
                   jax.ShapeDtypeStruct((nc, SUBLANES, N_STATES), F32)],
        scratch_shapes=[pltpu.VMEM((1, N_STATES), F32), pltpu.VMEM((1, N_STATES), F32),
                        pltpu.VMEM((SUBLANES, N_STATES), F32), pltpu.VMEM((SUBLANES, N_STATES), F32)],
        compiler_params=_params(("arbitrary",)),
    )(u, b_re, b_im, c_re, c_im, d_skip, *coef)


def _ssm_bwd(dy, u, s_re, s_im, i_re, i_im, b_re, b_im, c_re, c_im, d_skip, coef):
    seq = u.shape[0]
    nc = seq // SSM_T
    T, L = SSM_T, SSM_L

    def body(dy_ref, u_ref, sre_ref, sim_ref, ire_ref, iim_ref, bre_ref, bim_ref, cre_ref, cim_ref, d_ref,
             are_ref, aim_ref, cfr_ref, cfi_ref, alr_ref, ali_ref,
             du_ref, dbre_out, dbim_out, dcre_out, dcim_out, dd_ref, dar_ref, dai_ref, dcfr_ref, dcfi_ref,
             lre, lim, car_re, car_im, end_re, end_im, ini_re, ini_im, dbre_ref, dbim_ref, dcre_ref, dcim_ref):
        step = pl.program_id(0)

        @pl.when(step == 0)
        def _():
            car_re[...] = jnp.zeros_like(car_re)
            car_im[...] = jnp.zeros_like(car_im)
            for ref in (dbre_ref, dbim_ref, dcre_ref, dcim_ref, dd_ref, dar_ref, dai_ref, dcfr_ref, dcfi_ref):
                ref[...] = jnp.zeros_like(ref)

        for j in range(SSM_JB):
            dyb = dy_ref[:, j * LANES:(j + 1) * LANES].astype(_MXU)
            lre[:, _scan_cols(j)] = lax.dot_general(dyb, cre_ref[j], _NT, preferred_element_type=F32)
            lim[:, _scan_cols(j)] = -lax.dot_general(dyb, cim_ref[j], _NT, preferred_element_type=F32)

        for j in range(SSM_JB):
            cols = _scan_cols(j)
            ar, ai = _bcast8(are_ref[:, cols]), _bcast8(aim_ref[:, cols])

            def step1(t, s, cols=cols, ar=ar, ai=ai):
                sr, si = s
                rows = _rows8(L - 1 - t)
                return (ar * sr + ai * si + lre[rows, cols], ar * si - ai * sr + lim[rows, cols])

            zero = jnp.zeros((SUBLANES, SSM_SB), F32)
            er, ei = lax.fori_loop(0, L, step1, (zero, zero), unroll=4)
            end_re[:, cols] = er
            end_im[:, cols] = ei

        alr, ali = alr_ref[...], ali_ref[...]
        cr, ci = car_re[...], car_im[...]
        ini_re[SUBLANES - 1:SUBLANES, :] = cr
        ini_im[SUBLANES - 1:SUBLANES, :] = ci
        for i in range(SUBLANES - 2, -1, -1):
            er, ei = end_re[i + 1:i + 2, :], end_im[i + 1:i + 2, :]
            cr, ci = alr * cr + ali * ci + er, alr * ci - ali * cr + ei
            ini_re[i:i + 1, :] = cr
            ini_im[i:i + 1, :] = ci

        for j in range(SSM_JB):
            cols = _scan_cols(j)
            ar, ai = _bcast8(are_ref[:, cols]), _bcast8(aim_ref[:, cols])

            def step2(t, s, cols=cols, ar=ar, ai=ai):
                sr, si = s
                rows = _rows8(L - 1 - t)
                nr = ar * sr + ai * si + lre[rows, cols]
                ni = ar * si - ai * sr + lim[rows, cols]
                lre[rows, cols] = nr
                lim[rows, cols] = ni
                return nr, ni

            lax.fori_loop(0, L, step2, (ini_re[:, cols], ini_im[:, cols]), unroll=4)

        car_re[...] = lre[0:1, :]
        car_im[...] = lim[0:1, :]

        head, tail, body_rows = slice(0, SUBLANES), slice(SUBLANES, T), slice(0, T - SUBLANES)
        for j in range(SSM_JB):
            cols = _scan_cols(j)
            ch = slice(j * LANES, (j + 1) * LANES)
            lr, li = lre[:, cols], lim[:, cols]
            dar_ref[:, cols] += (_colsum(lre[tail, cols] * sre_ref[body_rows, cols] + lim[tail, cols] * sim_ref[body_rows, cols])
                                 + _colsum(lre[head, cols] * ire_ref[:, cols] + lim[head, cols] * iim_ref[:, cols]))
            dai_ref[:, cols] += (_colsum(lim[tail, cols] * sre_ref[body_rows, cols] - lre[tail, cols] * sim_ref[body_rows, cols])
                                 + _colsum(lim[head, cols] * ire_ref[:, cols] - lre[head, cols] * iim_ref[:, cols]))
            uf = u_ref[:, ch]
            ub = uf.astype(_MXU)
            bur = jnp.dot(ub, bre_ref[j], preferred_element_type=F32)
            bui = jnp.dot(ub, bim_ref[j], preferred_element_type=F32)
            dcfr_ref[:, cols] += _colsum(lr * bur + li * bui)
            dcfi_ref[:, cols] += _colsum(li * bur - lr * bui)
            cfr, cfi = cfr_ref[:, cols], cfi_ref[:, cols]
            dbur = (cfr * lr + cfi * li).astype(_MXU)
            dbui = (cfr * li - cfi * lr).astype(_MXU)
            dyf = dy_ref[:, ch]
            dyb = dyf.astype(_MXU)
            du_ref[:, ch] = (lax.dot_general(dbur, bre_ref[j], _NT, preferred_element_type=F32)
                             + lax.dot_general(dbui, bim_ref[j], _NT, preferred_element_type=F32)
                             + d_ref[:, ch] * dyf)
            dbre_ref[j] += lax.dot_general(ub, dbur, _TN, preferred_element_type=F32)
            dbim_ref[j] += lax.dot_general(ub, dbui, _TN, preferred_element_type=F32)
            dcre_ref[j] += lax.dot_general(sre_ref[:, cols].astype(_MXU), dyb, _TN, preferred_element_type=F32)
            dcim_ref[j] -= lax.dot_general(sim_ref[:, cols].astype(_MXU), dyb, _TN, preferred_element_type=F32)
            dd_ref[:, ch] += _colsum(dyf * uf)

        @pl.when(step == nc - 1)
        def _():
            for acc, out in ((dbre_ref, dbre_out), (dbim_ref, dbim_out), (dcre_ref, dcre_out), (dcim_ref, dcim_out)):
                pltpu.sync_copy(acc, out)

    tok = pl.BlockSpec((T, SSM_W), lambda c: (nc - 1 - c, 0))
    st = pl.BlockSpec((T, N_STATES), lambda c: (nc - 1 - c, 0))
    ini = pl.BlockSpec((None, SUBLANES, N_STATES), lambda c: (nc - 1 - c, 0, 0))
    bsp = pl.BlockSpec((SSM_JB, LANES, SSM_SB), lambda c: (0, 0, 0))
    csp = pl.BlockSpec((SSM_JB, SSM_SB, LANES), lambda c: (0, 0, 0))
    row_w = pl.BlockSpec((1, SSM_W), lambda c: (0, 0))
    row_s = pl.BlockSpec((1, N_STATES), lambda c: (0, 0))
    big = pltpu.VMEM((T, N_STATES), F32)
    one = pltpu.VMEM((1, N_STATES), F32)
    eight = pltpu.VMEM((SUBLANES, N_STATES), F32)
    return pl.pallas_call(
        body, name="ssm_bwd", grid=(nc,),
        in_specs=[tok, tok, st, st, ini, ini, bsp, bsp, csp, csp, row_w] + [row_s] * 6,
        out_specs=[tok, _ANY, _ANY, _ANY, _ANY, row_w, row_s, row_s, row_s, row_s],
        out_shape=[jax.ShapeDtypeStruct((seq, SSM_W), F32),
                   jax.ShapeDtypeStruct((SSM_JB, LANES, SSM_SB), F32), jax.ShapeDtypeStruct((SSM_JB, LANES, SSM_SB), F32),
                   jax.ShapeDtypeStruct((SSM_JB, SSM_SB, LANES), F32), jax.ShapeDtypeStruct((SSM_JB, SSM_SB, LANES), F32),
                   jax.ShapeDtypeStruct((1, SSM_W), F32)] + [jax.ShapeDtypeStruct((1, N_STATES), F32)] * 4,
        scratch_shapes=[big, big, one, one, eight, eight, eight, eight,
                        pltpu.VMEM((SSM_JB, LANES, SSM_SB), F32), pltpu.VMEM((SSM_JB, LANES, SSM_SB), F32),
                        pltpu.VMEM((SSM_JB, SSM_SB, LANES), F32), pltpu.VMEM((SSM_JB, SSM_SB, LANES), F32)],
        compiler_params=_params(("arbitrary",)),
    )(dy, u, s_re, s_im, i_re, i_im, b_re, b_im, c_re, c_im, d_skip, *coef)


def _block_diag_b(b):
    t = b.reshape(SSM_JB, 8, STATE, GROUP).transpose(0, 1, 3, 2)
    eye = jnp.eye(8, dtype=b.dtype)
    return (t[:, :, :, None, :] * eye[None, :, None, :, None]).reshape(SSM_JB, LANES, SSM_SB)


def _block_diag_c(c):
    t = c.reshape(SSM_JB, 8, GROUP, STATE).transpose(0, 1, 3, 2)
    eye = jnp.eye(8, dtype=c.dtype)
    return (t[:, :, :, None, :] * eye[None, :, None, :, None]).reshape(SSM_JB, SSM_SB, LANES)


def _diag_of_b(blk):
    t = blk.reshape(SSM_JB, 8, GROUP, 8, STATE)
    d = jnp.sum(t * jnp.eye(8, dtype=blk.dtype)[None, :, None, :, None], axis=3)
    return d.transpose(0, 1, 3, 2).reshape(N_GROUPS, STATE, GROUP)


def _diag_of_c(blk):
    t = blk.reshape(SSM_JB, 8, STATE, 8, GROUP)
    d = jnp.sum(t * jnp.eye(8, dtype=blk.dtype)[None, :, None, :, None], axis=3)
    return d.transpose(0, 1, 3, 2).reshape(N_GROUPS, GROUP, STATE)


def _to_scan_order(v):
    seq, w = v.shape
    return v.reshape(seq // SSM_T, SUBLANES, SSM_L, w).transpose(0, 2, 1, 3).reshape(seq, w)


def _from_scan_order(v):
    seq, w = v.shape
    return v.reshape(seq // SSM_T, SSM_L, SUBLANES, w).transpose(0, 2, 1, 3).reshape(seq, w)


def _adamw_math(w, g, m, v):
    nm = ADAM_B1 * m + (1.0 - ADAM_B1) * g
    nv = ADAM_B2 * v + (1.0 - ADAM_B2) * jnp.square(g)
    m_hat = nm / (1.0 - ADAM_B1 ** ADAM_STEP)
    v_hat = nv / (1.0 - ADAM_B2 ** ADAM_STEP)
    return -ADAM_LR * (m_hat / (jnp.sqrt(v_hat) + ADAM_EPS) + ADAM_WD * w), nm, nv


def _adamw(w, g, m, v, *, name, tm, deps=()):
    rows, cols = w.shape
    nd = len(deps)

    def body(w_ref, g_ref, m_ref, v_ref, *rest):
        d_ref, nm_ref, nv_ref = rest[nd:]
        d_ref[...], nm_ref[...], nv_ref[...] = _adamw_math(w_ref[...], g_ref[...], m_ref[...], v_ref[...])

    spec = pl.BlockSpec((tm, cols), lambda i: (i, 0))
    shp = jax.ShapeDtypeStruct((rows, cols), F32)
    return pl.pallas_call(body, name=name, grid=(rows // tm,), in_specs=[spec] * 4 + [_ANY] * nd,
                          out_specs=[spec] * 3, out_shape=[shp] * 3,
                          compiler_params=_params(("arbitrary",)))(w, g, m, v, *deps)


def _place():
    x, y, c = lax.axis_index("x"), lax.axis_index("y"), lax.axis_index("c")
    chips = [(1 - x, y), (x, 1 - y), (1 - x, 1 - y)]
    return x, y, c, chips


def _remote(src, dst, send_sem, recv_sem, dev):
    return pltpu.make_async_remote_copy(src_ref=src, dst_ref=dst, send_sem=send_sem, recv_sem=recv_sem,
                                        device_id=dev, device_id_type=MESH)


def _place_shard(w, mine_arr, *, name, tm=256):
    rows, cols = w.shape

    def body(m_ref, w_ref, o_ref):
        o_ref[...] = w_ref[...].astype(o_ref.dtype)

    return pl.pallas_call(
        body, name=name,
        grid_spec=pltpu.PrefetchScalarGridSpec(
            num_scalar_prefetch=1, grid=(rows // tm,),
            in_specs=[pl.BlockSpec((tm, cols), lambda i, m: (i, 0))],
            out_specs=pl.BlockSpec((None, tm, cols), lambda i, m: (m[0], i, 0))),
        out_shape=jax.ShapeDtypeStruct((N_CHIPS, rows, cols), _WIRE),
        compiler_params=_params(("arbitrary",)),
    )(mine_arr, w)


_HBM = pl.BlockSpec(memory_space=pltpu.HBM)
_SEM = pl.BlockSpec(memory_space=pltpu.SEMAPHORE)
_EFFECT = pltpu.SideEffectType.DATAFLOW_SIDE_EFFECTING


def _copies_start(name, bufs, plan, count, after=()):
    nb, na = len(bufs), len(after)

    def body(*refs):
        send_sems, recv_sems, token = refs[nb + na], refs[nb + na + 1], refs[-1]
        copies = plan(refs[:nb])
        assert len(copies) == count
        for i, (src, dst, dev, _) in enumerate(copies):
            _remote(src, dst, send_sems.at[i], recv_sems.at[i], dev).start()
        token[...] = jnp.zeros_like(token)

    res = pl.pallas_call(
        body, name=name, in_specs=[_HBM] * nb + [_ANY] * na,
        out_specs=(_SEM, _SEM, *[_HBM] * nb, pl.BlockSpec(memory_space=pltpu.VMEM)),
        out_shape=(pltpu.SemaphoreType.DMA((count,)), pltpu.SemaphoreType.DMA((count,)),
                   *[pltpu.HBM(b.shape, b.dtype) for b in bufs], jax.ShapeDtypeStruct((SUBLANES, LANES), F32)),
        input_output_aliases={i: 2 + i for i in range(nb)},
        compiler_params=pltpu.CompilerParams(has_side_effects=_EFFECT),
    )(*[pltpu.with_memory_space_constraint(b, pltpu.HBM) for b in bufs], *after)
    return (res[0], res[1]), list(res[2:2 + nb]), res[-1]


def _copies_wait(name, bufs, sems, plan, after=()):
    nb, na = len(bufs), len(after)

    def body(*refs):
        send_sems, recv_sems = refs[nb], refs[nb + 1]
        for i, (src, _, dev, land) in enumerate(plan(refs[:nb])):
            cp = _remote(src, land, send_sems.at[i], recv_sems.at[i], dev)
            cp.wait_send()
            cp.wait_recv()

    res = pl.pallas_call(
        body, name=name, in_specs=[_HBM] * nb + [_SEM, _SEM] + [_ANY] * na, out_specs=[_HBM] * nb,
        out_shape=[pltpu.HBM(b.shape, b.dtype) for b in bufs],
        input_output_aliases={i: i for i in range(nb)},
        compiler_params=pltpu.CompilerParams(has_side_effects=_EFFECT),
    )(*bufs, *sems, *after)
    return list(res)


def _plan_gather_ici(fulls, which=(0, 1, 2)):
    x, y, c, chips = _place()
    copies = []
    for f in fulls:
        half = pl.ds(c * (f.shape[1] // 2), f.shape[1] // 2)
        own = f.at[2 * x + y, half]
        for chip in [chips[k] for k in which]:
            copies.append((own, own, (*chip, c), f.at[2 * chip[0] + chip[1], half]))
    return copies


def _plan_gather_d2d(fulls, which=(0, 1, 2)):
    x, y, c, chips = _place()
    copies = []
    for f in fulls:
        r2 = f.shape[1] // 2
        for chip in [chips[k] for k in which]:
            blk = 2 * chip[0] + chip[1]
            landed = f.at[blk, pl.ds(c * r2, r2)]
            copies.append((landed, landed, (x, y, 1 - c), f.at[blk, pl.ds((1 - c) * r2, r2)]))
    return copies


def _plan_swap_halves(refs):
    x, y, c, _ = _place()
    n = len(refs) // 2
    copies = []
    for g, land in zip(refs[:n], refs[n:]):
        r2 = g.shape[1] // 2
        copies.append((g.at[:, pl.ds((1 - c) * r2, r2), :], land, (x, y, 1 - c), land))
    return copies


def _plan_scatter_chips(refs):
    x, y, c, chips = _place()
    n = len(refs) // 2
    copies = []
    for h, land in zip(refs[:n], refs[n:]):
        for k, chip in enumerate(chips):
            copies.append((h.at[2 * chip[0] + chip[1]], land.at[k], (*chip, c), land.at[k]))
    return copies


def _plan_join_halves(totals):
    x, y, c, _ = _place()
    copies = []
    for t in totals:
        r2 = t.shape[0] // 2
        mine = t.at[pl.ds(c * r2, r2)]
        copies.append((mine, mine, (x, y, 1 - c), t.at[pl.ds((1 - c) * r2, r2)]))
    return copies


def _add_sibling_half(g, got, c_arr, *, name, tm):
    _, rows, cols = g.shape
    r2 = rows // 2
    nb = r2 // tm

    def body(c_ref, g_ref, r_ref, o_ref):
        o_ref[...] = (g_ref[...].astype(F32) + r_ref[...].astype(F32)).astype(o_ref.dtype)

    return pl.pallas_call(
        body, name=name,
        grid_spec=pltpu.PrefetchScalarGridSpec(
            num_scalar_prefetch=1, grid=(N_CHIPS, nb),
            in_specs=[pl.BlockSpec((None, tm, cols), lambda b, i, c: (b, c[0] * nb + i, 0)),
                      pl.BlockSpec((None, tm, cols), lambda b, i, c: (b, i, 0))],
            out_specs=pl.BlockSpec((None, tm, cols), lambda b, i, c: (b, i, 0))),
        out_shape=jax.ShapeDtypeStruct((N_CHIPS, r2, cols), _WIRE),
        compiler_params=_params(("arbitrary", "arbitrary")),
    )(c_arr, g, got)


def _add_chips(h, got, place_arr, *, name, tm):
    _, r2, cols = h.shape
    nb = r2 // tm

    def body(p_ref, h_ref, r_ref, o_ref):
        o_ref[...] = ((h_ref[...].astype(F32) + r_ref[0].astype(F32)) + r_ref[1].astype(F32)) + r_ref[2].astype(F32)

    return pl.pallas_call(
        body, name=name,
        grid_spec=pltpu.PrefetchScalarGridSpec(
            num_scalar_prefetch=1, grid=(nb,),
            in_specs=[pl.BlockSpec((None, tm, cols), lambda i, p: (p[0], i, 0)),
                      pl.BlockSpec((3, tm, cols), lambda i, p: (0, i, 0))],
            out_specs=pl.BlockSpec((tm, cols), lambda i, p: (p[1] * nb + i, 0))),
        out_shape=jax.ShapeDtypeStruct((2 * r2, cols), F32),
        compiler_params=_params(("arbitrary",)),
    )(place_arr, h, got)


class _ReduceScatter:
    def __init__(self, tag, names, grads):
        self.tag, self.names, self.n = tag, names, len(names)
        core = lax.axis_index("c").astype(jnp.int32)
        chip = (2 * lax.axis_index("x") + lax.axis_index("y")).astype(jnp.int32)
        self.c_arr, self.place_arr = core.reshape(1), jnp.stack([chip, core])
        self.bufs = list(grads)

    def _start(self, step, bufs, plan, count, after):
        self.plan = plan
        self.step = f"grad_{step}_{self.tag}"
        self.sems, self.bufs, token = _copies_start(self.step + "_start", bufs, plan, count, after)
        return [token]

    def _wait(self, after):
        self.bufs = _copies_wait(self.step + "_wait", self.bufs, self.sems, self.plan, after)
        return self.bufs

    def start_swap(self, after=()):
        lands = [lax.empty((N_CHIPS, g.shape[1] // 2, g.shape[2]), g.dtype) for g in self.bufs]
        return self._start("swap", self.bufs + lands, _plan_swap_halves, self.n, after)

    def start_scatter(self, after):
        bufs = self._wait(after)
        pair = [_add_sibling_half(g, r, self.c_arr, name=f"grad_add_sibling_{nm}", tm=min(256, g.shape[1] // 2))
                for nm, g, r in zip(self.names, bufs[:self.n], bufs[self.n:])]
        lands = [lax.empty((3,) + h.shape[1:], h.dtype) for h in pair]
        return self._start("scatter", pair + lands, _plan_scatter_chips, 3 * self.n, ())

    def start_join(self, after):
        bufs = self._wait(after)
        total = [_add_chips(h, r, self.place_arr, name=f"grad_add_chips_{nm}", tm=min(256, h.shape[1]))
                 for nm, h, r in zip(self.names, bufs[:self.n], bufs[self.n:])]
        return self._start("join", total, _plan_join_halves, self.n, ())

    def finish(self, after):
        return dict(zip(self.names, self._wait(after)))


def _all_gather_small(v):
    m_per, n = v.shape

    def body(x_ref, out_ref, send_sems, recv_sems, local_sem):
        x, y, c, chips = _place()
        me, sibling = (x, y, c), (x, y, 1 - c)

        def rows(px, py, pc):
            return out_ref.at[4 * px + 2 * py + pc]

        def copy(k, block, to, src=None):
            return _remote(rows(*block) if src is None else src, rows(*block), send_sems.at[k], recv_sems.at[k], to)

        mine = pltpu.make_async_copy(x_ref, rows(*me), local_sem)
        mine.start()
        first = [copy(0, me, sibling, src=x_ref)]
        first += [copy(1 + j, me, (*chip, c), src=x_ref) for j, chip in enumerate(chips)]
        for cp in first:
            cp.start()
        passed = [copy(4 + j, (*chip, c), sibling) for j, chip in enumerate(chips)]
        for j, chip in enumerate(chips):
            copy(1 + j, (*chip, c), me).wait_recv()
            passed[j].start()
        copy(0, sibling, me).wait_recv()
        for j, chip in enumerate(chips):
            copy(4 + j, (*chip, 1 - c), me).wait_recv()
        for cp in first + passed:
            cp.wait_send()
        mine.wait()

    return pl.pallas_call(
        body, name="gather_small_grads",
        out_shape=jax.ShapeDtypeStruct((8, m_per, n), v.dtype),
        in_specs=[pl.BlockSpec(memory_space=pltpu.VMEM)], out_specs=pl.BlockSpec(memory_space=pltpu.VMEM),
        scratch_shapes=[pltpu.SemaphoreType.DMA((7,)), pltpu.SemaphoreType.DMA((7,)), pltpu.SemaphoreType.DMA],
        compiler_params=pltpu.CompilerParams(vmem_limit_bytes=VMEM_LIMIT),
    )(v)


def _sum8(v, *, name):
    _, m, n = v.shape

    def body(v_ref, o_ref):
        acc = v_ref[0]
        for d in range(1, 8):
            acc = acc + v_ref[d]
        o_ref[...] = acc

    return pl.pallas_call(body, name=name, out_shape=jax.ShapeDtypeStruct((m, n), F32),
                          compiler_params=pltpu.CompilerParams(vmem_limit_bytes=VMEM_LIMIT))(v)


def _local_step(x, target, norm_w, q_norm_w, k_norm_w, sinks, a_re, a_im, log_dt, b_re, b_im, c_re, c_im, d_skip,
                b_glu, io):
    seq = x.shape[0]
    qw2 = jnp.tile(q_norm_w.reshape(1, HEAD_DIM), (1, HEADS_PER_TILE))
    kw2 = jnp.tile(k_norm_w.reshape(1, HEAD_DIM), (1, HEADS_PER_TILE))
    nw, bg = norm_w.reshape(1, D_MODEL), b_glu.reshape(1, D_MODEL)
    dsk = d_skip.reshape(1, SSM_W)

    h, rstd = _rms_fwd(x, nw, deps=io.begin())
    proj, w_in4 = io.projection(h)
    attn, lse = _attn2_fwd(proj, qw2, kw2, sinks, deps=io.after_proj(proj))

    def gate_a(at, ag):
        return (at * (ag * _sigmoid(ag)),)

    (ya_in,) = _ew(gate_a, [(attn, "mat", 0), (proj, "mat", OFF_AGATE)], [(ATTN_W, _MXU)], rows=seq, ncol=2,
                   name="ew_attn_gate")
    w_ap4 = io.weight("w_attn_proj", ya_in)
    w_glu4, w_sp4, w_out = io.weight("w_glu", ya_in), io.weight("w_ssm_proj", ya_in), io.weight("w_out", ya_in)
    y_a = _mm(ya_in, w_ap4, mode="nn", name="mm_attn_proj", tm=2048, tn=512, tk=ATTN_W, b_blocked=True)

    flat_a = (a_re.reshape(1, N_STATES), a_im.reshape(1, N_STATES), jnp.repeat(log_dt, STATE).reshape(1, N_STATES))
    coef = _ssm_params_fwd(*flat_a)
    bre_blk, bim_blk = _block_diag_b(b_re).astype(_MXU), _block_diag_b(b_im).astype(_MXU)
    cre_blk, cim_blk = _block_diag_c(c_re).astype(_MXU), _block_diag_c(c_im).astype(_MXU)
    u_scan = _to_scan_order(proj[:, OFF_U * CW:OFF_U * CW + SSM_W])
    y_scan, s_re, s_im, i_re, i_im = _ssm_fwd(u_scan, bre_blk, bim_blk, cre_blk, cim_blk, dsk, coef)
    y_ssm = _from_scan_order(y_scan)

    (yg,) = _ew(lambda yv: (jax.nn.gelu(yv),), [(y_ssm, "mat", 0)], [(SSM_W, _MXU)], rows=seq, ncol=2, name="ew_gelu")
    glu = _mm(yg, w_glu4, mode="nn", name="mm_glu", tm=2048, tn=512, tk=SSM_W, b_blocked=True)

    def gate_s(ga, gb, ba, bb, z):
        return ((ga + ba) * _sigmoid(gb + bb) * (z * _sigmoid(z)),)

    (ys_in,) = _ew(gate_s, [(glu, "mat", 0), (glu, "mat", 2), (bg, "row", 0), (bg, "row", 2), (proj, "mat", OFF_Z)],
                   [(SSM_W, _MXU)], rows=seq, ncol=2, name="ew_ssm_gate")
    y_s = _mm(ys_in, w_sp4, mode="nn", name="mm_ssm_proj", tm=2048, tn=512, tk=SSM_W, b_blocked=True)

    def merge(ga, gs, ya, ys):
        return (_sigmoid(ga) * ya + _sigmoid(gs) * ys,)

    (merged,) = _ew(merge, [(proj, "mat", OFF_GA), (proj, "mat", OFF_GS), (y_a, "mat", 0), (y_s, "mat", 0)],
                    [(D_MODEL, _MXU)], rows=seq, ncol=4, name="ew_merge")
    mo = _mm(merged, w_out, mode="nn", name="mm_out", tm=512, tn=D_MODEL, tk=D_MODEL)

    def loss_head(xv, mv, tv):
        err = (xv + mv) - tv
        dout = err * (1.0 / D_MODEL)
        return dout, dout, _colsum(err * err)

    dout, dout_b, sq = _ew(loss_head, [(x, "mat", 0), (mo, "mat", 0), (target, "mat", 0)],
                           [(D_MODEL, F32), (D_MODEL, _MXU)], rows=seq, ncol=4, n_acc=1, name="ew_loss")
    loss = 0.5 * jnp.sum(sq) / D_MODEL

    d_merged = _mm(dout_b, w_out, mode="nt", name="mm_d_merged", tm=512, tn=D_MODEL, tk=D_MODEL)
    g_w_out = _mm(merged, dout_b, mode="tn", name="mm_g_w_out", tm=1024, tn=D_MODEL, tk=1024, out_dtype=_WIRE)

    def merge_bwd(dm, ga, gs, ya, ys):
        sa, ss = _sigmoid(ga), _sigmoid(gs)
        return sa * dm, ss * dm, dm * ya * sa * (1.0 - sa), dm * ys * ss * (1.0 - ss)

    d_ya, d_ys, d_ga, d_gs = _ew(
        merge_bwd, [(d_merged, "mat", 0), (proj, "mat", OFF_GA), (proj, "mat", OFF_GS), (y_a, "mat", 0), (y_s, "mat", 0)],
        [(D_MODEL, _MXU)] * 4, rows=seq, ncol=4, name="ew_merge_bwd")

    d_ya_in = _mm(d_ya, w_ap4, mode="nt", name="mm_d_attn_gate", tm=2048, tn=ATTN_W, tk=512, b_blocked=True)
    g_w_ap = _mm(ya_in, d_ya, mode="tn", name="mm_g_w_attn_proj", tm=ATTN_W, tn=512, tk=2048, out_dtype=_WIRE,
                 out_blocked=True)

    d_ys_in = _mm(d_ys, w_sp4, mode="nt", name="mm_d_ssm_gate", tm=2048, tn=SSM_W, tk=512, b_blocked=True)
    g_w_sp = _mm(ys_in, d_ys, mode="tn", name="mm_g_w_ssm_proj", tm=SSM_W, tn=512, tk=2048, out_dtype=_WIRE,
                 out_blocked=True)

    def gate_s_bwd(dv, ga, gb, ba, bb, z):
        a, sb = ga + ba, _sigmoid(gb + bb)
        f, df = _silu_and_grad(z)
        dga = dv * sb * f
        dgb = dv * a * f * sb * (1.0 - sb)
        return dga, dgb, dv * a * sb * df, _colsum(dga), _colsum(dgb)

    d_glu_a, d_glu_b, d_z, g_bga, g_bgb = _ew(
        gate_s_bwd, [(d_ys_in, "mat", 0), (glu, "mat", 0), (glu, "mat", 2), (bg, "row", 0), (bg, "row", 2),
                     (proj, "mat", OFF_Z)],
        [(SSM_W, _MXU)] * 3, rows=seq, ncol=2, n_acc=2, name="ew_ssm_gate_bwd")
    d_glu = jnp.concatenate([d_glu_a, d_glu_b], axis=1)
    d_yg = _mm(d_glu, w_glu4, mode="nt", name="mm_d_gelu", tm=2048, tn=SSM_W, tk=512, b_blocked=True)
    g_w_glu = _mm(yg, d_glu, mode="tn", name="mm_g_w_glu", tm=SSM_W, tn=512, tk=2048, out_dtype=_WIRE, out_blocked=True)
    dep = io.later_grads(dict(w_attn_proj=g_w_ap, w_glu=g_w_glu, w_ssm_proj=g_w_sp,
                              w_out=g_w_out.reshape(N_CHIPS, D_MODEL // N_CHIPS, D_MODEL)))

    def gate_a_bwd(dv, at, ag):
        f, df = _silu_and_grad(ag)
        return dv * f, dv * at * df

    d_attn, d_agate = _ew(gate_a_bwd, [(d_ya_in, "mat", 0), (attn, "mat", 0), (proj, "mat", OFF_AGATE)],
                          [(ATTN_W, F32), (ATTN_W, _MXU)], rows=seq, ncol=2, name="ew_attn_gate_bwd", deps=dep)

    def gelu_bwd(dv, yv):
        return (jax.vjp(jax.nn.gelu, yv)[1](dv)[0],)

    (d_yssm,) = _ew(gelu_bwd, [(d_yg, "mat", 0), (y_ssm, "mat", 0)], [(SSM_W, F32)], rows=seq, ncol=2, name="ew_gelu_bwd",
                    deps=dep)
    dep = io.before_attention_backward([d_attn, d_yssm])
    d_qkv, g_qw2, g_kw2, g_sk = _attn2_bwd(proj, qw2, kw2, sinks, lse, d_attn, deps=dep)
    (du_scan, g_bre, g_bim, g_cre, g_cim, g_dsk, g_abr, g_abi, g_cfr, g_cfi) = _ssm_bwd(
        _to_scan_order(d_yssm), u_scan, s_re, s_im, i_re, i_im, bre_blk, bim_blk, cre_blk, cim_blk, dsk, coef)
    g_are, g_aim, g_ldt = _ssm_params_bwd(*flat_a, g_abr, g_abi, g_cfr, g_cfi)
    g_are, g_aim = g_are.reshape(N_GROUPS, STATE), g_aim.reshape(N_GROUPS, STATE)
    g_ldt = g_ldt.reshape(N_GROUPS, STATE).sum(axis=1)
    d_u = _from_scan_order(du_scan)

    d_proj = jnp.concatenate([d_qkv, d_agate, d_u.astype(_MXU), d_z, d_ga, d_gs], axis=1)
    dep = io.before_input_projection_grad([d_proj]) + io.small_grads(dict(
        q_norm_w=g_qw2[0, :HEAD_DIM] + g_qw2[0, HEAD_DIM:], k_norm_w=g_kw2[0, :HEAD_DIM] + g_kw2[0, HEAD_DIM:],
        sinks=g_sk.reshape(N_Q_HEADS), A_re=g_are, A_im=g_aim, log_dt=g_ldt,
        B_re=_diag_of_b(g_bre), B_im=_diag_of_b(g_bim), C_re=_diag_of_c(g_cre), C_im=_diag_of_c(g_cim),
        D_skip=g_dsk.reshape(N_GROUPS, GROUP), b_glu=jnp.concatenate([g_bga, g_bgb], axis=1).reshape(D_MODEL)))
    g_w_in = _mm(h, d_proj, mode="tn", name="mm_g_w_in", tm=1024, tn=IN_W // 4, tk=1024, out_dtype=_WIRE,
                 out_blocked=True, deps=dep)
    dep = io.input_projection_grad(g_w_in)
    d_h = _mm(d_proj, w_in4, mode="nt", name="mm_d_h", tm=512, tn=D_MODEL, tk=IN_W // 4, b_blocked=True, deps=dep)
    grad_x, g_nw = _rms_bwd(d_h, x, rstd, nw, dout)
    return loss, grad_x, g_nw.reshape(D_MODEL)


_SMALL = ["norm_w", "q_norm_w", "k_norm_w", "sinks", "A_re", "A_im", "log_dt", "B_re", "B_im", "C_re", "C_im",
          "D_skip", "b_glu"]
_BIG = ["w_in", "w_attn_proj", "w_glu", "w_ssm_proj", "w_out"]
_LATER = _BIG[1:]
_RELATIONS = ("flip_x", "flip_y", "flip_xy")
_ORDER = ["norm_w", "w_in", "q_norm_w", "k_norm_w", "sinks", "w_attn_proj", "A_re", "A_im", "log_dt", "B_re", "B_im",
          "C_re", "C_im", "D_skip", "w_glu", "b_glu", "w_ssm_proj", "w_out"]
_PACK_W = 1024


def _packed_rows(size):
    unit = SUBLANES * _PACK_W
    return -(-size // unit) * SUBLANES


def _pack_small(d, names):
    parts = []
    for n in names:
        flat = d[n].reshape(-1).astype(F32)
        rows = _packed_rows(flat.shape[0])
        parts.append(jnp.pad(flat, (0, rows * _PACK_W - flat.shape[0])).reshape(rows, _PACK_W))
    return jnp.concatenate(parts, axis=0)


def _unpack_small(packed, like, names):
    out, pos = {}, 0
    for n in names:
        rows = _packed_rows(like[n].size)
        out[n] = packed[pos:pos + rows].reshape(-1)[:like[n].size].reshape(like[n].shape)
        pos += rows
    return out


def _place_block(v, index_arr, *, name):
    rows, cols = v.shape

    def body(i_ref, v_ref, o_ref):
        o_ref[...] = v_ref[...]

    return pl.pallas_call(
        body, name=name,
        grid_spec=pltpu.PrefetchScalarGridSpec(
            num_scalar_prefetch=1, grid=(1,),
            in_specs=[pl.BlockSpec((rows, cols), lambda i, d: (0, 0))],
            out_specs=pl.BlockSpec((None, rows, cols), lambda i, d: (d[0], 0, 0))),
        out_shape=jax.ShapeDtypeStruct((8, rows, cols), v.dtype),
        compiler_params=_params(("arbitrary",)),
    )(index_arr, v)


def _plan_all_to_all(refs):
    (land,) = refs
    x, y, c, _ = _place()
    own = land.at[4 * x + 2 * y + c]
    copies = []
    for fx, fy, fc in [(0, 0, 1), (0, 1, 0), (0, 1, 1), (1, 0, 0), (1, 0, 1), (1, 1, 0), (1, 1, 1)]:
        px, py, pc = (1 - x) if fx else x, (1 - y) if fy else y, (1 - c) if fc else c
        copies.append((own, own, (px, py, pc), land.at[4 * px + 2 * py + pc]))
    return copies


def _as2d(a):
    return a.reshape(1, -1) if a.ndim == 1 else a


def _adamw_whole(w, g, m, v, *, name):
    shape = w.shape
    w, g, m, v = _as2d(w), _as2d(g), _as2d(m), _as2d(v)

    def body(w_ref, g_ref, m_ref, v_ref, d_ref, nm_ref, nv_ref):
        d_ref[...], nm_ref[...], nv_ref[...] = _adamw_math(w_ref[...], g_ref[...], m_ref[...], v_ref[...])

    outs = pl.pallas_call(body, name=name, out_shape=[jax.ShapeDtypeStruct(w.shape, F32)] * 3)(w, g, m, v)
    return [o.reshape(shape) for o in outs]


class _Exchanges:
    def __init__(self, w, m, v):
        self.w, self.m, self.v = w, m, v
        self.grads, self.delta, self.new_m, self.new_v = {}, {}, {}, {}

    def _adamw(self, names, deps):
        for n in names:
            self.delta[n], self.new_m[n], self.new_v[n] = _adamw(
                self.w[n], self.grads[n], self.m[n], self.v[n], name=f"adamw_{n}", tm=128, deps=deps)

    def begin(self):
        chip = (2 * lax.axis_index("x") + lax.axis_index("y")).astype(jnp.int32).reshape(1)
        full = {n: _place_shard(self.w[n], chip, name=f"place_{n}") for n in _BIG}
        bufs, token, self.w_in_sems = [full["w_in"]], None, []
        for k, tag in enumerate(_RELATIONS):
            sems, bufs, token = _copies_start(f"gather_ici_w_in_{tag}_start", bufs, functools.partial(_plan_gather_ici, which=(k,)),
                                              1, after=() if token is None else [token])
            self.w_in_sems.append(sems)
        self.w_in_buf = bufs
        self.rest = _copies_start("gather_ici_rest_start", [full[n] for n in _LATER], _plan_gather_ici,
                                  3 * len(_LATER), after=[token])
        return [self.rest[2]]

    def projection(self, h):
        x, y = lax.axis_index("x"), lax.axis_index("y")
        blks = [jnp.asarray(b, jnp.int32).reshape(1)
                for b in (2 * x + y, 2 * (1 - x) + y, 2 * x + (1 - y), 2 * (1 - x) + (1 - y))]
        def landed(bufs, k, after):
            plan = functools.partial(_plan_gather_ici, which=(k,))
            bufs = _copies_wait(f"gather_ici_w_in_{_RELATIONS[k]}_wait", bufs, self.w_in_sems[k], plan, after)
            plan = functools.partial(_plan_gather_d2d, which=(k,))
            sems, bufs, token = _copies_start(f"gather_d2d_w_in_{_RELATIONS[k]}_start", bufs, plan, 1)
            return bufs, (sems, plan, token)

        def handed(bufs, k, pending, after):
            sems, plan, _ = pending
            return _copies_wait(f"gather_d2d_w_in_{_RELATIONS[k]}_wait", bufs, sems, plan, after)

        bufs = self.w_in_buf
        proj = _mm_chip_block(h, bufs[0], blks[0], None, name="mm_proj_own")
        bufs, d2d_x = landed(bufs, 0, [proj])
        bufs, d2d_y = landed(bufs, 1, [d2d_x[2]])
        bufs = handed(bufs, 0, d2d_x, [d2d_y[2]])
        proj = _mm_chip_block(h, bufs[0], blks[1], proj, name="mm_proj_flip_x")
        bufs = handed(bufs, 1, d2d_y, [proj])
        proj = _mm_chip_block(h, bufs[0], blks[2], proj, name="mm_proj_flip_y")
        bufs, d2d_xy = landed(bufs, 2, [proj])
        bufs = handed(bufs, 2, d2d_xy, [d2d_xy[2]])
        proj = _mm_chip_block(h, bufs[0], blks[3], proj, name="mm_proj_flip_xy")
        return proj, bufs[0]

    def weight(self, name, after):
        if self.rest is not None:
            sems, bufs = self.rest
            later = dict(zip(_LATER, _copies_wait("gather_d2d_rest_wait", bufs, sems, _plan_gather_d2d, [after])))
            later["w_out"] = later["w_out"].reshape(D_MODEL, D_MODEL)
            self.later, self.rest = later, None
        return self.later[name]

    def after_proj(self, proj):
        sems, bufs, _ = self.rest
        bufs = _copies_wait("gather_ici_rest_wait", bufs, sems, _plan_gather_ici, [proj])
        sems, bufs, token = _copies_start("gather_d2d_rest_start", bufs, _plan_gather_d2d, 3 * len(_LATER))
        self.rest = (sems, bufs)
        return [token]

    def later_grads(self, grads):
        self.rs_later = _ReduceScatter("later", _LATER, [grads[n] for n in _LATER])
        return self.rs_later.start_swap()

    def before_attention_backward(self, after):
        return self.rs_later.start_scatter(after)

    def before_input_projection_grad(self, after):
        return self.rs_later.start_join(after)

    def input_projection_grad(self, g_w_in):
        self.grads.update(self.rs_later.finish([g_w_in]))
        self.rs_in = _ReduceScatter("w_in", ["w_in"], [g_w_in])
        self._adamw(_LATER, self.rs_in.start_swap())
        return self.rs_in.start_scatter([self.delta[n] for n in _LATER])

    def _adamw_small(self, names):
        for n in names:
            self.delta[n], self.new_m[n], self.new_v[n] = _adamw_whole(
                self.w[n], self.grads[n], self.m[n], self.v[n], name=f"adamw_{n}")

    def small_grads(self, grads):
        me = (4 * lax.axis_index("x") + 2 * lax.axis_index("y") + lax.axis_index("c")).astype(jnp.int32).reshape(1)
        land = _place_block(_pack_small(grads, _SMALL[1:]), me, name="place_small_grads")
        self.small = _copies_start("gather_small_start", [land], _plan_all_to_all, 7)
        return [self.small[2]]

    def finish(self, g_norm_w, after):
        sems, bufs, _ = self.small
        (land,) = _copies_wait("gather_small_wait", bufs, sems, _plan_all_to_all, after)
        self.grads.update(_unpack_small(_sum8(land, name="sum_small_grads"), self.w, _SMALL[1:]))
        self._adamw_small(_SMALL[1:])
        late = _sum8(_all_gather_small(_pack_small(dict(norm_w=g_norm_w), _SMALL[:1])), name="sum_norm_w_grad")
        self.grads.update(_unpack_small(late, self.w, _SMALL[:1]))
        self._adamw_small(_SMALL[:1])
        self.grads.update(self.rs_in.finish(self.rs_in.start_join([self.delta[_SMALL[0]]])))
        self._adamw(["w_in"], ())


def kernel(x, norm_w, w_in, q_norm_w, k_norm_w, sinks, w_attn_proj, A_re, A_im, log_dt, B_re, B_im, C_re, C_im, D_skip, w_glu, b_glu, w_ssm_proj, w_out, loss_target, m_norm_w, m_w_in, m_q_norm_w, m_k_norm_w, m_sinks, m_w_attn_proj, m_A_re, m_A_im, m_log_dt, m_B_re, m_B_im, m_C_re, m_C_im, m_D_skip, m_w_glu, m_b_glu, m_w_ssm_proj, m_w_out, v_norm_w, v_w_in, v_q_norm_w, v_k_norm_w, v_sinks, v_w_attn_proj, v_A_re, v_A_im, v_log_dt, v_B_re, v_B_im, v_C_re, v_C_im, v_D_skip, v_w_glu, v_b_glu, v_w_ssm_proj, v_w_out):
    w = dict(norm_w=norm_w, w_in=w_in, q_norm_w=q_norm_w, k_norm_w=k_norm_w, sinks=sinks, w_attn_proj=w_attn_proj,
             A_re=A_re, A_im=A_im, log_dt=log_dt, B_re=B_re, B_im=B_im, C_re=C_re, C_im=C_im, D_skip=D_skip,
             w_glu=w_glu, b_glu=b_glu, w_ssm_proj=w_ssm_proj, w_out=w_out)
    m = dict(norm_w=m_norm_w, w_in=m_w_in, q_norm_w=m_q_norm_w, k_norm_w=m_k_norm_w, sinks=m_sinks,
             w_attn_proj=m_w_attn_proj, A_re=m_A_re, A_im=m_A_im, log_dt=m_log_dt, B_re=m_B_re, B_im=m_B_im,
             C_re=m_C_re, C_im=m_C_im, D_skip=m_D_skip, w_glu=m_w_glu, b_glu=m_b_glu, w_ssm_proj=m_w_ssm_proj,
             w_out=m_w_out)
    v = dict(norm_w=v_norm_w, w_in=v_w_in, q_norm_w=v_q_norm_w, k_norm_w=v_k_norm_w, sinks=v_sinks,
             w_attn_proj=v_w_attn_proj, A_re=v_A_re, A_im=v_A_im, log_dt=v_log_dt, B_re=v_B_re, B_im=v_B_im,
             C_re=v_C_re, C_im=v_C_im, D_skip=v_D_skip, w_glu=v_w_glu, b_glu=v_b_glu, w_ssm_proj=v_w_ssm_proj,
             w_out=v_w_out)

    io = _Exchanges(w, m, v)
    loss, grad_x, g_norm_w = _local_step(x[0], loss_target[0], norm_w, q_norm_w, k_norm_w, sinks, A_re, A_im, log_dt,
                                         B_re, B_im, C_re, C_im, D_skip, b_glu, io)
    loss = lax.psum(loss, ("x", "y", "c"))
    io.finish(g_norm_w, [grad_x])
    grads, delta, new_m, new_v = io.grads, io.delta, io.new_m, io.new_v

    return (loss, grad_x[None], *[grads[n] for n in _ORDER], *[delta[n] for n in _ORDER],
            *[new_m[n] for n in _ORDER], *[new_v[n] for n in _ORDER])
```

```python
import functools
import math

import jax
import jax.numpy as jnp
from jax import lax
from jax.experimental import pallas as pl
from jax.experimental.pallas import tpu as pltpu

F32 = jnp.float32
_MXU = jnp.bfloat16
_WIRE = jnp.bfloat16

LANES = 128
SUBLANES = 8
VMEM_LIMIT = 56 * 1024 * 1024

D_MODEL = 2048
HEAD_DIM = 64
N_Q_HEADS = 16
N_KV_HEADS = 4
Q_PER_KV = 4
ATTN_W = 1024
KV_W = 256
WINDOW = 128
SSM_W = 1024
GROUP = 16
N_GROUPS = 64
STATE = 64
N_STATES = N_GROUPS * STATE
IN_W = 8704
NORM_EPS = 1e-6
N_CHIPS = 4
CW = 512
OFF_AGATE, OFF_U, OFF_Z, OFF_GA, OFF_GS = 3, 5, 7, 9, 13

SSM_T = 256
SSM_L = SSM_T // SUBLANES
SSM_JB = 8
SSM_SB = N_STATES // SSM_JB

ADAM_LR, ADAM_B1, ADAM_B2, ADAM_EPS, ADAM_WD, ADAM_STEP = 0.001, 0.9, 0.999, 1e-08, 0.01, 10

MESH = pl.DeviceIdType.MESH
_ANY = pl.BlockSpec(memory_space=pl.ANY)


def _params(sem=None):
    return pltpu.CompilerParams(dimension_semantics=sem, vmem_limit_bytes=VMEM_LIMIT)


def _mm(a, b, *, mode, name, tm, tn, tk, out_dtype=F32, b_blocked=False, out_blocked=False, deps=()):
    nd = len(deps)
    if mode == "tn":
        K, M = a.shape
    else:
        M, K = a.shape
    if mode == "nn":
        N = b.shape[0] * b.shape[2] if b_blocked else b.shape[1]
    elif mode == "nt":
        N = b.shape[1] if b_blocked else b.shape[0]
    else:
        N = b.shape[1]
    tm, tn, tk = min(tm, M), min(tn, N), min(tk, K)
    nj, ni, nk = N // tn, M // tm, K // tk
    assert nj * tn == N and ni * tm == M and nk * tk == K, (name, M, N, K)
    dims = {"nn": (((1,), (0,)), ((), ())), "nt": (((1,), (1,)), ((), ())), "tn": (((0,), (0,)), ((), ()))}[mode]

    if mode == "tn":
        a_spec = pl.BlockSpec((tk, tm), lambda j, i, k: (k, i))
    else:
        a_spec = pl.BlockSpec((tm, tk), lambda j, i, k: (i, k))
    if mode == "nn":
        if b_blocked:
            assert b.shape[0] == nj and b.shape[2] == tn
            b_spec = pl.BlockSpec((None, tk, tn), lambda j, i, k: (j, k, 0))
        else:
            b_spec = pl.BlockSpec((tk, tn), lambda j, i, k: (k, j))
    elif mode == "nt":
        if b_blocked:
            assert b.shape[0] == nk and b.shape[2] == tk
            b_spec = pl.BlockSpec((None, tn, tk), lambda j, i, k: (k, j, 0))
        else:
            b_spec = pl.BlockSpec((tn, tk), lambda j, i, k: (j, k))
    else:
        b_spec = pl.BlockSpec((tk, tn), lambda j, i, k: (k, j))
    if out_blocked:
        assert nj == N_CHIPS
        o_spec = pl.BlockSpec((None, tm, tn), lambda j, i, k: (j, i, 0))
        o_shape = jax.ShapeDtypeStruct((nj, M, tn), out_dtype)
    else:
        o_spec = pl.BlockSpec((tm, tn), lambda j, i, k: (i, j))
        o_shape = jax.ShapeDtypeStruct((M, N), out_dtype)
    use_acc = nk > 1 and out_dtype != F32

    def body(a_ref, b_ref, *rest):
        o_ref, scratch = rest[nd], rest[nd + 1:]
        part = lax.dot_general(a_ref[...].astype(_MXU), b_ref[...].astype(_MXU), dims,
                               preferred_element_type=F32)
        if nk == 1:
            o_ref[...] = part.astype(o_ref.dtype)
            return
        k = pl.program_id(2)
        acc = scratch[0] if use_acc else o_ref

        @pl.when(k == 0)
        def _():
            acc[...] = part

        @pl.when(k > 0)
        def _():
            acc[...] += part

        if use_acc:
            @pl.when(k == nk - 1)
            def _():
                o_ref[...] = acc[...].astype(o_ref.dtype)

    return pl.pallas_call(
        body, name=name, grid=(nj, ni, nk), in_specs=[a_spec, b_spec] + [_ANY] * nd, out_specs=o_spec,
        out_shape=o_shape, scratch_shapes=[pltpu.VMEM((tm, tn), F32)] if use_acc else [],
        compiler_params=_params(("parallel", "parallel", "arbitrary")),
    )(a, b, *deps)


def _mm_chip_block(a, b4, blk, prev, *, name, tm=512, deps=()):
    M, K = a.shape
    nchip, _, C = b4.shape
    tm = min(tm, M)
    extra = ([] if prev is None else [prev]) + list(deps)

    def body(blk_ref, a_ref, b_ref, *rest):
        rest[-1][...] = jnp.dot(a_ref[...].astype(_MXU), b_ref[...].astype(_MXU), preferred_element_type=F32)

    return pl.pallas_call(
        body, name=name,
        grid_spec=pltpu.PrefetchScalarGridSpec(
            num_scalar_prefetch=1, grid=(M // tm,),
            in_specs=[pl.BlockSpec((tm, K), lambda i, c: (i, 0)), pl.BlockSpec((None, K, C), lambda i, c: (c[0], 0, 0))]
            + [_ANY] * len(extra),
            out_specs=pl.BlockSpec((tm, C), lambda i, c: (i, c[0]))),
        out_shape=jax.ShapeDtypeStruct((M, nchip * C), F32),
        input_output_aliases={} if prev is None else {3: 0},
        compiler_params=_params(("arbitrary",)),
    )(blk, a, b4, *extra)


def _ew(fn, ins, outs, *, rows, ncol, name, n_acc=0, tm=512, deps=()):
    n_in, n_out, nd = len(ins), len(outs), len(deps)
    tm = min(tm, rows)
    in_specs = []
    for _, kind, col0 in ins:
        if kind == "mat":
            in_specs.append(pl.BlockSpec((tm, CW), lambda j, i, c0=col0: (i, c0 + j)))
        else:
            in_specs.append(pl.BlockSpec((1, CW), lambda j, i, c0=col0: (0, c0 + j)))
    out_specs = [pl.BlockSpec((tm, CW), lambda j, i: (i, j)) for _ in outs]
    out_shape = [jax.ShapeDtypeStruct((rows, w), dt) for w, dt in outs]
    for _ in range(n_acc):
        out_specs.append(pl.BlockSpec((1, CW), lambda j, i: (0, j)))
        out_shape.append(jax.ShapeDtypeStruct((1, ncol * CW), F32))

    def body(*refs):
        vals = fn(*[r[...] for r in refs[:n_in]])
        refs = refs[n_in + nd:]
        for r, v in zip(refs[:n_out], vals[:n_out]):
            r[...] = v.astype(r.dtype)
        i = pl.program_id(1)
        for r, v in zip(refs[n_out:], vals[n_out:]):
            @pl.when(i == 0)
            def _(r=r, v=v):
                r[...] = v

            @pl.when(i > 0)
            def _(r=r, v=v):
                r[...] += v

    res = pl.pallas_call(
        body, name=name, grid=(ncol, rows // tm), in_specs=in_specs + [_ANY] * nd, out_specs=out_specs,
        out_shape=out_shape, compiler_params=_params(("parallel", "arbitrary")),
    )(*[a for a, _, _ in ins], *deps)
    return res


def _colsum(v):
    return jnp.sum(v, axis=0, keepdims=True)


def _sigmoid(v):
    return jax.nn.sigmoid(v)


def _silu_and_grad(v):
    s = _sigmoid(v)
    return v * s, s * (1.0 + v * (1.0 - s))


def _rms_fwd(x, w, *, tm=512, deps=()):
    rows, d = x.shape
    nd = len(deps)

    def body(x_ref, w_ref, *rest):
        h_ref, r_ref = rest[nd:]
        xv = x_ref[...]
        r = lax.rsqrt(jnp.mean(xv * xv, axis=-1, keepdims=True) + NORM_EPS)
        h_ref[...] = (xv * r * w_ref[...]).astype(h_ref.dtype)
        r_ref[...] = r

    return pl.pallas_call(
        body, name="rms_fwd", grid=(rows // tm,),
        in_specs=[pl.BlockSpec((tm, d), lambda i: (i, 0)), pl.BlockSpec((1, d), lambda i: (0, 0))] + [_ANY] * nd,
        out_specs=[pl.BlockSpec((tm, d), lambda i: (i, 0)), pl.BlockSpec((tm, 1), lambda i: (i, 0))],
        out_shape=[jax.ShapeDtypeStruct((rows, d), _MXU), jax.ShapeDtypeStruct((rows, 1), F32)],
        compiler_params=_params(("arbitrary",)),
    )(x, w, *deps)


def _rms_bwd(dh, x, rstd, w, dout, *, tm=256):
    rows, d = x.shape

    def body(dh_ref, x_ref, r_ref, w_ref, do_ref, gx_ref, gw_ref):
        dhv, xv, r, wv = dh_ref[...], x_ref[...], r_ref[...], w_ref[...]
        xr = xv * r
        t = jnp.mean(dhv * wv * xr, axis=-1, keepdims=True)
        gx_ref[...] = do_ref[...] + r * (wv * dhv - xr * t)
        part = _colsum(dhv * xr)
        i = pl.program_id(0)

        @pl.when(i == 0)
        def _():
            gw_ref[...] = part

        @pl.when(i > 0)
        def _():
            gw_ref[...] += part

    return pl.pallas_call(
        body, name="rms_bwd", grid=(rows // tm,),
        in_specs=[pl.BlockSpec((tm, d), lambda i: (i, 0)), pl.BlockSpec((tm, d), lambda i: (i, 0)),
                  pl.BlockSpec((tm, 1), lambda i: (i, 0)), pl.BlockSpec((1, d), lambda i: (0, 0)),
                  pl.BlockSpec((tm, d), lambda i: (i, 0))],
        out_specs=[pl.BlockSpec((tm, d), lambda i: (i, 0)), pl.BlockSpec((1, d), lambda i: (0, 0))],
        out_shape=[jax.ShapeDtypeStruct((rows, d), F32), jax.ShapeDtypeStruct((1, d), F32)],
        compiler_params=_params(("arbitrary",)),
    )(dh, x, rstd, w, dout)


_NT = (((1,), (1,)), ((), ()))
_TN = (((0,), (0,)), ((), ()))


QKV_W = ATTN_W + 2 * KV_W
HEADS_PER_TILE = LANES // HEAD_DIM


def _low_half(rows):
    return lax.broadcasted_iota(jnp.int32, (rows, LANES), 1) < HEAD_DIM


def _pair_mean(t, low):
    m_lo = jnp.sum(jnp.where(low, t, 0.0), axis=-1, keepdims=True)
    m_hi = jnp.sum(jnp.where(low, 0.0, t), axis=-1, keepdims=True)
    return jnp.where(low, m_lo, m_hi) * (1.0 / HEAD_DIM)


def _pair_rstd(t, low):
    return lax.rsqrt(_pair_mean(t * t, low) + NORM_EPS)


def _dup_half(t, hi, low):
    swapped = pltpu.roll(t, HEAD_DIM, 1)
    return jnp.where(low, swapped, t) if hi else jnp.where(low, t, swapped)


def _fold_halves(t):
    return t + pltpu.roll(t, HEAD_DIM, 1)


def _split_heads(t, low):
    return [jnp.where(low, t, 0.0), jnp.where(low, 0.0, t)]


def _stacked_band_mask(n):
    rows = Q_PER_KV * WINDOW
    qi = lax.broadcasted_iota(jnp.int32, (rows, 2 * WINDOW), 0) % WINDOW + WINDOW
    kj = lax.broadcasted_iota(jnp.int32, (rows, 2 * WINDOW), 1)
    diff = qi - kj
    first_key = jnp.where(n > 0, 0, WINDOW)
    return (diff >= 0) & (diff < WINDOW) & (kj >= first_key)


def _stacked_sinks(sink_ref, g):
    blk = lax.broadcasted_iota(jnp.int32, (Q_PER_KV * WINDOW, 1), 0) // WINDOW
    col = jnp.full((Q_PER_KV * WINDOW, 1), sink_ref[Q_PER_KV * g], F32)
    for r in range(1, Q_PER_KV):
        col = jnp.where(blk == r, sink_ref[Q_PER_KV * g + r], col)
    return col


def _attn_in_specs(nblk, rev):
    def cur(n):
        return (nblk - 1 - n) if rev else n

    q_spec = pl.BlockSpec((WINDOW, ATTN_W), lambda n: (cur(n), 0))
    kvc_spec = pl.BlockSpec((WINDOW, 2 * KV_W), lambda n: (cur(n), ATTN_W // (2 * KV_W)))
    kvp_spec = pl.BlockSpec((WINDOW, 2 * KV_W), lambda n: (jnp.maximum(cur(n) - 1, 0), ATTN_W // (2 * KV_W)))
    w_spec = pl.BlockSpec((1, LANES), lambda n: (0, 0))
    l_spec = pl.BlockSpec((WINDOW, N_Q_HEADS), lambda n: (cur(n), 0))
    return q_spec, kvc_spec, kvp_spec, w_spec, l_spec


def _attn2_fwd(proj, qw2, kw2, sinks, deps=()):
    seq = proj.shape[0]
    nblk = seq // WINDOW
    scale = 1.0 / math.sqrt(HEAD_DIM)
    q_spec, kvc_spec, kvp_spec, w_spec, l_spec = _attn_in_specs(nblk, False)
    nd = len(deps)

    def body(sink_ref, q_ref, kvc_ref, kvp_ref, qw_ref, kw_ref, *rest):
        o_ref, lse_ref = rest[nd:]
        n = pl.program_id(0)
        low, low2 = _low_half(WINDOW), _low_half(2 * WINDOW)
        valid = _stacked_band_mask(n)
        head_lane = lax.broadcasted_iota(jnp.int32, (WINDOW, N_Q_HEADS), 1)
        kv = jnp.concatenate([kvp_ref[...], kvc_ref[...]], axis=0)
        qwv, kwv = qw_ref[...], kw_ref[...]
        lse_blk = jnp.zeros((WINDOW, N_Q_HEADS), F32)
        for t in range(N_KV_HEADS // HEADS_PER_TILE):
            kt = kv[:, t * LANES:(t + 1) * LANES]
            vt = kv[:, KV_W + t * LANES:KV_W + (t + 1) * LANES]
            kn = kt * _pair_rstd(kt, low2) * kwv
            for hi in range(HEADS_PER_TILE):
                g = HEADS_PER_TILE * t + hi
                kdup = _dup_half(kn, hi, low2).astype(_MXU)
                vdup = _dup_half(vt, hi, low2).astype(_MXU)
                stack = []
                for tq in (2 * g, 2 * g + 1):
                    qt = q_ref[:, tq * LANES:(tq + 1) * LANES]
                    stack += _split_heads(qt * _pair_rstd(qt, low) * qwv, low)
                qs = jnp.concatenate(stack, axis=0).astype(_MXU)
                s = lax.dot_general(qs, kdup, _NT, preferred_element_type=F32) * scale
                s = jnp.where(valid, s, -1e30)
                sink = _stacked_sinks(sink_ref, g)
                m = jnp.maximum(jnp.max(s, axis=-1, keepdims=True), sink)
                e = jnp.exp(s - m)
                z = jnp.sum(e, axis=-1, keepdims=True) + jnp.exp(sink - m)
                o = jnp.dot((e / z).astype(_MXU), vdup, preferred_element_type=F32)
                for i, tq in enumerate((2 * g, 2 * g + 1)):
                    o_ref[:, tq * LANES:(tq + 1) * LANES] = jnp.where(
                        low, o[2 * i * WINDOW:(2 * i + 1) * WINDOW], o[(2 * i + 1) * WINDOW:(2 * i + 2) * WINDOW])
                lse = m + jnp.log(z)
                for r in range(Q_PER_KV):
                    lse_blk = jnp.where(head_lane == Q_PER_KV * g + r, lse[r * WINDOW:(r + 1) * WINDOW], lse_blk)
        lse_ref[...] = lse_blk

    return pl.pallas_call(
        body, name="attn_fwd", grid=(nblk,),
        in_specs=[pl.BlockSpec(memory_space=pltpu.SMEM), q_spec, kvc_spec, kvp_spec, w_spec, w_spec] + [_ANY] * nd,
        out_specs=[q_spec, l_spec],
        out_shape=[jax.ShapeDtypeStruct((seq, ATTN_W), F32), jax.ShapeDtypeStruct((seq, N_Q_HEADS), F32)],
        compiler_params=_params(("arbitrary",)),
    )(sinks, proj, proj, proj, qw2, kw2, *deps)


def _attn2_bwd(proj, qw2, kw2, sinks, lse, do, deps=()):
    seq = proj.shape[0]
    nblk = seq // WINDOW
    scale = 1.0 / math.sqrt(HEAD_DIM)
    q_spec, kvc_spec, kvp_spec, w_spec, l_spec = _attn_in_specs(nblk, True)
    s_spec = pl.BlockSpec((1, N_Q_HEADS), lambda n: (0, 0))
    d_spec = pl.BlockSpec((WINDOW, QKV_W), lambda n: (nblk - 1 - n, 0))
    nd = len(deps)

    def body(sink_ref, q_ref, kvc_ref, kvp_ref, qw_ref, kw_ref, lse_ref, do_ref, *rest):
        d_ref, dqw_ref, dkw_ref, dsk_ref, carry = rest[nd:]
        step = pl.program_id(0)
        n = nblk - 1 - step

        @pl.when(step == 0)
        def _():
            carry[...] = jnp.zeros_like(carry)
            dqw_ref[...] = jnp.zeros_like(dqw_ref)
            dkw_ref[...] = jnp.zeros_like(dkw_ref)
            dsk_ref[...] = jnp.zeros_like(dsk_ref)

        low, low2 = _low_half(WINDOW), _low_half(2 * WINDOW)
        valid = _stacked_band_mask(n)
        head_lane = lax.broadcasted_iota(jnp.int32, (WINDOW, N_Q_HEADS), 1)
        sink_lane = lax.broadcasted_iota(jnp.int32, (1, N_Q_HEADS), 1)
        kv = jnp.concatenate([kvp_ref[...], kvc_ref[...]], axis=0)
        qwv, kwv = qw_ref[...], kw_ref[...]
        lse_blk = lse_ref[...]
        dqw = jnp.zeros((1, LANES), F32)
        dkw = jnp.zeros((1, LANES), F32)
        dsk = jnp.zeros((1, N_Q_HEADS), F32)
        for t in range(N_KV_HEADS // HEADS_PER_TILE):
            kt = kv[:, t * LANES:(t + 1) * LANES]
            vt = kv[:, KV_W + t * LANES:KV_W + (t + 1) * LANES]
            rk = _pair_rstd(kt, low2)
            kn = kt * rk * kwv
            dkn_t = jnp.zeros((2 * WINDOW, LANES), F32)
            dv_t = jnp.zeros((2 * WINDOW, LANES), F32)
            for hi in range(HEADS_PER_TILE):
                g = HEADS_PER_TILE * t + hi
                kdup = _dup_half(kn, hi, low2).astype(_MXU)
                vdup = _dup_half(vt, hi, low2).astype(_MXU)
                tiles = (2 * g, 2 * g + 1)
                qx, rq, stack, dstack, lse_rows = [], [], [], [], []
                for tq in tiles:
                    qt = q_ref[:, tq * LANES:(tq + 1) * LANES]
                    r = _pair_rstd(qt, low)
                    rq.append(r)
                    qx.append(qt * r)
                    stack += _split_heads(qx[-1] * qwv, low)
                    dstack += _split_heads(do_ref[:, tq * LANES:(tq + 1) * LANES], low)
                for r in range(Q_PER_KV):
                    lse_rows.append(jnp.sum(jnp.where(head_lane == Q_PER_KV * g + r, lse_blk, 0.0), axis=-1, keepdims=True))
                qs = jnp.concatenate(stack, axis=0).astype(_MXU)
                dos = jnp.concatenate(dstack, axis=0).astype(_MXU)
                lse_col = jnp.concatenate(lse_rows, axis=0)
                s = lax.dot_general(qs, kdup, _NT, preferred_element_type=F32) * scale
                s = jnp.where(valid, s, -1e30)
                p = jnp.exp(s - lse_col)
                dp = lax.dot_general(dos, vdup, _NT, preferred_element_type=F32)
                dsum = jnp.sum(p * dp, axis=-1, keepdims=True)
                ds = (p * (dp - dsum) * scale).astype(_MXU)
                dsink = -jnp.exp(_stacked_sinks(sink_ref, g) - lse_col) * dsum
                for r in range(Q_PER_KV):
                    dsk = dsk + jnp.where(sink_lane == Q_PER_KV * g + r, _colsum(dsink[r * WINDOW:(r + 1) * WINDOW]), 0.0)
                dv_g = _fold_halves(lax.dot_general(p.astype(_MXU), dos, _TN, preferred_element_type=F32))
                dkn_g = _fold_halves(lax.dot_general(ds, qs, _TN, preferred_element_type=F32))
                dv_t = jnp.where(low2, dv_t, dv_g) if hi else jnp.where(low2, dv_g, dv_t)
                dkn_t = jnp.where(low2, dkn_t, dkn_g) if hi else jnp.where(low2, dkn_g, dkn_t)
                dqn = jnp.dot(ds, kdup, preferred_element_type=F32)
                for i, tq in enumerate(tiles):
                    dqn_t = jnp.where(low, dqn[2 * i * WINDOW:(2 * i + 1) * WINDOW],
                                      dqn[(2 * i + 1) * WINDOW:(2 * i + 2) * WINDOW])
                    dq = rq[i] * (qwv * dqn_t - qx[i] * _pair_mean(dqn_t * qwv * qx[i], low))
                    d_ref[:, tq * LANES:(tq + 1) * LANES] = dq.astype(d_ref.dtype)
                    dqw = dqw + _colsum(dqn_t * qx[i])
            k_cols = slice(t * LANES, (t + 1) * LANES)
            v_cols = slice(KV_W + t * LANES, KV_W + (t + 1) * LANES)
            dkn_c = dkn_t[WINDOW:] + carry[:, k_cols]
            rc = rk[WINDOW:]
            kx = kt[WINDOW:] * rc
            dk = rc * (kwv * dkn_c - kx * _pair_mean(dkn_c * kwv * kx, low))
            d_ref[:, ATTN_W + t * LANES:ATTN_W + (t + 1) * LANES] = dk.astype(d_ref.dtype)
            d_ref[:, ATTN_W + KV_W + t * LANES:ATTN_W + KV_W + (t + 1) * LANES] = (
                dv_t[WINDOW:] + carry[:, v_cols]).astype(d_ref.dtype)
            carry[:, k_cols] = dkn_t[:WINDOW]
            carry[:, v_cols] = dv_t[:WINDOW]
            dkw = dkw + _colsum(dkn_c * kx)
        dqw_ref[...] += dqw
        dkw_ref[...] += dkw
        dsk_ref[...] += dsk

    return pl.pallas_call(
        body, name="attn_bwd", grid=(nblk,),
        in_specs=[pl.BlockSpec(memory_space=pltpu.SMEM), q_spec, kvc_spec, kvp_spec, w_spec, w_spec, l_spec, q_spec]
        + [_ANY] * nd,
        out_specs=[d_spec, w_spec, w_spec, s_spec],
        out_shape=[jax.ShapeDtypeStruct((seq, QKV_W), _MXU), jax.ShapeDtypeStruct((1, LANES), F32),
                   jax.ShapeDtypeStruct((1, LANES), F32), jax.ShapeDtypeStruct((1, N_Q_HEADS), F32)],
        scratch_shapes=[pltpu.VMEM((WINDOW, 2 * KV_W), F32)],
        compiler_params=_params(("arbitrary",)),
    )(sinks, proj, proj, proj, qw2, kw2, lse, do, *deps)


def _ssm_discretise(a_re, a_im, log_dt):
    dt = jnp.exp(log_dt)
    mag = jnp.exp(dt * a_re)
    ab_re = mag * jnp.cos(dt * a_im)
    ab_im = mag * jnp.sin(dt * a_im)
    num_re = ab_re - 1.0
    num_im = ab_im
    den = a_re * a_re + a_im * a_im
    cf_re = (num_re * a_re + num_im * a_im) / den
    cf_im = (num_im * a_re - num_re * a_im) / den
    return ab_re, ab_im, cf_re, cf_im


def _ssm_params_fwd(a_re, a_im, log_dt):
    shp = jax.ShapeDtypeStruct(a_re.shape, F32)

    def body(are_ref, aim_ref, ldt_ref, abr_ref, abi_ref, cfr_ref, cfi_ref, alr_ref, ali_ref):
        abr, abi, cfr, cfi = _ssm_discretise(are_ref[...], aim_ref[...], ldt_ref[...])
        abr_ref[...], abi_ref[...], cfr_ref[...], cfi_ref[...] = abr, abi, cfr, cfi
        pr, pi = abr, abi
        for _ in range(int(math.log2(SSM_L))):
            pr, pi = pr * pr - pi * pi, 2.0 * pr * pi
        alr_ref[...], ali_ref[...] = pr, pi

    return pl.pallas_call(body, name="ssm_params_fwd", out_shape=[shp] * 6)(a_re, a_im, log_dt)


def _ssm_params_bwd(a_re, a_im, log_dt, d_abr, d_abi, d_cfr, d_cfi):
    def body(are_ref, aim_ref, ldt_ref, g0, g1, g2, g3, dare_ref, daim_ref, dldt_ref):
        _, vjp = jax.vjp(_ssm_discretise, are_ref[...], aim_ref[...], ldt_ref[...])
        dare_ref[...], daim_ref[...], dldt_ref[...] = vjp((g0[...], g1[...], g2[...], g3[...]))

    return pl.pallas_call(
        body, name="ssm_params_bwd",
        out_shape=[jax.ShapeDtypeStruct(a_re.shape, F32), jax.ShapeDtypeStruct(a_im.shape, F32),
                   jax.ShapeDtypeStruct(log_dt.shape, F32)],
    )(a_re, a_im, log_dt, d_abr, d_abi, d_cfr, d_cfi)


def _scan_cols(j):
    return pl.ds(j * SSM_SB, SSM_SB)


def _rows8(r):
    return pl.ds(pl.multiple_of(r * SUBLANES, SUBLANES), SUBLANES)


def _bcast8(row):
    return jnp.broadcast_to(row, (SUBLANES, row.shape[-1]))


def _ssm_fwd(u, b_re, b_im, c_re, c_im, d_skip, coef):
    seq = u.shape[0]
    nc = seq // SSM_T
    T, L = SSM_T, SSM_L

    def body(u_ref, bre_ref, bim_ref, cre_ref, cim_ref, d_ref, are_ref, aim_ref, cfr_ref, cfi_ref, alr_ref, ali_ref,
             y_ref, sre_ref, sim_ref, ire_ref, iim_ref, car_re, car_im, end_re, end_im):
        c = pl.program_id(0)

        @pl.when(c == 0)
        def _():
            car_re[...] = jnp.zeros_like(car_re)
            car_im[...] = jnp.zeros_like(car_im)

        for j in range(SSM_JB):
            ub = u_ref[:, j * LANES:(j + 1) * LANES].astype(_MXU)
            bur = jnp.dot(ub, bre_ref[j], preferred_element_type=F32)
            bui = jnp.dot(ub, bim_ref[j], preferred_element_type=F32)
            cfr, cfi = cfr_ref[:, _scan_cols(j)], cfi_ref[:, _scan_cols(j)]
            sre_ref[:, _scan_cols(j)] = cfr * bur - cfi * bui
            sim_ref[:, _scan_cols(j)] = cfr * bui + cfi * bur

        for j in range(SSM_JB):
            cols = _scan_cols(j)
            ar, ai = _bcast8(are_ref[:, cols]), _bcast8(aim_ref[:, cols])

            def step1(r, s, cols=cols, ar=ar, ai=ai):
                sr, si = s
                rows = _rows8(r)
                return (ar * sr - ai * si + sre_ref[rows, cols], ar * si + ai * sr + sim_ref[rows, cols])

            zero = jnp.zeros((SUBLANES, SSM_SB), F32)
            er, ei = lax.fori_loop(0, L, step1, (zero, zero), unroll=4)
            end_re[:, cols] = er
            end_im[:, cols] = ei

        alr, ali = alr_ref[...], ali_ref[...]
        cr, ci = car_re[...], car_im[...]
        ire_ref[0:1, :] = cr
        iim_ref[0:1, :] = ci
        for i in range(1, SUBLANES):
            er, ei = end_re[i - 1:i, :], end_im[i - 1:i, :]
            cr, ci = alr * cr - ali * ci + er, alr * ci + ali * cr + ei
            ire_ref[i:i + 1, :] = cr
            iim_ref[i:i + 1, :] = ci

        for j in range(SSM_JB):
            cols = _scan_cols(j)
            ar, ai = _bcast8(are_ref[:, cols]), _bcast8(aim_ref[:, cols])

            def step2(r, s, cols=cols, ar=ar, ai=ai):
                sr, si = s
                rows = _rows8(r)
                nr = ar * sr - ai * si + sre_ref[rows, cols]
                ni = ar * si + ai * sr + sim_ref[rows, cols]
                sre_ref[rows, cols] = nr
                sim_ref[rows, cols] = ni
                return nr, ni

            lax.fori_loop(0, L, step2, (ire_ref[:, cols], iim_ref[:, cols]), unroll=4)

        car_re[...] = sre_ref[T - 1:T, :]
        car_im[...] = sim_ref[T - 1:T, :]

        for j in range(SSM_JB):
            cols = _scan_cols(j)
            ch = slice(j * LANES, (j + 1) * LANES)
            y = (jnp.dot(sre_ref[:, cols].astype(_MXU), cre_ref[j], preferred_element_type=F32)
                 - jnp.dot(sim_ref[:, cols].astype(_MXU), cim_ref[j], preferred_element_type=F32))
            y_ref[:, ch] = y + d_ref[:, ch] * u_ref[:, ch]

    tok = pl.BlockSpec((T, SSM_W), lambda c: (c, 0))
    st = pl.BlockSpec((T, N_STATES), lambda c: (c, 0))
    ini = pl.BlockSpec((None, SUBLANES, N_STATES), lambda c: (c, 0, 0))
    bsp = pl.BlockSpec((SSM_JB, LANES, SSM_SB), lambda c: (0, 0, 0))
    csp = pl.BlockSpec((SSM_JB, SSM_SB, LANES), lambda c: (0, 0, 0))
    row_w = pl.BlockSpec((1, SSM_W), lambda c: (0, 0))
    row_s = pl.BlockSpec((1, N_STATES), lambda c: (0, 0))
    return pl.pallas_call(
        body, name="ssm_fwd", grid=(nc,),
        in_specs=[tok, bsp, bsp, csp, csp, row_w] + [row_s] * 6,
        out_specs=[tok, st, st, ini, ini],
        out_shape=[jax.ShapeDtypeStruct((seq, SSM_W), F32),
                   jax.ShapeDtypeStruct((seq, N_STATES), F32), jax.ShapeDtypeStruct((seq, N_STATES), F32),
                   jax.ShapeDtypeStruct((nc, SUBLANES, N_STATES), F32),
                   jax.ShapeDtypeStruct((nc, SUBLANES, N_STATES), F32)],
        scratch_shapes=[pltpu.VMEM((1, N_STATES), F32), pltpu.VMEM((1, N_STATES), F32),
                        pltpu.VMEM((SUBLANES, N_STATES), F32), pltpu.VMEM((SUBLANES, N_STATES), F32)],
        compiler_params=_params(("arbitrary",)),
    )(u, b_re, b_im, c_re, c_im, d_skip, *coef)


def _ssm_bwd(dy, u, s_re, s_im, i_re, i_im, b_re, b_im, c_re, c_im, d_skip, coef):
    seq = u.shape[0]
    nc = seq // SSM_T
    T, L = SSM_T, SSM_L

    def body(dy_ref, u_ref, sre_ref, sim_ref, ire_ref, iim_ref, bre_ref, bim_ref, cre_ref, cim_ref, d_ref,
             are_ref, aim_ref, cfr_ref, cfi_ref, alr_ref, ali_ref,
             du_ref, dbre_out, dbim_out, dcre_out, dcim_out, dd_ref, dar_ref, dai_ref, dcfr_ref, dcfi_ref,
             lre, lim, car_re, car_im, end_re, end_im, ini_re, ini_im, dbre_ref, dbim_ref, dcre_ref, dcim_ref):
        step = pl.program_id(0)

        @pl.when(step == 0)
        def _():
            car_re[...] = jnp.zeros_like(car_re)
            car_im[...] = jnp.zeros_like(car_im)
            for ref in (dbre_ref, dbim_ref, dcre_ref, dcim_ref, dd_ref, dar_ref, dai_ref, dcfr_ref, dcfi_ref):
                ref[...] = jnp.zeros_like(ref)

        for j in range(SSM_JB):
            dyb = dy_ref[:, j * LANES:(j + 1) * LANES].astype(_MXU)
            lre[:, _scan_cols(j)] = lax.dot_general(dyb, cre_ref[j], _NT, preferred_element_type=F32)
            lim[:, _scan_cols(j)] = -lax.dot_general(dyb, cim_ref[j], _NT, preferred_element_type=F32)

        for j in range(SSM_JB):
            cols = _scan_cols(j)
            ar, ai = _bcast8(are_ref[:, cols]), _bcast8(aim_ref[:, cols])

            def step1(t, s, cols=cols, ar=ar, ai=ai):
                sr, si = s
                rows = _rows8(L - 1 - t)
                return (ar * sr + ai * si + lre[rows, cols], ar * si - ai * sr + lim[rows, cols])

            zero = jnp.zeros((SUBLANES, SSM_SB), F32)
            er, ei = lax.fori_loop(0, L, step1, (zero, zero), unroll=4)
            end_re[:, cols] = er
            end_im[:, cols] = ei

        alr, ali = alr_ref[...], ali_ref[...]
        cr, ci = car_re[...], car_im[...]
        ini_re[SUBLANES - 1:SUBLANES, :] = cr
        ini_im[SUBLANES - 1:SUBLANES, :] = ci
        for i in range(SUBLANES - 2, -1, -1):
            er, ei = end_re[i + 1:i + 2, :], end_im[i + 1:i + 2, :]
            cr, ci = alr * cr + ali * ci + er, alr * ci - ali * cr + ei
            ini_re[i:i + 1, :] = cr
            ini_im[i:i + 1, :] = ci

        for j in range(SSM_JB):
            cols = _scan_cols(j)
            ar, ai = _bcast8(are_ref[:, cols]), _bcast8(aim_ref[:, cols])

            def step2(t, s, cols=cols, ar=ar, ai=ai):
                sr, si = s
                rows = _rows8(L - 1 - t)
                nr = ar * sr + ai * si + lre[rows, cols]
                ni = ar * si - ai * sr + lim[rows, cols]
                lre[rows, cols] = nr
                lim[rows, cols] = ni
                return nr, ni

            lax.fori_loop(0, L, step2, (ini_re[:, cols], ini_im[:, cols]), unroll=4)

        car_re[...] = lre[0:1, :]
        car_im[...] = lim[0:1, :]

        head, tail, body_rows = slice(0, SUBLANES), slice(SUBLANES, T), slice(0, T - SUBLANES)
        for j in range(SSM_JB):
            cols = _scan_cols(j)
            ch = slice(j * LANES, (j + 1) * LANES)
            lr, li = lre[:, cols], lim[:, cols]
            dar_ref[:, cols] += (_colsum(lre[tail, cols] * sre_ref[body_rows, cols] + lim[tail, cols] * sim_ref[body_rows, cols])
                                 + _colsum(lre[head, cols] * ire_ref[:, cols] + lim[head, cols] * iim_ref[:, cols]))
            dai_ref[:, cols] += (_colsum(lim[tail, cols] * sre_ref[body_rows, cols] - lre[tail, cols] * sim_ref[body_rows, cols])
                                 + _colsum(lim[head, cols] * ire_ref[:, cols] - lre[head, cols] * iim_ref[:, cols]))
            uf = u_ref[:, ch]
            ub = uf.astype(_MXU)
            bur = jnp.dot(ub, bre_ref[j], preferred_element_type=F32)
            bui = jnp.dot(ub, bim_ref[j], preferred_element_type=F32)
            dcfr_ref[:, cols] += _colsum(lr * bur + li * bui)
            dcfi_ref[:, cols] += _colsum(li * bur - lr * bui)
            cfr, cfi = cfr_ref[:, cols], cfi_ref[:, cols]
            dbur = (cfr * lr + cfi * li).astype(_MXU)
            dbui = (cfr * li - cfi * lr).astype(_MXU)
            dyf = dy_ref[:, ch]
            dyb = dyf.astype(_MXU)
            du_ref[:, ch] = (lax.dot_general(dbur, bre_ref[j], _NT, preferred_element_type=F32)
                             + lax.dot_general(dbui, bim_ref[j], _NT, preferred_element_type=F32)
                             + d_ref[:, ch] * dyf)
            dbre_ref[j] += lax.dot_general(ub, dbur, _TN, preferred_element_type=F32)
            dbim_ref[j] += lax.dot_general(ub, dbui, _TN, preferred_element_type=F32)
            dcre_ref[j] += lax.dot_general(sre_ref[:, cols].astype(_MXU), dyb, _TN, preferred_element_type=F32)
            dcim_ref[j] -= lax.dot_general(sim_ref[:, cols].astype(_MXU), dyb, _TN, preferred_element_type=F32)
            dd_ref[:, ch] += _colsum(dyf * uf)

        @pl.when(step == nc - 1)
        def _():
            for acc, out in ((dbre_ref, dbre_out), (dbim_ref, dbim_out), (dcre_ref, dcre_out), (dcim_ref, dcim_out)):
                pltpu.sync_copy(acc, out)

    tok = pl.BlockSpec((T, SSM_W), lambda c: (nc - 1 - c, 0))
    st = pl.BlockSpec((T, N_STATES), lambda c: (nc - 1 - c, 0))
    ini = pl.BlockSpec((None, SUBLANES, N_STATES), lambda c: (nc - 1 - c, 0, 0))
    bsp = pl.BlockSpec((SSM_JB, LANES, SSM_SB), lambda c: (0, 0, 0))
    csp = pl.BlockSpec((SSM_JB, SSM_SB, LANES), lambda c: (0, 0, 0))
    row_w = pl.BlockSpec((1, SSM_W), lambda c: (0, 0))
    row_s = pl.BlockSpec((1, N_STATES), lambda c: (0, 0))
    big = pltpu.VMEM((T, N_STATES), F32)
    one = pltpu.VMEM((1, N_STATES), F32)
    eight = pltpu.VMEM((SUBLANES, N_STATES), F32)
    return pl.pallas_call(
        body, name="ssm_bwd", grid=(nc,),
        in_specs=[tok, tok, st, st, ini, ini, bsp, bsp, csp, csp, row_w] + [row_s] * 6,
        out_specs=[tok, _ANY, _ANY, _ANY, _ANY, row_w, row_s, row_s, row_s, row_s],
        out_shape=[jax.ShapeDtypeStruct((seq, SSM_W), F32),
                   jax.ShapeDtypeStruct((SSM_JB, LANES, SSM_SB), F32), jax.ShapeDtypeStruct((SSM_JB, LANES, SSM_SB), F32),
                   jax.ShapeDtypeStruct((SSM_JB, SSM_SB, LANES), F32), jax.ShapeDtypeStruct((SSM_JB, SSM_SB, LANES), F32),
                   jax.ShapeDtypeStruct((1, SSM_W), F32)] + [jax.ShapeDtypeStruct((1, N_STATES), F32)] * 4,
        scratch_shapes=[big, big, one, one, eight, eight, eight, eight,
                        pltpu.VMEM((SSM_JB, LANES, SSM_SB), F32), pltpu.VMEM((SSM_JB, LANES, SSM_SB), F32),
                        pltpu.VMEM((SSM_JB, SSM_SB, LANES), F32), pltpu.VMEM((SSM_JB, SSM_SB, LANES), F32)],
        compiler_params=_params(("arbitrary",)),
    )(dy, u, s_re, s_im, i_re, i_im, b_re, b_im, c_re, c_im, d_skip, *coef)


def _block_diag_b(b):
    t = b.reshape(SSM_JB, 8, STATE, GROUP).transpose(0, 1, 3, 2)
    eye = jnp.eye(8, dtype=b.dtype)
    return (t[:, :, :, None, :] * eye[None, :, None, :, None]).reshape(SSM_JB, LANES, SSM_SB)


def _block_diag_c(c):
    t = c.reshape(SSM_JB, 8, GROUP, STATE).transpose(0, 1, 3, 2)
    eye = jnp.eye(8, dtype=c.dtype)
    return (t[:, :, :, None, :] * eye[None, :, None, :, None]).reshape(SSM_JB, SSM_SB, LANES)


def _diag_of_b(blk):
    t = blk.reshape(SSM_JB, 8, GROUP, 8, STATE)
    d = jnp.sum(t * jnp.eye(8, dtype=blk.dtype)[None, :, None, :, None], axis=3)
    return d.transpose(0, 1, 3, 2).reshape(N_GROUPS, STATE, GROUP)


def _diag_of_c(blk):
    t = blk.reshape(SSM_JB, 8, STATE, 8, GROUP)
    d = jnp.sum(t * jnp.eye(8, dtype=blk.dtype)[None, :, None, :, None], axis=3)
    return d.transpose(0, 1, 3, 2).reshape(N_GROUPS, GROUP, STATE)


def _to_scan_order(v):
    seq, w = v.shape
    return v.reshape(seq // SSM_T, SUBLANES, SSM_L, w).transpose(0, 2, 1, 3).reshape(seq, w)


def _from_scan_order(v):
    seq, w = v.shape
    return v.reshape(seq // SSM_T, SSM_L, SUBLANES, w).transpose(0, 2, 1, 3).reshape(seq, w)


def _adamw_math(w, g, m, v):
    nm = ADAM_B1 * m + (1.0 - ADAM_B1) * g
    nv = ADAM_B2 * v + (1.0 - ADAM_B2) * jnp.square(g)
    m_hat = nm / (1.0 - ADAM_B1 ** ADAM_STEP)
    v_hat = nv / (1.0 - ADAM_B2 ** ADAM_STEP)
    return -ADAM_LR * (m_hat / (jnp.sqrt(v_hat) + ADAM_EPS) + ADAM_WD * w), nm, nv


def _adamw(w, g, m, v, *, name, tm, deps=()):
    rows, cols = w.shape
    nd = len(deps)

    def body(w_ref, g_ref, m_ref, v_ref, *rest):
        d_ref, nm_ref, nv_ref = rest[nd:]
        d_ref[...], nm_ref[...], nv_ref[...] = _adamw_math(w_ref[...], g_ref[...], m_ref[...], v_ref[...])

    spec = pl.BlockSpec((tm, cols), lambda i: (i, 0))
    shp = jax.ShapeDtypeStruct((rows, cols), F32)
    return pl.pallas_call(body, name=name, grid=(rows // tm,), in_specs=[spec] * 4 + [_ANY] * nd,
                          out_specs=[spec] * 3, out_shape=[shp] * 3,
                          compiler_params=_params(("arbitrary",)))(w, g, m, v, *deps)


def _place():
    x, y, c = lax.axis_index("x"), lax.axis_index("y"), lax.axis_index("c")
    chips = [(1 - x, y), (x, 1 - y), (1 - x, 1 - y)]
    return x, y, c, chips


def _remote(src, dst, send_sem, recv_sem, dev):
    return pltpu.make_async_remote_copy(src_ref=src, dst_ref=dst, send_sem=send_sem, recv_sem=recv_sem,
                                        device_id=dev, device_id_type=MESH)


def _place_shard(w, mine_arr, *, name, tm=256):
    rows, cols = w.shape

    def body(m_ref, w_ref, o_ref):
        o_ref[...] = w_ref[...].astype(o_ref.dtype)

    return pl.pallas_call(
        body, name=name,
        grid_spec=pltpu.PrefetchScalarGridSpec(
            num_scalar_prefetch=1, grid=(rows // tm,),
            in_specs=[pl.BlockSpec((tm, cols), lambda i, m: (i, 0))],
            out_specs=pl.BlockSpec((None, tm, cols), lambda i, m: (m[0], i, 0))),
        out_shape=jax.ShapeDtypeStruct((N_CHIPS, rows, cols), _WIRE),
        compiler_params=_params(("arbitrary",)),
    )(mine_arr, w)


_HBM = pl.BlockSpec(memory_space=pltpu.HBM)
_SEM = pl.BlockSpec(memory_space=pltpu.SEMAPHORE)
_EFFECT = pltpu.SideEffectType.DATAFLOW_SIDE_EFFECTING


def _copies_start(name, bufs, plan, count, after=()):
    nb, na = len(bufs), len(after)

    def body(*refs):
        send_sems, recv_sems, token = refs[nb + na], refs[nb + na + 1], refs[-1]
        copies = plan(refs[:nb])
        assert len(copies) == count
        for i, (src, dst, dev, _) in enumerate(copies):
            _remote(src, dst, send_sems.at[i], recv_sems.at[i], dev).start()
        token[...] = jnp.zeros_like(token)

    res = pl.pallas_call(
        body, name=name, in_specs=[_HBM] * nb + [_ANY] * na,
        out_specs=(_SEM, _SEM, *[_HBM] * nb, pl.BlockSpec(memory_space=pltpu.VMEM)),
        out_shape=(pltpu.SemaphoreType.DMA((count,)), pltpu.SemaphoreType.DMA((count,)),
                   *[pltpu.HBM(b.shape, b.dtype) for b in bufs], jax.ShapeDtypeStruct((SUBLANES, LANES), F32)),
        input_output_aliases={i: 2 + i for i in range(nb)},
        compiler_params=pltpu.CompilerParams(has_side_effects=_EFFECT),
    )(*[pltpu.with_memory_space_constraint(b, pltpu.HBM) for b in bufs], *after)
    return (res[0], res[1]), list(res[2:2 + nb]), res[-1]


def _copies_wait(name, bufs, sems, plan, after=(), which=None):
    nb, na = len(bufs), len(after)

    def body(*refs):
        send_sems, recv_sems = refs[nb], refs[nb + 1]
        for i, (src, _, dev, land) in enumerate(plan(refs[:nb])):
            if which is not None and i not in which:
                continue
            cp = _remote(src, land, send_sems.at[i], recv_sems.at[i], dev)
            cp.wait_send()
            cp.wait_recv()

    res = pl.pallas_call(
        body, name=name, in_specs=[_HBM] * nb + [_SEM, _SEM] + [_ANY] * na, out_specs=[_HBM] * nb,
        out_shape=[pltpu.HBM(b.shape, b.dtype) for b in bufs],
        input_output_aliases={i: i for i in range(nb)},
        compiler_params=pltpu.CompilerParams(has_side_effects=_EFFECT),
    )(*bufs, *sems, *after)
    return list(res)


def _plan_gather_ici(fulls, which=(0, 1, 2)):
    x, y, c, chips = _place()
    copies = []
    for f in fulls:
        half = pl.ds(c * (f.shape[1] // 2), f.shape[1] // 2)
        own = f.at[2 * x + y, half]
        for chip in [chips[k] for k in which]:
            copies.append((own, own, (*chip, c), f.at[2 * chip[0] + chip[1], half]))
    return copies


def _plan_gather_d2d(fulls, which=(0, 1, 2)):
    x, y, c, chips = _place()
    copies = []
    for f in fulls:
        r2 = f.shape[1] // 2
        for chip in [chips[k] for k in which]:
            blk = 2 * chip[0] + chip[1]
            landed = f.at[blk, pl.ds(c * r2, r2)]
            copies.append((landed, landed, (x, y, 1 - c), f.at[blk, pl.ds((1 - c) * r2, r2)]))
    return copies


def _plan_swap_halves(refs):
    x, y, c, _ = _place()
    n = len(refs) // 2
    copies = []
    for g, land in zip(refs[:n], refs[n:]):
        r2 = g.shape[1] // 2
        copies.append((g.at[:, pl.ds((1 - c) * r2, r2), :], land, (x, y, 1 - c), land))
    return copies


def _plan_scatter_chips(refs):
    x, y, c, chips = _place()
    n = len(refs) // 2
    copies = []
    for h, land in zip(refs[:n], refs[n:]):
        for k, chip in enumerate(chips):
            copies.append((h.at[2 * chip[0] + chip[1]], land.at[k], (*chip, c), land.at[k]))
    return copies


def _plan_join_halves(totals):
    x, y, c, _ = _place()
    copies = []
    for t in totals:
        r2 = t.shape[0] // 2
        mine = t.at[pl.ds(c * r2, r2)]
        copies.append((mine, mine, (x, y, 1 - c), t.at[pl.ds((1 - c) * r2, r2)]))
    return copies


def _add_sibling_half(g, got, c_arr, *, name, tm):
    _, rows, cols = g.shape
    r2 = rows // 2
    nb = r2 // tm

    def body(c_ref, g_ref, r_ref, o_ref):
        o_ref[...] = (g_ref[...].astype(F32) + r_ref[...].astype(F32)).astype(o_ref.dtype)

    return pl.pallas_call(
        body, name=name,
        grid_spec=pltpu.PrefetchScalarGridSpec(
            num_scalar_prefetch=1, grid=(N_CHIPS, nb),
            in_specs=[pl.BlockSpec((None, tm, cols), lambda b, i, c: (b, c[0] * nb + i, 0)),
                      pl.BlockSpec((None, tm, cols), lambda b, i, c: (b, i, 0))],
            out_specs=pl.BlockSpec((None, tm, cols), lambda b, i, c: (b, i, 0))),
        out_shape=jax.ShapeDtypeStruct((N_CHIPS, r2, cols), _WIRE),
        compiler_params=_params(("arbitrary", "arbitrary")),
    )(c_arr, g, got)


def _add_chips(h, got, place_arr, *, name, tm):
    _, r2, cols = h.shape
    nb = r2 // tm

    def body(p_ref, h_ref, r_ref, o_ref):
        o_ref[...] = ((h_ref[...].astype(F32) + r_ref[0].astype(F32)) + r_ref[1].astype(F32)) + r_ref[2].astype(F32)

    return pl.pallas_call(
        body, name=name,
        grid_spec=pltpu.PrefetchScalarGridSpec(
            num_scalar_prefetch=1, grid=(nb,),
            in_specs=[pl.BlockSpec((None, tm, cols), lambda i, p: (p[0], i, 0)),
                      pl.BlockSpec((3, tm, cols), lambda i, p: (0, i, 0))],
            out_specs=pl.BlockSpec((tm, cols), lambda i, p: (p[1] * nb + i, 0))),
        out_shape=jax.ShapeDtypeStruct((2 * r2, cols), F32),
        compiler_params=_params(("arbitrary",)),
    )(place_arr, h, got)


class _ReduceScatter:
    def __init__(self, tag, names, grads):
        self.tag, self.names, self.n = tag, names, len(names)
        core = lax.axis_index("c").astype(jnp.int32)
        chip = (2 * lax.axis_index("x") + lax.axis_index("y")).astype(jnp.int32)
        self.c_arr, self.place_arr = core.reshape(1), jnp.stack([chip, core])
        self.bufs = list(grads)

    def _start(self, step, bufs, plan, count, after):
        self.plan = plan
        self.step = f"grad_{step}_{self.tag}"
        self.sems, self.bufs, token = _copies_start(self.step + "_start", bufs, plan, count, after)
        return [token]

    def _wait(self, after):
        self.bufs = _copies_wait(self.step + "_wait", self.bufs, self.sems, self.plan, after)
        return self.bufs

    def start_swap(self, after=()):
        lands = [lax.empty((N_CHIPS, g.shape[1] // 2, g.shape[2]), g.dtype) for g in self.bufs]
        return self._start("swap", self.bufs + lands, _plan_swap_halves, self.n, after)

    def start_scatter(self, after):
        bufs = self._wait(after)
        pair = [_add_sibling_half(g, r, self.c_arr, name=f"grad_add_sibling_{nm}", tm=min(256, g.shape[1] // 2))
                for nm, g, r in zip(self.names, bufs[:self.n], bufs[self.n:])]
        lands = [lax.empty((3,) + h.shape[1:], h.dtype) for h in pair]
        return self._start("scatter", pair + lands, _plan_scatter_chips, 3 * self.n, ())

    def start_join(self, after):
        bufs = self._wait(after)
        total = [_add_chips(h, r, self.place_arr, name=f"grad_add_chips_{nm}", tm=min(256, h.shape[1]))
                 for nm, h, r in zip(self.names, bufs[:self.n], bufs[self.n:])]
        return self._start("join", total, _plan_join_halves, self.n, ())

    def finish(self, after):
        return dict(zip(self.names, self._wait(after)))


def _all_gather_small(v):
    m_per, n = v.shape

    def body(x_ref, out_ref, send_sems, recv_sems, local_sem):
        x, y, c, chips = _place()
        me, sibling = (x, y, c), (x, y, 1 - c)

        def rows(px, py, pc):
            return out_ref.at[4 * px + 2 * py + pc]

        def copy(k, block, to, src=None):
            return _remote(rows(*block) if src is None else src, rows(*block), send_sems.at[k], recv_sems.at[k], to)

        mine = pltpu.make_async_copy(x_ref, rows(*me), local_sem)
        mine.start()
        first = [copy(0, me, sibling, src=x_ref)]
        first += [copy(1 + j, me, (*chip, c), src=x_ref) for j, chip in enumerate(chips)]
        for cp in first:
            cp.start()
        passed = [copy(4 + j, (*chip, c), sibling) for j, chip in enumerate(chips)]
        for j, chip in enumerate(chips):
            copy(1 + j, (*chip, c), me).wait_recv()
            passed[j].start()
        copy(0, sibling, me).wait_recv()
        for j, chip in enumerate(chips):
            copy(4 + j, (*chip, 1 - c), me).wait_recv()
        for cp in first + passed:
            cp.wait_send()
        mine.wait()

    return pl.pallas_call(
        body, name="gather_small_grads",
        out_shape=jax.ShapeDtypeStruct((8, m_per, n), v.dtype),
        in_specs=[pl.BlockSpec(memory_space=pltpu.VMEM)], out_specs=pl.BlockSpec(memory_space=pltpu.VMEM),
        scratch_shapes=[pltpu.SemaphoreType.DMA((7,)), pltpu.SemaphoreType.DMA((7,)), pltpu.SemaphoreType.DMA],
        compiler_params=pltpu.CompilerParams(vmem_limit_bytes=VMEM_LIMIT),
    )(v)


def _sum8(v, *, name):
    _, m, n = v.shape

    def body(v_ref, o_ref):
        acc = v_ref[0]
        for d in range(1, 8):
            acc = acc + v_ref[d]
        o_ref[...] = acc

    return pl.pallas_call(body, name=name, out_shape=jax.ShapeDtypeStruct((m, n), F32),
                          compiler_params=pltpu.CompilerParams(vmem_limit_bytes=VMEM_LIMIT))(v)


def _local_step(x, target, norm_w, q_norm_w, k_norm_w, sinks, a_re, a_im, log_dt, b_re, b_im, c_re, c_im, d_skip,
                b_glu, io):
    seq = x.shape[0]
    qw2 = jnp.tile(q_norm_w.reshape(1, HEAD_DIM), (1, HEADS_PER_TILE))
    kw2 = jnp.tile(k_norm_w.reshape(1, HEAD_DIM), (1, HEADS_PER_TILE))
    nw, bg = norm_w.reshape(1, D_MODEL), b_glu.reshape(1, D_MODEL)
    dsk = d_skip.reshape(1, SSM_W)

    h, rstd = _rms_fwd(x, nw, deps=io.begin())
    proj, w_in4 = io.projection(h)
    attn, lse = _attn2_fwd(proj, qw2, kw2, sinks, deps=io.after_proj(proj))

    def gate_a(at, ag):
        return (at * (ag * _sigmoid(ag)),)

    (ya_in,) = _ew(gate_a, [(attn, "mat", 0), (proj, "mat", OFF_AGATE)], [(ATTN_W, _MXU)], rows=seq, ncol=2,
                   name="ew_attn_gate")
    w_ap4 = io.weight("w_attn_proj", ya_in)
    w_glu4, w_sp4, w_out = io.weight("w_glu", ya_in), io.weight("w_ssm_proj", ya_in), io.weight("w_out", ya_in)
    y_a = _mm(ya_in, w_ap4, mode="nn", name="mm_attn_proj", tm=2048, tn=512, tk=ATTN_W, b_blocked=True)

    flat_a = (a_re.reshape(1, N_STATES), a_im.reshape(1, N_STATES), jnp.repeat(log_dt, STATE).reshape(1, N_STATES))
    coef = _ssm_params_fwd(*flat_a)
    bre_blk, bim_blk = _block_diag_b(b_re).astype(_MXU), _block_diag_b(b_im).astype(_MXU)
    cre_blk, cim_blk = _block_diag_c(c_re).astype(_MXU), _block_diag_c(c_im).astype(_MXU)
    u_scan = _to_scan_order(proj[:, OFF_U * CW:OFF_U * CW + SSM_W])
    y_scan, s_re, s_im, i_re, i_im = _ssm_fwd(u_scan, bre_blk, bim_blk, cre_blk, cim_blk, dsk, coef)
    y_ssm = _from_scan_order(y_scan)

    (yg,) = _ew(lambda yv: (jax.nn.gelu(yv),), [(y_ssm, "mat", 0)], [(SSM_W, _MXU)], rows=seq, ncol=2, name="ew_gelu")
    glu = _mm(yg, w_glu4, mode="nn", name="mm_glu", tm=2048, tn=512, tk=SSM_W, b_blocked=True)

    def gate_s(ga, gb, ba, bb, z):
        return ((ga + ba) * _sigmoid(gb + bb) * (z * _sigmoid(z)),)

    (ys_in,) = _ew(gate_s, [(glu, "mat", 0), (glu, "mat", 2), (bg, "row", 0), (bg, "row", 2), (proj, "mat", OFF_Z)],
                   [(SSM_W, _MXU)], rows=seq, ncol=2, name="ew_ssm_gate")
    y_s = _mm(ys_in, w_sp4, mode="nn", name="mm_ssm_proj", tm=2048, tn=512, tk=SSM_W, b_blocked=True)

    def merge(ga, gs, ya, ys):
        return (_sigmoid(ga) * ya + _sigmoid(gs) * ys,)

    (merged,) = _ew(merge, [(proj, "mat", OFF_GA), (proj, "mat", OFF_GS), (y_a, "mat", 0), (y_s, "mat", 0)],
                    [(D_MODEL, _MXU)], rows=seq, ncol=4, name="ew_merge")
    mo = _mm(merged, w_out, mode="nn", name="mm_out", tm=512, tn=D_MODEL, tk=D_MODEL)

    def loss_head(xv, mv, tv):
        err = (xv + mv) - tv
        dout = err * (1.0 / D_MODEL)
        return dout, dout, _colsum(err * err)

    dout, dout_b, sq = _ew(loss_head, [(x, "mat", 0), (mo, "mat", 0), (target, "mat", 0)],
                           [(D_MODEL, F32), (D_MODEL, _MXU)], rows=seq, ncol=4, n_acc=1, name="ew_loss")
    loss = 0.5 * jnp.sum(sq) / D_MODEL

    d_merged = _mm(dout_b, w_out, mode="nt", name="mm_d_merged", tm=512, tn=D_MODEL, tk=D_MODEL)
    g_w_out = _mm(merged, dout_b, mode="tn", name="mm_g_w_out", tm=1024, tn=D_MODEL, tk=1024, out_dtype=_WIRE)

    def merge_bwd(dm, ga, gs, ya, ys):
        sa, ss = _sigmoid(ga), _sigmoid(gs)
        return sa * dm, ss * dm, dm * ya * sa * (1.0 - sa), dm * ys * ss * (1.0 - ss)

    d_ya, d_ys, d_ga, d_gs = _ew(
        merge_bwd, [(d_merged, "mat", 0), (proj, "mat", OFF_GA), (proj, "mat", OFF_GS), (y_a, "mat", 0), (y_s, "mat", 0)],
        [(D_MODEL, _MXU)] * 4, rows=seq, ncol=4, name="ew_merge_bwd")

    d_ya_in = _mm(d_ya, w_ap4, mode="nt", name="mm_d_attn_gate", tm=2048, tn=ATTN_W, tk=512, b_blocked=True)
    g_w_ap = _mm(ya_in, d_ya, mode="tn", name="mm_g_w_attn_proj", tm=ATTN_W, tn=512, tk=2048, out_dtype=_WIRE,
                 out_blocked=True)

    d_ys_in = _mm(d_ys, w_sp4, mode="nt", name="mm_d_ssm_gate", tm=2048, tn=SSM_W, tk=512, b_blocked=True)
    g_w_sp = _mm(ys_in, d_ys, mode="tn", name="mm_g_w_ssm_proj", tm=SSM_W, tn=512, tk=2048, out_dtype=_WIRE,
                 out_blocked=True)

    def gate_s_bwd(dv, ga, gb, ba, bb, z):
        a, sb = ga + ba, _sigmoid(gb + bb)
        f, df = _silu_and_grad(z)
        dga = dv * sb * f
        dgb = dv * a * f * sb * (1.0 - sb)
        return dga, dgb, dv * a * sb * df, _colsum(dga), _colsum(dgb)

    d_glu_a, d_glu_b, d_z, g_bga, g_bgb = _ew(
        gate_s_bwd, [(d_ys_in, "mat", 0), (glu, "mat", 0), (glu, "mat", 2), (bg, "row", 0), (bg, "row", 2),
                     (proj, "mat", OFF_Z)],
        [(SSM_W, _MXU)] * 3, rows=seq, ncol=2, n_acc=2, name="ew_ssm_gate_bwd")
    d_glu = jnp.concatenate([d_glu_a, d_glu_b], axis=1)
    d_yg = _mm(d_glu, w_glu4, mode="nt", name="mm_d_gelu", tm=2048, tn=SSM_W, tk=512, b_blocked=True)
    g_w_glu = _mm(yg, d_glu, mode="tn", name="mm_g_w_glu", tm=SSM_W, tn=512, tk=2048, out_dtype=_WIRE, out_blocked=True)
    dep = io.later_grads(dict(w_attn_proj=g_w_ap, w_glu=g_w_glu, w_ssm_proj=g_w_sp,
                              w_out=g_w_out.reshape(N_CHIPS, D_MODEL // N_CHIPS, D_MODEL)))

    def gate_a_bwd(dv, at, ag):
        f, df = _silu_and_grad(ag)
        return dv * f, dv * at * df

    d_attn, d_agate = _ew(gate_a_bwd, [(d_ya_in, "mat", 0), (attn, "mat", 0), (proj, "mat", OFF_AGATE)],
                          [(ATTN_W, F32), (ATTN_W, _MXU)], rows=seq, ncol=2, name="ew_attn_gate_bwd", deps=dep)

    def gelu_bwd(dv, yv):
        return (jax.vjp(jax.nn.gelu, yv)[1](dv)[0],)

    (d_yssm,) = _ew(gelu_bwd, [(d_yg, "mat", 0), (y_ssm, "mat", 0)], [(SSM_W, F32)], rows=seq, ncol=2, name="ew_gelu_bwd",
                    deps=dep)
    dep = io.before_attention_backward([d_attn, d_yssm])
    d_qkv, g_qw2, g_kw2, g_sk = _attn2_bwd(proj, qw2, kw2, sinks, lse, d_attn, deps=dep)
    (du_scan, g_bre, g_bim, g_cre, g_cim, g_dsk, g_abr, g_abi, g_cfr, g_cfi) = _ssm_bwd(
        _to_scan_order(d_yssm), u_scan, s_re, s_im, i_re, i_im, bre_blk, bim_blk, cre_blk, cim_blk, dsk, coef)
    g_are, g_aim, g_ldt = _ssm_params_bwd(*flat_a, g_abr, g_abi, g_cfr, g_cfi)
    g_are, g_aim = g_are.reshape(N_GROUPS, STATE), g_aim.reshape(N_GROUPS, STATE)
    g_ldt = g_ldt.reshape(N_GROUPS, STATE).sum(axis=1)
    d_u = _from_scan_order(du_scan)

    d_proj = jnp.concatenate([d_qkv, d_agate, d_u.astype(_MXU), d_z, d_ga, d_gs], axis=1)
    dep = io.before_input_projection_grad([d_proj]) + io.small_grads(dict(
        q_norm_w=g_qw2[0, :HEAD_DIM] + g_qw2[0, HEAD_DIM:], k_norm_w=g_kw2[0, :HEAD_DIM] + g_kw2[0, HEAD_DIM:],
        sinks=g_sk.reshape(N_Q_HEADS), A_re=g_are, A_im=g_aim, log_dt=g_ldt,
        B_re=_diag_of_b(g_bre), B_im=_diag_of_b(g_bim), C_re=_diag_of_c(g_cre), C_im=_diag_of_c(g_cim),
        D_skip=g_dsk.reshape(N_GROUPS, GROUP), b_glu=jnp.concatenate([g_bga, g_bgb], axis=1).reshape(D_MODEL)))
    g_w_in = _mm(h, d_proj, mode="tn", name="mm_g_w_in", tm=1024, tn=IN_W // 4, tk=1024, out_dtype=_WIRE,
                 out_blocked=True, deps=dep)
    dep = io.input_projection_grad(g_w_in)
    d_h = _mm(d_proj, w_in4, mode="nt", name="mm_d_h", tm=512, tn=D_MODEL, tk=IN_W // 4, b_blocked=True, deps=dep)
    grad_x, g_nw = _rms_bwd(d_h, x, rstd, nw, dout)
    return loss, grad_x, g_nw.reshape(D_MODEL)


_SMALL = ["norm_w", "q_norm_w", "k_norm_w", "sinks", "A_re", "A_im", "log_dt", "B_re", "B_im", "C_re", "C_im",
          "D_skip", "b_glu"]
_BIG = ["w_in", "w_attn_proj", "w_glu", "w_ssm_proj", "w_out"]
_LATER = _BIG[1:]
_RELATIONS = ("flip_x", "flip_y", "flip_xy")
_ORDER = ["norm_w", "w_in", "q_norm_w", "k_norm_w", "sinks", "w_attn_proj", "A_re", "A_im", "log_dt", "B_re", "B_im",
          "C_re", "C_im", "D_skip", "w_glu", "b_glu", "w_ssm_proj", "w_out"]
_PACK_W = 1024


def _packed_rows(size):
    unit = SUBLANES * _PACK_W
    return -(-size // unit) * SUBLANES


def _pack_small(d, names):
    parts = []
    for n in names:
        flat = d[n].reshape(-1).astype(F32)
        rows = _packed_rows(flat.shape[0])
        parts.append(jnp.pad(flat, (0, rows * _PACK_W - flat.shape[0])).reshape(rows, _PACK_W))
    return jnp.concatenate(parts, axis=0)


def _unpack_small(packed, like, names):
    out, pos = {}, 0
    for n in names:
        rows = _packed_rows(like[n].size)
        out[n] = packed[pos:pos + rows].reshape(-1)[:like[n].size].reshape(like[n].shape)
        pos += rows
    return out


def _place_block(v, index_arr, *, name):
    rows, cols = v.shape

    def body(i_ref, v_ref, o_ref):
        o_ref[...] = v_ref[...]

    return pl.pallas_call(
        body, name=name,
        grid_spec=pltpu.PrefetchScalarGridSpec(
            num_scalar_prefetch=1, grid=(1,),
            in_specs=[pl.BlockSpec((rows, cols), lambda i, d: (0, 0))],
            out_specs=pl.BlockSpec((None, rows, cols), lambda i, d: (d[0], 0, 0))),
        out_shape=jax.ShapeDtypeStruct((8, rows, cols), v.dtype),
        compiler_params=_params(("arbitrary",)),
    )(index_arr, v)


def _plan_all_to_all(refs):
    (land,) = refs
    x, y, c, _ = _place()
    own = land.at[4 * x + 2 * y + c]
    copies = []
    for fx, fy, fc in [(0, 0, 1), (0, 1, 0), (0, 1, 1), (1, 0, 0), (1, 0, 1), (1, 1, 0), (1, 1, 1)]:
        px, py, pc = (1 - x) if fx else x, (1 - y) if fy else y, (1 - c) if fc else c
        copies.append((own, own, (px, py, pc), land.at[4 * px + 2 * py + pc]))
    return copies


def _as2d(a):
    return a.reshape(1, -1) if a.ndim == 1 else a


def _adamw_whole(w, g, m, v, *, name):
    shape = w.shape
    w, g, m, v = _as2d(w), _as2d(g), _as2d(m), _as2d(v)

    def body(w_ref, g_ref, m_ref, v_ref, d_ref, nm_ref, nv_ref):
        d_ref[...], nm_ref[...], nv_ref[...] = _adamw_math(w_ref[...], g_ref[...], m_ref[...], v_ref[...])

    outs = pl.pallas_call(body, name=name, out_shape=[jax.ShapeDtypeStruct(w.shape, F32)] * 3)(w, g, m, v)
    return [o.reshape(shape) for o in outs]


class _Exchanges:
    def __init__(self, w, m, v):
        self.w, self.m, self.v = w, m, v
        self.grads, self.delta, self.new_m, self.new_v = {}, {}, {}, {}

    def _adamw(self, names, deps):
        for n in names:
            self.delta[n], self.new_m[n], self.new_v[n] = _adamw(
                self.w[n], self.grads[n], self.m[n], self.v[n], name=f"adamw_{n}", tm=128, deps=deps)

    def begin(self):
        chip = (2 * lax.axis_index("x") + lax.axis_index("y")).astype(jnp.int32).reshape(1)
        full = {n: _place_shard(self.w[n], chip, name=f"place_{n}") for n in _BIG}
        self.w_in_sems, self.w_in_buf, token = _copies_start("gather_ici_w_in_start", [full["w_in"]], _plan_gather_ici, 3)
        self.rest = _copies_start("gather_ici_rest_start", [full[n] for n in _LATER], _plan_gather_ici,
                                  3 * len(_LATER), after=[token])
        return [self.rest[2]]

    def projection(self, h):
        x, y = lax.axis_index("x"), lax.axis_index("y")
        blks = [jnp.asarray(b, jnp.int32).reshape(1)
                for b in (2 * x + y, 2 * (1 - x) + y, 2 * x + (1 - y), 2 * (1 - x) + (1 - y))]
        def landed(bufs, k, after):
            bufs = _copies_wait(f"gather_ici_w_in_{_RELATIONS[k]}_wait", bufs, self.w_in_sems, _plan_gather_ici, after,
                                which=(k,))
            plan = functools.partial(_plan_gather_d2d, which=(k,))
            sems, bufs, token = _copies_start(f"gather_d2d_w_in_{_RELATIONS[k]}_start", bufs, plan, 1)
            return bufs, (sems, plan, token)

        def handed(bufs, k, pending, after):
            sems, plan, _ = pending
            return _copies_wait(f"gather_d2d_w_in_{_RELATIONS[k]}_wait", bufs, sems, plan, after)

        bufs = self.w_in_buf
        proj = _mm_chip_block(h, bufs[0], blks[0], None, name="mm_proj_own")
        bufs, d2d_x = landed(bufs, 0, [proj])
        bufs, d2d_y = landed(bufs, 1, [d2d_x[2]])
        bufs = handed(bufs, 0, d2d_x, [d2d_y[2]])
        proj = _mm_chip_block(h, bufs[0], blks[1], proj, name="mm_proj_flip_x")
        bufs = handed(bufs, 1, d2d_y, [proj])
        proj = _mm_chip_block(h, bufs[0], blks[2], proj, name="mm_proj_flip_y")
        bufs, d2d_xy = landed(bufs, 2, [proj])
        bufs = handed(bufs, 2, d2d_xy, [d2d_xy[2]])
        proj = _mm_chip_block(h, bufs[0], blks[3], proj, name="mm_proj_flip_xy")
        return proj, bufs[0]

    def weight(self, name, after):
        if self.rest is not None:
            sems, bufs = self.rest
            later = dict(zip(_LATER, _copies_wait("gather_d2d_rest_wait", bufs, sems, _plan_gather_d2d, [after])))
            later["w_out"] = later["w_out"].reshape(D_MODEL, D_MODEL)
            self.later, self.rest = later, None
        return self.later[name]

    def after_proj(self, proj):
        sems, bufs, _ = self.rest
        bufs = _copies_wait("gather_ici_rest_wait", bufs, sems, _plan_gather_ici, [proj])
        sems, bufs, token = _copies_start("gather_d2d_rest_start", bufs, _plan_gather_d2d, 3 * len(_LATER))
        self.rest = (sems, bufs)
        return [token]

    def later_grads(self, grads):
        self.rs_later = _ReduceScatter("later", _LATER, [grads[n] for n in _LATER])
        return self.rs_later.start_swap()

    def before_attention_backward(self, after):
        return self.rs_later.start_scatter(after)

    def before_input_projection_grad(self, after):
        return self.rs_later.start_join(after)

    def input_projection_grad(self, g_w_in):
        self.grads.update(self.rs_later.finish([g_w_in]))
        self.rs_in = _ReduceScatter("w_in", ["w_in"], [g_w_in])
        self._adamw(_LATER, self.rs_in.start_swap())
        return self.rs_in.start_scatter([self.delta[n] for n in _LATER])

    def _adamw_small(self, names):
        for n in names:
            self.delta[n], self.new_m[n], self.new_v[n] = _adamw_whole(
                self.w[n], self.grads[n], self.m[n], self.v[n], name=f"adamw_{n}")

    def small_grads(self, grads):
        me = (4 * lax.axis_index("x") + 2 * lax.axis_index("y") + lax.axis_index("c")).astype(jnp.int32).reshape(1)
        land = _place_block(_pack_small(grads, _SMALL[1:]), me, name="place_small_grads")
        self.small = _copies_start("gather_small_start", [land], _plan_all_to_all, 7)
        return [self.small[2]]

    def finish(self, g_norm_w, after):
        sems, bufs, _ = self.small
        (land,) = _copies_wait("gather_small_wait", bufs, sems, _plan_all_to_all, after)
        self.grads.update(_unpack_small(_sum8(land, name="sum_small_grads"), self.w, _SMALL[1:]))
        self._adamw_small(_SMALL[1:])
        late = _sum8(_all_gather_small(_pack_small(dict(norm_w=g_norm_w), _SMALL[:1])), name="sum_norm_w_grad")
        self.grads.update(_unpack_small(late, self.w, _SMALL[:1]))
        self._adamw_small(_SMALL[:1])
        self.grads.update(self.rs_in.finish(self.rs_in.start_join([self.delta[_SMALL[0]]])))
        self._adamw(["w_in"], ())


def kernel(x, norm_w, w_in, q_norm_w, k_norm_w, sinks, w_attn_proj, A_re, A_im, log_dt, B_re, B_im, C_re, C_im, D_skip, w_glu, b_glu, w_ssm_proj, w_out, loss_target, m_norm_w, m_w_in, m_q_norm_w, m_k_norm_w, m_sinks, m_w_attn_proj, m_A_re, m_A_im, m_log_dt, m_B_re, m_B_im, m_C_re, m_C_im, m_D_skip, m_w_glu, m_b_glu, m_w_ssm_proj, m_w_out, v_norm_w, v_w_in, v_q_norm_w, v_k_norm_w, v_sinks, v_w_attn_proj, v_A_re, v_A_im, v_log_dt, v_B_re, v_B_im, v_C_re, v_C_im, v_D_skip, v_w_glu, v_b_glu, v_w_ssm_proj, v_w_out):
    w = dict(norm_w=norm_w, w_in=w_in, q_norm_w=q_norm_w, k_norm_w=k_norm_w, sinks=sinks, w_attn_proj=w_attn_proj,
             A_re=A_re, A_im=A_im, log_dt=log_dt, B_re=B_re, B_im=B_im, C_re=C_re, C_im=C_im, D_skip=D_skip,
             w_glu=w_glu, b_glu=b_glu, w_ssm_proj=w_ssm_proj, w_out=w_out)
    m = dict(norm_w=m_norm_w, w_in=m_w_in, q_norm_w=m_q_norm_w, k_norm_w=m_k_norm_w, sinks=m_sinks,
             w_attn_proj=m_w_attn_proj, A_re=m_A_re, A_im=m_A_im, log_dt=m_log_dt, B_re=m_B_re, B_im=m_B_im,
             C_re=m_C_re, C_im=m_C_im, D_skip=m_D_skip, w_glu=m_w_glu, b_glu=m_b_glu, w_ssm_proj=m_w_ssm_proj,
             w_out=m_w_out)
    v = dict(norm_w=v_norm_w, w_in=v_w_in, q_norm_w=v_q_norm_w, k_norm_w=v_k_norm_w, sinks=v_sinks,
             w_attn_proj=v_w_attn_proj, A_re=v_A_re, A_im=v_A_im, log_dt=v_log_dt, B_re=v_B_re, B_im=v_B_im,
             C_re=v_C_re, C_im=v_C_im, D_skip=v_D_skip, w_glu=v_w_glu, b_glu=v_b_glu, w_ssm_proj=v_w_ssm_proj,
             w_out=v_w_out)

    io = _Exchanges(w, m, v)
    loss, grad_x, g_norm_w = _local_step(x[0], loss_target[0], norm_w, q_norm_w, k_norm_w, sinks, A_re, A_im, log_dt,
                                         B_re, B_im, C_re, C_im, D_skip, b_glu, io)
    loss = lax.psum(loss, ("x", "y", "c"))
    io.finish(g_norm_w, [grad_x])
    grads, delta, new_m, new_v = io.grads, io.delta, io.new_m, io.new_v

    return (loss, grad_x[None], *[grads[n] for n in _ORDER], *[delta[n] for n in _ORDER],
            *[new_m[n] for n in _ORDER], *[new_v[n] for n in _ORDER])
```

```python
import functools
import math

import jax
import jax.numpy as jnp
from jax import lax
from jax.experimental import pallas as pl
from jax.experimental.pallas import tpu as pltpu

F32 = jnp.float32
_MXU = jnp.bfloat16
_WIRE = jnp.bfloat16

LANES = 128
SUBLANES = 8
VMEM_LIMIT = 56 * 1024 * 1024

D_MODEL = 2048
HEAD_DIM = 64
N_Q_HEADS = 16
N_KV_HEADS = 4
Q_PER_KV = 4
ATTN_W = 1024
KV_W = 256
WINDOW = 128
SSM_W = 1024
GROUP = 16
N_GROUPS = 64
STATE = 64
N_STATES = N_GROUPS * STATE
IN_W = 8704
NORM_EPS = 1e-6
N_CHIPS = 4
CW = 512
OFF_AGATE, OFF_U, OFF_Z, OFF_GA, OFF_GS = 3, 5, 7, 9, 13

SSM_T = 256
SSM_L = SSM_T // SUBLANES
SSM_JB = 8
SSM_SB = N_STATES // SSM_JB

ADAM_LR, ADAM_B1, ADAM_B2, ADAM_EPS, ADAM_WD, ADAM_STEP = 0.001, 0.9, 0.999, 1e-08, 0.01, 10

MESH = pl.DeviceIdType.MESH
_ANY = pl.BlockSpec(memory_space=pl.ANY)


def _params(sem=None):
    return pltpu.CompilerParams(dimension_semantics=sem, vmem_limit_bytes=VMEM_LIMIT)


def _mm(a, b, *, mode, name, tm, tn, tk, out_dtype=F32, b_blocked=False, out_blocked=False, deps=()):
    nd = len(deps)
    if mode == "tn":
        K, M = a.shape
    else:
        M, K = a.shape
    if mode == "nn":
        N = b.shape[0] * b.shape[2] if b_blocked else b.shape[1]
    elif mode == "nt":
        N = b.shape[1] if b_blocked else b.shape[0]
    else:
        N = b.shape[1]
    tm, tn, tk = min(tm, M), min(tn, N), min(tk, K)
    nj, ni, nk = N // tn, M // tm, K // tk
    assert nj * tn == N and ni * tm == M and nk * tk == K, (name, M, N, K)
    dims = {"nn": (((1,), (0,)), ((), ())), "nt": (((1,), (1,)), ((), ())), "tn": (((0,), (0,)), ((), ()))}[mode]

    if mode == "tn":
        a_spec = pl.BlockSpec((tk, tm), lambda j, i, k: (k, i))
    else:
        a_spec = pl.BlockSpec((tm, tk), lambda j, i, k: (i, k))
    if mode == "nn":
        if b_blocked:
            assert b.shape[0] == nj and b.shape[2] == tn
            b_spec = pl.BlockSpec((None, tk, tn), lambda j, i, k: (j, k, 0))
        else:
            b_spec = pl.BlockSpec((tk, tn), lambda j, i, k: (k, j))
    elif mode == "nt":
        if b_blocked:
            assert b.shape[0] == nk and b.shape[2] == tk
            b_spec = pl.BlockSpec((None, tn, tk), lambda j, i, k: (k, j, 0))
        else:
            b_spec = pl.BlockSpec((tn, tk), lambda j, i, k: (j, k))
    else:
        b_spec = pl.BlockSpec((tk, tn), lambda j, i, k: (k, j))
    if out_blocked:
        assert nj == N_CHIPS
        o_spec = pl.BlockSpec((None, tm, tn), lambda j, i, k: (j, i, 0))
        o_shape = jax.ShapeDtypeStruct((nj, M, tn), out_dtype)
    else:
        o_spec = pl.BlockSpec((tm, tn), lambda j, i, k: (i, j))
        o_shape = jax.ShapeDtypeStruct((M, N), out_dtype)
    use_acc = nk > 1 and out_dtype != F32

    def body(a_ref, b_ref, *rest):
        o_ref, scratch = rest[nd], rest[nd + 1:]
        part = lax.dot_general(a_ref[...].astype(_MXU), b_ref[...].astype(_MXU), dims,
                               preferred_element_type=F32)
        if nk == 1:
            o_ref[...] = part.astype(o_ref.dtype)
            return
        k = pl.program_id(2)
        acc = scratch[0] if use_acc else o_ref

        @pl.when(k == 0)
        def _():
            acc[...] = part

        @pl.when(k > 0)
        def _():
            acc[...] += part

        if use_acc:
            @pl.when(k == nk - 1)
            def _():
                o_ref[...] = acc[...].astype(o_ref.dtype)

    return pl.pallas_call(
        body, name=name, grid=(nj, ni, nk), in_specs=[a_spec, b_spec] + [_ANY] * nd, out_specs=o_spec,
        out_shape=o_shape, scratch_shapes=[pltpu.VMEM((tm, tn), F32)] if use_acc else [],
        compiler_params=_params(("parallel", "parallel", "arbitrary")),
    )(a, b, *deps)


def _mm_chip_block(a, b4, blk, prev, *, name, tm=512, deps=()):
    M, K = a.shape
    nchip, _, C = b4.shape
    tm = min(tm, M)
    extra = ([] if prev is None else [prev]) + list(deps)

    def body(blk_ref, a_ref, b_ref, *rest):
        rest[-1][...] = jnp.dot(a_ref[...].astype(_MXU), b_ref[...].astype(_MXU), preferred_element_type=F32)

    return pl.pallas_call(
        body, name=name,
        grid_spec=pltpu.PrefetchScalarGridSpec(
            num_scalar_prefetch=1, grid=(M // tm,),
            in_specs=[pl.BlockSpec((tm, K), lambda i, c: (i, 0)), pl.BlockSpec((None, K, C), lambda i, c: (c[0], 0, 0))]
            + [_ANY] * len(extra),
            out_specs=pl.BlockSpec((tm, C), lambda i, c: (i, c[0]))),
        out_shape=jax.ShapeDtypeStruct((M, nchip * C), F32),
        input_output_aliases={} if prev is None else {3: 0},
        compiler_params=_params(("arbitrary",)),
    )(blk, a, b4, *extra)


def _ew(fn, ins, outs, *, rows, ncol, name, n_acc=0, tm=512, deps=()):
    n_in, n_out, nd = len(ins), len(outs), len(deps)
    tm = min(tm, rows)
    in_specs = []
    for _, kind, col0 in ins:
        if kind == "mat":
            in_specs.append(pl.BlockSpec((tm, CW), lambda j, i, c0=col0: (i, c0 + j)))
        else:
            in_specs.append(pl.BlockSpec((1, CW), lambda j, i, c0=col0: (0, c0 + j)))
    out_specs = [pl.BlockSpec((tm, CW), lambda j, i: (i, j)) for _ in outs]
    out_shape = [jax.ShapeDtypeStruct((rows, w), dt) for w, dt in outs]
    for _ in range(n_acc):
        out_specs.append(pl.BlockSpec((1, CW), lambda j, i: (0, j)))
        out_shape.append(jax.ShapeDtypeStruct((1, ncol * CW), F32))

    def body(*refs):
        vals = fn(*[r[...] for r in refs[:n_in]])
        refs = refs[n_in + nd:]
        for r, v in zip(refs[:n_out], vals[:n_out]):
            r[...] = v.astype(r.dtype)
        i = pl.program_id(1)
        for r, v in zip(refs[n_out:], vals[n_out:]):
            @pl.when(i == 0)
            def _(r=r, v=v):
                r[...] = v

            @pl.when(i > 0)
            def _(r=r, v=v):
                r[...] += v

    res = pl.pallas_call(
        body, name=name, grid=(ncol, rows // tm), in_specs=in_specs + [_ANY] * nd, out_specs=out_specs,
        out_shape=out_shape, compiler_params=_params(("parallel", "arbitrary")),
    )(*[a for a, _, _ in ins], *deps)
    return res


def _colsum(v):
    return jnp.sum(v, axis=0, keepdims=True)


def _sigmoid(v):
    return jax.nn.sigmoid(v)


def _silu_and_grad(v):
    s = _sigmoid(v)
    return v * s, s * (1.0 + v * (1.0 - s))


def _rms_fwd(x, w, *, tm=512, deps=()):
    rows, d = x.shape
    nd = len(deps)

    def body(x_ref, w_ref, *rest):
        h_ref, r_ref = rest[nd:]
        xv = x_ref[...]
        r = lax.rsqrt(jnp.mean(xv * xv, axis=-1, keepdims=True) + NORM_EPS)
        h_ref[...] = (xv * r * w_ref[...]).astype(h_ref.dtype)
        r_ref[...] = r

    return pl.pallas_call(
        body, name="rms_fwd", grid=(rows // tm,),
        in_specs=[pl.BlockSpec((tm, d), lambda i: (i, 0)), pl.BlockSpec((1, d), lambda i: (0, 0))] + [_ANY] * nd,
        out_specs=[pl.BlockSpec((tm, d), lambda i: (i, 0)), pl.BlockSpec((tm, 1), lambda i: (i, 0))],
        out_shape=[jax.ShapeDtypeStruct((rows, d), _MXU), jax.ShapeDtypeStruct((rows, 1), F32)],
        compiler_params=_params(("arbitrary",)),
    )(x, w, *deps)


def _rms_bwd(dh, x, rstd, w, dout, *, tm=256):
    rows, d = x.shape

    def body(dh_ref, x_ref, r_ref, w_ref, do_ref, gx_ref, gw_ref):
        dhv, xv, r, wv = dh_ref[...], x_ref[...], r_ref[...], w_ref[...]
        xr = xv * r
        t = jnp.mean(dhv * wv * xr, axis=-1, keepdims=True)
        gx_ref[...] = do_ref[...] + r * (wv * dhv - xr * t)
        part = _colsum(dhv * xr)
        i = pl.program_id(0)

        @pl.when(i == 0)
        def _():
            gw_ref[...] = part

        @pl.when(i > 0)
        def _():
            gw_ref[...] += part

    return pl.pallas_call(
        body, name="rms_bwd", grid=(rows // tm,),
        in_specs=[pl.BlockSpec((tm, d), lambda i: (i, 0)), pl.BlockSpec((tm, d), lambda i: (i, 0)),
                  pl.BlockSpec((tm, 1), lambda i: (i, 0)), pl.BlockSpec((1, d), lambda i: (0, 0)),
                  pl.BlockSpec((tm, d), lambda i: (i, 0))],
        out_specs=[pl.BlockSpec((tm, d), lambda i: (i, 0)), pl.BlockSpec((1, d), lambda i: (0, 0))],
        out_shape=[jax.ShapeDtypeStruct((rows, d), F32), jax.ShapeDtypeStruct((1, d), F32)],
        compiler_params=_params(("arbitrary",)),
    )(dh, x, rstd, w, dout)


_NT = (((1,), (1,)), ((), ()))
_TN = (((0,), (0,)), ((), ()))


QKV_W = ATTN_W + 2 * KV_W
HEADS_PER_TILE = LANES // HEAD_DIM


def _low_half(rows):
    return lax.broadcasted_iota(jnp.int32, (rows, LANES), 1) < HEAD_DIM


def _pair_mean(t, low):
    m_lo = jnp.sum(jnp.where(low, t, 0.0), axis=-1, keepdims=True)
    m_hi = jnp.sum(jnp.where(low, 0.0, t), axis=-1, keepdims=True)
    return jnp.where(low, m_lo, m_hi) * (1.0 / HEAD_DIM)


def _pair_rstd(t, low):
    return lax.rsqrt(_pair_mean(t * t, low) + NORM_EPS)


def _dup_half(t, hi, low):
    swapped = pltpu.roll(t, HEAD_DIM, 1)
    return jnp.where(low, swapped, t) if hi else jnp.where(low, t, swapped)


def _fold_halves(t):
    return t + pltpu.roll(t, HEAD_DIM, 1)


def _split_heads(t, low):
    return [jnp.where(low, t, 0.0), jnp.where(low, 0.0, t)]


def _stacked_band_mask(n):
    rows = Q_PER_KV * WINDOW
    qi = lax.broadcasted_iota(jnp.int32, (rows, 2 * WINDOW), 0) % WINDOW + WINDOW
    kj = lax.broadcasted_iota(jnp.int32, (rows, 2 * WINDOW), 1)
    diff = qi - kj
    first_key = jnp.where(n > 0, 0, WINDOW)
    return (diff >= 0) & (diff < WINDOW) & (kj >= first_key)


def _stacked_sinks(sink_ref, g):
    blk = lax.broadcasted_iota(jnp.int32, (Q_PER_KV * WINDOW, 1), 0) // WINDOW
    col = jnp.full((Q_PER_KV * WINDOW, 1), sink_ref[Q_PER_KV * g], F32)
    for r in range(1, Q_PER_KV):
        col = jnp.where(blk == r, sink_ref[Q_PER_KV * g + r], col)
    return col


def _attn_in_specs(nblk, rev):
    def cur(n):
        return (nblk - 1 - n) if rev else n

    q_spec = pl.BlockSpec((WINDOW, ATTN_W), lambda n: (cur(n), 0))
    kvc_spec = pl.BlockSpec((WINDOW, 2 * KV_W), lambda n: (cur(n), ATTN_W // (2 * KV_W)))
    kvp_spec = pl.BlockSpec((WINDOW, 2 * KV_W), lambda n: (jnp.maximum(cur(n) - 1, 0), ATTN_W // (2 * KV_W)))
    w_spec = pl.BlockSpec((1, LANES), lambda n: (0, 0))
    l_spec = pl.BlockSpec((WINDOW, N_Q_HEADS), lambda n: (cur(n), 0))
    return q_spec, kvc_spec, kvp_spec, w_spec, l_spec


def _attn2_fwd(proj, qw2, kw2, sinks, deps=()):
    seq = proj.shape[0]
    nblk = seq // WINDOW
    scale = 1.0 / math.sqrt(HEAD_DIM)
    q_spec, kvc_spec, kvp_spec, w_spec, l_spec = _attn_in_specs(nblk, False)
    nd = len(deps)

    def body(sink_ref, q_ref, kvc_ref, kvp_ref, qw_ref, kw_ref, *rest):
        o_ref, lse_ref = rest[nd:]
        n = pl.program_id(0)
        low, low2 = _low_half(WINDOW), _low_half(2 * WINDOW)
        valid = _stacked_band_mask(n)
        head_lane = lax.broadcasted_iota(jnp.int32, (WINDOW, N_Q_HEADS), 1)
        kv = jnp.concatenate([kvp_ref[...], kvc_ref[...]], axis=0)
        qwv, kwv = qw_ref[...], kw_ref[...]
        lse_blk = jnp.zeros((WINDOW, N_Q_HEADS), F32)
        for t in range(N_KV_HEADS // HEADS_PER_TILE):
            kt = kv[:, t * LANES:(t + 1) * LANES]
            vt = kv[:, KV_W + t * LANES:KV_W + (t + 1) * LANES]
            kn = kt * _pair_rstd(kt, low2) * kwv
            for hi in range(HEADS_PER_TILE):
                g = HEADS_PER_TILE * t + hi
                kdup = _dup_half(kn, hi, low2).astype(_MXU)
                vdup = _dup_half(vt, hi, low2).astype(_MXU)
                stack = []
                for tq in (2 * g, 2 * g + 1):
                    qt = q_ref[:, tq * LANES:(tq + 1) * LANES]
                    stack += _split_heads(qt * _pair_rstd(qt, low) * qwv, low)
                qs = jnp.concatenate(stack, axis=0).astype(_MXU)
                s = lax.dot_general(qs, kdup, _NT, preferred_element_type=F32) * scale
                s = jnp.where(valid, s, -1e30)
                sink = _stacked_sinks(sink_ref, g)
                m = jnp.maximum(jnp.max(s, axis=-1, keepdims=True), sink)
                e = jnp.exp(s - m)
                z = jnp.sum(e, axis=-1, keepdims=True) + jnp.exp(sink - m)
                o = jnp.dot((e / z).astype(_MXU), vdup, preferred_element_type=F32)
                for i, tq in enumerate((2 * g, 2 * g + 1)):
                    o_ref[:, tq * LANES:(tq + 1) * LANES] = jnp.where(
                        low, o[2 * i * WINDOW:(2 * i + 1) * WINDOW], o[(2 * i + 1) * WINDOW:(2 * i + 2) * WINDOW])
                lse = m + jnp.log(z)
                for r in range(Q_PER_KV):
                    lse_blk = jnp.where(head_lane == Q_PER_KV * g + r, lse[r * WINDOW:(r + 1) * WINDOW], lse_blk)
        lse_ref[...] = lse_blk

    return pl.pallas_call(
        body, name="attn_fwd", grid=(nblk,),
        in_specs=[pl.BlockSpec(memory_space=pltpu.SMEM), q_spec, kvc_spec, kvp_spec, w_spec, w_spec] + [_ANY] * nd,
        out_specs=[q_spec, l_spec],
        out_shape=[jax.ShapeDtypeStruct((seq, ATTN_W), F32), jax.ShapeDtypeStruct((seq, N_Q_HEADS), F32)],
        compiler_params=_params(("arbitrary",)),
    )(sinks, proj, proj, proj, qw2, kw2, *deps)


def _attn2_bwd(proj, qw2, kw2, sinks, lse, do, deps=()):
    seq = proj.shape[0]
    nblk = seq // WINDOW
    scale = 1.0 / math.sqrt(HEAD_DIM)
    q_spec, kvc_spec, kvp_spec, w_spec, l_spec = _attn_in_specs(nblk, True)
    s_spec = pl.BlockSpec((1, N_Q_HEADS), lambda n: (0, 0))
    d_spec = pl.BlockSpec((WINDOW, QKV_W), lambda n: (nblk - 1 - n, 0))
    nd = len(deps)

    def body(sink_ref, q_ref, kvc_ref, kvp_ref, qw_ref, kw_ref, lse_ref, do_ref, *rest):
        d_ref, dqw_ref, dkw_ref, dsk_ref, carry = rest[nd:]
        step = pl.program_id(0)
        n = nblk - 1 - step

        @pl.when(step == 0)
        def _():
            carry[...] = jnp.zeros_like(carry)
            dqw_ref[...] = jnp.zeros_like(dqw_ref)
            dkw_ref[...] = jnp.zeros_like(dkw_ref)
            dsk_ref[...] = jnp.zeros_like(dsk_ref)

        low, low2 = _low_half(WINDOW), _low_half(2 * WINDOW)
        valid = _stacked_band_mask(n)
        head_lane = lax.broadcasted_iota(jnp.int32, (WINDOW, N_Q_HEADS), 1)
        sink_lane = lax.broadcasted_iota(jnp.int32, (1, N_Q_HEADS), 1)
        kv = jnp.concatenate([kvp_ref[...], kvc_ref[...]], axis=0)
        qwv, kwv = qw_ref[...], kw_ref[...]
        lse_blk = lse_ref[...]
        dqw = jnp.zeros((1, LANES), F32)
        dkw = jnp.zeros((1, LANES), F32)
        dsk = jnp.zeros((1, N_Q_HEADS), F32)
        for t in range(N_KV_HEADS // HEADS_PER_TILE):
            kt = kv[:, t * LANES:(t + 1) * LANES]
            vt = kv[:, KV_W + t * LANES:KV_W + (t + 1) * LANES]
            rk = _pair_rstd(kt, low2)
            kn = kt * rk * kwv
            dkn_t = jnp.zeros((2 * WINDOW, LANES), F32)
            dv_t = jnp.zeros((2 * WINDOW, LANES), F32)
            for hi in range(HEADS_PER_TILE):
                g = HEADS_PER_TILE * t + hi
                kdup = _dup_half(kn, hi, low2).astype(_MXU)
                vdup = _dup_half(vt, hi, low2).astype(_MXU)
                tiles = (2 * g, 2 * g + 1)
                qx, rq, stack, dstack, lse_rows = [], [], [], [], []
                for tq in tiles:
                    qt = q_ref[:, tq * LANES:(tq + 1) * LANES]
                    r = _pair_rstd(qt, low)
                    rq.append(r)
                    qx.append(qt * r)
                    stack += _split_heads(qx[-1] * qwv, low)
                    dstack += _split_heads(do_ref[:, tq * LANES:(tq + 1) * LANES], low)
                for r in range(Q_PER_KV):
                    lse_rows.append(jnp.sum(jnp.where(head_lane == Q_PER_KV * g + r, lse_blk, 0.0), axis=-1, keepdims=True))
                qs = jnp.concatenate(stack, axis=0).astype(_MXU)
                dos = jnp.concatenate(dstack, axis=0).astype(_MXU)
                lse_col = jnp.concatenate(lse_rows, axis=0)
                s = lax.dot_general(qs, kdup, _NT, preferred_element_type=F32) * scale
                s = jnp.where(valid, s, -1e30)
                p = jnp.exp(s - lse_col)
                dp = lax.dot_general(dos, vdup, _NT, preferred_element_type=F32)
                dsum = jnp.sum(p * dp, axis=-1, keepdims=True)
                ds = (p * (dp - dsum) * scale).astype(_MXU)
                dsink = -jnp.exp(_stacked_sinks(sink_ref, g) - lse_col) * dsum
                for r in range(Q_PER_KV):
                    dsk = dsk + jnp.where(sink_lane == Q_PER_KV * g + r, _colsum(dsink[r * WINDOW:(r + 1) * WINDOW]), 0.0)
                dv_g = _fold_halves(lax.dot_general(p.astype(_MXU), dos, _TN, preferred_element_type=F32))
                dkn_g = _fold_halves(lax.dot_general(ds, qs, _TN, preferred_element_type=F32))
                dv_t = jnp.where(low2, dv_t, dv_g) if hi else jnp.where(low2, dv_g, dv_t)
                dkn_t = jnp.where(low2, dkn_t, dkn_g) if hi else jnp.where(low2, dkn_g, dkn_t)
                dqn = jnp.dot(ds, kdup, preferred_element_type=F32)
                for i, tq in enumerate(tiles):
                    dqn_t = jnp.where(low, dqn[2 * i * WINDOW:(2 * i + 1) * WINDOW],
                                      dqn[(2 * i + 1) * WINDOW:(2 * i + 2) * WINDOW])
                    dq = rq[i] * (qwv * dqn_t - qx[i] * _pair_mean(dqn_t * qwv * qx[i], low))
                    d_ref[:, tq * LANES:(tq + 1) * LANES] = dq.astype(d_ref.dtype)
                    dqw = dqw + _colsum(dqn_t * qx[i])
            k_cols = slice(t * LANES, (t + 1) * LANES)
            v_cols = slice(KV_W + t * LANES, KV_W + (t + 1) * LANES)
            dkn_c = dkn_t[WINDOW:] + carry[:, k_cols]
            rc = rk[WINDOW:]
            kx = kt[WINDOW:] * rc
            dk = rc * (kwv * dkn_c - kx * _pair_mean(dkn_c * kwv * kx, low))
            d_ref[:, ATTN_W + t * LANES:ATTN_W + (t + 1) * LANES] = dk.astype(d_ref.dtype)
            d_ref[:, ATTN_W + KV_W + t * LANES:ATTN_W + KV_W + (t + 1) * LANES] = (
                dv_t[WINDOW:] + carry[:, v_cols]).astype(d_ref.dtype)
            carry[:, k_cols] = dkn_t[:WINDOW]
            carry[:, v_cols] = dv_t[:WINDOW]
            dkw = dkw + _colsum(dkn_c * kx)
        dqw_ref[...] += dqw
        dkw_ref[...] += dkw
        dsk_ref[...] += dsk

    return pl.pallas_call(
        body, name="attn_bwd", grid=(nblk,),
        in_specs=[pl.BlockSpec(memory_space=pltpu.SMEM), q_spec, kvc_spec, kvp_spec, w_spec, w_spec, l_spec, q_spec]
        + [_ANY] * nd,
        out_specs=[d_spec, w_spec, w_spec, s_spec],
        out_shape=[jax.ShapeDtypeStruct((seq, QKV_W), _MXU), jax.ShapeDtypeStruct((1, LANES), F32),
                   jax.ShapeDtypeStruct((1, LANES), F32), jax.ShapeDtypeStruct((1, N_Q_HEADS), F32)],
        scratch_shapes=[pltpu.VMEM((WINDOW, 2 * KV_W), F32)],
        compiler_params=_params(("arbitrary",)),
    )(sinks, proj, proj, proj, qw2, kw2, lse, do, *deps)


def _ssm_discretise(a_re, a_im, log_dt):
    dt = jnp.exp(log_dt)
    mag = jnp.exp(dt * a_re)
    ab_re = mag * jnp.cos(dt * a_im)
    ab_im = mag * jnp.sin(dt * a_im)
    num_re = ab_re - 1.0
    num_im = ab_im
    den = a_re * a_re + a_im * a_im
    cf_re = (num_re * a_re + num_im * a_im) / den
    cf_im = (num_im * a_re - num_re * a_im) / den
    return ab_re, ab_im, cf_re, cf_im


def _ssm_params_fwd(a_re, a_im, log_dt):
    shp = jax.ShapeDtypeStruct(a_re.shape, F32)

    def body(are_ref, aim_ref, ldt_ref, abr_ref, abi_ref, cfr_ref, cfi_ref, alr_ref, ali_ref):
        abr, abi, cfr, cfi = _ssm_discretise(are_ref[...], aim_ref[...], ldt_ref[...])
        abr_ref[...], abi_ref[...], cfr_ref[...], cfi_ref[...] = abr, abi, cfr, cfi
        pr, pi = abr, abi
        for _ in range(int(math.log2(SSM_L))):
            pr, pi = pr * pr - pi * pi, 2.0 * pr * pi
        alr_ref[...], ali_ref[...] = pr, pi

    return pl.pallas_call(body, name="ssm_params_fwd", out_shape=[shp] * 6)(a_re, a_im, log_dt)


def _ssm_params_bwd(a_re, a_im, log_dt, d_abr, d_abi, d_cfr, d_cfi):
    def body(are_ref, aim_ref, ldt_ref, g0, g1, g2, g3, dare_ref, daim_ref, dldt_ref):
        _, vjp = jax.vjp(_ssm_discretise, are_ref[...], aim_ref[...], ldt_ref[...])
        dare_ref[...], daim_ref[...], dldt_ref[...] = vjp((g0[...], g1[...], g2[...], g3[...]))

    return pl.pallas_call(
        body, name="ssm_params_bwd",
        out_shape=[jax.ShapeDtypeStruct(a_re.shape, F32), jax.ShapeDtypeStruct(a_im.shape, F32),
                   jax.ShapeDtypeStruct(log_dt.shape, F32)],
    )(a_re, a_im, log_dt, d_abr, d_abi, d_cfr, d_cfi)


def _scan_cols(j):
    return pl.ds(j * SSM_SB, SSM_SB)


def _rows8(r):
    return pl.ds(pl.multiple_of(r * SUBLANES, SUBLANES), SUBLANES)


def _bcast8(row):
    return jnp.broadcast_to(row, (SUBLANES, row.shape[-1]))


SCAN_UNROLL = 8


def _scan_loop(n, step, init):
    def trip(o, carry):
        for i in range(SCAN_UNROLL):
            carry = step(o * SCAN_UNROLL + i, carry)
        return carry

    return lax.fori_loop(0, n // SCAN_UNROLL, trip, init)


def _ssm_fwd(u, b_re, b_im, c_re, c_im, d_skip, coef):
    seq = u.shape[0]
    nc = seq // SSM_T
    T, L = SSM_T, SSM_L

    def body(u_ref, bre_ref, bim_ref, cre_ref, cim_ref, d_ref, are_ref, aim_ref, cfr_ref, cfi_ref, alr_ref, ali_ref,
             y_ref, sre_ref, sim_ref, ire_ref, iim_ref, car_re, car_im, end_re, end_im):
        c = pl.program_id(0)

        @pl.when(c == 0)
        def _():
            car_re[...] = jnp.zeros_like(car_re)
            car_im[...] = jnp.zeros_like(car_im)

        for j in range(SSM_JB):
            ub = u_ref[:, j * LANES:(j + 1) * LANES].astype(_MXU)
            bur = jnp.dot(ub, bre_ref[j], preferred_element_type=F32)
            bui = jnp.dot(ub, bim_ref[j], preferred_element_type=F32)
            cfr, cfi = cfr_ref[:, _scan_cols(j)], cfi_ref[:, _scan_cols(j)]
            sre_ref[:, _scan_cols(j)] = cfr * bur - cfi * bui
            sim_ref[:, _scan_cols(j)] = cfr * bui + cfi * bur

        for j in range(SSM_JB):
            cols = _scan_cols(j)
            ar, ai = _bcast8(are_ref[:, cols]), _bcast8(aim_ref[:, cols])

            def step1(r, s, cols=cols, ar=ar, ai=ai):
                sr, si = s
                rows = _rows8(r)
                return (ar * sr - ai * si + sre_ref[rows, cols], ar * si + ai * sr + sim_ref[rows, cols])

            zero = jnp.zeros((SUBLANES, SSM_SB), F32)
            er, ei = _scan_loop(L, step1, (zero, zero))
            end_re[:, cols] = er
            end_im[:, cols] = ei

        alr, ali = alr_ref[...], ali_ref[...]
        cr, ci = car_re[...], car_im[...]
        ire_ref[0:1, :] = cr
        iim_ref[0:1, :] = ci
        for i in range(1, SUBLANES):
            er, ei = end_re[i - 1:i, :], end_im[i - 1:i, :]
            cr, ci = alr * cr - ali * ci + er, alr * ci + ali * cr + ei
            ire_ref[i:i + 1, :] = cr
            iim_ref[i:i + 1, :] = ci

        for j in range(SSM_JB):
            cols = _scan_cols(j)
            ar, ai = _bcast8(are_ref[:, cols]), _bcast8(aim_ref[:, cols])

            def step2(r, s, cols=cols, ar=ar, ai=ai):
                sr, si = s
                rows = _rows8(r)
                nr = ar * sr - ai * si + sre_ref[rows, cols]
                ni = ar * si + ai * sr + sim_ref[rows, cols]
                sre_ref[rows, cols] = nr
                sim_ref[rows, cols] = ni
                return nr, ni

            _scan_loop(L, step2, (ire_ref[:, cols], iim_ref[:, cols]))

        car_re[...] = sre_ref[T - 1:T, :]
        car_im[...] = sim_ref[T - 1:T, :]

        for j in range(SSM_JB):
            cols = _scan_cols(j)
            ch = slice(j * LANES, (j + 1) * LANES)
            y = (jnp.dot(sre_ref[:, cols].astype(_MXU), cre_ref[j], preferred_element_type=F32)
                 - jnp.dot(sim_ref[:, cols].astype(_MXU), cim_ref[j], preferred_element_type=F32))
            y_ref[:, ch] = y + d_ref[:, ch] * u_ref[:, ch]

    tok = pl.BlockSpec((T, SSM_W), lambda c: (c, 0))
    st = pl.BlockSpec((T, N_STATES), lambda c: (c, 0))
    ini = pl.BlockSpec((None, SUBLANES, N_STATES), lambda c: (c, 0, 0))
    bsp = pl.BlockSpec((SSM_JB, LANES, SSM_SB), lambda c: (0, 0, 0))
    csp = pl.BlockSpec((SSM_JB, SSM_SB, LANES), lambda c: (0, 0, 0))
    row_w = pl.BlockSpec((1, SSM_W), lambda c: (0, 0))
    row_s = pl.BlockSpec((1, N_STATES), lambda c: (0, 0))
    return pl.pallas_call(
        body, name="ssm_fwd", grid=(nc,),
        in_specs=[tok, bsp, bsp, csp, csp, row_w] + [row_s] * 6,
        out_specs=[tok, st, st, ini, ini],
        out_shape=[jax.ShapeDtypeStruct((seq, SSM_W), F32),
                   jax.ShapeDtypeStruct((seq, N_STATES), F32), jax.ShapeDtypeStruct((seq, N_STATES), F32),
                   jax.ShapeDtypeStruct((nc, SUBLANES, N_STATES), F32),
                   jax.ShapeDtypeStruct((nc, SUBLANES, N_STATES), F32)],
        scratch_shapes=[pltpu.VMEM((1, N_STATES), F32), pltpu.VMEM((1, N_STATES), F32),
                        pltpu.VMEM((SUBLANES, N_STATES), F32), pltpu.VMEM((SUBLANES, N_STATES), F32)],
        compiler_params=_params(("arbitrary",)),
    )(u, b_re, b_im, c_re, c_im, d_skip, *coef)


def _ssm_bwd(dy, u, s_re, s_im, i_re, i_im, b_re, b_im, c_re, c_im, d_skip, coef):
    seq = u.shape[0]
    nc = seq // SSM_T
    T, L = SSM_T, SSM_L

    def body(dy_ref, u_ref, sre_ref, sim_ref, ire_ref, iim_ref, bre_ref, bim_ref, cre_ref, cim_ref, d_ref,
             are_ref, aim_ref, cfr_ref, cfi_ref, alr_ref, ali_ref,
             du_ref, dbre_out, dbim_out, dcre_out, dcim_out, dd_ref, dar_ref, dai_ref, dcfr_ref, dcfi_ref,
             lre, lim, car_re, car_im, end_re, end_im, ini_re, ini_im, dbre_ref, dbim_ref, dcre_ref, dcim_ref):
        step = pl.program_id(0)

        @pl.when(step == 0)
        def _():
            car_re[...] = jnp.zeros_like(car_re)
            car_im[...] = jnp.zeros_like(car_im)
            for ref in (dbre_ref, dbim_ref, dcre_ref, dcim_ref, dd_ref, dar_ref, dai_ref, dcfr_ref, dcfi_ref):
                ref[...] = jnp.zeros_like(ref)

        for j in range(SSM_JB):
            dyb = dy_ref[:, j * LANES:(j + 1) * LANES].astype(_MXU)
            lre[:, _scan_cols(j)] = lax.dot_general(dyb, cre_ref[j], _NT, preferred_element_type=F32)
            lim[:, _scan_cols(j)] = -lax.dot_general(dyb, cim_ref[j], _NT, preferred_element_type=F32)

        for j in range(SSM_JB):
            cols = _scan_cols(j)
            ar, ai = _bcast8(are_ref[:, cols]), _bcast8(aim_ref[:, cols])

            def step1(t, s, cols=cols, ar=ar, ai=ai):
                sr, si = s
                rows = _rows8(L - 1 - t)
                return (ar * sr + ai * si + lre[rows, cols], ar * si - ai * sr + lim[rows, cols])

            zero = jnp.zeros((SUBLANES, SSM_SB), F32)
            er, ei = _scan_loop(L, step1, (zero, zero))
            end_re[:, cols] = er
            end_im[:, cols] = ei

        alr, ali = alr_ref[...], ali_ref[...]
        cr, ci = car_re[...], car_im[...]
        ini_re[SUBLANES - 1:SUBLANES, :] = cr
        ini_im[SUBLANES - 1:SUBLANES, :] = ci
        for i in range(SUBLANES - 2, -1, -1):
            er, ei = end_re[i + 1:i + 2, :], end_im[i + 1:i + 2, :]
            cr, ci = alr * cr + ali * ci + er, alr * ci - ali * cr + ei
            ini_re[i:i + 1, :] = cr
            ini_im[i:i + 1, :] = ci

        for j in range(SSM_JB):
            cols = _scan_cols(j)
            ar, ai = _bcast8(are_ref[:, cols]), _bcast8(aim_ref[:, cols])

            def step2(t, s, cols=cols, ar=ar, ai=ai):
                sr, si = s
                rows = _rows8(L - 1 - t)
                nr = ar * sr + ai * si + lre[rows, cols]
                ni = ar * si - ai * sr + lim[rows, cols]
                lre[rows, cols] = nr
                lim[rows, cols] = ni
                return nr, ni

            _scan_loop(L, step2, (ini_re[:, cols], ini_im[:, cols]))

        car_re[...] = lre[0:1, :]
        car_im[...] = lim[0:1, :]

        head, tail, body_rows = slice(0, SUBLANES), slice(SUBLANES, T), slice(0, T - SUBLANES)
        for j in range(SSM_JB):
            cols = _scan_cols(j)
            ch = slice(j * LANES, (j + 1) * LANES)
            lr, li = lre[:, cols], lim[:, cols]
            dar_ref[:, cols] += (_colsum(lre[tail, cols] * sre_ref[body_rows, cols] + lim[tail, cols] * sim_ref[body_rows, cols])
                                 + _colsum(lre[head, cols] * ire_ref[:, cols] + lim[head, cols] * iim_ref[:, cols]))
            dai_ref[:, cols] += (_colsum(lim[tail, cols] * sre_ref[body_rows, cols] - lre[tail, cols] * sim_ref[body_rows, cols])
                                 + _colsum(lim[head, cols] * ire_ref[:, cols] - lre[head, cols] * iim_ref[:, cols]))
            uf = u_ref[:, ch]
            ub = uf.astype(_MXU)
            bur = jnp.dot(ub, bre_ref[j], preferred_element_type=F32)
            bui = jnp.dot(ub, bim_ref[j], preferred_element_type=F32)
            dcfr_ref[:, cols] += _colsum(lr * bur + li * bui)
            dcfi_ref[:, cols] += _colsum(li * bur - lr * bui)
            cfr, cfi = cfr_ref[:, cols], cfi_ref[:, cols]
            dbur = (cfr * lr + cfi * li).astype(_MXU)
            dbui = (cfr * li - cfi * lr).astype(_MXU)
            dyf = dy_ref[:, ch]
            dyb = dyf.astype(_MXU)
            du_ref[:, ch] = (lax.dot_general(dbur, bre_ref[j], _NT, preferred_element_type=F32)
                             + lax.dot_general(dbui, bim_ref[j], _NT, preferred_element_type=F32)
                             + d_ref[:, ch] * dyf)
            dbre_ref[j] += lax.dot_general(ub, dbur, _TN, preferred_element_type=F32)
            dbim_ref[j] += lax.dot_general(ub, dbui, _TN, preferred_element_type=F32)
            dcre_ref[j] += lax.dot_general(sre_ref[:, cols].astype(_MXU), dyb, _TN, preferred_element_type=F32)
            dcim_ref[j] -= lax.dot_general(sim_ref[:, cols].astype(_MXU), dyb, _TN, preferred_element_type=F32)
            dd_ref[:, ch] += _colsum(dyf * uf)

        @pl.when(step == nc - 1)
        def _():
            for acc, out in ((dbre_ref, dbre_out), (dbim_ref, dbim_out), (dcre_ref, dcre_out), (dcim_ref, dcim_out)):
                pltpu.sync_copy(acc, out)

    tok = pl.BlockSpec((T, SSM_W), lambda c: (nc - 1 - c, 0))
    st = pl.BlockSpec((T, N_STATES), lambda c: (nc - 1 - c, 0))
    ini = pl.BlockSpec((None, SUBLANES, N_STATES), lambda c: (nc - 1 - c, 0, 0))
    bsp = pl.BlockSpec((SSM_JB, LANES, SSM_SB), lambda c: (0, 0, 0))
    csp = pl.BlockSpec((SSM_JB, SSM_SB, LANES), lambda c: (0, 0, 0))
    row_w = pl.BlockSpec((1, SSM_W), lambda c: (0, 0))
    row_s = pl.BlockSpec((1, N_STATES), lambda c: (0, 0))
    big = pltpu.VMEM((T, N_STATES), F32)
    one = pltpu.VMEM((1, N_STATES), F32)
    eight = pltpu.VMEM((SUBLANES, N_STATES), F32)
    return pl.pallas_call(
        body, name="ssm_bwd", grid=(nc,),
        in_specs=[tok, tok, st, st, ini, ini, bsp, bsp, csp, csp, row_w] + [row_s] * 6,
        out_specs=[tok, _ANY, _ANY, _ANY, _ANY, row_w, row_s, row_s, row_s, row_s],
        out_shape=[jax.ShapeDtypeStruct((seq, SSM_W), F32),
                   jax.ShapeDtypeStruct((SSM_JB, LANES, SSM_SB), F32), jax.ShapeDtypeStruct((SSM_JB, LANES, SSM_SB), F32),
                   jax.ShapeDtypeStruct((SSM_JB, SSM_SB, LANES), F32), jax.ShapeDtypeStruct((SSM_JB, SSM_SB, LANES), F32),
                   jax.ShapeDtypeStruct((1, SSM_W), F32)] + [jax.ShapeDtypeStruct((1, N_STATES), F32)] * 4,
        scratch_shapes=[big, big, one, one, eight, eight, eight, eight,
                        pltpu.VMEM((SSM_JB, LANES, SSM_SB), F32), pltpu.VMEM((SSM_JB, LANES, SSM_SB), F32),
                        pltpu.VMEM((SSM_JB, SSM_SB, LANES), F32), pltpu.VMEM((SSM_JB, SSM_SB, LANES), F32)],
        compiler_params=_params(("arbitrary",)),
    )(dy, u, s_re, s_im, i_re, i_im, b_re, b_im, c_re, c_im, d_skip, *coef)


def _block_diag_b(b):
    t = b.reshape(SSM_JB, 8, STATE, GROUP).transpose(0, 1, 3, 2)
    eye = jnp.eye(8, dtype=b.dtype)
    return (t[:, :, :, None, :] * eye[None, :, None, :, None]).reshape(SSM_JB, LANES, SSM_SB)


def _block_diag_c(c):
    t = c.reshape(SSM_JB, 8, GROUP, STATE).transpose(0, 1, 3, 2)
    eye = jnp.eye(8, dtype=c.dtype)
    return (t[:, :, :, None, :] * eye[None, :, None, :, None]).reshape(SSM_JB, SSM_SB, LANES)


def _diag_of_b(blk):
    t = blk.reshape(SSM_JB, 8, GROUP, 8, STATE)
    d = jnp.sum(t * jnp.eye(8, dtype=blk.dtype)[None, :, None, :, None], axis=3)
    return d.transpose(0, 1, 3, 2).reshape(N_GROUPS, STATE, GROUP)


def _diag_of_c(blk):
    t = blk.reshape(SSM_JB, 8, STATE, 8, GROUP)
    d = jnp.sum(t * jnp.eye(8, dtype=blk.dtype)[None, :, None, :, None], axis=3)
    return d.transpose(0, 1, 3, 2).reshape(N_GROUPS, GROUP, STATE)


def _to_scan_order(v):
    seq, w = v.shape
    return v.reshape(seq // SSM_T, SUBLANES, SSM_L, w).transpose(0, 2, 1, 3).reshape(seq, w)


def _from_scan_order(v):
    seq, w = v.shape
    return v.reshape(seq // SSM_T, SSM_L, SUBLANES, w).transpose(0, 2, 1, 3).reshape(seq, w)


def _adamw_math(w, g, m, v):
    nm = ADAM_B1 * m + (1.0 - ADAM_B1) * g
    nv = ADAM_B2 * v + (1.0 - ADAM_B2) * jnp.square(g)
    m_hat = nm / (1.0 - ADAM_B1 ** ADAM_STEP)
    v_hat = nv / (1.0 - ADAM_B2 ** ADAM_STEP)
    return -ADAM_LR * (m_hat / (jnp.sqrt(v_hat) + ADAM_EPS) + ADAM_WD * w), nm, nv


def _adamw(w, g, m, v, *, name, tm, deps=()):
    rows, cols = w.shape
    nd = len(deps)

    def body(w_ref, g_ref, m_ref, v_ref, *rest):
        d_ref, nm_ref, nv_ref = rest[nd:]
        d_ref[...], nm_ref[...], nv_ref[...] = _adamw_math(w_ref[...], g_ref[...], m_ref[...], v_ref[...])

    spec = pl.BlockSpec((tm, cols), lambda i: (i, 0))
    shp = jax.ShapeDtypeStruct((rows, cols), F32)
    return pl.pallas_call(body, name=name, grid=(rows // tm,), in_specs=[spec] * 4 + [_ANY] * nd,
                          out_specs=[spec] * 3, out_shape=[shp] * 3,
                          compiler_params=_params(("arbitrary",)))(w, g, m, v, *deps)


def _place():
    x, y, c = lax.axis_index("x"), lax.axis_index("y"), lax.axis_index("c")
    chips = [(1 - x, y), (x, 1 - y), (1 - x, 1 - y)]
    return x, y, c, chips


def _remote(src, dst, send_sem, recv_sem, dev):
    return pltpu.make_async_remote_copy(src_ref=src, dst_ref=dst, send_sem=send_sem, recv_sem=recv_sem,
                                        device_id=dev, device_id_type=MESH)


def _place_shard(w, mine_arr, *, name, tm=256):
    rows, cols = w.shape

    def body(m_ref, w_ref, o_ref):
        o_ref[...] = w_ref[...].astype(o_ref.dtype)

    return pl.pallas_call(
        body, name=name,
        grid_spec=pltpu.PrefetchScalarGridSpec(
            num_scalar_prefetch=1, grid=(rows // tm,),
            in_specs=[pl.BlockSpec((tm, cols), lambda i, m: (i, 0))],
            out_specs=pl.BlockSpec((None, tm, cols), lambda i, m: (m[0], i, 0))),
        out_shape=jax.ShapeDtypeStruct((N_CHIPS, rows, cols), _WIRE),
        compiler_params=_params(("arbitrary",)),
    )(mine_arr, w)


_HBM = pl.BlockSpec(memory_space=pltpu.HBM)
_SEM = pl.BlockSpec(memory_space=pltpu.SEMAPHORE)
_EFFECT = pltpu.SideEffectType.DATAFLOW_SIDE_EFFECTING


def _copies_start(name, bufs, plan, count, after=()):
    nb, na = len(bufs), len(after)

    def body(*refs):
        send_sems, recv_sems, token = refs[nb + na], refs[nb + na + 1], refs[-1]
        copies = plan(refs[:nb])
        assert len(copies) == count
        for i, (src, dst, dev, _) in enumerate(copies):
            _remote(src, dst, send_sems.at[i], recv_sems.at[i], dev).start()
        token[...] = jnp.zeros_like(token)

    res = pl.pallas_call(
        body, name=name, in_specs=[_HBM] * nb + [_ANY] * na,
        out_specs=(_SEM, _SEM, *[_HBM] * nb, pl.BlockSpec(memory_space=pltpu.VMEM)),
        out_shape=(pltpu.SemaphoreType.DMA((count,)), pltpu.SemaphoreType.DMA((count,)),
                   *[pltpu.HBM(b.shape, b.dtype) for b in bufs], jax.ShapeDtypeStruct((SUBLANES, LANES), F32)),
        input_output_aliases={i: 2 + i for i in range(nb)},
        compiler_params=pltpu.CompilerParams(has_side_effects=_EFFECT),
    )(*[pltpu.with_memory_space_constraint(b, pltpu.HBM) for b in bufs], *after)
    return (res[0], res[1]), list(res[2:2 + nb]), res[-1]


def _copies_wait(name, bufs, sems, plan, after=(), which=None):
    nb, na = len(bufs), len(after)

    def body(*refs):
        send_sems, recv_sems = refs[nb], refs[nb + 1]
        for i, (src, _, dev, land) in enumerate(plan(refs[:nb])):
            if which is not None and i not in which:
                continue
            cp = _remote(src, land, send_sems.at[i], recv_sems.at[i], dev)
            cp.wait_send()
            cp.wait_recv()

    res = pl.pallas_call(
        body, name=name, in_specs=[_HBM] * nb + [_SEM, _SEM] + [_ANY] * na, out_specs=[_HBM] * nb,
        out_shape=[pltpu.HBM(b.shape, b.dtype) for b in bufs],
        input_output_aliases={i: i for i in range(nb)},
        compiler_params=pltpu.CompilerParams(has_side_effects=_EFFECT),
    )(*bufs, *sems, *after)
    return list(res)


def _plan_gather_ici(fulls, which=(0, 1, 2)):
    x, y, c, chips = _place()
    copies = []
    for f in fulls:
        half = pl.ds(c * (f.shape[1] // 2), f.shape[1] // 2)
        own = f.at[2 * x + y, half]
        for chip in [chips[k] for k in which]:
            copies.append((own, own, (*chip, c), f.at[2 * chip[0] + chip[1], half]))
    return copies


def _plan_gather_d2d(fulls, which=(0, 1, 2)):
    x, y, c, chips = _place()
    copies = []
    for f in fulls:
        r2 = f.shape[1] // 2
        for chip in [chips[k] for k in which]:
            blk = 2 * chip[0] + chip[1]
            landed = f.at[blk, pl.ds(c * r2, r2)]
            copies.append((landed, landed, (x, y, 1 - c), f.at[blk, pl.ds((1 - c) * r2, r2)]))
    return copies


def _plan_swap_halves(refs):
    x, y, c, _ = _place()
    n = len(refs) // 2
    copies = []
    for g, land in zip(refs[:n], refs[n:]):
        r2 = g.shape[1] // 2
        copies.append((g.at[:, pl.ds((1 - c) * r2, r2), :], land, (x, y, 1 - c), land))
    return copies


def _plan_scatter_chips(refs):
    x, y, c, chips = _place()
    n = len(refs) // 2
    copies = []
    for h, land in zip(refs[:n], refs[n:]):
        for k, chip in enumerate(chips):
            copies.append((h.at[2 * chip[0] + chip[1]], land.at[k], (*chip, c), land.at[k]))
    return copies


def _plan_join_halves(totals):
    x, y, c, _ = _place()
    copies = []
    for t in totals:
        r2 = t.shape[0] // 2
        mine = t.at[pl.ds(c * r2, r2)]
        copies.append((mine, mine, (x, y, 1 - c), t.at[pl.ds((1 - c) * r2, r2)]))
    return copies


def _add_sibling_half(g, got, c_arr, *, name, tm):
    _, rows, cols = g.shape
    r2 = rows // 2
    nb = r2 // tm

    def body(c_ref, g_ref, r_ref, o_ref):
        o_ref[...] = (g_ref[...].astype(F32) + r_ref[...].astype(F32)).astype(o_ref.dtype)

    return pl.pallas_call(
        body, name=name,
        grid_spec=pltpu.PrefetchScalarGridSpec(
            num_scalar_prefetch=1, grid=(N_CHIPS, nb),
            in_specs=[pl.BlockSpec((None, tm, cols), lambda b, i, c: (b, c[0] * nb + i, 0)),
                      pl.BlockSpec((None, tm, cols), lambda b, i, c: (b, i, 0))],
            out_specs=pl.BlockSpec((None, tm, cols), lambda b, i, c: (b, i, 0))),
        out_shape=jax.ShapeDtypeStruct((N_CHIPS, r2, cols), _WIRE),
        compiler_params=_params(("arbitrary", "arbitrary")),
    )(c_arr, g, got)


def _add_chips(h, got, place_arr, *, name, tm):
    _, r2, cols = h.shape
    nb = r2 // tm

    def body(p_ref, h_ref, r_ref, o_ref):
        o_ref[...] = ((h_ref[...].astype(F32) + r_ref[0].astype(F32)) + r_ref[1].astype(F32)) + r_ref[2].astype(F32)

    return pl.pallas_call(
        body, name=name,
        grid_spec=pltpu.PrefetchScalarGridSpec(
            num_scalar_prefetch=1, grid=(nb,),
            in_specs=[pl.BlockSpec((None, tm, cols), lambda i, p: (p[0], i, 0)),
                      pl.BlockSpec((3, tm, cols), lambda i, p: (0, i, 0))],
            out_specs=pl.BlockSpec((tm, cols), lambda i, p: (p[1] * nb + i, 0))),
        out_shape=jax.ShapeDtypeStruct((2 * r2, cols), F32),
        compiler_params=_params(("arbitrary",)),
    )(place_arr, h, got)


class _ReduceScatter:
    def __init__(self, tag, names, grads):
        self.tag, self.names, self.n = tag, names, len(names)
        core = lax.axis_index("c").astype(jnp.int32)
        chip = (2 * lax.axis_index("x") + lax.axis_index("y")).astype(jnp.int32)
        self.c_arr, self.place_arr = core.reshape(1), jnp.stack([chip, core])
        self.bufs = list(grads)

    def _start(self, step, bufs, plan, count, after):
        self.plan = plan
        self.step = f"grad_{step}_{self.tag}"
        self.sems, self.bufs, token = _copies_start(self.step + "_start", bufs, plan, count, after)
        return [token]

    def _wait(self, after):
        self.bufs = _copies_wait(self.step + "_wait", self.bufs, self.sems, self.plan, after)
        return self.bufs

    def start_swap(self, after=()):
        lands = [lax.empty((N_CHIPS, g.shape[1] // 2, g.shape[2]), g.dtype) for g in self.bufs]
        return self._start("swap", self.bufs + lands, _plan_swap_halves, self.n, after)

    def start_scatter(self, after):
        bufs = self._wait(after)
        pair = [_add_sibling_half(g, r, self.c_arr, name=f"grad_add_sibling_{nm}", tm=min(256, g.shape[1] // 2))
                for nm, g, r in zip(self.names, bufs[:self.n], bufs[self.n:])]
        lands = [lax.empty((3,) + h.shape[1:], h.dtype) for h in pair]
        return self._start("scatter", pair + lands, _plan_scatter_chips, 3 * self.n, ())

    def start_join(self, after):
        bufs = self._wait(after)
        total = [_add_chips(h, r, self.place_arr, name=f"grad_add_chips_{nm}", tm=min(256, h.shape[1]))
                 for nm, h, r in zip(self.names, bufs[:self.n], bufs[self.n:])]
        return self._start("join", total, _plan_join_halves, self.n, ())

    def finish(self, after):
        return dict(zip(self.names, self._wait(after)))


def _all_gather_small(v):
    m_per, n = v.shape

    def body(x_ref, out_ref, send_sems, recv_sems, local_sem):
        x, y, c, chips = _place()
        me, sibling = (x, y, c), (x, y, 1 - c)

        def rows(px, py, pc):
            return out_ref.at[4 * px + 2 * py + pc]

        def copy(k, block, to, src=None):
            return _remote(rows(*block) if src is None else src, rows(*block), send_sems.at[k], recv_sems.at[k], to)

        mine = pltpu.make_async_copy(x_ref, rows(*me), local_sem)
        mine.start()
        first = [copy(0, me, sibling, src=x_ref)]
        first += [copy(1 + j, me, (*chip, c), src=x_ref) for j, chip in enumerate(chips)]
        for cp in first:
            cp.start()
        passed = [copy(4 + j, (*chip, c), sibling) for j, chip in enumerate(chips)]
        for j, chip in enumerate(chips):
            copy(1 + j, (*chip, c), me).wait_recv()
            passed[j].start()
        copy(0, sibling, me).wait_recv()
        for j, chip in enumerate(chips):
            copy(4 + j, (*chip, 1 - c), me).wait_recv()
        for cp in first + passed:
            cp.wait_send()
        mine.wait()

    return pl.pallas_call(
        body, name="gather_small_grads",
        out_shape=jax.ShapeDtypeStruct((8, m_per, n), v.dtype),
        in_specs=[pl.BlockSpec(memory_space=pltpu.VMEM)], out_specs=pl.BlockSpec(memory_space=pltpu.VMEM),
        scratch_shapes=[pltpu.SemaphoreType.DMA((7,)), pltpu.SemaphoreType.DMA((7,)), pltpu.SemaphoreType.DMA],
        compiler_params=pltpu.CompilerParams(vmem_limit_bytes=VMEM_LIMIT),
    )(v)


def _sum8(v, *, name):
    _, m, n = v.shape

    def body(v_ref, o_ref):
        acc = v_ref[0]
        for d in range(1, 8):
            acc = acc + v_ref[d]
        o_ref[...] = acc

    return pl.pallas_call(body, name=name, out_shape=jax.ShapeDtypeStruct((m, n), F32),
                          compiler_params=pltpu.CompilerParams(vmem_limit_bytes=VMEM_LIMIT))(v)


def _local_step(x, target, norm_w, q_norm_w, k_norm_w, sinks, a_re, a_im, log_dt, b_re, b_im, c_re, c_im, d_skip,
                b_glu, io):
    seq = x.shape[0]
    qw2 = jnp.tile(q_norm_w.reshape(1, HEAD_DIM), (1, HEADS_PER_TILE))
    kw2 = jnp.tile(k_norm_w.reshape(1, HEAD_DIM), (1, HEADS_PER_TILE))
    nw, bg = norm_w.reshape(1, D_MODEL), b_glu.reshape(1, D_MODEL)
    dsk = d_skip.reshape(1, SSM_W)

    h, rstd = _rms_fwd(x, nw, deps=io.begin())
    proj, w_in4 = io.projection(h)
    attn, lse = _attn2_fwd(proj, qw2, kw2, sinks, deps=io.after_proj(proj))

    def gate_a(at, ag):
        return (at * (ag * _sigmoid(ag)),)

    (ya_in,) = _ew(gate_a, [(attn, "mat", 0), (proj, "mat", OFF_AGATE)], [(ATTN_W, _MXU)], rows=seq, ncol=2,
                   name="ew_attn_gate")
    w_ap4 = io.weight("w_attn_proj", ya_in)
    w_glu4, w_sp4, w_out = io.weight("w_glu", ya_in), io.weight("w_ssm_proj", ya_in), io.weight("w_out", ya_in)
    y_a = _mm(ya_in, w_ap4, mode="nn", name="mm_attn_proj", tm=2048, tn=512, tk=ATTN_W, b_blocked=True)

    flat_a = (a_re.reshape(1, N_STATES), a_im.reshape(1, N_STATES), jnp.repeat(log_dt, STATE).reshape(1, N_STATES))
    coef = _ssm_params_fwd(*flat_a)
    bre_blk, bim_blk = _block_diag_b(b_re).astype(_MXU), _block_diag_b(b_im).astype(_MXU)
    cre_blk, cim_blk = _block_diag_c(c_re).astype(_MXU), _block_diag_c(c_im).astype(_MXU)
    u_scan = _to_scan_order(proj[:, OFF_U * CW:OFF_U * CW + SSM_W])
    y_scan, s_re, s_im, i_re, i_im = _ssm_fwd(u_scan, bre_blk, bim_blk, cre_blk, cim_blk, dsk, coef)
    y_ssm = _from_scan_order(y_scan)

    (yg,) = _ew(lambda yv: (jax.nn.gelu(yv),), [(y_ssm, "mat", 0)], [(SSM_W, _MXU)], rows=seq, ncol=2, name="ew_gelu")
    glu = _mm(yg, w_glu4, mode="nn", name="mm_glu", tm=2048, tn=512, tk=SSM_W, b_blocked=True)

    def gate_s(ga, gb, ba, bb, z):
        return ((ga + ba) * _sigmoid(gb + bb) * (z * _sigmoid(z)),)

    (ys_in,) = _ew(gate_s, [(glu, "mat", 0), (glu, "mat", 2), (bg, "row", 0), (bg, "row", 2), (proj, "mat", OFF_Z)],
                   [(SSM_W, _MXU)], rows=seq, ncol=2, name="ew_ssm_gate")
    y_s = _mm(ys_in, w_sp4, mode="nn", name="mm_ssm_proj", tm=2048, tn=512, tk=SSM_W, b_blocked=True)

    def merge(ga, gs, ya, ys):
        return (_sigmoid(ga) * ya + _sigmoid(gs) * ys,)

    (merged,) = _ew(merge, [(proj, "mat", OFF_GA), (proj, "mat", OFF_GS), (y_a, "mat", 0), (y_s, "mat", 0)],
                    [(D_MODEL, _MXU)], rows=seq, ncol=4, name="ew_merge")
    mo = _mm(merged, w_out, mode="nn", name="mm_out", tm=512, tn=D_MODEL, tk=D_MODEL)

    def loss_head(xv, mv, tv):
        err = (xv + mv) - tv
        dout = err * (1.0 / D_MODEL)
        return dout, dout, _colsum(err * err)

    dout, dout_b, sq = _ew(loss_head, [(x, "mat", 0), (mo, "mat", 0), (target, "mat", 0)],
                           [(D_MODEL, F32), (D_MODEL, _MXU)], rows=seq, ncol=4, n_acc=1, name="ew_loss")
    loss = 0.5 * jnp.sum(sq) / D_MODEL

    d_merged = _mm(dout_b, w_out, mode="nt", name="mm_d_merged", tm=512, tn=D_MODEL, tk=D_MODEL)
    g_w_out = _mm(merged, dout_b, mode="tn", name="mm_g_w_out", tm=1024, tn=D_MODEL, tk=1024, out_dtype=_WIRE)

    def merge_bwd(dm, ga, gs, ya, ys):
        sa, ss = _sigmoid(ga), _sigmoid(gs)
        return sa * dm, ss * dm, dm * ya * sa * (1.0 - sa), dm * ys * ss * (1.0 - ss)

    d_ya, d_ys, d_ga, d_gs = _ew(
        merge_bwd, [(d_merged, "mat", 0), (proj, "mat", OFF_GA), (proj, "mat", OFF_GS), (y_a, "mat", 0), (y_s, "mat", 0)],
        [(D_MODEL, _MXU)] * 4, rows=seq, ncol=4, name="ew_merge_bwd")

    d_ya_in = _mm(d_ya, w_ap4, mode="nt", name="mm_d_attn_gate", tm=2048, tn=ATTN_W, tk=512, b_blocked=True)
    g_w_ap = _mm(ya_in, d_ya, mode="tn", name="mm_g_w_attn_proj", tm=ATTN_W, tn=512, tk=2048, out_dtype=_WIRE,
                 out_blocked=True)

    d_ys_in = _mm(d_ys, w_sp4, mode="nt", name="mm_d_ssm_gate", tm=2048, tn=SSM_W, tk=512, b_blocked=True)
    g_w_sp = _mm(ys_in, d_ys, mode="tn", name="mm_g_w_ssm_proj", tm=SSM_W, tn=512, tk=2048, out_dtype=_WIRE,
                 out_blocked=True)

    def gate_s_bwd(dv, ga, gb, ba, bb, z):
        a, sb = ga + ba, _sigmoid(gb + bb)
        f, df = _silu_and_grad(z)
        dga = dv * sb * f
        dgb = dv * a * f * sb * (1.0 - sb)
        return dga, dgb, dv * a * sb * df, _colsum(dga), _colsum(dgb)

    d_glu_a, d_glu_b, d_z, g_bga, g_bgb = _ew(
        gate_s_bwd, [(d_ys_in, "mat", 0), (glu, "mat", 0), (glu, "mat", 2), (bg, "row", 0), (bg, "row", 2),
                     (proj, "mat", OFF_Z)],
        [(SSM_W, _MXU)] * 3, rows=seq, ncol=2, n_acc=2, name="ew_ssm_gate_bwd")
    d_glu = jnp.concatenate([d_glu_a, d_glu_b], axis=1)
    d_yg = _mm(d_glu, w_glu4, mode="nt", name="mm_d_gelu", tm=2048, tn=SSM_W, tk=512, b_blocked=True)
    g_w_glu = _mm(yg, d_glu, mode="tn", name="mm_g_w_glu", tm=SSM_W, tn=512, tk=2048, out_dtype=_WIRE, out_blocked=True)
    dep = io.later_grads(dict(w_attn_proj=g_w_ap, w_glu=g_w_glu, w_ssm_proj=g_w_sp,
                              w_out=g_w_out.reshape(N_CHIPS, D_MODEL // N_CHIPS, D_MODEL)))

    def gate_a_bwd(dv, at, ag):
        f, df = _silu_and_grad(ag)
        return dv * f, dv * at * df

    d_attn, d_agate = _ew(gate_a_bwd, [(d_ya_in, "mat", 0), (attn, "mat", 0), (proj, "mat", OFF_AGATE)],
                          [(ATTN_W, F32), (ATTN_W, _MXU)], rows=seq, ncol=2, name="ew_attn_gate_bwd", deps=dep)

    def gelu_bwd(dv, yv):
        return (jax.vjp(jax.nn.gelu, yv)[1](dv)[0],)

    (d_yssm,) = _ew(gelu_bwd, [(d_yg, "mat", 0), (y_ssm, "mat", 0)], [(SSM_W, F32)], rows=seq, ncol=2, name="ew_gelu_bwd",
                    deps=dep)
    dep = io.before_attention_backward([d_attn, d_yssm])
    d_qkv, g_qw2, g_kw2, g_sk = _attn2_bwd(proj, qw2, kw2, sinks, lse, d_attn, deps=dep)
    (du_scan, g_bre, g_bim, g_cre, g_cim, g_dsk, g_abr, g_abi, g_cfr, g_cfi) = _ssm_bwd(
        _to_scan_order(d_yssm), u_scan, s_re, s_im, i_re, i_im, bre_blk, bim_blk, cre_blk, cim_blk, dsk, coef)
    g_are, g_aim, g_ldt = _ssm_params_bwd(*flat_a, g_abr, g_abi, g_cfr, g_cfi)
    g_are, g_aim = g_are.reshape(N_GROUPS, STATE), g_aim.reshape(N_GROUPS, STATE)
    g_ldt = g_ldt.reshape(N_GROUPS, STATE).sum(axis=1)
    d_u = _from_scan_order(du_scan)

    d_proj = jnp.concatenate([d_qkv, d_agate, d_u.astype(_MXU), d_z, d_ga, d_gs], axis=1)
    dep = io.before_input_projection_grad([d_proj]) + io.small_grads(dict(
        q_norm_w=g_qw2[0, :HEAD_DIM] + g_qw2[0, HEAD_DIM:], k_norm_w=g_kw2[0, :HEAD_DIM] + g_kw2[0, HEAD_DIM:],
        sinks=g_sk.reshape(N_Q_HEADS), A_re=g_are, A_im=g_aim, log_dt=g_ldt,
        B_re=_diag_of_b(g_bre), B_im=_diag_of_b(g_bim), C_re=_diag_of_c(g_cre), C_im=_diag_of_c(g_cim),
        D_skip=g_dsk.reshape(N_GROUPS, GROUP), b_glu=jnp.concatenate([g_bga, g_bgb], axis=1).reshape(D_MODEL)))
    g_w_in = _mm(h, d_proj, mode="tn", name="mm_g_w_in", tm=1024, tn=IN_W // 4, tk=1024, out_dtype=_WIRE,
                 out_blocked=True, deps=dep)
    dep = io.input_projection_grad(g_w_in)
    d_h = _mm(d_proj, w_in4, mode="nt", name="mm_d_h", tm=512, tn=D_MODEL, tk=IN_W // 4, b_blocked=True, deps=dep)
    grad_x, g_nw = _rms_bwd(d_h, x, rstd, nw, dout)
    return loss, grad_x, g_nw.reshape(D_MODEL)


_SMALL = ["norm_w", "q_norm_w", "k_norm_w", "sinks", "A_re", "A_im", "log_dt", "B_re", "B_im", "C_re", "C_im",
          "D_skip", "b_glu"]
_BIG = ["w_in", "w_attn_proj", "w_glu", "w_ssm_proj", "w_out"]
_LATER = _BIG[1:]
_RELATIONS = ("flip_x", "flip_y", "flip_xy")
_ORDER = ["norm_w", "w_in", "q_norm_w", "k_norm_w", "sinks", "w_attn_proj", "A_re", "A_im", "log_dt", "B_re", "B_im",
          "C_re", "C_im", "D_skip", "w_glu", "b_glu", "w_ssm_proj", "w_out"]
_PACK_W = 1024


def _packed_rows(size):
    unit = SUBLANES * _PACK_W
    return -(-size // unit) * SUBLANES


def _pack_small(d, names):
    parts = []
    for n in names:
        flat = d[n].reshape(-1).astype(F32)
        rows = _packed_rows(flat.shape[0])
        parts.append(jnp.pad(flat, (0, rows * _PACK_W - flat.shape[0])).reshape(rows, _PACK_W))
    return jnp.concatenate(parts, axis=0)


def _unpack_small(packed, like, names):
    out, pos = {}, 0
    for n in names:
        rows = _packed_rows(like[n].size)
        out[n] = packed[pos:pos + rows].reshape(-1)[:like[n].size].reshape(like[n].shape)
        pos += rows
    return out


def _place_block(v, index_arr, *, name):
    rows, cols = v.shape

    def body(i_ref, v_ref, o_ref):
        o_ref[...] = v_ref[...]

    return pl.pallas_call(
        body, name=name,
        grid_spec=pltpu.PrefetchScalarGridSpec(
            num_scalar_prefetch=1, grid=(1,),
            in_specs=[pl.BlockSpec((rows, cols), lambda i, d: (0, 0))],
            out_specs=pl.BlockSpec((None, rows, cols), lambda i, d: (d[0], 0, 0))),
        out_shape=jax.ShapeDtypeStruct((8, rows, cols), v.dtype),
        compiler_params=_params(("arbitrary",)),
    )(index_arr, v)


def _plan_all_to_all(refs):
    (land,) = refs
    x, y, c, _ = _place()
    own = land.at[4 * x + 2 * y + c]
    copies = []
    for fx, fy, fc in [(0, 0, 1), (0, 1, 0), (0, 1, 1), (1, 0, 0), (1, 0, 1), (1, 1, 0), (1, 1, 1)]:
        px, py, pc = (1 - x) if fx else x, (1 - y) if fy else y, (1 - c) if fc else c
        copies.append((own, own, (px, py, pc), land.at[4 * px + 2 * py + pc]))
    return copies


def _as2d(a):
    return a.reshape(1, -1) if a.ndim == 1 else a


def _adamw_whole(w, g, m, v, *, name):
    shape = w.shape
    w, g, m, v = _as2d(w), _as2d(g), _as2d(m), _as2d(v)

    def body(w_ref, g_ref, m_ref, v_ref, d_ref, nm_ref, nv_ref):
        d_ref[...], nm_ref[...], nv_ref[...] = _adamw_math(w_ref[...], g_ref[...], m_ref[...], v_ref[...])

    outs = pl.pallas_call(body, name=name, out_shape=[jax.ShapeDtypeStruct(w.shape, F32)] * 3)(w, g, m, v)
    return [o.reshape(shape) for o in outs]


class _Exchanges:
    def __init__(self, w, m, v):
        self.w, self.m, self.v = w, m, v
        self.grads, self.delta, self.new_m, self.new_v = {}, {}, {}, {}

    def _adamw(self, names, deps):
        for n in names:
            self.delta[n], self.new_m[n], self.new_v[n] = _adamw(
                self.w[n], self.grads[n], self.m[n], self.v[n], name=f"adamw_{n}", tm=128, deps=deps)

    def begin(self):
        chip = (2 * lax.axis_index("x") + lax.axis_index("y")).astype(jnp.int32).reshape(1)
        full = {n: _place_shard(self.w[n], chip, name=f"place_{n}") for n in _BIG}
        self.w_in_sems, self.w_in_buf, token = _copies_start("gather_ici_w_in_start", [full["w_in"]], _plan_gather_ici, 3)
        self.rest = _copies_start("gather_ici_rest_start", [full[n] for n in _LATER], _plan_gather_ici,
                                  3 * len(_LATER), after=[token])
        return [self.rest[2]]

    def projection(self, h):
        x, y = lax.axis_index("x"), lax.axis_index("y")
        blks = [jnp.asarray(b, jnp.int32).reshape(1)
                for b in (2 * x + y, 2 * (1 - x) + y, 2 * x + (1 - y), 2 * (1 - x) + (1 - y))]
        def landed(bufs, k, after):
            bufs = _copies_wait(f"gather_ici_w_in_{_RELATIONS[k]}_wait", bufs, self.w_in_sems, _plan_gather_ici, after,
                                which=(k,))
            plan = functools.partial(_plan_gather_d2d, which=(k,))
            sems, bufs, token = _copies_start(f"gather_d2d_w_in_{_RELATIONS[k]}_start", bufs, plan, 1)
            return bufs, (sems, plan, token)

        def handed(bufs, k, pending, after):
            sems, plan, _ = pending
            return _copies_wait(f"gather_d2d_w_in_{_RELATIONS[k]}_wait", bufs, sems, plan, after)

        bufs = self.w_in_buf
        proj = _mm_chip_block(h, bufs[0], blks[0], None, name="mm_proj_own")
        bufs, d2d_x = landed(bufs, 0, [proj])
        bufs, d2d_y = landed(bufs, 1, [d2d_x[2]])
        bufs = handed(bufs, 0, d2d_x, [d2d_y[2]])
        proj = _mm_chip_block(h, bufs[0], blks[1], proj, name="mm_proj_flip_x")
        bufs = handed(bufs, 1, d2d_y, [proj])
        proj = _mm_chip_block(h, bufs[0], blks[2], proj, name="mm_proj_flip_y")
        bufs, d2d_xy = landed(bufs, 2, [proj])
        bufs = handed(bufs, 2, d2d_xy, [d2d_xy[2]])
        proj = _mm_chip_block(h, bufs[0], blks[3], proj, name="mm_proj_flip_xy")
        return proj, bufs[0]

    def weight(self, name, after):
        if self.rest is not None:
            sems, bufs = self.rest
            later = dict(zip(_LATER, _copies_wait("gather_d2d_rest_wait", bufs, sems, _plan_gather_d2d, [after])))
            later["w_out"] = later["w_out"].reshape(D_MODEL, D_MODEL)
            self.later, self.rest = later, None
        return self.later[name]

    def after_proj(self, proj):
        sems, bufs, _ = self.rest
        bufs = _copies_wait("gather_ici_rest_wait", bufs, sems, _plan_gather_ici, [proj])
        sems, bufs, token = _copies_start("gather_d2d_rest_start", bufs, _plan_gather_d2d, 3 * len(_LATER))
        self.rest = (sems, bufs)
        return [token]

    def later_grads(self, grads):
        self.rs_later = _ReduceScatter("later", _LATER, [grads[n] for n in _LATER])
        return self.rs_later.start_swap()

    def before_attention_backward(self, after):
        return self.rs_later.start_scatter(after)

    def before_input_projection_grad(self, after):
        return self.rs_later.start_join(after)

    def input_projection_grad(self, g_w_in):
        self.grads.update(self.rs_later.finish([g_w_in]))
        self.rs_in = _ReduceScatter("w_in", ["w_in"], [g_w_in])
        self._adamw(_LATER, self.rs_in.start_swap())
        return self.rs_in.start_scatter([self.delta[n] for n in _LATER])

    def _adamw_small(self, names):
        for n in names:
            self.delta[n], self.new_m[n], self.new_v[n] = _adamw_whole(
                self.w[n], self.grads[n], self.m[n], self.v[n], name=f"adamw_{n}")

    def small_grads(self, grads):
        me = (4 * lax.axis_index("x") + 2 * lax.axis_index("y") + lax.axis_index("c")).astype(jnp.int32).reshape(1)
        land = _place_block(_pack_small(grads, _SMALL[1:]), me, name="place_small_grads")
        self.small = _copies_start("gather_small_start", [land], _plan_all_to_all, 7)
        return [self.small[2]]

    def finish(self, g_norm_w, loss, after):
        sems, bufs, _ = self.small
        (land,) = _copies_wait("gather_small_wait", bufs, sems, _plan_all_to_all, after)
        self.grads.update(_unpack_small(_sum8(land, name="sum_small_grads"), self.w, _SMALL[1:]))
        self._adamw_small(_SMALL[1:])
        rows = _packed_rows(g_norm_w.size)
        late = jnp.concatenate([_pack_small(dict(norm_w=g_norm_w), _SMALL[:1]),
                                jnp.pad(loss.reshape(1, 1), ((0, SUBLANES - 1), (0, _PACK_W - 1)))], axis=0)
        late = _sum8(_all_gather_small(late), name="sum_norm_w_grad_and_loss")
        self.grads.update(_unpack_small(late[:rows], self.w, _SMALL[:1]))
        self._adamw_small(_SMALL[:1])
        self.grads.update(self.rs_in.finish(self.rs_in.start_join([self.delta[_SMALL[0]]])))
        self._adamw(["w_in"], ())
        return late[rows, 0]


def kernel(x, norm_w, w_in, q_norm_w, k_norm_w, sinks, w_attn_proj, A_re, A_im, log_dt, B_re, B_im, C_re, C_im, D_skip, w_glu, b_glu, w_ssm_proj, w_out, loss_target, m_norm_w, m_w_in, m_q_norm_w, m_k_norm_w, m_sinks, m_w_attn_proj, m_A_re, m_A_im, m_log_dt, m_B_re, m_B_im, m_C_re, m_C_im, m_D_skip, m_w_glu, m_b_glu, m_w_ssm_proj, m_w_out, v_norm_w, v_w_in, v_q_norm_w, v_k_norm_w, v_sinks, v_w_attn_proj, v_A_re, v_A_im, v_log_dt, v_B_re, v_B_im, v_C_re, v_C_im, v_D_skip, v_w_glu, v_b_glu, v_w_ssm_proj, v_w_out):
    w = dict(norm_w=norm_w, w_in=w_in, q_norm_w=q_norm_w, k_norm_w=k_norm_w, sinks=sinks, w_attn_proj=w_attn_proj,
             A_re=A_re, A_im=A_im, log_dt=log_dt, B_re=B_re, B_im=B_im, C_re=C_re, C_im=C_im, D_skip=D_skip,
             w_glu=w_glu, b_glu=b_glu, w_ssm_proj=w_ssm_proj, w_out=w_out)
    m = dict(norm_w=m_norm_w, w_in=m_w_in, q_norm_w=m_q_norm_w, k_norm_w=m_k_norm_w, sinks=m_sinks,
             w_attn_proj=m_w_attn_proj, A_re=m_A_re, A_im=m_A_im, log_dt=m_log_dt, B_re=m_B_re, B_im=m_B_im,
             C_re=m_C_re, C_im=m_C_im, D_skip=m_D_skip, w_glu=m_w_glu, b_glu=m_b_glu, w_ssm_proj=m_w_ssm_proj,
             w_out=m_w_out)
    v = dict(norm_w=v_norm_w, w_in=v_w_in, q_norm_w=v_q_norm_w, k_norm_w=v_k_norm_w, sinks=v_sinks,
             w_attn_proj=v_w_attn_proj, A_re=v_A_re, A_im=v_A_im, log_dt=v_log_dt, B_re=v_B_re, B_im=v_B_im,
             C_re=v_C_re, C_im=v_C_im, D_skip=v_D_skip, w_glu=v_w_glu, b_glu=v_b_glu, w_ssm_proj=v_w_ssm_proj,
             w_out=v_w_out)

    io = _Exchanges(w, m, v)
    loss, grad_x, g_norm_w = _local_step(x[0], loss_target[0], norm_w, q_norm_w, k_norm_w, sinks, A_re, A_im, log_dt,
                                         B_re, B_im, C_re, C_im, D_skip, b_glu, io)
    loss = io.finish(g_norm_w, loss, [grad_x])
    grads, delta, new_m, new_v = io.grads, io.delta, io.new_m, io.new_v

    return (loss, grad_x[None], *[grads[n] for n in _ORDER], *[delta[n] for n in _ORDER],
            *[new_m[n] for n in _ORDER], *[new_v[n] for n in _ORDER])
```

```python
import functools
import math

import jax
import jax.numpy as jnp
from jax import lax
from jax.experimental import pallas as pl
from jax.experimental.pallas import tpu as pltpu

F32 = jnp.float32
_MXU = jnp.bfloat16
_WIRE = jnp.bfloat16

LANES = 128
SUBLANES = 8
VMEM_LIMIT = 56 * 1024 * 1024

D_MODEL = 2048
HEAD_DIM = 64
N_Q_HEADS = 16
N_KV_HEADS = 4
Q_PER_KV = 4
ATTN_W = 1024
KV_W = 256
WINDOW = 128
SSM_W = 1024
GROUP = 16
N_GROUPS = 64
STATE = 64
N_STATES = N_GROUPS * STATE
IN_W = 8704
NORM_EPS = 1e-6
N_CHIPS = 4
CW = 512
OFF_AGATE, OFF_U, OFF_Z, OFF_GA, OFF_GS = 3, 5, 7, 9, 13

SSM_T = 256
SSM_L = SSM_T // SUBLANES
SSM_JB = 8
SSM_SB = N_STATES // SSM_JB

ADAM_LR, ADAM_B1, ADAM_B2, ADAM_EPS, ADAM_WD, ADAM_STEP = 0.001, 0.9, 0.999, 1e-08, 0.01, 10

MESH = pl.DeviceIdType.MESH
_ANY = pl.BlockSpec(memory_space=pl.ANY)


def _params(sem=None):
    return pltpu.CompilerParams(dimension_semantics=sem, vmem_limit_bytes=VMEM_LIMIT)


def _mm(a, b, *, mode, name, tm, tn, tk, out_dtype=F32, b_blocked=False, out_blocked=False, deps=()):
    nd = len(deps)
    if mode == "tn":
        K, M = a.shape
    else:
        M, K = a.shape
    if mode == "nn":
        N = b.shape[0] * b.shape[2] if b_blocked else b.shape[1]
    elif mode == "nt":
        N = b.shape[1] if b_blocked else b.shape[0]
    else:
        N = b.shape[1]
    tm, tn, tk = min(tm, M), min(tn, N), min(tk, K)
    nj, ni, nk = N // tn, M // tm, K // tk
    assert nj * tn == N and ni * tm == M and nk * tk == K, (name, M, N, K)
    dims = {"nn": (((1,), (0,)), ((), ())), "nt": (((1,), (1,)), ((), ())), "tn": (((0,), (0,)), ((), ()))}[mode]

    if mode == "tn":
        a_spec = pl.BlockSpec((tk, tm), lambda j, i, k: (k, i))
    else:
        a_spec = pl.BlockSpec((tm, tk), lambda j, i, k: (i, k))
    if mode == "nn":
        if b_blocked:
            assert b.shape[0] == nj and b.shape[2] == tn
            b_spec = pl.BlockSpec((None, tk, tn), lambda j, i, k: (j, k, 0))
        else:
            b_spec = pl.BlockSpec((tk, tn), lambda j, i, k: (k, j))
    elif mode == "nt":
        if b_blocked:
            assert b.shape[0] == nk and b.shape[2] == tk
            b_spec = pl.BlockSpec((None, tn, tk), lambda j, i, k: (k, j, 0))
        else:
            b_spec = pl.BlockSpec((tn, tk), lambda j, i, k: (j, k))
    else:
        b_spec = pl.BlockSpec((tk, tn), lambda j, i, k: (k, j))
    if out_blocked:
        assert nj == N_CHIPS
        o_spec = pl.BlockSpec((None, tm, tn), lambda j, i, k: (j, i, 0))
        o_shape = jax.ShapeDtypeStruct((nj, M, tn), out_dtype)
    else:
        o_spec = pl.BlockSpec((tm, tn), lambda j, i, k: (i, j))
        o_shape = jax.ShapeDtypeStruct((M, N), out_dtype)
    use_acc = nk > 1 and out_dtype != F32

    def body(a_ref, b_ref, *rest):
        o_ref, scratch = rest[nd], rest[nd + 1:]
        part = lax.dot_general(a_ref[...].astype(_MXU), b_ref[...].astype(_MXU), dims,
                               preferred_element_type=F32)
        if nk == 1:
            o_ref[...] = part.astype(o_ref.dtype)
            return
        k = pl.program_id(2)
        acc = scratch[0] if use_acc else o_ref

        @pl.when(k == 0)
        def _():
            acc[...] = part

        @pl.when(k > 0)
        def _():
            acc[...] += part

        if use_acc:
            @pl.when(k == nk - 1)
            def _():
                o_ref[...] = acc[...].astype(o_ref.dtype)

    return pl.pallas_call(
        body, name=name, grid=(nj, ni, nk), in_specs=[a_spec, b_spec] + [_ANY] * nd, out_specs=o_spec,
        out_shape=o_shape, scratch_shapes=[pltpu.VMEM((tm, tn), F32)] if use_acc else [],
        compiler_params=_params(("parallel", "parallel", "arbitrary")),
    )(a, b, *deps)


def _mm_chip_block(a, b4, blk, prev, *, name, tm=512, deps=()):
    M, K = a.shape
    nchip, _, C = b4.shape
    tm = min(tm, M)
    extra = ([] if prev is None else [prev]) + list(deps)

    def body(blk_ref, a_ref, b_ref, *rest):
        rest[-1][...] = jnp.dot(a_ref[...].astype(_MXU), b_ref[...].astype(_MXU), preferred_element_type=F32)

    return pl.pallas_call(
        body, name=name,
        grid_spec=pltpu.PrefetchScalarGridSpec(
            num_scalar_prefetch=1, grid=(M // tm,),
            in_specs=[pl.BlockSpec((tm, K), lambda i, c: (i, 0)), pl.BlockSpec((None, K, C), lambda i, c: (c[0], 0, 0))]
            + [_ANY] * len(extra),
            out_specs=pl.BlockSpec((tm, C), lambda i, c: (i, c[0]))),
        out_shape=jax.ShapeDtypeStruct((M, nchip * C), F32),
        input_output_aliases={} if prev is None else {3: 0},
        compiler_params=_params(("arbitrary",)),
    )(blk, a, b4, *extra)


def _mm_out_loss(merged, w_out, x, target, *, tm=256):
    rows, d = x.shape

    def body(m_ref, w_ref, x_ref, t_ref, d_ref, db_ref, sq_ref):
        mo = jnp.dot(m_ref[...].astype(_MXU), w_ref[...].astype(_MXU), preferred_element_type=F32)
        err = (x_ref[...] + mo) - t_ref[...]
        dout = err * (1.0 / d)
        d_ref[...] = dout
        db_ref[...] = dout.astype(db_ref.dtype)
        part = _colsum(err * err)
        i = pl.program_id(0)

        @pl.when(i == 0)
        def _():
            sq_ref[...] = part

        @pl.when(i > 0)
        def _():
            sq_ref[...] += part

    tile = pl.BlockSpec((tm, d), lambda i: (i, 0))
    return pl.pallas_call(
        body, name="mm_out_loss", grid=(rows // tm,),
        in_specs=[tile, pl.BlockSpec((d, d), lambda i: (0, 0)), tile, tile],
        out_specs=[tile, tile, pl.BlockSpec((1, d), lambda i: (0, 0))],
        out_shape=[jax.ShapeDtypeStruct((rows, d), F32), jax.ShapeDtypeStruct((rows, d), _MXU),
                   jax.ShapeDtypeStruct((1, d), F32)],
        compiler_params=_params(("arbitrary",)),
    )(merged, w_out, x, target)


def _mm_merge_bwd(dout_b, w_out, proj, y_a, y_s, *, tm=256):
    rows, d = y_a.shape
    ncol = d // CW

    def body(do_ref, w_ref, *refs):
        ga_refs, gs_refs = refs[:ncol], refs[ncol:2 * ncol]
        ya_ref, ys_ref, dya_ref, dys_ref, dga_ref, dgs_ref = refs[2 * ncol:]
        dm = lax.dot_general(do_ref[...].astype(_MXU), w_ref[...].astype(_MXU), _NT, preferred_element_type=F32)
        for j in range(ncol):
            cols = slice(j * CW, (j + 1) * CW)
            dmj = dm[:, cols]
            sa, ss = _sigmoid(ga_refs[j][...]), _sigmoid(gs_refs[j][...])
            dya_ref[:, cols] = (sa * dmj).astype(dya_ref.dtype)
            dys_ref[:, cols] = (ss * dmj).astype(dys_ref.dtype)
            dga_ref[:, cols] = (dmj * ya_ref[:, cols] * sa * (1.0 - sa)).astype(dga_ref.dtype)
            dgs_ref[:, cols] = (dmj * ys_ref[:, cols] * ss * (1.0 - ss)).astype(dgs_ref.dtype)

    tile = pl.BlockSpec((tm, d), lambda i: (i, 0))
    gate = [pl.BlockSpec((tm, CW), lambda i, c=off + j: (i, c)) for off in (OFF_GA, OFF_GS) for j in range(ncol)]
    return pl.pallas_call(
        body, name="mm_merge_bwd", grid=(rows // tm,),
        in_specs=[tile, pl.BlockSpec((d, d), lambda i: (0, 0))] + gate + [tile, tile],
        out_specs=[tile] * 4, out_shape=[jax.ShapeDtypeStruct((rows, d), _MXU)] * 4,
        compiler_params=_params(("arbitrary",)),
    )(dout_b, w_out, *([proj] * (2 * ncol)), y_a, y_s)


def _ew(fn, ins, outs, *, rows, ncol, name, n_acc=0, tm=512, deps=()):
    n_in, n_out, nd = len(ins), len(outs), len(deps)
    tm = min(tm, rows)
    in_specs = []
    for _, kind, col0 in ins:
        if kind == "mat":
            in_specs.append(pl.BlockSpec((tm, CW), lambda j, i, c0=col0: (i, c0 + j)))
        else:
            in_specs.append(pl.BlockSpec((1, CW), lambda j, i, c0=col0: (0, c0 + j)))
    out_specs = [pl.BlockSpec((tm, CW), lambda j, i: (i, j)) for _ in outs]
    out_shape = [jax.ShapeDtypeStruct((rows, w), dt) for w, dt in outs]
    for _ in range(n_acc):
        out_specs.append(pl.BlockSpec((1, CW), lambda j, i: (0, j)))
        out_shape.append(jax.ShapeDtypeStruct((1, ncol * CW), F32))

    def body(*refs):
        vals = fn(*[r[...] for r in refs[:n_in]])
        refs = refs[n_in + nd:]
        for r, v in zip(refs[:n_out], vals[:n_out]):
            r[...] = v.astype(r.dtype)
        i = pl.program_id(1)
        for r, v in zip(refs[n_out:], vals[n_out:]):
            @pl.when(i == 0)
            def _(r=r, v=v):
                r[...] = v

            @pl.when(i > 0)
            def _(r=r, v=v):
                r[...] += v

    res = pl.pallas_call(
        body, name=name, grid=(ncol, rows // tm), in_specs=in_specs + [_ANY] * nd, out_specs=out_specs,
        out_shape=out_shape, compiler_params=_params(("parallel", "arbitrary")),
    )(*[a for a, _, _ in ins], *deps)
    return res


def _colsum(v):
    return jnp.sum(v, axis=0, keepdims=True)


def _sigmoid(v):
    return jax.nn.sigmoid(v)


def _silu_and_grad(v):
    s = _sigmoid(v)
    return v * s, s * (1.0 + v * (1.0 - s))


def _rms_fwd(x, w, *, tm=512, deps=()):
    rows, d = x.shape
    nd = len(deps)

    def body(x_ref, w_ref, *rest):
        h_ref, r_ref = rest[nd:]
        xv = x_ref[...]
        r = lax.rsqrt(jnp.mean(xv * xv, axis=-1, keepdims=True) + NORM_EPS)
        h_ref[...] = (xv * r * w_ref[...]).astype(h_ref.dtype)
        r_ref[...] = r

    return pl.pallas_call(
        body, name="rms_fwd", grid=(rows // tm,),
        in_specs=[pl.BlockSpec((tm, d), lambda i: (i, 0)), pl.BlockSpec((1, d), lambda i: (0, 0))] + [_ANY] * nd,
        out_specs=[pl.BlockSpec((tm, d), lambda i: (i, 0)), pl.BlockSpec((tm, 1), lambda i: (i, 0))],
        out_shape=[jax.ShapeDtypeStruct((rows, d), _MXU), jax.ShapeDtypeStruct((rows, 1), F32)],
        compiler_params=_params(("arbitrary",)),
    )(x, w, *deps)


def _rms_bwd(dh, x, rstd, w, dout, *, tm=256):
    rows, d = x.shape

    def body(dh_ref, x_ref, r_ref, w_ref, do_ref, gx_ref, gw_ref):
        dhv, xv, r, wv = dh_ref[...], x_ref[...], r_ref[...], w_ref[...]
        xr = xv * r
        t = jnp.mean(dhv * wv * xr, axis=-1, keepdims=True)
        gx_ref[...] = do_ref[...] + r * (wv * dhv - xr * t)
        part = _colsum(dhv * xr)
        i = pl.program_id(0)

        @pl.when(i == 0)
        def _():
            gw_ref[...] = part

        @pl.when(i > 0)
        def _():
            gw_ref[...] += part

    return pl.pallas_call(
        body, name="rms_bwd", grid=(rows // tm,),
        in_specs=[pl.BlockSpec((tm, d), lambda i: (i, 0)), pl.BlockSpec((tm, d), lambda i: (i, 0)),
                  pl.BlockSpec((tm, 1), lambda i: (i, 0)), pl.BlockSpec((1, d), lambda i: (0, 0)),
                  pl.BlockSpec((tm, d), lambda i: (i, 0))],
        out_specs=[pl.BlockSpec((tm, d), lambda i: (i, 0)), pl.BlockSpec((1, d), lambda i: (0, 0))],
        out_shape=[jax.ShapeDtypeStruct((rows, d), F32), jax.ShapeDtypeStruct((1, d), F32)],
        compiler_params=_params(("arbitrary",)),
    )(dh, x, rstd, w, dout)


_NT = (((1,), (1,)), ((), ()))
_TN = (((0,), (0,)), ((), ()))


QKV_W = ATTN_W + 2 * KV_W
HEADS_PER_TILE = LANES // HEAD_DIM


def _low_half(rows):
    return lax.broadcasted_iota(jnp.int32, (rows, LANES), 1) < HEAD_DIM


def _pair_mean(t, low):
    m_lo = jnp.sum(jnp.where(low, t, 0.0), axis=-1, keepdims=True)
    m_hi = jnp.sum(jnp.where(low, 0.0, t), axis=-1, keepdims=True)
    return jnp.where(low, m_lo, m_hi) * (1.0 / HEAD_DIM)


def _pair_rstd(t, low):
    return lax.rsqrt(_pair_mean(t * t, low) + NORM_EPS)


def _dup_half(t, hi, low):
    swapped = pltpu.roll(t, HEAD_DIM, 1)
    return jnp.where(low, swapped, t) if hi else jnp.where(low, t, swapped)


def _fold_halves(t):
    return t + pltpu.roll(t, HEAD_DIM, 1)


def _split_heads(t, low):
    return [jnp.where(low, t, 0.0), jnp.where(low, 0.0, t)]


def _stacked_band_mask(n):
    rows = Q_PER_KV * WINDOW
    qi = lax.broadcasted_iota(jnp.int32, (rows, 2 * WINDOW), 0) % WINDOW + WINDOW
    kj = lax.broadcasted_iota(jnp.int32, (rows, 2 * WINDOW), 1)
    diff = qi - kj
    first_key = jnp.where(n > 0, 0, WINDOW)
    return (diff >= 0) & (diff < WINDOW) & (kj >= first_key)


def _stacked_sinks(sink_ref, g):
    blk = lax.broadcasted_iota(jnp.int32, (Q_PER_KV * WINDOW, 1), 0) // WINDOW
    col = jnp.full((Q_PER_KV * WINDOW, 1), sink_ref[Q_PER_KV * g], F32)
    for r in range(1, Q_PER_KV):
        col = jnp.where(blk == r, sink_ref[Q_PER_KV * g + r], col)
    return col


def _attn_in_specs(nblk, rev):
    def cur(n):
        return (nblk - 1 - n) if rev else n

    q_spec = pl.BlockSpec((WINDOW, ATTN_W), lambda n: (cur(n), 0))
    kvc_spec = pl.BlockSpec((WINDOW, 2 * KV_W), lambda n: (cur(n), ATTN_W // (2 * KV_W)))
    kvp_spec = pl.BlockSpec((WINDOW, 2 * KV_W), lambda n: (jnp.maximum(cur(n) - 1, 0), ATTN_W // (2 * KV_W)))
    w_spec = pl.BlockSpec((1, LANES), lambda n: (0, 0))
    l_spec = pl.BlockSpec((WINDOW, N_Q_HEADS), lambda n: (cur(n), 0))
    return q_spec, kvc_spec, kvp_spec, w_spec, l_spec


def _attn2_fwd(proj, qw2, kw2, sinks, deps=()):
    seq = proj.shape[0]
    nblk = seq // WINDOW
    scale = 1.0 / math.sqrt(HEAD_DIM)
    q_spec, kvc_spec, kvp_spec, w_spec, l_spec = _attn_in_specs(nblk, False)
    nd = len(deps)

    def body(sink_ref, q_ref, kvc_ref, kvp_ref, qw_ref, kw_ref, *rest):
        o_ref, lse_ref = rest[nd:]
        n = pl.program_id(0)
        low, low2 = _low_half(WINDOW), _low_half(2 * WINDOW)
        valid = _stacked_band_mask(n)
        head_lane = lax.broadcasted_iota(jnp.int32, (WINDOW, N_Q_HEADS), 1)
        kv = jnp.concatenate([kvp_ref[...], kvc_ref[...]], axis=0)
        qwv, kwv = qw_ref[...], kw_ref[...]
        lse_blk = jnp.zeros((WINDOW, N_Q_HEADS), F32)
        for t in range(N_KV_HEADS // HEADS_PER_TILE):
            kt = kv[:, t * LANES:(t + 1) * LANES]
            vt = kv[:, KV_W + t * LANES:KV_W + (t + 1) * LANES]
            kn = kt * _pair_rstd(kt, low2) * kwv
            for hi in range(HEADS_PER_TILE):
                g = HEADS_PER_TILE * t + hi
                kdup = _dup_half(kn, hi, low2).astype(_MXU)
                vdup = _dup_half(vt, hi, low2).astype(_MXU)
                stack = []
                for tq in (2 * g, 2 * g + 1):
                    qt = q_ref[:, tq * LANES:(tq + 1) * LANES]
                    stack += _split_heads(qt * _pair_rstd(qt, low) * qwv, low)
                qs = jnp.concatenate(stack, axis=0).astype(_MXU)
                s = lax.dot_general(qs, kdup, _NT, preferred_element_type=F32) * scale
                s = jnp.where(valid, s, -1e30)
                sink = _stacked_sinks(sink_ref, g)
                m = jnp.maximum(jnp.max(s, axis=-1, keepdims=True), sink)
                e = jnp.exp(s - m)
                z = jnp.sum(e, axis=-1, keepdims=True) + jnp.exp(sink - m)
                o = jnp.dot((e / z).astype(_MXU), vdup, preferred_element_type=F32)
                for i, tq in enumerate((2 * g, 2 * g + 1)):
                    o_ref[:, tq * LANES:(tq + 1) * LANES] = jnp.where(
                        low, o[2 * i * WINDOW:(2 * i + 1) * WINDOW], o[(2 * i + 1) * WINDOW:(2 * i + 2) * WINDOW])
                lse = m + jnp.log(z)
                for r in range(Q_PER_KV):
                    lse_blk = jnp.where(head_lane == Q_PER_KV * g + r, lse[r * WINDOW:(r + 1) * WINDOW], lse_blk)
        lse_ref[...] = lse_blk

    return pl.pallas_call(
        body, name="attn_fwd", grid=(nblk,),
        in_specs=[pl.BlockSpec(memory_space=pltpu.SMEM), q_spec, kvc_spec, kvp_spec, w_spec, w_spec] + [_ANY] * nd,
        out_specs=[q_spec, l_spec],
        out_shape=[jax.ShapeDtypeStruct((seq, ATTN_W), F32), jax.ShapeDtypeStruct((seq, N_Q_HEADS), F32)],
        compiler_params=_params(("arbitrary",)),
    )(sinks, proj, proj, proj, qw2, kw2, *deps)


def _attn2_bwd(proj, qw2, kw2, sinks, lse, do, deps=()):
    seq = proj.shape[0]
    nblk = seq // WINDOW
    scale = 1.0 / math.sqrt(HEAD_DIM)
    q_spec, kvc_spec, kvp_spec, w_spec, l_spec = _attn_in_specs(nblk, True)
    s_spec = pl.BlockSpec((1, N_Q_HEADS), lambda n: (0, 0))
    d_spec = pl.BlockSpec((WINDOW, QKV_W), lambda n: (nblk - 1 - n, 0))
    nd = len(deps)

    def body(sink_ref, q_ref, kvc_ref, kvp_ref, qw_ref, kw_ref, lse_ref, do_ref, *rest):
        d_ref, dqw_ref, dkw_ref, dsk_ref, carry = rest[nd:]
        step = pl.program_id(0)
        n = nblk - 1 - step

        @pl.when(step == 0)
        def _():
            carry[...] = jnp.zeros_like(carry)
            dqw_ref[...] = jnp.zeros_like(dqw_ref)
            dkw_ref[...] = jnp.zeros_like(dkw_ref)
            dsk_ref[...] = jnp.zeros_like(dsk_ref)

        low, low2 = _low_half(WINDOW), _low_half(2 * WINDOW)
        valid = _stacked_band_mask(n)
        head_lane = lax.broadcasted_iota(jnp.int32, (WINDOW, N_Q_HEADS), 1)
        sink_lane = lax.broadcasted_iota(jnp.int32, (1, N_Q_HEADS), 1)
        kv = jnp.concatenate([kvp_ref[...], kvc_ref[...]], axis=0)
        qwv, kwv = qw_ref[...], kw_ref[...]
        lse_blk = lse_ref[...]
        dqw = jnp.zeros((1, LANES), F32)
        dkw = jnp.zeros((1, LANES), F32)
        dsk = jnp.zeros((1, N_Q_HEADS), F32)
        for t in range(N_KV_HEADS // HEADS_PER_TILE):
            kt = kv[:, t * LANES:(t + 1) * LANES]
            vt = kv[:, KV_W + t * LANES:KV_W + (t + 1) * LANES]
            rk = _pair_rstd(kt, low2)
            kn = kt * rk * kwv
            dkn_t = jnp.zeros((2 * WINDOW, LANES), F32)
            dv_t = jnp.zeros((2 * WINDOW, LANES), F32)
            for hi in range(HEADS_PER_TILE):
                g = HEADS_PER_TILE * t + hi
                kdup = _dup_half(kn, hi, low2).astype(_MXU)
                vdup = _dup_half(vt, hi, low2).astype(_MXU)
                tiles = (2 * g, 2 * g + 1)
                qx, rq, stack, dstack, lse_rows = [], [], [], [], []
                for tq in tiles:
                    qt = q_ref[:, tq * LANES:(tq + 1) * LANES]
                    r = _pair_rstd(qt, low)
                    rq.append(r)
                    qx.append(qt * r)
                    stack += _split_heads(qx[-1] * qwv, low)
                    dstack += _split_heads(do_ref[:, tq * LANES:(tq + 1) * LANES], low)
                for r in range(Q_PER_KV):
                    lse_rows.append(jnp.sum(jnp.where(head_lane == Q_PER_KV * g + r, lse_blk, 0.0), axis=-1, keepdims=True))
                qs = jnp.concatenate(stack, axis=0).astype(_MXU)
                dos = jnp.concatenate(dstack, axis=0).astype(_MXU)
                lse_col = jnp.concatenate(lse_rows, axis=0)
                s = lax.dot_general(qs, kdup, _NT, preferred_element_type=F32) * scale
                s = jnp.where(valid, s, -1e30)
                p = jnp.exp(s - lse_col)
                dp = lax.dot_general(dos, vdup, _NT, preferred_element_type=F32)
                dsum = jnp.sum(p * dp, axis=-1, keepdims=True)
                ds = (p * (dp - dsum) * scale).astype(_MXU)
                dsink = -jnp.exp(_stacked_sinks(sink_ref, g) - lse_col) * dsum
                for r in range(Q_PER_KV):
                    dsk = dsk + jnp.where(sink_lane == Q_PER_KV * g + r, _colsum(dsink[r * WINDOW:(r + 1) * WINDOW]), 0.0)
                dv_g = _fold_halves(lax.dot_general(p.astype(_MXU), dos, _TN, preferred_element_type=F32))
                dkn_g = _fold_halves(lax.dot_general(ds, qs, _TN, preferred_element_type=F32))
                dv_t = jnp.where(low2, dv_t, dv_g) if hi else jnp.where(low2, dv_g, dv_t)
                dkn_t = jnp.where(low2, dkn_t, dkn_g) if hi else jnp.where(low2, dkn_g, dkn_t)
                dqn = jnp.dot(ds, kdup, preferred_element_type=F32)
                for i, tq in enumerate(tiles):
                    dqn_t = jnp.where(low, dqn[2 * i * WINDOW:(2 * i + 1) * WINDOW],
                                      dqn[(2 * i + 1) * WINDOW:(2 * i + 2) * WINDOW])
                    dq = rq[i] * (qwv * dqn_t - qx[i] * _pair_mean(dqn_t * qwv * qx[i], low))
                    d_ref[:, tq * LANES:(tq + 1) * LANES] = dq.astype(d_ref.dtype)
                    dqw = dqw + _colsum(dqn_t * qx[i])
            k_cols = slice(t * LANES, (t + 1) * LANES)
            v_cols = slice(KV_W + t * LANES, KV_W + (t + 1) * LANES)
            dkn_c = dkn_t[WINDOW:] + carry[:, k_cols]
            rc = rk[WINDOW:]
            kx = kt[WINDOW:] * rc
            dk = rc * (kwv * dkn_c - kx * _pair_mean(dkn_c * kwv * kx, low))
            d_ref[:, ATTN_W + t * LANES:ATTN_W + (t + 1) * LANES] = dk.astype(d_ref.dtype)
            d_ref[:, ATTN_W + KV_W + t * LANES:ATTN_W + KV_W + (t + 1) * LANES] = (
                dv_t[WINDOW:] + carry[:, v_cols]).astype(d_ref.dtype)
            carry[:, k_cols] = dkn_t[:WINDOW]
            carry[:, v_cols] = dv_t[:WINDOW]
            dkw = dkw + _colsum(dkn_c * kx)
        dqw_ref[...] += dqw
        dkw_ref[...] += dkw
        dsk_ref[...] += dsk

    return pl.pallas_call(
        body, name="attn_bwd", grid=(nblk,),
        in_specs=[pl.BlockSpec(memory_space=pltpu.SMEM), q_spec, kvc_spec, kvp_spec, w_spec, w_spec, l_spec, q_spec]
        + [_ANY] * nd,
        out_specs=[d_spec, w_spec, w_spec, s_spec],
        out_shape=[jax.ShapeDtypeStruct((seq, QKV_W), _MXU), jax.ShapeDtypeStruct((1, LANES), F32),
                   jax.ShapeDtypeStruct((1, LANES), F32), jax.ShapeDtypeStruct((1, N_Q_HEADS), F32)],
        scratch_shapes=[pltpu.VMEM((WINDOW, 2 * KV_W), F32)],
        compiler_params=_params(("arbitrary",)),
    )(sinks, proj, proj, proj, qw2, kw2, lse, do, *deps)


def _ssm_discretise(a_re, a_im, log_dt):
    dt = jnp.exp(log_dt)
    mag = jnp.exp(dt * a_re)
    ab_re = mag * jnp.cos(dt * a_im)
    ab_im = mag * jnp.sin(dt * a_im)
    num_re = ab_re - 1.0
    num_im = ab_im
    den = a_re * a_re + a_im * a_im
    cf_re = (num_re * a_re + num_im * a_im) / den
    cf_im = (num_im * a_re - num_re * a_im) / den
    return ab_re, ab_im, cf_re, cf_im


def _ssm_params_fwd(a_re, a_im, log_dt):
    shp = jax.ShapeDtypeStruct(a_re.shape, F32)

    def body(are_ref, aim_ref, ldt_ref, abr_ref, abi_ref, cfr_ref, cfi_ref, alr_ref, ali_ref):
        abr, abi, cfr, cfi = _ssm_discretise(are_ref[...], aim_ref[...], ldt_ref[...])
        abr_ref[...], abi_ref[...], cfr_ref[...], cfi_ref[...] = abr, abi, cfr, cfi
        pr, pi = abr, abi
        for _ in range(int(math.log2(SSM_L))):
            pr, pi = pr * pr - pi * pi, 2.0 * pr * pi
        alr_ref[...], ali_ref[...] = pr, pi

    return pl.pallas_call(body, name="ssm_params_fwd", out_shape=[shp] * 6)(a_re, a_im, log_dt)


def _ssm_params_bwd(a_re, a_im, log_dt, d_abr, d_abi, d_cfr, d_cfi):
    def body(are_ref, aim_ref, ldt_ref, g0, g1, g2, g3, dare_ref, daim_ref, dldt_ref):
        _, vjp = jax.vjp(_ssm_discretise, are_ref[...], aim_ref[...], ldt_ref[...])
        dare_ref[...], daim_ref[...], dldt_ref[...] = vjp((g0[...], g1[...], g2[...], g3[...]))

    return pl.pallas_call(
        body, name="ssm_params_bwd",
        out_shape=[jax.ShapeDtypeStruct(a_re.shape, F32), jax.ShapeDtypeStruct(a_im.shape, F32),
                   jax.ShapeDtypeStruct(log_dt.shape, F32)],
    )(a_re, a_im, log_dt, d_abr, d_abi, d_cfr, d_cfi)


def _scan_cols(j):
    return pl.ds(j * SSM_SB, SSM_SB)


def _rows8(r):
    return pl.ds(pl.multiple_of(r * SUBLANES, SUBLANES), SUBLANES)


def _bcast8(row):
    return jnp.broadcast_to(row, (SUBLANES, row.shape[-1]))


SCAN_UNROLL = 8


def _scan_loop(n, step, init):
    def trip(o, carry):
        for i in range(SCAN_UNROLL):
            carry = step(o * SCAN_UNROLL + i, carry)
        return carry

    return lax.fori_loop(0, n // SCAN_UNROLL, trip, init)


def _ssm_fwd(u, b_re, b_im, c_re, c_im, d_skip, coef):
    seq = u.shape[0]
    nc = seq // SSM_T
    T, L = SSM_T, SSM_L

    def body(u_ref, bre_ref, bim_ref, cre_ref, cim_ref, d_ref, are_ref, aim_ref, cfr_ref, cfi_ref, alr_ref, ali_ref,
             y_ref, sre_ref, sim_ref, ire_ref, iim_ref, car_re, car_im, end_re, end_im):
        c = pl.program_id(0)

        @pl.when(c == 0)
        def _():
            car_re[...] = jnp.zeros_like(car_re)
            car_im[...] = jnp.zeros_like(car_im)

        for j in range(SSM_JB):
            ub = u_ref[:, j * LANES:(j + 1) * LANES].astype(_MXU)
            bur = jnp.dot(ub, bre_ref[j], preferred_element_type=F32)
            bui = jnp.dot(ub, bim_ref[j], preferred_element_type=F32)
            cfr, cfi = cfr_ref[:, _scan_cols(j)], cfi_ref[:, _scan_cols(j)]
            sre_ref[:, _scan_cols(j)] = cfr * bur - cfi * bui
            sim_ref[:, _scan_cols(j)] = cfr * bui + cfi * bur

        for j in range(SSM_JB):
            cols = _scan_cols(j)
            ar, ai = _bcast8(are_ref[:, cols]), _bcast8(aim_ref[:, cols])

            def step1(r, s, cols=cols, ar=ar, ai=ai):
                sr, si = s
                rows = _rows8(r)
                return (ar * sr - ai * si + sre_ref[rows, cols], ar * si + ai * sr + sim_ref[rows, cols])

            zero = jnp.zeros((SUBLANES, SSM_SB), F32)
            er, ei = _scan_loop(L, step1, (zero, zero))
            end_re[:, cols] = er
            end_im[:, cols] = ei

        alr, ali = alr_ref[...], ali_ref[...]
        cr, ci = car_re[...], car_im[...]
        ire_ref[0:1, :] = cr
        iim_ref[0:1, :] = ci
        for i in range(1, SUBLANES):
            er, ei = end_re[i - 1:i, :], end_im[i - 1:i, :]
            cr, ci = alr * cr - ali * ci + er, alr * ci + ali * cr + ei
            ire_ref[i:i + 1, :] = cr
            iim_ref[i:i + 1, :] = ci

        for j in range(SSM_JB):
            cols = _scan_cols(j)
            ar, ai = _bcast8(are_ref[:, cols]), _bcast8(aim_ref[:, cols])

            def step2(r, s, cols=cols, ar=ar, ai=ai):
                sr, si = s
                rows = _rows8(r)
                nr = ar * sr - ai * si + sre_ref[rows, cols]
                ni = ar * si + ai * sr + sim_ref[rows, cols]
                sre_ref[rows, cols] = nr
                sim_ref[rows, cols] = ni
                return nr, ni

            _scan_loop(L, step2, (ire_ref[:, cols], iim_ref[:, cols]))

        car_re[...] = sre_ref[T - 1:T, :]
        car_im[...] = sim_ref[T - 1:T, :]

        for j in range(SSM_JB):
            cols = _scan_cols(j)
            ch = slice(j * LANES, (j + 1) * LANES)
            y = (jnp.dot(sre_ref[:, cols].astype(_MXU), cre_ref[j], preferred_element_type=F32)
                 - jnp.dot(sim_ref[:, cols].astype(_MXU), cim_ref[j], preferred_element_type=F32))
            y_ref[:, ch] = y + d_ref[:, ch] * u_ref[:, ch]

    tok = pl.BlockSpec((T, SSM_W), lambda c: (c, 0))
    st = pl.BlockSpec((T, N_STATES), lambda c: (c, 0))
    ini = pl.BlockSpec((None, SUBLANES, N_STATES), lambda c: (c, 0, 0))
    bsp = pl.BlockSpec((SSM_JB, LANES, SSM_SB), lambda c: (0, 0, 0))
    csp = pl.BlockSpec((SSM_JB, SSM_SB, LANES), lambda c: (0, 0, 0))
    row_w = pl.BlockSpec((1, SSM_W), lambda c: (0, 0))
    row_s = pl.BlockSpec((1, N_STATES), lambda c: (0, 0))
    return pl.pallas_call(
        body, name="ssm_fwd", grid=(nc,),
        in_specs=[tok, bsp, bsp, csp, csp, row_w] + [row_s] * 6,
        out_specs=[tok, st, st, ini, ini],
        out_shape=[jax.ShapeDtypeStruct((seq, SSM_W), F32),
                   jax.ShapeDtypeStruct((seq, N_STATES), F32), jax.ShapeDtypeStruct((seq, N_STATES), F32),
                   jax.ShapeDtypeStruct((nc, SUBLANES, N_STATES), F32),
                   jax.ShapeDtypeStruct((nc, SUBLANES, N_STATES), F32)],
        scratch_shapes=[pltpu.VMEM((1, N_STATES), F32), pltpu.VMEM((1, N_STATES), F32),
                        pltpu.VMEM((SUBLANES, N_STATES), F32), pltpu.VMEM((SUBLANES, N_STATES), F32)],
        compiler_params=_params(("arbitrary",)),
    )(u, b_re, b_im, c_re, c_im, d_skip, *coef)


def _ssm_bwd(dy, u, s_re, s_im, i_re, i_im, b_re, b_im, c_re, c_im, d_skip, coef):
    seq = u.shape[0]
    nc = seq // SSM_T
    T, L = SSM_T, SSM_L

    def body(dy_ref, u_ref, sre_ref, sim_ref, ire_ref, iim_ref, bre_ref, bim_ref, cre_ref, cim_ref, d_ref,
             are_ref, aim_ref, cfr_ref, cfi_ref, alr_ref, ali_ref,
             du_ref, dbre_out, dbim_out, dcre_out, dcim_out, dd_ref, dar_ref, dai_ref, dcfr_ref, dcfi_ref,
             lre, lim, car_re, car_im, end_re, end_im, ini_re, ini_im, dbre_ref, dbim_ref, dcre_ref, dcim_ref):
        step = pl.program_id(0)

        @pl.when(step == 0)
        def _():
            car_re[...] = jnp.zeros_like(car_re)
            car_im[...] = jnp.zeros_like(car_im)
            for ref in (dbre_ref, dbim_ref, dcre_ref, dcim_ref, dd_ref, dar_ref, dai_ref, dcfr_ref, dcfi_ref):
                ref[...] = jnp.zeros_like(ref)

        for j in range(SSM_JB):
            dyb = dy_ref[:, j * LANES:(j + 1) * LANES].astype(_MXU)
            lre[:, _scan_cols(j)] = lax.dot_general(dyb, cre_ref[j], _NT, preferred_element_type=F32)
            lim[:, _scan_cols(j)] = -lax.dot_general(dyb, cim_ref[j], _NT, preferred_element_type=F32)

        for j in range(SSM_JB):
            cols = _scan_cols(j)
            ar, ai = _bcast8(are_ref[:, cols]), _bcast8(aim_ref[:, cols])

            def step1(t, s, cols=cols, ar=ar, ai=ai):
                sr, si = s
                rows = _rows8(L - 1 - t)
                return (ar * sr + ai * si + lre[rows, cols], ar * si - ai * sr + lim[rows, cols])

            zero = jnp.zeros((SUBLANES, SSM_SB), F32)
            er, ei = _scan_loop(L, step1, (zero, zero))
            end_re[:, cols] = er
            end_im[:, cols] = ei

        alr, ali = alr_ref[...], ali_ref[...]
        cr, ci = car_re[...], car_im[...]
        ini_re[SUBLANES - 1:SUBLANES, :] = cr
        ini_im[SUBLANES - 1:SUBLANES, :] = ci
        for i in range(SUBLANES - 2, -1, -1):
            er, ei = end_re[i + 1:i + 2, :], end_im[i + 1:i + 2, :]
            cr, ci = alr * cr + ali * ci + er, alr * ci - ali * cr + ei
            ini_re[i:i + 1, :] = cr
            ini_im[i:i + 1, :] = ci

        for j in range(SSM_JB):
            cols = _scan_cols(j)
            ar, ai = _bcast8(are_ref[:, cols]), _bcast8(aim_ref[:, cols])

            def step2(t, s, cols=cols, ar=ar, ai=ai):
                sr, si = s
                rows = _rows8(L - 1 - t)
                nr = ar * sr + ai * si + lre[rows, cols]
                ni = ar * si - ai * sr + lim[rows, cols]
                lre[rows, cols] = nr
                lim[rows, cols] = ni
                return nr, ni

            _scan_loop(L, step2, (ini_re[:, cols], ini_im[:, cols]))

        car_re[...] = lre[0:1, :]
        car_im[...] = lim[0:1, :]

        head, tail, body_rows = slice(0, SUBLANES), slice(SUBLANES, T), slice(0, T - SUBLANES)
        for j in range(SSM_JB):
            cols = _scan_cols(j)
            ch = slice(j * LANES, (j + 1) * LANES)
            lr, li = lre[:, cols], lim[:, cols]
            dar_ref[:, cols] += (_colsum(lre[tail, cols] * sre_ref[body_rows, cols] + lim[tail, cols] * sim_ref[body_rows, cols])
                                 + _colsum(lre[head, cols] * ire_ref[:, cols] + lim[head, cols] * iim_ref[:, cols]))
            dai_ref[:, cols] += (_colsum(lim[tail, cols] * sre_ref[body_rows, cols] - lre[tail, cols] * sim_ref[body_rows, cols])
                                 + _colsum(lim[head, cols] * ire_ref[:, cols] - lre[head, cols] * iim_ref[:, cols]))
            uf = u_ref[:, ch]
            ub = uf.astype(_MXU)
            bur = jnp.dot(ub, bre_ref[j], preferred_element_type=F32)
            bui = jnp.dot(ub, bim_ref[j], preferred_element_type=F32)
            dcfr_ref[:, cols] += _colsum(lr * bur + li * bui)
            dcfi_ref[:, cols] += _colsum(li * bur - lr * bui)
            cfr, cfi = cfr_ref[:, cols], cfi_ref[:, cols]
            dbur = (cfr * lr + cfi * li).astype(_MXU)
            dbui = (cfr * li - cfi * lr).astype(_MXU)
            dyf = dy_ref[:, ch]
            dyb = dyf.astype(_MXU)
            du_ref[:, ch] = (lax.dot_general(dbur, bre_ref[j], _NT, preferred_element_type=F32)
                             + lax.dot_general(dbui, bim_ref[j], _NT, preferred_element_type=F32)
                             + d_ref[:, ch] * dyf)
            dbre_ref[j] += lax.dot_general(ub, dbur, _TN, preferred_element_type=F32)
            dbim_ref[j] += lax.dot_general(ub, dbui, _TN, preferred_element_type=F32)
            dcre_ref[j] += lax.dot_general(sre_ref[:, cols].astype(_MXU), dyb, _TN, preferred_element_type=F32)
            dcim_ref[j] -= lax.dot_general(sim_ref[:, cols].astype(_MXU), dyb, _TN, preferred_element_type=F32)
            dd_ref[:, ch] += _colsum(dyf * uf)

        @pl.when(step == nc - 1)
        def _():
            for acc, out in ((dbre_ref, dbre_out), (dbim_ref, dbim_out), (dcre_ref, dcre_out), (dcim_ref, dcim_out)):
                pltpu.sync_copy(acc, out)

    tok = pl.BlockSpec((T, SSM_W), lambda c: (nc - 1 - c, 0))
    st = pl.BlockSpec((T, N_STATES), lambda c: (nc - 1 - c, 0))
    ini = pl.BlockSpec((None, SUBLANES, N_STATES), lambda c: (nc - 1 - c, 0, 0))
    bsp = pl.BlockSpec((SSM_JB, LANES, SSM_SB), lambda c: (0, 0, 0))
    csp = pl.BlockSpec((SSM_JB, SSM_SB, LANES), lambda c: (0, 0, 0))
    row_w = pl.BlockSpec((1, SSM_W), lambda c: (0, 0))
    row_s = pl.BlockSpec((1, N_STATES), lambda c: (0, 0))
    big = pltpu.VMEM((T, N_STATES), F32)
    one = pltpu.VMEM((1, N_STATES), F32)
    eight = pltpu.VMEM((SUBLANES, N_STATES), F32)
    return pl.pallas_call(
        body, name="ssm_bwd", grid=(nc,),
        in_specs=[tok, tok, st, st, ini, ini, bsp, bsp, csp, csp, row_w] + [row_s] * 6,
        out_specs=[tok, _ANY, _ANY, _ANY, _ANY, row_w, row_s, row_s, row_s, row_s],
        out_shape=[jax.ShapeDtypeStruct((seq, SSM_W), F32),
                   jax.ShapeDtypeStruct((SSM_JB, LANES, SSM_SB), F32), jax.ShapeDtypeStruct((SSM_JB, LANES, SSM_SB), F32),
                   jax.ShapeDtypeStruct((SSM_JB, SSM_SB, LANES), F32), jax.ShapeDtypeStruct((SSM_JB, SSM_SB, LANES), F32),
                   jax.ShapeDtypeStruct((1, SSM_W), F32)] + [jax.ShapeDtypeStruct((1, N_STATES), F32)] * 4,
        scratch_shapes=[big, big, one, one, eight, eight, eight, eight,
                        pltpu.VMEM((SSM_JB, LANES, SSM_SB), F32), pltpu.VMEM((SSM_JB, LANES, SSM_SB), F32),
                        pltpu.VMEM((SSM_JB, SSM_SB, LANES), F32), pltpu.VMEM((SSM_JB, SSM_SB, LANES), F32)],
        compiler_params=_params(("arbitrary",)),
    )(dy, u, s_re, s_im, i_re, i_im, b_re, b_im, c_re, c_im, d_skip, *coef)


def _block_diag_b(b):
    t = b.reshape(SSM_JB, 8, STATE, GROUP).transpose(0, 1, 3, 2)
    eye = jnp.eye(8, dtype=b.dtype)
    return (t[:, :, :, None, :] * eye[None, :, None, :, None]).reshape(SSM_JB, LANES, SSM_SB)


def _block_diag_c(c):
    t = c.reshape(SSM_JB, 8, GROUP, STATE).transpose(0, 1, 3, 2)
    eye = jnp.eye(8, dtype=c.dtype)
    return (t[:, :, :, None, :] * eye[None, :, None, :, None]).reshape(SSM_JB, SSM_SB, LANES)


def _diag_of_b(blk):
    t = blk.reshape(SSM_JB, 8, GROUP, 8, STATE)
    d = jnp.sum(t * jnp.eye(8, dtype=blk.dtype)[None, :, None, :, None], axis=3)
    return d.transpose(0, 1, 3, 2).reshape(N_GROUPS, STATE, GROUP)


def _diag_of_c(blk):
    t = blk.reshape(SSM_JB, 8, STATE, 8, GROUP)
    d = jnp.sum(t * jnp.eye(8, dtype=blk.dtype)[None, :, None, :, None], axis=3)
    return d.transpose(0, 1, 3, 2).reshape(N_GROUPS, GROUP, STATE)


def _to_scan_order(v):
    seq, w = v.shape
    return v.reshape(seq // SSM_T, SUBLANES, SSM_L, w).transpose(0, 2, 1, 3).reshape(seq, w)


def _from_scan_order(v):
    seq, w = v.shape
    return v.reshape(seq // SSM_T, SSM_L, SUBLANES, w).transpose(0, 2, 1, 3).reshape(seq, w)


def _adamw_math(w, g, m, v):
    nm = ADAM_B1 * m + (1.0 - ADAM_B1) * g
    nv = ADAM_B2 * v + (1.0 - ADAM_B2) * jnp.square(g)
    m_hat = nm / (1.0 - ADAM_B1 ** ADAM_STEP)
    v_hat = nv / (1.0 - ADAM_B2 ** ADAM_STEP)
    return -ADAM_LR * (m_hat / (jnp.sqrt(v_hat) + ADAM_EPS) + ADAM_WD * w), nm, nv


def _adamw(w, g, m, v, *, name, tm, deps=()):
    rows, cols = w.shape
    nd = len(deps)

    def body(w_ref, g_ref, m_ref, v_ref, *rest):
        d_ref, nm_ref, nv_ref = rest[nd:]
        d_ref[...], nm_ref[...], nv_ref[...] = _adamw_math(w_ref[...], g_ref[...], m_ref[...], v_ref[...])

    spec = pl.BlockSpec((tm, cols), lambda i: (i, 0))
    shp = jax.ShapeDtypeStruct((rows, cols), F32)
    return pl.pallas_call(body, name=name, grid=(rows // tm,), in_specs=[spec] * 4 + [_ANY] * nd,
                          out_specs=[spec] * 3, out_shape=[shp] * 3,
                          compiler_params=_params(("arbitrary",)))(w, g, m, v, *deps)


def _place():
    x, y, c = lax.axis_index("x"), lax.axis_index("y"), lax.axis_index("c")
    chips = [(1 - x, y), (x, 1 - y), (1 - x, 1 - y)]
    return x, y, c, chips


def _remote(src, dst, send_sem, recv_sem, dev):
    return pltpu.make_async_remote_copy(src_ref=src, dst_ref=dst, send_sem=send_sem, recv_sem=recv_sem,
                                        device_id=dev, device_id_type=MESH)


def _place_shard(w, mine_arr, *, name, tm=256):
    rows, cols = w.shape

    def body(m_ref, w_ref, o_ref):
        o_ref[...] = w_ref[...].astype(o_ref.dtype)

    return pl.pallas_call(
        body, name=name,
        grid_spec=pltpu.PrefetchScalarGridSpec(
            num_scalar_prefetch=1, grid=(rows // tm,),
            in_specs=[pl.BlockSpec((tm, cols), lambda i, m: (i, 0))],
            out_specs=pl.BlockSpec((None, tm, cols), lambda i, m: (m[0], i, 0))),
        out_shape=jax.ShapeDtypeStruct((N_CHIPS, rows, cols), _WIRE),
        compiler_params=_params(("arbitrary",)),
    )(mine_arr, w)


_HBM = pl.BlockSpec(memory_space=pltpu.HBM)
_SEM = pl.BlockSpec(memory_space=pltpu.SEMAPHORE)
_EFFECT = pltpu.SideEffectType.DATAFLOW_SIDE_EFFECTING


def _copies_start(name, bufs, plan, count, after=()):
    nb, na = len(bufs), len(after)

    def body(*refs):
        send_sems, recv_sems, token = refs[nb + na], refs[nb + na + 1], refs[-1]
        copies = plan(refs[:nb])
        assert len(copies) == count
        for i, (src, dst, dev, _) in enumerate(copies):
            _remote(src, dst, send_sems.at[i], recv_sems.at[i], dev).start()
        token[...] = jnp.zeros_like(token)

    res = pl.pallas_call(
        body, name=name, in_specs=[_HBM] * nb + [_ANY] * na,
        out_specs=(_SEM, _SEM, *[_HBM] * nb, pl.BlockSpec(memory_space=pltpu.VMEM)),
        out_shape=(pltpu.SemaphoreType.DMA((count,)), pltpu.SemaphoreType.DMA((count,)),
                   *[pltpu.HBM(b.shape, b.dtype) for b in bufs], jax.ShapeDtypeStruct((SUBLANES, LANES), F32)),
        input_output_aliases={i: 2 + i for i in range(nb)},
        compiler_params=pltpu.CompilerParams(has_side_effects=_EFFECT),
    )(*[pltpu.with_memory_space_constraint(b, pltpu.HBM) for b in bufs], *after)
    return (res[0], res[1]), list(res[2:2 + nb]), res[-1]


def _copies_wait(name, bufs, sems, plan, after=(), which=None):
    nb, na = len(bufs), len(after)

    def body(*refs):
        send_sems, recv_sems = refs[nb], refs[nb + 1]
        for i, (src, _, dev, land) in enumerate(plan(refs[:nb])):
            if which is not None and i not in which:
                continue
            cp = _remote(src, land, send_sems.at[i], recv_sems.at[i], dev)
            cp.wait_send()
            cp.wait_recv()

    res = pl.pallas_call(
        body, name=name, in_specs=[_HBM] * nb + [_SEM, _SEM] + [_ANY] * na, out_specs=[_HBM] * nb,
        out_shape=[pltpu.HBM(b.shape, b.dtype) for b in bufs],
        input_output_aliases={i: i for i in range(nb)},
        compiler_params=pltpu.CompilerParams(has_side_effects=_EFFECT),
    )(*bufs, *sems, *after)
    return list(res)


def _plan_gather_ici(fulls, which=(0, 1, 2)):
    x, y, c, chips = _place()
    copies = []
    for f in fulls:
        half = pl.ds(c * (f.shape[1] // 2), f.shape[1] // 2)
        own = f.at[2 * x + y, half]
        for chip in [chips[k] for k in which]:
            copies.append((own, own, (*chip, c), f.at[2 * chip[0] + chip[1], half]))
    return copies


def _plan_gather_d2d(fulls, which=(0, 1, 2)):
    x, y, c, chips = _place()
    copies = []
    for f in fulls:
        r2 = f.shape[1] // 2
        for chip in [chips[k] for k in which]:
            blk = 2 * chip[0] + chip[1]
            landed = f.at[blk, pl.ds(c * r2, r2)]
            copies.append((landed, landed, (x, y, 1 - c), f.at[blk, pl.ds((1 - c) * r2, r2)]))
    return copies


def _plan_swap_halves(refs):
    x, y, c, _ = _place()
    n = len(refs) // 2
    copies = []
    for g, land in zip(refs[:n], refs[n:]):
        r2 = g.shape[1] // 2
        copies.append((g.at[:, pl.ds((1 - c) * r2, r2), :], land, (x, y, 1 - c), land))
    return copies


def _plan_scatter_chips(refs):
    x, y, c, chips = _place()
    n = len(refs) // 2
    copies = []
    for h, land in zip(refs[:n], refs[n:]):
        for k, chip in enumerate(chips):
            copies.append((h.at[2 * chip[0] + chip[1]], land.at[k], (*chip, c), land.at[k]))
    return copies


def _plan_join_halves(totals):
    x, y, c, _ = _place()
    copies = []
    for t in totals:
        r2 = t.shape[0] // 2
        mine = t.at[pl.ds(c * r2, r2)]
        copies.append((mine, mine, (x, y, 1 - c), t.at[pl.ds((1 - c) * r2, r2)]))
    return copies


def _add_sibling_half(g, got, c_arr, *, name, tm):
    _, rows, cols = g.shape
    r2 = rows // 2
    nb = r2 // tm

    def body(c_ref, g_ref, r_ref, o_ref):
        o_ref[...] = (g_ref[...].astype(F32) + r_ref[...].astype(F32)).astype(o_ref.dtype)

    return pl.pallas_call(
        body, name=name,
        grid_spec=pltpu.PrefetchScalarGridSpec(
            num_scalar_prefetch=1, grid=(N_CHIPS, nb),
            in_specs=[pl.BlockSpec((None, tm, cols), lambda b, i, c: (b, c[0] * nb + i, 0)),
                      pl.BlockSpec((None, tm, cols), lambda b, i, c: (b, i, 0))],
            out_specs=pl.BlockSpec((None, tm, cols), lambda b, i, c: (b, i, 0))),
        out_shape=jax.ShapeDtypeStruct((N_CHIPS, r2, cols), _WIRE),
        compiler_params=_params(("arbitrary", "arbitrary")),
    )(c_arr, g, got)


def _add_chips(h, got, place_arr, *, name, tm):
    _, r2, cols = h.shape
    nb = r2 // tm

    def body(p_ref, h_ref, r_ref, o_ref):
        o_ref[...] = ((h_ref[...].astype(F32) + r_ref[0].astype(F32)) + r_ref[1].astype(F32)) + r_ref[2].astype(F32)

    return pl.pallas_call(
        body, name=name,
        grid_spec=pltpu.PrefetchScalarGridSpec(
            num_scalar_prefetch=1, grid=(nb,),
            in_specs=[pl.BlockSpec((None, tm, cols), lambda i, p: (p[0], i, 0)),
                      pl.BlockSpec((3, tm, cols), lambda i, p: (0, i, 0))],
            out_specs=pl.BlockSpec((tm, cols), lambda i, p: (p[1] * nb + i, 0))),
        out_shape=jax.ShapeDtypeStruct((2 * r2, cols), F32),
        compiler_params=_params(("arbitrary",)),
    )(place_arr, h, got)


class _ReduceScatter:
    def __init__(self, tag, names, grads):
        self.tag, self.names, self.n = tag, names, len(names)
        core = lax.axis_index("c").astype(jnp.int32)
        chip = (2 * lax.axis_index("x") + lax.axis_index("y")).astype(jnp.int32)
        self.c_arr, self.place_arr = core.reshape(1), jnp.stack([chip, core])
        self.bufs = list(grads)

    def _start(self, step, bufs, plan, count, after):
        self.plan = plan
        self.step = f"grad_{step}_{self.tag}"
        self.sems, self.bufs, token = _copies_start(self.step + "_start", bufs, plan, count, after)
        return [token]

    def _wait(self, after):
        self.bufs = _copies_wait(self.step + "_wait", self.bufs, self.sems, self.plan, after)
        return self.bufs

    def start_swap(self, after=()):
        lands = [lax.empty((N_CHIPS, g.shape[1] // 2, g.shape[2]), g.dtype) for g in self.bufs]
        return self._start("swap", self.bufs + lands, _plan_swap_halves, self.n, after)

    def start_scatter(self, after):
        bufs = self._wait(after)
        pair = [_add_sibling_half(g, r, self.c_arr, name=f"grad_add_sibling_{nm}", tm=min(256, g.shape[1] // 2))
                for nm, g, r in zip(self.names, bufs[:self.n], bufs[self.n:])]
        lands = [lax.empty((3,) + h.shape[1:], h.dtype) for h in pair]
        return self._start("scatter", pair + lands, _plan_scatter_chips, 3 * self.n, ())

    def start_join(self, after):
        bufs = self._wait(after)
        total = [_add_chips(h, r, self.place_arr, name=f"grad_add_chips_{nm}", tm=min(256, h.shape[1]))
                 for nm, h, r in zip(self.names, bufs[:self.n], bufs[self.n:])]
        return self._start("join", total, _plan_join_halves, self.n, ())

    def finish(self, after):
        return dict(zip(self.names, self._wait(after)))


def _all_gather_small(v):
    m_per, n = v.shape

    def body(x_ref, out_ref, send_sems, recv_sems, local_sem):
        x, y, c, chips = _place()
        me, sibling = (x, y, c), (x, y, 1 - c)

        def rows(px, py, pc):
            return out_ref.at[4 * px + 2 * py + pc]

        def copy(k, block, to, src=None):
            return _remote(rows(*block) if src is None else src, rows(*block), send_sems.at[k], recv_sems.at[k], to)

        mine = pltpu.make_async_copy(x_ref, rows(*me), local_sem)
        mine.start()
        first = [copy(0, me, sibling, src=x_ref)]
        first += [copy(1 + j, me, (*chip, c), src=x_ref) for j, chip in enumerate(chips)]
        for cp in first:
            cp.start()
        passed = [copy(4 + j, (*chip, c), sibling) for j, chip in enumerate(chips)]
        for j, chip in enumerate(chips):
            copy(1 + j, (*chip, c), me).wait_recv()
            passed[j].start()
        copy(0, sibling, me).wait_recv()
        for j, chip in enumerate(chips):
            copy(4 + j, (*chip, 1 - c), me).wait_recv()
        for cp in first + passed:
            cp.wait_send()
        mine.wait()

    return pl.pallas_call(
        body, name="gather_small_grads",
        out_shape=jax.ShapeDtypeStruct((8, m_per, n), v.dtype),
        in_specs=[pl.BlockSpec(memory_space=pltpu.VMEM)], out_specs=pl.BlockSpec(memory_space=pltpu.VMEM),
        scratch_shapes=[pltpu.SemaphoreType.DMA((7,)), pltpu.SemaphoreType.DMA((7,)), pltpu.SemaphoreType.DMA],
        compiler_params=pltpu.CompilerParams(vmem_limit_bytes=VMEM_LIMIT),
    )(v)


def _sum8(v, *, name):
    _, m, n = v.shape

    def body(v_ref, o_ref):
        acc = v_ref[0]
        for d in range(1, 8):
            acc = acc + v_ref[d]
        o_ref[...] = acc

    return pl.pallas_call(body, name=name, out_shape=jax.ShapeDtypeStruct((m, n), F32),
                          compiler_params=pltpu.CompilerParams(vmem_limit_bytes=VMEM_LIMIT))(v)


def _local_step(x, target, norm_w, q_norm_w, k_norm_w, sinks, a_re, a_im, log_dt, b_re, b_im, c_re, c_im, d_skip,
                b_glu, io):
    seq = x.shape[0]
    qw2 = jnp.tile(q_norm_w.reshape(1, HEAD_DIM), (1, HEADS_PER_TILE))
    kw2 = jnp.tile(k_norm_w.reshape(1, HEAD_DIM), (1, HEADS_PER_TILE))
    nw, bg = norm_w.reshape(1, D_MODEL), b_glu.reshape(1, D_MODEL)
    dsk = d_skip.reshape(1, SSM_W)

    h, rstd = _rms_fwd(x, nw, deps=io.begin())
    proj, w_in4 = io.projection(h)
    attn, lse = _attn2_fwd(proj, qw2, kw2, sinks, deps=io.after_proj(proj))

    def gate_a(at, ag):
        return (at * (ag * _sigmoid(ag)),)

    (ya_in,) = _ew(gate_a, [(attn, "mat", 0), (proj, "mat", OFF_AGATE)], [(ATTN_W, _MXU)], rows=seq, ncol=2,
                   name="ew_attn_gate")
    w_ap4 = io.weight("w_attn_proj", ya_in)
    w_glu4, w_sp4, w_out = io.weight("w_glu", ya_in), io.weight("w_ssm_proj", ya_in), io.weight("w_out", ya_in)
    y_a = _mm(ya_in, w_ap4, mode="nn", name="mm_attn_proj", tm=2048, tn=512, tk=ATTN_W, b_blocked=True)

    flat_a = (a_re.reshape(1, N_STATES), a_im.reshape(1, N_STATES), jnp.repeat(log_dt, STATE).reshape(1, N_STATES))
    coef = _ssm_params_fwd(*flat_a)
    bre_blk, bim_blk = _block_diag_b(b_re).astype(_MXU), _block_diag_b(b_im).astype(_MXU)
    cre_blk, cim_blk = _block_diag_c(c_re).astype(_MXU), _block_diag_c(c_im).astype(_MXU)
    u_scan = _to_scan_order(proj[:, OFF_U * CW:OFF_U * CW + SSM_W])
    y_scan, s_re, s_im, i_re, i_im = _ssm_fwd(u_scan, bre_blk, bim_blk, cre_blk, cim_blk, dsk, coef)
    y_ssm = _from_scan_order(y_scan)

    (yg,) = _ew(lambda yv: (jax.nn.gelu(yv),), [(y_ssm, "mat", 0)], [(SSM_W, _MXU)], rows=seq, ncol=2, name="ew_gelu")
    glu = _mm(yg, w_glu4, mode="nn", name="mm_glu", tm=2048, tn=512, tk=SSM_W, b_blocked=True)

    def gate_s(ga, gb, ba, bb, z):
        return ((ga + ba) * _sigmoid(gb + bb) * (z * _sigmoid(z)),)

    (ys_in,) = _ew(gate_s, [(glu, "mat", 0), (glu, "mat", 2), (bg, "row", 0), (bg, "row", 2), (proj, "mat", OFF_Z)],
                   [(SSM_W, _MXU)], rows=seq, ncol=2, name="ew_ssm_gate")
    y_s = _mm(ys_in, w_sp4, mode="nn", name="mm_ssm_proj", tm=2048, tn=512, tk=SSM_W, b_blocked=True)

    def merge(ga, gs, ya, ys):
        return (_sigmoid(ga) * ya + _sigmoid(gs) * ys,)

    (merged,) = _ew(merge, [(proj, "mat", OFF_GA), (proj, "mat", OFF_GS), (y_a, "mat", 0), (y_s, "mat", 0)],
                    [(D_MODEL, _MXU)], rows=seq, ncol=4, name="ew_merge")
    dout, dout_b, sq = _mm_out_loss(merged, w_out, x, target)
    loss = 0.5 * jnp.sum(sq) / D_MODEL

    d_ya, d_ys, d_ga, d_gs = _mm_merge_bwd(dout_b, w_out, proj, y_a, y_s)
    g_w_out = _mm(merged, dout_b, mode="tn", name="mm_g_w_out", tm=1024, tn=D_MODEL, tk=1024, out_dtype=_WIRE)

    d_ya_in = _mm(d_ya, w_ap4, mode="nt", name="mm_d_attn_gate", tm=2048, tn=ATTN_W, tk=512, b_blocked=True)
    g_w_ap = _mm(ya_in, d_ya, mode="tn", name="mm_g_w_attn_proj", tm=ATTN_W, tn=512, tk=2048, out_dtype=_WIRE,
                 out_blocked=True)

    d_ys_in = _mm(d_ys, w_sp4, mode="nt", name="mm_d_ssm_gate", tm=2048, tn=SSM_W, tk=512, b_blocked=True)
    g_w_sp = _mm(ys_in, d_ys, mode="tn", name="mm_g_w_ssm_proj", tm=SSM_W, tn=512, tk=2048, out_dtype=_WIRE,
                 out_blocked=True)

    def gate_s_bwd(dv, ga, gb, ba, bb, z):
        a, sb = ga + ba, _sigmoid(gb + bb)
        f, df = _silu_and_grad(z)
        dga = dv * sb * f
        dgb = dv * a * f * sb * (1.0 - sb)
        return dga, dgb, dv * a * sb * df, _colsum(dga), _colsum(dgb)

    d_glu_a, d_glu_b, d_z, g_bga, g_bgb = _ew(
        gate_s_bwd, [(d_ys_in, "mat", 0), (glu, "mat", 0), (glu, "mat", 2), (bg, "row", 0), (bg, "row", 2),
                     (proj, "mat", OFF_Z)],
        [(SSM_W, _MXU)] * 3, rows=seq, ncol=2, n_acc=2, name="ew_ssm_gate_bwd")
    d_glu = jnp.concatenate([d_glu_a, d_glu_b], axis=1)
    d_yg = _mm(d_glu, w_glu4, mode="nt", name="mm_d_gelu", tm=2048, tn=SSM_W, tk=512, b_blocked=True)
    g_w_glu = _mm(yg, d_glu, mode="tn", name="mm_g_w_glu", tm=SSM_W, tn=512, tk=2048, out_dtype=_WIRE, out_blocked=True)
    dep = io.later_grads(dict(w_attn_proj=g_w_ap, w_glu=g_w_glu, w_ssm_proj=g_w_sp,
                              w_out=g_w_out.reshape(N_CHIPS, D_MODEL // N_CHIPS, D_MODEL)))

    def gate_a_bwd(dv, at, ag):
        f, df = _silu_and_grad(ag)
        return dv * f, dv * at * df

    d_attn, d_agate = _ew(gate_a_bwd, [(d_ya_in, "mat", 0), (attn, "mat", 0), (proj, "mat", OFF_AGATE)],
                          [(ATTN_W, F32), (ATTN_W, _MXU)], rows=seq, ncol=2, name="ew_attn_gate_bwd", deps=dep)

    def gelu_bwd(dv, yv):
        return (jax.vjp(jax.nn.gelu, yv)[1](dv)[0],)

    (d_yssm,) = _ew(gelu_bwd, [(d_yg, "mat", 0), (y_ssm, "mat", 0)], [(SSM_W, F32)], rows=seq, ncol=2, name="ew_gelu_bwd",
                    deps=dep)
    dep = io.before_attention_backward([d_attn, d_yssm])
    d_qkv, g_qw2, g_kw2, g_sk = _attn2_bwd(proj, qw2, kw2, sinks, lse, d_attn, deps=dep)
    (du_scan, g_bre, g_bim, g_cre, g_cim, g_dsk, g_abr, g_abi, g_cfr, g_cfi) = _ssm_bwd(
        _to_scan_order(d_yssm), u_scan, s_re, s_im, i_re, i_im, bre_blk, bim_blk, cre_blk, cim_blk, dsk, coef)
    g_are, g_aim, g_ldt = _ssm_params_bwd(*flat_a, g_abr, g_abi, g_cfr, g_cfi)
    g_are, g_aim = g_are.reshape(N_GROUPS, STATE), g_aim.reshape(N_GROUPS, STATE)
    g_ldt = g_ldt.reshape(N_GROUPS, STATE).sum(axis=1)
    d_u = _from_scan_order(du_scan)

    d_proj = jnp.concatenate([d_qkv, d_agate, d_u.astype(_MXU), d_z, d_ga, d_gs], axis=1)
    dep = io.before_input_projection_grad([d_proj]) + io.small_grads(dict(
        q_norm_w=g_qw2[0, :HEAD_DIM] + g_qw2[0, HEAD_DIM:], k_norm_w=g_kw2[0, :HEAD_DIM] + g_kw2[0, HEAD_DIM:],
        sinks=g_sk.reshape(N_Q_HEADS), A_re=g_are, A_im=g_aim, log_dt=g_ldt,
        B_re=_diag_of_b(g_bre), B_im=_diag_of_b(g_bim), C_re=_diag_of_c(g_cre), C_im=_diag_of_c(g_cim),
        D_skip=g_dsk.reshape(N_GROUPS, GROUP), b_glu=jnp.concatenate([g_bga, g_bgb], axis=1).reshape(D_MODEL)))
    g_w_in = _mm(h, d_proj, mode="tn", name="mm_g_w_in", tm=1024, tn=IN_W // 4, tk=1024, out_dtype=_WIRE,
                 out_blocked=True, deps=dep)
    dep = io.input_projection_grad(g_w_in)
    d_h = _mm(d_proj, w_in4, mode="nt", name="mm_d_h", tm=512, tn=D_MODEL, tk=IN_W // 4, b_blocked=True, deps=dep)
    grad_x, g_nw = _rms_bwd(d_h, x, rstd, nw, dout)
    return loss, grad_x, g_nw.reshape(D_MODEL)


_SMALL = ["norm_w", "q_norm_w", "k_norm_w", "sinks", "A_re", "A_im", "log_dt", "B_re", "B_im", "C_re", "C_im",
          "D_skip", "b_glu"]
_BIG = ["w_in", "w_attn_proj", "w_glu", "w_ssm_proj", "w_out"]
_LATER = _BIG[1:]
_RELATIONS = ("flip_x", "flip_y", "flip_xy")
_ORDER = ["norm_w", "w_in", "q_norm_w", "k_norm_w", "sinks", "w_attn_proj", "A_re", "A_im", "log_dt", "B_re", "B_im",
          "C_re", "C_im", "D_skip", "w_glu", "b_glu", "w_ssm_proj", "w_out"]
_PACK_W = 1024


def _packed_rows(size):
    unit = SUBLANES * _PACK_W
    return -(-size // unit) * SUBLANES


def _pack_small(d, names):
    parts = []
    for n in names:
        flat = d[n].reshape(-1).astype(F32)
        rows = _packed_rows(flat.shape[0])
        parts.append(jnp.pad(flat, (0, rows * _PACK_W - flat.shape[0])).reshape(rows, _PACK_W))
    return jnp.concatenate(parts, axis=0)


def _unpack_small(packed, like, names):
    out, pos = {}, 0
    for n in names:
        rows = _packed_rows(like[n].size)
        out[n] = packed[pos:pos + rows].reshape(-1)[:like[n].size].reshape(like[n].shape)
        pos += rows
    return out


def _place_block(v, index_arr, *, name):
    rows, cols = v.shape

    def body(i_ref, v_ref, o_ref):
        o_ref[...] = v_ref[...]

    return pl.pallas_call(
        body, name=name,
        grid_spec=pltpu.PrefetchScalarGridSpec(
            num_scalar_prefetch=1, grid=(1,),
            in_specs=[pl.BlockSpec((rows, cols), lambda i, d: (0, 0))],
            out_specs=pl.BlockSpec((None, rows, cols), lambda i, d: (d[0], 0, 0))),
        out_shape=jax.ShapeDtypeStruct((8, rows, cols), v.dtype),
        compiler_params=_params(("arbitrary",)),
    )(index_arr, v)


def _plan_all_to_all(refs):
    (land,) = refs
    x, y, c, _ = _place()
    own = land.at[4 * x + 2 * y + c]
    copies = []
    for fx, fy, fc in [(0, 0, 1), (0, 1, 0), (0, 1, 1), (1, 0, 0), (1, 0, 1), (1, 1, 0), (1, 1, 1)]:
        px, py, pc = (1 - x) if fx else x, (1 - y) if fy else y, (1 - c) if fc else c
        copies.append((own, own, (px, py, pc), land.at[4 * px + 2 * py + pc]))
    return copies


def _as2d(a):
    return a.reshape(1, -1) if a.ndim == 1 else a


def _adamw_whole(w, g, m, v, *, name):
    shape = w.shape
    w, g, m, v = _as2d(w), _as2d(g), _as2d(m), _as2d(v)

    def body(w_ref, g_ref, m_ref, v_ref, d_ref, nm_ref, nv_ref):
        d_ref[...], nm_ref[...], nv_ref[...] = _adamw_math(w_ref[...], g_ref[...], m_ref[...], v_ref[...])

    outs = pl.pallas_call(body, name=name, out_shape=[jax.ShapeDtypeStruct(w.shape, F32)] * 3)(w, g, m, v)
    return [o.reshape(shape) for o in outs]


class _Exchanges:
    def __init__(self, w, m, v):
        self.w, self.m, self.v = w, m, v
        self.grads, self.delta, self.new_m, self.new_v = {}, {}, {}, {}

    def _adamw(self, names, deps):
        for n in names:
            self.delta[n], self.new_m[n], self.new_v[n] = _adamw(
                self.w[n], self.grads[n], self.m[n], self.v[n], name=f"adamw_{n}", tm=128, deps=deps)

    def begin(self):
        chip = (2 * lax.axis_index("x") + lax.axis_index("y")).astype(jnp.int32).reshape(1)
        full = {n: _place_shard(self.w[n], chip, name=f"place_{n}") for n in _BIG}
        self.w_in_sems, self.w_in_buf, token = _copies_start("gather_ici_w_in_start", [full["w_in"]], _plan_gather_ici, 3)
        self.rest = _copies_start("gather_ici_rest_start", [full[n] for n in _LATER], _plan_gather_ici,
                                  3 * len(_LATER), after=[token])
        return [self.rest[2]]

    def projection(self, h):
        x, y = lax.axis_index("x"), lax.axis_index("y")
        blks = [jnp.asarray(b, jnp.int32).reshape(1)
                for b in (2 * x + y, 2 * (1 - x) + y, 2 * x + (1 - y), 2 * (1 - x) + (1 - y))]
        def landed(bufs, k, after):
            bufs = _copies_wait(f"gather_ici_w_in_{_RELATIONS[k]}_wait", bufs, self.w_in_sems, _plan_gather_ici, after,
                                which=(k,))
            plan = functools.partial(_plan_gather_d2d, which=(k,))
            sems, bufs, token = _copies_start(f"gather_d2d_w_in_{_RELATIONS[k]}_start", bufs, plan, 1)
            return bufs, (sems, plan, token)

        def handed(bufs, k, pending, after):
            sems, plan, _ = pending
            return _copies_wait(f"gather_d2d_w_in_{_RELATIONS[k]}_wait", bufs, sems, plan, after)

        bufs = self.w_in_buf
        proj = _mm_chip_block(h, bufs[0], blks[0], None, name="mm_proj_own")
        bufs, d2d_x = landed(bufs, 0, [proj])
        bufs, d2d_y = landed(bufs, 1, [d2d_x[2]])
        bufs = handed(bufs, 0, d2d_x, [d2d_y[2]])
        proj = _mm_chip_block(h, bufs[0], blks[1], proj, name="mm_proj_flip_x")
        bufs = handed(bufs, 1, d2d_y, [proj])
        proj = _mm_chip_block(h, bufs[0], blks[2], proj, name="mm_proj_flip_y")
        bufs, d2d_xy = landed(bufs, 2, [proj])
        bufs = handed(bufs, 2, d2d_xy, [d2d_xy[2]])
        proj = _mm_chip_block(h, bufs[0], blks[3], proj, name="mm_proj_flip_xy")
        return proj, bufs[0]

    def weight(self, name, after):
        if self.rest is not None:
            sems, bufs = self.rest
            later = dict(zip(_LATER, _copies_wait("gather_d2d_rest_wait", bufs, sems, _plan_gather_d2d, [after])))
            later["w_out"] = later["w_out"].reshape(D_MODEL, D_MODEL)
            self.later, self.rest = later, None
        return self.later[name]

    def after_proj(self, proj):
        sems, bufs, _ = self.rest
        bufs = _copies_wait("gather_ici_rest_wait", bufs, sems, _plan_gather_ici, [proj])
        sems, bufs, token = _copies_start("gather_d2d_rest_start", bufs, _plan_gather_d2d, 3 * len(_LATER))
        self.rest = (sems, bufs)
        return [token]

    def later_grads(self, grads):
        self.rs_later = _ReduceScatter("later", _LATER, [grads[n] for n in _LATER])
        return self.rs_later.start_swap()

    def before_attention_backward(self, after):
        return self.rs_later.start_scatter(after)

    def before_input_projection_grad(self, after):
        return self.rs_later.start_join(after)

    def input_projection_grad(self, g_w_in):
        self.grads.update(self.rs_later.finish([g_w_in]))
        self.rs_in = _ReduceScatter("w_in", ["w_in"], [g_w_in])
        self._adamw(_LATER, self.rs_in.start_swap())
        return self.rs_in.start_scatter([self.delta[n] for n in _LATER])

    def _adamw_small(self, names):
        for n in names:
            self.delta[n], self.new_m[n], self.new_v[n] = _adamw_whole(
                self.w[n], self.grads[n], self.m[n], self.v[n], name=f"adamw_{n}")

    def small_grads(self, grads):
        me = (4 * lax.axis_index("x") + 2 * lax.axis_index("y") + lax.axis_index("c")).astype(jnp.int32).reshape(1)
        land = _place_block(_pack_small(grads, _SMALL[1:]), me, name="place_small_grads")
        self.small = _copies_start("gather_small_start", [land], _plan_all_to_all, 7)
        return [self.small[2]]

    def finish(self, g_norm_w, loss, after):
        sems, bufs, _ = self.small
        (land,) = _copies_wait("gather_small_wait", bufs, sems, _plan_all_to_all, after)
        self.grads.update(_unpack_small(_sum8(land, name="sum_small_grads"), self.w, _SMALL[1:]))
        self._adamw_small(_SMALL[1:])
        rows = _packed_rows(g_norm_w.size)
        late = jnp.concatenate([_pack_small(dict(norm_w=g_norm_w), _SMALL[:1]),
                                jnp.pad(loss.reshape(1, 1), ((0, SUBLANES - 1), (0, _PACK_W - 1)))], axis=0)
        late = _sum8(_all_gather_small(late), name="sum_norm_w_grad_and_loss")
        self.grads.update(_unpack_small(late[:rows], self.w, _SMALL[:1]))
        self._adamw_small(_SMALL[:1])
        self.grads.update(self.rs_in.finish(self.rs_in.start_join([self.delta[_SMALL[0]]])))
        self._adamw(["w_in"], ())
        return late[rows, 0]


def kernel(x, norm_w, w_in, q_norm_w, k_norm_w, sinks, w_attn_proj, A_re, A_im, log_dt, B_re, B_im, C_re, C_im, D_skip, w_glu, b_glu, w_ssm_proj, w_out, loss_target, m_norm_w, m_w_in, m_q_norm_w, m_k_norm_w, m_sinks, m_w_attn_proj, m_A_re, m_A_im, m_log_dt, m_B_re, m_B_im, m_C_re, m_C_im, m_D_skip, m_w_glu, m_b_glu, m_w_ssm_proj, m_w_out, v_norm_w, v_w_in, v_q_norm_w, v_k_norm_w, v_sinks, v_w_attn_proj, v_A_re, v_A_im, v_log_dt, v_B_re, v_B_im, v_C_re, v_C_im, v_D_skip, v_w_glu, v_b_glu, v_w_ssm_proj, v_w_out):
    w = dict(norm_w=norm_w, w_in=w_in, q_norm_w=q_norm_w, k_norm_w=k_norm_w, sinks=sinks, w_attn_proj=w_attn_proj,
             A_re=A_re, A_im=A_im, log_dt=log_dt, B_re=B_re, B_im=B_im, C_re=C_re, C_im=C_im, D_skip=D_skip,
             w_glu=w_glu, b_glu=b_glu, w_ssm_proj=w_ssm_proj, w_out=w_out)
    m = dict(norm_w=m_norm_w, w_in=m_w_in, q_norm_w=m_q_norm_w, k_norm_w=m_k_norm_w, sinks=m_sinks,
             w_attn_proj=m_w_attn_proj, A_re=m_A_re, A_im=m_A_im, log_dt=m_log_dt, B_re=m_B_re, B_im=m_B_im,
             C_re=m_C_re, C_im=m_C_im, D_skip=m_D_skip, w_glu=m_w_glu, b_glu=m_b_glu, w_ssm_proj=m_w_ssm_proj,
             w_out=m_w_out)
    v = dict(norm_w=v_norm_w, w_in=v_w_in, q_norm_w=v_q_norm_w, k_norm_w=v_k_norm_w, sinks=v_sinks,
             w_attn_proj=v_w_attn_proj, A_re=v_A_re, A_im=v_A_im, log_dt=v_log_dt, B_re=v_B_re, B_im=v_B_im,
             C_re=v_C_re, C_im=v_C_im, D_skip=v_D_skip, w_glu=v_w_glu, b_glu=v_b_glu, w_ssm_proj=v_w_ssm_proj,
             w_out=v_w_out)

    io = _Exchanges(w, m, v)
    loss, grad_x, g_norm_w = _local_step(x[0], loss_target[0], norm_w, q_norm_w, k_norm_w, sinks, A_re, A_im, log_dt,
                                         B_re, B_im, C_re, C_im, D_skip, b_glu, io)
    loss = io.finish(g_norm_w, loss, [grad_x])
    grads, delta, new_m, new_v = io.grads, io.delta, io.new_m, io.new_v

    return (loss, grad_x[None], *[grads[n] for n in _ORDER], *[delta[n] for n in _ORDER],
            *[new_m[n] for n in _ORDER], *[new_v[n] for n in _ORDER])
```

```python
import functools
import math

import jax
import jax.numpy as jnp
from jax import lax
from jax.experimental import pallas as pl
from jax.experimental.pallas import tpu as pltpu

F32 = jnp.float32
_MXU = jnp.bfloat16
_WIRE = jnp.bfloat16

LANES = 128
SUBLANES = 8
VMEM_LIMIT = 56 * 1024 * 1024

D_MODEL = 2048
HEAD_DIM = 64
N_Q_HEADS = 16
N_KV_HEADS = 4
Q_PER_KV = 4
ATTN_W = 1024
KV_W = 256
WINDOW = 128
SSM_W = 1024
GROUP = 16
N_GROUPS = 64
STATE = 64
N_STATES = N_GROUPS * STATE
IN_W = 8704
NORM_EPS = 1e-6
N_CHIPS = 4
CW = 512
OFF_AGATE, OFF_U, OFF_Z, OFF_GA, OFF_GS = 3, 5, 7, 9, 13

SSM_T = 256
SSM_L = SSM_T // SUBLANES
SSM_JB = 8
SSM_SB = N_STATES // SSM_JB

ADAM_LR, ADAM_B1, ADAM_B2, ADAM_EPS, ADAM_WD, ADAM_STEP = 0.001, 0.9, 0.999, 1e-08, 0.01, 10

MESH = pl.DeviceIdType.MESH
_ANY = pl.BlockSpec(memory_space=pl.ANY)


def _params(sem=None):
    return pltpu.CompilerParams(dimension_semantics=sem, vmem_limit_bytes=VMEM_LIMIT)


def _mm(a, b, *, mode, name, tm, tn, tk, out_dtype=F32, b_blocked=False, out_blocked=False, deps=()):
    nd = len(deps)
    if mode == "tn":
        K, M = a.shape
    else:
        M, K = a.shape
    if mode == "nn":
        N = b.shape[0] * b.shape[2] if b_blocked else b.shape[1]
    elif mode == "nt":
        N = b.shape[1] if b_blocked else b.shape[0]
    else:
        N = b.shape[1]
    tm, tn, tk = min(tm, M), min(tn, N), min(tk, K)
    nj, ni, nk = N // tn, M // tm, K // tk
    assert nj * tn == N and ni * tm == M and nk * tk == K, (name, M, N, K)
    dims = {"nn": (((1,), (0,)), ((), ())), "nt": (((1,), (1,)), ((), ())), "tn": (((0,), (0,)), ((), ()))}[mode]

    if mode == "tn":
        a_spec = pl.BlockSpec((tk, tm), lambda j, i, k: (k, i))
    else:
        a_spec = pl.BlockSpec((tm, tk), lambda j, i, k: (i, k))
    if mode == "nn":
        if b_blocked:
            assert b.shape[0] == nj and b.shape[2] == tn
            b_spec = pl.BlockSpec((None, tk, tn), lambda j, i, k: (j, k, 0))
        else:
            b_spec = pl.BlockSpec((tk, tn), lambda j, i, k: (k, j))
    elif mode == "nt":
        if b_blocked:
            assert b.shape[0] == nk and b.shape[2] == tk
            b_spec = pl.BlockSpec((None, tn, tk), lambda j, i, k: (k, j, 0))
        else:
            b_spec = pl.BlockSpec((tn, tk), lambda j, i, k: (j, k))
    else:
        b_spec = pl.BlockSpec((tk, tn), lambda j, i, k: (k, j))
    if out_blocked:
        assert nj == N_CHIPS
        o_spec = pl.BlockSpec((None, tm, tn), lambda j, i, k: (j, i, 0))
        o_shape = jax.ShapeDtypeStruct((nj, M, tn), out_dtype)
    else:
        o_spec = pl.BlockSpec((tm, tn), lambda j, i, k: (i, j))
        o_shape = jax.ShapeDtypeStruct((M, N), out_dtype)
    use_acc = nk > 1 and out_dtype != F32

    def body(a_ref, b_ref, *rest):
        o_ref, scratch = rest[nd], rest[nd + 1:]
        part = lax.dot_general(a_ref[...].astype(_MXU), b_ref[...].astype(_MXU), dims,
                               preferred_element_type=F32)
        if nk == 1:
            o_ref[...] = part.astype(o_ref.dtype)
            return
        k = pl.program_id(2)
        acc = scratch[0] if use_acc else o_ref

        @pl.when(k == 0)
        def _():
            acc[...] = part

        @pl.when(k > 0)
        def _():
            acc[...] += part

        if use_acc:
            @pl.when(k == nk - 1)
            def _():
                o_ref[...] = acc[...].astype(o_ref.dtype)

    return pl.pallas_call(
        body, name=name, grid=(nj, ni, nk), in_specs=[a_spec, b_spec] + [_ANY] * nd, out_specs=o_spec,
        out_shape=o_shape, scratch_shapes=[pltpu.VMEM((tm, tn), F32)] if use_acc else [],
        compiler_params=_params(("parallel", "parallel", "arbitrary")),
    )(a, b, *deps)


def _mm_chip_block(a, b4, blk, prev, *, name, tm=512, deps=()):
    M, K = a.shape
    nchip, _, C = b4.shape
    tm = min(tm, M)
    extra = ([] if prev is None else [prev]) + list(deps)

    def body(blk_ref, a_ref, b_ref, *rest):
        rest[-1][...] = jnp.dot(a_ref[...].astype(_MXU), b_ref[...].astype(_MXU), preferred_element_type=F32)

    return pl.pallas_call(
        body, name=name,
        grid_spec=pltpu.PrefetchScalarGridSpec(
            num_scalar_prefetch=1, grid=(M // tm,),
            in_specs=[pl.BlockSpec((tm, K), lambda i, c: (i, 0)), pl.BlockSpec((None, K, C), lambda i, c: (c[0], 0, 0))]
            + [_ANY] * len(extra),
            out_specs=pl.BlockSpec((tm, C), lambda i, c: (i, c[0]))),
        out_shape=jax.ShapeDtypeStruct((M, nchip * C), F32),
        input_output_aliases={} if prev is None else {3: 0},
        compiler_params=_params(("arbitrary",)),
    )(blk, a, b4, *extra)


def _mm_out_loss(merged, w_out, x, target, *, tm=256):
    rows, d = x.shape

    def body(m_ref, w_ref, x_ref, t_ref, d_ref, db_ref, sq_ref):
        mo = jnp.dot(m_ref[...].astype(_MXU), w_ref[...].astype(_MXU), preferred_element_type=F32)
        err = (x_ref[...] + mo) - t_ref[...]
        dout = err * (1.0 / d)
        d_ref[...] = dout
        db_ref[...] = dout.astype(db_ref.dtype)
        part = _colsum(err * err)
        i = pl.program_id(0)

        @pl.when(i == 0)
        def _():
            sq_ref[...] = part

        @pl.when(i > 0)
        def _():
            sq_ref[...] += part

    tile = pl.BlockSpec((tm, d), lambda i: (i, 0))
    return pl.pallas_call(
        body, name="mm_out_loss", grid=(rows // tm,),
        in_specs=[tile, pl.BlockSpec((d, d), lambda i: (0, 0)), tile, tile],
        out_specs=[tile, tile, pl.BlockSpec((1, d), lambda i: (0, 0))],
        out_shape=[jax.ShapeDtypeStruct((rows, d), F32), jax.ShapeDtypeStruct((rows, d), _MXU),
                   jax.ShapeDtypeStruct((1, d), F32)],
        compiler_params=_params(("arbitrary",)),
    )(merged, w_out, x, target)


def _mm_merge_bwd(dout_b, w_out, proj, y_a, y_s, *, tm=256):
    rows, d = y_a.shape
    ncol = d // CW

    def body(do_ref, w_ref, *refs):
        ga_refs, gs_refs = refs[:ncol], refs[ncol:2 * ncol]
        ya_ref, ys_ref, dya_ref, dys_ref, dga_ref, dgs_ref = refs[2 * ncol:]
        dm = lax.dot_general(do_ref[...].astype(_MXU), w_ref[...].astype(_MXU), _NT, preferred_element_type=F32)
        for j in range(ncol):
            cols = slice(j * CW, (j + 1) * CW)
            dmj = dm[:, cols]
            sa, ss = _sigmoid(ga_refs[j][...]), _sigmoid(gs_refs[j][...])
            dya_ref[:, cols] = (sa * dmj).astype(dya_ref.dtype)
            dys_ref[:, cols] = (ss * dmj).astype(dys_ref.dtype)
            dga_ref[:, cols] = (dmj * ya_ref[:, cols] * sa * (1.0 - sa)).astype(dga_ref.dtype)
            dgs_ref[:, cols] = (dmj * ys_ref[:, cols] * ss * (1.0 - ss)).astype(dgs_ref.dtype)

    tile = pl.BlockSpec((tm, d), lambda i: (i, 0))
    gate = [pl.BlockSpec((tm, CW), lambda i, c=off + j: (i, c)) for off in (OFF_GA, OFF_GS) for j in range(ncol)]
    return pl.pallas_call(
        body, name="mm_merge_bwd", grid=(rows // tm,),
        in_specs=[tile, pl.BlockSpec((d, d), lambda i: (0, 0))] + gate + [tile, tile],
        out_specs=[tile] * 4, out_shape=[jax.ShapeDtypeStruct((rows, d), _MXU)] * 4,
        compiler_params=_params(("arbitrary",)),
    )(dout_b, w_out, *([proj] * (2 * ncol)), y_a, y_s)


def _ew(fn, ins, outs, *, rows, ncol, name, n_acc=0, tm=512, deps=()):
    n_in, n_out, nd = len(ins), len(outs), len(deps)
    tm = min(tm, rows)
    in_specs = []
    for _, kind, col0 in ins:
        if kind == "mat":
            in_specs.append(pl.BlockSpec((tm, CW), lambda j, i, c0=col0: (i, c0 + j)))
        else:
            in_specs.append(pl.BlockSpec((1, CW), lambda j, i, c0=col0: (0, c0 + j)))
    out_specs = [pl.BlockSpec((tm, CW), lambda j, i: (i, j)) for _ in outs]
    out_shape = [jax.ShapeDtypeStruct((rows, w), dt) for w, dt in outs]
    for _ in range(n_acc):
        out_specs.append(pl.BlockSpec((1, CW), lambda j, i: (0, j)))
        out_shape.append(jax.ShapeDtypeStruct((1, ncol * CW), F32))

    def body(*refs):
        vals = fn(*[r[...] for r in refs[:n_in]])
        refs = refs[n_in + nd:]
        for r, v in zip(refs[:n_out], vals[:n_out]):
            r[...] = v.astype(r.dtype)
        i = pl.program_id(1)
        for r, v in zip(refs[n_out:], vals[n_out:]):
            @pl.when(i == 0)
            def _(r=r, v=v):
                r[...] = v

            @pl.when(i > 0)
            def _(r=r, v=v):
                r[...] += v

    res = pl.pallas_call(
        body, name=name, grid=(ncol, rows // tm), in_specs=in_specs + [_ANY] * nd, out_specs=out_specs,
        out_shape=out_shape, compiler_params=_params(("parallel", "arbitrary")),
    )(*[a for a, _, _ in ins], *deps)
    return res


def _colsum(v):
    return jnp.sum(v, axis=0, keepdims=True)


def _sigmoid(v):
    return jax.nn.sigmoid(v)


def _silu_and_grad(v):
    s = _sigmoid(v)
    return v * s, s * (1.0 + v * (1.0 - s))


def _rms_fwd(x, w, *, tm=512, deps=()):
    rows, d = x.shape
    nd = len(deps)

    def body(x_ref, w_ref, *rest):
        h_ref, r_ref = rest[nd:]
        xv = x_ref[...]
        r = lax.rsqrt(jnp.mean(xv * xv, axis=-1, keepdims=True) + NORM_EPS)
        h_ref[...] = (xv * r * w_ref[...]).astype(h_ref.dtype)
        r_ref[...] = r

    return pl.pallas_call(
        body, name="rms_fwd", grid=(rows // tm,),
        in_specs=[pl.BlockSpec((tm, d), lambda i: (i, 0)), pl.BlockSpec((1, d), lambda i: (0, 0))] + [_ANY] * nd,
        out_specs=[pl.BlockSpec((tm, d), lambda i: (i, 0)), pl.BlockSpec((tm, 1), lambda i: (i, 0))],
        out_shape=[jax.ShapeDtypeStruct((rows, d), _MXU), jax.ShapeDtypeStruct((rows, 1), F32)],
        compiler_params=_params(("arbitrary",)),
    )(x, w, *deps)


def _rms_bwd(dh, x, rstd, w, dout, *, tm=256):
    rows, d = x.shape

    def body(dh_ref, x_ref, r_ref, w_ref, do_ref, gx_ref, gw_ref):
        dhv, xv, r, wv = dh_ref[...], x_ref[...], r_ref[...], w_ref[...]
        xr = xv * r
        t = jnp.mean(dhv * wv * xr, axis=-1, keepdims=True)
        gx_ref[...] = do_ref[...] + r * (wv * dhv - xr * t)
        part = _colsum(dhv * xr)
        i = pl.program_id(0)

        @pl.when(i == 0)
        def _():
            gw_ref[...] = part

        @pl.when(i > 0)
        def _():
            gw_ref[...] += part

    return pl.pallas_call(
        body, name="rms_bwd", grid=(rows // tm,),
        in_specs=[pl.BlockSpec((tm, d), lambda i: (i, 0)), pl.BlockSpec((tm, d), lambda i: (i, 0)),
                  pl.BlockSpec((tm, 1), lambda i: (i, 0)), pl.BlockSpec((1, d), lambda i: (0, 0)),
                  pl.BlockSpec((tm, d), lambda i: (i, 0))],
        out_specs=[pl.BlockSpec((tm, d), lambda i: (i, 0)), pl.BlockSpec((1, d), lambda i: (0, 0))],
        out_shape=[jax.ShapeDtypeStruct((rows, d), F32), jax.ShapeDtypeStruct((1, d), F32)],
        compiler_params=_params(("arbitrary",)),
    )(dh, x, rstd, w, dout)


_NT = (((1,), (1,)), ((), ()))
_TN = (((0,), (0,)), ((), ()))


QKV_W = ATTN_W + 2 * KV_W
HEADS_PER_TILE = LANES // HEAD_DIM


def _low_half(rows):
    return lax.broadcasted_iota(jnp.int32, (rows, LANES), 1) < HEAD_DIM


def _pair_mean(t, low):
    m_lo = jnp.sum(jnp.where(low, t, 0.0), axis=-1, keepdims=True)
    m_hi = jnp.sum(jnp.where(low, 0.0, t), axis=-1, keepdims=True)
    return jnp.where(low, m_lo, m_hi) * (1.0 / HEAD_DIM)


def _pair_rstd(t, low):
    return lax.rsqrt(_pair_mean(t * t, low) + NORM_EPS)


def _dup_half(t, hi, low):
    swapped = pltpu.roll(t, HEAD_DIM, 1)
    return jnp.where(low, swapped, t) if hi else jnp.where(low, t, swapped)


def _fold_halves(t):
    return t + pltpu.roll(t, HEAD_DIM, 1)


def _split_heads(t, low):
    return [jnp.where(low, t, 0.0), jnp.where(low, 0.0, t)]


def _stacked_band_mask(n):
    rows = Q_PER_KV * WINDOW
    qi = lax.broadcasted_iota(jnp.int32, (rows, 2 * WINDOW), 0) % WINDOW + WINDOW
    kj = lax.broadcasted_iota(jnp.int32, (rows, 2 * WINDOW), 1)
    diff = qi - kj
    first_key = jnp.where(n > 0, 0, WINDOW)
    return (diff >= 0) & (diff < WINDOW) & (kj >= first_key)


def _stacked_sinks(sink_ref, g):
    blk = lax.broadcasted_iota(jnp.int32, (Q_PER_KV * WINDOW, 1), 0) // WINDOW
    col = jnp.full((Q_PER_KV * WINDOW, 1), sink_ref[Q_PER_KV * g], F32)
    for r in range(1, Q_PER_KV):
        col = jnp.where(blk == r, sink_ref[Q_PER_KV * g + r], col)
    return col


def _attn_in_specs(nblk, rev):
    def cur(n):
        return (nblk - 1 - n) if rev else n

    q_spec = pl.BlockSpec((WINDOW, ATTN_W), lambda n: (cur(n), 0))
    kvc_spec = pl.BlockSpec((WINDOW, 2 * KV_W), lambda n: (cur(n), ATTN_W // (2 * KV_W)))
    kvp_spec = pl.BlockSpec((WINDOW, 2 * KV_W), lambda n: (jnp.maximum(cur(n) - 1, 0), ATTN_W // (2 * KV_W)))
    w_spec = pl.BlockSpec((1, LANES), lambda n: (0, 0))
    l_spec = pl.BlockSpec((WINDOW, N_Q_HEADS), lambda n: (cur(n), 0))
    return q_spec, kvc_spec, kvp_spec, w_spec, l_spec


def _attn2_fwd(proj, qw2, kw2, sinks, deps=()):
    seq = proj.shape[0]
    nblk = seq // WINDOW
    scale = 1.0 / math.sqrt(HEAD_DIM)
    q_spec, kvc_spec, kvp_spec, w_spec, l_spec = _attn_in_specs(nblk, False)
    nd = len(deps)

    def body(sink_ref, q_ref, kvc_ref, kvp_ref, qw_ref, kw_ref, *rest):
        o_ref, lse_ref = rest[nd:]
        n = pl.program_id(0)
        low, low2 = _low_half(WINDOW), _low_half(2 * WINDOW)
        valid = _stacked_band_mask(n)
        head_lane = lax.broadcasted_iota(jnp.int32, (WINDOW, N_Q_HEADS), 1)
        kv = jnp.concatenate([kvp_ref[...], kvc_ref[...]], axis=0)
        qwv, kwv = qw_ref[...], kw_ref[...]
        lse_blk = jnp.zeros((WINDOW, N_Q_HEADS), F32)
        for t in range(N_KV_HEADS // HEADS_PER_TILE):
            kt = kv[:, t * LANES:(t + 1) * LANES]
            vt = kv[:, KV_W + t * LANES:KV_W + (t + 1) * LANES]
            kn = kt * _pair_rstd(kt, low2) * kwv
            for hi in range(HEADS_PER_TILE):
                g = HEADS_PER_TILE * t + hi
                kdup = _dup_half(kn, hi, low2).astype(_MXU)
                vdup = _dup_half(vt, hi, low2).astype(_MXU)
                stack = []
                for tq in (2 * g, 2 * g + 1):
                    qt = q_ref[:, tq * LANES:(tq + 1) * LANES]
                    stack += _split_heads(qt * _pair_rstd(qt, low) * qwv, low)
                qs = jnp.concatenate(stack, axis=0).astype(_MXU)
                s = lax.dot_general(qs, kdup, _NT, preferred_element_type=F32) * scale
                s = jnp.where(valid, s, -1e30)
                sink = _stacked_sinks(sink_ref, g)
                m = jnp.maximum(jnp.max(s, axis=-1, keepdims=True), sink)
                e = jnp.exp(s - m)
                z = jnp.sum(e, axis=-1, keepdims=True) + jnp.exp(sink - m)
                o = jnp.dot((e / z).astype(_MXU), vdup, preferred_element_type=F32)
                for i, tq in enumerate((2 * g, 2 * g + 1)):
                    o_ref[:, tq * LANES:(tq + 1) * LANES] = jnp.where(
                        low, o[2 * i * WINDOW:(2 * i + 1) * WINDOW], o[(2 * i + 1) * WINDOW:(2 * i + 2) * WINDOW])
                lse = m + jnp.log(z)
                for r in range(Q_PER_KV):
                    lse_blk = jnp.where(head_lane == Q_PER_KV * g + r, lse[r * WINDOW:(r + 1) * WINDOW], lse_blk)
        lse_ref[...] = lse_blk

    return pl.pallas_call(
        body, name="attn_fwd", grid=(nblk,),
        in_specs=[pl.BlockSpec(memory_space=pltpu.SMEM), q_spec, kvc_spec, kvp_spec, w_spec, w_spec] + [_ANY] * nd,
        out_specs=[q_spec, l_spec],
        out_shape=[jax.ShapeDtypeStruct((seq, ATTN_W), F32), jax.ShapeDtypeStruct((seq, N_Q_HEADS), F32)],
        compiler_params=_params(("arbitrary",)),
    )(sinks, proj, proj, proj, qw2, kw2, *deps)


def _attn2_bwd(proj, qw2, kw2, sinks, lse, do, deps=()):
    seq = proj.shape[0]
    nblk = seq // WINDOW
    scale = 1.0 / math.sqrt(HEAD_DIM)
    q_spec, kvc_spec, kvp_spec, w_spec, l_spec = _attn_in_specs(nblk, True)
    s_spec = pl.BlockSpec((1, N_Q_HEADS), lambda n: (0, 0))
    d_spec = pl.BlockSpec((WINDOW, QKV_W), lambda n: (nblk - 1 - n, 0))
    nd = len(deps)

    def body(sink_ref, q_ref, kvc_ref, kvp_ref, qw_ref, kw_ref, lse_ref, do_ref, *rest):
        d_ref, dqw_ref, dkw_ref, dsk_ref, carry = rest[nd:]
        step = pl.program_id(0)
        n = nblk - 1 - step

        @pl.when(step == 0)
        def _():
            carry[...] = jnp.zeros_like(carry)
            dqw_ref[...] = jnp.zeros_like(dqw_ref)
            dkw_ref[...] = jnp.zeros_like(dkw_ref)
            dsk_ref[...] = jnp.zeros_like(dsk_ref)

        low, low2 = _low_half(WINDOW), _low_half(2 * WINDOW)
        valid = _stacked_band_mask(n)
        head_lane = lax.broadcasted_iota(jnp.int32, (WINDOW, N_Q_HEADS), 1)
        sink_lane = lax.broadcasted_iota(jnp.int32, (1, N_Q_HEADS), 1)
        kv = jnp.concatenate([kvp_ref[...], kvc_ref[...]], axis=0)
        qwv, kwv = qw_ref[...], kw_ref[...]
        lse_blk = lse_ref[...]
        dqw = jnp.zeros((1, LANES), F32)
        dkw = jnp.zeros((1, LANES), F32)
        dsk = jnp.zeros((1, N_Q_HEADS), F32)
        for t in range(N_KV_HEADS // HEADS_PER_TILE):
            kt = kv[:, t * LANES:(t + 1) * LANES]
            vt = kv[:, KV_W + t * LANES:KV_W + (t + 1) * LANES]
            rk = _pair_rstd(kt, low2)
            kn = kt * rk * kwv
            dkn_t = jnp.zeros((2 * WINDOW, LANES), F32)
            dv_t = jnp.zeros((2 * WINDOW, LANES), F32)
            for hi in range(HEADS_PER_TILE):
                g = HEADS_PER_TILE * t + hi
                kdup = _dup_half(kn, hi, low2).astype(_MXU)
                vdup = _dup_half(vt, hi, low2).astype(_MXU)
                tiles = (2 * g, 2 * g + 1)
                qx, rq, stack, dstack, lse_rows = [], [], [], [], []
                for tq in tiles:
                    qt = q_ref[:, tq * LANES:(tq + 1) * LANES]
                    r = _pair_rstd(qt, low)
                    rq.append(r)
                    qx.append(qt * r)
                    stack += _split_heads(qx[-1] * qwv, low)
                    dstack += _split_heads(do_ref[:, tq * LANES:(tq + 1) * LANES], low)
                for r in range(Q_PER_KV):
                    lse_rows.append(jnp.sum(jnp.where(head_lane == Q_PER_KV * g + r, lse_blk, 0.0), axis=-1, keepdims=True))
                qs = jnp.concatenate(stack, axis=0).astype(_MXU)
                dos = jnp.concatenate(dstack, axis=0).astype(_MXU)
                lse_col = jnp.concatenate(lse_rows, axis=0)
                s = lax.dot_general(qs, kdup, _NT, preferred_element_type=F32) * scale
                s = jnp.where(valid, s, -1e30)
                p = jnp.exp(s - lse_col)
                dp = lax.dot_general(dos, vdup, _NT, preferred_element_type=F32)
                dsum = jnp.sum(p * dp, axis=-1, keepdims=True)
                ds = (p * (dp - dsum) * scale).astype(_MXU)
                dsink = -jnp.exp(_stacked_sinks(sink_ref, g) - lse_col) * dsum
                for r in range(Q_PER_KV):
                    dsk = dsk + jnp.where(sink_lane == Q_PER_KV * g + r, _colsum(dsink[r * WINDOW:(r + 1) * WINDOW]), 0.0)
                dv_g = _fold_halves(lax.dot_general(p.astype(_MXU), dos, _TN, preferred_element_type=F32))
                dkn_g = _fold_halves(lax.dot_general(ds, qs, _TN, preferred_element_type=F32))
                dv_t = jnp.where(low2, dv_t, dv_g) if hi else jnp.where(low2, dv_g, dv_t)
                dkn_t = jnp.where(low2, dkn_t, dkn_g) if hi else jnp.where(low2, dkn_g, dkn_t)
                dqn = jnp.dot(ds, kdup, preferred_element_type=F32)
                for i, tq in enumerate(tiles):
                    dqn_t = jnp.where(low, dqn[2 * i * WINDOW:(2 * i + 1) * WINDOW],
                                      dqn[(2 * i + 1) * WINDOW:(2 * i + 2) * WINDOW])
                    dq = rq[i] * (qwv * dqn_t - qx[i] * _pair_mean(dqn_t * qwv * qx[i], low))
                    d_ref[:, tq * LANES:(tq + 1) * LANES] = dq.astype(d_ref.dtype)
                    dqw = dqw + _colsum(dqn_t * qx[i])
            k_cols = slice(t * LANES, (t + 1) * LANES)
            v_cols = slice(KV_W + t * LANES, KV_W + (t + 1) * LANES)
            dkn_c = dkn_t[WINDOW:] + carry[:, k_cols]
            rc = rk[WINDOW:]
            kx = kt[WINDOW:] * rc
            dk = rc * (kwv * dkn_c - kx * _pair_mean(dkn_c * kwv * kx, low))
            d_ref[:, ATTN_W + t * LANES:ATTN_W + (t + 1) * LANES] = dk.astype(d_ref.dtype)
            d_ref[:, ATTN_W + KV_W + t * LANES:ATTN_W + KV_W + (t + 1) * LANES] = (
                dv_t[WINDOW:] + carry[:, v_cols]).astype(d_ref.dtype)
            carry[:, k_cols] = dkn_t[:WINDOW]
            carry[:, v_cols] = dv_t[:WINDOW]
            dkw = dkw + _colsum(dkn_c * kx)
        dqw_ref[...] += dqw
        dkw_ref[...] += dkw
        dsk_ref[...] += dsk

    return pl.pallas_call(
        body, name="attn_bwd", grid=(nblk,),
        in_specs=[pl.BlockSpec(memory_space=pltpu.SMEM), q_spec, kvc_spec, kvp_spec, w_spec, w_spec, l_spec, q_spec]
        + [_ANY] * nd,
        out_specs=[d_spec, w_spec, w_spec, s_spec],
        out_shape=[jax.ShapeDtypeStruct((seq, QKV_W), _MXU), jax.ShapeDtypeStruct((1, LANES), F32),
                   jax.ShapeDtypeStruct((1, LANES), F32), jax.ShapeDtypeStruct((1, N_Q_HEADS), F32)],
        scratch_shapes=[pltpu.VMEM((WINDOW, 2 * KV_W), F32)],
        compiler_params=_params(("arbitrary",)),
    )(sinks, proj, proj, proj, qw2, kw2, lse, do, *deps)


def _ssm_discretise(a_re, a_im, log_dt):
    dt = jnp.exp(log_dt)
    mag = jnp.exp(dt * a_re)
    ab_re = mag * jnp.cos(dt * a_im)
    ab_im = mag * jnp.sin(dt * a_im)
    num_re = ab_re - 1.0
    num_im = ab_im
    den = a_re * a_re + a_im * a_im
    cf_re = (num_re * a_re + num_im * a_im) / den
    cf_im = (num_im * a_re - num_re * a_im) / den
    return ab_re, ab_im, cf_re, cf_im


def _ssm_params_fwd(a_re, a_im, log_dt):
    shp = jax.ShapeDtypeStruct(a_re.shape, F32)

    def body(are_ref, aim_ref, ldt_ref, abr_ref, abi_ref, cfr_ref, cfi_ref, alr_ref, ali_ref):
        abr, abi, cfr, cfi = _ssm_discretise(are_ref[...], aim_ref[...], ldt_ref[...])
        abr_ref[...], abi_ref[...], cfr_ref[...], cfi_ref[...] = abr, abi, cfr, cfi
        pr, pi = abr, abi
        for _ in range(int(math.log2(SSM_L))):
            pr, pi = pr * pr - pi * pi, 2.0 * pr * pi
        alr_ref[...], ali_ref[...] = pr, pi

    return pl.pallas_call(body, name="ssm_params_fwd", out_shape=[shp] * 6)(a_re, a_im, log_dt)


def _ssm_params_bwd(a_re, a_im, log_dt, d_abr, d_abi, d_cfr, d_cfi):
    def body(are_ref, aim_ref, ldt_ref, g0, g1, g2, g3, dare_ref, daim_ref, dldt_ref):
        _, vjp = jax.vjp(_ssm_discretise, are_ref[...], aim_ref[...], ldt_ref[...])
        dare_ref[...], daim_ref[...], dldt_ref[...] = vjp((g0[...], g1[...], g2[...], g3[...]))

    return pl.pallas_call(
        body, name="ssm_params_bwd",
        out_shape=[jax.ShapeDtypeStruct(a_re.shape, F32), jax.ShapeDtypeStruct(a_im.shape, F32),
                   jax.ShapeDtypeStruct(log_dt.shape, F32)],
    )(a_re, a_im, log_dt, d_abr, d_abi, d_cfr, d_cfi)


def _scan_cols(j):
    return pl.ds(j * SSM_SB, SSM_SB)


def _rows8(r):
    return pl.ds(pl.multiple_of(r * SUBLANES, SUBLANES), SUBLANES)


def _bcast8(row):
    return jnp.broadcast_to(row, (SUBLANES, row.shape[-1]))


SCAN_UNROLL = 8


def _scan_loop(n, step, init):
    def trip(o, carry):
        for i in range(SCAN_UNROLL):
            carry = step(o * SCAN_UNROLL + i, carry)
        return carry

    return lax.fori_loop(0, n // SCAN_UNROLL, trip, init)


def _ssm_fwd(u, b_re, b_im, c_re, c_im, d_skip, coef):
    seq = u.shape[0]
    nc = seq // SSM_T
    T, L = SSM_T, SSM_L

    def body(u_ref, bre_ref, bim_ref, cre_ref, cim_ref, d_ref, are_ref, aim_ref, cfr_ref, cfi_ref, alr_ref, ali_ref,
             y_ref, sre_ref, sim_ref, ire_ref, iim_ref, car_re, car_im, end_re, end_im):
        c = pl.program_id(0)

        @pl.when(c == 0)
        def _():
            car_re[...] = jnp.zeros_like(car_re)
            car_im[...] = jnp.zeros_like(car_im)

        for j in range(SSM_JB):
            ub = u_ref[:, j * LANES:(j + 1) * LANES].astype(_MXU)
            bur = jnp.dot(ub, bre_ref[j], preferred_element_type=F32)
            bui = jnp.dot(ub, bim_ref[j], preferred_element_type=F32)
            cfr, cfi = cfr_ref[:, _scan_cols(j)], cfi_ref[:, _scan_cols(j)]
            sre_ref[:, _scan_cols(j)] = cfr * bur - cfi * bui
            sim_ref[:, _scan_cols(j)] = cfr * bui + cfi * bur

        for j in range(SSM_JB):
            cols = _scan_cols(j)
            ar, ai = _bcast8(are_ref[:, cols]), _bcast8(aim_ref[:, cols])

            def step1(r, s, cols=cols, ar=ar, ai=ai):
                sr, si = s
                rows = _rows8(r)
                return (ar * sr - ai * si + sre_ref[rows, cols], ar * si + ai * sr + sim_ref[rows, cols])

            zero = jnp.zeros((SUBLANES, SSM_SB), F32)
            er, ei = _scan_loop(L, step1, (zero, zero))
            end_re[:, cols] = er
            end_im[:, cols] = ei

        alr, ali = alr_ref[...], ali_ref[...]
        cr, ci = car_re[...], car_im[...]
        ire_ref[0:1, :] = cr
        iim_ref[0:1, :] = ci
        for i in range(1, SUBLANES):
            er, ei = end_re[i - 1:i, :], end_im[i - 1:i, :]
            cr, ci = alr * cr - ali * ci + er, alr * ci + ali * cr + ei
            ire_ref[i:i + 1, :] = cr
            iim_ref[i:i + 1, :] = ci

        for j in range(SSM_JB):
            cols = _scan_cols(j)
            ar, ai = _bcast8(are_ref[:, cols]), _bcast8(aim_ref[:, cols])

            def step2(r, s, cols=cols, ar=ar, ai=ai):
                sr, si = s
                rows = _rows8(r)
                nr = ar * sr - ai * si + sre_ref[rows, cols]
                ni = ar * si + ai * sr + sim_ref[rows, cols]
                sre_ref[rows, cols] = nr
                sim_ref[rows, cols] = ni
                return nr, ni

            _scan_loop(L, step2, (ire_ref[:, cols], iim_ref[:, cols]))

        car_re[...] = sre_ref[T - 1:T, :]
        car_im[...] = sim_ref[T - 1:T, :]

        for j in range(SSM_JB):
            cols = _scan_cols(j)
            ch = slice(j * LANES, (j + 1) * LANES)
            y = (jnp.dot(sre_ref[:, cols].astype(_MXU), cre_ref[j], preferred_element_type=F32)
                 - jnp.dot(sim_ref[:, cols].astype(_MXU), cim_ref[j], preferred_element_type=F32))
            y_ref[:, ch] = y + d_ref[:, ch] * u_ref[:, ch]

    tok = pl.BlockSpec((T, SSM_W), lambda c: (c, 0))
    st = pl.BlockSpec((T, N_STATES), lambda c: (c, 0))
    ini = pl.BlockSpec((None, SUBLANES, N_STATES), lambda c: (c, 0, 0))
    bsp = pl.BlockSpec((SSM_JB, LANES, SSM_SB), lambda c: (0, 0, 0))
    csp = pl.BlockSpec((SSM_JB, SSM_SB, LANES), lambda c: (0, 0, 0))
    row_w = pl.BlockSpec((1, SSM_W), lambda c: (0, 0))
    row_s = pl.BlockSpec((1, N_STATES), lambda c: (0, 0))
    return pl.pallas_call(
        body, name="ssm_fwd", grid=(nc,),
        in_specs=[tok, bsp, bsp, csp, csp, row_w] + [row_s] * 6,
        out_specs=[tok, st, st, ini, ini],
        out_shape=[jax.ShapeDtypeStruct((seq, SSM_W), F32),
                   jax.ShapeDtypeStruct((seq, N_STATES), F32), jax.ShapeDtypeStruct((seq, N_STATES), F32),
                   jax.ShapeDtypeStruct((nc, SUBLANES, N_STATES), F32),
                   jax.ShapeDtypeStruct((nc, SUBLANES, N_STATES), F32)],
        scratch_shapes=[pltpu.VMEM((1, N_STATES), F32), pltpu.VMEM((1, N_STATES), F32),
                        pltpu.VMEM((SUBLANES, N_STATES), F32), pltpu.VMEM((SUBLANES, N_STATES), F32)],
        compiler_params=_params(("arbitrary",)),
    )(u, b_re, b_im, c_re, c_im, d_skip, *coef)


def _ssm_bwd(dy, u, s_re, s_im, i_re, i_im, b_re, b_im, c_re, c_im, d_skip, coef):
    seq = u.shape[0]
    nc = seq // SSM_T
    T, L = SSM_T, SSM_L

    def body(dy_ref, u_ref, sre_ref, sim_ref, ire_ref, iim_ref, bre_ref, bim_ref, cre_ref, cim_ref, d_ref,
             are_ref, aim_ref, cfr_ref, cfi_ref, alr_ref, ali_ref,
             du_ref, dbre_out, dbim_out, dcre_out, dcim_out, dd_ref, dar_ref, dai_ref, dcfr_ref, dcfi_ref,
             lre, lim, car_re, car_im, end_re, end_im, ini_re, ini_im, dbre_ref, dbim_ref, dcre_ref, dcim_ref):
        step = pl.program_id(0)

        @pl.when(step == 0)
        def _():
            car_re[...] = jnp.zeros_like(car_re)
            car_im[...] = jnp.zeros_like(car_im)
            for ref in (dbre_ref, dbim_ref, dcre_ref, dcim_ref, dd_ref, dar_ref, dai_ref, dcfr_ref, dcfi_ref):
                ref[...] = jnp.zeros_like(ref)

        for j in range(SSM_JB):
            dyb = dy_ref[:, j * LANES:(j + 1) * LANES].astype(_MXU)
            lre[:, _scan_cols(j)] = lax.dot_general(dyb, cre_ref[j], _NT, preferred_element_type=F32)
            lim[:, _scan_cols(j)] = -lax.dot_general(dyb, cim_ref[j], _NT, preferred_element_type=F32)

        for j in range(SSM_JB):
            cols = _scan_cols(j)
            ar, ai = _bcast8(are_ref[:, cols]), _bcast8(aim_ref[:, cols])

            def step1(t, s, cols=cols, ar=ar, ai=ai):
                sr, si = s
                rows = _rows8(L - 1 - t)
                return (ar * sr + ai * si + lre[rows, cols], ar * si - ai * sr + lim[rows, cols])

            zero = jnp.zeros((SUBLANES, SSM_SB), F32)
            er, ei = _scan_loop(L, step1, (zero, zero))
            end_re[:, cols] = er
            end_im[:, cols] = ei

        alr, ali = alr_ref[...], ali_ref[...]
        cr, ci = car_re[...], car_im[...]
        ini_re[SUBLANES - 1:SUBLANES, :] = cr
        ini_im[SUBLANES - 1:SUBLANES, :] = ci
        for i in range(SUBLANES - 2, -1, -1):
            er, ei = end_re[i + 1:i + 2, :], end_im[i + 1:i + 2, :]
            cr, ci = alr * cr + ali * ci + er, alr * ci - ali * cr + ei
            ini_re[i:i + 1, :] = cr
            ini_im[i:i + 1, :] = ci

        for j in range(SSM_JB):
            cols = _scan_cols(j)
            ar, ai = _bcast8(are_ref[:, cols]), _bcast8(aim_ref[:, cols])

            def step2(t, s, cols=cols, ar=ar, ai=ai):
                sr, si = s
                rows = _rows8(L - 1 - t)
                nr = ar * sr + ai * si + lre[rows, cols]
                ni = ar * si - ai * sr + lim[rows, cols]
                lre[rows, cols] = nr
                lim[rows, cols] = ni
                return nr, ni

            _scan_loop(L, step2, (ini_re[:, cols], ini_im[:, cols]))

        car_re[...] = lre[0:1, :]
        car_im[...] = lim[0:1, :]

        head, tail, body_rows = slice(0, SUBLANES), slice(SUBLANES, T), slice(0, T - SUBLANES)
        for j in range(SSM_JB):
            cols = _scan_cols(j)
            ch = slice(j * LANES, (j + 1) * LANES)
            lr, li = lre[:, cols], lim[:, cols]
            dar_ref[:, cols] += (_colsum(lre[tail, cols] * sre_ref[body_rows, cols] + lim[tail, cols] * sim_ref[body_rows, cols])
                                 + _colsum(lre[head, cols] * ire_ref[:, cols] + lim[head, cols] * iim_ref[:, cols]))
            dai_ref[:, cols] += (_colsum(lim[tail, cols] * sre_ref[body_rows, cols] - lre[tail, cols] * sim_ref[body_rows, cols])
                                 + _colsum(lim[head, cols] * ire_ref[:, cols] - lre[head, cols] * iim_ref[:, cols]))
            uf = u_ref[:, ch]
            ub = uf.astype(_MXU)
            bur = jnp.dot(ub, bre_ref[j], preferred_element_type=F32)
            bui = jnp.dot(ub, bim_ref[j], preferred_element_type=F32)
            dcfr_ref[:, cols] += _colsum(lr * bur + li * bui)
            dcfi_ref[:, cols] += _colsum(li * bur - lr * bui)
            cfr, cfi = cfr_ref[:, cols], cfi_ref[:, cols]
            dbur = (cfr * lr + cfi * li).astype(_MXU)
            dbui = (cfr * li - cfi * lr).astype(_MXU)
            dyf = dy_ref[:, ch]
            dyb = dyf.astype(_MXU)
            du_ref[:, ch] = (lax.dot_general(dbur, bre_ref[j], _NT, preferred_element_type=F32)
                             + lax.dot_general(dbui, bim_ref[j], _NT, preferred_element_type=F32)
                             + d_ref[:, ch] * dyf)
            dbre_ref[j] += lax.dot_general(ub, dbur, _TN, preferred_element_type=F32)
            dbim_ref[j] += lax.dot_general(ub, dbui, _TN, preferred_element_type=F32)
            dcre_ref[j] += lax.dot_general(sre_ref[:, cols].astype(_MXU), dyb, _TN, preferred_element_type=F32)
            dcim_ref[j] -= lax.dot_general(sim_ref[:, cols].astype(_MXU), dyb, _TN, preferred_element_type=F32)
            dd_ref[:, ch] += _colsum(dyf * uf)

        @pl.when(step == nc - 1)
        def _():
            for acc, out in ((dbre_ref, dbre_out), (dbim_ref, dbim_out), (dcre_ref, dcre_out), (dcim_ref, dcim_out)):
                pltpu.sync_copy(acc, out)

    tok = pl.BlockSpec((T, SSM_W), lambda c: (nc - 1 - c, 0))
    st = pl.BlockSpec((T, N_STATES), lambda c: (nc - 1 - c, 0))
    ini = pl.BlockSpec((None, SUBLANES, N_STATES), lambda c: (nc - 1 - c, 0, 0))
    bsp = pl.BlockSpec((SSM_JB, LANES, SSM_SB), lambda c: (0, 0, 0))
    csp = pl.BlockSpec((SSM_JB, SSM_SB, LANES), lambda c: (0, 0, 0))
    row_w = pl.BlockSpec((1, SSM_W), lambda c: (0, 0))
    row_s = pl.BlockSpec((1, N_STATES), lambda c: (0, 0))
    big = pltpu.VMEM((T, N_STATES), F32)
    one = pltpu.VMEM((1, N_STATES), F32)
    eight = pltpu.VMEM((SUBLANES, N_STATES), F32)
    return pl.pallas_call(
        body, name="ssm_bwd", grid=(nc,),
        in_specs=[tok, tok, st, st, ini, ini, bsp, bsp, csp, csp, row_w] + [row_s] * 6,
        out_specs=[tok, _ANY, _ANY, _ANY, _ANY, row_w, row_s, row_s, row_s, row_s],
        out_shape=[jax.ShapeDtypeStruct((seq, SSM_W), F32),
                   jax.ShapeDtypeStruct((SSM_JB, LANES, SSM_SB), F32), jax.ShapeDtypeStruct((SSM_JB, LANES, SSM_SB), F32),
                   jax.ShapeDtypeStruct((SSM_JB, SSM_SB, LANES), F32), jax.ShapeDtypeStruct((SSM_JB, SSM_SB, LANES), F32),
                   jax.ShapeDtypeStruct((1, SSM_W), F32)] + [jax.ShapeDtypeStruct((1, N_STATES), F32)] * 4,
        scratch_shapes=[big, big, one, one, eight, eight, eight, eight,
                        pltpu.VMEM((SSM_JB, LANES, SSM_SB), F32), pltpu.VMEM((SSM_JB, LANES, SSM_SB), F32),
                        pltpu.VMEM((SSM_JB, SSM_SB, LANES), F32), pltpu.VMEM((SSM_JB, SSM_SB, LANES), F32)],
        compiler_params=_params(("arbitrary",)),
    )(dy, u, s_re, s_im, i_re, i_im, b_re, b_im, c_re, c_im, d_skip, *coef)


def _block_diag_b(b):
    t = b.reshape(SSM_JB, 8, STATE, GROUP).transpose(0, 1, 3, 2)
    eye = jnp.eye(8, dtype=b.dtype)
    return (t[:, :, :, None, :] * eye[None, :, None, :, None]).reshape(SSM_JB, LANES, SSM_SB)


def _block_diag_c(c):
    t = c.reshape(SSM_JB, 8, GROUP, STATE).transpose(0, 1, 3, 2)
    eye = jnp.eye(8, dtype=c.dtype)
    return (t[:, :, :, None, :] * eye[None, :, None, :, None]).reshape(SSM_JB, SSM_SB, LANES)


def _diag_of_b(blk):
    t = blk.reshape(SSM_JB, 8, GROUP, 8, STATE)
    d = jnp.sum(t * jnp.eye(8, dtype=blk.dtype)[None, :, None, :, None], axis=3)
    return d.transpose(0, 1, 3, 2).reshape(N_GROUPS, STATE, GROUP)


def _diag_of_c(blk):
    t = blk.reshape(SSM_JB, 8, STATE, 8, GROUP)
    d = jnp.sum(t * jnp.eye(8, dtype=blk.dtype)[None, :, None, :, None], axis=3)
    return d.transpose(0, 1, 3, 2).reshape(N_GROUPS, GROUP, STATE)


def _to_scan_order(v):
    seq, w = v.shape
    return v.reshape(seq // SSM_T, SUBLANES, SSM_L, w).transpose(0, 2, 1, 3).reshape(seq, w)


def _from_scan_order(v):
    seq, w = v.shape
    return v.reshape(seq // SSM_T, SSM_L, SUBLANES, w).transpose(0, 2, 1, 3).reshape(seq, w)


def _adamw_math(w, g, m, v):
    nm = ADAM_B1 * m + (1.0 - ADAM_B1) * g
    nv = ADAM_B2 * v + (1.0 - ADAM_B2) * jnp.square(g)
    m_hat = nm / (1.0 - ADAM_B1 ** ADAM_STEP)
    v_hat = nv / (1.0 - ADAM_B2 ** ADAM_STEP)
    return -ADAM_LR * (m_hat / (jnp.sqrt(v_hat) + ADAM_EPS) + ADAM_WD * w), nm, nv


def _adamw(w, g, m, v, *, name, tm, deps=()):
    rows, cols = w.shape
    nd = len(deps)

    def body(w_ref, g_ref, m_ref, v_ref, *rest):
        d_ref, nm_ref, nv_ref = rest[nd:]
        d_ref[...], nm_ref[...], nv_ref[...] = _adamw_math(w_ref[...], g_ref[...], m_ref[...], v_ref[...])

    spec = pl.BlockSpec((tm, cols), lambda i: (i, 0))
    shp = jax.ShapeDtypeStruct((rows, cols), F32)
    return pl.pallas_call(body, name=name, grid=(rows // tm,), in_specs=[spec] * 4 + [_ANY] * nd,
                          out_specs=[spec] * 3, out_shape=[shp] * 3,
                          compiler_params=_params(("arbitrary",)))(w, g, m, v, *deps)


def _place():
    x, y, c = lax.axis_index("x"), lax.axis_index("y"), lax.axis_index("c")
    chips = [(1 - x, y), (x, 1 - y), (1 - x, 1 - y)]
    return x, y, c, chips


def _remote(src, dst, send_sem, recv_sem, dev):
    return pltpu.make_async_remote_copy(src_ref=src, dst_ref=dst, send_sem=send_sem, recv_sem=recv_sem,
                                        device_id=dev, device_id_type=MESH)


def _place_shard(w, mine_arr, *, name, tm=256):
    rows, cols = w.shape

    def body(m_ref, w_ref, o_ref):
        o_ref[...] = w_ref[...].astype(o_ref.dtype)

    return pl.pallas_call(
        body, name=name,
        grid_spec=pltpu.PrefetchScalarGridSpec(
            num_scalar_prefetch=1, grid=(rows // tm,),
            in_specs=[pl.BlockSpec((tm, cols), lambda i, m: (i, 0))],
            out_specs=pl.BlockSpec((None, tm, cols), lambda i, m: (m[0], i, 0))),
        out_shape=jax.ShapeDtypeStruct((N_CHIPS, rows, cols), _WIRE),
        compiler_params=_params(("arbitrary",)),
    )(mine_arr, w)


_HBM = pl.BlockSpec(memory_space=pltpu.HBM)
_SEM = pl.BlockSpec(memory_space=pltpu.SEMAPHORE)
_EFFECT = pltpu.SideEffectType.DATAFLOW_SIDE_EFFECTING


def _copies_start(name, bufs, plan, count, after=()):
    nb, na = len(bufs), len(after)

    def body(*refs):
        send_sems, recv_sems, token = refs[nb + na], refs[nb + na + 1], refs[-1]
        copies = plan(refs[:nb])
        assert len(copies) == count
        for i, (src, dst, dev, _) in enumerate(copies):
            _remote(src, dst, send_sems.at[i], recv_sems.at[i], dev).start()
        token[...] = jnp.zeros_like(token)

    res = pl.pallas_call(
        body, name=name, in_specs=[_HBM] * nb + [_ANY] * na,
        out_specs=(_SEM, _SEM, *[_HBM] * nb, pl.BlockSpec(memory_space=pltpu.VMEM)),
        out_shape=(pltpu.SemaphoreType.DMA((count,)), pltpu.SemaphoreType.DMA((count,)),
                   *[pltpu.HBM(b.shape, b.dtype) for b in bufs], jax.ShapeDtypeStruct((SUBLANES, LANES), F32)),
        input_output_aliases={i: 2 + i for i in range(nb)},
        compiler_params=pltpu.CompilerParams(has_side_effects=_EFFECT),
    )(*[pltpu.with_memory_space_constraint(b, pltpu.HBM) for b in bufs], *after)
    return (res[0], res[1]), list(res[2:2 + nb]), res[-1]


def _copies_wait(name, bufs, sems, plan, after=(), which=None):
    nb, na = len(bufs), len(after)

    def body(*refs):
        send_sems, recv_sems = refs[nb], refs[nb + 1]
        for i, (src, _, dev, land) in enumerate(plan(refs[:nb])):
            if which is not None and i not in which:
                continue
            cp = _remote(src, land, send_sems.at[i], recv_sems.at[i], dev)
            cp.wait_send()
            cp.wait_recv()

    res = pl.pallas_call(
        body, name=name, in_specs=[_HBM] * nb + [_SEM, _SEM] + [_ANY] * na, out_specs=[_HBM] * nb,
        out_shape=[pltpu.HBM(b.shape, b.dtype) for b in bufs],
        input_output_aliases={i: i for i in range(nb)},
        compiler_params=pltpu.CompilerParams(has_side_effects=_EFFECT),
    )(*bufs, *sems, *after)
    return list(res)


def _plan_gather_ici(fulls, which=(0, 1, 2)):
    x, y, c, chips = _place()
    copies = []
    for f in fulls:
        half = pl.ds(c * (f.shape[1] // 2), f.shape[1] // 2)
        own = f.at[2 * x + y, half]
        for chip in [chips[k] for k in which]:
            copies.append((own, own, (*chip, c), f.at[2 * chip[0] + chip[1], half]))
    return copies


def _plan_gather_d2d(fulls, which=(0, 1, 2)):
    x, y, c, chips = _place()
    copies = []
    for f in fulls:
        r2 = f.shape[1] // 2
        for chip in [chips[k] for k in which]:
            blk = 2 * chip[0] + chip[1]
            landed = f.at[blk, pl.ds(c * r2, r2)]
            copies.append((landed, landed, (x, y, 1 - c), f.at[blk, pl.ds((1 - c) * r2, r2)]))
    return copies


def _plan_relay_direct(fulls):
    (f,) = fulls
    x, y, c, chips = _place()
    half = pl.ds(c * (f.shape[1] // 2), f.shape[1] // 2)
    own = f.at[2 * x + y, half]
    return [(own, own, (*chip, c), f.at[2 * chip[0] + chip[1], half]) for chip in chips[:2]]


def _plan_relay_forward(fulls, k):
    (f,) = fulls
    x, y, c, chips = _place()
    r2 = f.shape[1] // 2
    half, other = pl.ds(c * r2, r2), pl.ds((1 - c) * r2, r2)
    quarter = pl.ds(c * r2 + k * (r2 // 2), r2 // 2)
    blk, far = 2 * chips[k][0] + chips[k][1], 2 * chips[2][0] + chips[2][1]
    passed, landed = f.at[blk, quarter], f.at[blk, half]
    return [(passed, passed, (*chips[1 - k], c), f.at[far, quarter]), (landed, landed, (x, y, 1 - c), f.at[blk, other])]


def _plan_relay_last(fulls):
    (f,) = fulls
    x, y, c, chips = _place()
    r2 = f.shape[1] // 2
    far = 2 * chips[2][0] + chips[2][1]
    landed = f.at[far, pl.ds(c * r2, r2)]
    return [(landed, landed, (x, y, 1 - c), f.at[far, pl.ds((1 - c) * r2, r2)])]


def _plan_swap_halves(refs):
    x, y, c, _ = _place()
    n = len(refs) // 2
    copies = []
    for g, land in zip(refs[:n], refs[n:]):
        r2 = g.shape[1] // 2
        copies.append((g.at[:, pl.ds((1 - c) * r2, r2), :], land, (x, y, 1 - c), land))
    return copies


def _plan_scatter_chips(refs):
    x, y, c, chips = _place()
    n = len(refs) // 2
    copies = []
    for h, land in zip(refs[:n], refs[n:]):
        for k, chip in enumerate(chips):
            copies.append((h.at[2 * chip[0] + chip[1]], land.at[k], (*chip, c), land.at[k]))
    return copies


def _plan_join_halves(totals):
    x, y, c, _ = _place()
    copies = []
    for t in totals:
        r2 = t.shape[0] // 2
        mine = t.at[pl.ds(c * r2, r2)]
        copies.append((mine, mine, (x, y, 1 - c), t.at[pl.ds((1 - c) * r2, r2)]))
    return copies


def _add_sibling_half(g, got, c_arr, *, name, tm):
    _, rows, cols = g.shape
    r2 = rows // 2
    nb = r2 // tm

    def body(c_ref, g_ref, r_ref, o_ref):
        o_ref[...] = (g_ref[...].astype(F32) + r_ref[...].astype(F32)).astype(o_ref.dtype)

    return pl.pallas_call(
        body, name=name,
        grid_spec=pltpu.PrefetchScalarGridSpec(
            num_scalar_prefetch=1, grid=(N_CHIPS, nb),
            in_specs=[pl.BlockSpec((None, tm, cols), lambda b, i, c: (b, c[0] * nb + i, 0)),
                      pl.BlockSpec((None, tm, cols), lambda b, i, c: (b, i, 0))],
            out_specs=pl.BlockSpec((None, tm, cols), lambda b, i, c: (b, i, 0))),
        out_shape=jax.ShapeDtypeStruct((N_CHIPS, r2, cols), _WIRE),
        compiler_params=_params(("arbitrary", "arbitrary")),
    )(c_arr, g, got)


def _add_chips(h, got, place_arr, *, name, tm):
    _, r2, cols = h.shape
    nb = r2 // tm

    def body(p_ref, h_ref, r_ref, o_ref):
        o_ref[...] = ((h_ref[...].astype(F32) + r_ref[0].astype(F32)) + r_ref[1].astype(F32)) + r_ref[2].astype(F32)

    return pl.pallas_call(
        body, name=name,
        grid_spec=pltpu.PrefetchScalarGridSpec(
            num_scalar_prefetch=1, grid=(nb,),
            in_specs=[pl.BlockSpec((None, tm, cols), lambda i, p: (p[0], i, 0)),
                      pl.BlockSpec((3, tm, cols), lambda i, p: (0, i, 0))],
            out_specs=pl.BlockSpec((tm, cols), lambda i, p: (p[1] * nb + i, 0))),
        out_shape=jax.ShapeDtypeStruct((2 * r2, cols), F32),
        compiler_params=_params(("arbitrary",)),
    )(place_arr, h, got)


class _ReduceScatter:
    def __init__(self, tag, names, grads):
        self.tag, self.names, self.n = tag, names, len(names)
        core = lax.axis_index("c").astype(jnp.int32)
        chip = (2 * lax.axis_index("x") + lax.axis_index("y")).astype(jnp.int32)
        self.c_arr, self.place_arr = core.reshape(1), jnp.stack([chip, core])
        self.bufs = list(grads)

    def _start(self, step, bufs, plan, count, after):
        self.plan = plan
        self.step = f"grad_{step}_{self.tag}"
        self.sems, self.bufs, token = _copies_start(self.step + "_start", bufs, plan, count, after)
        return [token]

    def _wait(self, after):
        self.bufs = _copies_wait(self.step + "_wait", self.bufs, self.sems, self.plan, after)
        return self.bufs

    def start_swap(self, after=()):
        lands = [lax.empty((N_CHIPS, g.shape[1] // 2, g.shape[2]), g.dtype) for g in self.bufs]
        return self._start("swap", self.bufs + lands, _plan_swap_halves, self.n, after)

    def start_scatter(self, after):
        bufs = self._wait(after)
        pair = [_add_sibling_half(g, r, self.c_arr, name=f"grad_add_sibling_{nm}", tm=min(256, g.shape[1] // 2))
                for nm, g, r in zip(self.names, bufs[:self.n], bufs[self.n:])]
        lands = [lax.empty((3,) + h.shape[1:], h.dtype) for h in pair]
        return self._start("scatter", pair + lands, _plan_scatter_chips, 3 * self.n, ())

    def start_join(self, after):
        bufs = self._wait(after)
        total = [_add_chips(h, r, self.place_arr, name=f"grad_add_chips_{nm}", tm=min(256, h.shape[1]))
                 for nm, h, r in zip(self.names, bufs[:self.n], bufs[self.n:])]
        return self._start("join", total, _plan_join_halves, self.n, ())

    def finish(self, after):
        return dict(zip(self.names, self._wait(after)))


def _all_gather_small(v):
    m_per, n = v.shape

    def body(x_ref, out_ref, send_sems, recv_sems, local_sem):
        x, y, c, chips = _place()
        me, sibling = (x, y, c), (x, y, 1 - c)

        def rows(px, py, pc):
            return out_ref.at[4 * px + 2 * py + pc]

        def copy(k, block, to, src=None):
            return _remote(rows(*block) if src is None else src, rows(*block), send_sems.at[k], recv_sems.at[k], to)

        mine = pltpu.make_async_copy(x_ref, rows(*me), local_sem)
        mine.start()
        first = [copy(0, me, sibling, src=x_ref)]
        first += [copy(1 + j, me, (*chip, c), src=x_ref) for j, chip in enumerate(chips)]
        for cp in first:
            cp.start()
        passed = [copy(4 + j, (*chip, c), sibling) for j, chip in enumerate(chips)]
        for j, chip in enumerate(chips):
            copy(1 + j, (*chip, c), me).wait_recv()
            passed[j].start()
        copy(0, sibling, me).wait_recv()
        for j, chip in enumerate(chips):
            copy(4 + j, (*chip, 1 - c), me).wait_recv()
        for cp in first + passed:
            cp.wait_send()
        mine.wait()

    return pl.pallas_call(
        body, name="gather_small_grads",
        out_shape=jax.ShapeDtypeStruct((8, m_per, n), v.dtype),
        in_specs=[pl.BlockSpec(memory_space=pltpu.VMEM)], out_specs=pl.BlockSpec(memory_space=pltpu.VMEM),
        scratch_shapes=[pltpu.SemaphoreType.DMA((7,)), pltpu.SemaphoreType.DMA((7,)), pltpu.SemaphoreType.DMA],
        compiler_params=pltpu.CompilerParams(vmem_limit_bytes=VMEM_LIMIT),
    )(v)


def _sum8(v, *, name):
    _, m, n = v.shape

    def body(v_ref, o_ref):
        acc = v_ref[0]
        for d in range(1, 8):
            acc = acc + v_ref[d]
        o_ref[...] = acc

    return pl.pallas_call(body, name=name, out_shape=jax.ShapeDtypeStruct((m, n), F32),
                          compiler_params=pltpu.CompilerParams(vmem_limit_bytes=VMEM_LIMIT))(v)


def _local_step(x, target, norm_w, q_norm_w, k_norm_w, sinks, a_re, a_im, log_dt, b_re, b_im, c_re, c_im, d_skip,
                b_glu, io):
    seq = x.shape[0]
    qw2 = jnp.tile(q_norm_w.reshape(1, HEAD_DIM), (1, HEADS_PER_TILE))
    kw2 = jnp.tile(k_norm_w.reshape(1, HEAD_DIM), (1, HEADS_PER_TILE))
    nw, bg = norm_w.reshape(1, D_MODEL), b_glu.reshape(1, D_MODEL)
    dsk = d_skip.reshape(1, SSM_W)

    h, rstd = _rms_fwd(x, nw, deps=io.begin())
    proj, w_in4 = io.projection(h)
    attn, lse = _attn2_fwd(proj, qw2, kw2, sinks, deps=io.after_proj(proj))

    def gate_a(at, ag):
        return (at * (ag * _sigmoid(ag)),)

    (ya_in,) = _ew(gate_a, [(attn, "mat", 0), (proj, "mat", OFF_AGATE)], [(ATTN_W, _MXU)], rows=seq, ncol=2,
                   name="ew_attn_gate")
    w_ap4 = io.weight("w_attn_proj", ya_in)
    w_glu4, w_sp4, w_out = io.weight("w_glu", ya_in), io.weight("w_ssm_proj", ya_in), io.weight("w_out", ya_in)
    y_a = _mm(ya_in, w_ap4, mode="nn", name="mm_attn_proj", tm=2048, tn=512, tk=ATTN_W, b_blocked=True)

    flat_a = (a_re.reshape(1, N_STATES), a_im.reshape(1, N_STATES), jnp.repeat(log_dt, STATE).reshape(1, N_STATES))
    coef = _ssm_params_fwd(*flat_a)
    bre_blk, bim_blk = _block_diag_b(b_re).astype(_MXU), _block_diag_b(b_im).astype(_MXU)
    cre_blk, cim_blk = _block_diag_c(c_re).astype(_MXU), _block_diag_c(c_im).astype(_MXU)
    u_scan = _to_scan_order(proj[:, OFF_U * CW:OFF_U * CW + SSM_W])
    y_scan, s_re, s_im, i_re, i_im = _ssm_fwd(u_scan, bre_blk, bim_blk, cre_blk, cim_blk, dsk, coef)
    y_ssm = _from_scan_order(y_scan)

    (yg,) = _ew(lambda yv: (jax.nn.gelu(yv),), [(y_ssm, "mat", 0)], [(SSM_W, _MXU)], rows=seq, ncol=2, name="ew_gelu")
    glu = _mm(yg, w_glu4, mode="nn", name="mm_glu", tm=2048, tn=512, tk=SSM_W, b_blocked=True)

    def gate_s(ga, gb, ba, bb, z):
        return ((ga + ba) * _sigmoid(gb + bb) * (z * _sigmoid(z)),)

    (ys_in,) = _ew(gate_s, [(glu, "mat", 0), (glu, "mat", 2), (bg, "row", 0), (bg, "row", 2), (proj, "mat", OFF_Z)],
                   [(SSM_W, _MXU)], rows=seq, ncol=2, name="ew_ssm_gate")
    y_s = _mm(ys_in, w_sp4, mode="nn", name="mm_ssm_proj", tm=2048, tn=512, tk=SSM_W, b_blocked=True)

    def merge(ga, gs, ya, ys):
        return (_sigmoid(ga) * ya + _sigmoid(gs) * ys,)

    (merged,) = _ew(merge, [(proj, "mat", OFF_GA), (proj, "mat", OFF_GS), (y_a, "mat", 0), (y_s, "mat", 0)],
                    [(D_MODEL, _MXU)], rows=seq, ncol=4, name="ew_merge")
    dout, dout_b, sq = _mm_out_loss(merged, w_out, x, target)
    loss = 0.5 * jnp.sum(sq) / D_MODEL

    d_ya, d_ys, d_ga, d_gs = _mm_merge_bwd(dout_b, w_out, proj, y_a, y_s)
    g_w_out = _mm(merged, dout_b, mode="tn", name="mm_g_w_out", tm=1024, tn=D_MODEL, tk=1024, out_dtype=_WIRE)

    d_ya_in = _mm(d_ya, w_ap4, mode="nt", name="mm_d_attn_gate", tm=2048, tn=ATTN_W, tk=512, b_blocked=True)
    g_w_ap = _mm(ya_in, d_ya, mode="tn", name="mm_g_w_attn_proj", tm=ATTN_W, tn=512, tk=2048, out_dtype=_WIRE,
                 out_blocked=True)

    d_ys_in = _mm(d_ys, w_sp4, mode="nt", name="mm_d_ssm_gate", tm=2048, tn=SSM_W, tk=512, b_blocked=True)
    g_w_sp = _mm(ys_in, d_ys, mode="tn", name="mm_g_w_ssm_proj", tm=SSM_W, tn=512, tk=2048, out_dtype=_WIRE,
                 out_blocked=True)

    def gate_s_bwd(dv, ga, gb, ba, bb, z):
        a, sb = ga + ba, _sigmoid(gb + bb)
        f, df = _silu_and_grad(z)
        dga = dv * sb * f
        dgb = dv * a * f * sb * (1.0 - sb)
        return dga, dgb, dv * a * sb * df, _colsum(dga), _colsum(dgb)

    d_glu_a, d_glu_b, d_z, g_bga, g_bgb = _ew(
        gate_s_bwd, [(d_ys_in, "mat", 0), (glu, "mat", 0), (glu, "mat", 2), (bg, "row", 0), (bg, "row", 2),
                     (proj, "mat", OFF_Z)],
        [(SSM_W, _MXU)] * 3, rows=seq, ncol=2, n_acc=2, name="ew_ssm_gate_bwd")
    d_glu = jnp.concatenate([d_glu_a, d_glu_b], axis=1)
    d_yg = _mm(d_glu, w_glu4, mode="nt", name="mm_d_gelu", tm=2048, tn=SSM_W, tk=512, b_blocked=True)
    g_w_glu = _mm(yg, d_glu, mode="tn", name="mm_g_w_glu", tm=SSM_W, tn=512, tk=2048, out_dtype=_WIRE, out_blocked=True)
    dep = io.later_grads(dict(w_attn_proj=g_w_ap, w_glu=g_w_glu, w_ssm_proj=g_w_sp,
                              w_out=g_w_out.reshape(N_CHIPS, D_MODEL // N_CHIPS, D_MODEL)))

    def gate_a_bwd(dv, at, ag):
        f, df = _silu_and_grad(ag)
        return dv * f, dv * at * df

    d_attn, d_agate = _ew(gate_a_bwd, [(d_ya_in, "mat", 0), (attn, "mat", 0), (proj, "mat", OFF_AGATE)],
                          [(ATTN_W, F32), (ATTN_W, _MXU)], rows=seq, ncol=2, name="ew_attn_gate_bwd", deps=dep)

    def gelu_bwd(dv, yv):
        return (jax.vjp(jax.nn.gelu, yv)[1](dv)[0],)

    (d_yssm,) = _ew(gelu_bwd, [(d_yg, "mat", 0), (y_ssm, "mat", 0)], [(SSM_W, F32)], rows=seq, ncol=2, name="ew_gelu_bwd",
                    deps=dep)
    dep = io.before_attention_backward([d_attn, d_yssm])
    d_qkv, g_qw2, g_kw2, g_sk = _attn2_bwd(proj, qw2, kw2, sinks, lse, d_attn, deps=dep)
    (du_scan, g_bre, g_bim, g_cre, g_cim, g_dsk, g_abr, g_abi, g_cfr, g_cfi) = _ssm_bwd(
        _to_scan_order(d_yssm), u_scan, s_re, s_im, i_re, i_im, bre_blk, bim_blk, cre_blk, cim_blk, dsk, coef)
    g_are, g_aim, g_ldt = _ssm_params_bwd(*flat_a, g_abr, g_abi, g_cfr, g_cfi)
    g_are, g_aim = g_are.reshape(N_GROUPS, STATE), g_aim.reshape(N_GROUPS, STATE)
    g_ldt = g_ldt.reshape(N_GROUPS, STATE).sum(axis=1)
    d_u = _from_scan_order(du_scan)

    d_proj = jnp.concatenate([d_qkv, d_agate, d_u.astype(_MXU), d_z, d_ga, d_gs], axis=1)
    dep = io.before_input_projection_grad([d_proj]) + io.small_grads(dict(
        q_norm_w=g_qw2[0, :HEAD_DIM] + g_qw2[0, HEAD_DIM:], k_norm_w=g_kw2[0, :HEAD_DIM] + g_kw2[0, HEAD_DIM:],
        sinks=g_sk.reshape(N_Q_HEADS), A_re=g_are, A_im=g_aim, log_dt=g_ldt,
        B_re=_diag_of_b(g_bre), B_im=_diag_of_b(g_bim), C_re=_diag_of_c(g_cre), C_im=_diag_of_c(g_cim),
        D_skip=g_dsk.reshape(N_GROUPS, GROUP), b_glu=jnp.concatenate([g_bga, g_bgb], axis=1).reshape(D_MODEL)))
    g_w_in = _mm(h, d_proj, mode="tn", name="mm_g_w_in", tm=1024, tn=IN_W // 4, tk=1024, out_dtype=_WIRE,
                 out_blocked=True, deps=dep)
    dep = io.input_projection_grad(g_w_in)
    d_h = _mm(d_proj, w_in4, mode="nt", name="mm_d_h", tm=512, tn=D_MODEL, tk=IN_W // 4, b_blocked=True, deps=dep)
    grad_x, g_nw = _rms_bwd(d_h, x, rstd, nw, dout)
    return loss, grad_x, g_nw.reshape(D_MODEL)


_SMALL = ["norm_w", "q_norm_w", "k_norm_w", "sinks", "A_re", "A_im", "log_dt", "B_re", "B_im", "C_re", "C_im",
          "D_skip", "b_glu"]
_BIG = ["w_in", "w_attn_proj", "w_glu", "w_ssm_proj", "w_out"]
_LATER = _BIG[1:]
_RELATIONS = ("flip_x", "flip_y", "flip_xy")
_ORDER = ["norm_w", "w_in", "q_norm_w", "k_norm_w", "sinks", "w_attn_proj", "A_re", "A_im", "log_dt", "B_re", "B_im",
          "C_re", "C_im", "D_skip", "w_glu", "b_glu", "w_ssm_proj", "w_out"]
_PACK_W = 1024


def _packed_rows(size):
    unit = SUBLANES * _PACK_W
    return -(-size // unit) * SUBLANES


def _pack_small(d, names):
    parts = []
    for n in names:
        flat = d[n].reshape(-1).astype(F32)
        rows = _packed_rows(flat.shape[0])
        parts.append(jnp.pad(flat, (0, rows * _PACK_W - flat.shape[0])).reshape(rows, _PACK_W))
    return jnp.concatenate(parts, axis=0)


def _unpack_small(packed, like, names):
    out, pos = {}, 0
    for n in names:
        rows = _packed_rows(like[n].size)
        out[n] = packed[pos:pos + rows].reshape(-1)[:like[n].size].reshape(like[n].shape)
        pos += rows
    return out


def _place_block(v, index_arr, *, name):
    rows, cols = v.shape

    def body(i_ref, v_ref, o_ref):
        o_ref[...] = v_ref[...]

    return pl.pallas_call(
        body, name=name,
        grid_spec=pltpu.PrefetchScalarGridSpec(
            num_scalar_prefetch=1, grid=(1,),
            in_specs=[pl.BlockSpec((rows, cols), lambda i, d: (0, 0))],
            out_specs=pl.BlockSpec((None, rows, cols), lambda i, d: (d[0], 0, 0))),
        out_shape=jax.ShapeDtypeStruct((8, rows, cols), v.dtype),
        compiler_params=_params(("arbitrary",)),
    )(index_arr, v)


def _plan_all_to_all(refs):
    (land,) = refs
    x, y, c, _ = _place()
    own = land.at[4 * x + 2 * y + c]
    copies = []
    for fx, fy, fc in [(0, 0, 1), (0, 1, 0), (0, 1, 1), (1, 0, 0), (1, 0, 1), (1, 1, 0), (1, 1, 1)]:
        px, py, pc = (1 - x) if fx else x, (1 - y) if fy else y, (1 - c) if fc else c
        copies.append((own, own, (px, py, pc), land.at[4 * px + 2 * py + pc]))
    return copies


def _as2d(a):
    return a.reshape(1, -1) if a.ndim == 1 else a


def _adamw_whole(w, g, m, v, *, name):
    shape = w.shape
    w, g, m, v = _as2d(w), _as2d(g), _as2d(m), _as2d(v)

    def body(w_ref, g_ref, m_ref, v_ref, d_ref, nm_ref, nv_ref):
        d_ref[...], nm_ref[...], nv_ref[...] = _adamw_math(w_ref[...], g_ref[...], m_ref[...], v_ref[...])

    outs = pl.pallas_call(body, name=name, out_shape=[jax.ShapeDtypeStruct(w.shape, F32)] * 3)(w, g, m, v)
    return [o.reshape(shape) for o in outs]


class _Exchanges:
    def __init__(self, w, m, v):
        self.w, self.m, self.v = w, m, v
        self.grads, self.delta, self.new_m, self.new_v = {}, {}, {}, {}

    def _adamw(self, names, deps):
        for n in names:
            self.delta[n], self.new_m[n], self.new_v[n] = _adamw(
                self.w[n], self.grads[n], self.m[n], self.v[n], name=f"adamw_{n}", tm=128, deps=deps)

    def begin(self):
        chip = (2 * lax.axis_index("x") + lax.axis_index("y")).astype(jnp.int32).reshape(1)
        full = {n: _place_shard(self.w[n], chip, name=f"place_{n}") for n in _BIG}
        self.later_full = [full[n] for n in _LATER]
        self.w_in_sems, self.w_in_buf, token = _copies_start("gather_w_in_direct_start", [full["w_in"]],
                                                             _plan_relay_direct, 2)
        return [token]

    def projection(self, h):
        x, y = lax.axis_index("x"), lax.axis_index("y")
        blks = [jnp.asarray(b, jnp.int32).reshape(1)
                for b in (2 * x + y, 2 * (1 - x) + y, 2 * x + (1 - y), 2 * (1 - x) + (1 - y))]
        bufs = self.w_in_buf
        proj = _mm_chip_block(h, bufs[0], blks[0], None, name="mm_proj_own")
        relay, token = [], proj
        for k, tag in enumerate(_RELATIONS[:2]):
            bufs = _copies_wait(f"gather_w_in_direct_{tag}_wait", bufs, self.w_in_sems, _plan_relay_direct, [token],
                                which=(k,))
            plan = functools.partial(_plan_relay_forward, k=k)
            sems, bufs, token = _copies_start(f"gather_w_in_relay_{tag}_start", bufs, plan, 2)
            relay.append((sems, plan))
        self.rest = _copies_start("gather_ici_rest_start", self.later_full, _plan_gather_ici, 3 * len(_LATER),
                                  after=[token])
        token = self.rest[2]
        for k, tag in enumerate(_RELATIONS[:2]):
            bufs = _copies_wait(f"gather_w_in_hand_{tag}_wait", bufs, relay[k][0], relay[k][1], [token], which=(1,))
            token = proj = _mm_chip_block(h, bufs[0], blks[1 + k], proj, name=f"mm_proj_{tag}")
        for k, tag in enumerate(_RELATIONS[:2]):
            bufs = _copies_wait(f"gather_w_in_relay_{tag}_wait", bufs, relay[k][0], relay[k][1], [token], which=(0,))
        sems, bufs, token = _copies_start("gather_w_in_last_start", bufs, _plan_relay_last, 1)
        bufs = _copies_wait("gather_w_in_last_wait", bufs, sems, _plan_relay_last, [token])
        proj = _mm_chip_block(h, bufs[0], blks[3], proj, name="mm_proj_flip_xy")
        return proj, bufs[0]

    def weight(self, name, after):
        if self.rest is not None:
            sems, bufs = self.rest
            later = dict(zip(_LATER, _copies_wait("gather_d2d_rest_wait", bufs, sems, _plan_gather_d2d, [after])))
            later["w_out"] = later["w_out"].reshape(D_MODEL, D_MODEL)
            self.later, self.rest = later, None
        return self.later[name]

    def after_proj(self, proj):
        sems, bufs, _ = self.rest
        bufs = _copies_wait("gather_ici_rest_wait", bufs, sems, _plan_gather_ici, [proj])
        sems, bufs, token = _copies_start("gather_d2d_rest_start", bufs, _plan_gather_d2d, 3 * len(_LATER))
        self.rest = (sems, bufs)
        return [token]

    def later_grads(self, grads):
        self.rs_later = _ReduceScatter("later", _LATER, [grads[n] for n in _LATER])
        return self.rs_later.start_swap()

    def before_attention_backward(self, after):
        return self.rs_later.start_scatter(after)

    def before_input_projection_grad(self, after):
        return self.rs_later.start_join(after)

    def input_projection_grad(self, g_w_in):
        self.grads.update(self.rs_later.finish([g_w_in]))
        self.rs_in = _ReduceScatter("w_in", ["w_in"], [g_w_in])
        self._adamw(_LATER, self.rs_in.start_swap())
        return self.rs_in.start_scatter([self.delta[n] for n in _LATER])

    def _adamw_small(self, names):
        for n in names:
            self.delta[n], self.new_m[n], self.new_v[n] = _adamw_whole(
                self.w[n], self.grads[n], self.m[n], self.v[n], name=f"adamw_{n}")

    def small_grads(self, grads):
        me = (4 * lax.axis_index("x") + 2 * lax.axis_index("y") + lax.axis_index("c")).astype(jnp.int32).reshape(1)
        land = _place_block(_pack_small(grads, _SMALL[1:]), me, name="place_small_grads")
        self.small = _copies_start("gather_small_start", [land], _plan_all_to_all, 7)
        return [self.small[2]]

    def finish(self, g_norm_w, loss, after):
        sems, bufs, _ = self.small
        (land,) = _copies_wait("gather_small_wait", bufs, sems, _plan_all_to_all, after)
        self.grads.update(_unpack_small(_sum8(land, name="sum_small_grads"), self.w, _SMALL[1:]))
        self._adamw_small(_SMALL[1:])
        rows = _packed_rows(g_norm_w.size)
        late = jnp.concatenate([_pack_small(dict(norm_w=g_norm_w), _SMALL[:1]),
                                jnp.pad(loss.reshape(1, 1), ((0, SUBLANES - 1), (0, _PACK_W - 1)))], axis=0)
        late = _sum8(_all_gather_small(late), name="sum_norm_w_grad_and_loss")
        self.grads.update(_unpack_small(late[:rows], self.w, _SMALL[:1]))
        self._adamw_small(_SMALL[:1])
        self.grads.update(self.rs_in.finish(self.rs_in.start_join([self.delta[_SMALL[0]]])))
        self._adamw(["w_in"], ())
        return late[rows, 0]


def kernel(x, norm_w, w_in, q_norm_w, k_norm_w, sinks, w_attn_proj, A_re, A_im, log_dt, B_re, B_im, C_re, C_im, D_skip, w_glu, b_glu, w_ssm_proj, w_out, loss_target, m_norm_w, m_w_in, m_q_norm_w, m_k_norm_w, m_sinks, m_w_attn_proj, m_A_re, m_A_im, m_log_dt, m_B_re, m_B_im, m_C_re, m_C_im, m_D_skip, m_w_glu, m_b_glu, m_w_ssm_proj, m_w_out, v_norm_w, v_w_in, v_q_norm_w, v_k_norm_w, v_sinks, v_w_attn_proj, v_A_re, v_A_im, v_log_dt, v_B_re, v_B_im, v_C_re, v_C_im, v_D_skip, v_w_glu, v_b_glu, v_w_ssm_proj, v_w_out):
    w = dict(norm_w=norm_w, w_in=w_in, q_norm_w=q_norm_w, k_norm_w=k_norm_w, sinks=sinks, w_attn_proj=w_attn_proj,
             A_re=A_re, A_im=A_im, log_dt=log_dt, B_re=B_re, B_im=B_im, C_re=C_re, C_im=C_im, D_skip=D_skip,
             w_glu=w_glu, b_glu=b_glu, w_ssm_proj=w_ssm_proj, w_out=w_out)
    m = dict(norm_w=m_norm_w, w_in=m_w_in, q_norm_w=m_q_norm_w, k_norm_w=m_k_norm_w, sinks=m_sinks,
             w_attn_proj=m_w_attn_proj, A_re=m_A_re, A_im=m_A_im, log_dt=m_log_dt, B_re=m_B_re, B_im=m_B_im,
             C_re=m_C_re, C_im=m_C_im, D_skip=m_D_skip, w_glu=m_w_glu, b_glu=m_b_glu, w_ssm_proj=m_w_ssm_proj,
             w_out=m_w_out)
    v = dict(norm_w=v_norm_w, w_in=v_w_in, q_norm_w=v_q_norm_w, k_norm_w=v_k_norm_w, sinks=v_sinks,
             w_attn_proj=v_w_attn_proj, A_re=v_A_re, A_im=v_A_im, log_dt=v_log_dt, B_re=v_B_re, B_im=v_B_im,
             C_re=v_C_re, C_im=v_C_im, D_skip=v_D_skip, w_glu=v_w_glu, b_glu=v_b_glu, w_ssm_proj=v_w_ssm_proj,
             w_out=v_w_out)

    io = _Exchanges(w, m, v)
    loss, grad_x, g_norm_w = _local_step(x[0], loss_target[0], norm_w, q_norm_w, k_norm_w, sinks, A_re, A_im, log_dt,
                                         B_re, B_im, C_re, C_im, D_skip, b_glu, io)
    loss = io.finish(g_norm_w, loss, [grad_x])
    grads, delta, new_m, new_v = io.grads, io.delta, io.new_m, io.new_v

    return (loss, grad_x[None], *[grads[n] for n in _ORDER], *[delta[n] for n in _ORDER],
            *[new_m[n] for n in _ORDER], *[new_v[n] for n in _ORDER])
```

```python
import functools
import math

import jax
import jax.numpy as jnp
from jax import lax
from jax.experimental import pallas as pl
from jax.experimental.pallas import tpu as pltpu

F32 = jnp.float32
_MXU = jnp.bfloat16
_WIRE = jnp.bfloat16

LANES = 128
SUBLANES = 8
VMEM_LIMIT = 56 * 1024 * 1024

D_MODEL = 2048
HEAD_DIM = 64
N_Q_HEADS = 16
N_KV_HEADS = 4
Q_PER_KV = 4
ATTN_W = 1024
KV_W = 256
WINDOW = 128
SSM_W = 1024
GROUP = 16
N_GROUPS = 64
STATE = 64
N_STATES = N_GROUPS * STATE
IN_W = 8704
NORM_EPS = 1e-6
N_CHIPS = 4
CW = 512
OFF_AGATE, OFF_U, OFF_Z, OFF_GA, OFF_GS = 3, 5, 7, 9, 13

SSM_T = 256
SSM_L = SSM_T // SUBLANES
SSM_JB = 8
SSM_SB = N_STATES // SSM_JB

ADAM_LR, ADAM_B1, ADAM_B2, ADAM_EPS, ADAM_WD, ADAM_STEP = 0.001, 0.9, 0.999, 1e-08, 0.01, 10

MESH = pl.DeviceIdType.MESH
_ANY = pl.BlockSpec(memory_space=pl.ANY)


def _params(sem=None):
    return pltpu.CompilerParams(dimension_semantics=sem, vmem_limit_bytes=VMEM_LIMIT)


def _mm(a, b, *, mode, name, tm, tn, tk, out_dtype=F32, b_blocked=False, out_blocked=False, rows_outer=False,
        deps=()):
    nd = len(deps)
    if mode == "tn":
        K, M = a.shape
    else:
        M, K = a.shape
    if mode == "nn":
        N = b.shape[0] * b.shape[2] if b_blocked else b.shape[1]
    elif mode == "nt":
        N = b.shape[1] if b_blocked else b.shape[0]
    else:
        N = b.shape[1]
    tm, tn, tk = min(tm, M), min(tn, N), min(tk, K)
    nj, ni, nk = N // tn, M // tm, K // tk
    assert nj * tn == N and ni * tm == M and nk * tk == K, (name, M, N, K)
    dims = {"nn": (((1,), (0,)), ((), ())), "nt": (((1,), (1,)), ((), ())), "tn": (((0,), (0,)), ((), ()))}[mode]

    if mode == "tn":
        a_spec = pl.BlockSpec((tk, tm), lambda j, i, k: (k, i))
    else:
        a_spec = pl.BlockSpec((tm, tk), lambda j, i, k: (i, k))
    if mode == "nn":
        if b_blocked:
            assert b.shape[0] == nj and b.shape[2] == tn
            b_spec = pl.BlockSpec((None, tk, tn), lambda j, i, k: (j, k, 0))
        else:
            b_spec = pl.BlockSpec((tk, tn), lambda j, i, k: (k, j))
    elif mode == "nt":
        if b_blocked:
            assert b.shape[0] == nk and b.shape[2] == tk
            b_spec = pl.BlockSpec((None, tn, tk), lambda j, i, k: (k, j, 0))
        else:
            b_spec = pl.BlockSpec((tn, tk), lambda j, i, k: (j, k))
    else:
        b_spec = pl.BlockSpec((tk, tn), lambda j, i, k: (k, j))
    whole_out = out_blocked and nj == 1
    if whole_out:
        assert ni == 1
        o_spec = pl.BlockSpec((N_CHIPS, tm, tn // N_CHIPS), lambda j, i, k: (0, 0, 0))
        o_shape = jax.ShapeDtypeStruct((N_CHIPS, M, tn // N_CHIPS), out_dtype)
    elif out_blocked:
        assert nj == N_CHIPS
        o_spec = pl.BlockSpec((None, tm, tn), lambda j, i, k: (j, i, 0))
        o_shape = jax.ShapeDtypeStruct((nj, M, tn), out_dtype)
    else:
        o_spec = pl.BlockSpec((tm, tn), lambda j, i, k: (i, j))
        o_shape = jax.ShapeDtypeStruct((M, N), out_dtype)
    use_acc = nk > 1 and (out_dtype != F32 or whole_out)

    def body(a_ref, b_ref, *rest):
        o_ref, scratch = rest[nd], rest[nd + 1:]

        def product():
            return lax.dot_general(a_ref[...].astype(_MXU), b_ref[...].astype(_MXU), dims, preferred_element_type=F32)

        def write(result):
            if whole_out:
                w = tn // N_CHIPS
                for c in range(N_CHIPS):
                    o_ref[c] = result[:, c * w:(c + 1) * w].astype(o_ref.dtype)
            else:
                o_ref[...] = result.astype(o_ref.dtype)

        if nk == 1:
            write(product())
            return
        k = pl.program_id(2)
        acc = scratch[0] if use_acc else o_ref

        @pl.when(k == 0)
        def _():
            acc[...] = jnp.zeros_like(acc)

        acc[...] += product()

        if use_acc:
            @pl.when(k == nk - 1)
            def _():
                write(acc[...])

    specs = [a_spec, b_spec, o_spec]
    grid = (nj, ni, nk)
    if rows_outer:
        specs = [pl.BlockSpec(s.block_shape, lambda i, j, k, f=s.index_map: f(j, i, k)) for s in specs]
        grid = (ni, nj, nk)
    return pl.pallas_call(
        body, name=name, grid=grid, in_specs=specs[:2] + [_ANY] * nd, out_specs=specs[2],
        out_shape=o_shape, scratch_shapes=[pltpu.VMEM((tm, tn), F32)] if use_acc else [],
        compiler_params=_params(("parallel", "parallel", "arbitrary")),
    )(a, b, *deps)


def _mm_chip_block(a, b4, blk, prev, *, name, tm=512, deps=()):
    M, K = a.shape
    nchip, _, C = b4.shape
    tm = min(tm, M)
    extra = ([] if prev is None else [prev]) + list(deps)

    def body(blk_ref, a_ref, b_ref, *rest):
        rest[-1][...] = jnp.dot(a_ref[...].astype(_MXU), b_ref[...].astype(_MXU), preferred_element_type=F32)

    return pl.pallas_call(
        body, name=name,
        grid_spec=pltpu.PrefetchScalarGridSpec(
            num_scalar_prefetch=1, grid=(M // tm,),
            in_specs=[pl.BlockSpec((tm, K), lambda i, c: (i, 0)), pl.BlockSpec((None, K, C), lambda i, c: (c[0], 0, 0))]
            + [_ANY] * len(extra),
            out_specs=pl.BlockSpec((tm, C), lambda i, c: (i, c[0]))),
        out_shape=jax.ShapeDtypeStruct((M, nchip * C), F32),
        input_output_aliases={} if prev is None else {3: 0},
        compiler_params=_params(("arbitrary",)),
    )(blk, a, b4, *extra)


def _mm_out_loss(merged, w_out, x, target, *, tm=256):
    rows, d = x.shape

    def body(m_ref, w_ref, x_ref, t_ref, d_ref, db_ref, sq_ref):
        mo = jnp.dot(m_ref[...].astype(_MXU), w_ref[...].astype(_MXU), preferred_element_type=F32)
        err = (x_ref[...] + mo) - t_ref[...]
        dout = err * (1.0 / d)
        d_ref[...] = dout
        db_ref[...] = dout.astype(db_ref.dtype)
        part = _colsum(err * err)
        i = pl.program_id(0)

        @pl.when(i == 0)
        def _():
            sq_ref[...] = part

        @pl.when(i > 0)
        def _():
            sq_ref[...] += part

    tile = pl.BlockSpec((tm, d), lambda i: (i, 0))
    return pl.pallas_call(
        body, name="mm_out_loss", grid=(rows // tm,),
        in_specs=[tile, pl.BlockSpec((d, d), lambda i: (0, 0)), tile, tile],
        out_specs=[tile, tile, pl.BlockSpec((1, d), lambda i: (0, 0))],
        out_shape=[jax.ShapeDtypeStruct((rows, d), F32), jax.ShapeDtypeStruct((rows, d), _MXU),
                   jax.ShapeDtypeStruct((1, d), F32)],
        compiler_params=_params(("arbitrary",)),
    )(merged, w_out, x, target)


def _mm_merge_bwd(dout_b, w_out, proj, y_a, y_s, *, tm=256):
    rows, d = y_a.shape
    ncol = d // CW

    def body(do_ref, w_ref, *refs):
        ga_refs, gs_refs = refs[:ncol], refs[ncol:2 * ncol]
        ya_ref, ys_ref, dya_ref, dys_ref, dga_ref, dgs_ref = refs[2 * ncol:]
        dm = lax.dot_general(do_ref[...].astype(_MXU), w_ref[...].astype(_MXU), _NT, preferred_element_type=F32)
        for j in range(ncol):
            cols = slice(j * CW, (j + 1) * CW)
            dmj = dm[:, cols]
            sa, ss = _sigmoid(ga_refs[j][...]), _sigmoid(gs_refs[j][...])
            dya_ref[:, cols] = (sa * dmj).astype(dya_ref.dtype)
            dys_ref[:, cols] = (ss * dmj).astype(dys_ref.dtype)
            dga_ref[:, cols] = (dmj * ya_ref[:, cols] * sa * (1.0 - sa)).astype(dga_ref.dtype)
            dgs_ref[:, cols] = (dmj * ys_ref[:, cols] * ss * (1.0 - ss)).astype(dgs_ref.dtype)

    tile = pl.BlockSpec((tm, d), lambda i: (i, 0))
    gate = [pl.BlockSpec((tm, CW), lambda i, c=off + j: (i, c)) for off in (OFF_GA, OFF_GS) for j in range(ncol)]
    return pl.pallas_call(
        body, name="mm_merge_bwd", grid=(rows // tm,),
        in_specs=[tile, pl.BlockSpec((d, d), lambda i: (0, 0))] + gate + [tile, tile],
        out_specs=[tile] * 4, out_shape=[jax.ShapeDtypeStruct((rows, d), _MXU)] * 4,
        compiler_params=_params(("arbitrary",)),
    )(dout_b, w_out, *([proj] * (2 * ncol)), y_a, y_s)


def _ew(fn, ins, outs, *, rows, ncol, name, n_acc=0, tm=512, deps=()):
    n_in, n_out, nd = len(ins), len(outs), len(deps)
    tm = min(tm, rows)
    in_specs = []
    for _, kind, col0 in ins:
        if kind == "mat":
            in_specs.append(pl.BlockSpec((tm, CW), lambda j, i, c0=col0: (i, c0 + j)))
        else:
            in_specs.append(pl.BlockSpec((1, CW), lambda j, i, c0=col0: (0, c0 + j)))
    out_specs = [pl.BlockSpec((tm, CW), lambda j, i: (i, j)) for _ in outs]
    out_shape = [jax.ShapeDtypeStruct((rows, w), dt) for w, dt in outs]
    for _ in range(n_acc):
        out_specs.append(pl.BlockSpec((1, CW), lambda j, i: (0, j)))
        out_shape.append(jax.ShapeDtypeStruct((1, ncol * CW), F32))

    def body(*refs):
        vals = fn(*[r[...] for r in refs[:n_in]])
        refs = refs[n_in + nd:]
        for r, v in zip(refs[:n_out], vals[:n_out]):
            r[...] = v.astype(r.dtype)
        i = pl.program_id(1)
        for r, v in zip(refs[n_out:], vals[n_out:]):
            @pl.when(i == 0)
            def _(r=r, v=v):
                r[...] = v

            @pl.when(i > 0)
            def _(r=r, v=v):
                r[...] += v

    res = pl.pallas_call(
        body, name=name, grid=(ncol, rows // tm), in_specs=in_specs + [_ANY] * nd, out_specs=out_specs,
        out_shape=out_shape, compiler_params=_params(("parallel", "arbitrary")),
    )(*[a for a, _, _ in ins], *deps)
    return res


def _colsum(v):
    return jnp.sum(v, axis=0, keepdims=True)


def _sigmoid(v):
    return jax.nn.sigmoid(v)


def _silu_and_grad(v):
    s = _sigmoid(v)
    return v * s, s * (1.0 + v * (1.0 - s))


def _rms_fwd(x, w, *, tm=512, deps=()):
    rows, d = x.shape
    nd = len(deps)

    def body(x_ref, w_ref, *rest):
        h_ref, r_ref = rest[nd:]
        xv = x_ref[...]
        r = lax.rsqrt(jnp.mean(xv * xv, axis=-1, keepdims=True) + NORM_EPS)
        h_ref[...] = (xv * r * w_ref[...]).astype(h_ref.dtype)
        r_ref[...] = r

    return pl.pallas_call(
        body, name="rms_fwd", grid=(rows // tm,),
        in_specs=[pl.BlockSpec((tm, d), lambda i: (i, 0)), pl.BlockSpec((1, d), lambda i: (0, 0))] + [_ANY] * nd,
        out_specs=[pl.BlockSpec((tm, d), lambda i: (i, 0)), pl.BlockSpec((tm, 1), lambda i: (i, 0))],
        out_shape=[jax.ShapeDtypeStruct((rows, d), _MXU), jax.ShapeDtypeStruct((rows, 1), F32)],
        compiler_params=_params(("arbitrary",)),
    )(x, w, *deps)


def _rms_bwd(dh, x, rstd, w, dout, *, tm=256):
    rows, d = x.shape

    def body(dh_ref, x_ref, r_ref, w_ref, do_ref, gx_ref, gw_ref):
        dhv, xv, r, wv = dh_ref[...], x_ref[...], r_ref[...], w_ref[...]
        xr = xv * r
        t = jnp.mean(dhv * wv * xr, axis=-1, keepdims=True)
        gx_ref[...] = do_ref[...] + r * (wv * dhv - xr * t)
        part = _colsum(dhv * xr)
        i = pl.program_id(0)

        @pl.when(i == 0)
        def _():
            gw_ref[...] = part

        @pl.when(i > 0)
        def _():
            gw_ref[...] += part

    return pl.pallas_call(
        body, name="rms_bwd", grid=(rows // tm,),
        in_specs=[pl.BlockSpec((tm, d), lambda i: (i, 0)), pl.BlockSpec((tm, d), lambda i: (i, 0)),
                  pl.BlockSpec((tm, 1), lambda i: (i, 0)), pl.BlockSpec((1, d), lambda i: (0, 0)),
                  pl.BlockSpec((tm, d), lambda i: (i, 0))],
        out_specs=[pl.BlockSpec((tm, d), lambda i: (i, 0)), pl.BlockSpec((1, d), lambda i: (0, 0))],
        out_shape=[jax.ShapeDtypeStruct((rows, d), F32), jax.ShapeDtypeStruct((1, d), F32)],
        compiler_params=_params(("arbitrary",)),
    )(dh, x, rstd, w, dout)


_NT = (((1,), (1,)), ((), ()))
_TN = (((0,), (0,)), ((), ()))


QKV_W = ATTN_W + 2 * KV_W
HEADS_PER_TILE = LANES // HEAD_DIM


def _low_half(rows):
    return lax.broadcasted_iota(jnp.int32, (rows, LANES), 1) < HEAD_DIM


def _pair_mean(t, low):
    m_lo = jnp.sum(jnp.where(low, t, 0.0), axis=-1, keepdims=True)
    m_hi = jnp.sum(jnp.where(low, 0.0, t), axis=-1, keepdims=True)
    return jnp.where(low, m_lo, m_hi) * (1.0 / HEAD_DIM)


def _pair_rstd(t, low):
    return lax.rsqrt(_pair_mean(t * t, low) + NORM_EPS)


def _dup_half(t, hi, low):
    swapped = pltpu.roll(t, HEAD_DIM, 1)
    return jnp.where(low, swapped, t) if hi else jnp.where(low, t, swapped)


def _fold_halves(t):
    return t + pltpu.roll(t, HEAD_DIM, 1)


def _split_heads(t, low):
    return [jnp.where(low, t, 0.0), jnp.where(low, 0.0, t)]


def _stacked_band_mask(n):
    rows = Q_PER_KV * WINDOW
    qi = lax.broadcasted_iota(jnp.int32, (rows, 2 * WINDOW), 0) % WINDOW + WINDOW
    kj = lax.broadcasted_iota(jnp.int32, (rows, 2 * WINDOW), 1)
    diff = qi - kj
    first_key = jnp.where(n > 0, 0, WINDOW)
    return (diff >= 0) & (diff < WINDOW) & (kj >= first_key)


def _stacked_sinks(sink_ref, g):
    blk = lax.broadcasted_iota(jnp.int32, (Q_PER_KV * WINDOW, 1), 0) // WINDOW
    col = jnp.full((Q_PER_KV * WINDOW, 1), sink_ref[Q_PER_KV * g], F32)
    for r in range(1, Q_PER_KV):
        col = jnp.where(blk == r, sink_ref[Q_PER_KV * g + r], col)
    return col


def _attn_in_specs(nblk, rev):
    def cur(n):
        return (nblk - 1 - n) if rev else n

    q_spec = pl.BlockSpec((WINDOW, ATTN_W), lambda n: (cur(n), 0))
    kvc_spec = pl.BlockSpec((WINDOW, 2 * KV_W), lambda n: (cur(n), ATTN_W // (2 * KV_W)))
    kvp_spec = pl.BlockSpec((WINDOW, 2 * KV_W), lambda n: (jnp.maximum(cur(n) - 1, 0), ATTN_W // (2 * KV_W)))
    w_spec = pl.BlockSpec((1, LANES), lambda n: (0, 0))
    l_spec = pl.BlockSpec((WINDOW, N_Q_HEADS), lambda n: (cur(n), 0))
    return q_spec, kvc_spec, kvp_spec, w_spec, l_spec


def _attn2_fwd(proj, qw2, kw2, sinks, deps=()):
    seq = proj.shape[0]
    nblk = seq // WINDOW
    scale = 1.0 / math.sqrt(HEAD_DIM)
    q_spec, kvc_spec, kvp_spec, w_spec, l_spec = _attn_in_specs(nblk, False)
    nd = len(deps)

    def body(sink_ref, q_ref, kvc_ref, kvp_ref, qw_ref, kw_ref, *rest):
        o_ref, lse_ref = rest[nd:]
        n = pl.program_id(0)
        low, low2 = _low_half(WINDOW), _low_half(2 * WINDOW)
        valid = _stacked_band_mask(n)
        head_lane = lax.broadcasted_iota(jnp.int32, (WINDOW, N_Q_HEADS), 1)
        kv = jnp.concatenate([kvp_ref[...], kvc_ref[...]], axis=0)
        qwv, kwv = qw_ref[...], kw_ref[...]
        lse_blk = jnp.zeros((WINDOW, N_Q_HEADS), F32)
        for t in range(N_KV_HEADS // HEADS_PER_TILE):
            kt = kv[:, t * LANES:(t + 1) * LANES]
            vt = kv[:, KV_W + t * LANES:KV_W + (t + 1) * LANES]
            kn = kt * _pair_rstd(kt, low2) * kwv
            for hi in range(HEADS_PER_TILE):
                g = HEADS_PER_TILE * t + hi
                kdup = _dup_half(kn, hi, low2).astype(_MXU)
                vdup = _dup_half(vt, hi, low2).astype(_MXU)
                stack = []
                for tq in (2 * g, 2 * g + 1):
                    qt = q_ref[:, tq * LANES:(tq + 1) * LANES]
                    stack += _split_heads(qt * _pair_rstd(qt, low) * qwv, low)
                qs = jnp.concatenate(stack, axis=0).astype(_MXU)
                s = lax.dot_general(qs, kdup, _NT, preferred_element_type=F32) * scale
                s = jnp.where(valid, s, -1e30)
                sink = _stacked_sinks(sink_ref, g)
                m = jnp.maximum(jnp.max(s, axis=-1, keepdims=True), sink)
                e = jnp.exp(s - m)
                z = jnp.sum(e, axis=-1, keepdims=True) + jnp.exp(sink - m)
                o = jnp.dot((e / z).astype(_MXU), vdup, preferred_element_type=F32)
                for i, tq in enumerate((2 * g, 2 * g + 1)):
                    o_ref[:, tq * LANES:(tq + 1) * LANES] = jnp.where(
                        low, o[2 * i * WINDOW:(2 * i + 1) * WINDOW], o[(2 * i + 1) * WINDOW:(2 * i + 2) * WINDOW])
                lse = m + jnp.log(z)
                for r in range(Q_PER_KV):
                    lse_blk = jnp.where(head_lane == Q_PER_KV * g + r, lse[r * WINDOW:(r + 1) * WINDOW], lse_blk)
        lse_ref[...] = lse_blk

    return pl.pallas_call(
        body, name="attn_fwd", grid=(nblk,),
        in_specs=[pl.BlockSpec(memory_space=pltpu.SMEM), q_spec, kvc_spec, kvp_spec, w_spec, w_spec] + [_ANY] * nd,
        out_specs=[q_spec, l_spec],
        out_shape=[jax.ShapeDtypeStruct((seq, ATTN_W), F32), jax.ShapeDtypeStruct((seq, N_Q_HEADS), F32)],
        compiler_params=_params(("arbitrary",)),
    )(sinks, proj, proj, proj, qw2, kw2, *deps)


def _attn2_bwd(proj, qw2, kw2, sinks, lse, do, deps=()):
    seq = proj.shape[0]
    nblk = seq // WINDOW
    scale = 1.0 / math.sqrt(HEAD_DIM)
    q_spec, kvc_spec, kvp_spec, w_spec, l_spec = _attn_in_specs(nblk, True)
    s_spec = pl.BlockSpec((1, N_Q_HEADS), lambda n: (0, 0))
    d_spec = pl.BlockSpec((WINDOW, QKV_W), lambda n: (nblk - 1 - n, 0))
    nd = len(deps)

    def body(sink_ref, q_ref, kvc_ref, kvp_ref, qw_ref, kw_ref, lse_ref, do_ref, *rest):
        d_ref, dqw_ref, dkw_ref, dsk_ref, carry = rest[nd:]
        step = pl.program_id(0)
        n = nblk - 1 - step

        @pl.when(step == 0)
        def _():
            carry[...] = jnp.zeros_like(carry)
            dqw_ref[...] = jnp.zeros_like(dqw_ref)
            dkw_ref[...] = jnp.zeros_like(dkw_ref)
            dsk_ref[...] = jnp.zeros_like(dsk_ref)

        low, low2 = _low_half(WINDOW), _low_half(2 * WINDOW)
        valid = _stacked_band_mask(n)
        head_lane = lax.broadcasted_iota(jnp.int32, (WINDOW, N_Q_HEADS), 1)
        sink_lane = lax.broadcasted_iota(jnp.int32, (1, N_Q_HEADS), 1)
        kv = jnp.concatenate([kvp_ref[...], kvc_ref[...]], axis=0)
        qwv, kwv = qw_ref[...], kw_ref[...]
        lse_blk = lse_ref[...]
        dqw = jnp.zeros((1, LANES), F32)
        dkw = jnp.zeros((1, LANES), F32)
        dsk = jnp.zeros((1, N_Q_HEADS), F32)
        for t in range(N_KV_HEADS // HEADS_PER_TILE):
            kt = kv[:, t * LANES:(t + 1) * LANES]
            vt = kv[:, KV_W + t * LANES:KV_W + (t + 1) * LANES]
            rk = _pair_rstd(kt, low2)
            kn = kt * rk * kwv
            dkn_t = jnp.zeros((2 * WINDOW, LANES), F32)
            dv_t = jnp.zeros((2 * WINDOW, LANES), F32)
            for hi in range(HEADS_PER_TILE):
                g = HEADS_PER_TILE * t + hi
                kdup = _dup_half(kn, hi, low2).astype(_MXU)
                vdup = _dup_half(vt, hi, low2).astype(_MXU)
                tiles = (2 * g, 2 * g + 1)
                qx, rq, stack, dstack, lse_rows = [], [], [], [], []
                for tq in tiles:
                    qt = q_ref[:, tq * LANES:(tq + 1) * LANES]
                    r = _pair_rstd(qt, low)
                    rq.append(r)
                    qx.append(qt * r)
                    stack += _split_heads(qx[-1] * qwv, low)
                    dstack += _split_heads(do_ref[:, tq * LANES:(tq + 1) * LANES], low)
                for r in range(Q_PER_KV):
                    lse_rows.append(jnp.sum(jnp.where(head_lane == Q_PER_KV * g + r, lse_blk, 0.0), axis=-1, keepdims=True))
                qs = jnp.concatenate(stack, axis=0).astype(_MXU)
                dos = jnp.concatenate(dstack, axis=0).astype(_MXU)
                lse_col = jnp.concatenate(lse_rows, axis=0)
                s = lax.dot_general(qs, kdup, _NT, preferred_element_type=F32) * scale
                s = jnp.where(valid, s, -1e30)
                p = jnp.exp(s - lse_col)
                dp = lax.dot_general(dos, vdup, _NT, preferred_element_type=F32)
                dsum = jnp.sum(p * dp, axis=-1, keepdims=True)
                ds = (p * (dp - dsum) * scale).astype(_MXU)
                dsink = -jnp.exp(_stacked_sinks(sink_ref, g) - lse_col) * dsum
                for r in range(Q_PER_KV):
                    dsk = dsk + jnp.where(sink_lane == Q_PER_KV * g + r, _colsum(dsink[r * WINDOW:(r + 1) * WINDOW]), 0.0)
                dv_g = _fold_halves(lax.dot_general(p.astype(_MXU), dos, _TN, preferred_element_type=F32))
                dkn_g = _fold_halves(lax.dot_general(ds, qs, _TN, preferred_element_type=F32))
                dv_t = jnp.where(low2, dv_t, dv_g) if hi else jnp.where(low2, dv_g, dv_t)
                dkn_t = jnp.where(low2, dkn_t, dkn_g) if hi else jnp.where(low2, dkn_g, dkn_t)
                dqn = jnp.dot(ds, kdup, preferred_element_type=F32)
                for i, tq in enumerate(tiles):
                    dqn_t = jnp.where(low, dqn[2 * i * WINDOW:(2 * i + 1) * WINDOW],
                                      dqn[(2 * i + 1) * WINDOW:(2 * i + 2) * WINDOW])
                    dq = rq[i] * (qwv * dqn_t - qx[i] * _pair_mean(dqn_t * qwv * qx[i], low))
                    d_ref[:, tq * LANES:(tq + 1) * LANES] = dq.astype(d_ref.dtype)
                    dqw = dqw + _colsum(dqn_t * qx[i])
            k_cols = slice(t * LANES, (t + 1) * LANES)
            v_cols = slice(KV_W + t * LANES, KV_W + (t + 1) * LANES)
            dkn_c = dkn_t[WINDOW:] + carry[:, k_cols]
            rc = rk[WINDOW:]
            kx = kt[WINDOW:] * rc
            dk = rc * (kwv * dkn_c - kx * _pair_mean(dkn_c * kwv * kx, low))
            d_ref[:, ATTN_W + t * LANES:ATTN_W + (t + 1) * LANES] = dk.astype(d_ref.dtype)
            d_ref[:, ATTN_W + KV_W + t * LANES:ATTN_W + KV_W + (t + 1) * LANES] = (
                dv_t[WINDOW:] + carry[:, v_cols]).astype(d_ref.dtype)
            carry[:, k_cols] = dkn_t[:WINDOW]
            carry[:, v_cols] = dv_t[:WINDOW]
            dkw = dkw + _colsum(dkn_c * kx)
        dqw_ref[...] += dqw
        dkw_ref[...] += dkw
        dsk_ref[...] += dsk

    return pl.pallas_call(
        body, name="attn_bwd", grid=(nblk,),
        in_specs=[pl.BlockSpec(memory_space=pltpu.SMEM), q_spec, kvc_spec, kvp_spec, w_spec, w_spec, l_spec, q_spec]
        + [_ANY] * nd,
        out_specs=[d_spec, w_spec, w_spec, s_spec],
        out_shape=[jax.ShapeDtypeStruct((seq, QKV_W), _MXU), jax.ShapeDtypeStruct((1, LANES), F32),
                   jax.ShapeDtypeStruct((1, LANES), F32), jax.ShapeDtypeStruct((1, N_Q_HEADS), F32)],
        scratch_shapes=[pltpu.VMEM((WINDOW, 2 * KV_W), F32)],
        compiler_params=_params(("arbitrary",)),
    )(sinks, proj, proj, proj, qw2, kw2, lse, do, *deps)


def _ssm_discretise(a_re, a_im, log_dt):
    dt = jnp.exp(log_dt)
    mag = jnp.exp(dt * a_re)
    ab_re = mag * jnp.cos(dt * a_im)
    ab_im = mag * jnp.sin(dt * a_im)
    num_re = ab_re - 1.0
    num_im = ab_im
    den = a_re * a_re + a_im * a_im
    cf_re = (num_re * a_re + num_im * a_im) / den
    cf_im = (num_im * a_re - num_re * a_im) / den
    return ab_re, ab_im, cf_re, cf_im


def _ssm_params_fwd(a_re, a_im, log_dt):
    shp = jax.ShapeDtypeStruct(a_re.shape, F32)

    def body(are_ref, aim_ref, ldt_ref, abr_ref, abi_ref, cfr_ref, cfi_ref, alr_ref, ali_ref):
        abr, abi, cfr, cfi = _ssm_discretise(are_ref[...], aim_ref[...], ldt_ref[...])
        abr_ref[...], abi_ref[...], cfr_ref[...], cfi_ref[...] = abr, abi, cfr, cfi
        pr, pi = abr, abi
        for _ in range(int(math.log2(SSM_L))):
            pr, pi = pr * pr - pi * pi, 2.0 * pr * pi
        alr_ref[...], ali_ref[...] = pr, pi

    return pl.pallas_call(body, name="ssm_params_fwd", out_shape=[shp] * 6)(a_re, a_im, log_dt)


def _ssm_params_bwd(a_re, a_im, log_dt, d_abr, d_abi, d_cfr, d_cfi):
    def body(are_ref, aim_ref, ldt_ref, g0, g1, g2, g3, dare_ref, daim_ref, dldt_ref):
        _, vjp = jax.vjp(_ssm_discretise, are_ref[...], aim_ref[...], ldt_ref[...])
        dare_ref[...], daim_ref[...], dldt_ref[...] = vjp((g0[...], g1[...], g2[...], g3[...]))

    return pl.pallas_call(
        body, name="ssm_params_bwd",
        out_shape=[jax.ShapeDtypeStruct(a_re.shape, F32), jax.ShapeDtypeStruct(a_im.shape, F32),
                   jax.ShapeDtypeStruct(log_dt.shape, F32)],
    )(a_re, a_im, log_dt, d_abr, d_abi, d_cfr, d_cfi)


def _scan_cols(j):
    return pl.ds(j * SSM_SB, SSM_SB)


def _rows8(r):
    return pl.ds(pl.multiple_of(r * SUBLANES, SUBLANES), SUBLANES)


def _bcast8(row):
    return jnp.broadcast_to(row, (SUBLANES, row.shape[-1]))


SCAN_UNROLL = 8


def _scan_loop(n, step, init):
    def trip(o, carry):
        for i in range(SCAN_UNROLL):
            carry = step(o * SCAN_UNROLL + i, carry)
        return carry

    return lax.fori_loop(0, n // SCAN_UNROLL, trip, init)


def _ssm_fwd(u, b_re, b_im, c_re, c_im, d_skip, coef):
    seq = u.shape[0]
    nc = seq // SSM_T
    T, L = SSM_T, SSM_L

    def body(u_ref, bre_ref, bim_ref, cre_ref, cim_ref, d_ref, are_ref, aim_ref, cfr_ref, cfi_ref, alr_ref, ali_ref,
             y_ref, sre_ref, sim_ref, ire_ref, iim_ref, car_re, car_im, end_re, end_im):
        c = pl.program_id(0)

        @pl.when(c == 0)
        def _():
            car_re[...] = jnp.zeros_like(car_re)
            car_im[...] = jnp.zeros_like(car_im)

        for j in range(SSM_JB):
            ub = u_ref[:, j * LANES:(j + 1) * LANES].astype(_MXU)
            bur = jnp.dot(ub, bre_ref[j], preferred_element_type=F32)
            bui = jnp.dot(ub, bim_ref[j], preferred_element_type=F32)
            cfr, cfi = cfr_ref[:, _scan_cols(j)], cfi_ref[:, _scan_cols(j)]
            sre_ref[:, _scan_cols(j)] = cfr * bur - cfi * bui
            sim_ref[:, _scan_cols(j)] = cfr * bui + cfi * bur

        for j in range(SSM_JB):
            cols = _scan_cols(j)
            ar, ai = _bcast8(are_ref[:, cols]), _bcast8(aim_ref[:, cols])

            def step1(r, s, cols=cols, ar=ar, ai=ai):
                sr, si = s
                rows = _rows8(r)
                return (ar * sr - ai * si + sre_ref[rows, cols], ar * si + ai * sr + sim_ref[rows, cols])

            zero = jnp.zeros((SUBLANES, SSM_SB), F32)
            er, ei = _scan_loop(L, step1, (zero, zero))
            end_re[:, cols] = er
            end_im[:, cols] = ei

        alr, ali = alr_ref[...], ali_ref[...]
        cr, ci = car_re[...], car_im[...]
        ire_ref[0:1, :] = cr
        iim_ref[0:1, :] = ci
        for i in range(1, SUBLANES):
            er, ei = end_re[i - 1:i, :], end_im[i - 1:i, :]
            cr, ci = alr * cr - ali * ci + er, alr * ci + ali * cr + ei
            ire_ref[i:i + 1, :] = cr
            iim_ref[i:i + 1, :] = ci

        for j in range(SSM_JB):
            cols = _scan_cols(j)
            ar, ai = _bcast8(are_ref[:, cols]), _bcast8(aim_ref[:, cols])

            def step2(r, s, cols=cols, ar=ar, ai=ai):
                sr, si = s
                rows = _rows8(r)
                nr = ar * sr - ai * si + sre_ref[rows, cols]
                ni = ar * si + ai * sr + sim_ref[rows, cols]
                sre_ref[rows, cols] = nr
                sim_ref[rows, cols] = ni
                return nr, ni

            _scan_loop(L, step2, (ire_ref[:, cols], iim_ref[:, cols]))

        car_re[...] = sre_ref[T - 1:T, :]
        car_im[...] = sim_ref[T - 1:T, :]

        for j in range(SSM_JB):
            cols = _scan_cols(j)
            ch = slice(j * LANES, (j + 1) * LANES)
            y = (jnp.dot(sre_ref[:, cols].astype(_MXU), cre_ref[j], preferred_element_type=F32)
                 - jnp.dot(sim_ref[:, cols].astype(_MXU), cim_ref[j], preferred_element_type=F32))
            y_ref[:, ch] = y + d_ref[:, ch] * u_ref[:, ch]

    tok = pl.BlockSpec((T, SSM_W), lambda c: (c, 0))
    st = pl.BlockSpec((T, N_STATES), lambda c: (c, 0))
    ini = pl.BlockSpec((None, SUBLANES, N_STATES), lambda c: (c, 0, 0))
    bsp = pl.BlockSpec((SSM_JB, LANES, SSM_SB), lambda c: (0, 0, 0))
    csp = pl.BlockSpec((SSM_JB, SSM_SB, LANES), lambda c: (0, 0, 0))
    row_w = pl.BlockSpec((1, SSM_W), lambda c: (0, 0))
    row_s = pl.BlockSpec((1, N_STATES), lambda c: (0, 0))
    return pl.pallas_call(
        body, name="ssm_fwd", grid=(nc,),
        in_specs=[tok, bsp, bsp, csp, csp, row_w] + [row_s] * 6,
        out_specs=[tok, st, st, ini, ini],
        out_shape=[jax.ShapeDtypeStruct((seq, SSM_W), F32),
                   jax.ShapeDtypeStruct((seq, N_STATES), F32), jax.ShapeDtypeStruct((seq, N_STATES), F32),
                   jax.ShapeDtypeStruct((nc, SUBLANES, N_STATES), F32),
                   jax.ShapeDtypeStruct((nc, SUBLANES, N_STATES), F32)],
        scratch_shapes=[pltpu.VMEM((1, N_STATES), F32), pltpu.VMEM((1, N_STATES), F32),
                        pltpu.VMEM((SUBLANES, N_STATES), F32), pltpu.VMEM((SUBLANES, N_STATES), F32)],
        compiler_params=_params(("arbitrary",)),
    )(u, b_re, b_im, c_re, c_im, d_skip, *coef)


def _ssm_bwd(dy, u, s_re, s_im, i_re, i_im, b_re, b_im, c_re, c_im, d_skip, coef):
    seq = u.shape[0]
    nc = seq // SSM_T
    T, L = SSM_T, SSM_L

    def body(dy_ref, u_ref, sre_ref, sim_ref, ire_ref, iim_ref, bre_ref, bim_ref, cre_ref, cim_ref, d_ref,
             are_ref, aim_ref, cfr_ref, cfi_ref, alr_ref, ali_ref,
             du_ref, dbre_out, dbim_out, dcre_out, dcim_out, dd_ref, dar_ref, dai_ref, dcfr_ref, dcfi_ref,
             lre, lim, car_re, car_im, end_re, end_im, ini_re, ini_im, dbre_ref, dbim_ref, dcre_ref, dcim_ref):
        step = pl.program_id(0)

        @pl.when(step == 0)
        def _():
            car_re[...] = jnp.zeros_like(car_re)
            car_im[...] = jnp.zeros_like(car_im)
            for ref in (dbre_ref, dbim_ref, dcre_ref, dcim_ref, dd_ref, dar_ref, dai_ref, dcfr_ref, dcfi_ref):
                ref[...] = jnp.zeros_like(ref)

        for j in range(SSM_JB):
            dyb = dy_ref[:, j * LANES:(j + 1) * LANES].astype(_MXU)
            lre[:, _scan_cols(j)] = lax.dot_general(dyb, cre_ref[j], _NT, preferred_element_type=F32)
            lim[:, _scan_cols(j)] = -lax.dot_general(dyb, cim_ref[j], _NT, preferred_element_type=F32)

        for j in range(SSM_JB):
            cols = _scan_cols(j)
            ar, ai = _bcast8(are_ref[:, cols]), _bcast8(aim_ref[:, cols])

            def step1(t, s, cols=cols, ar=ar, ai=ai):
                sr, si = s
                rows = _rows8(L - 1 - t)
                return (ar * sr + ai * si + lre[rows, cols], ar * si - ai * sr + lim[rows, cols])

            zero = jnp.zeros((SUBLANES, SSM_SB), F32)
            er, ei = _scan_loop(L, step1, (zero, zero))
            end_re[:, cols] = er
            end_im[:, cols] = ei

        alr, ali = alr_ref[...], ali_ref[...]
        cr, ci = car_re[...], car_im[...]
        ini_re[SUBLANES - 1:SUBLANES, :] = cr
        ini_im[SUBLANES - 1:SUBLANES, :] = ci
        for i in range(SUBLANES - 2, -1, -1):
            er, ei = end_re[i + 1:i + 2, :], end_im[i + 1:i + 2, :]
            cr, ci = alr * cr + ali * ci + er, alr * ci - ali * cr + ei
            ini_re[i:i + 1, :] = cr
            ini_im[i:i + 1, :] = ci

        for j in range(SSM_JB):
            cols = _scan_cols(j)
            ar, ai = _bcast8(are_ref[:, cols]), _bcast8(aim_ref[:, cols])

            def step2(t, s, cols=cols, ar=ar, ai=ai):
                sr, si = s
                rows = _rows8(L - 1 - t)
                nr = ar * sr + ai * si + lre[rows, cols]
                ni = ar * si - ai * sr + lim[rows, cols]
                lre[rows, cols] = nr
                lim[rows, cols] = ni
                return nr, ni

            _scan_loop(L, step2, (ini_re[:, cols], ini_im[:, cols]))

        car_re[...] = lre[0:1, :]
        car_im[...] = lim[0:1, :]

        head, tail, body_rows = slice(0, SUBLANES), slice(SUBLANES, T), slice(0, T - SUBLANES)
        for j in range(SSM_JB):
            cols = _scan_cols(j)
            ch = slice(j * LANES, (j + 1) * LANES)
            lr, li = lre[:, cols], lim[:, cols]
            dar_ref[:, cols] += (_colsum(lre[tail, cols] * sre_ref[body_rows, cols] + lim[tail, cols] * sim_ref[body_rows, cols])
                                 + _colsum(lre[head, cols] * ire_ref[:, cols] + lim[head, cols] * iim_ref[:, cols]))
            dai_ref[:, cols] += (_colsum(lim[tail, cols] * sre_ref[body_rows, cols] - lre[tail, cols] * sim_ref[body_rows, cols])
                                 + _colsum(lim[head, cols] * ire_ref[:, cols] - lre[head, cols] * iim_ref[:, cols]))
            uf = u_ref[:, ch]
            ub = uf.astype(_MXU)
            bur = jnp.dot(ub, bre_ref[j], preferred_element_type=F32)
            bui = jnp.dot(ub, bim_ref[j], preferred_element_type=F32)
            dcfr_ref[:, cols] += _colsum(lr * bur + li * bui)
            dcfi_ref[:, cols] += _colsum(li * bur - lr * bui)
            cfr, cfi = cfr_ref[:, cols], cfi_ref[:, cols]
            dbur = (cfr * lr + cfi * li).astype(_MXU)
            dbui = (cfr * li - cfi * lr).astype(_MXU)
            dyf = dy_ref[:, ch]
            dyb = dyf.astype(_MXU)
            du_ref[:, ch] = (lax.dot_general(dbur, bre_ref[j], _NT, preferred_element_type=F32)
                             + lax.dot_general(dbui, bim_ref[j], _NT, preferred_element_type=F32)
                             + d_ref[:, ch] * dyf)
            dbre_ref[j] += lax.dot_general(ub, dbur, _TN, preferred_element_type=F32)
            dbim_ref[j] += lax.dot_general(ub, dbui, _TN, preferred_element_type=F32)
            dcre_ref[j] += lax.dot_general(sre_ref[:, cols].astype(_MXU), dyb, _TN, preferred_element_type=F32)
            dcim_ref[j] -= lax.dot_general(sim_ref[:, cols].astype(_MXU), dyb, _TN, preferred_element_type=F32)
            dd_ref[:, ch] += _colsum(dyf * uf)

        @pl.when(step == nc - 1)
        def _():
            for acc, out in ((dbre_ref, dbre_out), (dbim_ref, dbim_out), (dcre_ref, dcre_out), (dcim_ref, dcim_out)):
                pltpu.sync_copy(acc, out)

    tok = pl.BlockSpec((T, SSM_W), lambda c: (nc - 1 - c, 0))
    st = pl.BlockSpec((T, N_STATES), lambda c: (nc - 1 - c, 0))
    ini = pl.BlockSpec((None, SUBLANES, N_STATES), lambda c: (nc - 1 - c, 0, 0))
    bsp = pl.BlockSpec((SSM_JB, LANES, SSM_SB), lambda c: (0, 0, 0))
    csp = pl.BlockSpec((SSM_JB, SSM_SB, LANES), lambda c: (0, 0, 0))
    row_w = pl.BlockSpec((1, SSM_W), lambda c: (0, 0))
    row_s = pl.BlockSpec((1, N_STATES), lambda c: (0, 0))
    big = pltpu.VMEM((T, N_STATES), F32)
    one = pltpu.VMEM((1, N_STATES), F32)
    eight = pltpu.VMEM((SUBLANES, N_STATES), F32)
    return pl.pallas_call(
        body, name="ssm_bwd", grid=(nc,),
        in_specs=[tok, tok, st, st, ini, ini, bsp, bsp, csp, csp, row_w] + [row_s] * 6,
        out_specs=[tok, _ANY, _ANY, _ANY, _ANY, row_w, row_s, row_s, row_s, row_s],
        out_shape=[jax.ShapeDtypeStruct((seq, SSM_W), F32),
                   jax.ShapeDtypeStruct((SSM_JB, LANES, SSM_SB), F32), jax.ShapeDtypeStruct((SSM_JB, LANES, SSM_SB), F32),
                   jax.ShapeDtypeStruct((SSM_JB, SSM_SB, LANES), F32), jax.ShapeDtypeStruct((SSM_JB, SSM_SB, LANES), F32),
                   jax.ShapeDtypeStruct((1, SSM_W), F32)] + [jax.ShapeDtypeStruct((1, N_STATES), F32)] * 4,
        scratch_shapes=[big, big, one, one, eight, eight, eight, eight,
                        pltpu.VMEM((SSM_JB, LANES, SSM_SB), F32), pltpu.VMEM((SSM_JB, LANES, SSM_SB), F32),
                        pltpu.VMEM((SSM_JB, SSM_SB, LANES), F32), pltpu.VMEM((SSM_JB, SSM_SB, LANES), F32)],
        compiler_params=_params(("arbitrary",)),
    )(dy, u, s_re, s_im, i_re, i_im, b_re, b_im, c_re, c_im, d_skip, *coef)


def _block_diag_b(b):
    t = b.reshape(SSM_JB, 8, STATE, GROUP).transpose(0, 1, 3, 2)
    eye = jnp.eye(8, dtype=b.dtype)
    return (t[:, :, :, None, :] * eye[None, :, None, :, None]).reshape(SSM_JB, LANES, SSM_SB)


def _block_diag_c(c):
    t = c.reshape(SSM_JB, 8, GROUP, STATE).transpose(0, 1, 3, 2)
    eye = jnp.eye(8, dtype=c.dtype)
    return (t[:, :, :, None, :] * eye[None, :, None, :, None]).reshape(SSM_JB, SSM_SB, LANES)


def _diag_of_b(blk):
    t = blk.reshape(SSM_JB, 8, GROUP, 8, STATE)
    d = jnp.sum(t * jnp.eye(8, dtype=blk.dtype)[None, :, None, :, None], axis=3)
    return d.transpose(0, 1, 3, 2).reshape(N_GROUPS, STATE, GROUP)


def _diag_of_c(blk):
    t = blk.reshape(SSM_JB, 8, STATE, 8, GROUP)
    d = jnp.sum(t * jnp.eye(8, dtype=blk.dtype)[None, :, None, :, None], axis=3)
    return d.transpose(0, 1, 3, 2).reshape(N_GROUPS, GROUP, STATE)


def _to_scan_order(v):
    seq, w = v.shape
    return v.reshape(seq // SSM_T, SUBLANES, SSM_L, w).transpose(0, 2, 1, 3).reshape(seq, w)


def _from_scan_order(v):
    seq, w = v.shape
    return v.reshape(seq // SSM_T, SSM_L, SUBLANES, w).transpose(0, 2, 1, 3).reshape(seq, w)


def _adamw_math(w, g, m, v):
    nm = ADAM_B1 * m + (1.0 - ADAM_B1) * g
    nv = ADAM_B2 * v + (1.0 - ADAM_B2) * jnp.square(g)
    m_hat = nm / (1.0 - ADAM_B1 ** ADAM_STEP)
    v_hat = nv / (1.0 - ADAM_B2 ** ADAM_STEP)
    return -ADAM_LR * (m_hat / (jnp.sqrt(v_hat) + ADAM_EPS) + ADAM_WD * w), nm, nv


def _adamw(w, g, m, v, *, name, tm, deps=()):
    rows, cols = w.shape
    nd = len(deps)

    def body(w_ref, g_ref, m_ref, v_ref, *rest):
        d_ref, nm_ref, nv_ref = rest[nd:]
        d_ref[...], nm_ref[...], nv_ref[...] = _adamw_math(w_ref[...], g_ref[...], m_ref[...], v_ref[...])

    spec = pl.BlockSpec((tm, cols), lambda i: (i, 0))
    shp = jax.ShapeDtypeStruct((rows, cols), F32)
    return pl.pallas_call(body, name=name, grid=(rows // tm,), in_specs=[spec] * 4 + [_ANY] * nd,
                          out_specs=[spec] * 3, out_shape=[shp] * 3,
                          compiler_params=_params(("arbitrary",)))(w, g, m, v, *deps)


def _place():
    x, y, c = lax.axis_index("x"), lax.axis_index("y"), lax.axis_index("c")
    chips = [(1 - x, y), (x, 1 - y), (1 - x, 1 - y)]
    return x, y, c, chips


def _remote(src, dst, send_sem, recv_sem, dev):
    return pltpu.make_async_remote_copy(src_ref=src, dst_ref=dst, send_sem=send_sem, recv_sem=recv_sem,
                                        device_id=dev, device_id_type=MESH)


def _place_shard(w, mine_arr, *, name, tm=256):
    rows, cols = w.shape

    def body(m_ref, w_ref, o_ref):
        o_ref[...] = w_ref[...].astype(o_ref.dtype)

    return pl.pallas_call(
        body, name=name,
        grid_spec=pltpu.PrefetchScalarGridSpec(
            num_scalar_prefetch=1, grid=(rows // tm,),
            in_specs=[pl.BlockSpec((tm, cols), lambda i, m: (i, 0))],
            out_specs=pl.BlockSpec((None, tm, cols), lambda i, m: (m[0], i, 0))),
        out_shape=jax.ShapeDtypeStruct((N_CHIPS, rows, cols), _WIRE),
        compiler_params=_params(("arbitrary",)),
    )(mine_arr, w)


_HBM = pl.BlockSpec(memory_space=pltpu.HBM)
_SEM = pl.BlockSpec(memory_space=pltpu.SEMAPHORE)
_EFFECT = pltpu.SideEffectType.DATAFLOW_SIDE_EFFECTING


def _copies_start(name, bufs, plan, count, after=()):
    nb, na = len(bufs), len(after)

    def body(*refs):
        send_sems, recv_sems, token = refs[nb + na], refs[nb + na + 1], refs[-1]
        copies = plan(refs[:nb])
        assert len(copies) == count
        for i, (src, dst, dev, _) in enumerate(copies):
            _remote(src, dst, send_sems.at[i], recv_sems.at[i], dev).start()
        token[...] = jnp.zeros_like(token)

    res = pl.pallas_call(
        body, name=name, in_specs=[_HBM] * nb + [_ANY] * na,
        out_specs=(_SEM, _SEM, *[_HBM] * nb, pl.BlockSpec(memory_space=pltpu.VMEM)),
        out_shape=(pltpu.SemaphoreType.DMA((count,)), pltpu.SemaphoreType.DMA((count,)),
                   *[pltpu.HBM(b.shape, b.dtype) for b in bufs], jax.ShapeDtypeStruct((SUBLANES, LANES), F32)),
        input_output_aliases={i: 2 + i for i in range(nb)},
        compiler_params=pltpu.CompilerParams(has_side_effects=_EFFECT),
    )(*[pltpu.with_memory_space_constraint(b, pltpu.HBM) for b in bufs], *after)
    return (res[0], res[1]), list(res[2:2 + nb]), res[-1]


def _copies_wait(name, bufs, sems, plan, after=(), which=None):
    nb, na = len(bufs), len(after)

    def body(*refs):
        send_sems, recv_sems = refs[nb], refs[nb + 1]
        for i, (src, _, dev, land) in enumerate(plan(refs[:nb])):
            if which is not None and i not in which:
                continue
            cp = _remote(src, land, send_sems.at[i], recv_sems.at[i], dev)
            cp.wait_send()
            cp.wait_recv()

    res = pl.pallas_call(
        body, name=name, in_specs=[_HBM] * nb + [_SEM, _SEM] + [_ANY] * na, out_specs=[_HBM] * nb,
        out_shape=[pltpu.HBM(b.shape, b.dtype) for b in bufs],
        input_output_aliases={i: i for i in range(nb)},
        compiler_params=pltpu.CompilerParams(has_side_effects=_EFFECT),
    )(*bufs, *sems, *after)
    return list(res)


def _plan_gather_ici(fulls, which=(0, 1, 2)):
    x, y, c, chips = _place()
    copies = []
    for f in fulls:
        half = pl.ds(c * (f.shape[1] // 2), f.shape[1] // 2)
        own = f.at[2 * x + y, half]
        for chip in [chips[k] for k in which]:
            copies.append((own, own, (*chip, c), f.at[2 * chip[0] + chip[1], half]))
    return copies


def _plan_gather_d2d(fulls, which=(0, 1, 2)):
    x, y, c, chips = _place()
    copies = []
    for f in fulls:
        r2 = f.shape[1] // 2
        for chip in [chips[k] for k in which]:
            blk = 2 * chip[0] + chip[1]
            landed = f.at[blk, pl.ds(c * r2, r2)]
            copies.append((landed, landed, (x, y, 1 - c), f.at[blk, pl.ds((1 - c) * r2, r2)]))
    return copies


def _plan_relay_direct(fulls):
    (f,) = fulls
    x, y, c, chips = _place()
    half = pl.ds(c * (f.shape[1] // 2), f.shape[1] // 2)
    own = f.at[2 * x + y, half]
    return [(own, own, (*chip, c), f.at[2 * chip[0] + chip[1], half]) for chip in chips[:2]]


def _plan_relay_forward(fulls, k):
    (f,) = fulls
    x, y, c, chips = _place()
    r2 = f.shape[1] // 2
    half, other = pl.ds(c * r2, r2), pl.ds((1 - c) * r2, r2)
    quarter = pl.ds(c * r2 + k * (r2 // 2), r2 // 2)
    blk, far = 2 * chips[k][0] + chips[k][1], 2 * chips[2][0] + chips[2][1]
    passed, landed = f.at[blk, quarter], f.at[blk, half]
    return [(passed, passed, (*chips[1 - k], c), f.at[far, quarter]), (landed, landed, (x, y, 1 - c), f.at[blk, other])]


def _plan_relay_last(fulls):
    (f,) = fulls
    x, y, c, chips = _place()
    r2 = f.shape[1] // 2
    far = 2 * chips[2][0] + chips[2][1]
    landed = f.at[far, pl.ds(c * r2, r2)]
    return [(landed, landed, (x, y, 1 - c), f.at[far, pl.ds((1 - c) * r2, r2)])]


def _plan_swap_halves(refs):
    x, y, c, _ = _place()
    n = len(refs) // 2
    copies = []
    for g, land in zip(refs[:n], refs[n:]):
        r2 = g.shape[1] // 2
        copies.append((g.at[:, pl.ds((1 - c) * r2, r2), :], land, (x, y, 1 - c), land))
    return copies


def _plan_scatter_chips(refs):
    x, y, c, chips = _place()
    n = len(refs) // 2
    copies = []
    for h, land in zip(refs[:n], refs[n:]):
        for k, chip in enumerate(chips):
            copies.append((h.at[2 * chip[0] + chip[1]], land.at[k], (*chip, c), land.at[k]))
    return copies


def _plan_join_halves(totals):
    x, y, c, _ = _place()
    copies = []
    for t in totals:
        r2 = t.shape[0] // 2
        mine = t.at[pl.ds(c * r2, r2)]
        copies.append((mine, mine, (x, y, 1 - c), t.at[pl.ds((1 - c) * r2, r2)]))
    return copies


def _add_sibling_half(g, got, c_arr, *, name, tm):
    _, rows, cols = g.shape
    r2 = rows // 2
    nb = r2 // tm

    def body(c_ref, g_ref, r_ref, o_ref):
        o_ref[...] = (g_ref[...].astype(F32) + r_ref[...].astype(F32)).astype(o_ref.dtype)

    return pl.pallas_call(
        body, name=name,
        grid_spec=pltpu.PrefetchScalarGridSpec(
            num_scalar_prefetch=1, grid=(N_CHIPS, nb),
            in_specs=[pl.BlockSpec((None, tm, cols), lambda b, i, c: (b, c[0] * nb + i, 0)),
                      pl.BlockSpec((None, tm, cols), lambda b, i, c: (b, i, 0))],
            out_specs=pl.BlockSpec((None, tm, cols), lambda b, i, c: (b, i, 0))),
        out_shape=jax.ShapeDtypeStruct((N_CHIPS, r2, cols), _WIRE),
        compiler_params=_params(("arbitrary", "arbitrary")),
    )(c_arr, g, got)


def _add_chips(h, got, place_arr, *, name, tm):
    _, r2, cols = h.shape
    nb = r2 // tm

    def body(p_ref, h_ref, r_ref, o_ref):
        o_ref[...] = ((h_ref[...].astype(F32) + r_ref[0].astype(F32)) + r_ref[1].astype(F32)) + r_ref[2].astype(F32)

    return pl.pallas_call(
        body, name=name,
        grid_spec=pltpu.PrefetchScalarGridSpec(
            num_scalar_prefetch=1, grid=(nb,),
            in_specs=[pl.BlockSpec((None, tm, cols), lambda i, p: (p[0], i, 0)),
                      pl.BlockSpec((3, tm, cols), lambda i, p: (0, i, 0))],
            out_specs=pl.BlockSpec((tm, cols), lambda i, p: (p[1] * nb + i, 0))),
        out_shape=jax.ShapeDtypeStruct((2 * r2, cols), F32),
        compiler_params=_params(("arbitrary",)),
    )(place_arr, h, got)


class _ReduceScatter:
    def __init__(self, tag, names, grads):
        self.tag, self.names, self.n = tag, names, len(names)
        core = lax.axis_index("c").astype(jnp.int32)
        chip = (2 * lax.axis_index("x") + lax.axis_index("y")).astype(jnp.int32)
        self.c_arr, self.place_arr = core.reshape(1), jnp.stack([chip, core])
        self.bufs = list(grads)

    def _start(self, step, bufs, plan, count, after):
        self.plan = plan
        self.step = f"grad_{step}_{self.tag}"
        self.sems, self.bufs, token = _copies_start(self.step + "_start", bufs, plan, count, after)
        return [token]

    def _wait(self, after):
        self.bufs = _copies_wait(self.step + "_wait", self.bufs, self.sems, self.plan, after)
        return self.bufs

    def start_swap(self, after=()):
        lands = [lax.empty((N_CHIPS, g.shape[1] // 2, g.shape[2]), g.dtype) for g in self.bufs]
        return self._start("swap", self.bufs + lands, _plan_swap_halves, self.n, after)

    def start_scatter(self, after):
        bufs = self._wait(after)
        pair = [_add_sibling_half(g, r, self.c_arr, name=f"grad_add_sibling_{nm}", tm=min(256, g.shape[1] // 2))
                for nm, g, r in zip(self.names, bufs[:self.n], bufs[self.n:])]
        lands = [lax.empty((3,) + h.shape[1:], h.dtype) for h in pair]
        return self._start("scatter", pair + lands, _plan_scatter_chips, 3 * self.n, ())

    def start_join(self, after):
        bufs = self._wait(after)
        total = [_add_chips(h, r, self.place_arr, name=f"grad_add_chips_{nm}", tm=min(256, h.shape[1]))
                 for nm, h, r in zip(self.names, bufs[:self.n], bufs[self.n:])]
        return self._start("join", total, _plan_join_halves, self.n, ())

    def finish(self, after):
        return dict(zip(self.names, self._wait(after)))


def _all_gather_small(v):
    m_per, n = v.shape

    def body(x_ref, out_ref, send_sems, recv_sems, local_sem):
        x, y, c, chips = _place()
        me, sibling = (x, y, c), (x, y, 1 - c)

        def rows(px, py, pc):
            return out_ref.at[4 * px + 2 * py + pc]

        def copy(k, block, to, src=None):
            return _remote(rows(*block) if src is None else src, rows(*block), send_sems.at[k], recv_sems.at[k], to)

        mine = pltpu.make_async_copy(x_ref, rows(*me), local_sem)
        mine.start()
        first = [copy(0, me, sibling, src=x_ref)]
        first += [copy(1 + j, me, (*chip, c), src=x_ref) for j, chip in enumerate(chips)]
        for cp in first:
            cp.start()
        passed = [copy(4 + j, (*chip, c), sibling) for j, chip in enumerate(chips)]
        for j, chip in enumerate(chips):
            copy(1 + j, (*chip, c), me).wait_recv()
            passed[j].start()
        copy(0, sibling, me).wait_recv()
        for j, chip in enumerate(chips):
            copy(4 + j, (*chip, 1 - c), me).wait_recv()
        for cp in first + passed:
            cp.wait_send()
        mine.wait()

    return pl.pallas_call(
        body, name="gather_small_grads",
        out_shape=jax.ShapeDtypeStruct((8, m_per, n), v.dtype),
        in_specs=[pl.BlockSpec(memory_space=pltpu.VMEM)], out_specs=pl.BlockSpec(memory_space=pltpu.VMEM),
        scratch_shapes=[pltpu.SemaphoreType.DMA((7,)), pltpu.SemaphoreType.DMA((7,)), pltpu.SemaphoreType.DMA],
        compiler_params=pltpu.CompilerParams(vmem_limit_bytes=VMEM_LIMIT),
    )(v)


def _sum8(v, *, name):
    _, m, n = v.shape

    def body(v_ref, o_ref):
        acc = v_ref[0]
        for d in range(1, 8):
            acc = acc + v_ref[d]
        o_ref[...] = acc

    return pl.pallas_call(body, name=name, out_shape=jax.ShapeDtypeStruct((m, n), F32),
                          compiler_params=pltpu.CompilerParams(vmem_limit_bytes=VMEM_LIMIT))(v)


def _local_step(x, target, norm_w, q_norm_w, k_norm_w, sinks, a_re, a_im, log_dt, b_re, b_im, c_re, c_im, d_skip,
                b_glu, io):
    seq = x.shape[0]
    qw2 = jnp.tile(q_norm_w.reshape(1, HEAD_DIM), (1, HEADS_PER_TILE))
    kw2 = jnp.tile(k_norm_w.reshape(1, HEAD_DIM), (1, HEADS_PER_TILE))
    nw, bg = norm_w.reshape(1, D_MODEL), b_glu.reshape(1, D_MODEL)
    dsk = d_skip.reshape(1, SSM_W)

    h, rstd = _rms_fwd(x, nw, deps=io.begin())
    proj, w_in4 = io.projection(h)
    attn, lse = _attn2_fwd(proj, qw2, kw2, sinks, deps=io.after_proj(proj))

    def gate_a(at, ag):
        return (at * (ag * _sigmoid(ag)),)

    (ya_in,) = _ew(gate_a, [(attn, "mat", 0), (proj, "mat", OFF_AGATE)], [(ATTN_W, _MXU)], rows=seq, ncol=2,
                   name="ew_attn_gate")
    w_ap4 = io.weight("w_attn_proj", ya_in)
    w_glu4, w_sp4, w_out = io.weight("w_glu", ya_in), io.weight("w_ssm_proj", ya_in), io.weight("w_out", ya_in)
    y_a = _mm(ya_in, w_ap4, mode="nn", name="mm_attn_proj", tm=2048, tn=512, tk=ATTN_W, b_blocked=True,
              rows_outer=True)

    flat_a = (a_re.reshape(1, N_STATES), a_im.reshape(1, N_STATES), jnp.repeat(log_dt, STATE).reshape(1, N_STATES))
    coef = _ssm_params_fwd(*flat_a)
    bre_blk, bim_blk = _block_diag_b(b_re).astype(_MXU), _block_diag_b(b_im).astype(_MXU)
    cre_blk, cim_blk = _block_diag_c(c_re).astype(_MXU), _block_diag_c(c_im).astype(_MXU)
    u_scan = _to_scan_order(proj[:, OFF_U * CW:OFF_U * CW + SSM_W])
    y_scan, s_re, s_im, i_re, i_im = _ssm_fwd(u_scan, bre_blk, bim_blk, cre_blk, cim_blk, dsk, coef)
    y_ssm = _from_scan_order(y_scan)

    (yg,) = _ew(lambda yv: (jax.nn.gelu(yv),), [(y_ssm, "mat", 0)], [(SSM_W, _MXU)], rows=seq, ncol=2, name="ew_gelu")
    glu = _mm(yg, w_glu4, mode="nn", name="mm_glu", tm=2048, tn=512, tk=SSM_W, b_blocked=True, rows_outer=True)

    def gate_s(ga, gb, ba, bb, z):
        return ((ga + ba) * _sigmoid(gb + bb) * (z * _sigmoid(z)),)

    (ys_in,) = _ew(gate_s, [(glu, "mat", 0), (glu, "mat", 2), (bg, "row", 0), (bg, "row", 2), (proj, "mat", OFF_Z)],
                   [(SSM_W, _MXU)], rows=seq, ncol=2, name="ew_ssm_gate")
    y_s = _mm(ys_in, w_sp4, mode="nn", name="mm_ssm_proj", tm=2048, tn=512, tk=SSM_W, b_blocked=True,
              rows_outer=True)

    def merge(ga, gs, ya, ys):
        return (_sigmoid(ga) * ya + _sigmoid(gs) * ys,)

    (merged,) = _ew(merge, [(proj, "mat", OFF_GA), (proj, "mat", OFF_GS), (y_a, "mat", 0), (y_s, "mat", 0)],
                    [(D_MODEL, _MXU)], rows=seq, ncol=4, name="ew_merge")
    dout, dout_b, sq = _mm_out_loss(merged, w_out, x, target)
    loss = 0.5 * jnp.sum(sq) / D_MODEL

    d_ya, d_ys, d_ga, d_gs = _mm_merge_bwd(dout_b, w_out, proj, y_a, y_s)
    g_w_out = _mm(merged, dout_b, mode="tn", name="mm_g_w_out", tm=1024, tn=D_MODEL, tk=1024, out_dtype=_WIRE)

    d_ya_in = _mm(d_ya, w_ap4, mode="nt", name="mm_d_attn_gate", tm=2048, tn=ATTN_W, tk=512, b_blocked=True)
    g_w_ap = _mm(ya_in, d_ya, mode="tn", name="mm_g_w_attn_proj", tm=ATTN_W, tn=D_MODEL, tk=2048, out_dtype=_WIRE,
                 out_blocked=True)

    d_ys_in = _mm(d_ys, w_sp4, mode="nt", name="mm_d_ssm_gate", tm=2048, tn=SSM_W, tk=512, b_blocked=True)
    g_w_sp = _mm(ys_in, d_ys, mode="tn", name="mm_g_w_ssm_proj", tm=SSM_W, tn=D_MODEL, tk=2048, out_dtype=_WIRE,
                 out_blocked=True)

    def gate_s_bwd(dv, ga, gb, ba, bb, z):
        a, sb = ga + ba, _sigmoid(gb + bb)
        f, df = _silu_and_grad(z)
        dga = dv * sb * f
        dgb = dv * a * f * sb * (1.0 - sb)
        return dga, dgb, dv * a * sb * df, _colsum(dga), _colsum(dgb)

    d_glu_a, d_glu_b, d_z, g_bga, g_bgb = _ew(
        gate_s_bwd, [(d_ys_in, "mat", 0), (glu, "mat", 0), (glu, "mat", 2), (bg, "row", 0), (bg, "row", 2),
                     (proj, "mat", OFF_Z)],
        [(SSM_W, _MXU)] * 3, rows=seq, ncol=2, n_acc=2, name="ew_ssm_gate_bwd")
    d_glu = jnp.concatenate([d_glu_a, d_glu_b], axis=1)
    d_yg = _mm(d_glu, w_glu4, mode="nt", name="mm_d_gelu", tm=2048, tn=SSM_W, tk=512, b_blocked=True)
    g_w_glu = _mm(yg, d_glu, mode="tn", name="mm_g_w_glu", tm=SSM_W, tn=D_MODEL, tk=2048, out_dtype=_WIRE, out_blocked=True)
    dep = io.later_grads(dict(w_attn_proj=g_w_ap, w_glu=g_w_glu, w_ssm_proj=g_w_sp,
                              w_out=g_w_out.reshape(N_CHIPS, D_MODEL // N_CHIPS, D_MODEL)))

    def gate_a_bwd(dv, at, ag):
        f, df = _silu_and_grad(ag)
        return dv * f, dv * at * df

    d_attn, d_agate = _ew(gate_a_bwd, [(d_ya_in, "mat", 0), (attn, "mat", 0), (proj, "mat", OFF_AGATE)],
                          [(ATTN_W, F32), (ATTN_W, _MXU)], rows=seq, ncol=2, name="ew_attn_gate_bwd", deps=dep)

    def gelu_bwd(dv, yv):
        return (jax.vjp(jax.nn.gelu, yv)[1](dv)[0],)

    (d_yssm,) = _ew(gelu_bwd, [(d_yg, "mat", 0), (y_ssm, "mat", 0)], [(SSM_W, F32)], rows=seq, ncol=2, name="ew_gelu_bwd",
                    deps=dep)
    dep = io.before_attention_backward([d_attn, d_yssm])
    d_qkv, g_qw2, g_kw2, g_sk = _attn2_bwd(proj, qw2, kw2, sinks, lse, d_attn, deps=dep)
    (du_scan, g_bre, g_bim, g_cre, g_cim, g_dsk, g_abr, g_abi, g_cfr, g_cfi) = _ssm_bwd(
        _to_scan_order(d_yssm), u_scan, s_re, s_im, i_re, i_im, bre_blk, bim_blk, cre_blk, cim_blk, dsk, coef)
    g_are, g_aim, g_ldt = _ssm_params_bwd(*flat_a, g_abr, g_abi, g_cfr, g_cfi)
    g_are, g_aim = g_are.reshape(N_GROUPS, STATE), g_aim.reshape(N_GROUPS, STATE)
    g_ldt = g_ldt.reshape(N_GROUPS, STATE).sum(axis=1)
    d_u = _from_scan_order(du_scan)

    d_proj = jnp.concatenate([d_qkv, d_agate, d_u.astype(_MXU), d_z, d_ga, d_gs], axis=1)
    dep = io.before_input_projection_grad([d_proj]) + io.small_grads(dict(
        q_norm_w=g_qw2[0, :HEAD_DIM] + g_qw2[0, HEAD_DIM:], k_norm_w=g_kw2[0, :HEAD_DIM] + g_kw2[0, HEAD_DIM:],
        sinks=g_sk.reshape(N_Q_HEADS), A_re=g_are, A_im=g_aim, log_dt=g_ldt,
        B_re=_diag_of_b(g_bre), B_im=_diag_of_b(g_bim), C_re=_diag_of_c(g_cre), C_im=_diag_of_c(g_cim),
        D_skip=g_dsk.reshape(N_GROUPS, GROUP), b_glu=jnp.concatenate([g_bga, g_bgb], axis=1).reshape(D_MODEL)))
    g_w_in = _mm(h, d_proj, mode="tn", name="mm_g_w_in", tm=1024, tn=IN_W // 4, tk=1024, out_dtype=_WIRE,
                 out_blocked=True, deps=dep)
    dep = io.input_projection_grad(g_w_in)
    d_h = _mm(d_proj, w_in4, mode="nt", name="mm_d_h", tm=512, tn=D_MODEL, tk=IN_W // 4, b_blocked=True, deps=dep)
    grad_x, g_nw = _rms_bwd(d_h, x, rstd, nw, dout)
    return loss, grad_x, g_nw.reshape(D_MODEL)


_SMALL = ["norm_w", "q_norm_w", "k_norm_w", "sinks", "A_re", "A_im", "log_dt", "B_re", "B_im", "C_re", "C_im",
          "D_skip", "b_glu"]
_BIG = ["w_in", "w_attn_proj", "w_glu", "w_ssm_proj", "w_out"]
_LATER = _BIG[1:]
_RELATIONS = ("flip_x", "flip_y", "flip_xy")
_ORDER = ["norm_w", "w_in", "q_norm_w", "k_norm_w", "sinks", "w_attn_proj", "A_re", "A_im", "log_dt", "B_re", "B_im",
          "C_re", "C_im", "D_skip", "w_glu", "b_glu", "w_ssm_proj", "w_out"]
_PACK_W = 1024


def _packed_rows(size):
    unit = SUBLANES * _PACK_W
    return -(-size // unit) * SUBLANES


def _pack_small(d, names):
    parts = []
    for n in names:
        flat = d[n].reshape(-1).astype(F32)
        rows = _packed_rows(flat.shape[0])
        parts.append(jnp.pad(flat, (0, rows * _PACK_W - flat.shape[0])).reshape(rows, _PACK_W))
    return jnp.concatenate(parts, axis=0)


def _unpack_small(packed, like, names):
    out, pos = {}, 0
    for n in names:
        rows = _packed_rows(like[n].size)
        out[n] = packed[pos:pos + rows].reshape(-1)[:like[n].size].reshape(like[n].shape)
        pos += rows
    return out


def _place_block(v, index_arr, *, name):
    rows, cols = v.shape

    def body(i_ref, v_ref, o_ref):
        o_ref[...] = v_ref[...]

    return pl.pallas_call(
        body, name=name,
        grid_spec=pltpu.PrefetchScalarGridSpec(
            num_scalar_prefetch=1, grid=(1,),
            in_specs=[pl.BlockSpec((rows, cols), lambda i, d: (0, 0))],
            out_specs=pl.BlockSpec((None, rows, cols), lambda i, d: (d[0], 0, 0))),
        out_shape=jax.ShapeDtypeStruct((8, rows, cols), v.dtype),
        compiler_params=_params(("arbitrary",)),
    )(index_arr, v)


def _plan_all_to_all(refs):
    (land,) = refs
    x, y, c, _ = _place()
    own = land.at[4 * x + 2 * y + c]
    copies = []
    for fx, fy, fc in [(0, 0, 1), (0, 1, 0), (0, 1, 1), (1, 0, 0), (1, 0, 1), (1, 1, 0), (1, 1, 1)]:
        px, py, pc = (1 - x) if fx else x, (1 - y) if fy else y, (1 - c) if fc else c
        copies.append((own, own, (px, py, pc), land.at[4 * px + 2 * py + pc]))
    return copies


def _as2d(a):
    return a.reshape(1, -1) if a.ndim == 1 else a


def _adamw_whole(w, g, m, v, *, name):
    shape = w.shape
    w, g, m, v = _as2d(w), _as2d(g), _as2d(m), _as2d(v)

    def body(w_ref, g_ref, m_ref, v_ref, d_ref, nm_ref, nv_ref):
        d_ref[...], nm_ref[...], nv_ref[...] = _adamw_math(w_ref[...], g_ref[...], m_ref[...], v_ref[...])

    outs = pl.pallas_call(body, name=name, out_shape=[jax.ShapeDtypeStruct(w.shape, F32)] * 3)(w, g, m, v)
    return [o.reshape(shape) for o in outs]


class _Exchanges:
    def __init__(self, w, m, v):
        self.w, self.m, self.v = w, m, v
        self.grads, self.delta, self.new_m, self.new_v = {}, {}, {}, {}

    def _adamw(self, names, deps):
        for n in names:
            self.delta[n], self.new_m[n], self.new_v[n] = _adamw(
                self.w[n], self.grads[n], self.m[n], self.v[n], name=f"adamw_{n}", tm=128, deps=deps)

    def begin(self):
        chip = (2 * lax.axis_index("x") + lax.axis_index("y")).astype(jnp.int32).reshape(1)
        full = {n: _place_shard(self.w[n], chip, name=f"place_{n}") for n in _BIG}
        self.later_full = [full[n] for n in _LATER]
        self.w_in_sems, self.w_in_buf, token = _copies_start("gather_w_in_direct_start", [full["w_in"]],
                                                             _plan_relay_direct, 2)
        return [token]

    def projection(self, h):
        x, y = lax.axis_index("x"), lax.axis_index("y")
        blks = [jnp.asarray(b, jnp.int32).reshape(1)
                for b in (2 * x + y, 2 * (1 - x) + y, 2 * x + (1 - y), 2 * (1 - x) + (1 - y))]
        bufs = self.w_in_buf
        proj = _mm_chip_block(h, bufs[0], blks[0], None, name="mm_proj_own")
        relay, token = [], proj
        for k, tag in enumerate(_RELATIONS[:2]):
            bufs = _copies_wait(f"gather_w_in_direct_{tag}_wait", bufs, self.w_in_sems, _plan_relay_direct, [token],
                                which=(k,))
            plan = functools.partial(_plan_relay_forward, k=k)
            sems, bufs, token = _copies_start(f"gather_w_in_relay_{tag}_start", bufs, plan, 2)
            relay.append((sems, plan))
        self.rest = _copies_start("gather_ici_rest_start", self.later_full, _plan_gather_ici, 3 * len(_LATER),
                                  after=[token])
        token = self.rest[2]
        for k, tag in enumerate(_RELATIONS[:2]):
            bufs = _copies_wait(f"gather_w_in_hand_{tag}_wait", bufs, relay[k][0], relay[k][1], [token], which=(1,))
            token = proj = _mm_chip_block(h, bufs[0], blks[1 + k], proj, name=f"mm_proj_{tag}")
        for k, tag in enumerate(_RELATIONS[:2]):
            bufs = _copies_wait(f"gather_w_in_relay_{tag}_wait", bufs, relay[k][0], relay[k][1], [token], which=(0,))
        sems, bufs, token = _copies_start("gather_w_in_last_start", bufs, _plan_relay_last, 1)
        bufs = _copies_wait("gather_w_in_last_wait", bufs, sems, _plan_relay_last, [token])
        proj = _mm_chip_block(h, bufs[0], blks[3], proj, name="mm_proj_flip_xy")
        return proj, bufs[0]

    def weight(self, name, after):
        if self.rest is not None:
            sems, bufs = self.rest
            later = dict(zip(_LATER, _copies_wait("gather_d2d_rest_wait", bufs, sems, _plan_gather_d2d, [after])))
            later["w_out"] = later["w_out"].reshape(D_MODEL, D_MODEL)
            self.later, self.rest = later, None
        return self.later[name]

    def after_proj(self, proj):
        sems, bufs, _ = self.rest
        bufs = _copies_wait("gather_ici_rest_wait", bufs, sems, _plan_gather_ici, [proj])
        sems, bufs, token = _copies_start("gather_d2d_rest_start", bufs, _plan_gather_d2d, 3 * len(_LATER))
        self.rest = (sems, bufs)
        return [token]

    def later_grads(self, grads):
        self.rs_later = _ReduceScatter("later", _LATER, [grads[n] for n in _LATER])
        return self.rs_later.start_swap()

    def before_attention_backward(self, after):
        return self.rs_later.start_scatter(after)

    def before_input_projection_grad(self, after):
        return self.rs_later.start_join(after)

    def input_projection_grad(self, g_w_in):
        self.grads.update(self.rs_later.finish([g_w_in]))
        self.rs_in = _ReduceScatter("w_in", ["w_in"], [g_w_in])
        self._adamw(_LATER, self.rs_in.start_swap())
        return self.rs_in.start_scatter([self.delta[n] for n in _LATER])

    def _adamw_small(self, names):
        for n in names:
            self.delta[n], self.new_m[n], self.new_v[n] = _adamw_whole(
                self.w[n], self.grads[n], self.m[n], self.v[n], name=f"adamw_{n}")

    def small_grads(self, grads):
        me = (4 * lax.axis_index("x") + 2 * lax.axis_index("y") + lax.axis_index("c")).astype(jnp.int32).reshape(1)
        land = _place_block(_pack_small(grads, _SMALL[1:]), me, name="place_small_grads")
        self.small = _copies_start("gather_small_start", [land], _plan_all_to_all, 7)
        return [self.small[2]]

    def finish(self, g_norm_w, loss, after):
        sems, bufs, _ = self.small
        (land,) = _copies_wait("gather_small_wait", bufs, sems, _plan_all_to_all, after)
        self.grads.update(_unpack_small(_sum8(land, name="sum_small_grads"), self.w, _SMALL[1:]))
        self._adamw_small(_SMALL[1:])
        rows = _packed_rows(g_norm_w.size)
        late = jnp.concatenate([_pack_small(dict(norm_w=g_norm_w), _SMALL[:1]),
                                jnp.pad(loss.reshape(1, 1), ((0, SUBLANES - 1), (0, _PACK_W - 1)))], axis=0)
        late = _sum8(_all_gather_small(late), name="sum_norm_w_grad_and_loss")
        self.grads.update(_unpack_small(late[:rows], self.w, _SMALL[:1]))
        self._adamw_small(_SMALL[:1])
        self.grads.update(self.rs_in.finish(self.rs_in.start_join([self.delta[_SMALL[0]]])))
        self._adamw(["w_in"], ())
        return late[rows, 0]


def kernel(x, norm_w, w_in, q_norm_w, k_norm_w, sinks, w_attn_proj, A_re, A_im, log_dt, B_re, B_im, C_re, C_im, D_skip, w_glu, b_glu, w_ssm_proj, w_out, loss_target, m_norm_w, m_w_in, m_q_norm_w, m_k_norm_w, m_sinks, m_w_attn_proj, m_A_re, m_A_im, m_log_dt, m_B_re, m_B_im, m_C_re, m_C_im, m_D_skip, m_w_glu, m_b_glu, m_w_ssm_proj, m_w_out, v_norm_w, v_w_in, v_q_norm_w, v_k_norm_w, v_sinks, v_w_attn_proj, v_A_re, v_A_im, v_log_dt, v_B_re, v_B_im, v_C_re, v_C_im, v_D_skip, v_w_glu, v_b_glu, v_w_ssm_proj, v_w_out):
    w = dict(norm_w=norm_w, w_in=w_in, q_norm_w=q_norm_w, k_norm_w=k_norm_w, sinks=sinks, w_attn_proj=w_attn_proj,
             A_re=A_re, A_im=A_im, log_dt=log_dt, B_re=B_re, B_im=B_im, C_re=C_re, C_im=C_im, D_skip=D_skip,
             w_glu=w_glu, b_glu=b_glu, w_ssm_proj=w_ssm_proj, w_out=w_out)
    m = dict(norm_w=m_norm_w, w_in=m_w_in, q_norm_w=m_q_norm_w, k_norm_w=m_k_norm_w, sinks=m_sinks,
             w_attn_proj=m_w_attn_proj, A_re=m_A_re, A_im=m_A_im, log_dt=m_log_dt, B_re=m_B_re, B_im=m_B_im,
             C_re=m_C_re, C_im=m_C_im, D_skip=m_D_skip, w_glu=m_w_glu, b_glu=m_b_glu, w_ssm_proj=m_w_ssm_proj,
             w_out=m_w_out)
    v = dict(norm_w=v_norm_w, w_in=v_w_in, q_norm_w=v_q_norm_w, k_norm_w=v_k_norm_w, sinks=v_sinks,
             w_attn_proj=v_w_attn_proj, A_re=v_A_re, A_im=v_A_im, log_dt=v_log_dt, B_re=v_B_re, B_im=v_B_im,
             C_re=v_C_re, C_im=v_C_im, D_skip=v_D_skip, w_glu=v_w_glu, b_glu=v_b_glu, w_ssm_proj=v_w_ssm_proj,
             w_out=v_w_out)

    io = _Exchanges(w, m, v)
    loss, grad_x, g_norm_w = _local_step(x[0], loss_target[0], norm_w, q_norm_w, k_norm_w, sinks, A_re, A_im, log_dt,
                                         B_re, B_im, C_re, C_im, D_skip, b_glu, io)
    loss = io.finish(g_norm_w, loss, [grad_x])
    grads, delta, new_m, new_v = io.grads, io.delta, io.new_m, io.new_v

    return (loss, grad_x[None], *[grads[n] for n in _ORDER], *[delta[n] for n in _ORDER],
            *[new_m[n] for n in _ORDER], *[new_v[n] for n in _ORDER])
```

```python
import functools
import math

import jax
import jax.numpy as jnp
from jax import lax
from jax.experimental import pallas as pl
from jax.experimental.pallas import tpu as pltpu

F32 = jnp.float32
_MXU = jnp.bfloat16
_WIRE = jnp.bfloat16

LANES = 128
SUBLANES = 8
VMEM_LIMIT = 56 * 1024 * 1024

D_MODEL = 2048
HEAD_DIM = 64
N_Q_HEADS = 16
N_KV_HEADS = 4
Q_PER_KV = 4
ATTN_W = 1024
KV_W = 256
WINDOW = 128
SSM_W = 1024
GROUP = 16
N_GROUPS = 64
STATE = 64
N_STATES = N_GROUPS * STATE
IN_W = 8704
NORM_EPS = 1e-6
N_CHIPS = 4
CW = 512
OFF_AGATE, OFF_U, OFF_Z, OFF_GA, OFF_GS = 3, 5, 7, 9, 13

SSM_T = 256
SSM_L = SSM_T // SUBLANES
SSM_JB = 8
SSM_SB = N_STATES // SSM_JB

ADAM_LR, ADAM_B1, ADAM_B2, ADAM_EPS, ADAM_WD, ADAM_STEP = 0.001, 0.9, 0.999, 1e-08, 0.01, 10

MESH = pl.DeviceIdType.MESH
_ANY = pl.BlockSpec(memory_space=pl.ANY)


def _params(sem=None):
    return pltpu.CompilerParams(dimension_semantics=sem, vmem_limit_bytes=VMEM_LIMIT)


def _mm(a, b, *, mode, name, tm, tn, tk, out_dtype=F32, b_blocked=False, out_blocked=False, rows_outer=False,
        deps=()):
    nd = len(deps)
    if mode == "tn":
        K, M = a.shape
    else:
        M, K = a.shape
    if mode == "nn":
        N = b.shape[0] * b.shape[2] if b_blocked else b.shape[1]
    elif mode == "nt":
        N = b.shape[1] if b_blocked else b.shape[0]
    else:
        N = b.shape[1]
    tm, tn, tk = min(tm, M), min(tn, N), min(tk, K)
    nj, ni, nk = N // tn, M // tm, K // tk
    assert nj * tn == N and ni * tm == M and nk * tk == K, (name, M, N, K)
    dims = {"nn": (((1,), (0,)), ((), ())), "nt": (((1,), (1,)), ((), ())), "tn": (((0,), (0,)), ((), ()))}[mode]

    if mode == "tn":
        a_spec = pl.BlockSpec((tk, tm), lambda j, i, k: (k, i))
    else:
        a_spec = pl.BlockSpec((tm, tk), lambda j, i, k: (i, k))
    if mode == "nn":
        if b_blocked:
            assert b.shape[0] == nj and b.shape[2] == tn
            b_spec = pl.BlockSpec((None, tk, tn), lambda j, i, k: (j, k, 0))
        else:
            b_spec = pl.BlockSpec((tk, tn), lambda j, i, k: (k, j))
    elif mode == "nt":
        if b_blocked:
            assert b.shape[0] == nk and b.shape[2] == tk
            b_spec = pl.BlockSpec((None, tn, tk), lambda j, i, k: (k, j, 0))
        else:
            b_spec = pl.BlockSpec((tn, tk), lambda j, i, k: (j, k))
    else:
        b_spec = pl.BlockSpec((tk, tn), lambda j, i, k: (k, j))
    whole_out = out_blocked and nj == 1
    if whole_out:
        assert ni == 1
        o_spec = pl.BlockSpec((N_CHIPS, tm, tn // N_CHIPS), lambda j, i, k: (0, 0, 0))
        o_shape = jax.ShapeDtypeStruct((N_CHIPS, M, tn // N_CHIPS), out_dtype)
    elif out_blocked:
        assert nj == N_CHIPS
        o_spec = pl.BlockSpec((None, tm, tn), lambda j, i, k: (j, i, 0))
        o_shape = jax.ShapeDtypeStruct((nj, M, tn), out_dtype)
    else:
        o_spec = pl.BlockSpec((tm, tn), lambda j, i, k: (i, j))
        o_shape = jax.ShapeDtypeStruct((M, N), out_dtype)
    use_acc = nk > 1 and (out_dtype != F32 or whole_out)

    def body(a_ref, b_ref, *rest):
        o_ref, scratch = rest[nd], rest[nd + 1:]

        def product():
            return lax.dot_general(a_ref[...].astype(_MXU), b_ref[...].astype(_MXU), dims, preferred_element_type=F32)

        def write(result):
            if whole_out:
                w = tn // N_CHIPS
                for c in range(N_CHIPS):
                    o_ref[c] = result[:, c * w:(c + 1) * w].astype(o_ref.dtype)
            else:
                o_ref[...] = result.astype(o_ref.dtype)

        if nk == 1:
            write(product())
            return
        k = pl.program_id(2)
        acc = scratch[0] if use_acc else o_ref

        @pl.when(k == 0)
        def _():
            acc[...] = jnp.zeros_like(acc)

        acc[...] += product()

        if use_acc:
            @pl.when(k == nk - 1)
            def _():
                write(acc[...])

    specs = [a_spec, b_spec, o_spec]
    grid = (nj, ni, nk)
    if rows_outer:
        specs = [pl.BlockSpec(s.block_shape, lambda i, j, k, f=s.index_map: f(j, i, k)) for s in specs]
        grid = (ni, nj, nk)
    return pl.pallas_call(
        body, name=name, grid=grid, in_specs=specs[:2] + [_ANY] * nd, out_specs=specs[2],
        out_shape=o_shape, scratch_shapes=[pltpu.VMEM((tm, tn), F32)] if use_acc else [],
        compiler_params=_params(("parallel", "parallel", "arbitrary")),
    )(a, b, *deps)


def _mm_chip_block(a, b4, blk, prev, *, name, tm=512, deps=()):
    M, K = a.shape
    nchip, _, C = b4.shape
    tm = min(tm, M)
    extra = ([] if prev is None else [prev]) + list(deps)

    def body(blk_ref, a_ref, b_ref, *rest):
        rest[-1][...] = jnp.dot(a_ref[...].astype(_MXU), b_ref[...].astype(_MXU), preferred_element_type=F32)

    return pl.pallas_call(
        body, name=name,
        grid_spec=pltpu.PrefetchScalarGridSpec(
            num_scalar_prefetch=1, grid=(M // tm,),
            in_specs=[pl.BlockSpec((tm, K), lambda i, c: (i, 0)), pl.BlockSpec((None, K, C), lambda i, c: (c[0], 0, 0))]
            + [_ANY] * len(extra),
            out_specs=pl.BlockSpec((tm, C), lambda i, c: (i, c[0]))),
        out_shape=jax.ShapeDtypeStruct((M, nchip * C), F32),
        input_output_aliases={} if prev is None else {3: 0},
        compiler_params=_params(("arbitrary",)),
    )(blk, a, b4, *extra)


def _mm_out_loss(merged, w_out, x, target, *, tm=256):
    rows, d = x.shape

    def body(m_ref, w_ref, x_ref, t_ref, d_ref, db_ref, sq_ref):
        mo = jnp.dot(m_ref[...].astype(_MXU), w_ref[...].astype(_MXU), preferred_element_type=F32)
        err = (x_ref[...] + mo) - t_ref[...]
        dout = err * (1.0 / d)
        d_ref[...] = dout
        db_ref[...] = dout.astype(db_ref.dtype)
        part = _colsum(err * err)
        i = pl.program_id(0)

        @pl.when(i == 0)
        def _():
            sq_ref[...] = part

        @pl.when(i > 0)
        def _():
            sq_ref[...] += part

    tile = pl.BlockSpec((tm, d), lambda i: (i, 0))
    return pl.pallas_call(
        body, name="mm_out_loss", grid=(rows // tm,),
        in_specs=[tile, pl.BlockSpec((d, d), lambda i: (0, 0)), tile, tile],
        out_specs=[tile, tile, pl.BlockSpec((1, d), lambda i: (0, 0))],
        out_shape=[jax.ShapeDtypeStruct((rows, d), F32), jax.ShapeDtypeStruct((rows, d), _MXU),
                   jax.ShapeDtypeStruct((1, d), F32)],
        compiler_params=_params(("arbitrary",)),
    )(merged, w_out, x, target)


def _mm_merge_bwd(dout_b, w_out, proj, y_a, y_s, *, tm=256):
    rows, d = y_a.shape
    ncol = d // CW

    def body(do_ref, w_ref, *refs):
        ga_refs, gs_refs = refs[:ncol], refs[ncol:2 * ncol]
        ya_ref, ys_ref, dya_ref, dys_ref, dga_ref, dgs_ref = refs[2 * ncol:]
        dm = lax.dot_general(do_ref[...].astype(_MXU), w_ref[...].astype(_MXU), _NT, preferred_element_type=F32)
        for j in range(ncol):
            cols = slice(j * CW, (j + 1) * CW)
            dmj = dm[:, cols]
            sa, ss = _sigmoid(ga_refs[j][...]), _sigmoid(gs_refs[j][...])
            dya_ref[:, cols] = (sa * dmj).astype(dya_ref.dtype)
            dys_ref[:, cols] = (ss * dmj).astype(dys_ref.dtype)
            dga_ref[:, cols] = (dmj * ya_ref[:, cols] * sa * (1.0 - sa)).astype(dga_ref.dtype)
            dgs_ref[:, cols] = (dmj * ys_ref[:, cols] * ss * (1.0 - ss)).astype(dgs_ref.dtype)

    tile = pl.BlockSpec((tm, d), lambda i: (i, 0))
    gate = [pl.BlockSpec((tm, CW), lambda i, c=off + j: (i, c)) for off in (OFF_GA, OFF_GS) for j in range(ncol)]
    return pl.pallas_call(
        body, name="mm_merge_bwd", grid=(rows // tm,),
        in_specs=[tile, pl.BlockSpec((d, d), lambda i: (0, 0))] + gate + [tile, tile],
        out_specs=[tile] * 4, out_shape=[jax.ShapeDtypeStruct((rows, d), _MXU)] * 4,
        compiler_params=_params(("arbitrary",)),
    )(dout_b, w_out, *([proj] * (2 * ncol)), y_a, y_s)


def _ew(fn, ins, outs, *, rows, ncol, name, n_acc=0, tm=512, deps=()):
    n_in, n_out, nd = len(ins), len(outs), len(deps)
    tm = min(tm, rows)
    in_specs = []
    for _, kind, col0 in ins:
        if kind == "mat":
            in_specs.append(pl.BlockSpec((tm, CW), lambda j, i, c0=col0: (i, c0 + j)))
        else:
            in_specs.append(pl.BlockSpec((1, CW), lambda j, i, c0=col0: (0, c0 + j)))
    out_specs = [pl.BlockSpec((tm, CW), lambda j, i: (i, j)) for _ in outs]
    out_shape = [jax.ShapeDtypeStruct((rows, w), dt) for w, dt in outs]
    for _ in range(n_acc):
        out_specs.append(pl.BlockSpec((1, CW), lambda j, i: (0, j)))
        out_shape.append(jax.ShapeDtypeStruct((1, ncol * CW), F32))

    def body(*refs):
        vals = fn(*[r[...] for r in refs[:n_in]])
        refs = refs[n_in + nd:]
        for r, v in zip(refs[:n_out], vals[:n_out]):
            r[...] = v.astype(r.dtype)
        i = pl.program_id(1)
        for r, v in zip(refs[n_out:], vals[n_out:]):
            @pl.when(i == 0)
            def _(r=r, v=v):
                r[...] = v

            @pl.when(i > 0)
            def _(r=r, v=v):
                r[...] += v

    res = pl.pallas_call(
        body, name=name, grid=(ncol, rows // tm), in_specs=in_specs + [_ANY] * nd, out_specs=out_specs,
        out_shape=out_shape, compiler_params=_params(("parallel", "arbitrary")),
    )(*[a for a, _, _ in ins], *deps)
    return res


def _colsum(v):
    return jnp.sum(v, axis=0, keepdims=True)


def _sigmoid(v):
    return jax.nn.sigmoid(v)


def _silu_and_grad(v):
    s = _sigmoid(v)
    return v * s, s * (1.0 + v * (1.0 - s))


def _rms_fwd(x, w, *, tm=512, deps=()):
    rows, d = x.shape
    nd = len(deps)

    def body(x_ref, w_ref, *rest):
        h_ref, r_ref = rest[nd:]
        xv = x_ref[...]
        r = lax.rsqrt(jnp.mean(xv * xv, axis=-1, keepdims=True) + NORM_EPS)
        h_ref[...] = (xv * r * w_ref[...]).astype(h_ref.dtype)
        r_ref[...] = r

    return pl.pallas_call(
        body, name="rms_fwd", grid=(rows // tm,),
        in_specs=[pl.BlockSpec((tm, d), lambda i: (i, 0)), pl.BlockSpec((1, d), lambda i: (0, 0))] + [_ANY] * nd,
        out_specs=[pl.BlockSpec((tm, d), lambda i: (i, 0)), pl.BlockSpec((tm, 1), lambda i: (i, 0))],
        out_shape=[jax.ShapeDtypeStruct((rows, d), _MXU), jax.ShapeDtypeStruct((rows, 1), F32)],
        compiler_params=_params(("arbitrary",)),
    )(x, w, *deps)


def _rms_bwd(dh, x, rstd, w, dout, *, tm=256):
    rows, d = x.shape

    def body(dh_ref, x_ref, r_ref, w_ref, do_ref, gx_ref, gw_ref):
        dhv, xv, r, wv = dh_ref[...], x_ref[...], r_ref[...], w_ref[...]
        xr = xv * r
        t = jnp.mean(dhv * wv * xr, axis=-1, keepdims=True)
        gx_ref[...] = do_ref[...] + r * (wv * dhv - xr * t)
        part = _colsum(dhv * xr)
        i = pl.program_id(0)

        @pl.when(i == 0)
        def _():
            gw_ref[...] = part

        @pl.when(i > 0)
        def _():
            gw_ref[...] += part

    return pl.pallas_call(
        body, name="rms_bwd", grid=(rows // tm,),
        in_specs=[pl.BlockSpec((tm, d), lambda i: (i, 0)), pl.BlockSpec((tm, d), lambda i: (i, 0)),
                  pl.BlockSpec((tm, 1), lambda i: (i, 0)), pl.BlockSpec((1, d), lambda i: (0, 0)),
                  pl.BlockSpec((tm, d), lambda i: (i, 0))],
        out_specs=[pl.BlockSpec((tm, d), lambda i: (i, 0)), pl.BlockSpec((1, d), lambda i: (0, 0))],
        out_shape=[jax.ShapeDtypeStruct((rows, d), F32), jax.ShapeDtypeStruct((1, d), F32)],
        compiler_params=_params(("arbitrary",)),
    )(dh, x, rstd, w, dout)


_NT = (((1,), (1,)), ((), ()))
_TN = (((0,), (0,)), ((), ()))


QKV_W = ATTN_W + 2 * KV_W
HEADS_PER_TILE = LANES // HEAD_DIM


def _low_half(rows):
    return lax.broadcasted_iota(jnp.int32, (rows, LANES), 1) < HEAD_DIM


def _pair_mean(t, low):
    m_lo = jnp.sum(jnp.where(low, t, 0.0), axis=-1, keepdims=True)
    m_hi = jnp.sum(jnp.where(low, 0.0, t), axis=-1, keepdims=True)
    return jnp.where(low, m_lo, m_hi) * (1.0 / HEAD_DIM)


def _pair_rstd(t, low):
    return lax.rsqrt(_pair_mean(t * t, low) + NORM_EPS)


def _dup_half(t, hi, low):
    swapped = pltpu.roll(t, HEAD_DIM, 1)
    return jnp.where(low, swapped, t) if hi else jnp.where(low, t, swapped)


def _fold_halves(t):
    return t + pltpu.roll(t, HEAD_DIM, 1)


def _split_heads(t, low):
    return [jnp.where(low, t, 0.0), jnp.where(low, 0.0, t)]


def _stacked_band_mask(n):
    rows = Q_PER_KV * WINDOW
    qi = lax.broadcasted_iota(jnp.int32, (rows, 2 * WINDOW), 0) % WINDOW + WINDOW
    kj = lax.broadcasted_iota(jnp.int32, (rows, 2 * WINDOW), 1)
    diff = qi - kj
    first_key = jnp.where(n > 0, 0, WINDOW)
    return (diff >= 0) & (diff < WINDOW) & (kj >= first_key)


def _stacked_sinks(sink_ref, g):
    blk = lax.broadcasted_iota(jnp.int32, (Q_PER_KV * WINDOW, 1), 0) // WINDOW
    col = jnp.full((Q_PER_KV * WINDOW, 1), sink_ref[Q_PER_KV * g], F32)
    for r in range(1, Q_PER_KV):
        col = jnp.where(blk == r, sink_ref[Q_PER_KV * g + r], col)
    return col


def _attn_in_specs(nblk, rev):
    def cur(n):
        return (nblk - 1 - n) if rev else n

    q_spec = pl.BlockSpec((WINDOW, ATTN_W), lambda n: (cur(n), 0))
    kvc_spec = pl.BlockSpec((WINDOW, 2 * KV_W), lambda n: (cur(n), ATTN_W // (2 * KV_W)))
    kvp_spec = pl.BlockSpec((WINDOW, 2 * KV_W), lambda n: (jnp.maximum(cur(n) - 1, 0), ATTN_W // (2 * KV_W)))
    w_spec = pl.BlockSpec((1, LANES), lambda n: (0, 0))
    l_spec = pl.BlockSpec((WINDOW, N_Q_HEADS), lambda n: (cur(n), 0))
    gate_specs = [pl.BlockSpec((WINDOW, CW), lambda n, col=OFF_AGATE + j: (cur(n), col)) for j in range(ATTN_W // CW)]
    return q_spec, kvc_spec, kvp_spec, w_spec, l_spec, gate_specs


def _attn2_fwd(proj, qw2, kw2, sinks, deps=()):
    seq = proj.shape[0]
    nblk = seq // WINDOW
    scale = 1.0 / math.sqrt(HEAD_DIM)
    q_spec, kvc_spec, kvp_spec, w_spec, l_spec, gate_specs = _attn_in_specs(nblk, False)
    nd, ng = len(deps), len(gate_specs)

    def body(sink_ref, q_ref, kvc_ref, kvp_ref, qw_ref, kw_ref, *rest):
        gate_refs = rest[:ng]
        o_ref, lse_ref, ya_ref = rest[ng + nd:]
        n = pl.program_id(0)
        low, low2 = _low_half(WINDOW), _low_half(2 * WINDOW)
        valid = _stacked_band_mask(n)
        head_lane = lax.broadcasted_iota(jnp.int32, (WINDOW, N_Q_HEADS), 1)
        kv = jnp.concatenate([kvp_ref[...], kvc_ref[...]], axis=0)
        qwv, kwv = qw_ref[...], kw_ref[...]
        lse_blk = jnp.zeros((WINDOW, N_Q_HEADS), F32)
        for t in range(N_KV_HEADS // HEADS_PER_TILE):
            kt = kv[:, t * LANES:(t + 1) * LANES]
            vt = kv[:, KV_W + t * LANES:KV_W + (t + 1) * LANES]
            kn = kt * _pair_rstd(kt, low2) * kwv
            for hi in range(HEADS_PER_TILE):
                g = HEADS_PER_TILE * t + hi
                kdup = _dup_half(kn, hi, low2).astype(_MXU)
                vdup = _dup_half(vt, hi, low2).astype(_MXU)
                stack = []
                for tq in (2 * g, 2 * g + 1):
                    qt = q_ref[:, tq * LANES:(tq + 1) * LANES]
                    stack += _split_heads(qt * _pair_rstd(qt, low) * qwv, low)
                qs = jnp.concatenate(stack, axis=0).astype(_MXU)
                s = lax.dot_general(qs, kdup, _NT, preferred_element_type=F32) * scale
                s = jnp.where(valid, s, -1e30)
                sink = _stacked_sinks(sink_ref, g)
                m = jnp.maximum(jnp.max(s, axis=-1, keepdims=True), sink)
                e = jnp.exp(s - m)
                z = jnp.sum(e, axis=-1, keepdims=True) + jnp.exp(sink - m)
                o = jnp.dot((e / z).astype(_MXU), vdup, preferred_element_type=F32)
                for i, tq in enumerate((2 * g, 2 * g + 1)):
                    o_ref[:, tq * LANES:(tq + 1) * LANES] = jnp.where(
                        low, o[2 * i * WINDOW:(2 * i + 1) * WINDOW], o[(2 * i + 1) * WINDOW:(2 * i + 2) * WINDOW])
                lse = m + jnp.log(z)
                for r in range(Q_PER_KV):
                    lse_blk = jnp.where(head_lane == Q_PER_KV * g + r, lse[r * WINDOW:(r + 1) * WINDOW], lse_blk)
        lse_ref[...] = lse_blk
        for j, g_ref in enumerate(gate_refs):
            cols = slice(j * CW, (j + 1) * CW)
            gate = g_ref[...]
            ya_ref[:, cols] = (o_ref[:, cols] * (gate * _sigmoid(gate))).astype(ya_ref.dtype)

    return pl.pallas_call(
        body, name="attn_fwd", grid=(nblk,),
        in_specs=[pl.BlockSpec(memory_space=pltpu.SMEM), q_spec, kvc_spec, kvp_spec, w_spec, w_spec] + gate_specs
        + [_ANY] * nd,
        out_specs=[q_spec, l_spec, q_spec],
        out_shape=[jax.ShapeDtypeStruct((seq, ATTN_W), F32), jax.ShapeDtypeStruct((seq, N_Q_HEADS), F32),
                   jax.ShapeDtypeStruct((seq, ATTN_W), _MXU)],
        compiler_params=_params(("arbitrary",)),
    )(sinks, proj, proj, proj, qw2, kw2, *([proj] * ng), *deps)


def _attn2_bwd(proj, qw2, kw2, sinks, lse, attn, dya, deps=()):
    seq = proj.shape[0]
    nblk = seq // WINDOW
    scale = 1.0 / math.sqrt(HEAD_DIM)
    q_spec, kvc_spec, kvp_spec, w_spec, l_spec, gate_specs = _attn_in_specs(nblk, True)
    s_spec = pl.BlockSpec((1, N_Q_HEADS), lambda n: (0, 0))
    d_spec = pl.BlockSpec((WINDOW, QKV_W + ATTN_W), lambda n: (nblk - 1 - n, 0))
    nd, ng = len(deps), len(gate_specs)

    def body(sink_ref, q_ref, kvc_ref, kvp_ref, qw_ref, kw_ref, lse_ref, attn_ref, dya_ref, *rest):
        gate_refs = rest[:ng]
        d_ref, dqw_ref, dkw_ref, dsk_ref, carry, do_ref = rest[ng + nd:]
        step = pl.program_id(0)
        n = nblk - 1 - step

        @pl.when(step == 0)
        def _():
            carry[...] = jnp.zeros_like(carry)
            dqw_ref[...] = jnp.zeros_like(dqw_ref)
            dkw_ref[...] = jnp.zeros_like(dkw_ref)
            dsk_ref[...] = jnp.zeros_like(dsk_ref)

        for j, g_ref in enumerate(gate_refs):
            cols = slice(j * CW, (j + 1) * CW)
            f, df = _silu_and_grad(g_ref[...])
            dv = dya_ref[:, cols]
            do_ref[:, cols] = dv * f
            d_ref[:, QKV_W + j * CW:QKV_W + (j + 1) * CW] = (dv * attn_ref[:, cols] * df).astype(d_ref.dtype)

        low, low2 = _low_half(WINDOW), _low_half(2 * WINDOW)
        valid = _stacked_band_mask(n)
        head_lane = lax.broadcasted_iota(jnp.int32, (WINDOW, N_Q_HEADS), 1)
        sink_lane = lax.broadcasted_iota(jnp.int32, (1, N_Q_HEADS), 1)
        kv = jnp.concatenate([kvp_ref[...], kvc_ref[...]], axis=0)
        qwv, kwv = qw_ref[...], kw_ref[...]
        lse_blk = lse_ref[...]
        dqw = jnp.zeros((1, LANES), F32)
        dkw = jnp.zeros((1, LANES), F32)
        dsk = jnp.zeros((1, N_Q_HEADS), F32)
        for t in range(N_KV_HEADS // HEADS_PER_TILE):
            kt = kv[:, t * LANES:(t + 1) * LANES]
            vt = kv[:, KV_W + t * LANES:KV_W + (t + 1) * LANES]
            rk = _pair_rstd(kt, low2)
            kn = kt * rk * kwv
            dkn_t = jnp.zeros((2 * WINDOW, LANES), F32)
            dv_t = jnp.zeros((2 * WINDOW, LANES), F32)
            for hi in range(HEADS_PER_TILE):
                g = HEADS_PER_TILE * t + hi
                kdup = _dup_half(kn, hi, low2).astype(_MXU)
                vdup = _dup_half(vt, hi, low2).astype(_MXU)
                tiles = (2 * g, 2 * g + 1)
                qx, rq, stack, dstack, lse_rows = [], [], [], [], []
                for tq in tiles:
                    qt = q_ref[:, tq * LANES:(tq + 1) * LANES]
                    r = _pair_rstd(qt, low)
                    rq.append(r)
                    qx.append(qt * r)
                    stack += _split_heads(qx[-1] * qwv, low)
                    dstack += _split_heads(do_ref[:, tq * LANES:(tq + 1) * LANES], low)
                for r in range(Q_PER_KV):
                    lse_rows.append(jnp.sum(jnp.where(head_lane == Q_PER_KV * g + r, lse_blk, 0.0), axis=-1, keepdims=True))
                qs = jnp.concatenate(stack, axis=0).astype(_MXU)
                dos = jnp.concatenate(dstack, axis=0).astype(_MXU)
                lse_col = jnp.concatenate(lse_rows, axis=0)
                s = lax.dot_general(qs, kdup, _NT, preferred_element_type=F32) * scale
                s = jnp.where(valid, s, -1e30)
                p = jnp.exp(s - lse_col)
                dp = lax.dot_general(dos, vdup, _NT, preferred_element_type=F32)
                dsum = jnp.sum(p * dp, axis=-1, keepdims=True)
                ds = (p * (dp - dsum) * scale).astype(_MXU)
                dsink = -jnp.exp(_stacked_sinks(sink_ref, g) - lse_col) * dsum
                for r in range(Q_PER_KV):
                    dsk = dsk + jnp.where(sink_lane == Q_PER_KV * g + r, _colsum(dsink[r * WINDOW:(r + 1) * WINDOW]), 0.0)
                dv_g = _fold_halves(lax.dot_general(p.astype(_MXU), dos, _TN, preferred_element_type=F32))
                dkn_g = _fold_halves(lax.dot_general(ds, qs, _TN, preferred_element_type=F32))
                dv_t = jnp.where(low2, dv_t, dv_g) if hi else jnp.where(low2, dv_g, dv_t)
                dkn_t = jnp.where(low2, dkn_t, dkn_g) if hi else jnp.where(low2, dkn_g, dkn_t)
                dqn = jnp.dot(ds, kdup, preferred_element_type=F32)
                for i, tq in enumerate(tiles):
                    dqn_t = jnp.where(low, dqn[2 * i * WINDOW:(2 * i + 1) * WINDOW],
                                      dqn[(2 * i + 1) * WINDOW:(2 * i + 2) * WINDOW])
                    dq = rq[i] * (qwv * dqn_t - qx[i] * _pair_mean(dqn_t * qwv * qx[i], low))
                    d_ref[:, tq * LANES:(tq + 1) * LANES] = dq.astype(d_ref.dtype)
                    dqw = dqw + _colsum(dqn_t * qx[i])
            k_cols = slice(t * LANES, (t + 1) * LANES)
            v_cols = slice(KV_W + t * LANES, KV_W + (t + 1) * LANES)
            dkn_c = dkn_t[WINDOW:] + carry[:, k_cols]
            rc = rk[WINDOW:]
            kx = kt[WINDOW:] * rc
            dk = rc * (kwv * dkn_c - kx * _pair_mean(dkn_c * kwv * kx, low))
            d_ref[:, ATTN_W + t * LANES:ATTN_W + (t + 1) * LANES] = dk.astype(d_ref.dtype)
            d_ref[:, ATTN_W + KV_W + t * LANES:ATTN_W + KV_W + (t + 1) * LANES] = (
                dv_t[WINDOW:] + carry[:, v_cols]).astype(d_ref.dtype)
            carry[:, k_cols] = dkn_t[:WINDOW]
            carry[:, v_cols] = dv_t[:WINDOW]
            dkw = dkw + _colsum(dkn_c * kx)
        dqw_ref[...] += dqw
        dkw_ref[...] += dkw
        dsk_ref[...] += dsk

    return pl.pallas_call(
        body, name="attn_bwd", grid=(nblk,),
        in_specs=[pl.BlockSpec(memory_space=pltpu.SMEM), q_spec, kvc_spec, kvp_spec, w_spec, w_spec, l_spec, q_spec,
                  q_spec] + gate_specs + [_ANY] * nd,
        out_specs=[d_spec, w_spec, w_spec, s_spec],
        out_shape=[jax.ShapeDtypeStruct((seq, QKV_W + ATTN_W), _MXU), jax.ShapeDtypeStruct((1, LANES), F32),
                   jax.ShapeDtypeStruct((1, LANES), F32), jax.ShapeDtypeStruct((1, N_Q_HEADS), F32)],
        scratch_shapes=[pltpu.VMEM((WINDOW, 2 * KV_W), F32), pltpu.VMEM((WINDOW, ATTN_W), F32)],
        compiler_params=_params(("arbitrary",)),
    )(sinks, proj, proj, proj, qw2, kw2, lse, attn, dya, *([proj] * ng), *deps)


def _ssm_discretise(a_re, a_im, log_dt):
    dt = jnp.exp(log_dt)
    mag = jnp.exp(dt * a_re)
    ab_re = mag * jnp.cos(dt * a_im)
    ab_im = mag * jnp.sin(dt * a_im)
    num_re = ab_re - 1.0
    num_im = ab_im
    den = a_re * a_re + a_im * a_im
    cf_re = (num_re * a_re + num_im * a_im) / den
    cf_im = (num_im * a_re - num_re * a_im) / den
    return ab_re, ab_im, cf_re, cf_im


def _ssm_params_fwd(a_re, a_im, log_dt):
    shp = jax.ShapeDtypeStruct(a_re.shape, F32)

    def body(are_ref, aim_ref, ldt_ref, abr_ref, abi_ref, cfr_ref, cfi_ref, alr_ref, ali_ref):
        abr, abi, cfr, cfi = _ssm_discretise(are_ref[...], aim_ref[...], ldt_ref[...])
        abr_ref[...], abi_ref[...], cfr_ref[...], cfi_ref[...] = abr, abi, cfr, cfi
        pr, pi = abr, abi
        for _ in range(int(math.log2(SSM_L))):
            pr, pi = pr * pr - pi * pi, 2.0 * pr * pi
        alr_ref[...], ali_ref[...] = pr, pi

    return pl.pallas_call(body, name="ssm_params_fwd", out_shape=[shp] * 6)(a_re, a_im, log_dt)


def _ssm_params_bwd(a_re, a_im, log_dt, d_abr, d_abi, d_cfr, d_cfi):
    def body(are_ref, aim_ref, ldt_ref, g0, g1, g2, g3, dare_ref, daim_ref, dldt_ref):
        _, vjp = jax.vjp(_ssm_discretise, are_ref[...], aim_ref[...], ldt_ref[...])
        dare_ref[...], daim_ref[...], dldt_ref[...] = vjp((g0[...], g1[...], g2[...], g3[...]))

    return pl.pallas_call(
        body, name="ssm_params_bwd",
        out_shape=[jax.ShapeDtypeStruct(a_re.shape, F32), jax.ShapeDtypeStruct(a_im.shape, F32),
                   jax.ShapeDtypeStruct(log_dt.shape, F32)],
    )(a_re, a_im, log_dt, d_abr, d_abi, d_cfr, d_cfi)


def _scan_cols(j):
    return pl.ds(j * SSM_SB, SSM_SB)


def _rows8(r):
    return pl.ds(pl.multiple_of(r * SUBLANES, SUBLANES), SUBLANES)


def _bcast8(row):
    return jnp.broadcast_to(row, (SUBLANES, row.shape[-1]))


SCAN_UNROLL = 8


def _scan_loop(n, step, init):
    def trip(o, carry):
        for i in range(SCAN_UNROLL):
            carry = step(o * SCAN_UNROLL + i, carry)
        return carry

    return lax.fori_loop(0, n // SCAN_UNROLL, trip, init)


def _ssm_fwd(u, b_re, b_im, c_re, c_im, d_skip, coef):
    seq = u.shape[0]
    nc = seq // SSM_T
    T, L = SSM_T, SSM_L

    def body(u_ref, bre_ref, bim_ref, cre_ref, cim_ref, d_ref, are_ref, aim_ref, cfr_ref, cfi_ref, alr_ref, ali_ref,
             y_ref, yg_ref, sre_ref, sim_ref, ire_ref, iim_ref, car_re, car_im, end_re, end_im):
        c = pl.program_id(0)

        @pl.when(c == 0)
        def _():
            car_re[...] = jnp.zeros_like(car_re)
            car_im[...] = jnp.zeros_like(car_im)

        for j in range(SSM_JB):
            ub = u_ref[:, j * LANES:(j + 1) * LANES].astype(_MXU)
            bur = jnp.dot(ub, bre_ref[j], preferred_element_type=F32)
            bui = jnp.dot(ub, bim_ref[j], preferred_element_type=F32)
            cfr, cfi = cfr_ref[:, _scan_cols(j)], cfi_ref[:, _scan_cols(j)]
            sre_ref[:, _scan_cols(j)] = cfr * bur - cfi * bui
            sim_ref[:, _scan_cols(j)] = cfr * bui + cfi * bur

        for j in range(SSM_JB):
            cols = _scan_cols(j)
            ar, ai = _bcast8(are_ref[:, cols]), _bcast8(aim_ref[:, cols])

            def step1(r, s, cols=cols, ar=ar, ai=ai):
                sr, si = s
                rows = _rows8(r)
                return (ar * sr - ai * si + sre_ref[rows, cols], ar * si + ai * sr + sim_ref[rows, cols])

            zero = jnp.zeros((SUBLANES, SSM_SB), F32)
            er, ei = _scan_loop(L, step1, (zero, zero))
            end_re[:, cols] = er
            end_im[:, cols] = ei

        alr, ali = alr_ref[...], ali_ref[...]
        cr, ci = car_re[...], car_im[...]
        ire_ref[0:1, :] = cr
        iim_ref[0:1, :] = ci
        for i in range(1, SUBLANES):
            er, ei = end_re[i - 1:i, :], end_im[i - 1:i, :]
            cr, ci = alr * cr - ali * ci + er, alr * ci + ali * cr + ei
            ire_ref[i:i + 1, :] = cr
            iim_ref[i:i + 1, :] = ci

        for j in range(SSM_JB):
            cols = _scan_cols(j)
            ar, ai = _bcast8(are_ref[:, cols]), _bcast8(aim_ref[:, cols])

            def step2(r, s, cols=cols, ar=ar, ai=ai):
                sr, si = s
                rows = _rows8(r)
                nr = ar * sr - ai * si + sre_ref[rows, cols]
                ni = ar * si + ai * sr + sim_ref[rows, cols]
                sre_ref[rows, cols] = nr
                sim_ref[rows, cols] = ni
                return nr, ni

            _scan_loop(L, step2, (ire_ref[:, cols], iim_ref[:, cols]))

        car_re[...] = sre_ref[T - 1:T, :]
        car_im[...] = sim_ref[T - 1:T, :]

        for j in range(SSM_JB):
            cols = _scan_cols(j)
            ch = slice(j * LANES, (j + 1) * LANES)
            y = (jnp.dot(sre_ref[:, cols].astype(_MXU), cre_ref[j], preferred_element_type=F32)
                 - jnp.dot(sim_ref[:, cols].astype(_MXU), cim_ref[j], preferred_element_type=F32))
            y = y + d_ref[:, ch] * u_ref[:, ch]
            y_ref[:, ch] = y
            yg_ref[:, ch] = jax.nn.gelu(y).astype(yg_ref.dtype)

    tok = pl.BlockSpec((T, SSM_W), lambda c: (c, 0))
    st = pl.BlockSpec((T, N_STATES), lambda c: (c, 0))
    ini = pl.BlockSpec((None, SUBLANES, N_STATES), lambda c: (c, 0, 0))
    bsp = pl.BlockSpec((SSM_JB, LANES, SSM_SB), lambda c: (0, 0, 0))
    csp = pl.BlockSpec((SSM_JB, SSM_SB, LANES), lambda c: (0, 0, 0))
    row_w = pl.BlockSpec((1, SSM_W), lambda c: (0, 0))
    row_s = pl.BlockSpec((1, N_STATES), lambda c: (0, 0))
    return pl.pallas_call(
        body, name="ssm_fwd", grid=(nc,),
        in_specs=[tok, bsp, bsp, csp, csp, row_w] + [row_s] * 6,
        out_specs=[tok, tok, st, st, ini, ini],
        out_shape=[jax.ShapeDtypeStruct((seq, SSM_W), F32), jax.ShapeDtypeStruct((seq, SSM_W), _MXU),
                   jax.ShapeDtypeStruct((seq, N_STATES), F32), jax.ShapeDtypeStruct((seq, N_STATES), F32),
                   jax.ShapeDtypeStruct((nc, SUBLANES, N_STATES), F32),
                   jax.ShapeDtypeStruct((nc, SUBLANES, N_STATES), F32)],
        scratch_shapes=[pltpu.VMEM((1, N_STATES), F32), pltpu.VMEM((1, N_STATES), F32),
                        pltpu.VMEM((SUBLANES, N_STATES), F32), pltpu.VMEM((SUBLANES, N_STATES), F32)],
        compiler_params=_params(("arbitrary",)),
    )(u, b_re, b_im, c_re, c_im, d_skip, *coef)


def _ssm_bwd(dyg, y, u, s_re, s_im, i_re, i_im, b_re, b_im, c_re, c_im, d_skip, coef, deps=()):
    seq = u.shape[0]
    nc = seq // SSM_T
    T, L = SSM_T, SSM_L

    def body(dyg_ref, y_ref, u_ref, sre_ref, sim_ref, ire_ref, iim_ref, bre_ref, bim_ref, cre_ref, cim_ref, d_ref,
             are_ref, aim_ref, cfr_ref, cfi_ref, alr_ref, ali_ref, *rest):
        (du_ref, dbre_out, dbim_out, dcre_out, dcim_out, dd_ref, dar_ref, dai_ref, dcfr_ref, dcfi_ref,
         lre, lim, car_re, car_im, end_re, end_im, ini_re, ini_im, dbre_ref, dbim_ref, dcre_ref, dcim_ref,
         dy_ref) = rest[len(deps):]
        step = pl.program_id(0)
        dy_ref[...] = jax.vjp(jax.nn.gelu, y_ref[...])[1](dyg_ref[...])[0]

        @pl.when(step == 0)
        def _():
            car_re[...] = jnp.zeros_like(car_re)
            car_im[...] = jnp.zeros_like(car_im)
            for ref in (dbre_ref, dbim_ref, dcre_ref, dcim_ref, dd_ref, dar_ref, dai_ref, dcfr_ref, dcfi_ref):
                ref[...] = jnp.zeros_like(ref)

        for j in range(SSM_JB):
            dyb = dy_ref[:, j * LANES:(j + 1) * LANES].astype(_MXU)
            lre[:, _scan_cols(j)] = lax.dot_general(dyb, cre_ref[j], _NT, preferred_element_type=F32)
            lim[:, _scan_cols(j)] = -lax.dot_general(dyb, cim_ref[j], _NT, preferred_element_type=F32)

        for j in range(SSM_JB):
            cols = _scan_cols(j)
            ar, ai = _bcast8(are_ref[:, cols]), _bcast8(aim_ref[:, cols])

            def step1(t, s, cols=cols, ar=ar, ai=ai):
                sr, si = s
                rows = _rows8(L - 1 - t)
                return (ar * sr + ai * si + lre[rows, cols], ar * si - ai * sr + lim[rows, cols])

            zero = jnp.zeros((SUBLANES, SSM_SB), F32)
            er, ei = _scan_loop(L, step1, (zero, zero))
            end_re[:, cols] = er
            end_im[:, cols] = ei

        alr, ali = alr_ref[...], ali_ref[...]
        cr, ci = car_re[...], car_im[...]
        ini_re[SUBLANES - 1:SUBLANES, :] = cr
        ini_im[SUBLANES - 1:SUBLANES, :] = ci
        for i in range(SUBLANES - 2, -1, -1):
            er, ei = end_re[i + 1:i + 2, :], end_im[i + 1:i + 2, :]
            cr, ci = alr * cr + ali * ci + er, alr * ci - ali * cr + ei
            ini_re[i:i + 1, :] = cr
            ini_im[i:i + 1, :] = ci

        for j in range(SSM_JB):
            cols = _scan_cols(j)
            ar, ai = _bcast8(are_ref[:, cols]), _bcast8(aim_ref[:, cols])

            def step2(t, s, cols=cols, ar=ar, ai=ai):
                sr, si = s
                rows = _rows8(L - 1 - t)
                nr = ar * sr + ai * si + lre[rows, cols]
                ni = ar * si - ai * sr + lim[rows, cols]
                lre[rows, cols] = nr
                lim[rows, cols] = ni
                return nr, ni

            _scan_loop(L, step2, (ini_re[:, cols], ini_im[:, cols]))

        car_re[...] = lre[0:1, :]
        car_im[...] = lim[0:1, :]

        head, tail, body_rows = slice(0, SUBLANES), slice(SUBLANES, T), slice(0, T - SUBLANES)
        for j in range(SSM_JB):
            cols = _scan_cols(j)
            ch = slice(j * LANES, (j + 1) * LANES)
            lr, li = lre[:, cols], lim[:, cols]
            dar_ref[:, cols] += (_colsum(lre[tail, cols] * sre_ref[body_rows, cols] + lim[tail, cols] * sim_ref[body_rows, cols])
                                 + _colsum(lre[head, cols] * ire_ref[:, cols] + lim[head, cols] * iim_ref[:, cols]))
            dai_ref[:, cols] += (_colsum(lim[tail, cols] * sre_ref[body_rows, cols] - lre[tail, cols] * sim_ref[body_rows, cols])
                                 + _colsum(lim[head, cols] * ire_ref[:, cols] - lre[head, cols] * iim_ref[:, cols]))
            uf = u_ref[:, ch]
            ub = uf.astype(_MXU)
            bur = jnp.dot(ub, bre_ref[j], preferred_element_type=F32)
            bui = jnp.dot(ub, bim_ref[j], preferred_element_type=F32)
            dcfr_ref[:, cols] += _colsum(lr * bur + li * bui)
            dcfi_ref[:, cols] += _colsum(li * bur - lr * bui)
            cfr, cfi = cfr_ref[:, cols], cfi_ref[:, cols]
            dbur = (cfr * lr + cfi * li).astype(_MXU)
            dbui = (cfr * li - cfi * lr).astype(_MXU)
            dyf = dy_ref[:, ch]
            dyb = dyf.astype(_MXU)
            du_ref[:, ch] = (lax.dot_general(dbur, bre_ref[j], _NT, preferred_element_type=F32)
                             + lax.dot_general(dbui, bim_ref[j], _NT, preferred_element_type=F32)
                             + d_ref[:, ch] * dyf)
            dbre_ref[j] += lax.dot_general(ub, dbur, _TN, preferred_element_type=F32)
            dbim_ref[j] += lax.dot_general(ub, dbui, _TN, preferred_element_type=F32)
            dcre_ref[j] += lax.dot_general(sre_ref[:, cols].astype(_MXU), dyb, _TN, preferred_element_type=F32)
            dcim_ref[j] -= lax.dot_general(sim_ref[:, cols].astype(_MXU), dyb, _TN, preferred_element_type=F32)
            dd_ref[:, ch] += _colsum(dyf * uf)

        @pl.when(step == nc - 1)
        def _():
            for acc, out in ((dbre_ref, dbre_out), (dbim_ref, dbim_out), (dcre_ref, dcre_out), (dcim_ref, dcim_out)):
                pltpu.sync_copy(acc, out)

    tok = pl.BlockSpec((T, SSM_W), lambda c: (nc - 1 - c, 0))
    st = pl.BlockSpec((T, N_STATES), lambda c: (nc - 1 - c, 0))
    ini = pl.BlockSpec((None, SUBLANES, N_STATES), lambda c: (nc - 1 - c, 0, 0))
    bsp = pl.BlockSpec((SSM_JB, LANES, SSM_SB), lambda c: (0, 0, 0))
    csp = pl.BlockSpec((SSM_JB, SSM_SB, LANES), lambda c: (0, 0, 0))
    row_w = pl.BlockSpec((1, SSM_W), lambda c: (0, 0))
    row_s = pl.BlockSpec((1, N_STATES), lambda c: (0, 0))
    big = pltpu.VMEM((T, N_STATES), F32)
    one = pltpu.VMEM((1, N_STATES), F32)
    eight = pltpu.VMEM((SUBLANES, N_STATES), F32)
    return pl.pallas_call(
        body, name="ssm_bwd", grid=(nc,),
        in_specs=[tok, tok, tok, st, st, ini, ini, bsp, bsp, csp, csp, row_w] + [row_s] * 6 + [_ANY] * len(deps),
        out_specs=[tok, _ANY, _ANY, _ANY, _ANY, row_w, row_s, row_s, row_s, row_s],
        out_shape=[jax.ShapeDtypeStruct((seq, SSM_W), F32),
                   jax.ShapeDtypeStruct((SSM_JB, LANES, SSM_SB), F32), jax.ShapeDtypeStruct((SSM_JB, LANES, SSM_SB), F32),
                   jax.ShapeDtypeStruct((SSM_JB, SSM_SB, LANES), F32), jax.ShapeDtypeStruct((SSM_JB, SSM_SB, LANES), F32),
                   jax.ShapeDtypeStruct((1, SSM_W), F32)] + [jax.ShapeDtypeStruct((1, N_STATES), F32)] * 4,
        scratch_shapes=[big, big, one, one, eight, eight, eight, eight,
                        pltpu.VMEM((SSM_JB, LANES, SSM_SB), F32), pltpu.VMEM((SSM_JB, LANES, SSM_SB), F32),
                        pltpu.VMEM((SSM_JB, SSM_SB, LANES), F32), pltpu.VMEM((SSM_JB, SSM_SB, LANES), F32),
                        pltpu.VMEM((T, SSM_W), F32)],
        compiler_params=_params(("arbitrary",)),
    )(dyg, y, u, s_re, s_im, i_re, i_im, b_re, b_im, c_re, c_im, d_skip, *coef, *deps)


def _block_diag_b(b):
    t = b.reshape(SSM_JB, 8, STATE, GROUP).transpose(0, 1, 3, 2)
    eye = jnp.eye(8, dtype=b.dtype)
    return (t[:, :, :, None, :] * eye[None, :, None, :, None]).reshape(SSM_JB, LANES, SSM_SB)


def _block_diag_c(c):
    t = c.reshape(SSM_JB, 8, GROUP, STATE).transpose(0, 1, 3, 2)
    eye = jnp.eye(8, dtype=c.dtype)
    return (t[:, :, :, None, :] * eye[None, :, None, :, None]).reshape(SSM_JB, SSM_SB, LANES)


def _diag_of_b(blk):
    t = blk.reshape(SSM_JB, 8, GROUP, 8, STATE)
    d = jnp.sum(t * jnp.eye(8, dtype=blk.dtype)[None, :, None, :, None], axis=3)
    return d.transpose(0, 1, 3, 2).reshape(N_GROUPS, STATE, GROUP)


def _diag_of_c(blk):
    t = blk.reshape(SSM_JB, 8, STATE, 8, GROUP)
    d = jnp.sum(t * jnp.eye(8, dtype=blk.dtype)[None, :, None, :, None], axis=3)
    return d.transpose(0, 1, 3, 2).reshape(N_GROUPS, GROUP, STATE)


def _to_scan_order(v):
    seq, w = v.shape
    return v.reshape(seq // SSM_T, SUBLANES, SSM_L, w).transpose(0, 2, 1, 3).reshape(seq, w)


def _from_scan_order(v):
    seq, w = v.shape
    return v.reshape(seq // SSM_T, SSM_L, SUBLANES, w).transpose(0, 2, 1, 3).reshape(seq, w)


def _adamw_math(w, g, m, v):
    nm = ADAM_B1 * m + (1.0 - ADAM_B1) * g
    nv = ADAM_B2 * v + (1.0 - ADAM_B2) * jnp.square(g)
    m_hat = nm / (1.0 - ADAM_B1 ** ADAM_STEP)
    v_hat = nv / (1.0 - ADAM_B2 ** ADAM_STEP)
    return -ADAM_LR * (m_hat / (jnp.sqrt(v_hat) + ADAM_EPS) + ADAM_WD * w), nm, nv


def _adamw(w, g, m, v, *, name, tm, deps=()):
    rows, cols = w.shape
    nd = len(deps)

    def body(w_ref, g_ref, m_ref, v_ref, *rest):
        d_ref, nm_ref, nv_ref = rest[nd:]
        d_ref[...], nm_ref[...], nv_ref[...] = _adamw_math(w_ref[...], g_ref[...], m_ref[...], v_ref[...])

    spec = pl.BlockSpec((tm, cols), lambda i: (i, 0))
    shp = jax.ShapeDtypeStruct((rows, cols), F32)
    return pl.pallas_call(body, name=name, grid=(rows // tm,), in_specs=[spec] * 4 + [_ANY] * nd,
                          out_specs=[spec] * 3, out_shape=[shp] * 3,
                          compiler_params=_params(("arbitrary",)))(w, g, m, v, *deps)


def _place():
    x, y, c = lax.axis_index("x"), lax.axis_index("y"), lax.axis_index("c")
    chips = [(1 - x, y), (x, 1 - y), (1 - x, 1 - y)]
    return x, y, c, chips


def _remote(src, dst, send_sem, recv_sem, dev):
    return pltpu.make_async_remote_copy(src_ref=src, dst_ref=dst, send_sem=send_sem, recv_sem=recv_sem,
                                        device_id=dev, device_id_type=MESH)


def _place_shard(w, mine_arr, *, name, tm=256):
    rows, cols = w.shape

    def body(m_ref, w_ref, o_ref):
        o_ref[...] = w_ref[...].astype(o_ref.dtype)

    return pl.pallas_call(
        body, name=name,
        grid_spec=pltpu.PrefetchScalarGridSpec(
            num_scalar_prefetch=1, grid=(rows // tm,),
            in_specs=[pl.BlockSpec((tm, cols), lambda i, m: (i, 0))],
            out_specs=pl.BlockSpec((None, tm, cols), lambda i, m: (m[0], i, 0))),
        out_shape=jax.ShapeDtypeStruct((N_CHIPS, rows, cols), _WIRE),
        compiler_params=_params(("arbitrary",)),
    )(mine_arr, w)


_HBM = pl.BlockSpec(memory_space=pltpu.HBM)
_SEM = pl.BlockSpec(memory_space=pltpu.SEMAPHORE)
_EFFECT = pltpu.SideEffectType.DATAFLOW_SIDE_EFFECTING


def _copies_start(name, bufs, plan, count, after=()):
    nb, na = len(bufs), len(after)

    def body(*refs):
        send_sems, recv_sems, token = refs[nb + na], refs[nb + na + 1], refs[-1]
        copies = plan(refs[:nb])
        assert len(copies) == count
        for i, (src, dst, dev, _) in enumerate(copies):
            _remote(src, dst, send_sems.at[i], recv_sems.at[i], dev).start()
        token[...] = jnp.zeros_like(token)

    res = pl.pallas_call(
        body, name=name, in_specs=[_HBM] * nb + [_ANY] * na,
        out_specs=(_SEM, _SEM, *[_HBM] * nb, pl.BlockSpec(memory_space=pltpu.VMEM)),
        out_shape=(pltpu.SemaphoreType.DMA((count,)), pltpu.SemaphoreType.DMA((count,)),
                   *[pltpu.HBM(b.shape, b.dtype) for b in bufs], jax.ShapeDtypeStruct((SUBLANES, LANES), F32)),
        input_output_aliases={i: 2 + i for i in range(nb)},
        compiler_params=pltpu.CompilerParams(has_side_effects=_EFFECT),
    )(*[pltpu.with_memory_space_constraint(b, pltpu.HBM) for b in bufs], *after)
    return (res[0], res[1]), list(res[2:2 + nb]), res[-1]


def _copies_wait(name, bufs, sems, plan, after=(), which=None):
    nb, na = len(bufs), len(after)

    def body(*refs):
        send_sems, recv_sems = refs[nb], refs[nb + 1]
        for i, (src, _, dev, land) in enumerate(plan(refs[:nb])):
            if which is not None and i not in which:
                continue
            cp = _remote(src, land, send_sems.at[i], recv_sems.at[i], dev)
            cp.wait_send()
            cp.wait_recv()

    res = pl.pallas_call(
        body, name=name, in_specs=[_HBM] * nb + [_SEM, _SEM] + [_ANY] * na, out_specs=[_HBM] * nb,
        out_shape=[pltpu.HBM(b.shape, b.dtype) for b in bufs],
        input_output_aliases={i: i for i in range(nb)},
        compiler_params=pltpu.CompilerParams(has_side_effects=_EFFECT),
    )(*bufs, *sems, *after)
    return list(res)


def _plan_gather_ici(fulls, which=(0, 1, 2)):
    x, y, c, chips = _place()
    copies = []
    for f in fulls:
        half = pl.ds(c * (f.shape[1] // 2), f.shape[1] // 2)
        own = f.at[2 * x + y, half]
        for chip in [chips[k] for k in which]:
            copies.append((own, own, (*chip, c), f.at[2 * chip[0] + chip[1], half]))
    return copies


def _plan_gather_d2d(fulls, which=(0, 1, 2)):
    x, y, c, chips = _place()
    copies = []
    for f in fulls:
        r2 = f.shape[1] // 2
        for chip in [chips[k] for k in which]:
            blk = 2 * chip[0] + chip[1]
            landed = f.at[blk, pl.ds(c * r2, r2)]
            copies.append((landed, landed, (x, y, 1 - c), f.at[blk, pl.ds((1 - c) * r2, r2)]))
    return copies


def _plan_relay_direct(fulls):
    (f,) = fulls
    x, y, c, chips = _place()
    half = pl.ds(c * (f.shape[1] // 2), f.shape[1] // 2)
    own = f.at[2 * x + y, half]
    return [(own, own, (*chip, c), f.at[2 * chip[0] + chip[1], half]) for chip in chips[:2]]


def _plan_relay_forward(fulls, k):
    (f,) = fulls
    x, y, c, chips = _place()
    r2 = f.shape[1] // 2
    half, other = pl.ds(c * r2, r2), pl.ds((1 - c) * r2, r2)
    quarter = pl.ds(c * r2 + k * (r2 // 2), r2 // 2)
    blk, far = 2 * chips[k][0] + chips[k][1], 2 * chips[2][0] + chips[2][1]
    passed, landed = f.at[blk, quarter], f.at[blk, half]
    return [(passed, passed, (*chips[1 - k], c), f.at[far, quarter]), (landed, landed, (x, y, 1 - c), f.at[blk, other])]


def _plan_relay_last(fulls):
    (f,) = fulls
    x, y, c, chips = _place()
    r2 = f.shape[1] // 2
    far = 2 * chips[2][0] + chips[2][1]
    landed = f.at[far, pl.ds(c * r2, r2)]
    return [(landed, landed, (x, y, 1 - c), f.at[far, pl.ds((1 - c) * r2, r2)])]


def _plan_swap_halves(refs):
    x, y, c, _ = _place()
    n = len(refs) // 2
    copies = []
    for g, land in zip(refs[:n], refs[n:]):
        r2 = g.shape[1] // 2
        copies.append((g.at[:, pl.ds((1 - c) * r2, r2), :], land, (x, y, 1 - c), land))
    return copies


def _plan_scatter_chips(refs):
    x, y, c, chips = _place()
    n = len(refs) // 2
    copies = []
    for h, land in zip(refs[:n], refs[n:]):
        for k, chip in enumerate(chips):
            copies.append((h.at[2 * chip[0] + chip[1]], land.at[k], (*chip, c), land.at[k]))
    return copies


def _plan_join_halves(totals):
    x, y, c, _ = _place()
    copies = []
    for t in totals:
        r2 = t.shape[0] // 2
        mine = t.at[pl.ds(c * r2, r2)]
        copies.append((mine, mine, (x, y, 1 - c), t.at[pl.ds((1 - c) * r2, r2)]))
    return copies


def _add_sibling_half(g, got, c_arr, *, name, tm):
    _, rows, cols = g.shape
    r2 = rows // 2
    nb = r2 // tm

    def body(c_ref, g_ref, r_ref, o_ref):
        o_ref[...] = (g_ref[...].astype(F32) + r_ref[...].astype(F32)).astype(o_ref.dtype)

    return pl.pallas_call(
        body, name=name,
        grid_spec=pltpu.PrefetchScalarGridSpec(
            num_scalar_prefetch=1, grid=(N_CHIPS, nb),
            in_specs=[pl.BlockSpec((None, tm, cols), lambda b, i, c: (b, c[0] * nb + i, 0)),
                      pl.BlockSpec((None, tm, cols), lambda b, i, c: (b, i, 0))],
            out_specs=pl.BlockSpec((None, tm, cols), lambda b, i, c: (b, i, 0))),
        out_shape=jax.ShapeDtypeStruct((N_CHIPS, r2, cols), _WIRE),
        compiler_params=_params(("arbitrary", "arbitrary")),
    )(c_arr, g, got)


def _add_chips(h, got, place_arr, *, name, tm):
    _, r2, cols = h.shape
    nb = r2 // tm

    def body(p_ref, h_ref, r_ref, o_ref):
        o_ref[...] = ((h_ref[...].astype(F32) + r_ref[0].astype(F32)) + r_ref[1].astype(F32)) + r_ref[2].astype(F32)

    return pl.pallas_call(
        body, name=name,
        grid_spec=pltpu.PrefetchScalarGridSpec(
            num_scalar_prefetch=1, grid=(nb,),
            in_specs=[pl.BlockSpec((None, tm, cols), lambda i, p: (p[0], i, 0)),
                      pl.BlockSpec((3, tm, cols), lambda i, p: (0, i, 0))],
            out_specs=pl.BlockSpec((tm, cols), lambda i, p: (p[1] * nb + i, 0))),
        out_shape=jax.ShapeDtypeStruct((2 * r2, cols), F32),
        compiler_params=_params(("arbitrary",)),
    )(place_arr, h, got)


class _ReduceScatter:
    def __init__(self, tag, names, grads):
        self.tag, self.names, self.n = tag, names, len(names)
        core = lax.axis_index("c").astype(jnp.int32)
        chip = (2 * lax.axis_index("x") + lax.axis_index("y")).astype(jnp.int32)
        self.c_arr, self.place_arr = core.reshape(1), jnp.stack([chip, core])
        self.bufs = list(grads)

    def _start(self, step, bufs, plan, count, after):
        self.plan = plan
        self.step = f"grad_{step}_{self.tag}"
        self.sems, self.bufs, token = _copies_start(self.step + "_start", bufs, plan, count, after)
        return [token]

    def _wait(self, after):
        self.bufs = _copies_wait(self.step + "_wait", self.bufs, self.sems, self.plan, after)
        return self.bufs

    def start_swap(self, after=()):
        lands = [lax.empty((N_CHIPS, g.shape[1] // 2, g.shape[2]), g.dtype) for g in self.bufs]
        return self._start("swap", self.bufs + lands, _plan_swap_halves, self.n, after)

    def start_scatter(self, after):
        bufs = self._wait(after)
        pair = [_add_sibling_half(g, r, self.c_arr, name=f"grad_add_sibling_{nm}", tm=min(256, g.shape[1] // 2))
                for nm, g, r in zip(self.names, bufs[:self.n], bufs[self.n:])]
        lands = [lax.empty((3,) + h.shape[1:], h.dtype) for h in pair]
        return self._start("scatter", pair + lands, _plan_scatter_chips, 3 * self.n, ())

    def start_join(self, after):
        bufs = self._wait(after)
        total = [_add_chips(h, r, self.place_arr, name=f"grad_add_chips_{nm}", tm=min(256, h.shape[1]))
                 for nm, h, r in zip(self.names, bufs[:self.n], bufs[self.n:])]
        return self._start("join", total, _plan_join_halves, self.n, ())

    def finish(self, after):
        return dict(zip(self.names, self._wait(after)))


def _all_gather_small(v):
    m_per, n = v.shape

    def body(x_ref, out_ref, send_sems, recv_sems, local_sem):
        x, y, c, chips = _place()
        me, sibling = (x, y, c), (x, y, 1 - c)

        def rows(px, py, pc):
            return out_ref.at[4 * px + 2 * py + pc]

        def copy(k, block, to, src=None):
            return _remote(rows(*block) if src is None else src, rows(*block), send_sems.at[k], recv_sems.at[k], to)

        mine = pltpu.make_async_copy(x_ref, rows(*me), local_sem)
        mine.start()
        first = [copy(0, me, sibling, src=x_ref)]
        first += [copy(1 + j, me, (*chip, c), src=x_ref) for j, chip in enumerate(chips)]
        for cp in first:
            cp.start()
        passed = [copy(4 + j, (*chip, c), sibling) for j, chip in enumerate(chips)]
        for j, chip in enumerate(chips):
            copy(1 + j, (*chip, c), me).wait_recv()
            passed[j].start()
        copy(0, sibling, me).wait_recv()
        for j, chip in enumerate(chips):
            copy(4 + j, (*chip, 1 - c), me).wait_recv()
        for cp in first + passed:
            cp.wait_send()
        mine.wait()

    return pl.pallas_call(
        body, name="gather_small_grads",
        out_shape=jax.ShapeDtypeStruct((8, m_per, n), v.dtype),
        in_specs=[pl.BlockSpec(memory_space=pltpu.VMEM)], out_specs=pl.BlockSpec(memory_space=pltpu.VMEM),
        scratch_shapes=[pltpu.SemaphoreType.DMA((7,)), pltpu.SemaphoreType.DMA((7,)), pltpu.SemaphoreType.DMA],
        compiler_params=pltpu.CompilerParams(vmem_limit_bytes=VMEM_LIMIT),
    )(v)


def _sum8(v, *, name):
    _, m, n = v.shape

    def body(v_ref, o_ref):
        acc = v_ref[0]
        for d in range(1, 8):
            acc = acc + v_ref[d]
        o_ref[...] = acc

    return pl.pallas_call(body, name=name, out_shape=jax.ShapeDtypeStruct((m, n), F32),
                          compiler_params=pltpu.CompilerParams(vmem_limit_bytes=VMEM_LIMIT))(v)


def _local_step(x, target, norm_w, q_norm_w, k_norm_w, sinks, a_re, a_im, log_dt, b_re, b_im, c_re, c_im, d_skip,
                b_glu, io):
    seq = x.shape[0]
    qw2 = jnp.tile(q_norm_w.reshape(1, HEAD_DIM), (1, HEADS_PER_TILE))
    kw2 = jnp.tile(k_norm_w.reshape(1, HEAD_DIM), (1, HEADS_PER_TILE))
    nw, bg = norm_w.reshape(1, D_MODEL), b_glu.reshape(1, D_MODEL)
    dsk = d_skip.reshape(1, SSM_W)

    h, rstd = _rms_fwd(x, nw, deps=io.begin())
    proj, w_in4 = io.projection(h)
    attn, lse, ya_in = _attn2_fwd(proj, qw2, kw2, sinks, deps=io.after_proj(proj))
    w_ap4 = io.weight("w_attn_proj", ya_in)
    w_glu4, w_sp4, w_out = io.weight("w_glu", ya_in), io.weight("w_ssm_proj", ya_in), io.weight("w_out", ya_in)
    y_a = _mm(ya_in, w_ap4, mode="nn", name="mm_attn_proj", tm=2048, tn=512, tk=ATTN_W, b_blocked=True,
              rows_outer=True)

    flat_a = (a_re.reshape(1, N_STATES), a_im.reshape(1, N_STATES), jnp.repeat(log_dt, STATE).reshape(1, N_STATES))
    coef = _ssm_params_fwd(*flat_a)
    bre_blk, bim_blk = _block_diag_b(b_re).astype(_MXU), _block_diag_b(b_im).astype(_MXU)
    cre_blk, cim_blk = _block_diag_c(c_re).astype(_MXU), _block_diag_c(c_im).astype(_MXU)
    u_scan = _to_scan_order(proj[:, OFF_U * CW:OFF_U * CW + SSM_W])
    y_scan, yg_scan, s_re, s_im, i_re, i_im = _ssm_fwd(u_scan, bre_blk, bim_blk, cre_blk, cim_blk, dsk, coef)
    yg = _from_scan_order(yg_scan)
    glu = _mm(yg, w_glu4, mode="nn", name="mm_glu", tm=2048, tn=512, tk=SSM_W, b_blocked=True, rows_outer=True)

    def gate_s(ga, gb, ba, bb, z):
        return ((ga + ba) * _sigmoid(gb + bb) * (z * _sigmoid(z)),)

    (ys_in,) = _ew(gate_s, [(glu, "mat", 0), (glu, "mat", 2), (bg, "row", 0), (bg, "row", 2), (proj, "mat", OFF_Z)],
                   [(SSM_W, _MXU)], rows=seq, ncol=2, name="ew_ssm_gate")
    y_s = _mm(ys_in, w_sp4, mode="nn", name="mm_ssm_proj", tm=2048, tn=512, tk=SSM_W, b_blocked=True,
              rows_outer=True)

    def merge(ga, gs, ya, ys):
        return (_sigmoid(ga) * ya + _sigmoid(gs) * ys,)

    (merged,) = _ew(merge, [(proj, "mat", OFF_GA), (proj, "mat", OFF_GS), (y_a, "mat", 0), (y_s, "mat", 0)],
                    [(D_MODEL, _MXU)], rows=seq, ncol=4, name="ew_merge")
    dout, dout_b, sq = _mm_out_loss(merged, w_out, x, target)
    loss = 0.5 * jnp.sum(sq) / D_MODEL

    d_ya, d_ys, d_ga, d_gs = _mm_merge_bwd(dout_b, w_out, proj, y_a, y_s)
    g_w_out = _mm(merged, dout_b, mode="tn", name="mm_g_w_out", tm=1024, tn=D_MODEL, tk=1024, out_dtype=_WIRE)

    d_ya_in = _mm(d_ya, w_ap4, mode="nt", name="mm_d_attn_gate", tm=2048, tn=ATTN_W, tk=512, b_blocked=True)
    g_w_ap = _mm(ya_in, d_ya, mode="tn", name="mm_g_w_attn_proj", tm=ATTN_W, tn=D_MODEL, tk=2048, out_dtype=_WIRE,
                 out_blocked=True)

    d_ys_in = _mm(d_ys, w_sp4, mode="nt", name="mm_d_ssm_gate", tm=2048, tn=SSM_W, tk=512, b_blocked=True)
    g_w_sp = _mm(ys_in, d_ys, mode="tn", name="mm_g_w_ssm_proj", tm=SSM_W, tn=D_MODEL, tk=2048, out_dtype=_WIRE,
                 out_blocked=True)

    def gate_s_bwd(dv, ga, gb, ba, bb, z):
        a, sb = ga + ba, _sigmoid(gb + bb)
        f, df = _silu_and_grad(z)
        dga = dv * sb * f
        dgb = dv * a * f * sb * (1.0 - sb)
        return dga, dgb, dv * a * sb * df, _colsum(dga), _colsum(dgb)

    d_glu_a, d_glu_b, d_z, g_bga, g_bgb = _ew(
        gate_s_bwd, [(d_ys_in, "mat", 0), (glu, "mat", 0), (glu, "mat", 2), (bg, "row", 0), (bg, "row", 2),
                     (proj, "mat", OFF_Z)],
        [(SSM_W, _MXU)] * 3, rows=seq, ncol=2, n_acc=2, name="ew_ssm_gate_bwd")
    d_glu = jnp.concatenate([d_glu_a, d_glu_b], axis=1)
    d_yg = _mm(d_glu, w_glu4, mode="nt", name="mm_d_gelu", tm=2048, tn=SSM_W, tk=512, b_blocked=True)
    g_w_glu = _mm(yg, d_glu, mode="tn", name="mm_g_w_glu", tm=SSM_W, tn=D_MODEL, tk=2048, out_dtype=_WIRE, out_blocked=True)
    dep = io.later_grads(dict(w_attn_proj=g_w_ap, w_glu=g_w_glu, w_ssm_proj=g_w_sp,
                              w_out=g_w_out.reshape(N_CHIPS, D_MODEL // N_CHIPS, D_MODEL)))

    d_front, g_qw2, g_kw2, g_sk = _attn2_bwd(proj, qw2, kw2, sinks, lse, attn, d_ya_in, deps=dep)
    dep = io.before_scan_backward([d_front])
    (du_scan, g_bre, g_bim, g_cre, g_cim, g_dsk, g_abr, g_abi, g_cfr, g_cfi) = _ssm_bwd(
        _to_scan_order(d_yg), y_scan, u_scan, s_re, s_im, i_re, i_im, bre_blk, bim_blk, cre_blk, cim_blk, dsk, coef,
        deps=dep)
    g_are, g_aim, g_ldt = _ssm_params_bwd(*flat_a, g_abr, g_abi, g_cfr, g_cfi)
    g_are, g_aim = g_are.reshape(N_GROUPS, STATE), g_aim.reshape(N_GROUPS, STATE)
    g_ldt = g_ldt.reshape(N_GROUPS, STATE).sum(axis=1)
    d_u = _from_scan_order(du_scan)

    d_proj = jnp.concatenate([d_front, d_u.astype(_MXU), d_z, d_ga, d_gs], axis=1)
    dep = io.before_input_projection_grad([d_proj]) + io.small_grads(dict(
        q_norm_w=g_qw2[0, :HEAD_DIM] + g_qw2[0, HEAD_DIM:], k_norm_w=g_kw2[0, :HEAD_DIM] + g_kw2[0, HEAD_DIM:],
        sinks=g_sk.reshape(N_Q_HEADS), A_re=g_are, A_im=g_aim, log_dt=g_ldt,
        B_re=_diag_of_b(g_bre), B_im=_diag_of_b(g_bim), C_re=_diag_of_c(g_cre), C_im=_diag_of_c(g_cim),
        D_skip=g_dsk.reshape(N_GROUPS, GROUP), b_glu=jnp.concatenate([g_bga, g_bgb], axis=1).reshape(D_MODEL)))
    g_w_in = _mm(h, d_proj, mode="tn", name="mm_g_w_in", tm=1024, tn=IN_W // 4, tk=1024, out_dtype=_WIRE,
                 out_blocked=True, deps=dep)
    dep = io.input_projection_grad(g_w_in)
    d_h = _mm(d_proj, w_in4, mode="nt", name="mm_d_h", tm=512, tn=D_MODEL, tk=IN_W // 4, b_blocked=True, deps=dep)
    grad_x, g_nw = _rms_bwd(d_h, x, rstd, nw, dout)
    return loss, grad_x, g_nw.reshape(D_MODEL)


_SMALL = ["norm_w", "q_norm_w", "k_norm_w", "sinks", "A_re", "A_im", "log_dt", "B_re", "B_im", "C_re", "C_im",
          "D_skip", "b_glu"]
_BIG = ["w_in", "w_attn_proj", "w_glu", "w_ssm_proj", "w_out"]
_LATER = _BIG[1:]
_RELATIONS = ("flip_x", "flip_y", "flip_xy")
_ORDER = ["norm_w", "w_in", "q_norm_w", "k_norm_w", "sinks", "w_attn_proj", "A_re", "A_im", "log_dt", "B_re", "B_im",
          "C_re", "C_im", "D_skip", "w_glu", "b_glu", "w_ssm_proj", "w_out"]
_PACK_W = 1024


def _packed_rows(size):
    unit = SUBLANES * _PACK_W
    return -(-size // unit) * SUBLANES


def _pack_small(d, names):
    parts = []
    for n in names:
        flat = d[n].reshape(-1).astype(F32)
        rows = _packed_rows(flat.shape[0])
        parts.append(jnp.pad(flat, (0, rows * _PACK_W - flat.shape[0])).reshape(rows, _PACK_W))
    return jnp.concatenate(parts, axis=0)


def _unpack_small(packed, like, names):
    out, pos = {}, 0
    for n in names:
        rows = _packed_rows(like[n].size)
        out[n] = packed[pos:pos + rows].reshape(-1)[:like[n].size].reshape(like[n].shape)
        pos += rows
    return out


def _place_block(v, index_arr, *, name):
    rows, cols = v.shape

    def body(i_ref, v_ref, o_ref):
        o_ref[...] = v_ref[...]

    return pl.pallas_call(
        body, name=name,
        grid_spec=pltpu.PrefetchScalarGridSpec(
            num_scalar_prefetch=1, grid=(1,),
            in_specs=[pl.BlockSpec((rows, cols), lambda i, d: (0, 0))],
            out_specs=pl.BlockSpec((None, rows, cols), lambda i, d: (d[0], 0, 0))),
        out_shape=jax.ShapeDtypeStruct((8, rows, cols), v.dtype),
        compiler_params=_params(("arbitrary",)),
    )(index_arr, v)


def _plan_all_to_all(refs):
    (land,) = refs
    x, y, c, _ = _place()
    own = land.at[4 * x + 2 * y + c]
    copies = []
    for fx, fy, fc in [(0, 0, 1), (0, 1, 0), (0, 1, 1), (1, 0, 0), (1, 0, 1), (1, 1, 0), (1, 1, 1)]:
        px, py, pc = (1 - x) if fx else x, (1 - y) if fy else y, (1 - c) if fc else c
        copies.append((own, own, (px, py, pc), land.at[4 * px + 2 * py + pc]))
    return copies


def _as2d(a):
    return a.reshape(1, -1) if a.ndim == 1 else a


def _adamw_whole(w, g, m, v, *, name):
    shape = w.shape
    w, g, m, v = _as2d(w), _as2d(g), _as2d(m), _as2d(v)

    def body(w_ref, g_ref, m_ref, v_ref, d_ref, nm_ref, nv_ref):
        d_ref[...], nm_ref[...], nv_ref[...] = _adamw_math(w_ref[...], g_ref[...], m_ref[...], v_ref[...])

    outs = pl.pallas_call(body, name=name, out_shape=[jax.ShapeDtypeStruct(w.shape, F32)] * 3)(w, g, m, v)
    return [o.reshape(shape) for o in outs]


class _Exchanges:
    def __init__(self, w, m, v):
        self.w, self.m, self.v = w, m, v
        self.grads, self.delta, self.new_m, self.new_v = {}, {}, {}, {}

    def _adamw(self, names, deps):
        for n in names:
            self.delta[n], self.new_m[n], self.new_v[n] = _adamw(
                self.w[n], self.grads[n], self.m[n], self.v[n], name=f"adamw_{n}", tm=128, deps=deps)

    def begin(self):
        chip = (2 * lax.axis_index("x") + lax.axis_index("y")).astype(jnp.int32).reshape(1)
        full = {n: _place_shard(self.w[n], chip, name=f"place_{n}") for n in _BIG}
        self.later_full = [full[n] for n in _LATER]
        self.w_in_sems, self.w_in_buf, token = _copies_start("gather_w_in_direct_start", [full["w_in"]],
                                                             _plan_relay_direct, 2)
        return [token]

    def projection(self, h):
        x, y = lax.axis_index("x"), lax.axis_index("y")
        blks = [jnp.asarray(b, jnp.int32).reshape(1)
                for b in (2 * x + y, 2 * (1 - x) + y, 2 * x + (1 - y), 2 * (1 - x) + (1 - y))]
        bufs = self.w_in_buf
        proj = _mm_chip_block(h, bufs[0], blks[0], None, name="mm_proj_own")
        relay, token = [], proj
        for k, tag in enumerate(_RELATIONS[:2]):
            bufs = _copies_wait(f"gather_w_in_direct_{tag}_wait", bufs, self.w_in_sems, _plan_relay_direct, [token],
                                which=(k,))
            plan = functools.partial(_plan_relay_forward, k=k)
            sems, bufs, token = _copies_start(f"gather_w_in_relay_{tag}_start", bufs, plan, 2)
            relay.append((sems, plan))
        self.rest = _copies_start("gather_ici_rest_start", self.later_full, _plan_gather_ici, 3 * len(_LATER),
                                  after=[token])
        token = self.rest[2]
        for k, tag in enumerate(_RELATIONS[:2]):
            bufs = _copies_wait(f"gather_w_in_hand_{tag}_wait", bufs, relay[k][0], relay[k][1], [token], which=(1,))
            token = proj = _mm_chip_block(h, bufs[0], blks[1 + k], proj, name=f"mm_proj_{tag}")
        for k, tag in enumerate(_RELATIONS[:2]):
            bufs = _copies_wait(f"gather_w_in_relay_{tag}_wait", bufs, relay[k][0], relay[k][1], [token], which=(0,))
        sems, bufs, token = _copies_start("gather_w_in_last_start", bufs, _plan_relay_last, 1)
        bufs = _copies_wait("gather_w_in_last_wait", bufs, sems, _plan_relay_last, [token])
        proj = _mm_chip_block(h, bufs[0], blks[3], proj, name="mm_proj_flip_xy")
        return proj, bufs[0]

    def weight(self, name, after):
        if self.rest is not None:
            sems, bufs = self.rest
            later = dict(zip(_LATER, _copies_wait("gather_d2d_rest_wait", bufs, sems, _plan_gather_d2d, [after])))
            later["w_out"] = later["w_out"].reshape(D_MODEL, D_MODEL)
            self.later, self.rest = later, None
        return self.later[name]

    def after_proj(self, proj):
        sems, bufs, _ = self.rest
        bufs = _copies_wait("gather_ici_rest_wait", bufs, sems, _plan_gather_ici, [proj])
        sems, bufs, token = _copies_start("gather_d2d_rest_start", bufs, _plan_gather_d2d, 3 * len(_LATER))
        self.rest = (sems, bufs)
        return [token]

    def later_grads(self, grads):
        self.rs_later = _ReduceScatter("later", _LATER, [grads[n] for n in _LATER])
        return self.rs_later.start_swap()

    def before_scan_backward(self, after):
        return self.rs_later.start_scatter(after)

    def before_input_projection_grad(self, after):
        return self.rs_later.start_join(after)

    def input_projection_grad(self, g_w_in):
        self.grads.update(self.rs_later.finish([g_w_in]))
        self.rs_in = _ReduceScatter("w_in", ["w_in"], [g_w_in])
        self._adamw(_LATER, self.rs_in.start_swap())
        return self.rs_in.start_scatter([self.delta[n] for n in _LATER])

    def _adamw_small(self, names):
        for n in names:
            self.delta[n], self.new_m[n], self.new_v[n] = _adamw_whole(
                self.w[n], self.grads[n], self.m[n], self.v[n], name=f"adamw_{n}")

    def small_grads(self, grads):
        me = (4 * lax.axis_index("x") + 2 * lax.axis_index("y") + lax.axis_index("c")).astype(jnp.int32).reshape(1)
        land = _place_block(_pack_small(grads, _SMALL[1:]), me, name="place_small_grads")
        self.small = _copies_start("gather_small_start", [land], _plan_all_to_all, 7)
        return [self.small[2]]

    def finish(self, g_norm_w, loss, after):
        join = self.rs_in.start_join(after)
        sems, bufs, _ = self.small
        (land,) = _copies_wait("gather_small_wait", bufs, sems, _plan_all_to_all, join)
        self.grads.update(_unpack_small(_sum8(land, name="sum_small_grads"), self.w, _SMALL[1:]))
        self._adamw_small(_SMALL[1:])
        rows = _packed_rows(g_norm_w.size)
        late = jnp.concatenate([_pack_small(dict(norm_w=g_norm_w), _SMALL[:1]),
                                jnp.pad(loss.reshape(1, 1), ((0, SUBLANES - 1), (0, _PACK_W - 1)))], axis=0)
        late = _sum8(_all_gather_small(late), name="sum_norm_w_grad_and_loss")
        self.grads.update(_unpack_small(late[:rows], self.w, _SMALL[:1]))
        self._adamw_small(_SMALL[:1])
        self.grads.update(self.rs_in.finish([self.delta[_SMALL[0]]]))
        self._adamw(["w_in"], ())
        return late[rows, 0]


def kernel(x, norm_w, w_in, q_norm_w, k_norm_w, sinks, w_attn_proj, A_re, A_im, log_dt, B_re, B_im, C_re, C_im, D_skip, w_glu, b_glu, w_ssm_proj, w_out, loss_target, m_norm_w, m_w_in, m_q_norm_w, m_k_norm_w, m_sinks, m_w_attn_proj, m_A_re, m_A_im, m_log_dt, m_B_re, m_B_im, m_C_re, m_C_im, m_D_skip, m_w_glu, m_b_glu, m_w_ssm_proj, m_w_out, v_norm_w, v_w_in, v_q_norm_w, v_k_norm_w, v_sinks, v_w_attn_proj, v_A_re, v_A_im, v_log_dt, v_B_re, v_B_im, v_C_re, v_C_im, v_D_skip, v_w_glu, v_b_glu, v_w_ssm_proj, v_w_out):
    w = dict(norm_w=norm_w, w_in=w_in, q_norm_w=q_norm_w, k_norm_w=k_norm_w, sinks=sinks, w_attn_proj=w_attn_proj,
             A_re=A_re, A_im=A_im, log_dt=log_dt, B_re=B_re, B_im=B_im, C_re=C_re, C_im=C_im, D_skip=D_skip,
             w_glu=w_glu, b_glu=b_glu, w_ssm_proj=w_ssm_proj, w_out=w_out)
    m = dict(norm_w=m_norm_w, w_in=m_w_in, q_norm_w=m_q_norm_w, k_norm_w=m_k_norm_w, sinks=m_sinks,
             w_attn_proj=m_w_attn_proj, A_re=m_A_re, A_im=m_A_im, log_dt=m_log_dt, B_re=m_B_re, B_im=m_B_im,
             C_re=m_C_re, C_im=m_C_im, D_skip=m_D_skip, w_glu=m_w_glu, b_glu=m_b_glu, w_ssm_proj=m_w_ssm_proj,
             w_out=m_w_out)
    v = dict(norm_w=v_norm_w, w_in=v_w_in, q_norm_w=v_q_norm_w, k_norm_w=v_k_norm_w, sinks=v_sinks,
             w_attn_proj=v_w_attn_proj, A_re=v_A_re, A_im=v_A_im, log_dt=v_log_dt, B_re=v_B_re, B_im=v_B_im,
             C_re=v_C_re, C_im=v_C_im, D_skip=v_D_skip, w_glu=v_w_glu, b_glu=v_b_glu, w_ssm_proj=v_w_ssm_proj,
             w_out=v_w_out)

    io = _Exchanges(w, m, v)
    loss, grad_x, g_norm_w = _local_step(x[0], loss_target[0], norm_w, q_norm_w, k_norm_w, sinks, A_re, A_im, log_dt,
                                         B_re, B_im, C_re, C_im, D_skip, b_glu, io)
    loss = io.finish(g_norm_w, loss, [grad_x])
    grads, delta, new_m, new_v = io.grads, io.delta, io.new_m, io.new_v

    return (loss, grad_x[None], *[grads[n] for n in _ORDER], *[delta[n] for n in _ORDER],
            *[new_m[n] for n in _ORDER], *[new_v[n] for n in _ORDER])
```

```python
import functools
import math

import jax
import jax.numpy as jnp
from jax import lax
from jax.experimental import pallas as pl
from jax.experimental.pallas import tpu as pltpu

F32 = jnp.float32
_MXU = jnp.bfloat16
_WIRE = jnp.bfloat16

LANES = 128
SUBLANES = 8
VMEM_LIMIT = 56 * 1024 * 1024

D_MODEL = 2048
HEAD_DIM = 64
N_Q_HEADS = 16
N_KV_HEADS = 4
Q_PER_KV = 4
ATTN_W = 1024
KV_W = 256
WINDOW = 128
SSM_W = 1024
GROUP = 16
N_GROUPS = 64
STATE = 64
N_STATES = N_GROUPS * STATE
IN_W = 8704
NORM_EPS = 1e-6
N_CHIPS = 4
CW = 512
OFF_AGATE, OFF_U, OFF_Z, OFF_GA, OFF_GS = 3, 5, 7, 9, 13

SSM_T = 256
SSM_L = SSM_T // SUBLANES
SSM_JB = 8
SSM_SB = N_STATES // SSM_JB

ADAM_LR, ADAM_B1, ADAM_B2, ADAM_EPS, ADAM_WD, ADAM_STEP = 0.001, 0.9, 0.999, 1e-08, 0.01, 10

MESH = pl.DeviceIdType.MESH
_ANY = pl.BlockSpec(memory_space=pl.ANY)


def _params(sem=None):
    return pltpu.CompilerParams(dimension_semantics=sem, vmem_limit_bytes=VMEM_LIMIT)


def _mm(a, b, *, mode, name, tm, tn, tk, out_dtype=F32, b_blocked=False, out_blocked=False, rows_outer=False,
        deps=()):
    nd = len(deps)
    if mode == "tn":
        K, M = a.shape
    else:
        M, K = a.shape
    if mode == "nn":
        N = b.shape[0] * b.shape[2] if b_blocked else b.shape[1]
    elif mode == "nt":
        N = b.shape[1] if b_blocked else b.shape[0]
    else:
        N = b.shape[1]
    tm, tn, tk = min(tm, M), min(tn, N), min(tk, K)
    nj, ni, nk = N // tn, M // tm, K // tk
    assert nj * tn == N and ni * tm == M and nk * tk == K, (name, M, N, K)
    dims = {"nn": (((1,), (0,)), ((), ())), "nt": (((1,), (1,)), ((), ())), "tn": (((0,), (0,)), ((), ()))}[mode]

    if mode == "tn":
        a_spec = pl.BlockSpec((tk, tm), lambda j, i, k: (k, i))
    else:
        a_spec = pl.BlockSpec((tm, tk), lambda j, i, k: (i, k))
    if mode == "nn":
        if b_blocked:
            assert b.shape[0] == nj and b.shape[2] == tn
            b_spec = pl.BlockSpec((None, tk, tn), lambda j, i, k: (j, k, 0))
        else:
            b_spec = pl.BlockSpec((tk, tn), lambda j, i, k: (k, j))
    elif mode == "nt":
        if b_blocked:
            assert b.shape[0] == nk and b.shape[2] == tk
            b_spec = pl.BlockSpec((None, tn, tk), lambda j, i, k: (k, j, 0))
        else:
            b_spec = pl.BlockSpec((tn, tk), lambda j, i, k: (j, k))
    else:
        b_spec = pl.BlockSpec((tk, tn), lambda j, i, k: (k, j))
    whole_out = out_blocked and nj == 1
    if whole_out:
        assert ni == 1
        o_spec = pl.BlockSpec((N_CHIPS, tm, tn // N_CHIPS), lambda j, i, k: (0, 0, 0))
        o_shape = jax.ShapeDtypeStruct((N_CHIPS, M, tn // N_CHIPS), out_dtype)
    elif out_blocked:
        assert nj == N_CHIPS
        o_spec = pl.BlockSpec((None, tm, tn), lambda j, i, k: (j, i, 0))
        o_shape = jax.ShapeDtypeStruct((nj, M, tn), out_dtype)
    else:
        o_spec = pl.BlockSpec((tm, tn), lambda j, i, k: (i, j))
        o_shape = jax.ShapeDtypeStruct((M, N), out_dtype)
    use_acc = nk > 1 and (out_dtype != F32 or whole_out)

    def body(a_ref, b_ref, *rest):
        o_ref, scratch = rest[nd], rest[nd + 1:]

        def product():
            return lax.dot_general(a_ref[...].astype(_MXU), b_ref[...].astype(_MXU), dims, preferred_element_type=F32)

        def write(result):
            if whole_out:
                w = tn // N_CHIPS
                for c in range(N_CHIPS):
                    o_ref[c] = result[:, c * w:(c + 1) * w].astype(o_ref.dtype)
            else:
                o_ref[...] = result.astype(o_ref.dtype)

        if nk == 1:
            write(product())
            return
        k = pl.program_id(2)
        acc = scratch[0] if use_acc else o_ref

        @pl.when(k == 0)
        def _():
            acc[...] = jnp.zeros_like(acc)

        acc[...] += product()

        if use_acc:
            @pl.when(k == nk - 1)
            def _():
                write(acc[...])

    specs = [a_spec, b_spec, o_spec]
    grid = (nj, ni, nk)
    if rows_outer:
        specs = [pl.BlockSpec(s.block_shape, lambda i, j, k, f=s.index_map: f(j, i, k)) for s in specs]
        grid = (ni, nj, nk)
    return pl.pallas_call(
        body, name=name, grid=grid, in_specs=specs[:2] + [_ANY] * nd, out_specs=specs[2],
        out_shape=o_shape, scratch_shapes=[pltpu.VMEM((tm, tn), F32)] if use_acc else [],
        compiler_params=_params(("parallel", "parallel", "arbitrary")),
    )(a, b, *deps)


def _mm_chip_block(a, b4, blk, prev, *, name, tm=512, deps=()):
    M, K = a.shape
    nchip, _, C = b4.shape
    tm = min(tm, M)
    extra = ([] if prev is None else [prev]) + list(deps)

    def body(blk_ref, a_ref, b_ref, *rest):
        rest[-1][...] = jnp.dot(a_ref[...].astype(_MXU), b_ref[...].astype(_MXU), preferred_element_type=F32)

    return pl.pallas_call(
        body, name=name,
        grid_spec=pltpu.PrefetchScalarGridSpec(
            num_scalar_prefetch=1, grid=(M // tm,),
            in_specs=[pl.BlockSpec((tm, K), lambda i, c: (i, 0)), pl.BlockSpec((None, K, C), lambda i, c: (c[0], 0, 0))]
            + [_ANY] * len(extra),
            out_specs=pl.BlockSpec((tm, C), lambda i, c: (i, c[0]))),
        out_shape=jax.ShapeDtypeStruct((M, nchip * C), F32),
        input_output_aliases={} if prev is None else {3: 0},
        compiler_params=_params(("arbitrary",)),
    )(blk, a, b4, *extra)


def _mm_out_loss(merged, w_out, x, target, *, tm=256):
    rows, d = x.shape

    def body(m_ref, w_ref, x_ref, t_ref, d_ref, db_ref, sq_ref):
        mo = jnp.dot(m_ref[...].astype(_MXU), w_ref[...].astype(_MXU), preferred_element_type=F32)
        err = (x_ref[...] + mo) - t_ref[...]
        dout = err * (1.0 / d)
        d_ref[...] = dout
        db_ref[...] = dout.astype(db_ref.dtype)
        part = _colsum(err * err)
        i = pl.program_id(0)

        @pl.when(i == 0)
        def _():
            sq_ref[...] = part

        @pl.when(i > 0)
        def _():
            sq_ref[...] += part

    tile = pl.BlockSpec((tm, d), lambda i: (i, 0))
    return pl.pallas_call(
        body, name="mm_out_loss", grid=(rows // tm,),
        in_specs=[tile, pl.BlockSpec((d, d), lambda i: (0, 0)), tile, tile],
        out_specs=[tile, tile, pl.BlockSpec((1, d), lambda i: (0, 0))],
        out_shape=[jax.ShapeDtypeStruct((rows, d), F32), jax.ShapeDtypeStruct((rows, d), _MXU),
                   jax.ShapeDtypeStruct((1, d), F32)],
        compiler_params=_params(("arbitrary",)),
    )(merged, w_out, x, target)


def _mm_merge_bwd(dout_b, w_out, proj, y_a, y_s, *, tm=256):
    rows, d = y_a.shape
    ncol = d // CW

    def body(do_ref, w_ref, *refs):
        ga_refs, gs_refs = refs[:ncol], refs[ncol:2 * ncol]
        ya_ref, ys_ref, dya_ref, dys_ref, dg_ref = refs[2 * ncol:]
        dm = lax.dot_general(do_ref[...].astype(_MXU), w_ref[...].astype(_MXU), _NT, preferred_element_type=F32)
        for j in range(ncol):
            cols = slice(j * CW, (j + 1) * CW)
            dmj = dm[:, cols]
            sa, ss = _sigmoid(ga_refs[j][...]), _sigmoid(gs_refs[j][...])
            dya_ref[:, cols] = (sa * dmj).astype(dya_ref.dtype)
            dys_ref[:, cols] = (ss * dmj).astype(dys_ref.dtype)
            dg_ref[:, cols] = (dmj * ya_ref[:, cols] * sa * (1.0 - sa)).astype(dg_ref.dtype)
            dg_ref[:, d + j * CW:d + (j + 1) * CW] = (dmj * ys_ref[:, cols] * ss * (1.0 - ss)).astype(dg_ref.dtype)

    tile = pl.BlockSpec((tm, d), lambda i: (i, 0))
    gate = [pl.BlockSpec((tm, CW), lambda i, c=off + j: (i, c)) for off in (OFF_GA, OFF_GS) for j in range(ncol)]
    both = pl.BlockSpec((pl.Element(tm), pl.Element(2 * d)), lambda i: (i * tm, OFF_GA * CW))
    return pl.pallas_call(
        body, name="mm_merge_bwd", grid=(rows // tm,),
        in_specs=[tile, pl.BlockSpec((d, d), lambda i: (0, 0))] + gate + [tile, tile],
        out_specs=[tile, tile, both],
        out_shape=[jax.ShapeDtypeStruct((rows, d), _MXU)] * 2 + [jax.ShapeDtypeStruct((rows, IN_W), _MXU)],
        compiler_params=_params(("arbitrary",)),
    )(dout_b, w_out, *([proj] * (2 * ncol)), y_a, y_s)


def _ew(fn, ins, outs, *, rows, ncol, name, n_acc=0, tm=512, deps=(), into=None):
    deps = list(deps) + ([into[1]] if into else [])
    n_in, n_out, nd = len(ins), len(outs), len(deps)
    tm = min(tm, rows)
    in_specs = []
    for _, kind, col0 in ins:
        if kind == "mat":
            in_specs.append(pl.BlockSpec((tm, CW), lambda j, i, c0=col0: (i, c0 + j)))
        else:
            in_specs.append(pl.BlockSpec((1, CW), lambda j, i, c0=col0: (0, c0 + j)))
    out_specs = [pl.BlockSpec((tm, CW), lambda j, i: (i, j)) for _ in outs]
    out_shape = [jax.ShapeDtypeStruct((rows, w), dt) for w, dt in outs]
    if into:
        out_specs[into[0]] = pl.BlockSpec((tm, CW), lambda j, i, c0=into[2]: (i, c0 + j))
        out_shape[into[0]] = jax.ShapeDtypeStruct(into[1].shape, into[1].dtype)
    for _ in range(n_acc):
        out_specs.append(pl.BlockSpec((1, CW), lambda j, i: (0, j)))
        out_shape.append(jax.ShapeDtypeStruct((1, ncol * CW), F32))

    def body(*refs):
        vals = fn(*[r[...] for r in refs[:n_in]])
        refs = refs[n_in + nd:]
        for r, v in zip(refs[:n_out], vals[:n_out]):
            r[...] = v.astype(r.dtype)
        i = pl.program_id(1)
        for r, v in zip(refs[n_out:], vals[n_out:]):
            @pl.when(i == 0)
            def _(r=r, v=v):
                r[...] = v

            @pl.when(i > 0)
            def _(r=r, v=v):
                r[...] += v

    res = pl.pallas_call(
        body, name=name, grid=(ncol, rows // tm), in_specs=in_specs + [_ANY] * nd, out_specs=out_specs,
        out_shape=out_shape, input_output_aliases={n_in + nd - 1: into[0]} if into else {},
        compiler_params=_params(("parallel", "arbitrary")),
    )(*[a for a, _, _ in ins], *deps)
    return res


def _colsum(v):
    return jnp.sum(v, axis=0, keepdims=True)


def _sigmoid(v):
    return jax.nn.sigmoid(v)


def _silu_and_grad(v):
    s = _sigmoid(v)
    return v * s, s * (1.0 + v * (1.0 - s))


def _rms_fwd(x, w, *, tm=512, deps=()):
    rows, d = x.shape
    nd = len(deps)

    def body(x_ref, w_ref, *rest):
        h_ref, r_ref = rest[nd:]
        xv = x_ref[...]
        r = lax.rsqrt(jnp.mean(xv * xv, axis=-1, keepdims=True) + NORM_EPS)
        h_ref[...] = (xv * r * w_ref[...]).astype(h_ref.dtype)
        r_ref[...] = r

    return pl.pallas_call(
        body, name="rms_fwd", grid=(rows // tm,),
        in_specs=[pl.BlockSpec((tm, d), lambda i: (i, 0)), pl.BlockSpec((1, d), lambda i: (0, 0))] + [_ANY] * nd,
        out_specs=[pl.BlockSpec((tm, d), lambda i: (i, 0)), pl.BlockSpec((tm, 1), lambda i: (i, 0))],
        out_shape=[jax.ShapeDtypeStruct((rows, d), _MXU), jax.ShapeDtypeStruct((rows, 1), F32)],
        compiler_params=_params(("arbitrary",)),
    )(x, w, *deps)


def _rms_bwd(dh, x, rstd, w, dout, *, tm=256):
    rows, d = x.shape

    def body(dh_ref, x_ref, r_ref, w_ref, do_ref, gx_ref, gw_ref):
        dhv, xv, r, wv = dh_ref[...], x_ref[...], r_ref[...], w_ref[...]
        xr = xv * r
        t = jnp.mean(dhv * wv * xr, axis=-1, keepdims=True)
        gx_ref[...] = do_ref[...] + r * (wv * dhv - xr * t)
        part = _colsum(dhv * xr)
        i = pl.program_id(0)

        @pl.when(i == 0)
        def _():
            gw_ref[...] = part

        @pl.when(i > 0)
        def _():
            gw_ref[...] += part

    return pl.pallas_call(
        body, name="rms_bwd", grid=(rows // tm,),
        in_specs=[pl.BlockSpec((tm, d), lambda i: (i, 0)), pl.BlockSpec((tm, d), lambda i: (i, 0)),
                  pl.BlockSpec((tm, 1), lambda i: (i, 0)), pl.BlockSpec((1, d), lambda i: (0, 0)),
                  pl.BlockSpec((tm, d), lambda i: (i, 0))],
        out_specs=[pl.BlockSpec((tm, d), lambda i: (i, 0)), pl.BlockSpec((1, d), lambda i: (0, 0))],
        out_shape=[jax.ShapeDtypeStruct((rows, d), F32), jax.ShapeDtypeStruct((1, d), F32)],
        compiler_params=_params(("arbitrary",)),
    )(dh, x, rstd, w, dout)


_NT = (((1,), (1,)), ((), ()))
_TN = (((0,), (0,)), ((), ()))


QKV_W = ATTN_W + 2 * KV_W
HEADS_PER_TILE = LANES // HEAD_DIM


def _low_half(rows):
    return lax.broadcasted_iota(jnp.int32, (rows, LANES), 1) < HEAD_DIM


def _pair_mean(t, low):
    m_lo = jnp.sum(jnp.where(low, t, 0.0), axis=-1, keepdims=True)
    m_hi = jnp.sum(jnp.where(low, 0.0, t), axis=-1, keepdims=True)
    return jnp.where(low, m_lo, m_hi) * (1.0 / HEAD_DIM)


def _pair_rstd(t, low):
    return lax.rsqrt(_pair_mean(t * t, low) + NORM_EPS)


def _dup_half(t, hi, low):
    swapped = pltpu.roll(t, HEAD_DIM, 1)
    return jnp.where(low, swapped, t) if hi else jnp.where(low, t, swapped)


def _fold_halves(t):
    return t + pltpu.roll(t, HEAD_DIM, 1)


def _split_heads(t, low):
    return [jnp.where(low, t, 0.0), jnp.where(low, 0.0, t)]


def _stacked_band_mask(n):
    rows = Q_PER_KV * WINDOW
    qi = lax.broadcasted_iota(jnp.int32, (rows, 2 * WINDOW), 0) % WINDOW + WINDOW
    kj = lax.broadcasted_iota(jnp.int32, (rows, 2 * WINDOW), 1)
    diff = qi - kj
    first_key = jnp.where(n > 0, 0, WINDOW)
    return (diff >= 0) & (diff < WINDOW) & (kj >= first_key)


def _stacked_sinks(sink_ref, g):
    blk = lax.broadcasted_iota(jnp.int32, (Q_PER_KV * WINDOW, 1), 0) // WINDOW
    col = jnp.full((Q_PER_KV * WINDOW, 1), sink_ref[Q_PER_KV * g], F32)
    for r in range(1, Q_PER_KV):
        col = jnp.where(blk == r, sink_ref[Q_PER_KV * g + r], col)
    return col


def _attn_in_specs(nblk, rev):
    def cur(n):
        return (nblk - 1 - n) if rev else n

    q_spec = pl.BlockSpec((WINDOW, ATTN_W), lambda n: (cur(n), 0))
    kvc_spec = pl.BlockSpec((WINDOW, 2 * KV_W), lambda n: (cur(n), ATTN_W // (2 * KV_W)))
    kvp_spec = pl.BlockSpec((WINDOW, 2 * KV_W), lambda n: (jnp.maximum(cur(n) - 1, 0), ATTN_W // (2 * KV_W)))
    w_spec = pl.BlockSpec((1, LANES), lambda n: (0, 0))
    l_spec = pl.BlockSpec((WINDOW, N_Q_HEADS), lambda n: (cur(n), 0))
    gate_specs = [pl.BlockSpec((WINDOW, CW), lambda n, col=OFF_AGATE + j: (cur(n), col)) for j in range(ATTN_W // CW)]
    return q_spec, kvc_spec, kvp_spec, w_spec, l_spec, gate_specs


def _attn2_fwd(proj, qw2, kw2, sinks, deps=()):
    seq = proj.shape[0]
    nblk = seq // WINDOW
    scale = 1.0 / math.sqrt(HEAD_DIM)
    q_spec, kvc_spec, kvp_spec, w_spec, l_spec, gate_specs = _attn_in_specs(nblk, False)
    nd, ng = len(deps), len(gate_specs)

    def body(sink_ref, q_ref, kvc_ref, kvp_ref, qw_ref, kw_ref, *rest):
        gate_refs = rest[:ng]
        o_ref, lse_ref, ya_ref = rest[ng + nd:]
        n = pl.program_id(0)
        low, low2 = _low_half(WINDOW), _low_half(2 * WINDOW)
        valid = _stacked_band_mask(n)
        head_lane = lax.broadcasted_iota(jnp.int32, (WINDOW, N_Q_HEADS), 1)
        kv = jnp.concatenate([kvp_ref[...], kvc_ref[...]], axis=0)
        qwv, kwv = qw_ref[...], kw_ref[...]
        lse_blk = jnp.zeros((WINDOW, N_Q_HEADS), F32)
        for t in range(N_KV_HEADS // HEADS_PER_TILE):
            kt = kv[:, t * LANES:(t + 1) * LANES]
            vt = kv[:, KV_W + t * LANES:KV_W + (t + 1) * LANES]
            kn = kt * _pair_rstd(kt, low2) * kwv
            for hi in range(HEADS_PER_TILE):
                g = HEADS_PER_TILE * t + hi
                kdup = _dup_half(kn, hi, low2).astype(_MXU)
                vdup = _dup_half(vt, hi, low2).astype(_MXU)
                stack = []
                for tq in (2 * g, 2 * g + 1):
                    qt = q_ref[:, tq * LANES:(tq + 1) * LANES]
                    stack += _split_heads(qt * _pair_rstd(qt, low) * qwv, low)
                qs = jnp.concatenate(stack, axis=0).astype(_MXU)
                s = lax.dot_general(qs, kdup, _NT, preferred_element_type=F32) * scale
                s = jnp.where(valid, s, -1e30)
                sink = _stacked_sinks(sink_ref, g)
                m = jnp.maximum(jnp.max(s, axis=-1, keepdims=True), sink)
                e = jnp.exp(s - m)
                z = jnp.sum(e, axis=-1, keepdims=True) + jnp.exp(sink - m)
                o = jnp.dot((e / z).astype(_MXU), vdup, preferred_element_type=F32)
                for i, tq in enumerate((2 * g, 2 * g + 1)):
                    o_ref[:, tq * LANES:(tq + 1) * LANES] = jnp.where(
                        low, o[2 * i * WINDOW:(2 * i + 1) * WINDOW], o[(2 * i + 1) * WINDOW:(2 * i + 2) * WINDOW])
                lse = m + jnp.log(z)
                for r in range(Q_PER_KV):
                    lse_blk = jnp.where(head_lane == Q_PER_KV * g + r, lse[r * WINDOW:(r + 1) * WINDOW], lse_blk)
        lse_ref[...] = lse_blk
        for j, g_ref in enumerate(gate_refs):
            cols = slice(j * CW, (j + 1) * CW)
            gate = g_ref[...]
            ya_ref[:, cols] = (o_ref[:, cols] * (gate * _sigmoid(gate))).astype(ya_ref.dtype)

    return pl.pallas_call(
        body, name="attn_fwd", grid=(nblk,),
        in_specs=[pl.BlockSpec(memory_space=pltpu.SMEM), q_spec, kvc_spec, kvp_spec, w_spec, w_spec] + gate_specs
        + [_ANY] * nd,
        out_specs=[q_spec, l_spec, q_spec],
        out_shape=[jax.ShapeDtypeStruct((seq, ATTN_W), F32), jax.ShapeDtypeStruct((seq, N_Q_HEADS), F32),
                   jax.ShapeDtypeStruct((seq, ATTN_W), _MXU)],
        compiler_params=_params(("arbitrary",)),
    )(sinks, proj, proj, proj, qw2, kw2, *([proj] * ng), *deps)


def _attn2_bwd(proj, qw2, kw2, sinks, lse, attn, dya, d_proj, deps=()):
    seq = proj.shape[0]
    nblk = seq // WINDOW
    scale = 1.0 / math.sqrt(HEAD_DIM)
    q_spec, kvc_spec, kvp_spec, w_spec, l_spec, gate_specs = _attn_in_specs(nblk, True)
    s_spec = pl.BlockSpec((1, N_Q_HEADS), lambda n: (0, 0))
    d_spec = pl.BlockSpec((WINDOW, QKV_W + ATTN_W), lambda n: (nblk - 1 - n, 0))
    deps = list(deps) + [d_proj]
    nd, ng = len(deps), len(gate_specs)

    def body(sink_ref, q_ref, kvc_ref, kvp_ref, qw_ref, kw_ref, lse_ref, attn_ref, dya_ref, *rest):
        gate_refs = rest[:ng]
        d_ref, dqw_ref, dkw_ref, dsk_ref, carry, do_ref = rest[ng + nd:]
        step = pl.program_id(0)
        n = nblk - 1 - step

        @pl.when(step == 0)
        def _():
            carry[...] = jnp.zeros_like(carry)
            dqw_ref[...] = jnp.zeros_like(dqw_ref)
            dkw_ref[...] = jnp.zeros_like(dkw_ref)
            dsk_ref[...] = jnp.zeros_like(dsk_ref)

        for j, g_ref in enumerate(gate_refs):
            cols = slice(j * CW, (j + 1) * CW)
            f, df = _silu_and_grad(g_ref[...])
            dv = dya_ref[:, cols]
            do_ref[:, cols] = dv * f
            d_ref[:, QKV_W + j * CW:QKV_W + (j + 1) * CW] = (dv * attn_ref[:, cols] * df).astype(d_ref.dtype)

        low, low2 = _low_half(WINDOW), _low_half(2 * WINDOW)
        valid = _stacked_band_mask(n)
        head_lane = lax.broadcasted_iota(jnp.int32, (WINDOW, N_Q_HEADS), 1)
        sink_lane = lax.broadcasted_iota(jnp.int32, (1, N_Q_HEADS), 1)
        kv = jnp.concatenate([kvp_ref[...], kvc_ref[...]], axis=0)
        qwv, kwv = qw_ref[...], kw_ref[...]
        lse_blk = lse_ref[...]
        dqw = jnp.zeros((1, LANES), F32)
        dkw = jnp.zeros((1, LANES), F32)
        dsk = jnp.zeros((1, N_Q_HEADS), F32)
        for t in range(N_KV_HEADS // HEADS_PER_TILE):
            kt = kv[:, t * LANES:(t + 1) * LANES]
            vt = kv[:, KV_W + t * LANES:KV_W + (t + 1) * LANES]
            rk = _pair_rstd(kt, low2)
            kn = kt * rk * kwv
            dkn_t = jnp.zeros((2 * WINDOW, LANES), F32)
            dv_t = jnp.zeros((2 * WINDOW, LANES), F32)
            for hi in range(HEADS_PER_TILE):
                g = HEADS_PER_TILE * t + hi
                kdup = _dup_half(kn, hi, low2).astype(_MXU)
                vdup = _dup_half(vt, hi, low2).astype(_MXU)
                tiles = (2 * g, 2 * g + 1)
                qx, rq, stack, dstack, lse_rows = [], [], [], [], []
                for tq in tiles:
                    qt = q_ref[:, tq * LANES:(tq + 1) * LANES]
                    r = _pair_rstd(qt, low)
                    rq.append(r)
                    qx.append(qt * r)
                    stack += _split_heads(qx[-1] * qwv, low)
                    dstack += _split_heads(do_ref[:, tq * LANES:(tq + 1) * LANES], low)
                for r in range(Q_PER_KV):
                    lse_rows.append(jnp.sum(jnp.where(head_lane == Q_PER_KV * g + r, lse_blk, 0.0), axis=-1, keepdims=True))
                qs = jnp.concatenate(stack, axis=0).astype(_MXU)
                dos = jnp.concatenate(dstack, axis=0).astype(_MXU)
                lse_col = jnp.concatenate(lse_rows, axis=0)
                s = lax.dot_general(qs, kdup, _NT, preferred_element_type=F32) * scale
                s = jnp.where(valid, s, -1e30)
                p = jnp.exp(s - lse_col)
                dp = lax.dot_general(dos, vdup, _NT, preferred_element_type=F32)
                dsum = jnp.sum(p * dp, axis=-1, keepdims=True)
                ds = (p * (dp - dsum) * scale).astype(_MXU)
                dsink = -jnp.exp(_stacked_sinks(sink_ref, g) - lse_col) * dsum
                for r in range(Q_PER_KV):
                    dsk = dsk + jnp.where(sink_lane == Q_PER_KV * g + r, _colsum(dsink[r * WINDOW:(r + 1) * WINDOW]), 0.0)
                dv_g = _fold_halves(lax.dot_general(p.astype(_MXU), dos, _TN, preferred_element_type=F32))
                dkn_g = _fold_halves(lax.dot_general(ds, qs, _TN, preferred_element_type=F32))
                dv_t = jnp.where(low2, dv_t, dv_g) if hi else jnp.where(low2, dv_g, dv_t)
                dkn_t = jnp.where(low2, dkn_t, dkn_g) if hi else jnp.where(low2, dkn_g, dkn_t)
                dqn = jnp.dot(ds, kdup, preferred_element_type=F32)
                for i, tq in enumerate(tiles):
                    dqn_t = jnp.where(low, dqn[2 * i * WINDOW:(2 * i + 1) * WINDOW],
                                      dqn[(2 * i + 1) * WINDOW:(2 * i + 2) * WINDOW])
                    dq = rq[i] * (qwv * dqn_t - qx[i] * _pair_mean(dqn_t * qwv * qx[i], low))
                    d_ref[:, tq * LANES:(tq + 1) * LANES] = dq.astype(d_ref.dtype)
                    dqw = dqw + _colsum(dqn_t * qx[i])
            k_cols = slice(t * LANES, (t + 1) * LANES)
            v_cols = slice(KV_W + t * LANES, KV_W + (t + 1) * LANES)
            dkn_c = dkn_t[WINDOW:] + carry[:, k_cols]
            rc = rk[WINDOW:]
            kx = kt[WINDOW:] * rc
            dk = rc * (kwv * dkn_c - kx * _pair_mean(dkn_c * kwv * kx, low))
            d_ref[:, ATTN_W + t * LANES:ATTN_W + (t + 1) * LANES] = dk.astype(d_ref.dtype)
            d_ref[:, ATTN_W + KV_W + t * LANES:ATTN_W + KV_W + (t + 1) * LANES] = (
                dv_t[WINDOW:] + carry[:, v_cols]).astype(d_ref.dtype)
            carry[:, k_cols] = dkn_t[:WINDOW]
            carry[:, v_cols] = dv_t[:WINDOW]
            dkw = dkw + _colsum(dkn_c * kx)
        dqw_ref[...] += dqw
        dkw_ref[...] += dkw
        dsk_ref[...] += dsk

    return pl.pallas_call(
        body, name="attn_bwd", grid=(nblk,),
        in_specs=[pl.BlockSpec(memory_space=pltpu.SMEM), q_spec, kvc_spec, kvp_spec, w_spec, w_spec, l_spec, q_spec,
                  q_spec] + gate_specs + [_ANY] * nd,
        out_specs=[d_spec, w_spec, w_spec, s_spec],
        out_shape=[jax.ShapeDtypeStruct(d_proj.shape, d_proj.dtype), jax.ShapeDtypeStruct((1, LANES), F32),
                   jax.ShapeDtypeStruct((1, LANES), F32), jax.ShapeDtypeStruct((1, N_Q_HEADS), F32)],
        input_output_aliases={9 + ng + nd - 1: 0},
        scratch_shapes=[pltpu.VMEM((WINDOW, 2 * KV_W), F32), pltpu.VMEM((WINDOW, ATTN_W), F32)],
        compiler_params=_params(("arbitrary",)),
    )(sinks, proj, proj, proj, qw2, kw2, lse, attn, dya, *([proj] * ng), *deps)


def _ssm_discretise(a_re, a_im, log_dt):
    dt = jnp.exp(log_dt)
    mag = jnp.exp(dt * a_re)
    ab_re = mag * jnp.cos(dt * a_im)
    ab_im = mag * jnp.sin(dt * a_im)
    num_re = ab_re - 1.0
    num_im = ab_im
    den = a_re * a_re + a_im * a_im
    cf_re = (num_re * a_re + num_im * a_im) / den
    cf_im = (num_im * a_re - num_re * a_im) / den
    return ab_re, ab_im, cf_re, cf_im


def _ssm_params_fwd(a_re, a_im, log_dt):
    shp = jax.ShapeDtypeStruct(a_re.shape, F32)

    def body(are_ref, aim_ref, ldt_ref, abr_ref, abi_ref, cfr_ref, cfi_ref, alr_ref, ali_ref):
        abr, abi, cfr, cfi = _ssm_discretise(are_ref[...], aim_ref[...], ldt_ref[...])
        abr_ref[...], abi_ref[...], cfr_ref[...], cfi_ref[...] = abr, abi, cfr, cfi
        pr, pi = abr, abi
        for _ in range(int(math.log2(SSM_L))):
            pr, pi = pr * pr - pi * pi, 2.0 * pr * pi
        alr_ref[...], ali_ref[...] = pr, pi

    return pl.pallas_call(body, name="ssm_params_fwd", out_shape=[shp] * 6)(a_re, a_im, log_dt)


def _ssm_params_bwd(a_re, a_im, log_dt, d_abr, d_abi, d_cfr, d_cfi):
    def body(are_ref, aim_ref, ldt_ref, g0, g1, g2, g3, dare_ref, daim_ref, dldt_ref):
        _, vjp = jax.vjp(_ssm_discretise, are_ref[...], aim_ref[...], ldt_ref[...])
        dare_ref[...], daim_ref[...], dldt_ref[...] = vjp((g0[...], g1[...], g2[...], g3[...]))

    return pl.pallas_call(
        body, name="ssm_params_bwd",
        out_shape=[jax.ShapeDtypeStruct(a_re.shape, F32), jax.ShapeDtypeStruct(a_im.shape, F32),
                   jax.ShapeDtypeStruct(log_dt.shape, F32)],
    )(a_re, a_im, log_dt, d_abr, d_abi, d_cfr, d_cfi)


def _scan_cols(j):
    return pl.ds(j * SSM_SB, SSM_SB)


def _rows8(r):
    return pl.ds(pl.multiple_of(r * SUBLANES, SUBLANES), SUBLANES)


def _bcast8(row):
    return jnp.broadcast_to(row, (SUBLANES, row.shape[-1]))


SCAN_UNROLL = 8


def _scan_loop(n, step, init):
    def trip(o, carry):
        for i in range(SCAN_UNROLL):
            carry = step(o * SCAN_UNROLL + i, carry)
        return carry

    return lax.fori_loop(0, n // SCAN_UNROLL, trip, init)


def _ssm_fwd(u, b_re, b_im, c_re, c_im, d_skip, coef):
    seq = u.shape[0]
    nc = seq // SSM_T
    T, L = SSM_T, SSM_L

    def body(u_ref, bre_ref, bim_ref, cre_ref, cim_ref, d_ref, are_ref, aim_ref, cfr_ref, cfi_ref, alr_ref, ali_ref,
             y_ref, yg_ref, sre_ref, sim_ref, ire_ref, iim_ref, car_re, car_im, end_re, end_im):
        c = pl.program_id(0)

        @pl.when(c == 0)
        def _():
            car_re[...] = jnp.zeros_like(car_re)
            car_im[...] = jnp.zeros_like(car_im)

        for j in range(SSM_JB):
            ub = u_ref[:, j * LANES:(j + 1) * LANES].astype(_MXU)
            bur = jnp.dot(ub, bre_ref[j], preferred_element_type=F32)
            bui = jnp.dot(ub, bim_ref[j], preferred_element_type=F32)
            cfr, cfi = cfr_ref[:, _scan_cols(j)], cfi_ref[:, _scan_cols(j)]
            sre_ref[:, _scan_cols(j)] = cfr * bur - cfi * bui
            sim_ref[:, _scan_cols(j)] = cfr * bui + cfi * bur

        for j in range(SSM_JB):
            cols = _scan_cols(j)
            ar, ai = _bcast8(are_ref[:, cols]), _bcast8(aim_ref[:, cols])

            def step1(r, s, cols=cols, ar=ar, ai=ai):
                sr, si = s
                rows = _rows8(r)
                return (ar * sr - ai * si + sre_ref[rows, cols], ar * si + ai * sr + sim_ref[rows, cols])

            zero = jnp.zeros((SUBLANES, SSM_SB), F32)
            er, ei = _scan_loop(L, step1, (zero, zero))
            end_re[:, cols] = er
            end_im[:, cols] = ei

        alr, ali = alr_ref[...], ali_ref[...]
        cr, ci = car_re[...], car_im[...]
        ire_ref[0:1, :] = cr
        iim_ref[0:1, :] = ci
        for i in range(1, SUBLANES):
            er, ei = end_re[i - 1:i, :], end_im[i - 1:i, :]
            cr, ci = alr * cr - ali * ci + er, alr * ci + ali * cr + ei
            ire_ref[i:i + 1, :] = cr
            iim_ref[i:i + 1, :] = ci

        for j in range(SSM_JB):
            cols = _scan_cols(j)
            ar, ai = _bcast8(are_ref[:, cols]), _bcast8(aim_ref[:, cols])

            def step2(r, s, cols=cols, ar=ar, ai=ai):
                sr, si = s
                rows = _rows8(r)
                nr = ar * sr - ai * si + sre_ref[rows, cols]
                ni = ar * si + ai * sr + sim_ref[rows, cols]
                sre_ref[rows, cols] = nr
                sim_ref[rows, cols] = ni
                return nr, ni

            _scan_loop(L, step2, (ire_ref[:, cols], iim_ref[:, cols]))

        car_re[...] = sre_ref[T - 1:T, :]
        car_im[...] = sim_ref[T - 1:T, :]

        for j in range(SSM_JB):
            cols = _scan_cols(j)
            ch = slice(j * LANES, (j + 1) * LANES)
            y = (jnp.dot(sre_ref[:, cols].astype(_MXU), cre_ref[j], preferred_element_type=F32)
                 - jnp.dot(sim_ref[:, cols].astype(_MXU), cim_ref[j], preferred_element_type=F32))
            y = y + d_ref[:, ch] * u_ref[:, ch]
            y_ref[:, ch] = y
            yg_ref[:, ch] = jax.nn.gelu(y).astype(yg_ref.dtype)

    tok = pl.BlockSpec((T, SSM_W), lambda c: (c, 0))
    st = pl.BlockSpec((T, N_STATES), lambda c: (c, 0))
    ini = pl.BlockSpec((None, SUBLANES, N_STATES), lambda c: (c, 0, 0))
    bsp = pl.BlockSpec((SSM_JB, LANES, SSM_SB), lambda c: (0, 0, 0))
    csp = pl.BlockSpec((SSM_JB, SSM_SB, LANES), lambda c: (0, 0, 0))
    row_w = pl.BlockSpec((1, SSM_W), lambda c: (0, 0))
    row_s = pl.BlockSpec((1, N_STATES), lambda c: (0, 0))
    return pl.pallas_call(
        body, name="ssm_fwd", grid=(nc,),
        in_specs=[tok, bsp, bsp, csp, csp, row_w] + [row_s] * 6,
        out_specs=[tok, tok, st, st, ini, ini],
        out_shape=[jax.ShapeDtypeStruct((seq, SSM_W), F32), jax.ShapeDtypeStruct((seq, SSM_W), _MXU),
                   jax.ShapeDtypeStruct((seq, N_STATES), F32), jax.ShapeDtypeStruct((seq, N_STATES), F32),
                   jax.ShapeDtypeStruct((nc, SUBLANES, N_STATES), F32),
                   jax.ShapeDtypeStruct((nc, SUBLANES, N_STATES), F32)],
        scratch_shapes=[pltpu.VMEM((1, N_STATES), F32), pltpu.VMEM((1, N_STATES), F32),
                        pltpu.VMEM((SUBLANES, N_STATES), F32), pltpu.VMEM((SUBLANES, N_STATES), F32)],
        compiler_params=_params(("arbitrary",)),
    )(u, b_re, b_im, c_re, c_im, d_skip, *coef)


def _ssm_bwd(dyg, y, u, s_re, s_im, i_re, i_im, b_re, b_im, c_re, c_im, d_skip, coef, deps=()):
    seq = u.shape[0]
    nc = seq // SSM_T
    T, L = SSM_T, SSM_L

    def body(dyg_ref, y_ref, u_ref, sre_ref, sim_ref, ire_ref, iim_ref, bre_ref, bim_ref, cre_ref, cim_ref, d_ref,
             are_ref, aim_ref, cfr_ref, cfi_ref, alr_ref, ali_ref, *rest):
        (du_ref, dbre_out, dbim_out, dcre_out, dcim_out, dd_ref, dar_ref, dai_ref, dcfr_ref, dcfi_ref,
         lre, lim, car_re, car_im, end_re, end_im, ini_re, ini_im, dbre_ref, dbim_ref, dcre_ref, dcim_ref,
         dy_ref) = rest[len(deps):]
        step = pl.program_id(0)
        dy_ref[...] = jax.vjp(jax.nn.gelu, y_ref[...])[1](dyg_ref[...])[0]

        @pl.when(step == 0)
        def _():
            car_re[...] = jnp.zeros_like(car_re)
            car_im[...] = jnp.zeros_like(car_im)
            for ref in (dbre_ref, dbim_ref, dcre_ref, dcim_ref, dd_ref, dar_ref, dai_ref, dcfr_ref, dcfi_ref):
                ref[...] = jnp.zeros_like(ref)

        for j in range(SSM_JB):
            dyb = dy_ref[:, j * LANES:(j + 1) * LANES].astype(_MXU)
            lre[:, _scan_cols(j)] = lax.dot_general(dyb, cre_ref[j], _NT, preferred_element_type=F32)
            lim[:, _scan_cols(j)] = -lax.dot_general(dyb, cim_ref[j], _NT, preferred_element_type=F32)

        for j in range(SSM_JB):
            cols = _scan_cols(j)
            ar, ai = _bcast8(are_ref[:, cols]), _bcast8(aim_ref[:, cols])

            def step1(t, s, cols=cols, ar=ar, ai=ai):
                sr, si = s
                rows = _rows8(L - 1 - t)
                return (ar * sr + ai * si + lre[rows, cols], ar * si - ai * sr + lim[rows, cols])

            zero = jnp.zeros((SUBLANES, SSM_SB), F32)
            er, ei = _scan_loop(L, step1, (zero, zero))
            end_re[:, cols] = er
            end_im[:, cols] = ei

        alr, ali = alr_ref[...], ali_ref[...]
        cr, ci = car_re[...], car_im[...]
        ini_re[SUBLANES - 1:SUBLANES, :] = cr
        ini_im[SUBLANES - 1:SUBLANES, :] = ci
        for i in range(SUBLANES - 2, -1, -1):
            er, ei = end_re[i + 1:i + 2, :], end_im[i + 1:i + 2, :]
            cr, ci = alr * cr + ali * ci + er, alr * ci - ali * cr + ei
            ini_re[i:i + 1, :] = cr
            ini_im[i:i + 1, :] = ci

        for j in range(SSM_JB):
            cols = _scan_cols(j)
            ar, ai = _bcast8(are_ref[:, cols]), _bcast8(aim_ref[:, cols])

            def step2(t, s, cols=cols, ar=ar, ai=ai):
                sr, si = s
                rows = _rows8(L - 1 - t)
                nr = ar * sr + ai * si + lre[rows, cols]
                ni = ar * si - ai * sr + lim[rows, cols]
                lre[rows, cols] = nr
                lim[rows, cols] = ni
                return nr, ni

            _scan_loop(L, step2, (ini_re[:, cols], ini_im[:, cols]))

        car_re[...] = lre[0:1, :]
        car_im[...] = lim[0:1, :]

        head, tail, body_rows = slice(0, SUBLANES), slice(SUBLANES, T), slice(0, T - SUBLANES)
        for j in range(SSM_JB):
            cols = _scan_cols(j)
            ch = slice(j * LANES, (j + 1) * LANES)
            lr, li = lre[:, cols], lim[:, cols]
            dar_ref[:, cols] += (_colsum(lre[tail, cols] * sre_ref[body_rows, cols] + lim[tail, cols] * sim_ref[body_rows, cols])
                                 + _colsum(lre[head, cols] * ire_ref[:, cols] + lim[head, cols] * iim_ref[:, cols]))
            dai_ref[:, cols] += (_colsum(lim[tail, cols] * sre_ref[body_rows, cols] - lre[tail, cols] * sim_ref[body_rows, cols])
                                 + _colsum(lim[head, cols] * ire_ref[:, cols] - lre[head, cols] * iim_ref[:, cols]))
            uf = u_ref[:, ch]
            ub = uf.astype(_MXU)
            bur = jnp.dot(ub, bre_ref[j], preferred_element_type=F32)
            bui = jnp.dot(ub, bim_ref[j], preferred_element_type=F32)
            dcfr_ref[:, cols] += _colsum(lr * bur + li * bui)
            dcfi_ref[:, cols] += _colsum(li * bur - lr * bui)
            cfr, cfi = cfr_ref[:, cols], cfi_ref[:, cols]
            dbur = (cfr * lr + cfi * li).astype(_MXU)
            dbui = (cfr * li - cfi * lr).astype(_MXU)
            dyf = dy_ref[:, ch]
            dyb = dyf.astype(_MXU)
            du_ref[:, ch] = (lax.dot_general(dbur, bre_ref[j], _NT, preferred_element_type=F32)
                             + lax.dot_general(dbui, bim_ref[j], _NT, preferred_element_type=F32)
                             + d_ref[:, ch] * dyf)
            dbre_ref[j] += lax.dot_general(ub, dbur, _TN, preferred_element_type=F32)
            dbim_ref[j] += lax.dot_general(ub, dbui, _TN, preferred_element_type=F32)
            dcre_ref[j] += lax.dot_general(sre_ref[:, cols].astype(_MXU), dyb, _TN, preferred_element_type=F32)
            dcim_ref[j] -= lax.dot_general(sim_ref[:, cols].astype(_MXU), dyb, _TN, preferred_element_type=F32)
            dd_ref[:, ch] += _colsum(dyf * uf)

        @pl.when(step == nc - 1)
        def _():
            for acc, out in ((dbre_ref, dbre_out), (dbim_ref, dbim_out), (dcre_ref, dcre_out), (dcim_ref, dcim_out)):
                pltpu.sync_copy(acc, out)

    tok = pl.BlockSpec((T, SSM_W), lambda c: (nc - 1 - c, 0))
    st = pl.BlockSpec((T, N_STATES), lambda c: (nc - 1 - c, 0))
    ini = pl.BlockSpec((None, SUBLANES, N_STATES), lambda c: (nc - 1 - c, 0, 0))
    bsp = pl.BlockSpec((SSM_JB, LANES, SSM_SB), lambda c: (0, 0, 0))
    csp = pl.BlockSpec((SSM_JB, SSM_SB, LANES), lambda c: (0, 0, 0))
    row_w = pl.BlockSpec((1, SSM_W), lambda c: (0, 0))
    row_s = pl.BlockSpec((1, N_STATES), lambda c: (0, 0))
    big = pltpu.VMEM((T, N_STATES), F32)
    one = pltpu.VMEM((1, N_STATES), F32)
    eight = pltpu.VMEM((SUBLANES, N_STATES), F32)
    return pl.pallas_call(
        body, name="ssm_bwd", grid=(nc,),
        in_specs=[tok, tok, tok, st, st, ini, ini, bsp, bsp, csp, csp, row_w] + [row_s] * 6 + [_ANY] * len(deps),
        out_specs=[tok, _ANY, _ANY, _ANY, _ANY, row_w, row_s, row_s, row_s, row_s],
        out_shape=[jax.ShapeDtypeStruct((seq, SSM_W), F32),
                   jax.ShapeDtypeStruct((SSM_JB, LANES, SSM_SB), F32), jax.ShapeDtypeStruct((SSM_JB, LANES, SSM_SB), F32),
                   jax.ShapeDtypeStruct((SSM_JB, SSM_SB, LANES), F32), jax.ShapeDtypeStruct((SSM_JB, SSM_SB, LANES), F32),
                   jax.ShapeDtypeStruct((1, SSM_W), F32)] + [jax.ShapeDtypeStruct((1, N_STATES), F32)] * 4,
        scratch_shapes=[big, big, one, one, eight, eight, eight, eight,
                        pltpu.VMEM((SSM_JB, LANES, SSM_SB), F32), pltpu.VMEM((SSM_JB, LANES, SSM_SB), F32),
                        pltpu.VMEM((SSM_JB, SSM_SB, LANES), F32), pltpu.VMEM((SSM_JB, SSM_SB, LANES), F32),
                        pltpu.VMEM((T, SSM_W), F32)],
        compiler_params=_params(("arbitrary",)),
    )(dyg, y, u, s_re, s_im, i_re, i_im, b_re, b_im, c_re, c_im, d_skip, *coef, *deps)


def _block_diag_b(b):
    t = b.reshape(SSM_JB, 8, STATE, GROUP).transpose(0, 1, 3, 2)
    eye = jnp.eye(8, dtype=b.dtype)
    return (t[:, :, :, None, :] * eye[None, :, None, :, None]).reshape(SSM_JB, LANES, SSM_SB)


def _block_diag_c(c):
    t = c.reshape(SSM_JB, 8, GROUP, STATE).transpose(0, 1, 3, 2)
    eye = jnp.eye(8, dtype=c.dtype)
    return (t[:, :, :, None, :] * eye[None, :, None, :, None]).reshape(SSM_JB, SSM_SB, LANES)


def _diag_of_b(blk):
    t = blk.reshape(SSM_JB, 8, GROUP, 8, STATE)
    d = jnp.sum(t * jnp.eye(8, dtype=blk.dtype)[None, :, None, :, None], axis=3)
    return d.transpose(0, 1, 3, 2).reshape(N_GROUPS, STATE, GROUP)


def _diag_of_c(blk):
    t = blk.reshape(SSM_JB, 8, STATE, 8, GROUP)
    d = jnp.sum(t * jnp.eye(8, dtype=blk.dtype)[None, :, None, :, None], axis=3)
    return d.transpose(0, 1, 3, 2).reshape(N_GROUPS, GROUP, STATE)


def _place_from_scan_order(v, buf, col0):
    seq, w = v.shape
    assert buf.dtype == _MXU

    def body(v_ref, b_ref, o_ref):
        tok = lax.broadcasted_iota(jnp.int32, (SSM_T, SSM_T), 0)
        row = lax.broadcasted_iota(jnp.int32, (SSM_T, SSM_T), 1)
        pick = (row == SUBLANES * (tok % SSM_L) + tok // SSM_L).astype(_MXU)
        o_ref[...] = jnp.dot(pick, v_ref[...].astype(_MXU), preferred_element_type=F32).astype(o_ref.dtype)

    return pl.pallas_call(
        body, name="place_from_scan_order", grid=(seq // SSM_T, w // CW),
        in_specs=[pl.BlockSpec((SSM_T, CW), lambda c, j: (c, j)), _ANY],
        out_specs=pl.BlockSpec((SSM_T, CW), lambda c, j: (c, col0 + j)),
        out_shape=jax.ShapeDtypeStruct(buf.shape, buf.dtype), input_output_aliases={1: 0},
        compiler_params=_params(("arbitrary", "arbitrary")),
    )(v, buf)


def _to_scan_order(v):
    seq, w = v.shape
    return v.reshape(seq // SSM_T, SUBLANES, SSM_L, w).transpose(0, 2, 1, 3).reshape(seq, w)


def _from_scan_order(v):
    seq, w = v.shape
    return v.reshape(seq // SSM_T, SSM_L, SUBLANES, w).transpose(0, 2, 1, 3).reshape(seq, w)


def _adamw_math(w, g, m, v):
    nm = ADAM_B1 * m + (1.0 - ADAM_B1) * g
    nv = ADAM_B2 * v + (1.0 - ADAM_B2) * jnp.square(g)
    m_hat = nm / (1.0 - ADAM_B1 ** ADAM_STEP)
    v_hat = nv / (1.0 - ADAM_B2 ** ADAM_STEP)
    return -ADAM_LR * (m_hat / (jnp.sqrt(v_hat) + ADAM_EPS) + ADAM_WD * w), nm, nv


def _adamw(w, g, m, v, *, name, tm, deps=()):
    rows, cols = w.shape
    nd = len(deps)

    def body(w_ref, g_ref, m_ref, v_ref, *rest):
        d_ref, nm_ref, nv_ref = rest[nd:]
        d_ref[...], nm_ref[...], nv_ref[...] = _adamw_math(w_ref[...], g_ref[...], m_ref[...], v_ref[...])

    spec = pl.BlockSpec((tm, cols), lambda i: (i, 0))
    shp = jax.ShapeDtypeStruct((rows, cols), F32)
    return pl.pallas_call(body, name=name, grid=(rows // tm,), in_specs=[spec] * 4 + [_ANY] * nd,
                          out_specs=[spec] * 3, out_shape=[shp] * 3,
                          compiler_params=_params(("arbitrary",)))(w, g, m, v, *deps)


def _place():
    x, y, c = lax.axis_index("x"), lax.axis_index("y"), lax.axis_index("c")
    chips = [(1 - x, y), (x, 1 - y), (1 - x, 1 - y)]
    return x, y, c, chips


def _remote(src, dst, send_sem, recv_sem, dev):
    return pltpu.make_async_remote_copy(src_ref=src, dst_ref=dst, send_sem=send_sem, recv_sem=recv_sem,
                                        device_id=dev, device_id_type=MESH)


def _place_shard(w, mine_arr, *, name, tm=256):
    rows, cols = w.shape

    def body(m_ref, w_ref, o_ref):
        o_ref[...] = w_ref[...].astype(o_ref.dtype)

    return pl.pallas_call(
        body, name=name,
        grid_spec=pltpu.PrefetchScalarGridSpec(
            num_scalar_prefetch=1, grid=(rows // tm,),
            in_specs=[pl.BlockSpec((tm, cols), lambda i, m: (i, 0))],
            out_specs=pl.BlockSpec((None, tm, cols), lambda i, m: (m[0], i, 0))),
        out_shape=jax.ShapeDtypeStruct((N_CHIPS, rows, cols), _WIRE),
        compiler_params=_params(("arbitrary",)),
    )(mine_arr, w)


_HBM = pl.BlockSpec(memory_space=pltpu.HBM)
_SEM = pl.BlockSpec(memory_space=pltpu.SEMAPHORE)
_EFFECT = pltpu.SideEffectType.DATAFLOW_SIDE_EFFECTING


def _copies_start(name, bufs, plan, count, after=()):
    nb, na = len(bufs), len(after)

    def body(*refs):
        send_sems, recv_sems, token = refs[nb + na], refs[nb + na + 1], refs[-1]
        copies = plan(refs[:nb])
        assert len(copies) == count
        for i, (src, dst, dev, _) in enumerate(copies):
            _remote(src, dst, send_sems.at[i], recv_sems.at[i], dev).start()
        token[...] = jnp.zeros_like(token)

    res = pl.pallas_call(
        body, name=name, in_specs=[_HBM] * nb + [_ANY] * na,
        out_specs=(_SEM, _SEM, *[_HBM] * nb, pl.BlockSpec(memory_space=pltpu.VMEM)),
        out_shape=(pltpu.SemaphoreType.DMA((count,)), pltpu.SemaphoreType.DMA((count,)),
                   *[pltpu.HBM(b.shape, b.dtype) for b in bufs], jax.ShapeDtypeStruct((SUBLANES, LANES), F32)),
        input_output_aliases={i: 2 + i for i in range(nb)},
        compiler_params=pltpu.CompilerParams(has_side_effects=_EFFECT),
    )(*[pltpu.with_memory_space_constraint(b, pltpu.HBM) for b in bufs], *after)
    return (res[0], res[1]), list(res[2:2 + nb]), res[-1]


def _copies_wait(name, bufs, sems, plan, after=(), which=None):
    nb, na = len(bufs), len(after)

    def body(*refs):
        send_sems, recv_sems = refs[nb], refs[nb + 1]
        for i, (src, _, dev, land) in enumerate(plan(refs[:nb])):
            if which is not None and i not in which:
                continue
            cp = _remote(src, land, send_sems.at[i], recv_sems.at[i], dev)
            cp.wait_send()
            cp.wait_recv()

    res = pl.pallas_call(
        body, name=name, in_specs=[_HBM] * nb + [_SEM, _SEM] + [_ANY] * na, out_specs=[_HBM] * nb,
        out_shape=[pltpu.HBM(b.shape, b.dtype) for b in bufs],
        input_output_aliases={i: i for i in range(nb)},
        compiler_params=pltpu.CompilerParams(has_side_effects=_EFFECT),
    )(*bufs, *sems, *after)
    return list(res)


def _plan_gather_ici(fulls, which=(0, 1, 2)):
    x, y, c, chips = _place()
    copies = []
    for f in fulls:
        half = pl.ds(c * (f.shape[1] // 2), f.shape[1] // 2)
        own = f.at[2 * x + y, half]
        for chip in [chips[k] for k in which]:
            copies.append((own, own, (*chip, c), f.at[2 * chip[0] + chip[1], half]))
    return copies


def _plan_gather_d2d(fulls, which=(0, 1, 2)):
    x, y, c, chips = _place()
    copies = []
    for f in fulls:
        r2 = f.shape[1] // 2
        for chip in [chips[k] for k in which]:
            blk = 2 * chip[0] + chip[1]
            landed = f.at[blk, pl.ds(c * r2, r2)]
            copies.append((landed, landed, (x, y, 1 - c), f.at[blk, pl.ds((1 - c) * r2, r2)]))
    return copies


def _plan_relay_direct(fulls):
    (f,) = fulls
    x, y, c, chips = _place()
    half = pl.ds(c * (f.shape[1] // 2), f.shape[1] // 2)
    own = f.at[2 * x + y, half]
    return [(own, own, (*chip, c), f.at[2 * chip[0] + chip[1], half]) for chip in chips[:2]]


def _plan_relay_forward(fulls, k):
    (f,) = fulls
    x, y, c, chips = _place()
    r2 = f.shape[1] // 2
    half, other = pl.ds(c * r2, r2), pl.ds((1 - c) * r2, r2)
    quarter = pl.ds(c * r2 + k * (r2 // 2), r2 // 2)
    blk, far = 2 * chips[k][0] + chips[k][1], 2 * chips[2][0] + chips[2][1]
    passed, landed = f.at[blk, quarter], f.at[blk, half]
    return [(passed, passed, (*chips[1 - k], c), f.at[far, quarter]), (landed, landed, (x, y, 1 - c), f.at[blk, other])]


def _plan_relay_last(fulls):
    (f,) = fulls
    x, y, c, chips = _place()
    r2 = f.shape[1] // 2
    far = 2 * chips[2][0] + chips[2][1]
    landed = f.at[far, pl.ds(c * r2, r2)]
    return [(landed, landed, (x, y, 1 - c), f.at[far, pl.ds((1 - c) * r2, r2)])]


def _plan_swap_halves(refs):
    x, y, c, _ = _place()
    n = len(refs) // 2
    copies = []
    for g, land in zip(refs[:n], refs[n:]):
        r2 = g.shape[1] // 2
        copies.append((g.at[:, pl.ds((1 - c) * r2, r2), :], land, (x, y, 1 - c), land))
    return copies


def _plan_scatter_chips(refs):
    x, y, c, chips = _place()
    n = len(refs) // 2
    copies = []
    for h, land in zip(refs[:n], refs[n:]):
        for k, chip in enumerate(chips):
            copies.append((h.at[2 * chip[0] + chip[1]], land.at[k], (*chip, c), land.at[k]))
    return copies


def _plan_join_halves(totals):
    x, y, c, _ = _place()
    copies = []
    for t in totals:
        r2 = t.shape[0] // 2
        mine = t.at[pl.ds(c * r2, r2)]
        copies.append((mine, mine, (x, y, 1 - c), t.at[pl.ds((1 - c) * r2, r2)]))
    return copies


def _add_sibling_half(g, got, c_arr, *, name, tm):
    _, rows, cols = g.shape
    r2 = rows // 2
    nb = r2 // tm

    def body(c_ref, g_ref, r_ref, o_ref):
        o_ref[...] = (g_ref[...].astype(F32) + r_ref[...].astype(F32)).astype(o_ref.dtype)

    return pl.pallas_call(
        body, name=name,
        grid_spec=pltpu.PrefetchScalarGridSpec(
            num_scalar_prefetch=1, grid=(N_CHIPS, nb),
            in_specs=[pl.BlockSpec((None, tm, cols), lambda b, i, c: (b, c[0] * nb + i, 0)),
                      pl.BlockSpec((None, tm, cols), lambda b, i, c: (b, i, 0))],
            out_specs=pl.BlockSpec((None, tm, cols), lambda b, i, c: (b, i, 0))),
        out_shape=jax.ShapeDtypeStruct((N_CHIPS, r2, cols), _WIRE),
        compiler_params=_params(("arbitrary", "arbitrary")),
    )(c_arr, g, got)


def _add_chips(h, got, place_arr, *, name, tm):
    _, r2, cols = h.shape
    nb = r2 // tm

    def body(p_ref, h_ref, r_ref, o_ref):
        o_ref[...] = ((h_ref[...].astype(F32) + r_ref[0].astype(F32)) + r_ref[1].astype(F32)) + r_ref[2].astype(F32)

    return pl.pallas_call(
        body, name=name,
        grid_spec=pltpu.PrefetchScalarGridSpec(
            num_scalar_prefetch=1, grid=(nb,),
            in_specs=[pl.BlockSpec((None, tm, cols), lambda i, p: (p[0], i, 0)),
                      pl.BlockSpec((3, tm, cols), lambda i, p: (0, i, 0))],
            out_specs=pl.BlockSpec((tm, cols), lambda i, p: (p[1] * nb + i, 0))),
        out_shape=jax.ShapeDtypeStruct((2 * r2, cols), F32),
        compiler_params=_params(("arbitrary",)),
    )(place_arr, h, got)


class _ReduceScatter:
    def __init__(self, tag, names, grads):
        self.tag, self.names, self.n = tag, names, len(names)
        core = lax.axis_index("c").astype(jnp.int32)
        chip = (2 * lax.axis_index("x") + lax.axis_index("y")).astype(jnp.int32)
        self.c_arr, self.place_arr = core.reshape(1), jnp.stack([chip, core])
        self.bufs = list(grads)

    def _start(self, step, bufs, plan, count, after):
        self.plan = plan
        self.step = f"grad_{step}_{self.tag}"
        self.sems, self.bufs, token = _copies_start(self.step + "_start", bufs, plan, count, after)
        return [token]

    def _wait(self, after):
        self.bufs = _copies_wait(self.step + "_wait", self.bufs, self.sems, self.plan, after)
        return self.bufs

    def start_swap(self, after=()):
        lands = [lax.empty((N_CHIPS, g.shape[1] // 2, g.shape[2]), g.dtype) for g in self.bufs]
        return self._start("swap", self.bufs + lands, _plan_swap_halves, self.n, after)

    def start_scatter(self, after):
        bufs = self._wait(after)
        pair = [_add_sibling_half(g, r, self.c_arr, name=f"grad_add_sibling_{nm}", tm=min(256, g.shape[1] // 2))
                for nm, g, r in zip(self.names, bufs[:self.n], bufs[self.n:])]
        lands = [lax.empty((3,) + h.shape[1:], h.dtype) for h in pair]
        return self._start("scatter", pair + lands, _plan_scatter_chips, 3 * self.n, ())

    def start_join(self, after):
        bufs = self._wait(after)
        total = [_add_chips(h, r, self.place_arr, name=f"grad_add_chips_{nm}", tm=min(256, h.shape[1]))
                 for nm, h, r in zip(self.names, bufs[:self.n], bufs[self.n:])]
        return self._start("join", total, _plan_join_halves, self.n, ())

    def finish(self, after):
        return dict(zip(self.names, self._wait(after)))


def _all_gather_small(v):
    m_per, n = v.shape

    def body(x_ref, out_ref, send_sems, recv_sems, local_sem):
        x, y, c, chips = _place()
        me, sibling = (x, y, c), (x, y, 1 - c)

        def rows(px, py, pc):
            return out_ref.at[4 * px + 2 * py + pc]

        def copy(k, block, to, src=None):
            return _remote(rows(*block) if src is None else src, rows(*block), send_sems.at[k], recv_sems.at[k], to)

        mine = pltpu.make_async_copy(x_ref, rows(*me), local_sem)
        mine.start()
        first = [copy(0, me, sibling, src=x_ref)]
        first += [copy(1 + j, me, (*chip, c), src=x_ref) for j, chip in enumerate(chips)]
        for cp in first:
            cp.start()
        passed = [copy(4 + j, (*chip, c), sibling) for j, chip in enumerate(chips)]
        for j, chip in enumerate(chips):
            copy(1 + j, (*chip, c), me).wait_recv()
            passed[j].start()
        copy(0, sibling, me).wait_recv()
        for j, chip in enumerate(chips):
            copy(4 + j, (*chip, 1 - c), me).wait_recv()
        for cp in first + passed:
            cp.wait_send()
        mine.wait()

    return pl.pallas_call(
        body, name="gather_small_grads",
        out_shape=jax.ShapeDtypeStruct((8, m_per, n), v.dtype),
        in_specs=[pl.BlockSpec(memory_space=pltpu.VMEM)], out_specs=pl.BlockSpec(memory_space=pltpu.VMEM),
        scratch_shapes=[pltpu.SemaphoreType.DMA((7,)), pltpu.SemaphoreType.DMA((7,)), pltpu.SemaphoreType.DMA],
        compiler_params=pltpu.CompilerParams(vmem_limit_bytes=VMEM_LIMIT),
    )(v)


def _sum8(v, *, name):
    _, m, n = v.shape

    def body(v_ref, o_ref):
        acc = v_ref[0]
        for d in range(1, 8):
            acc = acc + v_ref[d]
        o_ref[...] = acc

    return pl.pallas_call(body, name=name, out_shape=jax.ShapeDtypeStruct((m, n), F32),
                          compiler_params=pltpu.CompilerParams(vmem_limit_bytes=VMEM_LIMIT))(v)


def _local_step(x, target, norm_w, q_norm_w, k_norm_w, sinks, a_re, a_im, log_dt, b_re, b_im, c_re, c_im, d_skip,
                b_glu, io):
    seq = x.shape[0]
    qw2 = jnp.tile(q_norm_w.reshape(1, HEAD_DIM), (1, HEADS_PER_TILE))
    kw2 = jnp.tile(k_norm_w.reshape(1, HEAD_DIM), (1, HEADS_PER_TILE))
    nw, bg = norm_w.reshape(1, D_MODEL), b_glu.reshape(1, D_MODEL)
    dsk = d_skip.reshape(1, SSM_W)

    h, rstd = _rms_fwd(x, nw, deps=io.begin())
    proj, w_in4 = io.projection(h)
    attn, lse, ya_in = _attn2_fwd(proj, qw2, kw2, sinks, deps=io.after_proj(proj))
    w_ap4 = io.weight("w_attn_proj", ya_in)
    w_glu4, w_sp4, w_out = io.weight("w_glu", ya_in), io.weight("w_ssm_proj", ya_in), io.weight("w_out", ya_in)
    y_a = _mm(ya_in, w_ap4, mode="nn", name="mm_attn_proj", tm=2048, tn=512, tk=ATTN_W, b_blocked=True,
              rows_outer=True)

    flat_a = (a_re.reshape(1, N_STATES), a_im.reshape(1, N_STATES), jnp.repeat(log_dt, STATE).reshape(1, N_STATES))
    coef = _ssm_params_fwd(*flat_a)
    bre_blk, bim_blk = _block_diag_b(b_re).astype(_MXU), _block_diag_b(b_im).astype(_MXU)
    cre_blk, cim_blk = _block_diag_c(c_re).astype(_MXU), _block_diag_c(c_im).astype(_MXU)
    u_scan = _to_scan_order(proj[:, OFF_U * CW:OFF_U * CW + SSM_W])
    y_scan, yg_scan, s_re, s_im, i_re, i_im = _ssm_fwd(u_scan, bre_blk, bim_blk, cre_blk, cim_blk, dsk, coef)
    yg = _from_scan_order(yg_scan)
    glu = _mm(yg, w_glu4, mode="nn", name="mm_glu", tm=2048, tn=512, tk=SSM_W, b_blocked=True, rows_outer=True)

    def gate_s(ga, gb, ba, bb, z):
        return ((ga + ba) * _sigmoid(gb + bb) * (z * _sigmoid(z)),)

    (ys_in,) = _ew(gate_s, [(glu, "mat", 0), (glu, "mat", 2), (bg, "row", 0), (bg, "row", 2), (proj, "mat", OFF_Z)],
                   [(SSM_W, _MXU)], rows=seq, ncol=2, name="ew_ssm_gate")
    y_s = _mm(ys_in, w_sp4, mode="nn", name="mm_ssm_proj", tm=2048, tn=512, tk=SSM_W, b_blocked=True,
              rows_outer=True)

    def merge(ga, gs, ya, ys):
        return (_sigmoid(ga) * ya + _sigmoid(gs) * ys,)

    (merged,) = _ew(merge, [(proj, "mat", OFF_GA), (proj, "mat", OFF_GS), (y_a, "mat", 0), (y_s, "mat", 0)],
                    [(D_MODEL, _MXU)], rows=seq, ncol=4, name="ew_merge")
    dout, dout_b, sq = _mm_out_loss(merged, w_out, x, target)
    loss = 0.5 * jnp.sum(sq) / D_MODEL

    d_ya, d_ys, d_proj = _mm_merge_bwd(dout_b, w_out, proj, y_a, y_s)
    g_w_out = _mm(merged, dout_b, mode="tn", name="mm_g_w_out", tm=1024, tn=D_MODEL, tk=1024, out_dtype=_WIRE)

    d_ya_in = _mm(d_ya, w_ap4, mode="nt", name="mm_d_attn_gate", tm=2048, tn=ATTN_W, tk=512, b_blocked=True)
    g_w_ap = _mm(ya_in, d_ya, mode="tn", name="mm_g_w_attn_proj", tm=ATTN_W, tn=D_MODEL, tk=2048, out_dtype=_WIRE,
                 out_blocked=True)

    d_ys_in = _mm(d_ys, w_sp4, mode="nt", name="mm_d_ssm_gate", tm=2048, tn=SSM_W, tk=512, b_blocked=True)
    g_w_sp = _mm(ys_in, d_ys, mode="tn", name="mm_g_w_ssm_proj", tm=SSM_W, tn=D_MODEL, tk=2048, out_dtype=_WIRE,
                 out_blocked=True)

    def gate_s_bwd(dv, ga, gb, ba, bb, z):
        a, sb = ga + ba, _sigmoid(gb + bb)
        f, df = _silu_and_grad(z)
        dga = dv * sb * f
        dgb = dv * a * f * sb * (1.0 - sb)
        return dga, dgb, dv * a * sb * df, _colsum(dga), _colsum(dgb)

    d_glu_a, d_glu_b, d_proj, g_bga, g_bgb = _ew(
        gate_s_bwd, [(d_ys_in, "mat", 0), (glu, "mat", 0), (glu, "mat", 2), (bg, "row", 0), (bg, "row", 2),
                     (proj, "mat", OFF_Z)],
        [(SSM_W, _MXU)] * 3, rows=seq, ncol=2, n_acc=2, name="ew_ssm_gate_bwd", into=(2, d_proj, OFF_Z))
    d_glu = jnp.concatenate([d_glu_a, d_glu_b], axis=1)
    d_yg = _mm(d_glu, w_glu4, mode="nt", name="mm_d_gelu", tm=2048, tn=SSM_W, tk=512, b_blocked=True)
    g_w_glu = _mm(yg, d_glu, mode="tn", name="mm_g_w_glu", tm=SSM_W, tn=D_MODEL, tk=2048, out_dtype=_WIRE, out_blocked=True)
    dep = io.later_grads(dict(w_attn_proj=g_w_ap, w_glu=g_w_glu, w_ssm_proj=g_w_sp,
                              w_out=g_w_out.reshape(N_CHIPS, D_MODEL // N_CHIPS, D_MODEL)))

    d_proj, g_qw2, g_kw2, g_sk = _attn2_bwd(proj, qw2, kw2, sinks, lse, attn, d_ya_in, d_proj, deps=dep)
    dep = io.before_scan_backward([d_proj])
    (du_scan, g_bre, g_bim, g_cre, g_cim, g_dsk, g_abr, g_abi, g_cfr, g_cfi) = _ssm_bwd(
        _to_scan_order(d_yg), y_scan, u_scan, s_re, s_im, i_re, i_im, bre_blk, bim_blk, cre_blk, cim_blk, dsk, coef,
        deps=dep)
    g_are, g_aim, g_ldt = _ssm_params_bwd(*flat_a, g_abr, g_abi, g_cfr, g_cfi)
    g_are, g_aim = g_are.reshape(N_GROUPS, STATE), g_aim.reshape(N_GROUPS, STATE)
    g_ldt = g_ldt.reshape(N_GROUPS, STATE).sum(axis=1)
    d_proj = _place_from_scan_order(du_scan, d_proj, OFF_U)
    dep = io.before_input_projection_grad([d_proj]) + io.small_grads(dict(
        q_norm_w=g_qw2[0, :HEAD_DIM] + g_qw2[0, HEAD_DIM:], k_norm_w=g_kw2[0, :HEAD_DIM] + g_kw2[0, HEAD_DIM:],
        sinks=g_sk.reshape(N_Q_HEADS), A_re=g_are, A_im=g_aim, log_dt=g_ldt,
        B_re=_diag_of_b(g_bre), B_im=_diag_of_b(g_bim), C_re=_diag_of_c(g_cre), C_im=_diag_of_c(g_cim),
        D_skip=g_dsk.reshape(N_GROUPS, GROUP), b_glu=jnp.concatenate([g_bga, g_bgb], axis=1).reshape(D_MODEL)))
    g_w_in = _mm(h, d_proj, mode="tn", name="mm_g_w_in", tm=1024, tn=IN_W // 4, tk=1024, out_dtype=_WIRE,
                 out_blocked=True, deps=dep)
    dep = io.input_projection_grad(g_w_in)
    d_h = _mm(d_proj, w_in4, mode="nt", name="mm_d_h", tm=512, tn=D_MODEL, tk=IN_W // 4, b_blocked=True, deps=dep)
    grad_x, g_nw = _rms_bwd(d_h, x, rstd, nw, dout)
    return loss, grad_x, g_nw.reshape(D_MODEL)


_SMALL = ["norm_w", "q_norm_w", "k_norm_w", "sinks", "A_re", "A_im", "log_dt", "B_re", "B_im", "C_re", "C_im",
          "D_skip", "b_glu"]
_BIG = ["w_in", "w_attn_proj", "w_glu", "w_ssm_proj", "w_out"]
_LATER = _BIG[1:]
_RELATIONS = ("flip_x", "flip_y", "flip_xy")
_ORDER = ["norm_w", "w_in", "q_norm_w", "k_norm_w", "sinks", "w_attn_proj", "A_re", "A_im", "log_dt", "B_re", "B_im",
          "C_re", "C_im", "D_skip", "w_glu", "b_glu", "w_ssm_proj", "w_out"]
_PACK_W = 1024


def _packed_rows(size):
    unit = SUBLANES * _PACK_W
    return -(-size // unit) * SUBLANES


def _pack_small(d, names):
    parts = []
    for n in names:
        flat = d[n].reshape(-1).astype(F32)
        rows = _packed_rows(flat.shape[0])
        parts.append(jnp.pad(flat, (0, rows * _PACK_W - flat.shape[0])).reshape(rows, _PACK_W))
    return jnp.concatenate(parts, axis=0)


def _unpack_small(packed, like, names):
    out, pos = {}, 0
    for n in names:
        rows = _packed_rows(like[n].size)
        out[n] = packed[pos:pos + rows].reshape(-1)[:like[n].size].reshape(like[n].shape)
        pos += rows
    return out


def _place_block(v, index_arr, *, name):
    rows, cols = v.shape

    def body(i_ref, v_ref, o_ref):
        o_ref[...] = v_ref[...]

    return pl.pallas_call(
        body, name=name,
        grid_spec=pltpu.PrefetchScalarGridSpec(
            num_scalar_prefetch=1, grid=(1,),
            in_specs=[pl.BlockSpec((rows, cols), lambda i, d: (0, 0))],
            out_specs=pl.BlockSpec((None, rows, cols), lambda i, d: (d[0], 0, 0))),
        out_shape=jax.ShapeDtypeStruct((8, rows, cols), v.dtype),
        compiler_params=_params(("arbitrary",)),
    )(index_arr, v)


def _plan_all_to_all(refs):
    (land,) = refs
    x, y, c, _ = _place()
    own = land.at[4 * x + 2 * y + c]
    copies = []
    for fx, fy, fc in [(0, 0, 1), (0, 1, 0), (0, 1, 1), (1, 0, 0), (1, 0, 1), (1, 1, 0), (1, 1, 1)]:
        px, py, pc = (1 - x) if fx else x, (1 - y) if fy else y, (1 - c) if fc else c
        copies.append((own, own, (px, py, pc), land.at[4 * px + 2 * py + pc]))
    return copies


def _as2d(a):
    return a.reshape(1, -1) if a.ndim == 1 else a


def _adamw_whole(w, g, m, v, *, name):
    shape = w.shape

    def body(w_ref, g_ref, m_ref, v_ref, d_ref, nm_ref, nv_ref):
        d_ref[...], nm_ref[...], nv_ref[...] = _adamw_math(w_ref[...], g_ref[...], m_ref[...], v_ref[...])

    outs = pl.pallas_call(body, name=name, out_shape=[jax.ShapeDtypeStruct(w.shape, F32)] * 3)(w, g, m, v)
    return [o.reshape(shape) for o in outs]


class _Exchanges:
    def __init__(self, w, m, v):
        self.w, self.m, self.v = w, m, v
        self.grads, self.delta, self.new_m, self.new_v = {}, {}, {}, {}

    def _adamw(self, names, deps):
        for n in names:
            self.delta[n], self.new_m[n], self.new_v[n] = _adamw(
                self.w[n], self.grads[n], self.m[n], self.v[n], name=f"adamw_{n}", tm=128, deps=deps)

    def begin(self):
        chip = (2 * lax.axis_index("x") + lax.axis_index("y")).astype(jnp.int32).reshape(1)
        full = {n: _place_shard(self.w[n], chip, name=f"place_{n}") for n in _BIG}
        self.later_full = [full[n] for n in _LATER]
        self.w_in_sems, self.w_in_buf, token = _copies_start("gather_w_in_direct_start", [full["w_in"]],
                                                             _plan_relay_direct, 2)
        return [token]

    def projection(self, h):
        x, y = lax.axis_index("x"), lax.axis_index("y")
        blks = [jnp.asarray(b, jnp.int32).reshape(1)
                for b in (2 * x + y, 2 * (1 - x) + y, 2 * x + (1 - y), 2 * (1 - x) + (1 - y))]
        bufs = self.w_in_buf
        proj = _mm_chip_block(h, bufs[0], blks[0], None, name="mm_proj_own")
        relay, token = [], proj
        for k, tag in enumerate(_RELATIONS[:2]):
            bufs = _copies_wait(f"gather_w_in_direct_{tag}_wait", bufs, self.w_in_sems, _plan_relay_direct, [token],
                                which=(k,))
            plan = functools.partial(_plan_relay_forward, k=k)
            sems, bufs, token = _copies_start(f"gather_w_in_relay_{tag}_start", bufs, plan, 2)
            relay.append((sems, plan))
        self.rest = _copies_start("gather_ici_rest_start", self.later_full, _plan_gather_ici, 3 * len(_LATER),
                                  after=[token])
        token = self.rest[2]
        for k, tag in enumerate(_RELATIONS[:2]):
            bufs = _copies_wait(f"gather_w_in_hand_{tag}_wait", bufs, relay[k][0], relay[k][1], [token], which=(1,))
            token = proj = _mm_chip_block(h, bufs[0], blks[1 + k], proj, name=f"mm_proj_{tag}")
        for k, tag in enumerate(_RELATIONS[:2]):
            bufs = _copies_wait(f"gather_w_in_relay_{tag}_wait", bufs, relay[k][0], relay[k][1], [token], which=(0,))
        sems, bufs, token = _copies_start("gather_w_in_last_start", bufs, _plan_relay_last, 1)
        bufs = _copies_wait("gather_w_in_last_wait", bufs, sems, _plan_relay_last, [token])
        proj = _mm_chip_block(h, bufs[0], blks[3], proj, name="mm_proj_flip_xy")
        return proj, bufs[0]

    def weight(self, name, after):
        if self.rest is not None:
            sems, bufs = self.rest
            later = dict(zip(_LATER, _copies_wait("gather_d2d_rest_wait", bufs, sems, _plan_gather_d2d, [after])))
            later["w_out"] = later["w_out"].reshape(D_MODEL, D_MODEL)
            self.later, self.rest = later, None
        return self.later[name]

    def after_proj(self, proj):
        sems, bufs, _ = self.rest
        bufs = _copies_wait("gather_ici_rest_wait", bufs, sems, _plan_gather_ici, [proj])
        sems, bufs, token = _copies_start("gather_d2d_rest_start", bufs, _plan_gather_d2d, 3 * len(_LATER))
        self.rest = (sems, bufs)
        return [token]

    def later_grads(self, grads):
        self.rs_later = _ReduceScatter("later", _LATER, [grads[n] for n in _LATER])
        return self.rs_later.start_swap()

    def before_scan_backward(self, after):
        return self.rs_later.start_scatter(after)

    def before_input_projection_grad(self, after):
        return self.rs_later.start_join(after)

    def input_projection_grad(self, g_w_in):
        self.grads.update(self.rs_later.finish([g_w_in]))
        self.rs_in = _ReduceScatter("w_in", ["w_in"], [g_w_in])
        self._adamw(_LATER, self.rs_in.start_swap())
        return self.rs_in.start_scatter([self.delta[n] for n in _LATER])

    def _adamw_small(self, names):
        for n in names:
            self.delta[n], self.new_m[n], self.new_v[n] = _adamw_whole(
                self.w[n], self.grads[n], self.m[n], self.v[n], name=f"adamw_{n}")

    def small_grads(self, grads):
        me = (4 * lax.axis_index("x") + 2 * lax.axis_index("y") + lax.axis_index("c")).astype(jnp.int32).reshape(1)
        land = _place_block(_pack_small(grads, _SMALL[1:]), me, name="place_small_grads")
        self.small = _copies_start("gather_small_start", [land], _plan_all_to_all, 7)
        return [self.small[2]]

    def finish(self, g_norm_w, loss, after):
        join = self.rs_in.start_join(after)
        sems, bufs, _ = self.small
        (land,) = _copies_wait("gather_small_wait", bufs, sems, _plan_all_to_all, join)
        self.grads.update(_unpack_small(_sum8(land, name="sum_small_grads"), self.w, _SMALL[1:]))
        self._adamw_small(_SMALL[1:])
        rows = _packed_rows(g_norm_w.size)
        late = jnp.concatenate([_pack_small(dict(norm_w=g_norm_w), _SMALL[:1]),
                                jnp.pad(loss.reshape(1, 1), ((0, SUBLANES - 1), (0, _PACK_W - 1)))], axis=0)
        late = _sum8(_all_gather_small(late), name="sum_norm_w_grad_and_loss")
        self.grads.update(_unpack_small(late[:rows], self.w, _SMALL[:1]))
        self._adamw_small(_SMALL[:1])
        self.grads.update(self.rs_in.finish([self.delta[_SMALL[0]]]))
        self._adamw(["w_in"], ())
        return late[rows, 0]


def kernel(x, norm_w, w_in, q_norm_w, k_norm_w, sinks, w_attn_proj, A_re, A_im, log_dt, B_re, B_im, C_re, C_im, D_skip, w_glu, b_glu, w_ssm_proj, w_out, loss_target, m_norm_w, m_w_in, m_q_norm_w, m_k_norm_w, m_sinks, m_w_attn_proj, m_A_re, m_A_im, m_log_dt, m_B_re, m_B_im, m_C_re, m_C_im, m_D_skip, m_w_glu, m_b_glu, m_w_ssm_proj, m_w_out, v_norm_w, v_w_in, v_q_norm_w, v_k_norm_w, v_sinks, v_w_attn_proj, v_A_re, v_A_im, v_log_dt, v_B_re, v_B_im, v_C_re, v_C_im, v_D_skip, v_w_glu, v_b_glu, v_w_ssm_proj, v_w_out):
    w = dict(norm_w=norm_w, w_in=w_in, q_norm_w=q_norm_w, k_norm_w=k_norm_w, sinks=sinks, w_attn_proj=w_attn_proj,
             A_re=A_re, A_im=A_im, log_dt=log_dt, B_re=B_re, B_im=B_im, C_re=C_re, C_im=C_im, D_skip=D_skip,
             w_glu=w_glu, b_glu=b_glu, w_ssm_proj=w_ssm_proj, w_out=w_out)
    m = dict(norm_w=m_norm_w, w_in=m_w_in, q_norm_w=m_q_norm_w, k_norm_w=m_k_norm_w, sinks=m_sinks,
             w_attn_proj=m_w_attn_proj, A_re=m_A_re, A_im=m_A_im, log_dt=m_log_dt, B_re=m_B_re, B_im=m_B_im,
             C_re=m_C_re, C_im=m_C_im, D_skip=m_D_skip, w_glu=m_w_glu, b_glu=m_b_glu, w_ssm_proj=m_w_ssm_proj,
             w_out=m_w_out)
    v = dict(norm_w=v_norm_w, w_in=v_w_in, q_norm_w=v_q_norm_w, k_norm_w=v_k_norm_w, sinks=v_sinks,
             w_attn_proj=v_w_attn_proj, A_re=v_A_re, A_im=v_A_im, log_dt=v_log_dt, B_re=v_B_re, B_im=v_B_im,
             C_re=v_C_re, C_im=v_C_im, D_skip=v_D_skip, w_glu=v_w_glu, b_glu=v_b_glu, w_ssm_proj=v_w_ssm_proj,
             w_out=v_w_out)

    io = _Exchanges(w, m, v)
    loss, grad_x, g_norm_w = _local_step(x[0], loss_target[0], norm_w, q_norm_w, k_norm_w, sinks, A_re, A_im, log_dt,
                                         B_re, B_im, C_re, C_im, D_skip, b_glu, io)
    loss = io.finish(g_norm_w, loss, [grad_x])
    grads, delta, new_m, new_v = io.grads, io.delta, io.new_m, io.new_v

    return (loss, grad_x[None], *[grads[n] for n in _ORDER], *[delta[n] for n in _ORDER],
            *[new_m[n] for n in _ORDER], *[new_v[n] for n in _ORDER])
```

```python
import functools
import math

import jax
import jax.numpy as jnp
from jax import lax
from jax.experimental import pallas as pl
from jax.experimental.pallas import tpu as pltpu

F32 = jnp.float32
_MXU = jnp.bfloat16
_WIRE = jnp.bfloat16

LANES = 128
SUBLANES = 8
VMEM_LIMIT = 56 * 1024 * 1024

D_MODEL = 2048
HEAD_DIM = 64
N_Q_HEADS = 16
N_KV_HEADS = 4
Q_PER_KV = 4
ATTN_W = 1024
KV_W = 256
WINDOW = 128
SSM_W = 1024
GROUP = 16
N_GROUPS = 64
STATE = 64
N_STATES = N_GROUPS * STATE
IN_W = 8704
NORM_EPS = 1e-6
N_CHIPS = 4
CW = 512
OFF_AGATE, OFF_U, OFF_Z, OFF_GA, OFF_GS = 3, 5, 7, 9, 13

SSM_T = 256
SSM_L = SSM_T // SUBLANES
SSM_JB = 8
SSM_SB = N_STATES // SSM_JB

ADAM_LR, ADAM_B1, ADAM_B2, ADAM_EPS, ADAM_WD, ADAM_STEP = 0.001, 0.9, 0.999, 1e-08, 0.01, 10

MESH = pl.DeviceIdType.MESH
_ANY = pl.BlockSpec(memory_space=pl.ANY)


def _params(sem=None):
    return pltpu.CompilerParams(dimension_semantics=sem, vmem_limit_bytes=VMEM_LIMIT)


def _mm(a, b, *, mode, name, tm, tn, tk, out_dtype=F32, b_blocked=False, out_blocked=False, rows_outer=False,
        deps=()):
    nd = len(deps)
    if mode == "tn":
        K, M = a.shape
    else:
        M, K = a.shape
    if mode == "nn":
        N = b.shape[0] * b.shape[2] if b_blocked else b.shape[1]
    elif mode == "nt":
        N = b.shape[1] if b_blocked else b.shape[0]
    else:
        N = b.shape[1]
    tm, tn, tk = min(tm, M), min(tn, N), min(tk, K)
    nj, ni, nk = N // tn, M // tm, K // tk
    assert nj * tn == N and ni * tm == M and nk * tk == K, (name, M, N, K)
    dims = {"nn": (((1,), (0,)), ((), ())), "nt": (((1,), (1,)), ((), ())), "tn": (((0,), (0,)), ((), ()))}[mode]

    if mode == "tn":
        a_spec = pl.BlockSpec((tk, tm), lambda j, i, k: (k, i))
    else:
        a_spec = pl.BlockSpec((tm, tk), lambda j, i, k: (i, k))
    if mode == "nn":
        if b_blocked:
            assert b.shape[0] == nj and b.shape[2] == tn
            b_spec = pl.BlockSpec((None, tk, tn), lambda j, i, k: (j, k, 0))
        else:
            b_spec = pl.BlockSpec((tk, tn), lambda j, i, k: (k, j))
    elif mode == "nt":
        if b_blocked:
            assert b.shape[0] == nk and b.shape[2] == tk
            b_spec = pl.BlockSpec((None, tn, tk), lambda j, i, k: (k, j, 0))
        else:
            b_spec = pl.BlockSpec((tn, tk), lambda j, i, k: (j, k))
    else:
        b_spec = pl.BlockSpec((tk, tn), lambda j, i, k: (k, j))
    whole_out = out_blocked and nj == 1
    if whole_out:
        assert ni == 1
        o_spec = pl.BlockSpec((N_CHIPS, tm, tn // N_CHIPS), lambda j, i, k: (0, 0, 0))
        o_shape = jax.ShapeDtypeStruct((N_CHIPS, M, tn // N_CHIPS), out_dtype)
    elif out_blocked:
        assert nj == N_CHIPS
        o_spec = pl.BlockSpec((None, tm, tn), lambda j, i, k: (j, i, 0))
        o_shape = jax.ShapeDtypeStruct((nj, M, tn), out_dtype)
    else:
        o_spec = pl.BlockSpec((tm, tn), lambda j, i, k: (i, j))
        o_shape = jax.ShapeDtypeStruct((M, N), out_dtype)
    use_acc = nk > 1 and (out_dtype != F32 or whole_out)

    def body(a_ref, b_ref, *rest):
        o_ref, scratch = rest[nd], rest[nd + 1:]

        def product():
            return lax.dot_general(a_ref[...].astype(_MXU), b_ref[...].astype(_MXU), dims, preferred_element_type=F32)

        def write(result):
            if whole_out:
                w = tn // N_CHIPS
                for c in range(N_CHIPS):
                    o_ref[c] = result[:, c * w:(c + 1) * w].astype(o_ref.dtype)
            else:
                o_ref[...] = result.astype(o_ref.dtype)

        if nk == 1:
            write(product())
            return
        k = pl.program_id(2)
        acc = scratch[0] if use_acc else o_ref

        @pl.when(k == 0)
        def _():
            acc[...] = jnp.zeros_like(acc)

        acc[...] += product()

        if use_acc:
            @pl.when(k == nk - 1)
            def _():
                write(acc[...])

    specs = [a_spec, b_spec, o_spec]
    grid = (nj, ni, nk)
    if rows_outer:
        specs = [pl.BlockSpec(s.block_shape, lambda i, j, k, f=s.index_map: f(j, i, k)) for s in specs]
        grid = (ni, nj, nk)
    return pl.pallas_call(
        body, name=name, grid=grid, in_specs=specs[:2] + [_ANY] * nd, out_specs=specs[2],
        out_shape=o_shape, scratch_shapes=[pltpu.VMEM((tm, tn), F32)] if use_acc else [],
        compiler_params=_params(("parallel", "parallel", "arbitrary")),
    )(a, b, *deps)


def _mm_chip_block(a, b4, blk, prev, *, name, tm=512, deps=()):
    M, K = a.shape
    nchip, _, C = b4.shape
    tm = min(tm, M)
    extra = ([] if prev is None else [prev]) + list(deps)

    def body(blk_ref, a_ref, b_ref, *rest):
        rest[-1][...] = jnp.dot(a_ref[...].astype(_MXU), b_ref[...].astype(_MXU), preferred_element_type=F32)

    return pl.pallas_call(
        body, name=name,
        grid_spec=pltpu.PrefetchScalarGridSpec(
            num_scalar_prefetch=1, grid=(M // tm,),
            in_specs=[pl.BlockSpec((tm, K), lambda i, c: (i, 0)), pl.BlockSpec((None, K, C), lambda i, c: (c[0], 0, 0))]
            + [_ANY] * len(extra),
            out_specs=pl.BlockSpec((tm, C), lambda i, c: (i, c[0]))),
        out_shape=jax.ShapeDtypeStruct((M, nchip * C), F32),
        input_output_aliases={} if prev is None else {3: 0},
        compiler_params=_params(("arbitrary",)),
    )(blk, a, b4, *extra)


def _mm_out_loss(merged, w_out, x, target, *, tm=256):
    rows, d = x.shape

    def body(m_ref, w_ref, x_ref, t_ref, d_ref, db_ref, sq_ref):
        mo = jnp.dot(m_ref[...].astype(_MXU), w_ref[...].astype(_MXU), preferred_element_type=F32)
        err = (x_ref[...] + mo) - t_ref[...]
        dout = err * (1.0 / d)
        d_ref[...] = dout
        db_ref[...] = dout.astype(db_ref.dtype)
        part = _colsum(err * err)
        i = pl.program_id(0)

        @pl.when(i == 0)
        def _():
            sq_ref[...] = part

        @pl.when(i > 0)
        def _():
            sq_ref[...] += part

    tile = pl.BlockSpec((tm, d), lambda i: (i, 0))
    return pl.pallas_call(
        body, name="mm_out_loss", grid=(rows // tm,),
        in_specs=[tile, pl.BlockSpec((d, d), lambda i: (0, 0)), tile, tile],
        out_specs=[tile, tile, pl.BlockSpec((1, d), lambda i: (0, 0))],
        out_shape=[jax.ShapeDtypeStruct((rows, d), F32), jax.ShapeDtypeStruct((rows, d), _MXU),
                   jax.ShapeDtypeStruct((1, d), F32)],
        compiler_params=_params(("arbitrary",)),
    )(merged, w_out, x, target)


def _mm_merge_bwd(dout_b, w_out, proj, y_a, y_s, *, tm=256):
    rows, d = y_a.shape
    ncol = d // CW

    def body(do_ref, w_ref, *refs):
        ga_refs, gs_refs = refs[:ncol], refs[ncol:2 * ncol]
        ya_ref, ys_ref, dya_ref, dys_ref, dg_ref = refs[2 * ncol:]
        dm = lax.dot_general(do_ref[...].astype(_MXU), w_ref[...].astype(_MXU), _NT, preferred_element_type=F32)
        for j in range(ncol):
            cols = slice(j * CW, (j + 1) * CW)
            dmj = dm[:, cols]
            sa, ss = _sigmoid(ga_refs[j][...]), _sigmoid(gs_refs[j][...])
            dya_ref[:, cols] = (sa * dmj).astype(dya_ref.dtype)
            dys_ref[:, cols] = (ss * dmj).astype(dys_ref.dtype)
            dg_ref[:, cols] = (dmj * ya_ref[:, cols] * sa * (1.0 - sa)).astype(dg_ref.dtype)
            dg_ref[:, d + j * CW:d + (j + 1) * CW] = (dmj * ys_ref[:, cols] * ss * (1.0 - ss)).astype(dg_ref.dtype)

    tile = pl.BlockSpec((tm, d), lambda i: (i, 0))
    gate = [pl.BlockSpec((tm, CW), lambda i, c=off + j: (i, c)) for off in (OFF_GA, OFF_GS) for j in range(ncol)]
    both = pl.BlockSpec((pl.Element(tm), pl.Element(2 * d)), lambda i: (i * tm, OFF_GA * CW))
    return pl.pallas_call(
        body, name="mm_merge_bwd", grid=(rows // tm,),
        in_specs=[tile, pl.BlockSpec((d, d), lambda i: (0, 0))] + gate + [tile, tile],
        out_specs=[tile, tile, both],
        out_shape=[jax.ShapeDtypeStruct((rows, d), _MXU)] * 2 + [jax.ShapeDtypeStruct((rows, IN_W), _MXU)],
        compiler_params=_params(("arbitrary",)),
    )(dout_b, w_out, *([proj] * (2 * ncol)), y_a, y_s)


def _ew(fn, ins, outs, *, rows, ncol, name, n_acc=0, tm=512, deps=(), into=None):
    deps = list(deps) + ([into[1]] if into else [])
    n_in, n_out, nd = len(ins), len(outs), len(deps)
    tm = min(tm, rows)
    in_specs = []
    for _, kind, col0 in ins:
        if kind == "mat":
            in_specs.append(pl.BlockSpec((tm, CW), lambda j, i, c0=col0: (i, c0 + j)))
        else:
            in_specs.append(pl.BlockSpec((1, CW), lambda j, i, c0=col0: (0, c0 + j)))
    out_specs = [pl.BlockSpec((tm, CW), lambda j, i: (i, j)) for _ in outs]
    out_shape = [jax.ShapeDtypeStruct((rows, w), dt) for w, dt in outs]
    if into:
        out_specs[into[0]] = pl.BlockSpec((tm, CW), lambda j, i, c0=into[2]: (i, c0 + j))
        out_shape[into[0]] = jax.ShapeDtypeStruct(into[1].shape, into[1].dtype)
    for _ in range(n_acc):
        out_specs.append(pl.BlockSpec((1, CW), lambda j, i: (0, j)))
        out_shape.append(jax.ShapeDtypeStruct((1, ncol * CW), F32))

    def body(*refs):
        vals = fn(*[r[...] for r in refs[:n_in]])
        refs = refs[n_in + nd:]
        for r, v in zip(refs[:n_out], vals[:n_out]):
            r[...] = v.astype(r.dtype)
        i = pl.program_id(1)
        for r, v in zip(refs[n_out:], vals[n_out:]):
            @pl.when(i == 0)
            def _(r=r, v=v):
                r[...] = v

            @pl.when(i > 0)
            def _(r=r, v=v):
                r[...] += v

    res = pl.pallas_call(
        body, name=name, grid=(ncol, rows // tm), in_specs=in_specs + [_ANY] * nd, out_specs=out_specs,
        out_shape=out_shape, input_output_aliases={n_in + nd - 1: into[0]} if into else {},
        compiler_params=_params(("parallel", "arbitrary")),
    )(*[a for a, _, _ in ins], *deps)
    return res


def _colsum(v):
    return jnp.sum(v, axis=0, keepdims=True)


def _sigmoid(v):
    return jax.nn.sigmoid(v)


def _silu_and_grad(v):
    s = _sigmoid(v)
    return v * s, s * (1.0 + v * (1.0 - s))


def _rms_fwd(x, w, *, tm=512, deps=()):
    rows, d = x.shape
    nd = len(deps)

    def body(x_ref, w_ref, *rest):
        h_ref, r_ref = rest[nd:]
        xv = x_ref[...]
        r = lax.rsqrt(jnp.mean(xv * xv, axis=-1, keepdims=True) + NORM_EPS)
        h_ref[...] = (xv * r * w_ref[...]).astype(h_ref.dtype)
        r_ref[...] = r

    return pl.pallas_call(
        body, name="rms_fwd", grid=(rows // tm,),
        in_specs=[pl.BlockSpec((tm, d), lambda i: (i, 0)), pl.BlockSpec((1, d), lambda i: (0, 0))] + [_ANY] * nd,
        out_specs=[pl.BlockSpec((tm, d), lambda i: (i, 0)), pl.BlockSpec((tm, 1), lambda i: (i, 0))],
        out_shape=[jax.ShapeDtypeStruct((rows, d), _MXU), jax.ShapeDtypeStruct((rows, 1), F32)],
        compiler_params=_params(("arbitrary",)),
    )(x, w, *deps)


def _rms_bwd(dh, x, rstd, w, dout, *, tm=256):
    rows, d = x.shape

    def body(dh_ref, x_ref, r_ref, w_ref, do_ref, gx_ref, gw_ref):
        dhv, xv, r, wv = dh_ref[...], x_ref[...], r_ref[...], w_ref[...]
        xr = xv * r
        t = jnp.mean(dhv * wv * xr, axis=-1, keepdims=True)
        gx_ref[...] = do_ref[...] + r * (wv * dhv - xr * t)
        part = _colsum(dhv * xr)
        i = pl.program_id(0)

        @pl.when(i == 0)
        def _():
            gw_ref[...] = part

        @pl.when(i > 0)
        def _():
            gw_ref[...] += part

    return pl.pallas_call(
        body, name="rms_bwd", grid=(rows // tm,),
        in_specs=[pl.BlockSpec((tm, d), lambda i: (i, 0)), pl.BlockSpec((tm, d), lambda i: (i, 0)),
                  pl.BlockSpec((tm, 1), lambda i: (i, 0)), pl.BlockSpec((1, d), lambda i: (0, 0)),
                  pl.BlockSpec((tm, d), lambda i: (i, 0))],
        out_specs=[pl.BlockSpec((tm, d), lambda i: (i, 0)), pl.BlockSpec((1, d), lambda i: (0, 0))],
        out_shape=[jax.ShapeDtypeStruct((rows, d), F32), jax.ShapeDtypeStruct((1, d), F32)],
        compiler_params=_params(("arbitrary",)),
    )(dh, x, rstd, w, dout)


_NT = (((1,), (1,)), ((), ()))
_TN = (((0,), (0,)), ((), ()))


QKV_W = ATTN_W + 2 * KV_W
HEADS_PER_TILE = LANES // HEAD_DIM


def _low_half(rows):
    return lax.broadcasted_iota(jnp.int32, (rows, LANES), 1) < HEAD_DIM


def _pair_mean(t, low):
    m_lo = jnp.sum(jnp.where(low, t, 0.0), axis=-1, keepdims=True)
    m_hi = jnp.sum(jnp.where(low, 0.0, t), axis=-1, keepdims=True)
    return jnp.where(low, m_lo, m_hi) * (1.0 / HEAD_DIM)


def _pair_rstd(t, low):
    return lax.rsqrt(_pair_mean(t * t, low) + NORM_EPS)


def _dup_half(t, hi, low):
    swapped = pltpu.roll(t, HEAD_DIM, 1)
    return jnp.where(low, swapped, t) if hi else jnp.where(low, t, swapped)


def _fold_halves(t):
    return t + pltpu.roll(t, HEAD_DIM, 1)


def _split_heads(t, low):
    return [jnp.where(low, t, 0.0), jnp.where(low, 0.0, t)]


def _stacked_band_mask(n):
    rows = Q_PER_KV * WINDOW
    qi = lax.broadcasted_iota(jnp.int32, (rows, 2 * WINDOW), 0) % WINDOW + WINDOW
    kj = lax.broadcasted_iota(jnp.int32, (rows, 2 * WINDOW), 1)
    diff = qi - kj
    first_key = jnp.where(n > 0, 0, WINDOW)
    return (diff >= 0) & (diff < WINDOW) & (kj >= first_key)


def _stacked_sinks(sink_ref, g):
    blk = lax.broadcasted_iota(jnp.int32, (Q_PER_KV * WINDOW, 1), 0) // WINDOW
    col = jnp.full((Q_PER_KV * WINDOW, 1), sink_ref[Q_PER_KV * g], F32)
    for r in range(1, Q_PER_KV):
        col = jnp.where(blk == r, sink_ref[Q_PER_KV * g + r], col)
    return col


def _attn_in_specs(nblk, rev):
    def cur(n):
        return (nblk - 1 - n) if rev else n

    q_spec = pl.BlockSpec((WINDOW, ATTN_W), lambda n: (cur(n), 0))
    kvc_spec = pl.BlockSpec((WINDOW, 2 * KV_W), lambda n: (cur(n), ATTN_W // (2 * KV_W)))
    kvp_spec = pl.BlockSpec((WINDOW, 2 * KV_W), lambda n: (jnp.maximum(cur(n) - 1, 0), ATTN_W // (2 * KV_W)))
    w_spec = pl.BlockSpec((1, LANES), lambda n: (0, 0))
    l_spec = pl.BlockSpec((WINDOW, N_Q_HEADS), lambda n: (cur(n), 0))
    gate_specs = [pl.BlockSpec((WINDOW, CW), lambda n, col=OFF_AGATE + j: (cur(n), col)) for j in range(ATTN_W // CW)]
    return q_spec, kvc_spec, kvp_spec, w_spec, l_spec, gate_specs


def _attn2_fwd(proj, qw2, kw2, sinks, deps=()):
    seq = proj.shape[0]
    nblk = seq // WINDOW
    scale = 1.0 / math.sqrt(HEAD_DIM)
    q_spec, kvc_spec, kvp_spec, w_spec, l_spec, gate_specs = _attn_in_specs(nblk, False)
    nd, ng = len(deps), len(gate_specs)

    def body(sink_ref, q_ref, kvc_ref, kvp_ref, qw_ref, kw_ref, *rest):
        gate_refs = rest[:ng]
        o_ref, lse_ref, ya_ref = rest[ng + nd:]
        n = pl.program_id(0)
        low, low2 = _low_half(WINDOW), _low_half(2 * WINDOW)
        valid = _stacked_band_mask(n)
        head_lane = lax.broadcasted_iota(jnp.int32, (WINDOW, N_Q_HEADS), 1)
        kv = jnp.concatenate([kvp_ref[...], kvc_ref[...]], axis=0)
        qwv, kwv = qw_ref[...], kw_ref[...]
        lse_blk = jnp.zeros((WINDOW, N_Q_HEADS), F32)
        for t in range(N_KV_HEADS // HEADS_PER_TILE):
            kt = kv[:, t * LANES:(t + 1) * LANES]
            vt = kv[:, KV_W + t * LANES:KV_W + (t + 1) * LANES]
            kn = kt * _pair_rstd(kt, low2) * kwv
            for hi in range(HEADS_PER_TILE):
                g = HEADS_PER_TILE * t + hi
                kdup = _dup_half(kn, hi, low2).astype(_MXU)
                vdup = _dup_half(vt, hi, low2).astype(_MXU)
                stack = []
                for tq in (2 * g, 2 * g + 1):
                    qt = q_ref[:, tq * LANES:(tq + 1) * LANES]
                    stack += _split_heads(qt * _pair_rstd(qt, low) * qwv, low)
                qs = jnp.concatenate(stack, axis=0).astype(_MXU)
                s = lax.dot_general(qs, kdup, _NT, preferred_element_type=F32) * scale
                s = jnp.where(valid, s, -1e30)
                sink = _stacked_sinks(sink_ref, g)
                m = jnp.maximum(jnp.max(s, axis=-1, keepdims=True), sink)
                e = jnp.exp(s - m)
                z = jnp.sum(e, axis=-1, keepdims=True) + jnp.exp(sink - m)
                o = jnp.dot((e / z).astype(_MXU), vdup, preferred_element_type=F32)
                for i, tq in enumerate((2 * g, 2 * g + 1)):
                    o_ref[:, tq * LANES:(tq + 1) * LANES] = jnp.where(
                        low, o[2 * i * WINDOW:(2 * i + 1) * WINDOW], o[(2 * i + 1) * WINDOW:(2 * i + 2) * WINDOW])
                lse = m + jnp.log(z)
                for r in range(Q_PER_KV):
                    lse_blk = jnp.where(head_lane == Q_PER_KV * g + r, lse[r * WINDOW:(r + 1) * WINDOW], lse_blk)
        lse_ref[...] = lse_blk
        for j, g_ref in enumerate(gate_refs):
            cols = slice(j * CW, (j + 1) * CW)
            gate = g_ref[...]
            ya_ref[:, cols] = (o_ref[:, cols] * (gate * _sigmoid(gate))).astype(ya_ref.dtype)

    return pl.pallas_call(
        body, name="attn_fwd", grid=(nblk,),
        in_specs=[pl.BlockSpec(memory_space=pltpu.SMEM), q_spec, kvc_spec, kvp_spec, w_spec, w_spec] + gate_specs
        + [_ANY] * nd,
        out_specs=[q_spec, l_spec, q_spec],
        out_shape=[jax.ShapeDtypeStruct((seq, ATTN_W), F32), jax.ShapeDtypeStruct((seq, N_Q_HEADS), F32),
                   jax.ShapeDtypeStruct((seq, ATTN_W), _MXU)],
        compiler_params=_params(("arbitrary",)),
    )(sinks, proj, proj, proj, qw2, kw2, *([proj] * ng), *deps)


def _attn2_bwd(proj, qw2, kw2, sinks, lse, attn, dya, d_proj, deps=()):
    seq = proj.shape[0]
    nblk = seq // WINDOW
    scale = 1.0 / math.sqrt(HEAD_DIM)
    q_spec, kvc_spec, kvp_spec, w_spec, l_spec, gate_specs = _attn_in_specs(nblk, True)
    s_spec = pl.BlockSpec((1, N_Q_HEADS), lambda n: (0, 0))
    d_spec = pl.BlockSpec((WINDOW, QKV_W + ATTN_W), lambda n: (nblk - 1 - n, 0))
    deps = list(deps) + [d_proj]
    nd, ng = len(deps), len(gate_specs)

    def body(sink_ref, q_ref, kvc_ref, kvp_ref, qw_ref, kw_ref, lse_ref, attn_ref, dya_ref, *rest):
        gate_refs = rest[:ng]
        d_ref, dqw_ref, dkw_ref, dsk_ref, carry, do_ref = rest[ng + nd:]
        step = pl.program_id(0)
        n = nblk - 1 - step

        @pl.when(step == 0)
        def _():
            carry[...] = jnp.zeros_like(carry)
            dqw_ref[...] = jnp.zeros_like(dqw_ref)
            dkw_ref[...] = jnp.zeros_like(dkw_ref)
            dsk_ref[...] = jnp.zeros_like(dsk_ref)

        for j, g_ref in enumerate(gate_refs):
            cols = slice(j * CW, (j + 1) * CW)
            f, df = _silu_and_grad(g_ref[...])
            dv = dya_ref[:, cols]
            do_ref[:, cols] = dv * f
            d_ref[:, QKV_W + j * CW:QKV_W + (j + 1) * CW] = (dv * attn_ref[:, cols] * df).astype(d_ref.dtype)

        low, low2 = _low_half(WINDOW), _low_half(2 * WINDOW)
        valid = _stacked_band_mask(n)
        head_lane = lax.broadcasted_iota(jnp.int32, (WINDOW, N_Q_HEADS), 1)
        sink_lane = lax.broadcasted_iota(jnp.int32, (1, N_Q_HEADS), 1)
        kv = jnp.concatenate([kvp_ref[...], kvc_ref[...]], axis=0)
        qwv, kwv = qw_ref[...], kw_ref[...]
        lse_blk = lse_ref[...]
        dqw = jnp.zeros((1, LANES), F32)
        dkw = jnp.zeros((1, LANES), F32)
        dsk = jnp.zeros((1, N_Q_HEADS), F32)
        for t in range(N_KV_HEADS // HEADS_PER_TILE):
            kt = kv[:, t * LANES:(t + 1) * LANES]
            vt = kv[:, KV_W + t * LANES:KV_W + (t + 1) * LANES]
            rk = _pair_rstd(kt, low2)
            kn = kt * rk * kwv
            dkn_t = jnp.zeros((2 * WINDOW, LANES), F32)
            dv_t = jnp.zeros((2 * WINDOW, LANES), F32)
            for hi in range(HEADS_PER_TILE):
                g = HEADS_PER_TILE * t + hi
                kdup = _dup_half(kn, hi, low2).astype(_MXU)
                vdup = _dup_half(vt, hi, low2).astype(_MXU)
                tiles = (2 * g, 2 * g + 1)
                qx, rq, stack, dstack, lse_rows = [], [], [], [], []
                for tq in tiles:
                    qt = q_ref[:, tq * LANES:(tq + 1) * LANES]
                    r = _pair_rstd(qt, low)
                    rq.append(r)
                    qx.append(qt * r)
                    stack += _split_heads(qx[-1] * qwv, low)
                    dstack += _split_heads(do_ref[:, tq * LANES:(tq + 1) * LANES], low)
                for r in range(Q_PER_KV):
                    lse_rows.append(jnp.sum(jnp.where(head_lane == Q_PER_KV * g + r, lse_blk, 0.0), axis=-1, keepdims=True))
                qs = jnp.concatenate(stack, axis=0).astype(_MXU)
                dos = jnp.concatenate(dstack, axis=0).astype(_MXU)
                lse_col = jnp.concatenate(lse_rows, axis=0)
                s = lax.dot_general(qs, kdup, _NT, preferred_element_type=F32) * scale
                s = jnp.where(valid, s, -1e30)
                p = jnp.exp(s - lse_col)
                dp = lax.dot_general(dos, vdup, _NT, preferred_element_type=F32)
                dsum = jnp.sum(p * dp, axis=-1, keepdims=True)
                ds = (p * (dp - dsum) * scale).astype(_MXU)
                dsink = -jnp.exp(_stacked_sinks(sink_ref, g) - lse_col) * dsum
                for r in range(Q_PER_KV):
                    dsk = dsk + jnp.where(sink_lane == Q_PER_KV * g + r, _colsum(dsink[r * WINDOW:(r + 1) * WINDOW]), 0.0)
                dv_g = _fold_halves(lax.dot_general(p.astype(_MXU), dos, _TN, preferred_element_type=F32))
                dkn_g = _fold_halves(lax.dot_general(ds, qs, _TN, preferred_element_type=F32))
                dv_t = jnp.where(low2, dv_t, dv_g) if hi else jnp.where(low2, dv_g, dv_t)
                dkn_t = jnp.where(low2, dkn_t, dkn_g) if hi else jnp.where(low2, dkn_g, dkn_t)
                dqn = jnp.dot(ds, kdup, preferred_element_type=F32)
                for i, tq in enumerate(tiles):
                    dqn_t = jnp.where(low, dqn[2 * i * WINDOW:(2 * i + 1) * WINDOW],
                                      dqn[(2 * i + 1) * WINDOW:(2 * i + 2) * WINDOW])
                    dq = rq[i] * (qwv * dqn_t - qx[i] * _pair_mean(dqn_t * qwv * qx[i], low))
                    d_ref[:, tq * LANES:(tq + 1) * LANES] = dq.astype(d_ref.dtype)
                    dqw = dqw + _colsum(dqn_t * qx[i])
            k_cols = slice(t * LANES, (t + 1) * LANES)
            v_cols = slice(KV_W + t * LANES, KV_W + (t + 1) * LANES)
            dkn_c = dkn_t[WINDOW:] + carry[:, k_cols]
            rc = rk[WINDOW:]
            kx = kt[WINDOW:] * rc
            dk = rc * (kwv * dkn_c - kx * _pair_mean(dkn_c * kwv * kx, low))
            d_ref[:, ATTN_W + t * LANES:ATTN_W + (t + 1) * LANES] = dk.astype(d_ref.dtype)
            d_ref[:, ATTN_W + KV_W + t * LANES:ATTN_W + KV_W + (t + 1) * LANES] = (
                dv_t[WINDOW:] + carry[:, v_cols]).astype(d_ref.dtype)
            carry[:, k_cols] = dkn_t[:WINDOW]
            carry[:, v_cols] = dv_t[:WINDOW]
            dkw = dkw + _colsum(dkn_c * kx)
        dqw_ref[...] += dqw
        dkw_ref[...] += dkw
        dsk_ref[...] += dsk

    return pl.pallas_call(
        body, name="attn_bwd", grid=(nblk,),
        in_specs=[pl.BlockSpec(memory_space=pltpu.SMEM), q_spec, kvc_spec, kvp_spec, w_spec, w_spec, l_spec, q_spec,
                  q_spec] + gate_specs + [_ANY] * nd,
        out_specs=[d_spec, w_spec, w_spec, s_spec],
        out_shape=[jax.ShapeDtypeStruct(d_proj.shape, d_proj.dtype), jax.ShapeDtypeStruct((1, LANES), F32),
                   jax.ShapeDtypeStruct((1, LANES), F32), jax.ShapeDtypeStruct((1, N_Q_HEADS), F32)],
        input_output_aliases={9 + ng + nd - 1: 0},
        scratch_shapes=[pltpu.VMEM((WINDOW, 2 * KV_W), F32), pltpu.VMEM((WINDOW, ATTN_W), F32)],
        compiler_params=_params(("arbitrary",)),
    )(sinks, proj, proj, proj, qw2, kw2, lse, attn, dya, *([proj] * ng), *deps)


def _ssm_discretise(a_re, a_im, log_dt):
    dt = jnp.exp(log_dt)
    mag = jnp.exp(dt * a_re)
    ab_re = mag * jnp.cos(dt * a_im)
    ab_im = mag * jnp.sin(dt * a_im)
    num_re = ab_re - 1.0
    num_im = ab_im
    den = a_re * a_re + a_im * a_im
    cf_re = (num_re * a_re + num_im * a_im) / den
    cf_im = (num_im * a_re - num_re * a_im) / den
    return ab_re, ab_im, cf_re, cf_im


def _ssm_params_fwd(a_re, a_im, log_dt):
    shp = jax.ShapeDtypeStruct(a_re.shape, F32)

    def body(are_ref, aim_ref, ldt_ref, abr_ref, abi_ref, cfr_ref, cfi_ref, alr_ref, ali_ref):
        abr, abi, cfr, cfi = _ssm_discretise(are_ref[...], aim_ref[...], ldt_ref[...])
        abr_ref[...], abi_ref[...], cfr_ref[...], cfi_ref[...] = abr, abi, cfr, cfi
        pr, pi = abr, abi
        for _ in range(int(math.log2(SSM_L))):
            pr, pi = pr * pr - pi * pi, 2.0 * pr * pi
        alr_ref[...], ali_ref[...] = pr, pi

    return pl.pallas_call(body, name="ssm_params_fwd", out_shape=[shp] * 6)(a_re, a_im, log_dt)


def _ssm_params_bwd(a_re, a_im, log_dt, d_abr, d_abi, d_cfr, d_cfi):
    def body(are_ref, aim_ref, ldt_ref, g0, g1, g2, g3, dare_ref, daim_ref, dldt_ref):
        _, vjp = jax.vjp(_ssm_discretise, are_ref[...], aim_ref[...], ldt_ref[...])
        dare_ref[...], daim_ref[...], dldt_ref[...] = vjp((g0[...], g1[...], g2[...], g3[...]))

    return pl.pallas_call(
        body, name="ssm_params_bwd",
        out_shape=[jax.ShapeDtypeStruct(a_re.shape, F32), jax.ShapeDtypeStruct(a_im.shape, F32),
                   jax.ShapeDtypeStruct(log_dt.shape, F32)],
    )(a_re, a_im, log_dt, d_abr, d_abi, d_cfr, d_cfi)


def _scan_cols(j):
    return pl.ds(j * SSM_SB, SSM_SB)


def _rows8(r):
    return pl.ds(pl.multiple_of(r * SUBLANES, SUBLANES), SUBLANES)


def _bcast8(row):
    return jnp.broadcast_to(row, (SUBLANES, row.shape[-1]))


def _token_order_pick():
    tok = lax.broadcasted_iota(jnp.int32, (SSM_T, SSM_T), 0)
    row = lax.broadcasted_iota(jnp.int32, (SSM_T, SSM_T), 1)
    return (row == SUBLANES * (tok % SSM_L) + tok // SSM_L).astype(_MXU)


SCAN_UNROLL = 8


def _scan_loop(n, step, init):
    def trip(o, carry):
        for i in range(SCAN_UNROLL):
            carry = step(o * SCAN_UNROLL + i, carry)
        return carry

    return lax.fori_loop(0, n // SCAN_UNROLL, trip, init)


def _ssm_fwd(u, b_re, b_im, c_re, c_im, d_skip, coef):
    seq = u.shape[0]
    nc = seq // SSM_T
    T, L = SSM_T, SSM_L

    def body(u_ref, bre_ref, bim_ref, cre_ref, cim_ref, d_ref, are_ref, aim_ref, cfr_ref, cfi_ref, alr_ref, ali_ref,
             y_ref, yg_ref, sre_ref, sim_ref, ire_ref, iim_ref, car_re, car_im, end_re, end_im):
        c = pl.program_id(0)

        @pl.when(c == 0)
        def _():
            car_re[...] = jnp.zeros_like(car_re)
            car_im[...] = jnp.zeros_like(car_im)

        for j in range(SSM_JB):
            ub = u_ref[:, j * LANES:(j + 1) * LANES].astype(_MXU)
            bur = jnp.dot(ub, bre_ref[j], preferred_element_type=F32)
            bui = jnp.dot(ub, bim_ref[j], preferred_element_type=F32)
            cfr, cfi = cfr_ref[:, _scan_cols(j)], cfi_ref[:, _scan_cols(j)]
            sre_ref[:, _scan_cols(j)] = cfr * bur - cfi * bui
            sim_ref[:, _scan_cols(j)] = cfr * bui + cfi * bur

        for j in range(SSM_JB):
            cols = _scan_cols(j)
            ar, ai = _bcast8(are_ref[:, cols]), _bcast8(aim_ref[:, cols])

            def step1(r, s, cols=cols, ar=ar, ai=ai):
                sr, si = s
                rows = _rows8(r)
                return (ar * sr - ai * si + sre_ref[rows, cols], ar * si + ai * sr + sim_ref[rows, cols])

            zero = jnp.zeros((SUBLANES, SSM_SB), F32)
            er, ei = _scan_loop(L, step1, (zero, zero))
            end_re[:, cols] = er
            end_im[:, cols] = ei

        alr, ali = alr_ref[...], ali_ref[...]
        cr, ci = car_re[...], car_im[...]
        ire_ref[0:1, :] = cr
        iim_ref[0:1, :] = ci
        for i in range(1, SUBLANES):
            er, ei = end_re[i - 1:i, :], end_im[i - 1:i, :]
            cr, ci = alr * cr - ali * ci + er, alr * ci + ali * cr + ei
            ire_ref[i:i + 1, :] = cr
            iim_ref[i:i + 1, :] = ci

        for j in range(SSM_JB):
            cols = _scan_cols(j)
            ar, ai = _bcast8(are_ref[:, cols]), _bcast8(aim_ref[:, cols])

            def step2(r, s, cols=cols, ar=ar, ai=ai):
                sr, si = s
                rows = _rows8(r)
                nr = ar * sr - ai * si + sre_ref[rows, cols]
                ni = ar * si + ai * sr + sim_ref[rows, cols]
                sre_ref[rows, cols] = nr
                sim_ref[rows, cols] = ni
                return nr, ni

            _scan_loop(L, step2, (ire_ref[:, cols], iim_ref[:, cols]))

        car_re[...] = sre_ref[T - 1:T, :]
        car_im[...] = sim_ref[T - 1:T, :]

        pick = _token_order_pick()
        for j in range(SSM_JB):
            cols = _scan_cols(j)
            ch = slice(j * LANES, (j + 1) * LANES)
            y = (jnp.dot(sre_ref[:, cols].astype(_MXU), cre_ref[j], preferred_element_type=F32)
                 - jnp.dot(sim_ref[:, cols].astype(_MXU), cim_ref[j], preferred_element_type=F32))
            y = y + d_ref[:, ch] * u_ref[:, ch]
            y_ref[:, ch] = y
            yg_ref[:, ch] = jnp.dot(pick, jax.nn.gelu(y).astype(_MXU), preferred_element_type=F32).astype(yg_ref.dtype)

    tok = pl.BlockSpec((T, SSM_W), lambda c: (c, 0))
    st = pl.BlockSpec((T, N_STATES), lambda c: (c, 0))
    ini = pl.BlockSpec((None, SUBLANES, N_STATES), lambda c: (c, 0, 0))
    bsp = pl.BlockSpec((SSM_JB, LANES, SSM_SB), lambda c: (0, 0, 0))
    csp = pl.BlockSpec((SSM_JB, SSM_SB, LANES), lambda c: (0, 0, 0))
    row_w = pl.BlockSpec((1, SSM_W), lambda c: (0, 0))
    row_s = pl.BlockSpec((1, N_STATES), lambda c: (0, 0))
    return pl.pallas_call(
        body, name="ssm_fwd", grid=(nc,),
        in_specs=[tok, bsp, bsp, csp, csp, row_w] + [row_s] * 6,
        out_specs=[tok, tok, st, st, ini, ini],
        out_shape=[jax.ShapeDtypeStruct((seq, SSM_W), F32), jax.ShapeDtypeStruct((seq, SSM_W), _MXU),
                   jax.ShapeDtypeStruct((seq, N_STATES), F32), jax.ShapeDtypeStruct((seq, N_STATES), F32),
                   jax.ShapeDtypeStruct((nc, SUBLANES, N_STATES), F32),
                   jax.ShapeDtypeStruct((nc, SUBLANES, N_STATES), F32)],
        scratch_shapes=[pltpu.VMEM((1, N_STATES), F32), pltpu.VMEM((1, N_STATES), F32),
                        pltpu.VMEM((SUBLANES, N_STATES), F32), pltpu.VMEM((SUBLANES, N_STATES), F32)],
        compiler_params=_params(("arbitrary",)),
    )(u, b_re, b_im, c_re, c_im, d_skip, *coef)


def _ssm_bwd(dyg, y, u, s_re, s_im, i_re, i_im, b_re, b_im, c_re, c_im, d_skip, coef, d_proj, deps=()):
    seq = u.shape[0]
    nc = seq // SSM_T
    T, L = SSM_T, SSM_L
    deps = list(deps) + [d_proj]

    def body(dyg_ref, y_ref, u_ref, sre_ref, sim_ref, ire_ref, iim_ref, bre_ref, bim_ref, cre_ref, cim_ref, d_ref,
             are_ref, aim_ref, cfr_ref, cfi_ref, alr_ref, ali_ref, *rest):
        (du_ref, dbre_out, dbim_out, dcre_out, dcim_out, dd_ref, dar_ref, dai_ref, dcfr_ref, dcfi_ref,
         lre, lim, car_re, car_im, end_re, end_im, ini_re, ini_im, dbre_ref, dbim_ref, dcre_ref, dcim_ref,
         dy_ref) = rest[len(deps):]
        step = pl.program_id(0)
        dy_ref[...] = jax.vjp(jax.nn.gelu, y_ref[...])[1](dyg_ref[...])[0]
        pick = _token_order_pick()

        @pl.when(step == 0)
        def _():
            car_re[...] = jnp.zeros_like(car_re)
            car_im[...] = jnp.zeros_like(car_im)
            for ref in (dbre_ref, dbim_ref, dcre_ref, dcim_ref, dd_ref, dar_ref, dai_ref, dcfr_ref, dcfi_ref):
                ref[...] = jnp.zeros_like(ref)

        for j in range(SSM_JB):
            dyb = dy_ref[:, j * LANES:(j + 1) * LANES].astype(_MXU)
            lre[:, _scan_cols(j)] = lax.dot_general(dyb, cre_ref[j], _NT, preferred_element_type=F32)
            lim[:, _scan_cols(j)] = -lax.dot_general(dyb, cim_ref[j], _NT, preferred_element_type=F32)

        for j in range(SSM_JB):
            cols = _scan_cols(j)
            ar, ai = _bcast8(are_ref[:, cols]), _bcast8(aim_ref[:, cols])

            def step1(t, s, cols=cols, ar=ar, ai=ai):
                sr, si = s
                rows = _rows8(L - 1 - t)
                return (ar * sr + ai * si + lre[rows, cols], ar * si - ai * sr + lim[rows, cols])

            zero = jnp.zeros((SUBLANES, SSM_SB), F32)
            er, ei = _scan_loop(L, step1, (zero, zero))
            end_re[:, cols] = er
            end_im[:, cols] = ei

        alr, ali = alr_ref[...], ali_ref[...]
        cr, ci = car_re[...], car_im[...]
        ini_re[SUBLANES - 1:SUBLANES, :] = cr
        ini_im[SUBLANES - 1:SUBLANES, :] = ci
        for i in range(SUBLANES - 2, -1, -1):
            er, ei = end_re[i + 1:i + 2, :], end_im[i + 1:i + 2, :]
            cr, ci = alr * cr + ali * ci + er, alr * ci - ali * cr + ei
            ini_re[i:i + 1, :] = cr
            ini_im[i:i + 1, :] = ci

        for j in range(SSM_JB):
            cols = _scan_cols(j)
            ar, ai = _bcast8(are_ref[:, cols]), _bcast8(aim_ref[:, cols])

            def step2(t, s, cols=cols, ar=ar, ai=ai):
                sr, si = s
                rows = _rows8(L - 1 - t)
                nr = ar * sr + ai * si + lre[rows, cols]
                ni = ar * si - ai * sr + lim[rows, cols]
                lre[rows, cols] = nr
                lim[rows, cols] = ni
                return nr, ni

            _scan_loop(L, step2, (ini_re[:, cols], ini_im[:, cols]))

        car_re[...] = lre[0:1, :]
        car_im[...] = lim[0:1, :]

        head, tail, body_rows = slice(0, SUBLANES), slice(SUBLANES, T), slice(0, T - SUBLANES)
        for j in range(SSM_JB):
            cols = _scan_cols(j)
            ch = slice(j * LANES, (j + 1) * LANES)
            lr, li = lre[:, cols], lim[:, cols]
            dar_ref[:, cols] += (_colsum(lre[tail, cols] * sre_ref[body_rows, cols] + lim[tail, cols] * sim_ref[body_rows, cols])
                                 + _colsum(lre[head, cols] * ire_ref[:, cols] + lim[head, cols] * iim_ref[:, cols]))
            dai_ref[:, cols] += (_colsum(lim[tail, cols] * sre_ref[body_rows, cols] - lre[tail, cols] * sim_ref[body_rows, cols])
                                 + _colsum(lim[head, cols] * ire_ref[:, cols] - lre[head, cols] * iim_ref[:, cols]))
            uf = u_ref[:, ch]
            ub = uf.astype(_MXU)
            bur = jnp.dot(ub, bre_ref[j], preferred_element_type=F32)
            bui = jnp.dot(ub, bim_ref[j], preferred_element_type=F32)
            dcfr_ref[:, cols] += _colsum(lr * bur + li * bui)
            dcfi_ref[:, cols] += _colsum(li * bur - lr * bui)
            cfr, cfi = cfr_ref[:, cols], cfi_ref[:, cols]
            dbur = (cfr * lr + cfi * li).astype(_MXU)
            dbui = (cfr * li - cfi * lr).astype(_MXU)
            dyf = dy_ref[:, ch]
            dyb = dyf.astype(_MXU)
            du = (lax.dot_general(dbur, bre_ref[j], _NT, preferred_element_type=F32)
                  + lax.dot_general(dbui, bim_ref[j], _NT, preferred_element_type=F32) + d_ref[:, ch] * dyf)
            du_ref[:, ch] = jnp.dot(pick, du.astype(_MXU), preferred_element_type=F32).astype(du_ref.dtype)
            dbre_ref[j] += lax.dot_general(ub, dbur, _TN, preferred_element_type=F32)
            dbim_ref[j] += lax.dot_general(ub, dbui, _TN, preferred_element_type=F32)
            dcre_ref[j] += lax.dot_general(sre_ref[:, cols].astype(_MXU), dyb, _TN, preferred_element_type=F32)
            dcim_ref[j] -= lax.dot_general(sim_ref[:, cols].astype(_MXU), dyb, _TN, preferred_element_type=F32)
            dd_ref[:, ch] += _colsum(dyf * uf)

        @pl.when(step == nc - 1)
        def _():
            for acc, out in ((dbre_ref, dbre_out), (dbim_ref, dbim_out), (dcre_ref, dcre_out), (dcim_ref, dcim_out)):
                pltpu.sync_copy(acc, out)

    tok = pl.BlockSpec((T, SSM_W), lambda c: (nc - 1 - c, 0))
    st = pl.BlockSpec((T, N_STATES), lambda c: (nc - 1 - c, 0))
    ini = pl.BlockSpec((None, SUBLANES, N_STATES), lambda c: (nc - 1 - c, 0, 0))
    bsp = pl.BlockSpec((SSM_JB, LANES, SSM_SB), lambda c: (0, 0, 0))
    csp = pl.BlockSpec((SSM_JB, SSM_SB, LANES), lambda c: (0, 0, 0))
    row_w = pl.BlockSpec((1, SSM_W), lambda c: (0, 0))
    row_s = pl.BlockSpec((1, N_STATES), lambda c: (0, 0))
    big = pltpu.VMEM((T, N_STATES), F32)
    one = pltpu.VMEM((1, N_STATES), F32)
    eight = pltpu.VMEM((SUBLANES, N_STATES), F32)
    return pl.pallas_call(
        body, name="ssm_bwd", grid=(nc,),
        in_specs=[tok, tok, tok, st, st, ini, ini, bsp, bsp, csp, csp, row_w] + [row_s] * 6 + [_ANY] * len(deps),
        out_specs=[pl.BlockSpec((pl.Element(T), pl.Element(SSM_W)), lambda c: ((nc - 1 - c) * T, OFF_U * CW)),
                   _ANY, _ANY, _ANY, _ANY, row_w, row_s, row_s, row_s, row_s],
        input_output_aliases={18 + len(deps) - 1: 0},
        out_shape=[jax.ShapeDtypeStruct(d_proj.shape, d_proj.dtype),
                   jax.ShapeDtypeStruct((SSM_JB, LANES, SSM_SB), F32), jax.ShapeDtypeStruct((SSM_JB, LANES, SSM_SB), F32),
                   jax.ShapeDtypeStruct((SSM_JB, SSM_SB, LANES), F32), jax.ShapeDtypeStruct((SSM_JB, SSM_SB, LANES), F32),
                   jax.ShapeDtypeStruct((1, SSM_W), F32)] + [jax.ShapeDtypeStruct((1, N_STATES), F32)] * 4,
        scratch_shapes=[big, big, one, one, eight, eight, eight, eight,
                        pltpu.VMEM((SSM_JB, LANES, SSM_SB), F32), pltpu.VMEM((SSM_JB, LANES, SSM_SB), F32),
                        pltpu.VMEM((SSM_JB, SSM_SB, LANES), F32), pltpu.VMEM((SSM_JB, SSM_SB, LANES), F32),
                        pltpu.VMEM((T, SSM_W), F32)],
        compiler_params=_params(("arbitrary",)),
    )(dyg, y, u, s_re, s_im, i_re, i_im, b_re, b_im, c_re, c_im, d_skip, *coef, *deps)


def _block_diag_b(b):
    t = b.reshape(SSM_JB, 8, STATE, GROUP).transpose(0, 1, 3, 2)
    eye = jnp.eye(8, dtype=b.dtype)
    return (t[:, :, :, None, :] * eye[None, :, None, :, None]).reshape(SSM_JB, LANES, SSM_SB)


def _block_diag_c(c):
    t = c.reshape(SSM_JB, 8, GROUP, STATE).transpose(0, 1, 3, 2)
    eye = jnp.eye(8, dtype=c.dtype)
    return (t[:, :, :, None, :] * eye[None, :, None, :, None]).reshape(SSM_JB, SSM_SB, LANES)


def _diag_of_b(blk):
    t = blk.reshape(SSM_JB, 8, GROUP, 8, STATE)
    d = jnp.sum(t * jnp.eye(8, dtype=blk.dtype)[None, :, None, :, None], axis=3)
    return d.transpose(0, 1, 3, 2).reshape(N_GROUPS, STATE, GROUP)


def _diag_of_c(blk):
    t = blk.reshape(SSM_JB, 8, STATE, 8, GROUP)
    d = jnp.sum(t * jnp.eye(8, dtype=blk.dtype)[None, :, None, :, None], axis=3)
    return d.transpose(0, 1, 3, 2).reshape(N_GROUPS, GROUP, STATE)


def _place_from_scan_order(v, buf, col0):
    seq, w = v.shape
    assert buf.dtype == _MXU

    def body(v_ref, b_ref, o_ref):
        tok = lax.broadcasted_iota(jnp.int32, (SSM_T, SSM_T), 0)
        row = lax.broadcasted_iota(jnp.int32, (SSM_T, SSM_T), 1)
        pick = (row == SUBLANES * (tok % SSM_L) + tok // SSM_L).astype(_MXU)
        o_ref[...] = jnp.dot(pick, v_ref[...].astype(_MXU), preferred_element_type=F32).astype(o_ref.dtype)

    return pl.pallas_call(
        body, name="place_from_scan_order", grid=(seq // SSM_T, w // CW),
        in_specs=[pl.BlockSpec((SSM_T, CW), lambda c, j: (c, j)), _ANY],
        out_specs=pl.BlockSpec((SSM_T, CW), lambda c, j: (c, col0 + j)),
        out_shape=jax.ShapeDtypeStruct(buf.shape, buf.dtype), input_output_aliases={1: 0},
        compiler_params=_params(("arbitrary", "arbitrary")),
    )(v, buf)


def _to_scan_order(v):
    seq, w = v.shape
    return v.reshape(seq // SSM_T, SUBLANES, SSM_L, w).transpose(0, 2, 1, 3).reshape(seq, w)


def _from_scan_order(v):
    seq, w = v.shape
    return v.reshape(seq // SSM_T, SSM_L, SUBLANES, w).transpose(0, 2, 1, 3).reshape(seq, w)


def _adamw_math(w, g, m, v):
    nm = ADAM_B1 * m + (1.0 - ADAM_B1) * g
    nv = ADAM_B2 * v + (1.0 - ADAM_B2) * jnp.square(g)
    m_hat = nm / (1.0 - ADAM_B1 ** ADAM_STEP)
    v_hat = nv / (1.0 - ADAM_B2 ** ADAM_STEP)
    return -ADAM_LR * (m_hat / (jnp.sqrt(v_hat) + ADAM_EPS) + ADAM_WD * w), nm, nv


def _adamw(w, g, m, v, *, name, tm, deps=()):
    rows, cols = w.shape
    nd = len(deps)

    def body(w_ref, g_ref, m_ref, v_ref, *rest):
        d_ref, nm_ref, nv_ref = rest[nd:]
        d_ref[...], nm_ref[...], nv_ref[...] = _adamw_math(w_ref[...], g_ref[...], m_ref[...], v_ref[...])

    spec = pl.BlockSpec((tm, cols), lambda i: (i, 0))
    shp = jax.ShapeDtypeStruct((rows, cols), F32)
    return pl.pallas_call(body, name=name, grid=(rows // tm,), in_specs=[spec] * 4 + [_ANY] * nd,
                          out_specs=[spec] * 3, out_shape=[shp] * 3,
                          compiler_params=_params(("arbitrary",)))(w, g, m, v, *deps)


def _place():
    x, y, c = lax.axis_index("x"), lax.axis_index("y"), lax.axis_index("c")
    chips = [(1 - x, y), (x, 1 - y), (1 - x, 1 - y)]
    return x, y, c, chips


def _remote(src, dst, send_sem, recv_sem, dev):
    return pltpu.make_async_remote_copy(src_ref=src, dst_ref=dst, send_sem=send_sem, recv_sem=recv_sem,
                                        device_id=dev, device_id_type=MESH)


def _place_shard(w, mine_arr, *, name, tm=256):
    rows, cols = w.shape

    def body(m_ref, w_ref, o_ref):
        o_ref[...] = w_ref[...].astype(o_ref.dtype)

    return pl.pallas_call(
        body, name=name,
        grid_spec=pltpu.PrefetchScalarGridSpec(
            num_scalar_prefetch=1, grid=(rows // tm,),
            in_specs=[pl.BlockSpec((tm, cols), lambda i, m: (i, 0))],
            out_specs=pl.BlockSpec((None, tm, cols), lambda i, m: (m[0], i, 0))),
        out_shape=jax.ShapeDtypeStruct((N_CHIPS, rows, cols), _WIRE),
        compiler_params=_params(("arbitrary",)),
    )(mine_arr, w)


_HBM = pl.BlockSpec(memory_space=pltpu.HBM)
_SEM = pl.BlockSpec(memory_space=pltpu.SEMAPHORE)
_EFFECT = pltpu.SideEffectType.DATAFLOW_SIDE_EFFECTING


def _copies_start(name, bufs, plan, count, after=()):
    nb, na = len(bufs), len(after)

    def body(*refs):
        send_sems, recv_sems, token = refs[nb + na], refs[nb + na + 1], refs[-1]
        copies = plan(refs[:nb])
        assert len(copies) == count
        for i, (src, dst, dev, _) in enumerate(copies):
            _remote(src, dst, send_sems.at[i], recv_sems.at[i], dev).start()
        token[...] = jnp.zeros_like(token)

    res = pl.pallas_call(
        body, name=name, in_specs=[_HBM] * nb + [_ANY] * na,
        out_specs=(_SEM, _SEM, *[_HBM] * nb, pl.BlockSpec(memory_space=pltpu.VMEM)),
        out_shape=(pltpu.SemaphoreType.DMA((count,)), pltpu.SemaphoreType.DMA((count,)),
                   *[pltpu.HBM(b.shape, b.dtype) for b in bufs], jax.ShapeDtypeStruct((SUBLANES, LANES), F32)),
        input_output_aliases={i: 2 + i for i in range(nb)},
        compiler_params=pltpu.CompilerParams(has_side_effects=_EFFECT),
    )(*[pltpu.with_memory_space_constraint(b, pltpu.HBM) for b in bufs], *after)
    return (res[0], res[1]), list(res[2:2 + nb]), res[-1]


def _copies_wait(name, bufs, sems, plan, after=(), which=None):
    nb, na = len(bufs), len(after)

    def body(*refs):
        send_sems, recv_sems = refs[nb], refs[nb + 1]
        for i, (src, _, dev, land) in enumerate(plan(refs[:nb])):
            if which is not None and i not in which:
                continue
            cp = _remote(src, land, send_sems.at[i], recv_sems.at[i], dev)
            cp.wait_send()
            cp.wait_recv()

    res = pl.pallas_call(
        body, name=name, in_specs=[_HBM] * nb + [_SEM, _SEM] + [_ANY] * na, out_specs=[_HBM] * nb,
        out_shape=[pltpu.HBM(b.shape, b.dtype) for b in bufs],
        input_output_aliases={i: i for i in range(nb)},
        compiler_params=pltpu.CompilerParams(has_side_effects=_EFFECT),
    )(*bufs, *sems, *after)
    return list(res)


def _plan_gather_ici(fulls, which=(0, 1, 2)):
    x, y, c, chips = _place()
    copies = []
    for f in fulls:
        half = pl.ds(c * (f.shape[1] // 2), f.shape[1] // 2)
        own = f.at[2 * x + y, half]
        for chip in [chips[k] for k in which]:
            copies.append((own, own, (*chip, c), f.at[2 * chip[0] + chip[1], half]))
    return copies


def _plan_gather_d2d(fulls, which=(0, 1, 2)):
    x, y, c, chips = _place()
    copies = []
    for f in fulls:
        r2 = f.shape[1] // 2
        for chip in [chips[k] for k in which]:
            blk = 2 * chip[0] + chip[1]
            landed = f.at[blk, pl.ds(c * r2, r2)]
            copies.append((landed, landed, (x, y, 1 - c), f.at[blk, pl.ds((1 - c) * r2, r2)]))
    return copies


def _plan_relay_direct(fulls):
    (f,) = fulls
    x, y, c, chips = _place()
    half = pl.ds(c * (f.shape[1] // 2), f.shape[1] // 2)
    own = f.at[2 * x + y, half]
    return [(own, own, (*chip, c), f.at[2 * chip[0] + chip[1], half]) for chip in chips[:2]]


def _plan_relay_forward(fulls, k):
    (f,) = fulls
    x, y, c, chips = _place()
    r2 = f.shape[1] // 2
    half, other = pl.ds(c * r2, r2), pl.ds((1 - c) * r2, r2)
    quarter = pl.ds(c * r2 + k * (r2 // 2), r2 // 2)
    blk, far = 2 * chips[k][0] + chips[k][1], 2 * chips[2][0] + chips[2][1]
    passed, landed = f.at[blk, quarter], f.at[blk, half]
    return [(passed, passed, (*chips[1 - k], c), f.at[far, quarter]), (landed, landed, (x, y, 1 - c), f.at[blk, other])]


def _plan_relay_last(fulls):
    (f,) = fulls
    x, y, c, chips = _place()
    r2 = f.shape[1] // 2
    far = 2 * chips[2][0] + chips[2][1]
    landed = f.at[far, pl.ds(c * r2, r2)]
    return [(landed, landed, (x, y, 1 - c), f.at[far, pl.ds((1 - c) * r2, r2)])]


def _plan_swap_halves(refs):
    x, y, c, _ = _place()
    n = len(refs) // 2
    copies = []
    for g, land in zip(refs[:n], refs[n:]):
        r2 = g.shape[1] // 2
        copies.append((g.at[:, pl.ds((1 - c) * r2, r2), :], land, (x, y, 1 - c), land))
    return copies


def _plan_scatter_chips(refs):
    x, y, c, chips = _place()
    n = len(refs) // 2
    copies = []
    for h, land in zip(refs[:n], refs[n:]):
        for k, chip in enumerate(chips):
            copies.append((h.at[2 * chip[0] + chip[1]], land.at[k], (*chip, c), land.at[k]))
    return copies


def _plan_join_halves(totals):
    x, y, c, _ = _place()
    copies = []
    for t in totals:
        r2 = t.shape[0] // 2
        mine = t.at[pl.ds(c * r2, r2)]
        copies.append((mine, mine, (x, y, 1 - c), t.at[pl.ds((1 - c) * r2, r2)]))
    return copies


def _add_sibling_half(g, got, c_arr, *, name, tm):
    _, rows, cols = g.shape
    r2 = rows // 2
    nb = r2 // tm

    def body(c_ref, g_ref, r_ref, o_ref):
        o_ref[...] = (g_ref[...].astype(F32) + r_ref[...].astype(F32)).astype(o_ref.dtype)

    return pl.pallas_call(
        body, name=name,
        grid_spec=pltpu.PrefetchScalarGridSpec(
            num_scalar_prefetch=1, grid=(N_CHIPS, nb),
            in_specs=[pl.BlockSpec((None, tm, cols), lambda b, i, c: (b, c[0] * nb + i, 0)),
                      pl.BlockSpec((None, tm, cols), lambda b, i, c: (b, i, 0))],
            out_specs=pl.BlockSpec((None, tm, cols), lambda b, i, c: (b, i, 0))),
        out_shape=jax.ShapeDtypeStruct((N_CHIPS, r2, cols), _WIRE),
        compiler_params=_params(("arbitrary", "arbitrary")),
    )(c_arr, g, got)


def _add_chips(h, got, place_arr, *, name, tm):
    _, r2, cols = h.shape
    nb = r2 // tm

    def body(p_ref, h_ref, r_ref, o_ref):
        o_ref[...] = ((h_ref[...].astype(F32) + r_ref[0].astype(F32)) + r_ref[1].astype(F32)) + r_ref[2].astype(F32)

    return pl.pallas_call(
        body, name=name,
        grid_spec=pltpu.PrefetchScalarGridSpec(
            num_scalar_prefetch=1, grid=(nb,),
            in_specs=[pl.BlockSpec((None, tm, cols), lambda i, p: (p[0], i, 0)),
                      pl.BlockSpec((3, tm, cols), lambda i, p: (0, i, 0))],
            out_specs=pl.BlockSpec((tm, cols), lambda i, p: (p[1] * nb + i, 0))),
        out_shape=jax.ShapeDtypeStruct((2 * r2, cols), F32),
        compiler_params=_params(("arbitrary",)),
    )(place_arr, h, got)


class _ReduceScatter:
    def __init__(self, tag, names, grads):
        self.tag, self.names, self.n = tag, names, len(names)
        core = lax.axis_index("c").astype(jnp.int32)
        chip = (2 * lax.axis_index("x") + lax.axis_index("y")).astype(jnp.int32)
        self.c_arr, self.place_arr = core.reshape(1), jnp.stack([chip, core])
        self.bufs = list(grads)

    def _start(self, step, bufs, plan, count, after):
        self.plan = plan
        self.step = f"grad_{step}_{self.tag}"
        self.sems, self.bufs, token = _copies_start(self.step + "_start", bufs, plan, count, after)
        return [token]

    def _wait(self, after):
        self.bufs = _copies_wait(self.step + "_wait", self.bufs, self.sems, self.plan, after)
        return self.bufs

    def start_swap(self, after=()):
        lands = [lax.empty((N_CHIPS, g.shape[1] // 2, g.shape[2]), g.dtype) for g in self.bufs]
        return self._start("swap", self.bufs + lands, _plan_swap_halves, self.n, after)

    def start_scatter(self, after):
        bufs = self._wait(after)
        pair = [_add_sibling_half(g, r, self.c_arr, name=f"grad_add_sibling_{nm}", tm=min(256, g.shape[1] // 2))
                for nm, g, r in zip(self.names, bufs[:self.n], bufs[self.n:])]
        lands = [lax.empty((3,) + h.shape[1:], h.dtype) for h in pair]
        return self._start("scatter", pair + lands, _plan_scatter_chips, 3 * self.n, ())

    def start_join(self, after):
        bufs = self._wait(after)
        total = [_add_chips(h, r, self.place_arr, name=f"grad_add_chips_{nm}", tm=min(256, h.shape[1]))
                 for nm, h, r in zip(self.names, bufs[:self.n], bufs[self.n:])]
        return self._start("join", total, _plan_join_halves, self.n, ())

    def finish(self, after):
        return dict(zip(self.names, self._wait(after)))


def _all_gather_small(v):
    m_per, n = v.shape

    def body(x_ref, out_ref, send_sems, recv_sems, local_sem):
        x, y, c, chips = _place()
        me, sibling = (x, y, c), (x, y, 1 - c)

        def rows(px, py, pc):
            return out_ref.at[4 * px + 2 * py + pc]

        def copy(k, block, to, src=None):
            return _remote(rows(*block) if src is None else src, rows(*block), send_sems.at[k], recv_sems.at[k], to)

        mine = pltpu.make_async_copy(x_ref, rows(*me), local_sem)
        mine.start()
        first = [copy(0, me, sibling, src=x_ref)]
        first += [copy(1 + j, me, (*chip, c), src=x_ref) for j, chip in enumerate(chips)]
        for cp in first:
            cp.start()
        passed = [copy(4 + j, (*chip, c), sibling) for j, chip in enumerate(chips)]
        for j, chip in enumerate(chips):
            copy(1 + j, (*chip, c), me).wait_recv()
            passed[j].start()
        copy(0, sibling, me).wait_recv()
        for j, chip in enumerate(chips):
            copy(4 + j, (*chip, 1 - c), me).wait_recv()
        for cp in first + passed:
            cp.wait_send()
        mine.wait()

    return pl.pallas_call(
        body, name="gather_small_grads",
        out_shape=jax.ShapeDtypeStruct((8, m_per, n), v.dtype),
        in_specs=[pl.BlockSpec(memory_space=pltpu.VMEM)], out_specs=pl.BlockSpec(memory_space=pltpu.VMEM),
        scratch_shapes=[pltpu.SemaphoreType.DMA((7,)), pltpu.SemaphoreType.DMA((7,)), pltpu.SemaphoreType.DMA],
        compiler_params=pltpu.CompilerParams(vmem_limit_bytes=VMEM_LIMIT),
    )(v)


def _sum8(v, *, name):
    _, m, n = v.shape

    def body(v_ref, o_ref):
        acc = v_ref[0]
        for d in range(1, 8):
            acc = acc + v_ref[d]
        o_ref[...] = acc

    return pl.pallas_call(body, name=name, out_shape=jax.ShapeDtypeStruct((m, n), F32),
                          compiler_params=pltpu.CompilerParams(vmem_limit_bytes=VMEM_LIMIT))(v)


def _local_step(x, target, norm_w, q_norm_w, k_norm_w, sinks, a_re, a_im, log_dt, b_re, b_im, c_re, c_im, d_skip,
                b_glu, io):
    seq = x.shape[0]
    qw2 = jnp.tile(q_norm_w.reshape(1, HEAD_DIM), (1, HEADS_PER_TILE))
    kw2 = jnp.tile(k_norm_w.reshape(1, HEAD_DIM), (1, HEADS_PER_TILE))
    nw, bg = norm_w.reshape(1, D_MODEL), b_glu.reshape(1, D_MODEL)
    dsk = d_skip.reshape(1, SSM_W)

    h, rstd = _rms_fwd(x, nw, deps=io.begin())
    proj, w_in4 = io.projection(h)
    attn, lse, ya_in = _attn2_fwd(proj, qw2, kw2, sinks, deps=io.after_proj(proj))
    w_ap4 = io.weight("w_attn_proj", ya_in)
    w_glu4, w_sp4, w_out = io.weight("w_glu", ya_in), io.weight("w_ssm_proj", ya_in), io.weight("w_out", ya_in)
    y_a = _mm(ya_in, w_ap4, mode="nn", name="mm_attn_proj", tm=2048, tn=512, tk=ATTN_W, b_blocked=True,
              rows_outer=True)

    flat_a = (a_re.reshape(1, N_STATES), a_im.reshape(1, N_STATES), jnp.repeat(log_dt, STATE).reshape(1, N_STATES))
    coef = _ssm_params_fwd(*flat_a)
    bre_blk, bim_blk = _block_diag_b(b_re).astype(_MXU), _block_diag_b(b_im).astype(_MXU)
    cre_blk, cim_blk = _block_diag_c(c_re).astype(_MXU), _block_diag_c(c_im).astype(_MXU)
    u_scan = _to_scan_order(proj[:, OFF_U * CW:OFF_U * CW + SSM_W])
    y_scan, yg, s_re, s_im, i_re, i_im = _ssm_fwd(u_scan, bre_blk, bim_blk, cre_blk, cim_blk, dsk, coef)
    glu = _mm(yg, w_glu4, mode="nn", name="mm_glu", tm=2048, tn=512, tk=SSM_W, b_blocked=True, rows_outer=True)

    def gate_s(ga, gb, ba, bb, z):
        return ((ga + ba) * _sigmoid(gb + bb) * (z * _sigmoid(z)),)

    (ys_in,) = _ew(gate_s, [(glu, "mat", 0), (glu, "mat", 2), (bg, "row", 0), (bg, "row", 2), (proj, "mat", OFF_Z)],
                   [(SSM_W, _MXU)], rows=seq, ncol=2, name="ew_ssm_gate")
    y_s = _mm(ys_in, w_sp4, mode="nn", name="mm_ssm_proj", tm=2048, tn=512, tk=SSM_W, b_blocked=True,
              rows_outer=True)

    def merge(ga, gs, ya, ys):
        return (_sigmoid(ga) * ya + _sigmoid(gs) * ys,)

    (merged,) = _ew(merge, [(proj, "mat", OFF_GA), (proj, "mat", OFF_GS), (y_a, "mat", 0), (y_s, "mat", 0)],
                    [(D_MODEL, _MXU)], rows=seq, ncol=4, name="ew_merge")
    dout, dout_b, sq = _mm_out_loss(merged, w_out, x, target)
    loss = 0.5 * jnp.sum(sq) / D_MODEL

    d_ya, d_ys, d_proj = _mm_merge_bwd(dout_b, w_out, proj, y_a, y_s)
    g_w_out = _mm(merged, dout_b, mode="tn", name="mm_g_w_out", tm=1024, tn=D_MODEL, tk=1024, out_dtype=_WIRE)

    d_ya_in = _mm(d_ya, w_ap4, mode="nt", name="mm_d_attn_gate", tm=2048, tn=ATTN_W, tk=512, b_blocked=True)
    g_w_ap = _mm(ya_in, d_ya, mode="tn", name="mm_g_w_attn_proj", tm=ATTN_W, tn=D_MODEL, tk=2048, out_dtype=_WIRE,
                 out_blocked=True)

    d_ys_in = _mm(d_ys, w_sp4, mode="nt", name="mm_d_ssm_gate", tm=2048, tn=SSM_W, tk=512, b_blocked=True)
    g_w_sp = _mm(ys_in, d_ys, mode="tn", name="mm_g_w_ssm_proj", tm=SSM_W, tn=D_MODEL, tk=2048, out_dtype=_WIRE,
                 out_blocked=True)

    def gate_s_bwd(dv, ga, gb, ba, bb, z):
        a, sb = ga + ba, _sigmoid(gb + bb)
        f, df = _silu_and_grad(z)
        dga = dv * sb * f
        dgb = dv * a * f * sb * (1.0 - sb)
        return dga, dgb, dv * a * sb * df, _colsum(dga), _colsum(dgb)

    d_glu_a, d_glu_b, d_proj, g_bga, g_bgb = _ew(
        gate_s_bwd, [(d_ys_in, "mat", 0), (glu, "mat", 0), (glu, "mat", 2), (bg, "row", 0), (bg, "row", 2),
                     (proj, "mat", OFF_Z)],
        [(SSM_W, _MXU)] * 3, rows=seq, ncol=2, n_acc=2, name="ew_ssm_gate_bwd", into=(2, d_proj, OFF_Z))
    d_glu = jnp.concatenate([d_glu_a, d_glu_b], axis=1)
    d_yg = _mm(d_glu, w_glu4, mode="nt", name="mm_d_gelu", tm=2048, tn=SSM_W, tk=512, b_blocked=True)
    g_w_glu = _mm(yg, d_glu, mode="tn", name="mm_g_w_glu", tm=SSM_W, tn=D_MODEL, tk=2048, out_dtype=_WIRE, out_blocked=True)
    dep = io.later_grads(dict(w_attn_proj=g_w_ap, w_glu=g_w_glu, w_ssm_proj=g_w_sp,
                              w_out=g_w_out.reshape(N_CHIPS, D_MODEL // N_CHIPS, D_MODEL)))

    d_proj, g_qw2, g_kw2, g_sk = _attn2_bwd(proj, qw2, kw2, sinks, lse, attn, d_ya_in, d_proj, deps=dep)
    dep = io.before_scan_backward([d_proj])
    (d_proj, g_bre, g_bim, g_cre, g_cim, g_dsk, g_abr, g_abi, g_cfr, g_cfi) = _ssm_bwd(
        _to_scan_order(d_yg), y_scan, u_scan, s_re, s_im, i_re, i_im, bre_blk, bim_blk, cre_blk, cim_blk, dsk, coef,
        d_proj, deps=dep)
    g_are, g_aim, g_ldt = _ssm_params_bwd(*flat_a, g_abr, g_abi, g_cfr, g_cfi)
    g_are, g_aim = g_are.reshape(N_GROUPS, STATE), g_aim.reshape(N_GROUPS, STATE)
    g_ldt = g_ldt.reshape(N_GROUPS, STATE).sum(axis=1)
    dep = io.before_input_projection_grad([d_proj]) + io.small_grads(dict(
        q_norm_w=g_qw2[0, :HEAD_DIM] + g_qw2[0, HEAD_DIM:], k_norm_w=g_kw2[0, :HEAD_DIM] + g_kw2[0, HEAD_DIM:],
        sinks=g_sk.reshape(N_Q_HEADS), A_re=g_are, A_im=g_aim, log_dt=g_ldt,
        B_re=_diag_of_b(g_bre), B_im=_diag_of_b(g_bim), C_re=_diag_of_c(g_cre), C_im=_diag_of_c(g_cim),
        D_skip=g_dsk.reshape(N_GROUPS, GROUP), b_glu=jnp.concatenate([g_bga, g_bgb], axis=1).reshape(D_MODEL)))
    g_w_in = _mm(h, d_proj, mode="tn", name="mm_g_w_in", tm=1024, tn=IN_W // 4, tk=1024, out_dtype=_WIRE,
                 out_blocked=True, deps=dep)
    dep = io.input_projection_grad(g_w_in)
    d_h = _mm(d_proj, w_in4, mode="nt", name="mm_d_h", tm=512, tn=D_MODEL, tk=IN_W // 4, b_blocked=True, deps=dep)
    grad_x, g_nw = _rms_bwd(d_h, x, rstd, nw, dout)
    return loss, grad_x, g_nw.reshape(D_MODEL)


_SMALL = ["norm_w", "q_norm_w", "k_norm_w", "sinks", "A_re", "A_im", "log_dt", "B_re", "B_im", "C_re", "C_im",
          "D_skip", "b_glu"]
_BIG = ["w_in", "w_attn_proj", "w_glu", "w_ssm_proj", "w_out"]
_LATER = _BIG[1:]
_RELATIONS = ("flip_x", "flip_y", "flip_xy")
_ORDER = ["norm_w", "w_in", "q_norm_w", "k_norm_w", "sinks", "w_attn_proj", "A_re", "A_im", "log_dt", "B_re", "B_im",
          "C_re", "C_im", "D_skip", "w_glu", "b_glu", "w_ssm_proj", "w_out"]
_PACK_W = 1024


def _packed_rows(size):
    unit = SUBLANES * _PACK_W
    return -(-size // unit) * SUBLANES


def _pack_small(d, names):
    parts = []
    for n in names:
        flat = d[n].reshape(-1).astype(F32)
        rows = _packed_rows(flat.shape[0])
        parts.append(jnp.pad(flat, (0, rows * _PACK_W - flat.shape[0])).reshape(rows, _PACK_W))
    return jnp.concatenate(parts, axis=0)


def _unpack_small(packed, like, names):
    out, pos = {}, 0
    for n in names:
        rows = _packed_rows(like[n].size)
        out[n] = packed[pos:pos + rows].reshape(-1)[:like[n].size].reshape(like[n].shape)
        pos += rows
    return out


def _place_block(v, index_arr, *, name):
    rows, cols = v.shape

    def body(i_ref, v_ref, o_ref):
        o_ref[...] = v_ref[...]

    return pl.pallas_call(
        body, name=name,
        grid_spec=pltpu.PrefetchScalarGridSpec(
            num_scalar_prefetch=1, grid=(1,),
            in_specs=[pl.BlockSpec((rows, cols), lambda i, d: (0, 0))],
            out_specs=pl.BlockSpec((None, rows, cols), lambda i, d: (d[0], 0, 0))),
        out_shape=jax.ShapeDtypeStruct((8, rows, cols), v.dtype),
        compiler_params=_params(("arbitrary",)),
    )(index_arr, v)


def _plan_all_to_all(refs):
    (land,) = refs
    x, y, c, _ = _place()
    own = land.at[4 * x + 2 * y + c]
    copies = []
    for fx, fy, fc in [(0, 0, 1), (0, 1, 0), (0, 1, 1), (1, 0, 0), (1, 0, 1), (1, 1, 0), (1, 1, 1)]:
        px, py, pc = (1 - x) if fx else x, (1 - y) if fy else y, (1 - c) if fc else c
        copies.append((own, own, (px, py, pc), land.at[4 * px + 2 * py + pc]))
    return copies


def _as2d(a):
    return a.reshape(1, -1) if a.ndim == 1 else a


def _adamw_whole(w, g, m, v, *, name):
    shape = w.shape

    def body(w_ref, g_ref, m_ref, v_ref, d_ref, nm_ref, nv_ref):
        d_ref[...], nm_ref[...], nv_ref[...] = _adamw_math(w_ref[...], g_ref[...], m_ref[...], v_ref[...])

    outs = pl.pallas_call(body, name=name, out_shape=[jax.ShapeDtypeStruct(w.shape, F32)] * 3)(w, g, m, v)
    return [o.reshape(shape) for o in outs]


class _Exchanges:
    def __init__(self, w, m, v):
        self.w, self.m, self.v = w, m, v
        self.grads, self.delta, self.new_m, self.new_v = {}, {}, {}, {}

    def _adamw(self, names, deps):
        for n in names:
            self.delta[n], self.new_m[n], self.new_v[n] = _adamw(
                self.w[n], self.grads[n], self.m[n], self.v[n], name=f"adamw_{n}", tm=128, deps=deps)

    def begin(self):
        chip = (2 * lax.axis_index("x") + lax.axis_index("y")).astype(jnp.int32).reshape(1)
        full = {n: _place_shard(self.w[n], chip, name=f"place_{n}") for n in _BIG}
        self.later_full = [full[n] for n in _LATER]
        self.w_in_sems, self.w_in_buf, token = _copies_start("gather_w_in_direct_start", [full["w_in"]],
                                                             _plan_relay_direct, 2)
        return [token]

    def projection(self, h):
        x, y = lax.axis_index("x"), lax.axis_index("y")
        blks = [jnp.asarray(b, jnp.int32).reshape(1)
                for b in (2 * x + y, 2 * (1 - x) + y, 2 * x + (1 - y), 2 * (1 - x) + (1 - y))]
        bufs = self.w_in_buf
        proj = _mm_chip_block(h, bufs[0], blks[0], None, name="mm_proj_own")
        relay, token = [], proj
        for k, tag in enumerate(_RELATIONS[:2]):
            bufs = _copies_wait(f"gather_w_in_direct_{tag}_wait", bufs, self.w_in_sems, _plan_relay_direct, [token],
                                which=(k,))
            plan = functools.partial(_plan_relay_forward, k=k)
            sems, bufs, token = _copies_start(f"gather_w_in_relay_{tag}_start", bufs, plan, 2)
            relay.append((sems, plan))
        self.rest = _copies_start("gather_ici_rest_start", self.later_full, _plan_gather_ici, 3 * len(_LATER),
                                  after=[token])
        token = self.rest[2]
        for k, tag in enumerate(_RELATIONS[:2]):
            bufs = _copies_wait(f"gather_w_in_hand_{tag}_wait", bufs, relay[k][0], relay[k][1], [token], which=(1,))
            token = proj = _mm_chip_block(h, bufs[0], blks[1 + k], proj, name=f"mm_proj_{tag}")
        for k, tag in enumerate(_RELATIONS[:2]):
            bufs = _copies_wait(f"gather_w_in_relay_{tag}_wait", bufs, relay[k][0], relay[k][1], [token], which=(0,))
        sems, bufs, token = _copies_start("gather_w_in_last_start", bufs, _plan_relay_last, 1)
        bufs = _copies_wait("gather_w_in_last_wait", bufs, sems, _plan_relay_last, [token])
        proj = _mm_chip_block(h, bufs[0], blks[3], proj, name="mm_proj_flip_xy")
        return proj, bufs[0]

    def weight(self, name, after):
        if self.rest is not None:
            sems, bufs = self.rest
            later = dict(zip(_LATER, _copies_wait("gather_d2d_rest_wait", bufs, sems, _plan_gather_d2d, [after])))
            later["w_out"] = later["w_out"].reshape(D_MODEL, D_MODEL)
            self.later, self.rest = later, None
        return self.later[name]

    def after_proj(self, proj):
        sems, bufs, _ = self.rest
        bufs = _copies_wait("gather_ici_rest_wait", bufs, sems, _plan_gather_ici, [proj])
        sems, bufs, token = _copies_start("gather_d2d_rest_start", bufs, _plan_gather_d2d, 3 * len(_LATER))
        self.rest = (sems, bufs)
        return [token]

    def later_grads(self, grads):
        self.rs_later = _ReduceScatter("later", _LATER, [grads[n] for n in _LATER])
        return self.rs_later.start_swap()

    def before_scan_backward(self, after):
        return self.rs_later.start_scatter(after)

    def before_input_projection_grad(self, after):
        return self.rs_later.start_join(after)

    def input_projection_grad(self, g_w_in):
        self.grads.update(self.rs_later.finish([g_w_in]))
        self.rs_in = _ReduceScatter("w_in", ["w_in"], [g_w_in])
        self._adamw(_LATER, self.rs_in.start_swap())
        return self.rs_in.start_scatter([self.delta[n] for n in _LATER])

    def _adamw_small(self, names):
        for n in names:
            self.delta[n], self.new_m[n], self.new_v[n] = _adamw_whole(
                self.w[n], self.grads[n], self.m[n], self.v[n], name=f"adamw_{n}")

    def small_grads(self, grads):
        me = (4 * lax.axis_index("x") + 2 * lax.axis_index("y") + lax.axis_index("c")).astype(jnp.int32).reshape(1)
        land = _place_block(_pack_small(grads, _SMALL[1:]), me, name="place_small_grads")
        self.small = _copies_start("gather_small_start", [land], _plan_all_to_all, 7)
        return [self.small[2]]

    def finish(self, g_norm_w, loss, after):
        join = self.rs_in.start_join(after)
        sems, bufs, _ = self.small
        (land,) = _copies_wait("gather_small_wait", bufs, sems, _plan_all_to_all, join)
        self.grads.update(_unpack_small(_sum8(land, name="sum_small_grads"), self.w, _SMALL[1:]))
        self._adamw_small(_SMALL[1:])
        rows = _packed_rows(g_norm_w.size)
        late = jnp.concatenate([_pack_small(dict(norm_w=g_norm_w), _SMALL[:1]),
                                jnp.pad(loss.reshape(1, 1), ((0, SUBLANES - 1), (0, _PACK_W - 1)))], axis=0)
        late = _sum8(_all_gather_small(late), name="sum_norm_w_grad_and_loss")
        self.grads.update(_unpack_small(late[:rows], self.w, _SMALL[:1]))
        self._adamw_small(_SMALL[:1])
        self.grads.update(self.rs_in.finish([self.delta[_SMALL[0]]]))
        self._adamw(["w_in"], ())
        return late[rows, 0]


def kernel(x, norm_w, w_in, q_norm_w, k_norm_w, sinks, w_attn_proj, A_re, A_im, log_dt, B_re, B_im, C_re, C_im, D_skip, w_glu, b_glu, w_ssm_proj, w_out, loss_target, m_norm_w, m_w_in, m_q_norm_w, m_k_norm_w, m_sinks, m_w_attn_proj, m_A_re, m_A_im, m_log_dt, m_B_re, m_B_im, m_C_re, m_C_im, m_D_skip, m_w_glu, m_b_glu, m_w_ssm_proj, m_w_out, v_norm_w, v_w_in, v_q_norm_w, v_k_norm_w, v_sinks, v_w_attn_proj, v_A_re, v_A_im, v_log_dt, v_B_re, v_B_im, v_C_re, v_C_im, v_D_skip, v_w_glu, v_b_glu, v_w_ssm_proj, v_w_out):
    w = dict(norm_w=norm_w, w_in=w_in, q_norm_w=q_norm_w, k_norm_w=k_norm_w, sinks=sinks, w_attn_proj=w_attn_proj,
             A_re=A_re, A_im=A_im, log_dt=log_dt, B_re=B_re, B_im=B_im, C_re=C_re, C_im=C_im, D_skip=D_skip,
             w_glu=w_glu, b_glu=b_glu, w_ssm_proj=w_ssm_proj, w_out=w_out)
    m = dict(norm_w=m_norm_w, w_in=m_w_in, q_norm_w=m_q_norm_w, k_norm_w=m_k_norm_w, sinks=m_sinks,
             w_attn_proj=m_w_attn_proj, A_re=m_A_re, A_im=m_A_im, log_dt=m_log_dt, B_re=m_B_re, B_im=m_B_im,
             C_re=m_C_re, C_im=m_C_im, D_skip=m_D_skip, w_glu=m_w_glu, b_glu=m_b_glu, w_ssm_proj=m_w_ssm_proj,
             w_out=m_w_out)
    v = dict(norm_w=v_norm_w, w_in=v_w_in, q_norm_w=v_q_norm_w, k_norm_w=v_k_norm_w, sinks=v_sinks,
             w_attn_proj=v_w_attn_proj, A_re=v_A_re, A_im=v_A_im, log_dt=v_log_dt, B_re=v_B_re, B_im=v_B_im,
             C_re=v_C_re, C_im=v_C_im, D_skip=v_D_skip, w_glu=v_w_glu, b_glu=v_b_glu, w_ssm_proj=v_w_ssm_proj,
             w_out=v_w_out)

    io = _Exchanges(w, m, v)
    loss, grad_x, g_norm_w = _local_step(x[0], loss_target[0], norm_w, q_norm_w, k_norm_w, sinks, A_re, A_im, log_dt,
                                         B_re, B_im, C_re, C_im, D_skip, b_glu, io)
    loss = io.finish(g_norm_w, loss, [grad_x])
    grads, delta, new_m, new_v = io.grads, io.delta, io.new_m, io.new_v

    return (loss, grad_x[None], *[grads[n] for n in _ORDER], *[delta[n] for n in _ORDER],
            *[new_m[n] for n in _ORDER], *[new_v[n] for n in _ORDER])
```

```python
import functools
import math

import jax
import jax.numpy as jnp
from jax import lax
from jax.experimental import pallas as pl
from jax.experimental.pallas import tpu as pltpu

F32 = jnp.float32
_MXU = jnp.bfloat16
_WIRE = jnp.bfloat16

LANES = 128
SUBLANES = 8
VMEM_LIMIT = 56 * 1024 * 1024

D_MODEL = 2048
HEAD_DIM = 64
N_Q_HEADS = 16
N_KV_HEADS = 4
Q_PER_KV = 4
ATTN_W = 1024
KV_W = 256
WINDOW = 128
SSM_W = 1024
GROUP = 16
N_GROUPS = 64
STATE = 64
N_STATES = N_GROUPS * STATE
IN_W = 8704
NORM_EPS = 1e-6
N_CHIPS = 4
CW = 512
OFF_AGATE, OFF_U, OFF_Z, OFF_GA, OFF_GS = 3, 5, 7, 9, 13

SSM_T = 256
SSM_L = SSM_T // SUBLANES
SSM_JB = 8
SSM_SB = N_STATES // SSM_JB

ADAM_LR, ADAM_B1, ADAM_B2, ADAM_EPS, ADAM_WD, ADAM_STEP = 0.001, 0.9, 0.999, 1e-08, 0.01, 10

MESH = pl.DeviceIdType.MESH
_ANY = pl.BlockSpec(memory_space=pl.ANY)


def _params(sem=None):
    return pltpu.CompilerParams(dimension_semantics=sem, vmem_limit_bytes=VMEM_LIMIT)


def _mm(a, b, *, mode, name, tm, tn, tk, out_dtype=F32, b_blocked=False, out_blocked=False, rows_outer=False,
        deps=()):
    nd = len(deps)
    if mode == "tn":
        K, M = a.shape
    else:
        M, K = a.shape
    if mode == "nn":
        N = b.shape[0] * b.shape[2] if b_blocked else b.shape[1]
    elif mode == "nt":
        N = b.shape[1] if b_blocked else b.shape[0]
    else:
        N = b.shape[1]
    tm, tn, tk = min(tm, M), min(tn, N), min(tk, K)
    nj, ni, nk = N // tn, M // tm, K // tk
    assert nj * tn == N and ni * tm == M and nk * tk == K, (name, M, N, K)
    dims = {"nn": (((1,), (0,)), ((), ())), "nt": (((1,), (1,)), ((), ())), "tn": (((0,), (0,)), ((), ()))}[mode]

    if mode == "tn":
        a_spec = pl.BlockSpec((tk, tm), lambda j, i, k: (k, i))
    else:
        a_spec = pl.BlockSpec((tm, tk), lambda j, i, k: (i, k))
    if mode == "nn":
        if b_blocked:
            assert b.shape[0] == nj and b.shape[2] == tn
            b_spec = pl.BlockSpec((None, tk, tn), lambda j, i, k: (j, k, 0))
        else:
            b_spec = pl.BlockSpec((tk, tn), lambda j, i, k: (k, j))
    elif mode == "nt":
        if b_blocked:
            assert b.shape[0] == nk and b.shape[2] == tk
            b_spec = pl.BlockSpec((None, tn, tk), lambda j, i, k: (k, j, 0))
        else:
            b_spec = pl.BlockSpec((tn, tk), lambda j, i, k: (j, k))
    else:
        b_spec = pl.BlockSpec((tk, tn), lambda j, i, k: (k, j))
    whole_out = out_blocked and nj == 1
    if whole_out:
        assert ni == 1
        o_spec = pl.BlockSpec((N_CHIPS, tm, tn // N_CHIPS), lambda j, i, k: (0, 0, 0))
        o_shape = jax.ShapeDtypeStruct((N_CHIPS, M, tn // N_CHIPS), out_dtype)
    elif out_blocked:
        assert nj == N_CHIPS
        o_spec = pl.BlockSpec((None, tm, tn), lambda j, i, k: (j, i, 0))
        o_shape = jax.ShapeDtypeStruct((nj, M, tn), out_dtype)
    else:
        o_spec = pl.BlockSpec((tm, tn), lambda j, i, k: (i, j))
        o_shape = jax.ShapeDtypeStruct((M, N), out_dtype)
    use_acc = nk > 1 and (out_dtype != F32 or whole_out)

    def body(a_ref, b_ref, *rest):
        o_ref, scratch = rest[nd], rest[nd + 1:]

        def product():
            return lax.dot_general(a_ref[...].astype(_MXU), b_ref[...].astype(_MXU), dims, preferred_element_type=F32)

        def write(result):
            if whole_out:
                w = tn // N_CHIPS
                for c in range(N_CHIPS):
                    o_ref[c] = result[:, c * w:(c + 1) * w].astype(o_ref.dtype)
            else:
                o_ref[...] = result.astype(o_ref.dtype)

        if nk == 1:
            write(product())
            return
        k = pl.program_id(2)
        acc = scratch[0] if use_acc else o_ref

        @pl.when(k == 0)
        def _():
            acc[...] = jnp.zeros_like(acc)

        acc[...] += product()

        if use_acc:
            @pl.when(k == nk - 1)
            def _():
                write(acc[...])

    specs = [a_spec, b_spec, o_spec]
    grid = (nj, ni, nk)
    if rows_outer:
        specs = [pl.BlockSpec(s.block_shape, lambda i, j, k, f=s.index_map: f(j, i, k)) for s in specs]
        grid = (ni, nj, nk)
    return pl.pallas_call(
        body, name=name, grid=grid, in_specs=specs[:2] + [_ANY] * nd, out_specs=specs[2],
        out_shape=o_shape, scratch_shapes=[pltpu.VMEM((tm, tn), F32)] if use_acc else [],
        compiler_params=_params(("parallel", "parallel", "arbitrary")),
    )(a, b, *deps)


def _mm_chip_block(a, b4, blk, prev, *, name, tm=512, deps=()):
    M, K = a.shape
    nchip, _, C = b4.shape
    tm = min(tm, M)
    extra = ([] if prev is None else [prev]) + list(deps)

    def body(blk_ref, a_ref, b_ref, *rest):
        rest[-1][...] = jnp.dot(a_ref[...].astype(_MXU), b_ref[...].astype(_MXU), preferred_element_type=F32)

    return pl.pallas_call(
        body, name=name,
        grid_spec=pltpu.PrefetchScalarGridSpec(
            num_scalar_prefetch=1, grid=(M // tm,),
            in_specs=[pl.BlockSpec((tm, K), lambda i, c: (i, 0)), pl.BlockSpec((None, K, C), lambda i, c: (c[0], 0, 0))]
            + [_ANY] * len(extra),
            out_specs=pl.BlockSpec((tm, C), lambda i, c: (i, c[0]))),
        out_shape=jax.ShapeDtypeStruct((M, nchip * C), F32),
        input_output_aliases={} if prev is None else {3: 0},
        compiler_params=_params(("arbitrary",)),
    )(blk, a, b4, *extra)


def _mm_out_loss(merged, w_out, x, target, *, tm=256):
    rows, d = x.shape

    def body(m_ref, w_ref, x_ref, t_ref, d_ref, db_ref, sq_ref):
        mo = jnp.dot(m_ref[...].astype(_MXU), w_ref[...].astype(_MXU), preferred_element_type=F32)
        err = (x_ref[...] + mo) - t_ref[...]
        dout = err * (1.0 / d)
        d_ref[...] = dout
        db_ref[...] = dout.astype(db_ref.dtype)
        part = _colsum(err * err)
        i = pl.program_id(0)

        @pl.when(i == 0)
        def _():
            sq_ref[...] = part

        @pl.when(i > 0)
        def _():
            sq_ref[...] += part

    tile = pl.BlockSpec((tm, d), lambda i: (i, 0))
    return pl.pallas_call(
        body, name="mm_out_loss", grid=(rows // tm,),
        in_specs=[tile, pl.BlockSpec((d, d), lambda i: (0, 0)), tile, tile],
        out_specs=[tile, tile, pl.BlockSpec((1, d), lambda i: (0, 0))],
        out_shape=[jax.ShapeDtypeStruct((rows, d), F32), jax.ShapeDtypeStruct((rows, d), _MXU),
                   jax.ShapeDtypeStruct((1, d), F32)],
        compiler_params=_params(("arbitrary",)),
    )(merged, w_out, x, target)


def _mm_merge_bwd(dout_b, w_out, proj, y_a, y_s, *, tm=256):
    rows, d = y_a.shape
    ncol = d // CW

    def body(do_ref, w_ref, *refs):
        ga_refs, gs_refs = refs[:ncol], refs[ncol:2 * ncol]
        ya_ref, ys_ref, dya_ref, dys_ref, dg_ref = refs[2 * ncol:]
        dm = lax.dot_general(do_ref[...].astype(_MXU), w_ref[...].astype(_MXU), _NT, preferred_element_type=F32)
        for j in range(ncol):
            cols = slice(j * CW, (j + 1) * CW)
            dmj = dm[:, cols]
            sa, ss = _sigmoid(ga_refs[j][...]), _sigmoid(gs_refs[j][...])
            dya_ref[:, cols] = (sa * dmj).astype(dya_ref.dtype)
            dys_ref[:, cols] = (ss * dmj).astype(dys_ref.dtype)
            dg_ref[:, cols] = (dmj * ya_ref[:, cols] * sa * (1.0 - sa)).astype(dg_ref.dtype)
            dg_ref[:, d + j * CW:d + (j + 1) * CW] = (dmj * ys_ref[:, cols] * ss * (1.0 - ss)).astype(dg_ref.dtype)

    tile = pl.BlockSpec((tm, d), lambda i: (i, 0))
    gate = [pl.BlockSpec((tm, CW), lambda i, c=off + j: (i, c)) for off in (OFF_GA, OFF_GS) for j in range(ncol)]
    both = pl.BlockSpec((pl.Element(tm), pl.Element(2 * d)), lambda i: (i * tm, OFF_GA * CW))
    return pl.pallas_call(
        body, name="mm_merge_bwd", grid=(rows // tm,),
        in_specs=[tile, pl.BlockSpec((d, d), lambda i: (0, 0))] + gate + [tile, tile],
        out_specs=[tile, tile, both],
        out_shape=[jax.ShapeDtypeStruct((rows, d), _MXU)] * 2 + [jax.ShapeDtypeStruct((rows, IN_W), _MXU)],
        compiler_params=_params(("arbitrary",)),
    )(dout_b, w_out, *([proj] * (2 * ncol)), y_a, y_s)


def _ew(fn, ins, outs, *, rows, ncol, name, n_acc=0, tm=512, deps=(), into=None):
    deps = list(deps) + ([into[1]] if into else [])
    n_in, n_out, nd = len(ins), len(outs), len(deps)
    tm = min(tm, rows)
    in_specs = []
    for _, kind, col0 in ins:
        if kind == "mat":
            in_specs.append(pl.BlockSpec((tm, CW), lambda j, i, c0=col0: (i, c0 + j)))
        else:
            in_specs.append(pl.BlockSpec((1, CW), lambda j, i, c0=col0: (0, c0 + j)))
    out_specs = [pl.BlockSpec((tm, CW), lambda j, i: (i, j)) for _ in outs]
    out_shape = [jax.ShapeDtypeStruct((rows, w), dt) for w, dt in outs]
    if into:
        out_specs[into[0]] = pl.BlockSpec((tm, CW), lambda j, i, c0=into[2]: (i, c0 + j))
        out_shape[into[0]] = jax.ShapeDtypeStruct(into[1].shape, into[1].dtype)
    for _ in range(n_acc):
        out_specs.append(pl.BlockSpec((1, CW), lambda j, i: (0, j)))
        out_shape.append(jax.ShapeDtypeStruct((1, ncol * CW), F32))

    def body(*refs):
        vals = fn(*[r[...] for r in refs[:n_in]])
        refs = refs[n_in + nd:]
        for r, v in zip(refs[:n_out], vals[:n_out]):
            r[...] = v.astype(r.dtype)
        i = pl.program_id(1)
        for r, v in zip(refs[n_out:], vals[n_out:]):
            @pl.when(i == 0)
            def _(r=r, v=v):
                r[...] = v

            @pl.when(i > 0)
            def _(r=r, v=v):
                r[...] += v

    res = pl.pallas_call(
        body, name=name, grid=(ncol, rows // tm), in_specs=in_specs + [_ANY] * nd, out_specs=out_specs,
        out_shape=out_shape, input_output_aliases={n_in + nd - 1: into[0]} if into else {},
        compiler_params=_params(("parallel", "arbitrary")),
    )(*[a for a, _, _ in ins], *deps)
    return res


def _colsum(v):
    return jnp.sum(v, axis=0, keepdims=True)


def _sigmoid(v):
    return jax.nn.sigmoid(v)


def _silu_and_grad(v):
    s = _sigmoid(v)
    return v * s, s * (1.0 + v * (1.0 - s))


def _rms_fwd(x, w, *, tm=512, deps=()):
    rows, d = x.shape
    nd = len(deps)

    def body(x_ref, w_ref, *rest):
        h_ref, r_ref = rest[nd:]
        xv = x_ref[...]
        r = lax.rsqrt(jnp.mean(xv * xv, axis=-1, keepdims=True) + NORM_EPS)
        h_ref[...] = (xv * r * w_ref[...]).astype(h_ref.dtype)
        r_ref[...] = r

    return pl.pallas_call(
        body, name="rms_fwd", grid=(rows // tm,),
        in_specs=[pl.BlockSpec((tm, d), lambda i: (i, 0)), pl.BlockSpec((1, d), lambda i: (0, 0))] + [_ANY] * nd,
        out_specs=[pl.BlockSpec((tm, d), lambda i: (i, 0)), pl.BlockSpec((tm, 1), lambda i: (i, 0))],
        out_shape=[jax.ShapeDtypeStruct((rows, d), _MXU), jax.ShapeDtypeStruct((rows, 1), F32)],
        compiler_params=_params(("arbitrary",)),
    )(x, w, *deps)


def _rms_bwd(dh, x, rstd, w, dout, *, tm=256):
    rows, d = x.shape

    def body(dh_ref, x_ref, r_ref, w_ref, do_ref, gx_ref, gw_ref):
        dhv, xv, r, wv = dh_ref[...], x_ref[...], r_ref[...], w_ref[...]
        xr = xv * r
        t = jnp.mean(dhv * wv * xr, axis=-1, keepdims=True)
        gx_ref[...] = do_ref[...] + r * (wv * dhv - xr * t)
        part = _colsum(dhv * xr)
        i = pl.program_id(0)

        @pl.when(i == 0)
        def _():
            gw_ref[...] = part

        @pl.when(i > 0)
        def _():
            gw_ref[...] += part

    return pl.pallas_call(
        body, name="rms_bwd", grid=(rows // tm,),
        in_specs=[pl.BlockSpec((tm, d), lambda i: (i, 0)), pl.BlockSpec((tm, d), lambda i: (i, 0)),
                  pl.BlockSpec((tm, 1), lambda i: (i, 0)), pl.BlockSpec((1, d), lambda i: (0, 0)),
                  pl.BlockSpec((tm, d), lambda i: (i, 0))],
        out_specs=[pl.BlockSpec((tm, d), lambda i: (i, 0)), pl.BlockSpec((1, d), lambda i: (0, 0))],
        out_shape=[jax.ShapeDtypeStruct((rows, d), F32), jax.ShapeDtypeStruct((1, d), F32)],
        compiler_params=_params(("arbitrary",)),
    )(dh, x, rstd, w, dout)


_NT = (((1,), (1,)), ((), ()))
_TN = (((0,), (0,)), ((), ()))


QKV_W = ATTN_W + 2 * KV_W
HEADS_PER_TILE = LANES // HEAD_DIM


def _low_half(rows):
    return lax.broadcasted_iota(jnp.int32, (rows, LANES), 1) < HEAD_DIM


def _pair_mean(t, low):
    m_lo = jnp.sum(jnp.where(low, t, 0.0), axis=-1, keepdims=True)
    m_hi = jnp.sum(jnp.where(low, 0.0, t), axis=-1, keepdims=True)
    return jnp.where(low, m_lo, m_hi) * (1.0 / HEAD_DIM)


def _pair_rstd(t, low):
    return lax.rsqrt(_pair_mean(t * t, low) + NORM_EPS)


def _dup_half(t, hi, low):
    swapped = pltpu.roll(t, HEAD_DIM, 1)
    return jnp.where(low, swapped, t) if hi else jnp.where(low, t, swapped)


def _fold_halves(t):
    return t + pltpu.roll(t, HEAD_DIM, 1)


def _split_heads(t, low):
    return [jnp.where(low, t, 0.0), jnp.where(low, 0.0, t)]


def _stacked_band_mask(n):
    rows = Q_PER_KV * WINDOW
    qi = lax.broadcasted_iota(jnp.int32, (rows, 2 * WINDOW), 0) % WINDOW + WINDOW
    kj = lax.broadcasted_iota(jnp.int32, (rows, 2 * WINDOW), 1)
    diff = qi - kj
    first_key = jnp.where(n > 0, 0, WINDOW)
    return (diff >= 0) & (diff < WINDOW) & (kj >= first_key)


def _stacked_sinks(sink_ref, g):
    blk = lax.broadcasted_iota(jnp.int32, (Q_PER_KV * WINDOW, 1), 0) // WINDOW
    col = jnp.full((Q_PER_KV * WINDOW, 1), sink_ref[Q_PER_KV * g], F32)
    for r in range(1, Q_PER_KV):
        col = jnp.where(blk == r, sink_ref[Q_PER_KV * g + r], col)
    return col


def _attn_in_specs(nblk, rev):
    def cur(n):
        return (nblk - 1 - n) if rev else n

    q_spec = pl.BlockSpec((WINDOW, ATTN_W), lambda n: (cur(n), 0))
    kvc_spec = pl.BlockSpec((WINDOW, 2 * KV_W), lambda n: (cur(n), ATTN_W // (2 * KV_W)))
    kvp_spec = pl.BlockSpec((WINDOW, 2 * KV_W), lambda n: (jnp.maximum(cur(n) - 1, 0), ATTN_W // (2 * KV_W)))
    w_spec = pl.BlockSpec((1, LANES), lambda n: (0, 0))
    l_spec = pl.BlockSpec((WINDOW, N_Q_HEADS), lambda n: (cur(n), 0))
    gate_specs = [pl.BlockSpec((WINDOW, CW), lambda n, col=OFF_AGATE + j: (cur(n), col)) for j in range(ATTN_W // CW)]
    return q_spec, kvc_spec, kvp_spec, w_spec, l_spec, gate_specs


def _attn2_fwd(proj, qw2, kw2, sinks, deps=()):
    seq = proj.shape[0]
    nblk = seq // WINDOW
    scale = 1.0 / math.sqrt(HEAD_DIM)
    q_spec, kvc_spec, kvp_spec, w_spec, l_spec, gate_specs = _attn_in_specs(nblk, False)
    nd, ng = len(deps), len(gate_specs)

    def body(sink_ref, q_ref, kvc_ref, kvp_ref, qw_ref, kw_ref, *rest):
        gate_refs = rest[:ng]
        o_ref, lse_ref, ya_ref = rest[ng + nd:]
        n = pl.program_id(0)
        low, low2 = _low_half(WINDOW), _low_half(2 * WINDOW)
        valid = _stacked_band_mask(n)
        head_lane = lax.broadcasted_iota(jnp.int32, (WINDOW, N_Q_HEADS), 1)
        kv = jnp.concatenate([kvp_ref[...], kvc_ref[...]], axis=0)
        qwv, kwv = qw_ref[...], kw_ref[...]
        lse_blk = jnp.zeros((WINDOW, N_Q_HEADS), F32)
        for t in range(N_KV_HEADS // HEADS_PER_TILE):
            kt = kv[:, t * LANES:(t + 1) * LANES]
            vt = kv[:, KV_W + t * LANES:KV_W + (t + 1) * LANES]
            kn = kt * _pair_rstd(kt, low2) * kwv
            for hi in range(HEADS_PER_TILE):
                g = HEADS_PER_TILE * t + hi
                kdup = _dup_half(kn, hi, low2).astype(_MXU)
                vdup = _dup_half(vt, hi, low2).astype(_MXU)
                stack = []
                for tq in (2 * g, 2 * g + 1):
                    qt = q_ref[:, tq * LANES:(tq + 1) * LANES]
                    stack += _split_heads(qt * _pair_rstd(qt, low) * qwv, low)
                qs = jnp.concatenate(stack, axis=0).astype(_MXU)
                s = lax.dot_general(qs, kdup, _NT, preferred_element_type=F32) * scale
                s = jnp.where(valid, s, -1e30)
                sink = _stacked_sinks(sink_ref, g)
                m = jnp.maximum(jnp.max(s, axis=-1, keepdims=True), sink)
                e = jnp.exp(s - m)
                z = jnp.sum(e, axis=-1, keepdims=True) + jnp.exp(sink - m)
                o = jnp.dot((e / z).astype(_MXU), vdup, preferred_element_type=F32)
                for i, tq in enumerate((2 * g, 2 * g + 1)):
                    o_ref[:, tq * LANES:(tq + 1) * LANES] = jnp.where(
                        low, o[2 * i * WINDOW:(2 * i + 1) * WINDOW], o[(2 * i + 1) * WINDOW:(2 * i + 2) * WINDOW])
                lse = m + jnp.log(z)
                for r in range(Q_PER_KV):
                    lse_blk = jnp.where(head_lane == Q_PER_KV * g + r, lse[r * WINDOW:(r + 1) * WINDOW], lse_blk)
        lse_ref[...] = lse_blk
        for j, g_ref in enumerate(gate_refs):
            cols = slice(j * CW, (j + 1) * CW)
            gate = g_ref[...]
            ya_ref[:, cols] = (o_ref[:, cols] * (gate * _sigmoid(gate))).astype(ya_ref.dtype)

    return pl.pallas_call(
        body, name="attn_fwd", grid=(nblk,),
        in_specs=[pl.BlockSpec(memory_space=pltpu.SMEM), q_spec, kvc_spec, kvp_spec, w_spec, w_spec] + gate_specs
        + [_ANY] * nd,
        out_specs=[q_spec, l_spec, q_spec],
        out_shape=[jax.ShapeDtypeStruct((seq, ATTN_W), F32), jax.ShapeDtypeStruct((seq, N_Q_HEADS), F32),
                   jax.ShapeDtypeStruct((seq, ATTN_W), _MXU)],
        compiler_params=_params(("arbitrary",)),
    )(sinks, proj, proj, proj, qw2, kw2, *([proj] * ng), *deps)


def _attn2_bwd(proj, qw2, kw2, sinks, lse, attn, dya, d_proj, deps=()):
    seq = proj.shape[0]
    nblk = seq // WINDOW
    scale = 1.0 / math.sqrt(HEAD_DIM)
    q_spec, kvc_spec, kvp_spec, w_spec, l_spec, gate_specs = _attn_in_specs(nblk, True)
    s_spec = pl.BlockSpec((1, N_Q_HEADS), lambda n: (0, 0))
    d_spec = pl.BlockSpec((WINDOW, QKV_W + ATTN_W), lambda n: (nblk - 1 - n, 0))
    deps = list(deps) + [d_proj]
    nd, ng = len(deps), len(gate_specs)

    def body(sink_ref, q_ref, kvc_ref, kvp_ref, qw_ref, kw_ref, lse_ref, attn_ref, dya_ref, *rest):
        gate_refs = rest[:ng]
        d_ref, dqw_ref, dkw_ref, dsk_ref, carry, do_ref = rest[ng + nd:]
        step = pl.program_id(0)
        n = nblk - 1 - step

        @pl.when(step == 0)
        def _():
            carry[...] = jnp.zeros_like(carry)
            dqw_ref[...] = jnp.zeros_like(dqw_ref)
            dkw_ref[...] = jnp.zeros_like(dkw_ref)
            dsk_ref[...] = jnp.zeros_like(dsk_ref)

        for j, g_ref in enumerate(gate_refs):
            cols = slice(j * CW, (j + 1) * CW)
            f, df = _silu_and_grad(g_ref[...])
            dv = dya_ref[:, cols]
            do_ref[:, cols] = dv * f
            d_ref[:, QKV_W + j * CW:QKV_W + (j + 1) * CW] = (dv * attn_ref[:, cols] * df).astype(d_ref.dtype)

        low, low2 = _low_half(WINDOW), _low_half(2 * WINDOW)
        valid = _stacked_band_mask(n)
        head_lane = lax.broadcasted_iota(jnp.int32, (WINDOW, N_Q_HEADS), 1)
        sink_lane = lax.broadcasted_iota(jnp.int32, (1, N_Q_HEADS), 1)
        kv = jnp.concatenate([kvp_ref[...], kvc_ref[...]], axis=0)
        qwv, kwv = qw_ref[...], kw_ref[...]
        lse_blk = lse_ref[...]
        dqw = jnp.zeros((1, LANES), F32)
        dkw = jnp.zeros((1, LANES), F32)
        dsk = jnp.zeros((1, N_Q_HEADS), F32)
        for t in range(N_KV_HEADS // HEADS_PER_TILE):
            kt = kv[:, t * LANES:(t + 1) * LANES]
            vt = kv[:, KV_W + t * LANES:KV_W + (t + 1) * LANES]
            rk = _pair_rstd(kt, low2)
            kn = kt * rk * kwv
            dkn_t = jnp.zeros((2 * WINDOW, LANES), F32)
            dv_t = jnp.zeros((2 * WINDOW, LANES), F32)
            for hi in range(HEADS_PER_TILE):
                g = HEADS_PER_TILE * t + hi
                kdup = _dup_half(kn, hi, low2).astype(_MXU)
                vdup = _dup_half(vt, hi, low2).astype(_MXU)
                tiles = (2 * g, 2 * g + 1)
                qx, rq, stack, dstack, lse_rows = [], [], [], [], []
                for tq in tiles:
                    qt = q_ref[:, tq * LANES:(tq + 1) * LANES]
                    r = _pair_rstd(qt, low)
                    rq.append(r)
                    qx.append(qt * r)
                    stack += _split_heads(qx[-1] * qwv, low)
                    dstack += _split_heads(do_ref[:, tq * LANES:(tq + 1) * LANES], low)
                for r in range(Q_PER_KV):
                    lse_rows.append(jnp.sum(jnp.where(head_lane == Q_PER_KV * g + r, lse_blk, 0.0), axis=-1, keepdims=True))
                qs = jnp.concatenate(stack, axis=0).astype(_MXU)
                dos = jnp.concatenate(dstack, axis=0).astype(_MXU)
                lse_col = jnp.concatenate(lse_rows, axis=0)
                s = lax.dot_general(qs, kdup, _NT, preferred_element_type=F32) * scale
                s = jnp.where(valid, s, -1e30)
                p = jnp.exp(s - lse_col)
                dp = lax.dot_general(dos, vdup, _NT, preferred_element_type=F32)
                dsum = jnp.sum(p * dp, axis=-1, keepdims=True)
                ds = (p * (dp - dsum) * scale).astype(_MXU)
                dsink = -jnp.exp(_stacked_sinks(sink_ref, g) - lse_col) * dsum
                for r in range(Q_PER_KV):
                    dsk = dsk + jnp.where(sink_lane == Q_PER_KV * g + r, _colsum(dsink[r * WINDOW:(r + 1) * WINDOW]), 0.0)
                dv_g = _fold_halves(lax.dot_general(p.astype(_MXU), dos, _TN, preferred_element_type=F32))
                dkn_g = _fold_halves(lax.dot_general(ds, qs, _TN, preferred_element_type=F32))
                dv_t = jnp.where(low2, dv_t, dv_g) if hi else jnp.where(low2, dv_g, dv_t)
                dkn_t = jnp.where(low2, dkn_t, dkn_g) if hi else jnp.where(low2, dkn_g, dkn_t)
                dqn = jnp.dot(ds, kdup, preferred_element_type=F32)
                for i, tq in enumerate(tiles):
                    dqn_t = jnp.where(low, dqn[2 * i * WINDOW:(2 * i + 1) * WINDOW],
                                      dqn[(2 * i + 1) * WINDOW:(2 * i + 2) * WINDOW])
                    dq = rq[i] * (qwv * dqn_t - qx[i] * _pair_mean(dqn_t * qwv * qx[i], low))
                    d_ref[:, tq * LANES:(tq + 1) * LANES] = dq.astype(d_ref.dtype)
                    dqw = dqw + _colsum(dqn_t * qx[i])
            k_cols = slice(t * LANES, (t + 1) * LANES)
            v_cols = slice(KV_W + t * LANES, KV_W + (t + 1) * LANES)
            dkn_c = dkn_t[WINDOW:] + carry[:, k_cols]
            rc = rk[WINDOW:]
            kx = kt[WINDOW:] * rc
            dk = rc * (kwv * dkn_c - kx * _pair_mean(dkn_c * kwv * kx, low))
            d_ref[:, ATTN_W + t * LANES:ATTN_W + (t + 1) * LANES] = dk.astype(d_ref.dtype)
            d_ref[:, ATTN_W + KV_W + t * LANES:ATTN_W + KV_W + (t + 1) * LANES] = (
                dv_t[WINDOW:] + carry[:, v_cols]).astype(d_ref.dtype)
            carry[:, k_cols] = dkn_t[:WINDOW]
            carry[:, v_cols] = dv_t[:WINDOW]
            dkw = dkw + _colsum(dkn_c * kx)
        dqw_ref[...] += dqw
        dkw_ref[...] += dkw
        dsk_ref[...] += dsk

    return pl.pallas_call(
        body, name="attn_bwd", grid=(nblk,),
        in_specs=[pl.BlockSpec(memory_space=pltpu.SMEM), q_spec, kvc_spec, kvp_spec, w_spec, w_spec, l_spec, q_spec,
                  q_spec] + gate_specs + [_ANY] * nd,
        out_specs=[d_spec, w_spec, w_spec, s_spec],
        out_shape=[jax.ShapeDtypeStruct(d_proj.shape, d_proj.dtype), jax.ShapeDtypeStruct((1, LANES), F32),
                   jax.ShapeDtypeStruct((1, LANES), F32), jax.ShapeDtypeStruct((1, N_Q_HEADS), F32)],
        input_output_aliases={9 + ng + nd - 1: 0},
        scratch_shapes=[pltpu.VMEM((WINDOW, 2 * KV_W), F32), pltpu.VMEM((WINDOW, ATTN_W), F32)],
        compiler_params=_params(("arbitrary",)),
    )(sinks, proj, proj, proj, qw2, kw2, lse, attn, dya, *([proj] * ng), *deps)


def _ssm_discretise(a_re, a_im, log_dt):
    dt = jnp.exp(log_dt)
    mag = jnp.exp(dt * a_re)
    ab_re = mag * jnp.cos(dt * a_im)
    ab_im = mag * jnp.sin(dt * a_im)
    num_re = ab_re - 1.0
    num_im = ab_im
    den = a_re * a_re + a_im * a_im
    cf_re = (num_re * a_re + num_im * a_im) / den
    cf_im = (num_im * a_re - num_re * a_im) / den
    return ab_re, ab_im, cf_re, cf_im


def _ssm_params_fwd(a_re, a_im, log_dt):
    shp = jax.ShapeDtypeStruct(a_re.shape, F32)

    def body(are_ref, aim_ref, ldt_ref, abr_ref, abi_ref, cfr_ref, cfi_ref, alr_ref, ali_ref):
        abr, abi, cfr, cfi = _ssm_discretise(are_ref[...], aim_ref[...], ldt_ref[...])
        abr_ref[...], abi_ref[...], cfr_ref[...], cfi_ref[...] = abr, abi, cfr, cfi
        pr, pi = abr, abi
        for _ in range(int(math.log2(SSM_L))):
            pr, pi = pr * pr - pi * pi, 2.0 * pr * pi
        alr_ref[...], ali_ref[...] = pr, pi

    return pl.pallas_call(body, name="ssm_params_fwd", out_shape=[shp] * 6)(a_re, a_im, log_dt)


def _ssm_params_bwd(a_re, a_im, log_dt, d_abr, d_abi, d_cfr, d_cfi):
    def body(are_ref, aim_ref, ldt_ref, g0, g1, g2, g3, dare_ref, daim_ref, dldt_ref):
        _, vjp = jax.vjp(_ssm_discretise, are_ref[...], aim_ref[...], ldt_ref[...])
        dare_ref[...], daim_ref[...], dldt_ref[...] = vjp((g0[...], g1[...], g2[...], g3[...]))

    return pl.pallas_call(
        body, name="ssm_params_bwd",
        out_shape=[jax.ShapeDtypeStruct(a_re.shape, F32), jax.ShapeDtypeStruct(a_im.shape, F32),
                   jax.ShapeDtypeStruct(log_dt.shape, F32)],
    )(a_re, a_im, log_dt, d_abr, d_abi, d_cfr, d_cfi)


def _scan_cols(j):
    return pl.ds(j * SSM_SB, SSM_SB)


def _rows8(r):
    return pl.ds(pl.multiple_of(r * SUBLANES, SUBLANES), SUBLANES)


def _bcast8(row):
    return jnp.broadcast_to(row, (SUBLANES, row.shape[-1]))


def _token_order_pick():
    tok = lax.broadcasted_iota(jnp.int32, (SSM_T, SSM_T), 0)
    row = lax.broadcasted_iota(jnp.int32, (SSM_T, SSM_T), 1)
    return (row == SUBLANES * (tok % SSM_L) + tok // SSM_L).astype(_MXU)


SCAN_UNROLL = 8


def _scan_loop(n, step, init):
    def trip(o, carry):
        for i in range(SCAN_UNROLL):
            carry = step(o * SCAN_UNROLL + i, carry)
        return carry

    return lax.fori_loop(0, n // SCAN_UNROLL, trip, init)


def _ssm_fwd(u, b_re, b_im, c_re, c_im, d_skip, coef):
    seq = u.shape[0]
    nc = seq // SSM_T
    T, L = SSM_T, SSM_L

    def body(u_ref, bre_ref, bim_ref, cre_ref, cim_ref, d_ref, are_ref, aim_ref, cfr_ref, cfi_ref, alr_ref, ali_ref,
             y_ref, yg_ref, sre_ref, sim_ref, ire_ref, iim_ref, car_re, car_im, end_re, end_im, yg_scan):
        c = pl.program_id(0)

        @pl.when(c == 0)
        def _():
            car_re[...] = jnp.zeros_like(car_re)
            car_im[...] = jnp.zeros_like(car_im)

        for j in range(SSM_JB):
            ub = u_ref[:, j * LANES:(j + 1) * LANES].astype(_MXU)
            bur = jnp.dot(ub, bre_ref[j], preferred_element_type=F32)
            bui = jnp.dot(ub, bim_ref[j], preferred_element_type=F32)
            cfr, cfi = cfr_ref[:, _scan_cols(j)], cfi_ref[:, _scan_cols(j)]
            sre_ref[:, _scan_cols(j)] = cfr * bur - cfi * bui
            sim_ref[:, _scan_cols(j)] = cfr * bui + cfi * bur

        for j in range(SSM_JB):
            cols = _scan_cols(j)
            ar, ai = _bcast8(are_ref[:, cols]), _bcast8(aim_ref[:, cols])

            def step1(r, s, cols=cols, ar=ar, ai=ai):
                sr, si = s
                rows = _rows8(r)
                return (ar * sr - ai * si + sre_ref[rows, cols], ar * si + ai * sr + sim_ref[rows, cols])

            zero = jnp.zeros((SUBLANES, SSM_SB), F32)
            er, ei = _scan_loop(L, step1, (zero, zero))
            end_re[:, cols] = er
            end_im[:, cols] = ei

        alr, ali = alr_ref[...], ali_ref[...]
        cr, ci = car_re[...], car_im[...]
        ire_ref[0:1, :] = cr
        iim_ref[0:1, :] = ci
        for i in range(1, SUBLANES):
            er, ei = end_re[i - 1:i, :], end_im[i - 1:i, :]
            cr, ci = alr * cr - ali * ci + er, alr * ci + ali * cr + ei
            ire_ref[i:i + 1, :] = cr
            iim_ref[i:i + 1, :] = ci

        for j in range(SSM_JB):
            cols = _scan_cols(j)
            ar, ai = _bcast8(are_ref[:, cols]), _bcast8(aim_ref[:, cols])

            def step2(r, s, cols=cols, ar=ar, ai=ai):
                sr, si = s
                rows = _rows8(r)
                nr = ar * sr - ai * si + sre_ref[rows, cols]
                ni = ar * si + ai * sr + sim_ref[rows, cols]
                sre_ref[rows, cols] = nr
                sim_ref[rows, cols] = ni
                return nr, ni

            _scan_loop(L, step2, (ire_ref[:, cols], iim_ref[:, cols]))

        car_re[...] = sre_ref[T - 1:T, :]
        car_im[...] = sim_ref[T - 1:T, :]

        for j in range(SSM_JB):
            cols = _scan_cols(j)
            ch = slice(j * LANES, (j + 1) * LANES)
            y = (jnp.dot(sre_ref[:, cols].astype(_MXU), cre_ref[j], preferred_element_type=F32)
                 - jnp.dot(sim_ref[:, cols].astype(_MXU), cim_ref[j], preferred_element_type=F32))
            y = y + d_ref[:, ch] * u_ref[:, ch]
            y_ref[:, ch] = y
            yg_scan[:, ch] = jax.nn.gelu(y).astype(yg_scan.dtype)
        yg_ref[...] = jnp.dot(_token_order_pick(), yg_scan[...], preferred_element_type=F32).astype(yg_ref.dtype)

    tok = pl.BlockSpec((T, SSM_W), lambda c: (c, 0))
    st = pl.BlockSpec((T, N_STATES), lambda c: (c, 0))
    ini = pl.BlockSpec((None, SUBLANES, N_STATES), lambda c: (c, 0, 0))
    bsp = pl.BlockSpec((SSM_JB, LANES, SSM_SB), lambda c: (0, 0, 0))
    csp = pl.BlockSpec((SSM_JB, SSM_SB, LANES), lambda c: (0, 0, 0))
    row_w = pl.BlockSpec((1, SSM_W), lambda c: (0, 0))
    row_s = pl.BlockSpec((1, N_STATES), lambda c: (0, 0))
    return pl.pallas_call(
        body, name="ssm_fwd", grid=(nc,),
        in_specs=[tok, bsp, bsp, csp, csp, row_w] + [row_s] * 6,
        out_specs=[tok, tok, st, st, ini, ini],
        out_shape=[jax.ShapeDtypeStruct((seq, SSM_W), F32), jax.ShapeDtypeStruct((seq, SSM_W), _MXU),
                   jax.ShapeDtypeStruct((seq, N_STATES), F32), jax.ShapeDtypeStruct((seq, N_STATES), F32),
                   jax.ShapeDtypeStruct((nc, SUBLANES, N_STATES), F32),
                   jax.ShapeDtypeStruct((nc, SUBLANES, N_STATES), F32)],
        scratch_shapes=[pltpu.VMEM((1, N_STATES), F32), pltpu.VMEM((1, N_STATES), F32),
                        pltpu.VMEM((SUBLANES, N_STATES), F32), pltpu.VMEM((SUBLANES, N_STATES), F32),
                        pltpu.VMEM((T, SSM_W), _MXU)],
        compiler_params=_params(("arbitrary",)),
    )(u, b_re, b_im, c_re, c_im, d_skip, *coef)


def _ssm_bwd(dyg, y, u, s_re, s_im, i_re, i_im, b_re, b_im, c_re, c_im, d_skip, coef, d_proj, deps=()):
    seq = u.shape[0]
    nc = seq // SSM_T
    T, L = SSM_T, SSM_L
    deps = list(deps) + [d_proj]

    def body(dyg_ref, y_ref, u_ref, sre_ref, sim_ref, ire_ref, iim_ref, bre_ref, bim_ref, cre_ref, cim_ref, d_ref,
             are_ref, aim_ref, cfr_ref, cfi_ref, alr_ref, ali_ref, *rest):
        (du_ref, dbre_out, dbim_out, dcre_out, dcim_out, dd_ref, dar_ref, dai_ref, dcfr_ref, dcfi_ref,
         lre, lim, car_re, car_im, end_re, end_im, ini_re, ini_im, dbre_ref, dbim_ref, dcre_ref, dcim_ref,
         dy_ref, du_scan) = rest[len(deps):]
        step = pl.program_id(0)
        dy_ref[...] = jax.vjp(jax.nn.gelu, y_ref[...])[1](dyg_ref[...])[0]

        @pl.when(step == 0)
        def _():
            car_re[...] = jnp.zeros_like(car_re)
            car_im[...] = jnp.zeros_like(car_im)
            for ref in (dbre_ref, dbim_ref, dcre_ref, dcim_ref, dd_ref, dar_ref, dai_ref, dcfr_ref, dcfi_ref):
                ref[...] = jnp.zeros_like(ref)

        for j in range(SSM_JB):
            dyb = dy_ref[:, j * LANES:(j + 1) * LANES].astype(_MXU)
            lre[:, _scan_cols(j)] = lax.dot_general(dyb, cre_ref[j], _NT, preferred_element_type=F32)
            lim[:, _scan_cols(j)] = -lax.dot_general(dyb, cim_ref[j], _NT, preferred_element_type=F32)

        for j in range(SSM_JB):
            cols = _scan_cols(j)
            ar, ai = _bcast8(are_ref[:, cols]), _bcast8(aim_ref[:, cols])

            def step1(t, s, cols=cols, ar=ar, ai=ai):
                sr, si = s
                rows = _rows8(L - 1 - t)
                return (ar * sr + ai * si + lre[rows, cols], ar * si - ai * sr + lim[rows, cols])

            zero = jnp.zeros((SUBLANES, SSM_SB), F32)
            er, ei = _scan_loop(L, step1, (zero, zero))
            end_re[:, cols] = er
            end_im[:, cols] = ei

        alr, ali = alr_ref[...], ali_ref[...]
        cr, ci = car_re[...], car_im[...]
        ini_re[SUBLANES - 1:SUBLANES, :] = cr
        ini_im[SUBLANES - 1:SUBLANES, :] = ci
        for i in range(SUBLANES - 2, -1, -1):
            er, ei = end_re[i + 1:i + 2, :], end_im[i + 1:i + 2, :]
            cr, ci = alr * cr + ali * ci + er, alr * ci - ali * cr + ei
            ini_re[i:i + 1, :] = cr
            ini_im[i:i + 1, :] = ci

        for j in range(SSM_JB):
            cols = _scan_cols(j)
            ar, ai = _bcast8(are_ref[:, cols]), _bcast8(aim_ref[:, cols])

            def step2(t, s, cols=cols, ar=ar, ai=ai):
                sr, si = s
                rows = _rows8(L - 1 - t)
                nr = ar * sr + ai * si + lre[rows, cols]
                ni = ar * si - ai * sr + lim[rows, cols]
                lre[rows, cols] = nr
                lim[rows, cols] = ni
                return nr, ni

            _scan_loop(L, step2, (ini_re[:, cols], ini_im[:, cols]))

        car_re[...] = lre[0:1, :]
        car_im[...] = lim[0:1, :]

        head, tail, body_rows = slice(0, SUBLANES), slice(SUBLANES, T), slice(0, T - SUBLANES)
        for j in range(SSM_JB):
            cols = _scan_cols(j)
            ch = slice(j * LANES, (j + 1) * LANES)
            lr, li = lre[:, cols], lim[:, cols]
            dar_ref[:, cols] += (_colsum(lre[tail, cols] * sre_ref[body_rows, cols] + lim[tail, cols] * sim_ref[body_rows, cols])
                                 + _colsum(lre[head, cols] * ire_ref[:, cols] + lim[head, cols] * iim_ref[:, cols]))
            dai_ref[:, cols] += (_colsum(lim[tail, cols] * sre_ref[body_rows, cols] - lre[tail, cols] * sim_ref[body_rows, cols])
                                 + _colsum(lim[head, cols] * ire_ref[:, cols] - lre[head, cols] * iim_ref[:, cols]))
            uf = u_ref[:, ch]
            ub = uf.astype(_MXU)
            bur = jnp.dot(ub, bre_ref[j], preferred_element_type=F32)
            bui = jnp.dot(ub, bim_ref[j], preferred_element_type=F32)
            dcfr_ref[:, cols] += _colsum(lr * bur + li * bui)
            dcfi_ref[:, cols] += _colsum(li * bur - lr * bui)
            cfr, cfi = cfr_ref[:, cols], cfi_ref[:, cols]
            dbur = (cfr * lr + cfi * li).astype(_MXU)
            dbui = (cfr * li - cfi * lr).astype(_MXU)
            dyf = dy_ref[:, ch]
            dyb = dyf.astype(_MXU)
            du = (lax.dot_general(dbur, bre_ref[j], _NT, preferred_element_type=F32)
                  + lax.dot_general(dbui, bim_ref[j], _NT, preferred_element_type=F32) + d_ref[:, ch] * dyf)
            du_scan[:, ch] = du.astype(du_scan.dtype)
            dbre_ref[j] += lax.dot_general(ub, dbur, _TN, preferred_element_type=F32)
            dbim_ref[j] += lax.dot_general(ub, dbui, _TN, preferred_element_type=F32)
            dcre_ref[j] += lax.dot_general(sre_ref[:, cols].astype(_MXU), dyb, _TN, preferred_element_type=F32)
            dcim_ref[j] -= lax.dot_general(sim_ref[:, cols].astype(_MXU), dyb, _TN, preferred_element_type=F32)
            dd_ref[:, ch] += _colsum(dyf * uf)
        du_ref[...] = jnp.dot(_token_order_pick(), du_scan[...], preferred_element_type=F32).astype(du_ref.dtype)

        @pl.when(step == nc - 1)
        def _():
            for acc, out in ((dbre_ref, dbre_out), (dbim_ref, dbim_out), (dcre_ref, dcre_out), (dcim_ref, dcim_out)):
                pltpu.sync_copy(acc, out)

    tok = pl.BlockSpec((T, SSM_W), lambda c: (nc - 1 - c, 0))
    st = pl.BlockSpec((T, N_STATES), lambda c: (nc - 1 - c, 0))
    ini = pl.BlockSpec((None, SUBLANES, N_STATES), lambda c: (nc - 1 - c, 0, 0))
    bsp = pl.BlockSpec((SSM_JB, LANES, SSM_SB), lambda c: (0, 0, 0))
    csp = pl.BlockSpec((SSM_JB, SSM_SB, LANES), lambda c: (0, 0, 0))
    row_w = pl.BlockSpec((1, SSM_W), lambda c: (0, 0))
    row_s = pl.BlockSpec((1, N_STATES), lambda c: (0, 0))
    big = pltpu.VMEM((T, N_STATES), F32)
    one = pltpu.VMEM((1, N_STATES), F32)
    eight = pltpu.VMEM((SUBLANES, N_STATES), F32)
    return pl.pallas_call(
        body, name="ssm_bwd", grid=(nc,),
        in_specs=[tok, tok, tok, st, st, ini, ini, bsp, bsp, csp, csp, row_w] + [row_s] * 6 + [_ANY] * len(deps),
        out_specs=[pl.BlockSpec((pl.Element(T), pl.Element(SSM_W)), lambda c: ((nc - 1 - c) * T, OFF_U * CW)),
                   _ANY, _ANY, _ANY, _ANY, row_w, row_s, row_s, row_s, row_s],
        input_output_aliases={18 + len(deps) - 1: 0},
        out_shape=[jax.ShapeDtypeStruct(d_proj.shape, d_proj.dtype),
                   jax.ShapeDtypeStruct((SSM_JB, LANES, SSM_SB), F32), jax.ShapeDtypeStruct((SSM_JB, LANES, SSM_SB), F32),
                   jax.ShapeDtypeStruct((SSM_JB, SSM_SB, LANES), F32), jax.ShapeDtypeStruct((SSM_JB, SSM_SB, LANES), F32),
                   jax.ShapeDtypeStruct((1, SSM_W), F32)] + [jax.ShapeDtypeStruct((1, N_STATES), F32)] * 4,
        scratch_shapes=[big, big, one, one, eight, eight, eight, eight,
                        pltpu.VMEM((SSM_JB, LANES, SSM_SB), F32), pltpu.VMEM((SSM_JB, LANES, SSM_SB), F32),
                        pltpu.VMEM((SSM_JB, SSM_SB, LANES), F32), pltpu.VMEM((SSM_JB, SSM_SB, LANES), F32),
                        pltpu.VMEM((T, SSM_W), F32), pltpu.VMEM((T, SSM_W), _MXU)],
        compiler_params=_params(("arbitrary",)),
    )(dyg, y, u, s_re, s_im, i_re, i_im, b_re, b_im, c_re, c_im, d_skip, *coef, *deps)


def _block_diag_b(b):
    t = b.reshape(SSM_JB, 8, STATE, GROUP).transpose(0, 1, 3, 2)
    eye = jnp.eye(8, dtype=b.dtype)
    return (t[:, :, :, None, :] * eye[None, :, None, :, None]).reshape(SSM_JB, LANES, SSM_SB)


def _block_diag_c(c):
    t = c.reshape(SSM_JB, 8, GROUP, STATE).transpose(0, 1, 3, 2)
    eye = jnp.eye(8, dtype=c.dtype)
    return (t[:, :, :, None, :] * eye[None, :, None, :, None]).reshape(SSM_JB, SSM_SB, LANES)


def _diag_of_b(blk):
    t = blk.reshape(SSM_JB, 8, GROUP, 8, STATE)
    d = jnp.sum(t * jnp.eye(8, dtype=blk.dtype)[None, :, None, :, None], axis=3)
    return d.transpose(0, 1, 3, 2).reshape(N_GROUPS, STATE, GROUP)


def _diag_of_c(blk):
    t = blk.reshape(SSM_JB, 8, STATE, 8, GROUP)
    d = jnp.sum(t * jnp.eye(8, dtype=blk.dtype)[None, :, None, :, None], axis=3)
    return d.transpose(0, 1, 3, 2).reshape(N_GROUPS, GROUP, STATE)


def _place_from_scan_order(v, buf, col0):
    seq, w = v.shape
    assert buf.dtype == _MXU

    def body(v_ref, b_ref, o_ref):
        tok = lax.broadcasted_iota(jnp.int32, (SSM_T, SSM_T), 0)
        row = lax.broadcasted_iota(jnp.int32, (SSM_T, SSM_T), 1)
        pick = (row == SUBLANES * (tok % SSM_L) + tok // SSM_L).astype(_MXU)
        o_ref[...] = jnp.dot(pick, v_ref[...].astype(_MXU), preferred_element_type=F32).astype(o_ref.dtype)

    return pl.pallas_call(
        body, name="place_from_scan_order", grid=(seq // SSM_T, w // CW),
        in_specs=[pl.BlockSpec((SSM_T, CW), lambda c, j: (c, j)), _ANY],
        out_specs=pl.BlockSpec((SSM_T, CW), lambda c, j: (c, col0 + j)),
        out_shape=jax.ShapeDtypeStruct(buf.shape, buf.dtype), input_output_aliases={1: 0},
        compiler_params=_params(("arbitrary", "arbitrary")),
    )(v, buf)


def _to_scan_order(v):
    seq, w = v.shape
    return v.reshape(seq // SSM_T, SUBLANES, SSM_L, w).transpose(0, 2, 1, 3).reshape(seq, w)


def _from_scan_order(v):
    seq, w = v.shape
    return v.reshape(seq // SSM_T, SSM_L, SUBLANES, w).transpose(0, 2, 1, 3).reshape(seq, w)


def _adamw_math(w, g, m, v):
    nm = ADAM_B1 * m + (1.0 - ADAM_B1) * g
    nv = ADAM_B2 * v + (1.0 - ADAM_B2) * jnp.square(g)
    m_hat = nm / (1.0 - ADAM_B1 ** ADAM_STEP)
    v_hat = nv / (1.0 - ADAM_B2 ** ADAM_STEP)
    return -ADAM_LR * (m_hat / (jnp.sqrt(v_hat) + ADAM_EPS) + ADAM_WD * w), nm, nv


def _adamw(w, g, m, v, *, name, tm, deps=()):
    rows, cols = w.shape
    nd = len(deps)

    def body(w_ref, g_ref, m_ref, v_ref, *rest):
        d_ref, nm_ref, nv_ref = rest[nd:]
        d_ref[...], nm_ref[...], nv_ref[...] = _adamw_math(w_ref[...], g_ref[...], m_ref[...], v_ref[...])

    spec = pl.BlockSpec((tm, cols), lambda i: (i, 0))
    shp = jax.ShapeDtypeStruct((rows, cols), F32)
    return pl.pallas_call(body, name=name, grid=(rows // tm,), in_specs=[spec] * 4 + [_ANY] * nd,
                          out_specs=[spec] * 3, out_shape=[shp] * 3,
                          compiler_params=_params(("arbitrary",)))(w, g, m, v, *deps)


def _place():
    x, y, c = lax.axis_index("x"), lax.axis_index("y"), lax.axis_index("c")
    chips = [(1 - x, y), (x, 1 - y), (1 - x, 1 - y)]
    return x, y, c, chips


def _remote(src, dst, send_sem, recv_sem, dev):
    return pltpu.make_async_remote_copy(src_ref=src, dst_ref=dst, send_sem=send_sem, recv_sem=recv_sem,
                                        device_id=dev, device_id_type=MESH)


def _place_shard(w, mine_arr, *, name, tm=256):
    rows, cols = w.shape

    def body(m_ref, w_ref, o_ref):
        o_ref[...] = w_ref[...].astype(o_ref.dtype)

    return pl.pallas_call(
        body, name=name,
        grid_spec=pltpu.PrefetchScalarGridSpec(
            num_scalar_prefetch=1, grid=(rows // tm,),
            in_specs=[pl.BlockSpec((tm, cols), lambda i, m: (i, 0))],
            out_specs=pl.BlockSpec((None, tm, cols), lambda i, m: (m[0], i, 0))),
        out_shape=jax.ShapeDtypeStruct((N_CHIPS, rows, cols), _WIRE),
        compiler_params=_params(("arbitrary",)),
    )(mine_arr, w)


_HBM = pl.BlockSpec(memory_space=pltpu.HBM)
_SEM = pl.BlockSpec(memory_space=pltpu.SEMAPHORE)
_EFFECT = pltpu.SideEffectType.DATAFLOW_SIDE_EFFECTING


def _copies_start(name, bufs, plan, count, after=()):
    nb, na = len(bufs), len(after)

    def body(*refs):
        send_sems, recv_sems, token = refs[nb + na], refs[nb + na + 1], refs[-1]
        copies = plan(refs[:nb])
        assert len(copies) == count
        for i, (src, dst, dev, _) in enumerate(copies):
            _remote(src, dst, send_sems.at[i], recv_sems.at[i], dev).start()
        token[...] = jnp.zeros_like(token)

    res = pl.pallas_call(
        body, name=name, in_specs=[_HBM] * nb + [_ANY] * na,
        out_specs=(_SEM, _SEM, *[_HBM] * nb, pl.BlockSpec(memory_space=pltpu.VMEM)),
        out_shape=(pltpu.SemaphoreType.DMA((count,)), pltpu.SemaphoreType.DMA((count,)),
                   *[pltpu.HBM(b.shape, b.dtype) for b in bufs], jax.ShapeDtypeStruct((SUBLANES, LANES), F32)),
        input_output_aliases={i: 2 + i for i in range(nb)},
        compiler_params=pltpu.CompilerParams(has_side_effects=_EFFECT),
    )(*[pltpu.with_memory_space_constraint(b, pltpu.HBM) for b in bufs], *after)
    return (res[0], res[1]), list(res[2:2 + nb]), res[-1]


def _copies_wait(name, bufs, sems, plan, after=(), which=None):
    nb, na = len(bufs), len(after)

    def body(*refs):
        send_sems, recv_sems = refs[nb], refs[nb + 1]
        for i, (src, _, dev, land) in enumerate(plan(refs[:nb])):
            if which is not None and i not in which:
                continue
            cp = _remote(src, land, send_sems.at[i], recv_sems.at[i], dev)
            cp.wait_send()
            cp.wait_recv()

    res = pl.pallas_call(
        body, name=name, in_specs=[_HBM] * nb + [_SEM, _SEM] + [_ANY] * na, out_specs=[_HBM] * nb,
        out_shape=[pltpu.HBM(b.shape, b.dtype) for b in bufs],
        input_output_aliases={i: i for i in range(nb)},
        compiler_params=pltpu.CompilerParams(has_side_effects=_EFFECT),
    )(*bufs, *sems, *after)
    return list(res)


def _plan_gather_ici(fulls, which=(0, 1, 2)):
    x, y, c, chips = _place()
    copies = []
    for f in fulls:
        half = pl.ds(c * (f.shape[1] // 2), f.shape[1] // 2)
        own = f.at[2 * x + y, half]
        for chip in [chips[k] for k in which]:
            copies.append((own, own, (*chip, c), f.at[2 * chip[0] + chip[1], half]))
    return copies


def _plan_gather_d2d(fulls, which=(0, 1, 2)):
    x, y, c, chips = _place()
    copies = []
    for f in fulls:
        r2 = f.shape[1] // 2
        for chip in [chips[k] for k in which]:
            blk = 2 * chip[0] + chip[1]
            landed = f.at[blk, pl.ds(c * r2, r2)]
            copies.append((landed, landed, (x, y, 1 - c), f.at[blk, pl.ds((1 - c) * r2, r2)]))
    return copies


def _plan_relay_direct(fulls):
    (f,) = fulls
    x, y, c, chips = _place()
    half = pl.ds(c * (f.shape[1] // 2), f.shape[1] // 2)
    own = f.at[2 * x + y, half]
    return [(own, own, (*chip, c), f.at[2 * chip[0] + chip[1], half]) for chip in chips[:2]]


def _plan_relay_forward(fulls, k):
    (f,) = fulls
    x, y, c, chips = _place()
    r2 = f.shape[1] // 2
    half, other = pl.ds(c * r2, r2), pl.ds((1 - c) * r2, r2)
    quarter = pl.ds(c * r2 + k * (r2 // 2), r2 // 2)
    blk, far = 2 * chips[k][0] + chips[k][1], 2 * chips[2][0] + chips[2][1]
    passed, landed = f.at[blk, quarter], f.at[blk, half]
    return [(passed, passed, (*chips[1 - k], c), f.at[far, quarter]), (landed, landed, (x, y, 1 - c), f.at[blk, other])]


def _plan_relay_last(fulls):
    (f,) = fulls
    x, y, c, chips = _place()
    r2 = f.shape[1] // 2
    far = 2 * chips[2][0] + chips[2][1]
    landed = f.at[far, pl.ds(c * r2, r2)]
    return [(landed, landed, (x, y, 1 - c), f.at[far, pl.ds((1 - c) * r2, r2)])]


def _plan_swap_halves(refs):
    x, y, c, _ = _place()
    n = len(refs) // 2
    copies = []
    for g, land in zip(refs[:n], refs[n:]):
        r2 = g.shape[1] // 2
        copies.append((g.at[:, pl.ds((1 - c) * r2, r2), :], land, (x, y, 1 - c), land))
    return copies


def _plan_scatter_chips(refs):
    x, y, c, chips = _place()
    n = len(refs) // 2
    copies = []
    for h, land in zip(refs[:n], refs[n:]):
        for k, chip in enumerate(chips):
            copies.append((h.at[2 * chip[0] + chip[1]], land.at[k], (*chip, c), land.at[k]))
    return copies


def _plan_join_halves(totals):
    x, y, c, _ = _place()
    copies = []
    for t in totals:
        r2 = t.shape[0] // 2
        mine = t.at[pl.ds(c * r2, r2)]
        copies.append((mine, mine, (x, y, 1 - c), t.at[pl.ds((1 - c) * r2, r2)]))
    return copies


def _add_sibling_half(g, got, c_arr, *, name, tm):
    _, rows, cols = g.shape
    r2 = rows // 2
    nb = r2 // tm

    def body(c_ref, g_ref, r_ref, o_ref):
        o_ref[...] = (g_ref[...].astype(F32) + r_ref[...].astype(F32)).astype(o_ref.dtype)

    return pl.pallas_call(
        body, name=name,
        grid_spec=pltpu.PrefetchScalarGridSpec(
            num_scalar_prefetch=1, grid=(N_CHIPS, nb),
            in_specs=[pl.BlockSpec((None, tm, cols), lambda b, i, c: (b, c[0] * nb + i, 0)),
                      pl.BlockSpec((None, tm, cols), lambda b, i, c: (b, i, 0))],
            out_specs=pl.BlockSpec((None, tm, cols), lambda b, i, c: (b, i, 0))),
        out_shape=jax.ShapeDtypeStruct((N_CHIPS, r2, cols), _WIRE),
        compiler_params=_params(("arbitrary", "arbitrary")),
    )(c_arr, g, got)


def _add_chips(h, got, place_arr, *, name, tm):
    _, r2, cols = h.shape
    nb = r2 // tm

    def body(p_ref, h_ref, r_ref, o_ref):
        o_ref[...] = ((h_ref[...].astype(F32) + r_ref[0].astype(F32)) + r_ref[1].astype(F32)) + r_ref[2].astype(F32)

    return pl.pallas_call(
        body, name=name,
        grid_spec=pltpu.PrefetchScalarGridSpec(
            num_scalar_prefetch=1, grid=(nb,),
            in_specs=[pl.BlockSpec((None, tm, cols), lambda i, p: (p[0], i, 0)),
                      pl.BlockSpec((3, tm, cols), lambda i, p: (0, i, 0))],
            out_specs=pl.BlockSpec((tm, cols), lambda i, p: (p[1] * nb + i, 0))),
        out_shape=jax.ShapeDtypeStruct((2 * r2, cols), F32),
        compiler_params=_params(("arbitrary",)),
    )(place_arr, h, got)


class _ReduceScatter:
    def __init__(self, tag, names, grads):
        self.tag, self.names, self.n = tag, names, len(names)
        core = lax.axis_index("c").astype(jnp.int32)
        chip = (2 * lax.axis_index("x") + lax.axis_index("y")).astype(jnp.int32)
        self.c_arr, self.place_arr = core.reshape(1), jnp.stack([chip, core])
        self.bufs = list(grads)

    def _start(self, step, bufs, plan, count, after):
        self.plan = plan
        self.step = f"grad_{step}_{self.tag}"
        self.sems, self.bufs, token = _copies_start(self.step + "_start", bufs, plan, count, after)
        return [token]

    def _wait(self, after):
        self.bufs = _copies_wait(self.step + "_wait", self.bufs, self.sems, self.plan, after)
        return self.bufs

    def start_swap(self, after=()):
        lands = [lax.empty((N_CHIPS, g.shape[1] // 2, g.shape[2]), g.dtype) for g in self.bufs]
        return self._start("swap", self.bufs + lands, _plan_swap_halves, self.n, after)

    def start_scatter(self, after):
        bufs = self._wait(after)
        pair = [_add_sibling_half(g, r, self.c_arr, name=f"grad_add_sibling_{nm}", tm=min(256, g.shape[1] // 2))
                for nm, g, r in zip(self.names, bufs[:self.n], bufs[self.n:])]
        lands = [lax.empty((3,) + h.shape[1:], h.dtype) for h in pair]
        return self._start("scatter", pair + lands, _plan_scatter_chips, 3 * self.n, ())

    def start_join(self, after):
        bufs = self._wait(after)
        total = [_add_chips(h, r, self.place_arr, name=f"grad_add_chips_{nm}", tm=min(256, h.shape[1]))
                 for nm, h, r in zip(self.names, bufs[:self.n], bufs[self.n:])]
        return self._start("join", total, _plan_join_halves, self.n, ())

    def finish(self, after):
        return dict(zip(self.names, self._wait(after)))


def _all_gather_small(v):
    m_per, n = v.shape

    def body(x_ref, out_ref, send_sems, recv_sems, local_sem):
        x, y, c, chips = _place()
        me, sibling = (x, y, c), (x, y, 1 - c)

        def rows(px, py, pc):
            return out_ref.at[4 * px + 2 * py + pc]

        def copy(k, block, to, src=None):
            return _remote(rows(*block) if src is None else src, rows(*block), send_sems.at[k], recv_sems.at[k], to)

        mine = pltpu.make_async_copy(x_ref, rows(*me), local_sem)
        mine.start()
        first = [copy(0, me, sibling, src=x_ref)]
        first += [copy(1 + j, me, (*chip, c), src=x_ref) for j, chip in enumerate(chips)]
        for cp in first:
            cp.start()
        passed = [copy(4 + j, (*chip, c), sibling) for j, chip in enumerate(chips)]
        for j, chip in enumerate(chips):
            copy(1 + j, (*chip, c), me).wait_recv()
            passed[j].start()
        copy(0, sibling, me).wait_recv()
        for j, chip in enumerate(chips):
            copy(4 + j, (*chip, 1 - c), me).wait_recv()
        for cp in first + passed:
            cp.wait_send()
        mine.wait()

    return pl.pallas_call(
        body, name="gather_small_grads",
        out_shape=jax.ShapeDtypeStruct((8, m_per, n), v.dtype),
        in_specs=[pl.BlockSpec(memory_space=pltpu.VMEM)], out_specs=pl.BlockSpec(memory_space=pltpu.VMEM),
        scratch_shapes=[pltpu.SemaphoreType.DMA((7,)), pltpu.SemaphoreType.DMA((7,)), pltpu.SemaphoreType.DMA],
        compiler_params=pltpu.CompilerParams(vmem_limit_bytes=VMEM_LIMIT),
    )(v)


def _sum8(v, *, name):
    _, m, n = v.shape

    def body(v_ref, o_ref):
        acc = v_ref[0]
        for d in range(1, 8):
            acc = acc + v_ref[d]
        o_ref[...] = acc

    return pl.pallas_call(body, name=name, out_shape=jax.ShapeDtypeStruct((m, n), F32),
                          compiler_params=pltpu.CompilerParams(vmem_limit_bytes=VMEM_LIMIT))(v)


def _local_step(x, target, norm_w, q_norm_w, k_norm_w, sinks, a_re, a_im, log_dt, b_re, b_im, c_re, c_im, d_skip,
                b_glu, io):
    seq = x.shape[0]
    qw2 = jnp.tile(q_norm_w.reshape(1, HEAD_DIM), (1, HEADS_PER_TILE))
    kw2 = jnp.tile(k_norm_w.reshape(1, HEAD_DIM), (1, HEADS_PER_TILE))
    nw, bg = norm_w.reshape(1, D_MODEL), b_glu.reshape(1, D_MODEL)
    dsk = d_skip.reshape(1, SSM_W)

    h, rstd = _rms_fwd(x, nw, deps=io.begin())
    proj, w_in4 = io.projection(h)
    attn, lse, ya_in = _attn2_fwd(proj, qw2, kw2, sinks, deps=io.after_proj(proj))
    w_ap4 = io.weight("w_attn_proj", ya_in)
    w_glu4, w_sp4, w_out = io.weight("w_glu", ya_in), io.weight("w_ssm_proj", ya_in), io.weight("w_out", ya_in)
    y_a = _mm(ya_in, w_ap4, mode="nn", name="mm_attn_proj", tm=2048, tn=512, tk=ATTN_W, b_blocked=True,
              rows_outer=True)

    flat_a = (a_re.reshape(1, N_STATES), a_im.reshape(1, N_STATES), jnp.repeat(log_dt, STATE).reshape(1, N_STATES))
    coef = _ssm_params_fwd(*flat_a)
    bre_blk, bim_blk = _block_diag_b(b_re).astype(_MXU), _block_diag_b(b_im).astype(_MXU)
    cre_blk, cim_blk = _block_diag_c(c_re).astype(_MXU), _block_diag_c(c_im).astype(_MXU)
    u_scan = _to_scan_order(proj[:, OFF_U * CW:OFF_U * CW + SSM_W])
    y_scan, yg, s_re, s_im, i_re, i_im = _ssm_fwd(u_scan, bre_blk, bim_blk, cre_blk, cim_blk, dsk, coef)
    glu = _mm(yg, w_glu4, mode="nn", name="mm_glu", tm=2048, tn=512, tk=SSM_W, b_blocked=True, rows_outer=True)

    def gate_s(ga, gb, ba, bb, z):
        return ((ga + ba) * _sigmoid(gb + bb) * (z * _sigmoid(z)),)

    (ys_in,) = _ew(gate_s, [(glu, "mat", 0), (glu, "mat", 2), (bg, "row", 0), (bg, "row", 2), (proj, "mat", OFF_Z)],
                   [(SSM_W, _MXU)], rows=seq, ncol=2, name="ew_ssm_gate")
    y_s = _mm(ys_in, w_sp4, mode="nn", name="mm_ssm_proj", tm=2048, tn=512, tk=SSM_W, b_blocked=True,
              rows_outer=True)

    def merge(ga, gs, ya, ys):
        return (_sigmoid(ga) * ya + _sigmoid(gs) * ys,)

    (merged,) = _ew(merge, [(proj, "mat", OFF_GA), (proj, "mat", OFF_GS), (y_a, "mat", 0), (y_s, "mat", 0)],
                    [(D_MODEL, _MXU)], rows=seq, ncol=4, name="ew_merge")
    dout, dout_b, sq = _mm_out_loss(merged, w_out, x, target)
    loss = 0.5 * jnp.sum(sq) / D_MODEL

    d_ya, d_ys, d_proj = _mm_merge_bwd(dout_b, w_out, proj, y_a, y_s)
    g_w_out = _mm(merged, dout_b, mode="tn", name="mm_g_w_out", tm=1024, tn=D_MODEL, tk=1024, out_dtype=_WIRE)

    d_ya_in = _mm(d_ya, w_ap4, mode="nt", name="mm_d_attn_gate", tm=2048, tn=ATTN_W, tk=512, b_blocked=True)
    g_w_ap = _mm(ya_in, d_ya, mode="tn", name="mm_g_w_attn_proj", tm=ATTN_W, tn=D_MODEL, tk=2048, out_dtype=_WIRE,
                 out_blocked=True)

    d_ys_in = _mm(d_ys, w_sp4, mode="nt", name="mm_d_ssm_gate", tm=2048, tn=SSM_W, tk=512, b_blocked=True)
    g_w_sp = _mm(ys_in, d_ys, mode="tn", name="mm_g_w_ssm_proj", tm=SSM_W, tn=D_MODEL, tk=2048, out_dtype=_WIRE,
                 out_blocked=True)

    def gate_s_bwd(dv, ga, gb, ba, bb, z):
        a, sb = ga + ba, _sigmoid(gb + bb)
        f, df = _silu_and_grad(z)
        dga = dv * sb * f
        dgb = dv * a * f * sb * (1.0 - sb)
        return dga, dgb, dv * a * sb * df, _colsum(dga), _colsum(dgb)

    d_glu_a, d_glu_b, d_proj, g_bga, g_bgb = _ew(
        gate_s_bwd, [(d_ys_in, "mat", 0), (glu, "mat", 0), (glu, "mat", 2), (bg, "row", 0), (bg, "row", 2),
                     (proj, "mat", OFF_Z)],
        [(SSM_W, _MXU)] * 3, rows=seq, ncol=2, n_acc=2, name="ew_ssm_gate_bwd", into=(2, d_proj, OFF_Z))
    d_glu = jnp.concatenate([d_glu_a, d_glu_b], axis=1)
    d_yg = _mm(d_glu, w_glu4, mode="nt", name="mm_d_gelu", tm=2048, tn=SSM_W, tk=512, b_blocked=True)
    g_w_glu = _mm(yg, d_glu, mode="tn", name="mm_g_w_glu", tm=SSM_W, tn=D_MODEL, tk=2048, out_dtype=_WIRE, out_blocked=True)
    dep = io.later_grads(dict(w_attn_proj=g_w_ap, w_glu=g_w_glu, w_ssm_proj=g_w_sp,
                              w_out=g_w_out.reshape(N_CHIPS, D_MODEL // N_CHIPS, D_MODEL)))

    d_proj, g_qw2, g_kw2, g_sk = _attn2_bwd(proj, qw2, kw2, sinks, lse, attn, d_ya_in, d_proj, deps=dep)
    dep = io.before_scan_backward([d_proj])
    (d_proj, g_bre, g_bim, g_cre, g_cim, g_dsk, g_abr, g_abi, g_cfr, g_cfi) = _ssm_bwd(
        _to_scan_order(d_yg), y_scan, u_scan, s_re, s_im, i_re, i_im, bre_blk, bim_blk, cre_blk, cim_blk, dsk, coef,
        d_proj, deps=dep)
    g_are, g_aim, g_ldt = _ssm_params_bwd(*flat_a, g_abr, g_abi, g_cfr, g_cfi)
    g_are, g_aim = g_are.reshape(N_GROUPS, STATE), g_aim.reshape(N_GROUPS, STATE)
    g_ldt = g_ldt.reshape(N_GROUPS, STATE).sum(axis=1)
    dep = io.before_input_projection_grad([d_proj]) + io.small_grads(dict(
        q_norm_w=g_qw2[0, :HEAD_DIM] + g_qw2[0, HEAD_DIM:], k_norm_w=g_kw2[0, :HEAD_DIM] + g_kw2[0, HEAD_DIM:],
        sinks=g_sk.reshape(N_Q_HEADS), A_re=g_are, A_im=g_aim, log_dt=g_ldt,
        B_re=_diag_of_b(g_bre), B_im=_diag_of_b(g_bim), C_re=_diag_of_c(g_cre), C_im=_diag_of_c(g_cim),
        D_skip=g_dsk.reshape(N_GROUPS, GROUP), b_glu=jnp.concatenate([g_bga, g_bgb], axis=1).reshape(D_MODEL)))
    g_w_in = _mm(h, d_proj, mode="tn", name="mm_g_w_in", tm=1024, tn=IN_W // 4, tk=1024, out_dtype=_WIRE,
                 out_blocked=True, deps=dep)
    dep = io.input_projection_grad(g_w_in)
    d_h = _mm(d_proj, w_in4, mode="nt", name="mm_d_h", tm=512, tn=D_MODEL, tk=IN_W // 4, b_blocked=True, deps=dep)
    grad_x, g_nw = _rms_bwd(d_h, x, rstd, nw, dout)
    return loss, grad_x, g_nw.reshape(D_MODEL)


_SMALL = ["norm_w", "q_norm_w", "k_norm_w", "sinks", "A_re", "A_im", "log_dt", "B_re", "B_im", "C_re", "C_im",
          "D_skip", "b_glu"]
_BIG = ["w_in", "w_attn_proj", "w_glu", "w_ssm_proj", "w_out"]
_LATER = _BIG[1:]
_RELATIONS = ("flip_x", "flip_y", "flip_xy")
_ORDER = ["norm_w", "w_in", "q_norm_w", "k_norm_w", "sinks", "w_attn_proj", "A_re", "A_im", "log_dt", "B_re", "B_im",
          "C_re", "C_im", "D_skip", "w_glu", "b_glu", "w_ssm_proj", "w_out"]
_PACK_W = 1024


def _packed_rows(size):
    unit = SUBLANES * _PACK_W
    return -(-size // unit) * SUBLANES


def _pack_small(d, names):
    parts = []
    for n in names:
        flat = d[n].reshape(-1).astype(F32)
        rows = _packed_rows(flat.shape[0])
        parts.append(jnp.pad(flat, (0, rows * _PACK_W - flat.shape[0])).reshape(rows, _PACK_W))
    return jnp.concatenate(parts, axis=0)


def _unpack_small(packed, like, names):
    out, pos = {}, 0
    for n in names:
        rows = _packed_rows(like[n].size)
        out[n] = packed[pos:pos + rows].reshape(-1)[:like[n].size].reshape(like[n].shape)
        pos += rows
    return out


def _place_block(v, index_arr, *, name):
    rows, cols = v.shape

    def body(i_ref, v_ref, o_ref):
        o_ref[...] = v_ref[...]

    return pl.pallas_call(
        body, name=name,
        grid_spec=pltpu.PrefetchScalarGridSpec(
            num_scalar_prefetch=1, grid=(1,),
            in_specs=[pl.BlockSpec((rows, cols), lambda i, d: (0, 0))],
            out_specs=pl.BlockSpec((None, rows, cols), lambda i, d: (d[0], 0, 0))),
        out_shape=jax.ShapeDtypeStruct((8, rows, cols), v.dtype),
        compiler_params=_params(("arbitrary",)),
    )(index_arr, v)


def _plan_all_to_all(refs):
    (land,) = refs
    x, y, c, _ = _place()
    own = land.at[4 * x + 2 * y + c]
    copies = []
    for fx, fy, fc in [(0, 0, 1), (0, 1, 0), (0, 1, 1), (1, 0, 0), (1, 0, 1), (1, 1, 0), (1, 1, 1)]:
        px, py, pc = (1 - x) if fx else x, (1 - y) if fy else y, (1 - c) if fc else c
        copies.append((own, own, (px, py, pc), land.at[4 * px + 2 * py + pc]))
    return copies


def _as2d(a):
    return a.reshape(1, -1) if a.ndim == 1 else a


def _adamw_whole(w, g, m, v, *, name):
    shape = w.shape

    def body(w_ref, g_ref, m_ref, v_ref, d_ref, nm_ref, nv_ref):
        d_ref[...], nm_ref[...], nv_ref[...] = _adamw_math(w_ref[...], g_ref[...], m_ref[...], v_ref[...])

    outs = pl.pallas_call(body, name=name, out_shape=[jax.ShapeDtypeStruct(w.shape, F32)] * 3)(w, g, m, v)
    return [o.reshape(shape) for o in outs]


class _Exchanges:
    def __init__(self, w, m, v):
        self.w, self.m, self.v = w, m, v
        self.grads, self.delta, self.new_m, self.new_v = {}, {}, {}, {}

    def _adamw(self, names, deps):
        for n in names:
            self.delta[n], self.new_m[n], self.new_v[n] = _adamw(
                self.w[n], self.grads[n], self.m[n], self.v[n], name=f"adamw_{n}", tm=128, deps=deps)

    def begin(self):
        chip = (2 * lax.axis_index("x") + lax.axis_index("y")).astype(jnp.int32).reshape(1)
        full = {n: _place_shard(self.w[n], chip, name=f"place_{n}") for n in _BIG}
        self.later_full = [full[n] for n in _LATER]
        self.w_in_sems, self.w_in_buf, token = _copies_start("gather_w_in_direct_start", [full["w_in"]],
                                                             _plan_relay_direct, 2)
        return [token]

    def projection(self, h):
        x, y = lax.axis_index("x"), lax.axis_index("y")
        blks = [jnp.asarray(b, jnp.int32).reshape(1)
                for b in (2 * x + y, 2 * (1 - x) + y, 2 * x + (1 - y), 2 * (1 - x) + (1 - y))]
        bufs = self.w_in_buf
        proj = _mm_chip_block(h, bufs[0], blks[0], None, name="mm_proj_own")
        relay, token = [], proj
        for k, tag in enumerate(_RELATIONS[:2]):
            bufs = _copies_wait(f"gather_w_in_direct_{tag}_wait", bufs, self.w_in_sems, _plan_relay_direct, [token],
                                which=(k,))
            plan = functools.partial(_plan_relay_forward, k=k)
            sems, bufs, token = _copies_start(f"gather_w_in_relay_{tag}_start", bufs, plan, 2)
            relay.append((sems, plan))
        self.rest = _copies_start("gather_ici_rest_start", self.later_full, _plan_gather_ici, 3 * len(_LATER),
                                  after=[token])
        token = self.rest[2]
        for k, tag in enumerate(_RELATIONS[:2]):
            bufs = _copies_wait(f"gather_w_in_hand_{tag}_wait", bufs, relay[k][0], relay[k][1], [token], which=(1,))
            token = proj = _mm_chip_block(h, bufs[0], blks[1 + k], proj, name=f"mm_proj_{tag}")
        for k, tag in enumerate(_RELATIONS[:2]):
            bufs = _copies_wait(f"gather_w_in_relay_{tag}_wait", bufs, relay[k][0], relay[k][1], [token], which=(0,))
        sems, bufs, token = _copies_start("gather_w_in_last_start", bufs, _plan_relay_last, 1)
        bufs = _copies_wait("gather_w_in_last_wait", bufs, sems, _plan_relay_last, [token])
        proj = _mm_chip_block(h, bufs[0], blks[3], proj, name="mm_proj_flip_xy")
        return proj, bufs[0]

    def weight(self, name, after):
        if self.rest is not None:
            sems, bufs = self.rest
            later = dict(zip(_LATER, _copies_wait("gather_d2d_rest_wait", bufs, sems, _plan_gather_d2d, [after])))
            later["w_out"] = later["w_out"].reshape(D_MODEL, D_MODEL)
            self.later, self.rest = later, None
        return self.later[name]

    def after_proj(self, proj):
        sems, bufs, _ = self.rest
        bufs = _copies_wait("gather_ici_rest_wait", bufs, sems, _plan_gather_ici, [proj])
        sems, bufs, token = _copies_start("gather_d2d_rest_start", bufs, _plan_gather_d2d, 3 * len(_LATER))
        self.rest = (sems, bufs)
        return [token]

    def later_grads(self, grads):
        self.rs_later = _ReduceScatter("later", _LATER, [grads[n] for n in _LATER])
        return self.rs_later.start_swap()

    def before_scan_backward(self, after):
        return self.rs_later.start_scatter(after)

    def before_input_projection_grad(self, after):
        return self.rs_later.start_join(after)

    def input_projection_grad(self, g_w_in):
        self.grads.update(self.rs_later.finish([g_w_in]))
        self.rs_in = _ReduceScatter("w_in", ["w_in"], [g_w_in])
        self._adamw(_LATER, self.rs_in.start_swap())
        return self.rs_in.start_scatter([self.delta[n] for n in _LATER])

    def _adamw_small(self, names):
        for n in names:
            self.delta[n], self.new_m[n], self.new_v[n] = _adamw_whole(
                self.w[n], self.grads[n], self.m[n], self.v[n], name=f"adamw_{n}")

    def small_grads(self, grads):
        me = (4 * lax.axis_index("x") + 2 * lax.axis_index("y") + lax.axis_index("c")).astype(jnp.int32).reshape(1)
        land = _place_block(_pack_small(grads, _SMALL[1:]), me, name="place_small_grads")
        self.small = _copies_start("gather_small_start", [land], _plan_all_to_all, 7)
        return [self.small[2]]

    def finish(self, g_norm_w, loss, after):
        join = self.rs_in.start_join(after)
        sems, bufs, _ = self.small
        (land,) = _copies_wait("gather_small_wait", bufs, sems, _plan_all_to_all, join)
        self.grads.update(_unpack_small(_sum8(land, name="sum_small_grads"), self.w, _SMALL[1:]))
        self._adamw_small(_SMALL[1:])
        rows = _packed_rows(g_norm_w.size)
        late = jnp.concatenate([_pack_small(dict(norm_w=g_norm_w), _SMALL[:1]),
                                jnp.pad(loss.reshape(1, 1), ((0, SUBLANES - 1), (0, _PACK_W - 1)))], axis=0)
        late = _sum8(_all_gather_small(late), name="sum_norm_w_grad_and_loss")
        self.grads.update(_unpack_small(late[:rows], self.w, _SMALL[:1]))
        self._adamw_small(_SMALL[:1])
        self.grads.update(self.rs_in.finish([self.delta[_SMALL[0]]]))
        self._adamw(["w_in"], ())
        return late[rows, 0]


def kernel(x, norm_w, w_in, q_norm_w, k_norm_w, sinks, w_attn_proj, A_re, A_im, log_dt, B_re, B_im, C_re, C_im, D_skip, w_glu, b_glu, w_ssm_proj, w_out, loss_target, m_norm_w, m_w_in, m_q_norm_w, m_k_norm_w, m_sinks, m_w_attn_proj, m_A_re, m_A_im, m_log_dt, m_B_re, m_B_im, m_C_re, m_C_im, m_D_skip, m_w_glu, m_b_glu, m_w_ssm_proj, m_w_out, v_norm_w, v_w_in, v_q_norm_w, v_k_norm_w, v_sinks, v_w_attn_proj, v_A_re, v_A_im, v_log_dt, v_B_re, v_B_im, v_C_re, v_C_im, v_D_skip, v_w_glu, v_b_glu, v_w_ssm_proj, v_w_out):
    w = dict(norm_w=norm_w, w_in=w_in, q_norm_w=q_norm_w, k_norm_w=k_norm_w, sinks=sinks, w_attn_proj=w_attn_proj,
             A_re=A_re, A_im=A_im, log_dt=log_dt, B_re=B_re, B_im=B_im, C_re=C_re, C_im=C_im, D_skip=D_skip,
             w_glu=w_glu, b_glu=b_glu, w_ssm_proj=w_ssm_proj, w_out=w_out)
    m = dict(norm_w=m_norm_w, w_in=m_w_in, q_norm_w=m_q_norm_w, k_norm_w=m_k_norm_w, sinks=m_sinks,
             w_attn_proj=m_w_attn_proj, A_re=m_A_re, A_im=m_A_im, log_dt=m_log_dt, B_re=m_B_re, B_im=m_B_im,
             C_re=m_C_re, C_im=m_C_im, D_skip=m_D_skip, w_glu=m_w_glu, b_glu=m_b_glu, w_ssm_proj=m_w_ssm_proj,
             w_out=m_w_out)
    v = dict(norm_w=v_norm_w, w_in=v_w_in, q_norm_w=v_q_norm_w, k_norm_w=v_k_norm_w, sinks=v_sinks,
             w_attn_proj=v_w_attn_proj, A_re=v_A_re, A_im=v_A_im, log_dt=v_log_dt, B_re=v_B_re, B_im=v_B_im,
             C_re=v_C_re, C_im=v_C_im, D_skip=v_D_skip, w_glu=v_w_glu, b_glu=v_b_glu, w_ssm_proj=v_w_ssm_proj,
             w_out=v_w_out)

    io = _Exchanges(w, m, v)
    loss, grad_x, g_norm_w = _local_step(x[0], loss_target[0], norm_w, q_norm_w, k_norm_w, sinks, A_re, A_im, log_dt,
                                         B_re, B_im, C_re, C_im, D_skip, b_glu, io)
    loss = io.finish(g_norm_w, loss, [grad_x])
    grads, delta, new_m, new_v = io.grads, io.delta, io.new_m, io.new_v

    return (loss, grad_x[None], *[grads[n] for n in _ORDER], *[delta[n] for n in _ORDER],
            *[new_m[n] for n in _ORDER], *[new_v[n] for n in _ORDER])
```

```python
import functools
import math

import jax
import jax.numpy as jnp
from jax import lax
from jax.experimental import pallas as pl
from jax.experimental.pallas import tpu as pltpu

F32 = jnp.float32
_MXU = jnp.bfloat16
_WIRE = jnp.bfloat16

LANES = 128
SUBLANES = 8
VMEM_LIMIT = 56 * 1024 * 1024

D_MODEL = 2048
HEAD_DIM = 64
N_Q_HEADS = 16
N_KV_HEADS = 4
Q_PER_KV = 4
ATTN_W = 1024
KV_W = 256
WINDOW = 128
SSM_W = 1024
GROUP = 16
N_GROUPS = 64
STATE = 64
N_STATES = N_GROUPS * STATE
IN_W = 8704
NORM_EPS = 1e-6
N_CHIPS = 4
CW = 512
OFF_AGATE, OFF_U, OFF_Z, OFF_GA, OFF_GS = 3, 5, 7, 9, 13

SSM_T = 256
SSM_L = SSM_T // SUBLANES
SSM_JB = 8
SSM_SB = N_STATES // SSM_JB

ADAM_LR, ADAM_B1, ADAM_B2, ADAM_EPS, ADAM_WD, ADAM_STEP = 0.001, 0.9, 0.999, 1e-08, 0.01, 10

MESH = pl.DeviceIdType.MESH
_ANY = pl.BlockSpec(memory_space=pl.ANY)


def _params(sem=None):
    return pltpu.CompilerParams(dimension_semantics=sem, vmem_limit_bytes=VMEM_LIMIT)


def _mm(a, b, *, mode, name, tm, tn, tk, out_dtype=F32, b_blocked=False, out_blocked=False, rows_outer=False,
        deps=()):
    nd = len(deps)
    if mode == "tn":
        K, M = a.shape
    else:
        M, K = a.shape
    if mode == "nn":
        N = b.shape[0] * b.shape[2] if b_blocked else b.shape[1]
    elif mode == "nt":
        N = b.shape[1] if b_blocked else b.shape[0]
    else:
        N = b.shape[1]
    tm, tn, tk = min(tm, M), min(tn, N), min(tk, K)
    nj, ni, nk = N // tn, M // tm, K // tk
    assert nj * tn == N and ni * tm == M and nk * tk == K, (name, M, N, K)
    dims = {"nn": (((1,), (0,)), ((), ())), "nt": (((1,), (1,)), ((), ())), "tn": (((0,), (0,)), ((), ()))}[mode]

    if mode == "tn":
        a_spec = pl.BlockSpec((tk, tm), lambda j, i, k: (k, i))
    else:
        a_spec = pl.BlockSpec((tm, tk), lambda j, i, k: (i, k))
    if mode == "nn":
        if b_blocked:
            assert b.shape[0] == nj and b.shape[2] == tn
            b_spec = pl.BlockSpec((None, tk, tn), lambda j, i, k: (j, k, 0))
        else:
            b_spec = pl.BlockSpec((tk, tn), lambda j, i, k: (k, j))
    elif mode == "nt":
        if b_blocked:
            assert b.shape[0] == nk and b.shape[2] == tk
            b_spec = pl.BlockSpec((None, tn, tk), lambda j, i, k: (k, j, 0))
        else:
            b_spec = pl.BlockSpec((tn, tk), lambda j, i, k: (j, k))
    else:
        b_spec = pl.BlockSpec((tk, tn), lambda j, i, k: (k, j))
    whole_out = out_blocked and nj == 1
    if whole_out:
        assert ni == 1
        o_spec = pl.BlockSpec((N_CHIPS, tm, tn // N_CHIPS), lambda j, i, k: (0, 0, 0))
        o_shape = jax.ShapeDtypeStruct((N_CHIPS, M, tn // N_CHIPS), out_dtype)
    elif out_blocked:
        assert nj == N_CHIPS
        o_spec = pl.BlockSpec((None, tm, tn), lambda j, i, k: (j, i, 0))
        o_shape = jax.ShapeDtypeStruct((nj, M, tn), out_dtype)
    else:
        o_spec = pl.BlockSpec((tm, tn), lambda j, i, k: (i, j))
        o_shape = jax.ShapeDtypeStruct((M, N), out_dtype)
    use_acc = nk > 1 and (out_dtype != F32 or whole_out)

    def body(a_ref, b_ref, *rest):
        o_ref, scratch = rest[nd], rest[nd + 1:]

        def product():
            return lax.dot_general(a_ref[...].astype(_MXU), b_ref[...].astype(_MXU), dims, preferred_element_type=F32)

        def write(result):
            if whole_out:
                w = tn // N_CHIPS
                for c in range(N_CHIPS):
                    o_ref[c] = result[:, c * w:(c + 1) * w].astype(o_ref.dtype)
            else:
                o_ref[...] = result.astype(o_ref.dtype)

        if nk == 1:
            write(product())
            return
        k = pl.program_id(2)
        acc = scratch[0] if use_acc else o_ref

        @pl.when(k == 0)
        def _():
            acc[...] = jnp.zeros_like(acc)

        acc[...] += product()

        if use_acc:
            @pl.when(k == nk - 1)
            def _():
                write(acc[...])

    specs = [a_spec, b_spec, o_spec]
    grid = (nj, ni, nk)
    if rows_outer:
        specs = [pl.BlockSpec(s.block_shape, lambda i, j, k, f=s.index_map: f(j, i, k)) for s in specs]
        grid = (ni, nj, nk)
    return pl.pallas_call(
        body, name=name, grid=grid, in_specs=specs[:2] + [_ANY] * nd, out_specs=specs[2],
        out_shape=o_shape, scratch_shapes=[pltpu.VMEM((tm, tn), F32)] if use_acc else [],
        compiler_params=_params(("parallel", "parallel", "arbitrary")),
    )(a, b, *deps)


def _mm_chip_block(a, b4, blk, prev, *, name, tm=512, deps=()):
    M, K = a.shape
    nchip, _, C = b4.shape
    tm = min(tm, M)
    extra = ([] if prev is None else [prev]) + list(deps)

    def body(blk_ref, a_ref, b_ref, *rest):
        rest[-1][...] = jnp.dot(a_ref[...].astype(_MXU), b_ref[...].astype(_MXU), preferred_element_type=F32)

    return pl.pallas_call(
        body, name=name,
        grid_spec=pltpu.PrefetchScalarGridSpec(
            num_scalar_prefetch=1, grid=(M // tm,),
            in_specs=[pl.BlockSpec((tm, K), lambda i, c: (i, 0)), pl.BlockSpec((None, K, C), lambda i, c: (c[0], 0, 0))]
            + [_ANY] * len(extra),
            out_specs=pl.BlockSpec((tm, C), lambda i, c: (i, c[0]))),
        out_shape=jax.ShapeDtypeStruct((M, nchip * C), F32),
        input_output_aliases={} if prev is None else {3: 0},
        compiler_params=_params(("arbitrary",)),
    )(blk, a, b4, *extra)


def _mm_out_loss(merged, w_out, x, target, *, tm=256):
    rows, d = x.shape

    def body(m_ref, w_ref, x_ref, t_ref, d_ref, db_ref, sq_ref):
        mo = jnp.dot(m_ref[...].astype(_MXU), w_ref[...].astype(_MXU), preferred_element_type=F32)
        err = (x_ref[...] + mo) - t_ref[...]
        dout = err * (1.0 / d)
        d_ref[...] = dout
        db_ref[...] = dout.astype(db_ref.dtype)
        part = _colsum(err * err)
        i = pl.program_id(0)

        @pl.when(i == 0)
        def _():
            sq_ref[...] = part

        @pl.when(i > 0)
        def _():
            sq_ref[...] += part

    tile = pl.BlockSpec((tm, d), lambda i: (i, 0))
    return pl.pallas_call(
        body, name="mm_out_loss", grid=(rows // tm,),
        in_specs=[tile, pl.BlockSpec((d, d), lambda i: (0, 0)), tile, tile],
        out_specs=[tile, tile, pl.BlockSpec((1, d), lambda i: (0, 0))],
        out_shape=[jax.ShapeDtypeStruct((rows, d), F32), jax.ShapeDtypeStruct((rows, d), _MXU),
                   jax.ShapeDtypeStruct((1, d), F32)],
        compiler_params=_params(("arbitrary",)),
    )(merged, w_out, x, target)


def _mm_merge_bwd(dout_b, w_out, proj, y_a, y_s, *, tm=256):
    rows, d = y_a.shape
    ncol = d // CW

    def body(do_ref, w_ref, *refs):
        ga_refs, gs_refs = refs[:ncol], refs[ncol:2 * ncol]
        ya_ref, ys_ref, dya_ref, dys_ref, dg_ref = refs[2 * ncol:]
        dm = lax.dot_general(do_ref[...].astype(_MXU), w_ref[...].astype(_MXU), _NT, preferred_element_type=F32)
        for j in range(ncol):
            cols = slice(j * CW, (j + 1) * CW)
            dmj = dm[:, cols]
            sa, ss = _sigmoid(ga_refs[j][...]), _sigmoid(gs_refs[j][...])
            dya_ref[:, cols] = (sa * dmj).astype(dya_ref.dtype)
            dys_ref[:, cols] = (ss * dmj).astype(dys_ref.dtype)
            dg_ref[:, cols] = (dmj * ya_ref[:, cols] * sa * (1.0 - sa)).astype(dg_ref.dtype)
            dg_ref[:, d + j * CW:d + (j + 1) * CW] = (dmj * ys_ref[:, cols] * ss * (1.0 - ss)).astype(dg_ref.dtype)

    tile = pl.BlockSpec((tm, d), lambda i: (i, 0))
    gate = [pl.BlockSpec((tm, CW), lambda i, c=off + j: (i, c)) for off in (OFF_GA, OFF_GS) for j in range(ncol)]
    both = pl.BlockSpec((pl.Element(tm), pl.Element(2 * d)), lambda i: (i * tm, OFF_GA * CW))
    return pl.pallas_call(
        body, name="mm_merge_bwd", grid=(rows // tm,),
        in_specs=[tile, pl.BlockSpec((d, d), lambda i: (0, 0))] + gate + [tile, tile],
        out_specs=[tile, tile, both],
        out_shape=[jax.ShapeDtypeStruct((rows, d), _MXU)] * 2 + [jax.ShapeDtypeStruct((rows, IN_W), _MXU)],
        compiler_params=_params(("arbitrary",)),
    )(dout_b, w_out, *([proj] * (2 * ncol)), y_a, y_s)


def _ew(fn, ins, outs, *, rows, ncol, name, n_acc=0, tm=512, deps=(), into=None):
    deps = list(deps) + ([into[1]] if into else [])
    n_in, n_out, nd = len(ins), len(outs), len(deps)
    tm = min(tm, rows)
    in_specs = []
    for _, kind, col0 in ins:
        if kind == "mat":
            in_specs.append(pl.BlockSpec((tm, CW), lambda j, i, c0=col0: (i, c0 + j)))
        else:
            in_specs.append(pl.BlockSpec((1, CW), lambda j, i, c0=col0: (0, c0 + j)))
    out_specs = [pl.BlockSpec((tm, CW), lambda j, i: (i, j)) for _ in outs]
    out_shape = [jax.ShapeDtypeStruct((rows, w), dt) for w, dt in outs]
    if into:
        out_specs[into[0]] = pl.BlockSpec((tm, CW), lambda j, i, c0=into[2]: (i, c0 + j))
        out_shape[into[0]] = jax.ShapeDtypeStruct(into[1].shape, into[1].dtype)
    for _ in range(n_acc):
        out_specs.append(pl.BlockSpec((1, CW), lambda j, i: (0, j)))
        out_shape.append(jax.ShapeDtypeStruct((1, ncol * CW), F32))

    def body(*refs):
        vals = fn(*[r[...] for r in refs[:n_in]])
        refs = refs[n_in + nd:]
        for r, v in zip(refs[:n_out], vals[:n_out]):
            r[...] = v.astype(r.dtype)
        i = pl.program_id(1)
        for r, v in zip(refs[n_out:], vals[n_out:]):
            @pl.when(i == 0)
            def _(r=r, v=v):
                r[...] = v

            @pl.when(i > 0)
            def _(r=r, v=v):
                r[...] += v

    res = pl.pallas_call(
        body, name=name, grid=(ncol, rows // tm), in_specs=in_specs + [_ANY] * nd, out_specs=out_specs,
        out_shape=out_shape, input_output_aliases={n_in + nd - 1: into[0]} if into else {},
        compiler_params=_params(("parallel", "arbitrary")),
    )(*[a for a, _, _ in ins], *deps)
    return res


def _colsum(v):
    return jnp.sum(v, axis=0, keepdims=True)


def _sigmoid(v):
    return jax.nn.sigmoid(v)


def _silu_and_grad(v):
    s = _sigmoid(v)
    return v * s, s * (1.0 + v * (1.0 - s))


def _rms_fwd(x, w, *, tm=512, deps=()):
    rows, d = x.shape
    nd = len(deps)

    def body(x_ref, w_ref, *rest):
        h_ref, r_ref = rest[nd:]
        xv = x_ref[...]
        r = lax.rsqrt(jnp.mean(xv * xv, axis=-1, keepdims=True) + NORM_EPS)
        h_ref[...] = (xv * r * w_ref[...]).astype(h_ref.dtype)
        r_ref[...] = r

    return pl.pallas_call(
        body, name="rms_fwd", grid=(rows // tm,),
        in_specs=[pl.BlockSpec((tm, d), lambda i: (i, 0)), pl.BlockSpec((1, d), lambda i: (0, 0))] + [_ANY] * nd,
        out_specs=[pl.BlockSpec((tm, d), lambda i: (i, 0)), pl.BlockSpec((tm, 1), lambda i: (i, 0))],
        out_shape=[jax.ShapeDtypeStruct((rows, d), _MXU), jax.ShapeDtypeStruct((rows, 1), F32)],
        compiler_params=_params(("arbitrary",)),
    )(x, w, *deps)


def _rms_bwd(dh, x, rstd, w, dout, *, tm=256):
    rows, d = x.shape

    def body(dh_ref, x_ref, r_ref, w_ref, do_ref, gx_ref, gw_ref):
        dhv, xv, r, wv = dh_ref[...], x_ref[...], r_ref[...], w_ref[...]
        xr = xv * r
        t = jnp.mean(dhv * wv * xr, axis=-1, keepdims=True)
        gx_ref[...] = do_ref[...] + r * (wv * dhv - xr * t)
        part = _colsum(dhv * xr)
        i = pl.program_id(0)

        @pl.when(i == 0)
        def _():
            gw_ref[...] = part

        @pl.when(i > 0)
        def _():
            gw_ref[...] += part

    return pl.pallas_call(
        body, name="rms_bwd", grid=(rows // tm,),
        in_specs=[pl.BlockSpec((tm, d), lambda i: (i, 0)), pl.BlockSpec((tm, d), lambda i: (i, 0)),
                  pl.BlockSpec((tm, 1), lambda i: (i, 0)), pl.BlockSpec((1, d), lambda i: (0, 0)),
                  pl.BlockSpec((tm, d), lambda i: (i, 0))],
        out_specs=[pl.BlockSpec((tm, d), lambda i: (i, 0)), pl.BlockSpec((1, d), lambda i: (0, 0))],
        out_shape=[jax.ShapeDtypeStruct((rows, d), F32), jax.ShapeDtypeStruct((1, d), F32)],
        compiler_params=_params(("arbitrary",)),
    )(dh, x, rstd, w, dout)


_NT = (((1,), (1,)), ((), ()))
_TN = (((0,), (0,)), ((), ()))


QKV_W = ATTN_W + 2 * KV_W
HEADS_PER_TILE = LANES // HEAD_DIM


def _low_half(rows):
    return lax.broadcasted_iota(jnp.int32, (rows, LANES), 1) < HEAD_DIM


def _pair_mean(t, low):
    m_lo = jnp.sum(jnp.where(low, t, 0.0), axis=-1, keepdims=True)
    m_hi = jnp.sum(jnp.where(low, 0.0, t), axis=-1, keepdims=True)
    return jnp.where(low, m_lo, m_hi) * (1.0 / HEAD_DIM)


def _pair_rstd(t, low):
    return lax.rsqrt(_pair_mean(t * t, low) + NORM_EPS)


def _dup_half(t, hi, low):
    swapped = pltpu.roll(t, HEAD_DIM, 1)
    return jnp.where(low, swapped, t) if hi else jnp.where(low, t, swapped)


def _fold_halves(t):
    return t + pltpu.roll(t, HEAD_DIM, 1)


def _split_heads(t, low):
    return [jnp.where(low, t, 0.0), jnp.where(low, 0.0, t)]


def _stacked_band_mask(n):
    rows = Q_PER_KV * WINDOW
    qi = lax.broadcasted_iota(jnp.int32, (rows, 2 * WINDOW), 0) % WINDOW + WINDOW
    kj = lax.broadcasted_iota(jnp.int32, (rows, 2 * WINDOW), 1)
    diff = qi - kj
    first_key = jnp.where(n > 0, 0, WINDOW)
    return (diff >= 0) & (diff < WINDOW) & (kj >= first_key)


def _stacked_sinks(sink_ref, g):
    blk = lax.broadcasted_iota(jnp.int32, (Q_PER_KV * WINDOW, 1), 0) // WINDOW
    col = jnp.full((Q_PER_KV * WINDOW, 1), sink_ref[Q_PER_KV * g], F32)
    for r in range(1, Q_PER_KV):
        col = jnp.where(blk == r, sink_ref[Q_PER_KV * g + r], col)
    return col


def _attn_in_specs(nblk, rev):
    def cur(n):
        return (nblk - 1 - n) if rev else n

    q_spec = pl.BlockSpec((WINDOW, ATTN_W), lambda n: (cur(n), 0))
    kvc_spec = pl.BlockSpec((WINDOW, 2 * KV_W), lambda n: (cur(n), ATTN_W // (2 * KV_W)))
    kvp_spec = pl.BlockSpec((WINDOW, 2 * KV_W), lambda n: (jnp.maximum(cur(n) - 1, 0), ATTN_W // (2 * KV_W)))
    w_spec = pl.BlockSpec((1, LANES), lambda n: (0, 0))
    l_spec = pl.BlockSpec((WINDOW, N_Q_HEADS), lambda n: (cur(n), 0))
    gate_specs = [pl.BlockSpec((WINDOW, CW), lambda n, col=OFF_AGATE + j: (cur(n), col)) for j in range(ATTN_W // CW)]
    return q_spec, kvc_spec, kvp_spec, w_spec, l_spec, gate_specs


def _attn2_fwd(proj, qw2, kw2, sinks, deps=()):
    seq = proj.shape[0]
    nblk = seq // WINDOW
    scale = 1.0 / math.sqrt(HEAD_DIM)
    q_spec, kvc_spec, kvp_spec, w_spec, l_spec, gate_specs = _attn_in_specs(nblk, False)
    nd, ng = len(deps), len(gate_specs)

    def body(sink_ref, q_ref, kvc_ref, kvp_ref, qw_ref, kw_ref, *rest):
        gate_refs = rest[:ng]
        o_ref, lse_ref, ya_ref = rest[ng + nd:]
        n = pl.program_id(0)
        low, low2 = _low_half(WINDOW), _low_half(2 * WINDOW)
        valid = _stacked_band_mask(n)
        head_lane = lax.broadcasted_iota(jnp.int32, (WINDOW, N_Q_HEADS), 1)
        kv = jnp.concatenate([kvp_ref[...], kvc_ref[...]], axis=0)
        qwv, kwv = qw_ref[...], kw_ref[...]
        lse_blk = jnp.zeros((WINDOW, N_Q_HEADS), F32)
        for t in range(N_KV_HEADS // HEADS_PER_TILE):
            kt = kv[:, t * LANES:(t + 1) * LANES]
            vt = kv[:, KV_W + t * LANES:KV_W + (t + 1) * LANES]
            kn = kt * _pair_rstd(kt, low2) * kwv
            for hi in range(HEADS_PER_TILE):
                g = HEADS_PER_TILE * t + hi
                kdup = _dup_half(kn, hi, low2).astype(_MXU)
                vdup = _dup_half(vt, hi, low2).astype(_MXU)
                stack = []
                for tq in (2 * g, 2 * g + 1):
                    qt = q_ref[:, tq * LANES:(tq + 1) * LANES]
                    stack += _split_heads(qt * _pair_rstd(qt, low) * qwv, low)
                qs = jnp.concatenate(stack, axis=0).astype(_MXU)
                s = lax.dot_general(qs, kdup, _NT, preferred_element_type=F32) * scale
                s = jnp.where(valid, s, -1e30)
                sink = _stacked_sinks(sink_ref, g)
                m = jnp.maximum(jnp.max(s, axis=-1, keepdims=True), sink)
                e = jnp.exp(s - m)
                z = jnp.sum(e, axis=-1, keepdims=True) + jnp.exp(sink - m)
                o = jnp.dot((e / z).astype(_MXU), vdup, preferred_element_type=F32)
                for i, tq in enumerate((2 * g, 2 * g + 1)):
                    o_ref[:, tq * LANES:(tq + 1) * LANES] = jnp.where(
                        low, o[2 * i * WINDOW:(2 * i + 1) * WINDOW], o[(2 * i + 1) * WINDOW:(2 * i + 2) * WINDOW])
                lse = m + jnp.log(z)
                for r in range(Q_PER_KV):
                    lse_blk = jnp.where(head_lane == Q_PER_KV * g + r, lse[r * WINDOW:(r + 1) * WINDOW], lse_blk)
        lse_ref[...] = lse_blk
        for j, g_ref in enumerate(gate_refs):
            cols = slice(j * CW, (j + 1) * CW)
            gate = g_ref[...]
            ya_ref[:, cols] = (o_ref[:, cols] * (gate * _sigmoid(gate))).astype(ya_ref.dtype)

    return pl.pallas_call(
        body, name="attn_fwd", grid=(nblk,),
        in_specs=[pl.BlockSpec(memory_space=pltpu.SMEM), q_spec, kvc_spec, kvp_spec, w_spec, w_spec] + gate_specs
        + [_ANY] * nd,
        out_specs=[q_spec, l_spec, q_spec],
        out_shape=[jax.ShapeDtypeStruct((seq, ATTN_W), F32), jax.ShapeDtypeStruct((seq, N_Q_HEADS), F32),
                   jax.ShapeDtypeStruct((seq, ATTN_W), _MXU)],
        compiler_params=_params(("arbitrary",)),
    )(sinks, proj, proj, proj, qw2, kw2, *([proj] * ng), *deps)


def _attn2_bwd(proj, qw2, kw2, sinks, lse, attn, dya, d_proj, deps=()):
    seq = proj.shape[0]
    nblk = seq // WINDOW
    scale = 1.0 / math.sqrt(HEAD_DIM)
    q_spec, kvc_spec, kvp_spec, w_spec, l_spec, gate_specs = _attn_in_specs(nblk, True)
    s_spec = pl.BlockSpec((1, N_Q_HEADS), lambda n: (0, 0))
    d_spec = pl.BlockSpec((WINDOW, QKV_W + ATTN_W), lambda n: (nblk - 1 - n, 0))
    deps = list(deps) + [d_proj]
    nd, ng = len(deps), len(gate_specs)

    def body(sink_ref, q_ref, kvc_ref, kvp_ref, qw_ref, kw_ref, lse_ref, attn_ref, dya_ref, *rest):
        gate_refs = rest[:ng]
        d_ref, dqw_ref, dkw_ref, dsk_ref, carry, do_ref = rest[ng + nd:]
        step = pl.program_id(0)
        n = nblk - 1 - step

        @pl.when(step == 0)
        def _():
            carry[...] = jnp.zeros_like(carry)
            dqw_ref[...] = jnp.zeros_like(dqw_ref)
            dkw_ref[...] = jnp.zeros_like(dkw_ref)
            dsk_ref[...] = jnp.zeros_like(dsk_ref)

        for j, g_ref in enumerate(gate_refs):
            cols = slice(j * CW, (j + 1) * CW)
            f, df = _silu_and_grad(g_ref[...])
            dv = dya_ref[:, cols]
            do_ref[:, cols] = dv * f
            d_ref[:, QKV_W + j * CW:QKV_W + (j + 1) * CW] = (dv * attn_ref[:, cols] * df).astype(d_ref.dtype)

        low, low2 = _low_half(WINDOW), _low_half(2 * WINDOW)
        valid = _stacked_band_mask(n)
        head_lane = lax.broadcasted_iota(jnp.int32, (WINDOW, N_Q_HEADS), 1)
        sink_lane = lax.broadcasted_iota(jnp.int32, (1, N_Q_HEADS), 1)
        kv = jnp.concatenate([kvp_ref[...], kvc_ref[...]], axis=0)
        qwv, kwv = qw_ref[...], kw_ref[...]
        lse_blk = lse_ref[...]
        dqw = jnp.zeros((1, LANES), F32)
        dkw = jnp.zeros((1, LANES), F32)
        dsk = jnp.zeros((1, N_Q_HEADS), F32)
        for t in range(N_KV_HEADS // HEADS_PER_TILE):
            kt = kv[:, t * LANES:(t + 1) * LANES]
            vt = kv[:, KV_W + t * LANES:KV_W + (t + 1) * LANES]
            rk = _pair_rstd(kt, low2)
            kn = kt * rk * kwv
            dkn_t = jnp.zeros((2 * WINDOW, LANES), F32)
            dv_t = jnp.zeros((2 * WINDOW, LANES), F32)
            for hi in range(HEADS_PER_TILE):
                g = HEADS_PER_TILE * t + hi
                kdup = _dup_half(kn, hi, low2).astype(_MXU)
                vdup = _dup_half(vt, hi, low2).astype(_MXU)
                tiles = (2 * g, 2 * g + 1)
                qx, rq, stack, dstack, lse_rows = [], [], [], [], []
                for tq in tiles:
                    qt = q_ref[:, tq * LANES:(tq + 1) * LANES]
                    r = _pair_rstd(qt, low)
                    rq.append(r)
                    qx.append(qt * r)
                    stack += _split_heads(qx[-1] * qwv, low)
                    dstack += _split_heads(do_ref[:, tq * LANES:(tq + 1) * LANES], low)
                for r in range(Q_PER_KV):
                    lse_rows.append(jnp.sum(jnp.where(head_lane == Q_PER_KV * g + r, lse_blk, 0.0), axis=-1, keepdims=True))
                qs = jnp.concatenate(stack, axis=0).astype(_MXU)
                dos = jnp.concatenate(dstack, axis=0).astype(_MXU)
                lse_col = jnp.concatenate(lse_rows, axis=0)
                s = lax.dot_general(qs, kdup, _NT, preferred_element_type=F32) * scale
                s = jnp.where(valid, s, -1e30)
                p = jnp.exp(s - lse_col)
                dp = lax.dot_general(dos, vdup, _NT, preferred_element_type=F32)
                dsum = jnp.sum(p * dp, axis=-1, keepdims=True)
                ds = (p * (dp - dsum) * scale).astype(_MXU)
                dsink = -jnp.exp(_stacked_sinks(sink_ref, g) - lse_col) * dsum
                for r in range(Q_PER_KV):
                    dsk = dsk + jnp.where(sink_lane == Q_PER_KV * g + r, _colsum(dsink[r * WINDOW:(r + 1) * WINDOW]), 0.0)
                dv_g = _fold_halves(lax.dot_general(p.astype(_MXU), dos, _TN, preferred_element_type=F32))
                dkn_g = _fold_halves(lax.dot_general(ds, qs, _TN, preferred_element_type=F32))
                dv_t = jnp.where(low2, dv_t, dv_g) if hi else jnp.where(low2, dv_g, dv_t)
                dkn_t = jnp.where(low2, dkn_t, dkn_g) if hi else jnp.where(low2, dkn_g, dkn_t)
                dqn = jnp.dot(ds, kdup, preferred_element_type=F32)
                for i, tq in enumerate(tiles):
                    dqn_t = jnp.where(low, dqn[2 * i * WINDOW:(2 * i + 1) * WINDOW],
                                      dqn[(2 * i + 1) * WINDOW:(2 * i + 2) * WINDOW])
                    dq = rq[i] * (qwv * dqn_t - qx[i] * _pair_mean(dqn_t * qwv * qx[i], low))
                    d_ref[:, tq * LANES:(tq + 1) * LANES] = dq.astype(d_ref.dtype)
                    dqw = dqw + _colsum(dqn_t * qx[i])
            k_cols = slice(t * LANES, (t + 1) * LANES)
            v_cols = slice(KV_W + t * LANES, KV_W + (t + 1) * LANES)
            dkn_c = dkn_t[WINDOW:] + carry[:, k_cols]
            rc = rk[WINDOW:]
            kx = kt[WINDOW:] * rc
            dk = rc * (kwv * dkn_c - kx * _pair_mean(dkn_c * kwv * kx, low))
            d_ref[:, ATTN_W + t * LANES:ATTN_W + (t + 1) * LANES] = dk.astype(d_ref.dtype)
            d_ref[:, ATTN_W + KV_W + t * LANES:ATTN_W + KV_W + (t + 1) * LANES] = (
                dv_t[WINDOW:] + carry[:, v_cols]).astype(d_ref.dtype)
            carry[:, k_cols] = dkn_t[:WINDOW]
            carry[:, v_cols] = dv_t[:WINDOW]
            dkw = dkw + _colsum(dkn_c * kx)
        dqw_ref[...] += dqw
        dkw_ref[...] += dkw
        dsk_ref[...] += dsk

    return pl.pallas_call(
        body, name="attn_bwd", grid=(nblk,),
        in_specs=[pl.BlockSpec(memory_space=pltpu.SMEM), q_spec, kvc_spec, kvp_spec, w_spec, w_spec, l_spec, q_spec,
                  q_spec] + gate_specs + [_ANY] * nd,
        out_specs=[d_spec, w_spec, w_spec, s_spec],
        out_shape=[jax.ShapeDtypeStruct(d_proj.shape, d_proj.dtype), jax.ShapeDtypeStruct((1, LANES), F32),
                   jax.ShapeDtypeStruct((1, LANES), F32), jax.ShapeDtypeStruct((1, N_Q_HEADS), F32)],
        input_output_aliases={9 + ng + nd - 1: 0},
        scratch_shapes=[pltpu.VMEM((WINDOW, 2 * KV_W), F32), pltpu.VMEM((WINDOW, ATTN_W), F32)],
        compiler_params=_params(("arbitrary",)),
    )(sinks, proj, proj, proj, qw2, kw2, lse, attn, dya, *([proj] * ng), *deps)


def _ssm_discretise(a_re, a_im, log_dt):
    dt = jnp.exp(log_dt)
    mag = jnp.exp(dt * a_re)
    ab_re = mag * jnp.cos(dt * a_im)
    ab_im = mag * jnp.sin(dt * a_im)
    num_re = ab_re - 1.0
    num_im = ab_im
    den = a_re * a_re + a_im * a_im
    cf_re = (num_re * a_re + num_im * a_im) / den
    cf_im = (num_im * a_re - num_re * a_im) / den
    return ab_re, ab_im, cf_re, cf_im


def _ssm_params_fwd(a_re, a_im, log_dt):
    shp = jax.ShapeDtypeStruct(a_re.shape, F32)

    def body(are_ref, aim_ref, ldt_ref, abr_ref, abi_ref, cfr_ref, cfi_ref, alr_ref, ali_ref):
        abr, abi, cfr, cfi = _ssm_discretise(are_ref[...], aim_ref[...], ldt_ref[...])
        abr_ref[...], abi_ref[...], cfr_ref[...], cfi_ref[...] = abr, abi, cfr, cfi
        pr, pi = abr, abi
        for _ in range(int(math.log2(SSM_L))):
            pr, pi = pr * pr - pi * pi, 2.0 * pr * pi
        alr_ref[...], ali_ref[...] = pr, pi

    return pl.pallas_call(body, name="ssm_params_fwd", out_shape=[shp] * 6)(a_re, a_im, log_dt)


def _ssm_params_bwd(a_re, a_im, log_dt, d_abr, d_abi, d_cfr, d_cfi):
    def body(are_ref, aim_ref, ldt_ref, g0, g1, g2, g3, dare_ref, daim_ref, dldt_ref):
        _, vjp = jax.vjp(_ssm_discretise, are_ref[...], aim_ref[...], ldt_ref[...])
        dare_ref[...], daim_ref[...], dldt_ref[...] = vjp((g0[...], g1[...], g2[...], g3[...]))

    return pl.pallas_call(
        body, name="ssm_params_bwd",
        out_shape=[jax.ShapeDtypeStruct(a_re.shape, F32), jax.ShapeDtypeStruct(a_im.shape, F32),
                   jax.ShapeDtypeStruct(log_dt.shape, F32)],
    )(a_re, a_im, log_dt, d_abr, d_abi, d_cfr, d_cfi)


def _scan_cols(j):
    return pl.ds(j * SSM_SB, SSM_SB)


def _rows8(r):
    return pl.ds(pl.multiple_of(r * SUBLANES, SUBLANES), SUBLANES)


def _bcast8(row):
    return jnp.broadcast_to(row, (SUBLANES, row.shape[-1]))


def _token_order_pick():
    tok = lax.broadcasted_iota(jnp.int32, (SSM_T, SSM_T), 0)
    row = lax.broadcasted_iota(jnp.int32, (SSM_T, SSM_T), 1)
    return (row == SUBLANES * (tok % SSM_L) + tok // SSM_L).astype(_MXU)


SCAN_UNROLL = 8


def _scan_loop(n, step, init):
    def trip(o, carry):
        for i in range(SCAN_UNROLL):
            carry = step(o * SCAN_UNROLL + i, carry)
        return carry

    return lax.fori_loop(0, n // SCAN_UNROLL, trip, init)


def _ssm_fwd(u, b_re, b_im, c_re, c_im, d_skip, coef):
    seq = u.shape[0]
    nc = seq // SSM_T
    T, L = SSM_T, SSM_L

    def body(u_ref, bre_ref, bim_ref, cre_ref, cim_ref, d_ref, are_ref, aim_ref, cfr_ref, cfi_ref, alr_ref, ali_ref,
             y_ref, yg_ref, sre_ref, sim_ref, ire_ref, iim_ref, car_re, car_im, end_re, end_im, yg_scan):
        c = pl.program_id(0)

        @pl.when(c == 0)
        def _():
            car_re[...] = jnp.zeros_like(car_re)
            car_im[...] = jnp.zeros_like(car_im)

        for j in range(SSM_JB):
            ub = u_ref[:, j * LANES:(j + 1) * LANES].astype(_MXU)
            bur = jnp.dot(ub, bre_ref[j], preferred_element_type=F32)
            bui = jnp.dot(ub, bim_ref[j], preferred_element_type=F32)
            cfr, cfi = cfr_ref[:, _scan_cols(j)], cfi_ref[:, _scan_cols(j)]
            sre_ref[:, _scan_cols(j)] = cfr * bur - cfi * bui
            sim_ref[:, _scan_cols(j)] = cfr * bui + cfi * bur

        for j in range(SSM_JB):
            cols = _scan_cols(j)
            ar, ai = _bcast8(are_ref[:, cols]), _bcast8(aim_ref[:, cols])

            def step1(r, s, cols=cols, ar=ar, ai=ai):
                sr, si = s
                rows = _rows8(r)
                return (ar * sr - ai * si + sre_ref[rows, cols], ar * si + ai * sr + sim_ref[rows, cols])

            zero = jnp.zeros((SUBLANES, SSM_SB), F32)
            er, ei = _scan_loop(L, step1, (zero, zero))
            end_re[:, cols] = er
            end_im[:, cols] = ei

        alr, ali = alr_ref[...], ali_ref[...]
        cr, ci = car_re[...], car_im[...]
        ire_ref[0:1, :] = cr
        iim_ref[0:1, :] = ci
        for i in range(1, SUBLANES):
            er, ei = end_re[i - 1:i, :], end_im[i - 1:i, :]
            cr, ci = alr * cr - ali * ci + er, alr * ci + ali * cr + ei
            ire_ref[i:i + 1, :] = cr
            iim_ref[i:i + 1, :] = ci

        for j in range(SSM_JB):
            cols = _scan_cols(j)
            ar, ai = _bcast8(are_ref[:, cols]), _bcast8(aim_ref[:, cols])

            def step2(r, s, cols=cols, ar=ar, ai=ai):
                sr, si = s
                rows = _rows8(r)
                nr = ar * sr - ai * si + sre_ref[rows, cols]
                ni = ar * si + ai * sr + sim_ref[rows, cols]
                sre_ref[rows, cols] = nr
                sim_ref[rows, cols] = ni
                return nr, ni

            _scan_loop(L, step2, (ire_ref[:, cols], iim_ref[:, cols]))

        car_re[...] = sre_ref[T - 1:T, :]
        car_im[...] = sim_ref[T - 1:T, :]

        for j in range(SSM_JB):
            cols = _scan_cols(j)
            ch = slice(j * LANES, (j + 1) * LANES)
            y = (jnp.dot(sre_ref[:, cols].astype(_MXU), cre_ref[j], preferred_element_type=F32)
                 - jnp.dot(sim_ref[:, cols].astype(_MXU), cim_ref[j], preferred_element_type=F32))
            y = y + d_ref[:, ch] * u_ref[:, ch]
            y_ref[:, ch] = y
            yg_scan[:, ch] = jax.nn.gelu(y).astype(yg_scan.dtype)
        yg_ref[...] = jnp.dot(_token_order_pick(), yg_scan[...], preferred_element_type=F32).astype(yg_ref.dtype)

    tok = pl.BlockSpec((T, SSM_W), lambda c: (c, 0))
    st = pl.BlockSpec((T, N_STATES), lambda c: (c, 0))
    ini = pl.BlockSpec((None, SUBLANES, N_STATES), lambda c: (c, 0, 0))
    bsp = pl.BlockSpec((SSM_JB, LANES, SSM_SB), lambda c: (0, 0, 0))
    csp = pl.BlockSpec((SSM_JB, SSM_SB, LANES), lambda c: (0, 0, 0))
    row_w = pl.BlockSpec((1, SSM_W), lambda c: (0, 0))
    row_s = pl.BlockSpec((1, N_STATES), lambda c: (0, 0))
    return pl.pallas_call(
        body, name="ssm_fwd", grid=(nc,),
        in_specs=[tok, bsp, bsp, csp, csp, row_w] + [row_s] * 6,
        out_specs=[tok, tok, st, st, ini, ini],
        out_shape=[jax.ShapeDtypeStruct((seq, SSM_W), F32), jax.ShapeDtypeStruct((seq, SSM_W), _MXU),
                   jax.ShapeDtypeStruct((seq, N_STATES), F32), jax.ShapeDtypeStruct((seq, N_STATES), F32),
                   jax.ShapeDtypeStruct((nc, SUBLANES, N_STATES), F32),
                   jax.ShapeDtypeStruct((nc, SUBLANES, N_STATES), F32)],
        scratch_shapes=[pltpu.VMEM((1, N_STATES), F32), pltpu.VMEM((1, N_STATES), F32),
                        pltpu.VMEM((SUBLANES, N_STATES), F32), pltpu.VMEM((SUBLANES, N_STATES), F32),
                        pltpu.VMEM((T, SSM_W), _MXU)],
        compiler_params=_params(("arbitrary",)),
    )(u, b_re, b_im, c_re, c_im, d_skip, *coef)


def _ssm_bwd(dyg, y, u, s_re, s_im, i_re, i_im, b_re, b_im, c_re, c_im, d_skip, coef, d_proj, deps=()):
    seq = u.shape[0]
    nc = seq // SSM_T
    T, L = SSM_T, SSM_L
    deps = list(deps) + [d_proj]

    def body(dyg_ref, y_ref, u_ref, sre_ref, sim_ref, ire_ref, iim_ref, bre_ref, bim_ref, cre_ref, cim_ref, d_ref,
             are_ref, aim_ref, cfr_ref, cfi_ref, alr_ref, ali_ref, *rest):
        (du_ref, dbre_out, dbim_out, dcre_out, dcim_out, dd_ref, dar_ref, dai_ref, dcfr_ref, dcfi_ref,
         lre, lim, car_re, car_im, end_re, end_im, ini_re, ini_im, dbre_ref, dbim_ref, dcre_ref, dcim_ref,
         dy_ref, du_scan) = rest[len(deps):]
        step = pl.program_id(0)
        dy_ref[...] = jax.vjp(jax.nn.gelu, y_ref[...])[1](dyg_ref[...])[0]

        @pl.when(step == 0)
        def _():
            car_re[...] = jnp.zeros_like(car_re)
            car_im[...] = jnp.zeros_like(car_im)
            for ref in (dbre_ref, dbim_ref, dcre_ref, dcim_ref, dd_ref, dar_ref, dai_ref, dcfr_ref, dcfi_ref):
                ref[...] = jnp.zeros_like(ref)

        for j in range(SSM_JB):
            dyb = dy_ref[:, j * LANES:(j + 1) * LANES].astype(_MXU)
            lre[:, _scan_cols(j)] = lax.dot_general(dyb, cre_ref[j], _NT, preferred_element_type=F32)
            lim[:, _scan_cols(j)] = -lax.dot_general(dyb, cim_ref[j], _NT, preferred_element_type=F32)

        for j in range(SSM_JB):
            cols = _scan_cols(j)
            ar, ai = _bcast8(are_ref[:, cols]), _bcast8(aim_ref[:, cols])

            def step1(t, s, cols=cols, ar=ar, ai=ai):
                sr, si = s
                rows = _rows8(L - 1 - t)
                return (ar * sr + ai * si + lre[rows, cols], ar * si - ai * sr + lim[rows, cols])

            zero = jnp.zeros((SUBLANES, SSM_SB), F32)
            er, ei = _scan_loop(L, step1, (zero, zero))
            end_re[:, cols] = er
            end_im[:, cols] = ei

        alr, ali = alr_ref[...], ali_ref[...]
        cr, ci = car_re[...], car_im[...]
        ini_re[SUBLANES - 1:SUBLANES, :] = cr
        ini_im[SUBLANES - 1:SUBLANES, :] = ci
        for i in range(SUBLANES - 2, -1, -1):
            er, ei = end_re[i + 1:i + 2, :], end_im[i + 1:i + 2, :]
            cr, ci = alr * cr + ali * ci + er, alr * ci - ali * cr + ei
            ini_re[i:i + 1, :] = cr
            ini_im[i:i + 1, :] = ci

        for j in range(SSM_JB):
            cols = _scan_cols(j)
            ar, ai = _bcast8(are_ref[:, cols]), _bcast8(aim_ref[:, cols])

            def step2(t, s, cols=cols, ar=ar, ai=ai):
                sr, si = s
                rows = _rows8(L - 1 - t)
                nr = ar * sr + ai * si + lre[rows, cols]
                ni = ar * si - ai * sr + lim[rows, cols]
                lre[rows, cols] = nr
                lim[rows, cols] = ni
                return nr, ni

            _scan_loop(L, step2, (ini_re[:, cols], ini_im[:, cols]))

        car_re[...] = lre[0:1, :]
        car_im[...] = lim[0:1, :]

        head, tail, body_rows = slice(0, SUBLANES), slice(SUBLANES, T), slice(0, T - SUBLANES)
        for j in range(SSM_JB):
            cols = _scan_cols(j)
            ch = slice(j * LANES, (j + 1) * LANES)
            lr, li = lre[:, cols], lim[:, cols]
            dar_ref[:, cols] += (_colsum(lre[tail, cols] * sre_ref[body_rows, cols] + lim[tail, cols] * sim_ref[body_rows, cols])
                                 + _colsum(lre[head, cols] * ire_ref[:, cols] + lim[head, cols] * iim_ref[:, cols]))
            dai_ref[:, cols] += (_colsum(lim[tail, cols] * sre_ref[body_rows, cols] - lre[tail, cols] * sim_ref[body_rows, cols])
                                 + _colsum(lim[head, cols] * ire_ref[:, cols] - lre[head, cols] * iim_ref[:, cols]))
            uf = u_ref[:, ch]
            ub = uf.astype(_MXU)
            bur = jnp.dot(ub, bre_ref[j], preferred_element_type=F32)
            bui = jnp.dot(ub, bim_ref[j], preferred_element_type=F32)
            dcfr_ref[:, cols] += _colsum(lr * bur + li * bui)
            dcfi_ref[:, cols] += _colsum(li * bur - lr * bui)
            cfr, cfi = cfr_ref[:, cols], cfi_ref[:, cols]
            dbur = (cfr * lr + cfi * li).astype(_MXU)
            dbui = (cfr * li - cfi * lr).astype(_MXU)
            dyf = dy_ref[:, ch]
            dyb = dyf.astype(_MXU)
            du = (lax.dot_general(dbur, bre_ref[j], _NT, preferred_element_type=F32)
                  + lax.dot_general(dbui, bim_ref[j], _NT, preferred_element_type=F32) + d_ref[:, ch] * dyf)
            du_scan[:, ch] = du.astype(du_scan.dtype)
            dbre_ref[j] += lax.dot_general(ub, dbur, _TN, preferred_element_type=F32)
            dbim_ref[j] += lax.dot_general(ub, dbui, _TN, preferred_element_type=F32)
            dcre_ref[j] += lax.dot_general(sre_ref[:, cols].astype(_MXU), dyb, _TN, preferred_element_type=F32)
            dcim_ref[j] -= lax.dot_general(sim_ref[:, cols].astype(_MXU), dyb, _TN, preferred_element_type=F32)
            dd_ref[:, ch] += _colsum(dyf * uf)
        du_ref[...] = jnp.dot(_token_order_pick(), du_scan[...], preferred_element_type=F32).astype(du_ref.dtype)

        @pl.when(step == nc - 1)
        def _():
            for acc, out in ((dbre_ref, dbre_out), (dbim_ref, dbim_out), (dcre_ref, dcre_out), (dcim_ref, dcim_out)):
                pltpu.sync_copy(acc, out)

    tok = pl.BlockSpec((T, SSM_W), lambda c: (nc - 1 - c, 0))
    st = pl.BlockSpec((T, N_STATES), lambda c: (nc - 1 - c, 0))
    ini = pl.BlockSpec((None, SUBLANES, N_STATES), lambda c: (nc - 1 - c, 0, 0))
    bsp = pl.BlockSpec((SSM_JB, LANES, SSM_SB), lambda c: (0, 0, 0))
    csp = pl.BlockSpec((SSM_JB, SSM_SB, LANES), lambda c: (0, 0, 0))
    row_w = pl.BlockSpec((1, SSM_W), lambda c: (0, 0))
    row_s = pl.BlockSpec((1, N_STATES), lambda c: (0, 0))
    big = pltpu.VMEM((T, N_STATES), F32)
    one = pltpu.VMEM((1, N_STATES), F32)
    eight = pltpu.VMEM((SUBLANES, N_STATES), F32)
    return pl.pallas_call(
        body, name="ssm_bwd", grid=(nc,),
        in_specs=[tok, tok, tok, st, st, ini, ini, bsp, bsp, csp, csp, row_w] + [row_s] * 6 + [_ANY] * len(deps),
        out_specs=[pl.BlockSpec((pl.Element(T), pl.Element(SSM_W)), lambda c: ((nc - 1 - c) * T, OFF_U * CW)),
                   _ANY, _ANY, _ANY, _ANY, row_w, row_s, row_s, row_s, row_s],
        input_output_aliases={18 + len(deps) - 1: 0},
        out_shape=[jax.ShapeDtypeStruct(d_proj.shape, d_proj.dtype),
                   jax.ShapeDtypeStruct((SSM_JB, LANES, SSM_SB), F32), jax.ShapeDtypeStruct((SSM_JB, LANES, SSM_SB), F32),
                   jax.ShapeDtypeStruct((SSM_JB, SSM_SB, LANES), F32), jax.ShapeDtypeStruct((SSM_JB, SSM_SB, LANES), F32),
                   jax.ShapeDtypeStruct((1, SSM_W), F32)] + [jax.ShapeDtypeStruct((1, N_STATES), F32)] * 4,
        scratch_shapes=[big, big, one, one, eight, eight, eight, eight,
                        pltpu.VMEM((SSM_JB, LANES, SSM_SB), F32), pltpu.VMEM((SSM_JB, LANES, SSM_SB), F32),
                        pltpu.VMEM((SSM_JB, SSM_SB, LANES), F32), pltpu.VMEM((SSM_JB, SSM_SB, LANES), F32),
                        pltpu.VMEM((T, SSM_W), F32), pltpu.VMEM((T, SSM_W), _MXU)],
        compiler_params=_params(("arbitrary",)),
    )(dyg, y, u, s_re, s_im, i_re, i_im, b_re, b_im, c_re, c_im, d_skip, *coef, *deps)


def _block_diag_b(b):
    t = b.reshape(SSM_JB, 8, STATE, GROUP).transpose(0, 1, 3, 2)
    eye = jnp.eye(8, dtype=b.dtype)
    return (t[:, :, :, None, :] * eye[None, :, None, :, None]).reshape(SSM_JB, LANES, SSM_SB)


def _block_diag_c(c):
    t = c.reshape(SSM_JB, 8, GROUP, STATE).transpose(0, 1, 3, 2)
    eye = jnp.eye(8, dtype=c.dtype)
    return (t[:, :, :, None, :] * eye[None, :, None, :, None]).reshape(SSM_JB, SSM_SB, LANES)


def _diag_of_b(blk):
    t = blk.reshape(SSM_JB, 8, GROUP, 8, STATE)
    d = jnp.sum(t * jnp.eye(8, dtype=blk.dtype)[None, :, None, :, None], axis=3)
    return d.transpose(0, 1, 3, 2).reshape(N_GROUPS, STATE, GROUP)


def _diag_of_c(blk):
    t = blk.reshape(SSM_JB, 8, STATE, 8, GROUP)
    d = jnp.sum(t * jnp.eye(8, dtype=blk.dtype)[None, :, None, :, None], axis=3)
    return d.transpose(0, 1, 3, 2).reshape(N_GROUPS, GROUP, STATE)


def _to_scan_order(v):
    seq, w = v.shape
    return v.reshape(seq // SSM_T, SUBLANES, SSM_L, w).transpose(0, 2, 1, 3).reshape(seq, w)


def _adamw_math(w, g, m, v):
    nm = ADAM_B1 * m + (1.0 - ADAM_B1) * g
    nv = ADAM_B2 * v + (1.0 - ADAM_B2) * jnp.square(g)
    m_hat = nm / (1.0 - ADAM_B1 ** ADAM_STEP)
    v_hat = nv / (1.0 - ADAM_B2 ** ADAM_STEP)
    return -ADAM_LR * (m_hat / (jnp.sqrt(v_hat) + ADAM_EPS) + ADAM_WD * w), nm, nv


def _adamw(w, g, m, v, *, name, tm, deps=()):
    rows, cols = w.shape
    nd = len(deps)

    def body(w_ref, g_ref, m_ref, v_ref, *rest):
        d_ref, nm_ref, nv_ref = rest[nd:]
        d_ref[...], nm_ref[...], nv_ref[...] = _adamw_math(w_ref[...], g_ref[...], m_ref[...], v_ref[...])

    spec = pl.BlockSpec((tm, cols), lambda i: (i, 0))
    shp = jax.ShapeDtypeStruct((rows, cols), F32)
    return pl.pallas_call(body, name=name, grid=(rows // tm,), in_specs=[spec] * 4 + [_ANY] * nd,
                          out_specs=[spec] * 3, out_shape=[shp] * 3,
                          compiler_params=_params(("arbitrary",)))(w, g, m, v, *deps)


def _place():
    x, y, c = lax.axis_index("x"), lax.axis_index("y"), lax.axis_index("c")
    chips = [(1 - x, y), (x, 1 - y), (1 - x, 1 - y)]
    return x, y, c, chips


def _remote(src, dst, send_sem, recv_sem, dev):
    return pltpu.make_async_remote_copy(src_ref=src, dst_ref=dst, send_sem=send_sem, recv_sem=recv_sem,
                                        device_id=dev, device_id_type=MESH)


def _place_shard(w, mine_arr, *, name, tm=256):
    rows, cols = w.shape

    def body(m_ref, w_ref, o_ref):
        o_ref[...] = w_ref[...].astype(o_ref.dtype)

    return pl.pallas_call(
        body, name=name,
        grid_spec=pltpu.PrefetchScalarGridSpec(
            num_scalar_prefetch=1, grid=(rows // tm,),
            in_specs=[pl.BlockSpec((tm, cols), lambda i, m: (i, 0))],
            out_specs=pl.BlockSpec((None, tm, cols), lambda i, m: (m[0], i, 0))),
        out_shape=jax.ShapeDtypeStruct((N_CHIPS, rows, cols), _WIRE),
        compiler_params=_params(("arbitrary",)),
    )(mine_arr, w)


_HBM = pl.BlockSpec(memory_space=pltpu.HBM)
_SEM = pl.BlockSpec(memory_space=pltpu.SEMAPHORE)
_EFFECT = pltpu.SideEffectType.DATAFLOW_SIDE_EFFECTING


def _copies_start(name, bufs, plan, count, after=()):
    nb, na = len(bufs), len(after)

    def body(*refs):
        send_sems, recv_sems, token = refs[nb + na], refs[nb + na + 1], refs[-1]
        copies = plan(refs[:nb])
        assert len(copies) == count
        for i, (src, dst, dev, _) in enumerate(copies):
            _remote(src, dst, send_sems.at[i], recv_sems.at[i], dev).start()
        token[...] = jnp.zeros_like(token)

    res = pl.pallas_call(
        body, name=name, in_specs=[_HBM] * nb + [_ANY] * na,
        out_specs=(_SEM, _SEM, *[_HBM] * nb, pl.BlockSpec(memory_space=pltpu.VMEM)),
        out_shape=(pltpu.SemaphoreType.DMA((count,)), pltpu.SemaphoreType.DMA((count,)),
                   *[pltpu.HBM(b.shape, b.dtype) for b in bufs], jax.ShapeDtypeStruct((SUBLANES, LANES), F32)),
        input_output_aliases={i: 2 + i for i in range(nb)},
        compiler_params=pltpu.CompilerParams(has_side_effects=_EFFECT),
    )(*[pltpu.with_memory_space_constraint(b, pltpu.HBM) for b in bufs], *after)
    return (res[0], res[1]), list(res[2:2 + nb]), res[-1]


def _copies_wait(name, bufs, sems, plan, after=(), which=None):
    nb, na = len(bufs), len(after)

    def body(*refs):
        send_sems, recv_sems = refs[nb], refs[nb + 1]
        for i, (src, _, dev, land) in enumerate(plan(refs[:nb])):
            if which is not None and i not in which:
                continue
            cp = _remote(src, land, send_sems.at[i], recv_sems.at[i], dev)
            cp.wait_send()
            cp.wait_recv()

    res = pl.pallas_call(
        body, name=name, in_specs=[_HBM] * nb + [_SEM, _SEM] + [_ANY] * na, out_specs=[_HBM] * nb,
        out_shape=[pltpu.HBM(b.shape, b.dtype) for b in bufs],
        input_output_aliases={i: i for i in range(nb)},
        compiler_params=pltpu.CompilerParams(has_side_effects=_EFFECT),
    )(*bufs, *sems, *after)
    return list(res)


def _plan_gather_ici(fulls, which=(0, 1, 2)):
    x, y, c, chips = _place()
    copies = []
    for f in fulls:
        half = pl.ds(c * (f.shape[1] // 2), f.shape[1] // 2)
        own = f.at[2 * x + y, half]
        for chip in [chips[k] for k in which]:
            copies.append((own, own, (*chip, c), f.at[2 * chip[0] + chip[1], half]))
    return copies


def _plan_gather_d2d(fulls, which=(0, 1, 2)):
    x, y, c, chips = _place()
    copies = []
    for f in fulls:
        r2 = f.shape[1] // 2
        for chip in [chips[k] for k in which]:
            blk = 2 * chip[0] + chip[1]
            landed = f.at[blk, pl.ds(c * r2, r2)]
            copies.append((landed, landed, (x, y, 1 - c), f.at[blk, pl.ds((1 - c) * r2, r2)]))
    return copies


def _plan_relay_direct(fulls):
    (f,) = fulls
    x, y, c, chips = _place()
    half = pl.ds(c * (f.shape[1] // 2), f.shape[1] // 2)
    own = f.at[2 * x + y, half]
    return [(own, own, (*chip, c), f.at[2 * chip[0] + chip[1], half]) for chip in chips[:2]]


def _plan_relay_forward(fulls, k):
    (f,) = fulls
    x, y, c, chips = _place()
    r2 = f.shape[1] // 2
    half, other = pl.ds(c * r2, r2), pl.ds((1 - c) * r2, r2)
    quarter = pl.ds(c * r2 + k * (r2 // 2), r2 // 2)
    blk, far = 2 * chips[k][0] + chips[k][1], 2 * chips[2][0] + chips[2][1]
    passed, landed = f.at[blk, quarter], f.at[blk, half]
    return [(passed, passed, (*chips[1 - k], c), f.at[far, quarter]), (landed, landed, (x, y, 1 - c), f.at[blk, other])]


def _plan_relay_last(fulls):
    (f,) = fulls
    x, y, c, chips = _place()
    r2 = f.shape[1] // 2
    far = 2 * chips[2][0] + chips[2][1]
    landed = f.at[far, pl.ds(c * r2, r2)]
    return [(landed, landed, (x, y, 1 - c), f.at[far, pl.ds((1 - c) * r2, r2)])]


def _plan_swap_halves(refs):
    x, y, c, _ = _place()
    n = len(refs) // 2
    copies = []
    for g, land in zip(refs[:n], refs[n:]):
        r2 = g.shape[1] // 2
        copies.append((g.at[:, pl.ds((1 - c) * r2, r2), :], land, (x, y, 1 - c), land))
    return copies


def _plan_scatter_chips(refs):
    x, y, c, chips = _place()
    n = len(refs) // 2
    copies = []
    for h, land in zip(refs[:n], refs[n:]):
        for k, chip in enumerate(chips):
            copies.append((h.at[2 * chip[0] + chip[1]], land.at[k], (*chip, c), land.at[k]))
    return copies


def _plan_join_halves(totals):
    x, y, c, _ = _place()
    copies = []
    for t in totals:
        r2 = t.shape[0] // 2
        mine = t.at[pl.ds(c * r2, r2)]
        copies.append((mine, mine, (x, y, 1 - c), t.at[pl.ds((1 - c) * r2, r2)]))
    return copies


def _add_sibling_half(g, got, c_arr, *, name, tm):
    _, rows, cols = g.shape
    r2 = rows // 2
    nb = r2 // tm

    def body(c_ref, g_ref, r_ref, o_ref):
        o_ref[...] = (g_ref[...].astype(F32) + r_ref[...].astype(F32)).astype(o_ref.dtype)

    return pl.pallas_call(
        body, name=name,
        grid_spec=pltpu.PrefetchScalarGridSpec(
            num_scalar_prefetch=1, grid=(N_CHIPS, nb),
            in_specs=[pl.BlockSpec((None, tm, cols), lambda b, i, c: (b, c[0] * nb + i, 0)),
                      pl.BlockSpec((None, tm, cols), lambda b, i, c: (b, i, 0))],
            out_specs=pl.BlockSpec((None, tm, cols), lambda b, i, c: (b, i, 0))),
        out_shape=jax.ShapeDtypeStruct((N_CHIPS, r2, cols), _WIRE),
        compiler_params=_params(("arbitrary", "arbitrary")),
    )(c_arr, g, got)


def _add_chips(h, got, place_arr, *, name, tm):
    _, r2, cols = h.shape
    nb = r2 // tm

    def body(p_ref, h_ref, r_ref, o_ref):
        o_ref[...] = ((h_ref[...].astype(F32) + r_ref[0].astype(F32)) + r_ref[1].astype(F32)) + r_ref[2].astype(F32)

    return pl.pallas_call(
        body, name=name,
        grid_spec=pltpu.PrefetchScalarGridSpec(
            num_scalar_prefetch=1, grid=(nb,),
            in_specs=[pl.BlockSpec((None, tm, cols), lambda i, p: (p[0], i, 0)),
                      pl.BlockSpec((3, tm, cols), lambda i, p: (0, i, 0))],
            out_specs=pl.BlockSpec((tm, cols), lambda i, p: (p[1] * nb + i, 0))),
        out_shape=jax.ShapeDtypeStruct((2 * r2, cols), F32),
        compiler_params=_params(("arbitrary",)),
    )(place_arr, h, got)


class _ReduceScatter:
    def __init__(self, tag, names, grads):
        self.tag, self.names, self.n = tag, names, len(names)
        core = lax.axis_index("c").astype(jnp.int32)
        chip = (2 * lax.axis_index("x") + lax.axis_index("y")).astype(jnp.int32)
        self.c_arr, self.place_arr = core.reshape(1), jnp.stack([chip, core])
        self.bufs = list(grads)

    def _start(self, step, bufs, plan, count, after):
        self.plan = plan
        self.step = f"grad_{step}_{self.tag}"
        self.sems, self.bufs, token = _copies_start(self.step + "_start", bufs, plan, count, after)
        return [token]

    def _wait(self, after):
        self.bufs = _copies_wait(self.step + "_wait", self.bufs, self.sems, self.plan, after)
        return self.bufs

    def start_swap(self, after=()):
        lands = [lax.empty((N_CHIPS, g.shape[1] // 2, g.shape[2]), g.dtype) for g in self.bufs]
        return self._start("swap", self.bufs + lands, _plan_swap_halves, self.n, after)

    def start_scatter(self, after):
        bufs = self._wait(after)
        pair = [_add_sibling_half(g, r, self.c_arr, name=f"grad_add_sibling_{nm}", tm=min(256, g.shape[1] // 2))
                for nm, g, r in zip(self.names, bufs[:self.n], bufs[self.n:])]
        lands = [lax.empty((3,) + h.shape[1:], h.dtype) for h in pair]
        return self._start("scatter", pair + lands, _plan_scatter_chips, 3 * self.n, ())

    def start_join(self, after):
        bufs = self._wait(after)
        total = [_add_chips(h, r, self.place_arr, name=f"grad_add_chips_{nm}", tm=min(256, h.shape[1]))
                 for nm, h, r in zip(self.names, bufs[:self.n], bufs[self.n:])]
        return self._start("join", total, _plan_join_halves, self.n, ())

    def finish(self, after):
        return dict(zip(self.names, self._wait(after)))


def _all_gather_small(v):
    m_per, n = v.shape

    def body(x_ref, out_ref, send_sems, recv_sems, local_sem):
        x, y, c, chips = _place()
        me, sibling = (x, y, c), (x, y, 1 - c)

        def rows(px, py, pc):
            return out_ref.at[4 * px + 2 * py + pc]

        def copy(k, block, to, src=None):
            return _remote(rows(*block) if src is None else src, rows(*block), send_sems.at[k], recv_sems.at[k], to)

        mine = pltpu.make_async_copy(x_ref, rows(*me), local_sem)
        mine.start()
        first = [copy(0, me, sibling, src=x_ref)]
        first += [copy(1 + j, me, (*chip, c), src=x_ref) for j, chip in enumerate(chips)]
        for cp in first:
            cp.start()
        passed = [copy(4 + j, (*chip, c), sibling) for j, chip in enumerate(chips)]
        for j, chip in enumerate(chips):
            copy(1 + j, (*chip, c), me).wait_recv()
            passed[j].start()
        copy(0, sibling, me).wait_recv()
        for j, chip in enumerate(chips):
            copy(4 + j, (*chip, 1 - c), me).wait_recv()
        for cp in first + passed:
            cp.wait_send()
        mine.wait()

    return pl.pallas_call(
        body, name="gather_small_grads",
        out_shape=jax.ShapeDtypeStruct((8, m_per, n), v.dtype),
        in_specs=[pl.BlockSpec(memory_space=pltpu.VMEM)], out_specs=pl.BlockSpec(memory_space=pltpu.VMEM),
        scratch_shapes=[pltpu.SemaphoreType.DMA((7,)), pltpu.SemaphoreType.DMA((7,)), pltpu.SemaphoreType.DMA],
        compiler_params=pltpu.CompilerParams(vmem_limit_bytes=VMEM_LIMIT),
    )(v)


def _sum8(v, *, name):
    _, m, n = v.shape

    def body(v_ref, o_ref):
        acc = v_ref[0]
        for d in range(1, 8):
            acc = acc + v_ref[d]
        o_ref[...] = acc

    return pl.pallas_call(body, name=name, out_shape=jax.ShapeDtypeStruct((m, n), F32),
                          compiler_params=pltpu.CompilerParams(vmem_limit_bytes=VMEM_LIMIT))(v)


def _local_step(x, target, norm_w, q_norm_w, k_norm_w, sinks, a_re, a_im, log_dt, b_re, b_im, c_re, c_im, d_skip,
                b_glu, io):
    seq = x.shape[0]
    qw2 = jnp.tile(q_norm_w.reshape(1, HEAD_DIM), (1, HEADS_PER_TILE))
    kw2 = jnp.tile(k_norm_w.reshape(1, HEAD_DIM), (1, HEADS_PER_TILE))
    nw, bg = norm_w.reshape(1, D_MODEL), b_glu.reshape(1, D_MODEL)
    dsk = d_skip.reshape(1, SSM_W)

    h, rstd = _rms_fwd(x, nw, deps=io.begin())
    proj, w_in4 = io.projection(h)
    attn, lse, ya_in = _attn2_fwd(proj, qw2, kw2, sinks, deps=io.after_proj(proj))
    w_ap4 = io.weight("w_attn_proj", ya_in)
    w_glu4, w_sp4, w_out = io.weight("w_glu", ya_in), io.weight("w_ssm_proj", ya_in), io.weight("w_out", ya_in)
    y_a = _mm(ya_in, w_ap4, mode="nn", name="mm_attn_proj", tm=2048, tn=512, tk=ATTN_W, b_blocked=True,
              rows_outer=True)

    flat_a = (a_re.reshape(1, N_STATES), a_im.reshape(1, N_STATES), jnp.repeat(log_dt, STATE).reshape(1, N_STATES))
    coef = _ssm_params_fwd(*flat_a)
    bre_blk, bim_blk = _block_diag_b(b_re).astype(_MXU), _block_diag_b(b_im).astype(_MXU)
    cre_blk, cim_blk = _block_diag_c(c_re).astype(_MXU), _block_diag_c(c_im).astype(_MXU)
    u_scan = _to_scan_order(proj[:, OFF_U * CW:OFF_U * CW + SSM_W])
    y_scan, yg, s_re, s_im, i_re, i_im = _ssm_fwd(u_scan, bre_blk, bim_blk, cre_blk, cim_blk, dsk, coef)
    glu = _mm(yg, w_glu4, mode="nn", name="mm_glu", tm=2048, tn=512, tk=SSM_W, b_blocked=True, rows_outer=True)

    def gate_s(ga, gb, ba, bb, z):
        return ((ga + ba) * _sigmoid(gb + bb) * (z * _sigmoid(z)),)

    (ys_in,) = _ew(gate_s, [(glu, "mat", 0), (glu, "mat", 2), (bg, "row", 0), (bg, "row", 2), (proj, "mat", OFF_Z)],
                   [(SSM_W, _MXU)], rows=seq, ncol=2, name="ew_ssm_gate")
    y_s = _mm(ys_in, w_sp4, mode="nn", name="mm_ssm_proj", tm=2048, tn=512, tk=SSM_W, b_blocked=True,
              rows_outer=True)

    def merge(ga, gs, ya, ys):
        return (_sigmoid(ga) * ya + _sigmoid(gs) * ys,)

    (merged,) = _ew(merge, [(proj, "mat", OFF_GA), (proj, "mat", OFF_GS), (y_a, "mat", 0), (y_s, "mat", 0)],
                    [(D_MODEL, _MXU)], rows=seq, ncol=4, name="ew_merge")
    dout, dout_b, sq = _mm_out_loss(merged, w_out, x, target)
    loss = 0.5 * jnp.sum(sq) / D_MODEL

    d_ya, d_ys, d_proj = _mm_merge_bwd(dout_b, w_out, proj, y_a, y_s)
    g_w_out = _mm(merged, dout_b, mode="tn", name="mm_g_w_out", tm=1024, tn=D_MODEL, tk=1024, out_dtype=_WIRE)

    d_ya_in = _mm(d_ya, w_ap4, mode="nt", name="mm_d_attn_gate", tm=2048, tn=ATTN_W, tk=512, b_blocked=True)
    g_w_ap = _mm(ya_in, d_ya, mode="tn", name="mm_g_w_attn_proj", tm=ATTN_W, tn=D_MODEL, tk=2048, out_dtype=_WIRE,
                 out_blocked=True)

    d_ys_in = _mm(d_ys, w_sp4, mode="nt", name="mm_d_ssm_gate", tm=2048, tn=SSM_W, tk=512, b_blocked=True)
    g_w_sp = _mm(ys_in, d_ys, mode="tn", name="mm_g_w_ssm_proj", tm=SSM_W, tn=D_MODEL, tk=2048, out_dtype=_WIRE,
                 out_blocked=True)

    def gate_s_bwd(dv, ga, gb, ba, bb, z):
        a, sb = ga + ba, _sigmoid(gb + bb)
        f, df = _silu_and_grad(z)
        dga = dv * sb * f
        dgb = dv * a * f * sb * (1.0 - sb)
        return dga, dgb, dv * a * sb * df, _colsum(dga), _colsum(dgb)

    d_glu_a, d_glu_b, d_proj, g_bga, g_bgb = _ew(
        gate_s_bwd, [(d_ys_in, "mat", 0), (glu, "mat", 0), (glu, "mat", 2), (bg, "row", 0), (bg, "row", 2),
                     (proj, "mat", OFF_Z)],
        [(SSM_W, _MXU)] * 3, rows=seq, ncol=2, n_acc=2, name="ew_ssm_gate_bwd", into=(2, d_proj, OFF_Z))
    d_glu = jnp.concatenate([d_glu_a, d_glu_b], axis=1)
    d_yg = _mm(d_glu, w_glu4, mode="nt", name="mm_d_gelu", tm=2048, tn=SSM_W, tk=512, b_blocked=True)
    g_w_glu = _mm(yg, d_glu, mode="tn", name="mm_g_w_glu", tm=SSM_W, tn=D_MODEL, tk=2048, out_dtype=_WIRE, out_blocked=True)
    dep = io.later_grads(dict(w_attn_proj=g_w_ap, w_glu=g_w_glu, w_ssm_proj=g_w_sp,
                              w_out=g_w_out.reshape(N_CHIPS, D_MODEL // N_CHIPS, D_MODEL)))

    d_proj, g_qw2, g_kw2, g_sk = _attn2_bwd(proj, qw2, kw2, sinks, lse, attn, d_ya_in, d_proj, deps=dep)
    dep = io.before_scan_backward([d_proj])
    (d_proj, g_bre, g_bim, g_cre, g_cim, g_dsk, g_abr, g_abi, g_cfr, g_cfi) = _ssm_bwd(
        _to_scan_order(d_yg), y_scan, u_scan, s_re, s_im, i_re, i_im, bre_blk, bim_blk, cre_blk, cim_blk, dsk, coef,
        d_proj, deps=dep)
    g_are, g_aim, g_ldt = _ssm_params_bwd(*flat_a, g_abr, g_abi, g_cfr, g_cfi)
    g_are, g_aim = g_are.reshape(N_GROUPS, STATE), g_aim.reshape(N_GROUPS, STATE)
    g_ldt = g_ldt.reshape(N_GROUPS, STATE).sum(axis=1)
    dep = io.before_input_projection_grad([d_proj]) + io.small_grads(dict(
        q_norm_w=g_qw2[0, :HEAD_DIM] + g_qw2[0, HEAD_DIM:], k_norm_w=g_kw2[0, :HEAD_DIM] + g_kw2[0, HEAD_DIM:],
        sinks=g_sk.reshape(N_Q_HEADS), A_re=g_are, A_im=g_aim, log_dt=g_ldt,
        B_re=_diag_of_b(g_bre), B_im=_diag_of_b(g_bim), C_re=_diag_of_c(g_cre), C_im=_diag_of_c(g_cim),
        D_skip=g_dsk.reshape(N_GROUPS, GROUP), b_glu=jnp.concatenate([g_bga, g_bgb], axis=1).reshape(D_MODEL)))
    g_w_in = _mm(h, d_proj, mode="tn", name="mm_g_w_in", tm=1024, tn=IN_W // 4, tk=1024, out_dtype=_WIRE,
                 out_blocked=True, deps=dep)
    dep = io.input_projection_grad(g_w_in)
    d_h = _mm(d_proj, w_in4, mode="nt", name="mm_d_h", tm=1024, tn=D_MODEL, tk=IN_W // 4, b_blocked=True, deps=dep)
    grad_x, g_nw = _rms_bwd(d_h, x, rstd, nw, dout)
    return loss, grad_x, g_nw.reshape(D_MODEL)


_SMALL = ["norm_w", "q_norm_w", "k_norm_w", "sinks", "A_re", "A_im", "log_dt", "B_re", "B_im", "C_re", "C_im",
          "D_skip", "b_glu"]
_BIG = ["w_in", "w_attn_proj", "w_glu", "w_ssm_proj", "w_out"]
_LATER = _BIG[1:]
_RELATIONS = ("flip_x", "flip_y", "flip_xy")
_ORDER = ["norm_w", "w_in", "q_norm_w", "k_norm_w", "sinks", "w_attn_proj", "A_re", "A_im", "log_dt", "B_re", "B_im",
          "C_re", "C_im", "D_skip", "w_glu", "b_glu", "w_ssm_proj", "w_out"]
_PACK_W = 1024


def _packed_rows(size):
    unit = SUBLANES * _PACK_W
    return -(-size // unit) * SUBLANES


def _pack_small(d, names):
    parts = []
    for n in names:
        flat = d[n].reshape(-1).astype(F32)
        rows = _packed_rows(flat.shape[0])
        parts.append(jnp.pad(flat, (0, rows * _PACK_W - flat.shape[0])).reshape(rows, _PACK_W))
    return jnp.concatenate(parts, axis=0)


def _unpack_small(packed, like, names):
    out, pos = {}, 0
    for n in names:
        rows = _packed_rows(like[n].size)
        out[n] = packed[pos:pos + rows].reshape(-1)[:like[n].size].reshape(like[n].shape)
        pos += rows
    return out


def _place_block(v, index_arr, *, name):
    rows, cols = v.shape

    def body(i_ref, v_ref, o_ref):
        o_ref[...] = v_ref[...]

    return pl.pallas_call(
        body, name=name,
        grid_spec=pltpu.PrefetchScalarGridSpec(
            num_scalar_prefetch=1, grid=(1,),
            in_specs=[pl.BlockSpec((rows, cols), lambda i, d: (0, 0))],
            out_specs=pl.BlockSpec((None, rows, cols), lambda i, d: (d[0], 0, 0))),
        out_shape=jax.ShapeDtypeStruct((8, rows, cols), v.dtype),
        compiler_params=_params(("arbitrary",)),
    )(index_arr, v)


def _plan_all_to_all(refs):
    (land,) = refs
    x, y, c, _ = _place()
    own = land.at[4 * x + 2 * y + c]
    copies = []
    for fx, fy, fc in [(0, 0, 1), (0, 1, 0), (0, 1, 1), (1, 0, 0), (1, 0, 1), (1, 1, 0), (1, 1, 1)]:
        px, py, pc = (1 - x) if fx else x, (1 - y) if fy else y, (1 - c) if fc else c
        copies.append((own, own, (px, py, pc), land.at[4 * px + 2 * py + pc]))
    return copies


def _adamw_whole(w, g, m, v, *, name):
    def body(w_ref, g_ref, m_ref, v_ref, d_ref, nm_ref, nv_ref):
        d_ref[...], nm_ref[...], nv_ref[...] = _adamw_math(w_ref[...], g_ref[...], m_ref[...], v_ref[...])

    return pl.pallas_call(body, name=name, out_shape=[jax.ShapeDtypeStruct(w.shape, F32)] * 3)(w, g, m, v)


class _Exchanges:
    def __init__(self, w, m, v):
        self.w, self.m, self.v = w, m, v
        self.grads, self.delta, self.new_m, self.new_v = {}, {}, {}, {}

    def _adamw(self, names, deps):
        for n in names:
            self.delta[n], self.new_m[n], self.new_v[n] = _adamw(
                self.w[n], self.grads[n], self.m[n], self.v[n], name=f"adamw_{n}", tm=128, deps=deps)

    def begin(self):
        chip = (2 * lax.axis_index("x") + lax.axis_index("y")).astype(jnp.int32).reshape(1)
        full = {n: _place_shard(self.w[n], chip, name=f"place_{n}") for n in _BIG}
        self.later_full = [full[n] for n in _LATER]
        self.w_in_sems, self.w_in_buf, token = _copies_start("gather_w_in_direct_start", [full["w_in"]],
                                                             _plan_relay_direct, 2)
        return [token]

    def projection(self, h):
        x, y = lax.axis_index("x"), lax.axis_index("y")
        blks = [jnp.asarray(b, jnp.int32).reshape(1)
                for b in (2 * x + y, 2 * (1 - x) + y, 2 * x + (1 - y), 2 * (1 - x) + (1 - y))]
        bufs = self.w_in_buf
        proj = _mm_chip_block(h, bufs[0], blks[0], None, name="mm_proj_own")
        relay, token = [], proj
        for k, tag in enumerate(_RELATIONS[:2]):
            bufs = _copies_wait(f"gather_w_in_direct_{tag}_wait", bufs, self.w_in_sems, _plan_relay_direct, [token],
                                which=(k,))
            plan = functools.partial(_plan_relay_forward, k=k)
            sems, bufs, token = _copies_start(f"gather_w_in_relay_{tag}_start", bufs, plan, 2)
            relay.append((sems, plan))
        self.rest = _copies_start("gather_ici_rest_start", self.later_full, _plan_gather_ici, 3 * len(_LATER),
                                  after=[token])
        token = self.rest[2]
        for k, tag in enumerate(_RELATIONS[:2]):
            bufs = _copies_wait(f"gather_w_in_hand_{tag}_wait", bufs, relay[k][0], relay[k][1], [token], which=(1,))
            token = proj = _mm_chip_block(h, bufs[0], blks[1 + k], proj, name=f"mm_proj_{tag}")
        for k, tag in enumerate(_RELATIONS[:2]):
            bufs = _copies_wait(f"gather_w_in_relay_{tag}_wait", bufs, relay[k][0], relay[k][1], [token], which=(0,))
        sems, bufs, token = _copies_start("gather_w_in_last_start", bufs, _plan_relay_last, 1)
        bufs = _copies_wait("gather_w_in_last_wait", bufs, sems, _plan_relay_last, [token])
        proj = _mm_chip_block(h, bufs[0], blks[3], proj, name="mm_proj_flip_xy")
        return proj, bufs[0]

    def weight(self, name, after):
        if self.rest is not None:
            sems, bufs = self.rest
            later = dict(zip(_LATER, _copies_wait("gather_d2d_rest_wait", bufs, sems, _plan_gather_d2d, [after])))
            later["w_out"] = later["w_out"].reshape(D_MODEL, D_MODEL)
            self.later, self.rest = later, None
        return self.later[name]

    def after_proj(self, proj):
        sems, bufs, _ = self.rest
        bufs = _copies_wait("gather_ici_rest_wait", bufs, sems, _plan_gather_ici, [proj])
        sems, bufs, token = _copies_start("gather_d2d_rest_start", bufs, _plan_gather_d2d, 3 * len(_LATER))
        self.rest = (sems, bufs)
        return [token]

    def later_grads(self, grads):
        self.rs_later = _ReduceScatter("later", _LATER, [grads[n] for n in _LATER])
        return self.rs_later.start_swap()

    def before_scan_backward(self, after):
        return self.rs_later.start_scatter(after)

    def before_input_projection_grad(self, after):
        return self.rs_later.start_join(after)

    def input_projection_grad(self, g_w_in):
        self.grads.update(self.rs_later.finish([g_w_in]))
        self.rs_in = _ReduceScatter("w_in", ["w_in"], [g_w_in])
        self._adamw(_LATER, self.rs_in.start_swap())
        return self.rs_in.start_scatter([self.delta[n] for n in _LATER])

    def _adamw_small(self, names):
        for n in names:
            self.delta[n], self.new_m[n], self.new_v[n] = _adamw_whole(
                self.w[n], self.grads[n], self.m[n], self.v[n], name=f"adamw_{n}")

    def small_grads(self, grads):
        me = (4 * lax.axis_index("x") + 2 * lax.axis_index("y") + lax.axis_index("c")).astype(jnp.int32).reshape(1)
        land = _place_block(_pack_small(grads, _SMALL[1:]), me, name="place_small_grads")
        self.small = _copies_start("gather_small_start", [land], _plan_all_to_all, 7)
        return [self.small[2]]

    def finish(self, g_norm_w, loss, after):
        join = self.rs_in.start_join(after)
        sems, bufs, _ = self.small
        (land,) = _copies_wait("gather_small_wait", bufs, sems, _plan_all_to_all, join)
        self.grads.update(_unpack_small(_sum8(land, name="sum_small_grads"), self.w, _SMALL[1:]))
        self._adamw_small(_SMALL[1:])
        rows = _packed_rows(g_norm_w.size)
        late = jnp.concatenate([_pack_small(dict(norm_w=g_norm_w), _SMALL[:1]),
                                jnp.pad(loss.reshape(1, 1), ((0, SUBLANES - 1), (0, _PACK_W - 1)))], axis=0)
        late = _sum8(_all_gather_small(late), name="sum_norm_w_grad_and_loss")
        self.grads.update(_unpack_small(late[:rows], self.w, _SMALL[:1]))
        self._adamw_small(_SMALL[:1])
        self.grads.update(self.rs_in.finish([self.delta[_SMALL[0]]]))
        self._adamw(["w_in"], ())
        return late[rows, 0]


def kernel(x, norm_w, w_in, q_norm_w, k_norm_w, sinks, w_attn_proj, A_re, A_im, log_dt, B_re, B_im, C_re, C_im, D_skip, w_glu, b_glu, w_ssm_proj, w_out, loss_target, m_norm_w, m_w_in, m_q_norm_w, m_k_norm_w, m_sinks, m_w_attn_proj, m_A_re, m_A_im, m_log_dt, m_B_re, m_B_im, m_C_re, m_C_im, m_D_skip, m_w_glu, m_b_glu, m_w_ssm_proj, m_w_out, v_norm_w, v_w_in, v_q_norm_w, v_k_norm_w, v_sinks, v_w_attn_proj, v_A_re, v_A_im, v_log_dt, v_B_re, v_B_im, v_C_re, v_C_im, v_D_skip, v_w_glu, v_b_glu, v_w_ssm_proj, v_w_out):
    w = dict(norm_w=norm_w, w_in=w_in, q_norm_w=q_norm_w, k_norm_w=k_norm_w, sinks=sinks, w_attn_proj=w_attn_proj,
             A_re=A_re, A_im=A_im, log_dt=log_dt, B_re=B_re, B_im=B_im, C_re=C_re, C_im=C_im, D_skip=D_skip,
             w_glu=w_glu, b_glu=b_glu, w_ssm_proj=w_ssm_proj, w_out=w_out)
    m = dict(norm_w=m_norm_w, w_in=m_w_in, q_norm_w=m_q_norm_w, k_norm_w=m_k_norm_w, sinks=m_sinks,
             w_attn_proj=m_w_attn_proj, A_re=m_A_re, A_im=m_A_im, log_dt=m_log_dt, B_re=m_B_re, B_im=m_B_im,
             C_re=m_C_re, C_im=m_C_im, D_skip=m_D_skip, w_glu=m_w_glu, b_glu=m_b_glu, w_ssm_proj=m_w_ssm_proj,
             w_out=m_w_out)
    v = dict(norm_w=v_norm_w, w_in=v_w_in, q_norm_w=v_q_norm_w, k_norm_w=v_k_norm_w, sinks=v_sinks,
             w_attn_proj=v_w_attn_proj, A_re=v_A_re, A_im=v_A_im, log_dt=v_log_dt, B_re=v_B_re, B_im=v_B_im,
             C_re=v_C_re, C_im=v_C_im, D_skip=v_D_skip, w_glu=v_w_glu, b_glu=v_b_glu, w_ssm_proj=v_w_ssm_proj,
             w_out=v_w_out)

    io = _Exchanges(w, m, v)
    loss, grad_x, g_norm_w = _local_step(x[0], loss_target[0], norm_w, q_norm_w, k_norm_w, sinks, A_re, A_im, log_dt,
                                         B_re, B_im, C_re, C_im, D_skip, b_glu, io)
    loss = io.finish(g_norm_w, loss, [grad_x])
    grads, delta, new_m, new_v = io.grads, io.delta, io.new_m, io.new_v

    return (loss, grad_x[None], *[grads[n] for n in _ORDER], *[delta[n] for n in _ORDER],
            *[new_m[n] for n in _ORDER], *[new_v[n] for n in _ORDER])
```

```python
import functools
import math

import jax
import jax.numpy as jnp
from jax import lax
from jax.experimental import pallas as pl
from jax.experimental.pallas import tpu as pltpu

F32 = jnp.float32
_MXU = jnp.bfloat16
_WIRE = jnp.bfloat16

LANES = 128
SUBLANES = 8
VMEM_LIMIT = 56 * 1024 * 1024

D_MODEL = 2048
HEAD_DIM = 64
N_Q_HEADS = 16
N_KV_HEADS = 4
Q_PER_KV = 4
ATTN_W = 1024
KV_W = 256
WINDOW = 128
SSM_W = 1024
GROUP = 16
N_GROUPS = 64
STATE = 64
N_STATES = N_GROUPS * STATE
IN_W = 8704
NORM_EPS = 1e-6
N_CHIPS = 4
CW = 512
OFF_AGATE, OFF_U, OFF_Z, OFF_GA, OFF_GS = 3, 5, 7, 9, 13

SSM_T = 256
SSM_L = SSM_T // SUBLANES
SSM_JB = 8
SSM_SB = N_STATES // SSM_JB

ADAM_LR, ADAM_B1, ADAM_B2, ADAM_EPS, ADAM_WD, ADAM_STEP = 0.001, 0.9, 0.999, 1e-08, 0.01, 10

MESH = pl.DeviceIdType.MESH
_ANY = pl.BlockSpec(memory_space=pl.ANY)


def _params(sem=None):
    return pltpu.CompilerParams(dimension_semantics=sem, vmem_limit_bytes=VMEM_LIMIT)


def _mm(a, b, *, mode, name, tm, tn, tk, out_dtype=F32, b_blocked=False, out_blocked=False, rows_outer=False,
        deps=()):
    nd = len(deps)
    if mode == "tn":
        K, M = a.shape
    else:
        M, K = a.shape
    if mode == "nn":
        N = b.shape[0] * b.shape[2] if b_blocked else b.shape[1]
    elif mode == "nt":
        N = b.shape[1] if b_blocked else b.shape[0]
    else:
        N = b.shape[1]
    tm, tn, tk = min(tm, M), min(tn, N), min(tk, K)
    nj, ni, nk = N // tn, M // tm, K // tk
    assert nj * tn == N and ni * tm == M and nk * tk == K, (name, M, N, K)
    dims = {"nn": (((1,), (0,)), ((), ())), "nt": (((1,), (1,)), ((), ())), "tn": (((0,), (0,)), ((), ()))}[mode]

    if mode == "tn":
        a_spec = pl.BlockSpec((tk, tm), lambda j, i, k: (k, i))
    else:
        a_spec = pl.BlockSpec((tm, tk), lambda j, i, k: (i, k))
    if mode == "nn":
        if b_blocked:
            assert b.shape[0] == nj and b.shape[2] == tn
            b_spec = pl.BlockSpec((None, tk, tn), lambda j, i, k: (j, k, 0))
        else:
            b_spec = pl.BlockSpec((tk, tn), lambda j, i, k: (k, j))
    elif mode == "nt":
        if b_blocked:
            assert b.shape[0] == nk and b.shape[2] == tk
            b_spec = pl.BlockSpec((None, tn, tk), lambda j, i, k: (k, j, 0))
        else:
            b_spec = pl.BlockSpec((tn, tk), lambda j, i, k: (j, k))
    else:
        b_spec = pl.BlockSpec((tk, tn), lambda j, i, k: (k, j))
    whole_out = out_blocked and nj == 1
    if whole_out:
        assert ni == 1
        o_spec = pl.BlockSpec((N_CHIPS, tm, tn // N_CHIPS), lambda j, i, k: (0, 0, 0))
        o_shape = jax.ShapeDtypeStruct((N_CHIPS, M, tn // N_CHIPS), out_dtype)
    elif out_blocked:
        assert nj == N_CHIPS
        o_spec = pl.BlockSpec((None, tm, tn), lambda j, i, k: (j, i, 0))
        o_shape = jax.ShapeDtypeStruct((nj, M, tn), out_dtype)
    else:
        o_spec = pl.BlockSpec((tm, tn), lambda j, i, k: (i, j))
        o_shape = jax.ShapeDtypeStruct((M, N), out_dtype)
    use_acc = nk > 1 and (out_dtype != F32 or whole_out)

    def body(a_ref, b_ref, *rest):
        o_ref, scratch = rest[nd], rest[nd + 1:]

        def product():
            return lax.dot_general(a_ref[...].astype(_MXU), b_ref[...].astype(_MXU), dims, preferred_element_type=F32)

        def write(result):
            if whole_out:
                w = tn // N_CHIPS
                for c in range(N_CHIPS):
                    o_ref[c] = result[:, c * w:(c + 1) * w].astype(o_ref.dtype)
            else:
                o_ref[...] = result.astype(o_ref.dtype)

        if nk == 1:
            write(product())
            return
        k = pl.program_id(2)
        acc = scratch[0] if use_acc else o_ref

        @pl.when(k == 0)
        def _():
            acc[...] = jnp.zeros_like(acc)

        acc[...] += product()

        if use_acc:
            @pl.when(k == nk - 1)
            def _():
                write(acc[...])

    specs = [a_spec, b_spec, o_spec]
    grid = (nj, ni, nk)
    if rows_outer:
        specs = [pl.BlockSpec(s.block_shape, lambda i, j, k, f=s.index_map: f(j, i, k)) for s in specs]
        grid = (ni, nj, nk)
    return pl.pallas_call(
        body, name=name, grid=grid, in_specs=specs[:2] + [_ANY] * nd, out_specs=specs[2],
        out_shape=o_shape, scratch_shapes=[pltpu.VMEM((tm, tn), F32)] if use_acc else [],
        compiler_params=_params(("parallel", "parallel", "arbitrary")),
    )(a, b, *deps)


def _mm_chip_block(a, b4, blk, prev, *, name, tm=512, deps=()):
    M, K = a.shape
    nchip, _, C = b4.shape
    tm = min(tm, M)
    extra = ([] if prev is None else [prev]) + list(deps)

    def body(blk_ref, a_ref, b_ref, *rest):
        rest[-1][...] = jnp.dot(a_ref[...].astype(_MXU), b_ref[...].astype(_MXU), preferred_element_type=F32)

    return pl.pallas_call(
        body, name=name,
        grid_spec=pltpu.PrefetchScalarGridSpec(
            num_scalar_prefetch=1, grid=(M // tm,),
            in_specs=[pl.BlockSpec((tm, K), lambda i, c: (i, 0)), pl.BlockSpec((None, K, C), lambda i, c: (c[0], 0, 0))]
            + [_ANY] * len(extra),
            out_specs=pl.BlockSpec((tm, C), lambda i, c: (i, c[0]))),
        out_shape=jax.ShapeDtypeStruct((M, nchip * C), F32),
        input_output_aliases={} if prev is None else {3: 0},
        compiler_params=_params(("arbitrary",)),
    )(blk, a, b4, *extra)


def _mm_out_loss(merged, w_out, x, target, *, tm=256):
    rows, d = x.shape

    def body(m_ref, w_ref, x_ref, t_ref, d_ref, db_ref, sq_ref):
        mo = jnp.dot(m_ref[...].astype(_MXU), w_ref[...].astype(_MXU), preferred_element_type=F32)
        err = (x_ref[...] + mo) - t_ref[...]
        dout = err * (1.0 / d)
        d_ref[...] = dout
        db_ref[...] = dout.astype(db_ref.dtype)
        part = _colsum(err * err)
        i = pl.program_id(0)

        @pl.when(i == 0)
        def _():
            sq_ref[...] = part

        @pl.when(i > 0)
        def _():
            sq_ref[...] += part

    tile = pl.BlockSpec((tm, d), lambda i: (i, 0))
    return pl.pallas_call(
        body, name="mm_out_loss", grid=(rows // tm,),
        in_specs=[tile, pl.BlockSpec((d, d), lambda i: (0, 0)), tile, tile],
        out_specs=[tile, tile, pl.BlockSpec((1, d), lambda i: (0, 0))],
        out_shape=[jax.ShapeDtypeStruct((rows, d), F32), jax.ShapeDtypeStruct((rows, d), _MXU),
                   jax.ShapeDtypeStruct((1, d), F32)],
        compiler_params=_params(("arbitrary",)),
    )(merged, w_out, x, target)


def _mm_merge_bwd(dout_b, w_out, proj, y_a, y_s, *, tm=256):
    rows, d = y_a.shape
    ncol = d // CW

    def body(do_ref, w_ref, *refs):
        ga_refs, gs_refs = refs[:ncol], refs[ncol:2 * ncol]
        ya_ref, ys_ref, dya_ref, dys_ref, dg_ref = refs[2 * ncol:]
        dm = lax.dot_general(do_ref[...].astype(_MXU), w_ref[...].astype(_MXU), _NT, preferred_element_type=F32)
        for j in range(ncol):
            cols = slice(j * CW, (j + 1) * CW)
            dmj = dm[:, cols]
            sa, ss = _sigmoid(ga_refs[j][...]), _sigmoid(gs_refs[j][...])
            dya_ref[:, cols] = (sa * dmj).astype(dya_ref.dtype)
            dys_ref[:, cols] = (ss * dmj).astype(dys_ref.dtype)
            dg_ref[:, cols] = (dmj * ya_ref[:, cols] * sa * (1.0 - sa)).astype(dg_ref.dtype)
            dg_ref[:, d + j * CW:d + (j + 1) * CW] = (dmj * ys_ref[:, cols] * ss * (1.0 - ss)).astype(dg_ref.dtype)

    tile = pl.BlockSpec((tm, d), lambda i: (i, 0))
    gate = [pl.BlockSpec((tm, CW), lambda i, c=off + j: (i, c)) for off in (OFF_GA, OFF_GS) for j in range(ncol)]
    both = pl.BlockSpec((pl.Element(tm), pl.Element(2 * d)), lambda i: (i * tm, OFF_GA * CW))
    return pl.pallas_call(
        body, name="mm_merge_bwd", grid=(rows // tm,),
        in_specs=[tile, pl.BlockSpec((d, d), lambda i: (0, 0))] + gate + [tile, tile],
        out_specs=[tile, tile, both],
        out_shape=[jax.ShapeDtypeStruct((rows, d), _MXU)] * 2 + [jax.ShapeDtypeStruct((rows, IN_W), _MXU)],
        compiler_params=_params(("arbitrary",)),
    )(dout_b, w_out, *([proj] * (2 * ncol)), y_a, y_s)


def _mm_ssm_gate_bwd(d_ys, w_sp4, glu, b_glu, proj, d_proj, *, tm=512):
    rows, w = glu.shape[0], glu.shape[1] // 2
    nk, tk = w_sp4.shape[0], w_sp4.shape[2]
    tm = min(tm, rows)

    def body(dy_ref, w_ref, ga_ref, gb_ref, ba_ref, bb_ref, z0_ref, z1_ref, buf_ref, dg_ref, dz_ref, db_ref, acc):
        i, k = pl.program_id(0), pl.program_id(1)

        @pl.when(k == 0)
        def _():
            acc[...] = jnp.zeros_like(acc)

        acc[...] += lax.dot_general(dy_ref[...].astype(_MXU), w_ref[...].astype(_MXU), _NT, preferred_element_type=F32)

        @pl.when(k == nk - 1)
        def _():
            dv = acc[...]
            a, sb = ga_ref[...] + ba_ref[...], _sigmoid(gb_ref[...] + bb_ref[...])
            f, df = _silu_and_grad(jnp.concatenate([z0_ref[...], z1_ref[...]], axis=1))
            dga = dv * sb * f
            dgb = dv * a * f * sb * (1.0 - sb)
            dg_ref[:, :w] = dga.astype(dg_ref.dtype)
            dg_ref[:, w:] = dgb.astype(dg_ref.dtype)
            dz_ref[...] = (dv * a * sb * df).astype(dz_ref.dtype)
            part = jnp.concatenate([_colsum(dga), _colsum(dgb)], axis=1)

            @pl.when(i == 0)
            def _():
                db_ref[...] = part

            @pl.when(i > 0)
            def _():
                db_ref[...] += part

    half = lambda c: pl.BlockSpec((tm, w), lambda i, k: (i, c))
    bias = lambda c: pl.BlockSpec((1, w), lambda i, k: (0, c))
    zcol = lambda c: pl.BlockSpec((tm, CW), lambda i, k: (i, OFF_Z + c))
    return pl.pallas_call(
        body, name="mm_ssm_gate_bwd", grid=(rows // tm, nk),
        in_specs=[pl.BlockSpec((tm, tk), lambda i, k: (i, k)), pl.BlockSpec((None, w, tk), lambda i, k: (k, 0, 0)),
                  half(0), half(1), bias(0), bias(1), zcol(0), zcol(1), _ANY],
        out_specs=[pl.BlockSpec((tm, 2 * w), lambda i, k: (i, 0)),
                   pl.BlockSpec((pl.Element(tm), pl.Element(w)), lambda i, k: (i * tm, OFF_Z * CW)),
                   pl.BlockSpec((1, 2 * w), lambda i, k: (0, 0))],
        out_shape=[jax.ShapeDtypeStruct((rows, 2 * w), _MXU), jax.ShapeDtypeStruct(d_proj.shape, d_proj.dtype),
                   jax.ShapeDtypeStruct((1, 2 * w), F32)],
        input_output_aliases={8: 1},
        scratch_shapes=[pltpu.VMEM((tm, w), F32)],
        compiler_params=_params(("arbitrary", "arbitrary")),
    )(d_ys, w_sp4, glu, glu, b_glu, b_glu, proj, proj, d_proj)


def _ew(fn, ins, outs, *, rows, ncol, name, n_acc=0, tm=512, deps=(), into=None):
    deps = list(deps) + ([into[1]] if into else [])
    n_in, n_out, nd = len(ins), len(outs), len(deps)
    tm = min(tm, rows)
    in_specs = []
    for _, kind, col0 in ins:
        if kind == "mat":
            in_specs.append(pl.BlockSpec((tm, CW), lambda j, i, c0=col0: (i, c0 + j)))
        else:
            in_specs.append(pl.BlockSpec((1, CW), lambda j, i, c0=col0: (0, c0 + j)))
    out_specs = [pl.BlockSpec((tm, CW), lambda j, i: (i, j)) for _ in outs]
    out_shape = [jax.ShapeDtypeStruct((rows, w), dt) for w, dt in outs]
    if into:
        out_specs[into[0]] = pl.BlockSpec((tm, CW), lambda j, i, c0=into[2]: (i, c0 + j))
        out_shape[into[0]] = jax.ShapeDtypeStruct(into[1].shape, into[1].dtype)
    for _ in range(n_acc):
        out_specs.append(pl.BlockSpec((1, CW), lambda j, i: (0, j)))
        out_shape.append(jax.ShapeDtypeStruct((1, ncol * CW), F32))

    def body(*refs):
        vals = fn(*[r[...] for r in refs[:n_in]])
        refs = refs[n_in + nd:]
        for r, v in zip(refs[:n_out], vals[:n_out]):
            r[...] = v.astype(r.dtype)
        i = pl.program_id(1)
        for r, v in zip(refs[n_out:], vals[n_out:]):
            @pl.when(i == 0)
            def _(r=r, v=v):
                r[...] = v

            @pl.when(i > 0)
            def _(r=r, v=v):
                r[...] += v

    res = pl.pallas_call(
        body, name=name, grid=(ncol, rows // tm), in_specs=in_specs + [_ANY] * nd, out_specs=out_specs,
        out_shape=out_shape, input_output_aliases={n_in + nd - 1: into[0]} if into else {},
        compiler_params=_params(("parallel", "arbitrary")),
    )(*[a for a, _, _ in ins], *deps)
    return res


def _colsum(v):
    return jnp.sum(v, axis=0, keepdims=True)


def _sigmoid(v):
    return jax.nn.sigmoid(v)


def _silu_and_grad(v):
    s = _sigmoid(v)
    return v * s, s * (1.0 + v * (1.0 - s))


def _rms_fwd(x, w, *, tm=512, deps=()):
    rows, d = x.shape
    nd = len(deps)

    def body(x_ref, w_ref, *rest):
        h_ref, r_ref = rest[nd:]
        xv = x_ref[...]
        r = lax.rsqrt(jnp.mean(xv * xv, axis=-1, keepdims=True) + NORM_EPS)
        h_ref[...] = (xv * r * w_ref[...]).astype(h_ref.dtype)
        r_ref[...] = r

    return pl.pallas_call(
        body, name="rms_fwd", grid=(rows // tm,),
        in_specs=[pl.BlockSpec((tm, d), lambda i: (i, 0)), pl.BlockSpec((1, d), lambda i: (0, 0))] + [_ANY] * nd,
        out_specs=[pl.BlockSpec((tm, d), lambda i: (i, 0)), pl.BlockSpec((tm, 1), lambda i: (i, 0))],
        out_shape=[jax.ShapeDtypeStruct((rows, d), _MXU), jax.ShapeDtypeStruct((rows, 1), F32)],
        compiler_params=_params(("arbitrary",)),
    )(x, w, *deps)


def _rms_bwd(dh, x, rstd, w, dout, *, tm=256):
    rows, d = x.shape

    def body(dh_ref, x_ref, r_ref, w_ref, do_ref, gx_ref, gw_ref):
        dhv, xv, r, wv = dh_ref[...], x_ref[...], r_ref[...], w_ref[...]
        xr = xv * r
        t = jnp.mean(dhv * wv * xr, axis=-1, keepdims=True)
        gx_ref[...] = do_ref[...] + r * (wv * dhv - xr * t)
        part = _colsum(dhv * xr)
        i = pl.program_id(0)

        @pl.when(i == 0)
        def _():
            gw_ref[...] = part

        @pl.when(i > 0)
        def _():
            gw_ref[...] += part

    return pl.pallas_call(
        body, name="rms_bwd", grid=(rows // tm,),
        in_specs=[pl.BlockSpec((tm, d), lambda i: (i, 0)), pl.BlockSpec((tm, d), lambda i: (i, 0)),
                  pl.BlockSpec((tm, 1), lambda i: (i, 0)), pl.BlockSpec((1, d), lambda i: (0, 0)),
                  pl.BlockSpec((tm, d), lambda i: (i, 0))],
        out_specs=[pl.BlockSpec((tm, d), lambda i: (i, 0)), pl.BlockSpec((1, d), lambda i: (0, 0))],
        out_shape=[jax.ShapeDtypeStruct((rows, d), F32), jax.ShapeDtypeStruct((1, d), F32)],
        compiler_params=_params(("arbitrary",)),
    )(dh, x, rstd, w, dout)


_NT = (((1,), (1,)), ((), ()))
_TN = (((0,), (0,)), ((), ()))


QKV_W = ATTN_W + 2 * KV_W
HEADS_PER_TILE = LANES // HEAD_DIM


def _low_half(rows):
    return lax.broadcasted_iota(jnp.int32, (rows, LANES), 1) < HEAD_DIM


def _pair_mean(t, low):
    m_lo = jnp.sum(jnp.where(low, t, 0.0), axis=-1, keepdims=True)
    m_hi = jnp.sum(jnp.where(low, 0.0, t), axis=-1, keepdims=True)
    return jnp.where(low, m_lo, m_hi) * (1.0 / HEAD_DIM)


def _pair_rstd(t, low):
    return lax.rsqrt(_pair_mean(t * t, low) + NORM_EPS)


def _dup_half(t, hi, low):
    swapped = pltpu.roll(t, HEAD_DIM, 1)
    return jnp.where(low, swapped, t) if hi else jnp.where(low, t, swapped)


def _fold_halves(t):
    return t + pltpu.roll(t, HEAD_DIM, 1)


def _split_heads(t, low):
    return [jnp.where(low, t, 0.0), jnp.where(low, 0.0, t)]


def _stacked_band_mask(n):
    rows = Q_PER_KV * WINDOW
    qi = lax.broadcasted_iota(jnp.int32, (rows, 2 * WINDOW), 0) % WINDOW + WINDOW
    kj = lax.broadcasted_iota(jnp.int32, (rows, 2 * WINDOW), 1)
    diff = qi - kj
    first_key = jnp.where(n > 0, 0, WINDOW)
    return (diff >= 0) & (diff < WINDOW) & (kj >= first_key)


def _stacked_sinks(sink_ref, g):
    blk = lax.broadcasted_iota(jnp.int32, (Q_PER_KV * WINDOW, 1), 0) // WINDOW
    col = jnp.full((Q_PER_KV * WINDOW, 1), sink_ref[Q_PER_KV * g], F32)
    for r in range(1, Q_PER_KV):
        col = jnp.where(blk == r, sink_ref[Q_PER_KV * g + r], col)
    return col


def _attn_in_specs(nblk, rev):
    def cur(n):
        return (nblk - 1 - n) if rev else n

    q_spec = pl.BlockSpec((WINDOW, ATTN_W), lambda n: (cur(n), 0))
    kvc_spec = pl.BlockSpec((WINDOW, 2 * KV_W), lambda n: (cur(n), ATTN_W // (2 * KV_W)))
    kvp_spec = pl.BlockSpec((WINDOW, 2 * KV_W), lambda n: (jnp.maximum(cur(n) - 1, 0), ATTN_W // (2 * KV_W)))
    w_spec = pl.BlockSpec((1, LANES), lambda n: (0, 0))
    l_spec = pl.BlockSpec((WINDOW, N_Q_HEADS), lambda n: (cur(n), 0))
    gate_specs = [pl.BlockSpec((WINDOW, CW), lambda n, col=OFF_AGATE + j: (cur(n), col)) for j in range(ATTN_W // CW)]
    return q_spec, kvc_spec, kvp_spec, w_spec, l_spec, gate_specs


def _attn2_fwd(proj, qw2, kw2, sinks, deps=()):
    seq = proj.shape[0]
    nblk = seq // WINDOW
    scale = 1.0 / math.sqrt(HEAD_DIM)
    q_spec, kvc_spec, kvp_spec, w_spec, l_spec, gate_specs = _attn_in_specs(nblk, False)
    nd, ng = len(deps), len(gate_specs)

    def body(sink_ref, q_ref, kvc_ref, kvp_ref, qw_ref, kw_ref, *rest):
        gate_refs = rest[:ng]
        o_ref, lse_ref, ya_ref = rest[ng + nd:]
        n = pl.program_id(0)
        low, low2 = _low_half(WINDOW), _low_half(2 * WINDOW)
        valid = _stacked_band_mask(n)
        head_lane = lax.broadcasted_iota(jnp.int32, (WINDOW, N_Q_HEADS), 1)
        kv = jnp.concatenate([kvp_ref[...], kvc_ref[...]], axis=0)
        qwv, kwv = qw_ref[...], kw_ref[...]
        lse_blk = jnp.zeros((WINDOW, N_Q_HEADS), F32)
        for t in range(N_KV_HEADS // HEADS_PER_TILE):
            kt = kv[:, t * LANES:(t + 1) * LANES]
            vt = kv[:, KV_W + t * LANES:KV_W + (t + 1) * LANES]
            kn = kt * _pair_rstd(kt, low2) * kwv
            for hi in range(HEADS_PER_TILE):
                g = HEADS_PER_TILE * t + hi
                kdup = _dup_half(kn, hi, low2).astype(_MXU)
                vdup = _dup_half(vt, hi, low2).astype(_MXU)
                stack = []
                for tq in (2 * g, 2 * g + 1):
                    qt = q_ref[:, tq * LANES:(tq + 1) * LANES]
                    stack += _split_heads(qt * _pair_rstd(qt, low) * qwv, low)
                qs = jnp.concatenate(stack, axis=0).astype(_MXU)
                s = lax.dot_general(qs, kdup, _NT, preferred_element_type=F32) * scale
                s = jnp.where(valid, s, -1e30)
                sink = _stacked_sinks(sink_ref, g)
                m = jnp.maximum(jnp.max(s, axis=-1, keepdims=True), sink)
                e = jnp.exp(s - m)
                z = jnp.sum(e, axis=-1, keepdims=True) + jnp.exp(sink - m)
                o = jnp.dot((e / z).astype(_MXU), vdup, preferred_element_type=F32)
                for i, tq in enumerate((2 * g, 2 * g + 1)):
                    o_ref[:, tq * LANES:(tq + 1) * LANES] = jnp.where(
                        low, o[2 * i * WINDOW:(2 * i + 1) * WINDOW], o[(2 * i + 1) * WINDOW:(2 * i + 2) * WINDOW])
                lse = m + jnp.log(z)
                for r in range(Q_PER_KV):
                    lse_blk = jnp.where(head_lane == Q_PER_KV * g + r, lse[r * WINDOW:(r + 1) * WINDOW], lse_blk)
        lse_ref[...] = lse_blk
        for j, g_ref in enumerate(gate_refs):
            cols = slice(j * CW, (j + 1) * CW)
            gate = g_ref[...]
            ya_ref[:, cols] = (o_ref[:, cols] * (gate * _sigmoid(gate))).astype(ya_ref.dtype)

    return pl.pallas_call(
        body, name="attn_fwd", grid=(nblk,),
        in_specs=[pl.BlockSpec(memory_space=pltpu.SMEM), q_spec, kvc_spec, kvp_spec, w_spec, w_spec] + gate_specs
        + [_ANY] * nd,
        out_specs=[q_spec, l_spec, q_spec],
        out_shape=[jax.ShapeDtypeStruct((seq, ATTN_W), F32), jax.ShapeDtypeStruct((seq, N_Q_HEADS), F32),
                   jax.ShapeDtypeStruct((seq, ATTN_W), _MXU)],
        compiler_params=_params(("arbitrary",)),
    )(sinks, proj, proj, proj, qw2, kw2, *([proj] * ng), *deps)


def _attn2_bwd(proj, qw2, kw2, sinks, lse, attn, dya, d_proj, deps=()):
    seq = proj.shape[0]
    nblk = seq // WINDOW
    scale = 1.0 / math.sqrt(HEAD_DIM)
    q_spec, kvc_spec, kvp_spec, w_spec, l_spec, gate_specs = _attn_in_specs(nblk, True)
    s_spec = pl.BlockSpec((1, N_Q_HEADS), lambda n: (0, 0))
    d_spec = pl.BlockSpec((WINDOW, QKV_W + ATTN_W), lambda n: (nblk - 1 - n, 0))
    deps = list(deps) + [d_proj]
    nd, ng = len(deps), len(gate_specs)

    def body(sink_ref, q_ref, kvc_ref, kvp_ref, qw_ref, kw_ref, lse_ref, attn_ref, dya_ref, *rest):
        gate_refs = rest[:ng]
        d_ref, dqw_ref, dkw_ref, dsk_ref, carry, do_ref = rest[ng + nd:]
        step = pl.program_id(0)
        n = nblk - 1 - step

        @pl.when(step == 0)
        def _():
            carry[...] = jnp.zeros_like(carry)
            dqw_ref[...] = jnp.zeros_like(dqw_ref)
            dkw_ref[...] = jnp.zeros_like(dkw_ref)
            dsk_ref[...] = jnp.zeros_like(dsk_ref)

        for j, g_ref in enumerate(gate_refs):
            cols = slice(j * CW, (j + 1) * CW)
            f, df = _silu_and_grad(g_ref[...])
            dv = dya_ref[:, cols]
            do_ref[:, cols] = dv * f
            d_ref[:, QKV_W + j * CW:QKV_W + (j + 1) * CW] = (dv * attn_ref[:, cols] * df).astype(d_ref.dtype)

        low, low2 = _low_half(WINDOW), _low_half(2 * WINDOW)
        valid = _stacked_band_mask(n)
        head_lane = lax.broadcasted_iota(jnp.int32, (WINDOW, N_Q_HEADS), 1)
        sink_lane = lax.broadcasted_iota(jnp.int32, (1, N_Q_HEADS), 1)
        kv = jnp.concatenate([kvp_ref[...], kvc_ref[...]], axis=0)
        qwv, kwv = qw_ref[...], kw_ref[...]
        lse_blk = lse_ref[...]
        dqw = jnp.zeros((1, LANES), F32)
        dkw = jnp.zeros((1, LANES), F32)
        dsk = jnp.zeros((1, N_Q_HEADS), F32)
        for t in range(N_KV_HEADS // HEADS_PER_TILE):
            kt = kv[:, t * LANES:(t + 1) * LANES]
            vt = kv[:, KV_W + t * LANES:KV_W + (t + 1) * LANES]
            rk = _pair_rstd(kt, low2)
            kn = kt * rk * kwv
            dkn_t = jnp.zeros((2 * WINDOW, LANES), F32)
            dv_t = jnp.zeros((2 * WINDOW, LANES), F32)
            for hi in range(HEADS_PER_TILE):
                g = HEADS_PER_TILE * t + hi
                kdup = _dup_half(kn, hi, low2).astype(_MXU)
                vdup = _dup_half(vt, hi, low2).astype(_MXU)
                tiles = (2 * g, 2 * g + 1)
                qx, rq, stack, dstack, lse_rows = [], [], [], [], []
                for tq in tiles:
                    qt = q_ref[:, tq * LANES:(tq + 1) * LANES]
                    r = _pair_rstd(qt, low)
                    rq.append(r)
                    qx.append(qt * r)
                    stack += _split_heads(qx[-1] * qwv, low)
                    dstack += _split_heads(do_ref[:, tq * LANES:(tq + 1) * LANES], low)
                for r in range(Q_PER_KV):
                    lse_rows.append(jnp.sum(jnp.where(head_lane == Q_PER_KV * g + r, lse_blk, 0.0), axis=-1, keepdims=True))
                qs = jnp.concatenate(stack, axis=0).astype(_MXU)
                dos = jnp.concatenate(dstack, axis=0).astype(_MXU)
                lse_col = jnp.concatenate(lse_rows, axis=0)
                s = lax.dot_general(qs, kdup, _NT, preferred_element_type=F32) * scale
                s = jnp.where(valid, s, -1e30)
                p = jnp.exp(s - lse_col)
                dp = lax.dot_general(dos, vdup, _NT, preferred_element_type=F32)
                dsum = jnp.sum(p * dp, axis=-1, keepdims=True)
                ds = (p * (dp - dsum) * scale).astype(_MXU)
                dsink = -jnp.exp(_stacked_sinks(sink_ref, g) - lse_col) * dsum
                for r in range(Q_PER_KV):
                    dsk = dsk + jnp.where(sink_lane == Q_PER_KV * g + r, _colsum(dsink[r * WINDOW:(r + 1) * WINDOW]), 0.0)
                dv_g = _fold_halves(lax.dot_general(p.astype(_MXU), dos, _TN, preferred_element_type=F32))
                dkn_g = _fold_halves(lax.dot_general(ds, qs, _TN, preferred_element_type=F32))
                dv_t = jnp.where(low2, dv_t, dv_g) if hi else jnp.where(low2, dv_g, dv_t)
                dkn_t = jnp.where(low2, dkn_t, dkn_g) if hi else jnp.where(low2, dkn_g, dkn_t)
                dqn = jnp.dot(ds, kdup, preferred_element_type=F32)
                for i, tq in enumerate(tiles):
                    dqn_t = jnp.where(low, dqn[2 * i * WINDOW:(2 * i + 1) * WINDOW],
                                      dqn[(2 * i + 1) * WINDOW:(2 * i + 2) * WINDOW])
                    dq = rq[i] * (qwv * dqn_t - qx[i] * _pair_mean(dqn_t * qwv * qx[i], low))
                    d_ref[:, tq * LANES:(tq + 1) * LANES] = dq.astype(d_ref.dtype)
                    dqw = dqw + _colsum(dqn_t * qx[i])
            k_cols = slice(t * LANES, (t + 1) * LANES)
            v_cols = slice(KV_W + t * LANES, KV_W + (t + 1) * LANES)
            dkn_c = dkn_t[WINDOW:] + carry[:, k_cols]
            rc = rk[WINDOW:]
            kx = kt[WINDOW:] * rc
            dk = rc * (kwv * dkn_c - kx * _pair_mean(dkn_c * kwv * kx, low))
            d_ref[:, ATTN_W + t * LANES:ATTN_W + (t + 1) * LANES] = dk.astype(d_ref.dtype)
            d_ref[:, ATTN_W + KV_W + t * LANES:ATTN_W + KV_W + (t + 1) * LANES] = (
                dv_t[WINDOW:] + carry[:, v_cols]).astype(d_ref.dtype)
            carry[:, k_cols] = dkn_t[:WINDOW]
            carry[:, v_cols] = dv_t[:WINDOW]
            dkw = dkw + _colsum(dkn_c * kx)
        dqw_ref[...] += dqw
        dkw_ref[...] += dkw
        dsk_ref[...] += dsk

    return pl.pallas_call(
        body, name="attn_bwd", grid=(nblk,),
        in_specs=[pl.BlockSpec(memory_space=pltpu.SMEM), q_spec, kvc_spec, kvp_spec, w_spec, w_spec, l_spec, q_spec,
                  q_spec] + gate_specs + [_ANY] * nd,
        out_specs=[d_spec, w_spec, w_spec, s_spec],
        out_shape=[jax.ShapeDtypeStruct(d_proj.shape, d_proj.dtype), jax.ShapeDtypeStruct((1, LANES), F32),
                   jax.ShapeDtypeStruct((1, LANES), F32), jax.ShapeDtypeStruct((1, N_Q_HEADS), F32)],
        input_output_aliases={9 + ng + nd - 1: 0},
        scratch_shapes=[pltpu.VMEM((WINDOW, 2 * KV_W), F32), pltpu.VMEM((WINDOW, ATTN_W), F32)],
        compiler_params=_params(("arbitrary",)),
    )(sinks, proj, proj, proj, qw2, kw2, lse, attn, dya, *([proj] * ng), *deps)


def _ssm_discretise(a_re, a_im, log_dt):
    dt = jnp.exp(log_dt)
    mag = jnp.exp(dt * a_re)
    ab_re = mag * jnp.cos(dt * a_im)
    ab_im = mag * jnp.sin(dt * a_im)
    num_re = ab_re - 1.0
    num_im = ab_im
    den = a_re * a_re + a_im * a_im
    cf_re = (num_re * a_re + num_im * a_im) / den
    cf_im = (num_im * a_re - num_re * a_im) / den
    return ab_re, ab_im, cf_re, cf_im


def _ssm_params_fwd(a_re, a_im, log_dt):
    shp = jax.ShapeDtypeStruct(a_re.shape, F32)

    def body(are_ref, aim_ref, ldt_ref, abr_ref, abi_ref, cfr_ref, cfi_ref, alr_ref, ali_ref):
        abr, abi, cfr, cfi = _ssm_discretise(are_ref[...], aim_ref[...], ldt_ref[...])
        abr_ref[...], abi_ref[...], cfr_ref[...], cfi_ref[...] = abr, abi, cfr, cfi
        pr, pi = abr, abi
        for _ in range(int(math.log2(SSM_L))):
            pr, pi = pr * pr - pi * pi, 2.0 * pr * pi
        alr_ref[...], ali_ref[...] = pr, pi

    return pl.pallas_call(body, name="ssm_params_fwd", out_shape=[shp] * 6)(a_re, a_im, log_dt)


def _ssm_params_bwd(a_re, a_im, log_dt, d_abr, d_abi, d_cfr, d_cfi):
    def body(are_ref, aim_ref, ldt_ref, g0, g1, g2, g3, dare_ref, daim_ref, dldt_ref):
        _, vjp = jax.vjp(_ssm_discretise, are_ref[...], aim_ref[...], ldt_ref[...])
        dare_ref[...], daim_ref[...], dldt_ref[...] = vjp((g0[...], g1[...], g2[...], g3[...]))

    return pl.pallas_call(
        body, name="ssm_params_bwd",
        out_shape=[jax.ShapeDtypeStruct(a_re.shape, F32), jax.ShapeDtypeStruct(a_im.shape, F32),
                   jax.ShapeDtypeStruct(log_dt.shape, F32)],
    )(a_re, a_im, log_dt, d_abr, d_abi, d_cfr, d_cfi)


def _scan_cols(j):
    return pl.ds(j * SSM_SB, SSM_SB)


def _rows8(r):
    return pl.ds(pl.multiple_of(r * SUBLANES, SUBLANES), SUBLANES)


def _bcast8(row):
    return jnp.broadcast_to(row, (SUBLANES, row.shape[-1]))


def _token_order_pick():
    tok = lax.broadcasted_iota(jnp.int32, (SSM_T, SSM_T), 0)
    row = lax.broadcasted_iota(jnp.int32, (SSM_T, SSM_T), 1)
    return (row == SUBLANES * (tok % SSM_L) + tok // SSM_L).astype(_MXU)


SCAN_UNROLL = 8


def _scan_loop(n, step, init):
    def trip(o, carry):
        for i in range(SCAN_UNROLL):
            carry = step(o * SCAN_UNROLL + i, carry)
        return carry

    return lax.fori_loop(0, n // SCAN_UNROLL, trip, init)


def _ssm_fwd(u, b_re, b_im, c_re, c_im, d_skip, coef):
    seq = u.shape[0]
    nc = seq // SSM_T
    T, L = SSM_T, SSM_L

    def body(u_ref, bre_ref, bim_ref, cre_ref, cim_ref, d_ref, are_ref, aim_ref, cfr_ref, cfi_ref, alr_ref, ali_ref,
             y_ref, yg_ref, sre_ref, sim_ref, ire_ref, iim_ref, car_re, car_im, end_re, end_im, yg_scan):
        c = pl.program_id(0)

        @pl.when(c == 0)
        def _():
            car_re[...] = jnp.zeros_like(car_re)
            car_im[...] = jnp.zeros_like(car_im)

        for j in range(SSM_JB):
            ub = u_ref[:, j * LANES:(j + 1) * LANES].astype(_MXU)
            bur = jnp.dot(ub, bre_ref[j], preferred_element_type=F32)
            bui = jnp.dot(ub, bim_ref[j], preferred_element_type=F32)
            cfr, cfi = cfr_ref[:, _scan_cols(j)], cfi_ref[:, _scan_cols(j)]
            sre_ref[:, _scan_cols(j)] = cfr * bur - cfi * bui
            sim_ref[:, _scan_cols(j)] = cfr * bui + cfi * bur

        for j in range(SSM_JB):
            cols = _scan_cols(j)
            ar, ai = _bcast8(are_ref[:, cols]), _bcast8(aim_ref[:, cols])

            def step1(r, s, cols=cols, ar=ar, ai=ai):
                sr, si = s
                rows = _rows8(r)
                return (ar * sr - ai * si + sre_ref[rows, cols], ar * si + ai * sr + sim_ref[rows, cols])

            zero = jnp.zeros((SUBLANES, SSM_SB), F32)
            er, ei = _scan_loop(L, step1, (zero, zero))
            end_re[:, cols] = er
            end_im[:, cols] = ei

        alr, ali = alr_ref[...], ali_ref[...]
        cr, ci = car_re[...], car_im[...]
        ire_ref[0:1, :] = cr
        iim_ref[0:1, :] = ci
        for i in range(1, SUBLANES):
            er, ei = end_re[i - 1:i, :], end_im[i - 1:i, :]
            cr, ci = alr * cr - ali * ci + er, alr * ci + ali * cr + ei
            ire_ref[i:i + 1, :] = cr
            iim_ref[i:i + 1, :] = ci

        for j in range(SSM_JB):
            cols = _scan_cols(j)
            ar, ai = _bcast8(are_ref[:, cols]), _bcast8(aim_ref[:, cols])

            def step2(r, s, cols=cols, ar=ar, ai=ai):
                sr, si = s
                rows = _rows8(r)
                nr = ar * sr - ai * si + sre_ref[rows, cols]
                ni = ar * si + ai * sr + sim_ref[rows, cols]
                sre_ref[rows, cols] = nr
                sim_ref[rows, cols] = ni
                return nr, ni

            _scan_loop(L, step2, (ire_ref[:, cols], iim_ref[:, cols]))

        car_re[...] = sre_ref[T - 1:T, :]
        car_im[...] = sim_ref[T - 1:T, :]

        for j in range(SSM_JB):
            cols = _scan_cols(j)
            ch = slice(j * LANES, (j + 1) * LANES)
            y = (jnp.dot(sre_ref[:, cols].astype(_MXU), cre_ref[j], preferred_element_type=F32)
                 - jnp.dot(sim_ref[:, cols].astype(_MXU), cim_ref[j], preferred_element_type=F32))
            y = y + d_ref[:, ch] * u_ref[:, ch]
            y_ref[:, ch] = y
            yg_scan[:, ch] = jax.nn.gelu(y).astype(yg_scan.dtype)
        yg_ref[...] = jnp.dot(_token_order_pick(), yg_scan[...], preferred_element_type=F32).astype(yg_ref.dtype)

    tok = pl.BlockSpec((T, SSM_W), lambda c: (c, 0))
    st = pl.BlockSpec((T, N_STATES), lambda c: (c, 0))
    ini = pl.BlockSpec((None, SUBLANES, N_STATES), lambda c: (c, 0, 0))
    bsp = pl.BlockSpec((SSM_JB, LANES, SSM_SB), lambda c: (0, 0, 0))
    csp = pl.BlockSpec((SSM_JB, SSM_SB, LANES), lambda c: (0, 0, 0))
    row_w = pl.BlockSpec((1, SSM_W), lambda c: (0, 0))
    row_s = pl.BlockSpec((1, N_STATES), lambda c: (0, 0))
    return pl.pallas_call(
        body, name="ssm_fwd", grid=(nc,),
        in_specs=[tok, bsp, bsp, csp, csp, row_w] + [row_s] * 6,
        out_specs=[tok, tok, st, st, ini, ini],
        out_shape=[jax.ShapeDtypeStruct((seq, SSM_W), F32), jax.ShapeDtypeStruct((seq, SSM_W), _MXU),
                   jax.ShapeDtypeStruct((seq, N_STATES), F32), jax.ShapeDtypeStruct((seq, N_STATES), F32),
                   jax.ShapeDtypeStruct((nc, SUBLANES, N_STATES), F32),
                   jax.ShapeDtypeStruct((nc, SUBLANES, N_STATES), F32)],
        scratch_shapes=[pltpu.VMEM((1, N_STATES), F32), pltpu.VMEM((1, N_STATES), F32),
                        pltpu.VMEM((SUBLANES, N_STATES), F32), pltpu.VMEM((SUBLANES, N_STATES), F32),
                        pltpu.VMEM((T, SSM_W), _MXU)],
        compiler_params=_params(("arbitrary",)),
    )(u, b_re, b_im, c_re, c_im, d_skip, *coef)


def _ssm_bwd(dyg, y, u, s_re, s_im, i_re, i_im, b_re, b_im, c_re, c_im, d_skip, coef, d_proj, deps=()):
    seq = u.shape[0]
    nc = seq // SSM_T
    T, L = SSM_T, SSM_L
    deps = list(deps) + [d_proj]

    def body(dyg_ref, y_ref, u_ref, sre_ref, sim_ref, ire_ref, iim_ref, bre_ref, bim_ref, cre_ref, cim_ref, d_ref,
             are_ref, aim_ref, cfr_ref, cfi_ref, alr_ref, ali_ref, *rest):
        (du_ref, dbre_out, dbim_out, dcre_out, dcim_out, dd_ref, dar_ref, dai_ref, dcfr_ref, dcfi_ref,
         lre, lim, car_re, car_im, end_re, end_im, ini_re, ini_im, dbre_ref, dbim_ref, dcre_ref, dcim_ref,
         dy_ref, du_scan) = rest[len(deps):]
        step = pl.program_id(0)
        dy_ref[...] = jax.vjp(jax.nn.gelu, y_ref[...])[1](dyg_ref[...])[0]

        @pl.when(step == 0)
        def _():
            car_re[...] = jnp.zeros_like(car_re)
            car_im[...] = jnp.zeros_like(car_im)
            for ref in (dbre_ref, dbim_ref, dcre_ref, dcim_ref, dd_ref, dar_ref, dai_ref, dcfr_ref, dcfi_ref):
                ref[...] = jnp.zeros_like(ref)

        for j in range(SSM_JB):
            dyb = dy_ref[:, j * LANES:(j + 1) * LANES].astype(_MXU)
            lre[:, _scan_cols(j)] = lax.dot_general(dyb, cre_ref[j], _NT, preferred_element_type=F32)
            lim[:, _scan_cols(j)] = -lax.dot_general(dyb, cim_ref[j], _NT, preferred_element_type=F32)

        for j in range(SSM_JB):
            cols = _scan_cols(j)
            ar, ai = _bcast8(are_ref[:, cols]), _bcast8(aim_ref[:, cols])

            def step1(t, s, cols=cols, ar=ar, ai=ai):
                sr, si = s
                rows = _rows8(L - 1 - t)
                return (ar * sr + ai * si + lre[rows, cols], ar * si - ai * sr + lim[rows, cols])

            zero = jnp.zeros((SUBLANES, SSM_SB), F32)
            er, ei = _scan_loop(L, step1, (zero, zero))
            end_re[:, cols] = er
            end_im[:, cols] = ei

        alr, ali = alr_ref[...], ali_ref[...]
        cr, ci = car_re[...], car_im[...]
        ini_re[SUBLANES - 1:SUBLANES, :] = cr
        ini_im[SUBLANES - 1:SUBLANES, :] = ci
        for i in range(SUBLANES - 2, -1, -1):
            er, ei = end_re[i + 1:i + 2, :], end_im[i + 1:i + 2, :]
            cr, ci = alr * cr + ali * ci + er, alr * ci - ali * cr + ei
            ini_re[i:i + 1, :] = cr
            ini_im[i:i + 1, :] = ci

        for j in range(SSM_JB):
            cols = _scan_cols(j)
            ar, ai = _bcast8(are_ref[:, cols]), _bcast8(aim_ref[:, cols])

            def step2(t, s, cols=cols, ar=ar, ai=ai):
                sr, si = s
                rows = _rows8(L - 1 - t)
                nr = ar * sr + ai * si + lre[rows, cols]
                ni = ar * si - ai * sr + lim[rows, cols]
                lre[rows, cols] = nr
                lim[rows, cols] = ni
                return nr, ni

            _scan_loop(L, step2, (ini_re[:, cols], ini_im[:, cols]))

        car_re[...] = lre[0:1, :]
        car_im[...] = lim[0:1, :]

        head, tail, body_rows = slice(0, SUBLANES), slice(SUBLANES, T), slice(0, T - SUBLANES)
        for j in range(SSM_JB):
            cols = _scan_cols(j)
            ch = slice(j * LANES, (j + 1) * LANES)
            lr, li = lre[:, cols], lim[:, cols]
            dar_ref[:, cols] += (_colsum(lre[tail, cols] * sre_ref[body_rows, cols] + lim[tail, cols] * sim_ref[body_rows, cols])
                                 + _colsum(lre[head, cols] * ire_ref[:, cols] + lim[head, cols] * iim_ref[:, cols]))
            dai_ref[:, cols] += (_colsum(lim[tail, cols] * sre_ref[body_rows, cols] - lre[tail, cols] * sim_ref[body_rows, cols])
                                 + _colsum(lim[head, cols] * ire_ref[:, cols] - lre[head, cols] * iim_ref[:, cols]))
            uf = u_ref[:, ch]
            ub = uf.astype(_MXU)
            bur = jnp.dot(ub, bre_ref[j], preferred_element_type=F32)
            bui = jnp.dot(ub, bim_ref[j], preferred_element_type=F32)
            dcfr_ref[:, cols] += _colsum(lr * bur + li * bui)
            dcfi_ref[:, cols] += _colsum(li * bur - lr * bui)
            cfr, cfi = cfr_ref[:, cols], cfi_ref[:, cols]
            dbur = (cfr * lr + cfi * li).astype(_MXU)
            dbui = (cfr * li - cfi * lr).astype(_MXU)
            dyf = dy_ref[:, ch]
            dyb = dyf.astype(_MXU)
            du = (lax.dot_general(dbur, bre_ref[j], _NT, preferred_element_type=F32)
                  + lax.dot_general(dbui, bim_ref[j], _NT, preferred_element_type=F32) + d_ref[:, ch] * dyf)
            du_scan[:, ch] = du.astype(du_scan.dtype)
            dbre_ref[j] += lax.dot_general(ub, dbur, _TN, preferred_element_type=F32)
            dbim_ref[j] += lax.dot_general(ub, dbui, _TN, preferred_element_type=F32)
            dcre_ref[j] += lax.dot_general(sre_ref[:, cols].astype(_MXU), dyb, _TN, preferred_element_type=F32)
            dcim_ref[j] -= lax.dot_general(sim_ref[:, cols].astype(_MXU), dyb, _TN, preferred_element_type=F32)
            dd_ref[:, ch] += _colsum(dyf * uf)
        du_ref[...] = jnp.dot(_token_order_pick(), du_scan[...], preferred_element_type=F32).astype(du_ref.dtype)

        @pl.when(step == nc - 1)
        def _():
            for acc, out in ((dbre_ref, dbre_out), (dbim_ref, dbim_out), (dcre_ref, dcre_out), (dcim_ref, dcim_out)):
                pltpu.sync_copy(acc, out)

    tok = pl.BlockSpec((T, SSM_W), lambda c: (nc - 1 - c, 0))
    st = pl.BlockSpec((T, N_STATES), lambda c: (nc - 1 - c, 0))
    ini = pl.BlockSpec((None, SUBLANES, N_STATES), lambda c: (nc - 1 - c, 0, 0))
    bsp = pl.BlockSpec((SSM_JB, LANES, SSM_SB), lambda c: (0, 0, 0))
    csp = pl.BlockSpec((SSM_JB, SSM_SB, LANES), lambda c: (0, 0, 0))
    row_w = pl.BlockSpec((1, SSM_W), lambda c: (0, 0))
    row_s = pl.BlockSpec((1, N_STATES), lambda c: (0, 0))
    big = pltpu.VMEM((T, N_STATES), F32)
    one = pltpu.VMEM((1, N_STATES), F32)
    eight = pltpu.VMEM((SUBLANES, N_STATES), F32)
    return pl.pallas_call(
        body, name="ssm_bwd", grid=(nc,),
        in_specs=[tok, tok, tok, st, st, ini, ini, bsp, bsp, csp, csp, row_w] + [row_s] * 6 + [_ANY] * len(deps),
        out_specs=[pl.BlockSpec((pl.Element(T), pl.Element(SSM_W)), lambda c: ((nc - 1 - c) * T, OFF_U * CW)),
                   _ANY, _ANY, _ANY, _ANY, row_w, row_s, row_s, row_s, row_s],
        input_output_aliases={18 + len(deps) - 1: 0},
        out_shape=[jax.ShapeDtypeStruct(d_proj.shape, d_proj.dtype),
                   jax.ShapeDtypeStruct((SSM_JB, LANES, SSM_SB), F32), jax.ShapeDtypeStruct((SSM_JB, LANES, SSM_SB), F32),
                   jax.ShapeDtypeStruct((SSM_JB, SSM_SB, LANES), F32), jax.ShapeDtypeStruct((SSM_JB, SSM_SB, LANES), F32),
                   jax.ShapeDtypeStruct((1, SSM_W), F32)] + [jax.ShapeDtypeStruct((1, N_STATES), F32)] * 4,
        scratch_shapes=[big, big, one, one, eight, eight, eight, eight,
                        pltpu.VMEM((SSM_JB, LANES, SSM_SB), F32), pltpu.VMEM((SSM_JB, LANES, SSM_SB), F32),
                        pltpu.VMEM((SSM_JB, SSM_SB, LANES), F32), pltpu.VMEM((SSM_JB, SSM_SB, LANES), F32),
                        pltpu.VMEM((T, SSM_W), F32), pltpu.VMEM((T, SSM_W), _MXU)],
        compiler_params=_params(("arbitrary",)),
    )(dyg, y, u, s_re, s_im, i_re, i_im, b_re, b_im, c_re, c_im, d_skip, *coef, *deps)


def _block_diag_b(b):
    t = b.reshape(SSM_JB, 8, STATE, GROUP).transpose(0, 1, 3, 2)
    eye = jnp.eye(8, dtype=b.dtype)
    return (t[:, :, :, None, :] * eye[None, :, None, :, None]).reshape(SSM_JB, LANES, SSM_SB)


def _block_diag_c(c):
    t = c.reshape(SSM_JB, 8, GROUP, STATE).transpose(0, 1, 3, 2)
    eye = jnp.eye(8, dtype=c.dtype)
    return (t[:, :, :, None, :] * eye[None, :, None, :, None]).reshape(SSM_JB, SSM_SB, LANES)


def _diag_of_b(blk):
    t = blk.reshape(SSM_JB, 8, GROUP, 8, STATE)
    d = jnp.sum(t * jnp.eye(8, dtype=blk.dtype)[None, :, None, :, None], axis=3)
    return d.transpose(0, 1, 3, 2).reshape(N_GROUPS, STATE, GROUP)


def _diag_of_c(blk):
    t = blk.reshape(SSM_JB, 8, STATE, 8, GROUP)
    d = jnp.sum(t * jnp.eye(8, dtype=blk.dtype)[None, :, None, :, None], axis=3)
    return d.transpose(0, 1, 3, 2).reshape(N_GROUPS, GROUP, STATE)


def _to_scan_order(v):
    seq, w = v.shape
    return v.reshape(seq // SSM_T, SUBLANES, SSM_L, w).transpose(0, 2, 1, 3).reshape(seq, w)


def _adamw_math(w, g, m, v):
    nm = ADAM_B1 * m + (1.0 - ADAM_B1) * g
    nv = ADAM_B2 * v + (1.0 - ADAM_B2) * jnp.square(g)
    m_hat = nm / (1.0 - ADAM_B1 ** ADAM_STEP)
    v_hat = nv / (1.0 - ADAM_B2 ** ADAM_STEP)
    return -ADAM_LR * (m_hat / (jnp.sqrt(v_hat) + ADAM_EPS) + ADAM_WD * w), nm, nv


def _adamw(w, g, m, v, *, name, tm, deps=()):
    rows, cols = w.shape
    nd = len(deps)

    def body(w_ref, g_ref, m_ref, v_ref, *rest):
        d_ref, nm_ref, nv_ref = rest[nd:]
        d_ref[...], nm_ref[...], nv_ref[...] = _adamw_math(w_ref[...], g_ref[...], m_ref[...], v_ref[...])

    spec = pl.BlockSpec((tm, cols), lambda i: (i, 0))
    shp = jax.ShapeDtypeStruct((rows, cols), F32)
    return pl.pallas_call(body, name=name, grid=(rows // tm,), in_specs=[spec] * 4 + [_ANY] * nd,
                          out_specs=[spec] * 3, out_shape=[shp] * 3,
                          compiler_params=_params(("arbitrary",)))(w, g, m, v, *deps)


def _place():
    x, y, c = lax.axis_index("x"), lax.axis_index("y"), lax.axis_index("c")
    chips = [(1 - x, y), (x, 1 - y), (1 - x, 1 - y)]
    return x, y, c, chips


def _remote(src, dst, send_sem, recv_sem, dev):
    return pltpu.make_async_remote_copy(src_ref=src, dst_ref=dst, send_sem=send_sem, recv_sem=recv_sem,
                                        device_id=dev, device_id_type=MESH)


def _place_shard(w, mine_arr, *, name, tm=256, deps=()):
    rows, cols = w.shape

    def body(m_ref, w_ref, *rest):
        rest[-1][...] = w_ref[...].astype(rest[-1].dtype)

    return pl.pallas_call(
        body, name=name,
        grid_spec=pltpu.PrefetchScalarGridSpec(
            num_scalar_prefetch=1, grid=(rows // tm,),
            in_specs=[pl.BlockSpec((tm, cols), lambda i, m: (i, 0))] + [_ANY] * len(deps),
            out_specs=pl.BlockSpec((None, tm, cols), lambda i, m: (m[0], i, 0))),
        out_shape=jax.ShapeDtypeStruct((N_CHIPS, rows, cols), _WIRE),
        compiler_params=_params(("arbitrary",)),
    )(mine_arr, w, *deps)


_HBM = pl.BlockSpec(memory_space=pltpu.HBM)
_SEM = pl.BlockSpec(memory_space=pltpu.SEMAPHORE)
_EFFECT = pltpu.SideEffectType.DATAFLOW_SIDE_EFFECTING


def _copies_start(name, bufs, plan, count, after=()):
    nb, na = len(bufs), len(after)

    def body(*refs):
        send_sems, recv_sems, token = refs[nb + na], refs[nb + na + 1], refs[-1]
        copies = plan(refs[:nb])
        assert len(copies) == count
        for i, (src, dst, dev, _) in enumerate(copies):
            _remote(src, dst, send_sems.at[i], recv_sems.at[i], dev).start()
        token[...] = jnp.zeros_like(token)

    res = pl.pallas_call(
        body, name=name, in_specs=[_HBM] * nb + [_ANY] * na,
        out_specs=(_SEM, _SEM, *[_HBM] * nb, pl.BlockSpec(memory_space=pltpu.VMEM)),
        out_shape=(pltpu.SemaphoreType.DMA((count,)), pltpu.SemaphoreType.DMA((count,)),
                   *[pltpu.HBM(b.shape, b.dtype) for b in bufs], jax.ShapeDtypeStruct((SUBLANES, LANES), F32)),
        input_output_aliases={i: 2 + i for i in range(nb)},
        compiler_params=pltpu.CompilerParams(has_side_effects=_EFFECT),
    )(*[pltpu.with_memory_space_constraint(b, pltpu.HBM) for b in bufs], *after)
    return (res[0], res[1]), list(res[2:2 + nb]), res[-1]


def _copies_wait(name, bufs, sems, plan, after=(), which=None):
    nb, na = len(bufs), len(after)

    def body(*refs):
        send_sems, recv_sems = refs[nb], refs[nb + 1]
        for i, (src, _, dev, land) in enumerate(plan(refs[:nb])):
            if which is not None and i not in which:
                continue
            cp = _remote(src, land, send_sems.at[i], recv_sems.at[i], dev)
            cp.wait_send()
            cp.wait_recv()

    res = pl.pallas_call(
        body, name=name, in_specs=[_HBM] * nb + [_SEM, _SEM] + [_ANY] * na, out_specs=[_HBM] * nb,
        out_shape=[pltpu.HBM(b.shape, b.dtype) for b in bufs],
        input_output_aliases={i: i for i in range(nb)},
        compiler_params=pltpu.CompilerParams(has_side_effects=_EFFECT),
    )(*bufs, *sems, *after)
    return list(res)


def _plan_gather_ici(fulls, which=(0, 1, 2)):
    x, y, c, chips = _place()
    copies = []
    for f in fulls:
        half = pl.ds(c * (f.shape[1] // 2), f.shape[1] // 2)
        own = f.at[2 * x + y, half]
        for chip in [chips[k] for k in which]:
            copies.append((own, own, (*chip, c), f.at[2 * chip[0] + chip[1], half]))
    return copies


def _plan_gather_d2d(fulls, which=(0, 1, 2)):
    x, y, c, chips = _place()
    copies = []
    for f in fulls:
        r2 = f.shape[1] // 2
        for chip in [chips[k] for k in which]:
            blk = 2 * chip[0] + chip[1]
            landed = f.at[blk, pl.ds(c * r2, r2)]
            copies.append((landed, landed, (x, y, 1 - c), f.at[blk, pl.ds((1 - c) * r2, r2)]))
    return copies


def _plan_relay_direct(fulls):
    (f,) = fulls
    x, y, c, chips = _place()
    half = pl.ds(c * (f.shape[1] // 2), f.shape[1] // 2)
    own = f.at[2 * x + y, half]
    return [(own, own, (*chip, c), f.at[2 * chip[0] + chip[1], half]) for chip in chips[:2]]


def _plan_relay_forward(fulls, k):
    (f,) = fulls
    x, y, c, chips = _place()
    r2 = f.shape[1] // 2
    half, other = pl.ds(c * r2, r2), pl.ds((1 - c) * r2, r2)
    quarter = pl.ds(c * r2 + k * (r2 // 2), r2 // 2)
    blk, far = 2 * chips[k][0] + chips[k][1], 2 * chips[2][0] + chips[2][1]
    passed, landed = f.at[blk, quarter], f.at[blk, half]
    return [(passed, passed, (*chips[1 - k], c), f.at[far, quarter]), (landed, landed, (x, y, 1 - c), f.at[blk, other])]


def _plan_relay_last(fulls):
    (f,) = fulls
    x, y, c, chips = _place()
    r2 = f.shape[1] // 2
    far = 2 * chips[2][0] + chips[2][1]
    landed = f.at[far, pl.ds(c * r2, r2)]
    return [(landed, landed, (x, y, 1 - c), f.at[far, pl.ds((1 - c) * r2, r2)])]


def _plan_swap_halves(refs):
    x, y, c, _ = _place()
    n = len(refs) // 2
    copies = []
    for g, land in zip(refs[:n], refs[n:]):
        r2 = g.shape[1] // 2
        copies.append((g.at[:, pl.ds((1 - c) * r2, r2), :], land, (x, y, 1 - c), land))
    return copies


def _plan_scatter_chips(refs):
    x, y, c, chips = _place()
    n = len(refs) // 2
    copies = []
    for h, land in zip(refs[:n], refs[n:]):
        for k, chip in enumerate(chips):
            copies.append((h.at[2 * chip[0] + chip[1]], land.at[k], (*chip, c), land.at[k]))
    return copies


def _plan_join_halves(totals):
    x, y, c, _ = _place()
    copies = []
    for t in totals:
        r2 = t.shape[0] // 2
        mine = t.at[pl.ds(c * r2, r2)]
        copies.append((mine, mine, (x, y, 1 - c), t.at[pl.ds((1 - c) * r2, r2)]))
    return copies


def _add_sibling_half(g, got, c_arr, *, name, tm):
    _, rows, cols = g.shape
    r2 = rows // 2
    nb = r2 // tm

    def body(c_ref, g_ref, r_ref, o_ref):
        o_ref[...] = (g_ref[...].astype(F32) + r_ref[...].astype(F32)).astype(o_ref.dtype)

    return pl.pallas_call(
        body, name=name,
        grid_spec=pltpu.PrefetchScalarGridSpec(
            num_scalar_prefetch=1, grid=(N_CHIPS, nb),
            in_specs=[pl.BlockSpec((None, tm, cols), lambda b, i, c: (b, c[0] * nb + i, 0)),
                      pl.BlockSpec((None, tm, cols), lambda b, i, c: (b, i, 0))],
            out_specs=pl.BlockSpec((None, tm, cols), lambda b, i, c: (b, i, 0))),
        out_shape=jax.ShapeDtypeStruct((N_CHIPS, r2, cols), _WIRE),
        compiler_params=_params(("arbitrary", "arbitrary")),
    )(c_arr, g, got)


def _add_chips(h, got, place_arr, *, name, tm):
    _, r2, cols = h.shape
    nb = r2 // tm

    def body(p_ref, h_ref, r_ref, o_ref):
        o_ref[...] = ((h_ref[...].astype(F32) + r_ref[0].astype(F32)) + r_ref[1].astype(F32)) + r_ref[2].astype(F32)

    return pl.pallas_call(
        body, name=name,
        grid_spec=pltpu.PrefetchScalarGridSpec(
            num_scalar_prefetch=1, grid=(nb,),
            in_specs=[pl.BlockSpec((None, tm, cols), lambda i, p: (p[0], i, 0)),
                      pl.BlockSpec((3, tm, cols), lambda i, p: (0, i, 0))],
            out_specs=pl.BlockSpec((tm, cols), lambda i, p: (p[1] * nb + i, 0))),
        out_shape=jax.ShapeDtypeStruct((2 * r2, cols), F32),
        compiler_params=_params(("arbitrary",)),
    )(place_arr, h, got)


class _ReduceScatter:
    def __init__(self, tag, names, grads):
        self.tag, self.names, self.n = tag, names, len(names)
        core = lax.axis_index("c").astype(jnp.int32)
        chip = (2 * lax.axis_index("x") + lax.axis_index("y")).astype(jnp.int32)
        self.c_arr, self.place_arr = core.reshape(1), jnp.stack([chip, core])
        self.bufs = list(grads)

    def _start(self, step, bufs, plan, count, after):
        self.plan = plan
        self.step = f"grad_{step}_{self.tag}"
        self.sems, self.bufs, token = _copies_start(self.step + "_start", bufs, plan, count, after)
        return [token]

    def _wait(self, after):
        self.bufs = _copies_wait(self.step + "_wait", self.bufs, self.sems, self.plan, after)
        return self.bufs

    def start_swap(self, after=()):
        lands = [lax.empty((N_CHIPS, g.shape[1] // 2, g.shape[2]), g.dtype) for g in self.bufs]
        return self._start("swap", self.bufs + lands, _plan_swap_halves, self.n, after)

    def start_scatter(self, after):
        bufs = self._wait(after)
        pair = [_add_sibling_half(g, r, self.c_arr, name=f"grad_add_sibling_{nm}", tm=min(256, g.shape[1] // 2))
                for nm, g, r in zip(self.names, bufs[:self.n], bufs[self.n:])]
        lands = [lax.empty((3,) + h.shape[1:], h.dtype) for h in pair]
        return self._start("scatter", pair + lands, _plan_scatter_chips, 3 * self.n, ())

    def start_join(self, after):
        bufs = self._wait(after)
        total = [_add_chips(h, r, self.place_arr, name=f"grad_add_chips_{nm}", tm=min(256, h.shape[1]))
                 for nm, h, r in zip(self.names, bufs[:self.n], bufs[self.n:])]
        return self._start("join", total, _plan_join_halves, self.n, ())

    def finish(self, after):
        return dict(zip(self.names, self._wait(after)))


def _all_gather_small(v):
    m_per, n = v.shape

    def body(x_ref, out_ref, send_sems, recv_sems, local_sem):
        x, y, c, chips = _place()
        me, sibling = (x, y, c), (x, y, 1 - c)

        def rows(px, py, pc):
            return out_ref.at[4 * px + 2 * py + pc]

        def copy(k, block, to, src=None):
            return _remote(rows(*block) if src is None else src, rows(*block), send_sems.at[k], recv_sems.at[k], to)

        mine = pltpu.make_async_copy(x_ref, rows(*me), local_sem)
        mine.start()
        first = [copy(0, me, sibling, src=x_ref)]
        first += [copy(1 + j, me, (*chip, c), src=x_ref) for j, chip in enumerate(chips)]
        for cp in first:
            cp.start()
        passed = [copy(4 + j, (*chip, c), sibling) for j, chip in enumerate(chips)]
        for j, chip in enumerate(chips):
            copy(1 + j, (*chip, c), me).wait_recv()
            passed[j].start()
        copy(0, sibling, me).wait_recv()
        for j, chip in enumerate(chips):
            copy(4 + j, (*chip, 1 - c), me).wait_recv()
        for cp in first + passed:
            cp.wait_send()
        mine.wait()

    return pl.pallas_call(
        body, name="gather_small_grads",
        out_shape=jax.ShapeDtypeStruct((8, m_per, n), v.dtype),
        in_specs=[pl.BlockSpec(memory_space=pltpu.VMEM)], out_specs=pl.BlockSpec(memory_space=pltpu.VMEM),
        scratch_shapes=[pltpu.SemaphoreType.DMA((7,)), pltpu.SemaphoreType.DMA((7,)), pltpu.SemaphoreType.DMA],
        compiler_params=pltpu.CompilerParams(vmem_limit_bytes=VMEM_LIMIT),
    )(v)


def _sum8(v, *, name):
    _, m, n = v.shape

    def body(v_ref, o_ref):
        acc = v_ref[0]
        for d in range(1, 8):
            acc = acc + v_ref[d]
        o_ref[...] = acc

    return pl.pallas_call(body, name=name, out_shape=jax.ShapeDtypeStruct((m, n), F32),
                          compiler_params=pltpu.CompilerParams(vmem_limit_bytes=VMEM_LIMIT))(v)


def _local_step(x, target, norm_w, q_norm_w, k_norm_w, sinks, a_re, a_im, log_dt, b_re, b_im, c_re, c_im, d_skip,
                b_glu, io):
    seq = x.shape[0]
    qw2 = jnp.tile(q_norm_w.reshape(1, HEAD_DIM), (1, HEADS_PER_TILE))
    kw2 = jnp.tile(k_norm_w.reshape(1, HEAD_DIM), (1, HEADS_PER_TILE))
    nw, bg = norm_w.reshape(1, D_MODEL), b_glu.reshape(1, D_MODEL)
    dsk = d_skip.reshape(1, SSM_W)

    h, rstd = _rms_fwd(x, nw, deps=io.begin())
    proj, w_in4 = io.projection(h)
    attn, lse, ya_in = _attn2_fwd(proj, qw2, kw2, sinks, deps=io.after_proj(proj))
    w_ap4 = io.weight("w_attn_proj", ya_in)
    w_glu4, w_sp4, w_out = io.weight("w_glu", ya_in), io.weight("w_ssm_proj", ya_in), io.weight("w_out", ya_in)
    y_a = _mm(ya_in, w_ap4, mode="nn", name="mm_attn_proj", tm=2048, tn=512, tk=ATTN_W, b_blocked=True,
              rows_outer=True)

    flat_a = (a_re.reshape(1, N_STATES), a_im.reshape(1, N_STATES), jnp.repeat(log_dt, STATE).reshape(1, N_STATES))
    coef = _ssm_params_fwd(*flat_a)
    bre_blk, bim_blk = _block_diag_b(b_re).astype(_MXU), _block_diag_b(b_im).astype(_MXU)
    cre_blk, cim_blk = _block_diag_c(c_re).astype(_MXU), _block_diag_c(c_im).astype(_MXU)
    u_scan = _to_scan_order(proj[:, OFF_U * CW:OFF_U * CW + SSM_W])
    y_scan, yg, s_re, s_im, i_re, i_im = _ssm_fwd(u_scan, bre_blk, bim_blk, cre_blk, cim_blk, dsk, coef)
    glu = _mm(yg, w_glu4, mode="nn", name="mm_glu", tm=2048, tn=512, tk=SSM_W, b_blocked=True, rows_outer=True)

    def gate_s(ga, gb, ba, bb, z):
        return ((ga + ba) * _sigmoid(gb + bb) * (z * _sigmoid(z)),)

    (ys_in,) = _ew(gate_s, [(glu, "mat", 0), (glu, "mat", 2), (bg, "row", 0), (bg, "row", 2), (proj, "mat", OFF_Z)],
                   [(SSM_W, _MXU)], rows=seq, ncol=2, name="ew_ssm_gate")
    y_s = _mm(ys_in, w_sp4, mode="nn", name="mm_ssm_proj", tm=2048, tn=512, tk=SSM_W, b_blocked=True,
              rows_outer=True)

    def merge(ga, gs, ya, ys):
        return (_sigmoid(ga) * ya + _sigmoid(gs) * ys,)

    (merged,) = _ew(merge, [(proj, "mat", OFF_GA), (proj, "mat", OFF_GS), (y_a, "mat", 0), (y_s, "mat", 0)],
                    [(D_MODEL, _MXU)], rows=seq, ncol=4, name="ew_merge")
    dout, dout_b, sq = _mm_out_loss(merged, w_out, x, target)
    loss = 0.5 * jnp.sum(sq) / D_MODEL

    d_ya, d_ys, d_proj = _mm_merge_bwd(dout_b, w_out, proj, y_a, y_s)
    g_w_out = _mm(merged, dout_b, mode="tn", name="mm_g_w_out", tm=1024, tn=D_MODEL, tk=1024, out_dtype=_WIRE)

    d_ya_in = _mm(d_ya, w_ap4, mode="nt", name="mm_d_attn_gate", tm=2048, tn=ATTN_W, tk=512, b_blocked=True)
    g_w_ap = _mm(ya_in, d_ya, mode="tn", name="mm_g_w_attn_proj", tm=ATTN_W, tn=D_MODEL, tk=2048, out_dtype=_WIRE,
                 out_blocked=True)

    g_w_sp = _mm(ys_in, d_ys, mode="tn", name="mm_g_w_ssm_proj", tm=SSM_W, tn=D_MODEL, tk=2048, out_dtype=_WIRE,
                 out_blocked=True)
    d_glu, d_proj, g_bglu = _mm_ssm_gate_bwd(d_ys, w_sp4, glu, bg, proj, d_proj)
    d_yg = _mm(d_glu, w_glu4, mode="nt", name="mm_d_gelu", tm=2048, tn=SSM_W, tk=512, b_blocked=True)
    g_w_glu = _mm(yg, d_glu, mode="tn", name="mm_g_w_glu", tm=SSM_W, tn=D_MODEL, tk=2048, out_dtype=_WIRE, out_blocked=True)
    dep = io.later_grads(dict(w_attn_proj=g_w_ap, w_glu=g_w_glu, w_ssm_proj=g_w_sp,
                              w_out=g_w_out.reshape(N_CHIPS, D_MODEL // N_CHIPS, D_MODEL)))

    d_proj, g_qw2, g_kw2, g_sk = _attn2_bwd(proj, qw2, kw2, sinks, lse, attn, d_ya_in, d_proj, deps=dep)
    dep = io.before_scan_backward([d_proj])
    (d_proj, g_bre, g_bim, g_cre, g_cim, g_dsk, g_abr, g_abi, g_cfr, g_cfi) = _ssm_bwd(
        _to_scan_order(d_yg), y_scan, u_scan, s_re, s_im, i_re, i_im, bre_blk, bim_blk, cre_blk, cim_blk, dsk, coef,
        d_proj, deps=dep)
    g_are, g_aim, g_ldt = _ssm_params_bwd(*flat_a, g_abr, g_abi, g_cfr, g_cfi)
    g_are, g_aim = g_are.reshape(N_GROUPS, STATE), g_aim.reshape(N_GROUPS, STATE)
    g_ldt = g_ldt.reshape(N_GROUPS, STATE).sum(axis=1)
    dep = io.before_input_projection_grad([d_proj]) + io.small_grads(dict(
        q_norm_w=g_qw2[0, :HEAD_DIM] + g_qw2[0, HEAD_DIM:], k_norm_w=g_kw2[0, :HEAD_DIM] + g_kw2[0, HEAD_DIM:],
        sinks=g_sk.reshape(N_Q_HEADS), A_re=g_are, A_im=g_aim, log_dt=g_ldt,
        B_re=_diag_of_b(g_bre), B_im=_diag_of_b(g_bim), C_re=_diag_of_c(g_cre), C_im=_diag_of_c(g_cim),
        D_skip=g_dsk.reshape(N_GROUPS, GROUP), b_glu=g_bglu.reshape(D_MODEL)))
    g_w_in = _mm(h, d_proj, mode="tn", name="mm_g_w_in", tm=1024, tn=IN_W // 4, tk=1024, out_dtype=_WIRE,
                 out_blocked=True, deps=dep)
    dep = io.input_projection_grad(g_w_in)
    d_h = _mm(d_proj, w_in4, mode="nt", name="mm_d_h", tm=1024, tn=D_MODEL, tk=IN_W // 4, b_blocked=True, deps=dep)
    grad_x, g_nw = _rms_bwd(d_h, x, rstd, nw, dout)
    return loss, grad_x, g_nw.reshape(D_MODEL)


_SMALL = ["norm_w", "q_norm_w", "k_norm_w", "sinks", "A_re", "A_im", "log_dt", "B_re", "B_im", "C_re", "C_im",
          "D_skip", "b_glu"]
_BIG = ["w_in", "w_attn_proj", "w_glu", "w_ssm_proj", "w_out"]
_LATER = _BIG[1:]
_RELATIONS = ("flip_x", "flip_y", "flip_xy")
_ORDER = ["norm_w", "w_in", "q_norm_w", "k_norm_w", "sinks", "w_attn_proj", "A_re", "A_im", "log_dt", "B_re", "B_im",
          "C_re", "C_im", "D_skip", "w_glu", "b_glu", "w_ssm_proj", "w_out"]
_PACK_W = 1024


def _packed_rows(size):
    unit = SUBLANES * _PACK_W
    return -(-size // unit) * SUBLANES


def _pack_small(d, names):
    parts = []
    for n in names:
        flat = d[n].reshape(-1).astype(F32)
        rows = _packed_rows(flat.shape[0])
        parts.append(jnp.pad(flat, (0, rows * _PACK_W - flat.shape[0])).reshape(rows, _PACK_W))
    return jnp.concatenate(parts, axis=0)


def _unpack_small(packed, like, names):
    out, pos = {}, 0
    for n in names:
        rows = _packed_rows(like[n].size)
        out[n] = packed[pos:pos + rows].reshape(-1)[:like[n].size].reshape(like[n].shape)
        pos += rows
    return out


def _place_block(v, index_arr, *, name):
    rows, cols = v.shape

    def body(i_ref, v_ref, o_ref):
        o_ref[...] = v_ref[...]

    return pl.pallas_call(
        body, name=name,
        grid_spec=pltpu.PrefetchScalarGridSpec(
            num_scalar_prefetch=1, grid=(1,),
            in_specs=[pl.BlockSpec((rows, cols), lambda i, d: (0, 0))],
            out_specs=pl.BlockSpec((None, rows, cols), lambda i, d: (d[0], 0, 0))),
        out_shape=jax.ShapeDtypeStruct((8, rows, cols), v.dtype),
        compiler_params=_params(("arbitrary",)),
    )(index_arr, v)


def _plan_all_to_all(refs):
    (land,) = refs
    x, y, c, _ = _place()
    own = land.at[4 * x + 2 * y + c]
    copies = []
    for fx, fy, fc in [(0, 0, 1), (0, 1, 0), (0, 1, 1), (1, 0, 0), (1, 0, 1), (1, 1, 0), (1, 1, 1)]:
        px, py, pc = (1 - x) if fx else x, (1 - y) if fy else y, (1 - c) if fc else c
        copies.append((own, own, (px, py, pc), land.at[4 * px + 2 * py + pc]))
    return copies


def _adamw_whole(w, g, m, v, *, name):
    def body(w_ref, g_ref, m_ref, v_ref, d_ref, nm_ref, nv_ref):
        d_ref[...], nm_ref[...], nv_ref[...] = _adamw_math(w_ref[...], g_ref[...], m_ref[...], v_ref[...])

    return pl.pallas_call(body, name=name, out_shape=[jax.ShapeDtypeStruct(w.shape, F32)] * 3)(w, g, m, v)


class _Exchanges:
    def __init__(self, w, m, v):
        self.w, self.m, self.v = w, m, v
        self.grads, self.delta, self.new_m, self.new_v = {}, {}, {}, {}

    def _adamw(self, names, deps):
        for n in names:
            self.delta[n], self.new_m[n], self.new_v[n] = _adamw(
                self.w[n], self.grads[n], self.m[n], self.v[n], name=f"adamw_{n}", tm=128, deps=deps)

    def begin(self):
        chip = (2 * lax.axis_index("x") + lax.axis_index("y")).astype(jnp.int32).reshape(1)
        w_in = _place_shard(self.w["w_in"], chip, name="place_w_in")
        self.w_in_sems, self.w_in_buf, token = _copies_start("gather_w_in_direct_start", [w_in], _plan_relay_direct, 2)
        self.later_full = [_place_shard(self.w[n], chip, name=f"place_{n}", deps=[token]) for n in _LATER]
        return self.later_full

    def projection(self, h):
        x, y = lax.axis_index("x"), lax.axis_index("y")
        blks = [jnp.asarray(b, jnp.int32).reshape(1)
                for b in (2 * x + y, 2 * (1 - x) + y, 2 * x + (1 - y), 2 * (1 - x) + (1 - y))]
        bufs = self.w_in_buf
        proj = _mm_chip_block(h, bufs[0], blks[0], None, name="mm_proj_own")
        relay, token = [], proj
        for k, tag in enumerate(_RELATIONS[:2]):
            bufs = _copies_wait(f"gather_w_in_direct_{tag}_wait", bufs, self.w_in_sems, _plan_relay_direct, [token],
                                which=(k,))
            plan = functools.partial(_plan_relay_forward, k=k)
            sems, bufs, token = _copies_start(f"gather_w_in_relay_{tag}_start", bufs, plan, 2)
            relay.append((sems, plan))
        self.rest = _copies_start("gather_ici_rest_start", self.later_full, _plan_gather_ici, 3 * len(_LATER),
                                  after=[token])
        token = self.rest[2]
        for k, tag in enumerate(_RELATIONS[:2]):
            bufs = _copies_wait(f"gather_w_in_hand_{tag}_wait", bufs, relay[k][0], relay[k][1], [token], which=(1,))
            token = proj = _mm_chip_block(h, bufs[0], blks[1 + k], proj, name=f"mm_proj_{tag}")
        for k, tag in enumerate(_RELATIONS[:2]):
            bufs = _copies_wait(f"gather_w_in_relay_{tag}_wait", bufs, relay[k][0], relay[k][1], [token], which=(0,))
        sems, bufs, token = _copies_start("gather_w_in_last_start", bufs, _plan_relay_last, 1)
        bufs = _copies_wait("gather_w_in_last_wait", bufs, sems, _plan_relay_last, [token])
        proj = _mm_chip_block(h, bufs[0], blks[3], proj, name="mm_proj_flip_xy")
        return proj, bufs[0]

    def weight(self, name, after):
        if self.rest is not None:
            sems, bufs = self.rest
            later = dict(zip(_LATER, _copies_wait("gather_d2d_rest_wait", bufs, sems, _plan_gather_d2d, [after])))
            later["w_out"] = later["w_out"].reshape(D_MODEL, D_MODEL)
            self.later, self.rest = later, None
        return self.later[name]

    def after_proj(self, proj):
        sems, bufs, _ = self.rest
        bufs = _copies_wait("gather_ici_rest_wait", bufs, sems, _plan_gather_ici, [proj])
        sems, bufs, token = _copies_start("gather_d2d_rest_start", bufs, _plan_gather_d2d, 3 * len(_LATER))
        self.rest = (sems, bufs)
        return [token]

    def later_grads(self, grads):
        self.rs_later = _ReduceScatter("later", _LATER, [grads[n] for n in _LATER])
        return self.rs_later.start_swap()

    def before_scan_backward(self, after):
        return self.rs_later.start_scatter(after)

    def before_input_projection_grad(self, after):
        return self.rs_later.start_join(after)

    def input_projection_grad(self, g_w_in):
        self.grads.update(self.rs_later.finish([g_w_in]))
        self.rs_in = _ReduceScatter("w_in", ["w_in"], [g_w_in])
        self._adamw(_LATER, self.rs_in.start_swap())
        return self.rs_in.start_scatter([self.delta[n] for n in _LATER])

    def _adamw_small(self, names):
        for n in names:
            self.delta[n], self.new_m[n], self.new_v[n] = _adamw_whole(
                self.w[n], self.grads[n], self.m[n], self.v[n], name=f"adamw_{n}")

    def small_grads(self, grads):
        me = (4 * lax.axis_index("x") + 2 * lax.axis_index("y") + lax.axis_index("c")).astype(jnp.int32).reshape(1)
        land = _place_block(_pack_small(grads, _SMALL[1:]), me, name="place_small_grads")
        self.small = _copies_start("gather_small_start", [land], _plan_all_to_all, 7)
        return [self.small[2]]

    def finish(self, g_norm_w, loss, after):
        join = self.rs_in.start_join(after)
        sems, bufs, _ = self.small
        (land,) = _copies_wait("gather_small_wait", bufs, sems, _plan_all_to_all, join)
        self.grads.update(_unpack_small(_sum8(land, name="sum_small_grads"), self.w, _SMALL[1:]))
        self._adamw_small(_SMALL[1:])
        rows = _packed_rows(g_norm_w.size)
        late = jnp.concatenate([_pack_small(dict(norm_w=g_norm_w), _SMALL[:1]),
                                jnp.pad(loss.reshape(1, 1), ((0, SUBLANES - 1), (0, _PACK_W - 1)))], axis=0)
        late = _sum8(_all_gather_small(late), name="sum_norm_w_grad_and_loss")
        self.grads.update(_unpack_small(late[:rows], self.w, _SMALL[:1]))
        self._adamw_small(_SMALL[:1])
        self.grads.update(self.rs_in.finish([self.delta[_SMALL[0]]]))
        self._adamw(["w_in"], ())
        return late[rows, 0]


def kernel(x, norm_w, w_in, q_norm_w, k_norm_w, sinks, w_attn_proj, A_re, A_im, log_dt, B_re, B_im, C_re, C_im, D_skip, w_glu, b_glu, w_ssm_proj, w_out, loss_target, m_norm_w, m_w_in, m_q_norm_w, m_k_norm_w, m_sinks, m_w_attn_proj, m_A_re, m_A_im, m_log_dt, m_B_re, m_B_im, m_C_re, m_C_im, m_D_skip, m_w_glu, m_b_glu, m_w_ssm_proj, m_w_out, v_norm_w, v_w_in, v_q_norm_w, v_k_norm_w, v_sinks, v_w_attn_proj, v_A_re, v_A_im, v_log_dt, v_B_re, v_B_im, v_C_re, v_C_im, v_D_skip, v_w_glu, v_b_glu, v_w_ssm_proj, v_w_out):
    w = dict(norm_w=norm_w, w_in=w_in, q_norm_w=q_norm_w, k_norm_w=k_norm_w, sinks=sinks, w_attn_proj=w_attn_proj,
             A_re=A_re, A_im=A_im, log_dt=log_dt, B_re=B_re, B_im=B_im, C_re=C_re, C_im=C_im, D_skip=D_skip,
             w_glu=w_glu, b_glu=b_glu, w_ssm_proj=w_ssm_proj, w_out=w_out)
    m = dict(norm_w=m_norm_w, w_in=m_w_in, q_norm_w=m_q_norm_w, k_norm_w=m_k_norm_w, sinks=m_sinks,
             w_attn_proj=m_w_attn_proj, A_re=m_A_re, A_im=m_A_im, log_dt=m_log_dt, B_re=m_B_re, B_im=m_B_im,
             C_re=m_C_re, C_im=m_C_im, D_skip=m_D_skip, w_glu=m_w_glu, b_glu=m_b_glu, w_ssm_proj=m_w_ssm_proj,
             w_out=m_w_out)
    v = dict(norm_w=v_norm_w, w_in=v_w_in, q_norm_w=v_q_norm_w, k_norm_w=v_k_norm_w, sinks=v_sinks,
             w_attn_proj=v_w_attn_proj, A_re=v_A_re, A_im=v_A_im, log_dt=v_log_dt, B_re=v_B_re, B_im=v_B_im,
             C_re=v_C_re, C_im=v_C_im, D_skip=v_D_skip, w_glu=v_w_glu, b_glu=v_b_glu, w_ssm_proj=v_w_ssm_proj,
             w_out=v_w_out)

    io = _Exchanges(w, m, v)
    loss, grad_x, g_norm_w = _local_step(x[0], loss_target[0], norm_w, q_norm_w, k_norm_w, sinks, A_re, A_im, log_dt,
                                         B_re, B_im, C_re, C_im, D_skip, b_glu, io)
    loss = io.finish(g_norm_w, loss, [grad_x])
    grads, delta, new_m, new_v = io.grads, io.delta, io.new_m, io.new_v

    return (loss, grad_x[None], *[grads[n] for n in _ORDER], *[delta[n] for n in _ORDER],
            *[new_m[n] for n in _ORDER], *[new_v[n] for n in _ORDER])
```

```python
import functools
import math

import jax
import jax.numpy as jnp
from jax import lax
from jax.experimental import pallas as pl
from jax.experimental.pallas import tpu as pltpu

F32 = jnp.float32
_MXU = jnp.bfloat16
_WIRE = jnp.bfloat16

LANES = 128
SUBLANES = 8
VMEM_LIMIT = 56 * 1024 * 1024

D_MODEL = 2048
HEAD_DIM = 64
N_Q_HEADS = 16
N_KV_HEADS = 4
Q_PER_KV = 4
ATTN_W = 1024
KV_W = 256
WINDOW = 128
SSM_W = 1024
GROUP = 16
N_GROUPS = 64
STATE = 64
N_STATES = N_GROUPS * STATE
IN_W = 8704
NORM_EPS = 1e-6
N_CHIPS = 4
CW = 512
OFF_AGATE, OFF_U, OFF_Z, OFF_GA, OFF_GS = 3, 5, 7, 9, 13

SSM_T = 256
SSM_L = SSM_T // SUBLANES
SSM_JB = 8
SSM_SB = N_STATES // SSM_JB

ADAM_LR, ADAM_B1, ADAM_B2, ADAM_EPS, ADAM_WD, ADAM_STEP = 0.001, 0.9, 0.999, 1e-08, 0.01, 10

MESH = pl.DeviceIdType.MESH
_ANY = pl.BlockSpec(memory_space=pl.ANY)


def _params(sem=None):
    return pltpu.CompilerParams(dimension_semantics=sem, vmem_limit_bytes=VMEM_LIMIT)


def _mm(a, b, *, mode, name, tm, tn, tk, out_dtype=F32, b_blocked=False, out_blocked=False, rows_outer=False,
        deps=()):
    nd = len(deps)
    if mode == "tn":
        K, M = a.shape
    else:
        M, K = a.shape
    if mode == "nn":
        N = b.shape[0] * b.shape[2] if b_blocked else b.shape[1]
    elif mode == "nt":
        N = b.shape[1] if b_blocked else b.shape[0]
    else:
        N = b.shape[1]
    tm, tn, tk = min(tm, M), min(tn, N), min(tk, K)
    nj, ni, nk = N // tn, M // tm, K // tk
    assert nj * tn == N and ni * tm == M and nk * tk == K, (name, M, N, K)
    dims = {"nn": (((1,), (0,)), ((), ())), "nt": (((1,), (1,)), ((), ())), "tn": (((0,), (0,)), ((), ()))}[mode]

    if mode == "tn":
        a_spec = pl.BlockSpec((tk, tm), lambda j, i, k: (k, i))
    else:
        a_spec = pl.BlockSpec((tm, tk), lambda j, i, k: (i, k))
    if mode == "nn":
        if b_blocked:
            assert b.shape[0] == nj and b.shape[2] == tn
            b_spec = pl.BlockSpec((None, tk, tn), lambda j, i, k: (j, k, 0))
        else:
            b_spec = pl.BlockSpec((tk, tn), lambda j, i, k: (k, j))
    elif mode == "nt":
        if b_blocked:
            assert b.shape[0] == nk and b.shape[2] == tk
            b_spec = pl.BlockSpec((None, tn, tk), lambda j, i, k: (k, j, 0))
        else:
            b_spec = pl.BlockSpec((tn, tk), lambda j, i, k: (j, k))
    else:
        b_spec = pl.BlockSpec((tk, tn), lambda j, i, k: (k, j))
    whole_out = out_blocked and nj == 1
    if whole_out:
        assert ni == 1
        o_spec = pl.BlockSpec((N_CHIPS, tm, tn // N_CHIPS), lambda j, i, k: (0, 0, 0))
        o_shape = jax.ShapeDtypeStruct((N_CHIPS, M, tn // N_CHIPS), out_dtype)
    elif out_blocked:
        assert nj == N_CHIPS
        o_spec = pl.BlockSpec((None, tm, tn), lambda j, i, k: (j, i, 0))
        o_shape = jax.ShapeDtypeStruct((nj, M, tn), out_dtype)
    else:
        o_spec = pl.BlockSpec((tm, tn), lambda j, i, k: (i, j))
        o_shape = jax.ShapeDtypeStruct((M, N), out_dtype)
    use_acc = nk > 1 and (out_dtype != F32 or whole_out)

    def body(a_ref, b_ref, *rest):
        o_ref, scratch = rest[nd], rest[nd + 1:]

        def product():
            return lax.dot_general(a_ref[...].astype(_MXU), b_ref[...].astype(_MXU), dims, preferred_element_type=F32)

        def write(result):
            if whole_out:
                w = tn // N_CHIPS
                for c in range(N_CHIPS):
                    o_ref[c] = result[:, c * w:(c + 1) * w].astype(o_ref.dtype)
            else:
                o_ref[...] = result.astype(o_ref.dtype)

        if nk == 1:
            write(product())
            return
        k = pl.program_id(2)
        acc = scratch[0] if use_acc else o_ref

        @pl.when(k == 0)
        def _():
            acc[...] = jnp.zeros_like(acc)

        acc[...] += product()

        if use_acc:
            @pl.when(k == nk - 1)
            def _():
                write(acc[...])

    specs = [a_spec, b_spec, o_spec]
    grid = (nj, ni, nk)
    if rows_outer:
        specs = [pl.BlockSpec(s.block_shape, lambda i, j, k, f=s.index_map: f(j, i, k)) for s in specs]
        grid = (ni, nj, nk)
    return pl.pallas_call(
        body, name=name, grid=grid, in_specs=specs[:2] + [_ANY] * nd, out_specs=specs[2],
        out_shape=o_shape, scratch_shapes=[pltpu.VMEM((tm, tn), F32)] if use_acc else [],
        compiler_params=_params(("parallel", "parallel", "arbitrary")),
    )(a, b, *deps)


def _mm_chip_block(a, b4, blk, prev, *, name, tm=512, deps=()):
    M, K = a.shape
    nchip, _, C = b4.shape
    tm = min(tm, M)
    extra = ([] if prev is None else [prev]) + list(deps)

    def body(blk_ref, a_ref, b_ref, *rest):
        rest[-1][...] = jnp.dot(a_ref[...].astype(_MXU), b_ref[...].astype(_MXU), preferred_element_type=F32)

    return pl.pallas_call(
        body, name=name,
        grid_spec=pltpu.PrefetchScalarGridSpec(
            num_scalar_prefetch=1, grid=(M // tm,),
            in_specs=[pl.BlockSpec((tm, K), lambda i, c: (i, 0)), pl.BlockSpec((None, K, C), lambda i, c: (c[0], 0, 0))]
            + [_ANY] * len(extra),
            out_specs=pl.BlockSpec((tm, C), lambda i, c: (i, c[0]))),
        out_shape=jax.ShapeDtypeStruct((M, nchip * C), F32),
        input_output_aliases={} if prev is None else {3: 0},
        compiler_params=_params(("arbitrary",)),
    )(blk, a, b4, *extra)


def _mm_out_loss(merged, w_out, x, target, *, tm=256):
    rows, d = x.shape

    def body(m_ref, w_ref, x_ref, t_ref, d_ref, db_ref, sq_ref):
        mo = jnp.dot(m_ref[...].astype(_MXU), w_ref[...].astype(_MXU), preferred_element_type=F32)
        err = (x_ref[...] + mo) - t_ref[...]
        dout = err * (1.0 / d)
        d_ref[...] = dout
        db_ref[...] = dout.astype(db_ref.dtype)
        part = _colsum(err * err)
        i = pl.program_id(0)

        @pl.when(i == 0)
        def _():
            sq_ref[...] = part

        @pl.when(i > 0)
        def _():
            sq_ref[...] += part

    tile = pl.BlockSpec((tm, d), lambda i: (i, 0))
    return pl.pallas_call(
        body, name="mm_out_loss", grid=(rows // tm,),
        in_specs=[tile, pl.BlockSpec((d, d), lambda i: (0, 0)), tile, tile],
        out_specs=[tile, tile, pl.BlockSpec((1, d), lambda i: (0, 0))],
        out_shape=[jax.ShapeDtypeStruct((rows, d), F32), jax.ShapeDtypeStruct((rows, d), _MXU),
                   jax.ShapeDtypeStruct((1, d), F32)],
        compiler_params=_params(("arbitrary",)),
    )(merged, w_out, x, target)


def _mm_merge_bwd(dout_b, w_out, proj, y_a, y_s, *, tm=256):
    rows, d = y_a.shape
    ncol = d // CW

    def body(do_ref, w_ref, *refs):
        ga_refs, gs_refs = refs[:ncol], refs[ncol:2 * ncol]
        ya_ref, ys_ref, dya_ref, dys_ref, dg_ref = refs[2 * ncol:]
        dm = lax.dot_general(do_ref[...].astype(_MXU), w_ref[...].astype(_MXU), _NT, preferred_element_type=F32)
        for j in range(ncol):
            cols = slice(j * CW, (j + 1) * CW)
            dmj = dm[:, cols]
            sa, ss = _sigmoid(ga_refs[j][...]), _sigmoid(gs_refs[j][...])
            dya_ref[:, cols] = (sa * dmj).astype(dya_ref.dtype)
            dys_ref[:, cols] = (ss * dmj).astype(dys_ref.dtype)
            dg_ref[:, cols] = (dmj * ya_ref[:, cols] * sa * (1.0 - sa)).astype(dg_ref.dtype)
            dg_ref[:, d + j * CW:d + (j + 1) * CW] = (dmj * ys_ref[:, cols] * ss * (1.0 - ss)).astype(dg_ref.dtype)

    tile = pl.BlockSpec((tm, d), lambda i: (i, 0))
    gate = [pl.BlockSpec((tm, CW), lambda i, c=off + j: (i, c)) for off in (OFF_GA, OFF_GS) for j in range(ncol)]
    both = pl.BlockSpec((pl.Element(tm), pl.Element(2 * d)), lambda i: (i * tm, OFF_GA * CW))
    return pl.pallas_call(
        body, name="mm_merge_bwd", grid=(rows // tm,),
        in_specs=[tile, pl.BlockSpec((d, d), lambda i: (0, 0))] + gate + [tile, tile],
        out_specs=[tile, tile, both],
        out_shape=[jax.ShapeDtypeStruct((rows, d), _MXU)] * 2 + [jax.ShapeDtypeStruct((rows, IN_W), _MXU)],
        compiler_params=_params(("arbitrary",)),
    )(dout_b, w_out, *([proj] * (2 * ncol)), y_a, y_s)


def _mm_ssm_gate_bwd(d_ys, w_sp4, glu, b_glu, proj, d_proj, *, tm=512):
    rows, w = glu.shape[0], glu.shape[1] // 2
    nk, tk = w_sp4.shape[0], w_sp4.shape[2]
    tm = min(tm, rows)

    def body(dy_ref, w_ref, ga_ref, gb_ref, ba_ref, bb_ref, z0_ref, z1_ref, buf_ref, dg_ref, dz_ref, db_ref, acc):
        i, k = pl.program_id(0), pl.program_id(1)

        @pl.when(k == 0)
        def _():
            acc[...] = jnp.zeros_like(acc)

        acc[...] += lax.dot_general(dy_ref[...].astype(_MXU), w_ref[...].astype(_MXU), _NT, preferred_element_type=F32)

        @pl.when(k == nk - 1)
        def _():
            dv = acc[...]
            a, sb = ga_ref[...] + ba_ref[...], _sigmoid(gb_ref[...] + bb_ref[...])
            f, df = _silu_and_grad(jnp.concatenate([z0_ref[...], z1_ref[...]], axis=1))
            dga = dv * sb * f
            dgb = dv * a * f * sb * (1.0 - sb)
            dg_ref[:, :w] = dga.astype(dg_ref.dtype)
            dg_ref[:, w:] = dgb.astype(dg_ref.dtype)
            dz_ref[...] = (dv * a * sb * df).astype(dz_ref.dtype)
            part = jnp.concatenate([_colsum(dga), _colsum(dgb)], axis=1)

            @pl.when(i == 0)
            def _():
                db_ref[...] = part

            @pl.when(i > 0)
            def _():
                db_ref[...] += part

    half = lambda c: pl.BlockSpec((tm, w), lambda i, k: (i, c))
    bias = lambda c: pl.BlockSpec((1, w), lambda i, k: (0, c))
    zcol = lambda c: pl.BlockSpec((tm, CW), lambda i, k: (i, OFF_Z + c))
    return pl.pallas_call(
        body, name="mm_ssm_gate_bwd", grid=(rows // tm, nk),
        in_specs=[pl.BlockSpec((tm, tk), lambda i, k: (i, k)), pl.BlockSpec((None, w, tk), lambda i, k: (k, 0, 0)),
                  half(0), half(1), bias(0), bias(1), zcol(0), zcol(1), _ANY],
        out_specs=[pl.BlockSpec((tm, 2 * w), lambda i, k: (i, 0)),
                   pl.BlockSpec((pl.Element(tm), pl.Element(w)), lambda i, k: (i * tm, OFF_Z * CW)),
                   pl.BlockSpec((1, 2 * w), lambda i, k: (0, 0))],
        out_shape=[jax.ShapeDtypeStruct((rows, 2 * w), _MXU), jax.ShapeDtypeStruct(d_proj.shape, d_proj.dtype),
                   jax.ShapeDtypeStruct((1, 2 * w), F32)],
        input_output_aliases={8: 1},
        scratch_shapes=[pltpu.VMEM((tm, w), F32)],
        compiler_params=_params(("arbitrary", "arbitrary")),
    )(d_ys, w_sp4, glu, glu, b_glu, b_glu, proj, proj, d_proj)


def _ew(fn, ins, outs, *, rows, ncol, name, n_acc=0, tm=512, deps=(), into=None):
    deps = list(deps) + ([into[1]] if into else [])
    n_in, n_out, nd = len(ins), len(outs), len(deps)
    tm = min(tm, rows)
    in_specs = []
    for _, kind, col0 in ins:
        if kind == "mat":
            in_specs.append(pl.BlockSpec((tm, CW), lambda j, i, c0=col0: (i, c0 + j)))
        else:
            in_specs.append(pl.BlockSpec((1, CW), lambda j, i, c0=col0: (0, c0 + j)))
    out_specs = [pl.BlockSpec((tm, CW), lambda j, i: (i, j)) for _ in outs]
    out_shape = [jax.ShapeDtypeStruct((rows, w), dt) for w, dt in outs]
    if into:
        out_specs[into[0]] = pl.BlockSpec((tm, CW), lambda j, i, c0=into[2]: (i, c0 + j))
        out_shape[into[0]] = jax.ShapeDtypeStruct(into[1].shape, into[1].dtype)
    for _ in range(n_acc):
        out_specs.append(pl.BlockSpec((1, CW), lambda j, i: (0, j)))
        out_shape.append(jax.ShapeDtypeStruct((1, ncol * CW), F32))

    def body(*refs):
        vals = fn(*[r[...] for r in refs[:n_in]])
        refs = refs[n_in + nd:]
        for r, v in zip(refs[:n_out], vals[:n_out]):
            r[...] = v.astype(r.dtype)
        i = pl.program_id(1)
        for r, v in zip(refs[n_out:], vals[n_out:]):
            @pl.when(i == 0)
            def _(r=r, v=v):
                r[...] = v

            @pl.when(i > 0)
            def _(r=r, v=v):
                r[...] += v

    res = pl.pallas_call(
        body, name=name, grid=(ncol, rows // tm), in_specs=in_specs + [_ANY] * nd, out_specs=out_specs,
        out_shape=out_shape, input_output_aliases={n_in + nd - 1: into[0]} if into else {},
        compiler_params=_params(("parallel", "arbitrary")),
    )(*[a for a, _, _ in ins], *deps)
    return res


def _colsum(v):
    return jnp.sum(v, axis=0, keepdims=True)


def _sigmoid(v):
    return jax.nn.sigmoid(v)


def _silu_and_grad(v):
    s = _sigmoid(v)
    return v * s, s * (1.0 + v * (1.0 - s))


def _rms_fwd(x, w, *, tm=512, deps=()):
    rows, d = x.shape
    nd = len(deps)

    def body(x_ref, w_ref, *rest):
        h_ref, r_ref = rest[nd:]
        xv = x_ref[...]
        r = lax.rsqrt(jnp.mean(xv * xv, axis=-1, keepdims=True) + NORM_EPS)
        h_ref[...] = (xv * r * w_ref[...]).astype(h_ref.dtype)
        r_ref[...] = r

    return pl.pallas_call(
        body, name="rms_fwd", grid=(rows // tm,),
        in_specs=[pl.BlockSpec((tm, d), lambda i: (i, 0)), pl.BlockSpec((1, d), lambda i: (0, 0))] + [_ANY] * nd,
        out_specs=[pl.BlockSpec((tm, d), lambda i: (i, 0)), pl.BlockSpec((tm, 1), lambda i: (i, 0))],
        out_shape=[jax.ShapeDtypeStruct((rows, d), _MXU), jax.ShapeDtypeStruct((rows, 1), F32)],
        compiler_params=_params(("arbitrary",)),
    )(x, w, *deps)


def _rms_bwd(dh, x, rstd, w, dout, *, tm=256):
    rows, d = x.shape

    def body(dh_ref, x_ref, r_ref, w_ref, do_ref, gx_ref, gw_ref):
        dhv, xv, r, wv = dh_ref[...], x_ref[...], r_ref[...], w_ref[...]
        xr = xv * r
        t = jnp.mean(dhv * wv * xr, axis=-1, keepdims=True)
        gx_ref[...] = do_ref[...] + r * (wv * dhv - xr * t)
        part = _colsum(dhv * xr)
        i = pl.program_id(0)

        @pl.when(i == 0)
        def _():
            gw_ref[...] = part

        @pl.when(i > 0)
        def _():
            gw_ref[...] += part

    return pl.pallas_call(
        body, name="rms_bwd", grid=(rows // tm,),
        in_specs=[pl.BlockSpec((tm, d), lambda i: (i, 0)), pl.BlockSpec((tm, d), lambda i: (i, 0)),
                  pl.BlockSpec((tm, 1), lambda i: (i, 0)), pl.BlockSpec((1, d), lambda i: (0, 0)),
                  pl.BlockSpec((tm, d), lambda i: (i, 0))],
        out_specs=[pl.BlockSpec((tm, d), lambda i: (i, 0)), pl.BlockSpec((1, d), lambda i: (0, 0))],
        out_shape=[jax.ShapeDtypeStruct((rows, d), F32), jax.ShapeDtypeStruct((1, d), F32)],
        compiler_params=_params(("arbitrary",)),
    )(dh, x, rstd, w, dout)


_NT = (((1,), (1,)), ((), ()))
_TN = (((0,), (0,)), ((), ()))


QKV_W = ATTN_W + 2 * KV_W
HEADS_PER_TILE = LANES // HEAD_DIM


def _low_half(rows):
    return lax.broadcasted_iota(jnp.int32, (rows, LANES), 1) < HEAD_DIM


def _pair_mean(t, low):
    m_lo = jnp.sum(jnp.where(low, t, 0.0), axis=-1, keepdims=True)
    m_hi = jnp.sum(jnp.where(low, 0.0, t), axis=-1, keepdims=True)
    return jnp.where(low, m_lo, m_hi) * (1.0 / HEAD_DIM)


def _pair_rstd(t, low):
    return lax.rsqrt(_pair_mean(t * t, low) + NORM_EPS)


def _dup_half(t, hi, low):
    swapped = pltpu.roll(t, HEAD_DIM, 1)
    return jnp.where(low, swapped, t) if hi else jnp.where(low, t, swapped)


def _fold_halves(t):
    return t + pltpu.roll(t, HEAD_DIM, 1)


def _split_heads(t, low):
    return [jnp.where(low, t, 0.0), jnp.where(low, 0.0, t)]


def _stacked_band_mask(n):
    rows = Q_PER_KV * WINDOW
    qi = lax.broadcasted_iota(jnp.int32, (rows, 2 * WINDOW), 0) % WINDOW + WINDOW
    kj = lax.broadcasted_iota(jnp.int32, (rows, 2 * WINDOW), 1)
    diff = qi - kj
    first_key = jnp.where(n > 0, 0, WINDOW)
    return (diff >= 0) & (diff < WINDOW) & (kj >= first_key)


def _stacked_sinks(sink_ref, g):
    blk = lax.broadcasted_iota(jnp.int32, (Q_PER_KV * WINDOW, 1), 0) // WINDOW
    col = jnp.full((Q_PER_KV * WINDOW, 1), sink_ref[Q_PER_KV * g], F32)
    for r in range(1, Q_PER_KV):
        col = jnp.where(blk == r, sink_ref[Q_PER_KV * g + r], col)
    return col


def _attn_in_specs(nblk, rev):
    def cur(n):
        return (nblk - 1 - n) if rev else n

    q_spec = pl.BlockSpec((WINDOW, ATTN_W), lambda n: (cur(n), 0))
    kvc_spec = pl.BlockSpec((WINDOW, 2 * KV_W), lambda n: (cur(n), ATTN_W // (2 * KV_W)))
    kvp_spec = pl.BlockSpec((WINDOW, 2 * KV_W), lambda n: (jnp.maximum(cur(n) - 1, 0), ATTN_W // (2 * KV_W)))
    w_spec = pl.BlockSpec((1, LANES), lambda n: (0, 0))
    l_spec = pl.BlockSpec((WINDOW, N_Q_HEADS), lambda n: (cur(n), 0))
    gate_specs = [pl.BlockSpec((WINDOW, CW), lambda n, col=OFF_AGATE + j: (cur(n), col)) for j in range(ATTN_W // CW)]
    return q_spec, kvc_spec, kvp_spec, w_spec, l_spec, gate_specs


def _attn2_fwd(proj, qw2, kw2, sinks, deps=()):
    seq = proj.shape[0]
    nblk = seq // WINDOW
    scale = 1.0 / math.sqrt(HEAD_DIM)
    q_spec, kvc_spec, kvp_spec, w_spec, l_spec, gate_specs = _attn_in_specs(nblk, False)
    nd, ng = len(deps), len(gate_specs)

    def body(sink_ref, q_ref, kvc_ref, kvp_ref, qw_ref, kw_ref, *rest):
        gate_refs = rest[:ng]
        o_ref, lse_ref, ya_ref = rest[ng + nd:]
        n = pl.program_id(0)
        low, low2 = _low_half(WINDOW), _low_half(2 * WINDOW)
        valid = _stacked_band_mask(n)
        head_lane = lax.broadcasted_iota(jnp.int32, (WINDOW, N_Q_HEADS), 1)
        kv = jnp.concatenate([kvp_ref[...], kvc_ref[...]], axis=0)
        qwv, kwv = qw_ref[...], kw_ref[...]
        lse_blk = jnp.zeros((WINDOW, N_Q_HEADS), F32)
        for t in range(N_KV_HEADS // HEADS_PER_TILE):
            kt = kv[:, t * LANES:(t + 1) * LANES]
            vt = kv[:, KV_W + t * LANES:KV_W + (t + 1) * LANES]
            kn = kt * _pair_rstd(kt, low2) * kwv
            for hi in range(HEADS_PER_TILE):
                g = HEADS_PER_TILE * t + hi
                kdup = _dup_half(kn, hi, low2).astype(_MXU)
                vdup = _dup_half(vt, hi, low2).astype(_MXU)
                stack = []
                for tq in (2 * g, 2 * g + 1):
                    qt = q_ref[:, tq * LANES:(tq + 1) * LANES]
                    stack += _split_heads(qt * _pair_rstd(qt, low) * qwv, low)
                qs = jnp.concatenate(stack, axis=0).astype(_MXU)
                s = lax.dot_general(qs, kdup, _NT, preferred_element_type=F32) * scale
                s = jnp.where(valid, s, -1e30)
                sink = _stacked_sinks(sink_ref, g)
                m = jnp.maximum(jnp.max(s, axis=-1, keepdims=True), sink)
                e = jnp.exp(s - m)
                z = jnp.sum(e, axis=-1, keepdims=True) + jnp.exp(sink - m)
                o = jnp.dot((e / z).astype(_MXU), vdup, preferred_element_type=F32)
                for i, tq in enumerate((2 * g, 2 * g + 1)):
                    o_ref[:, tq * LANES:(tq + 1) * LANES] = jnp.where(
                        low, o[2 * i * WINDOW:(2 * i + 1) * WINDOW], o[(2 * i + 1) * WINDOW:(2 * i + 2) * WINDOW])
                lse = m + jnp.log(z)
                for r in range(Q_PER_KV):
                    lse_blk = jnp.where(head_lane == Q_PER_KV * g + r, lse[r * WINDOW:(r + 1) * WINDOW], lse_blk)
        lse_ref[...] = lse_blk
        for j, g_ref in enumerate(gate_refs):
            cols = slice(j * CW, (j + 1) * CW)
            gate = g_ref[...]
            ya_ref[:, cols] = (o_ref[:, cols] * (gate * _sigmoid(gate))).astype(ya_ref.dtype)

    return pl.pallas_call(
        body, name="attn_fwd", grid=(nblk,),
        in_specs=[pl.BlockSpec(memory_space=pltpu.SMEM), q_spec, kvc_spec, kvp_spec, w_spec, w_spec] + gate_specs
        + [_ANY] * nd,
        out_specs=[q_spec, l_spec, q_spec],
        out_shape=[jax.ShapeDtypeStruct((seq, ATTN_W), F32), jax.ShapeDtypeStruct((seq, N_Q_HEADS), F32),
                   jax.ShapeDtypeStruct((seq, ATTN_W), _MXU)],
        compiler_params=_params(("arbitrary",)),
    )(sinks, proj, proj, proj, qw2, kw2, *([proj] * ng), *deps)


def _attn2_bwd(proj, qw2, kw2, sinks, lse, attn, dya, d_proj, deps=()):
    seq = proj.shape[0]
    nblk = seq // WINDOW
    scale = 1.0 / math.sqrt(HEAD_DIM)
    q_spec, kvc_spec, kvp_spec, w_spec, l_spec, gate_specs = _attn_in_specs(nblk, True)
    s_spec = pl.BlockSpec((1, N_Q_HEADS), lambda n: (0, 0))
    d_spec = pl.BlockSpec((WINDOW, QKV_W + ATTN_W), lambda n: (nblk - 1 - n, 0))
    deps = list(deps) + [d_proj]
    nd, ng = len(deps), len(gate_specs)

    def body(sink_ref, q_ref, kvc_ref, kvp_ref, qw_ref, kw_ref, lse_ref, attn_ref, dya_ref, *rest):
        gate_refs = rest[:ng]
        d_ref, dqw_ref, dkw_ref, dsk_ref, carry, do_ref = rest[ng + nd:]
        step = pl.program_id(0)
        n = nblk - 1 - step

        @pl.when(step == 0)
        def _():
            carry[...] = jnp.zeros_like(carry)
            dqw_ref[...] = jnp.zeros_like(dqw_ref)
            dkw_ref[...] = jnp.zeros_like(dkw_ref)
            dsk_ref[...] = jnp.zeros_like(dsk_ref)

        for j, g_ref in enumerate(gate_refs):
            cols = slice(j * CW, (j + 1) * CW)
            f, df = _silu_and_grad(g_ref[...])
            dv = dya_ref[:, cols]
            do_ref[:, cols] = dv * f
            d_ref[:, QKV_W + j * CW:QKV_W + (j + 1) * CW] = (dv * attn_ref[:, cols] * df).astype(d_ref.dtype)

        low, low2 = _low_half(WINDOW), _low_half(2 * WINDOW)
        valid = _stacked_band_mask(n)
        head_lane = lax.broadcasted_iota(jnp.int32, (WINDOW, N_Q_HEADS), 1)
        sink_lane = lax.broadcasted_iota(jnp.int32, (1, N_Q_HEADS), 1)
        kv = jnp.concatenate([kvp_ref[...], kvc_ref[...]], axis=0)
        qwv, kwv = qw_ref[...], kw_ref[...]
        lse_blk = lse_ref[...]
        dqw = jnp.zeros((1, LANES), F32)
        dkw = jnp.zeros((1, LANES), F32)
        dsk = jnp.zeros((1, N_Q_HEADS), F32)
        for t in range(N_KV_HEADS // HEADS_PER_TILE):
            kt = kv[:, t * LANES:(t + 1) * LANES]
            vt = kv[:, KV_W + t * LANES:KV_W + (t + 1) * LANES]
            rk = _pair_rstd(kt, low2)
            kn = kt * rk * kwv
            dkn_t = jnp.zeros((2 * WINDOW, LANES), F32)
            dv_t = jnp.zeros((2 * WINDOW, LANES), F32)
            for hi in range(HEADS_PER_TILE):
                g = HEADS_PER_TILE * t + hi
                kdup = _dup_half(kn, hi, low2).astype(_MXU)
                vdup = _dup_half(vt, hi, low2).astype(_MXU)
                tiles = (2 * g, 2 * g + 1)
                qx, rq, stack, dstack, lse_rows = [], [], [], [], []
                for tq in tiles:
                    qt = q_ref[:, tq * LANES:(tq + 1) * LANES]
                    r = _pair_rstd(qt, low)
                    rq.append(r)
                    qx.append(qt * r)
                    stack += _split_heads(qx[-1] * qwv, low)
                    dstack += _split_heads(do_ref[:, tq * LANES:(tq + 1) * LANES], low)
                for r in range(Q_PER_KV):
                    lse_rows.append(jnp.sum(jnp.where(head_lane == Q_PER_KV * g + r, lse_blk, 0.0), axis=-1, keepdims=True))
                qs = jnp.concatenate(stack, axis=0).astype(_MXU)
                dos = jnp.concatenate(dstack, axis=0).astype(_MXU)
                lse_col = jnp.concatenate(lse_rows, axis=0)
                s = lax.dot_general(qs, kdup, _NT, preferred_element_type=F32) * scale
                s = jnp.where(valid, s, -1e30)
                p = jnp.exp(s - lse_col)
                dp = lax.dot_general(dos, vdup, _NT, preferred_element_type=F32)
                dsum = jnp.sum(p * dp, axis=-1, keepdims=True)
                ds = (p * (dp - dsum) * scale).astype(_MXU)
                dsink = -jnp.exp(_stacked_sinks(sink_ref, g) - lse_col) * dsum
                for r in range(Q_PER_KV):
                    dsk = dsk + jnp.where(sink_lane == Q_PER_KV * g + r, _colsum(dsink[r * WINDOW:(r + 1) * WINDOW]), 0.0)
                dv_g = _fold_halves(lax.dot_general(p.astype(_MXU), dos, _TN, preferred_element_type=F32))
                dkn_g = _fold_halves(lax.dot_general(ds, qs, _TN, preferred_element_type=F32))
                dv_t = jnp.where(low2, dv_t, dv_g) if hi else jnp.where(low2, dv_g, dv_t)
                dkn_t = jnp.where(low2, dkn_t, dkn_g) if hi else jnp.where(low2, dkn_g, dkn_t)
                dqn = jnp.dot(ds, kdup, preferred_element_type=F32)
                for i, tq in enumerate(tiles):
                    dqn_t = jnp.where(low, dqn[2 * i * WINDOW:(2 * i + 1) * WINDOW],
                                      dqn[(2 * i + 1) * WINDOW:(2 * i + 2) * WINDOW])
                    dq = rq[i] * (qwv * dqn_t - qx[i] * _pair_mean(dqn_t * qwv * qx[i], low))
                    d_ref[:, tq * LANES:(tq + 1) * LANES] = dq.astype(d_ref.dtype)
                    dqw = dqw + _colsum(dqn_t * qx[i])
            k_cols = slice(t * LANES, (t + 1) * LANES)
            v_cols = slice(KV_W + t * LANES, KV_W + (t + 1) * LANES)
            dkn_c = dkn_t[WINDOW:] + carry[:, k_cols]
            rc = rk[WINDOW:]
            kx = kt[WINDOW:] * rc
            dk = rc * (kwv * dkn_c - kx * _pair_mean(dkn_c * kwv * kx, low))
            d_ref[:, ATTN_W + t * LANES:ATTN_W + (t + 1) * LANES] = dk.astype(d_ref.dtype)
            d_ref[:, ATTN_W + KV_W + t * LANES:ATTN_W + KV_W + (t + 1) * LANES] = (
                dv_t[WINDOW:] + carry[:, v_cols]).astype(d_ref.dtype)
            carry[:, k_cols] = dkn_t[:WINDOW]
            carry[:, v_cols] = dv_t[:WINDOW]
            dkw = dkw + _colsum(dkn_c * kx)
        dqw_ref[...] += dqw
        dkw_ref[...] += dkw
        dsk_ref[...] += dsk

    return pl.pallas_call(
        body, name="attn_bwd", grid=(nblk,),
        in_specs=[pl.BlockSpec(memory_space=pltpu.SMEM), q_spec, kvc_spec, kvp_spec, w_spec, w_spec, l_spec, q_spec,
                  q_spec] + gate_specs + [_ANY] * nd,
        out_specs=[d_spec, w_spec, w_spec, s_spec],
        out_shape=[jax.ShapeDtypeStruct(d_proj.shape, d_proj.dtype), jax.ShapeDtypeStruct((1, LANES), F32),
                   jax.ShapeDtypeStruct((1, LANES), F32), jax.ShapeDtypeStruct((1, N_Q_HEADS), F32)],
        input_output_aliases={9 + ng + nd - 1: 0},
        scratch_shapes=[pltpu.VMEM((WINDOW, 2 * KV_W), F32), pltpu.VMEM((WINDOW, ATTN_W), F32)],
        compiler_params=_params(("arbitrary",)),
    )(sinks, proj, proj, proj, qw2, kw2, lse, attn, dya, *([proj] * ng), *deps)


def _ssm_discretise(a_re, a_im, log_dt):
    dt = jnp.exp(log_dt)
    mag = jnp.exp(dt * a_re)
    ab_re = mag * jnp.cos(dt * a_im)
    ab_im = mag * jnp.sin(dt * a_im)
    num_re = ab_re - 1.0
    num_im = ab_im
    den = a_re * a_re + a_im * a_im
    cf_re = (num_re * a_re + num_im * a_im) / den
    cf_im = (num_im * a_re - num_re * a_im) / den
    return ab_re, ab_im, cf_re, cf_im


def _ssm_params_fwd(a_re, a_im, log_dt):
    shp = jax.ShapeDtypeStruct(a_re.shape, F32)

    def body(are_ref, aim_ref, ldt_ref, abr_ref, abi_ref, cfr_ref, cfi_ref, alr_ref, ali_ref):
        abr, abi, cfr, cfi = _ssm_discretise(are_ref[...], aim_ref[...], ldt_ref[...])
        abr_ref[...], abi_ref[...], cfr_ref[...], cfi_ref[...] = abr, abi, cfr, cfi
        pr, pi = abr, abi
        for _ in range(int(math.log2(SSM_L))):
            pr, pi = pr * pr - pi * pi, 2.0 * pr * pi
        alr_ref[...], ali_ref[...] = pr, pi

    return pl.pallas_call(body, name="ssm_params_fwd", out_shape=[shp] * 6)(a_re, a_im, log_dt)


def _ssm_params_bwd(a_re, a_im, log_dt, d_abr, d_abi, d_cfr, d_cfi):
    def body(are_ref, aim_ref, ldt_ref, g0, g1, g2, g3, dare_ref, daim_ref, dldt_ref):
        _, vjp = jax.vjp(_ssm_discretise, are_ref[...], aim_ref[...], ldt_ref[...])
        dare_ref[...], daim_ref[...], dldt_ref[...] = vjp((g0[...], g1[...], g2[...], g3[...]))

    return pl.pallas_call(
        body, name="ssm_params_bwd",
        out_shape=[jax.ShapeDtypeStruct(a_re.shape, F32), jax.ShapeDtypeStruct(a_im.shape, F32),
                   jax.ShapeDtypeStruct(log_dt.shape, F32)],
    )(a_re, a_im, log_dt, d_abr, d_abi, d_cfr, d_cfi)


def _scan_cols(j):
    return pl.ds(j * SSM_SB, SSM_SB)


def _rows8(r):
    return pl.ds(pl.multiple_of(r * SUBLANES, SUBLANES), SUBLANES)


def _bcast8(row):
    return jnp.broadcast_to(row, (SUBLANES, row.shape[-1]))


def _token_order_pick():
    tok = lax.broadcasted_iota(jnp.int32, (SSM_T, SSM_T), 0)
    row = lax.broadcasted_iota(jnp.int32, (SSM_T, SSM_T), 1)
    return (row == SUBLANES * (tok % SSM_L) + tok // SSM_L).astype(_MXU)


SCAN_UNROLL = 8


def _scan_loop(n, step, init):
    def trip(o, carry):
        for i in range(SCAN_UNROLL):
            carry = step(o * SCAN_UNROLL + i, carry)
        return carry

    return lax.fori_loop(0, n // SCAN_UNROLL, trip, init)


def _ssm_fwd(u, b_re, b_im, c_re, c_im, d_skip, coef):
    seq = u.shape[0]
    nc = seq // SSM_T
    T, L = SSM_T, SSM_L

    def body(u_ref, bre_ref, bim_ref, cre_ref, cim_ref, d_ref, are_ref, aim_ref, cfr_ref, cfi_ref, alr_ref, ali_ref,
             y_ref, yg_ref, sre_ref, sim_ref, ire_ref, iim_ref, car_re, car_im, end_re, end_im, yg_scan):
        c = pl.program_id(0)

        @pl.when(c == 0)
        def _():
            car_re[...] = jnp.zeros_like(car_re)
            car_im[...] = jnp.zeros_like(car_im)

        for j in range(SSM_JB):
            ub = u_ref[:, j * LANES:(j + 1) * LANES].astype(_MXU)
            bur = jnp.dot(ub, bre_ref[j], preferred_element_type=F32)
            bui = jnp.dot(ub, bim_ref[j], preferred_element_type=F32)
            cfr, cfi = cfr_ref[:, _scan_cols(j)], cfi_ref[:, _scan_cols(j)]
            sre_ref[:, _scan_cols(j)] = cfr * bur - cfi * bui
            sim_ref[:, _scan_cols(j)] = cfr * bui + cfi * bur

        for j in range(SSM_JB):
            cols = _scan_cols(j)
            ar, ai = _bcast8(are_ref[:, cols]), _bcast8(aim_ref[:, cols])

            def step1(r, s, cols=cols, ar=ar, ai=ai):
                sr, si = s
                rows = _rows8(r)
                return (ar * sr - ai * si + sre_ref[rows, cols], ar * si + ai * sr + sim_ref[rows, cols])

            zero = jnp.zeros((SUBLANES, SSM_SB), F32)
            er, ei = _scan_loop(L, step1, (zero, zero))
            end_re[:, cols] = er
            end_im[:, cols] = ei

        alr, ali = alr_ref[...], ali_ref[...]
        cr, ci = car_re[...], car_im[...]
        ire_ref[0:1, :] = cr
        iim_ref[0:1, :] = ci
        for i in range(1, SUBLANES):
            er, ei = end_re[i - 1:i, :], end_im[i - 1:i, :]
            cr, ci = alr * cr - ali * ci + er, alr * ci + ali * cr + ei
            ire_ref[i:i + 1, :] = cr
            iim_ref[i:i + 1, :] = ci

        for j in range(SSM_JB):
            cols = _scan_cols(j)
            ar, ai = _bcast8(are_ref[:, cols]), _bcast8(aim_ref[:, cols])

            def step2(r, s, cols=cols, ar=ar, ai=ai):
                sr, si = s
                rows = _rows8(r)
                nr = ar * sr - ai * si + sre_ref[rows, cols]
                ni = ar * si + ai * sr + sim_ref[rows, cols]
                sre_ref[rows, cols] = nr
                sim_ref[rows, cols] = ni
                return nr, ni

            _scan_loop(L, step2, (ire_ref[:, cols], iim_ref[:, cols]))

        car_re[...] = sre_ref[T - 1:T, :]
        car_im[...] = sim_ref[T - 1:T, :]

        for j in range(SSM_JB):
            cols = _scan_cols(j)
            ch = slice(j * LANES, (j + 1) * LANES)
            y = (jnp.dot(sre_ref[:, cols].astype(_MXU), cre_ref[j], preferred_element_type=F32)
                 - jnp.dot(sim_ref[:, cols].astype(_MXU), cim_ref[j], preferred_element_type=F32))
            y = y + d_ref[:, ch] * u_ref[:, ch]
            y_ref[:, ch] = y
            yg_scan[:, ch] = jax.nn.gelu(y).astype(yg_scan.dtype)
        yg_ref[...] = jnp.dot(_token_order_pick(), yg_scan[...], preferred_element_type=F32).astype(yg_ref.dtype)

    tok = pl.BlockSpec((T, SSM_W), lambda c: (c, 0))
    st = pl.BlockSpec((T, N_STATES), lambda c: (c, 0))
    ini = pl.BlockSpec((None, SUBLANES, N_STATES), lambda c: (c, 0, 0))
    bsp = pl.BlockSpec((SSM_JB, LANES, SSM_SB), lambda c: (0, 0, 0))
    csp = pl.BlockSpec((SSM_JB, SSM_SB, LANES), lambda c: (0, 0, 0))
    row_w = pl.BlockSpec((1, SSM_W), lambda c: (0, 0))
    row_s = pl.BlockSpec((1, N_STATES), lambda c: (0, 0))
    return pl.pallas_call(
        body, name="ssm_fwd", grid=(nc,),
        in_specs=[tok, bsp, bsp, csp, csp, row_w] + [row_s] * 6,
        out_specs=[tok, tok, st, st, ini, ini],
        out_shape=[jax.ShapeDtypeStruct((seq, SSM_W), F32), jax.ShapeDtypeStruct((seq, SSM_W), _MXU),
                   jax.ShapeDtypeStruct((seq, N_STATES), F32), jax.ShapeDtypeStruct((seq, N_STATES), F32),
                   jax.ShapeDtypeStruct((nc, SUBLANES, N_STATES), F32),
                   jax.ShapeDtypeStruct((nc, SUBLANES, N_STATES), F32)],
        scratch_shapes=[pltpu.VMEM((1, N_STATES), F32), pltpu.VMEM((1, N_STATES), F32),
                        pltpu.VMEM((SUBLANES, N_STATES), F32), pltpu.VMEM((SUBLANES, N_STATES), F32),
                        pltpu.VMEM((T, SSM_W), _MXU)],
        compiler_params=_params(("arbitrary",)),
    )(u, b_re, b_im, c_re, c_im, d_skip, *coef)


def _ssm_bwd(dyg, y, u, s_re, s_im, i_re, i_im, b_re, b_im, c_re, c_im, d_skip, coef, d_proj, deps=()):
    seq = u.shape[0]
    nc = seq // SSM_T
    T, L = SSM_T, SSM_L
    deps = list(deps) + [d_proj]

    def body(dyg_ref, y_ref, u_ref, sre_ref, sim_ref, ire_ref, iim_ref, bre_ref, bim_ref, cre_ref, cim_ref, d_ref,
             are_ref, aim_ref, cfr_ref, cfi_ref, alr_ref, ali_ref, *rest):
        (du_ref, dbre_out, dbim_out, dcre_out, dcim_out, dd_ref, dar_ref, dai_ref, dcfr_ref, dcfi_ref,
         lre, lim, car_re, car_im, end_re, end_im, ini_re, ini_im, dbre_ref, dbim_ref, dcre_ref, dcim_ref,
         dy_ref, du_scan) = rest[len(deps):]
        step = pl.program_id(0)
        dy_ref[...] = jax.vjp(jax.nn.gelu, y_ref[...])[1](dyg_ref[...])[0]

        @pl.when(step == 0)
        def _():
            car_re[...] = jnp.zeros_like(car_re)
            car_im[...] = jnp.zeros_like(car_im)
            for ref in (dbre_ref, dbim_ref, dcre_ref, dcim_ref, dd_ref, dar_ref, dai_ref, dcfr_ref, dcfi_ref):
                ref[...] = jnp.zeros_like(ref)

        for j in range(SSM_JB):
            dyb = dy_ref[:, j * LANES:(j + 1) * LANES].astype(_MXU)
            lre[:, _scan_cols(j)] = lax.dot_general(dyb, cre_ref[j], _NT, preferred_element_type=F32)
            lim[:, _scan_cols(j)] = -lax.dot_general(dyb, cim_ref[j], _NT, preferred_element_type=F32)

        for j in range(SSM_JB):
            cols = _scan_cols(j)
            ar, ai = _bcast8(are_ref[:, cols]), _bcast8(aim_ref[:, cols])

            def step1(t, s, cols=cols, ar=ar, ai=ai):
                sr, si = s
                rows = _rows8(L - 1 - t)
                return (ar * sr + ai * si + lre[rows, cols], ar * si - ai * sr + lim[rows, cols])

            zero = jnp.zeros((SUBLANES, SSM_SB), F32)
            er, ei = _scan_loop(L, step1, (zero, zero))
            end_re[:, cols] = er
            end_im[:, cols] = ei

        alr, ali = alr_ref[...], ali_ref[...]
        cr, ci = car_re[...], car_im[...]
        ini_re[SUBLANES - 1:SUBLANES, :] = cr
        ini_im[SUBLANES - 1:SUBLANES, :] = ci
        for i in range(SUBLANES - 2, -1, -1):
            er, ei = end_re[i + 1:i + 2, :], end_im[i + 1:i + 2, :]
            cr, ci = alr * cr + ali * ci + er, alr * ci - ali * cr + ei
            ini_re[i:i + 1, :] = cr
            ini_im[i:i + 1, :] = ci

        for j in range(SSM_JB):
            cols = _scan_cols(j)
            ar, ai = _bcast8(are_ref[:, cols]), _bcast8(aim_ref[:, cols])

            def step2(t, s, cols=cols, ar=ar, ai=ai):
                sr, si = s
                rows = _rows8(L - 1 - t)
                nr = ar * sr + ai * si + lre[rows, cols]
                ni = ar * si - ai * sr + lim[rows, cols]
                lre[rows, cols] = nr
                lim[rows, cols] = ni
                return nr, ni

            _scan_loop(L, step2, (ini_re[:, cols], ini_im[:, cols]))

        car_re[...] = lre[0:1, :]
        car_im[...] = lim[0:1, :]

        head, tail, body_rows = slice(0, SUBLANES), slice(SUBLANES, T), slice(0, T - SUBLANES)
        for j in range(SSM_JB):
            cols = _scan_cols(j)
            ch = slice(j * LANES, (j + 1) * LANES)
            lr, li = lre[:, cols], lim[:, cols]
            lt_r, lt_i, sp_r, sp_i = lre[tail, cols], lim[tail, cols], sre_ref[body_rows, cols], sim_ref[body_rows, cols]
            lh_r, lh_i, si_r, si_i = lre[head, cols], lim[head, cols], ire_ref[:, cols], iim_ref[:, cols]
            dar_ref[:, cols] += _colsum(lt_r * sp_r + lt_i * sp_i) + _colsum(lh_r * si_r + lh_i * si_i)
            dai_ref[:, cols] += _colsum(lt_i * sp_r - lt_r * sp_i) + _colsum(lh_i * si_r - lh_r * si_i)
            uf = u_ref[:, ch]
            ub = uf.astype(_MXU)
            bur = jnp.dot(ub, bre_ref[j], preferred_element_type=F32)
            bui = jnp.dot(ub, bim_ref[j], preferred_element_type=F32)
            dcfr_ref[:, cols] += _colsum(lr * bur + li * bui)
            dcfi_ref[:, cols] += _colsum(li * bur - lr * bui)
            cfr, cfi = cfr_ref[:, cols], cfi_ref[:, cols]
            dbur = (cfr * lr + cfi * li).astype(_MXU)
            dbui = (cfr * li - cfi * lr).astype(_MXU)
            dyf = dy_ref[:, ch]
            dyb = dyf.astype(_MXU)
            du = (lax.dot_general(dbur, bre_ref[j], _NT, preferred_element_type=F32)
                  + lax.dot_general(dbui, bim_ref[j], _NT, preferred_element_type=F32) + d_ref[:, ch] * dyf)
            du_scan[:, ch] = du.astype(du_scan.dtype)
            dbre_ref[j] += lax.dot_general(ub, dbur, _TN, preferred_element_type=F32)
            dbim_ref[j] += lax.dot_general(ub, dbui, _TN, preferred_element_type=F32)
            dcre_ref[j] += lax.dot_general(sre_ref[:, cols].astype(_MXU), dyb, _TN, preferred_element_type=F32)
            dcim_ref[j] -= lax.dot_general(sim_ref[:, cols].astype(_MXU), dyb, _TN, preferred_element_type=F32)
            dd_ref[:, ch] += _colsum(dyf * uf)
        du_ref[...] = jnp.dot(_token_order_pick(), du_scan[...], preferred_element_type=F32).astype(du_ref.dtype)

        @pl.when(step == nc - 1)
        def _():
            for acc, out in ((dbre_ref, dbre_out), (dbim_ref, dbim_out), (dcre_ref, dcre_out), (dcim_ref, dcim_out)):
                pltpu.sync_copy(acc, out)

    tok = pl.BlockSpec((T, SSM_W), lambda c: (nc - 1 - c, 0))
    st = pl.BlockSpec((T, N_STATES), lambda c: (nc - 1 - c, 0))
    ini = pl.BlockSpec((None, SUBLANES, N_STATES), lambda c: (nc - 1 - c, 0, 0))
    bsp = pl.BlockSpec((SSM_JB, LANES, SSM_SB), lambda c: (0, 0, 0))
    csp = pl.BlockSpec((SSM_JB, SSM_SB, LANES), lambda c: (0, 0, 0))
    row_w = pl.BlockSpec((1, SSM_W), lambda c: (0, 0))
    row_s = pl.BlockSpec((1, N_STATES), lambda c: (0, 0))
    big = pltpu.VMEM((T, N_STATES), F32)
    one = pltpu.VMEM((1, N_STATES), F32)
    eight = pltpu.VMEM((SUBLANES, N_STATES), F32)
    return pl.pallas_call(
        body, name="ssm_bwd", grid=(nc,),
        in_specs=[tok, tok, tok, st, st, ini, ini, bsp, bsp, csp, csp, row_w] + [row_s] * 6 + [_ANY] * len(deps),
        out_specs=[pl.BlockSpec((pl.Element(T), pl.Element(SSM_W)), lambda c: ((nc - 1 - c) * T, OFF_U * CW)),
                   _ANY, _ANY, _ANY, _ANY, row_w, row_s, row_s, row_s, row_s],
        input_output_aliases={18 + len(deps) - 1: 0},
        out_shape=[jax.ShapeDtypeStruct(d_proj.shape, d_proj.dtype),
                   jax.ShapeDtypeStruct((SSM_JB, LANES, SSM_SB), F32), jax.ShapeDtypeStruct((SSM_JB, LANES, SSM_SB), F32),
                   jax.ShapeDtypeStruct((SSM_JB, SSM_SB, LANES), F32), jax.ShapeDtypeStruct((SSM_JB, SSM_SB, LANES), F32),
                   jax.ShapeDtypeStruct((1, SSM_W), F32)] + [jax.ShapeDtypeStruct((1, N_STATES), F32)] * 4,
        scratch_shapes=[big, big, one, one, eight, eight, eight, eight,
                        pltpu.VMEM((SSM_JB, LANES, SSM_SB), F32), pltpu.VMEM((SSM_JB, LANES, SSM_SB), F32),
                        pltpu.VMEM((SSM_JB, SSM_SB, LANES), F32), pltpu.VMEM((SSM_JB, SSM_SB, LANES), F32),
                        pltpu.VMEM((T, SSM_W), F32), pltpu.VMEM((T, SSM_W), _MXU)],
        compiler_params=_params(("arbitrary",)),
    )(dyg, y, u, s_re, s_im, i_re, i_im, b_re, b_im, c_re, c_im, d_skip, *coef, *deps)


def _block_diag_b(b):
    t = b.reshape(SSM_JB, 8, STATE, GROUP).transpose(0, 1, 3, 2)
    eye = jnp.eye(8, dtype=b.dtype)
    return (t[:, :, :, None, :] * eye[None, :, None, :, None]).reshape(SSM_JB, LANES, SSM_SB)


def _block_diag_c(c):
    t = c.reshape(SSM_JB, 8, GROUP, STATE).transpose(0, 1, 3, 2)
    eye = jnp.eye(8, dtype=c.dtype)
    return (t[:, :, :, None, :] * eye[None, :, None, :, None]).reshape(SSM_JB, SSM_SB, LANES)


def _diag_of_b(blk):
    t = blk.reshape(SSM_JB, 8, GROUP, 8, STATE)
    d = jnp.sum(t * jnp.eye(8, dtype=blk.dtype)[None, :, None, :, None], axis=3)
    return d.transpose(0, 1, 3, 2).reshape(N_GROUPS, STATE, GROUP)


def _diag_of_c(blk):
    t = blk.reshape(SSM_JB, 8, STATE, 8, GROUP)
    d = jnp.sum(t * jnp.eye(8, dtype=blk.dtype)[None, :, None, :, None], axis=3)
    return d.transpose(0, 1, 3, 2).reshape(N_GROUPS, GROUP, STATE)


def _to_scan_order(v):
    seq, w = v.shape
    return v.reshape(seq // SSM_T, SUBLANES, SSM_L, w).transpose(0, 2, 1, 3).reshape(seq, w)


def _adamw_math(w, g, m, v):
    nm = ADAM_B1 * m + (1.0 - ADAM_B1) * g
    nv = ADAM_B2 * v + (1.0 - ADAM_B2) * jnp.square(g)
    m_hat = nm / (1.0 - ADAM_B1 ** ADAM_STEP)
    v_hat = nv / (1.0 - ADAM_B2 ** ADAM_STEP)
    return -ADAM_LR * (m_hat / (jnp.sqrt(v_hat) + ADAM_EPS) + ADAM_WD * w), nm, nv


def _adamw(w, g, m, v, *, name, tm, deps=()):
    rows, cols = w.shape
    nd = len(deps)

    def body(w_ref, g_ref, m_ref, v_ref, *rest):
        d_ref, nm_ref, nv_ref = rest[nd:]
        d_ref[...], nm_ref[...], nv_ref[...] = _adamw_math(w_ref[...], g_ref[...], m_ref[...], v_ref[...])

    spec = pl.BlockSpec((tm, cols), lambda i: (i, 0))
    shp = jax.ShapeDtypeStruct((rows, cols), F32)
    return pl.pallas_call(body, name=name, grid=(rows // tm,), in_specs=[spec] * 4 + [_ANY] * nd,
                          out_specs=[spec] * 3, out_shape=[shp] * 3,
                          compiler_params=_params(("arbitrary",)))(w, g, m, v, *deps)


def _place():
    x, y, c = lax.axis_index("x"), lax.axis_index("y"), lax.axis_index("c")
    chips = [(1 - x, y), (x, 1 - y), (1 - x, 1 - y)]
    return x, y, c, chips


def _remote(src, dst, send_sem, recv_sem, dev):
    return pltpu.make_async_remote_copy(src_ref=src, dst_ref=dst, send_sem=send_sem, recv_sem=recv_sem,
                                        device_id=dev, device_id_type=MESH)


def _place_shard(w, mine_arr, *, name, tm=256, deps=()):
    rows, cols = w.shape

    def body(m_ref, w_ref, *rest):
        rest[-1][...] = w_ref[...].astype(rest[-1].dtype)

    return pl.pallas_call(
        body, name=name,
        grid_spec=pltpu.PrefetchScalarGridSpec(
            num_scalar_prefetch=1, grid=(rows // tm,),
            in_specs=[pl.BlockSpec((tm, cols), lambda i, m: (i, 0))] + [_ANY] * len(deps),
            out_specs=pl.BlockSpec((None, tm, cols), lambda i, m: (m[0], i, 0))),
        out_shape=jax.ShapeDtypeStruct((N_CHIPS, rows, cols), _WIRE),
        compiler_params=_params(("arbitrary",)),
    )(mine_arr, w, *deps)


_HBM = pl.BlockSpec(memory_space=pltpu.HBM)
_SEM = pl.BlockSpec(memory_space=pltpu.SEMAPHORE)
_EFFECT = pltpu.SideEffectType.DATAFLOW_SIDE_EFFECTING


def _copies_start(name, bufs, plan, count, after=()):
    nb, na = len(bufs), len(after)

    def body(*refs):
        send_sems, recv_sems, token = refs[nb + na], refs[nb + na + 1], refs[-1]
        copies = plan(refs[:nb])
        assert len(copies) == count
        for i, (src, dst, dev, _) in enumerate(copies):
            _remote(src, dst, send_sems.at[i], recv_sems.at[i], dev).start()
        token[...] = jnp.zeros_like(token)

    res = pl.pallas_call(
        body, name=name, in_specs=[_HBM] * nb + [_ANY] * na,
        out_specs=(_SEM, _SEM, *[_HBM] * nb, pl.BlockSpec(memory_space=pltpu.VMEM)),
        out_shape=(pltpu.SemaphoreType.DMA((count,)), pltpu.SemaphoreType.DMA((count,)),
                   *[pltpu.HBM(b.shape, b.dtype) for b in bufs], jax.ShapeDtypeStruct((SUBLANES, LANES), F32)),
        input_output_aliases={i: 2 + i for i in range(nb)},
        compiler_params=pltpu.CompilerParams(has_side_effects=_EFFECT),
    )(*[pltpu.with_memory_space_constraint(b, pltpu.HBM) for b in bufs], *after)
    return (res[0], res[1]), list(res[2:2 + nb]), res[-1]


def _copies_wait(name, bufs, sems, plan, after=(), which=None):
    nb, na = len(bufs), len(after)

    def body(*refs):
        send_sems, recv_sems = refs[nb], refs[nb + 1]
        for i, (src, _, dev, land) in enumerate(plan(refs[:nb])):
            if which is not None and i not in which:
                continue
            cp = _remote(src, land, send_sems.at[i], recv_sems.at[i], dev)
            cp.wait_send()
            cp.wait_recv()

    res = pl.pallas_call(
        body, name=name, in_specs=[_HBM] * nb + [_SEM, _SEM] + [_ANY] * na, out_specs=[_HBM] * nb,
        out_shape=[pltpu.HBM(b.shape, b.dtype) for b in bufs],
        input_output_aliases={i: i for i in range(nb)},
        compiler_params=pltpu.CompilerParams(has_side_effects=_EFFECT),
    )(*bufs, *sems, *after)
    return list(res)


def _plan_gather_ici(fulls, which=(0, 1, 2)):
    x, y, c, chips = _place()
    copies = []
    for f in fulls:
        half = pl.ds(c * (f.shape[1] // 2), f.shape[1] // 2)
        own = f.at[2 * x + y, half]
        for chip in [chips[k] for k in which]:
            copies.append((own, own, (*chip, c), f.at[2 * chip[0] + chip[1], half]))
    return copies


def _plan_gather_d2d(fulls, which=(0, 1, 2)):
    x, y, c, chips = _place()
    copies = []
    for f in fulls:
        r2 = f.shape[1] // 2
        for chip in [chips[k] for k in which]:
            blk = 2 * chip[0] + chip[1]
            landed = f.at[blk, pl.ds(c * r2, r2)]
            copies.append((landed, landed, (x, y, 1 - c), f.at[blk, pl.ds((1 - c) * r2, r2)]))
    return copies


def _plan_relay_direct(fulls):
    (f,) = fulls
    x, y, c, chips = _place()
    half = pl.ds(c * (f.shape[1] // 2), f.shape[1] // 2)
    own = f.at[2 * x + y, half]
    return [(own, own, (*chip, c), f.at[2 * chip[0] + chip[1], half]) for chip in chips[:2]]


def _plan_relay_forward(fulls, k):
    (f,) = fulls
    x, y, c, chips = _place()
    r2 = f.shape[1] // 2
    half, other = pl.ds(c * r2, r2), pl.ds((1 - c) * r2, r2)
    quarter = pl.ds(c * r2 + k * (r2 // 2), r2 // 2)
    blk, far = 2 * chips[k][0] + chips[k][1], 2 * chips[2][0] + chips[2][1]
    passed, landed = f.at[blk, quarter], f.at[blk, half]
    return [(passed, passed, (*chips[1 - k], c), f.at[far, quarter]), (landed, landed, (x, y, 1 - c), f.at[blk, other])]


def _plan_relay_last(fulls):
    (f,) = fulls
    x, y, c, chips = _place()
    r2 = f.shape[1] // 2
    far = 2 * chips[2][0] + chips[2][1]
    landed = f.at[far, pl.ds(c * r2, r2)]
    return [(landed, landed, (x, y, 1 - c), f.at[far, pl.ds((1 - c) * r2, r2)])]


def _plan_swap_halves(refs):
    x, y, c, _ = _place()
    n = len(refs) // 2
    copies = []
    for g, land in zip(refs[:n], refs[n:]):
        r2 = g.shape[1] // 2
        copies.append((g.at[:, pl.ds((1 - c) * r2, r2), :], land, (x, y, 1 - c), land))
    return copies


def _plan_scatter_chips(refs):
    x, y, c, chips = _place()
    n = len(refs) // 2
    copies = []
    for h, land in zip(refs[:n], refs[n:]):
        for k, chip in enumerate(chips):
            copies.append((h.at[2 * chip[0] + chip[1]], land.at[k], (*chip, c), land.at[k]))
    return copies


def _plan_join_halves(totals):
    x, y, c, _ = _place()
    copies = []
    for t in totals:
        r2 = t.shape[0] // 2
        mine = t.at[pl.ds(c * r2, r2)]
        copies.append((mine, mine, (x, y, 1 - c), t.at[pl.ds((1 - c) * r2, r2)]))
    return copies


def _add_sibling_half(g, got, c_arr, *, name, tm):
    _, rows, cols = g.shape
    r2 = rows // 2
    nb = r2 // tm

    def body(c_ref, g_ref, r_ref, o_ref):
        o_ref[...] = (g_ref[...].astype(F32) + r_ref[...].astype(F32)).astype(o_ref.dtype)

    return pl.pallas_call(
        body, name=name,
        grid_spec=pltpu.PrefetchScalarGridSpec(
            num_scalar_prefetch=1, grid=(N_CHIPS, nb),
            in_specs=[pl.BlockSpec((None, tm, cols), lambda b, i, c: (b, c[0] * nb + i, 0)),
                      pl.BlockSpec((None, tm, cols), lambda b, i, c: (b, i, 0))],
            out_specs=pl.BlockSpec((None, tm, cols), lambda b, i, c: (b, i, 0))),
        out_shape=jax.ShapeDtypeStruct((N_CHIPS, r2, cols), _WIRE),
        compiler_params=_params(("arbitrary", "arbitrary")),
    )(c_arr, g, got)


def _add_chips(h, got, place_arr, *, name, tm):
    _, r2, cols = h.shape
    nb = r2 // tm

    def body(p_ref, h_ref, r_ref, o_ref):
        o_ref[...] = ((h_ref[...].astype(F32) + r_ref[0].astype(F32)) + r_ref[1].astype(F32)) + r_ref[2].astype(F32)

    return pl.pallas_call(
        body, name=name,
        grid_spec=pltpu.PrefetchScalarGridSpec(
            num_scalar_prefetch=1, grid=(nb,),
            in_specs=[pl.BlockSpec((None, tm, cols), lambda i, p: (p[0], i, 0)),
                      pl.BlockSpec((3, tm, cols), lambda i, p: (0, i, 0))],
            out_specs=pl.BlockSpec((tm, cols), lambda i, p: (p[1] * nb + i, 0))),
        out_shape=jax.ShapeDtypeStruct((2 * r2, cols), F32),
        compiler_params=_params(("arbitrary",)),
    )(place_arr, h, got)


class _ReduceScatter:
    def __init__(self, tag, names, grads):
        self.tag, self.names, self.n = tag, names, len(names)
        core = lax.axis_index("c").astype(jnp.int32)
        chip = (2 * lax.axis_index("x") + lax.axis_index("y")).astype(jnp.int32)
        self.c_arr, self.place_arr = core.reshape(1), jnp.stack([chip, core])
        self.bufs = list(grads)

    def _start(self, step, bufs, plan, count, after):
        self.plan = plan
        self.step = f"grad_{step}_{self.tag}"
        self.sems, self.bufs, token = _copies_start(self.step + "_start", bufs, plan, count, after)
        return [token]

    def _wait(self, after):
        self.bufs = _copies_wait(self.step + "_wait", self.bufs, self.sems, self.plan, after)
        return self.bufs

    def start_swap(self, after=()):
        lands = [lax.empty((N_CHIPS, g.shape[1] // 2, g.shape[2]), g.dtype) for g in self.bufs]
        return self._start("swap", self.bufs + lands, _plan_swap_halves, self.n, after)

    def start_scatter(self, after):
        bufs = self._wait(after)
        pair = [_add_sibling_half(g, r, self.c_arr, name=f"grad_add_sibling_{nm}", tm=min(256, g.shape[1] // 2))
                for nm, g, r in zip(self.names, bufs[:self.n], bufs[self.n:])]
        lands = [lax.empty((3,) + h.shape[1:], h.dtype) for h in pair]
        return self._start("scatter", pair + lands, _plan_scatter_chips, 3 * self.n, ())

    def start_join(self, after):
        bufs = self._wait(after)
        total = [_add_chips(h, r, self.place_arr, name=f"grad_add_chips_{nm}", tm=min(256, h.shape[1]))
                 for nm, h, r in zip(self.names, bufs[:self.n], bufs[self.n:])]
        return self._start("join", total, _plan_join_halves, self.n, ())

    def finish(self, after):
        return dict(zip(self.names, self._wait(after)))


def _all_gather_small(v):
    m_per, n = v.shape

    def body(x_ref, out_ref, send_sems, recv_sems, local_sem):
        x, y, c, chips = _place()
        me, sibling = (x, y, c), (x, y, 1 - c)

        def rows(px, py, pc):
            return out_ref.at[4 * px + 2 * py + pc]

        def copy(k, block, to, src=None):
            return _remote(rows(*block) if src is None else src, rows(*block), send_sems.at[k], recv_sems.at[k], to)

        mine = pltpu.make_async_copy(x_ref, rows(*me), local_sem)
        mine.start()
        first = [copy(0, me, sibling, src=x_ref)]
        first += [copy(1 + j, me, (*chip, c), src=x_ref) for j, chip in enumerate(chips)]
        for cp in first:
            cp.start()
        passed = [copy(4 + j, (*chip, c), sibling) for j, chip in enumerate(chips)]
        for j, chip in enumerate(chips):
            copy(1 + j, (*chip, c), me).wait_recv()
            passed[j].start()
        copy(0, sibling, me).wait_recv()
        for j, chip in enumerate(chips):
            copy(4 + j, (*chip, 1 - c), me).wait_recv()
        for cp in first + passed:
            cp.wait_send()
        mine.wait()

    return pl.pallas_call(
        body, name="gather_small_grads",
        out_shape=jax.ShapeDtypeStruct((8, m_per, n), v.dtype),
        in_specs=[pl.BlockSpec(memory_space=pltpu.VMEM)], out_specs=pl.BlockSpec(memory_space=pltpu.VMEM),
        scratch_shapes=[pltpu.SemaphoreType.DMA((7,)), pltpu.SemaphoreType.DMA((7,)), pltpu.SemaphoreType.DMA],
        compiler_params=pltpu.CompilerParams(vmem_limit_bytes=VMEM_LIMIT),
    )(v)


def _sum8(v, *, name):
    _, m, n = v.shape

    def body(v_ref, o_ref):
        acc = v_ref[0]
        for d in range(1, 8):
            acc = acc + v_ref[d]
        o_ref[...] = acc

    return pl.pallas_call(body, name=name, out_shape=jax.ShapeDtypeStruct((m, n), F32),
                          compiler_params=pltpu.CompilerParams(vmem_limit_bytes=VMEM_LIMIT))(v)


def _local_step(x, target, norm_w, q_norm_w, k_norm_w, sinks, a_re, a_im, log_dt, b_re, b_im, c_re, c_im, d_skip,
                b_glu, io):
    seq = x.shape[0]
    qw2 = jnp.tile(q_norm_w.reshape(1, HEAD_DIM), (1, HEADS_PER_TILE))
    kw2 = jnp.tile(k_norm_w.reshape(1, HEAD_DIM), (1, HEADS_PER_TILE))
    nw, bg = norm_w.reshape(1, D_MODEL), b_glu.reshape(1, D_MODEL)
    dsk = d_skip.reshape(1, SSM_W)

    h, rstd = _rms_fwd(x, nw, deps=io.begin())
    proj, w_in4 = io.projection(h)
    attn, lse, ya_in = _attn2_fwd(proj, qw2, kw2, sinks, deps=io.after_proj(proj))
    w_ap4 = io.weight("w_attn_proj", ya_in)
    w_glu4, w_sp4, w_out = io.weight("w_glu", ya_in), io.weight("w_ssm_proj", ya_in), io.weight("w_out", ya_in)
    y_a = _mm(ya_in, w_ap4, mode="nn", name="mm_attn_proj", tm=2048, tn=512, tk=ATTN_W, b_blocked=True,
              rows_outer=True)

    flat_a = (a_re.reshape(1, N_STATES), a_im.reshape(1, N_STATES), jnp.repeat(log_dt, STATE).reshape(1, N_STATES))
    coef = _ssm_params_fwd(*flat_a)
    bre_blk, bim_blk = _block_diag_b(b_re).astype(_MXU), _block_diag_b(b_im).astype(_MXU)
    cre_blk, cim_blk = _block_diag_c(c_re).astype(_MXU), _block_diag_c(c_im).astype(_MXU)
    u_scan = _to_scan_order(proj[:, OFF_U * CW:OFF_U * CW + SSM_W])
    y_scan, yg, s_re, s_im, i_re, i_im = _ssm_fwd(u_scan, bre_blk, bim_blk, cre_blk, cim_blk, dsk, coef)
    glu = _mm(yg, w_glu4, mode="nn", name="mm_glu", tm=2048, tn=512, tk=SSM_W, b_blocked=True, rows_outer=True)

    def gate_s(ga, gb, ba, bb, z):
        return ((ga + ba) * _sigmoid(gb + bb) * (z * _sigmoid(z)),)

    (ys_in,) = _ew(gate_s, [(glu, "mat", 0), (glu, "mat", 2), (bg, "row", 0), (bg, "row", 2), (proj, "mat", OFF_Z)],
                   [(SSM_W, _MXU)], rows=seq, ncol=2, name="ew_ssm_gate")
    y_s = _mm(ys_in, w_sp4, mode="nn", name="mm_ssm_proj", tm=2048, tn=512, tk=SSM_W, b_blocked=True,
              rows_outer=True)

    def merge(ga, gs, ya, ys):
        return (_sigmoid(ga) * ya + _sigmoid(gs) * ys,)

    (merged,) = _ew(merge, [(proj, "mat", OFF_GA), (proj, "mat", OFF_GS), (y_a, "mat", 0), (y_s, "mat", 0)],
                    [(D_MODEL, _MXU)], rows=seq, ncol=4, name="ew_merge")
    dout, dout_b, sq = _mm_out_loss(merged, w_out, x, target)
    loss = 0.5 * jnp.sum(sq) / D_MODEL

    d_ya, d_ys, d_proj = _mm_merge_bwd(dout_b, w_out, proj, y_a, y_s)
    g_w_out = _mm(merged, dout_b, mode="tn", name="mm_g_w_out", tm=1024, tn=D_MODEL, tk=1024, out_dtype=_WIRE)

    d_ya_in = _mm(d_ya, w_ap4, mode="nt", name="mm_d_attn_gate", tm=2048, tn=ATTN_W, tk=512, b_blocked=True)
    g_w_ap = _mm(ya_in, d_ya, mode="tn", name="mm_g_w_attn_proj", tm=ATTN_W, tn=D_MODEL, tk=2048, out_dtype=_WIRE,
                 out_blocked=True)

    g_w_sp = _mm(ys_in, d_ys, mode="tn", name="mm_g_w_ssm_proj", tm=SSM_W, tn=D_MODEL, tk=2048, out_dtype=_WIRE,
                 out_blocked=True)
    d_glu, d_proj, g_bglu = _mm_ssm_gate_bwd(d_ys, w_sp4, glu, bg, proj, d_proj)
    d_yg = _mm(d_glu, w_glu4, mode="nt", name="mm_d_gelu", tm=2048, tn=SSM_W, tk=512, b_blocked=True)
    g_w_glu = _mm(yg, d_glu, mode="tn", name="mm_g_w_glu", tm=SSM_W, tn=D_MODEL, tk=2048, out_dtype=_WIRE, out_blocked=True)
    dep = io.later_grads(dict(w_attn_proj=g_w_ap, w_glu=g_w_glu, w_ssm_proj=g_w_sp,
                              w_out=g_w_out.reshape(N_CHIPS, D_MODEL // N_CHIPS, D_MODEL)))

    d_proj, g_qw2, g_kw2, g_sk = _attn2_bwd(proj, qw2, kw2, sinks, lse, attn, d_ya_in, d_proj, deps=dep)
    dep = io.before_scan_backward([d_proj])
    (d_proj, g_bre, g_bim, g_cre, g_cim, g_dsk, g_abr, g_abi, g_cfr, g_cfi) = _ssm_bwd(
        _to_scan_order(d_yg), y_scan, u_scan, s_re, s_im, i_re, i_im, bre_blk, bim_blk, cre_blk, cim_blk, dsk, coef,
        d_proj, deps=dep)
    g_are, g_aim, g_ldt = _ssm_params_bwd(*flat_a, g_abr, g_abi, g_cfr, g_cfi)
    g_are, g_aim = g_are.reshape(N_GROUPS, STATE), g_aim.reshape(N_GROUPS, STATE)
    g_ldt = g_ldt.reshape(N_GROUPS, STATE).sum(axis=1)
    dep = io.before_input_projection_grad([d_proj]) + io.small_grads(dict(
        q_norm_w=g_qw2[0, :HEAD_DIM] + g_qw2[0, HEAD_DIM:], k_norm_w=g_kw2[0, :HEAD_DIM] + g_kw2[0, HEAD_DIM:],
        sinks=g_sk.reshape(N_Q_HEADS), A_re=g_are, A_im=g_aim, log_dt=g_ldt,
        B_re=_diag_of_b(g_bre), B_im=_diag_of_b(g_bim), C_re=_diag_of_c(g_cre), C_im=_diag_of_c(g_cim),
        D_skip=g_dsk.reshape(N_GROUPS, GROUP), b_glu=g_bglu.reshape(D_MODEL)))
    g_w_in = _mm(h, d_proj, mode="tn", name="mm_g_w_in", tm=1024, tn=IN_W // 4, tk=1024, out_dtype=_WIRE,
                 out_blocked=True, deps=dep)
    dep = io.input_projection_grad(g_w_in)
    d_h = _mm(d_proj, w_in4, mode="nt", name="mm_d_h", tm=1024, tn=D_MODEL, tk=IN_W // 4, b_blocked=True, deps=dep)
    grad_x, g_nw = _rms_bwd(d_h, x, rstd, nw, dout)
    return loss, grad_x, g_nw.reshape(D_MODEL)


_SMALL = ["norm_w", "q_norm_w", "k_norm_w", "sinks", "A_re", "A_im", "log_dt", "B_re", "B_im", "C_re", "C_im",
          "D_skip", "b_glu"]
_BIG = ["w_in", "w_attn_proj", "w_glu", "w_ssm_proj", "w_out"]
_LATER = _BIG[1:]
_RELATIONS = ("flip_x", "flip_y", "flip_xy")
_ORDER = ["norm_w", "w_in", "q_norm_w", "k_norm_w", "sinks", "w_attn_proj", "A_re", "A_im", "log_dt", "B_re", "B_im",
          "C_re", "C_im", "D_skip", "w_glu", "b_glu", "w_ssm_proj", "w_out"]
_PACK_W = 1024


def _packed_rows(size):
    unit = SUBLANES * _PACK_W
    return -(-size // unit) * SUBLANES


def _pack_small(d, names):
    parts = []
    for n in names:
        flat = d[n].reshape(-1).astype(F32)
        rows = _packed_rows(flat.shape[0])
        parts.append(jnp.pad(flat, (0, rows * _PACK_W - flat.shape[0])).reshape(rows, _PACK_W))
    return jnp.concatenate(parts, axis=0)


def _unpack_small(packed, like, names):
    out, pos = {}, 0
    for n in names:
        rows = _packed_rows(like[n].size)
        out[n] = packed[pos:pos + rows].reshape(-1)[:like[n].size].reshape(like[n].shape)
        pos += rows
    return out


def _place_block(v, index_arr, *, name):
    rows, cols = v.shape

    def body(i_ref, v_ref, o_ref):
        o_ref[...] = v_ref[...]

    return pl.pallas_call(
        body, name=name,
        grid_spec=pltpu.PrefetchScalarGridSpec(
            num_scalar_prefetch=1, grid=(1,),
            in_specs=[pl.BlockSpec((rows, cols), lambda i, d: (0, 0))],
            out_specs=pl.BlockSpec((None, rows, cols), lambda i, d: (d[0], 0, 0))),
        out_shape=jax.ShapeDtypeStruct((8, rows, cols), v.dtype),
        compiler_params=_params(("arbitrary",)),
    )(index_arr, v)


def _plan_all_to_all(refs):
    (land,) = refs
    x, y, c, _ = _place()
    own = land.at[4 * x + 2 * y + c]
    copies = []
    for fx, fy, fc in [(0, 0, 1), (0, 1, 0), (0, 1, 1), (1, 0, 0), (1, 0, 1), (1, 1, 0), (1, 1, 1)]:
        px, py, pc = (1 - x) if fx else x, (1 - y) if fy else y, (1 - c) if fc else c
        copies.append((own, own, (px, py, pc), land.at[4 * px + 2 * py + pc]))
    return copies


def _adamw_whole(w, g, m, v, *, name):
    def body(w_ref, g_ref, m_ref, v_ref, d_ref, nm_ref, nv_ref):
        d_ref[...], nm_ref[...], nv_ref[...] = _adamw_math(w_ref[...], g_ref[...], m_ref[...], v_ref[...])

    return pl.pallas_call(body, name=name, out_shape=[jax.ShapeDtypeStruct(w.shape, F32)] * 3)(w, g, m, v)


class _Exchanges:
    def __init__(self, w, m, v):
        self.w, self.m, self.v = w, m, v
        self.grads, self.delta, self.new_m, self.new_v = {}, {}, {}, {}

    def _adamw(self, names, deps):
        for n in names:
            self.delta[n], self.new_m[n], self.new_v[n] = _adamw(
                self.w[n], self.grads[n], self.m[n], self.v[n], name=f"adamw_{n}", tm=128, deps=deps)

    def begin(self):
        chip = (2 * lax.axis_index("x") + lax.axis_index("y")).astype(jnp.int32).reshape(1)
        w_in = _place_shard(self.w["w_in"], chip, name="place_w_in")
        self.w_in_sems, self.w_in_buf, token = _copies_start("gather_w_in_direct_start", [w_in], _plan_relay_direct, 2)
        self.later_full = [_place_shard(self.w[n], chip, name=f"place_{n}", deps=[token]) for n in _LATER]
        return self.later_full

    def projection(self, h):
        x, y = lax.axis_index("x"), lax.axis_index("y")
        blks = [jnp.asarray(b, jnp.int32).reshape(1)
                for b in (2 * x + y, 2 * (1 - x) + y, 2 * x + (1 - y), 2 * (1 - x) + (1 - y))]
        bufs = self.w_in_buf
        proj = _mm_chip_block(h, bufs[0], blks[0], None, name="mm_proj_own")
        relay, token = [], proj
        for k, tag in enumerate(_RELATIONS[:2]):
            bufs = _copies_wait(f"gather_w_in_direct_{tag}_wait", bufs, self.w_in_sems, _plan_relay_direct, [token],
                                which=(k,))
            plan = functools.partial(_plan_relay_forward, k=k)
            sems, bufs, token = _copies_start(f"gather_w_in_relay_{tag}_start", bufs, plan, 2)
            relay.append((sems, plan))
        self.rest = _copies_start("gather_ici_rest_start", self.later_full, _plan_gather_ici, 3 * len(_LATER),
                                  after=[token])
        token = self.rest[2]
        for k, tag in enumerate(_RELATIONS[:2]):
            bufs = _copies_wait(f"gather_w_in_hand_{tag}_wait", bufs, relay[k][0], relay[k][1], [token], which=(1,))
            token = proj = _mm_chip_block(h, bufs[0], blks[1 + k], proj, name=f"mm_proj_{tag}")
        for k, tag in enumerate(_RELATIONS[:2]):
            bufs = _copies_wait(f"gather_w_in_relay_{tag}_wait", bufs, relay[k][0], relay[k][1], [token], which=(0,))
        sems, bufs, token = _copies_start("gather_w_in_last_start", bufs, _plan_relay_last, 1)
        bufs = _copies_wait("gather_w_in_last_wait", bufs, sems, _plan_relay_last, [token])
        proj = _mm_chip_block(h, bufs[0], blks[3], proj, name="mm_proj_flip_xy")
        return proj, bufs[0]

    def weight(self, name, after):
        if self.rest is not None:
            sems, bufs = self.rest
            later = dict(zip(_LATER, _copies_wait("gather_d2d_rest_wait", bufs, sems, _plan_gather_d2d, [after])))
            later["w_out"] = later["w_out"].reshape(D_MODEL, D_MODEL)
            self.later, self.rest = later, None
        return self.later[name]

    def after_proj(self, proj):
        sems, bufs, _ = self.rest
        bufs = _copies_wait("gather_ici_rest_wait", bufs, sems, _plan_gather_ici, [proj])
        sems, bufs, token = _copies_start("gather_d2d_rest_start", bufs, _plan_gather_d2d, 3 * len(_LATER))
        self.rest = (sems, bufs)
        return [token]

    def later_grads(self, grads):
        self.rs_later = _ReduceScatter("later", _LATER, [grads[n] for n in _LATER])
        return self.rs_later.start_swap()

    def before_scan_backward(self, after):
        return self.rs_later.start_scatter(after)

    def before_input_projection_grad(self, after):
        return self.rs_later.start_join(after)

    def input_projection_grad(self, g_w_in):
        self.grads.update(self.rs_later.finish([g_w_in]))
        self.rs_in = _ReduceScatter("w_in", ["w_in"], [g_w_in])
        self._adamw(_LATER, self.rs_in.start_swap())
        return self.rs_in.start_scatter([self.delta[n] for n in _LATER])

    def _adamw_small(self, names):
        for n in names:
            self.delta[n], self.new_m[n], self.new_v[n] = _adamw_whole(
                self.w[n], self.grads[n], self.m[n], self.v[n], name=f"adamw_{n}")

    def small_grads(self, grads):
        me = (4 * lax.axis_index("x") + 2 * lax.axis_index("y") + lax.axis_index("c")).astype(jnp.int32).reshape(1)
        land = _place_block(_pack_small(grads, _SMALL[1:]), me, name="place_small_grads")
        self.small = _copies_start("gather_small_start", [land], _plan_all_to_all, 7)
        return [self.small[2]]

    def finish(self, g_norm_w, loss, after):
        join = self.rs_in.start_join(after)
        sems, bufs, _ = self.small
        (land,) = _copies_wait("gather_small_wait", bufs, sems, _plan_all_to_all, join)
        self.grads.update(_unpack_small(_sum8(land, name="sum_small_grads"), self.w, _SMALL[1:]))
        self._adamw_small(_SMALL[1:])
        rows = _packed_rows(g_norm_w.size)
        late = jnp.concatenate([_pack_small(dict(norm_w=g_norm_w), _SMALL[:1]),
                                jnp.pad(loss.reshape(1, 1), ((0, SUBLANES - 1), (0, _PACK_W - 1)))], axis=0)
        late = _sum8(_all_gather_small(late), name="sum_norm_w_grad_and_loss")
        self.grads.update(_unpack_small(late[:rows], self.w, _SMALL[:1]))
        self._adamw_small(_SMALL[:1])
        self.grads.update(self.rs_in.finish([self.delta[_SMALL[0]]]))
        self._adamw(["w_in"], ())
        return late[rows, 0]


def kernel(x, norm_w, w_in, q_norm_w, k_norm_w, sinks, w_attn_proj, A_re, A_im, log_dt, B_re, B_im, C_re, C_im, D_skip, w_glu, b_glu, w_ssm_proj, w_out, loss_target, m_norm_w, m_w_in, m_q_norm_w, m_k_norm_w, m_sinks, m_w_attn_proj, m_A_re, m_A_im, m_log_dt, m_B_re, m_B_im, m_C_re, m_C_im, m_D_skip, m_w_glu, m_b_glu, m_w_ssm_proj, m_w_out, v_norm_w, v_w_in, v_q_norm_w, v_k_norm_w, v_sinks, v_w_attn_proj, v_A_re, v_A_im, v_log_dt, v_B_re, v_B_im, v_C_re, v_C_im, v_D_skip, v_w_glu, v_b_glu, v_w_ssm_proj, v_w_out):
    w = dict(norm_w=norm_w, w_in=w_in, q_norm_w=q_norm_w, k_norm_w=k_norm_w, sinks=sinks, w_attn_proj=w_attn_proj,
             A_re=A_re, A_im=A_im, log_dt=log_dt, B_re=B_re, B_im=B_im, C_re=C_re, C_im=C_im, D_skip=D_skip,
             w_glu=w_glu, b_glu=b_glu, w_ssm_proj=w_ssm_proj, w_out=w_out)
    m = dict(norm_w=m_norm_w, w_in=m_w_in, q_norm_w=m_q_norm_w, k_norm_w=m_k_norm_w, sinks=m_sinks,
             w_attn_proj=m_w_attn_proj, A_re=m_A_re, A_im=m_A_im, log_dt=m_log_dt, B_re=m_B_re, B_im=m_B_im,
             C_re=m_C_re, C_im=m_C_im, D_skip=m_D_skip, w_glu=m_w_glu, b_glu=m_b_glu, w_ssm_proj=m_w_ssm_proj,
             w_out=m_w_out)
    v = dict(norm_w=v_norm_w, w_in=v_w_in, q_norm_w=v_q_norm_w, k_norm_w=v_k_norm_w, sinks=v_sinks,
             w_attn_proj=v_w_attn_proj, A_re=v_A_re, A_im=v_A_im, log_dt=v_log_dt, B_re=v_B_re, B_im=v_B_im,
             C_re=v_C_re, C_im=v_C_im, D_skip=v_D_skip, w_glu=v_w_glu, b_glu=v_b_glu, w_ssm_proj=v_w_ssm_proj,
             w_out=v_w_out)

    io = _Exchanges(w, m, v)
    loss, grad_x, g_norm_w = _local_step(x[0], loss_target[0], norm_w, q_norm_w, k_norm_w, sinks, A_re, A_im, log_dt,
                                         B_re, B_im, C_re, C_im, D_skip, b_glu, io)
    loss = io.finish(g_norm_w, loss, [grad_x])
    grads, delta, new_m, new_v = io.grads, io.delta, io.new_m, io.new_v

    return (loss, grad_x[None], *[grads[n] for n in _ORDER], *[delta[n] for n in _ORDER],
            *[new_m[n] for n in _ORDER], *[new_v[n] for n in _ORDER])
```

```python
import functools
import math

import jax
import jax.numpy as jnp
from jax import lax
from jax.experimental import pallas as pl
from jax.experimental.pallas import tpu as pltpu

F32 = jnp.float32
_MXU = jnp.bfloat16
_WIRE = jnp.bfloat16

LANES = 128
SUBLANES = 8
VMEM_LIMIT = 56 * 1024 * 1024

D_MODEL = 2048
HEAD_DIM = 64
N_Q_HEADS = 16
N_KV_HEADS = 4
Q_PER_KV = 4
ATTN_W = 1024
KV_W = 256
WINDOW = 128
SSM_W = 1024
GROUP = 16
N_GROUPS = 64
STATE = 64
N_STATES = N_GROUPS * STATE
IN_W = 8704
NORM_EPS = 1e-6
N_CHIPS = 4
CW = 512
OFF_AGATE, OFF_U, OFF_Z, OFF_GA, OFF_GS = 3, 5, 7, 9, 13

SSM_T = 256
SSM_L = SSM_T // SUBLANES
SSM_JB = 8
SSM_SB = N_STATES // SSM_JB

ADAM_LR, ADAM_B1, ADAM_B2, ADAM_EPS, ADAM_WD, ADAM_STEP = 0.001, 0.9, 0.999, 1e-08, 0.01, 10

MESH = pl.DeviceIdType.MESH
_ANY = pl.BlockSpec(memory_space=pl.ANY)


def _params(sem=None):
    return pltpu.CompilerParams(dimension_semantics=sem, vmem_limit_bytes=VMEM_LIMIT)


def _mm(a, b, *, mode, name, tm, tn, tk, out_dtype=F32, b_blocked=False, out_blocked=False, rows_outer=False,
        deps=()):
    nd = len(deps)
    if mode == "tn":
        K, M = a.shape
    else:
        M, K = a.shape
    if mode == "nn":
        N = b.shape[0] * b.shape[2] if b_blocked else b.shape[1]
    elif mode == "nt":
        N = b.shape[1] if b_blocked else b.shape[0]
    else:
        N = b.shape[1]
    tm, tn, tk = min(tm, M), min(tn, N), min(tk, K)
    nj, ni, nk = N // tn, M // tm, K // tk
    assert nj * tn == N and ni * tm == M and nk * tk == K, (name, M, N, K)
    dims = {"nn": (((1,), (0,)), ((), ())), "nt": (((1,), (1,)), ((), ())), "tn": (((0,), (0,)), ((), ()))}[mode]

    if mode == "tn":
        a_spec = pl.BlockSpec((tk, tm), lambda j, i, k: (k, i))
    else:
        a_spec = pl.BlockSpec((tm, tk), lambda j, i, k: (i, k))
    if mode == "nn":
        if b_blocked:
            assert b.shape[0] == nj and b.shape[2] == tn
            b_spec = pl.BlockSpec((None, tk, tn), lambda j, i, k: (j, k, 0))
        else:
            b_spec = pl.BlockSpec((tk, tn), lambda j, i, k: (k, j))
    elif mode == "nt":
        if b_blocked:
            assert b.shape[0] == nk and b.shape[2] == tk
            b_spec = pl.BlockSpec((None, tn, tk), lambda j, i, k: (k, j, 0))
        else:
            b_spec = pl.BlockSpec((tn, tk), lambda j, i, k: (j, k))
    else:
        b_spec = pl.BlockSpec((tk, tn), lambda j, i, k: (k, j))
    whole_out = out_blocked and nj == 1
    if whole_out:
        assert ni == 1
        o_spec = pl.BlockSpec((N_CHIPS, tm, tn // N_CHIPS), lambda j, i, k: (0, 0, 0))
        o_shape = jax.ShapeDtypeStruct((N_CHIPS, M, tn // N_CHIPS), out_dtype)
    elif out_blocked:
        assert nj == N_CHIPS
        o_spec = pl.BlockSpec((None, tm, tn), lambda j, i, k: (j, i, 0))
        o_shape = jax.ShapeDtypeStruct((nj, M, tn), out_dtype)
    else:
        o_spec = pl.BlockSpec((tm, tn), lambda j, i, k: (i, j))
        o_shape = jax.ShapeDtypeStruct((M, N), out_dtype)
    use_acc = nk > 1 and (out_dtype != F32 or whole_out)

    def body(a_ref, b_ref, *rest):
        o_ref, scratch = rest[nd], rest[nd + 1:]

        def product():
            return lax.dot_general(a_ref[...].astype(_MXU), b_ref[...].astype(_MXU), dims, preferred_element_type=F32)

        def write(result):
            if whole_out:
                w = tn // N_CHIPS
                for c in range(N_CHIPS):
                    o_ref[c] = result[:, c * w:(c + 1) * w].astype(o_ref.dtype)
            else:
                o_ref[...] = result.astype(o_ref.dtype)

        if nk == 1:
            write(product())
            return
        k = pl.program_id(2)
        acc = scratch[0] if use_acc else o_ref

        @pl.when(k == 0)
        def _():
            acc[...] = jnp.zeros_like(acc)

        acc[...] += product()

        if use_acc:
            @pl.when(k == nk - 1)
            def _():
                write(acc[...])

    specs = [a_spec, b_spec, o_spec]
    grid = (nj, ni, nk)
    if rows_outer:
        specs = [pl.BlockSpec(s.block_shape, lambda i, j, k, f=s.index_map: f(j, i, k)) for s in specs]
        grid = (ni, nj, nk)
    return pl.pallas_call(
        body, name=name, grid=grid, in_specs=specs[:2] + [_ANY] * nd, out_specs=specs[2],
        out_shape=o_shape, scratch_shapes=[pltpu.VMEM((tm, tn), F32)] if use_acc else [],
        compiler_params=_params(("parallel", "parallel", "arbitrary")),
    )(a, b, *deps)


def _mm_chip_block(a, b4, blk, prev, *, name, tm=512, deps=()):
    M, K = a.shape
    nchip, _, C = b4.shape
    tm = min(tm, M)
    extra = ([] if prev is None else [prev]) + list(deps)

    def body(blk_ref, a_ref, b_ref, *rest):
        rest[-1][...] = jnp.dot(a_ref[...].astype(_MXU), b_ref[...].astype(_MXU), preferred_element_type=F32)

    return pl.pallas_call(
        body, name=name,
        grid_spec=pltpu.PrefetchScalarGridSpec(
            num_scalar_prefetch=1, grid=(M // tm,),
            in_specs=[pl.BlockSpec((tm, K), lambda i, c: (i, 0)), pl.BlockSpec((None, K, C), lambda i, c: (c[0], 0, 0))]
            + [_ANY] * len(extra),
            out_specs=pl.BlockSpec((tm, C), lambda i, c: (i, c[0]))),
        out_shape=jax.ShapeDtypeStruct((M, nchip * C), F32),
        input_output_aliases={} if prev is None else {3: 0},
        compiler_params=_params(("arbitrary",)),
    )(blk, a, b4, *extra)


def _mm_merge_out_loss(proj, y_a, y_s, w_out, x, target, *, tm=256):
    rows, d = x.shape
    ncol = d // CW

    def body(*refs):
        ga_refs, gs_refs = refs[:ncol], refs[ncol:2 * ncol]
        ya_ref, ys_ref, w_ref, x_ref, t_ref, m_ref, d_ref, db_ref, sq_ref = refs[2 * ncol:]
        for j in range(ncol):
            cols = slice(j * CW, (j + 1) * CW)
            m_ref[:, cols] = (_sigmoid(ga_refs[j][...]) * ya_ref[:, cols]
                              + _sigmoid(gs_refs[j][...]) * ys_ref[:, cols]).astype(m_ref.dtype)
        mo = jnp.dot(m_ref[...], w_ref[...].astype(_MXU), preferred_element_type=F32)
        err = (x_ref[...] + mo) - t_ref[...]
        dout = err * (1.0 / d)
        d_ref[...] = dout
        db_ref[...] = dout.astype(db_ref.dtype)
        part = _colsum(err * err)
        i = pl.program_id(0)

        @pl.when(i == 0)
        def _():
            sq_ref[...] = part

        @pl.when(i > 0)
        def _():
            sq_ref[...] += part

    tile = pl.BlockSpec((tm, d), lambda i: (i, 0))
    gate = [pl.BlockSpec((tm, CW), lambda i, c=off + j: (i, c)) for off in (OFF_GA, OFF_GS) for j in range(ncol)]
    return pl.pallas_call(
        body, name="mm_merge_out_loss", grid=(rows // tm,),
        in_specs=gate + [tile, tile, pl.BlockSpec((d, d), lambda i: (0, 0), pipeline_mode=pl.Buffered(1)), tile, tile],
        out_specs=[tile, tile, tile, pl.BlockSpec((1, d), lambda i: (0, 0))],
        out_shape=[jax.ShapeDtypeStruct((rows, d), _MXU), jax.ShapeDtypeStruct((rows, d), F32),
                   jax.ShapeDtypeStruct((rows, d), _MXU), jax.ShapeDtypeStruct((1, d), F32)],
        compiler_params=_params(("arbitrary",)),
    )(*([proj] * (2 * ncol)), y_a, y_s, w_out, x, target)


def _mm_merge_bwd(dout_b, w_out, proj, y_a, y_s, *, tm=256):
    rows, d = y_a.shape
    ncol = d // CW

    def body(do_ref, w_ref, *refs):
        ga_refs, gs_refs = refs[:ncol], refs[ncol:2 * ncol]
        ya_ref, ys_ref, dya_ref, dys_ref, dg_ref = refs[2 * ncol:]
        dm = lax.dot_general(do_ref[...].astype(_MXU), w_ref[...].astype(_MXU), _NT, preferred_element_type=F32)
        for j in range(ncol):
            cols = slice(j * CW, (j + 1) * CW)
            dmj = dm[:, cols]
            sa, ss = _sigmoid(ga_refs[j][...]), _sigmoid(gs_refs[j][...])
            dya_ref[:, cols] = (sa * dmj).astype(dya_ref.dtype)
            dys_ref[:, cols] = (ss * dmj).astype(dys_ref.dtype)
            dg_ref[:, cols] = (dmj * ya_ref[:, cols] * sa * (1.0 - sa)).astype(dg_ref.dtype)
            dg_ref[:, d + j * CW:d + (j + 1) * CW] = (dmj * ys_ref[:, cols] * ss * (1.0 - ss)).astype(dg_ref.dtype)

    tile = pl.BlockSpec((tm, d), lambda i: (i, 0))
    gate = [pl.BlockSpec((tm, CW), lambda i, c=off + j: (i, c)) for off in (OFF_GA, OFF_GS) for j in range(ncol)]
    both = pl.BlockSpec((pl.Element(tm), pl.Element(2 * d)), lambda i: (i * tm, OFF_GA * CW))
    return pl.pallas_call(
        body, name="mm_merge_bwd", grid=(rows // tm,),
        in_specs=[tile, pl.BlockSpec((d, d), lambda i: (0, 0))] + gate + [tile, tile],
        out_specs=[tile, tile, both],
        out_shape=[jax.ShapeDtypeStruct((rows, d), _MXU)] * 2 + [jax.ShapeDtypeStruct((rows, IN_W), _MXU)],
        compiler_params=_params(("arbitrary",)),
    )(dout_b, w_out, *([proj] * (2 * ncol)), y_a, y_s)


def _mm_glu_gate(yg, w_glu4, b_glu, proj, *, tm=512):
    rows, k = yg.shape
    nj, _, tn = w_glu4.shape
    w = nj * tn // 2
    tm = min(tm, rows)

    def body(a_ref, w_ref, ba_ref, bb_ref, z0_ref, z1_ref, glu_ref, ys_ref, row):
        j = pl.program_id(1)
        blk = jnp.dot(a_ref[...].astype(_MXU), w_ref[...].astype(_MXU), preferred_element_type=F32)
        glu_ref[...] = blk
        for c in range(nj):
            @pl.when(j == c)
            def _(c=c):
                row[:, c * tn:(c + 1) * tn] = blk

        @pl.when(j == nj - 1)
        def _():
            z = jnp.concatenate([z0_ref[...], z1_ref[...]], axis=1)
            ys_ref[...] = ((row[:, :w] + ba_ref[...]) * _sigmoid(row[:, w:] + bb_ref[...])
                           * (z * _sigmoid(z))).astype(ys_ref.dtype)

    bias = lambda c: pl.BlockSpec((1, w), lambda i, j: (0, c))
    zcol = lambda c: pl.BlockSpec((tm, CW), lambda i, j: (i, OFF_Z + c))
    return pl.pallas_call(
        body, name="mm_glu_gate", grid=(rows // tm, nj),
        in_specs=[pl.BlockSpec((tm, k), lambda i, j: (i, 0)), pl.BlockSpec((None, k, tn), lambda i, j: (j, 0, 0)),
                  bias(0), bias(1), zcol(0), zcol(1)],
        out_specs=[pl.BlockSpec((tm, tn), lambda i, j: (i, j)), pl.BlockSpec((tm, w), lambda i, j: (i, 0))],
        out_shape=[jax.ShapeDtypeStruct((rows, nj * tn), F32), jax.ShapeDtypeStruct((rows, w), _MXU)],
        scratch_shapes=[pltpu.VMEM((tm, nj * tn), F32)],
        compiler_params=_params(("arbitrary", "arbitrary")),
    )(yg, w_glu4, b_glu, b_glu, proj, proj)


def _mm_ssm_gate_bwd(d_ys, w_sp4, glu, b_glu, proj, d_proj, *, tm=512):
    rows, w = glu.shape[0], glu.shape[1] // 2
    nk, tk = w_sp4.shape[0], w_sp4.shape[2]
    tm = min(tm, rows)

    def body(dy_ref, w_ref, ga_ref, gb_ref, ba_ref, bb_ref, z0_ref, z1_ref, buf_ref, dg_ref, dz_ref, db_ref, acc):
        i, k = pl.program_id(0), pl.program_id(1)

        @pl.when(k == 0)
        def _():
            acc[...] = jnp.zeros_like(acc)

        acc[...] += lax.dot_general(dy_ref[...].astype(_MXU), w_ref[...].astype(_MXU), _NT, preferred_element_type=F32)

        @pl.when(k == nk - 1)
        def _():
            dv = acc[...]
            a, sb = ga_ref[...] + ba_ref[...], _sigmoid(gb_ref[...] + bb_ref[...])
            f, df = _silu_and_grad(jnp.concatenate([z0_ref[...], z1_ref[...]], axis=1))
            dga = dv * sb * f
            dgb = dv * a * f * sb * (1.0 - sb)
            dg_ref[:, :w] = dga.astype(dg_ref.dtype)
            dg_ref[:, w:] = dgb.astype(dg_ref.dtype)
            dz_ref[...] = (dv * a * sb * df).astype(dz_ref.dtype)
            part = jnp.concatenate([_colsum(dga), _colsum(dgb)], axis=1)

            @pl.when(i == 0)
            def _():
                db_ref[...] = part

            @pl.when(i > 0)
            def _():
                db_ref[...] += part

    half = lambda c: pl.BlockSpec((tm, w), lambda i, k: (i, c))
    bias = lambda c: pl.BlockSpec((1, w), lambda i, k: (0, c))
    zcol = lambda c: pl.BlockSpec((tm, CW), lambda i, k: (i, OFF_Z + c))
    return pl.pallas_call(
        body, name="mm_ssm_gate_bwd", grid=(rows // tm, nk),
        in_specs=[pl.BlockSpec((tm, tk), lambda i, k: (i, k)), pl.BlockSpec((None, w, tk), lambda i, k: (k, 0, 0)),
                  half(0), half(1), bias(0), bias(1), zcol(0), zcol(1), _ANY],
        out_specs=[pl.BlockSpec((tm, 2 * w), lambda i, k: (i, 0)),
                   pl.BlockSpec((pl.Element(tm), pl.Element(w)), lambda i, k: (i * tm, OFF_Z * CW)),
                   pl.BlockSpec((1, 2 * w), lambda i, k: (0, 0))],
        out_shape=[jax.ShapeDtypeStruct((rows, 2 * w), _MXU), jax.ShapeDtypeStruct(d_proj.shape, d_proj.dtype),
                   jax.ShapeDtypeStruct((1, 2 * w), F32)],
        input_output_aliases={8: 1},
        scratch_shapes=[pltpu.VMEM((tm, w), F32)],
        compiler_params=_params(("arbitrary", "arbitrary")),
    )(d_ys, w_sp4, glu, glu, b_glu, b_glu, proj, proj, d_proj)


def _ew(fn, ins, outs, *, rows, ncol, name, n_acc=0, tm=512, deps=(), into=None):
    deps = list(deps) + ([into[1]] if into else [])
    n_in, n_out, nd = len(ins), len(outs), len(deps)
    tm = min(tm, rows)
    in_specs = []
    for _, kind, col0 in ins:
        if kind == "mat":
            in_specs.append(pl.BlockSpec((tm, CW), lambda j, i, c0=col0: (i, c0 + j)))
        else:
            in_specs.append(pl.BlockSpec((1, CW), lambda j, i, c0=col0: (0, c0 + j)))
    out_specs = [pl.BlockSpec((tm, CW), lambda j, i: (i, j)) for _ in outs]
    out_shape = [jax.ShapeDtypeStruct((rows, w), dt) for w, dt in outs]
    if into:
        out_specs[into[0]] = pl.BlockSpec((tm, CW), lambda j, i, c0=into[2]: (i, c0 + j))
        out_shape[into[0]] = jax.ShapeDtypeStruct(into[1].shape, into[1].dtype)
    for _ in range(n_acc):
        out_specs.append(pl.BlockSpec((1, CW), lambda j, i: (0, j)))
        out_shape.append(jax.ShapeDtypeStruct((1, ncol * CW), F32))

    def body(*refs):
        vals = fn(*[r[...] for r in refs[:n_in]])
        refs = refs[n_in + nd:]
        for r, v in zip(refs[:n_out], vals[:n_out]):
            r[...] = v.astype(r.dtype)
        i = pl.program_id(1)
        for r, v in zip(refs[n_out:], vals[n_out:]):
            @pl.when(i == 0)
            def _(r=r, v=v):
                r[...] = v

            @pl.when(i > 0)
            def _(r=r, v=v):
                r[...] += v

    res = pl.pallas_call(
        body, name=name, grid=(ncol, rows // tm), in_specs=in_specs + [_ANY] * nd, out_specs=out_specs,
        out_shape=out_shape, input_output_aliases={n_in + nd - 1: into[0]} if into else {},
        compiler_params=_params(("parallel", "arbitrary")),
    )(*[a for a, _, _ in ins], *deps)
    return res


def _colsum(v):
    return jnp.sum(v, axis=0, keepdims=True)


def _sigmoid(v):
    return jax.nn.sigmoid(v)


def _silu_and_grad(v):
    s = _sigmoid(v)
    return v * s, s * (1.0 + v * (1.0 - s))


def _rms_fwd(x, w, *, tm=512, deps=()):
    rows, d = x.shape
    nd = len(deps)

    def body(x_ref, w_ref, *rest):
        h_ref, r_ref = rest[nd:]
        xv = x_ref[...]
        r = lax.rsqrt(jnp.mean(xv * xv, axis=-1, keepdims=True) + NORM_EPS)
        h_ref[...] = (xv * r * w_ref[...]).astype(h_ref.dtype)
        r_ref[...] = r

    return pl.pallas_call(
        body, name="rms_fwd", grid=(rows // tm,),
        in_specs=[pl.BlockSpec((tm, d), lambda i: (i, 0)), pl.BlockSpec((1, d), lambda i: (0, 0))] + [_ANY] * nd,
        out_specs=[pl.BlockSpec((tm, d), lambda i: (i, 0)), pl.BlockSpec((tm, 1), lambda i: (i, 0))],
        out_shape=[jax.ShapeDtypeStruct((rows, d), _MXU), jax.ShapeDtypeStruct((rows, 1), F32)],
        compiler_params=_params(("arbitrary",)),
    )(x, w, *deps)


def _rms_bwd(dh, x, rstd, w, dout, *, tm=256):
    rows, d = x.shape

    def body(dh_ref, x_ref, r_ref, w_ref, do_ref, gx_ref, gw_ref):
        dhv, xv, r, wv = dh_ref[...], x_ref[...], r_ref[...], w_ref[...]
        xr = xv * r
        t = jnp.mean(dhv * wv * xr, axis=-1, keepdims=True)
        gx_ref[...] = do_ref[...] + r * (wv * dhv - xr * t)
        part = _colsum(dhv * xr)
        i = pl.program_id(0)

        @pl.when(i == 0)
        def _():
            gw_ref[...] = part

        @pl.when(i > 0)
        def _():
            gw_ref[...] += part

    return pl.pallas_call(
        body, name="rms_bwd", grid=(rows // tm,),
        in_specs=[pl.BlockSpec((tm, d), lambda i: (i, 0)), pl.BlockSpec((tm, d), lambda i: (i, 0)),
                  pl.BlockSpec((tm, 1), lambda i: (i, 0)), pl.BlockSpec((1, d), lambda i: (0, 0)),
                  pl.BlockSpec((tm, d), lambda i: (i, 0))],
        out_specs=[pl.BlockSpec((tm, d), lambda i: (i, 0)), pl.BlockSpec((1, d), lambda i: (0, 0))],
        out_shape=[jax.ShapeDtypeStruct((rows, d), F32), jax.ShapeDtypeStruct((1, d), F32)],
        compiler_params=_params(("arbitrary",)),
    )(dh, x, rstd, w, dout)


_NT = (((1,), (1,)), ((), ()))
_TN = (((0,), (0,)), ((), ()))


QKV_W = ATTN_W + 2 * KV_W
HEADS_PER_TILE = LANES // HEAD_DIM


def _low_half(rows):
    return lax.broadcasted_iota(jnp.int32, (rows, LANES), 1) < HEAD_DIM


def _pair_mean(t, low):
    m_lo = jnp.sum(jnp.where(low, t, 0.0), axis=-1, keepdims=True)
    m_hi = jnp.sum(jnp.where(low, 0.0, t), axis=-1, keepdims=True)
    return jnp.where(low, m_lo, m_hi) * (1.0 / HEAD_DIM)


def _pair_rstd(t, low):
    return lax.rsqrt(_pair_mean(t * t, low) + NORM_EPS)


def _dup_half(t, hi, low):
    swapped = pltpu.roll(t, HEAD_DIM, 1)
    return jnp.where(low, swapped, t) if hi else jnp.where(low, t, swapped)


def _fold_halves(t):
    return t + pltpu.roll(t, HEAD_DIM, 1)


def _split_heads(t, low):
    return [jnp.where(low, t, 0.0), jnp.where(low, 0.0, t)]


def _stacked_band_mask(n):
    rows = Q_PER_KV * WINDOW
    qi = lax.broadcasted_iota(jnp.int32, (rows, 2 * WINDOW), 0) % WINDOW + WINDOW
    kj = lax.broadcasted_iota(jnp.int32, (rows, 2 * WINDOW), 1)
    diff = qi - kj
    first_key = jnp.where(n > 0, 0, WINDOW)
    return (diff >= 0) & (diff < WINDOW) & (kj >= first_key)


def _stacked_sinks(sink_ref, g):
    blk = lax.broadcasted_iota(jnp.int32, (Q_PER_KV * WINDOW, 1), 0) // WINDOW
    col = jnp.full((Q_PER_KV * WINDOW, 1), sink_ref[Q_PER_KV * g], F32)
    for r in range(1, Q_PER_KV):
        col = jnp.where(blk == r, sink_ref[Q_PER_KV * g + r], col)
    return col


def _attn_in_specs(nblk, rev):
    def cur(n):
        return (nblk - 1 - n) if rev else n

    q_spec = pl.BlockSpec((WINDOW, ATTN_W), lambda n: (cur(n), 0))
    kvc_spec = pl.BlockSpec((WINDOW, 2 * KV_W), lambda n: (cur(n), ATTN_W // (2 * KV_W)))
    kvp_spec = pl.BlockSpec((WINDOW, 2 * KV_W), lambda n: (jnp.maximum(cur(n) - 1, 0), ATTN_W // (2 * KV_W)))
    w_spec = pl.BlockSpec((1, LANES), lambda n: (0, 0))
    l_spec = pl.BlockSpec((WINDOW, N_Q_HEADS), lambda n: (cur(n), 0))
    gate_specs = [pl.BlockSpec((WINDOW, CW), lambda n, col=OFF_AGATE + j: (cur(n), col)) for j in range(ATTN_W // CW)]
    return q_spec, kvc_spec, kvp_spec, w_spec, l_spec, gate_specs


def _attn2_fwd(proj, qw2, kw2, sinks, deps=()):
    seq = proj.shape[0]
    nblk = seq // WINDOW
    scale = 1.0 / math.sqrt(HEAD_DIM)
    q_spec, kvc_spec, kvp_spec, w_spec, l_spec, gate_specs = _attn_in_specs(nblk, False)
    nd, ng = len(deps), len(gate_specs)

    def body(sink_ref, q_ref, kvc_ref, kvp_ref, qw_ref, kw_ref, *rest):
        gate_refs = rest[:ng]
        o_ref, lse_ref, ya_ref = rest[ng + nd:]
        n = pl.program_id(0)
        low, low2 = _low_half(WINDOW), _low_half(2 * WINDOW)
        valid = _stacked_band_mask(n)
        head_lane = lax.broadcasted_iota(jnp.int32, (WINDOW, N_Q_HEADS), 1)
        kv = jnp.concatenate([kvp_ref[...], kvc_ref[...]], axis=0)
        qwv, kwv = qw_ref[...], kw_ref[...]
        lse_blk = jnp.zeros((WINDOW, N_Q_HEADS), F32)
        for t in range(N_KV_HEADS // HEADS_PER_TILE):
            kt = kv[:, t * LANES:(t + 1) * LANES]
            vt = kv[:, KV_W + t * LANES:KV_W + (t + 1) * LANES]
            kn = kt * _pair_rstd(kt, low2) * kwv
            for hi in range(HEADS_PER_TILE):
                g = HEADS_PER_TILE * t + hi
                kdup = _dup_half(kn, hi, low2).astype(_MXU)
                vdup = _dup_half(vt, hi, low2).astype(_MXU)
                stack = []
                for tq in (2 * g, 2 * g + 1):
                    qt = q_ref[:, tq * LANES:(tq + 1) * LANES]
                    stack += _split_heads(qt * _pair_rstd(qt, low) * qwv, low)
                qs = jnp.concatenate(stack, axis=0).astype(_MXU)
                s = lax.dot_general(qs, kdup, _NT, preferred_element_type=F32) * scale
                s = jnp.where(valid, s, -1e30)
                sink = _stacked_sinks(sink_ref, g)
                m = jnp.maximum(jnp.max(s, axis=-1, keepdims=True), sink)
                e = jnp.exp(s - m)
                z = jnp.sum(e, axis=-1, keepdims=True) + jnp.exp(sink - m)
                o = jnp.dot((e / z).astype(_MXU), vdup, preferred_element_type=F32)
                for i, tq in enumerate((2 * g, 2 * g + 1)):
                    o_ref[:, tq * LANES:(tq + 1) * LANES] = jnp.where(
                        low, o[2 * i * WINDOW:(2 * i + 1) * WINDOW], o[(2 * i + 1) * WINDOW:(2 * i + 2) * WINDOW])
                lse = m + jnp.log(z)
                for r in range(Q_PER_KV):
                    lse_blk = jnp.where(head_lane == Q_PER_KV * g + r, lse[r * WINDOW:(r + 1) * WINDOW], lse_blk)
        lse_ref[...] = lse_blk
        for j, g_ref in enumerate(gate_refs):
            cols = slice(j * CW, (j + 1) * CW)
            gate = g_ref[...]
            ya_ref[:, cols] = (o_ref[:, cols] * (gate * _sigmoid(gate))).astype(ya_ref.dtype)

    return pl.pallas_call(
        body, name="attn_fwd", grid=(nblk,),
        in_specs=[pl.BlockSpec(memory_space=pltpu.SMEM), q_spec, kvc_spec, kvp_spec, w_spec, w_spec] + gate_specs
        + [_ANY] * nd,
        out_specs=[q_spec, l_spec, q_spec],
        out_shape=[jax.ShapeDtypeStruct((seq, ATTN_W), F32), jax.ShapeDtypeStruct((seq, N_Q_HEADS), F32),
                   jax.ShapeDtypeStruct((seq, ATTN_W), _MXU)],
        compiler_params=_params(("arbitrary",)),
    )(sinks, proj, proj, proj, qw2, kw2, *([proj] * ng), *deps)


def _attn2_bwd(proj, qw2, kw2, sinks, lse, attn, dya, d_proj, deps=()):
    seq = proj.shape[0]
    nblk = seq // WINDOW
    scale = 1.0 / math.sqrt(HEAD_DIM)
    q_spec, kvc_spec, kvp_spec, w_spec, l_spec, gate_specs = _attn_in_specs(nblk, True)
    s_spec = pl.BlockSpec((1, N_Q_HEADS), lambda n: (0, 0))
    d_spec = pl.BlockSpec((WINDOW, QKV_W + ATTN_W), lambda n: (nblk - 1 - n, 0))
    deps = list(deps) + [d_proj]
    nd, ng = len(deps), len(gate_specs)

    def body(sink_ref, q_ref, kvc_ref, kvp_ref, qw_ref, kw_ref, lse_ref, attn_ref, dya_ref, *rest):
        gate_refs = rest[:ng]
        d_ref, dqw_ref, dkw_ref, dsk_ref, carry, do_ref = rest[ng + nd:]
        step = pl.program_id(0)
        n = nblk - 1 - step

        @pl.when(step == 0)
        def _():
            carry[...] = jnp.zeros_like(carry)
            dqw_ref[...] = jnp.zeros_like(dqw_ref)
            dkw_ref[...] = jnp.zeros_like(dkw_ref)
            dsk_ref[...] = jnp.zeros_like(dsk_ref)

        for j, g_ref in enumerate(gate_refs):
            cols = slice(j * CW, (j + 1) * CW)
            f, df = _silu_and_grad(g_ref[...])
            dv = dya_ref[:, cols]
            do_ref[:, cols] = dv * f
            d_ref[:, QKV_W + j * CW:QKV_W + (j + 1) * CW] = (dv * attn_ref[:, cols] * df).astype(d_ref.dtype)

        low, low2 = _low_half(WINDOW), _low_half(2 * WINDOW)
        valid = _stacked_band_mask(n)
        head_lane = lax.broadcasted_iota(jnp.int32, (WINDOW, N_Q_HEADS), 1)
        sink_lane = lax.broadcasted_iota(jnp.int32, (1, N_Q_HEADS), 1)
        kv = jnp.concatenate([kvp_ref[...], kvc_ref[...]], axis=0)
        qwv, kwv = qw_ref[...], kw_ref[...]
        lse_blk = lse_ref[...]
        dqw = jnp.zeros((1, LANES), F32)
        dkw = jnp.zeros((1, LANES), F32)
        dsk = jnp.zeros((1, N_Q_HEADS), F32)
        for t in range(N_KV_HEADS // HEADS_PER_TILE):
            kt = kv[:, t * LANES:(t + 1) * LANES]
            vt = kv[:, KV_W + t * LANES:KV_W + (t + 1) * LANES]
            rk = _pair_rstd(kt, low2)
            kn = kt * rk * kwv
            dkn_t = jnp.zeros((2 * WINDOW, LANES), F32)
            dv_t = jnp.zeros((2 * WINDOW, LANES), F32)
            for hi in range(HEADS_PER_TILE):
                g = HEADS_PER_TILE * t + hi
                kdup = _dup_half(kn, hi, low2).astype(_MXU)
                vdup = _dup_half(vt, hi, low2).astype(_MXU)
                tiles = (2 * g, 2 * g + 1)
                qx, rq, stack, dstack, lse_rows = [], [], [], [], []
                for tq in tiles:
                    qt = q_ref[:, tq * LANES:(tq + 1) * LANES]
                    r = _pair_rstd(qt, low)
                    rq.append(r)
                    qx.append(qt * r)
                    stack += _split_heads(qx[-1] * qwv, low)
                    dstack += _split_heads(do_ref[:, tq * LANES:(tq + 1) * LANES], low)
                for r in range(Q_PER_KV):
                    lse_rows.append(jnp.sum(jnp.where(head_lane == Q_PER_KV * g + r, lse_blk, 0.0), axis=-1, keepdims=True))
                qs = jnp.concatenate(stack, axis=0).astype(_MXU)
                dos = jnp.concatenate(dstack, axis=0).astype(_MXU)
                lse_col = jnp.concatenate(lse_rows, axis=0)
                s = lax.dot_general(qs, kdup, _NT, preferred_element_type=F32) * scale
                s = jnp.where(valid, s, -1e30)
                p = jnp.exp(s - lse_col)
                dp = lax.dot_general(dos, vdup, _NT, preferred_element_type=F32)
                dsum = jnp.sum(p * dp, axis=-1, keepdims=True)
                ds = (p * (dp - dsum) * scale).astype(_MXU)
                dsink = -jnp.exp(_stacked_sinks(sink_ref, g) - lse_col) * dsum
                for r in range(Q_PER_KV):
                    dsk = dsk + jnp.where(sink_lane == Q_PER_KV * g + r, _colsum(dsink[r * WINDOW:(r + 1) * WINDOW]), 0.0)
                dv_g = _fold_halves(lax.dot_general(p.astype(_MXU), dos, _TN, preferred_element_type=F32))
                dkn_g = _fold_halves(lax.dot_general(ds, qs, _TN, preferred_element_type=F32))
                dv_t = jnp.where(low2, dv_t, dv_g) if hi else jnp.where(low2, dv_g, dv_t)
                dkn_t = jnp.where(low2, dkn_t, dkn_g) if hi else jnp.where(low2, dkn_g, dkn_t)
                dqn = jnp.dot(ds, kdup, preferred_element_type=F32)
                for i, tq in enumerate(tiles):
                    dqn_t = jnp.where(low, dqn[2 * i * WINDOW:(2 * i + 1) * WINDOW],
                                      dqn[(2 * i + 1) * WINDOW:(2 * i + 2) * WINDOW])
                    dq = rq[i] * (qwv * dqn_t - qx[i] * _pair_mean(dqn_t * qwv * qx[i], low))
                    d_ref[:, tq * LANES:(tq + 1) * LANES] = dq.astype(d_ref.dtype)
                    dqw = dqw + _colsum(dqn_t * qx[i])
            k_cols = slice(t * LANES, (t + 1) * LANES)
            v_cols = slice(KV_W + t * LANES, KV_W + (t + 1) * LANES)
            dkn_c = dkn_t[WINDOW:] + carry[:, k_cols]
            rc = rk[WINDOW:]
            kx = kt[WINDOW:] * rc
            dk = rc * (kwv * dkn_c - kx * _pair_mean(dkn_c * kwv * kx, low))
            d_ref[:, ATTN_W + t * LANES:ATTN_W + (t + 1) * LANES] = dk.astype(d_ref.dtype)
            d_ref[:, ATTN_W + KV_W + t * LANES:ATTN_W + KV_W + (t + 1) * LANES] = (
                dv_t[WINDOW:] + carry[:, v_cols]).astype(d_ref.dtype)
            carry[:, k_cols] = dkn_t[:WINDOW]
            carry[:, v_cols] = dv_t[:WINDOW]
            dkw = dkw + _colsum(dkn_c * kx)
        dqw_ref[...] += dqw
        dkw_ref[...] += dkw
        dsk_ref[...] += dsk

    return pl.pallas_call(
        body, name="attn_bwd", grid=(nblk,),
        in_specs=[pl.BlockSpec(memory_space=pltpu.SMEM), q_spec, kvc_spec, kvp_spec, w_spec, w_spec, l_spec, q_spec,
                  q_spec] + gate_specs + [_ANY] * nd,
        out_specs=[d_spec, w_spec, w_spec, s_spec],
        out_shape=[jax.ShapeDtypeStruct(d_proj.shape, d_proj.dtype), jax.ShapeDtypeStruct((1, LANES), F32),
                   jax.ShapeDtypeStruct((1, LANES), F32), jax.ShapeDtypeStruct((1, N_Q_HEADS), F32)],
        input_output_aliases={9 + ng + nd - 1: 0},
        scratch_shapes=[pltpu.VMEM((WINDOW, 2 * KV_W), F32), pltpu.VMEM((WINDOW, ATTN_W), F32)],
        compiler_params=_params(("arbitrary",)),
    )(sinks, proj, proj, proj, qw2, kw2, lse, attn, dya, *([proj] * ng), *deps)


def _ssm_discretise(a_re, a_im, log_dt):
    dt = jnp.exp(log_dt)
    mag = jnp.exp(dt * a_re)
    ab_re = mag * jnp.cos(dt * a_im)
    ab_im = mag * jnp.sin(dt * a_im)
    num_re = ab_re - 1.0
    num_im = ab_im
    den = a_re * a_re + a_im * a_im
    cf_re = (num_re * a_re + num_im * a_im) / den
    cf_im = (num_im * a_re - num_re * a_im) / den
    return ab_re, ab_im, cf_re, cf_im


def _ssm_params_fwd(a_re, a_im, log_dt):
    shp = jax.ShapeDtypeStruct(a_re.shape, F32)

    def body(are_ref, aim_ref, ldt_ref, abr_ref, abi_ref, cfr_ref, cfi_ref, alr_ref, ali_ref):
        abr, abi, cfr, cfi = _ssm_discretise(are_ref[...], aim_ref[...], ldt_ref[...])
        abr_ref[...], abi_ref[...], cfr_ref[...], cfi_ref[...] = abr, abi, cfr, cfi
        pr, pi = abr, abi
        for _ in range(int(math.log2(SSM_L))):
            pr, pi = pr * pr - pi * pi, 2.0 * pr * pi
        alr_ref[...], ali_ref[...] = pr, pi

    return pl.pallas_call(body, name="ssm_params_fwd", out_shape=[shp] * 6)(a_re, a_im, log_dt)


def _ssm_params_bwd(a_re, a_im, log_dt, d_abr, d_abi, d_cfr, d_cfi):
    def body(are_ref, aim_ref, ldt_ref, g0, g1, g2, g3, dare_ref, daim_ref, dldt_ref):
        _, vjp = jax.vjp(_ssm_discretise, are_ref[...], aim_ref[...], ldt_ref[...])
        dare_ref[...], daim_ref[...], dldt_ref[...] = vjp((g0[...], g1[...], g2[...], g3[...]))

    return pl.pallas_call(
        body, name="ssm_params_bwd",
        out_shape=[jax.ShapeDtypeStruct(a_re.shape, F32), jax.ShapeDtypeStruct(a_im.shape, F32),
                   jax.ShapeDtypeStruct(log_dt.shape, F32)],
    )(a_re, a_im, log_dt, d_abr, d_abi, d_cfr, d_cfi)


def _scan_cols(j):
    return pl.ds(j * SSM_SB, SSM_SB)


def _rows8(r):
    return pl.ds(pl.multiple_of(r * SUBLANES, SUBLANES), SUBLANES)


def _bcast8(row):
    return jnp.broadcast_to(row, (SUBLANES, row.shape[-1]))


def _token_order_pick():
    tok = lax.broadcasted_iota(jnp.int32, (SSM_T, SSM_T), 0)
    row = lax.broadcasted_iota(jnp.int32, (SSM_T, SSM_T), 1)
    return (row == SUBLANES * (tok % SSM_L) + tok // SSM_L).astype(_MXU)


SCAN_UNROLL = 8


def _scan_loop(n, step, init):
    def trip(o, carry):
        for i in range(SCAN_UNROLL):
            carry = step(o * SCAN_UNROLL + i, carry)
        return carry

    return lax.fori_loop(0, n // SCAN_UNROLL, trip, init)


def _ssm_fwd(u, b_re, b_im, c_re, c_im, d_skip, coef):
    seq = u.shape[0]
    nc = seq // SSM_T
    T, L = SSM_T, SSM_L

    def body(u_ref, bre_ref, bim_ref, cre_ref, cim_ref, d_ref, are_ref, aim_ref, cfr_ref, cfi_ref, alr_ref, ali_ref,
             y_ref, yg_ref, sre_ref, sim_ref, ire_ref, iim_ref, car_re, car_im, end_re, end_im, yg_scan):
        c = pl.program_id(0)

        @pl.when(c == 0)
        def _():
            car_re[...] = jnp.zeros_like(car_re)
            car_im[...] = jnp.zeros_like(car_im)

        for j in range(SSM_JB):
            ub = u_ref[:, j * LANES:(j + 1) * LANES].astype(_MXU)
            bur = jnp.dot(ub, bre_ref[j], preferred_element_type=F32)
            bui = jnp.dot(ub, bim_ref[j], preferred_element_type=F32)
            cfr, cfi = cfr_ref[:, _scan_cols(j)], cfi_ref[:, _scan_cols(j)]
            sre_ref[:, _scan_cols(j)] = cfr * bur - cfi * bui
            sim_ref[:, _scan_cols(j)] = cfr * bui + cfi * bur

        for j in range(SSM_JB):
            cols = _scan_cols(j)
            ar, ai = _bcast8(are_ref[:, cols]), _bcast8(aim_ref[:, cols])

            def step1(r, s, cols=cols, ar=ar, ai=ai):
                sr, si = s
                rows = _rows8(r)
                return (ar * sr - ai * si + sre_ref[rows, cols], ar * si + ai * sr + sim_ref[rows, cols])

            zero = jnp.zeros((SUBLANES, SSM_SB), F32)
            er, ei = _scan_loop(L, step1, (zero, zero))
            end_re[:, cols] = er
            end_im[:, cols] = ei

        alr, ali = alr_ref[...], ali_ref[...]
        cr, ci = car_re[...], car_im[...]
        ire_ref[0:1, :] = cr
        iim_ref[0:1, :] = ci
        for i in range(1, SUBLANES):
            er, ei = end_re[i - 1:i, :], end_im[i - 1:i, :]
            cr, ci = alr * cr - ali * ci + er, alr * ci + ali * cr + ei
            ire_ref[i:i + 1, :] = cr
            iim_ref[i:i + 1, :] = ci

        for j in range(SSM_JB):
            cols = _scan_cols(j)
            ar, ai = _bcast8(are_ref[:, cols]), _bcast8(aim_ref[:, cols])

            def step2(r, s, cols=cols, ar=ar, ai=ai):
                sr, si = s
                rows = _rows8(r)
                nr = ar * sr - ai * si + sre_ref[rows, cols]
                ni = ar * si + ai * sr + sim_ref[rows, cols]
                sre_ref[rows, cols] = nr
                sim_ref[rows, cols] = ni
                return nr, ni

            _scan_loop(L, step2, (ire_ref[:, cols], iim_ref[:, cols]))

        car_re[...] = sre_ref[T - 1:T, :]
        car_im[...] = sim_ref[T - 1:T, :]

        for j in range(SSM_JB):
            cols = _scan_cols(j)
            ch = slice(j * LANES, (j + 1) * LANES)
            y = (jnp.dot(sre_ref[:, cols].astype(_MXU), cre_ref[j], preferred_element_type=F32)
                 - jnp.dot(sim_ref[:, cols].astype(_MXU), cim_ref[j], preferred_element_type=F32))
            y = y + d_ref[:, ch] * u_ref[:, ch]
            y_ref[:, ch] = y
            yg_scan[:, ch] = jax.nn.gelu(y).astype(yg_scan.dtype)
        yg_ref[...] = jnp.dot(_token_order_pick(), yg_scan[...], preferred_element_type=F32).astype(yg_ref.dtype)

    tok = pl.BlockSpec((T, SSM_W), lambda c: (c, 0))
    st = pl.BlockSpec((T, N_STATES), lambda c: (c, 0))
    ini = pl.BlockSpec((None, SUBLANES, N_STATES), lambda c: (c, 0, 0))
    bsp = pl.BlockSpec((SSM_JB, LANES, SSM_SB), lambda c: (0, 0, 0))
    csp = pl.BlockSpec((SSM_JB, SSM_SB, LANES), lambda c: (0, 0, 0))
    row_w = pl.BlockSpec((1, SSM_W), lambda c: (0, 0))
    row_s = pl.BlockSpec((1, N_STATES), lambda c: (0, 0))
    return pl.pallas_call(
        body, name="ssm_fwd", grid=(nc,),
        in_specs=[tok, bsp, bsp, csp, csp, row_w] + [row_s] * 6,
        out_specs=[tok, tok, st, st, ini, ini],
        out_shape=[jax.ShapeDtypeStruct((seq, SSM_W), F32), jax.ShapeDtypeStruct((seq, SSM_W), _MXU),
                   jax.ShapeDtypeStruct((seq, N_STATES), F32), jax.ShapeDtypeStruct((seq, N_STATES), F32),
                   jax.ShapeDtypeStruct((nc, SUBLANES, N_STATES), F32),
                   jax.ShapeDtypeStruct((nc, SUBLANES, N_STATES), F32)],
        scratch_shapes=[pltpu.VMEM((1, N_STATES), F32), pltpu.VMEM((1, N_STATES), F32),
                        pltpu.VMEM((SUBLANES, N_STATES), F32), pltpu.VMEM((SUBLANES, N_STATES), F32),
                        pltpu.VMEM((T, SSM_W), _MXU)],
        compiler_params=_params(("arbitrary",)),
    )(u, b_re, b_im, c_re, c_im, d_skip, *coef)


def _ssm_bwd(dyg, y, u, s_re, s_im, i_re, i_im, b_re, b_im, c_re, c_im, d_skip, coef, d_proj, deps=()):
    seq = u.shape[0]
    nc = seq // SSM_T
    T, L = SSM_T, SSM_L
    deps = list(deps) + [d_proj]

    def body(dyg_ref, y_ref, u_ref, sre_ref, sim_ref, ire_ref, iim_ref, bre_ref, bim_ref, cre_ref, cim_ref, d_ref,
             are_ref, aim_ref, cfr_ref, cfi_ref, alr_ref, ali_ref, *rest):
        (du_ref, dbre_out, dbim_out, dcre_out, dcim_out, dd_ref, dar_ref, dai_ref, dcfr_ref, dcfi_ref,
         lre, lim, car_re, car_im, end_re, end_im, ini_re, ini_im, dbre_ref, dbim_ref, dcre_ref, dcim_ref,
         dy_ref, du_scan) = rest[len(deps):]
        step = pl.program_id(0)
        dy_ref[...] = jax.vjp(jax.nn.gelu, y_ref[...])[1](dyg_ref[...])[0]

        @pl.when(step == 0)
        def _():
            car_re[...] = jnp.zeros_like(car_re)
            car_im[...] = jnp.zeros_like(car_im)
            for ref in (dbre_ref, dbim_ref, dcre_ref, dcim_ref, dd_ref, dar_ref, dai_ref, dcfr_ref, dcfi_ref):
                ref[...] = jnp.zeros_like(ref)

        for j in range(SSM_JB):
            dyb = dy_ref[:, j * LANES:(j + 1) * LANES].astype(_MXU)
            lre[:, _scan_cols(j)] = lax.dot_general(dyb, cre_ref[j], _NT, preferred_element_type=F32)
            lim[:, _scan_cols(j)] = -lax.dot_general(dyb, cim_ref[j], _NT, preferred_element_type=F32)

        for j in range(SSM_JB):
            cols = _scan_cols(j)
            ar, ai = _bcast8(are_ref[:, cols]), _bcast8(aim_ref[:, cols])

            def step1(t, s, cols=cols, ar=ar, ai=ai):
                sr, si = s
                rows = _rows8(L - 1 - t)
                return (ar * sr + ai * si + lre[rows, cols], ar * si - ai * sr + lim[rows, cols])

            zero = jnp.zeros((SUBLANES, SSM_SB), F32)
            er, ei = _scan_loop(L, step1, (zero, zero))
            end_re[:, cols] = er
            end_im[:, cols] = ei

        alr, ali = alr_ref[...], ali_ref[...]
        cr, ci = car_re[...], car_im[...]
        ini_re[SUBLANES - 1:SUBLANES, :] = cr
        ini_im[SUBLANES - 1:SUBLANES, :] = ci
        for i in range(SUBLANES - 2, -1, -1):
            er, ei = end_re[i + 1:i + 2, :], end_im[i + 1:i + 2, :]
            cr, ci = alr * cr + ali * ci + er, alr * ci - ali * cr + ei
            ini_re[i:i + 1, :] = cr
            ini_im[i:i + 1, :] = ci

        for j in range(SSM_JB):
            cols = _scan_cols(j)
            ar, ai = _bcast8(are_ref[:, cols]), _bcast8(aim_ref[:, cols])

            def step2(t, s, cols=cols, ar=ar, ai=ai):
                sr, si = s
                rows = _rows8(L - 1 - t)
                nr = ar * sr + ai * si + lre[rows, cols]
                ni = ar * si - ai * sr + lim[rows, cols]
                lre[rows, cols] = nr
                lim[rows, cols] = ni
                return nr, ni

            _scan_loop(L, step2, (ini_re[:, cols], ini_im[:, cols]))

        car_re[...] = lre[0:1, :]
        car_im[...] = lim[0:1, :]

        head, tail, body_rows = slice(0, SUBLANES), slice(SUBLANES, T), slice(0, T - SUBLANES)
        for j in range(SSM_JB):
            cols = _scan_cols(j)
            ch = slice(j * LANES, (j + 1) * LANES)
            lr, li = lre[:, cols], lim[:, cols]
            lt_r, lt_i, sp_r, sp_i = lre[tail, cols], lim[tail, cols], sre_ref[body_rows, cols], sim_ref[body_rows, cols]
            lh_r, lh_i, si_r, si_i = lre[head, cols], lim[head, cols], ire_ref[:, cols], iim_ref[:, cols]
            dar_ref[:, cols] += _colsum(lt_r * sp_r + lt_i * sp_i) + _colsum(lh_r * si_r + lh_i * si_i)
            dai_ref[:, cols] += _colsum(lt_i * sp_r - lt_r * sp_i) + _colsum(lh_i * si_r - lh_r * si_i)
            uf = u_ref[:, ch]
            ub = uf.astype(_MXU)
            bur = jnp.dot(ub, bre_ref[j], preferred_element_type=F32)
            bui = jnp.dot(ub, bim_ref[j], preferred_element_type=F32)
            dcfr_ref[:, cols] += _colsum(lr * bur + li * bui)
            dcfi_ref[:, cols] += _colsum(li * bur - lr * bui)
            cfr, cfi = cfr_ref[:, cols], cfi_ref[:, cols]
            dbur = (cfr * lr + cfi * li).astype(_MXU)
            dbui = (cfr * li - cfi * lr).astype(_MXU)
            dyf = dy_ref[:, ch]
            dyb = dyf.astype(_MXU)
            du = (lax.dot_general(dbur, bre_ref[j], _NT, preferred_element_type=F32)
                  + lax.dot_general(dbui, bim_ref[j], _NT, preferred_element_type=F32) + d_ref[:, ch] * dyf)
            du_scan[:, ch] = du.astype(du_scan.dtype)
            dbre_ref[j] += lax.dot_general(ub, dbur, _TN, preferred_element_type=F32)
            dbim_ref[j] += lax.dot_general(ub, dbui, _TN, preferred_element_type=F32)
            dcre_ref[j] += lax.dot_general(sre_ref[:, cols].astype(_MXU), dyb, _TN, preferred_element_type=F32)
            dcim_ref[j] -= lax.dot_general(sim_ref[:, cols].astype(_MXU), dyb, _TN, preferred_element_type=F32)
            dd_ref[:, ch] += _colsum(dyf * uf)
        du_ref[...] = jnp.dot(_token_order_pick(), du_scan[...], preferred_element_type=F32).astype(du_ref.dtype)

        @pl.when(step == nc - 1)
        def _():
            for acc, out in ((dbre_ref, dbre_out), (dbim_ref, dbim_out), (dcre_ref, dcre_out), (dcim_ref, dcim_out)):
                pltpu.sync_copy(acc, out)

    tok = pl.BlockSpec((T, SSM_W), lambda c: (nc - 1 - c, 0))
    st = pl.BlockSpec((T, N_STATES), lambda c: (nc - 1 - c, 0))
    ini = pl.BlockSpec((None, SUBLANES, N_STATES), lambda c: (nc - 1 - c, 0, 0))
    bsp = pl.BlockSpec((SSM_JB, LANES, SSM_SB), lambda c: (0, 0, 0))
    csp = pl.BlockSpec((SSM_JB, SSM_SB, LANES), lambda c: (0, 0, 0))
    row_w = pl.BlockSpec((1, SSM_W), lambda c: (0, 0))
    row_s = pl.BlockSpec((1, N_STATES), lambda c: (0, 0))
    big = pltpu.VMEM((T, N_STATES), F32)
    one = pltpu.VMEM((1, N_STATES), F32)
    eight = pltpu.VMEM((SUBLANES, N_STATES), F32)
    return pl.pallas_call(
        body, name="ssm_bwd", grid=(nc,),
        in_specs=[tok, tok, tok, st, st, ini, ini, bsp, bsp, csp, csp, row_w] + [row_s] * 6 + [_ANY] * len(deps),
        out_specs=[pl.BlockSpec((pl.Element(T), pl.Element(SSM_W)), lambda c: ((nc - 1 - c) * T, OFF_U * CW)),
                   _ANY, _ANY, _ANY, _ANY, row_w, row_s, row_s, row_s, row_s],
        input_output_aliases={18 + len(deps) - 1: 0},
        out_shape=[jax.ShapeDtypeStruct(d_proj.shape, d_proj.dtype),
                   jax.ShapeDtypeStruct((SSM_JB, LANES, SSM_SB), F32), jax.ShapeDtypeStruct((SSM_JB, LANES, SSM_SB), F32),
                   jax.ShapeDtypeStruct((SSM_JB, SSM_SB, LANES), F32), jax.ShapeDtypeStruct((SSM_JB, SSM_SB, LANES), F32),
                   jax.ShapeDtypeStruct((1, SSM_W), F32)] + [jax.ShapeDtypeStruct((1, N_STATES), F32)] * 4,
        scratch_shapes=[big, big, one, one, eight, eight, eight, eight,
                        pltpu.VMEM((SSM_JB, LANES, SSM_SB), F32), pltpu.VMEM((SSM_JB, LANES, SSM_SB), F32),
                        pltpu.VMEM((SSM_JB, SSM_SB, LANES), F32), pltpu.VMEM((SSM_JB, SSM_SB, LANES), F32),
                        pltpu.VMEM((T, SSM_W), F32), pltpu.VMEM((T, SSM_W), _MXU)],
        compiler_params=_params(("arbitrary",)),
    )(dyg, y, u, s_re, s_im, i_re, i_im, b_re, b_im, c_re, c_im, d_skip, *coef, *deps)


def _block_diag_b(b):
    t = b.reshape(SSM_JB, 8, STATE, GROUP).transpose(0, 1, 3, 2)
    eye = jnp.eye(8, dtype=b.dtype)
    return (t[:, :, :, None, :] * eye[None, :, None, :, None]).reshape(SSM_JB, LANES, SSM_SB)


def _block_diag_c(c):
    t = c.reshape(SSM_JB, 8, GROUP, STATE).transpose(0, 1, 3, 2)
    eye = jnp.eye(8, dtype=c.dtype)
    return (t[:, :, :, None, :] * eye[None, :, None, :, None]).reshape(SSM_JB, SSM_SB, LANES)


def _diag_of_b(blk):
    t = blk.reshape(SSM_JB, 8, GROUP, 8, STATE)
    d = jnp.sum(t * jnp.eye(8, dtype=blk.dtype)[None, :, None, :, None], axis=3)
    return d.transpose(0, 1, 3, 2).reshape(N_GROUPS, STATE, GROUP)


def _diag_of_c(blk):
    t = blk.reshape(SSM_JB, 8, STATE, 8, GROUP)
    d = jnp.sum(t * jnp.eye(8, dtype=blk.dtype)[None, :, None, :, None], axis=3)
    return d.transpose(0, 1, 3, 2).reshape(N_GROUPS, GROUP, STATE)


def _to_scan_order(v):
    seq, w = v.shape
    return v.reshape(seq // SSM_T, SUBLANES, SSM_L, w).transpose(0, 2, 1, 3).reshape(seq, w)


def _adamw_math(w, g, m, v):
    nm = ADAM_B1 * m + (1.0 - ADAM_B1) * g
    nv = ADAM_B2 * v + (1.0 - ADAM_B2) * jnp.square(g)
    m_hat = nm / (1.0 - ADAM_B1 ** ADAM_STEP)
    v_hat = nv / (1.0 - ADAM_B2 ** ADAM_STEP)
    return -ADAM_LR * (m_hat / (jnp.sqrt(v_hat) + ADAM_EPS) + ADAM_WD * w), nm, nv


def _adamw(w, g, m, v, *, name, tm, deps=()):
    rows, cols = w.shape
    nd = len(deps)

    def body(w_ref, g_ref, m_ref, v_ref, *rest):
        d_ref, nm_ref, nv_ref = rest[nd:]
        d_ref[...], nm_ref[...], nv_ref[...] = _adamw_math(w_ref[...], g_ref[...], m_ref[...], v_ref[...])

    spec = pl.BlockSpec((tm, cols), lambda i: (i, 0))
    shp = jax.ShapeDtypeStruct((rows, cols), F32)
    return pl.pallas_call(body, name=name, grid=(rows // tm,), in_specs=[spec] * 4 + [_ANY] * nd,
                          out_specs=[spec] * 3, out_shape=[shp] * 3,
                          compiler_params=_params(("arbitrary",)))(w, g, m, v, *deps)


def _place():
    x, y, c = lax.axis_index("x"), lax.axis_index("y"), lax.axis_index("c")
    chips = [(1 - x, y), (x, 1 - y), (1 - x, 1 - y)]
    return x, y, c, chips


def _remote(src, dst, send_sem, recv_sem, dev):
    return pltpu.make_async_remote_copy(src_ref=src, dst_ref=dst, send_sem=send_sem, recv_sem=recv_sem,
                                        device_id=dev, device_id_type=MESH)


def _place_shard(w, mine_arr, *, name, tm=256, deps=()):
    rows, cols = w.shape

    def body(m_ref, w_ref, *rest):
        rest[-1][...] = w_ref[...].astype(rest[-1].dtype)

    return pl.pallas_call(
        body, name=name,
        grid_spec=pltpu.PrefetchScalarGridSpec(
            num_scalar_prefetch=1, grid=(rows // tm,),
            in_specs=[pl.BlockSpec((tm, cols), lambda i, m: (i, 0))] + [_ANY] * len(deps),
            out_specs=pl.BlockSpec((None, tm, cols), lambda i, m: (m[0], i, 0))),
        out_shape=jax.ShapeDtypeStruct((N_CHIPS, rows, cols), _WIRE),
        compiler_params=_params(("arbitrary",)),
    )(mine_arr, w, *deps)


_HBM = pl.BlockSpec(memory_space=pltpu.HBM)
_SEM = pl.BlockSpec(memory_space=pltpu.SEMAPHORE)
_EFFECT = pltpu.SideEffectType.DATAFLOW_SIDE_EFFECTING


def _copies_start(name, bufs, plan, count, after=()):
    nb, na = len(bufs), len(after)

    def body(*refs):
        send_sems, recv_sems, token = refs[nb + na], refs[nb + na + 1], refs[-1]
        copies = plan(refs[:nb])
        assert len(copies) == count
        for i, (src, dst, dev, _) in enumerate(copies):
            _remote(src, dst, send_sems.at[i], recv_sems.at[i], dev).start()
        token[...] = jnp.zeros_like(token)

    res = pl.pallas_call(
        body, name=name, in_specs=[_HBM] * nb + [_ANY] * na,
        out_specs=(_SEM, _SEM, *[_HBM] * nb, pl.BlockSpec(memory_space=pltpu.VMEM)),
        out_shape=(pltpu.SemaphoreType.DMA((count,)), pltpu.SemaphoreType.DMA((count,)),
                   *[pltpu.HBM(b.shape, b.dtype) for b in bufs], jax.ShapeDtypeStruct((SUBLANES, LANES), F32)),
        input_output_aliases={i: 2 + i for i in range(nb)},
        compiler_params=pltpu.CompilerParams(has_side_effects=_EFFECT),
    )(*[pltpu.with_memory_space_constraint(b, pltpu.HBM) for b in bufs], *after)
    return (res[0], res[1]), list(res[2:2 + nb]), res[-1]


def _copies_wait(name, bufs, sems, plan, after=(), which=None):
    nb, na = len(bufs), len(after)

    def body(*refs):
        send_sems, recv_sems = refs[nb], refs[nb + 1]
        for i, (src, _, dev, land) in enumerate(plan(refs[:nb])):
            if which is not None and i not in which:
                continue
            cp = _remote(src, land, send_sems.at[i], recv_sems.at[i], dev)
            cp.wait_send()
            cp.wait_recv()

    res = pl.pallas_call(
        body, name=name, in_specs=[_HBM] * nb + [_SEM, _SEM] + [_ANY] * na, out_specs=[_HBM] * nb,
        out_shape=[pltpu.HBM(b.shape, b.dtype) for b in bufs],
        input_output_aliases={i: i for i in range(nb)},
        compiler_params=pltpu.CompilerParams(has_side_effects=_EFFECT),
    )(*bufs, *sems, *after)
    return list(res)


def _plan_gather_ici(fulls, which=(0, 1, 2)):
    x, y, c, chips = _place()
    copies = []
    for f in fulls:
        half = pl.ds(c * (f.shape[1] // 2), f.shape[1] // 2)
        own = f.at[2 * x + y, half]
        for chip in [chips[k] for k in which]:
            copies.append((own, own, (*chip, c), f.at[2 * chip[0] + chip[1], half]))
    return copies


def _plan_gather_d2d(fulls, which=(0, 1, 2)):
    x, y, c, chips = _place()
    copies = []
    for f in fulls:
        r2 = f.shape[1] // 2
        for chip in [chips[k] for k in which]:
            blk = 2 * chip[0] + chip[1]
            landed = f.at[blk, pl.ds(c * r2, r2)]
            copies.append((landed, landed, (x, y, 1 - c), f.at[blk, pl.ds((1 - c) * r2, r2)]))
    return copies


def _plan_relay_direct(fulls):
    (f,) = fulls
    x, y, c, chips = _place()
    half = pl.ds(c * (f.shape[1] // 2), f.shape[1] // 2)
    own = f.at[2 * x + y, half]
    return [(own, own, (*chip, c), f.at[2 * chip[0] + chip[1], half]) for chip in chips[:2]]


def _plan_relay_forward(fulls, k):
    (f,) = fulls
    x, y, c, chips = _place()
    r2 = f.shape[1] // 2
    half, other = pl.ds(c * r2, r2), pl.ds((1 - c) * r2, r2)
    quarter = pl.ds(c * r2 + k * (r2 // 2), r2 // 2)
    blk, far = 2 * chips[k][0] + chips[k][1], 2 * chips[2][0] + chips[2][1]
    passed, landed = f.at[blk, quarter], f.at[blk, half]
    return [(passed, passed, (*chips[1 - k], c), f.at[far, quarter]), (landed, landed, (x, y, 1 - c), f.at[blk, other])]


def _plan_relay_last(fulls):
    (f,) = fulls
    x, y, c, chips = _place()
    r2 = f.shape[1] // 2
    far = 2 * chips[2][0] + chips[2][1]
    landed = f.at[far, pl.ds(c * r2, r2)]
    return [(landed, landed, (x, y, 1 - c), f.at[far, pl.ds((1 - c) * r2, r2)])]


def _plan_swap_halves(refs):
    x, y, c, _ = _place()
    n = len(refs) // 2
    copies = []
    for g, land in zip(refs[:n], refs[n:]):
        r2 = g.shape[1] // 2
        copies.append((g.at[:, pl.ds((1 - c) * r2, r2), :], land, (x, y, 1 - c), land))
    return copies


def _plan_scatter_chips(refs):
    x, y, c, chips = _place()
    n = len(refs) // 2
    copies = []
    for h, land in zip(refs[:n], refs[n:]):
        for k, chip in enumerate(chips):
            copies.append((h.at[2 * chip[0] + chip[1]], land.at[k], (*chip, c), land.at[k]))
    return copies


def _plan_join_halves(totals):
    x, y, c, _ = _place()
    copies = []
    for t in totals:
        r2 = t.shape[0] // 2
        mine = t.at[pl.ds(c * r2, r2)]
        copies.append((mine, mine, (x, y, 1 - c), t.at[pl.ds((1 - c) * r2, r2)]))
    return copies


def _add_sibling_half(g, got, c_arr, *, name, tm):
    _, rows, cols = g.shape
    r2 = rows // 2
    nb = r2 // tm

    def body(c_ref, g_ref, r_ref, o_ref):
        o_ref[...] = (g_ref[...].astype(F32) + r_ref[...].astype(F32)).astype(o_ref.dtype)

    return pl.pallas_call(
        body, name=name,
        grid_spec=pltpu.PrefetchScalarGridSpec(
            num_scalar_prefetch=1, grid=(N_CHIPS, nb),
            in_specs=[pl.BlockSpec((None, tm, cols), lambda b, i, c: (b, c[0] * nb + i, 0)),
                      pl.BlockSpec((None, tm, cols), lambda b, i, c: (b, i, 0))],
            out_specs=pl.BlockSpec((None, tm, cols), lambda b, i, c: (b, i, 0))),
        out_shape=jax.ShapeDtypeStruct((N_CHIPS, r2, cols), _WIRE),
        compiler_params=_params(("arbitrary", "arbitrary")),
    )(c_arr, g, got)


def _add_chips(h, got, place_arr, *, name, tm):
    _, r2, cols = h.shape
    nb = r2 // tm

    def body(p_ref, h_ref, r_ref, o_ref):
        o_ref[...] = ((h_ref[...].astype(F32) + r_ref[0].astype(F32)) + r_ref[1].astype(F32)) + r_ref[2].astype(F32)

    return pl.pallas_call(
        body, name=name,
        grid_spec=pltpu.PrefetchScalarGridSpec(
            num_scalar_prefetch=1, grid=(nb,),
            in_specs=[pl.BlockSpec((None, tm, cols), lambda i, p: (p[0], i, 0)),
                      pl.BlockSpec((3, tm, cols), lambda i, p: (0, i, 0))],
            out_specs=pl.BlockSpec((tm, cols), lambda i, p: (p[1] * nb + i, 0))),
        out_shape=jax.ShapeDtypeStruct((2 * r2, cols), F32),
        compiler_params=_params(("arbitrary",)),
    )(place_arr, h, got)


class _ReduceScatter:
    def __init__(self, tag, names, grads):
        self.tag, self.names, self.n = tag, names, len(names)
        core = lax.axis_index("c").astype(jnp.int32)
        chip = (2 * lax.axis_index("x") + lax.axis_index("y")).astype(jnp.int32)
        self.c_arr, self.place_arr = core.reshape(1), jnp.stack([chip, core])
        self.bufs = list(grads)

    def _start(self, step, bufs, plan, count, after):
        self.plan = plan
        self.step = f"grad_{step}_{self.tag}"
        self.sems, self.bufs, token = _copies_start(self.step + "_start", bufs, plan, count, after)
        return [token]

    def _wait(self, after):
        self.bufs = _copies_wait(self.step + "_wait", self.bufs, self.sems, self.plan, after)
        return self.bufs

    def start_swap(self, after=()):
        lands = [lax.empty((N_CHIPS, g.shape[1] // 2, g.shape[2]), g.dtype) for g in self.bufs]
        return self._start("swap", self.bufs + lands, _plan_swap_halves, self.n, after)

    def start_scatter(self, after):
        bufs = self._wait(after)
        pair = [_add_sibling_half(g, r, self.c_arr, name=f"grad_add_sibling_{nm}", tm=min(256, g.shape[1] // 2))
                for nm, g, r in zip(self.names, bufs[:self.n], bufs[self.n:])]
        lands = [lax.empty((3,) + h.shape[1:], h.dtype) for h in pair]
        return self._start("scatter", pair + lands, _plan_scatter_chips, 3 * self.n, ())

    def start_join(self, after):
        bufs = self._wait(after)
        total = [_add_chips(h, r, self.place_arr, name=f"grad_add_chips_{nm}", tm=min(256, h.shape[1]))
                 for nm, h, r in zip(self.names, bufs[:self.n], bufs[self.n:])]
        return self._start("join", total, _plan_join_halves, self.n, ())

    def finish(self, after):
        return dict(zip(self.names, self._wait(after)))


def _all_gather_small(v):
    m_per, n = v.shape

    def body(x_ref, out_ref, send_sems, recv_sems, local_sem):
        x, y, c, chips = _place()
        me, sibling = (x, y, c), (x, y, 1 - c)

        def rows(px, py, pc):
            return out_ref.at[4 * px + 2 * py + pc]

        def copy(k, block, to, src=None):
            return _remote(rows(*block) if src is None else src, rows(*block), send_sems.at[k], recv_sems.at[k], to)

        mine = pltpu.make_async_copy(x_ref, rows(*me), local_sem)
        mine.start()
        first = [copy(0, me, sibling, src=x_ref)]
        first += [copy(1 + j, me, (*chip, c), src=x_ref) for j, chip in enumerate(chips)]
        for cp in first:
            cp.start()
        passed = [copy(4 + j, (*chip, c), sibling) for j, chip in enumerate(chips)]
        for j, chip in enumerate(chips):
            copy(1 + j, (*chip, c), me).wait_recv()
            passed[j].start()
        copy(0, sibling, me).wait_recv()
        for j, chip in enumerate(chips):
            copy(4 + j, (*chip, 1 - c), me).wait_recv()
        for cp in first + passed:
            cp.wait_send()
        mine.wait()

    return pl.pallas_call(
        body, name="gather_small_grads",
        out_shape=jax.ShapeDtypeStruct((8, m_per, n), v.dtype),
        in_specs=[pl.BlockSpec(memory_space=pltpu.VMEM)], out_specs=pl.BlockSpec(memory_space=pltpu.VMEM),
        scratch_shapes=[pltpu.SemaphoreType.DMA((7,)), pltpu.SemaphoreType.DMA((7,)), pltpu.SemaphoreType.DMA],
        compiler_params=pltpu.CompilerParams(vmem_limit_bytes=VMEM_LIMIT),
    )(v)


def _sum8(v, *, name):
    _, m, n = v.shape

    def body(v_ref, o_ref):
        acc = v_ref[0]
        for d in range(1, 8):
            acc = acc + v_ref[d]
        o_ref[...] = acc

    return pl.pallas_call(body, name=name, out_shape=jax.ShapeDtypeStruct((m, n), F32),
                          compiler_params=pltpu.CompilerParams(vmem_limit_bytes=VMEM_LIMIT))(v)


def _local_step(x, target, norm_w, q_norm_w, k_norm_w, sinks, a_re, a_im, log_dt, b_re, b_im, c_re, c_im, d_skip,
                b_glu, io):
    seq = x.shape[0]
    qw2 = jnp.tile(q_norm_w.reshape(1, HEAD_DIM), (1, HEADS_PER_TILE))
    kw2 = jnp.tile(k_norm_w.reshape(1, HEAD_DIM), (1, HEADS_PER_TILE))
    nw, bg = norm_w.reshape(1, D_MODEL), b_glu.reshape(1, D_MODEL)
    dsk = d_skip.reshape(1, SSM_W)

    h, rstd = _rms_fwd(x, nw, deps=io.begin())
    proj, w_in4 = io.projection(h)
    attn, lse, ya_in = _attn2_fwd(proj, qw2, kw2, sinks, deps=io.after_proj(proj))
    w_ap4 = io.weight("w_attn_proj", ya_in)
    w_glu4, w_sp4, w_out = io.weight("w_glu", ya_in), io.weight("w_ssm_proj", ya_in), io.weight("w_out", ya_in)
    y_a = _mm(ya_in, w_ap4, mode="nn", name="mm_attn_proj", tm=2048, tn=512, tk=ATTN_W, b_blocked=True,
              rows_outer=True)

    flat_a = (a_re.reshape(1, N_STATES), a_im.reshape(1, N_STATES), jnp.repeat(log_dt, STATE).reshape(1, N_STATES))
    coef = _ssm_params_fwd(*flat_a)
    bre_blk, bim_blk = _block_diag_b(b_re).astype(_MXU), _block_diag_b(b_im).astype(_MXU)
    cre_blk, cim_blk = _block_diag_c(c_re).astype(_MXU), _block_diag_c(c_im).astype(_MXU)
    u_scan = _to_scan_order(proj[:, OFF_U * CW:OFF_U * CW + SSM_W])
    y_scan, yg, s_re, s_im, i_re, i_im = _ssm_fwd(u_scan, bre_blk, bim_blk, cre_blk, cim_blk, dsk, coef)
    glu, ys_in = _mm_glu_gate(yg, w_glu4, bg, proj)
    y_s = _mm(ys_in, w_sp4, mode="nn", name="mm_ssm_proj", tm=2048, tn=512, tk=SSM_W, b_blocked=True,
              rows_outer=True)

    merged, dout, dout_b, sq = _mm_merge_out_loss(proj, y_a, y_s, w_out, x, target)
    loss = 0.5 * jnp.sum(sq) / D_MODEL

    d_ya, d_ys, d_proj = _mm_merge_bwd(dout_b, w_out, proj, y_a, y_s)
    g_w_out = _mm(merged, dout_b, mode="tn", name="mm_g_w_out", tm=1024, tn=D_MODEL, tk=1024, out_dtype=_WIRE)

    d_ya_in = _mm(d_ya, w_ap4, mode="nt", name="mm_d_attn_gate", tm=2048, tn=ATTN_W, tk=512, b_blocked=True)
    g_w_ap = _mm(ya_in, d_ya, mode="tn", name="mm_g_w_attn_proj", tm=ATTN_W, tn=D_MODEL, tk=2048, out_dtype=_WIRE,
                 out_blocked=True)

    g_w_sp = _mm(ys_in, d_ys, mode="tn", name="mm_g_w_ssm_proj", tm=SSM_W, tn=D_MODEL, tk=2048, out_dtype=_WIRE,
                 out_blocked=True)
    d_glu, d_proj, g_bglu = _mm_ssm_gate_bwd(d_ys, w_sp4, glu, bg, proj, d_proj)
    d_yg = _mm(d_glu, w_glu4, mode="nt", name="mm_d_gelu", tm=2048, tn=SSM_W, tk=512, b_blocked=True)
    g_w_glu = _mm(yg, d_glu, mode="tn", name="mm_g_w_glu", tm=SSM_W, tn=D_MODEL, tk=2048, out_dtype=_WIRE, out_blocked=True)
    dep = io.later_grads(dict(w_attn_proj=g_w_ap, w_glu=g_w_glu, w_ssm_proj=g_w_sp,
                              w_out=g_w_out.reshape(N_CHIPS, D_MODEL // N_CHIPS, D_MODEL)))

    d_proj, g_qw2, g_kw2, g_sk = _attn2_bwd(proj, qw2, kw2, sinks, lse, attn, d_ya_in, d_proj, deps=dep)
    dep = io.before_scan_backward([d_proj])
    (d_proj, g_bre, g_bim, g_cre, g_cim, g_dsk, g_abr, g_abi, g_cfr, g_cfi) = _ssm_bwd(
        _to_scan_order(d_yg), y_scan, u_scan, s_re, s_im, i_re, i_im, bre_blk, bim_blk, cre_blk, cim_blk, dsk, coef,
        d_proj, deps=dep)
    g_are, g_aim, g_ldt = _ssm_params_bwd(*flat_a, g_abr, g_abi, g_cfr, g_cfi)
    g_are, g_aim = g_are.reshape(N_GROUPS, STATE), g_aim.reshape(N_GROUPS, STATE)
    g_ldt = g_ldt.reshape(N_GROUPS, STATE).sum(axis=1)
    dep = io.before_input_projection_grad([d_proj]) + io.small_grads(dict(
        q_norm_w=g_qw2[0, :HEAD_DIM] + g_qw2[0, HEAD_DIM:], k_norm_w=g_kw2[0, :HEAD_DIM] + g_kw2[0, HEAD_DIM:],
        sinks=g_sk.reshape(N_Q_HEADS), A_re=g_are, A_im=g_aim, log_dt=g_ldt,
        B_re=_diag_of_b(g_bre), B_im=_diag_of_b(g_bim), C_re=_diag_of_c(g_cre), C_im=_diag_of_c(g_cim),
        D_skip=g_dsk.reshape(N_GROUPS, GROUP), b_glu=g_bglu.reshape(D_MODEL)))
    g_w_in = _mm(h, d_proj, mode="tn", name="mm_g_w_in", tm=1024, tn=IN_W // 4, tk=1024, out_dtype=_WIRE,
                 out_blocked=True, deps=dep)
    dep = io.input_projection_grad(g_w_in)
    d_h = _mm(d_proj, w_in4, mode="nt", name="mm_d_h", tm=1024, tn=D_MODEL, tk=IN_W // 4, b_blocked=True, deps=dep)
    grad_x, g_nw = _rms_bwd(d_h, x, rstd, nw, dout)
    return loss, grad_x, g_nw.reshape(D_MODEL)


_SMALL = ["norm_w", "q_norm_w", "k_norm_w", "sinks", "A_re", "A_im", "log_dt", "B_re", "B_im", "C_re", "C_im",
          "D_skip", "b_glu"]
_BIG = ["w_in", "w_attn_proj", "w_glu", "w_ssm_proj", "w_out"]
_LATER = _BIG[1:]
_RELATIONS = ("flip_x", "flip_y", "flip_xy")
_ORDER = ["norm_w", "w_in", "q_norm_w", "k_norm_w", "sinks", "w_attn_proj", "A_re", "A_im", "log_dt", "B_re", "B_im",
          "C_re", "C_im", "D_skip", "w_glu", "b_glu", "w_ssm_proj", "w_out"]
_PACK_W = 1024


def _packed_rows(size):
    unit = SUBLANES * _PACK_W
    return -(-size // unit) * SUBLANES


def _pack_small(d, names):
    parts = []
    for n in names:
        flat = d[n].reshape(-1).astype(F32)
        rows = _packed_rows(flat.shape[0])
        parts.append(jnp.pad(flat, (0, rows * _PACK_W - flat.shape[0])).reshape(rows, _PACK_W))
    return jnp.concatenate(parts, axis=0)


def _unpack_small(packed, like, names):
    out, pos = {}, 0
    for n in names:
        rows = _packed_rows(like[n].size)
        out[n] = packed[pos:pos + rows].reshape(-1)[:like[n].size].reshape(like[n].shape)
        pos += rows
    return out


def _place_block(v, index_arr, *, name):
    rows, cols = v.shape

    def body(i_ref, v_ref, o_ref):
        o_ref[...] = v_ref[...]

    return pl.pallas_call(
        body, name=name,
        grid_spec=pltpu.PrefetchScalarGridSpec(
            num_scalar_prefetch=1, grid=(1,),
            in_specs=[pl.BlockSpec((rows, cols), lambda i, d: (0, 0))],
            out_specs=pl.BlockSpec((None, rows, cols), lambda i, d: (d[0], 0, 0))),
        out_shape=jax.ShapeDtypeStruct((8, rows, cols), v.dtype),
        compiler_params=_params(("arbitrary",)),
    )(index_arr, v)


def _plan_all_to_all(refs):
    (land,) = refs
    x, y, c, _ = _place()
    own = land.at[4 * x + 2 * y + c]
    copies = []
    for fx, fy, fc in [(0, 0, 1), (0, 1, 0), (0, 1, 1), (1, 0, 0), (1, 0, 1), (1, 1, 0), (1, 1, 1)]:
        px, py, pc = (1 - x) if fx else x, (1 - y) if fy else y, (1 - c) if fc else c
        copies.append((own, own, (px, py, pc), land.at[4 * px + 2 * py + pc]))
    return copies


def _adamw_whole(w, g, m, v, *, name):
    def body(w_ref, g_ref, m_ref, v_ref, d_ref, nm_ref, nv_ref):
        d_ref[...], nm_ref[...], nv_ref[...] = _adamw_math(w_ref[...], g_ref[...], m_ref[...], v_ref[...])

    return pl.pallas_call(body, name=name, out_shape=[jax.ShapeDtypeStruct(w.shape, F32)] * 3)(w, g, m, v)


class _Exchanges:
    def __init__(self, w, m, v):
        self.w, self.m, self.v = w, m, v
        self.grads, self.delta, self.new_m, self.new_v = {}, {}, {}, {}

    def _adamw(self, names, deps):
        for n in names:
            self.delta[n], self.new_m[n], self.new_v[n] = _adamw(
                self.w[n], self.grads[n], self.m[n], self.v[n], name=f"adamw_{n}", tm=128, deps=deps)

    def begin(self):
        chip = (2 * lax.axis_index("x") + lax.axis_index("y")).astype(jnp.int32).reshape(1)
        w_in = _place_shard(self.w["w_in"], chip, name="place_w_in")
        self.w_in_sems, self.w_in_buf, token = _copies_start("gather_w_in_direct_start", [w_in], _plan_relay_direct, 2)
        self.later_full = [_place_shard(self.w[n], chip, name=f"place_{n}", deps=[token]) for n in _LATER]
        return self.later_full

    def projection(self, h):
        x, y = lax.axis_index("x"), lax.axis_index("y")
        blks = [jnp.asarray(b, jnp.int32).reshape(1)
                for b in (2 * x + y, 2 * (1 - x) + y, 2 * x + (1 - y), 2 * (1 - x) + (1 - y))]
        bufs = self.w_in_buf
        proj = _mm_chip_block(h, bufs[0], blks[0], None, name="mm_proj_own")
        relay, token = [], proj
        for k, tag in enumerate(_RELATIONS[:2]):
            bufs = _copies_wait(f"gather_w_in_direct_{tag}_wait", bufs, self.w_in_sems, _plan_relay_direct, [token],
                                which=(k,))
            plan = functools.partial(_plan_relay_forward, k=k)
            sems, bufs, token = _copies_start(f"gather_w_in_relay_{tag}_start", bufs, plan, 2)
            relay.append((sems, plan))
        self.rest = _copies_start("gather_ici_rest_start", self.later_full, _plan_gather_ici, 3 * len(_LATER),
                                  after=[token])
        token = self.rest[2]
        for k, tag in enumerate(_RELATIONS[:2]):
            bufs = _copies_wait(f"gather_w_in_hand_{tag}_wait", bufs, relay[k][0], relay[k][1], [token], which=(1,))
            token = proj = _mm_chip_block(h, bufs[0], blks[1 + k], proj, name=f"mm_proj_{tag}")
        for k, tag in enumerate(_RELATIONS[:2]):
            bufs = _copies_wait(f"gather_w_in_relay_{tag}_wait", bufs, relay[k][0], relay[k][1], [token], which=(0,))
        sems, bufs, token = _copies_start("gather_w_in_last_start", bufs, _plan_relay_last, 1)
        bufs = _copies_wait("gather_w_in_last_wait", bufs, sems, _plan_relay_last, [token])
        proj = _mm_chip_block(h, bufs[0], blks[3], proj, name="mm_proj_flip_xy")
        return proj, bufs[0]

    def weight(self, name, after):
        if self.rest is not None:
            sems, bufs = self.rest
            later = dict(zip(_LATER, _copies_wait("gather_d2d_rest_wait", bufs, sems, _plan_gather_d2d, [after])))
            later["w_out"] = later["w_out"].reshape(D_MODEL, D_MODEL)
            self.later, self.rest = later, None
        return self.later[name]

    def after_proj(self, proj):
        sems, bufs, _ = self.rest
        bufs = _copies_wait("gather_ici_rest_wait", bufs, sems, _plan_gather_ici, [proj])
        sems, bufs, token = _copies_start("gather_d2d_rest_start", bufs, _plan_gather_d2d, 3 * len(_LATER))
        self.rest = (sems, bufs)
        return [token]

    def later_grads(self, grads):
        self.rs_later = _ReduceScatter("later", _LATER, [grads[n] for n in _LATER])
        return self.rs_later.start_swap()

    def before_scan_backward(self, after):
        return self.rs_later.start_scatter(after)

    def before_input_projection_grad(self, after):
        return self.rs_later.start_join(after)

    def input_projection_grad(self, g_w_in):
        self.grads.update(self.rs_later.finish([g_w_in]))
        self.rs_in = _ReduceScatter("w_in", ["w_in"], [g_w_in])
        self._adamw(_LATER, self.rs_in.start_swap())
        return self.rs_in.start_scatter([self.delta[n] for n in _LATER])

    def _adamw_small(self, names):
        for n in names:
            self.delta[n], self.new_m[n], self.new_v[n] = _adamw_whole(
                self.w[n], self.grads[n], self.m[n], self.v[n], name=f"adamw_{n}")

    def small_grads(self, grads):
        me = (4 * lax.axis_index("x") + 2 * lax.axis_index("y") + lax.axis_index("c")).astype(jnp.int32).reshape(1)
        land = _place_block(_pack_small(grads, _SMALL[1:]), me, name="place_small_grads")
        self.small = _copies_start("gather_small_start", [land], _plan_all_to_all, 7)
        return [self.small[2]]

    def finish(self, g_norm_w, loss, after):
        join = self.rs_in.start_join(after)
        sems, bufs, _ = self.small
        (land,) = _copies_wait("gather_small_wait", bufs, sems, _plan_all_to_all, join)
        self.grads.update(_unpack_small(_sum8(land, name="sum_small_grads"), self.w, _SMALL[1:]))
        self._adamw_small(_SMALL[1:])
        rows = _packed_rows(g_norm_w.size)
        late = jnp.concatenate([_pack_small(dict(norm_w=g_norm_w), _SMALL[:1]),
                                jnp.pad(loss.reshape(1, 1), ((0, SUBLANES - 1), (0, _PACK_W - 1)))], axis=0)
        late = _sum8(_all_gather_small(late), name="sum_norm_w_grad_and_loss")
        self.grads.update(_unpack_small(late[:rows], self.w, _SMALL[:1]))
        self._adamw_small(_SMALL[:1])
        self.grads.update(self.rs_in.finish([self.delta[_SMALL[0]]]))
        self._adamw(["w_in"], ())
        return late[rows, 0]


def kernel(x, norm_w, w_in, q_norm_w, k_norm_w, sinks, w_attn_proj, A_re, A_im, log_dt, B_re, B_im, C_re, C_im, D_skip, w_glu, b_glu, w_ssm_proj, w_out, loss_target, m_norm_w, m_w_in, m_q_norm_w, m_k_norm_w, m_sinks, m_w_attn_proj, m_A_re, m_A_im, m_log_dt, m_B_re, m_B_im, m_C_re, m_C_im, m_D_skip, m_w_glu, m_b_glu, m_w_ssm_proj, m_w_out, v_norm_w, v_w_in, v_q_norm_w, v_k_norm_w, v_sinks, v_w_attn_proj, v_A_re, v_A_im, v_log_dt, v_B_re, v_B_im, v_C_re, v_C_im, v_D_skip, v_w_glu, v_b_glu, v_w_ssm_proj, v_w_out):
    w = dict(norm_w=norm_w, w_in=w_in, q_norm_w=q_norm_w, k_norm_w=k_norm_w, sinks=sinks, w_attn_proj=w_attn_proj,
             A_re=A_re, A_im=A_im, log_dt=log_dt, B_re=B_re, B_im=B_im, C_re=C_re, C_im=C_im, D_skip=D_skip,
             w_glu=w_glu, b_glu=b_glu, w_ssm_proj=w_ssm_proj, w_out=w_out)
    m = dict(norm_w=m_norm_w, w_in=m_w_in, q_norm_w=m_q_norm_w, k_norm_w=m_k_norm_w, sinks=m_sinks,
             w_attn_proj=m_w_attn_proj, A_re=m_A_re, A_im=m_A_im, log_dt=m_log_dt, B_re=m_B_re, B_im=m_B_im,
             C_re=m_C_re, C_im=m_C_im, D_skip=m_D_skip, w_glu=m_w_glu, b_glu=m_b_glu, w_ssm_proj=m_w_ssm_proj,
             w_out=m_w_out)
    v = dict(norm_w=v_norm_w, w_in=v_w_in, q_norm_w=v_q_norm_w, k_norm_w=v_k_norm_w, sinks=v_sinks,
             w_attn_proj=v_w_attn_proj, A_re=v_A_re, A_im=v_A_im, log_dt=v_log_dt, B_re=v_B_re, B_im=v_B_im,
             C_re=v_C_re, C_im=v_C_im, D_skip=v_D_skip, w_glu=v_w_glu, b_glu=v_b_glu, w_ssm_proj=v_w_ssm_proj,
             w_out=v_w_out)

    io = _Exchanges(w, m, v)
    loss, grad_x, g_norm_w = _local_step(x[0], loss_target[0], norm_w, q_norm_w, k_norm_w, sinks, A_re, A_im, log_dt,
                                         B_re, B_im, C_re, C_im, D_skip, b_glu, io)
    loss = io.finish(g_norm_w, loss, [grad_x])
    grads, delta, new_m, new_v = io.grads, io.delta, io.new_m, io.new_v

    return (loss, grad_x[None], *[grads[n] for n in _ORDER], *[delta[n] for n in _ORDER],
            *[new_m[n] for n in _ORDER], *[new_v[n] for n in _ORDER])
```

```python
import functools
import math

import jax
import jax.numpy as jnp
from jax import lax
from jax.experimental import pallas as pl
from jax.experimental.pallas import tpu as pltpu

F32 = jnp.float32
_MXU = jnp.bfloat16
_WIRE = jnp.bfloat16

LANES = 128
SUBLANES = 8
VMEM_LIMIT = 56 * 1024 * 1024

D_MODEL = 2048
HEAD_DIM = 64
N_Q_HEADS = 16
N_KV_HEADS = 4
Q_PER_KV = 4
ATTN_W = 1024
KV_W = 256
WINDOW = 128
SSM_W = 1024
GROUP = 16
N_GROUPS = 64
STATE = 64
N_STATES = N_GROUPS * STATE
IN_W = 8704
NORM_EPS = 1e-6
N_CHIPS = 4
CW = 512
OFF_AGATE, OFF_U, OFF_Z, OFF_GA, OFF_GS = 3, 5, 7, 9, 13

SSM_T = 256
SSM_L = SSM_T // SUBLANES
SSM_JB = 8
SSM_SB = N_STATES // SSM_JB

ADAM_LR, ADAM_B1, ADAM_B2, ADAM_EPS, ADAM_WD, ADAM_STEP = 0.001, 0.9, 0.999, 1e-08, 0.01, 10

MESH = pl.DeviceIdType.MESH
_ANY = pl.BlockSpec(memory_space=pl.ANY)


def _params(sem=None):
    return pltpu.CompilerParams(dimension_semantics=sem, vmem_limit_bytes=VMEM_LIMIT)


def _mm(a, b, *, mode, name, tm, tn, tk, out_dtype=F32, b_blocked=False, out_blocked=False, rows_outer=False,
        deps=()):
    nd = len(deps)
    if mode == "tn":
        K, M = a.shape
    else:
        M, K = a.shape
    if mode == "nn":
        N = b.shape[0] * b.shape[2] if b_blocked else b.shape[1]
    elif mode == "nt":
        N = b.shape[1] if b_blocked else b.shape[0]
    else:
        N = b.shape[1]
    tm, tn, tk = min(tm, M), min(tn, N), min(tk, K)
    nj, ni, nk = N // tn, M // tm, K // tk
    assert nj * tn == N and ni * tm == M and nk * tk == K, (name, M, N, K)
    dims = {"nn": (((1,), (0,)), ((), ())), "nt": (((1,), (1,)), ((), ())), "tn": (((0,), (0,)), ((), ()))}[mode]

    if mode == "tn":
        a_spec = pl.BlockSpec((tk, tm), lambda j, i, k: (k, i))
    else:
        a_spec = pl.BlockSpec((tm, tk), lambda j, i, k: (i, k))
    if mode == "nn":
        if b_blocked:
            assert b.shape[0] == nj and b.shape[2] == tn
            b_spec = pl.BlockSpec((None, tk, tn), lambda j, i, k: (j, k, 0))
        else:
            b_spec = pl.BlockSpec((tk, tn), lambda j, i, k: (k, j))
    elif mode == "nt":
        if b_blocked:
            assert b.shape[0] == nk and b.shape[2] == tk
            b_spec = pl.BlockSpec((None, tn, tk), lambda j, i, k: (k, j, 0))
        else:
            b_spec = pl.BlockSpec((tn, tk), lambda j, i, k: (j, k))
    else:
        b_spec = pl.BlockSpec((tk, tn), lambda j, i, k: (k, j))
    whole_out = out_blocked and nj == 1
    if whole_out:
        assert ni == 1
        o_spec = pl.BlockSpec((N_CHIPS, tm, tn // N_CHIPS), lambda j, i, k: (0, 0, 0))
        o_shape = jax.ShapeDtypeStruct((N_CHIPS, M, tn // N_CHIPS), out_dtype)
    elif out_blocked:
        assert nj == N_CHIPS
        o_spec = pl.BlockSpec((None, tm, tn), lambda j, i, k: (j, i, 0))
        o_shape = jax.ShapeDtypeStruct((nj, M, tn), out_dtype)
    else:
        o_spec = pl.BlockSpec((tm, tn), lambda j, i, k: (i, j))
        o_shape = jax.ShapeDtypeStruct((M, N), out_dtype)
    use_acc = nk > 1 and (out_dtype != F32 or whole_out)

    def body(a_ref, b_ref, *rest):
        o_ref, scratch = rest[nd], rest[nd + 1:]

        def product():
            return lax.dot_general(a_ref[...].astype(_MXU), b_ref[...].astype(_MXU), dims, preferred_element_type=F32)

        def write(result):
            if whole_out:
                w = tn // N_CHIPS
                for c in range(N_CHIPS):
                    o_ref[c] = result[:, c * w:(c + 1) * w].astype(o_ref.dtype)
            else:
                o_ref[...] = result.astype(o_ref.dtype)

        if nk == 1:
            write(product())
            return
        k = pl.program_id(2)
        acc = scratch[0] if use_acc else o_ref

        @pl.when(k == 0)
        def _():
            acc[...] = jnp.zeros_like(acc)

        acc[...] += product()

        if use_acc:
            @pl.when(k == nk - 1)
            def _():
                write(acc[...])

    specs = [a_spec, b_spec, o_spec]
    grid = (nj, ni, nk)
    if rows_outer:
        specs = [pl.BlockSpec(s.block_shape, lambda i, j, k, f=s.index_map: f(j, i, k)) for s in specs]
        grid = (ni, nj, nk)
    return pl.pallas_call(
        body, name=name, grid=grid, in_specs=specs[:2] + [_ANY] * nd, out_specs=specs[2],
        out_shape=o_shape, scratch_shapes=[pltpu.VMEM((tm, tn), F32)] if use_acc else [],
        compiler_params=_params(("parallel", "parallel", "arbitrary")),
    )(a, b, *deps)


def _mm_chip_block(a, b4, blk, prev, *, name, tm=512, deps=()):
    M, K = a.shape
    nchip, _, C = b4.shape
    tm = min(tm, M)
    extra = ([] if prev is None else [prev]) + list(deps)

    def body(blk_ref, a_ref, b_ref, *rest):
        rest[-1][...] = jnp.dot(a_ref[...].astype(_MXU), b_ref[...].astype(_MXU), preferred_element_type=F32)

    return pl.pallas_call(
        body, name=name,
        grid_spec=pltpu.PrefetchScalarGridSpec(
            num_scalar_prefetch=1, grid=(M // tm,),
            in_specs=[pl.BlockSpec((tm, K), lambda i, c: (i, 0)), pl.BlockSpec((None, K, C), lambda i, c: (c[0], 0, 0))]
            + [_ANY] * len(extra),
            out_specs=pl.BlockSpec((tm, C), lambda i, c: (i, c[0]))),
        out_shape=jax.ShapeDtypeStruct((M, nchip * C), F32),
        input_output_aliases={} if prev is None else {3: 0},
        compiler_params=_params(("arbitrary",)),
    )(blk, a, b4, *extra)


def _mm_merge_out_loss(proj, y_a, y_s, w_out, x, target, *, tm=256):
    rows, d = x.shape
    ncol = d // CW

    def body(*refs):
        ga_refs, gs_refs = refs[:ncol], refs[ncol:2 * ncol]
        ya_ref, ys_ref, w_ref, x_ref, t_ref, m_ref, d_ref, db_ref, sq_ref = refs[2 * ncol:]
        for j in range(ncol):
            cols = slice(j * CW, (j + 1) * CW)
            m_ref[:, cols] = (_sigmoid(ga_refs[j][...]) * ya_ref[:, cols]
                              + _sigmoid(gs_refs[j][...]) * ys_ref[:, cols]).astype(m_ref.dtype)
        mo = jnp.dot(m_ref[...], w_ref[...].astype(_MXU), preferred_element_type=F32)
        err = (x_ref[...] + mo) - t_ref[...]
        dout = err * (1.0 / d)
        d_ref[...] = dout
        db_ref[...] = dout.astype(db_ref.dtype)
        part = _colsum(err * err)
        i = pl.program_id(0)

        @pl.when(i == 0)
        def _():
            sq_ref[...] = part

        @pl.when(i > 0)
        def _():
            sq_ref[...] += part

    tile = pl.BlockSpec((tm, d), lambda i: (i, 0))
    gate = [pl.BlockSpec((tm, CW), lambda i, c=off + j: (i, c)) for off in (OFF_GA, OFF_GS) for j in range(ncol)]
    return pl.pallas_call(
        body, name="mm_merge_out_loss", grid=(rows // tm,),
        in_specs=gate + [tile, tile, pl.BlockSpec((d, d), lambda i: (0, 0), pipeline_mode=pl.Buffered(1)), tile, tile],
        out_specs=[tile, tile, tile, pl.BlockSpec((1, d), lambda i: (0, 0))],
        out_shape=[jax.ShapeDtypeStruct((rows, d), _MXU), jax.ShapeDtypeStruct((rows, d), F32),
                   jax.ShapeDtypeStruct((rows, d), _MXU), jax.ShapeDtypeStruct((1, d), F32)],
        compiler_params=_params(("arbitrary",)),
    )(*([proj] * (2 * ncol)), y_a, y_s, w_out, x, target)


def _mm_merge_bwd(dout_b, w_out, proj, y_a, y_s, *, tm=256):
    rows, d = y_a.shape
    ncol = d // CW

    def body(do_ref, w_ref, *refs):
        ga_refs, gs_refs = refs[:ncol], refs[ncol:2 * ncol]
        ya_ref, ys_ref, dya_ref, dys_ref, dg_ref = refs[2 * ncol:]
        dm = lax.dot_general(do_ref[...].astype(_MXU), w_ref[...].astype(_MXU), _NT, preferred_element_type=F32)
        for j in range(ncol):
            cols = slice(j * CW, (j + 1) * CW)
            dmj = dm[:, cols]
            sa, ss = _sigmoid(ga_refs[j][...]), _sigmoid(gs_refs[j][...])
            dya_ref[:, cols] = (sa * dmj).astype(dya_ref.dtype)
            dys_ref[:, cols] = (ss * dmj).astype(dys_ref.dtype)
            dg_ref[:, cols] = (dmj * ya_ref[:, cols] * sa * (1.0 - sa)).astype(dg_ref.dtype)
            dg_ref[:, d + j * CW:d + (j + 1) * CW] = (dmj * ys_ref[:, cols] * ss * (1.0 - ss)).astype(dg_ref.dtype)

    tile = pl.BlockSpec((tm, d), lambda i: (i, 0))
    gate = [pl.BlockSpec((tm, CW), lambda i, c=off + j: (i, c)) for off in (OFF_GA, OFF_GS) for j in range(ncol)]
    both = pl.BlockSpec((pl.Element(tm), pl.Element(2 * d)), lambda i: (i * tm, OFF_GA * CW))
    return pl.pallas_call(
        body, name="mm_merge_bwd", grid=(rows // tm,),
        in_specs=[tile, pl.BlockSpec((d, d), lambda i: (0, 0))] + gate + [tile, tile],
        out_specs=[tile, tile, both],
        out_shape=[jax.ShapeDtypeStruct((rows, d), _MXU)] * 2 + [jax.ShapeDtypeStruct((rows, IN_W), _MXU)],
        compiler_params=_params(("arbitrary",)),
    )(dout_b, w_out, *([proj] * (2 * ncol)), y_a, y_s)


def _mm_glu_gate(yg, w_glu4, b_glu, proj, *, tm=1024):
    rows, k = yg.shape
    nj, _, tn = w_glu4.shape
    w = nj * tn // 2
    tm = min(tm, rows)

    def body(a_ref, w_ref, ba_ref, bb_ref, z0_ref, z1_ref, glu_ref, ys_ref):
        j = pl.program_id(1)
        for c in range(nj):
            @pl.when(j == c)
            def _(c=c):
                glu_ref[:, c * tn:(c + 1) * tn] = jnp.dot(a_ref[...].astype(_MXU), w_ref[...].astype(_MXU),
                                                          preferred_element_type=F32)

        @pl.when(j == nj - 1)
        def _():
            z = jnp.concatenate([z0_ref[...], z1_ref[...]], axis=1)
            ys_ref[...] = ((glu_ref[:, :w] + ba_ref[...]) * _sigmoid(glu_ref[:, w:] + bb_ref[...])
                           * (z * _sigmoid(z))).astype(ys_ref.dtype)

    bias = lambda c: pl.BlockSpec((1, w), lambda i, j: (0, c))
    zcol = lambda c: pl.BlockSpec((tm, CW), lambda i, j: (i, OFF_Z + c))
    return pl.pallas_call(
        body, name="mm_glu_gate", grid=(rows // tm, nj),
        in_specs=[pl.BlockSpec((tm, k), lambda i, j: (i, 0)), pl.BlockSpec((None, k, tn), lambda i, j: (j, 0, 0)),
                  bias(0), bias(1), zcol(0), zcol(1)],
        out_specs=[pl.BlockSpec((tm, nj * tn), lambda i, j: (i, 0)), pl.BlockSpec((tm, w), lambda i, j: (i, 0))],
        out_shape=[jax.ShapeDtypeStruct((rows, nj * tn), F32), jax.ShapeDtypeStruct((rows, w), _MXU)],
        compiler_params=_params(("arbitrary", "arbitrary")),
    )(yg, w_glu4, b_glu, b_glu, proj, proj)


def _mm_ssm_gate_bwd(d_ys, w_sp4, glu, b_glu, proj, d_proj, *, tm=512):
    rows, w = glu.shape[0], glu.shape[1] // 2
    nk, tk = w_sp4.shape[0], w_sp4.shape[2]
    tm = min(tm, rows)

    def body(dy_ref, w_ref, ga_ref, gb_ref, ba_ref, bb_ref, z0_ref, z1_ref, buf_ref, dg_ref, dz_ref, db_ref, acc):
        i, k = pl.program_id(0), pl.program_id(1)

        @pl.when(k == 0)
        def _():
            acc[...] = jnp.zeros_like(acc)

        acc[...] += lax.dot_general(dy_ref[...].astype(_MXU), w_ref[...].astype(_MXU), _NT, preferred_element_type=F32)

        @pl.when(k == nk - 1)
        def _():
            dv = acc[...]
            a, sb = ga_ref[...] + ba_ref[...], _sigmoid(gb_ref[...] + bb_ref[...])
            f, df = _silu_and_grad(jnp.concatenate([z0_ref[...], z1_ref[...]], axis=1))
            dga = dv * sb * f
            dgb = dv * a * f * sb * (1.0 - sb)
            dg_ref[:, :w] = dga.astype(dg_ref.dtype)
            dg_ref[:, w:] = dgb.astype(dg_ref.dtype)
            dz_ref[...] = (dv * a * sb * df).astype(dz_ref.dtype)
            part = jnp.concatenate([_colsum(dga), _colsum(dgb)], axis=1)

            @pl.when(i == 0)
            def _():
                db_ref[...] = part

            @pl.when(i > 0)
            def _():
                db_ref[...] += part

    half = lambda c: pl.BlockSpec((tm, w), lambda i, k: (i, c))
    bias = lambda c: pl.BlockSpec((1, w), lambda i, k: (0, c))
    zcol = lambda c: pl.BlockSpec((tm, CW), lambda i, k: (i, OFF_Z + c))
    return pl.pallas_call(
        body, name="mm_ssm_gate_bwd", grid=(rows // tm, nk),
        in_specs=[pl.BlockSpec((tm, tk), lambda i, k: (i, k)), pl.BlockSpec((None, w, tk), lambda i, k: (k, 0, 0)),
                  half(0), half(1), bias(0), bias(1), zcol(0), zcol(1), _ANY],
        out_specs=[pl.BlockSpec((tm, 2 * w), lambda i, k: (i, 0)),
                   pl.BlockSpec((pl.Element(tm), pl.Element(w)), lambda i, k: (i * tm, OFF_Z * CW)),
                   pl.BlockSpec((1, 2 * w), lambda i, k: (0, 0))],
        out_shape=[jax.ShapeDtypeStruct((rows, 2 * w), _MXU), jax.ShapeDtypeStruct(d_proj.shape, d_proj.dtype),
                   jax.ShapeDtypeStruct((1, 2 * w), F32)],
        input_output_aliases={8: 1},
        scratch_shapes=[pltpu.VMEM((tm, w), F32)],
        compiler_params=_params(("arbitrary", "arbitrary")),
    )(d_ys, w_sp4, glu, glu, b_glu, b_glu, proj, proj, d_proj)


def _ew(fn, ins, outs, *, rows, ncol, name, n_acc=0, tm=512, deps=(), into=None):
    deps = list(deps) + ([into[1]] if into else [])
    n_in, n_out, nd = len(ins), len(outs), len(deps)
    tm = min(tm, rows)
    in_specs = []
    for _, kind, col0 in ins:
        if kind == "mat":
            in_specs.append(pl.BlockSpec((tm, CW), lambda j, i, c0=col0: (i, c0 + j)))
        else:
            in_specs.append(pl.BlockSpec((1, CW), lambda j, i, c0=col0: (0, c0 + j)))
    out_specs = [pl.BlockSpec((tm, CW), lambda j, i: (i, j)) for _ in outs]
    out_shape = [jax.ShapeDtypeStruct((rows, w), dt) for w, dt in outs]
    if into:
        out_specs[into[0]] = pl.BlockSpec((tm, CW), lambda j, i, c0=into[2]: (i, c0 + j))
        out_shape[into[0]] = jax.ShapeDtypeStruct(into[1].shape, into[1].dtype)
    for _ in range(n_acc):
        out_specs.append(pl.BlockSpec((1, CW), lambda j, i: (0, j)))
        out_shape.append(jax.ShapeDtypeStruct((1, ncol * CW), F32))

    def body(*refs):
        vals = fn(*[r[...] for r in refs[:n_in]])
        refs = refs[n_in + nd:]
        for r, v in zip(refs[:n_out], vals[:n_out]):
            r[...] = v.astype(r.dtype)
        i = pl.program_id(1)
        for r, v in zip(refs[n_out:], vals[n_out:]):
            @pl.when(i == 0)
            def _(r=r, v=v):
                r[...] = v

            @pl.when(i > 0)
            def _(r=r, v=v):
                r[...] += v

    res = pl.pallas_call(
        body, name=name, grid=(ncol, rows // tm), in_specs=in_specs + [_ANY] * nd, out_specs=out_specs,
        out_shape=out_shape, input_output_aliases={n_in + nd - 1: into[0]} if into else {},
        compiler_params=_params(("parallel", "arbitrary")),
    )(*[a for a, _, _ in ins], *deps)
    return res


def _colsum(v):
    return jnp.sum(v, axis=0, keepdims=True)


def _sigmoid(v):
    return jax.nn.sigmoid(v)


def _silu_and_grad(v):
    s = _sigmoid(v)
    return v * s, s * (1.0 + v * (1.0 - s))


def _rms_fwd(x, w, *, tm=512, deps=()):
    rows, d = x.shape
    nd = len(deps)

    def body(x_ref, w_ref, *rest):
        h_ref, r_ref = rest[nd:]
        xv = x_ref[...]
        r = lax.rsqrt(jnp.mean(xv * xv, axis=-1, keepdims=True) + NORM_EPS)
        h_ref[...] = (xv * r * w_ref[...]).astype(h_ref.dtype)
        r_ref[...] = r

    return pl.pallas_call(
        body, name="rms_fwd", grid=(rows // tm,),
        in_specs=[pl.BlockSpec((tm, d), lambda i: (i, 0)), pl.BlockSpec((1, d), lambda i: (0, 0))] + [_ANY] * nd,
        out_specs=[pl.BlockSpec((tm, d), lambda i: (i, 0)), pl.BlockSpec((tm, 1), lambda i: (i, 0))],
        out_shape=[jax.ShapeDtypeStruct((rows, d), _MXU), jax.ShapeDtypeStruct((rows, 1), F32)],
        compiler_params=_params(("arbitrary",)),
    )(x, w, *deps)


def _rms_bwd(dh, x, rstd, w, dout, *, tm=256):
    rows, d = x.shape

    def body(dh_ref, x_ref, r_ref, w_ref, do_ref, gx_ref, gw_ref):
        dhv, xv, r, wv = dh_ref[...], x_ref[...], r_ref[...], w_ref[...]
        xr = xv * r
        t = jnp.mean(dhv * wv * xr, axis=-1, keepdims=True)
        gx_ref[...] = do_ref[...] + r * (wv * dhv - xr * t)
        part = _colsum(dhv * xr)
        i = pl.program_id(0)

        @pl.when(i == 0)
        def _():
            gw_ref[...] = part

        @pl.when(i > 0)
        def _():
            gw_ref[...] += part

    return pl.pallas_call(
        body, name="rms_bwd", grid=(rows // tm,),
        in_specs=[pl.BlockSpec((tm, d), lambda i: (i, 0)), pl.BlockSpec((tm, d), lambda i: (i, 0)),
                  pl.BlockSpec((tm, 1), lambda i: (i, 0)), pl.BlockSpec((1, d), lambda i: (0, 0)),
                  pl.BlockSpec((tm, d), lambda i: (i, 0))],
        out_specs=[pl.BlockSpec((tm, d), lambda i: (i, 0)), pl.BlockSpec((1, d), lambda i: (0, 0))],
        out_shape=[jax.ShapeDtypeStruct((rows, d), F32), jax.ShapeDtypeStruct((1, d), F32)],
        compiler_params=_params(("arbitrary",)),
    )(dh, x, rstd, w, dout)


_NT = (((1,), (1,)), ((), ()))
_TN = (((0,), (0,)), ((), ()))


QKV_W = ATTN_W + 2 * KV_W
HEADS_PER_TILE = LANES // HEAD_DIM


def _low_half(rows):
    return lax.broadcasted_iota(jnp.int32, (rows, LANES), 1) < HEAD_DIM


def _pair_mean(t, low):
    m_lo = jnp.sum(jnp.where(low, t, 0.0), axis=-1, keepdims=True)
    m_hi = jnp.sum(jnp.where(low, 0.0, t), axis=-1, keepdims=True)
    return jnp.where(low, m_lo, m_hi) * (1.0 / HEAD_DIM)


def _pair_rstd(t, low):
    return lax.rsqrt(_pair_mean(t * t, low) + NORM_EPS)


def _dup_half(t, hi, low):
    swapped = pltpu.roll(t, HEAD_DIM, 1)
    return jnp.where(low, swapped, t) if hi else jnp.where(low, t, swapped)


def _fold_halves(t):
    return t + pltpu.roll(t, HEAD_DIM, 1)


def _split_heads(t, low):
    return [jnp.where(low, t, 0.0), jnp.where(low, 0.0, t)]


def _stacked_band_mask(n):
    rows = Q_PER_KV * WINDOW
    qi = lax.broadcasted_iota(jnp.int32, (rows, 2 * WINDOW), 0) % WINDOW + WINDOW
    kj = lax.broadcasted_iota(jnp.int32, (rows, 2 * WINDOW), 1)
    diff = qi - kj
    first_key = jnp.where(n > 0, 0, WINDOW)
    return (diff >= 0) & (diff < WINDOW) & (kj >= first_key)


def _stacked_sinks(sink_ref, g):
    blk = lax.broadcasted_iota(jnp.int32, (Q_PER_KV * WINDOW, 1), 0) // WINDOW
    col = jnp.full((Q_PER_KV * WINDOW, 1), sink_ref[Q_PER_KV * g], F32)
    for r in range(1, Q_PER_KV):
        col = jnp.where(blk == r, sink_ref[Q_PER_KV * g + r], col)
    return col


def _attn_in_specs(nblk, rev):
    def cur(n):
        return (nblk - 1 - n) if rev else n

    q_spec = pl.BlockSpec((WINDOW, ATTN_W), lambda n: (cur(n), 0))
    kvc_spec = pl.BlockSpec((WINDOW, 2 * KV_W), lambda n: (cur(n), ATTN_W // (2 * KV_W)))
    kvp_spec = pl.BlockSpec((WINDOW, 2 * KV_W), lambda n: (jnp.maximum(cur(n) - 1, 0), ATTN_W // (2 * KV_W)))
    w_spec = pl.BlockSpec((1, LANES), lambda n: (0, 0))
    l_spec = pl.BlockSpec((WINDOW, N_Q_HEADS), lambda n: (cur(n), 0))
    gate_specs = [pl.BlockSpec((WINDOW, CW), lambda n, col=OFF_AGATE + j: (cur(n), col)) for j in range(ATTN_W // CW)]
    return q_spec, kvc_spec, kvp_spec, w_spec, l_spec, gate_specs


def _attn2_fwd(proj, qw2, kw2, sinks, deps=()):
    seq = proj.shape[0]
    nblk = seq // WINDOW
    scale = 1.0 / math.sqrt(HEAD_DIM)
    q_spec, kvc_spec, kvp_spec, w_spec, l_spec, gate_specs = _attn_in_specs(nblk, False)
    nd, ng = len(deps), len(gate_specs)

    def body(sink_ref, q_ref, kvc_ref, kvp_ref, qw_ref, kw_ref, *rest):
        gate_refs = rest[:ng]
        o_ref, lse_ref, ya_ref = rest[ng + nd:]
        n = pl.program_id(0)
        low, low2 = _low_half(WINDOW), _low_half(2 * WINDOW)
        valid = _stacked_band_mask(n)
        head_lane = lax.broadcasted_iota(jnp.int32, (WINDOW, N_Q_HEADS), 1)
        kv = jnp.concatenate([kvp_ref[...], kvc_ref[...]], axis=0)
        qwv, kwv = qw_ref[...], kw_ref[...]
        lse_blk = jnp.zeros((WINDOW, N_Q_HEADS), F32)
        for t in range(N_KV_HEADS // HEADS_PER_TILE):
            kt = kv[:, t * LANES:(t + 1) * LANES]
            vt = kv[:, KV_W + t * LANES:KV_W + (t + 1) * LANES]
            kn = kt * _pair_rstd(kt, low2) * kwv
            for hi in range(HEADS_PER_TILE):
                g = HEADS_PER_TILE * t + hi
                kdup = _dup_half(kn, hi, low2).astype(_MXU)
                vdup = _dup_half(vt, hi, low2).astype(_MXU)
                stack = []
                for tq in (2 * g, 2 * g + 1):
                    qt = q_ref[:, tq * LANES:(tq + 1) * LANES]
                    stack += _split_heads(qt * _pair_rstd(qt, low) * qwv, low)
                qs = jnp.concatenate(stack, axis=0).astype(_MXU)
                s = lax.dot_general(qs, kdup, _NT, preferred_element_type=F32) * scale
                s = jnp.where(valid, s, -1e30)
                sink = _stacked_sinks(sink_ref, g)
                m = jnp.maximum(jnp.max(s, axis=-1, keepdims=True), sink)
                e = jnp.exp(s - m)
                z = jnp.sum(e, axis=-1, keepdims=True) + jnp.exp(sink - m)
                o = jnp.dot((e / z).astype(_MXU), vdup, preferred_element_type=F32)
                for i, tq in enumerate((2 * g, 2 * g + 1)):
                    o_ref[:, tq * LANES:(tq + 1) * LANES] = jnp.where(
                        low, o[2 * i * WINDOW:(2 * i + 1) * WINDOW], o[(2 * i + 1) * WINDOW:(2 * i + 2) * WINDOW])
                lse = m + jnp.log(z)
                for r in range(Q_PER_KV):
                    lse_blk = jnp.where(head_lane == Q_PER_KV * g + r, lse[r * WINDOW:(r + 1) * WINDOW], lse_blk)
        lse_ref[...] = lse_blk
        for j, g_ref in enumerate(gate_refs):
            cols = slice(j * CW, (j + 1) * CW)
            gate = g_ref[...]
            ya_ref[:, cols] = (o_ref[:, cols] * (gate * _sigmoid(gate))).astype(ya_ref.dtype)

    return pl.pallas_call(
        body, name="attn_fwd", grid=(nblk,),
        in_specs=[pl.BlockSpec(memory_space=pltpu.SMEM), q_spec, kvc_spec, kvp_spec, w_spec, w_spec] + gate_specs
        + [_ANY] * nd,
        out_specs=[q_spec, l_spec, q_spec],
        out_shape=[jax.ShapeDtypeStruct((seq, ATTN_W), F32), jax.ShapeDtypeStruct((seq, N_Q_HEADS), F32),
                   jax.ShapeDtypeStruct((seq, ATTN_W), _MXU)],
        compiler_params=_params(("arbitrary",)),
    )(sinks, proj, proj, proj, qw2, kw2, *([proj] * ng), *deps)


def _attn2_bwd(proj, qw2, kw2, sinks, lse, attn, dya, d_proj, deps=()):
    seq = proj.shape[0]
    nblk = seq // WINDOW
    scale = 1.0 / math.sqrt(HEAD_DIM)
    q_spec, kvc_spec, kvp_spec, w_spec, l_spec, gate_specs = _attn_in_specs(nblk, True)
    s_spec = pl.BlockSpec((1, N_Q_HEADS), lambda n: (0, 0))
    d_spec = pl.BlockSpec((WINDOW, QKV_W + ATTN_W), lambda n: (nblk - 1 - n, 0))
    deps = list(deps) + [d_proj]
    nd, ng = len(deps), len(gate_specs)

    def body(sink_ref, q_ref, kvc_ref, kvp_ref, qw_ref, kw_ref, lse_ref, attn_ref, dya_ref, *rest):
        gate_refs = rest[:ng]
        d_ref, dqw_ref, dkw_ref, dsk_ref, carry, do_ref = rest[ng + nd:]
        step = pl.program_id(0)
        n = nblk - 1 - step

        @pl.when(step == 0)
        def _():
            carry[...] = jnp.zeros_like(carry)
            dqw_ref[...] = jnp.zeros_like(dqw_ref)
            dkw_ref[...] = jnp.zeros_like(dkw_ref)
            dsk_ref[...] = jnp.zeros_like(dsk_ref)

        for j, g_ref in enumerate(gate_refs):
            cols = slice(j * CW, (j + 1) * CW)
            f, df = _silu_and_grad(g_ref[...])
            dv = dya_ref[:, cols]
            do_ref[:, cols] = dv * f
            d_ref[:, QKV_W + j * CW:QKV_W + (j + 1) * CW] = (dv * attn_ref[:, cols] * df).astype(d_ref.dtype)

        low, low2 = _low_half(WINDOW), _low_half(2 * WINDOW)
        valid = _stacked_band_mask(n)
        head_lane = lax.broadcasted_iota(jnp.int32, (WINDOW, N_Q_HEADS), 1)
        sink_lane = lax.broadcasted_iota(jnp.int32, (1, N_Q_HEADS), 1)
        kv = jnp.concatenate([kvp_ref[...], kvc_ref[...]], axis=0)
        qwv, kwv = qw_ref[...], kw_ref[...]
        lse_blk = lse_ref[...]
        dqw = jnp.zeros((1, LANES), F32)
        dkw = jnp.zeros((1, LANES), F32)
        dsk = jnp.zeros((1, N_Q_HEADS), F32)
        for t in range(N_KV_HEADS // HEADS_PER_TILE):
            kt = kv[:, t * LANES:(t + 1) * LANES]
            vt = kv[:, KV_W + t * LANES:KV_W + (t + 1) * LANES]
            rk = _pair_rstd(kt, low2)
            kn = kt * rk * kwv
            dkn_t = jnp.zeros((2 * WINDOW, LANES), F32)
            dv_t = jnp.zeros((2 * WINDOW, LANES), F32)
            for hi in range(HEADS_PER_TILE):
                g = HEADS_PER_TILE * t + hi
                kdup = _dup_half(kn, hi, low2).astype(_MXU)
                vdup = _dup_half(vt, hi, low2).astype(_MXU)
                tiles = (2 * g, 2 * g + 1)
                qx, rq, stack, dstack, lse_rows = [], [], [], [], []
                for tq in tiles:
                    qt = q_ref[:, tq * LANES:(tq + 1) * LANES]
                    r = _pair_rstd(qt, low)
                    rq.append(r)
                    qx.append(qt * r)
                    stack += _split_heads(qx[-1] * qwv, low)
                    dstack += _split_heads(do_ref[:, tq * LANES:(tq + 1) * LANES], low)
                for r in range(Q_PER_KV):
                    lse_rows.append(jnp.sum(jnp.where(head_lane == Q_PER_KV * g + r, lse_blk, 0.0), axis=-1, keepdims=True))
                qs = jnp.concatenate(stack, axis=0).astype(_MXU)
                dos = jnp.concatenate(dstack, axis=0).astype(_MXU)
                lse_col = jnp.concatenate(lse_rows, axis=0)
                s = lax.dot_general(qs, kdup, _NT, preferred_element_type=F32) * scale
                s = jnp.where(valid, s, -1e30)
                p = jnp.exp(s - lse_col)
                dp = lax.dot_general(dos, vdup, _NT, preferred_element_type=F32)
                dsum = jnp.sum(p * dp, axis=-1, keepdims=True)
                ds = (p * (dp - dsum) * scale).astype(_MXU)
                dsink = -jnp.exp(_stacked_sinks(sink_ref, g) - lse_col) * dsum
                for r in range(Q_PER_KV):
                    dsk = dsk + jnp.where(sink_lane == Q_PER_KV * g + r, _colsum(dsink[r * WINDOW:(r + 1) * WINDOW]), 0.0)
                dv_g = _fold_halves(lax.dot_general(p.astype(_MXU), dos, _TN, preferred_element_type=F32))
                dkn_g = _fold_halves(lax.dot_general(ds, qs, _TN, preferred_element_type=F32))
                dv_t = jnp.where(low2, dv_t, dv_g) if hi else jnp.where(low2, dv_g, dv_t)
                dkn_t = jnp.where(low2, dkn_t, dkn_g) if hi else jnp.where(low2, dkn_g, dkn_t)
                dqn = jnp.dot(ds, kdup, preferred_element_type=F32)
                for i, tq in enumerate(tiles):
                    dqn_t = jnp.where(low, dqn[2 * i * WINDOW:(2 * i + 1) * WINDOW],
                                      dqn[(2 * i + 1) * WINDOW:(2 * i + 2) * WINDOW])
                    dq = rq[i] * (qwv * dqn_t - qx[i] * _pair_mean(dqn_t * qwv * qx[i], low))
                    d_ref[:, tq * LANES:(tq + 1) * LANES] = dq.astype(d_ref.dtype)
                    dqw = dqw + _colsum(dqn_t * qx[i])
            k_cols = slice(t * LANES, (t + 1) * LANES)
            v_cols = slice(KV_W + t * LANES, KV_W + (t + 1) * LANES)
            dkn_c = dkn_t[WINDOW:] + carry[:, k_cols]
            rc = rk[WINDOW:]
            kx = kt[WINDOW:] * rc
            dk = rc * (kwv * dkn_c - kx * _pair_mean(dkn_c * kwv * kx, low))
            d_ref[:, ATTN_W + t * LANES:ATTN_W + (t + 1) * LANES] = dk.astype(d_ref.dtype)
            d_ref[:, ATTN_W + KV_W + t * LANES:ATTN_W + KV_W + (t + 1) * LANES] = (
                dv_t[WINDOW:] + carry[:, v_cols]).astype(d_ref.dtype)
            carry[:, k_cols] = dkn_t[:WINDOW]
            carry[:, v_cols] = dv_t[:WINDOW]
            dkw = dkw + _colsum(dkn_c * kx)
        dqw_ref[...] += dqw
        dkw_ref[...] += dkw
        dsk_ref[...] += dsk

    return pl.pallas_call(
        body, name="attn_bwd", grid=(nblk,),
        in_specs=[pl.BlockSpec(memory_space=pltpu.SMEM), q_spec, kvc_spec, kvp_spec, w_spec, w_spec, l_spec, q_spec,
                  q_spec] + gate_specs + [_ANY] * nd,
        out_specs=[d_spec, w_spec, w_spec, s_spec],
        out_shape=[jax.ShapeDtypeStruct(d_proj.shape, d_proj.dtype), jax.ShapeDtypeStruct((1, LANES), F32),
                   jax.ShapeDtypeStruct((1, LANES), F32), jax.ShapeDtypeStruct((1, N_Q_HEADS), F32)],
        input_output_aliases={9 + ng + nd - 1: 0},
        scratch_shapes=[pltpu.VMEM((WINDOW, 2 * KV_W), F32), pltpu.VMEM((WINDOW, ATTN_W), F32)],
        compiler_params=_params(("arbitrary",)),
    )(sinks, proj, proj, proj, qw2, kw2, lse, attn, dya, *([proj] * ng), *deps)


def _ssm_discretise(a_re, a_im, log_dt):
    dt = jnp.exp(log_dt)
    mag = jnp.exp(dt * a_re)
    ab_re = mag * jnp.cos(dt * a_im)
    ab_im = mag * jnp.sin(dt * a_im)
    num_re = ab_re - 1.0
    num_im = ab_im
    den = a_re * a_re + a_im * a_im
    cf_re = (num_re * a_re + num_im * a_im) / den
    cf_im = (num_im * a_re - num_re * a_im) / den
    return ab_re, ab_im, cf_re, cf_im


def _ssm_params_fwd(a_re, a_im, log_dt):
    shp = jax.ShapeDtypeStruct(a_re.shape, F32)

    def body(are_ref, aim_ref, ldt_ref, abr_ref, abi_ref, cfr_ref, cfi_ref, alr_ref, ali_ref):
        abr, abi, cfr, cfi = _ssm_discretise(are_ref[...], aim_ref[...], ldt_ref[...])
        abr_ref[...], abi_ref[...], cfr_ref[...], cfi_ref[...] = abr, abi, cfr, cfi
        pr, pi = abr, abi
        for _ in range(int(math.log2(SSM_L))):
            pr, pi = pr * pr - pi * pi, 2.0 * pr * pi
        alr_ref[...], ali_ref[...] = pr, pi

    return pl.pallas_call(body, name="ssm_params_fwd", out_shape=[shp] * 6)(a_re, a_im, log_dt)


def _ssm_params_bwd(a_re, a_im, log_dt, d_abr, d_abi, d_cfr, d_cfi):
    def body(are_ref, aim_ref, ldt_ref, g0, g1, g2, g3, dare_ref, daim_ref, dldt_ref):
        _, vjp = jax.vjp(_ssm_discretise, are_ref[...], aim_ref[...], ldt_ref[...])
        dare_ref[...], daim_ref[...], dldt_ref[...] = vjp((g0[...], g1[...], g2[...], g3[...]))

    return pl.pallas_call(
        body, name="ssm_params_bwd",
        out_shape=[jax.ShapeDtypeStruct(a_re.shape, F32), jax.ShapeDtypeStruct(a_im.shape, F32),
                   jax.ShapeDtypeStruct(log_dt.shape, F32)],
    )(a_re, a_im, log_dt, d_abr, d_abi, d_cfr, d_cfi)


def _scan_cols(j):
    return pl.ds(j * SSM_SB, SSM_SB)


def _rows8(r):
    return pl.ds(pl.multiple_of(r * SUBLANES, SUBLANES), SUBLANES)


def _bcast8(row):
    return jnp.broadcast_to(row, (SUBLANES, row.shape[-1]))


def _token_order_pick():
    tok = lax.broadcasted_iota(jnp.int32, (SSM_T, SSM_T), 0)
    row = lax.broadcasted_iota(jnp.int32, (SSM_T, SSM_T), 1)
    return (row == SUBLANES * (tok % SSM_L) + tok // SSM_L).astype(_MXU)


SCAN_UNROLL = 8


def _scan_loop(n, step, init):
    def trip(o, carry):
        for i in range(SCAN_UNROLL):
            carry = step(o * SCAN_UNROLL + i, carry)
        return carry

    return lax.fori_loop(0, n // SCAN_UNROLL, trip, init)


def _ssm_fwd(u, b_re, b_im, c_re, c_im, d_skip, coef):
    seq = u.shape[0]
    nc = seq // SSM_T
    T, L = SSM_T, SSM_L

    def body(u_ref, bre_ref, bim_ref, cre_ref, cim_ref, d_ref, are_ref, aim_ref, cfr_ref, cfi_ref, alr_ref, ali_ref,
             y_ref, yg_ref, sre_ref, sim_ref, ire_ref, iim_ref, car_re, car_im, end_re, end_im, yg_scan):
        c = pl.program_id(0)

        @pl.when(c == 0)
        def _():
            car_re[...] = jnp.zeros_like(car_re)
            car_im[...] = jnp.zeros_like(car_im)

        for j in range(SSM_JB):
            ub = u_ref[:, j * LANES:(j + 1) * LANES].astype(_MXU)
            bur = jnp.dot(ub, bre_ref[j], preferred_element_type=F32)
            bui = jnp.dot(ub, bim_ref[j], preferred_element_type=F32)
            cfr, cfi = cfr_ref[:, _scan_cols(j)], cfi_ref[:, _scan_cols(j)]
            sre_ref[:, _scan_cols(j)] = cfr * bur - cfi * bui
            sim_ref[:, _scan_cols(j)] = cfr * bui + cfi * bur

        for j in range(SSM_JB):
            cols = _scan_cols(j)
            ar, ai = _bcast8(are_ref[:, cols]), _bcast8(aim_ref[:, cols])

            def step1(r, s, cols=cols, ar=ar, ai=ai):
                sr, si = s
                rows = _rows8(r)
                return (ar * sr - ai * si + sre_ref[rows, cols], ar * si + ai * sr + sim_ref[rows, cols])

            zero = jnp.zeros((SUBLANES, SSM_SB), F32)
            er, ei = _scan_loop(L, step1, (zero, zero))
            end_re[:, cols] = er
            end_im[:, cols] = ei

        alr, ali = alr_ref[...], ali_ref[...]
        cr, ci = car_re[...], car_im[...]
        ire_ref[0:1, :] = cr
        iim_ref[0:1, :] = ci
        for i in range(1, SUBLANES):
            er, ei = end_re[i - 1:i, :], end_im[i - 1:i, :]
            cr, ci = alr * cr - ali * ci + er, alr * ci + ali * cr + ei
            ire_ref[i:i + 1, :] = cr
            iim_ref[i:i + 1, :] = ci

        for j in range(SSM_JB):
            cols = _scan_cols(j)
            ar, ai = _bcast8(are_ref[:, cols]), _bcast8(aim_ref[:, cols])

            def step2(r, s, cols=cols, ar=ar, ai=ai):
                sr, si = s
                rows = _rows8(r)
                nr = ar * sr - ai * si + sre_ref[rows, cols]
                ni = ar * si + ai * sr + sim_ref[rows, cols]
                sre_ref[rows, cols] = nr
                sim_ref[rows, cols] = ni
                return nr, ni

            _scan_loop(L, step2, (ire_ref[:, cols], iim_ref[:, cols]))

        car_re[...] = sre_ref[T - 1:T, :]
        car_im[...] = sim_ref[T - 1:T, :]

        for j in range(SSM_JB):
            cols = _scan_cols(j)
            ch = slice(j * LANES, (j + 1) * LANES)
            y = (jnp.dot(sre_ref[:, cols].astype(_MXU), cre_ref[j], preferred_element_type=F32)
                 - jnp.dot(sim_ref[:, cols].astype(_MXU), cim_ref[j], preferred_element_type=F32))
            y = y + d_ref[:, ch] * u_ref[:, ch]
            y_ref[:, ch] = y
            yg_scan[:, ch] = jax.nn.gelu(y).astype(yg_scan.dtype)
        yg_ref[...] = jnp.dot(_token_order_pick(), yg_scan[...], preferred_element_type=F32).astype(yg_ref.dtype)

    tok = pl.BlockSpec((T, SSM_W), lambda c: (c, 0))
    st = pl.BlockSpec((T, N_STATES), lambda c: (c, 0))
    ini = pl.BlockSpec((None, SUBLANES, N_STATES), lambda c: (c, 0, 0))
    bsp = pl.BlockSpec((SSM_JB, LANES, SSM_SB), lambda c: (0, 0, 0))
    csp = pl.BlockSpec((SSM_JB, SSM_SB, LANES), lambda c: (0, 0, 0))
    row_w = pl.BlockSpec((1, SSM_W), lambda c: (0, 0))
    row_s = pl.BlockSpec((1, N_STATES), lambda c: (0, 0))
    return pl.pallas_call(
        body, name="ssm_fwd", grid=(nc,),
        in_specs=[tok, bsp, bsp, csp, csp, row_w] + [row_s] * 6,
        out_specs=[tok, tok, st, st, ini, ini],
        out_shape=[jax.ShapeDtypeStruct((seq, SSM_W), F32), jax.ShapeDtypeStruct((seq, SSM_W), _MXU),
                   jax.ShapeDtypeStruct((seq, N_STATES), F32), jax.ShapeDtypeStruct((seq, N_STATES), F32),
                   jax.ShapeDtypeStruct((nc, SUBLANES, N_STATES), F32),
                   jax.ShapeDtypeStruct((nc, SUBLANES, N_STATES), F32)],
        scratch_shapes=[pltpu.VMEM((1, N_STATES), F32), pltpu.VMEM((1, N_STATES), F32),
                        pltpu.VMEM((SUBLANES, N_STATES), F32), pltpu.VMEM((SUBLANES, N_STATES), F32),
                        pltpu.VMEM((T, SSM_W), _MXU)],
        compiler_params=_params(("arbitrary",)),
    )(u, b_re, b_im, c_re, c_im, d_skip, *coef)


def _ssm_bwd(dyg, y, u, s_re, s_im, i_re, i_im, b_re, b_im, c_re, c_im, d_skip, coef, d_proj, deps=()):
    seq = u.shape[0]
    nc = seq // SSM_T
    T, L = SSM_T, SSM_L
    deps = list(deps) + [d_proj]

    def body(dyg_ref, y_ref, u_ref, sre_ref, sim_ref, ire_ref, iim_ref, bre_ref, bim_ref, cre_ref, cim_ref, d_ref,
             are_ref, aim_ref, cfr_ref, cfi_ref, alr_ref, ali_ref, *rest):
        (du_ref, dbre_out, dbim_out, dcre_out, dcim_out, dd_ref, dar_ref, dai_ref, dcfr_ref, dcfi_ref,
         lre, lim, car_re, car_im, end_re, end_im, ini_re, ini_im, dbre_ref, dbim_ref, dcre_ref, dcim_ref,
         dy_ref, du_scan) = rest[len(deps):]
        step = pl.program_id(0)
        dy_ref[...] = jax.vjp(jax.nn.gelu, y_ref[...])[1](dyg_ref[...])[0]

        @pl.when(step == 0)
        def _():
            car_re[...] = jnp.zeros_like(car_re)
            car_im[...] = jnp.zeros_like(car_im)
            for ref in (dbre_ref, dbim_ref, dcre_ref, dcim_ref, dd_ref, dar_ref, dai_ref, dcfr_ref, dcfi_ref):
                ref[...] = jnp.zeros_like(ref)

        for j in range(SSM_JB):
            dyb = dy_ref[:, j * LANES:(j + 1) * LANES].astype(_MXU)
            lre[:, _scan_cols(j)] = lax.dot_general(dyb, cre_ref[j], _NT, preferred_element_type=F32)
            lim[:, _scan_cols(j)] = -lax.dot_general(dyb, cim_ref[j], _NT, preferred_element_type=F32)

        for j in range(SSM_JB):
            cols = _scan_cols(j)
            ar, ai = _bcast8(are_ref[:, cols]), _bcast8(aim_ref[:, cols])

            def step1(t, s, cols=cols, ar=ar, ai=ai):
                sr, si = s
                rows = _rows8(L - 1 - t)
                return (ar * sr + ai * si + lre[rows, cols], ar * si - ai * sr + lim[rows, cols])

            zero = jnp.zeros((SUBLANES, SSM_SB), F32)
            er, ei = _scan_loop(L, step1, (zero, zero))
            end_re[:, cols] = er
            end_im[:, cols] = ei

        alr, ali = alr_ref[...], ali_ref[...]
        cr, ci = car_re[...], car_im[...]
        ini_re[SUBLANES - 1:SUBLANES, :] = cr
        ini_im[SUBLANES - 1:SUBLANES, :] = ci
        for i in range(SUBLANES - 2, -1, -1):
            er, ei = end_re[i + 1:i + 2, :], end_im[i + 1:i + 2, :]
            cr, ci = alr * cr + ali * ci + er, alr * ci - ali * cr + ei
            ini_re[i:i + 1, :] = cr
            ini_im[i:i + 1, :] = ci

        for j in range(SSM_JB):
            cols = _scan_cols(j)
            ar, ai = _bcast8(are_ref[:, cols]), _bcast8(aim_ref[:, cols])

            def step2(t, s, cols=cols, ar=ar, ai=ai):
                sr, si = s
                rows = _rows8(L - 1 - t)
                nr = ar * sr + ai * si + lre[rows, cols]
                ni = ar * si - ai * sr + lim[rows, cols]
                lre[rows, cols] = nr
                lim[rows, cols] = ni
                return nr, ni

            _scan_loop(L, step2, (ini_re[:, cols], ini_im[:, cols]))

        car_re[...] = lre[0:1, :]
        car_im[...] = lim[0:1, :]

        head, tail, body_rows = slice(0, SUBLANES), slice(SUBLANES, T), slice(0, T - SUBLANES)
        for j in range(SSM_JB):
            cols = _scan_cols(j)
            ch = slice(j * LANES, (j + 1) * LANES)
            lr, li = lre[:, cols], lim[:, cols]
            lt_r, lt_i, sp_r, sp_i = lre[tail, cols], lim[tail, cols], sre_ref[body_rows, cols], sim_ref[body_rows, cols]
            lh_r, lh_i, si_r, si_i = lre[head, cols], lim[head, cols], ire_ref[:, cols], iim_ref[:, cols]
            dar_ref[:, cols] += _colsum(lt_r * sp_r + lt_i * sp_i) + _colsum(lh_r * si_r + lh_i * si_i)
            dai_ref[:, cols] += _colsum(lt_i * sp_r - lt_r * sp_i) + _colsum(lh_i * si_r - lh_r * si_i)
            uf = u_ref[:, ch]
            ub = uf.astype(_MXU)
            bur = jnp.dot(ub, bre_ref[j], preferred_element_type=F32)
            bui = jnp.dot(ub, bim_ref[j], preferred_element_type=F32)
            dcfr_ref[:, cols] += _colsum(lr * bur + li * bui)
            dcfi_ref[:, cols] += _colsum(li * bur - lr * bui)
            cfr, cfi = cfr_ref[:, cols], cfi_ref[:, cols]
            dbur = (cfr * lr + cfi * li).astype(_MXU)
            dbui = (cfr * li - cfi * lr).astype(_MXU)
            dyf = dy_ref[:, ch]
            dyb = dyf.astype(_MXU)
            du = (lax.dot_general(dbur, bre_ref[j], _NT, preferred_element_type=F32)
                  + lax.dot_general(dbui, bim_ref[j], _NT, preferred_element_type=F32) + d_ref[:, ch] * dyf)
            du_scan[:, ch] = du.astype(du_scan.dtype)
            dbre_ref[j] += lax.dot_general(ub, dbur, _TN, preferred_element_type=F32)
            dbim_ref[j] += lax.dot_general(ub, dbui, _TN, preferred_element_type=F32)
            dcre_ref[j] += lax.dot_general(sre_ref[:, cols].astype(_MXU), dyb, _TN, preferred_element_type=F32)
            dcim_ref[j] -= lax.dot_general(sim_ref[:, cols].astype(_MXU), dyb, _TN, preferred_element_type=F32)
            dd_ref[:, ch] += _colsum(dyf * uf)
        du_ref[...] = jnp.dot(_token_order_pick(), du_scan[...], preferred_element_type=F32).astype(du_ref.dtype)

        @pl.when(step == nc - 1)
        def _():
            for acc, out in ((dbre_ref, dbre_out), (dbim_ref, dbim_out), (dcre_ref, dcre_out), (dcim_ref, dcim_out)):
                pltpu.sync_copy(acc, out)

    tok = pl.BlockSpec((T, SSM_W), lambda c: (nc - 1 - c, 0))
    st = pl.BlockSpec((T, N_STATES), lambda c: (nc - 1 - c, 0))
    ini = pl.BlockSpec((None, SUBLANES, N_STATES), lambda c: (nc - 1 - c, 0, 0))
    bsp = pl.BlockSpec((SSM_JB, LANES, SSM_SB), lambda c: (0, 0, 0))
    csp = pl.BlockSpec((SSM_JB, SSM_SB, LANES), lambda c: (0, 0, 0))
    row_w = pl.BlockSpec((1, SSM_W), lambda c: (0, 0))
    row_s = pl.BlockSpec((1, N_STATES), lambda c: (0, 0))
    big = pltpu.VMEM((T, N_STATES), F32)
    one = pltpu.VMEM((1, N_STATES), F32)
    eight = pltpu.VMEM((SUBLANES, N_STATES), F32)
    return pl.pallas_call(
        body, name="ssm_bwd", grid=(nc,),
        in_specs=[tok, tok, tok, st, st, ini, ini, bsp, bsp, csp, csp, row_w] + [row_s] * 6 + [_ANY] * len(deps),
        out_specs=[pl.BlockSpec((pl.Element(T), pl.Element(SSM_W)), lambda c: ((nc - 1 - c) * T, OFF_U * CW)),
                   _ANY, _ANY, _ANY, _ANY, row_w, row_s, row_s, row_s, row_s],
        input_output_aliases={18 + len(deps) - 1: 0},
        out_shape=[jax.ShapeDtypeStruct(d_proj.shape, d_proj.dtype),
                   jax.ShapeDtypeStruct((SSM_JB, LANES, SSM_SB), F32), jax.ShapeDtypeStruct((SSM_JB, LANES, SSM_SB), F32),
                   jax.ShapeDtypeStruct((SSM_JB, SSM_SB, LANES), F32), jax.ShapeDtypeStruct((SSM_JB, SSM_SB, LANES), F32),
                   jax.ShapeDtypeStruct((1, SSM_W), F32)] + [jax.ShapeDtypeStruct((1, N_STATES), F32)] * 4,
        scratch_shapes=[big, big, one, one, eight, eight, eight, eight,
                        pltpu.VMEM((SSM_JB, LANES, SSM_SB), F32), pltpu.VMEM((SSM_JB, LANES, SSM_SB), F32),
                        pltpu.VMEM((SSM_JB, SSM_SB, LANES), F32), pltpu.VMEM((SSM_JB, SSM_SB, LANES), F32),
                        pltpu.VMEM((T, SSM_W), F32), pltpu.VMEM((T, SSM_W), _MXU)],
        compiler_params=_params(("arbitrary",)),
    )(dyg, y, u, s_re, s_im, i_re, i_im, b_re, b_im, c_re, c_im, d_skip, *coef, *deps)


def _block_diag_b(b):
    t = b.reshape(SSM_JB, 8, STATE, GROUP).transpose(0, 1, 3, 2)
    eye = jnp.eye(8, dtype=b.dtype)
    return (t[:, :, :, None, :] * eye[None, :, None, :, None]).reshape(SSM_JB, LANES, SSM_SB)


def _block_diag_c(c):
    t = c.reshape(SSM_JB, 8, GROUP, STATE).transpose(0, 1, 3, 2)
    eye = jnp.eye(8, dtype=c.dtype)
    return (t[:, :, :, None, :] * eye[None, :, None, :, None]).reshape(SSM_JB, SSM_SB, LANES)


def _diag_of_b(blk):
    t = blk.reshape(SSM_JB, 8, GROUP, 8, STATE)
    d = jnp.sum(t * jnp.eye(8, dtype=blk.dtype)[None, :, None, :, None], axis=3)
    return d.transpose(0, 1, 3, 2).reshape(N_GROUPS, STATE, GROUP)


def _diag_of_c(blk):
    t = blk.reshape(SSM_JB, 8, STATE, 8, GROUP)
    d = jnp.sum(t * jnp.eye(8, dtype=blk.dtype)[None, :, None, :, None], axis=3)
    return d.transpose(0, 1, 3, 2).reshape(N_GROUPS, GROUP, STATE)


def _to_scan_order(v):
    seq, w = v.shape
    return v.reshape(seq // SSM_T, SUBLANES, SSM_L, w).transpose(0, 2, 1, 3).reshape(seq, w)


def _adamw_math(w, g, m, v):
    nm = ADAM_B1 * m + (1.0 - ADAM_B1) * g
    nv = ADAM_B2 * v + (1.0 - ADAM_B2) * jnp.square(g)
    m_hat = nm / (1.0 - ADAM_B1 ** ADAM_STEP)
    v_hat = nv / (1.0 - ADAM_B2 ** ADAM_STEP)
    return -ADAM_LR * (m_hat / (jnp.sqrt(v_hat) + ADAM_EPS) + ADAM_WD * w), nm, nv


def _adamw(w, g, m, v, *, name, tm, deps=()):
    rows, cols = w.shape
    nd = len(deps)

    def body(w_ref, g_ref, m_ref, v_ref, *rest):
        d_ref, nm_ref, nv_ref = rest[nd:]
        d_ref[...], nm_ref[...], nv_ref[...] = _adamw_math(w_ref[...], g_ref[...], m_ref[...], v_ref[...])

    spec = pl.BlockSpec((tm, cols), lambda i: (i, 0))
    shp = jax.ShapeDtypeStruct((rows, cols), F32)
    return pl.pallas_call(body, name=name, grid=(rows // tm,), in_specs=[spec] * 4 + [_ANY] * nd,
                          out_specs=[spec] * 3, out_shape=[shp] * 3,
                          compiler_params=_params(("arbitrary",)))(w, g, m, v, *deps)


def _place():
    x, y, c = lax.axis_index("x"), lax.axis_index("y"), lax.axis_index("c")
    chips = [(1 - x, y), (x, 1 - y), (1 - x, 1 - y)]
    return x, y, c, chips


def _remote(src, dst, send_sem, recv_sem, dev):
    return pltpu.make_async_remote_copy(src_ref=src, dst_ref=dst, send_sem=send_sem, recv_sem=recv_sem,
                                        device_id=dev, device_id_type=MESH)


def _place_shard(w, mine_arr, *, name, tm=256, deps=()):
    rows, cols = w.shape

    def body(m_ref, w_ref, *rest):
        rest[-1][...] = w_ref[...].astype(rest[-1].dtype)

    return pl.pallas_call(
        body, name=name,
        grid_spec=pltpu.PrefetchScalarGridSpec(
            num_scalar_prefetch=1, grid=(rows // tm,),
            in_specs=[pl.BlockSpec((tm, cols), lambda i, m: (i, 0))] + [_ANY] * len(deps),
            out_specs=pl.BlockSpec((None, tm, cols), lambda i, m: (m[0], i, 0))),
        out_shape=jax.ShapeDtypeStruct((N_CHIPS, rows, cols), _WIRE),
        compiler_params=_params(("arbitrary",)),
    )(mine_arr, w, *deps)


_HBM = pl.BlockSpec(memory_space=pltpu.HBM)
_SEM = pl.BlockSpec(memory_space=pltpu.SEMAPHORE)
_EFFECT = pltpu.SideEffectType.DATAFLOW_SIDE_EFFECTING


def _copies_start(name, bufs, plan, count, after=()):
    nb, na = len(bufs), len(after)

    def body(*refs):
        send_sems, recv_sems, token = refs[nb + na], refs[nb + na + 1], refs[-1]
        copies = plan(refs[:nb])
        assert len(copies) == count
        for i, (src, dst, dev, _) in enumerate(copies):
            _remote(src, dst, send_sems.at[i], recv_sems.at[i], dev).start()
        token[...] = jnp.zeros_like(token)

    res = pl.pallas_call(
        body, name=name, in_specs=[_HBM] * nb + [_ANY] * na,
        out_specs=(_SEM, _SEM, *[_HBM] * nb, pl.BlockSpec(memory_space=pltpu.VMEM)),
        out_shape=(pltpu.SemaphoreType.DMA((count,)), pltpu.SemaphoreType.DMA((count,)),
                   *[pltpu.HBM(b.shape, b.dtype) for b in bufs], jax.ShapeDtypeStruct((SUBLANES, LANES), F32)),
        input_output_aliases={i: 2 + i for i in range(nb)},
        compiler_params=pltpu.CompilerParams(has_side_effects=_EFFECT),
    )(*[pltpu.with_memory_space_constraint(b, pltpu.HBM) for b in bufs], *after)
    return (res[0], res[1]), list(res[2:2 + nb]), res[-1]


def _copies_wait(name, bufs, sems, plan, after=(), which=None):
    nb, na = len(bufs), len(after)

    def body(*refs):
        send_sems, recv_sems = refs[nb], refs[nb + 1]
        for i, (src, _, dev, land) in enumerate(plan(refs[:nb])):
            if which is not None and i not in which:
                continue
            cp = _remote(src, land, send_sems.at[i], recv_sems.at[i], dev)
            cp.wait_send()
            cp.wait_recv()

    res = pl.pallas_call(
        body, name=name, in_specs=[_HBM] * nb + [_SEM, _SEM] + [_ANY] * na, out_specs=[_HBM] * nb,
        out_shape=[pltpu.HBM(b.shape, b.dtype) for b in bufs],
        input_output_aliases={i: i for i in range(nb)},
        compiler_params=pltpu.CompilerParams(has_side_effects=_EFFECT),
    )(*bufs, *sems, *after)
    return list(res)


def _plan_gather_ici(fulls, which=(0, 1, 2)):
    x, y, c, chips = _place()
    copies = []
    for f in fulls:
        half = pl.ds(c * (f.shape[1] // 2), f.shape[1] // 2)
        own = f.at[2 * x + y, half]
        for chip in [chips[k] for k in which]:
            copies.append((own, own, (*chip, c), f.at[2 * chip[0] + chip[1], half]))
    return copies


def _plan_gather_d2d(fulls, which=(0, 1, 2)):
    x, y, c, chips = _place()
    copies = []
    for f in fulls:
        r2 = f.shape[1] // 2
        for chip in [chips[k] for k in which]:
            blk = 2 * chip[0] + chip[1]
            landed = f.at[blk, pl.ds(c * r2, r2)]
            copies.append((landed, landed, (x, y, 1 - c), f.at[blk, pl.ds((1 - c) * r2, r2)]))
    return copies


def _plan_relay_direct(fulls):
    (f,) = fulls
    x, y, c, chips = _place()
    half = pl.ds(c * (f.shape[1] // 2), f.shape[1] // 2)
    own = f.at[2 * x + y, half]
    return [(own, own, (*chip, c), f.at[2 * chip[0] + chip[1], half]) for chip in chips[:2]]


def _plan_relay_forward(fulls, k):
    (f,) = fulls
    x, y, c, chips = _place()
    r2 = f.shape[1] // 2
    half, other = pl.ds(c * r2, r2), pl.ds((1 - c) * r2, r2)
    quarter = pl.ds(c * r2 + k * (r2 // 2), r2 // 2)
    blk, far = 2 * chips[k][0] + chips[k][1], 2 * chips[2][0] + chips[2][1]
    passed, landed = f.at[blk, quarter], f.at[blk, half]
    return [(passed, passed, (*chips[1 - k], c), f.at[far, quarter]), (landed, landed, (x, y, 1 - c), f.at[blk, other])]


def _plan_relay_last(fulls):
    (f,) = fulls
    x, y, c, chips = _place()
    r2 = f.shape[1] // 2
    far = 2 * chips[2][0] + chips[2][1]
    landed = f.at[far, pl.ds(c * r2, r2)]
    return [(landed, landed, (x, y, 1 - c), f.at[far, pl.ds((1 - c) * r2, r2)])]


def _plan_swap_halves(refs):
    x, y, c, _ = _place()
    n = len(refs) // 2
    copies = []
    for g, land in zip(refs[:n], refs[n:]):
        r2 = g.shape[1] // 2
        copies.append((g.at[:, pl.ds((1 - c) * r2, r2), :], land, (x, y, 1 - c), land))
    return copies


def _plan_scatter_chips(refs):
    x, y, c, chips = _place()
    n = len(refs) // 2
    copies = []
    for h, land in zip(refs[:n], refs[n:]):
        for k, chip in enumerate(chips):
            copies.append((h.at[2 * chip[0] + chip[1]], land.at[k], (*chip, c), land.at[k]))
    return copies


def _plan_join_halves(totals):
    x, y, c, _ = _place()
    copies = []
    for t in totals:
        r2 = t.shape[0] // 2
        mine = t.at[pl.ds(c * r2, r2)]
        copies.append((mine, mine, (x, y, 1 - c), t.at[pl.ds((1 - c) * r2, r2)]))
    return copies


def _add_sibling_half(g, got, c_arr, *, name, tm):
    _, rows, cols = g.shape
    r2 = rows // 2
    nb = r2 // tm

    def body(c_ref, g_ref, r_ref, o_ref):
        o_ref[...] = (g_ref[...].astype(F32) + r_ref[...].astype(F32)).astype(o_ref.dtype)

    return pl.pallas_call(
        body, name=name,
        grid_spec=pltpu.PrefetchScalarGridSpec(
            num_scalar_prefetch=1, grid=(N_CHIPS, nb),
            in_specs=[pl.BlockSpec((None, tm, cols), lambda b, i, c: (b, c[0] * nb + i, 0)),
                      pl.BlockSpec((None, tm, cols), lambda b, i, c: (b, i, 0))],
            out_specs=pl.BlockSpec((None, tm, cols), lambda b, i, c: (b, i, 0))),
        out_shape=jax.ShapeDtypeStruct((N_CHIPS, r2, cols), _WIRE),
        compiler_params=_params(("arbitrary", "arbitrary")),
    )(c_arr, g, got)


def _add_chips(h, got, place_arr, *, name, tm):
    _, r2, cols = h.shape
    nb = r2 // tm

    def body(p_ref, h_ref, r_ref, o_ref):
        o_ref[...] = ((h_ref[...].astype(F32) + r_ref[0].astype(F32)) + r_ref[1].astype(F32)) + r_ref[2].astype(F32)

    return pl.pallas_call(
        body, name=name,
        grid_spec=pltpu.PrefetchScalarGridSpec(
            num_scalar_prefetch=1, grid=(nb,),
            in_specs=[pl.BlockSpec((None, tm, cols), lambda i, p: (p[0], i, 0)),
                      pl.BlockSpec((3, tm, cols), lambda i, p: (0, i, 0))],
            out_specs=pl.BlockSpec((tm, cols), lambda i, p: (p[1] * nb + i, 0))),
        out_shape=jax.ShapeDtypeStruct((2 * r2, cols), F32),
        compiler_params=_params(("arbitrary",)),
    )(place_arr, h, got)


class _ReduceScatter:
    def __init__(self, tag, names, grads):
        self.tag, self.names, self.n = tag, names, len(names)
        core = lax.axis_index("c").astype(jnp.int32)
        chip = (2 * lax.axis_index("x") + lax.axis_index("y")).astype(jnp.int32)
        self.c_arr, self.place_arr = core.reshape(1), jnp.stack([chip, core])
        self.bufs = list(grads)

    def _start(self, step, bufs, plan, count, after):
        self.plan = plan
        self.step = f"grad_{step}_{self.tag}"
        self.sems, self.bufs, token = _copies_start(self.step + "_start", bufs, plan, count, after)
        return [token]

    def _wait(self, after):
        self.bufs = _copies_wait(self.step + "_wait", self.bufs, self.sems, self.plan, after)
        return self.bufs

    def start_swap(self, after=()):
        lands = [lax.empty((N_CHIPS, g.shape[1] // 2, g.shape[2]), g.dtype) for g in self.bufs]
        return self._start("swap", self.bufs + lands, _plan_swap_halves, self.n, after)

    def start_scatter(self, after):
        bufs = self._wait(after)
        pair = [_add_sibling_half(g, r, self.c_arr, name=f"grad_add_sibling_{nm}", tm=min(256, g.shape[1] // 2))
                for nm, g, r in zip(self.names, bufs[:self.n], bufs[self.n:])]
        lands = [lax.empty((3,) + h.shape[1:], h.dtype) for h in pair]
        return self._start("scatter", pair + lands, _plan_scatter_chips, 3 * self.n, ())

    def start_join(self, after):
        bufs = self._wait(after)
        total = [_add_chips(h, r, self.place_arr, name=f"grad_add_chips_{nm}", tm=min(256, h.shape[1]))
                 for nm, h, r in zip(self.names, bufs[:self.n], bufs[self.n:])]
        return self._start("join", total, _plan_join_halves, self.n, ())

    def finish(self, after):
        return dict(zip(self.names, self._wait(after)))


def _all_gather_small(v):
    m_per, n = v.shape

    def body(x_ref, out_ref, send_sems, recv_sems, local_sem):
        x, y, c, chips = _place()
        me, sibling = (x, y, c), (x, y, 1 - c)

        def rows(px, py, pc):
            return out_ref.at[4 * px + 2 * py + pc]

        def copy(k, block, to, src=None):
            return _remote(rows(*block) if src is None else src, rows(*block), send_sems.at[k], recv_sems.at[k], to)

        mine = pltpu.make_async_copy(x_ref, rows(*me), local_sem)
        mine.start()
        first = [copy(0, me, sibling, src=x_ref)]
        first += [copy(1 + j, me, (*chip, c), src=x_ref) for j, chip in enumerate(chips)]
        for cp in first:
            cp.start()
        passed = [copy(4 + j, (*chip, c), sibling) for j, chip in enumerate(chips)]
        for j, chip in enumerate(chips):
            copy(1 + j, (*chip, c), me).wait_recv()
            passed[j].start()
        copy(0, sibling, me).wait_recv()
        for j, chip in enumerate(chips):
            copy(4 + j, (*chip, 1 - c), me).wait_recv()
        for cp in first + passed:
            cp.wait_send()
        mine.wait()

    return pl.pallas_call(
        body, name="gather_small_grads",
        out_shape=jax.ShapeDtypeStruct((8, m_per, n), v.dtype),
        in_specs=[pl.BlockSpec(memory_space=pltpu.VMEM)], out_specs=pl.BlockSpec(memory_space=pltpu.VMEM),
        scratch_shapes=[pltpu.SemaphoreType.DMA((7,)), pltpu.SemaphoreType.DMA((7,)), pltpu.SemaphoreType.DMA],
        compiler_params=pltpu.CompilerParams(vmem_limit_bytes=VMEM_LIMIT),
    )(v)


def _sum8(v, *, name):
    _, m, n = v.shape

    def body(v_ref, o_ref):
        acc = v_ref[0]
        for d in range(1, 8):
            acc = acc + v_ref[d]
        o_ref[...] = acc

    return pl.pallas_call(body, name=name, out_shape=jax.ShapeDtypeStruct((m, n), F32),
                          compiler_params=pltpu.CompilerParams(vmem_limit_bytes=VMEM_LIMIT))(v)


def _local_step(x, target, norm_w, q_norm_w, k_norm_w, sinks, a_re, a_im, log_dt, b_re, b_im, c_re, c_im, d_skip,
                b_glu, io):
    seq = x.shape[0]
    qw2 = jnp.tile(q_norm_w.reshape(1, HEAD_DIM), (1, HEADS_PER_TILE))
    kw2 = jnp.tile(k_norm_w.reshape(1, HEAD_DIM), (1, HEADS_PER_TILE))
    nw, bg = norm_w.reshape(1, D_MODEL), b_glu.reshape(1, D_MODEL)
    dsk = d_skip.reshape(1, SSM_W)

    h, rstd = _rms_fwd(x, nw, deps=io.begin())
    proj, w_in4 = io.projection(h)
    attn, lse, ya_in = _attn2_fwd(proj, qw2, kw2, sinks, deps=io.after_proj(proj))
    w_ap4 = io.weight("w_attn_proj", ya_in)
    w_glu4, w_sp4, w_out = io.weight("w_glu", ya_in), io.weight("w_ssm_proj", ya_in), io.weight("w_out", ya_in)
    y_a = _mm(ya_in, w_ap4, mode="nn", name="mm_attn_proj", tm=2048, tn=512, tk=ATTN_W, b_blocked=True,
              rows_outer=True)

    flat_a = (a_re.reshape(1, N_STATES), a_im.reshape(1, N_STATES), jnp.repeat(log_dt, STATE).reshape(1, N_STATES))
    coef = _ssm_params_fwd(*flat_a)
    bre_blk, bim_blk = _block_diag_b(b_re).astype(_MXU), _block_diag_b(b_im).astype(_MXU)
    cre_blk, cim_blk = _block_diag_c(c_re).astype(_MXU), _block_diag_c(c_im).astype(_MXU)
    u_scan = _to_scan_order(proj[:, OFF_U * CW:OFF_U * CW + SSM_W])
    y_scan, yg, s_re, s_im, i_re, i_im = _ssm_fwd(u_scan, bre_blk, bim_blk, cre_blk, cim_blk, dsk, coef)
    glu, ys_in = _mm_glu_gate(yg, w_glu4, bg, proj)
    y_s = _mm(ys_in, w_sp4, mode="nn", name="mm_ssm_proj", tm=2048, tn=512, tk=SSM_W, b_blocked=True,
              rows_outer=True)

    merged, dout, dout_b, sq = _mm_merge_out_loss(proj, y_a, y_s, w_out, x, target)
    loss = 0.5 * jnp.sum(sq) / D_MODEL

    d_ya, d_ys, d_proj = _mm_merge_bwd(dout_b, w_out, proj, y_a, y_s)
    g_w_out = _mm(merged, dout_b, mode="tn", name="mm_g_w_out", tm=1024, tn=D_MODEL, tk=1024, out_dtype=_WIRE)

    d_ya_in = _mm(d_ya, w_ap4, mode="nt", name="mm_d_attn_gate", tm=2048, tn=ATTN_W, tk=512, b_blocked=True)
    g_w_ap = _mm(ya_in, d_ya, mode="tn", name="mm_g_w_attn_proj", tm=ATTN_W, tn=D_MODEL, tk=2048, out_dtype=_WIRE,
                 out_blocked=True)

    g_w_sp = _mm(ys_in, d_ys, mode="tn", name="mm_g_w_ssm_proj", tm=SSM_W, tn=D_MODEL, tk=2048, out_dtype=_WIRE,
                 out_blocked=True)
    d_glu, d_proj, g_bglu = _mm_ssm_gate_bwd(d_ys, w_sp4, glu, bg, proj, d_proj)
    d_yg = _mm(d_glu, w_glu4, mode="nt", name="mm_d_gelu", tm=2048, tn=SSM_W, tk=512, b_blocked=True)
    g_w_glu = _mm(yg, d_glu, mode="tn", name="mm_g_w_glu", tm=SSM_W, tn=D_MODEL, tk=2048, out_dtype=_WIRE, out_blocked=True)
    dep = io.later_grads(dict(w_attn_proj=g_w_ap, w_glu=g_w_glu, w_ssm_proj=g_w_sp,
                              w_out=g_w_out.reshape(N_CHIPS, D_MODEL // N_CHIPS, D_MODEL)))

    d_proj, g_qw2, g_kw2, g_sk = _attn2_bwd(proj, qw2, kw2, sinks, lse, attn, d_ya_in, d_proj, deps=dep)
    dep = io.before_scan_backward([d_proj])
    (d_proj, g_bre, g_bim, g_cre, g_cim, g_dsk, g_abr, g_abi, g_cfr, g_cfi) = _ssm_bwd(
        _to_scan_order(d_yg), y_scan, u_scan, s_re, s_im, i_re, i_im, bre_blk, bim_blk, cre_blk, cim_blk, dsk, coef,
        d_proj, deps=dep)
    g_are, g_aim, g_ldt = _ssm_params_bwd(*flat_a, g_abr, g_abi, g_cfr, g_cfi)
    g_are, g_aim = g_are.reshape(N_GROUPS, STATE), g_aim.reshape(N_GROUPS, STATE)
    g_ldt = g_ldt.reshape(N_GROUPS, STATE).sum(axis=1)
    dep = io.before_input_projection_grad([d_proj]) + io.small_grads(dict(
        q_norm_w=g_qw2[0, :HEAD_DIM] + g_qw2[0, HEAD_DIM:], k_norm_w=g_kw2[0, :HEAD_DIM] + g_kw2[0, HEAD_DIM:],
        sinks=g_sk.reshape(N_Q_HEADS), A_re=g_are, A_im=g_aim, log_dt=g_ldt,
        B_re=_diag_of_b(g_bre), B_im=_diag_of_b(g_bim), C_re=_diag_of_c(g_cre), C_im=_diag_of_c(g_cim),
        D_skip=g_dsk.reshape(N_GROUPS, GROUP), b_glu=g_bglu.reshape(D_MODEL)))
    g_w_in = _mm(h, d_proj, mode="tn", name="mm_g_w_in", tm=1024, tn=IN_W // 4, tk=1024, out_dtype=_WIRE,
                 out_blocked=True, deps=dep)
    dep = io.input_projection_grad(g_w_in)
    d_h = _mm(d_proj, w_in4, mode="nt", name="mm_d_h", tm=1024, tn=D_MODEL, tk=IN_W // 4, b_blocked=True, deps=dep)
    grad_x, g_nw = _rms_bwd(d_h, x, rstd, nw, dout)
    return loss, grad_x, g_nw.reshape(D_MODEL)


_SMALL = ["norm_w", "q_norm_w", "k_norm_w", "sinks", "A_re", "A_im", "log_dt", "B_re", "B_im", "C_re", "C_im",
          "D_skip", "b_glu"]
_BIG = ["w_in", "w_attn_proj", "w_glu", "w_ssm_proj", "w_out"]
_LATER = _BIG[1:]
_RELATIONS = ("flip_x", "flip_y", "flip_xy")
_ORDER = ["norm_w", "w_in", "q_norm_w", "k_norm_w", "sinks", "w_attn_proj", "A_re", "A_im", "log_dt", "B_re", "B_im",
          "C_re", "C_im", "D_skip", "w_glu", "b_glu", "w_ssm_proj", "w_out"]
_PACK_W = 1024


def _packed_rows(size):
    unit = SUBLANES * _PACK_W
    return -(-size // unit) * SUBLANES


def _pack_small(d, names):
    parts = []
    for n in names:
        flat = d[n].reshape(-1).astype(F32)
        rows = _packed_rows(flat.shape[0])
        parts.append(jnp.pad(flat, (0, rows * _PACK_W - flat.shape[0])).reshape(rows, _PACK_W))
    return jnp.concatenate(parts, axis=0)


def _unpack_small(packed, like, names):
    out, pos = {}, 0
    for n in names:
        rows = _packed_rows(like[n].size)
        out[n] = packed[pos:pos + rows].reshape(-1)[:like[n].size].reshape(like[n].shape)
        pos += rows
    return out


def _place_block(v, index_arr, *, name):
    rows, cols = v.shape

    def body(i_ref, v_ref, o_ref):
        o_ref[...] = v_ref[...]

    return pl.pallas_call(
        body, name=name,
        grid_spec=pltpu.PrefetchScalarGridSpec(
            num_scalar_prefetch=1, grid=(1,),
            in_specs=[pl.BlockSpec((rows, cols), lambda i, d: (0, 0))],
            out_specs=pl.BlockSpec((None, rows, cols), lambda i, d: (d[0], 0, 0))),
        out_shape=jax.ShapeDtypeStruct((8, rows, cols), v.dtype),
        compiler_params=_params(("arbitrary",)),
    )(index_arr, v)


def _plan_all_to_all(refs):
    (land,) = refs
    x, y, c, _ = _place()
    own = land.at[4 * x + 2 * y + c]
    copies = []
    for fx, fy, fc in [(0, 0, 1), (0, 1, 0), (0, 1, 1), (1, 0, 0), (1, 0, 1), (1, 1, 0), (1, 1, 1)]:
        px, py, pc = (1 - x) if fx else x, (1 - y) if fy else y, (1 - c) if fc else c
        copies.append((own, own, (px, py, pc), land.at[4 * px + 2 * py + pc]))
    return copies


def _adamw_whole(w, g, m, v, *, name):
    def body(w_ref, g_ref, m_ref, v_ref, d_ref, nm_ref, nv_ref):
        d_ref[...], nm_ref[...], nv_ref[...] = _adamw_math(w_ref[...], g_ref[...], m_ref[...], v_ref[...])

    return pl.pallas_call(body, name=name, out_shape=[jax.ShapeDtypeStruct(w.shape, F32)] * 3)(w, g, m, v)


class _Exchanges:
    def __init__(self, w, m, v):
        self.w, self.m, self.v = w, m, v
        self.grads, self.delta, self.new_m, self.new_v = {}, {}, {}, {}

    def _adamw(self, names, deps):
        for n in names:
            self.delta[n], self.new_m[n], self.new_v[n] = _adamw(
                self.w[n], self.grads[n], self.m[n], self.v[n], name=f"adamw_{n}", tm=128, deps=deps)

    def begin(self):
        chip = (2 * lax.axis_index("x") + lax.axis_index("y")).astype(jnp.int32).reshape(1)
        w_in = _place_shard(self.w["w_in"], chip, name="place_w_in")
        self.w_in_sems, self.w_in_buf, token = _copies_start("gather_w_in_direct_start", [w_in], _plan_relay_direct, 2)
        self.later_full = [_place_shard(self.w[n], chip, name=f"place_{n}", deps=[token]) for n in _LATER]
        return self.later_full

    def projection(self, h):
        x, y = lax.axis_index("x"), lax.axis_index("y")
        blks = [jnp.asarray(b, jnp.int32).reshape(1)
                for b in (2 * x + y, 2 * (1 - x) + y, 2 * x + (1 - y), 2 * (1 - x) + (1 - y))]
        bufs = self.w_in_buf
        proj = _mm_chip_block(h, bufs[0], blks[0], None, name="mm_proj_own")
        relay, token = [], proj
        for k, tag in enumerate(_RELATIONS[:2]):
            bufs = _copies_wait(f"gather_w_in_direct_{tag}_wait", bufs, self.w_in_sems, _plan_relay_direct, [token],
                                which=(k,))
            plan = functools.partial(_plan_relay_forward, k=k)
            sems, bufs, token = _copies_start(f"gather_w_in_relay_{tag}_start", bufs, plan, 2)
            relay.append((sems, plan))
        self.rest = _copies_start("gather_ici_rest_start", self.later_full, _plan_gather_ici, 3 * len(_LATER),
                                  after=[token])
        token = self.rest[2]
        for k, tag in enumerate(_RELATIONS[:2]):
            bufs = _copies_wait(f"gather_w_in_hand_{tag}_wait", bufs, relay[k][0], relay[k][1], [token], which=(1,))
            token = proj = _mm_chip_block(h, bufs[0], blks[1 + k], proj, name=f"mm_proj_{tag}")
        for k, tag in enumerate(_RELATIONS[:2]):
            bufs = _copies_wait(f"gather_w_in_relay_{tag}_wait", bufs, relay[k][0], relay[k][1], [token], which=(0,))
        sems, bufs, token = _copies_start("gather_w_in_last_start", bufs, _plan_relay_last, 1)
        bufs = _copies_wait("gather_w_in_last_wait", bufs, sems, _plan_relay_last, [token])
        proj = _mm_chip_block(h, bufs[0], blks[3], proj, name="mm_proj_flip_xy")
        return proj, bufs[0]

    def weight(self, name, after):
        if self.rest is not None:
            sems, bufs = self.rest
            later = dict(zip(_LATER, _copies_wait("gather_d2d_rest_wait", bufs, sems, _plan_gather_d2d, [after])))
            later["w_out"] = later["w_out"].reshape(D_MODEL, D_MODEL)
            self.later, self.rest = later, None
        return self.later[name]

    def after_proj(self, proj):
        sems, bufs, _ = self.rest
        bufs = _copies_wait("gather_ici_rest_wait", bufs, sems, _plan_gather_ici, [proj])
        sems, bufs, token = _copies_start("gather_d2d_rest_start", bufs, _plan_gather_d2d, 3 * len(_LATER))
        self.rest = (sems, bufs)
        return [token]

    def later_grads(self, grads):
        self.rs_later = _ReduceScatter("later", _LATER, [grads[n] for n in _LATER])
        return self.rs_later.start_swap()

    def before_scan_backward(self, after):
        return self.rs_later.start_scatter(after)

    def before_input_projection_grad(self, after):
        return self.rs_later.start_join(after)

    def input_projection_grad(self, g_w_in):
        self.grads.update(self.rs_later.finish([g_w_in]))
        self.rs_in = _ReduceScatter("w_in", ["w_in"], [g_w_in])
        self._adamw(_LATER, self.rs_in.start_swap())
        return self.rs_in.start_scatter([self.delta[n] for n in _LATER])

    def _adamw_small(self, names):
        for n in names:
            self.delta[n], self.new_m[n], self.new_v[n] = _adamw_whole(
                self.w[n], self.grads[n], self.m[n], self.v[n], name=f"adamw_{n}")

    def small_grads(self, grads):
        me = (4 * lax.axis_index("x") + 2 * lax.axis_index("y") + lax.axis_index("c")).astype(jnp.int32).reshape(1)
        land = _place_block(_pack_small(grads, _SMALL[1:]), me, name="place_small_grads")
        self.small = _copies_start("gather_small_start", [land], _plan_all_to_all, 7)
        return [self.small[2]]

    def finish(self, g_norm_w, loss, after):
        join = self.rs_in.start_join(after)
        sems, bufs, _ = self.small
        (land,) = _copies_wait("gather_small_wait", bufs, sems, _plan_all_to_all, join)
        self.grads.update(_unpack_small(_sum8(land, name="sum_small_grads"), self.w, _SMALL[1:]))
        self._adamw_small(_SMALL[1:])
        rows = _packed_rows(g_norm_w.size)
        late = jnp.concatenate([_pack_small(dict(norm_w=g_norm_w), _SMALL[:1]),
                                jnp.pad(loss.reshape(1, 1), ((0, SUBLANES - 1), (0, _PACK_W - 1)))], axis=0)
        late = _sum8(_all_gather_small(late), name="sum_norm_w_grad_and_loss")
        self.grads.update(_unpack_small(late[:rows], self.w, _SMALL[:1]))
        self._adamw_small(_SMALL[:1])
        self.grads.update(self.rs_in.finish([self.delta[_SMALL[0]]]))
        self._adamw(["w_in"], ())
        return late[rows, 0]


def kernel(x, norm_w, w_in, q_norm_w, k_norm_w, sinks, w_attn_proj, A_re, A_im, log_dt, B_re, B_im, C_re, C_im, D_skip, w_glu, b_glu, w_ssm_proj, w_out, loss_target, m_norm_w, m_w_in, m_q_norm_w, m_k_norm_w, m_sinks, m_w_attn_proj, m_A_re, m_A_im, m_log_dt, m_B_re, m_B_im, m_C_re, m_C_im, m_D_skip, m_w_glu, m_b_glu, m_w_ssm_proj, m_w_out, v_norm_w, v_w_in, v_q_norm_w, v_k_norm_w, v_sinks, v_w_attn_proj, v_A_re, v_A_im, v_log_dt, v_B_re, v_B_im, v_C_re, v_C_im, v_D_skip, v_w_glu, v_b_glu, v_w_ssm_proj, v_w_out):
    w = dict(norm_w=norm_w, w_in=w_in, q_norm_w=q_norm_w, k_norm_w=k_norm_w, sinks=sinks, w_attn_proj=w_attn_proj,
             A_re=A_re, A_im=A_im, log_dt=log_dt, B_re=B_re, B_im=B_im, C_re=C_re, C_im=C_im, D_skip=D_skip,
             w_glu=w_glu, b_glu=b_glu, w_ssm_proj=w_ssm_proj, w_out=w_out)
    m = dict(norm_w=m_norm_w, w_in=m_w_in, q_norm_w=m_q_norm_w, k_norm_w=m_k_norm_w, sinks=m_sinks,
             w_attn_proj=m_w_attn_proj, A_re=m_A_re, A_im=m_A_im, log_dt=m_log_dt, B_re=m_B_re, B_im=m_B_im,
             C_re=m_C_re, C_im=m_C_im, D_skip=m_D_skip, w_glu=m_w_glu, b_glu=m_b_glu, w_ssm_proj=m_w_ssm_proj,
             w_out=m_w_out)
    v = dict(norm_w=v_norm_w, w_in=v_w_in, q_norm_w=v_q_norm_w, k_norm_w=v_k_norm_w, sinks=v_sinks,
             w_attn_proj=v_w_attn_proj, A_re=v_A_re, A_im=v_A_im, log_dt=v_log_dt, B_re=v_B_re, B_im=v_B_im,
             C_re=v_C_re, C_im=v_C_im, D_skip=v_D_skip, w_glu=v_w_glu, b_glu=v_b_glu, w_ssm_proj=v_w_ssm_proj,
             w_out=v_w_out)

    io = _Exchanges(w, m, v)
    loss, grad_x, g_norm_w = _local_step(x[0], loss_target[0], norm_w, q_norm_w, k_norm_w, sinks, A_re, A_im, log_dt,
                                         B_re, B_im, C_re, C_im, D_skip, b_glu, io)
    loss = io.finish(g_norm_w, loss, [grad_x])
    grads, delta, new_m, new_v = io.grads, io.delta, io.new_m, io.new_v

    return (loss, grad_x[None], *[grads[n] for n in _ORDER], *[delta[n] for n in _ORDER],
            *[new_m[n] for n in _ORDER], *[new_v[n] for n in _ORDER])
```

```python
import functools
import math

import jax
import jax.numpy as jnp
from jax import lax
from jax.experimental import pallas as pl
from jax.experimental.pallas import tpu as pltpu

F32 = jnp.float32
_MXU = jnp.bfloat16
_WIRE = jnp.bfloat16

LANES = 128
SUBLANES = 8
VMEM_LIMIT = 56 * 1024 * 1024

D_MODEL = 2048
HEAD_DIM = 64
N_Q_HEADS = 16
N_KV_HEADS = 4
Q_PER_KV = 4
ATTN_W = 1024
KV_W = 256
WINDOW = 128
SSM_W = 1024
GROUP = 16
N_GROUPS = 64
STATE = 64
N_STATES = N_GROUPS * STATE
IN_W = 8704
NORM_EPS = 1e-6
N_CHIPS = 4
CW = 512
OFF_AGATE, OFF_U, OFF_Z, OFF_GA, OFF_GS = 3, 5, 7, 9, 13

SSM_T = 256
SSM_L = SSM_T // SUBLANES
SSM_JB = 8
SSM_SB = N_STATES // SSM_JB

ADAM_LR, ADAM_B1, ADAM_B2, ADAM_EPS, ADAM_WD, ADAM_STEP = 0.001, 0.9, 0.999, 1e-08, 0.01, 10

MESH = pl.DeviceIdType.MESH
_ANY = pl.BlockSpec(memory_space=pl.ANY)


def _params(sem=None):
    return pltpu.CompilerParams(dimension_semantics=sem, vmem_limit_bytes=VMEM_LIMIT)


def _mm(a, b, *, mode, name, tm, tn, tk, out_dtype=F32, b_blocked=False, out_blocked=False, rows_outer=False,
        deps=()):
    nd = len(deps)
    if mode == "tn":
        K, M = a.shape
    else:
        M, K = a.shape
    if mode == "nn":
        N = b.shape[0] * b.shape[2] if b_blocked else b.shape[1]
    elif mode == "nt":
        N = b.shape[1] if b_blocked else b.shape[0]
    else:
        N = b.shape[1]
    tm, tn, tk = min(tm, M), min(tn, N), min(tk, K)
    nj, ni, nk = N // tn, M // tm, K // tk
    assert nj * tn == N and ni * tm == M and nk * tk == K, (name, M, N, K)
    dims = {"nn": (((1,), (0,)), ((), ())), "nt": (((1,), (1,)), ((), ())), "tn": (((0,), (0,)), ((), ()))}[mode]

    if mode == "tn":
        a_spec = pl.BlockSpec((tk, tm), lambda j, i, k: (k, i))
    else:
        a_spec = pl.BlockSpec((tm, tk), lambda j, i, k: (i, k))
    if mode == "nn":
        if b_blocked:
            assert b.shape[0] == nj and b.shape[2] == tn
            b_spec = pl.BlockSpec((None, tk, tn), lambda j, i, k: (j, k, 0))
        else:
            b_spec = pl.BlockSpec((tk, tn), lambda j, i, k: (k, j))
    elif mode == "nt":
        if b_blocked:
            assert b.shape[0] == nk and b.shape[2] == tk
            b_spec = pl.BlockSpec((None, tn, tk), lambda j, i, k: (k, j, 0))
        else:
            b_spec = pl.BlockSpec((tn, tk), lambda j, i, k: (j, k))
    else:
        b_spec = pl.BlockSpec((tk, tn), lambda j, i, k: (k, j))
    whole_out = out_blocked and nj == 1
    if whole_out:
        assert ni == 1
        o_spec = pl.BlockSpec((N_CHIPS, tm, tn // N_CHIPS), lambda j, i, k: (0, 0, 0))
        o_shape = jax.ShapeDtypeStruct((N_CHIPS, M, tn // N_CHIPS), out_dtype)
    elif out_blocked:
        assert nj == N_CHIPS
        o_spec = pl.BlockSpec((None, tm, tn), lambda j, i, k: (j, i, 0))
        o_shape = jax.ShapeDtypeStruct((nj, M, tn), out_dtype)
    else:
        o_spec = pl.BlockSpec((tm, tn), lambda j, i, k: (i, j))
        o_shape = jax.ShapeDtypeStruct((M, N), out_dtype)
    use_acc = nk > 1 and (out_dtype != F32 or whole_out)

    def body(a_ref, b_ref, *rest):
        o_ref, scratch = rest[nd], rest[nd + 1:]

        def product():
            return lax.dot_general(a_ref[...].astype(_MXU), b_ref[...].astype(_MXU), dims, preferred_element_type=F32)

        def write(result):
            if whole_out:
                w = tn // N_CHIPS
                for c in range(N_CHIPS):
                    o_ref[c] = result[:, c * w:(c + 1) * w].astype(o_ref.dtype)
            else:
                o_ref[...] = result.astype(o_ref.dtype)

        if nk == 1:
            write(product())
            return
        k = pl.program_id(2)
        acc = scratch[0] if use_acc else o_ref

        @pl.when(k == 0)
        def _():
            acc[...] = jnp.zeros_like(acc)

        acc[...] += product()

        if use_acc:
            @pl.when(k == nk - 1)
            def _():
                write(acc[...])

    specs = [a_spec, b_spec, o_spec]
    grid = (nj, ni, nk)
    if rows_outer:
        specs = [pl.BlockSpec(s.block_shape, lambda i, j, k, f=s.index_map: f(j, i, k)) for s in specs]
        grid = (ni, nj, nk)
    return pl.pallas_call(
        body, name=name, grid=grid, in_specs=specs[:2] + [_ANY] * nd, out_specs=specs[2],
        out_shape=o_shape, scratch_shapes=[pltpu.VMEM((tm, tn), F32)] if use_acc else [],
        compiler_params=_params(("parallel", "parallel", "arbitrary")),
    )(a, b, *deps)


def _mm_chip_block(a, b4, blk, prev, *, name, tm=512, deps=()):
    M, K = a.shape
    nchip, _, C = b4.shape
    tm = min(tm, M)
    extra = ([] if prev is None else [prev]) + list(deps)

    def body(blk_ref, a_ref, b_ref, *rest):
        rest[-1][...] = jnp.dot(a_ref[...].astype(_MXU), b_ref[...].astype(_MXU), preferred_element_type=F32)

    return pl.pallas_call(
        body, name=name,
        grid_spec=pltpu.PrefetchScalarGridSpec(
            num_scalar_prefetch=1, grid=(M // tm,),
            in_specs=[pl.BlockSpec((tm, K), lambda i, c: (i, 0)), pl.BlockSpec((None, K, C), lambda i, c: (c[0], 0, 0))]
            + [_ANY] * len(extra),
            out_specs=pl.BlockSpec((tm, C), lambda i, c: (i, c[0]))),
        out_shape=jax.ShapeDtypeStruct((M, nchip * C), F32),
        input_output_aliases={} if prev is None else {3: 0},
        compiler_params=_params(("arbitrary",)),
    )(blk, a, b4, *extra)


def _mm_merge_out_loss(proj, y_a, y_s, w_out, x, target, *, tm=256):
    rows, d = x.shape
    ncol = d // CW

    def body(*refs):
        ga_refs, gs_refs = refs[:ncol], refs[ncol:2 * ncol]
        ya_ref, ys_ref, w_ref, x_ref, t_ref, m_ref, d_ref, db_ref, sq_ref = refs[2 * ncol:]
        for j in range(ncol):
            cols = slice(j * CW, (j + 1) * CW)
            m_ref[:, cols] = (_sigmoid(ga_refs[j][...]) * ya_ref[:, cols].astype(F32)
                              + _sigmoid(gs_refs[j][...]) * ys_ref[:, cols].astype(F32)).astype(m_ref.dtype)
        mo = jnp.dot(m_ref[...], w_ref[...].astype(_MXU), preferred_element_type=F32)
        err = (x_ref[...] + mo) - t_ref[...]
        dout = err * (1.0 / d)
        d_ref[...] = dout
        db_ref[...] = dout.astype(db_ref.dtype)
        part = _colsum(err * err)
        i = pl.program_id(0)

        @pl.when(i == 0)
        def _():
            sq_ref[...] = part

        @pl.when(i > 0)
        def _():
            sq_ref[...] += part

    tile = pl.BlockSpec((tm, d), lambda i: (i, 0))
    gate = [pl.BlockSpec((tm, CW), lambda i, c=off + j: (i, c)) for off in (OFF_GA, OFF_GS) for j in range(ncol)]
    return pl.pallas_call(
        body, name="mm_merge_out_loss", grid=(rows // tm,),
        in_specs=gate + [tile, tile, pl.BlockSpec((d, d), lambda i: (0, 0), pipeline_mode=pl.Buffered(1)), tile, tile],
        out_specs=[tile, tile, tile, pl.BlockSpec((1, d), lambda i: (0, 0))],
        out_shape=[jax.ShapeDtypeStruct((rows, d), _MXU), jax.ShapeDtypeStruct((rows, d), F32),
                   jax.ShapeDtypeStruct((rows, d), _MXU), jax.ShapeDtypeStruct((1, d), F32)],
        compiler_params=_params(("arbitrary",)),
    )(*([proj] * (2 * ncol)), y_a, y_s, w_out, x, target)


def _mm_merge_bwd(dout_b, w_out, proj, y_a, y_s, *, tm=256):
    rows, d = y_a.shape
    ncol = d // CW

    def body(do_ref, w_ref, *refs):
        ga_refs, gs_refs = refs[:ncol], refs[ncol:2 * ncol]
        ya_ref, ys_ref, dya_ref, dys_ref, dg_ref = refs[2 * ncol:]
        dm = lax.dot_general(do_ref[...].astype(_MXU), w_ref[...].astype(_MXU), _NT, preferred_element_type=F32)
        for j in range(ncol):
            cols = slice(j * CW, (j + 1) * CW)
            dmj = dm[:, cols]
            sa, ss = _sigmoid(ga_refs[j][...]), _sigmoid(gs_refs[j][...])
            dya_ref[:, cols] = (sa * dmj).astype(dya_ref.dtype)
            dys_ref[:, cols] = (ss * dmj).astype(dys_ref.dtype)
            dg_ref[:, cols] = (dmj * ya_ref[:, cols].astype(F32) * sa * (1.0 - sa)).astype(dg_ref.dtype)
            dg_ref[:, d + j * CW:d + (j + 1) * CW] = (dmj * ys_ref[:, cols].astype(F32) * ss
                                                      * (1.0 - ss)).astype(dg_ref.dtype)

    tile = pl.BlockSpec((tm, d), lambda i: (i, 0))
    gate = [pl.BlockSpec((tm, CW), lambda i, c=off + j: (i, c)) for off in (OFF_GA, OFF_GS) for j in range(ncol)]
    both = pl.BlockSpec((pl.Element(tm), pl.Element(2 * d)), lambda i: (i * tm, OFF_GA * CW))
    return pl.pallas_call(
        body, name="mm_merge_bwd", grid=(rows // tm,),
        in_specs=[tile, pl.BlockSpec((d, d), lambda i: (0, 0))] + gate + [tile, tile],
        out_specs=[tile, tile, both],
        out_shape=[jax.ShapeDtypeStruct((rows, d), _MXU)] * 2 + [jax.ShapeDtypeStruct((rows, IN_W), _MXU)],
        compiler_params=_params(("arbitrary",)),
    )(dout_b, w_out, *([proj] * (2 * ncol)), y_a, y_s)


def _mm_glu_gate(yg, w_glu4, b_glu, proj, *, tm=1024):
    rows, k = yg.shape
    nj, _, tn = w_glu4.shape
    w = nj * tn // 2
    tm = min(tm, rows)

    def body(a_ref, w_ref, ba_ref, bb_ref, z0_ref, z1_ref, glu_ref, ys_ref):
        j = pl.program_id(1)
        for c in range(nj):
            @pl.when(j == c)
            def _(c=c):
                glu_ref[:, c * tn:(c + 1) * tn] = jnp.dot(a_ref[...].astype(_MXU), w_ref[...].astype(_MXU),
                                                          preferred_element_type=F32)

        @pl.when(j == nj - 1)
        def _():
            z = jnp.concatenate([z0_ref[...], z1_ref[...]], axis=1)
            ys_ref[...] = ((glu_ref[:, :w] + ba_ref[...]) * _sigmoid(glu_ref[:, w:] + bb_ref[...])
                           * (z * _sigmoid(z))).astype(ys_ref.dtype)

    bias = lambda c: pl.BlockSpec((1, w), lambda i, j: (0, c))
    zcol = lambda c: pl.BlockSpec((tm, CW), lambda i, j: (i, OFF_Z + c))
    return pl.pallas_call(
        body, name="mm_glu_gate", grid=(rows // tm, nj),
        in_specs=[pl.BlockSpec((tm, k), lambda i, j: (i, 0)), pl.BlockSpec((None, k, tn), lambda i, j: (j, 0, 0)),
                  bias(0), bias(1), zcol(0), zcol(1)],
        out_specs=[pl.BlockSpec((tm, nj * tn), lambda i, j: (i, 0)), pl.BlockSpec((tm, w), lambda i, j: (i, 0))],
        out_shape=[jax.ShapeDtypeStruct((rows, nj * tn), F32), jax.ShapeDtypeStruct((rows, w), _MXU)],
        compiler_params=_params(("arbitrary", "arbitrary")),
    )(yg, w_glu4, b_glu, b_glu, proj, proj)


def _mm_ssm_gate_bwd(d_ys, w_sp4, glu, b_glu, proj, d_proj, *, tm=512):
    rows, w = glu.shape[0], glu.shape[1] // 2
    nk, tk = w_sp4.shape[0], w_sp4.shape[2]
    tm = min(tm, rows)

    def body(dy_ref, w_ref, ga_ref, gb_ref, ba_ref, bb_ref, z0_ref, z1_ref, buf_ref, dg_ref, dz_ref, db_ref, acc):
        i, k = pl.program_id(0), pl.program_id(1)

        @pl.when(k == 0)
        def _():
            acc[...] = jnp.zeros_like(acc)

        acc[...] += lax.dot_general(dy_ref[...].astype(_MXU), w_ref[...].astype(_MXU), _NT, preferred_element_type=F32)

        @pl.when(k == nk - 1)
        def _():
            dv = acc[...]
            a, sb = ga_ref[...] + ba_ref[...], _sigmoid(gb_ref[...] + bb_ref[...])
            f, df = _silu_and_grad(jnp.concatenate([z0_ref[...], z1_ref[...]], axis=1))
            dga = dv * sb * f
            dgb = dv * a * f * sb * (1.0 - sb)
            dg_ref[:, :w] = dga.astype(dg_ref.dtype)
            dg_ref[:, w:] = dgb.astype(dg_ref.dtype)
            dz_ref[...] = (dv * a * sb * df).astype(dz_ref.dtype)
            part = jnp.concatenate([_colsum(dga), _colsum(dgb)], axis=1)

            @pl.when(i == 0)
            def _():
                db_ref[...] = part

            @pl.when(i > 0)
            def _():
                db_ref[...] += part

    half = lambda c: pl.BlockSpec((tm, w), lambda i, k: (i, c))
    bias = lambda c: pl.BlockSpec((1, w), lambda i, k: (0, c))
    zcol = lambda c: pl.BlockSpec((tm, CW), lambda i, k: (i, OFF_Z + c))
    return pl.pallas_call(
        body, name="mm_ssm_gate_bwd", grid=(rows // tm, nk),
        in_specs=[pl.BlockSpec((tm, tk), lambda i, k: (i, k)), pl.BlockSpec((None, w, tk), lambda i, k: (k, 0, 0)),
                  half(0), half(1), bias(0), bias(1), zcol(0), zcol(1), _ANY],
        out_specs=[pl.BlockSpec((tm, 2 * w), lambda i, k: (i, 0)),
                   pl.BlockSpec((pl.Element(tm), pl.Element(w)), lambda i, k: (i * tm, OFF_Z * CW)),
                   pl.BlockSpec((1, 2 * w), lambda i, k: (0, 0))],
        out_shape=[jax.ShapeDtypeStruct((rows, 2 * w), _MXU), jax.ShapeDtypeStruct(d_proj.shape, d_proj.dtype),
                   jax.ShapeDtypeStruct((1, 2 * w), F32)],
        input_output_aliases={8: 1},
        scratch_shapes=[pltpu.VMEM((tm, w), F32)],
        compiler_params=_params(("arbitrary", "arbitrary")),
    )(d_ys, w_sp4, glu, glu, b_glu, b_glu, proj, proj, d_proj)


def _ew(fn, ins, outs, *, rows, ncol, name, n_acc=0, tm=512, deps=(), into=None):
    deps = list(deps) + ([into[1]] if into else [])
    n_in, n_out, nd = len(ins), len(outs), len(deps)
    tm = min(tm, rows)
    in_specs = []
    for _, kind, col0 in ins:
        if kind == "mat":
            in_specs.append(pl.BlockSpec((tm, CW), lambda j, i, c0=col0: (i, c0 + j)))
        else:
            in_specs.append(pl.BlockSpec((1, CW), lambda j, i, c0=col0: (0, c0 + j)))
    out_specs = [pl.BlockSpec((tm, CW), lambda j, i: (i, j)) for _ in outs]
    out_shape = [jax.ShapeDtypeStruct((rows, w), dt) for w, dt in outs]
    if into:
        out_specs[into[0]] = pl.BlockSpec((tm, CW), lambda j, i, c0=into[2]: (i, c0 + j))
        out_shape[into[0]] = jax.ShapeDtypeStruct(into[1].shape, into[1].dtype)
    for _ in range(n_acc):
        out_specs.append(pl.BlockSpec((1, CW), lambda j, i: (0, j)))
        out_shape.append(jax.ShapeDtypeStruct((1, ncol * CW), F32))

    def body(*refs):
        vals = fn(*[r[...] for r in refs[:n_in]])
        refs = refs[n_in + nd:]
        for r, v in zip(refs[:n_out], vals[:n_out]):
            r[...] = v.astype(r.dtype)
        i = pl.program_id(1)
        for r, v in zip(refs[n_out:], vals[n_out:]):
            @pl.when(i == 0)
            def _(r=r, v=v):
                r[...] = v

            @pl.when(i > 0)
            def _(r=r, v=v):
                r[...] += v

    res = pl.pallas_call(
        body, name=name, grid=(ncol, rows // tm), in_specs=in_specs + [_ANY] * nd, out_specs=out_specs,
        out_shape=out_shape, input_output_aliases={n_in + nd - 1: into[0]} if into else {},
        compiler_params=_params(("parallel", "arbitrary")),
    )(*[a for a, _, _ in ins], *deps)
    return res


def _colsum(v):
    return jnp.sum(v, axis=0, keepdims=True)


def _sigmoid(v):
    return jax.nn.sigmoid(v)


def _silu_and_grad(v):
    s = _sigmoid(v)
    return v * s, s * (1.0 + v * (1.0 - s))


def _rms_fwd(x, w, *, tm=512, deps=()):
    rows, d = x.shape
    nd = len(deps)

    def body(x_ref, w_ref, *rest):
        h_ref, r_ref = rest[nd:]
        xv = x_ref[...]
        r = lax.rsqrt(jnp.mean(xv * xv, axis=-1, keepdims=True) + NORM_EPS)
        h_ref[...] = (xv * r * w_ref[...]).astype(h_ref.dtype)
        r_ref[...] = r

    return pl.pallas_call(
        body, name="rms_fwd", grid=(rows // tm,),
        in_specs=[pl.BlockSpec((tm, d), lambda i: (i, 0)), pl.BlockSpec((1, d), lambda i: (0, 0))] + [_ANY] * nd,
        out_specs=[pl.BlockSpec((tm, d), lambda i: (i, 0)), pl.BlockSpec((tm, 1), lambda i: (i, 0))],
        out_shape=[jax.ShapeDtypeStruct((rows, d), _MXU), jax.ShapeDtypeStruct((rows, 1), F32)],
        compiler_params=_params(("arbitrary",)),
    )(x, w, *deps)


def _rms_bwd(dh, x, rstd, w, dout, *, tm=256):
    rows, d = x.shape

    def body(dh_ref, x_ref, r_ref, w_ref, do_ref, gx_ref, gw_ref):
        dhv, xv, r, wv = dh_ref[...], x_ref[...], r_ref[...], w_ref[...]
        xr = xv * r
        t = jnp.mean(dhv * wv * xr, axis=-1, keepdims=True)
        gx_ref[...] = do_ref[...] + r * (wv * dhv - xr * t)
        part = _colsum(dhv * xr)
        i = pl.program_id(0)

        @pl.when(i == 0)
        def _():
            gw_ref[...] = part

        @pl.when(i > 0)
        def _():
            gw_ref[...] += part

    return pl.pallas_call(
        body, name="rms_bwd", grid=(rows // tm,),
        in_specs=[pl.BlockSpec((tm, d), lambda i: (i, 0)), pl.BlockSpec((tm, d), lambda i: (i, 0)),
                  pl.BlockSpec((tm, 1), lambda i: (i, 0)), pl.BlockSpec((1, d), lambda i: (0, 0)),
                  pl.BlockSpec((tm, d), lambda i: (i, 0))],
        out_specs=[pl.BlockSpec((tm, d), lambda i: (i, 0)), pl.BlockSpec((1, d), lambda i: (0, 0))],
        out_shape=[jax.ShapeDtypeStruct((rows, d), F32), jax.ShapeDtypeStruct((1, d), F32)],
        compiler_params=_params(("arbitrary",)),
    )(dh, x, rstd, w, dout)


_NT = (((1,), (1,)), ((), ()))
_TN = (((0,), (0,)), ((), ()))


QKV_W = ATTN_W + 2 * KV_W
HEADS_PER_TILE = LANES // HEAD_DIM


def _low_half(rows):
    return lax.broadcasted_iota(jnp.int32, (rows, LANES), 1) < HEAD_DIM


def _pair_mean(t, low):
    m_lo = jnp.sum(jnp.where(low, t, 0.0), axis=-1, keepdims=True)
    m_hi = jnp.sum(jnp.where(low, 0.0, t), axis=-1, keepdims=True)
    return jnp.where(low, m_lo, m_hi) * (1.0 / HEAD_DIM)


def _pair_rstd(t, low):
    return lax.rsqrt(_pair_mean(t * t, low) + NORM_EPS)


def _dup_half(t, hi, low):
    swapped = pltpu.roll(t, HEAD_DIM, 1)
    return jnp.where(low, swapped, t) if hi else jnp.where(low, t, swapped)


def _fold_halves(t):
    return t + pltpu.roll(t, HEAD_DIM, 1)


def _split_heads(t, low):
    return [jnp.where(low, t, 0.0), jnp.where(low, 0.0, t)]


def _stacked_band_mask(n):
    rows = Q_PER_KV * WINDOW
    qi = lax.broadcasted_iota(jnp.int32, (rows, 2 * WINDOW), 0) % WINDOW + WINDOW
    kj = lax.broadcasted_iota(jnp.int32, (rows, 2 * WINDOW), 1)
    diff = qi - kj
    first_key = jnp.where(n > 0, 0, WINDOW)
    return (diff >= 0) & (diff < WINDOW) & (kj >= first_key)


def _stacked_sinks(sink_ref, g):
    blk = lax.broadcasted_iota(jnp.int32, (Q_PER_KV * WINDOW, 1), 0) // WINDOW
    col = jnp.full((Q_PER_KV * WINDOW, 1), sink_ref[Q_PER_KV * g], F32)
    for r in range(1, Q_PER_KV):
        col = jnp.where(blk == r, sink_ref[Q_PER_KV * g + r], col)
    return col


def _attn_in_specs(nblk, rev):
    def cur(n):
        return (nblk - 1 - n) if rev else n

    q_spec = pl.BlockSpec((WINDOW, ATTN_W), lambda n: (cur(n), 0))
    kvc_spec = pl.BlockSpec((WINDOW, 2 * KV_W), lambda n: (cur(n), ATTN_W // (2 * KV_W)))
    kvp_spec = pl.BlockSpec((WINDOW, 2 * KV_W), lambda n: (jnp.maximum(cur(n) - 1, 0), ATTN_W // (2 * KV_W)))
    w_spec = pl.BlockSpec((1, LANES), lambda n: (0, 0))
    l_spec = pl.BlockSpec((WINDOW, N_Q_HEADS), lambda n: (cur(n), 0))
    gate_specs = [pl.BlockSpec((WINDOW, CW), lambda n, col=OFF_AGATE + j: (cur(n), col)) for j in range(ATTN_W // CW)]
    return q_spec, kvc_spec, kvp_spec, w_spec, l_spec, gate_specs


def _attn2_fwd(proj, qw2, kw2, sinks, deps=()):
    seq = proj.shape[0]
    nblk = seq // WINDOW
    scale = 1.0 / math.sqrt(HEAD_DIM)
    q_spec, kvc_spec, kvp_spec, w_spec, l_spec, gate_specs = _attn_in_specs(nblk, False)
    nd, ng = len(deps), len(gate_specs)

    def body(sink_ref, q_ref, kvc_ref, kvp_ref, qw_ref, kw_ref, *rest):
        gate_refs = rest[:ng]
        o_ref, lse_ref, ya_ref = rest[ng + nd:]
        n = pl.program_id(0)
        low, low2 = _low_half(WINDOW), _low_half(2 * WINDOW)
        valid = _stacked_band_mask(n)
        head_lane = lax.broadcasted_iota(jnp.int32, (WINDOW, N_Q_HEADS), 1)
        kv = jnp.concatenate([kvp_ref[...], kvc_ref[...]], axis=0)
        qwv, kwv = qw_ref[...], kw_ref[...]
        lse_blk = jnp.zeros((WINDOW, N_Q_HEADS), F32)
        for t in range(N_KV_HEADS // HEADS_PER_TILE):
            kt = kv[:, t * LANES:(t + 1) * LANES]
            vt = kv[:, KV_W + t * LANES:KV_W + (t + 1) * LANES]
            kn = kt * _pair_rstd(kt, low2) * kwv
            for hi in range(HEADS_PER_TILE):
                g = HEADS_PER_TILE * t + hi
                kdup = _dup_half(kn, hi, low2).astype(_MXU)
                vdup = _dup_half(vt, hi, low2).astype(_MXU)
                stack = []
                for tq in (2 * g, 2 * g + 1):
                    qt = q_ref[:, tq * LANES:(tq + 1) * LANES]
                    stack += _split_heads(qt * _pair_rstd(qt, low) * qwv, low)
                qs = jnp.concatenate(stack, axis=0).astype(_MXU)
                s = lax.dot_general(qs, kdup, _NT, preferred_element_type=F32) * scale
                s = jnp.where(valid, s, -1e30)
                sink = _stacked_sinks(sink_ref, g)
                m = jnp.maximum(jnp.max(s, axis=-1, keepdims=True), sink)
                e = jnp.exp(s - m)
                z = jnp.sum(e, axis=-1, keepdims=True) + jnp.exp(sink - m)
                o = jnp.dot((e / z).astype(_MXU), vdup, preferred_element_type=F32)
                for i, tq in enumerate((2 * g, 2 * g + 1)):
                    o_ref[:, tq * LANES:(tq + 1) * LANES] = jnp.where(
                        low, o[2 * i * WINDOW:(2 * i + 1) * WINDOW], o[(2 * i + 1) * WINDOW:(2 * i + 2) * WINDOW])
                lse = m + jnp.log(z)
                for r in range(Q_PER_KV):
                    lse_blk = jnp.where(head_lane == Q_PER_KV * g + r, lse[r * WINDOW:(r + 1) * WINDOW], lse_blk)
        lse_ref[...] = lse_blk
        for j, g_ref in enumerate(gate_refs):
            cols = slice(j * CW, (j + 1) * CW)
            gate = g_ref[...]
            ya_ref[:, cols] = (o_ref[:, cols] * (gate * _sigmoid(gate))).astype(ya_ref.dtype)

    return pl.pallas_call(
        body, name="attn_fwd", grid=(nblk,),
        in_specs=[pl.BlockSpec(memory_space=pltpu.SMEM), q_spec, kvc_spec, kvp_spec, w_spec, w_spec] + gate_specs
        + [_ANY] * nd,
        out_specs=[q_spec, l_spec, q_spec],
        out_shape=[jax.ShapeDtypeStruct((seq, ATTN_W), F32), jax.ShapeDtypeStruct((seq, N_Q_HEADS), F32),
                   jax.ShapeDtypeStruct((seq, ATTN_W), _MXU)],
        compiler_params=_params(("arbitrary",)),
    )(sinks, proj, proj, proj, qw2, kw2, *([proj] * ng), *deps)


def _attn2_bwd(proj, qw2, kw2, sinks, lse, attn, dya, d_proj, deps=()):
    seq = proj.shape[0]
    nblk = seq // WINDOW
    scale = 1.0 / math.sqrt(HEAD_DIM)
    q_spec, kvc_spec, kvp_spec, w_spec, l_spec, gate_specs = _attn_in_specs(nblk, True)
    s_spec = pl.BlockSpec((1, N_Q_HEADS), lambda n: (0, 0))
    d_spec = pl.BlockSpec((WINDOW, QKV_W + ATTN_W), lambda n: (nblk - 1 - n, 0))
    deps = list(deps) + [d_proj]
    nd, ng = len(deps), len(gate_specs)

    def body(sink_ref, q_ref, kvc_ref, kvp_ref, qw_ref, kw_ref, lse_ref, attn_ref, dya_ref, *rest):
        gate_refs = rest[:ng]
        d_ref, dqw_ref, dkw_ref, dsk_ref, carry, do_ref = rest[ng + nd:]
        step = pl.program_id(0)
        n = nblk - 1 - step

        @pl.when(step == 0)
        def _():
            carry[...] = jnp.zeros_like(carry)
            dqw_ref[...] = jnp.zeros_like(dqw_ref)
            dkw_ref[...] = jnp.zeros_like(dkw_ref)
            dsk_ref[...] = jnp.zeros_like(dsk_ref)

        for j, g_ref in enumerate(gate_refs):
            cols = slice(j * CW, (j + 1) * CW)
            f, df = _silu_and_grad(g_ref[...])
            dv = dya_ref[:, cols]
            do_ref[:, cols] = dv * f
            d_ref[:, QKV_W + j * CW:QKV_W + (j + 1) * CW] = (dv * attn_ref[:, cols] * df).astype(d_ref.dtype)

        low, low2 = _low_half(WINDOW), _low_half(2 * WINDOW)
        valid = _stacked_band_mask(n)
        head_lane = lax.broadcasted_iota(jnp.int32, (WINDOW, N_Q_HEADS), 1)
        sink_lane = lax.broadcasted_iota(jnp.int32, (1, N_Q_HEADS), 1)
        kv = jnp.concatenate([kvp_ref[...], kvc_ref[...]], axis=0)
        qwv, kwv = qw_ref[...], kw_ref[...]
        lse_blk = lse_ref[...]
        dqw = jnp.zeros((1, LANES), F32)
        dkw = jnp.zeros((1, LANES), F32)
        dsk = jnp.zeros((1, N_Q_HEADS), F32)
        for t in range(N_KV_HEADS // HEADS_PER_TILE):
            kt = kv[:, t * LANES:(t + 1) * LANES]
            vt = kv[:, KV_W + t * LANES:KV_W + (t + 1) * LANES]
            rk = _pair_rstd(kt, low2)
            kn = kt * rk * kwv
            dkn_t = jnp.zeros((2 * WINDOW, LANES), F32)
            dv_t = jnp.zeros((2 * WINDOW, LANES), F32)
            for hi in range(HEADS_PER_TILE):
                g = HEADS_PER_TILE * t + hi
                kdup = _dup_half(kn, hi, low2).astype(_MXU)
                vdup = _dup_half(vt, hi, low2).astype(_MXU)
                tiles = (2 * g, 2 * g + 1)
                qx, rq, stack, dstack, lse_rows = [], [], [], [], []
                for tq in tiles:
                    qt = q_ref[:, tq * LANES:(tq + 1) * LANES]
                    r = _pair_rstd(qt, low)
                    rq.append(r)
                    qx.append(qt * r)
                    stack += _split_heads(qx[-1] * qwv, low)
                    dstack += _split_heads(do_ref[:, tq * LANES:(tq + 1) * LANES], low)
                for r in range(Q_PER_KV):
                    lse_rows.append(jnp.sum(jnp.where(head_lane == Q_PER_KV * g + r, lse_blk, 0.0), axis=-1, keepdims=True))
                qs = jnp.concatenate(stack, axis=0).astype(_MXU)
                dos = jnp.concatenate(dstack, axis=0).astype(_MXU)
                lse_col = jnp.concatenate(lse_rows, axis=0)
                s = lax.dot_general(qs, kdup, _NT, preferred_element_type=F32) * scale
                s = jnp.where(valid, s, -1e30)
                p = jnp.exp(s - lse_col)
                dp = lax.dot_general(dos, vdup, _NT, preferred_element_type=F32)
                dsum = jnp.sum(p * dp, axis=-1, keepdims=True)
                ds = (p * (dp - dsum) * scale).astype(_MXU)
                dsink = -jnp.exp(_stacked_sinks(sink_ref, g) - lse_col) * dsum
                for r in range(Q_PER_KV):
                    dsk = dsk + jnp.where(sink_lane == Q_PER_KV * g + r, _colsum(dsink[r * WINDOW:(r + 1) * WINDOW]), 0.0)
                dv_g = _fold_halves(lax.dot_general(p.astype(_MXU), dos, _TN, preferred_element_type=F32))
                dkn_g = _fold_halves(lax.dot_general(ds, qs, _TN, preferred_element_type=F32))
                dv_t = jnp.where(low2, dv_t, dv_g) if hi else jnp.where(low2, dv_g, dv_t)
                dkn_t = jnp.where(low2, dkn_t, dkn_g) if hi else jnp.where(low2, dkn_g, dkn_t)
                dqn = jnp.dot(ds, kdup, preferred_element_type=F32)
                for i, tq in enumerate(tiles):
                    dqn_t = jnp.where(low, dqn[2 * i * WINDOW:(2 * i + 1) * WINDOW],
                                      dqn[(2 * i + 1) * WINDOW:(2 * i + 2) * WINDOW])
                    dq = rq[i] * (qwv * dqn_t - qx[i] * _pair_mean(dqn_t * qwv * qx[i], low))
                    d_ref[:, tq * LANES:(tq + 1) * LANES] = dq.astype(d_ref.dtype)
                    dqw = dqw + _colsum(dqn_t * qx[i])
            k_cols = slice(t * LANES, (t + 1) * LANES)
            v_cols = slice(KV_W + t * LANES, KV_W + (t + 1) * LANES)
            dkn_c = dkn_t[WINDOW:] + carry[:, k_cols]
            rc = rk[WINDOW:]
            kx = kt[WINDOW:] * rc
            dk = rc * (kwv * dkn_c - kx * _pair_mean(dkn_c * kwv * kx, low))
            d_ref[:, ATTN_W + t * LANES:ATTN_W + (t + 1) * LANES] = dk.astype(d_ref.dtype)
            d_ref[:, ATTN_W + KV_W + t * LANES:ATTN_W + KV_W + (t + 1) * LANES] = (
                dv_t[WINDOW:] + carry[:, v_cols]).astype(d_ref.dtype)
            carry[:, k_cols] = dkn_t[:WINDOW]
            carry[:, v_cols] = dv_t[:WINDOW]
            dkw = dkw + _colsum(dkn_c * kx)
        dqw_ref[...] += dqw
        dkw_ref[...] += dkw
        dsk_ref[...] += dsk

    return pl.pallas_call(
        body, name="attn_bwd", grid=(nblk,),
        in_specs=[pl.BlockSpec(memory_space=pltpu.SMEM), q_spec, kvc_spec, kvp_spec, w_spec, w_spec, l_spec, q_spec,
                  q_spec] + gate_specs + [_ANY] * nd,
        out_specs=[d_spec, w_spec, w_spec, s_spec],
        out_shape=[jax.ShapeDtypeStruct(d_proj.shape, d_proj.dtype), jax.ShapeDtypeStruct((1, LANES), F32),
                   jax.ShapeDtypeStruct((1, LANES), F32), jax.ShapeDtypeStruct((1, N_Q_HEADS), F32)],
        input_output_aliases={9 + ng + nd - 1: 0},
        scratch_shapes=[pltpu.VMEM((WINDOW, 2 * KV_W), F32), pltpu.VMEM((WINDOW, ATTN_W), F32)],
        compiler_params=_params(("arbitrary",)),
    )(sinks, proj, proj, proj, qw2, kw2, lse, attn, dya, *([proj] * ng), *deps)


def _ssm_discretise(a_re, a_im, log_dt):
    dt = jnp.exp(log_dt)
    mag = jnp.exp(dt * a_re)
    ab_re = mag * jnp.cos(dt * a_im)
    ab_im = mag * jnp.sin(dt * a_im)
    num_re = ab_re - 1.0
    num_im = ab_im
    den = a_re * a_re + a_im * a_im
    cf_re = (num_re * a_re + num_im * a_im) / den
    cf_im = (num_im * a_re - num_re * a_im) / den
    return ab_re, ab_im, cf_re, cf_im


def _ssm_params_fwd(a_re, a_im, log_dt):
    shp = jax.ShapeDtypeStruct(a_re.shape, F32)

    def body(are_ref, aim_ref, ldt_ref, abr_ref, abi_ref, cfr_ref, cfi_ref, alr_ref, ali_ref):
        abr, abi, cfr, cfi = _ssm_discretise(are_ref[...], aim_ref[...], ldt_ref[...])
        abr_ref[...], abi_ref[...], cfr_ref[...], cfi_ref[...] = abr, abi, cfr, cfi
        pr, pi = abr, abi
        for _ in range(int(math.log2(SSM_L))):
            pr, pi = pr * pr - pi * pi, 2.0 * pr * pi
        alr_ref[...], ali_ref[...] = pr, pi

    return pl.pallas_call(body, name="ssm_params_fwd", out_shape=[shp] * 6)(a_re, a_im, log_dt)


def _ssm_params_bwd(a_re, a_im, log_dt, d_abr, d_abi, d_cfr, d_cfi):
    def body(are_ref, aim_ref, ldt_ref, g0, g1, g2, g3, dare_ref, daim_ref, dldt_ref):
        _, vjp = jax.vjp(_ssm_discretise, are_ref[...], aim_ref[...], ldt_ref[...])
        dare_ref[...], daim_ref[...], dldt_ref[...] = vjp((g0[...], g1[...], g2[...], g3[...]))

    return pl.pallas_call(
        body, name="ssm_params_bwd",
        out_shape=[jax.ShapeDtypeStruct(a_re.shape, F32), jax.ShapeDtypeStruct(a_im.shape, F32),
                   jax.ShapeDtypeStruct(log_dt.shape, F32)],
    )(a_re, a_im, log_dt, d_abr, d_abi, d_cfr, d_cfi)


def _scan_cols(j):
    return pl.ds(j * SSM_SB, SSM_SB)


def _rows8(r):
    return pl.ds(pl.multiple_of(r * SUBLANES, SUBLANES), SUBLANES)


def _bcast8(row):
    return jnp.broadcast_to(row, (SUBLANES, row.shape[-1]))


def _token_order_pick():
    tok = lax.broadcasted_iota(jnp.int32, (SSM_T, SSM_T), 0)
    row = lax.broadcasted_iota(jnp.int32, (SSM_T, SSM_T), 1)
    return (row == SUBLANES * (tok % SSM_L) + tok // SSM_L).astype(_MXU)


SCAN_UNROLL = 8


def _scan_loop(n, step, init):
    def trip(o, carry):
        for i in range(SCAN_UNROLL):
            carry = step(o * SCAN_UNROLL + i, carry)
        return carry

    return lax.fori_loop(0, n // SCAN_UNROLL, trip, init)


def _ssm_fwd(u, b_re, b_im, c_re, c_im, d_skip, coef):
    seq = u.shape[0]
    nc = seq // SSM_T
    T, L = SSM_T, SSM_L

    def body(u_ref, bre_ref, bim_ref, cre_ref, cim_ref, d_ref, are_ref, aim_ref, cfr_ref, cfi_ref, alr_ref, ali_ref,
             y_ref, yg_ref, sre_ref, sim_ref, ire_ref, iim_ref, car_re, car_im, end_re, end_im, yg_scan):
        c = pl.program_id(0)

        @pl.when(c == 0)
        def _():
            car_re[...] = jnp.zeros_like(car_re)
            car_im[...] = jnp.zeros_like(car_im)

        for j in range(SSM_JB):
            ub = u_ref[:, j * LANES:(j + 1) * LANES].astype(_MXU)
            bur = jnp.dot(ub, bre_ref[j], preferred_element_type=F32)
            bui = jnp.dot(ub, bim_ref[j], preferred_element_type=F32)
            cfr, cfi = cfr_ref[:, _scan_cols(j)], cfi_ref[:, _scan_cols(j)]
            sre_ref[:, _scan_cols(j)] = cfr * bur - cfi * bui
            sim_ref[:, _scan_cols(j)] = cfr * bui + cfi * bur

        for j in range(SSM_JB):
            cols = _scan_cols(j)
            ar, ai = _bcast8(are_ref[:, cols]), _bcast8(aim_ref[:, cols])

            def step1(r, s, cols=cols, ar=ar, ai=ai):
                sr, si = s
                rows = _rows8(r)
                return (ar * sr - ai * si + sre_ref[rows, cols], ar * si + ai * sr + sim_ref[rows, cols])

            zero = jnp.zeros((SUBLANES, SSM_SB), F32)
            er, ei = _scan_loop(L, step1, (zero, zero))
            end_re[:, cols] = er
            end_im[:, cols] = ei

        alr, ali = alr_ref[...], ali_ref[...]
        cr, ci = car_re[...], car_im[...]
        ire_ref[0:1, :] = cr
        iim_ref[0:1, :] = ci
        for i in range(1, SUBLANES):
            er, ei = end_re[i - 1:i, :], end_im[i - 1:i, :]
            cr, ci = alr * cr - ali * ci + er, alr * ci + ali * cr + ei
            ire_ref[i:i + 1, :] = cr
            iim_ref[i:i + 1, :] = ci

        for j in range(SSM_JB):
            cols = _scan_cols(j)
            ar, ai = _bcast8(are_ref[:, cols]), _bcast8(aim_ref[:, cols])

            def step2(r, s, cols=cols, ar=ar, ai=ai):
                sr, si = s
                rows = _rows8(r)
                nr = ar * sr - ai * si + sre_ref[rows, cols]
                ni = ar * si + ai * sr + sim_ref[rows, cols]
                sre_ref[rows, cols] = nr
                sim_ref[rows, cols] = ni
                return nr, ni

            _scan_loop(L, step2, (ire_ref[:, cols], iim_ref[:, cols]))

        car_re[...] = sre_ref[T - 1:T, :]
        car_im[...] = sim_ref[T - 1:T, :]

        for j in range(SSM_JB):
            cols = _scan_cols(j)
            ch = slice(j * LANES, (j + 1) * LANES)
            y = (jnp.dot(sre_ref[:, cols].astype(_MXU), cre_ref[j], preferred_element_type=F32)
                 - jnp.dot(sim_ref[:, cols].astype(_MXU), cim_ref[j], preferred_element_type=F32))
            y = y + d_ref[:, ch] * u_ref[:, ch]
            y_ref[:, ch] = y
            yg_scan[:, ch] = jax.nn.gelu(y).astype(yg_scan.dtype)
        yg_ref[...] = jnp.dot(_token_order_pick(), yg_scan[...], preferred_element_type=F32).astype(yg_ref.dtype)

    tok = pl.BlockSpec((T, SSM_W), lambda c: (c, 0))
    st = pl.BlockSpec((T, N_STATES), lambda c: (c, 0))
    ini = pl.BlockSpec((None, SUBLANES, N_STATES), lambda c: (c, 0, 0))
    bsp = pl.BlockSpec((SSM_JB, LANES, SSM_SB), lambda c: (0, 0, 0))
    csp = pl.BlockSpec((SSM_JB, SSM_SB, LANES), lambda c: (0, 0, 0))
    row_w = pl.BlockSpec((1, SSM_W), lambda c: (0, 0))
    row_s = pl.BlockSpec((1, N_STATES), lambda c: (0, 0))
    return pl.pallas_call(
        body, name="ssm_fwd", grid=(nc,),
        in_specs=[tok, bsp, bsp, csp, csp, row_w] + [row_s] * 6,
        out_specs=[tok, tok, st, st, ini, ini],
        out_shape=[jax.ShapeDtypeStruct((seq, SSM_W), F32), jax.ShapeDtypeStruct((seq, SSM_W), _MXU),
                   jax.ShapeDtypeStruct((seq, N_STATES), F32), jax.ShapeDtypeStruct((seq, N_STATES), F32),
                   jax.ShapeDtypeStruct((nc, SUBLANES, N_STATES), F32),
                   jax.ShapeDtypeStruct((nc, SUBLANES, N_STATES), F32)],
        scratch_shapes=[pltpu.VMEM((1, N_STATES), F32), pltpu.VMEM((1, N_STATES), F32),
                        pltpu.VMEM((SUBLANES, N_STATES), F32), pltpu.VMEM((SUBLANES, N_STATES), F32),
                        pltpu.VMEM((T, SSM_W), _MXU)],
        compiler_params=_params(("arbitrary",)),
    )(u, b_re, b_im, c_re, c_im, d_skip, *coef)


def _ssm_bwd(dyg, y, u, s_re, s_im, i_re, i_im, b_re, b_im, c_re, c_im, d_skip, coef, d_proj, deps=()):
    seq = u.shape[0]
    nc = seq // SSM_T
    T, L = SSM_T, SSM_L
    deps = list(deps) + [d_proj]

    def body(dyg_ref, y_ref, u_ref, sre_ref, sim_ref, ire_ref, iim_ref, bre_ref, bim_ref, cre_ref, cim_ref, d_ref,
             are_ref, aim_ref, cfr_ref, cfi_ref, alr_ref, ali_ref, *rest):
        (du_ref, dbre_out, dbim_out, dcre_out, dcim_out, dd_ref, dar_ref, dai_ref, dcfr_ref, dcfi_ref,
         lre, lim, car_re, car_im, end_re, end_im, ini_re, ini_im, dbre_ref, dbim_ref, dcre_ref, dcim_ref,
         dy_ref, du_scan) = rest[len(deps):]
        step = pl.program_id(0)
        dy_ref[...] = jax.vjp(jax.nn.gelu, y_ref[...])[1](dyg_ref[...])[0]

        @pl.when(step == 0)
        def _():
            car_re[...] = jnp.zeros_like(car_re)
            car_im[...] = jnp.zeros_like(car_im)
            for ref in (dbre_ref, dbim_ref, dcre_ref, dcim_ref, dd_ref, dar_ref, dai_ref, dcfr_ref, dcfi_ref):
                ref[...] = jnp.zeros_like(ref)

        for j in range(SSM_JB):
            dyb = dy_ref[:, j * LANES:(j + 1) * LANES].astype(_MXU)
            lre[:, _scan_cols(j)] = lax.dot_general(dyb, cre_ref[j], _NT, preferred_element_type=F32)
            lim[:, _scan_cols(j)] = -lax.dot_general(dyb, cim_ref[j], _NT, preferred_element_type=F32)

        for j in range(SSM_JB):
            cols = _scan_cols(j)
            ar, ai = _bcast8(are_ref[:, cols]), _bcast8(aim_ref[:, cols])

            def step1(t, s, cols=cols, ar=ar, ai=ai):
                sr, si = s
                rows = _rows8(L - 1 - t)
                return (ar * sr + ai * si + lre[rows, cols], ar * si - ai * sr + lim[rows, cols])

            zero = jnp.zeros((SUBLANES, SSM_SB), F32)
            er, ei = _scan_loop(L, step1, (zero, zero))
            end_re[:, cols] = er
            end_im[:, cols] = ei

        alr, ali = alr_ref[...], ali_ref[...]
        cr, ci = car_re[...], car_im[...]
        ini_re[SUBLANES - 1:SUBLANES, :] = cr
        ini_im[SUBLANES - 1:SUBLANES, :] = ci
        for i in range(SUBLANES - 2, -1, -1):
            er, ei = end_re[i + 1:i + 2, :], end_im[i + 1:i + 2, :]
            cr, ci = alr * cr + ali * ci + er, alr * ci - ali * cr + ei
            ini_re[i:i + 1, :] = cr
            ini_im[i:i + 1, :] = ci

        for j in range(SSM_JB):
            cols = _scan_cols(j)
            ar, ai = _bcast8(are_ref[:, cols]), _bcast8(aim_ref[:, cols])

            def step2(t, s, cols=cols, ar=ar, ai=ai):
                sr, si = s
                rows = _rows8(L - 1 - t)
                nr = ar * sr + ai * si + lre[rows, cols]
                ni = ar * si - ai * sr + lim[rows, cols]
                lre[rows, cols] = nr
                lim[rows, cols] = ni
                return nr, ni

            _scan_loop(L, step2, (ini_re[:, cols], ini_im[:, cols]))

        car_re[...] = lre[0:1, :]
        car_im[...] = lim[0:1, :]

        head, tail, body_rows = slice(0, SUBLANES), slice(SUBLANES, T), slice(0, T - SUBLANES)
        for j in range(SSM_JB):
            cols = _scan_cols(j)
            ch = slice(j * LANES, (j + 1) * LANES)
            lr, li = lre[:, cols], lim[:, cols]
            lt_r, lt_i, sp_r, sp_i = lre[tail, cols], lim[tail, cols], sre_ref[body_rows, cols], sim_ref[body_rows, cols]
            lh_r, lh_i, si_r, si_i = lre[head, cols], lim[head, cols], ire_ref[:, cols], iim_ref[:, cols]
            dar_ref[:, cols] += _colsum(lt_r * sp_r + lt_i * sp_i) + _colsum(lh_r * si_r + lh_i * si_i)
            dai_ref[:, cols] += _colsum(lt_i * sp_r - lt_r * sp_i) + _colsum(lh_i * si_r - lh_r * si_i)
            uf = u_ref[:, ch]
            ub = uf.astype(_MXU)
            bur = jnp.dot(ub, bre_ref[j], preferred_element_type=F32)
            bui = jnp.dot(ub, bim_ref[j], preferred_element_type=F32)
            dcfr_ref[:, cols] += _colsum(lr * bur + li * bui)
            dcfi_ref[:, cols] += _colsum(li * bur - lr * bui)
            cfr, cfi = cfr_ref[:, cols], cfi_ref[:, cols]
            dbur = (cfr * lr + cfi * li).astype(_MXU)
            dbui = (cfr * li - cfi * lr).astype(_MXU)
            dyf = dy_ref[:, ch]
            dyb = dyf.astype(_MXU)
            du = (lax.dot_general(dbur, bre_ref[j], _NT, preferred_element_type=F32)
                  + lax.dot_general(dbui, bim_ref[j], _NT, preferred_element_type=F32) + d_ref[:, ch] * dyf)
            du_scan[:, ch] = du.astype(du_scan.dtype)
            dbre_ref[j] += lax.dot_general(ub, dbur, _TN, preferred_element_type=F32)
            dbim_ref[j] += lax.dot_general(ub, dbui, _TN, preferred_element_type=F32)
            dcre_ref[j] += lax.dot_general(sre_ref[:, cols].astype(_MXU), dyb, _TN, preferred_element_type=F32)
            dcim_ref[j] -= lax.dot_general(sim_ref[:, cols].astype(_MXU), dyb, _TN, preferred_element_type=F32)
            dd_ref[:, ch] += _colsum(dyf * uf)
        du_ref[...] = jnp.dot(_token_order_pick(), du_scan[...], preferred_element_type=F32).astype(du_ref.dtype)

        @pl.when(step == nc - 1)
        def _():
            for acc, out in ((dbre_ref, dbre_out), (dbim_ref, dbim_out), (dcre_ref, dcre_out), (dcim_ref, dcim_out)):
                pltpu.sync_copy(acc, out)

    tok = pl.BlockSpec((T, SSM_W), lambda c: (nc - 1 - c, 0))
    st = pl.BlockSpec((T, N_STATES), lambda c: (nc - 1 - c, 0))
    ini = pl.BlockSpec((None, SUBLANES, N_STATES), lambda c: (nc - 1 - c, 0, 0))
    bsp = pl.BlockSpec((SSM_JB, LANES, SSM_SB), lambda c: (0, 0, 0))
    csp = pl.BlockSpec((SSM_JB, SSM_SB, LANES), lambda c: (0, 0, 0))
    row_w = pl.BlockSpec((1, SSM_W), lambda c: (0, 0))
    row_s = pl.BlockSpec((1, N_STATES), lambda c: (0, 0))
    big = pltpu.VMEM((T, N_STATES), F32)
    one = pltpu.VMEM((1, N_STATES), F32)
    eight = pltpu.VMEM((SUBLANES, N_STATES), F32)
    return pl.pallas_call(
        body, name="ssm_bwd", grid=(nc,),
        in_specs=[tok, tok, tok, st, st, ini, ini, bsp, bsp, csp, csp, row_w] + [row_s] * 6 + [_ANY] * len(deps),
        out_specs=[pl.BlockSpec((pl.Element(T), pl.Element(SSM_W)), lambda c: ((nc - 1 - c) * T, OFF_U * CW)),
                   _ANY, _ANY, _ANY, _ANY, row_w, row_s, row_s, row_s, row_s],
        input_output_aliases={18 + len(deps) - 1: 0},
        out_shape=[jax.ShapeDtypeStruct(d_proj.shape, d_proj.dtype),
                   jax.ShapeDtypeStruct((SSM_JB, LANES, SSM_SB), F32), jax.ShapeDtypeStruct((SSM_JB, LANES, SSM_SB), F32),
                   jax.ShapeDtypeStruct((SSM_JB, SSM_SB, LANES), F32), jax.ShapeDtypeStruct((SSM_JB, SSM_SB, LANES), F32),
                   jax.ShapeDtypeStruct((1, SSM_W), F32)] + [jax.ShapeDtypeStruct((1, N_STATES), F32)] * 4,
        scratch_shapes=[big, big, one, one, eight, eight, eight, eight,
                        pltpu.VMEM((SSM_JB, LANES, SSM_SB), F32), pltpu.VMEM((SSM_JB, LANES, SSM_SB), F32),
                        pltpu.VMEM((SSM_JB, SSM_SB, LANES), F32), pltpu.VMEM((SSM_JB, SSM_SB, LANES), F32),
                        pltpu.VMEM((T, SSM_W), F32), pltpu.VMEM((T, SSM_W), _MXU)],
        compiler_params=_params(("arbitrary",)),
    )(dyg, y, u, s_re, s_im, i_re, i_im, b_re, b_im, c_re, c_im, d_skip, *coef, *deps)


def _block_diag_b(b):
    t = b.reshape(SSM_JB, 8, STATE, GROUP).transpose(0, 1, 3, 2)
    eye = jnp.eye(8, dtype=b.dtype)
    return (t[:, :, :, None, :] * eye[None, :, None, :, None]).reshape(SSM_JB, LANES, SSM_SB)


def _block_diag_c(c):
    t = c.reshape(SSM_JB, 8, GROUP, STATE).transpose(0, 1, 3, 2)
    eye = jnp.eye(8, dtype=c.dtype)
    return (t[:, :, :, None, :] * eye[None, :, None, :, None]).reshape(SSM_JB, SSM_SB, LANES)


def _diag_of_b(blk):
    t = blk.reshape(SSM_JB, 8, GROUP, 8, STATE)
    d = jnp.sum(t * jnp.eye(8, dtype=blk.dtype)[None, :, None, :, None], axis=3)
    return d.transpose(0, 1, 3, 2).reshape(N_GROUPS, STATE, GROUP)


def _diag_of_c(blk):
    t = blk.reshape(SSM_JB, 8, STATE, 8, GROUP)
    d = jnp.sum(t * jnp.eye(8, dtype=blk.dtype)[None, :, None, :, None], axis=3)
    return d.transpose(0, 1, 3, 2).reshape(N_GROUPS, GROUP, STATE)


def _to_scan_order(v):
    seq, w = v.shape
    return v.reshape(seq // SSM_T, SUBLANES, SSM_L, w).transpose(0, 2, 1, 3).reshape(seq, w)


def _adamw_math(w, g, m, v):
    nm = ADAM_B1 * m + (1.0 - ADAM_B1) * g
    nv = ADAM_B2 * v + (1.0 - ADAM_B2) * jnp.square(g)
    m_hat = nm / (1.0 - ADAM_B1 ** ADAM_STEP)
    v_hat = nv / (1.0 - ADAM_B2 ** ADAM_STEP)
    return -ADAM_LR * (m_hat / (jnp.sqrt(v_hat) + ADAM_EPS) + ADAM_WD * w), nm, nv


def _adamw(w, g, m, v, *, name, tm, deps=()):
    rows, cols = w.shape
    nd = len(deps)

    def body(w_ref, g_ref, m_ref, v_ref, *rest):
        d_ref, nm_ref, nv_ref = rest[nd:]
        d_ref[...], nm_ref[...], nv_ref[...] = _adamw_math(w_ref[...], g_ref[...], m_ref[...], v_ref[...])

    spec = pl.BlockSpec((tm, cols), lambda i: (i, 0))
    shp = jax.ShapeDtypeStruct((rows, cols), F32)
    return pl.pallas_call(body, name=name, grid=(rows // tm,), in_specs=[spec] * 4 + [_ANY] * nd,
                          out_specs=[spec] * 3, out_shape=[shp] * 3,
                          compiler_params=_params(("arbitrary",)))(w, g, m, v, *deps)


def _place():
    x, y, c = lax.axis_index("x"), lax.axis_index("y"), lax.axis_index("c")
    chips = [(1 - x, y), (x, 1 - y), (1 - x, 1 - y)]
    return x, y, c, chips


def _remote(src, dst, send_sem, recv_sem, dev):
    return pltpu.make_async_remote_copy(src_ref=src, dst_ref=dst, send_sem=send_sem, recv_sem=recv_sem,
                                        device_id=dev, device_id_type=MESH)


def _place_shard(w, mine_arr, *, name, tm=256, deps=()):
    rows, cols = w.shape

    def body(m_ref, w_ref, *rest):
        rest[-1][...] = w_ref[...].astype(rest[-1].dtype)

    return pl.pallas_call(
        body, name=name,
        grid_spec=pltpu.PrefetchScalarGridSpec(
            num_scalar_prefetch=1, grid=(rows // tm,),
            in_specs=[pl.BlockSpec((tm, cols), lambda i, m: (i, 0))] + [_ANY] * len(deps),
            out_specs=pl.BlockSpec((None, tm, cols), lambda i, m: (m[0], i, 0))),
        out_shape=jax.ShapeDtypeStruct((N_CHIPS, rows, cols), _WIRE),
        compiler_params=_params(("arbitrary",)),
    )(mine_arr, w, *deps)


_HBM = pl.BlockSpec(memory_space=pltpu.HBM)
_SEM = pl.BlockSpec(memory_space=pltpu.SEMAPHORE)
_EFFECT = pltpu.SideEffectType.DATAFLOW_SIDE_EFFECTING


def _copies_start(name, bufs, plan, count, after=()):
    nb, na = len(bufs), len(after)

    def body(*refs):
        send_sems, recv_sems, token = refs[nb + na], refs[nb + na + 1], refs[-1]
        copies = plan(refs[:nb])
        assert len(copies) == count
        for i, (src, dst, dev, _) in enumerate(copies):
            _remote(src, dst, send_sems.at[i], recv_sems.at[i], dev).start()
        token[...] = jnp.zeros_like(token)

    res = pl.pallas_call(
        body, name=name, in_specs=[_HBM] * nb + [_ANY] * na,
        out_specs=(_SEM, _SEM, *[_HBM] * nb, pl.BlockSpec(memory_space=pltpu.VMEM)),
        out_shape=(pltpu.SemaphoreType.DMA((count,)), pltpu.SemaphoreType.DMA((count,)),
                   *[pltpu.HBM(b.shape, b.dtype) for b in bufs], jax.ShapeDtypeStruct((SUBLANES, LANES), F32)),
        input_output_aliases={i: 2 + i for i in range(nb)},
        compiler_params=pltpu.CompilerParams(has_side_effects=_EFFECT),
    )(*[pltpu.with_memory_space_constraint(b, pltpu.HBM) for b in bufs], *after)
    return (res[0], res[1]), list(res[2:2 + nb]), res[-1]


def _copies_wait(name, bufs, sems, plan, after=(), which=None):
    nb, na = len(bufs), len(after)

    def body(*refs):
        send_sems, recv_sems = refs[nb], refs[nb + 1]
        for i, (src, _, dev, land) in enumerate(plan(refs[:nb])):
            if which is not None and i not in which:
                continue
            cp = _remote(src, land, send_sems.at[i], recv_sems.at[i], dev)
            cp.wait_send()
            cp.wait_recv()

    res = pl.pallas_call(
        body, name=name, in_specs=[_HBM] * nb + [_SEM, _SEM] + [_ANY] * na, out_specs=[_HBM] * nb,
        out_shape=[pltpu.HBM(b.shape, b.dtype) for b in bufs],
        input_output_aliases={i: i for i in range(nb)},
        compiler_params=pltpu.CompilerParams(has_side_effects=_EFFECT),
    )(*bufs, *sems, *after)
    return list(res)


def _plan_gather_ici(fulls, which=(0, 1, 2)):
    x, y, c, chips = _place()
    copies = []
    for f in fulls:
        half = pl.ds(c * (f.shape[1] // 2), f.shape[1] // 2)
        own = f.at[2 * x + y, half]
        for chip in [chips[k] for k in which]:
            copies.append((own, own, (*chip, c), f.at[2 * chip[0] + chip[1], half]))
    return copies


def _plan_gather_d2d(fulls, which=(0, 1, 2)):
    x, y, c, chips = _place()
    copies = []
    for f in fulls:
        r2 = f.shape[1] // 2
        for chip in [chips[k] for k in which]:
            blk = 2 * chip[0] + chip[1]
            landed = f.at[blk, pl.ds(c * r2, r2)]
            copies.append((landed, landed, (x, y, 1 - c), f.at[blk, pl.ds((1 - c) * r2, r2)]))
    return copies


def _plan_relay_direct(fulls):
    (f,) = fulls
    x, y, c, chips = _place()
    half = pl.ds(c * (f.shape[1] // 2), f.shape[1] // 2)
    own = f.at[2 * x + y, half]
    return [(own, own, (*chip, c), f.at[2 * chip[0] + chip[1], half]) for chip in chips[:2]]


def _plan_relay_forward(fulls, k):
    (f,) = fulls
    x, y, c, chips = _place()
    r2 = f.shape[1] // 2
    half, other = pl.ds(c * r2, r2), pl.ds((1 - c) * r2, r2)
    quarter = pl.ds(c * r2 + k * (r2 // 2), r2 // 2)
    blk, far = 2 * chips[k][0] + chips[k][1], 2 * chips[2][0] + chips[2][1]
    passed, landed = f.at[blk, quarter], f.at[blk, half]
    return [(passed, passed, (*chips[1 - k], c), f.at[far, quarter]), (landed, landed, (x, y, 1 - c), f.at[blk, other])]


def _plan_relay_last(fulls):
    (f,) = fulls
    x, y, c, chips = _place()
    r2 = f.shape[1] // 2
    far = 2 * chips[2][0] + chips[2][1]
    landed = f.at[far, pl.ds(c * r2, r2)]
    return [(landed, landed, (x, y, 1 - c), f.at[far, pl.ds((1 - c) * r2, r2)])]


def _plan_swap_halves(refs):
    x, y, c, _ = _place()
    n = len(refs) // 2
    copies = []
    for g, land in zip(refs[:n], refs[n:]):
        r2 = g.shape[1] // 2
        copies.append((g.at[:, pl.ds((1 - c) * r2, r2), :], land, (x, y, 1 - c), land))
    return copies


def _plan_scatter_chips(refs):
    x, y, c, chips = _place()
    n = len(refs) // 2
    copies = []
    for h, land in zip(refs[:n], refs[n:]):
        for k, chip in enumerate(chips):
            copies.append((h.at[2 * chip[0] + chip[1]], land.at[k], (*chip, c), land.at[k]))
    return copies


def _plan_join_halves(totals):
    x, y, c, _ = _place()
    copies = []
    for t in totals:
        r2 = t.shape[0] // 2
        mine = t.at[pl.ds(c * r2, r2)]
        copies.append((mine, mine, (x, y, 1 - c), t.at[pl.ds((1 - c) * r2, r2)]))
    return copies


def _add_sibling_half(g, got, c_arr, *, name, tm):
    _, rows, cols = g.shape
    r2 = rows // 2
    nb = r2 // tm

    def body(c_ref, g_ref, r_ref, o_ref):
        o_ref[...] = (g_ref[...].astype(F32) + r_ref[...].astype(F32)).astype(o_ref.dtype)

    return pl.pallas_call(
        body, name=name,
        grid_spec=pltpu.PrefetchScalarGridSpec(
            num_scalar_prefetch=1, grid=(N_CHIPS, nb),
            in_specs=[pl.BlockSpec((None, tm, cols), lambda b, i, c: (b, c[0] * nb + i, 0)),
                      pl.BlockSpec((None, tm, cols), lambda b, i, c: (b, i, 0))],
            out_specs=pl.BlockSpec((None, tm, cols), lambda b, i, c: (b, i, 0))),
        out_shape=jax.ShapeDtypeStruct((N_CHIPS, r2, cols), _WIRE),
        compiler_params=_params(("arbitrary", "arbitrary")),
    )(c_arr, g, got)


def _add_chips(h, got, place_arr, *, name, tm):
    _, r2, cols = h.shape
    nb = r2 // tm

    def body(p_ref, h_ref, r_ref, o_ref):
        o_ref[...] = ((h_ref[...].astype(F32) + r_ref[0].astype(F32)) + r_ref[1].astype(F32)) + r_ref[2].astype(F32)

    return pl.pallas_call(
        body, name=name,
        grid_spec=pltpu.PrefetchScalarGridSpec(
            num_scalar_prefetch=1, grid=(nb,),
            in_specs=[pl.BlockSpec((None, tm, cols), lambda i, p: (p[0], i, 0)),
                      pl.BlockSpec((3, tm, cols), lambda i, p: (0, i, 0))],
            out_specs=pl.BlockSpec((tm, cols), lambda i, p: (p[1] * nb + i, 0))),
        out_shape=jax.ShapeDtypeStruct((2 * r2, cols), F32),
        compiler_params=_params(("arbitrary",)),
    )(place_arr, h, got)


class _ReduceScatter:
    def __init__(self, tag, names, grads):
        self.tag, self.names, self.n = tag, names, len(names)
        core = lax.axis_index("c").astype(jnp.int32)
        chip = (2 * lax.axis_index("x") + lax.axis_index("y")).astype(jnp.int32)
        self.c_arr, self.place_arr = core.reshape(1), jnp.stack([chip, core])
        self.bufs = list(grads)

    def _start(self, step, bufs, plan, count, after):
        self.plan = plan
        self.step = f"grad_{step}_{self.tag}"
        self.sems, self.bufs, token = _copies_start(self.step + "_start", bufs, plan, count, after)
        return [token]

    def _wait(self, after):
        self.bufs = _copies_wait(self.step + "_wait", self.bufs, self.sems, self.plan, after)
        return self.bufs

    def start_swap(self, after=()):
        lands = [lax.empty((N_CHIPS, g.shape[1] // 2, g.shape[2]), g.dtype) for g in self.bufs]
        return self._start("swap", self.bufs + lands, _plan_swap_halves, self.n, after)

    def start_scatter(self, after):
        bufs = self._wait(after)
        pair = [_add_sibling_half(g, r, self.c_arr, name=f"grad_add_sibling_{nm}", tm=min(256, g.shape[1] // 2))
                for nm, g, r in zip(self.names, bufs[:self.n], bufs[self.n:])]
        lands = [lax.empty((3,) + h.shape[1:], h.dtype) for h in pair]
        return self._start("scatter", pair + lands, _plan_scatter_chips, 3 * self.n, ())

    def start_join(self, after):
        bufs = self._wait(after)
        total = [_add_chips(h, r, self.place_arr, name=f"grad_add_chips_{nm}", tm=min(256, h.shape[1]))
                 for nm, h, r in zip(self.names, bufs[:self.n], bufs[self.n:])]
        return self._start("join", total, _plan_join_halves, self.n, ())

    def finish(self, after):
        return dict(zip(self.names, self._wait(after)))


def _all_gather_small(v):
    m_per, n = v.shape

    def body(x_ref, out_ref, send_sems, recv_sems, local_sem):
        x, y, c, chips = _place()
        me, sibling = (x, y, c), (x, y, 1 - c)

        def rows(px, py, pc):
            return out_ref.at[4 * px + 2 * py + pc]

        def copy(k, block, to, src=None):
            return _remote(rows(*block) if src is None else src, rows(*block), send_sems.at[k], recv_sems.at[k], to)

        mine = pltpu.make_async_copy(x_ref, rows(*me), local_sem)
        mine.start()
        first = [copy(0, me, sibling, src=x_ref)]
        first += [copy(1 + j, me, (*chip, c), src=x_ref) for j, chip in enumerate(chips)]
        for cp in first:
            cp.start()
        passed = [copy(4 + j, (*chip, c), sibling) for j, chip in enumerate(chips)]
        for j, chip in enumerate(chips):
            copy(1 + j, (*chip, c), me).wait_recv()
            passed[j].start()
        copy(0, sibling, me).wait_recv()
        for j, chip in enumerate(chips):
            copy(4 + j, (*chip, 1 - c), me).wait_recv()
        for cp in first + passed:
            cp.wait_send()
        mine.wait()

    return pl.pallas_call(
        body, name="gather_small_grads",
        out_shape=jax.ShapeDtypeStruct((8, m_per, n), v.dtype),
        in_specs=[pl.BlockSpec(memory_space=pltpu.VMEM)], out_specs=pl.BlockSpec(memory_space=pltpu.VMEM),
        scratch_shapes=[pltpu.SemaphoreType.DMA((7,)), pltpu.SemaphoreType.DMA((7,)), pltpu.SemaphoreType.DMA],
        compiler_params=pltpu.CompilerParams(vmem_limit_bytes=VMEM_LIMIT),
    )(v)


def _sum8(v, *, name):
    _, m, n = v.shape

    def body(v_ref, o_ref):
        acc = v_ref[0]
        for d in range(1, 8):
            acc = acc + v_ref[d]
        o_ref[...] = acc

    return pl.pallas_call(body, name=name, out_shape=jax.ShapeDtypeStruct((m, n), F32),
                          compiler_params=pltpu.CompilerParams(vmem_limit_bytes=VMEM_LIMIT))(v)


def _local_step(x, target, norm_w, q_norm_w, k_norm_w, sinks, a_re, a_im, log_dt, b_re, b_im, c_re, c_im, d_skip,
                b_glu, io):
    seq = x.shape[0]
    qw2 = jnp.tile(q_norm_w.reshape(1, HEAD_DIM), (1, HEADS_PER_TILE))
    kw2 = jnp.tile(k_norm_w.reshape(1, HEAD_DIM), (1, HEADS_PER_TILE))
    nw, bg = norm_w.reshape(1, D_MODEL), b_glu.reshape(1, D_MODEL)
    dsk = d_skip.reshape(1, SSM_W)

    h, rstd = _rms_fwd(x, nw, deps=io.begin())
    proj, w_in4 = io.projection(h)
    attn, lse, ya_in = _attn2_fwd(proj, qw2, kw2, sinks, deps=io.after_proj(proj))
    w_ap4 = io.weight("w_attn_proj", ya_in)
    w_glu4, w_sp4, w_out = io.weight("w_glu", ya_in), io.weight("w_ssm_proj", ya_in), io.weight("w_out", ya_in)
    y_a = _mm(ya_in, w_ap4, mode="nn", name="mm_attn_proj", tm=2048, tn=512, tk=ATTN_W, b_blocked=True,
              rows_outer=True, out_dtype=_MXU)

    flat_a = (a_re.reshape(1, N_STATES), a_im.reshape(1, N_STATES), jnp.repeat(log_dt, STATE).reshape(1, N_STATES))
    coef = _ssm_params_fwd(*flat_a)
    bre_blk, bim_blk = _block_diag_b(b_re).astype(_MXU), _block_diag_b(b_im).astype(_MXU)
    cre_blk, cim_blk = _block_diag_c(c_re).astype(_MXU), _block_diag_c(c_im).astype(_MXU)
    u_scan = _to_scan_order(proj[:, OFF_U * CW:OFF_U * CW + SSM_W])
    y_scan, yg, s_re, s_im, i_re, i_im = _ssm_fwd(u_scan, bre_blk, bim_blk, cre_blk, cim_blk, dsk, coef)
    glu, ys_in = _mm_glu_gate(yg, w_glu4, bg, proj)
    y_s = _mm(ys_in, w_sp4, mode="nn", name="mm_ssm_proj", tm=2048, tn=512, tk=SSM_W, b_blocked=True,
              rows_outer=True, out_dtype=_MXU)

    merged, dout, dout_b, sq = _mm_merge_out_loss(proj, y_a, y_s, w_out, x, target)
    loss = 0.5 * jnp.sum(sq) / D_MODEL

    d_ya, d_ys, d_proj = _mm_merge_bwd(dout_b, w_out, proj, y_a, y_s)
    g_w_out = _mm(merged, dout_b, mode="tn", name="mm_g_w_out", tm=1024, tn=D_MODEL, tk=1024, out_dtype=_WIRE)

    d_ya_in = _mm(d_ya, w_ap4, mode="nt", name="mm_d_attn_gate", tm=2048, tn=ATTN_W, tk=512, b_blocked=True)
    g_w_ap = _mm(ya_in, d_ya, mode="tn", name="mm_g_w_attn_proj", tm=ATTN_W, tn=D_MODEL, tk=2048, out_dtype=_WIRE,
                 out_blocked=True)

    g_w_sp = _mm(ys_in, d_ys, mode="tn", name="mm_g_w_ssm_proj", tm=SSM_W, tn=D_MODEL, tk=2048, out_dtype=_WIRE,
                 out_blocked=True)
    d_glu, d_proj, g_bglu = _mm_ssm_gate_bwd(d_ys, w_sp4, glu, bg, proj, d_proj)
    d_yg = _mm(d_glu, w_glu4, mode="nt", name="mm_d_gelu", tm=2048, tn=SSM_W, tk=512, b_blocked=True)
    g_w_glu = _mm(yg, d_glu, mode="tn", name="mm_g_w_glu", tm=SSM_W, tn=D_MODEL, tk=2048, out_dtype=_WIRE, out_blocked=True)
    dep = io.later_grads(dict(w_attn_proj=g_w_ap, w_glu=g_w_glu, w_ssm_proj=g_w_sp,
                              w_out=g_w_out.reshape(N_CHIPS, D_MODEL // N_CHIPS, D_MODEL)))

    d_proj, g_qw2, g_kw2, g_sk = _attn2_bwd(proj, qw2, kw2, sinks, lse, attn, d_ya_in, d_proj, deps=dep)
    dep = io.before_scan_backward([d_proj])
    (d_proj, g_bre, g_bim, g_cre, g_cim, g_dsk, g_abr, g_abi, g_cfr, g_cfi) = _ssm_bwd(
        _to_scan_order(d_yg), y_scan, u_scan, s_re, s_im, i_re, i_im, bre_blk, bim_blk, cre_blk, cim_blk, dsk, coef,
        d_proj, deps=dep)
    g_are, g_aim, g_ldt = _ssm_params_bwd(*flat_a, g_abr, g_abi, g_cfr, g_cfi)
    g_are, g_aim = g_are.reshape(N_GROUPS, STATE), g_aim.reshape(N_GROUPS, STATE)
    g_ldt = g_ldt.reshape(N_GROUPS, STATE).sum(axis=1)
    dep = io.before_input_projection_grad([d_proj]) + io.small_grads(dict(
        q_norm_w=g_qw2[0, :HEAD_DIM] + g_qw2[0, HEAD_DIM:], k_norm_w=g_kw2[0, :HEAD_DIM] + g_kw2[0, HEAD_DIM:],
        sinks=g_sk.reshape(N_Q_HEADS), A_re=g_are, A_im=g_aim, log_dt=g_ldt,
        B_re=_diag_of_b(g_bre), B_im=_diag_of_b(g_bim), C_re=_diag_of_c(g_cre), C_im=_diag_of_c(g_cim),
        D_skip=g_dsk.reshape(N_GROUPS, GROUP), b_glu=g_bglu.reshape(D_MODEL)))
    g_w_in = _mm(h, d_proj, mode="tn", name="mm_g_w_in", tm=1024, tn=IN_W // 4, tk=1024, out_dtype=_WIRE,
                 out_blocked=True, deps=dep)
    dep = io.input_projection_grad(g_w_in)
    d_h = _mm(d_proj, w_in4, mode="nt", name="mm_d_h", tm=1024, tn=D_MODEL, tk=IN_W // 4, b_blocked=True, deps=dep)
    grad_x, g_nw = _rms_bwd(d_h, x, rstd, nw, dout)
    return loss, grad_x, g_nw.reshape(D_MODEL)


_SMALL = ["norm_w", "q_norm_w", "k_norm_w", "sinks", "A_re", "A_im", "log_dt", "B_re", "B_im", "C_re", "C_im",
          "D_skip", "b_glu"]
_BIG = ["w_in", "w_attn_proj", "w_glu", "w_ssm_proj", "w_out"]
_LATER = _BIG[1:]
_RELATIONS = ("flip_x", "flip_y", "flip_xy")
_ORDER = ["norm_w", "w_in", "q_norm_w", "k_norm_w", "sinks", "w_attn_proj", "A_re", "A_im", "log_dt", "B_re", "B_im",
          "C_re", "C_im", "D_skip", "w_glu", "b_glu", "w_ssm_proj", "w_out"]
_PACK_W = 1024


def _packed_rows(size):
    unit = SUBLANES * _PACK_W
    return -(-size // unit) * SUBLANES


def _pack_small(d, names):
    parts = []
    for n in names:
        flat = d[n].reshape(-1).astype(F32)
        rows = _packed_rows(flat.shape[0])
        parts.append(jnp.pad(flat, (0, rows * _PACK_W - flat.shape[0])).reshape(rows, _PACK_W))
    return jnp.concatenate(parts, axis=0)


def _unpack_small(packed, like, names):
    out, pos = {}, 0
    for n in names:
        rows = _packed_rows(like[n].size)
        out[n] = packed[pos:pos + rows].reshape(-1)[:like[n].size].reshape(like[n].shape)
        pos += rows
    return out


def _place_block(v, index_arr, *, name):
    rows, cols = v.shape

    def body(i_ref, v_ref, o_ref):
        o_ref[...] = v_ref[...]

    return pl.pallas_call(
        body, name=name,
        grid_spec=pltpu.PrefetchScalarGridSpec(
            num_scalar_prefetch=1, grid=(1,),
            in_specs=[pl.BlockSpec((rows, cols), lambda i, d: (0, 0))],
            out_specs=pl.BlockSpec((None, rows, cols), lambda i, d: (d[0], 0, 0))),
        out_shape=jax.ShapeDtypeStruct((8, rows, cols), v.dtype),
        compiler_params=_params(("arbitrary",)),
    )(index_arr, v)


def _plan_all_to_all(refs):
    (land,) = refs
    x, y, c, _ = _place()
    own = land.at[4 * x + 2 * y + c]
    copies = []
    for fx, fy, fc in [(0, 0, 1), (0, 1, 0), (0, 1, 1), (1, 0, 0), (1, 0, 1), (1, 1, 0), (1, 1, 1)]:
        px, py, pc = (1 - x) if fx else x, (1 - y) if fy else y, (1 - c) if fc else c
        copies.append((own, own, (px, py, pc), land.at[4 * px + 2 * py + pc]))
    return copies


def _adamw_whole(w, g, m, v, *, name):
    def body(w_ref, g_ref, m_ref, v_ref, d_ref, nm_ref, nv_ref):
        d_ref[...], nm_ref[...], nv_ref[...] = _adamw_math(w_ref[...], g_ref[...], m_ref[...], v_ref[...])

    return pl.pallas_call(body, name=name, out_shape=[jax.ShapeDtypeStruct(w.shape, F32)] * 3)(w, g, m, v)


class _Exchanges:
    def __init__(self, w, m, v):
        self.w, self.m, self.v = w, m, v
        self.grads, self.delta, self.new_m, self.new_v = {}, {}, {}, {}

    def _adamw(self, names, deps):
        for n in names:
            self.delta[n], self.new_m[n], self.new_v[n] = _adamw(
                self.w[n], self.grads[n], self.m[n], self.v[n], name=f"adamw_{n}", tm=128, deps=deps)

    def begin(self):
        chip = (2 * lax.axis_index("x") + lax.axis_index("y")).astype(jnp.int32).reshape(1)
        w_in = _place_shard(self.w["w_in"], chip, name="place_w_in")
        self.w_in_sems, self.w_in_buf, token = _copies_start("gather_w_in_direct_start", [w_in], _plan_relay_direct, 2)
        self.later_full = [_place_shard(self.w[n], chip, name=f"place_{n}", deps=[token]) for n in _LATER]
        return self.later_full

    def projection(self, h):
        x, y = lax.axis_index("x"), lax.axis_index("y")
        blks = [jnp.asarray(b, jnp.int32).reshape(1)
                for b in (2 * x + y, 2 * (1 - x) + y, 2 * x + (1 - y), 2 * (1 - x) + (1 - y))]
        bufs = self.w_in_buf
        proj = _mm_chip_block(h, bufs[0], blks[0], None, name="mm_proj_own")
        relay, token = [], proj
        for k, tag in enumerate(_RELATIONS[:2]):
            bufs = _copies_wait(f"gather_w_in_direct_{tag}_wait", bufs, self.w_in_sems, _plan_relay_direct, [token],
                                which=(k,))
            plan = functools.partial(_plan_relay_forward, k=k)
            sems, bufs, token = _copies_start(f"gather_w_in_relay_{tag}_start", bufs, plan, 2)
            relay.append((sems, plan))
        self.rest = _copies_start("gather_ici_rest_start", self.later_full, _plan_gather_ici, 3 * len(_LATER),
                                  after=[token])
        token = self.rest[2]
        for k, tag in enumerate(_RELATIONS[:2]):
            bufs = _copies_wait(f"gather_w_in_hand_{tag}_wait", bufs, relay[k][0], relay[k][1], [token], which=(1,))
            token = proj = _mm_chip_block(h, bufs[0], blks[1 + k], proj, name=f"mm_proj_{tag}")
        for k, tag in enumerate(_RELATIONS[:2]):
            bufs = _copies_wait(f"gather_w_in_relay_{tag}_wait", bufs, relay[k][0], relay[k][1], [token], which=(0,))
        sems, bufs, token = _copies_start("gather_w_in_last_start", bufs, _plan_relay_last, 1)
        bufs = _copies_wait("gather_w_in_last_wait", bufs, sems, _plan_relay_last, [token])
        proj = _mm_chip_block(h, bufs[0], blks[3], proj, name="mm_proj_flip_xy")
        return proj, bufs[0]

    def weight(self, name, after):
        if self.rest is not None:
            sems, bufs = self.rest
            later = dict(zip(_LATER, _copies_wait("gather_d2d_rest_wait", bufs, sems, _plan_gather_d2d, [after])))
            later["w_out"] = later["w_out"].reshape(D_MODEL, D_MODEL)
            self.later, self.rest = later, None
        return self.later[name]

    def after_proj(self, proj):
        sems, bufs, _ = self.rest
        bufs = _copies_wait("gather_ici_rest_wait", bufs, sems, _plan_gather_ici, [proj])
        sems, bufs, token = _copies_start("gather_d2d_rest_start", bufs, _plan_gather_d2d, 3 * len(_LATER))
        self.rest = (sems, bufs)
        return [token]

    def later_grads(self, grads):
        self.rs_later = _ReduceScatter("later", _LATER, [grads[n] for n in _LATER])
        return self.rs_later.start_swap()

    def before_scan_backward(self, after):
        return self.rs_later.start_scatter(after)

    def before_input_projection_grad(self, after):
        return self.rs_later.start_join(after)

    def input_projection_grad(self, g_w_in):
        self.grads.update(self.rs_later.finish([g_w_in]))
        self.rs_in = _ReduceScatter("w_in", ["w_in"], [g_w_in])
        self._adamw(_LATER, self.rs_in.start_swap())
        return self.rs_in.start_scatter([self.delta[n] for n in _LATER])

    def _adamw_small(self, names):
        for n in names:
            self.delta[n], self.new_m[n], self.new_v[n] = _adamw_whole(
                self.w[n], self.grads[n], self.m[n], self.v[n], name=f"adamw_{n}")

    def small_grads(self, grads):
        me = (4 * lax.axis_index("x") + 2 * lax.axis_index("y") + lax.axis_index("c")).astype(jnp.int32).reshape(1)
        land = _place_block(_pack_small(grads, _SMALL[1:]), me, name="place_small_grads")
        self.small = _copies_start("gather_small_start", [land], _plan_all_to_all, 7)
        return [self.small[2]]

    def finish(self, g_norm_w, loss, after):
        join = self.rs_in.start_join(after)
        sems, bufs, _ = self.small
        (land,) = _copies_wait("gather_small_wait", bufs, sems, _plan_all_to_all, join)
        self.grads.update(_unpack_small(_sum8(land, name="sum_small_grads"), self.w, _SMALL[1:]))
        self._adamw_small(_SMALL[1:])
        rows = _packed_rows(g_norm_w.size)
        late = jnp.concatenate([_pack_small(dict(norm_w=g_norm_w), _SMALL[:1]),
                                jnp.pad(loss.reshape(1, 1), ((0, SUBLANES - 1), (0, _PACK_W - 1)))], axis=0)
        late = _sum8(_all_gather_small(late), name="sum_norm_w_grad_and_loss")
        self.grads.update(_unpack_small(late[:rows], self.w, _SMALL[:1]))
        self._adamw_small(_SMALL[:1])
        self.grads.update(self.rs_in.finish([self.delta[_SMALL[0]]]))
        self._adamw(["w_in"], ())
        return late[rows, 0]


def kernel(x, norm_w, w_in, q_norm_w, k_norm_w, sinks, w_attn_proj, A_re, A_im, log_dt, B_re, B_im, C_re, C_im, D_skip, w_glu, b_glu, w_ssm_proj, w_out, loss_target, m_norm_w, m_w_in, m_q_norm_w, m_k_norm_w, m_sinks, m_w_attn_proj, m_A_re, m_A_im, m_log_dt, m_B_re, m_B_im, m_C_re, m_C_im, m_D_skip, m_w_glu, m_b_glu, m_w_ssm_proj, m_w_out, v_norm_w, v_w_in, v_q_norm_w, v_k_norm_w, v_sinks, v_w_attn_proj, v_A_re, v_A_im, v_log_dt, v_B_re, v_B_im, v_C_re, v_C_im, v_D_skip, v_w_glu, v_b_glu, v_w_ssm_proj, v_w_out):
    w = dict(norm_w=norm_w, w_in=w_in, q_norm_w=q_norm_w, k_norm_w=k_norm_w, sinks=sinks, w_attn_proj=w_attn_proj,
             A_re=A_re, A_im=A_im, log_dt=log_dt, B_re=B_re, B_im=B_im, C_re=C_re, C_im=C_im, D_skip=D_skip,
             w_glu=w_glu, b_glu=b_glu, w_ssm_proj=w_ssm_proj, w_out=w_out)
    m = dict(norm_w=m_norm_w, w_in=m_w_in, q_norm_w=m_q_norm_w, k_norm_w=m_k_norm_w, sinks=m_sinks,
             w_attn_proj=m_w_attn_proj, A_re=m_A_re, A_im=m_A_im, log_dt=m_log_dt, B_re=m_B_re, B_im=m_B_im,
             C_re=m_C_re, C_im=m_C_im, D_skip=m_D_skip, w_glu=m_w_glu, b_glu=m_b_glu, w_ssm_proj=m_w_ssm_proj,
             w_out=m_w_out)
    v = dict(norm_w=v_norm_w, w_in=v_w_in, q_norm_w=v_q_norm_w, k_norm_w=v_k_norm_w, sinks=v_sinks,
             w_attn_proj=v_w_attn_proj, A_re=v_A_re, A_im=v_A_im, log_dt=v_log_dt, B_re=v_B_re, B_im=v_B_im,
             C_re=v_C_re, C_im=v_C_im, D_skip=v_D_skip, w_glu=v_w_glu, b_glu=v_b_glu, w_ssm_proj=v_w_ssm_proj,
             w_out=v_w_out)

    io = _Exchanges(w, m, v)
    loss, grad_x, g_norm_w = _local_step(x[0], loss_target[0], norm_w, q_norm_w, k_norm_w, sinks, A_re, A_im, log_dt,
                                         B_re, B_im, C_re, C_im, D_skip, b_glu, io)
    loss = io.finish(g_norm_w, loss, [grad_x])
    grads, delta, new_m, new_v = io.grads, io.delta, io.new_m, io.new_v

    return (loss, grad_x[None], *[grads[n] for n in _ORDER], *[delta[n] for n in _ORDER],
            *[new_m[n] for n in _ORDER], *[new_v[n] for n in _ORDER])
```

```python
import functools
import math

import jax
import jax.numpy as jnp
from jax import lax
from jax.experimental import pallas as pl
from jax.experimental.pallas import tpu as pltpu

F32 = jnp.float32
_MXU = jnp.bfloat16
_WIRE = jnp.bfloat16

LANES = 128
SUBLANES = 8
VMEM_LIMIT = 56 * 1024 * 1024

D_MODEL = 2048
HEAD_DIM = 64
N_Q_HEADS = 16
N_KV_HEADS = 4
Q_PER_KV = 4
ATTN_W = 1024
KV_W = 256
WINDOW = 128
SSM_W = 1024
GROUP = 16
N_GROUPS = 64
STATE = 64
N_STATES = N_GROUPS * STATE
IN_W = 8704
NORM_EPS = 1e-6
N_CHIPS = 4
CW = 512
OFF_AGATE, OFF_U, OFF_Z, OFF_GA, OFF_GS = 3, 5, 7, 9, 13

SSM_T = 256
SSM_L = SSM_T // SUBLANES
SSM_JB = 8
SSM_SB = N_STATES // SSM_JB

ADAM_LR, ADAM_B1, ADAM_B2, ADAM_EPS, ADAM_WD, ADAM_STEP = 0.001, 0.9, 0.999, 1e-08, 0.01, 10

MESH = pl.DeviceIdType.MESH
_ANY = pl.BlockSpec(memory_space=pl.ANY)


def _params(sem=None):
    return pltpu.CompilerParams(dimension_semantics=sem, vmem_limit_bytes=VMEM_LIMIT)


def _mm(a, b, *, mode, name, tm, tn, tk, out_dtype=F32, b_blocked=False, out_blocked=False, rows_outer=False,
        deps=()):
    nd = len(deps)
    if mode == "tn":
        K, M = a.shape
    else:
        M, K = a.shape
    if mode == "nn":
        N = b.shape[0] * b.shape[2] if b_blocked else b.shape[1]
    elif mode == "nt":
        N = b.shape[1] if b_blocked else b.shape[0]
    else:
        N = b.shape[1]
    tm, tn, tk = min(tm, M), min(tn, N), min(tk, K)
    nj, ni, nk = N // tn, M // tm, K // tk
    assert nj * tn == N and ni * tm == M and nk * tk == K, (name, M, N, K)
    dims = {"nn": (((1,), (0,)), ((), ())), "nt": (((1,), (1,)), ((), ())), "tn": (((0,), (0,)), ((), ()))}[mode]

    if mode == "tn":
        a_spec = pl.BlockSpec((tk, tm), lambda j, i, k: (k, i))
    else:
        a_spec = pl.BlockSpec((tm, tk), lambda j, i, k: (i, k))
    if mode == "nn":
        if b_blocked:
            assert b.shape[0] == nj and b.shape[2] == tn
            b_spec = pl.BlockSpec((None, tk, tn), lambda j, i, k: (j, k, 0))
        else:
            b_spec = pl.BlockSpec((tk, tn), lambda j, i, k: (k, j))
    elif mode == "nt":
        if b_blocked:
            assert b.shape[0] == nk and b.shape[2] == tk
            b_spec = pl.BlockSpec((None, tn, tk), lambda j, i, k: (k, j, 0))
        else:
            b_spec = pl.BlockSpec((tn, tk), lambda j, i, k: (j, k))
    else:
        b_spec = pl.BlockSpec((tk, tn), lambda j, i, k: (k, j))
    whole_out = out_blocked and nj == 1
    if whole_out:
        assert ni == 1
        o_spec = pl.BlockSpec((N_CHIPS, tm, tn // N_CHIPS), lambda j, i, k: (0, 0, 0))
        o_shape = jax.ShapeDtypeStruct((N_CHIPS, M, tn // N_CHIPS), out_dtype)
    elif out_blocked:
        assert nj == N_CHIPS
        o_spec = pl.BlockSpec((None, tm, tn), lambda j, i, k: (j, i, 0))
        o_shape = jax.ShapeDtypeStruct((nj, M, tn), out_dtype)
    else:
        o_spec = pl.BlockSpec((tm, tn), lambda j, i, k: (i, j))
        o_shape = jax.ShapeDtypeStruct((M, N), out_dtype)
    use_acc = nk > 1 and (out_dtype != F32 or whole_out)

    def body(a_ref, b_ref, *rest):
        o_ref, scratch = rest[nd], rest[nd + 1:]

        def product():
            return lax.dot_general(a_ref[...].astype(_MXU), b_ref[...].astype(_MXU), dims, preferred_element_type=F32)

        def write(result):
            if whole_out:
                w = tn // N_CHIPS
                for c in range(N_CHIPS):
                    o_ref[c] = result[:, c * w:(c + 1) * w].astype(o_ref.dtype)
            else:
                o_ref[...] = result.astype(o_ref.dtype)

        if nk == 1:
            write(product())
            return
        k = pl.program_id(2)
        acc = scratch[0] if use_acc else o_ref

        @pl.when(k == 0)
        def _():
            acc[...] = jnp.zeros_like(acc)

        acc[...] += product()

        if use_acc:
            @pl.when(k == nk - 1)
            def _():
                write(acc[...])

    specs = [a_spec, b_spec, o_spec]
    grid = (nj, ni, nk)
    if rows_outer:
        specs = [pl.BlockSpec(s.block_shape, lambda i, j, k, f=s.index_map: f(j, i, k)) for s in specs]
        grid = (ni, nj, nk)
    return pl.pallas_call(
        body, name=name, grid=grid, in_specs=specs[:2] + [_ANY] * nd, out_specs=specs[2],
        out_shape=o_shape, scratch_shapes=[pltpu.VMEM((tm, tn), F32)] if use_acc else [],
        compiler_params=_params(("parallel", "parallel", "arbitrary")),
    )(a, b, *deps)


def _mm_chip_block(a, b4, blk, prev, *, name, tm=512, out_dtype=F32, deps=()):
    M, K = a.shape
    nchip, _, C = b4.shape
    tm = min(tm, M)
    extra = ([] if prev is None else [prev]) + list(deps)

    def body(blk_ref, a_ref, b_ref, *rest):
        rest[-1][...] = jnp.dot(a_ref[...].astype(_MXU), b_ref[...].astype(_MXU),
                                preferred_element_type=F32).astype(rest[-1].dtype)

    return pl.pallas_call(
        body, name=name,
        grid_spec=pltpu.PrefetchScalarGridSpec(
            num_scalar_prefetch=1, grid=(M // tm,),
            in_specs=[pl.BlockSpec((tm, K), lambda i, c: (i, 0)), pl.BlockSpec((None, K, C), lambda i, c: (c[0], 0, 0))]
            + [_ANY] * len(extra),
            out_specs=pl.BlockSpec((tm, C), lambda i, c: (i, c[0]))),
        out_shape=jax.ShapeDtypeStruct((M, nchip * C), out_dtype),
        input_output_aliases={} if prev is None else {3: 0},
        compiler_params=_params(("arbitrary",)),
    )(blk, a, b4, *extra)


def _mm_merge_out_loss(proj, y_a, y_s, w_out, x, target, *, tm=256):
    rows, d = x.shape
    ncol = d // CW

    def body(*refs):
        ga_refs, gs_refs = refs[:ncol], refs[ncol:2 * ncol]
        ya_ref, ys_ref, w_ref, x_ref, t_ref, m_ref, d_ref, db_ref, sq_ref = refs[2 * ncol:]
        for j in range(ncol):
            cols = slice(j * CW, (j + 1) * CW)
            m_ref[:, cols] = (_sigmoid(ga_refs[j][...].astype(F32)) * ya_ref[:, cols].astype(F32)
                              + _sigmoid(gs_refs[j][...].astype(F32)) * ys_ref[:, cols].astype(F32)).astype(m_ref.dtype)
        mo = jnp.dot(m_ref[...], w_ref[...].astype(_MXU), preferred_element_type=F32)
        err = (x_ref[...] + mo) - t_ref[...]
        dout = err * (1.0 / d)
        d_ref[...] = dout
        db_ref[...] = dout.astype(db_ref.dtype)
        part = _colsum(err * err)
        i = pl.program_id(0)

        @pl.when(i == 0)
        def _():
            sq_ref[...] = part

        @pl.when(i > 0)
        def _():
            sq_ref[...] += part

    tile = pl.BlockSpec((tm, d), lambda i: (i, 0))
    gate = [pl.BlockSpec((tm, CW), lambda i, c=off + j: (i, c)) for off in (OFF_GA, OFF_GS) for j in range(ncol)]
    return pl.pallas_call(
        body, name="mm_merge_out_loss", grid=(rows // tm,),
        in_specs=gate + [tile, tile, pl.BlockSpec((d, d), lambda i: (0, 0), pipeline_mode=pl.Buffered(1)), tile, tile],
        out_specs=[tile, tile, tile, pl.BlockSpec((1, d), lambda i: (0, 0))],
        out_shape=[jax.ShapeDtypeStruct((rows, d), _MXU), jax.ShapeDtypeStruct((rows, d), F32),
                   jax.ShapeDtypeStruct((rows, d), _MXU), jax.ShapeDtypeStruct((1, d), F32)],
        compiler_params=_params(("arbitrary",)),
    )(*([proj] * (2 * ncol)), y_a, y_s, w_out, x, target)


def _mm_merge_bwd(dout_b, w_out, proj, y_a, y_s, *, tm=256):
    rows, d = y_a.shape
    ncol = d // CW

    def body(do_ref, w_ref, *refs):
        ga_refs, gs_refs = refs[:ncol], refs[ncol:2 * ncol]
        ya_ref, ys_ref, dya_ref, dys_ref, dg_ref = refs[2 * ncol:]
        dm = lax.dot_general(do_ref[...].astype(_MXU), w_ref[...].astype(_MXU), _NT, preferred_element_type=F32)
        for j in range(ncol):
            cols = slice(j * CW, (j + 1) * CW)
            dmj = dm[:, cols]
            sa, ss = _sigmoid(ga_refs[j][...].astype(F32)), _sigmoid(gs_refs[j][...].astype(F32))
            dya_ref[:, cols] = (sa * dmj).astype(dya_ref.dtype)
            dys_ref[:, cols] = (ss * dmj).astype(dys_ref.dtype)
            dg_ref[:, cols] = (dmj * ya_ref[:, cols].astype(F32) * sa * (1.0 - sa)).astype(dg_ref.dtype)
            dg_ref[:, d + j * CW:d + (j + 1) * CW] = (dmj * ys_ref[:, cols].astype(F32) * ss
                                                      * (1.0 - ss)).astype(dg_ref.dtype)

    tile = pl.BlockSpec((tm, d), lambda i: (i, 0))
    gate = [pl.BlockSpec((tm, CW), lambda i, c=off + j: (i, c)) for off in (OFF_GA, OFF_GS) for j in range(ncol)]
    both = pl.BlockSpec((pl.Element(tm), pl.Element(2 * d)), lambda i: (i * tm, OFF_GA * CW))
    return pl.pallas_call(
        body, name="mm_merge_bwd", grid=(rows // tm,),
        in_specs=[tile, pl.BlockSpec((d, d), lambda i: (0, 0))] + gate + [tile, tile],
        out_specs=[tile, tile, both],
        out_shape=[jax.ShapeDtypeStruct((rows, d), _MXU)] * 2 + [jax.ShapeDtypeStruct((rows, IN_W), _MXU)],
        compiler_params=_params(("arbitrary",)),
    )(dout_b, w_out, *([proj] * (2 * ncol)), y_a, y_s)


def _mm_glu_gate(yg, w_glu4, b_glu, proj, *, tm=1024):
    rows, k = yg.shape
    nj, _, tn = w_glu4.shape
    w = nj * tn // 2
    tm = min(tm, rows)

    def body(a_ref, w_ref, ba_ref, bb_ref, z0_ref, z1_ref, glu_ref, ys_ref):
        j = pl.program_id(1)
        for c in range(nj):
            @pl.when(j == c)
            def _(c=c):
                glu_ref[:, c * tn:(c + 1) * tn] = jnp.dot(a_ref[...].astype(_MXU), w_ref[...].astype(_MXU),
                                                          preferred_element_type=F32)

        @pl.when(j == nj - 1)
        def _():
            z = jnp.concatenate([z0_ref[...], z1_ref[...]], axis=1).astype(F32)
            ys_ref[...] = ((glu_ref[:, :w] + ba_ref[...]) * _sigmoid(glu_ref[:, w:] + bb_ref[...])
                           * (z * _sigmoid(z))).astype(ys_ref.dtype)

    bias = lambda c: pl.BlockSpec((1, w), lambda i, j: (0, c))
    zcol = lambda c: pl.BlockSpec((tm, CW), lambda i, j: (i, OFF_Z + c))
    return pl.pallas_call(
        body, name="mm_glu_gate", grid=(rows // tm, nj),
        in_specs=[pl.BlockSpec((tm, k), lambda i, j: (i, 0)), pl.BlockSpec((None, k, tn), lambda i, j: (j, 0, 0)),
                  bias(0), bias(1), zcol(0), zcol(1)],
        out_specs=[pl.BlockSpec((tm, nj * tn), lambda i, j: (i, 0)), pl.BlockSpec((tm, w), lambda i, j: (i, 0))],
        out_shape=[jax.ShapeDtypeStruct((rows, nj * tn), F32), jax.ShapeDtypeStruct((rows, w), _MXU)],
        compiler_params=_params(("arbitrary", "arbitrary")),
    )(yg, w_glu4, b_glu, b_glu, proj, proj)


def _mm_ssm_gate_bwd(d_ys, w_sp4, glu, b_glu, proj, d_proj, *, tm=512):
    rows, w = glu.shape[0], glu.shape[1] // 2
    nk, tk = w_sp4.shape[0], w_sp4.shape[2]
    tm = min(tm, rows)

    def body(dy_ref, w_ref, ga_ref, gb_ref, ba_ref, bb_ref, z0_ref, z1_ref, buf_ref, dg_ref, dz_ref, db_ref, acc):
        i, k = pl.program_id(0), pl.program_id(1)

        @pl.when(k == 0)
        def _():
            acc[...] = jnp.zeros_like(acc)

        acc[...] += lax.dot_general(dy_ref[...].astype(_MXU), w_ref[...].astype(_MXU), _NT, preferred_element_type=F32)

        @pl.when(k == nk - 1)
        def _():
            dv = acc[...]
            a, sb = ga_ref[...] + ba_ref[...], _sigmoid(gb_ref[...] + bb_ref[...])
            f, df = _silu_and_grad(jnp.concatenate([z0_ref[...], z1_ref[...]], axis=1).astype(F32))
            dga = dv * sb * f
            dgb = dv * a * f * sb * (1.0 - sb)
            dg_ref[:, :w] = dga.astype(dg_ref.dtype)
            dg_ref[:, w:] = dgb.astype(dg_ref.dtype)
            dz_ref[...] = (dv * a * sb * df).astype(dz_ref.dtype)
            part = jnp.concatenate([_colsum(dga), _colsum(dgb)], axis=1)

            @pl.when(i == 0)
            def _():
                db_ref[...] = part

            @pl.when(i > 0)
            def _():
                db_ref[...] += part

    half = lambda c: pl.BlockSpec((tm, w), lambda i, k: (i, c))
    bias = lambda c: pl.BlockSpec((1, w), lambda i, k: (0, c))
    zcol = lambda c: pl.BlockSpec((tm, CW), lambda i, k: (i, OFF_Z + c))
    return pl.pallas_call(
        body, name="mm_ssm_gate_bwd", grid=(rows // tm, nk),
        in_specs=[pl.BlockSpec((tm, tk), lambda i, k: (i, k)), pl.BlockSpec((None, w, tk), lambda i, k: (k, 0, 0)),
                  half(0), half(1), bias(0), bias(1), zcol(0), zcol(1), _ANY],
        out_specs=[pl.BlockSpec((tm, 2 * w), lambda i, k: (i, 0)),
                   pl.BlockSpec((pl.Element(tm), pl.Element(w)), lambda i, k: (i * tm, OFF_Z * CW)),
                   pl.BlockSpec((1, 2 * w), lambda i, k: (0, 0))],
        out_shape=[jax.ShapeDtypeStruct((rows, 2 * w), _MXU), jax.ShapeDtypeStruct(d_proj.shape, d_proj.dtype),
                   jax.ShapeDtypeStruct((1, 2 * w), F32)],
        input_output_aliases={8: 1},
        scratch_shapes=[pltpu.VMEM((tm, w), F32)],
        compiler_params=_params(("arbitrary", "arbitrary")),
    )(d_ys, w_sp4, glu, glu, b_glu, b_glu, proj, proj, d_proj)


def _ew(fn, ins, outs, *, rows, ncol, name, n_acc=0, tm=512, deps=(), into=None):
    deps = list(deps) + ([into[1]] if into else [])
    n_in, n_out, nd = len(ins), len(outs), len(deps)
    tm = min(tm, rows)
    in_specs = []
    for _, kind, col0 in ins:
        if kind == "mat":
            in_specs.append(pl.BlockSpec((tm, CW), lambda j, i, c0=col0: (i, c0 + j)))
        else:
            in_specs.append(pl.BlockSpec((1, CW), lambda j, i, c0=col0: (0, c0 + j)))
    out_specs = [pl.BlockSpec((tm, CW), lambda j, i: (i, j)) for _ in outs]
    out_shape = [jax.ShapeDtypeStruct((rows, w), dt) for w, dt in outs]
    if into:
        out_specs[into[0]] = pl.BlockSpec((tm, CW), lambda j, i, c0=into[2]: (i, c0 + j))
        out_shape[into[0]] = jax.ShapeDtypeStruct(into[1].shape, into[1].dtype)
    for _ in range(n_acc):
        out_specs.append(pl.BlockSpec((1, CW), lambda j, i: (0, j)))
        out_shape.append(jax.ShapeDtypeStruct((1, ncol * CW), F32))

    def body(*refs):
        vals = fn(*[r[...] for r in refs[:n_in]])
        refs = refs[n_in + nd:]
        for r, v in zip(refs[:n_out], vals[:n_out]):
            r[...] = v.astype(r.dtype)
        i = pl.program_id(1)
        for r, v in zip(refs[n_out:], vals[n_out:]):
            @pl.when(i == 0)
            def _(r=r, v=v):
                r[...] = v

            @pl.when(i > 0)
            def _(r=r, v=v):
                r[...] += v

    res = pl.pallas_call(
        body, name=name, grid=(ncol, rows // tm), in_specs=in_specs + [_ANY] * nd, out_specs=out_specs,
        out_shape=out_shape, input_output_aliases={n_in + nd - 1: into[0]} if into else {},
        compiler_params=_params(("parallel", "arbitrary")),
    )(*[a for a, _, _ in ins], *deps)
    return res


def _colsum(v):
    return jnp.sum(v, axis=0, keepdims=True)


def _sigmoid(v):
    return jax.nn.sigmoid(v)


def _silu_and_grad(v):
    s = _sigmoid(v)
    return v * s, s * (1.0 + v * (1.0 - s))


def _rms_fwd(x, w, *, tm=512, deps=()):
    rows, d = x.shape
    nd = len(deps)

    def body(x_ref, w_ref, *rest):
        h_ref, r_ref = rest[nd:]
        xv = x_ref[...]
        r = lax.rsqrt(jnp.mean(xv * xv, axis=-1, keepdims=True) + NORM_EPS)
        h_ref[...] = (xv * r * w_ref[...]).astype(h_ref.dtype)
        r_ref[...] = r

    return pl.pallas_call(
        body, name="rms_fwd", grid=(rows // tm,),
        in_specs=[pl.BlockSpec((tm, d), lambda i: (i, 0)), pl.BlockSpec((1, d), lambda i: (0, 0))] + [_ANY] * nd,
        out_specs=[pl.BlockSpec((tm, d), lambda i: (i, 0)), pl.BlockSpec((tm, 1), lambda i: (i, 0))],
        out_shape=[jax.ShapeDtypeStruct((rows, d), _MXU), jax.ShapeDtypeStruct((rows, 1), F32)],
        compiler_params=_params(("arbitrary",)),
    )(x, w, *deps)


def _rms_bwd(dh, x, rstd, w, dout, *, tm=256):
    rows, d = x.shape

    def body(dh_ref, x_ref, r_ref, w_ref, do_ref, gx_ref, gw_ref):
        dhv, xv, r, wv = dh_ref[...], x_ref[...], r_ref[...], w_ref[...]
        xr = xv * r
        t = jnp.mean(dhv * wv * xr, axis=-1, keepdims=True)
        gx_ref[...] = do_ref[...] + r * (wv * dhv - xr * t)
        part = _colsum(dhv * xr)
        i = pl.program_id(0)

        @pl.when(i == 0)
        def _():
            gw_ref[...] = part

        @pl.when(i > 0)
        def _():
            gw_ref[...] += part

    return pl.pallas_call(
        body, name="rms_bwd", grid=(rows // tm,),
        in_specs=[pl.BlockSpec((tm, d), lambda i: (i, 0)), pl.BlockSpec((tm, d), lambda i: (i, 0)),
                  pl.BlockSpec((tm, 1), lambda i: (i, 0)), pl.BlockSpec((1, d), lambda i: (0, 0)),
                  pl.BlockSpec((tm, d), lambda i: (i, 0))],
        out_specs=[pl.BlockSpec((tm, d), lambda i: (i, 0)), pl.BlockSpec((1, d), lambda i: (0, 0))],
        out_shape=[jax.ShapeDtypeStruct((rows, d), F32), jax.ShapeDtypeStruct((1, d), F32)],
        compiler_params=_params(("arbitrary",)),
    )(dh, x, rstd, w, dout)


_NT = (((1,), (1,)), ((), ()))
_TN = (((0,), (0,)), ((), ()))


QKV_W = ATTN_W + 2 * KV_W
HEADS_PER_TILE = LANES // HEAD_DIM


def _low_half(rows):
    return lax.broadcasted_iota(jnp.int32, (rows, LANES), 1) < HEAD_DIM


def _pair_mean(t, low):
    m_lo = jnp.sum(jnp.where(low, t, 0.0), axis=-1, keepdims=True)
    m_hi = jnp.sum(jnp.where(low, 0.0, t), axis=-1, keepdims=True)
    return jnp.where(low, m_lo, m_hi) * (1.0 / HEAD_DIM)


def _pair_rstd(t, low):
    return lax.rsqrt(_pair_mean(t * t, low) + NORM_EPS)


def _dup_half(t, hi, low):
    swapped = pltpu.roll(t, HEAD_DIM, 1)
    return jnp.where(low, swapped, t) if hi else jnp.where(low, t, swapped)


def _fold_halves(t):
    return t + pltpu.roll(t, HEAD_DIM, 1)


def _split_heads(t, low):
    return [jnp.where(low, t, 0.0), jnp.where(low, 0.0, t)]


def _stacked_band_mask(n):
    rows = Q_PER_KV * WINDOW
    qi = lax.broadcasted_iota(jnp.int32, (rows, 2 * WINDOW), 0) % WINDOW + WINDOW
    kj = lax.broadcasted_iota(jnp.int32, (rows, 2 * WINDOW), 1)
    diff = qi - kj
    first_key = jnp.where(n > 0, 0, WINDOW)
    return (diff >= 0) & (diff < WINDOW) & (kj >= first_key)


def _stacked_sinks(sink_ref, g):
    blk = lax.broadcasted_iota(jnp.int32, (Q_PER_KV * WINDOW, 1), 0) // WINDOW
    col = jnp.full((Q_PER_KV * WINDOW, 1), sink_ref[Q_PER_KV * g], F32)
    for r in range(1, Q_PER_KV):
        col = jnp.where(blk == r, sink_ref[Q_PER_KV * g + r], col)
    return col


def _attn_in_specs(nblk, rev):
    def cur(n):
        return (nblk - 1 - n) if rev else n

    q_spec = pl.BlockSpec((WINDOW, ATTN_W), lambda n: (cur(n), 0))
    kvc_spec = pl.BlockSpec((WINDOW, 2 * KV_W), lambda n: (cur(n), ATTN_W // (2 * KV_W)))
    kvp_spec = pl.BlockSpec((WINDOW, 2 * KV_W), lambda n: (jnp.maximum(cur(n) - 1, 0), ATTN_W // (2 * KV_W)))
    w_spec = pl.BlockSpec((1, LANES), lambda n: (0, 0))
    l_spec = pl.BlockSpec((WINDOW, N_Q_HEADS), lambda n: (cur(n), 0))
    gate_specs = [pl.BlockSpec((WINDOW, CW), lambda n, col=OFF_AGATE + j: (cur(n), col)) for j in range(ATTN_W // CW)]
    return q_spec, kvc_spec, kvp_spec, w_spec, l_spec, gate_specs


def _attn2_fwd(proj, qw2, kw2, sinks, deps=()):
    seq = proj.shape[0]
    nblk = seq // WINDOW
    scale = 1.0 / math.sqrt(HEAD_DIM)
    q_spec, kvc_spec, kvp_spec, w_spec, l_spec, gate_specs = _attn_in_specs(nblk, False)
    nd, ng = len(deps), len(gate_specs)

    def body(sink_ref, q_ref, kvc_ref, kvp_ref, qw_ref, kw_ref, *rest):
        gate_refs = rest[:ng]
        o_ref, lse_ref, ya_ref = rest[ng + nd:]
        n = pl.program_id(0)
        low, low2 = _low_half(WINDOW), _low_half(2 * WINDOW)
        valid = _stacked_band_mask(n)
        head_lane = lax.broadcasted_iota(jnp.int32, (WINDOW, N_Q_HEADS), 1)
        kv = jnp.concatenate([kvp_ref[...], kvc_ref[...]], axis=0).astype(F32)
        qwv, kwv = qw_ref[...], kw_ref[...]
        lse_blk = jnp.zeros((WINDOW, N_Q_HEADS), F32)
        for t in range(N_KV_HEADS // HEADS_PER_TILE):
            kt = kv[:, t * LANES:(t + 1) * LANES]
            vt = kv[:, KV_W + t * LANES:KV_W + (t + 1) * LANES]
            kn = kt * _pair_rstd(kt, low2) * kwv
            for hi in range(HEADS_PER_TILE):
                g = HEADS_PER_TILE * t + hi
                kdup = _dup_half(kn, hi, low2).astype(_MXU)
                vdup = _dup_half(vt, hi, low2).astype(_MXU)
                stack = []
                for tq in (2 * g, 2 * g + 1):
                    qt = q_ref[:, tq * LANES:(tq + 1) * LANES].astype(F32)
                    stack += _split_heads(qt * _pair_rstd(qt, low) * qwv, low)
                qs = jnp.concatenate(stack, axis=0).astype(_MXU)
                s = lax.dot_general(qs, kdup, _NT, preferred_element_type=F32) * scale
                s = jnp.where(valid, s, -1e30)
                sink = _stacked_sinks(sink_ref, g)
                m = jnp.maximum(jnp.max(s, axis=-1, keepdims=True), sink)
                e = jnp.exp(s - m)
                z = jnp.sum(e, axis=-1, keepdims=True) + jnp.exp(sink - m)
                o = jnp.dot((e / z).astype(_MXU), vdup, preferred_element_type=F32)
                for i, tq in enumerate((2 * g, 2 * g + 1)):
                    o_ref[:, tq * LANES:(tq + 1) * LANES] = jnp.where(
                        low, o[2 * i * WINDOW:(2 * i + 1) * WINDOW], o[(2 * i + 1) * WINDOW:(2 * i + 2) * WINDOW])
                lse = m + jnp.log(z)
                for r in range(Q_PER_KV):
                    lse_blk = jnp.where(head_lane == Q_PER_KV * g + r, lse[r * WINDOW:(r + 1) * WINDOW], lse_blk)
        lse_ref[...] = lse_blk
        for j, g_ref in enumerate(gate_refs):
            cols = slice(j * CW, (j + 1) * CW)
            gate = g_ref[...].astype(F32)
            ya_ref[:, cols] = (o_ref[:, cols] * (gate * _sigmoid(gate))).astype(ya_ref.dtype)

    return pl.pallas_call(
        body, name="attn_fwd", grid=(nblk,),
        in_specs=[pl.BlockSpec(memory_space=pltpu.SMEM), q_spec, kvc_spec, kvp_spec, w_spec, w_spec] + gate_specs
        + [_ANY] * nd,
        out_specs=[q_spec, l_spec, q_spec],
        out_shape=[jax.ShapeDtypeStruct((seq, ATTN_W), F32), jax.ShapeDtypeStruct((seq, N_Q_HEADS), F32),
                   jax.ShapeDtypeStruct((seq, ATTN_W), _MXU)],
        compiler_params=_params(("arbitrary",)),
    )(sinks, proj, proj, proj, qw2, kw2, *([proj] * ng), *deps)


def _attn2_bwd(proj, qw2, kw2, sinks, lse, attn, dya, d_proj, deps=()):
    seq = proj.shape[0]
    nblk = seq // WINDOW
    scale = 1.0 / math.sqrt(HEAD_DIM)
    q_spec, kvc_spec, kvp_spec, w_spec, l_spec, gate_specs = _attn_in_specs(nblk, True)
    s_spec = pl.BlockSpec((1, N_Q_HEADS), lambda n: (0, 0))
    d_spec = pl.BlockSpec((WINDOW, QKV_W + ATTN_W), lambda n: (nblk - 1 - n, 0))
    deps = list(deps) + [d_proj]
    nd, ng = len(deps), len(gate_specs)

    def body(sink_ref, q_ref, kvc_ref, kvp_ref, qw_ref, kw_ref, lse_ref, attn_ref, dya_ref, *rest):
        gate_refs = rest[:ng]
        d_ref, dqw_ref, dkw_ref, dsk_ref, carry, do_ref = rest[ng + nd:]
        step = pl.program_id(0)
        n = nblk - 1 - step

        @pl.when(step == 0)
        def _():
            carry[...] = jnp.zeros_like(carry)
            dqw_ref[...] = jnp.zeros_like(dqw_ref)
            dkw_ref[...] = jnp.zeros_like(dkw_ref)
            dsk_ref[...] = jnp.zeros_like(dsk_ref)

        for j, g_ref in enumerate(gate_refs):
            cols = slice(j * CW, (j + 1) * CW)
            f, df = _silu_and_grad(g_ref[...].astype(F32))
            dv = dya_ref[:, cols]
            do_ref[:, cols] = dv * f
            d_ref[:, QKV_W + j * CW:QKV_W + (j + 1) * CW] = (dv * attn_ref[:, cols] * df).astype(d_ref.dtype)

        low, low2 = _low_half(WINDOW), _low_half(2 * WINDOW)
        valid = _stacked_band_mask(n)
        head_lane = lax.broadcasted_iota(jnp.int32, (WINDOW, N_Q_HEADS), 1)
        sink_lane = lax.broadcasted_iota(jnp.int32, (1, N_Q_HEADS), 1)
        kv = jnp.concatenate([kvp_ref[...], kvc_ref[...]], axis=0).astype(F32)
        qwv, kwv = qw_ref[...], kw_ref[...]
        lse_blk = lse_ref[...]
        dqw = jnp.zeros((1, LANES), F32)
        dkw = jnp.zeros((1, LANES), F32)
        dsk = jnp.zeros((1, N_Q_HEADS), F32)
        for t in range(N_KV_HEADS // HEADS_PER_TILE):
            kt = kv[:, t * LANES:(t + 1) * LANES]
            vt = kv[:, KV_W + t * LANES:KV_W + (t + 1) * LANES]
            rk = _pair_rstd(kt, low2)
            kn = kt * rk * kwv
            dkn_t = jnp.zeros((2 * WINDOW, LANES), F32)
            dv_t = jnp.zeros((2 * WINDOW, LANES), F32)
            for hi in range(HEADS_PER_TILE):
                g = HEADS_PER_TILE * t + hi
                kdup = _dup_half(kn, hi, low2).astype(_MXU)
                vdup = _dup_half(vt, hi, low2).astype(_MXU)
                tiles = (2 * g, 2 * g + 1)
                qx, rq, stack, dstack, lse_rows = [], [], [], [], []
                for tq in tiles:
                    qt = q_ref[:, tq * LANES:(tq + 1) * LANES].astype(F32)
                    r = _pair_rstd(qt, low)
                    rq.append(r)
                    qx.append(qt * r)
                    stack += _split_heads(qx[-1] * qwv, low)
                    dstack += _split_heads(do_ref[:, tq * LANES:(tq + 1) * LANES], low)
                for r in range(Q_PER_KV):
                    lse_rows.append(jnp.sum(jnp.where(head_lane == Q_PER_KV * g + r, lse_blk, 0.0), axis=-1, keepdims=True))
                qs = jnp.concatenate(stack, axis=0).astype(_MXU)
                dos = jnp.concatenate(dstack, axis=0).astype(_MXU)
                lse_col = jnp.concatenate(lse_rows, axis=0)
                s = lax.dot_general(qs, kdup, _NT, preferred_element_type=F32) * scale
                s = jnp.where(valid, s, -1e30)
                p = jnp.exp(s - lse_col)
                dp = lax.dot_general(dos, vdup, _NT, preferred_element_type=F32)
                dsum = jnp.sum(p * dp, axis=-1, keepdims=True)
                ds = (p * (dp - dsum) * scale).astype(_MXU)
                dsink = -jnp.exp(_stacked_sinks(sink_ref, g) - lse_col) * dsum
                for r in range(Q_PER_KV):
                    dsk = dsk + jnp.where(sink_lane == Q_PER_KV * g + r, _colsum(dsink[r * WINDOW:(r + 1) * WINDOW]), 0.0)
                dv_g = _fold_halves(lax.dot_general(p.astype(_MXU), dos, _TN, preferred_element_type=F32))
                dkn_g = _fold_halves(lax.dot_general(ds, qs, _TN, preferred_element_type=F32))
                dv_t = jnp.where(low2, dv_t, dv_g) if hi else jnp.where(low2, dv_g, dv_t)
                dkn_t = jnp.where(low2, dkn_t, dkn_g) if hi else jnp.where(low2, dkn_g, dkn_t)
                dqn = jnp.dot(ds, kdup, preferred_element_type=F32)
                for i, tq in enumerate(tiles):
                    dqn_t = jnp.where(low, dqn[2 * i * WINDOW:(2 * i + 1) * WINDOW],
                                      dqn[(2 * i + 1) * WINDOW:(2 * i + 2) * WINDOW])
                    dq = rq[i] * (qwv * dqn_t - qx[i] * _pair_mean(dqn_t * qwv * qx[i], low))
                    d_ref[:, tq * LANES:(tq + 1) * LANES] = dq.astype(d_ref.dtype)
                    dqw = dqw + _colsum(dqn_t * qx[i])
            k_cols = slice(t * LANES, (t + 1) * LANES)
            v_cols = slice(KV_W + t * LANES, KV_W + (t + 1) * LANES)
            dkn_c = dkn_t[WINDOW:] + carry[:, k_cols]
            rc = rk[WINDOW:]
            kx = kt[WINDOW:] * rc
            dk = rc * (kwv * dkn_c - kx * _pair_mean(dkn_c * kwv * kx, low))
            d_ref[:, ATTN_W + t * LANES:ATTN_W + (t + 1) * LANES] = dk.astype(d_ref.dtype)
            d_ref[:, ATTN_W + KV_W + t * LANES:ATTN_W + KV_W + (t + 1) * LANES] = (
                dv_t[WINDOW:] + carry[:, v_cols]).astype(d_ref.dtype)
            carry[:, k_cols] = dkn_t[:WINDOW]
            carry[:, v_cols] = dv_t[:WINDOW]
            dkw = dkw + _colsum(dkn_c * kx)
        dqw_ref[...] += dqw
        dkw_ref[...] += dkw
        dsk_ref[...] += dsk

    return pl.pallas_call(
        body, name="attn_bwd", grid=(nblk,),
        in_specs=[pl.BlockSpec(memory_space=pltpu.SMEM), q_spec, kvc_spec, kvp_spec, w_spec, w_spec, l_spec, q_spec,
                  q_spec] + gate_specs + [_ANY] * nd,
        out_specs=[d_spec, w_spec, w_spec, s_spec],
        out_shape=[jax.ShapeDtypeStruct(d_proj.shape, d_proj.dtype), jax.ShapeDtypeStruct((1, LANES), F32),
                   jax.ShapeDtypeStruct((1, LANES), F32), jax.ShapeDtypeStruct((1, N_Q_HEADS), F32)],
        input_output_aliases={9 + ng + nd - 1: 0},
        scratch_shapes=[pltpu.VMEM((WINDOW, 2 * KV_W), F32), pltpu.VMEM((WINDOW, ATTN_W), F32)],
        compiler_params=_params(("arbitrary",)),
    )(sinks, proj, proj, proj, qw2, kw2, lse, attn, dya, *([proj] * ng), *deps)


def _ssm_discretise(a_re, a_im, log_dt):
    dt = jnp.exp(log_dt)
    mag = jnp.exp(dt * a_re)
    ab_re = mag * jnp.cos(dt * a_im)
    ab_im = mag * jnp.sin(dt * a_im)
    num_re = ab_re - 1.0
    num_im = ab_im
    den = a_re * a_re + a_im * a_im
    cf_re = (num_re * a_re + num_im * a_im) / den
    cf_im = (num_im * a_re - num_re * a_im) / den
    return ab_re, ab_im, cf_re, cf_im


def _ssm_params_fwd(a_re, a_im, log_dt):
    shp = jax.ShapeDtypeStruct(a_re.shape, F32)

    def body(are_ref, aim_ref, ldt_ref, abr_ref, abi_ref, cfr_ref, cfi_ref, alr_ref, ali_ref):
        abr, abi, cfr, cfi = _ssm_discretise(are_ref[...], aim_ref[...], ldt_ref[...])
        abr_ref[...], abi_ref[...], cfr_ref[...], cfi_ref[...] = abr, abi, cfr, cfi
        pr, pi = abr, abi
        for _ in range(int(math.log2(SSM_L))):
            pr, pi = pr * pr - pi * pi, 2.0 * pr * pi
        alr_ref[...], ali_ref[...] = pr, pi

    return pl.pallas_call(body, name="ssm_params_fwd", out_shape=[shp] * 6)(a_re, a_im, log_dt)


def _ssm_params_bwd(a_re, a_im, log_dt, d_abr, d_abi, d_cfr, d_cfi):
    def body(are_ref, aim_ref, ldt_ref, g0, g1, g2, g3, dare_ref, daim_ref, dldt_ref):
        _, vjp = jax.vjp(_ssm_discretise, are_ref[...], aim_ref[...], ldt_ref[...])
        dare_ref[...], daim_ref[...], dldt_ref[...] = vjp((g0[...], g1[...], g2[...], g3[...]))

    return pl.pallas_call(
        body, name="ssm_params_bwd",
        out_shape=[jax.ShapeDtypeStruct(a_re.shape, F32), jax.ShapeDtypeStruct(a_im.shape, F32),
                   jax.ShapeDtypeStruct(log_dt.shape, F32)],
    )(a_re, a_im, log_dt, d_abr, d_abi, d_cfr, d_cfi)


def _scan_cols(j):
    return pl.ds(j * SSM_SB, SSM_SB)


def _rows8(r):
    return pl.ds(pl.multiple_of(r * SUBLANES, SUBLANES), SUBLANES)


def _bcast8(row):
    return jnp.broadcast_to(row, (SUBLANES, row.shape[-1]))


def _token_order_pick():
    tok = lax.broadcasted_iota(jnp.int32, (SSM_T, SSM_T), 0)
    row = lax.broadcasted_iota(jnp.int32, (SSM_T, SSM_T), 1)
    return (row == SUBLANES * (tok % SSM_L) + tok // SSM_L).astype(_MXU)


SCAN_UNROLL = 8


def _scan_loop(n, step, init):
    def trip(o, carry):
        for i in range(SCAN_UNROLL):
            carry = step(o * SCAN_UNROLL + i, carry)
        return carry

    return lax.fori_loop(0, n // SCAN_UNROLL, trip, init)


def _ssm_fwd(u, b_re, b_im, c_re, c_im, d_skip, coef):
    seq = u.shape[0]
    nc = seq // SSM_T
    T, L = SSM_T, SSM_L

    def body(u_ref, bre_ref, bim_ref, cre_ref, cim_ref, d_ref, are_ref, aim_ref, cfr_ref, cfi_ref, alr_ref, ali_ref,
             y_ref, yg_ref, sre_ref, sim_ref, ire_ref, iim_ref, car_re, car_im, end_re, end_im, yg_scan):
        c = pl.program_id(0)

        @pl.when(c == 0)
        def _():
            car_re[...] = jnp.zeros_like(car_re)
            car_im[...] = jnp.zeros_like(car_im)

        for j in range(SSM_JB):
            ub = u_ref[:, j * LANES:(j + 1) * LANES].astype(_MXU)
            bur = jnp.dot(ub, bre_ref[j], preferred_element_type=F32)
            bui = jnp.dot(ub, bim_ref[j], preferred_element_type=F32)
            cfr, cfi = cfr_ref[:, _scan_cols(j)], cfi_ref[:, _scan_cols(j)]
            sre_ref[:, _scan_cols(j)] = cfr * bur - cfi * bui
            sim_ref[:, _scan_cols(j)] = cfr * bui + cfi * bur

        for j in range(SSM_JB):
            cols = _scan_cols(j)
            ar, ai = _bcast8(are_ref[:, cols]), _bcast8(aim_ref[:, cols])

            def step1(r, s, cols=cols, ar=ar, ai=ai):
                sr, si = s
                rows = _rows8(r)
                return (ar * sr - ai * si + sre_ref[rows, cols], ar * si + ai * sr + sim_ref[rows, cols])

            zero = jnp.zeros((SUBLANES, SSM_SB), F32)
            er, ei = _scan_loop(L, step1, (zero, zero))
            end_re[:, cols] = er
            end_im[:, cols] = ei

        alr, ali = alr_ref[...], ali_ref[...]
        cr, ci = car_re[...], car_im[...]
        ire_ref[0:1, :] = cr
        iim_ref[0:1, :] = ci
        for i in range(1, SUBLANES):
            er, ei = end_re[i - 1:i, :], end_im[i - 1:i, :]
            cr, ci = alr * cr - ali * ci + er, alr * ci + ali * cr + ei
            ire_ref[i:i + 1, :] = cr
            iim_ref[i:i + 1, :] = ci

        for j in range(SSM_JB):
            cols = _scan_cols(j)
            ar, ai = _bcast8(are_ref[:, cols]), _bcast8(aim_ref[:, cols])

            def step2(r, s, cols=cols, ar=ar, ai=ai):
                sr, si = s
                rows = _rows8(r)
                nr = ar * sr - ai * si + sre_ref[rows, cols]
                ni = ar * si + ai * sr + sim_ref[rows, cols]
                sre_ref[rows, cols] = nr
                sim_ref[rows, cols] = ni
                return nr, ni

            _scan_loop(L, step2, (ire_ref[:, cols], iim_ref[:, cols]))

        car_re[...] = sre_ref[T - 1:T, :]
        car_im[...] = sim_ref[T - 1:T, :]

        for j in range(SSM_JB):
            cols = _scan_cols(j)
            ch = slice(j * LANES, (j + 1) * LANES)
            y = (jnp.dot(sre_ref[:, cols].astype(_MXU), cre_ref[j], preferred_element_type=F32)
                 - jnp.dot(sim_ref[:, cols].astype(_MXU), cim_ref[j], preferred_element_type=F32))
            y = y + d_ref[:, ch] * u_ref[:, ch].astype(F32)
            y_ref[:, ch] = y
            yg_scan[:, ch] = jax.nn.gelu(y).astype(yg_scan.dtype)
        yg_ref[...] = jnp.dot(_token_order_pick(), yg_scan[...], preferred_element_type=F32).astype(yg_ref.dtype)

    tok = pl.BlockSpec((T, SSM_W), lambda c: (c, 0))
    st = pl.BlockSpec((T, N_STATES), lambda c: (c, 0))
    ini = pl.BlockSpec((None, SUBLANES, N_STATES), lambda c: (c, 0, 0))
    bsp = pl.BlockSpec((SSM_JB, LANES, SSM_SB), lambda c: (0, 0, 0))
    csp = pl.BlockSpec((SSM_JB, SSM_SB, LANES), lambda c: (0, 0, 0))
    row_w = pl.BlockSpec((1, SSM_W), lambda c: (0, 0))
    row_s = pl.BlockSpec((1, N_STATES), lambda c: (0, 0))
    return pl.pallas_call(
        body, name="ssm_fwd", grid=(nc,),
        in_specs=[tok, bsp, bsp, csp, csp, row_w] + [row_s] * 6,
        out_specs=[tok, tok, st, st, ini, ini],
        out_shape=[jax.ShapeDtypeStruct((seq, SSM_W), F32), jax.ShapeDtypeStruct((seq, SSM_W), _MXU),
                   jax.ShapeDtypeStruct((seq, N_STATES), F32), jax.ShapeDtypeStruct((seq, N_STATES), F32),
                   jax.ShapeDtypeStruct((nc, SUBLANES, N_STATES), F32),
                   jax.ShapeDtypeStruct((nc, SUBLANES, N_STATES), F32)],
        scratch_shapes=[pltpu.VMEM((1, N_STATES), F32), pltpu.VMEM((1, N_STATES), F32),
                        pltpu.VMEM((SUBLANES, N_STATES), F32), pltpu.VMEM((SUBLANES, N_STATES), F32),
                        pltpu.VMEM((T, SSM_W), _MXU)],
        compiler_params=_params(("arbitrary",)),
    )(u, b_re, b_im, c_re, c_im, d_skip, *coef)


def _ssm_bwd(dyg, y, u, s_re, s_im, i_re, i_im, b_re, b_im, c_re, c_im, d_skip, coef, d_proj, deps=()):
    seq = u.shape[0]
    nc = seq // SSM_T
    T, L = SSM_T, SSM_L
    deps = list(deps) + [d_proj]

    def body(dyg_ref, y_ref, u_ref, sre_ref, sim_ref, ire_ref, iim_ref, bre_ref, bim_ref, cre_ref, cim_ref, d_ref,
             are_ref, aim_ref, cfr_ref, cfi_ref, alr_ref, ali_ref, *rest):
        (du_ref, dbre_out, dbim_out, dcre_out, dcim_out, dd_ref, dar_ref, dai_ref, dcfr_ref, dcfi_ref,
         lre, lim, car_re, car_im, end_re, end_im, ini_re, ini_im, dbre_ref, dbim_ref, dcre_ref, dcim_ref,
         dy_ref, du_scan) = rest[len(deps):]
        step = pl.program_id(0)
        dy_ref[...] = jax.vjp(jax.nn.gelu, y_ref[...])[1](dyg_ref[...])[0]

        @pl.when(step == 0)
        def _():
            car_re[...] = jnp.zeros_like(car_re)
            car_im[...] = jnp.zeros_like(car_im)
            for ref in (dbre_ref, dbim_ref, dcre_ref, dcim_ref, dd_ref, dar_ref, dai_ref, dcfr_ref, dcfi_ref):
                ref[...] = jnp.zeros_like(ref)

        for j in range(SSM_JB):
            dyb = dy_ref[:, j * LANES:(j + 1) * LANES].astype(_MXU)
            lre[:, _scan_cols(j)] = lax.dot_general(dyb, cre_ref[j], _NT, preferred_element_type=F32)
            lim[:, _scan_cols(j)] = -lax.dot_general(dyb, cim_ref[j], _NT, preferred_element_type=F32)

        for j in range(SSM_JB):
            cols = _scan_cols(j)
            ar, ai = _bcast8(are_ref[:, cols]), _bcast8(aim_ref[:, cols])

            def step1(t, s, cols=cols, ar=ar, ai=ai):
                sr, si = s
                rows = _rows8(L - 1 - t)
                return (ar * sr + ai * si + lre[rows, cols], ar * si - ai * sr + lim[rows, cols])

            zero = jnp.zeros((SUBLANES, SSM_SB), F32)
            er, ei = _scan_loop(L, step1, (zero, zero))
            end_re[:, cols] = er
            end_im[:, cols] = ei

        alr, ali = alr_ref[...], ali_ref[...]
        cr, ci = car_re[...], car_im[...]
        ini_re[SUBLANES - 1:SUBLANES, :] = cr
        ini_im[SUBLANES - 1:SUBLANES, :] = ci
        for i in range(SUBLANES - 2, -1, -1):
            er, ei = end_re[i + 1:i + 2, :], end_im[i + 1:i + 2, :]
            cr, ci = alr * cr + ali * ci + er, alr * ci - ali * cr + ei
            ini_re[i:i + 1, :] = cr
            ini_im[i:i + 1, :] = ci

        for j in range(SSM_JB):
            cols = _scan_cols(j)
            ar, ai = _bcast8(are_ref[:, cols]), _bcast8(aim_ref[:, cols])

            def step2(t, s, cols=cols, ar=ar, ai=ai):
                sr, si = s
                rows = _rows8(L - 1 - t)
                nr = ar * sr + ai * si + lre[rows, cols]
                ni = ar * si - ai * sr + lim[rows, cols]
                lre[rows, cols] = nr
                lim[rows, cols] = ni
                return nr, ni

            _scan_loop(L, step2, (ini_re[:, cols], ini_im[:, cols]))

        car_re[...] = lre[0:1, :]
        car_im[...] = lim[0:1, :]

        head, tail, body_rows = slice(0, SUBLANES), slice(SUBLANES, T), slice(0, T - SUBLANES)
        for j in range(SSM_JB):
            cols = _scan_cols(j)
            ch = slice(j * LANES, (j + 1) * LANES)
            lr, li = lre[:, cols], lim[:, cols]
            lt_r, lt_i, sp_r, sp_i = lre[tail, cols], lim[tail, cols], sre_ref[body_rows, cols], sim_ref[body_rows, cols]
            lh_r, lh_i, si_r, si_i = lre[head, cols], lim[head, cols], ire_ref[:, cols], iim_ref[:, cols]
            dar_ref[:, cols] += _colsum(lt_r * sp_r + lt_i * sp_i) + _colsum(lh_r * si_r + lh_i * si_i)
            dai_ref[:, cols] += _colsum(lt_i * sp_r - lt_r * sp_i) + _colsum(lh_i * si_r - lh_r * si_i)
            ub = u_ref[:, ch].astype(_MXU)
            uf = ub.astype(F32)
            bur = jnp.dot(ub, bre_ref[j], preferred_element_type=F32)
            bui = jnp.dot(ub, bim_ref[j], preferred_element_type=F32)
            dcfr_ref[:, cols] += _colsum(lr * bur + li * bui)
            dcfi_ref[:, cols] += _colsum(li * bur - lr * bui)
            cfr, cfi = cfr_ref[:, cols], cfi_ref[:, cols]
            dbur = (cfr * lr + cfi * li).astype(_MXU)
            dbui = (cfr * li - cfi * lr).astype(_MXU)
            dyf = dy_ref[:, ch]
            dyb = dyf.astype(_MXU)
            du = (lax.dot_general(dbur, bre_ref[j], _NT, preferred_element_type=F32)
                  + lax.dot_general(dbui, bim_ref[j], _NT, preferred_element_type=F32) + d_ref[:, ch] * dyf)
            du_scan[:, ch] = du.astype(du_scan.dtype)
            dbre_ref[j] += lax.dot_general(ub, dbur, _TN, preferred_element_type=F32)
            dbim_ref[j] += lax.dot_general(ub, dbui, _TN, preferred_element_type=F32)
            dcre_ref[j] += lax.dot_general(sre_ref[:, cols].astype(_MXU), dyb, _TN, preferred_element_type=F32)
            dcim_ref[j] -= lax.dot_general(sim_ref[:, cols].astype(_MXU), dyb, _TN, preferred_element_type=F32)
            dd_ref[:, ch] += _colsum(dyf * uf)
        du_ref[...] = jnp.dot(_token_order_pick(), du_scan[...], preferred_element_type=F32).astype(du_ref.dtype)

        @pl.when(step == nc - 1)
        def _():
            for acc, out in ((dbre_ref, dbre_out), (dbim_ref, dbim_out), (dcre_ref, dcre_out), (dcim_ref, dcim_out)):
                pltpu.sync_copy(acc, out)

    tok = pl.BlockSpec((T, SSM_W), lambda c: (nc - 1 - c, 0))
    st = pl.BlockSpec((T, N_STATES), lambda c: (nc - 1 - c, 0))
    ini = pl.BlockSpec((None, SUBLANES, N_STATES), lambda c: (nc - 1 - c, 0, 0))
    bsp = pl.BlockSpec((SSM_JB, LANES, SSM_SB), lambda c: (0, 0, 0))
    csp = pl.BlockSpec((SSM_JB, SSM_SB, LANES), lambda c: (0, 0, 0))
    row_w = pl.BlockSpec((1, SSM_W), lambda c: (0, 0))
    row_s = pl.BlockSpec((1, N_STATES), lambda c: (0, 0))
    big = pltpu.VMEM((T, N_STATES), F32)
    one = pltpu.VMEM((1, N_STATES), F32)
    eight = pltpu.VMEM((SUBLANES, N_STATES), F32)
    return pl.pallas_call(
        body, name="ssm_bwd", grid=(nc,),
        in_specs=[tok, tok, tok, st, st, ini, ini, bsp, bsp, csp, csp, row_w] + [row_s] * 6 + [_ANY] * len(deps),
        out_specs=[pl.BlockSpec((pl.Element(T), pl.Element(SSM_W)), lambda c: ((nc - 1 - c) * T, OFF_U * CW)),
                   _ANY, _ANY, _ANY, _ANY, row_w, row_s, row_s, row_s, row_s],
        input_output_aliases={18 + len(deps) - 1: 0},
        out_shape=[jax.ShapeDtypeStruct(d_proj.shape, d_proj.dtype),
                   jax.ShapeDtypeStruct((SSM_JB, LANES, SSM_SB), F32), jax.ShapeDtypeStruct((SSM_JB, LANES, SSM_SB), F32),
                   jax.ShapeDtypeStruct((SSM_JB, SSM_SB, LANES), F32), jax.ShapeDtypeStruct((SSM_JB, SSM_SB, LANES), F32),
                   jax.ShapeDtypeStruct((1, SSM_W), F32)] + [jax.ShapeDtypeStruct((1, N_STATES), F32)] * 4,
        scratch_shapes=[big, big, one, one, eight, eight, eight, eight,
                        pltpu.VMEM((SSM_JB, LANES, SSM_SB), F32), pltpu.VMEM((SSM_JB, LANES, SSM_SB), F32),
                        pltpu.VMEM((SSM_JB, SSM_SB, LANES), F32), pltpu.VMEM((SSM_JB, SSM_SB, LANES), F32),
                        pltpu.VMEM((T, SSM_W), F32), pltpu.VMEM((T, SSM_W), _MXU)],
        compiler_params=_params(("arbitrary",)),
    )(dyg, y, u, s_re, s_im, i_re, i_im, b_re, b_im, c_re, c_im, d_skip, *coef, *deps)


def _block_diag_b(b):
    t = b.reshape(SSM_JB, 8, STATE, GROUP).transpose(0, 1, 3, 2)
    eye = jnp.eye(8, dtype=b.dtype)
    return (t[:, :, :, None, :] * eye[None, :, None, :, None]).reshape(SSM_JB, LANES, SSM_SB)


def _block_diag_c(c):
    t = c.reshape(SSM_JB, 8, GROUP, STATE).transpose(0, 1, 3, 2)
    eye = jnp.eye(8, dtype=c.dtype)
    return (t[:, :, :, None, :] * eye[None, :, None, :, None]).reshape(SSM_JB, SSM_SB, LANES)


def _diag_of_b(blk):
    t = blk.reshape(SSM_JB, 8, GROUP, 8, STATE)
    d = jnp.sum(t * jnp.eye(8, dtype=blk.dtype)[None, :, None, :, None], axis=3)
    return d.transpose(0, 1, 3, 2).reshape(N_GROUPS, STATE, GROUP)


def _diag_of_c(blk):
    t = blk.reshape(SSM_JB, 8, STATE, 8, GROUP)
    d = jnp.sum(t * jnp.eye(8, dtype=blk.dtype)[None, :, None, :, None], axis=3)
    return d.transpose(0, 1, 3, 2).reshape(N_GROUPS, GROUP, STATE)


def _to_scan_order(v):
    seq, w = v.shape
    return v.reshape(seq // SSM_T, SUBLANES, SSM_L, w).transpose(0, 2, 1, 3).reshape(seq, w)


def _adamw_math(w, g, m, v):
    nm = ADAM_B1 * m + (1.0 - ADAM_B1) * g
    nv = ADAM_B2 * v + (1.0 - ADAM_B2) * jnp.square(g)
    m_hat = nm / (1.0 - ADAM_B1 ** ADAM_STEP)
    v_hat = nv / (1.0 - ADAM_B2 ** ADAM_STEP)
    return -ADAM_LR * (m_hat / (jnp.sqrt(v_hat) + ADAM_EPS) + ADAM_WD * w), nm, nv


def _adamw(w, g, m, v, *, name, tm, deps=()):
    rows, cols = w.shape
    nd = len(deps)

    def body(w_ref, g_ref, m_ref, v_ref, *rest):
        d_ref, nm_ref, nv_ref = rest[nd:]
        d_ref[...], nm_ref[...], nv_ref[...] = _adamw_math(w_ref[...], g_ref[...], m_ref[...], v_ref[...])

    spec = pl.BlockSpec((tm, cols), lambda i: (i, 0))
    shp = jax.ShapeDtypeStruct((rows, cols), F32)
    return pl.pallas_call(body, name=name, grid=(rows // tm,), in_specs=[spec] * 4 + [_ANY] * nd,
                          out_specs=[spec] * 3, out_shape=[shp] * 3,
                          compiler_params=_params(("arbitrary",)))(w, g, m, v, *deps)


def _place():
    x, y, c = lax.axis_index("x"), lax.axis_index("y"), lax.axis_index("c")
    chips = [(1 - x, y), (x, 1 - y), (1 - x, 1 - y)]
    return x, y, c, chips


def _remote(src, dst, send_sem, recv_sem, dev):
    return pltpu.make_async_remote_copy(src_ref=src, dst_ref=dst, send_sem=send_sem, recv_sem=recv_sem,
                                        device_id=dev, device_id_type=MESH)


def _place_shard(w, mine_arr, *, name, tm=256, deps=()):
    rows, cols = w.shape

    def body(m_ref, w_ref, *rest):
        rest[-1][...] = w_ref[...].astype(rest[-1].dtype)

    return pl.pallas_call(
        body, name=name,
        grid_spec=pltpu.PrefetchScalarGridSpec(
            num_scalar_prefetch=1, grid=(rows // tm,),
            in_specs=[pl.BlockSpec((tm, cols), lambda i, m: (i, 0))] + [_ANY] * len(deps),
            out_specs=pl.BlockSpec((None, tm, cols), lambda i, m: (m[0], i, 0))),
        out_shape=jax.ShapeDtypeStruct((N_CHIPS, rows, cols), _WIRE),
        compiler_params=_params(("arbitrary",)),
    )(mine_arr, w, *deps)


_HBM = pl.BlockSpec(memory_space=pltpu.HBM)
_SEM = pl.BlockSpec(memory_space=pltpu.SEMAPHORE)
_EFFECT = pltpu.SideEffectType.DATAFLOW_SIDE_EFFECTING


def _copies_start(name, bufs, plan, count, after=()):
    nb, na = len(bufs), len(after)

    def body(*refs):
        send_sems, recv_sems, token = refs[nb + na], refs[nb + na + 1], refs[-1]
        copies = plan(refs[:nb])
        assert len(copies) == count
        for i, (src, dst, dev, _) in enumerate(copies):
            _remote(src, dst, send_sems.at[i], recv_sems.at[i], dev).start()
        token[...] = jnp.zeros_like(token)

    res = pl.pallas_call(
        body, name=name, in_specs=[_HBM] * nb + [_ANY] * na,
        out_specs=(_SEM, _SEM, *[_HBM] * nb, pl.BlockSpec(memory_space=pltpu.VMEM)),
        out_shape=(pltpu.SemaphoreType.DMA((count,)), pltpu.SemaphoreType.DMA((count,)),
                   *[pltpu.HBM(b.shape, b.dtype) for b in bufs], jax.ShapeDtypeStruct((SUBLANES, LANES), F32)),
        input_output_aliases={i: 2 + i for i in range(nb)},
        compiler_params=pltpu.CompilerParams(has_side_effects=_EFFECT),
    )(*[pltpu.with_memory_space_constraint(b, pltpu.HBM) for b in bufs], *after)
    return (res[0], res[1]), list(res[2:2 + nb]), res[-1]


def _copies_wait(name, bufs, sems, plan, after=(), which=None):
    nb, na = len(bufs), len(after)

    def body(*refs):
        send_sems, recv_sems = refs[nb], refs[nb + 1]
        for i, (src, _, dev, land) in enumerate(plan(refs[:nb])):
            if which is not None and i not in which:
                continue
            cp = _remote(src, land, send_sems.at[i], recv_sems.at[i], dev)
            cp.wait_send()
            cp.wait_recv()

    res = pl.pallas_call(
        body, name=name, in_specs=[_HBM] * nb + [_SEM, _SEM] + [_ANY] * na, out_specs=[_HBM] * nb,
        out_shape=[pltpu.HBM(b.shape, b.dtype) for b in bufs],
        input_output_aliases={i: i for i in range(nb)},
        compiler_params=pltpu.CompilerParams(has_side_effects=_EFFECT),
    )(*bufs, *sems, *after)
    return list(res)


def _plan_gather_ici(fulls, which=(0, 1, 2)):
    x, y, c, chips = _place()
    copies = []
    for f in fulls:
        half = pl.ds(c * (f.shape[1] // 2), f.shape[1] // 2)
        own = f.at[2 * x + y, half]
        for chip in [chips[k] for k in which]:
            copies.append((own, own, (*chip, c), f.at[2 * chip[0] + chip[1], half]))
    return copies


def _plan_gather_d2d(fulls, which=(0, 1, 2)):
    x, y, c, chips = _place()
    copies = []
    for f in fulls:
        r2 = f.shape[1] // 2
        for chip in [chips[k] for k in which]:
            blk = 2 * chip[0] + chip[1]
            landed = f.at[blk, pl.ds(c * r2, r2)]
            copies.append((landed, landed, (x, y, 1 - c), f.at[blk, pl.ds((1 - c) * r2, r2)]))
    return copies


def _plan_relay_direct(fulls):
    (f,) = fulls
    x, y, c, chips = _place()
    half = pl.ds(c * (f.shape[1] // 2), f.shape[1] // 2)
    own = f.at[2 * x + y, half]
    return [(own, own, (*chip, c), f.at[2 * chip[0] + chip[1], half]) for chip in chips[:2]]


def _plan_relay_forward(fulls, k):
    (f,) = fulls
    x, y, c, chips = _place()
    r2 = f.shape[1] // 2
    half, other = pl.ds(c * r2, r2), pl.ds((1 - c) * r2, r2)
    quarter = pl.ds(c * r2 + k * (r2 // 2), r2 // 2)
    blk, far = 2 * chips[k][0] + chips[k][1], 2 * chips[2][0] + chips[2][1]
    passed, landed = f.at[blk, quarter], f.at[blk, half]
    return [(passed, passed, (*chips[1 - k], c), f.at[far, quarter]), (landed, landed, (x, y, 1 - c), f.at[blk, other])]


def _plan_relay_last(fulls):
    (f,) = fulls
    x, y, c, chips = _place()
    r2 = f.shape[1] // 2
    far = 2 * chips[2][0] + chips[2][1]
    landed = f.at[far, pl.ds(c * r2, r2)]
    return [(landed, landed, (x, y, 1 - c), f.at[far, pl.ds((1 - c) * r2, r2)])]


def _plan_swap_halves(refs):
    x, y, c, _ = _place()
    n = len(refs) // 2
    copies = []
    for g, land in zip(refs[:n], refs[n:]):
        r2 = g.shape[1] // 2
        copies.append((g.at[:, pl.ds((1 - c) * r2, r2), :], land, (x, y, 1 - c), land))
    return copies


def _plan_scatter_chips(refs):
    x, y, c, chips = _place()
    n = len(refs) // 2
    copies = []
    for h, land in zip(refs[:n], refs[n:]):
        for k, chip in enumerate(chips):
            copies.append((h.at[2 * chip[0] + chip[1]], land.at[k], (*chip, c), land.at[k]))
    return copies


def _plan_join_halves(totals):
    x, y, c, _ = _place()
    copies = []
    for t in totals:
        r2 = t.shape[0] // 2
        mine = t.at[pl.ds(c * r2, r2)]
        copies.append((mine, mine, (x, y, 1 - c), t.at[pl.ds((1 - c) * r2, r2)]))
    return copies


def _add_sibling_half(g, got, c_arr, *, name, tm):
    _, rows, cols = g.shape
    r2 = rows // 2
    nb = r2 // tm

    def body(c_ref, g_ref, r_ref, o_ref):
        o_ref[...] = (g_ref[...].astype(F32) + r_ref[...].astype(F32)).astype(o_ref.dtype)

    return pl.pallas_call(
        body, name=name,
        grid_spec=pltpu.PrefetchScalarGridSpec(
            num_scalar_prefetch=1, grid=(N_CHIPS, nb),
            in_specs=[pl.BlockSpec((None, tm, cols), lambda b, i, c: (b, c[0] * nb + i, 0)),
                      pl.BlockSpec((None, tm, cols), lambda b, i, c: (b, i, 0))],
            out_specs=pl.BlockSpec((None, tm, cols), lambda b, i, c: (b, i, 0))),
        out_shape=jax.ShapeDtypeStruct((N_CHIPS, r2, cols), _WIRE),
        compiler_params=_params(("arbitrary", "arbitrary")),
    )(c_arr, g, got)


def _add_chips(h, got, place_arr, *, name, tm):
    _, r2, cols = h.shape
    nb = r2 // tm

    def body(p_ref, h_ref, r_ref, o_ref):
        o_ref[...] = ((h_ref[...].astype(F32) + r_ref[0].astype(F32)) + r_ref[1].astype(F32)) + r_ref[2].astype(F32)

    return pl.pallas_call(
        body, name=name,
        grid_spec=pltpu.PrefetchScalarGridSpec(
            num_scalar_prefetch=1, grid=(nb,),
            in_specs=[pl.BlockSpec((None, tm, cols), lambda i, p: (p[0], i, 0)),
                      pl.BlockSpec((3, tm, cols), lambda i, p: (0, i, 0))],
            out_specs=pl.BlockSpec((tm, cols), lambda i, p: (p[1] * nb + i, 0))),
        out_shape=jax.ShapeDtypeStruct((2 * r2, cols), F32),
        compiler_params=_params(("arbitrary",)),
    )(place_arr, h, got)


class _ReduceScatter:
    def __init__(self, tag, names, grads):
        self.tag, self.names, self.n = tag, names, len(names)
        core = lax.axis_index("c").astype(jnp.int32)
        chip = (2 * lax.axis_index("x") + lax.axis_index("y")).astype(jnp.int32)
        self.c_arr, self.place_arr = core.reshape(1), jnp.stack([chip, core])
        self.bufs = list(grads)

    def _start(self, step, bufs, plan, count, after):
        self.plan = plan
        self.step = f"grad_{step}_{self.tag}"
        self.sems, self.bufs, token = _copies_start(self.step + "_start", bufs, plan, count, after)
        return [token]

    def _wait(self, after):
        self.bufs = _copies_wait(self.step + "_wait", self.bufs, self.sems, self.plan, after)
        return self.bufs

    def start_swap(self, after=()):
        lands = [lax.empty((N_CHIPS, g.shape[1] // 2, g.shape[2]), g.dtype) for g in self.bufs]
        return self._start("swap", self.bufs + lands, _plan_swap_halves, self.n, after)

    def start_scatter(self, after):
        bufs = self._wait(after)
        pair = [_add_sibling_half(g, r, self.c_arr, name=f"grad_add_sibling_{nm}", tm=min(256, g.shape[1] // 2))
                for nm, g, r in zip(self.names, bufs[:self.n], bufs[self.n:])]
        lands = [lax.empty((3,) + h.shape[1:], h.dtype) for h in pair]
        return self._start("scatter", pair + lands, _plan_scatter_chips, 3 * self.n, ())

    def start_join(self, after):
        bufs = self._wait(after)
        total = [_add_chips(h, r, self.place_arr, name=f"grad_add_chips_{nm}", tm=min(256, h.shape[1]))
                 for nm, h, r in zip(self.names, bufs[:self.n], bufs[self.n:])]
        return self._start("join", total, _plan_join_halves, self.n, ())

    def finish(self, after):
        return dict(zip(self.names, self._wait(after)))


def _all_gather_small(v):
    m_per, n = v.shape

    def body(x_ref, out_ref, send_sems, recv_sems, local_sem):
        x, y, c, chips = _place()
        me, sibling = (x, y, c), (x, y, 1 - c)

        def rows(px, py, pc):
            return out_ref.at[4 * px + 2 * py + pc]

        def copy(k, block, to, src=None):
            return _remote(rows(*block) if src is None else src, rows(*block), send_sems.at[k], recv_sems.at[k], to)

        mine = pltpu.make_async_copy(x_ref, rows(*me), local_sem)
        mine.start()
        first = [copy(0, me, sibling, src=x_ref)]
        first += [copy(1 + j, me, (*chip, c), src=x_ref) for j, chip in enumerate(chips)]
        for cp in first:
            cp.start()
        passed = [copy(4 + j, (*chip, c), sibling) for j, chip in enumerate(chips)]
        for j, chip in enumerate(chips):
            copy(1 + j, (*chip, c), me).wait_recv()
            passed[j].start()
        copy(0, sibling, me).wait_recv()
        for j, chip in enumerate(chips):
            copy(4 + j, (*chip, 1 - c), me).wait_recv()
        for cp in first + passed:
            cp.wait_send()
        mine.wait()

    return pl.pallas_call(
        body, name="gather_small_grads",
        out_shape=jax.ShapeDtypeStruct((8, m_per, n), v.dtype),
        in_specs=[pl.BlockSpec(memory_space=pltpu.VMEM)], out_specs=pl.BlockSpec(memory_space=pltpu.VMEM),
        scratch_shapes=[pltpu.SemaphoreType.DMA((7,)), pltpu.SemaphoreType.DMA((7,)), pltpu.SemaphoreType.DMA],
        compiler_params=pltpu.CompilerParams(vmem_limit_bytes=VMEM_LIMIT),
    )(v)


def _sum8(v, *, name):
    _, m, n = v.shape

    def body(v_ref, o_ref):
        acc = v_ref[0]
        for d in range(1, 8):
            acc = acc + v_ref[d]
        o_ref[...] = acc

    return pl.pallas_call(body, name=name, out_shape=jax.ShapeDtypeStruct((m, n), F32),
                          compiler_params=pltpu.CompilerParams(vmem_limit_bytes=VMEM_LIMIT))(v)


def _local_step(x, target, norm_w, q_norm_w, k_norm_w, sinks, a_re, a_im, log_dt, b_re, b_im, c_re, c_im, d_skip,
                b_glu, io):
    seq = x.shape[0]
    qw2 = jnp.tile(q_norm_w.reshape(1, HEAD_DIM), (1, HEADS_PER_TILE))
    kw2 = jnp.tile(k_norm_w.reshape(1, HEAD_DIM), (1, HEADS_PER_TILE))
    nw, bg = norm_w.reshape(1, D_MODEL), b_glu.reshape(1, D_MODEL)
    dsk = d_skip.reshape(1, SSM_W)

    h, rstd = _rms_fwd(x, nw, deps=io.begin())
    proj, w_in4 = io.projection(h)
    attn, lse, ya_in = _attn2_fwd(proj, qw2, kw2, sinks, deps=io.after_proj(proj))
    w_ap4 = io.weight("w_attn_proj", ya_in)
    w_glu4, w_sp4, w_out = io.weight("w_glu", ya_in), io.weight("w_ssm_proj", ya_in), io.weight("w_out", ya_in)
    y_a = _mm(ya_in, w_ap4, mode="nn", name="mm_attn_proj", tm=2048, tn=512, tk=ATTN_W, b_blocked=True,
              rows_outer=True, out_dtype=_MXU)

    flat_a = (a_re.reshape(1, N_STATES), a_im.reshape(1, N_STATES), jnp.repeat(log_dt, STATE).reshape(1, N_STATES))
    coef = _ssm_params_fwd(*flat_a)
    bre_blk, bim_blk = _block_diag_b(b_re).astype(_MXU), _block_diag_b(b_im).astype(_MXU)
    cre_blk, cim_blk = _block_diag_c(c_re).astype(_MXU), _block_diag_c(c_im).astype(_MXU)
    u_scan = _to_scan_order(proj[:, OFF_U * CW:OFF_U * CW + SSM_W])
    y_scan, yg, s_re, s_im, i_re, i_im = _ssm_fwd(u_scan, bre_blk, bim_blk, cre_blk, cim_blk, dsk, coef)
    glu, ys_in = _mm_glu_gate(yg, w_glu4, bg, proj)
    y_s = _mm(ys_in, w_sp4, mode="nn", name="mm_ssm_proj", tm=2048, tn=512, tk=SSM_W, b_blocked=True,
              rows_outer=True, out_dtype=_MXU)

    merged, dout, dout_b, sq = _mm_merge_out_loss(proj, y_a, y_s, w_out, x, target)
    loss = 0.5 * jnp.sum(sq) / D_MODEL

    d_ya, d_ys, d_proj = _mm_merge_bwd(dout_b, w_out, proj, y_a, y_s)
    g_w_out = _mm(merged, dout_b, mode="tn", name="mm_g_w_out", tm=1024, tn=D_MODEL, tk=1024, out_dtype=_WIRE)

    d_ya_in = _mm(d_ya, w_ap4, mode="nt", name="mm_d_attn_gate", tm=2048, tn=ATTN_W, tk=512, b_blocked=True)
    g_w_ap = _mm(ya_in, d_ya, mode="tn", name="mm_g_w_attn_proj", tm=ATTN_W, tn=D_MODEL, tk=2048, out_dtype=_WIRE,
                 out_blocked=True)

    g_w_sp = _mm(ys_in, d_ys, mode="tn", name="mm_g_w_ssm_proj", tm=SSM_W, tn=D_MODEL, tk=2048, out_dtype=_WIRE,
                 out_blocked=True)
    d_glu, d_proj, g_bglu = _mm_ssm_gate_bwd(d_ys, w_sp4, glu, bg, proj, d_proj)
    d_yg = _mm(d_glu, w_glu4, mode="nt", name="mm_d_gelu", tm=2048, tn=SSM_W, tk=512, b_blocked=True)
    g_w_glu = _mm(yg, d_glu, mode="tn", name="mm_g_w_glu", tm=SSM_W, tn=D_MODEL, tk=2048, out_dtype=_WIRE, out_blocked=True)
    dep = io.later_grads(dict(w_attn_proj=g_w_ap, w_glu=g_w_glu, w_ssm_proj=g_w_sp,
                              w_out=g_w_out.reshape(N_CHIPS, D_MODEL // N_CHIPS, D_MODEL)))

    d_proj, g_qw2, g_kw2, g_sk = _attn2_bwd(proj, qw2, kw2, sinks, lse, attn, d_ya_in, d_proj, deps=dep)
    dep = io.before_scan_backward([d_proj])
    (d_proj, g_bre, g_bim, g_cre, g_cim, g_dsk, g_abr, g_abi, g_cfr, g_cfi) = _ssm_bwd(
        _to_scan_order(d_yg), y_scan, u_scan, s_re, s_im, i_re, i_im, bre_blk, bim_blk, cre_blk, cim_blk, dsk, coef,
        d_proj, deps=dep)
    g_are, g_aim, g_ldt = _ssm_params_bwd(*flat_a, g_abr, g_abi, g_cfr, g_cfi)
    g_are, g_aim = g_are.reshape(N_GROUPS, STATE), g_aim.reshape(N_GROUPS, STATE)
    g_ldt = g_ldt.reshape(N_GROUPS, STATE).sum(axis=1)
    dep = io.before_input_projection_grad([d_proj]) + io.small_grads(dict(
        q_norm_w=g_qw2[0, :HEAD_DIM] + g_qw2[0, HEAD_DIM:], k_norm_w=g_kw2[0, :HEAD_DIM] + g_kw2[0, HEAD_DIM:],
        sinks=g_sk.reshape(N_Q_HEADS), A_re=g_are, A_im=g_aim, log_dt=g_ldt,
        B_re=_diag_of_b(g_bre), B_im=_diag_of_b(g_bim), C_re=_diag_of_c(g_cre), C_im=_diag_of_c(g_cim),
        D_skip=g_dsk.reshape(N_GROUPS, GROUP), b_glu=g_bglu.reshape(D_MODEL)))
    g_w_in = _mm(h, d_proj, mode="tn", name="mm_g_w_in", tm=1024, tn=IN_W // 4, tk=1024, out_dtype=_WIRE,
                 out_blocked=True, deps=dep)
    dep = io.input_projection_grad(g_w_in)
    d_h = _mm(d_proj, w_in4, mode="nt", name="mm_d_h", tm=1024, tn=D_MODEL, tk=IN_W // 4, b_blocked=True, deps=dep)
    grad_x, g_nw = _rms_bwd(d_h, x, rstd, nw, dout)
    return loss, grad_x, g_nw.reshape(D_MODEL)


_SMALL = ["norm_w", "q_norm_w", "k_norm_w", "sinks", "A_re", "A_im", "log_dt", "B_re", "B_im", "C_re", "C_im",
          "D_skip", "b_glu"]
_BIG = ["w_in", "w_attn_proj", "w_glu", "w_ssm_proj", "w_out"]
_LATER = _BIG[1:]
_RELATIONS = ("flip_x", "flip_y", "flip_xy")
_ORDER = ["norm_w", "w_in", "q_norm_w", "k_norm_w", "sinks", "w_attn_proj", "A_re", "A_im", "log_dt", "B_re", "B_im",
          "C_re", "C_im", "D_skip", "w_glu", "b_glu", "w_ssm_proj", "w_out"]
_PACK_W = 1024


def _packed_rows(size):
    unit = SUBLANES * _PACK_W
    return -(-size // unit) * SUBLANES


def _pack_small(d, names):
    parts = []
    for n in names:
        flat = d[n].reshape(-1).astype(F32)
        rows = _packed_rows(flat.shape[0])
        parts.append(jnp.pad(flat, (0, rows * _PACK_W - flat.shape[0])).reshape(rows, _PACK_W))
    return jnp.concatenate(parts, axis=0)


def _unpack_small(packed, like, names):
    out, pos = {}, 0
    for n in names:
        rows = _packed_rows(like[n].size)
        out[n] = packed[pos:pos + rows].reshape(-1)[:like[n].size].reshape(like[n].shape)
        pos += rows
    return out


def _place_block(v, index_arr, *, name):
    rows, cols = v.shape

    def body(i_ref, v_ref, o_ref):
        o_ref[...] = v_ref[...]

    return pl.pallas_call(
        body, name=name,
        grid_spec=pltpu.PrefetchScalarGridSpec(
            num_scalar_prefetch=1, grid=(1,),
            in_specs=[pl.BlockSpec((rows, cols), lambda i, d: (0, 0))],
            out_specs=pl.BlockSpec((None, rows, cols), lambda i, d: (d[0], 0, 0))),
        out_shape=jax.ShapeDtypeStruct((8, rows, cols), v.dtype),
        compiler_params=_params(("arbitrary",)),
    )(index_arr, v)


def _plan_all_to_all(refs):
    (land,) = refs
    x, y, c, _ = _place()
    own = land.at[4 * x + 2 * y + c]
    copies = []
    for fx, fy, fc in [(0, 0, 1), (0, 1, 0), (0, 1, 1), (1, 0, 0), (1, 0, 1), (1, 1, 0), (1, 1, 1)]:
        px, py, pc = (1 - x) if fx else x, (1 - y) if fy else y, (1 - c) if fc else c
        copies.append((own, own, (px, py, pc), land.at[4 * px + 2 * py + pc]))
    return copies


def _adamw_whole(w, g, m, v, *, name):
    def body(w_ref, g_ref, m_ref, v_ref, d_ref, nm_ref, nv_ref):
        d_ref[...], nm_ref[...], nv_ref[...] = _adamw_math(w_ref[...], g_ref[...], m_ref[...], v_ref[...])

    return pl.pallas_call(body, name=name, out_shape=[jax.ShapeDtypeStruct(w.shape, F32)] * 3)(w, g, m, v)


class _Exchanges:
    def __init__(self, w, m, v):
        self.w, self.m, self.v = w, m, v
        self.grads, self.delta, self.new_m, self.new_v = {}, {}, {}, {}

    def _adamw(self, names, deps):
        for n in names:
            self.delta[n], self.new_m[n], self.new_v[n] = _adamw(
                self.w[n], self.grads[n], self.m[n], self.v[n], name=f"adamw_{n}", tm=128, deps=deps)

    def begin(self):
        chip = (2 * lax.axis_index("x") + lax.axis_index("y")).astype(jnp.int32).reshape(1)
        w_in = _place_shard(self.w["w_in"], chip, name="place_w_in")
        self.w_in_sems, self.w_in_buf, token = _copies_start("gather_w_in_direct_start", [w_in], _plan_relay_direct, 2)
        self.later_full = [_place_shard(self.w[n], chip, name=f"place_{n}", deps=[token]) for n in _LATER]
        return self.later_full

    def projection(self, h):
        x, y = lax.axis_index("x"), lax.axis_index("y")
        blks = [jnp.asarray(b, jnp.int32).reshape(1)
                for b in (2 * x + y, 2 * (1 - x) + y, 2 * x + (1 - y), 2 * (1 - x) + (1 - y))]
        bufs = self.w_in_buf
        proj = _mm_chip_block(h, bufs[0], blks[0], None, name="mm_proj_own", out_dtype=_MXU)
        relay, token = [], proj
        for k, tag in enumerate(_RELATIONS[:2]):
            bufs = _copies_wait(f"gather_w_in_direct_{tag}_wait", bufs, self.w_in_sems, _plan_relay_direct, [token],
                                which=(k,))
            plan = functools.partial(_plan_relay_forward, k=k)
            sems, bufs, token = _copies_start(f"gather_w_in_relay_{tag}_start", bufs, plan, 2)
            relay.append((sems, plan))
        self.rest = _copies_start("gather_ici_rest_start", self.later_full, _plan_gather_ici, 3 * len(_LATER),
                                  after=[token])
        token = self.rest[2]
        for k, tag in enumerate(_RELATIONS[:2]):
            bufs = _copies_wait(f"gather_w_in_hand_{tag}_wait", bufs, relay[k][0], relay[k][1], [token], which=(1,))
            token = proj = _mm_chip_block(h, bufs[0], blks[1 + k], proj, name=f"mm_proj_{tag}", out_dtype=_MXU)
        for k, tag in enumerate(_RELATIONS[:2]):
            bufs = _copies_wait(f"gather_w_in_relay_{tag}_wait", bufs, relay[k][0], relay[k][1], [token], which=(0,))
        sems, bufs, token = _copies_start("gather_w_in_last_start", bufs, _plan_relay_last, 1)
        bufs = _copies_wait("gather_w_in_last_wait", bufs, sems, _plan_relay_last, [token])
        proj = _mm_chip_block(h, bufs[0], blks[3], proj, name="mm_proj_flip_xy", out_dtype=_MXU)
        return proj, bufs[0]

    def weight(self, name, after):
        if self.rest is not None:
            sems, bufs = self.rest
            later = dict(zip(_LATER, _copies_wait("gather_d2d_rest_wait", bufs, sems, _plan_gather_d2d, [after])))
            later["w_out"] = later["w_out"].reshape(D_MODEL, D_MODEL)
            self.later, self.rest = later, None
        return self.later[name]

    def after_proj(self, proj):
        sems, bufs, _ = self.rest
        bufs = _copies_wait("gather_ici_rest_wait", bufs, sems, _plan_gather_ici, [proj])
        sems, bufs, token = _copies_start("gather_d2d_rest_start", bufs, _plan_gather_d2d, 3 * len(_LATER))
        self.rest = (sems, bufs)
        return [token]

    def later_grads(self, grads):
        self.rs_later = _ReduceScatter("later", _LATER, [grads[n] for n in _LATER])
        return self.rs_later.start_swap()

    def before_scan_backward(self, after):
        return self.rs_later.start_scatter(after)

    def before_input_projection_grad(self, after):
        return self.rs_later.start_join(after)

    def input_projection_grad(self, g_w_in):
        self.grads.update(self.rs_later.finish([g_w_in]))
        self.rs_in = _ReduceScatter("w_in", ["w_in"], [g_w_in])
        self._adamw(_LATER, self.rs_in.start_swap())
        return self.rs_in.start_scatter([self.delta[n] for n in _LATER])

    def _adamw_small(self, names):
        for n in names:
            self.delta[n], self.new_m[n], self.new_v[n] = _adamw_whole(
                self.w[n], self.grads[n], self.m[n], self.v[n], name=f"adamw_{n}")

    def small_grads(self, grads):
        me = (4 * lax.axis_index("x") + 2 * lax.axis_index("y") + lax.axis_index("c")).astype(jnp.int32).reshape(1)
        land = _place_block(_pack_small(grads, _SMALL[1:]), me, name="place_small_grads")
        self.small = _copies_start("gather_small_start", [land], _plan_all_to_all, 7)
        return [self.small[2]]

    def finish(self, g_norm_w, loss, after):
        join = self.rs_in.start_join(after)
        sems, bufs, _ = self.small
        (land,) = _copies_wait("gather_small_wait", bufs, sems, _plan_all_to_all, join)
        self.grads.update(_unpack_small(_sum8(land, name="sum_small_grads"), self.w, _SMALL[1:]))
        self._adamw_small(_SMALL[1:])
        rows = _packed_rows(g_norm_w.size)
        late = jnp.concatenate([_pack_small(dict(norm_w=g_norm_w), _SMALL[:1]),
                                jnp.pad(loss.reshape(1, 1), ((0, SUBLANES - 1), (0, _PACK_W - 1)))], axis=0)
        late = _sum8(_all_gather_small(late), name="sum_norm_w_grad_and_loss")
        self.grads.update(_unpack_small(late[:rows], self.w, _SMALL[:1]))
        self._adamw_small(_SMALL[:1])
        self.grads.update(self.rs_in.finish([self.delta[_SMALL[0]]]))
        self._adamw(["w_in"], ())
        return late[rows, 0]


def kernel(x, norm_w, w_in, q_norm_w, k_norm_w, sinks, w_attn_proj, A_re, A_im, log_dt, B_re, B_im, C_re, C_im, D_skip, w_glu, b_glu, w_ssm_proj, w_out, loss_target, m_norm_w, m_w_in, m_q_norm_w, m_k_norm_w, m_sinks, m_w_attn_proj, m_A_re, m_A_im, m_log_dt, m_B_re, m_B_im, m_C_re, m_C_im, m_D_skip, m_w_glu, m_b_glu, m_w_ssm_proj, m_w_out, v_norm_w, v_w_in, v_q_norm_w, v_k_norm_w, v_sinks, v_w_attn_proj, v_A_re, v_A_im, v_log_dt, v_B_re, v_B_im, v_C_re, v_C_im, v_D_skip, v_w_glu, v_b_glu, v_w_ssm_proj, v_w_out):
    w = dict(norm_w=norm_w, w_in=w_in, q_norm_w=q_norm_w, k_norm_w=k_norm_w, sinks=sinks, w_attn_proj=w_attn_proj,
             A_re=A_re, A_im=A_im, log_dt=log_dt, B_re=B_re, B_im=B_im, C_re=C_re, C_im=C_im, D_skip=D_skip,
             w_glu=w_glu, b_glu=b_glu, w_ssm_proj=w_ssm_proj, w_out=w_out)
    m = dict(norm_w=m_norm_w, w_in=m_w_in, q_norm_w=m_q_norm_w, k_norm_w=m_k_norm_w, sinks=m_sinks,
             w_attn_proj=m_w_attn_proj, A_re=m_A_re, A_im=m_A_im, log_dt=m_log_dt, B_re=m_B_re, B_im=m_B_im,
             C_re=m_C_re, C_im=m_C_im, D_skip=m_D_skip, w_glu=m_w_glu, b_glu=m_b_glu, w_ssm_proj=m_w_ssm_proj,
             w_out=m_w_out)
    v = dict(norm_w=v_norm_w, w_in=v_w_in, q_norm_w=v_q_norm_w, k_norm_w=v_k_norm_w, sinks=v_sinks,
             w_attn_proj=v_w_attn_proj, A_re=v_A_re, A_im=v_A_im, log_dt=v_log_dt, B_re=v_B_re, B_im=v_B_im,
             C_re=v_C_re, C_im=v_C_im, D_skip=v_D_skip, w_glu=v_w_glu, b_glu=v_b_glu, w_ssm_proj=v_w_ssm_proj,
             w_out=v_w_out)

    io = _Exchanges(w, m, v)
    loss, grad_x, g_norm_w = _local_step(x[0], loss_target[0], norm_w, q_norm_w, k_norm_w, sinks, A_re, A_im, log_dt,
                                         B_re, B_im, C_re, C_im, D_skip, b_glu, io)
    loss = io.finish(g_norm_w, loss, [grad_x])
    grads, delta, new_m, new_v = io.grads, io.delta, io.new_m, io.new_v

    return (loss, grad_x[None], *[grads[n] for n in _ORDER], *[delta[n] for n in _ORDER],
            *[new_m[n] for n in _ORDER], *[new_v[n] for n in _ORDER])
```

```python
import functools
import math

import jax
import jax.numpy as jnp
from jax import lax
from jax.experimental import pallas as pl
from jax.experimental.pallas import tpu as pltpu

F32 = jnp.float32
_MXU = jnp.bfloat16
_WIRE = jnp.bfloat16

LANES = 128
SUBLANES = 8
VMEM_LIMIT = 56 * 1024 * 1024

D_MODEL = 2048
HEAD_DIM = 64
N_Q_HEADS = 16
N_KV_HEADS = 4
Q_PER_KV = 4
ATTN_W = 1024
KV_W = 256
WINDOW = 128
SSM_W = 1024
GROUP = 16
N_GROUPS = 64
STATE = 64
N_STATES = N_GROUPS * STATE
IN_W = 8704
NORM_EPS = 1e-6
N_CHIPS = 4
CW = 512
OFF_AGATE, OFF_U, OFF_Z, OFF_GA, OFF_GS = 3, 5, 7, 9, 13

SSM_T = 256
SSM_L = SSM_T // SUBLANES
SSM_JB = 8
SSM_SB = N_STATES // SSM_JB

ADAM_LR, ADAM_B1, ADAM_B2, ADAM_EPS, ADAM_WD, ADAM_STEP = 0.001, 0.9, 0.999, 1e-08, 0.01, 10

MESH = pl.DeviceIdType.MESH
_ANY = pl.BlockSpec(memory_space=pl.ANY)


def _params(sem=None):
    return pltpu.CompilerParams(dimension_semantics=sem, vmem_limit_bytes=VMEM_LIMIT)


def _mm(a, b, *, mode, name, tm, tn, tk, out_dtype=F32, b_blocked=False, out_blocked=False, rows_outer=False,
        deps=()):
    nd = len(deps)
    if mode == "tn":
        K, M = a.shape
    else:
        M, K = a.shape
    if mode == "nn":
        N = b.shape[0] * b.shape[2] if b_blocked else b.shape[1]
    elif mode == "nt":
        N = b.shape[1] if b_blocked else b.shape[0]
    else:
        N = b.shape[1]
    tm, tn, tk = min(tm, M), min(tn, N), min(tk, K)
    nj, ni, nk = N // tn, M // tm, K // tk
    assert nj * tn == N and ni * tm == M and nk * tk == K, (name, M, N, K)
    dims = {"nn": (((1,), (0,)), ((), ())), "nt": (((1,), (1,)), ((), ())), "tn": (((0,), (0,)), ((), ()))}[mode]

    if mode == "tn":
        a_spec = pl.BlockSpec((tk, tm), lambda j, i, k: (k, i))
    else:
        a_spec = pl.BlockSpec((tm, tk), lambda j, i, k: (i, k))
    if mode == "nn":
        if b_blocked:
            assert b.shape[0] == nj and b.shape[2] == tn
            b_spec = pl.BlockSpec((None, tk, tn), lambda j, i, k: (j, k, 0))
        else:
            b_spec = pl.BlockSpec((tk, tn), lambda j, i, k: (k, j))
    elif mode == "nt":
        if b_blocked:
            assert b.shape[0] == nk and b.shape[2] == tk
            b_spec = pl.BlockSpec((None, tn, tk), lambda j, i, k: (k, j, 0))
        else:
            b_spec = pl.BlockSpec((tn, tk), lambda j, i, k: (j, k))
    else:
        b_spec = pl.BlockSpec((tk, tn), lambda j, i, k: (k, j))
    whole_out = out_blocked and nj == 1
    if whole_out:
        assert ni == 1
        o_spec = pl.BlockSpec((N_CHIPS, tm, tn // N_CHIPS), lambda j, i, k: (0, 0, 0))
        o_shape = jax.ShapeDtypeStruct((N_CHIPS, M, tn // N_CHIPS), out_dtype)
    elif out_blocked:
        assert nj == N_CHIPS
        o_spec = pl.BlockSpec((None, tm, tn), lambda j, i, k: (j, i, 0))
        o_shape = jax.ShapeDtypeStruct((nj, M, tn), out_dtype)
    else:
        o_spec = pl.BlockSpec((tm, tn), lambda j, i, k: (i, j))
        o_shape = jax.ShapeDtypeStruct((M, N), out_dtype)
    use_acc = nk > 1 and (out_dtype != F32 or whole_out)

    def body(a_ref, b_ref, *rest):
        o_ref, scratch = rest[nd], rest[nd + 1:]

        def product():
            return lax.dot_general(a_ref[...].astype(_MXU), b_ref[...].astype(_MXU), dims, preferred_element_type=F32)

        def write(result):
            if whole_out:
                w = tn // N_CHIPS
                for c in range(N_CHIPS):
                    o_ref[c] = result[:, c * w:(c + 1) * w].astype(o_ref.dtype)
            else:
                o_ref[...] = result.astype(o_ref.dtype)

        if nk == 1:
            write(product())
            return
        k = pl.program_id(2)
        acc = scratch[0] if use_acc else o_ref

        @pl.when(k == 0)
        def _():
            acc[...] = jnp.zeros_like(acc)

        acc[...] += product()

        if use_acc:
            @pl.when(k == nk - 1)
            def _():
                write(acc[...])

    specs = [a_spec, b_spec, o_spec]
    grid = (nj, ni, nk)
    if rows_outer:
        specs = [pl.BlockSpec(s.block_shape, lambda i, j, k, f=s.index_map: f(j, i, k)) for s in specs]
        grid = (ni, nj, nk)
    return pl.pallas_call(
        body, name=name, grid=grid, in_specs=specs[:2] + [_ANY] * nd, out_specs=specs[2],
        out_shape=o_shape, scratch_shapes=[pltpu.VMEM((tm, tn), F32)] if use_acc else [],
        compiler_params=_params(("parallel", "parallel", "arbitrary")),
    )(a, b, *deps)


def _mm_chip_block(a, b4, blk, prev, *, name, tm=512, out_dtype=F32, deps=()):
    M, K = a.shape
    nchip, _, C = b4.shape
    tm = min(tm, M)
    extra = ([] if prev is None else [prev]) + list(deps)

    def body(blk_ref, a_ref, b_ref, *rest):
        rest[-1][...] = jnp.dot(a_ref[...].astype(_MXU), b_ref[...].astype(_MXU),
                                preferred_element_type=F32).astype(rest[-1].dtype)

    return pl.pallas_call(
        body, name=name,
        grid_spec=pltpu.PrefetchScalarGridSpec(
            num_scalar_prefetch=1, grid=(M // tm,),
            in_specs=[pl.BlockSpec((tm, K), lambda i, c: (i, 0)), pl.BlockSpec((None, K, C), lambda i, c: (c[0], 0, 0))]
            + [_ANY] * len(extra),
            out_specs=pl.BlockSpec((tm, C), lambda i, c: (i, c[0]))),
        out_shape=jax.ShapeDtypeStruct((M, nchip * C), out_dtype),
        input_output_aliases={} if prev is None else {3: 0},
        compiler_params=_params(("arbitrary",)),
    )(blk, a, b4, *extra)


def _mm_merge_out_loss(proj, y_a, y_s, w_out, x, target, *, tm=256):
    rows, d = x.shape
    ncol = d // CW

    def body(*refs):
        ga_refs, gs_refs = refs[:ncol], refs[ncol:2 * ncol]
        ya_ref, ys_ref, w_ref, x_ref, t_ref, m_ref, d_ref, db_ref, sq_ref = refs[2 * ncol:]
        for j in range(ncol):
            cols = slice(j * CW, (j + 1) * CW)
            m_ref[:, cols] = (_sigmoid(ga_refs[j][...].astype(F32)) * ya_ref[:, cols].astype(F32)
                              + _sigmoid(gs_refs[j][...].astype(F32)) * ys_ref[:, cols].astype(F32)).astype(m_ref.dtype)
        mo = jnp.dot(m_ref[...], w_ref[...].astype(_MXU), preferred_element_type=F32)
        err = (x_ref[...] + mo) - t_ref[...]
        dout = err * (1.0 / d)
        d_ref[...] = dout
        db_ref[...] = dout.astype(db_ref.dtype)
        part = _colsum(err * err)
        i = pl.program_id(0)

        @pl.when(i == 0)
        def _():
            sq_ref[...] = part

        @pl.when(i > 0)
        def _():
            sq_ref[...] += part

    tile = pl.BlockSpec((tm, d), lambda i: (i, 0))
    gate = [pl.BlockSpec((tm, CW), lambda i, c=off + j: (i, c)) for off in (OFF_GA, OFF_GS) for j in range(ncol)]
    return pl.pallas_call(
        body, name="mm_merge_out_loss", grid=(rows // tm,),
        in_specs=gate + [tile, tile, pl.BlockSpec((d, d), lambda i: (0, 0), pipeline_mode=pl.Buffered(1)), tile, tile],
        out_specs=[tile, tile, tile, pl.BlockSpec((1, d), lambda i: (0, 0))],
        out_shape=[jax.ShapeDtypeStruct((rows, d), _MXU), jax.ShapeDtypeStruct((rows, d), F32),
                   jax.ShapeDtypeStruct((rows, d), _MXU), jax.ShapeDtypeStruct((1, d), F32)],
        compiler_params=_params(("arbitrary",)),
    )(*([proj] * (2 * ncol)), y_a, y_s, w_out, x, target)


def _mm_merge_bwd(dout_b, w_out, proj, y_a, y_s, *, tm=256):
    rows, d = y_a.shape
    ncol = d // CW

    def body(do_ref, w_ref, *refs):
        ga_refs, gs_refs = refs[:ncol], refs[ncol:2 * ncol]
        ya_ref, ys_ref, dya_ref, dys_ref, dg_ref = refs[2 * ncol:]
        dm = lax.dot_general(do_ref[...].astype(_MXU), w_ref[...].astype(_MXU), _NT, preferred_element_type=F32)
        for j in range(ncol):
            cols = slice(j * CW, (j + 1) * CW)
            dmj = dm[:, cols]
            sa, ss = _sigmoid(ga_refs[j][...].astype(F32)), _sigmoid(gs_refs[j][...].astype(F32))
            dya_ref[:, cols] = (sa * dmj).astype(dya_ref.dtype)
            dys_ref[:, cols] = (ss * dmj).astype(dys_ref.dtype)
            dg_ref[:, cols] = (dmj * ya_ref[:, cols].astype(F32) * sa * (1.0 - sa)).astype(dg_ref.dtype)
            dg_ref[:, d + j * CW:d + (j + 1) * CW] = (dmj * ys_ref[:, cols].astype(F32) * ss
                                                      * (1.0 - ss)).astype(dg_ref.dtype)

    tile = pl.BlockSpec((tm, d), lambda i: (i, 0))
    gate = [pl.BlockSpec((tm, CW), lambda i, c=off + j: (i, c)) for off in (OFF_GA, OFF_GS) for j in range(ncol)]
    both = pl.BlockSpec((pl.Element(tm), pl.Element(2 * d)), lambda i: (i * tm, OFF_GA * CW))
    return pl.pallas_call(
        body, name="mm_merge_bwd", grid=(rows // tm,),
        in_specs=[tile, pl.BlockSpec((d, d), lambda i: (0, 0))] + gate + [tile, tile],
        out_specs=[tile, tile, both],
        out_shape=[jax.ShapeDtypeStruct((rows, d), _MXU)] * 2 + [jax.ShapeDtypeStruct((rows, IN_W), _MXU)],
        compiler_params=_params(("arbitrary",)),
    )(dout_b, w_out, *([proj] * (2 * ncol)), y_a, y_s)


def _mm_glu_gate(yg, w_glu4, b_glu, proj, *, tm=1024):
    rows, k = yg.shape
    nj, _, tn = w_glu4.shape
    w = nj * tn // 2
    tm = min(tm, rows)

    def body(a_ref, w_ref, ba_ref, bb_ref, z0_ref, z1_ref, glu_ref, ys_ref):
        j = pl.program_id(1)
        for c in range(nj):
            @pl.when(j == c)
            def _(c=c):
                glu_ref[:, c * tn:(c + 1) * tn] = jnp.dot(a_ref[...].astype(_MXU), w_ref[...].astype(_MXU),
                                                          preferred_element_type=F32).astype(glu_ref.dtype)

        @pl.when(j == nj - 1)
        def _():
            z = jnp.concatenate([z0_ref[...], z1_ref[...]], axis=1).astype(F32)
            ys_ref[...] = ((glu_ref[:, :w].astype(F32) + ba_ref[...]) * _sigmoid(glu_ref[:, w:].astype(F32) + bb_ref[...])
                           * (z * _sigmoid(z))).astype(ys_ref.dtype)

    bias = lambda c: pl.BlockSpec((1, w), lambda i, j: (0, c))
    zcol = lambda c: pl.BlockSpec((tm, CW), lambda i, j: (i, OFF_Z + c))
    return pl.pallas_call(
        body, name="mm_glu_gate", grid=(rows // tm, nj),
        in_specs=[pl.BlockSpec((tm, k), lambda i, j: (i, 0)), pl.BlockSpec((None, k, tn), lambda i, j: (j, 0, 0)),
                  bias(0), bias(1), zcol(0), zcol(1)],
        out_specs=[pl.BlockSpec((tm, nj * tn), lambda i, j: (i, 0)), pl.BlockSpec((tm, w), lambda i, j: (i, 0))],
        out_shape=[jax.ShapeDtypeStruct((rows, nj * tn), _MXU), jax.ShapeDtypeStruct((rows, w), _MXU)],
        compiler_params=_params(("arbitrary", "arbitrary")),
    )(yg, w_glu4, b_glu, b_glu, proj, proj)


def _mm_ssm_gate_bwd(d_ys, w_sp4, glu, b_glu, proj, d_proj, *, tm=512):
    rows, w = glu.shape[0], glu.shape[1] // 2
    nk, tk = w_sp4.shape[0], w_sp4.shape[2]
    tm = min(tm, rows)

    def body(dy_ref, w_ref, ga_ref, gb_ref, ba_ref, bb_ref, z0_ref, z1_ref, buf_ref, dg_ref, dz_ref, db_ref, acc):
        i, k = pl.program_id(0), pl.program_id(1)

        @pl.when(k == 0)
        def _():
            acc[...] = jnp.zeros_like(acc)

        acc[...] += lax.dot_general(dy_ref[...].astype(_MXU), w_ref[...].astype(_MXU), _NT, preferred_element_type=F32)

        @pl.when(k == nk - 1)
        def _():
            dv = acc[...]
            a, sb = ga_ref[...].astype(F32) + ba_ref[...], _sigmoid(gb_ref[...].astype(F32) + bb_ref[...])
            f, df = _silu_and_grad(jnp.concatenate([z0_ref[...], z1_ref[...]], axis=1).astype(F32))
            dga = dv * sb * f
            dgb = dv * a * f * sb * (1.0 - sb)
            dg_ref[:, :w] = dga.astype(dg_ref.dtype)
            dg_ref[:, w:] = dgb.astype(dg_ref.dtype)
            dz_ref[...] = (dv * a * sb * df).astype(dz_ref.dtype)
            part = jnp.concatenate([_colsum(dga), _colsum(dgb)], axis=1)

            @pl.when(i == 0)
            def _():
                db_ref[...] = part

            @pl.when(i > 0)
            def _():
                db_ref[...] += part

    half = lambda c: pl.BlockSpec((tm, w), lambda i, k: (i, c))
    bias = lambda c: pl.BlockSpec((1, w), lambda i, k: (0, c))
    zcol = lambda c: pl.BlockSpec((tm, CW), lambda i, k: (i, OFF_Z + c))
    return pl.pallas_call(
        body, name="mm_ssm_gate_bwd", grid=(rows // tm, nk),
        in_specs=[pl.BlockSpec((tm, tk), lambda i, k: (i, k)), pl.BlockSpec((None, w, tk), lambda i, k: (k, 0, 0)),
                  half(0), half(1), bias(0), bias(1), zcol(0), zcol(1), _ANY],
        out_specs=[pl.BlockSpec((tm, 2 * w), lambda i, k: (i, 0)),
                   pl.BlockSpec((pl.Element(tm), pl.Element(w)), lambda i, k: (i * tm, OFF_Z * CW)),
                   pl.BlockSpec((1, 2 * w), lambda i, k: (0, 0))],
        out_shape=[jax.ShapeDtypeStruct((rows, 2 * w), _MXU), jax.ShapeDtypeStruct(d_proj.shape, d_proj.dtype),
                   jax.ShapeDtypeStruct((1, 2 * w), F32)],
        input_output_aliases={8: 1},
        scratch_shapes=[pltpu.VMEM((tm, w), F32)],
        compiler_params=_params(("arbitrary", "arbitrary")),
    )(d_ys, w_sp4, glu, glu, b_glu, b_glu, proj, proj, d_proj)


def _ew(fn, ins, outs, *, rows, ncol, name, n_acc=0, tm=512, deps=(), into=None):
    deps = list(deps) + ([into[1]] if into else [])
    n_in, n_out, nd = len(ins), len(outs), len(deps)
    tm = min(tm, rows)
    in_specs = []
    for _, kind, col0 in ins:
        if kind == "mat":
            in_specs.append(pl.BlockSpec((tm, CW), lambda j, i, c0=col0: (i, c0 + j)))
        else:
            in_specs.append(pl.BlockSpec((1, CW), lambda j, i, c0=col0: (0, c0 + j)))
    out_specs = [pl.BlockSpec((tm, CW), lambda j, i: (i, j)) for _ in outs]
    out_shape = [jax.ShapeDtypeStruct((rows, w), dt) for w, dt in outs]
    if into:
        out_specs[into[0]] = pl.BlockSpec((tm, CW), lambda j, i, c0=into[2]: (i, c0 + j))
        out_shape[into[0]] = jax.ShapeDtypeStruct(into[1].shape, into[1].dtype)
    for _ in range(n_acc):
        out_specs.append(pl.BlockSpec((1, CW), lambda j, i: (0, j)))
        out_shape.append(jax.ShapeDtypeStruct((1, ncol * CW), F32))

    def body(*refs):
        vals = fn(*[r[...] for r in refs[:n_in]])
        refs = refs[n_in + nd:]
        for r, v in zip(refs[:n_out], vals[:n_out]):
            r[...] = v.astype(r.dtype)
        i = pl.program_id(1)
        for r, v in zip(refs[n_out:], vals[n_out:]):
            @pl.when(i == 0)
            def _(r=r, v=v):
                r[...] = v

            @pl.when(i > 0)
            def _(r=r, v=v):
                r[...] += v

    res = pl.pallas_call(
        body, name=name, grid=(ncol, rows // tm), in_specs=in_specs + [_ANY] * nd, out_specs=out_specs,
        out_shape=out_shape, input_output_aliases={n_in + nd - 1: into[0]} if into else {},
        compiler_params=_params(("parallel", "arbitrary")),
    )(*[a for a, _, _ in ins], *deps)
    return res


def _colsum(v):
    return jnp.sum(v, axis=0, keepdims=True)


def _sigmoid(v):
    return jax.nn.sigmoid(v)


def _silu_and_grad(v):
    s = _sigmoid(v)
    return v * s, s * (1.0 + v * (1.0 - s))


def _rms_fwd(x, w, *, tm=512, deps=()):
    rows, d = x.shape
    nd = len(deps)

    def body(x_ref, w_ref, *rest):
        h_ref, r_ref = rest[nd:]
        xv = x_ref[...]
        r = lax.rsqrt(jnp.mean(xv * xv, axis=-1, keepdims=True) + NORM_EPS)
        h_ref[...] = (xv * r * w_ref[...]).astype(h_ref.dtype)
        r_ref[...] = r

    return pl.pallas_call(
        body, name="rms_fwd", grid=(rows // tm,),
        in_specs=[pl.BlockSpec((tm, d), lambda i: (i, 0)), pl.BlockSpec((1, d), lambda i: (0, 0))] + [_ANY] * nd,
        out_specs=[pl.BlockSpec((tm, d), lambda i: (i, 0)), pl.BlockSpec((tm, 1), lambda i: (i, 0))],
        out_shape=[jax.ShapeDtypeStruct((rows, d), _MXU), jax.ShapeDtypeStruct((rows, 1), F32)],
        compiler_params=_params(("arbitrary",)),
    )(x, w, *deps)


def _rms_bwd(dh, x, rstd, w, dout, *, tm=256):
    rows, d = x.shape

    def body(dh_ref, x_ref, r_ref, w_ref, do_ref, gx_ref, gw_ref):
        dhv, xv, r, wv = dh_ref[...], x_ref[...], r_ref[...], w_ref[...]
        xr = xv * r
        t = jnp.mean(dhv * wv * xr, axis=-1, keepdims=True)
        gx_ref[...] = do_ref[...] + r * (wv * dhv - xr * t)
        part = _colsum(dhv * xr)
        i = pl.program_id(0)

        @pl.when(i == 0)
        def _():
            gw_ref[...] = part

        @pl.when(i > 0)
        def _():
            gw_ref[...] += part

    return pl.pallas_call(
        body, name="rms_bwd", grid=(rows // tm,),
        in_specs=[pl.BlockSpec((tm, d), lambda i: (i, 0)), pl.BlockSpec((tm, d), lambda i: (i, 0)),
                  pl.BlockSpec((tm, 1), lambda i: (i, 0)), pl.BlockSpec((1, d), lambda i: (0, 0)),
                  pl.BlockSpec((tm, d), lambda i: (i, 0))],
        out_specs=[pl.BlockSpec((tm, d), lambda i: (i, 0)), pl.BlockSpec((1, d), lambda i: (0, 0))],
        out_shape=[jax.ShapeDtypeStruct((rows, d), F32), jax.ShapeDtypeStruct((1, d), F32)],
        compiler_params=_params(("arbitrary",)),
    )(dh, x, rstd, w, dout)


_NT = (((1,), (1,)), ((), ()))
_TN = (((0,), (0,)), ((), ()))


QKV_W = ATTN_W + 2 * KV_W
HEADS_PER_TILE = LANES // HEAD_DIM


def _low_half(rows):
    return lax.broadcasted_iota(jnp.int32, (rows, LANES), 1) < HEAD_DIM


def _pair_mean(t, low):
    m_lo = jnp.sum(jnp.where(low, t, 0.0), axis=-1, keepdims=True)
    m_hi = jnp.sum(jnp.where(low, 0.0, t), axis=-1, keepdims=True)
    return jnp.where(low, m_lo, m_hi) * (1.0 / HEAD_DIM)


def _pair_rstd(t, low):
    return lax.rsqrt(_pair_mean(t * t, low) + NORM_EPS)


def _dup_half(t, hi, low):
    swapped = pltpu.roll(t, HEAD_DIM, 1)
    return jnp.where(low, swapped, t) if hi else jnp.where(low, t, swapped)


def _fold_halves(t):
    return t + pltpu.roll(t, HEAD_DIM, 1)


def _split_heads(t, low):
    return [jnp.where(low, t, 0.0), jnp.where(low, 0.0, t)]


def _stacked_band_mask(n):
    rows = Q_PER_KV * WINDOW
    qi = lax.broadcasted_iota(jnp.int32, (rows, 2 * WINDOW), 0) % WINDOW + WINDOW
    kj = lax.broadcasted_iota(jnp.int32, (rows, 2 * WINDOW), 1)
    diff = qi - kj
    first_key = jnp.where(n > 0, 0, WINDOW)
    return (diff >= 0) & (diff < WINDOW) & (kj >= first_key)


def _stacked_sinks(sink_ref, g):
    blk = lax.broadcasted_iota(jnp.int32, (Q_PER_KV * WINDOW, 1), 0) // WINDOW
    col = jnp.full((Q_PER_KV * WINDOW, 1), sink_ref[Q_PER_KV * g], F32)
    for r in range(1, Q_PER_KV):
        col = jnp.where(blk == r, sink_ref[Q_PER_KV * g + r], col)
    return col


def _attn_in_specs(nblk, rev):
    def cur(n):
        return (nblk - 1 - n) if rev else n

    q_spec = pl.BlockSpec((WINDOW, ATTN_W), lambda n: (cur(n), 0))
    kvc_spec = pl.BlockSpec((WINDOW, 2 * KV_W), lambda n: (cur(n), ATTN_W // (2 * KV_W)))
    kvp_spec = pl.BlockSpec((WINDOW, 2 * KV_W), lambda n: (jnp.maximum(cur(n) - 1, 0), ATTN_W // (2 * KV_W)))
    w_spec = pl.BlockSpec((1, LANES), lambda n: (0, 0))
    l_spec = pl.BlockSpec((WINDOW, N_Q_HEADS), lambda n: (cur(n), 0))
    gate_specs = [pl.BlockSpec((WINDOW, CW), lambda n, col=OFF_AGATE + j: (cur(n), col)) for j in range(ATTN_W // CW)]
    return q_spec, kvc_spec, kvp_spec, w_spec, l_spec, gate_specs


def _attn2_fwd(proj, qw2, kw2, sinks, deps=()):
    seq = proj.shape[0]
    nblk = seq // WINDOW
    scale = 1.0 / math.sqrt(HEAD_DIM)
    q_spec, kvc_spec, kvp_spec, w_spec, l_spec, gate_specs = _attn_in_specs(nblk, False)
    nd, ng = len(deps), len(gate_specs)

    def body(sink_ref, q_ref, kvc_ref, kvp_ref, qw_ref, kw_ref, *rest):
        gate_refs = rest[:ng]
        o_ref, lse_ref, ya_ref = rest[ng + nd:]
        n = pl.program_id(0)
        low, low2 = _low_half(WINDOW), _low_half(2 * WINDOW)
        valid = _stacked_band_mask(n)
        head_lane = lax.broadcasted_iota(jnp.int32, (WINDOW, N_Q_HEADS), 1)
        kv = jnp.concatenate([kvp_ref[...], kvc_ref[...]], axis=0).astype(F32)
        qwv, kwv = qw_ref[...], kw_ref[...]
        lse_blk = jnp.zeros((WINDOW, N_Q_HEADS), F32)
        for t in range(N_KV_HEADS // HEADS_PER_TILE):
            kt = kv[:, t * LANES:(t + 1) * LANES]
            vt = kv[:, KV_W + t * LANES:KV_W + (t + 1) * LANES]
            kn = kt * _pair_rstd(kt, low2) * kwv
            for hi in range(HEADS_PER_TILE):
                g = HEADS_PER_TILE * t + hi
                kdup = _dup_half(kn, hi, low2).astype(_MXU)
                vdup = _dup_half(vt, hi, low2).astype(_MXU)
                stack = []
                for tq in (2 * g, 2 * g + 1):
                    qt = q_ref[:, tq * LANES:(tq + 1) * LANES].astype(F32)
                    stack += _split_heads(qt * _pair_rstd(qt, low) * qwv, low)
                qs = jnp.concatenate(stack, axis=0).astype(_MXU)
                s = lax.dot_general(qs, kdup, _NT, preferred_element_type=F32) * scale
                s = jnp.where(valid, s, -1e30)
                sink = _stacked_sinks(sink_ref, g)
                m = jnp.maximum(jnp.max(s, axis=-1, keepdims=True), sink)
                e = jnp.exp(s - m)
                z = jnp.sum(e, axis=-1, keepdims=True) + jnp.exp(sink - m)
                o = jnp.dot((e / z).astype(_MXU), vdup, preferred_element_type=F32)
                for i, tq in enumerate((2 * g, 2 * g + 1)):
                    tile = slice(tq * LANES, (tq + 1) * LANES)
                    out = jnp.where(low, o[2 * i * WINDOW:(2 * i + 1) * WINDOW],
                                    o[(2 * i + 1) * WINDOW:(2 * i + 2) * WINDOW])
                    gate = gate_refs[tq * LANES // CW][:, tq * LANES % CW:tq * LANES % CW + LANES].astype(F32)
                    o_ref[:, tile] = out.astype(o_ref.dtype)
                    ya_ref[:, tile] = (out * (gate * _sigmoid(gate))).astype(ya_ref.dtype)
                lse = m + jnp.log(z)
                for r in range(Q_PER_KV):
                    lse_blk = jnp.where(head_lane == Q_PER_KV * g + r, lse[r * WINDOW:(r + 1) * WINDOW], lse_blk)
        lse_ref[...] = lse_blk

    return pl.pallas_call(
        body, name="attn_fwd", grid=(nblk,),
        in_specs=[pl.BlockSpec(memory_space=pltpu.SMEM), q_spec, kvc_spec, kvp_spec, w_spec, w_spec] + gate_specs
        + [_ANY] * nd,
        out_specs=[q_spec, l_spec, q_spec],
        out_shape=[jax.ShapeDtypeStruct((seq, ATTN_W), _MXU), jax.ShapeDtypeStruct((seq, N_Q_HEADS), F32),
                   jax.ShapeDtypeStruct((seq, ATTN_W), _MXU)],
        compiler_params=_params(("arbitrary",)),
    )(sinks, proj, proj, proj, qw2, kw2, *([proj] * ng), *deps)


def _attn2_bwd(proj, qw2, kw2, sinks, lse, attn, dya, d_proj, deps=()):
    seq = proj.shape[0]
    nblk = seq // WINDOW
    scale = 1.0 / math.sqrt(HEAD_DIM)
    q_spec, kvc_spec, kvp_spec, w_spec, l_spec, gate_specs = _attn_in_specs(nblk, True)
    s_spec = pl.BlockSpec((1, N_Q_HEADS), lambda n: (0, 0))
    d_spec = pl.BlockSpec((WINDOW, QKV_W + ATTN_W), lambda n: (nblk - 1 - n, 0))
    deps = list(deps) + [d_proj]
    nd, ng = len(deps), len(gate_specs)

    def body(sink_ref, q_ref, kvc_ref, kvp_ref, qw_ref, kw_ref, lse_ref, attn_ref, dya_ref, *rest):
        gate_refs = rest[:ng]
        d_ref, dqw_ref, dkw_ref, dsk_ref, carry, do_ref = rest[ng + nd:]
        step = pl.program_id(0)
        n = nblk - 1 - step

        @pl.when(step == 0)
        def _():
            carry[...] = jnp.zeros_like(carry)
            dqw_ref[...] = jnp.zeros_like(dqw_ref)
            dkw_ref[...] = jnp.zeros_like(dkw_ref)
            dsk_ref[...] = jnp.zeros_like(dsk_ref)

        for j, g_ref in enumerate(gate_refs):
            cols = slice(j * CW, (j + 1) * CW)
            f, df = _silu_and_grad(g_ref[...].astype(F32))
            dv = dya_ref[:, cols]
            do_ref[:, cols] = dv * f
            d_ref[:, QKV_W + j * CW:QKV_W + (j + 1) * CW] = (dv * attn_ref[:, cols].astype(F32) * df).astype(d_ref.dtype)

        low, low2 = _low_half(WINDOW), _low_half(2 * WINDOW)
        valid = _stacked_band_mask(n)
        head_lane = lax.broadcasted_iota(jnp.int32, (WINDOW, N_Q_HEADS), 1)
        sink_lane = lax.broadcasted_iota(jnp.int32, (1, N_Q_HEADS), 1)
        kv = jnp.concatenate([kvp_ref[...], kvc_ref[...]], axis=0).astype(F32)
        qwv, kwv = qw_ref[...], kw_ref[...]
        lse_blk = lse_ref[...]
        dqw = jnp.zeros((1, LANES), F32)
        dkw = jnp.zeros((1, LANES), F32)
        dsk = jnp.zeros((1, N_Q_HEADS), F32)
        for t in range(N_KV_HEADS // HEADS_PER_TILE):
            kt = kv[:, t * LANES:(t + 1) * LANES]
            vt = kv[:, KV_W + t * LANES:KV_W + (t + 1) * LANES]
            rk = _pair_rstd(kt, low2)
            kn = kt * rk * kwv
            dkn_t = jnp.zeros((2 * WINDOW, LANES), F32)
            dv_t = jnp.zeros((2 * WINDOW, LANES), F32)
            for hi in range(HEADS_PER_TILE):
                g = HEADS_PER_TILE * t + hi
                kdup = _dup_half(kn, hi, low2).astype(_MXU)
                vdup = _dup_half(vt, hi, low2).astype(_MXU)
                tiles = (2 * g, 2 * g + 1)
                qx, rq, stack, dstack, lse_rows = [], [], [], [], []
                for tq in tiles:
                    qt = q_ref[:, tq * LANES:(tq + 1) * LANES].astype(F32)
                    r = _pair_rstd(qt, low)
                    rq.append(r)
                    qx.append(qt * r)
                    stack += _split_heads(qx[-1] * qwv, low)
                    dstack += _split_heads(do_ref[:, tq * LANES:(tq + 1) * LANES], low)
                for r in range(Q_PER_KV):
                    lse_rows.append(jnp.sum(jnp.where(head_lane == Q_PER_KV * g + r, lse_blk, 0.0), axis=-1, keepdims=True))
                qs = jnp.concatenate(stack, axis=0).astype(_MXU)
                dos = jnp.concatenate(dstack, axis=0).astype(_MXU)
                lse_col = jnp.concatenate(lse_rows, axis=0)
                s = lax.dot_general(qs, kdup, _NT, preferred_element_type=F32) * scale
                s = jnp.where(valid, s, -1e30)
                p = jnp.exp(s - lse_col)
                dp = lax.dot_general(dos, vdup, _NT, preferred_element_type=F32)
                dsum = jnp.sum(p * dp, axis=-1, keepdims=True)
                ds = (p * (dp - dsum) * scale).astype(_MXU)
                dsink = -jnp.exp(_stacked_sinks(sink_ref, g) - lse_col) * dsum
                for r in range(Q_PER_KV):
                    dsk = dsk + jnp.where(sink_lane == Q_PER_KV * g + r, _colsum(dsink[r * WINDOW:(r + 1) * WINDOW]), 0.0)
                dv_g = _fold_halves(lax.dot_general(p.astype(_MXU), dos, _TN, preferred_element_type=F32))
                dkn_g = _fold_halves(lax.dot_general(ds, qs, _TN, preferred_element_type=F32))
                dv_t = jnp.where(low2, dv_t, dv_g) if hi else jnp.where(low2, dv_g, dv_t)
                dkn_t = jnp.where(low2, dkn_t, dkn_g) if hi else jnp.where(low2, dkn_g, dkn_t)
                dqn = jnp.dot(ds, kdup, preferred_element_type=F32)
                for i, tq in enumerate(tiles):
                    dqn_t = jnp.where(low, dqn[2 * i * WINDOW:(2 * i + 1) * WINDOW],
                                      dqn[(2 * i + 1) * WINDOW:(2 * i + 2) * WINDOW])
                    dq = rq[i] * (qwv * dqn_t - qx[i] * _pair_mean(dqn_t * qwv * qx[i], low))
                    d_ref[:, tq * LANES:(tq + 1) * LANES] = dq.astype(d_ref.dtype)
                    dqw = dqw + _colsum(dqn_t * qx[i])
            k_cols = slice(t * LANES, (t + 1) * LANES)
            v_cols = slice(KV_W + t * LANES, KV_W + (t + 1) * LANES)
            dkn_c = dkn_t[WINDOW:] + carry[:, k_cols]
            rc = rk[WINDOW:]
            kx = kt[WINDOW:] * rc
            dk = rc * (kwv * dkn_c - kx * _pair_mean(dkn_c * kwv * kx, low))
            d_ref[:, ATTN_W + t * LANES:ATTN_W + (t + 1) * LANES] = dk.astype(d_ref.dtype)
            d_ref[:, ATTN_W + KV_W + t * LANES:ATTN_W + KV_W + (t + 1) * LANES] = (
                dv_t[WINDOW:] + carry[:, v_cols]).astype(d_ref.dtype)
            carry[:, k_cols] = dkn_t[:WINDOW]
            carry[:, v_cols] = dv_t[:WINDOW]
            dkw = dkw + _colsum(dkn_c * kx)
        dqw_ref[...] += dqw
        dkw_ref[...] += dkw
        dsk_ref[...] += dsk

    return pl.pallas_call(
        body, name="attn_bwd", grid=(nblk,),
        in_specs=[pl.BlockSpec(memory_space=pltpu.SMEM), q_spec, kvc_spec, kvp_spec, w_spec, w_spec, l_spec, q_spec,
                  q_spec] + gate_specs + [_ANY] * nd,
        out_specs=[d_spec, w_spec, w_spec, s_spec],
        out_shape=[jax.ShapeDtypeStruct(d_proj.shape, d_proj.dtype), jax.ShapeDtypeStruct((1, LANES), F32),
                   jax.ShapeDtypeStruct((1, LANES), F32), jax.ShapeDtypeStruct((1, N_Q_HEADS), F32)],
        input_output_aliases={9 + ng + nd - 1: 0},
        scratch_shapes=[pltpu.VMEM((WINDOW, 2 * KV_W), F32), pltpu.VMEM((WINDOW, ATTN_W), F32)],
        compiler_params=_params(("arbitrary",)),
    )(sinks, proj, proj, proj, qw2, kw2, lse, attn, dya, *([proj] * ng), *deps)


def _ssm_discretise(a_re, a_im, log_dt):
    dt = jnp.exp(log_dt)
    mag = jnp.exp(dt * a_re)
    ab_re = mag * jnp.cos(dt * a_im)
    ab_im = mag * jnp.sin(dt * a_im)
    num_re = ab_re - 1.0
    num_im = ab_im
    den = a_re * a_re + a_im * a_im
    cf_re = (num_re * a_re + num_im * a_im) / den
    cf_im = (num_im * a_re - num_re * a_im) / den
    return ab_re, ab_im, cf_re, cf_im


def _ssm_params_fwd(a_re, a_im, log_dt):
    shp = jax.ShapeDtypeStruct(a_re.shape, F32)

    def body(are_ref, aim_ref, ldt_ref, abr_ref, abi_ref, cfr_ref, cfi_ref, alr_ref, ali_ref):
        abr, abi, cfr, cfi = _ssm_discretise(are_ref[...], aim_ref[...], ldt_ref[...])
        abr_ref[...], abi_ref[...], cfr_ref[...], cfi_ref[...] = abr, abi, cfr, cfi
        pr, pi = abr, abi
        for _ in range(int(math.log2(SSM_L))):
            pr, pi = pr * pr - pi * pi, 2.0 * pr * pi
        alr_ref[...], ali_ref[...] = pr, pi

    return pl.pallas_call(body, name="ssm_params_fwd", out_shape=[shp] * 6)(a_re, a_im, log_dt)


def _ssm_params_bwd(a_re, a_im, log_dt, d_abr, d_abi, d_cfr, d_cfi):
    def body(are_ref, aim_ref, ldt_ref, g0, g1, g2, g3, dare_ref, daim_ref, dldt_ref):
        _, vjp = jax.vjp(_ssm_discretise, are_ref[...], aim_ref[...], ldt_ref[...])
        dare_ref[...], daim_ref[...], dldt_ref[...] = vjp((g0[...], g1[...], g2[...], g3[...]))

    return pl.pallas_call(
        body, name="ssm_params_bwd",
        out_shape=[jax.ShapeDtypeStruct(a_re.shape, F32), jax.ShapeDtypeStruct(a_im.shape, F32),
                   jax.ShapeDtypeStruct(log_dt.shape, F32)],
    )(a_re, a_im, log_dt, d_abr, d_abi, d_cfr, d_cfi)


def _scan_cols(j):
    return pl.ds(j * SSM_SB, SSM_SB)


def _rows8(r):
    return pl.ds(pl.multiple_of(r * SUBLANES, SUBLANES), SUBLANES)


def _bcast8(row):
    return jnp.broadcast_to(row, (SUBLANES, row.shape[-1]))


def _token_order_pick():
    tok = lax.broadcasted_iota(jnp.int32, (SSM_T, SSM_T), 0)
    row = lax.broadcasted_iota(jnp.int32, (SSM_T, SSM_T), 1)
    return (row == SUBLANES * (tok % SSM_L) + tok // SSM_L).astype(_MXU)


SCAN_UNROLL = 8


def _scan_loop(n, step, init):
    def trip(o, carry):
        for i in range(SCAN_UNROLL):
            carry = step(o * SCAN_UNROLL + i, carry)
        return carry

    return lax.fori_loop(0, n // SCAN_UNROLL, trip, init)


def _ssm_fwd(u, b_re, b_im, c_re, c_im, d_skip, coef):
    seq = u.shape[0]
    nc = seq // SSM_T
    T, L = SSM_T, SSM_L

    def body(u_ref, bre_ref, bim_ref, cre_ref, cim_ref, d_ref, are_ref, aim_ref, cfr_ref, cfi_ref, alr_ref, ali_ref,
             y_ref, yg_ref, sre_ref, sim_ref, ire_ref, iim_ref, car_re, car_im, end_re, end_im, yg_scan):
        c = pl.program_id(0)

        @pl.when(c == 0)
        def _():
            car_re[...] = jnp.zeros_like(car_re)
            car_im[...] = jnp.zeros_like(car_im)

        for j in range(SSM_JB):
            ub = u_ref[:, j * LANES:(j + 1) * LANES].astype(_MXU)
            bur = jnp.dot(ub, bre_ref[j], preferred_element_type=F32)
            bui = jnp.dot(ub, bim_ref[j], preferred_element_type=F32)
            cfr, cfi = cfr_ref[:, _scan_cols(j)], cfi_ref[:, _scan_cols(j)]
            sre_ref[:, _scan_cols(j)] = cfr * bur - cfi * bui
            sim_ref[:, _scan_cols(j)] = cfr * bui + cfi * bur

        for j in range(SSM_JB):
            cols = _scan_cols(j)
            ar, ai = _bcast8(are_ref[:, cols]), _bcast8(aim_ref[:, cols])

            def step1(r, s, cols=cols, ar=ar, ai=ai):
                sr, si = s
                rows = _rows8(r)
                return (ar * sr - ai * si + sre_ref[rows, cols], ar * si + ai * sr + sim_ref[rows, cols])

            zero = jnp.zeros((SUBLANES, SSM_SB), F32)
            er, ei = _scan_loop(L, step1, (zero, zero))
            end_re[:, cols] = er
            end_im[:, cols] = ei

        alr, ali = alr_ref[...], ali_ref[...]
        cr, ci = car_re[...], car_im[...]
        ire_ref[0:1, :] = cr
        iim_ref[0:1, :] = ci
        for i in range(1, SUBLANES):
            er, ei = end_re[i - 1:i, :], end_im[i - 1:i, :]
            cr, ci = alr * cr - ali * ci + er, alr * ci + ali * cr + ei
            ire_ref[i:i + 1, :] = cr
            iim_ref[i:i + 1, :] = ci

        for j in range(SSM_JB):
            cols = _scan_cols(j)
            ar, ai = _bcast8(are_ref[:, cols]), _bcast8(aim_ref[:, cols])

            def step2(r, s, cols=cols, ar=ar, ai=ai):
                sr, si = s
                rows = _rows8(r)
                nr = ar * sr - ai * si + sre_ref[rows, cols]
                ni = ar * si + ai * sr + sim_ref[rows, cols]
                sre_ref[rows, cols] = nr
                sim_ref[rows, cols] = ni
                return nr, ni

            _scan_loop(L, step2, (ire_ref[:, cols], iim_ref[:, cols]))

        car_re[...] = sre_ref[T - 1:T, :]
        car_im[...] = sim_ref[T - 1:T, :]

        for j in range(SSM_JB):
            cols = _scan_cols(j)
            ch = slice(j * LANES, (j + 1) * LANES)
            y = (jnp.dot(sre_ref[:, cols].astype(_MXU), cre_ref[j], preferred_element_type=F32)
                 - jnp.dot(sim_ref[:, cols].astype(_MXU), cim_ref[j], preferred_element_type=F32))
            y = y + d_ref[:, ch] * u_ref[:, ch].astype(F32)
            y_ref[:, ch] = y
            yg_scan[:, ch] = jax.nn.gelu(y).astype(yg_scan.dtype)
        yg_ref[...] = jnp.dot(_token_order_pick(), yg_scan[...], preferred_element_type=F32).astype(yg_ref.dtype)

    tok = pl.BlockSpec((T, SSM_W), lambda c: (c, 0))
    st = pl.BlockSpec((T, N_STATES), lambda c: (c, 0))
    ini = pl.BlockSpec((None, SUBLANES, N_STATES), lambda c: (c, 0, 0))
    bsp = pl.BlockSpec((SSM_JB, LANES, SSM_SB), lambda c: (0, 0, 0))
    csp = pl.BlockSpec((SSM_JB, SSM_SB, LANES), lambda c: (0, 0, 0))
    row_w = pl.BlockSpec((1, SSM_W), lambda c: (0, 0))
    row_s = pl.BlockSpec((1, N_STATES), lambda c: (0, 0))
    return pl.pallas_call(
        body, name="ssm_fwd", grid=(nc,),
        in_specs=[tok, bsp, bsp, csp, csp, row_w] + [row_s] * 6,
        out_specs=[tok, tok, st, st, ini, ini],
        out_shape=[jax.ShapeDtypeStruct((seq, SSM_W), F32), jax.ShapeDtypeStruct((seq, SSM_W), _MXU),
                   jax.ShapeDtypeStruct((seq, N_STATES), F32), jax.ShapeDtypeStruct((seq, N_STATES), F32),
                   jax.ShapeDtypeStruct((nc, SUBLANES, N_STATES), F32),
                   jax.ShapeDtypeStruct((nc, SUBLANES, N_STATES), F32)],
        scratch_shapes=[pltpu.VMEM((1, N_STATES), F32), pltpu.VMEM((1, N_STATES), F32),
                        pltpu.VMEM((SUBLANES, N_STATES), F32), pltpu.VMEM((SUBLANES, N_STATES), F32),
                        pltpu.VMEM((T, SSM_W), _MXU)],
        compiler_params=_params(("arbitrary",)),
    )(u, b_re, b_im, c_re, c_im, d_skip, *coef)


def _ssm_bwd(dyg, y, u, s_re, s_im, i_re, i_im, b_re, b_im, c_re, c_im, d_skip, coef, d_proj, deps=()):
    seq = u.shape[0]
    nc = seq // SSM_T
    T, L = SSM_T, SSM_L
    deps = list(deps) + [d_proj]

    def body(dyg_ref, y_ref, u_ref, sre_ref, sim_ref, ire_ref, iim_ref, bre_ref, bim_ref, cre_ref, cim_ref, d_ref,
             are_ref, aim_ref, cfr_ref, cfi_ref, alr_ref, ali_ref, *rest):
        (du_ref, dbre_out, dbim_out, dcre_out, dcim_out, dd_ref, dar_ref, dai_ref, dcfr_ref, dcfi_ref,
         lre, lim, car_re, car_im, end_re, end_im, ini_re, ini_im, dbre_ref, dbim_ref, dcre_ref, dcim_ref,
         dy_ref, du_scan) = rest[len(deps):]
        step = pl.program_id(0)
        dy_ref[...] = jax.vjp(jax.nn.gelu, y_ref[...])[1](dyg_ref[...])[0]

        @pl.when(step == 0)
        def _():
            car_re[...] = jnp.zeros_like(car_re)
            car_im[...] = jnp.zeros_like(car_im)
            for ref in (dbre_ref, dbim_ref, dcre_ref, dcim_ref, dd_ref, dar_ref, dai_ref, dcfr_ref, dcfi_ref):
                ref[...] = jnp.zeros_like(ref)

        for j in range(SSM_JB):
            dyb = dy_ref[:, j * LANES:(j + 1) * LANES].astype(_MXU)
            lre[:, _scan_cols(j)] = lax.dot_general(dyb, cre_ref[j], _NT, preferred_element_type=F32)
            lim[:, _scan_cols(j)] = -lax.dot_general(dyb, cim_ref[j], _NT, preferred_element_type=F32)

        for j in range(SSM_JB):
            cols = _scan_cols(j)
            ar, ai = _bcast8(are_ref[:, cols]), _bcast8(aim_ref[:, cols])

            def step1(t, s, cols=cols, ar=ar, ai=ai):
                sr, si = s
                rows = _rows8(L - 1 - t)
                return (ar * sr + ai * si + lre[rows, cols], ar * si - ai * sr + lim[rows, cols])

            zero = jnp.zeros((SUBLANES, SSM_SB), F32)
            er, ei = _scan_loop(L, step1, (zero, zero))
            end_re[:, cols] = er
            end_im[:, cols] = ei

        alr, ali = alr_ref[...], ali_ref[...]
        cr, ci = car_re[...], car_im[...]
        ini_re[SUBLANES - 1:SUBLANES, :] = cr
        ini_im[SUBLANES - 1:SUBLANES, :] = ci
        for i in range(SUBLANES - 2, -1, -1):
            er, ei = end_re[i + 1:i + 2, :], end_im[i + 1:i + 2, :]
            cr, ci = alr * cr + ali * ci + er, alr * ci - ali * cr + ei
            ini_re[i:i + 1, :] = cr
            ini_im[i:i + 1, :] = ci

        for j in range(SSM_JB):
            cols = _scan_cols(j)
            ar, ai = _bcast8(are_ref[:, cols]), _bcast8(aim_ref[:, cols])

            def step2(t, s, cols=cols, ar=ar, ai=ai):
                sr, si = s
                rows = _rows8(L - 1 - t)
                nr = ar * sr + ai * si + lre[rows, cols]
                ni = ar * si - ai * sr + lim[rows, cols]
                lre[rows, cols] = nr
                lim[rows, cols] = ni
                return nr, ni

            _scan_loop(L, step2, (ini_re[:, cols], ini_im[:, cols]))

        car_re[...] = lre[0:1, :]
        car_im[...] = lim[0:1, :]

        head, tail, body_rows = slice(0, SUBLANES), slice(SUBLANES, T), slice(0, T - SUBLANES)
        for j in range(SSM_JB):
            cols = _scan_cols(j)
            ch = slice(j * LANES, (j + 1) * LANES)
            lr, li = lre[:, cols], lim[:, cols]
            lt_r, lt_i, sp_r, sp_i = lre[tail, cols], lim[tail, cols], sre_ref[body_rows, cols], sim_ref[body_rows, cols]
            lh_r, lh_i, si_r, si_i = lre[head, cols], lim[head, cols], ire_ref[:, cols], iim_ref[:, cols]
            dar_ref[:, cols] += _colsum(lt_r * sp_r + lt_i * sp_i) + _colsum(lh_r * si_r + lh_i * si_i)
            dai_ref[:, cols] += _colsum(lt_i * sp_r - lt_r * sp_i) + _colsum(lh_i * si_r - lh_r * si_i)
            ub = u_ref[:, ch].astype(_MXU)
            uf = ub.astype(F32)
            bur = jnp.dot(ub, bre_ref[j], preferred_element_type=F32)
            bui = jnp.dot(ub, bim_ref[j], preferred_element_type=F32)
            dcfr_ref[:, cols] += _colsum(lr * bur + li * bui)
            dcfi_ref[:, cols] += _colsum(li * bur - lr * bui)
            cfr, cfi = cfr_ref[:, cols], cfi_ref[:, cols]
            dbur = (cfr * lr + cfi * li).astype(_MXU)
            dbui = (cfr * li - cfi * lr).astype(_MXU)
            dyf = dy_ref[:, ch]
            dyb = dyf.astype(_MXU)
            du = (lax.dot_general(dbur, bre_ref[j], _NT, preferred_element_type=F32)
                  + lax.dot_general(dbui, bim_ref[j], _NT, preferred_element_type=F32) + d_ref[:, ch] * dyf)
            du_scan[:, ch] = du.astype(du_scan.dtype)
            dbre_ref[j] += lax.dot_general(ub, dbur, _TN, preferred_element_type=F32)
            dbim_ref[j] += lax.dot_general(ub, dbui, _TN, preferred_element_type=F32)
            dcre_ref[j] += lax.dot_general(sre_ref[:, cols].astype(_MXU), dyb, _TN, preferred_element_type=F32)
            dcim_ref[j] -= lax.dot_general(sim_ref[:, cols].astype(_MXU), dyb, _TN, preferred_element_type=F32)
            dd_ref[:, ch] += _colsum(dyf * uf)
        du_ref[...] = jnp.dot(_token_order_pick(), du_scan[...], preferred_element_type=F32).astype(du_ref.dtype)

        @pl.when(step == nc - 1)
        def _():
            for acc, out in ((dbre_ref, dbre_out), (dbim_ref, dbim_out), (dcre_ref, dcre_out), (dcim_ref, dcim_out)):
                pltpu.sync_copy(acc, out)

    tok = pl.BlockSpec((T, SSM_W), lambda c: (nc - 1 - c, 0))
    st = pl.BlockSpec((T, N_STATES), lambda c: (nc - 1 - c, 0))
    ini = pl.BlockSpec((None, SUBLANES, N_STATES), lambda c: (nc - 1 - c, 0, 0))
    bsp = pl.BlockSpec((SSM_JB, LANES, SSM_SB), lambda c: (0, 0, 0))
    csp = pl.BlockSpec((SSM_JB, SSM_SB, LANES), lambda c: (0, 0, 0))
    row_w = pl.BlockSpec((1, SSM_W), lambda c: (0, 0))
    row_s = pl.BlockSpec((1, N_STATES), lambda c: (0, 0))
    big = pltpu.VMEM((T, N_STATES), F32)
    one = pltpu.VMEM((1, N_STATES), F32)
    eight = pltpu.VMEM((SUBLANES, N_STATES), F32)
    return pl.pallas_call(
        body, name="ssm_bwd", grid=(nc,),
        in_specs=[tok, tok, tok, st, st, ini, ini, bsp, bsp, csp, csp, row_w] + [row_s] * 6 + [_ANY] * len(deps),
        out_specs=[pl.BlockSpec((pl.Element(T), pl.Element(SSM_W)), lambda c: ((nc - 1 - c) * T, OFF_U * CW)),
                   _ANY, _ANY, _ANY, _ANY, row_w, row_s, row_s, row_s, row_s],
        input_output_aliases={18 + len(deps) - 1: 0},
        out_shape=[jax.ShapeDtypeStruct(d_proj.shape, d_proj.dtype),
                   jax.ShapeDtypeStruct((SSM_JB, LANES, SSM_SB), F32), jax.ShapeDtypeStruct((SSM_JB, LANES, SSM_SB), F32),
                   jax.ShapeDtypeStruct((SSM_JB, SSM_SB, LANES), F32), jax.ShapeDtypeStruct((SSM_JB, SSM_SB, LANES), F32),
                   jax.ShapeDtypeStruct((1, SSM_W), F32)] + [jax.ShapeDtypeStruct((1, N_STATES), F32)] * 4,
        scratch_shapes=[big, big, one, one, eight, eight, eight, eight,
                        pltpu.VMEM((SSM_JB, LANES, SSM_SB), F32), pltpu.VMEM((SSM_JB, LANES, SSM_SB), F32),
                        pltpu.VMEM((SSM_JB, SSM_SB, LANES), F32), pltpu.VMEM((SSM_JB, SSM_SB, LANES), F32),
                        pltpu.VMEM((T, SSM_W), F32), pltpu.VMEM((T, SSM_W), _MXU)],
        compiler_params=_params(("arbitrary",)),
    )(dyg, y, u, s_re, s_im, i_re, i_im, b_re, b_im, c_re, c_im, d_skip, *coef, *deps)


def _block_diag_b(b):
    t = b.reshape(SSM_JB, 8, STATE, GROUP).transpose(0, 1, 3, 2)
    eye = jnp.eye(8, dtype=b.dtype)
    return (t[:, :, :, None, :] * eye[None, :, None, :, None]).reshape(SSM_JB, LANES, SSM_SB)


def _block_diag_c(c):
    t = c.reshape(SSM_JB, 8, GROUP, STATE).transpose(0, 1, 3, 2)
    eye = jnp.eye(8, dtype=c.dtype)
    return (t[:, :, :, None, :] * eye[None, :, None, :, None]).reshape(SSM_JB, SSM_SB, LANES)


def _diag_of_b(blk):
    t = blk.reshape(SSM_JB, 8, GROUP, 8, STATE)
    d = jnp.sum(t * jnp.eye(8, dtype=blk.dtype)[None, :, None, :, None], axis=3)
    return d.transpose(0, 1, 3, 2).reshape(N_GROUPS, STATE, GROUP)


def _diag_of_c(blk):
    t = blk.reshape(SSM_JB, 8, STATE, 8, GROUP)
    d = jnp.sum(t * jnp.eye(8, dtype=blk.dtype)[None, :, None, :, None], axis=3)
    return d.transpose(0, 1, 3, 2).reshape(N_GROUPS, GROUP, STATE)


def _to_scan_order(v):
    seq, w = v.shape
    return v.reshape(seq // SSM_T, SUBLANES, SSM_L, w).transpose(0, 2, 1, 3).reshape(seq, w)


def _adamw_math(w, g, m, v):
    nm = ADAM_B1 * m + (1.0 - ADAM_B1) * g
    nv = ADAM_B2 * v + (1.0 - ADAM_B2) * jnp.square(g)
    m_hat = nm / (1.0 - ADAM_B1 ** ADAM_STEP)
    v_hat = nv / (1.0 - ADAM_B2 ** ADAM_STEP)
    return -ADAM_LR * (m_hat / (jnp.sqrt(v_hat) + ADAM_EPS) + ADAM_WD * w), nm, nv


def _adamw(w, g, m, v, *, name, tm, deps=()):
    rows, cols = w.shape
    nd = len(deps)

    def body(w_ref, g_ref, m_ref, v_ref, *rest):
        d_ref, nm_ref, nv_ref = rest[nd:]
        d_ref[...], nm_ref[...], nv_ref[...] = _adamw_math(w_ref[...], g_ref[...], m_ref[...], v_ref[...])

    spec = pl.BlockSpec((tm, cols), lambda i: (i, 0))
    shp = jax.ShapeDtypeStruct((rows, cols), F32)
    return pl.pallas_call(body, name=name, grid=(rows // tm,), in_specs=[spec] * 4 + [_ANY] * nd,
                          out_specs=[spec] * 3, out_shape=[shp] * 3,
                          compiler_params=_params(("arbitrary",)))(w, g, m, v, *deps)


def _place():
    x, y, c = lax.axis_index("x"), lax.axis_index("y"), lax.axis_index("c")
    chips = [(1 - x, y), (x, 1 - y), (1 - x, 1 - y)]
    return x, y, c, chips


def _remote(src, dst, send_sem, recv_sem, dev):
    return pltpu.make_async_remote_copy(src_ref=src, dst_ref=dst, send_sem=send_sem, recv_sem=recv_sem,
                                        device_id=dev, device_id_type=MESH)


def _place_shard(w, mine_arr, *, name, tm=256, deps=()):
    rows, cols = w.shape

    def body(m_ref, w_ref, *rest):
        rest[-1][...] = w_ref[...].astype(rest[-1].dtype)

    return pl.pallas_call(
        body, name=name,
        grid_spec=pltpu.PrefetchScalarGridSpec(
            num_scalar_prefetch=1, grid=(rows // tm,),
            in_specs=[pl.BlockSpec((tm, cols), lambda i, m: (i, 0))] + [_ANY] * len(deps),
            out_specs=pl.BlockSpec((None, tm, cols), lambda i, m: (m[0], i, 0))),
        out_shape=jax.ShapeDtypeStruct((N_CHIPS, rows, cols), _WIRE),
        compiler_params=_params(("arbitrary",)),
    )(mine_arr, w, *deps)


_HBM = pl.BlockSpec(memory_space=pltpu.HBM)
_SEM = pl.BlockSpec(memory_space=pltpu.SEMAPHORE)
_EFFECT = pltpu.SideEffectType.DATAFLOW_SIDE_EFFECTING


def _copies_start(name, bufs, plan, count, after=()):
    nb, na = len(bufs), len(after)

    def body(*refs):
        send_sems, recv_sems, token = refs[nb + na], refs[nb + na + 1], refs[-1]
        copies = plan(refs[:nb])
        assert len(copies) == count
        for i, (src, dst, dev, _) in enumerate(copies):
            _remote(src, dst, send_sems.at[i], recv_sems.at[i], dev).start()
        token[...] = jnp.zeros_like(token)

    res = pl.pallas_call(
        body, name=name, in_specs=[_HBM] * nb + [_ANY] * na,
        out_specs=(_SEM, _SEM, *[_HBM] * nb, pl.BlockSpec(memory_space=pltpu.VMEM)),
        out_shape=(pltpu.SemaphoreType.DMA((count,)), pltpu.SemaphoreType.DMA((count,)),
                   *[pltpu.HBM(b.shape, b.dtype) for b in bufs], jax.ShapeDtypeStruct((SUBLANES, LANES), F32)),
        input_output_aliases={i: 2 + i for i in range(nb)},
        compiler_params=pltpu.CompilerParams(has_side_effects=_EFFECT),
    )(*[pltpu.with_memory_space_constraint(b, pltpu.HBM) for b in bufs], *after)
    return (res[0], res[1]), list(res[2:2 + nb]), res[-1]


def _copies_wait(name, bufs, sems, plan, after=(), which=None):
    nb, na = len(bufs), len(after)

    def body(*refs):
        send_sems, recv_sems = refs[nb], refs[nb + 1]
        for i, (src, _, dev, land) in enumerate(plan(refs[:nb])):
            if which is not None and i not in which:
                continue
            cp = _remote(src, land, send_sems.at[i], recv_sems.at[i], dev)
            cp.wait_send()
            cp.wait_recv()

    res = pl.pallas_call(
        body, name=name, in_specs=[_HBM] * nb + [_SEM, _SEM] + [_ANY] * na, out_specs=[_HBM] * nb,
        out_shape=[pltpu.HBM(b.shape, b.dtype) for b in bufs],
        input_output_aliases={i: i for i in range(nb)},
        compiler_params=pltpu.CompilerParams(has_side_effects=_EFFECT),
    )(*bufs, *sems, *after)
    return list(res)


def _plan_gather_ici(fulls, which=(0, 1, 2)):
    x, y, c, chips = _place()
    copies = []
    for f in fulls:
        half = pl.ds(c * (f.shape[1] // 2), f.shape[1] // 2)
        own = f.at[2 * x + y, half]
        for chip in [chips[k] for k in which]:
            copies.append((own, own, (*chip, c), f.at[2 * chip[0] + chip[1], half]))
    return copies


def _plan_gather_d2d(fulls, which=(0, 1, 2)):
    x, y, c, chips = _place()
    copies = []
    for f in fulls:
        r2 = f.shape[1] // 2
        for chip in [chips[k] for k in which]:
            blk = 2 * chip[0] + chip[1]
            landed = f.at[blk, pl.ds(c * r2, r2)]
            copies.append((landed, landed, (x, y, 1 - c), f.at[blk, pl.ds((1 - c) * r2, r2)]))
    return copies


def _plan_relay_direct(fulls):
    (f,) = fulls
    x, y, c, chips = _place()
    half = pl.ds(c * (f.shape[1] // 2), f.shape[1] // 2)
    own = f.at[2 * x + y, half]
    return [(own, own, (*chip, c), f.at[2 * chip[0] + chip[1], half]) for chip in chips[:2]]


def _plan_relay_forward(fulls, k):
    (f,) = fulls
    x, y, c, chips = _place()
    r2 = f.shape[1] // 2
    half, other = pl.ds(c * r2, r2), pl.ds((1 - c) * r2, r2)
    quarter = pl.ds(c * r2 + k * (r2 // 2), r2 // 2)
    blk, far = 2 * chips[k][0] + chips[k][1], 2 * chips[2][0] + chips[2][1]
    passed, landed = f.at[blk, quarter], f.at[blk, half]
    return [(passed, passed, (*chips[1 - k], c), f.at[far, quarter]), (landed, landed, (x, y, 1 - c), f.at[blk, other])]


def _plan_relay_last(fulls):
    (f,) = fulls
    x, y, c, chips = _place()
    r2 = f.shape[1] // 2
    far = 2 * chips[2][0] + chips[2][1]
    landed = f.at[far, pl.ds(c * r2, r2)]
    return [(landed, landed, (x, y, 1 - c), f.at[far, pl.ds((1 - c) * r2, r2)])]


def _plan_swap_halves(refs):
    x, y, c, _ = _place()
    n = len(refs) // 2
    copies = []
    for g, land in zip(refs[:n], refs[n:]):
        r2 = g.shape[1] // 2
        copies.append((g.at[:, pl.ds((1 - c) * r2, r2), :], land, (x, y, 1 - c), land))
    return copies


def _plan_scatter_chips(refs):
    x, y, c, chips = _place()
    n = len(refs) // 2
    copies = []
    for h, land in zip(refs[:n], refs[n:]):
        for k, chip in enumerate(chips):
            copies.append((h.at[2 * chip[0] + chip[1]], land.at[k], (*chip, c), land.at[k]))
    return copies


def _plan_join_halves(totals):
    x, y, c, _ = _place()
    copies = []
    for t in totals:
        r2 = t.shape[0] // 2
        mine = t.at[pl.ds(c * r2, r2)]
        copies.append((mine, mine, (x, y, 1 - c), t.at[pl.ds((1 - c) * r2, r2)]))
    return copies


def _add_sibling_half(g, got, c_arr, *, name, tm):
    _, rows, cols = g.shape
    r2 = rows // 2
    nb = r2 // tm

    def body(c_ref, g_ref, r_ref, o_ref):
        o_ref[...] = (g_ref[...].astype(F32) + r_ref[...].astype(F32)).astype(o_ref.dtype)

    return pl.pallas_call(
        body, name=name,
        grid_spec=pltpu.PrefetchScalarGridSpec(
            num_scalar_prefetch=1, grid=(N_CHIPS, nb),
            in_specs=[pl.BlockSpec((None, tm, cols), lambda b, i, c: (b, c[0] * nb + i, 0)),
                      pl.BlockSpec((None, tm, cols), lambda b, i, c: (b, i, 0))],
            out_specs=pl.BlockSpec((None, tm, cols), lambda b, i, c: (b, i, 0))),
        out_shape=jax.ShapeDtypeStruct((N_CHIPS, r2, cols), _WIRE),
        compiler_params=_params(("arbitrary", "arbitrary")),
    )(c_arr, g, got)


def _add_chips(h, got, place_arr, *, name, tm):
    _, r2, cols = h.shape
    nb = r2 // tm

    def body(p_ref, h_ref, r_ref, o_ref):
        o_ref[...] = ((h_ref[...].astype(F32) + r_ref[0].astype(F32)) + r_ref[1].astype(F32)) + r_ref[2].astype(F32)

    return pl.pallas_call(
        body, name=name,
        grid_spec=pltpu.PrefetchScalarGridSpec(
            num_scalar_prefetch=1, grid=(nb,),
            in_specs=[pl.BlockSpec((None, tm, cols), lambda i, p: (p[0], i, 0)),
                      pl.BlockSpec((3, tm, cols), lambda i, p: (0, i, 0))],
            out_specs=pl.BlockSpec((tm, cols), lambda i, p: (p[1] * nb + i, 0))),
        out_shape=jax.ShapeDtypeStruct((2 * r2, cols), F32),
        compiler_params=_params(("arbitrary",)),
    )(place_arr, h, got)


class _ReduceScatter:
    def __init__(self, tag, names, grads):
        self.tag, self.names, self.n = tag, names, len(names)
        core = lax.axis_index("c").astype(jnp.int32)
        chip = (2 * lax.axis_index("x") + lax.axis_index("y")).astype(jnp.int32)
        self.c_arr, self.place_arr = core.reshape(1), jnp.stack([chip, core])
        self.bufs = list(grads)

    def _start(self, step, bufs, plan, count, after):
        self.plan = plan
        self.step = f"grad_{step}_{self.tag}"
        self.sems, self.bufs, token = _copies_start(self.step + "_start", bufs, plan, count, after)
        return [token]

    def _wait(self, after):
        self.bufs = _copies_wait(self.step + "_wait", self.bufs, self.sems, self.plan, after)
        return self.bufs

    def start_swap(self, after=()):
        lands = [lax.empty((N_CHIPS, g.shape[1] // 2, g.shape[2]), g.dtype) for g in self.bufs]
        return self._start("swap", self.bufs + lands, _plan_swap_halves, self.n, after)

    def start_scatter(self, after):
        bufs = self._wait(after)
        pair = [_add_sibling_half(g, r, self.c_arr, name=f"grad_add_sibling_{nm}", tm=min(256, g.shape[1] // 2))
                for nm, g, r in zip(self.names, bufs[:self.n], bufs[self.n:])]
        lands = [lax.empty((3,) + h.shape[1:], h.dtype) for h in pair]
        return self._start("scatter", pair + lands, _plan_scatter_chips, 3 * self.n, ())

    def start_join(self, after):
        bufs = self._wait(after)
        total = [_add_chips(h, r, self.place_arr, name=f"grad_add_chips_{nm}", tm=min(256, h.shape[1]))
                 for nm, h, r in zip(self.names, bufs[:self.n], bufs[self.n:])]
        return self._start("join", total, _plan_join_halves, self.n, ())

    def finish(self, after):
        return dict(zip(self.names, self._wait(after)))


def _all_gather_small(v):
    m_per, n = v.shape

    def body(x_ref, out_ref, send_sems, recv_sems, local_sem):
        x, y, c, chips = _place()
        me, sibling = (x, y, c), (x, y, 1 - c)

        def rows(px, py, pc):
            return out_ref.at[4 * px + 2 * py + pc]

        def copy(k, block, to, src=None):
            return _remote(rows(*block) if src is None else src, rows(*block), send_sems.at[k], recv_sems.at[k], to)

        mine = pltpu.make_async_copy(x_ref, rows(*me), local_sem)
        mine.start()
        first = [copy(0, me, sibling, src=x_ref)]
        first += [copy(1 + j, me, (*chip, c), src=x_ref) for j, chip in enumerate(chips)]
        for cp in first:
            cp.start()
        passed = [copy(4 + j, (*chip, c), sibling) for j, chip in enumerate(chips)]
        for j, chip in enumerate(chips):
            copy(1 + j, (*chip, c), me).wait_recv()
            passed[j].start()
        copy(0, sibling, me).wait_recv()
        for j, chip in enumerate(chips):
            copy(4 + j, (*chip, 1 - c), me).wait_recv()
        for cp in first + passed:
            cp.wait_send()
        mine.wait()

    return pl.pallas_call(
        body, name="gather_small_grads",
        out_shape=jax.ShapeDtypeStruct((8, m_per, n), v.dtype),
        in_specs=[pl.BlockSpec(memory_space=pltpu.VMEM)], out_specs=pl.BlockSpec(memory_space=pltpu.VMEM),
        scratch_shapes=[pltpu.SemaphoreType.DMA((7,)), pltpu.SemaphoreType.DMA((7,)), pltpu.SemaphoreType.DMA],
        compiler_params=pltpu.CompilerParams(vmem_limit_bytes=VMEM_LIMIT),
    )(v)


def _sum8(v, *, name):
    _, m, n = v.shape

    def body(v_ref, o_ref):
        acc = v_ref[0]
        for d in range(1, 8):
            acc = acc + v_ref[d]
        o_ref[...] = acc

    return pl.pallas_call(body, name=name, out_shape=jax.ShapeDtypeStruct((m, n), F32),
                          compiler_params=pltpu.CompilerParams(vmem_limit_bytes=VMEM_LIMIT))(v)


def _local_step(x, target, norm_w, q_norm_w, k_norm_w, sinks, a_re, a_im, log_dt, b_re, b_im, c_re, c_im, d_skip,
                b_glu, io):
    seq = x.shape[0]
    qw2 = jnp.tile(q_norm_w.reshape(1, HEAD_DIM), (1, HEADS_PER_TILE))
    kw2 = jnp.tile(k_norm_w.reshape(1, HEAD_DIM), (1, HEADS_PER_TILE))
    nw, bg = norm_w.reshape(1, D_MODEL), b_glu.reshape(1, D_MODEL)
    dsk = d_skip.reshape(1, SSM_W)

    h, rstd = _rms_fwd(x, nw, deps=io.begin())
    proj, w_in4 = io.projection(h)
    attn, lse, ya_in = _attn2_fwd(proj, qw2, kw2, sinks, deps=io.after_proj(proj))
    w_ap4 = io.weight("w_attn_proj", ya_in)
    w_glu4, w_sp4, w_out = io.weight("w_glu", ya_in), io.weight("w_ssm_proj", ya_in), io.weight("w_out", ya_in)
    y_a = _mm(ya_in, w_ap4, mode="nn", name="mm_attn_proj", tm=2048, tn=512, tk=ATTN_W, b_blocked=True,
              rows_outer=True, out_dtype=_MXU)

    flat_a = (a_re.reshape(1, N_STATES), a_im.reshape(1, N_STATES), jnp.repeat(log_dt, STATE).reshape(1, N_STATES))
    coef = _ssm_params_fwd(*flat_a)
    bre_blk, bim_blk = _block_diag_b(b_re).astype(_MXU), _block_diag_b(b_im).astype(_MXU)
    cre_blk, cim_blk = _block_diag_c(c_re).astype(_MXU), _block_diag_c(c_im).astype(_MXU)
    u_scan = _to_scan_order(proj[:, OFF_U * CW:OFF_U * CW + SSM_W])
    y_scan, yg, s_re, s_im, i_re, i_im = _ssm_fwd(u_scan, bre_blk, bim_blk, cre_blk, cim_blk, dsk, coef)
    glu, ys_in = _mm_glu_gate(yg, w_glu4, bg, proj)
    y_s = _mm(ys_in, w_sp4, mode="nn", name="mm_ssm_proj", tm=2048, tn=512, tk=SSM_W, b_blocked=True,
              rows_outer=True, out_dtype=_MXU)

    merged, dout, dout_b, sq = _mm_merge_out_loss(proj, y_a, y_s, w_out, x, target)
    loss = 0.5 * jnp.sum(sq) / D_MODEL

    d_ya, d_ys, d_proj = _mm_merge_bwd(dout_b, w_out, proj, y_a, y_s)
    g_w_out = _mm(merged, dout_b, mode="tn", name="mm_g_w_out", tm=1024, tn=D_MODEL, tk=1024, out_dtype=_WIRE)

    d_ya_in = _mm(d_ya, w_ap4, mode="nt", name="mm_d_attn_gate", tm=2048, tn=ATTN_W, tk=512, b_blocked=True)
    g_w_ap = _mm(ya_in, d_ya, mode="tn", name="mm_g_w_attn_proj", tm=ATTN_W, tn=D_MODEL, tk=2048, out_dtype=_WIRE,
                 out_blocked=True)

    g_w_sp = _mm(ys_in, d_ys, mode="tn", name="mm_g_w_ssm_proj", tm=SSM_W, tn=D_MODEL, tk=2048, out_dtype=_WIRE,
                 out_blocked=True)
    d_glu, d_proj, g_bglu = _mm_ssm_gate_bwd(d_ys, w_sp4, glu, bg, proj, d_proj)
    d_yg = _mm(d_glu, w_glu4, mode="nt", name="mm_d_gelu", tm=2048, tn=SSM_W, tk=512, b_blocked=True)
    g_w_glu = _mm(yg, d_glu, mode="tn", name="mm_g_w_glu", tm=SSM_W, tn=D_MODEL, tk=2048, out_dtype=_WIRE, out_blocked=True)
    dep = io.later_grads(dict(w_attn_proj=g_w_ap, w_glu=g_w_glu, w_ssm_proj=g_w_sp,
                              w_out=g_w_out.reshape(N_CHIPS, D_MODEL // N_CHIPS, D_MODEL)))

    d_proj, g_qw2, g_kw2, g_sk = _attn2_bwd(proj, qw2, kw2, sinks, lse, attn, d_ya_in, d_proj, deps=dep)
    dep = io.before_scan_backward([d_proj])
    (d_proj, g_bre, g_bim, g_cre, g_cim, g_dsk, g_abr, g_abi, g_cfr, g_cfi) = _ssm_bwd(
        _to_scan_order(d_yg), y_scan, u_scan, s_re, s_im, i_re, i_im, bre_blk, bim_blk, cre_blk, cim_blk, dsk, coef,
        d_proj, deps=dep)
    g_are, g_aim, g_ldt = _ssm_params_bwd(*flat_a, g_abr, g_abi, g_cfr, g_cfi)
    g_are, g_aim = g_are.reshape(N_GROUPS, STATE), g_aim.reshape(N_GROUPS, STATE)
    g_ldt = g_ldt.reshape(N_GROUPS, STATE).sum(axis=1)
    dep = io.before_input_projection_grad([d_proj]) + io.small_grads(dict(
        q_norm_w=g_qw2[0, :HEAD_DIM] + g_qw2[0, HEAD_DIM:], k_norm_w=g_kw2[0, :HEAD_DIM] + g_kw2[0, HEAD_DIM:],
        sinks=g_sk.reshape(N_Q_HEADS), A_re=g_are, A_im=g_aim, log_dt=g_ldt,
        B_re=_diag_of_b(g_bre), B_im=_diag_of_b(g_bim), C_re=_diag_of_c(g_cre), C_im=_diag_of_c(g_cim),
        D_skip=g_dsk.reshape(N_GROUPS, GROUP), b_glu=g_bglu.reshape(D_MODEL)))
    g_w_in = _mm(h, d_proj, mode="tn", name="mm_g_w_in", tm=1024, tn=IN_W // 4, tk=1024, out_dtype=_WIRE,
                 out_blocked=True, deps=dep)
    dep = io.input_projection_grad(g_w_in)
    d_h = _mm(d_proj, w_in4, mode="nt", name="mm_d_h", tm=1024, tn=D_MODEL, tk=IN_W // 4, b_blocked=True, deps=dep)
    grad_x, g_nw = _rms_bwd(d_h, x, rstd, nw, dout)
    return loss, grad_x, g_nw.reshape(D_MODEL)


_SMALL = ["norm_w", "q_norm_w", "k_norm_w", "sinks", "A_re", "A_im", "log_dt", "B_re", "B_im", "C_re", "C_im",
          "D_skip", "b_glu"]
_BIG = ["w_in", "w_attn_proj", "w_glu", "w_ssm_proj", "w_out"]
_LATER = _BIG[1:]
_RELATIONS = ("flip_x", "flip_y", "flip_xy")
_ORDER = ["norm_w", "w_in", "q_norm_w", "k_norm_w", "sinks", "w_attn_proj", "A_re", "A_im", "log_dt", "B_re", "B_im",
          "C_re", "C_im", "D_skip", "w_glu", "b_glu", "w_ssm_proj", "w_out"]
_PACK_W = 1024


def _packed_rows(size):
    unit = SUBLANES * _PACK_W
    return -(-size // unit) * SUBLANES


def _pack_small(d, names):
    parts = []
    for n in names:
        flat = d[n].reshape(-1).astype(F32)
        rows = _packed_rows(flat.shape[0])
        parts.append(jnp.pad(flat, (0, rows * _PACK_W - flat.shape[0])).reshape(rows, _PACK_W))
    return jnp.concatenate(parts, axis=0)


def _unpack_small(packed, like, names):
    out, pos = {}, 0
    for n in names:
        rows = _packed_rows(like[n].size)
        out[n] = packed[pos:pos + rows].reshape(-1)[:like[n].size].reshape(like[n].shape)
        pos += rows
    return out


def _place_block(v, index_arr, *, name):
    rows, cols = v.shape

    def body(i_ref, v_ref, o_ref):
        o_ref[...] = v_ref[...]

    return pl.pallas_call(
        body, name=name,
        grid_spec=pltpu.PrefetchScalarGridSpec(
            num_scalar_prefetch=1, grid=(1,),
            in_specs=[pl.BlockSpec((rows, cols), lambda i, d: (0, 0))],
            out_specs=pl.BlockSpec((None, rows, cols), lambda i, d: (d[0], 0, 0))),
        out_shape=jax.ShapeDtypeStruct((8, rows, cols), v.dtype),
        compiler_params=_params(("arbitrary",)),
    )(index_arr, v)


def _plan_all_to_all(refs):
    (land,) = refs
    x, y, c, _ = _place()
    own = land.at[4 * x + 2 * y + c]
    copies = []
    for fx, fy, fc in [(0, 0, 1), (0, 1, 0), (0, 1, 1), (1, 0, 0), (1, 0, 1), (1, 1, 0), (1, 1, 1)]:
        px, py, pc = (1 - x) if fx else x, (1 - y) if fy else y, (1 - c) if fc else c
        copies.append((own, own, (px, py, pc), land.at[4 * px + 2 * py + pc]))
    return copies


def _adamw_whole(w, g, m, v, *, name):
    def body(w_ref, g_ref, m_ref, v_ref, d_ref, nm_ref, nv_ref):
        d_ref[...], nm_ref[...], nv_ref[...] = _adamw_math(w_ref[...], g_ref[...], m_ref[...], v_ref[...])

    return pl.pallas_call(body, name=name, out_shape=[jax.ShapeDtypeStruct(w.shape, F32)] * 3)(w, g, m, v)


class _Exchanges:
    def __init__(self, w, m, v):
        self.w, self.m, self.v = w, m, v
        self.grads, self.delta, self.new_m, self.new_v = {}, {}, {}, {}

    def _adamw(self, names, deps):
        for n in names:
            self.delta[n], self.new_m[n], self.new_v[n] = _adamw(
                self.w[n], self.grads[n], self.m[n], self.v[n], name=f"adamw_{n}", tm=128, deps=deps)

    def begin(self):
        chip = (2 * lax.axis_index("x") + lax.axis_index("y")).astype(jnp.int32).reshape(1)
        w_in = _place_shard(self.w["w_in"], chip, name="place_w_in")
        self.w_in_sems, self.w_in_buf, token = _copies_start("gather_w_in_direct_start", [w_in], _plan_relay_direct, 2)
        self.later_full = [_place_shard(self.w[n], chip, name=f"place_{n}", deps=[token]) for n in _LATER]
        return self.later_full

    def projection(self, h):
        x, y = lax.axis_index("x"), lax.axis_index("y")
        blks = [jnp.asarray(b, jnp.int32).reshape(1)
                for b in (2 * x + y, 2 * (1 - x) + y, 2 * x + (1 - y), 2 * (1 - x) + (1 - y))]
        bufs = self.w_in_buf
        proj = _mm_chip_block(h, bufs[0], blks[0], None, name="mm_proj_own", out_dtype=_MXU)
        relay, token = [], proj
        for k, tag in enumerate(_RELATIONS[:2]):
            bufs = _copies_wait(f"gather_w_in_direct_{tag}_wait", bufs, self.w_in_sems, _plan_relay_direct, [token],
                                which=(k,))
            plan = functools.partial(_plan_relay_forward, k=k)
            sems, bufs, token = _copies_start(f"gather_w_in_relay_{tag}_start", bufs, plan, 2)
            relay.append((sems, plan))
        self.rest = _copies_start("gather_ici_rest_start", self.later_full, _plan_gather_ici, 3 * len(_LATER),
                                  after=[token])
        token = self.rest[2]
        for k, tag in enumerate(_RELATIONS[:2]):
            bufs = _copies_wait(f"gather_w_in_hand_{tag}_wait", bufs, relay[k][0], relay[k][1], [token], which=(1,))
            token = proj = _mm_chip_block(h, bufs[0], blks[1 + k], proj, name=f"mm_proj_{tag}", out_dtype=_MXU)
        for k, tag in enumerate(_RELATIONS[:2]):
            bufs = _copies_wait(f"gather_w_in_relay_{tag}_wait", bufs, relay[k][0], relay[k][1], [token], which=(0,))
        sems, bufs, token = _copies_start("gather_w_in_last_start", bufs, _plan_relay_last, 1)
        bufs = _copies_wait("gather_w_in_last_wait", bufs, sems, _plan_relay_last, [token])
        proj = _mm_chip_block(h, bufs[0], blks[3], proj, name="mm_proj_flip_xy", out_dtype=_MXU)
        return proj, bufs[0]

    def weight(self, name, after):
        if self.rest is not None:
            sems, bufs = self.rest
            later = dict(zip(_LATER, _copies_wait("gather_d2d_rest_wait", bufs, sems, _plan_gather_d2d, [after])))
            later["w_out"] = later["w_out"].reshape(D_MODEL, D_MODEL)
            self.later, self.rest = later, None
        return self.later[name]

    def after_proj(self, proj):
        sems, bufs, _ = self.rest
        bufs = _copies_wait("gather_ici_rest_wait", bufs, sems, _plan_gather_ici, [proj])
        sems, bufs, token = _copies_start("gather_d2d_rest_start", bufs, _plan_gather_d2d, 3 * len(_LATER))
        self.rest = (sems, bufs)
        return [token]

    def later_grads(self, grads):
        self.rs_later = _ReduceScatter("later", _LATER, [grads[n] for n in _LATER])
        return self.rs_later.start_swap()

    def before_scan_backward(self, after):
        return self.rs_later.start_scatter(after)

    def before_input_projection_grad(self, after):
        return self.rs_later.start_join(after)

    def input_projection_grad(self, g_w_in):
        self.grads.update(self.rs_later.finish([g_w_in]))
        self.rs_in = _ReduceScatter("w_in", ["w_in"], [g_w_in])
        self._adamw(_LATER, self.rs_in.start_swap())
        return self.rs_in.start_scatter([self.delta[n] for n in _LATER])

    def _adamw_small(self, names):
        for n in names:
            self.delta[n], self.new_m[n], self.new_v[n] = _adamw_whole(
                self.w[n], self.grads[n], self.m[n], self.v[n], name=f"adamw_{n}")

    def small_grads(self, grads):
        me = (4 * lax.axis_index("x") + 2 * lax.axis_index("y") + lax.axis_index("c")).astype(jnp.int32).reshape(1)
        land = _place_block(_pack_small(grads, _SMALL[1:]), me, name="place_small_grads")
        self.small = _copies_start("gather_small_start", [land], _plan_all_to_all, 7)
        return [self.small[2]]

    def finish(self, g_norm_w, loss, after):
        join = self.rs_in.start_join(after)
        sems, bufs, _ = self.small
        (land,) = _copies_wait("gather_small_wait", bufs, sems, _plan_all_to_all, join)
        self.grads.update(_unpack_small(_sum8(land, name="sum_small_grads"), self.w, _SMALL[1:]))
        self._adamw_small(_SMALL[1:])
        rows = _packed_rows(g_norm_w.size)
        late = jnp.concatenate([_pack_small(dict(norm_w=g_norm_w), _SMALL[:1]),
                                jnp.pad(loss.reshape(1, 1), ((0, SUBLANES - 1), (0, _PACK_W - 1)))], axis=0)
        late = _sum8(_all_gather_small(late), name="sum_norm_w_grad_and_loss")
        self.grads.update(_unpack_small(late[:rows], self.w, _SMALL[:1]))
        self._adamw_small(_SMALL[:1])
        self.grads.update(self.rs_in.finish([self.delta[_SMALL[0]]]))
        self._adamw(["w_in"], ())
        return late[rows, 0]


def kernel(x, norm_w, w_in, q_norm_w, k_norm_w, sinks, w_attn_proj, A_re, A_im, log_dt, B_re, B_im, C_re, C_im, D_skip, w_glu, b_glu, w_ssm_proj, w_out, loss_target, m_norm_w, m_w_in, m_q_norm_w, m_k_norm_w, m_sinks, m_w_attn_proj, m_A_re, m_A_im, m_log_dt, m_B_re, m_B_im, m_C_re, m_C_im, m_D_skip, m_w_glu, m_b_glu, m_w_ssm_proj, m_w_out, v_norm_w, v_w_in, v_q_norm_w, v_k_norm_w, v_sinks, v_w_attn_proj, v_A_re, v_A_im, v_log_dt, v_B_re, v_B_im, v_C_re, v_C_im, v_D_skip, v_w_glu, v_b_glu, v_w_ssm_proj, v_w_out):
    w = dict(norm_w=norm_w, w_in=w_in, q_norm_w=q_norm_w, k_norm_w=k_norm_w, sinks=sinks, w_attn_proj=w_attn_proj,
             A_re=A_re, A_im=A_im, log_dt=log_dt, B_re=B_re, B_im=B_im, C_re=C_re, C_im=C_im, D_skip=D_skip,
             w_glu=w_glu, b_glu=b_glu, w_ssm_proj=w_ssm_proj, w_out=w_out)
    m = dict(norm_w=m_norm_w, w_in=m_w_in, q_norm_w=m_q_norm_w, k_norm_w=m_k_norm_w, sinks=m_sinks,
             w_attn_proj=m_w_attn_proj, A_re=m_A_re, A_im=m_A_im, log_dt=m_log_dt, B_re=m_B_re, B_im=m_B_im,
             C_re=m_C_re, C_im=m_C_im, D_skip=m_D_skip, w_glu=m_w_glu, b_glu=m_b_glu, w_ssm_proj=m_w_ssm_proj,
             w_out=m_w_out)
    v = dict(norm_w=v_norm_w, w_in=v_w_in, q_norm_w=v_q_norm_w, k_norm_w=v_k_norm_w, sinks=v_sinks,
             w_attn_proj=v_w_attn_proj, A_re=v_A_re, A_im=v_A_im, log_dt=v_log_dt, B_re=v_B_re, B_im=v_B_im,
             C_re=v_C_re, C_im=v_C_im, D_skip=v_D_skip, w_glu=v_w_glu, b_glu=v_b_glu, w_ssm_proj=v_w_ssm_proj,
             w_out=v_w_out)

    io = _Exchanges(w, m, v)
    loss, grad_x, g_norm_w = _local_step(x[0], loss_target[0], norm_w, q_norm_w, k_norm_w, sinks, A_re, A_im, log_dt,
                                         B_re, B_im, C_re, C_im, D_skip, b_glu, io)
    loss = io.finish(g_norm_w, loss, [grad_x])
    grads, delta, new_m, new_v = io.grads, io.delta, io.new_m, io.new_v

    return (loss, grad_x[None], *[grads[n] for n in _ORDER], *[delta[n] for n in _ORDER],
            *[new_m[n] for n in _ORDER], *[new_v[n] for n in _ORDER])
```

```python
import functools
import math

import jax
import jax.numpy as jnp
from jax import lax
from jax.experimental import pallas as pl
from jax.experimental.pallas import tpu as pltpu

F32 = jnp.float32
_MXU = jnp.bfloat16
_WIRE = jnp.bfloat16

LANES = 128
SUBLANES = 8
VMEM_LIMIT = 56 * 1024 * 1024

D_MODEL = 2048
HEAD_DIM = 64
N_Q_HEADS = 16
N_KV_HEADS = 4
Q_PER_KV = 4
ATTN_W = 1024
KV_W = 256
WINDOW = 128
SSM_W = 1024
GROUP = 16
N_GROUPS = 64
STATE = 64
N_STATES = N_GROUPS * STATE
IN_W = 8704
NORM_EPS = 1e-6
N_CHIPS = 4
CW = 512
OFF_AGATE, OFF_U, OFF_Z, OFF_GA, OFF_GS = 3, 5, 7, 9, 13

SSM_T = 256
SSM_L = SSM_T // SUBLANES
SSM_JB = 8
SSM_SB = N_STATES // SSM_JB

ADAM_LR, ADAM_B1, ADAM_B2, ADAM_EPS, ADAM_WD, ADAM_STEP = 0.001, 0.9, 0.999, 1e-08, 0.01, 10

MESH = pl.DeviceIdType.MESH
_ANY = pl.BlockSpec(memory_space=pl.ANY)


def _params(sem=None):
    return pltpu.CompilerParams(dimension_semantics=sem, vmem_limit_bytes=VMEM_LIMIT)


def _mm(a, b, *, mode, name, tm, tn, tk, out_dtype=F32, b_blocked=False, out_blocked=False, rows_outer=False,
        deps=()):
    nd = len(deps)
    if mode == "tn":
        K, M = a.shape
    else:
        M, K = a.shape
    if mode == "nn":
        N = b.shape[0] * b.shape[2] if b_blocked else b.shape[1]
    elif mode == "nt":
        N = b.shape[1] if b_blocked else b.shape[0]
    else:
        N = b.shape[1]
    tm, tn, tk = min(tm, M), min(tn, N), min(tk, K)
    nj, ni, nk = N // tn, M // tm, K // tk
    assert nj * tn == N and ni * tm == M and nk * tk == K, (name, M, N, K)
    dims = {"nn": (((1,), (0,)), ((), ())), "nt": (((1,), (1,)), ((), ())), "tn": (((0,), (0,)), ((), ()))}[mode]

    if mode == "tn":
        a_spec = pl.BlockSpec((tk, tm), lambda j, i, k: (k, i))
    else:
        a_spec = pl.BlockSpec((tm, tk), lambda j, i, k: (i, k))
    if mode == "nn":
        if b_blocked:
            assert b.shape[0] == nj and b.shape[2] == tn
            b_spec = pl.BlockSpec((None, tk, tn), lambda j, i, k: (j, k, 0))
        else:
            b_spec = pl.BlockSpec((tk, tn), lambda j, i, k: (k, j))
    elif mode == "nt":
        if b_blocked:
            assert b.shape[0] == nk and b.shape[2] == tk
            b_spec = pl.BlockSpec((None, tn, tk), lambda j, i, k: (k, j, 0))
        else:
            b_spec = pl.BlockSpec((tn, tk), lambda j, i, k: (j, k))
    else:
        b_spec = pl.BlockSpec((tk, tn), lambda j, i, k: (k, j))
    whole_out = out_blocked and nj == 1
    if whole_out:
        assert ni == 1
        o_spec = pl.BlockSpec((N_CHIPS, tm, tn // N_CHIPS), lambda j, i, k: (0, 0, 0))
        o_shape = jax.ShapeDtypeStruct((N_CHIPS, M, tn // N_CHIPS), out_dtype)
    elif out_blocked:
        assert nj == N_CHIPS
        o_spec = pl.BlockSpec((None, tm, tn), lambda j, i, k: (j, i, 0))
        o_shape = jax.ShapeDtypeStruct((nj, M, tn), out_dtype)
    else:
        o_spec = pl.BlockSpec((tm, tn), lambda j, i, k: (i, j))
        o_shape = jax.ShapeDtypeStruct((M, N), out_dtype)
    use_acc = nk > 1 and (out_dtype != F32 or whole_out)

    def body(a_ref, b_ref, *rest):
        o_ref, scratch = rest[nd], rest[nd + 1:]

        def product():
            return lax.dot_general(a_ref[...].astype(_MXU), b_ref[...].astype(_MXU), dims, preferred_element_type=F32)

        def write(result):
            if whole_out:
                w = tn // N_CHIPS
                for c in range(N_CHIPS):
                    o_ref[c] = result[:, c * w:(c + 1) * w].astype(o_ref.dtype)
            else:
                o_ref[...] = result.astype(o_ref.dtype)

        if nk == 1:
            write(product())
            return
        k = pl.program_id(2)
        acc = scratch[0] if use_acc else o_ref

        @pl.when(k == 0)
        def _():
            acc[...] = jnp.zeros_like(acc)

        acc[...] += product()

        if use_acc:
            @pl.when(k == nk - 1)
            def _():
                write(acc[...])

    specs = [a_spec, b_spec, o_spec]
    grid = (nj, ni, nk)
    if rows_outer:
        specs = [pl.BlockSpec(s.block_shape, lambda i, j, k, f=s.index_map: f(j, i, k)) for s in specs]
        grid = (ni, nj, nk)
    return pl.pallas_call(
        body, name=name, grid=grid, in_specs=specs[:2] + [_ANY] * nd, out_specs=specs[2],
        out_shape=o_shape, scratch_shapes=[pltpu.VMEM((tm, tn), F32)] if use_acc else [],
        compiler_params=_params(("parallel", "parallel", "arbitrary")),
    )(a, b, *deps)


def _mm_chip_block(a, b4, blk, prev, *, name, tm=512, out_dtype=F32, deps=()):
    M, K = a.shape
    nchip, _, C = b4.shape
    tm = min(tm, M)
    extra = ([] if prev is None else [prev]) + list(deps)

    def body(blk_ref, a_ref, b_ref, *rest):
        rest[-1][...] = jnp.dot(a_ref[...].astype(_MXU), b_ref[...].astype(_MXU),
                                preferred_element_type=F32).astype(rest[-1].dtype)

    return pl.pallas_call(
        body, name=name,
        grid_spec=pltpu.PrefetchScalarGridSpec(
            num_scalar_prefetch=1, grid=(M // tm,),
            in_specs=[pl.BlockSpec((tm, K), lambda i, c: (i, 0)), pl.BlockSpec((None, K, C), lambda i, c: (c[0], 0, 0))]
            + [_ANY] * len(extra),
            out_specs=pl.BlockSpec((tm, C), lambda i, c: (i, c[0]))),
        out_shape=jax.ShapeDtypeStruct((M, nchip * C), out_dtype),
        input_output_aliases={} if prev is None else {3: 0},
        compiler_params=_params(("arbitrary",)),
    )(blk, a, b4, *extra)


def _mm_merge_out_loss(proj, y_a, y_s, w_out, x, target, *, tm=256):
    rows, d = x.shape
    ncol = d // CW

    def body(*refs):
        ga_refs, gs_refs = refs[:ncol], refs[ncol:2 * ncol]
        ya_ref, ys_ref, w_ref, x_ref, t_ref, m_ref, d_ref, db_ref, sq_ref = refs[2 * ncol:]
        for j in range(ncol):
            cols = slice(j * CW, (j + 1) * CW)
            m_ref[:, cols] = (_sigmoid(ga_refs[j][...].astype(F32)) * ya_ref[:, cols].astype(F32)
                              + _sigmoid(gs_refs[j][...].astype(F32)) * ys_ref[:, cols].astype(F32)).astype(m_ref.dtype)
        mo = jnp.dot(m_ref[...], w_ref[...].astype(_MXU), preferred_element_type=F32)
        err = (x_ref[...] + mo) - t_ref[...]
        dout = err * (1.0 / d)
        d_ref[...] = dout
        db_ref[...] = dout.astype(db_ref.dtype)
        part = _colsum(err * err)
        i = pl.program_id(0)

        @pl.when(i == 0)
        def _():
            sq_ref[...] = part

        @pl.when(i > 0)
        def _():
            sq_ref[...] += part

    tile = pl.BlockSpec((tm, d), lambda i: (i, 0))
    gate = [pl.BlockSpec((tm, CW), lambda i, c=off + j: (i, c)) for off in (OFF_GA, OFF_GS) for j in range(ncol)]
    return pl.pallas_call(
        body, name="mm_merge_out_loss", grid=(rows // tm,),
        in_specs=gate + [tile, tile, pl.BlockSpec((d, d), lambda i: (0, 0), pipeline_mode=pl.Buffered(1)), tile, tile],
        out_specs=[tile, tile, tile, pl.BlockSpec((1, d), lambda i: (0, 0))],
        out_shape=[jax.ShapeDtypeStruct((rows, d), _MXU), jax.ShapeDtypeStruct((rows, d), F32),
                   jax.ShapeDtypeStruct((rows, d), _MXU), jax.ShapeDtypeStruct((1, d), F32)],
        compiler_params=_params(("arbitrary",)),
    )(*([proj] * (2 * ncol)), y_a, y_s, w_out, x, target)


def _mm_merge_bwd(dout_b, w_out, proj, y_a, y_s, *, tm=256):
    rows, d = y_a.shape
    ncol = d // CW

    def body(do_ref, w_ref, *refs):
        ga_refs, gs_refs = refs[:ncol], refs[ncol:2 * ncol]
        ya_ref, ys_ref, dya_ref, dys_ref, dg_ref = refs[2 * ncol:]
        dm = lax.dot_general(do_ref[...].astype(_MXU), w_ref[...].astype(_MXU), _NT, preferred_element_type=F32)
        for j in range(ncol):
            cols = slice(j * CW, (j + 1) * CW)
            dmj = dm[:, cols]
            sa, ss = _sigmoid(ga_refs[j][...].astype(F32)), _sigmoid(gs_refs[j][...].astype(F32))
            dya_ref[:, cols] = (sa * dmj).astype(dya_ref.dtype)
            dys_ref[:, cols] = (ss * dmj).astype(dys_ref.dtype)
            dg_ref[:, cols] = (dmj * ya_ref[:, cols].astype(F32) * sa * (1.0 - sa)).astype(dg_ref.dtype)
            dg_ref[:, d + j * CW:d + (j + 1) * CW] = (dmj * ys_ref[:, cols].astype(F32) * ss
                                                      * (1.0 - ss)).astype(dg_ref.dtype)

    tile = pl.BlockSpec((tm, d), lambda i: (i, 0))
    gate = [pl.BlockSpec((tm, CW), lambda i, c=off + j: (i, c)) for off in (OFF_GA, OFF_GS) for j in range(ncol)]
    both = pl.BlockSpec((pl.Element(tm), pl.Element(2 * d)), lambda i: (i * tm, OFF_GA * CW))
    return pl.pallas_call(
        body, name="mm_merge_bwd", grid=(rows // tm,),
        in_specs=[tile, pl.BlockSpec((d, d), lambda i: (0, 0))] + gate + [tile, tile],
        out_specs=[tile, tile, both],
        out_shape=[jax.ShapeDtypeStruct((rows, d), _MXU)] * 2 + [jax.ShapeDtypeStruct((rows, IN_W), _MXU)],
        compiler_params=_params(("arbitrary",)),
    )(dout_b, w_out, *([proj] * (2 * ncol)), y_a, y_s)


def _mm_glu_gate(yg, w_glu4, b_glu, proj, *, tm=1024):
    rows, k = yg.shape
    nj, _, tn = w_glu4.shape
    w = nj * tn // 2
    tm = min(tm, rows)

    def body(a_ref, w_ref, ba_ref, bb_ref, z0_ref, z1_ref, glu_ref, ys_ref):
        j = pl.program_id(1)
        for c in range(nj):
            @pl.when(j == c)
            def _(c=c):
                glu_ref[:, c * tn:(c + 1) * tn] = jnp.dot(a_ref[...].astype(_MXU), w_ref[...].astype(_MXU),
                                                          preferred_element_type=F32).astype(glu_ref.dtype)

        @pl.when(j == nj - 1)
        def _():
            z = jnp.concatenate([z0_ref[...], z1_ref[...]], axis=1).astype(F32)
            ys_ref[...] = ((glu_ref[:, :w].astype(F32) + ba_ref[...]) * _sigmoid(glu_ref[:, w:].astype(F32) + bb_ref[...])
                           * (z * _sigmoid(z))).astype(ys_ref.dtype)

    bias = lambda c: pl.BlockSpec((1, w), lambda i, j: (0, c))
    zcol = lambda c: pl.BlockSpec((tm, CW), lambda i, j: (i, OFF_Z + c))
    return pl.pallas_call(
        body, name="mm_glu_gate", grid=(rows // tm, nj),
        in_specs=[pl.BlockSpec((tm, k), lambda i, j: (i, 0)), pl.BlockSpec((None, k, tn), lambda i, j: (j, 0, 0)),
                  bias(0), bias(1), zcol(0), zcol(1)],
        out_specs=[pl.BlockSpec((tm, nj * tn), lambda i, j: (i, 0)), pl.BlockSpec((tm, w), lambda i, j: (i, 0))],
        out_shape=[jax.ShapeDtypeStruct((rows, nj * tn), _MXU), jax.ShapeDtypeStruct((rows, w), _MXU)],
        compiler_params=_params(("arbitrary", "arbitrary")),
    )(yg, w_glu4, b_glu, b_glu, proj, proj)


def _mm_ssm_gate_bwd(d_ys, w_sp4, glu, b_glu, proj, d_proj, *, tm=1024):
    rows, w = glu.shape[0], glu.shape[1] // 2
    nk, tk = w_sp4.shape[0], w_sp4.shape[2]
    tm = min(tm, rows)

    def body(dy_ref, w_ref, ga_ref, gb_ref, ba_ref, bb_ref, z0_ref, z1_ref, buf_ref, dg_ref, dz_ref, db_ref, acc):
        i, k = pl.program_id(0), pl.program_id(1)

        @pl.when(k == 0)
        def _():
            acc[...] = jnp.zeros_like(acc)

        acc[...] += lax.dot_general(dy_ref[...].astype(_MXU), w_ref[...].astype(_MXU), _NT, preferred_element_type=F32)

        @pl.when(k == nk - 1)
        def _():
            dv = acc[...]
            a, sb = ga_ref[...].astype(F32) + ba_ref[...], _sigmoid(gb_ref[...].astype(F32) + bb_ref[...])
            f, df = _silu_and_grad(jnp.concatenate([z0_ref[...], z1_ref[...]], axis=1).astype(F32))
            dga = dv * sb * f
            dgb = dv * a * f * sb * (1.0 - sb)
            dg_ref[:, :w] = dga.astype(dg_ref.dtype)
            dg_ref[:, w:] = dgb.astype(dg_ref.dtype)
            dz_ref[...] = (dv * a * sb * df).astype(dz_ref.dtype)
            part = jnp.concatenate([_colsum(dga), _colsum(dgb)], axis=1)

            @pl.when(i == 0)
            def _():
                db_ref[...] = part

            @pl.when(i > 0)
            def _():
                db_ref[...] += part

    half = lambda c: pl.BlockSpec((tm, w), lambda i, k: (i, c))
    bias = lambda c: pl.BlockSpec((1, w), lambda i, k: (0, c))
    zcol = lambda c: pl.BlockSpec((tm, CW), lambda i, k: (i, OFF_Z + c))
    return pl.pallas_call(
        body, name="mm_ssm_gate_bwd", grid=(rows // tm, nk),
        in_specs=[pl.BlockSpec((tm, tk), lambda i, k: (i, k)), pl.BlockSpec((None, w, tk), lambda i, k: (k, 0, 0)),
                  half(0), half(1), bias(0), bias(1), zcol(0), zcol(1), _ANY],
        out_specs=[pl.BlockSpec((tm, 2 * w), lambda i, k: (i, 0)),
                   pl.BlockSpec((pl.Element(tm), pl.Element(w)), lambda i, k: (i * tm, OFF_Z * CW)),
                   pl.BlockSpec((1, 2 * w), lambda i, k: (0, 0))],
        out_shape=[jax.ShapeDtypeStruct((rows, 2 * w), _MXU), jax.ShapeDtypeStruct(d_proj.shape, d_proj.dtype),
                   jax.ShapeDtypeStruct((1, 2 * w), F32)],
        input_output_aliases={8: 1},
        scratch_shapes=[pltpu.VMEM((tm, w), F32)],
        compiler_params=_params(("arbitrary", "arbitrary")),
    )(d_ys, w_sp4, glu, glu, b_glu, b_glu, proj, proj, d_proj)


def _ew(fn, ins, outs, *, rows, ncol, name, n_acc=0, tm=512, deps=(), into=None):
    deps = list(deps) + ([into[1]] if into else [])
    n_in, n_out, nd = len(ins), len(outs), len(deps)
    tm = min(tm, rows)
    in_specs = []
    for _, kind, col0 in ins:
        if kind == "mat":
            in_specs.append(pl.BlockSpec((tm, CW), lambda j, i, c0=col0: (i, c0 + j)))
        else:
            in_specs.append(pl.BlockSpec((1, CW), lambda j, i, c0=col0: (0, c0 + j)))
    out_specs = [pl.BlockSpec((tm, CW), lambda j, i: (i, j)) for _ in outs]
    out_shape = [jax.ShapeDtypeStruct((rows, w), dt) for w, dt in outs]
    if into:
        out_specs[into[0]] = pl.BlockSpec((tm, CW), lambda j, i, c0=into[2]: (i, c0 + j))
        out_shape[into[0]] = jax.ShapeDtypeStruct(into[1].shape, into[1].dtype)
    for _ in range(n_acc):
        out_specs.append(pl.BlockSpec((1, CW), lambda j, i: (0, j)))
        out_shape.append(jax.ShapeDtypeStruct((1, ncol * CW), F32))

    def body(*refs):
        vals = fn(*[r[...] for r in refs[:n_in]])
        refs = refs[n_in + nd:]
        for r, v in zip(refs[:n_out], vals[:n_out]):
            r[...] = v.astype(r.dtype)
        i = pl.program_id(1)
        for r, v in zip(refs[n_out:], vals[n_out:]):
            @pl.when(i == 0)
            def _(r=r, v=v):
                r[...] = v

            @pl.when(i > 0)
            def _(r=r, v=v):
                r[...] += v

    res = pl.pallas_call(
        body, name=name, grid=(ncol, rows // tm), in_specs=in_specs + [_ANY] * nd, out_specs=out_specs,
        out_shape=out_shape, input_output_aliases={n_in + nd - 1: into[0]} if into else {},
        compiler_params=_params(("parallel", "arbitrary")),
    )(*[a for a, _, _ in ins], *deps)
    return res


def _colsum(v):
    return jnp.sum(v, axis=0, keepdims=True)


def _sigmoid(v):
    return jax.nn.sigmoid(v)


def _silu_and_grad(v):
    s = _sigmoid(v)
    return v * s, s * (1.0 + v * (1.0 - s))


def _rms_fwd(x, w, *, tm=512, deps=()):
    rows, d = x.shape
    nd = len(deps)

    def body(x_ref, w_ref, *rest):
        h_ref, r_ref = rest[nd:]
        xv = x_ref[...]
        r = lax.rsqrt(jnp.mean(xv * xv, axis=-1, keepdims=True) + NORM_EPS)
        h_ref[...] = (xv * r * w_ref[...]).astype(h_ref.dtype)
        r_ref[...] = r

    return pl.pallas_call(
        body, name="rms_fwd", grid=(rows // tm,),
        in_specs=[pl.BlockSpec((tm, d), lambda i: (i, 0)), pl.BlockSpec((1, d), lambda i: (0, 0))] + [_ANY] * nd,
        out_specs=[pl.BlockSpec((tm, d), lambda i: (i, 0)), pl.BlockSpec((tm, 1), lambda i: (i, 0))],
        out_shape=[jax.ShapeDtypeStruct((rows, d), _MXU), jax.ShapeDtypeStruct((rows, 1), F32)],
        compiler_params=_params(("arbitrary",)),
    )(x, w, *deps)


def _rms_bwd(dh, x, rstd, w, dout, *, tm=256):
    rows, d = x.shape

    def body(dh_ref, x_ref, r_ref, w_ref, do_ref, gx_ref, gw_ref):
        dhv, xv, r, wv = dh_ref[...], x_ref[...], r_ref[...], w_ref[...]
        xr = xv * r
        t = jnp.mean(dhv * wv * xr, axis=-1, keepdims=True)
        gx_ref[...] = do_ref[...] + r * (wv * dhv - xr * t)
        part = _colsum(dhv * xr)
        i = pl.program_id(0)

        @pl.when(i == 0)
        def _():
            gw_ref[...] = part

        @pl.when(i > 0)
        def _():
            gw_ref[...] += part

    return pl.pallas_call(
        body, name="rms_bwd", grid=(rows // tm,),
        in_specs=[pl.BlockSpec((tm, d), lambda i: (i, 0)), pl.BlockSpec((tm, d), lambda i: (i, 0)),
                  pl.BlockSpec((tm, 1), lambda i: (i, 0)), pl.BlockSpec((1, d), lambda i: (0, 0)),
                  pl.BlockSpec((tm, d), lambda i: (i, 0))],
        out_specs=[pl.BlockSpec((tm, d), lambda i: (i, 0)), pl.BlockSpec((1, d), lambda i: (0, 0))],
        out_shape=[jax.ShapeDtypeStruct((rows, d), F32), jax.ShapeDtypeStruct((1, d), F32)],
        compiler_params=_params(("arbitrary",)),
    )(dh, x, rstd, w, dout)


_NT = (((1,), (1,)), ((), ()))
_TN = (((0,), (0,)), ((), ()))


QKV_W = ATTN_W + 2 * KV_W
HEADS_PER_TILE = LANES // HEAD_DIM


def _low_half(rows):
    return lax.broadcasted_iota(jnp.int32, (rows, LANES), 1) < HEAD_DIM


def _pair_mean(t, low):
    m_lo = jnp.sum(jnp.where(low, t, 0.0), axis=-1, keepdims=True)
    m_hi = jnp.sum(jnp.where(low, 0.0, t), axis=-1, keepdims=True)
    return jnp.where(low, m_lo, m_hi) * (1.0 / HEAD_DIM)


def _pair_rstd(t, low):
    return lax.rsqrt(_pair_mean(t * t, low) + NORM_EPS)


def _dup_half(t, hi, low):
    swapped = pltpu.roll(t, HEAD_DIM, 1)
    return jnp.where(low, swapped, t) if hi else jnp.where(low, t, swapped)


def _fold_halves(t):
    return t + pltpu.roll(t, HEAD_DIM, 1)


def _split_heads(t, low):
    return [jnp.where(low, t, 0.0), jnp.where(low, 0.0, t)]


def _stacked_band_mask(n):
    rows = Q_PER_KV * WINDOW
    qi = lax.broadcasted_iota(jnp.int32, (rows, 2 * WINDOW), 0) % WINDOW + WINDOW
    kj = lax.broadcasted_iota(jnp.int32, (rows, 2 * WINDOW), 1)
    diff = qi - kj
    first_key = jnp.where(n > 0, 0, WINDOW)
    return (diff >= 0) & (diff < WINDOW) & (kj >= first_key)


def _stacked_sinks(sink_ref, g):
    blk = lax.broadcasted_iota(jnp.int32, (Q_PER_KV * WINDOW, 1), 0) // WINDOW
    col = jnp.full((Q_PER_KV * WINDOW, 1), sink_ref[Q_PER_KV * g], F32)
    for r in range(1, Q_PER_KV):
        col = jnp.where(blk == r, sink_ref[Q_PER_KV * g + r], col)
    return col


def _attn_in_specs(nblk, rev):
    def cur(n):
        return (nblk - 1 - n) if rev else n

    q_spec = pl.BlockSpec((WINDOW, ATTN_W), lambda n: (cur(n), 0))
    kvc_spec = pl.BlockSpec((WINDOW, 2 * KV_W), lambda n: (cur(n), ATTN_W // (2 * KV_W)))
    kvp_spec = pl.BlockSpec((WINDOW, 2 * KV_W), lambda n: (jnp.maximum(cur(n) - 1, 0), ATTN_W // (2 * KV_W)))
    w_spec = pl.BlockSpec((1, LANES), lambda n: (0, 0))
    l_spec = pl.BlockSpec((WINDOW, N_Q_HEADS), lambda n: (cur(n), 0))
    gate_specs = [pl.BlockSpec((WINDOW, CW), lambda n, col=OFF_AGATE + j: (cur(n), col)) for j in range(ATTN_W // CW)]
    return q_spec, kvc_spec, kvp_spec, w_spec, l_spec, gate_specs


def _attn2_fwd(proj, qw2, kw2, sinks, deps=()):
    seq = proj.shape[0]
    nblk = seq // WINDOW
    scale = 1.0 / math.sqrt(HEAD_DIM)
    q_spec, kvc_spec, kvp_spec, w_spec, l_spec, gate_specs = _attn_in_specs(nblk, False)
    nd, ng = len(deps), len(gate_specs)

    def body(sink_ref, q_ref, kvc_ref, kvp_ref, qw_ref, kw_ref, *rest):
        gate_refs = rest[:ng]
        o_ref, lse_ref, ya_ref = rest[ng + nd:]
        n = pl.program_id(0)
        low, low2 = _low_half(WINDOW), _low_half(2 * WINDOW)
        valid = _stacked_band_mask(n)
        head_lane = lax.broadcasted_iota(jnp.int32, (WINDOW, N_Q_HEADS), 1)
        kv = jnp.concatenate([kvp_ref[...], kvc_ref[...]], axis=0).astype(F32)
        qwv, kwv = qw_ref[...], kw_ref[...]
        lse_blk = jnp.zeros((WINDOW, N_Q_HEADS), F32)
        for t in range(N_KV_HEADS // HEADS_PER_TILE):
            kt = kv[:, t * LANES:(t + 1) * LANES]
            vt = kv[:, KV_W + t * LANES:KV_W + (t + 1) * LANES]
            kn = kt * _pair_rstd(kt, low2) * kwv
            for hi in range(HEADS_PER_TILE):
                g = HEADS_PER_TILE * t + hi
                kdup = _dup_half(kn, hi, low2).astype(_MXU)
                vdup = _dup_half(vt, hi, low2).astype(_MXU)
                stack = []
                for tq in (2 * g, 2 * g + 1):
                    qt = q_ref[:, tq * LANES:(tq + 1) * LANES].astype(F32)
                    stack += _split_heads(qt * _pair_rstd(qt, low) * qwv, low)
                qs = jnp.concatenate(stack, axis=0).astype(_MXU)
                s = lax.dot_general(qs, kdup, _NT, preferred_element_type=F32) * scale
                s = jnp.where(valid, s, -1e30)
                sink = _stacked_sinks(sink_ref, g)
                m = jnp.maximum(jnp.max(s, axis=-1, keepdims=True), sink)
                e = jnp.exp(s - m)
                z = jnp.sum(e, axis=-1, keepdims=True) + jnp.exp(sink - m)
                o = jnp.dot((e / z).astype(_MXU), vdup, preferred_element_type=F32)
                for i, tq in enumerate((2 * g, 2 * g + 1)):
                    tile = slice(tq * LANES, (tq + 1) * LANES)
                    out = jnp.where(low, o[2 * i * WINDOW:(2 * i + 1) * WINDOW],
                                    o[(2 * i + 1) * WINDOW:(2 * i + 2) * WINDOW])
                    gate = gate_refs[tq * LANES // CW][:, tq * LANES % CW:tq * LANES % CW + LANES].astype(F32)
                    o_ref[:, tile] = out.astype(o_ref.dtype)
                    ya_ref[:, tile] = (out * (gate * _sigmoid(gate))).astype(ya_ref.dtype)
                lse = m + jnp.log(z)
                for r in range(Q_PER_KV):
                    lse_blk = jnp.where(head_lane == Q_PER_KV * g + r, lse[r * WINDOW:(r + 1) * WINDOW], lse_blk)
        lse_ref[...] = lse_blk

    return pl.pallas_call(
        body, name="attn_fwd", grid=(nblk,),
        in_specs=[pl.BlockSpec(memory_space=pltpu.SMEM), q_spec, kvc_spec, kvp_spec, w_spec, w_spec] + gate_specs
        + [_ANY] * nd,
        out_specs=[q_spec, l_spec, q_spec],
        out_shape=[jax.ShapeDtypeStruct((seq, ATTN_W), _MXU), jax.ShapeDtypeStruct((seq, N_Q_HEADS), F32),
                   jax.ShapeDtypeStruct((seq, ATTN_W), _MXU)],
        compiler_params=_params(("arbitrary",)),
    )(sinks, proj, proj, proj, qw2, kw2, *([proj] * ng), *deps)


def _attn2_bwd(proj, qw2, kw2, sinks, lse, attn, dya, d_proj, deps=()):
    seq = proj.shape[0]
    nblk = seq // WINDOW
    scale = 1.0 / math.sqrt(HEAD_DIM)
    q_spec, kvc_spec, kvp_spec, w_spec, l_spec, gate_specs = _attn_in_specs(nblk, True)
    s_spec = pl.BlockSpec((1, N_Q_HEADS), lambda n: (0, 0))
    d_spec = pl.BlockSpec((WINDOW, QKV_W + ATTN_W), lambda n: (nblk - 1 - n, 0))
    deps = list(deps) + [d_proj]
    nd, ng = len(deps), len(gate_specs)

    def body(sink_ref, q_ref, kvc_ref, kvp_ref, qw_ref, kw_ref, lse_ref, attn_ref, dya_ref, *rest):
        gate_refs = rest[:ng]
        d_ref, dqw_ref, dkw_ref, dsk_ref, carry, do_ref = rest[ng + nd:]
        step = pl.program_id(0)
        n = nblk - 1 - step

        @pl.when(step == 0)
        def _():
            carry[...] = jnp.zeros_like(carry)
            dqw_ref[...] = jnp.zeros_like(dqw_ref)
            dkw_ref[...] = jnp.zeros_like(dkw_ref)
            dsk_ref[...] = jnp.zeros_like(dsk_ref)

        for j, g_ref in enumerate(gate_refs):
            cols = slice(j * CW, (j + 1) * CW)
            f, df = _silu_and_grad(g_ref[...].astype(F32))
            dv = dya_ref[:, cols]
            do_ref[:, cols] = dv * f
            d_ref[:, QKV_W + j * CW:QKV_W + (j + 1) * CW] = (dv * attn_ref[:, cols].astype(F32) * df).astype(d_ref.dtype)

        low, low2 = _low_half(WINDOW), _low_half(2 * WINDOW)
        valid = _stacked_band_mask(n)
        head_lane = lax.broadcasted_iota(jnp.int32, (WINDOW, N_Q_HEADS), 1)
        sink_lane = lax.broadcasted_iota(jnp.int32, (1, N_Q_HEADS), 1)
        kv = jnp.concatenate([kvp_ref[...], kvc_ref[...]], axis=0).astype(F32)
        qwv, kwv = qw_ref[...], kw_ref[...]
        lse_blk = lse_ref[...]
        dqw = jnp.zeros((1, LANES), F32)
        dkw = jnp.zeros((1, LANES), F32)
        dsk = jnp.zeros((1, N_Q_HEADS), F32)
        for t in range(N_KV_HEADS // HEADS_PER_TILE):
            kt = kv[:, t * LANES:(t + 1) * LANES]
            vt = kv[:, KV_W + t * LANES:KV_W + (t + 1) * LANES]
            rk = _pair_rstd(kt, low2)
            kn = kt * rk * kwv
            dkn_t = jnp.zeros((2 * WINDOW, LANES), F32)
            dv_t = jnp.zeros((2 * WINDOW, LANES), F32)
            for hi in range(HEADS_PER_TILE):
                g = HEADS_PER_TILE * t + hi
                kdup = _dup_half(kn, hi, low2).astype(_MXU)
                vdup = _dup_half(vt, hi, low2).astype(_MXU)
                tiles = (2 * g, 2 * g + 1)
                qx, rq, stack, dstack, lse_rows = [], [], [], [], []
                for tq in tiles:
                    qt = q_ref[:, tq * LANES:(tq + 1) * LANES].astype(F32)
                    r = _pair_rstd(qt, low)
                    rq.append(r)
                    qx.append(qt * r)
                    stack += _split_heads(qx[-1] * qwv, low)
                    dstack += _split_heads(do_ref[:, tq * LANES:(tq + 1) * LANES], low)
                for r in range(Q_PER_KV):
                    lse_rows.append(jnp.sum(jnp.where(head_lane == Q_PER_KV * g + r, lse_blk, 0.0), axis=-1, keepdims=True))
                qs = jnp.concatenate(stack, axis=0).astype(_MXU)
                dos = jnp.concatenate(dstack, axis=0).astype(_MXU)
                lse_col = jnp.concatenate(lse_rows, axis=0)
                s = lax.dot_general(qs, kdup, _NT, preferred_element_type=F32) * scale
                s = jnp.where(valid, s, -1e30)
                p = jnp.exp(s - lse_col)
                dp = lax.dot_general(dos, vdup, _NT, preferred_element_type=F32)
                dsum = jnp.sum(p * dp, axis=-1, keepdims=True)
                ds = (p * (dp - dsum) * scale).astype(_MXU)
                dsink = -jnp.exp(_stacked_sinks(sink_ref, g) - lse_col) * dsum
                for r in range(Q_PER_KV):
                    dsk = dsk + jnp.where(sink_lane == Q_PER_KV * g + r, _colsum(dsink[r * WINDOW:(r + 1) * WINDOW]), 0.0)
                dv_g = _fold_halves(lax.dot_general(p.astype(_MXU), dos, _TN, preferred_element_type=F32))
                dkn_g = _fold_halves(lax.dot_general(ds, qs, _TN, preferred_element_type=F32))
                dv_t = jnp.where(low2, dv_t, dv_g) if hi else jnp.where(low2, dv_g, dv_t)
                dkn_t = jnp.where(low2, dkn_t, dkn_g) if hi else jnp.where(low2, dkn_g, dkn_t)
                dqn = jnp.dot(ds, kdup, preferred_element_type=F32)
                for i, tq in enumerate(tiles):
                    dqn_t = jnp.where(low, dqn[2 * i * WINDOW:(2 * i + 1) * WINDOW],
                                      dqn[(2 * i + 1) * WINDOW:(2 * i + 2) * WINDOW])
                    dq = rq[i] * (qwv * dqn_t - qx[i] * _pair_mean(dqn_t * qwv * qx[i], low))
                    d_ref[:, tq * LANES:(tq + 1) * LANES] = dq.astype(d_ref.dtype)
                    dqw = dqw + _colsum(dqn_t * qx[i])
            k_cols = slice(t * LANES, (t + 1) * LANES)
            v_cols = slice(KV_W + t * LANES, KV_W + (t + 1) * LANES)
            dkn_c = dkn_t[WINDOW:] + carry[:, k_cols]
            rc = rk[WINDOW:]
            kx = kt[WINDOW:] * rc
            dk = rc * (kwv * dkn_c - kx * _pair_mean(dkn_c * kwv * kx, low))
            d_ref[:, ATTN_W + t * LANES:ATTN_W + (t + 1) * LANES] = dk.astype(d_ref.dtype)
            d_ref[:, ATTN_W + KV_W + t * LANES:ATTN_W + KV_W + (t + 1) * LANES] = (
                dv_t[WINDOW:] + carry[:, v_cols]).astype(d_ref.dtype)
            carry[:, k_cols] = dkn_t[:WINDOW]
            carry[:, v_cols] = dv_t[:WINDOW]
            dkw = dkw + _colsum(dkn_c * kx)
        dqw_ref[...] += dqw
        dkw_ref[...] += dkw
        dsk_ref[...] += dsk

    return pl.pallas_call(
        body, name="attn_bwd", grid=(nblk,),
        in_specs=[pl.BlockSpec(memory_space=pltpu.SMEM), q_spec, kvc_spec, kvp_spec, w_spec, w_spec, l_spec, q_spec,
                  q_spec] + gate_specs + [_ANY] * nd,
        out_specs=[d_spec, w_spec, w_spec, s_spec],
        out_shape=[jax.ShapeDtypeStruct(d_proj.shape, d_proj.dtype), jax.ShapeDtypeStruct((1, LANES), F32),
                   jax.ShapeDtypeStruct((1, LANES), F32), jax.ShapeDtypeStruct((1, N_Q_HEADS), F32)],
        input_output_aliases={9 + ng + nd - 1: 0},
        scratch_shapes=[pltpu.VMEM((WINDOW, 2 * KV_W), F32), pltpu.VMEM((WINDOW, ATTN_W), F32)],
        compiler_params=_params(("arbitrary",)),
    )(sinks, proj, proj, proj, qw2, kw2, lse, attn, dya, *([proj] * ng), *deps)


def _ssm_discretise(a_re, a_im, log_dt):
    dt = jnp.exp(log_dt)
    mag = jnp.exp(dt * a_re)
    ab_re = mag * jnp.cos(dt * a_im)
    ab_im = mag * jnp.sin(dt * a_im)
    num_re = ab_re - 1.0
    num_im = ab_im
    den = a_re * a_re + a_im * a_im
    cf_re = (num_re * a_re + num_im * a_im) / den
    cf_im = (num_im * a_re - num_re * a_im) / den
    return ab_re, ab_im, cf_re, cf_im


def _ssm_params_fwd(a_re, a_im, log_dt):
    shp = jax.ShapeDtypeStruct(a_re.shape, F32)

    def body(are_ref, aim_ref, ldt_ref, abr_ref, abi_ref, cfr_ref, cfi_ref, alr_ref, ali_ref):
        abr, abi, cfr, cfi = _ssm_discretise(are_ref[...], aim_ref[...], ldt_ref[...])
        abr_ref[...], abi_ref[...], cfr_ref[...], cfi_ref[...] = abr, abi, cfr, cfi
        pr, pi = abr, abi
        for _ in range(int(math.log2(SSM_L))):
            pr, pi = pr * pr - pi * pi, 2.0 * pr * pi
        alr_ref[...], ali_ref[...] = pr, pi

    return pl.pallas_call(body, name="ssm_params_fwd", out_shape=[shp] * 6)(a_re, a_im, log_dt)


def _ssm_params_bwd(a_re, a_im, log_dt, d_abr, d_abi, d_cfr, d_cfi):
    def body(are_ref, aim_ref, ldt_ref, g0, g1, g2, g3, dare_ref, daim_ref, dldt_ref):
        _, vjp = jax.vjp(_ssm_discretise, are_ref[...], aim_ref[...], ldt_ref[...])
        dare_ref[...], daim_ref[...], dldt_ref[...] = vjp((g0[...], g1[...], g2[...], g3[...]))

    return pl.pallas_call(
        body, name="ssm_params_bwd",
        out_shape=[jax.ShapeDtypeStruct(a_re.shape, F32), jax.ShapeDtypeStruct(a_im.shape, F32),
                   jax.ShapeDtypeStruct(log_dt.shape, F32)],
    )(a_re, a_im, log_dt, d_abr, d_abi, d_cfr, d_cfi)


def _scan_cols(j):
    return pl.ds(j * SSM_SB, SSM_SB)


def _rows8(r):
    return pl.ds(pl.multiple_of(r * SUBLANES, SUBLANES), SUBLANES)


def _bcast8(row):
    return jnp.broadcast_to(row, (SUBLANES, row.shape[-1]))


def _token_order_pick():
    tok = lax.broadcasted_iota(jnp.int32, (SSM_T, SSM_T), 0)
    row = lax.broadcasted_iota(jnp.int32, (SSM_T, SSM_T), 1)
    return (row == SUBLANES * (tok % SSM_L) + tok // SSM_L).astype(_MXU)


SCAN_UNROLL = 8


def _scan_loop(n, step, init):
    def trip(o, carry):
        for i in range(SCAN_UNROLL):
            carry = step(o * SCAN_UNROLL + i, carry)
        return carry

    return lax.fori_loop(0, n // SCAN_UNROLL, trip, init)


def _ssm_fwd(u, b_re, b_im, c_re, c_im, d_skip, coef):
    seq = u.shape[0]
    nc = seq // SSM_T
    T, L = SSM_T, SSM_L

    def body(u_ref, bre_ref, bim_ref, cre_ref, cim_ref, d_ref, are_ref, aim_ref, cfr_ref, cfi_ref, alr_ref, ali_ref,
             y_ref, yg_ref, sre_ref, sim_ref, ire_ref, iim_ref, car_re, car_im, end_re, end_im, yg_scan):
        c = pl.program_id(0)

        @pl.when(c == 0)
        def _():
            car_re[...] = jnp.zeros_like(car_re)
            car_im[...] = jnp.zeros_like(car_im)

        for j in range(SSM_JB):
            ub = u_ref[:, j * LANES:(j + 1) * LANES].astype(_MXU)
            bur = jnp.dot(ub, bre_ref[j], preferred_element_type=F32)
            bui = jnp.dot(ub, bim_ref[j], preferred_element_type=F32)
            cfr, cfi = cfr_ref[:, _scan_cols(j)], cfi_ref[:, _scan_cols(j)]
            sre_ref[:, _scan_cols(j)] = cfr * bur - cfi * bui
            sim_ref[:, _scan_cols(j)] = cfr * bui + cfi * bur

        for j in range(SSM_JB):
            cols = _scan_cols(j)
            ar, ai = _bcast8(are_ref[:, cols]), _bcast8(aim_ref[:, cols])

            def step1(r, s, cols=cols, ar=ar, ai=ai):
                sr, si = s
                rows = _rows8(r)
                return (ar * sr - ai * si + sre_ref[rows, cols], ar * si + ai * sr + sim_ref[rows, cols])

            zero = jnp.zeros((SUBLANES, SSM_SB), F32)
            er, ei = _scan_loop(L, step1, (zero, zero))
            end_re[:, cols] = er
            end_im[:, cols] = ei

        alr, ali = alr_ref[...], ali_ref[...]
        cr, ci = car_re[...], car_im[...]
        ire_ref[0:1, :] = cr
        iim_ref[0:1, :] = ci
        for i in range(1, SUBLANES):
            er, ei = end_re[i - 1:i, :], end_im[i - 1:i, :]
            cr, ci = alr * cr - ali * ci + er, alr * ci + ali * cr + ei
            ire_ref[i:i + 1, :] = cr
            iim_ref[i:i + 1, :] = ci

        for j in range(SSM_JB):
            cols = _scan_cols(j)
            ar, ai = _bcast8(are_ref[:, cols]), _bcast8(aim_ref[:, cols])

            def step2(r, s, cols=cols, ar=ar, ai=ai):
                sr, si = s
                rows = _rows8(r)
                nr = ar * sr - ai * si + sre_ref[rows, cols]
                ni = ar * si + ai * sr + sim_ref[rows, cols]
                sre_ref[rows, cols] = nr
                sim_ref[rows, cols] = ni
                return nr, ni

            _scan_loop(L, step2, (ire_ref[:, cols], iim_ref[:, cols]))

        car_re[...] = sre_ref[T - 1:T, :]
        car_im[...] = sim_ref[T - 1:T, :]

        for j in range(SSM_JB):
            cols = _scan_cols(j)
            ch = slice(j * LANES, (j + 1) * LANES)
            y = (jnp.dot(sre_ref[:, cols].astype(_MXU), cre_ref[j], preferred_element_type=F32)
                 - jnp.dot(sim_ref[:, cols].astype(_MXU), cim_ref[j], preferred_element_type=F32))
            y = y + d_ref[:, ch] * u_ref[:, ch].astype(F32)
            y_ref[:, ch] = y
            yg_scan[:, ch] = jax.nn.gelu(y).astype(yg_scan.dtype)
        yg_ref[...] = jnp.dot(_token_order_pick(), yg_scan[...], preferred_element_type=F32).astype(yg_ref.dtype)

    tok = pl.BlockSpec((T, SSM_W), lambda c: (c, 0))
    st = pl.BlockSpec((T, N_STATES), lambda c: (c, 0))
    ini = pl.BlockSpec((None, SUBLANES, N_STATES), lambda c: (c, 0, 0))
    bsp = pl.BlockSpec((SSM_JB, LANES, SSM_SB), lambda c: (0, 0, 0))
    csp = pl.BlockSpec((SSM_JB, SSM_SB, LANES), lambda c: (0, 0, 0))
    row_w = pl.BlockSpec((1, SSM_W), lambda c: (0, 0))
    row_s = pl.BlockSpec((1, N_STATES), lambda c: (0, 0))
    return pl.pallas_call(
        body, name="ssm_fwd", grid=(nc,),
        in_specs=[tok, bsp, bsp, csp, csp, row_w] + [row_s] * 6,
        out_specs=[tok, tok, st, st, ini, ini],
        out_shape=[jax.ShapeDtypeStruct((seq, SSM_W), F32), jax.ShapeDtypeStruct((seq, SSM_W), _MXU),
                   jax.ShapeDtypeStruct((seq, N_STATES), F32), jax.ShapeDtypeStruct((seq, N_STATES), F32),
                   jax.ShapeDtypeStruct((nc, SUBLANES, N_STATES), F32),
                   jax.ShapeDtypeStruct((nc, SUBLANES, N_STATES), F32)],
        scratch_shapes=[pltpu.VMEM((1, N_STATES), F32), pltpu.VMEM((1, N_STATES), F32),
                        pltpu.VMEM((SUBLANES, N_STATES), F32), pltpu.VMEM((SUBLANES, N_STATES), F32),
                        pltpu.VMEM((T, SSM_W), _MXU)],
        compiler_params=_params(("arbitrary",)),
    )(u, b_re, b_im, c_re, c_im, d_skip, *coef)


def _ssm_bwd(dyg, y, u, s_re, s_im, i_re, i_im, b_re, b_im, c_re, c_im, d_skip, coef, d_proj, deps=()):
    seq = u.shape[0]
    nc = seq // SSM_T
    T, L = SSM_T, SSM_L
    deps = list(deps) + [d_proj]

    def body(dyg_ref, y_ref, u_ref, sre_ref, sim_ref, ire_ref, iim_ref, bre_ref, bim_ref, cre_ref, cim_ref, d_ref,
             are_ref, aim_ref, cfr_ref, cfi_ref, alr_ref, ali_ref, *rest):
        (du_ref, dbre_out, dbim_out, dcre_out, dcim_out, dd_ref, dar_ref, dai_ref, dcfr_ref, dcfi_ref,
         lre, lim, car_re, car_im, end_re, end_im, ini_re, ini_im, dbre_ref, dbim_ref, dcre_ref, dcim_ref,
         dy_ref, du_scan) = rest[len(deps):]
        step = pl.program_id(0)
        dy_ref[...] = jax.vjp(jax.nn.gelu, y_ref[...])[1](dyg_ref[...])[0]

        @pl.when(step == 0)
        def _():
            car_re[...] = jnp.zeros_like(car_re)
            car_im[...] = jnp.zeros_like(car_im)
            for ref in (dbre_ref, dbim_ref, dcre_ref, dcim_ref, dd_ref, dar_ref, dai_ref, dcfr_ref, dcfi_ref):
                ref[...] = jnp.zeros_like(ref)

        for j in range(SSM_JB):
            dyb = dy_ref[:, j * LANES:(j + 1) * LANES].astype(_MXU)
            lre[:, _scan_cols(j)] = lax.dot_general(dyb, cre_ref[j], _NT, preferred_element_type=F32)
            lim[:, _scan_cols(j)] = -lax.dot_general(dyb, cim_ref[j], _NT, preferred_element_type=F32)

        for j in range(SSM_JB):
            cols = _scan_cols(j)
            ar, ai = _bcast8(are_ref[:, cols]), _bcast8(aim_ref[:, cols])

            def step1(t, s, cols=cols, ar=ar, ai=ai):
                sr, si = s
                rows = _rows8(L - 1 - t)
                return (ar * sr + ai * si + lre[rows, cols], ar * si - ai * sr + lim[rows, cols])

            zero = jnp.zeros((SUBLANES, SSM_SB), F32)
            er, ei = _scan_loop(L, step1, (zero, zero))
            end_re[:, cols] = er
            end_im[:, cols] = ei

        alr, ali = alr_ref[...], ali_ref[...]
        cr, ci = car_re[...], car_im[...]
        ini_re[SUBLANES - 1:SUBLANES, :] = cr
        ini_im[SUBLANES - 1:SUBLANES, :] = ci
        for i in range(SUBLANES - 2, -1, -1):
            er, ei = end_re[i + 1:i + 2, :], end_im[i + 1:i + 2, :]
            cr, ci = alr * cr + ali * ci + er, alr * ci - ali * cr + ei
            ini_re[i:i + 1, :] = cr
            ini_im[i:i + 1, :] = ci

        for j in range(SSM_JB):
            cols = _scan_cols(j)
            ar, ai = _bcast8(are_ref[:, cols]), _bcast8(aim_ref[:, cols])

            def step2(t, s, cols=cols, ar=ar, ai=ai):
                sr, si = s
                rows = _rows8(L - 1 - t)
                nr = ar * sr + ai * si + lre[rows, cols]
                ni = ar * si - ai * sr + lim[rows, cols]
                lre[rows, cols] = nr
                lim[rows, cols] = ni
                return nr, ni

            _scan_loop(L, step2, (ini_re[:, cols], ini_im[:, cols]))

        car_re[...] = lre[0:1, :]
        car_im[...] = lim[0:1, :]

        head, tail, body_rows = slice(0, SUBLANES), slice(SUBLANES, T), slice(0, T - SUBLANES)
        for j in range(SSM_JB):
            cols = _scan_cols(j)
            ch = slice(j * LANES, (j + 1) * LANES)
            lr, li = lre[:, cols], lim[:, cols]
            lt_r, lt_i, sp_r, sp_i = lre[tail, cols], lim[tail, cols], sre_ref[body_rows, cols], sim_ref[body_rows, cols]
            lh_r, lh_i, si_r, si_i = lre[head, cols], lim[head, cols], ire_ref[:, cols], iim_ref[:, cols]
            dar_ref[:, cols] += _colsum(lt_r * sp_r + lt_i * sp_i) + _colsum(lh_r * si_r + lh_i * si_i)
            dai_ref[:, cols] += _colsum(lt_i * sp_r - lt_r * sp_i) + _colsum(lh_i * si_r - lh_r * si_i)
            ub = u_ref[:, ch].astype(_MXU)
            uf = ub.astype(F32)
            bur = jnp.dot(ub, bre_ref[j], preferred_element_type=F32)
            bui = jnp.dot(ub, bim_ref[j], preferred_element_type=F32)
            dcfr_ref[:, cols] += _colsum(lr * bur + li * bui)
            dcfi_ref[:, cols] += _colsum(li * bur - lr * bui)
            cfr, cfi = cfr_ref[:, cols], cfi_ref[:, cols]
            dbur = (cfr * lr + cfi * li).astype(_MXU)
            dbui = (cfr * li - cfi * lr).astype(_MXU)
            dyf = dy_ref[:, ch]
            dyb = dyf.astype(_MXU)
            du = (lax.dot_general(dbur, bre_ref[j], _NT, preferred_element_type=F32)
                  + lax.dot_general(dbui, bim_ref[j], _NT, preferred_element_type=F32) + d_ref[:, ch] * dyf)
            du_scan[:, ch] = du.astype(du_scan.dtype)
            dbre_ref[j] += lax.dot_general(ub, dbur, _TN, preferred_element_type=F32)
            dbim_ref[j] += lax.dot_general(ub, dbui, _TN, preferred_element_type=F32)
            dcre_ref[j] += lax.dot_general(sre_ref[:, cols].astype(_MXU), dyb, _TN, preferred_element_type=F32)
            dcim_ref[j] -= lax.dot_general(sim_ref[:, cols].astype(_MXU), dyb, _TN, preferred_element_type=F32)
            dd_ref[:, ch] += _colsum(dyf * uf)
        du_ref[...] = jnp.dot(_token_order_pick(), du_scan[...], preferred_element_type=F32).astype(du_ref.dtype)

        @pl.when(step == nc - 1)
        def _():
            for acc, out in ((dbre_ref, dbre_out), (dbim_ref, dbim_out), (dcre_ref, dcre_out), (dcim_ref, dcim_out)):
                pltpu.sync_copy(acc, out)

    tok = pl.BlockSpec((T, SSM_W), lambda c: (nc - 1 - c, 0))
    st = pl.BlockSpec((T, N_STATES), lambda c: (nc - 1 - c, 0))
    ini = pl.BlockSpec((None, SUBLANES, N_STATES), lambda c: (nc - 1 - c, 0, 0))
    bsp = pl.BlockSpec((SSM_JB, LANES, SSM_SB), lambda c: (0, 0, 0))
    csp = pl.BlockSpec((SSM_JB, SSM_SB, LANES), lambda c: (0, 0, 0))
    row_w = pl.BlockSpec((1, SSM_W), lambda c: (0, 0))
    row_s = pl.BlockSpec((1, N_STATES), lambda c: (0, 0))
    big = pltpu.VMEM((T, N_STATES), F32)
    one = pltpu.VMEM((1, N_STATES), F32)
    eight = pltpu.VMEM((SUBLANES, N_STATES), F32)
    return pl.pallas_call(
        body, name="ssm_bwd", grid=(nc,),
        in_specs=[tok, tok, tok, st, st, ini, ini, bsp, bsp, csp, csp, row_w] + [row_s] * 6 + [_ANY] * len(deps),
        out_specs=[pl.BlockSpec((pl.Element(T), pl.Element(SSM_W)), lambda c: ((nc - 1 - c) * T, OFF_U * CW)),
                   _ANY, _ANY, _ANY, _ANY, row_w, row_s, row_s, row_s, row_s],
        input_output_aliases={18 + len(deps) - 1: 0},
        out_shape=[jax.ShapeDtypeStruct(d_proj.shape, d_proj.dtype),
                   jax.ShapeDtypeStruct((SSM_JB, LANES, SSM_SB), F32), jax.ShapeDtypeStruct((SSM_JB, LANES, SSM_SB), F32),
                   jax.ShapeDtypeStruct((SSM_JB, SSM_SB, LANES), F32), jax.ShapeDtypeStruct((SSM_JB, SSM_SB, LANES), F32),
                   jax.ShapeDtypeStruct((1, SSM_W), F32)] + [jax.ShapeDtypeStruct((1, N_STATES), F32)] * 4,
        scratch_shapes=[big, big, one, one, eight, eight, eight, eight,
                        pltpu.VMEM((SSM_JB, LANES, SSM_SB), F32), pltpu.VMEM((SSM_JB, LANES, SSM_SB), F32),
                        pltpu.VMEM((SSM_JB, SSM_SB, LANES), F32), pltpu.VMEM((SSM_JB, SSM_SB, LANES), F32),
                        pltpu.VMEM((T, SSM_W), F32), pltpu.VMEM((T, SSM_W), _MXU)],
        compiler_params=_params(("arbitrary",)),
    )(dyg, y, u, s_re, s_im, i_re, i_im, b_re, b_im, c_re, c_im, d_skip, *coef, *deps)


def _block_diag_b(b):
    t = b.reshape(SSM_JB, 8, STATE, GROUP).transpose(0, 1, 3, 2)
    eye = jnp.eye(8, dtype=b.dtype)
    return (t[:, :, :, None, :] * eye[None, :, None, :, None]).reshape(SSM_JB, LANES, SSM_SB)


def _block_diag_c(c):
    t = c.reshape(SSM_JB, 8, GROUP, STATE).transpose(0, 1, 3, 2)
    eye = jnp.eye(8, dtype=c.dtype)
    return (t[:, :, :, None, :] * eye[None, :, None, :, None]).reshape(SSM_JB, SSM_SB, LANES)


def _diag_of_b(blk):
    t = blk.reshape(SSM_JB, 8, GROUP, 8, STATE)
    d = jnp.sum(t * jnp.eye(8, dtype=blk.dtype)[None, :, None, :, None], axis=3)
    return d.transpose(0, 1, 3, 2).reshape(N_GROUPS, STATE, GROUP)


def _diag_of_c(blk):
    t = blk.reshape(SSM_JB, 8, STATE, 8, GROUP)
    d = jnp.sum(t * jnp.eye(8, dtype=blk.dtype)[None, :, None, :, None], axis=3)
    return d.transpose(0, 1, 3, 2).reshape(N_GROUPS, GROUP, STATE)


def _to_scan_order(v):
    seq, w = v.shape
    return v.reshape(seq // SSM_T, SUBLANES, SSM_L, w).transpose(0, 2, 1, 3).reshape(seq, w)


def _adamw_math(w, g, m, v):
    nm = ADAM_B1 * m + (1.0 - ADAM_B1) * g
    nv = ADAM_B2 * v + (1.0 - ADAM_B2) * jnp.square(g)
    m_hat = nm / (1.0 - ADAM_B1 ** ADAM_STEP)
    v_hat = nv / (1.0 - ADAM_B2 ** ADAM_STEP)
    return -ADAM_LR * (m_hat / (jnp.sqrt(v_hat) + ADAM_EPS) + ADAM_WD * w), nm, nv


def _adamw(w, g, m, v, *, name, tm, deps=()):
    rows, cols = w.shape
    nd = len(deps)

    def body(w_ref, g_ref, m_ref, v_ref, *rest):
        d_ref, nm_ref, nv_ref = rest[nd:]
        d_ref[...], nm_ref[...], nv_ref[...] = _adamw_math(w_ref[...], g_ref[...], m_ref[...], v_ref[...])

    spec = pl.BlockSpec((tm, cols), lambda i: (i, 0))
    shp = jax.ShapeDtypeStruct((rows, cols), F32)
    return pl.pallas_call(body, name=name, grid=(rows // tm,), in_specs=[spec] * 4 + [_ANY] * nd,
                          out_specs=[spec] * 3, out_shape=[shp] * 3,
                          compiler_params=_params(("arbitrary",)))(w, g, m, v, *deps)


def _place():
    x, y, c = lax.axis_index("x"), lax.axis_index("y"), lax.axis_index("c")
    chips = [(1 - x, y), (x, 1 - y), (1 - x, 1 - y)]
    return x, y, c, chips


def _remote(src, dst, send_sem, recv_sem, dev):
    return pltpu.make_async_remote_copy(src_ref=src, dst_ref=dst, send_sem=send_sem, recv_sem=recv_sem,
                                        device_id=dev, device_id_type=MESH)


def _place_shard(w, mine_arr, *, name, tm=256, deps=()):
    rows, cols = w.shape

    def body(m_ref, w_ref, *rest):
        rest[-1][...] = w_ref[...].astype(rest[-1].dtype)

    return pl.pallas_call(
        body, name=name,
        grid_spec=pltpu.PrefetchScalarGridSpec(
            num_scalar_prefetch=1, grid=(rows // tm,),
            in_specs=[pl.BlockSpec((tm, cols), lambda i, m: (i, 0))] + [_ANY] * len(deps),
            out_specs=pl.BlockSpec((None, tm, cols), lambda i, m: (m[0], i, 0))),
        out_shape=jax.ShapeDtypeStruct((N_CHIPS, rows, cols), _WIRE),
        compiler_params=_params(("arbitrary",)),
    )(mine_arr, w, *deps)


_HBM = pl.BlockSpec(memory_space=pltpu.HBM)
_SEM = pl.BlockSpec(memory_space=pltpu.SEMAPHORE)
_EFFECT = pltpu.SideEffectType.DATAFLOW_SIDE_EFFECTING


def _copies_start(name, bufs, plan, count, after=()):
    nb, na = len(bufs), len(after)

    def body(*refs):
        send_sems, recv_sems, token = refs[nb + na], refs[nb + na + 1], refs[-1]
        copies = plan(refs[:nb])
        assert len(copies) == count
        for i, (src, dst, dev, _) in enumerate(copies):
            _remote(src, dst, send_sems.at[i], recv_sems.at[i], dev).start()
        token[...] = jnp.zeros_like(token)

    res = pl.pallas_call(
        body, name=name, in_specs=[_HBM] * nb + [_ANY] * na,
        out_specs=(_SEM, _SEM, *[_HBM] * nb, pl.BlockSpec(memory_space=pltpu.VMEM)),
        out_shape=(pltpu.SemaphoreType.DMA((count,)), pltpu.SemaphoreType.DMA((count,)),
                   *[pltpu.HBM(b.shape, b.dtype) for b in bufs], jax.ShapeDtypeStruct((SUBLANES, LANES), F32)),
        input_output_aliases={i: 2 + i for i in range(nb)},
        compiler_params=pltpu.CompilerParams(has_side_effects=_EFFECT),
    )(*[pltpu.with_memory_space_constraint(b, pltpu.HBM) for b in bufs], *after)
    return (res[0], res[1]), list(res[2:2 + nb]), res[-1]


def _copies_wait(name, bufs, sems, plan, after=(), which=None):
    nb, na = len(bufs), len(after)

    def body(*refs):
        send_sems, recv_sems = refs[nb], refs[nb + 1]
        for i, (src, _, dev, land) in enumerate(plan(refs[:nb])):
            if which is not None and i not in which:
                continue
            cp = _remote(src, land, send_sems.at[i], recv_sems.at[i], dev)
            cp.wait_send()
            cp.wait_recv()

    res = pl.pallas_call(
        body, name=name, in_specs=[_HBM] * nb + [_SEM, _SEM] + [_ANY] * na, out_specs=[_HBM] * nb,
        out_shape=[pltpu.HBM(b.shape, b.dtype) for b in bufs],
        input_output_aliases={i: i for i in range(nb)},
        compiler_params=pltpu.CompilerParams(has_side_effects=_EFFECT),
    )(*bufs, *sems, *after)
    return list(res)


def _plan_gather_ici(fulls, which=(0, 1, 2)):
    x, y, c, chips = _place()
    copies = []
    for f in fulls:
        half = pl.ds(c * (f.shape[1] // 2), f.shape[1] // 2)
        own = f.at[2 * x + y, half]
        for chip in [chips[k] for k in which]:
            copies.append((own, own, (*chip, c), f.at[2 * chip[0] + chip[1], half]))
    return copies


def _plan_gather_d2d(fulls, which=(0, 1, 2)):
    x, y, c, chips = _place()
    copies = []
    for f in fulls:
        r2 = f.shape[1] // 2
        for chip in [chips[k] for k in which]:
            blk = 2 * chip[0] + chip[1]
            landed = f.at[blk, pl.ds(c * r2, r2)]
            copies.append((landed, landed, (x, y, 1 - c), f.at[blk, pl.ds((1 - c) * r2, r2)]))
    return copies


def _plan_relay_direct(fulls):
    (f,) = fulls
    x, y, c, chips = _place()
    half = pl.ds(c * (f.shape[1] // 2), f.shape[1] // 2)
    own = f.at[2 * x + y, half]
    return [(own, own, (*chip, c), f.at[2 * chip[0] + chip[1], half]) for chip in chips[:2]]


def _plan_relay_forward(fulls, k):
    (f,) = fulls
    x, y, c, chips = _place()
    r2 = f.shape[1] // 2
    half, other = pl.ds(c * r2, r2), pl.ds((1 - c) * r2, r2)
    quarter = pl.ds(c * r2 + k * (r2 // 2), r2 // 2)
    blk, far = 2 * chips[k][0] + chips[k][1], 2 * chips[2][0] + chips[2][1]
    passed, landed = f.at[blk, quarter], f.at[blk, half]
    return [(passed, passed, (*chips[1 - k], c), f.at[far, quarter]), (landed, landed, (x, y, 1 - c), f.at[blk, other])]


def _plan_relay_last(fulls):
    (f,) = fulls
    x, y, c, chips = _place()
    r2 = f.shape[1] // 2
    far = 2 * chips[2][0] + chips[2][1]
    landed = f.at[far, pl.ds(c * r2, r2)]
    return [(landed, landed, (x, y, 1 - c), f.at[far, pl.ds((1 - c) * r2, r2)])]


def _plan_swap_halves(refs):
    x, y, c, _ = _place()
    n = len(refs) // 2
    copies = []
    for g, land in zip(refs[:n], refs[n:]):
        r2 = g.shape[1] // 2
        copies.append((g.at[:, pl.ds((1 - c) * r2, r2), :], land, (x, y, 1 - c), land))
    return copies


def _plan_scatter_chips(refs):
    x, y, c, chips = _place()
    n = len(refs) // 2
    copies = []
    for h, land in zip(refs[:n], refs[n:]):
        for k, chip in enumerate(chips):
            copies.append((h.at[2 * chip[0] + chip[1]], land.at[k], (*chip, c), land.at[k]))
    return copies


def _plan_join_halves(totals):
    x, y, c, _ = _place()
    copies = []
    for t in totals:
        r2 = t.shape[0] // 2
        mine = t.at[pl.ds(c * r2, r2)]
        copies.append((mine, mine, (x, y, 1 - c), t.at[pl.ds((1 - c) * r2, r2)]))
    return copies


def _add_sibling_half(g, got, c_arr, *, name, tm):
    _, rows, cols = g.shape
    r2 = rows // 2
    nb = r2 // tm

    def body(c_ref, g_ref, r_ref, o_ref):
        o_ref[...] = (g_ref[...].astype(F32) + r_ref[...].astype(F32)).astype(o_ref.dtype)

    return pl.pallas_call(
        body, name=name,
        grid_spec=pltpu.PrefetchScalarGridSpec(
            num_scalar_prefetch=1, grid=(N_CHIPS, nb),
            in_specs=[pl.BlockSpec((None, tm, cols), lambda b, i, c: (b, c[0] * nb + i, 0)),
                      pl.BlockSpec((None, tm, cols), lambda b, i, c: (b, i, 0))],
            out_specs=pl.BlockSpec((None, tm, cols), lambda b, i, c: (b, i, 0))),
        out_shape=jax.ShapeDtypeStruct((N_CHIPS, r2, cols), _WIRE),
        compiler_params=_params(("arbitrary", "arbitrary")),
    )(c_arr, g, got)


def _add_chips(h, got, place_arr, *, name, tm):
    _, r2, cols = h.shape
    nb = r2 // tm

    def body(p_ref, h_ref, r_ref, o_ref):
        o_ref[...] = ((h_ref[...].astype(F32) + r_ref[0].astype(F32)) + r_ref[1].astype(F32)) + r_ref[2].astype(F32)

    return pl.pallas_call(
        body, name=name,
        grid_spec=pltpu.PrefetchScalarGridSpec(
            num_scalar_prefetch=1, grid=(nb,),
            in_specs=[pl.BlockSpec((None, tm, cols), lambda i, p: (p[0], i, 0)),
                      pl.BlockSpec((3, tm, cols), lambda i, p: (0, i, 0))],
            out_specs=pl.BlockSpec((tm, cols), lambda i, p: (p[1] * nb + i, 0))),
        out_shape=jax.ShapeDtypeStruct((2 * r2, cols), F32),
        compiler_params=_params(("arbitrary",)),
    )(place_arr, h, got)


class _ReduceScatter:
    def __init__(self, tag, names, grads):
        self.tag, self.names, self.n = tag, names, len(names)
        core = lax.axis_index("c").astype(jnp.int32)
        chip = (2 * lax.axis_index("x") + lax.axis_index("y")).astype(jnp.int32)
        self.c_arr, self.place_arr = core.reshape(1), jnp.stack([chip, core])
        self.bufs = list(grads)

    def _start(self, step, bufs, plan, count, after):
        self.plan = plan
        self.step = f"grad_{step}_{self.tag}"
        self.sems, self.bufs, token = _copies_start(self.step + "_start", bufs, plan, count, after)
        return [token]

    def _wait(self, after):
        self.bufs = _copies_wait(self.step + "_wait", self.bufs, self.sems, self.plan, after)
        return self.bufs

    def start_swap(self, after=()):
        lands = [lax.empty((N_CHIPS, g.shape[1] // 2, g.shape[2]), g.dtype) for g in self.bufs]
        return self._start("swap", self.bufs + lands, _plan_swap_halves, self.n, after)

    def start_scatter(self, after):
        bufs = self._wait(after)
        pair = [_add_sibling_half(g, r, self.c_arr, name=f"grad_add_sibling_{nm}", tm=min(256, g.shape[1] // 2))
                for nm, g, r in zip(self.names, bufs[:self.n], bufs[self.n:])]
        lands = [lax.empty((3,) + h.shape[1:], h.dtype) for h in pair]
        return self._start("scatter", pair + lands, _plan_scatter_chips, 3 * self.n, ())

    def start_join(self, after):
        bufs = self._wait(after)
        total = [_add_chips(h, r, self.place_arr, name=f"grad_add_chips_{nm}", tm=min(256, h.shape[1]))
                 for nm, h, r in zip(self.names, bufs[:self.n], bufs[self.n:])]
        return self._start("join", total, _plan_join_halves, self.n, ())

    def finish(self, after):
        return dict(zip(self.names, self._wait(after)))


def _all_gather_small(v):
    m_per, n = v.shape

    def body(x_ref, out_ref, send_sems, recv_sems, local_sem):
        x, y, c, chips = _place()
        me, sibling = (x, y, c), (x, y, 1 - c)

        def rows(px, py, pc):
            return out_ref.at[4 * px + 2 * py + pc]

        def copy(k, block, to, src=None):
            return _remote(rows(*block) if src is None else src, rows(*block), send_sems.at[k], recv_sems.at[k], to)

        mine = pltpu.make_async_copy(x_ref, rows(*me), local_sem)
        mine.start()
        first = [copy(0, me, sibling, src=x_ref)]
        first += [copy(1 + j, me, (*chip, c), src=x_ref) for j, chip in enumerate(chips)]
        for cp in first:
            cp.start()
        passed = [copy(4 + j, (*chip, c), sibling) for j, chip in enumerate(chips)]
        for j, chip in enumerate(chips):
            copy(1 + j, (*chip, c), me).wait_recv()
            passed[j].start()
        copy(0, sibling, me).wait_recv()
        for j, chip in enumerate(chips):
            copy(4 + j, (*chip, 1 - c), me).wait_recv()
        for cp in first + passed:
            cp.wait_send()
        mine.wait()

    return pl.pallas_call(
        body, name="gather_small_grads",
        out_shape=jax.ShapeDtypeStruct((8, m_per, n), v.dtype),
        in_specs=[pl.BlockSpec(memory_space=pltpu.VMEM)], out_specs=pl.BlockSpec(memory_space=pltpu.VMEM),
        scratch_shapes=[pltpu.SemaphoreType.DMA((7,)), pltpu.SemaphoreType.DMA((7,)), pltpu.SemaphoreType.DMA],
        compiler_params=pltpu.CompilerParams(vmem_limit_bytes=VMEM_LIMIT),
    )(v)


def _sum8(v, *, name):
    _, m, n = v.shape

    def body(v_ref, o_ref):
        acc = v_ref[0]
        for d in range(1, 8):
            acc = acc + v_ref[d]
        o_ref[...] = acc

    return pl.pallas_call(body, name=name, out_shape=jax.ShapeDtypeStruct((m, n), F32),
                          compiler_params=pltpu.CompilerParams(vmem_limit_bytes=VMEM_LIMIT))(v)


def _local_step(x, target, norm_w, q_norm_w, k_norm_w, sinks, a_re, a_im, log_dt, b_re, b_im, c_re, c_im, d_skip,
                b_glu, io):
    seq = x.shape[0]
    qw2 = jnp.tile(q_norm_w.reshape(1, HEAD_DIM), (1, HEADS_PER_TILE))
    kw2 = jnp.tile(k_norm_w.reshape(1, HEAD_DIM), (1, HEADS_PER_TILE))
    nw, bg = norm_w.reshape(1, D_MODEL), b_glu.reshape(1, D_MODEL)
    dsk = d_skip.reshape(1, SSM_W)

    h, rstd = _rms_fwd(x, nw, deps=io.begin())
    proj, w_in4 = io.projection(h)
    attn, lse, ya_in = _attn2_fwd(proj, qw2, kw2, sinks, deps=io.after_proj(proj))
    w_ap4 = io.weight("w_attn_proj", ya_in)
    w_glu4, w_sp4, w_out = io.weight("w_glu", ya_in), io.weight("w_ssm_proj", ya_in), io.weight("w_out", ya_in)
    y_a = _mm(ya_in, w_ap4, mode="nn", name="mm_attn_proj", tm=2048, tn=512, tk=ATTN_W, b_blocked=True,
              rows_outer=True, out_dtype=_MXU)

    flat_a = (a_re.reshape(1, N_STATES), a_im.reshape(1, N_STATES), jnp.repeat(log_dt, STATE).reshape(1, N_STATES))
    coef = _ssm_params_fwd(*flat_a)
    bre_blk, bim_blk = _block_diag_b(b_re).astype(_MXU), _block_diag_b(b_im).astype(_MXU)
    cre_blk, cim_blk = _block_diag_c(c_re).astype(_MXU), _block_diag_c(c_im).astype(_MXU)
    u_scan = _to_scan_order(proj[:, OFF_U * CW:OFF_U * CW + SSM_W])
    y_scan, yg, s_re, s_im, i_re, i_im = _ssm_fwd(u_scan, bre_blk, bim_blk, cre_blk, cim_blk, dsk, coef)
    glu, ys_in = _mm_glu_gate(yg, w_glu4, bg, proj)
    y_s = _mm(ys_in, w_sp4, mode="nn", name="mm_ssm_proj", tm=2048, tn=512, tk=SSM_W, b_blocked=True,
              rows_outer=True, out_dtype=_MXU)

    merged, dout, dout_b, sq = _mm_merge_out_loss(proj, y_a, y_s, w_out, x, target)
    loss = 0.5 * jnp.sum(sq) / D_MODEL

    d_ya, d_ys, d_proj = _mm_merge_bwd(dout_b, w_out, proj, y_a, y_s)
    g_w_out = _mm(merged, dout_b, mode="tn", name="mm_g_w_out", tm=1024, tn=D_MODEL, tk=1024, out_dtype=_WIRE)

    d_ya_in = _mm(d_ya, w_ap4, mode="nt", name="mm_d_attn_gate", tm=2048, tn=ATTN_W, tk=512, b_blocked=True)
    g_w_ap = _mm(ya_in, d_ya, mode="tn", name="mm_g_w_attn_proj", tm=ATTN_W, tn=D_MODEL, tk=2048, out_dtype=_WIRE,
                 out_blocked=True)

    g_w_sp = _mm(ys_in, d_ys, mode="tn", name="mm_g_w_ssm_proj", tm=SSM_W, tn=D_MODEL, tk=2048, out_dtype=_WIRE,
                 out_blocked=True)
    d_glu, d_proj, g_bglu = _mm_ssm_gate_bwd(d_ys, w_sp4, glu, bg, proj, d_proj)
    d_yg = _mm(d_glu, w_glu4, mode="nt", name="mm_d_gelu", tm=2048, tn=SSM_W, tk=512, b_blocked=True)
    g_w_glu = _mm(yg, d_glu, mode="tn", name="mm_g_w_glu", tm=SSM_W, tn=D_MODEL, tk=2048, out_dtype=_WIRE, out_blocked=True)
    dep = io.later_grads(dict(w_attn_proj=g_w_ap, w_glu=g_w_glu, w_ssm_proj=g_w_sp,
                              w_out=g_w_out.reshape(N_CHIPS, D_MODEL // N_CHIPS, D_MODEL)))

    d_proj, g_qw2, g_kw2, g_sk = _attn2_bwd(proj, qw2, kw2, sinks, lse, attn, d_ya_in, d_proj, deps=dep)
    dep = io.before_scan_backward([d_proj])
    (d_proj, g_bre, g_bim, g_cre, g_cim, g_dsk, g_abr, g_abi, g_cfr, g_cfi) = _ssm_bwd(
        _to_scan_order(d_yg), y_scan, u_scan, s_re, s_im, i_re, i_im, bre_blk, bim_blk, cre_blk, cim_blk, dsk, coef,
        d_proj, deps=dep)
    g_are, g_aim, g_ldt = _ssm_params_bwd(*flat_a, g_abr, g_abi, g_cfr, g_cfi)
    g_are, g_aim = g_are.reshape(N_GROUPS, STATE), g_aim.reshape(N_GROUPS, STATE)
    g_ldt = g_ldt.reshape(N_GROUPS, STATE).sum(axis=1)
    dep = io.before_input_projection_grad([d_proj]) + io.small_grads(dict(
        q_norm_w=g_qw2[0, :HEAD_DIM] + g_qw2[0, HEAD_DIM:], k_norm_w=g_kw2[0, :HEAD_DIM] + g_kw2[0, HEAD_DIM:],
        sinks=g_sk.reshape(N_Q_HEADS), A_re=g_are, A_im=g_aim, log_dt=g_ldt,
        B_re=_diag_of_b(g_bre), B_im=_diag_of_b(g_bim), C_re=_diag_of_c(g_cre), C_im=_diag_of_c(g_cim),
        D_skip=g_dsk.reshape(N_GROUPS, GROUP), b_glu=g_bglu.reshape(D_MODEL)))
    g_w_in = _mm(h, d_proj, mode="tn", name="mm_g_w_in", tm=1024, tn=IN_W // 4, tk=1024, out_dtype=_WIRE,
                 out_blocked=True, deps=dep)
    dep = io.input_projection_grad(g_w_in)
    d_h = _mm(d_proj, w_in4, mode="nt", name="mm_d_h", tm=1024, tn=D_MODEL, tk=IN_W // 4, b_blocked=True, deps=dep)
    grad_x, g_nw = _rms_bwd(d_h, x, rstd, nw, dout)
    return loss, grad_x, g_nw.reshape(D_MODEL)


_SMALL = ["norm_w", "q_norm_w", "k_norm_w", "sinks", "A_re", "A_im", "log_dt", "B_re", "B_im", "C_re", "C_im",
          "D_skip", "b_glu"]
_BIG = ["w_in", "w_attn_proj", "w_glu", "w_ssm_proj", "w_out"]
_LATER = _BIG[1:]
_RELATIONS = ("flip_x", "flip_y", "flip_xy")
_ORDER = ["norm_w", "w_in", "q_norm_w", "k_norm_w", "sinks", "w_attn_proj", "A_re", "A_im", "log_dt", "B_re", "B_im",
          "C_re", "C_im", "D_skip", "w_glu", "b_glu", "w_ssm_proj", "w_out"]
_PACK_W = 1024


def _packed_rows(size):
    unit = SUBLANES * _PACK_W
    return -(-size // unit) * SUBLANES


def _pack_small(d, names):
    parts = []
    for n in names:
        flat = d[n].reshape(-1).astype(F32)
        rows = _packed_rows(flat.shape[0])
        parts.append(jnp.pad(flat, (0, rows * _PACK_W - flat.shape[0])).reshape(rows, _PACK_W))
    return jnp.concatenate(parts, axis=0)


def _unpack_small(packed, like, names):
    out, pos = {}, 0
    for n in names:
        rows = _packed_rows(like[n].size)
        out[n] = packed[pos:pos + rows].reshape(-1)[:like[n].size].reshape(like[n].shape)
        pos += rows
    return out


def _place_block(v, index_arr, *, name):
    rows, cols = v.shape

    def body(i_ref, v_ref, o_ref):
        o_ref[...] = v_ref[...]

    return pl.pallas_call(
        body, name=name,
        grid_spec=pltpu.PrefetchScalarGridSpec(
            num_scalar_prefetch=1, grid=(1,),
            in_specs=[pl.BlockSpec((rows, cols), lambda i, d: (0, 0))],
            out_specs=pl.BlockSpec((None, rows, cols), lambda i, d: (d[0], 0, 0))),
        out_shape=jax.ShapeDtypeStruct((8, rows, cols), v.dtype),
        compiler_params=_params(("arbitrary",)),
    )(index_arr, v)


def _plan_all_to_all(refs):
    (land,) = refs
    x, y, c, _ = _place()
    own = land.at[4 * x + 2 * y + c]
    copies = []
    for fx, fy, fc in [(0, 0, 1), (0, 1, 0), (0, 1, 1), (1, 0, 0), (1, 0, 1), (1, 1, 0), (1, 1, 1)]:
        px, py, pc = (1 - x) if fx else x, (1 - y) if fy else y, (1 - c) if fc else c
        copies.append((own, own, (px, py, pc), land.at[4 * px + 2 * py + pc]))
    return copies


def _adamw_whole(w, g, m, v, *, name):
    def body(w_ref, g_ref, m_ref, v_ref, d_ref, nm_ref, nv_ref):
        d_ref[...], nm_ref[...], nv_ref[...] = _adamw_math(w_ref[...], g_ref[...], m_ref[...], v_ref[...])

    return pl.pallas_call(body, name=name, out_shape=[jax.ShapeDtypeStruct(w.shape, F32)] * 3)(w, g, m, v)


class _Exchanges:
    def __init__(self, w, m, v):
        self.w, self.m, self.v = w, m, v
        self.grads, self.delta, self.new_m, self.new_v = {}, {}, {}, {}

    def _adamw(self, names, deps):
        for n in names:
            self.delta[n], self.new_m[n], self.new_v[n] = _adamw(
                self.w[n], self.grads[n], self.m[n], self.v[n], name=f"adamw_{n}", tm=128, deps=deps)

    def begin(self):
        chip = (2 * lax.axis_index("x") + lax.axis_index("y")).astype(jnp.int32).reshape(1)
        w_in = _place_shard(self.w["w_in"], chip, name="place_w_in")
        self.w_in_sems, self.w_in_buf, token = _copies_start("gather_w_in_direct_start", [w_in], _plan_relay_direct, 2)
        self.later_full = [_place_shard(self.w[n], chip, name=f"place_{n}", deps=[token]) for n in _LATER]
        return self.later_full

    def projection(self, h):
        x, y = lax.axis_index("x"), lax.axis_index("y")
        blks = [jnp.asarray(b, jnp.int32).reshape(1)
                for b in (2 * x + y, 2 * (1 - x) + y, 2 * x + (1 - y), 2 * (1 - x) + (1 - y))]
        bufs = self.w_in_buf
        proj = _mm_chip_block(h, bufs[0], blks[0], None, name="mm_proj_own", out_dtype=_MXU)
        relay, token = [], proj
        for k, tag in enumerate(_RELATIONS[:2]):
            bufs = _copies_wait(f"gather_w_in_direct_{tag}_wait", bufs, self.w_in_sems, _plan_relay_direct, [token],
                                which=(k,))
            plan = functools.partial(_plan_relay_forward, k=k)
            sems, bufs, token = _copies_start(f"gather_w_in_relay_{tag}_start", bufs, plan, 2)
            relay.append((sems, plan))
        self.rest = _copies_start("gather_ici_rest_start", self.later_full, _plan_gather_ici, 3 * len(_LATER),
                                  after=[token])
        token = self.rest[2]
        for k, tag in enumerate(_RELATIONS[:2]):
            bufs = _copies_wait(f"gather_w_in_hand_{tag}_wait", bufs, relay[k][0], relay[k][1], [token], which=(1,))
            token = proj = _mm_chip_block(h, bufs[0], blks[1 + k], proj, name=f"mm_proj_{tag}", out_dtype=_MXU)
        for k, tag in enumerate(_RELATIONS[:2]):
            bufs = _copies_wait(f"gather_w_in_relay_{tag}_wait", bufs, relay[k][0], relay[k][1], [token], which=(0,))
        sems, bufs, token = _copies_start("gather_w_in_last_start", bufs, _plan_relay_last, 1)
        bufs = _copies_wait("gather_w_in_last_wait", bufs, sems, _plan_relay_last, [token])
        proj = _mm_chip_block(h, bufs[0], blks[3], proj, name="mm_proj_flip_xy", out_dtype=_MXU)
        return proj, bufs[0]

    def weight(self, name, after):
        if self.rest is not None:
            sems, bufs = self.rest
            later = dict(zip(_LATER, _copies_wait("gather_d2d_rest_wait", bufs, sems, _plan_gather_d2d, [after])))
            later["w_out"] = later["w_out"].reshape(D_MODEL, D_MODEL)
            self.later, self.rest = later, None
        return self.later[name]

    def after_proj(self, proj):
        sems, bufs, _ = self.rest
        bufs = _copies_wait("gather_ici_rest_wait", bufs, sems, _plan_gather_ici, [proj])
        sems, bufs, token = _copies_start("gather_d2d_rest_start", bufs, _plan_gather_d2d, 3 * len(_LATER))
        self.rest = (sems, bufs)
        return [token]

    def later_grads(self, grads):
        self.rs_later = _ReduceScatter("later", _LATER, [grads[n] for n in _LATER])
        return self.rs_later.start_swap()

    def before_scan_backward(self, after):
        return self.rs_later.start_scatter(after)

    def before_input_projection_grad(self, after):
        return self.rs_later.start_join(after)

    def input_projection_grad(self, g_w_in):
        self.grads.update(self.rs_later.finish([g_w_in]))
        self.rs_in = _ReduceScatter("w_in", ["w_in"], [g_w_in])
        self._adamw(_LATER, self.rs_in.start_swap())
        return self.rs_in.start_scatter([self.delta[n] for n in _LATER])

    def _adamw_small(self, names):
        for n in names:
            self.delta[n], self.new_m[n], self.new_v[n] = _adamw_whole(
                self.w[n], self.grads[n], self.m[n], self.v[n], name=f"adamw_{n}")

    def small_grads(self, grads):
        me = (4 * lax.axis_index("x") + 2 * lax.axis_index("y") + lax.axis_index("c")).astype(jnp.int32).reshape(1)
        land = _place_block(_pack_small(grads, _SMALL[1:]), me, name="place_small_grads")
        self.small = _copies_start("gather_small_start", [land], _plan_all_to_all, 7)
        return [self.small[2]]

    def finish(self, g_norm_w, loss, after):
        join = self.rs_in.start_join(after)
        sems, bufs, _ = self.small
        (land,) = _copies_wait("gather_small_wait", bufs, sems, _plan_all_to_all, join)
        self.grads.update(_unpack_small(_sum8(land, name="sum_small_grads"), self.w, _SMALL[1:]))
        self._adamw_small(_SMALL[1:])
        rows = _packed_rows(g_norm_w.size)
        late = jnp.concatenate([_pack_small(dict(norm_w=g_norm_w), _SMALL[:1]),
                                jnp.pad(loss.reshape(1, 1), ((0, SUBLANES - 1), (0, _PACK_W - 1)))], axis=0)
        late = _sum8(_all_gather_small(late), name="sum_norm_w_grad_and_loss")
        self.grads.update(_unpack_small(late[:rows], self.w, _SMALL[:1]))
        self._adamw_small(_SMALL[:1])
        self.grads.update(self.rs_in.finish([self.delta[_SMALL[0]]]))
        self._adamw(["w_in"], ())
        return late[rows, 0]


def kernel(x, norm_w, w_in, q_norm_w, k_norm_w, sinks, w_attn_proj, A_re, A_im, log_dt, B_re, B_im, C_re, C_im, D_skip, w_glu, b_glu, w_ssm_proj, w_out, loss_target, m_norm_w, m_w_in, m_q_norm_w, m_k_norm_w, m_sinks, m_w_attn_proj, m_A_re, m_A_im, m_log_dt, m_B_re, m_B_im, m_C_re, m_C_im, m_D_skip, m_w_glu, m_b_glu, m_w_ssm_proj, m_w_out, v_norm_w, v_w_in, v_q_norm_w, v_k_norm_w, v_sinks, v_w_attn_proj, v_A_re, v_A_im, v_log_dt, v_B_re, v_B_im, v_C_re, v_C_im, v_D_skip, v_w_glu, v_b_glu, v_w_ssm_proj, v_w_out):
    w = dict(norm_w=norm_w, w_in=w_in, q_norm_w=q_norm_w, k_norm_w=k_norm_w, sinks=sinks, w_attn_proj=w_attn_proj,
             A_re=A_re, A_im=A_im, log_dt=log_dt, B_re=B_re, B_im=B_im, C_re=C_re, C_im=C_im, D_skip=D_skip,
             w_glu=w_glu, b_glu=b_glu, w_ssm_proj=w_ssm_proj, w_out=w_out)
    m = dict(norm_w=m_norm_w, w_in=m_w_in, q_norm_w=m_q_norm_w, k_norm_w=m_k_norm_w, sinks=m_sinks,
             w_attn_proj=m_w_attn_proj, A_re=m_A_re, A_im=m_A_im, log_dt=m_log_dt, B_re=m_B_re, B_im=m_B_im,
             C_re=m_C_re, C_im=m_C_im, D_skip=m_D_skip, w_glu=m_w_glu, b_glu=m_b_glu, w_ssm_proj=m_w_ssm_proj,
             w_out=m_w_out)
    v = dict(norm_w=v_norm_w, w_in=v_w_in, q_norm_w=v_q_norm_w, k_norm_w=v_k_norm_w, sinks=v_sinks,
             w_attn_proj=v_w_attn_proj, A_re=v_A_re, A_im=v_A_im, log_dt=v_log_dt, B_re=v_B_re, B_im=v_B_im,
             C_re=v_C_re, C_im=v_C_im, D_skip=v_D_skip, w_glu=v_w_glu, b_glu=v_b_glu, w_ssm_proj=v_w_ssm_proj,
             w_out=v_w_out)

    io = _Exchanges(w, m, v)
    loss, grad_x, g_norm_w = _local_step(x[0], loss_target[0], norm_w, q_norm_w, k_norm_w, sinks, A_re, A_im, log_dt,
                                         B_re, B_im, C_re, C_im, D_skip, b_glu, io)
    loss = io.finish(g_norm_w, loss, [grad_x])
    grads, delta, new_m, new_v = io.grads, io.delta, io.new_m, io.new_v

    return (loss, grad_x[None], *[grads[n] for n in _ORDER], *[delta[n] for n in _ORDER],
            *[new_m[n] for n in _ORDER], *[new_v[n] for n in _ORDER])
```

```python
import functools
import math

import jax
import jax.numpy as jnp
from jax import lax
from jax.experimental import pallas as pl
from jax.experimental.pallas import tpu as pltpu

F32 = jnp.float32
_MXU = jnp.bfloat16
_WIRE = jnp.bfloat16

LANES = 128
SUBLANES = 8
VMEM_LIMIT = 56 * 1024 * 1024

D_MODEL = 2048
HEAD_DIM = 64
N_Q_HEADS = 16
N_KV_HEADS = 4
Q_PER_KV = 4
ATTN_W = 1024
KV_W = 256
WINDOW = 128
SSM_W = 1024
GROUP = 16
N_GROUPS = 64
STATE = 64
N_STATES = N_GROUPS * STATE
IN_W = 8704
NORM_EPS = 1e-6
N_CHIPS = 4
CW = 512
OFF_AGATE, OFF_U, OFF_Z, OFF_GA, OFF_GS = 3, 5, 7, 9, 13

SSM_T = 256
SSM_L = SSM_T // SUBLANES
SSM_JB = 8
SSM_SB = N_STATES // SSM_JB

ADAM_LR, ADAM_B1, ADAM_B2, ADAM_EPS, ADAM_WD, ADAM_STEP = 0.001, 0.9, 0.999, 1e-08, 0.01, 10

MESH = pl.DeviceIdType.MESH
_ANY = pl.BlockSpec(memory_space=pl.ANY)


def _params(sem=None):
    return pltpu.CompilerParams(dimension_semantics=sem, vmem_limit_bytes=VMEM_LIMIT)


def _mm(a, b, *, mode, name, tm, tn, tk, out_dtype=F32, b_blocked=False, out_blocked=False, rows_outer=False,
        deps=()):
    nd = len(deps)
    if mode == "tn":
        K, M = a.shape
    else:
        M, K = a.shape
    if mode == "nn":
        N = b.shape[0] * b.shape[2] if b_blocked else b.shape[1]
    elif mode == "nt":
        N = b.shape[1] if b_blocked else b.shape[0]
    else:
        N = b.shape[1]
    tm, tn, tk = min(tm, M), min(tn, N), min(tk, K)
    nj, ni, nk = N // tn, M // tm, K // tk
    assert nj * tn == N and ni * tm == M and nk * tk == K, (name, M, N, K)
    dims = {"nn": (((1,), (0,)), ((), ())), "nt": (((1,), (1,)), ((), ())), "tn": (((0,), (0,)), ((), ()))}[mode]

    if mode == "tn":
        a_spec = pl.BlockSpec((tk, tm), lambda j, i, k: (k, i))
    else:
        a_spec = pl.BlockSpec((tm, tk), lambda j, i, k: (i, k))
    if mode == "nn":
        if b_blocked:
            assert b.shape[0] == nj and b.shape[2] == tn
            b_spec = pl.BlockSpec((None, tk, tn), lambda j, i, k: (j, k, 0))
        else:
            b_spec = pl.BlockSpec((tk, tn), lambda j, i, k: (k, j))
    elif mode == "nt":
        if b_blocked:
            assert b.shape[0] == nk and b.shape[2] == tk
            b_spec = pl.BlockSpec((None, tn, tk), lambda j, i, k: (k, j, 0))
        else:
            b_spec = pl.BlockSpec((tn, tk), lambda j, i, k: (j, k))
    else:
        b_spec = pl.BlockSpec((tk, tn), lambda j, i, k: (k, j))
    whole_out = out_blocked and nj == 1
    if whole_out:
        assert ni == 1
        o_spec = pl.BlockSpec((N_CHIPS, tm, tn // N_CHIPS), lambda j, i, k: (0, 0, 0))
        o_shape = jax.ShapeDtypeStruct((N_CHIPS, M, tn // N_CHIPS), out_dtype)
    elif out_blocked:
        assert nj == N_CHIPS
        o_spec = pl.BlockSpec((None, tm, tn), lambda j, i, k: (j, i, 0))
        o_shape = jax.ShapeDtypeStruct((nj, M, tn), out_dtype)
    else:
        o_spec = pl.BlockSpec((tm, tn), lambda j, i, k: (i, j))
        o_shape = jax.ShapeDtypeStruct((M, N), out_dtype)
    use_acc = nk > 1 and (out_dtype != F32 or whole_out)

    def body(a_ref, b_ref, *rest):
        o_ref, scratch = rest[nd], rest[nd + 1:]

        def product():
            return lax.dot_general(a_ref[...].astype(_MXU), b_ref[...].astype(_MXU), dims, preferred_element_type=F32)

        def write(result):
            if whole_out:
                w = tn // N_CHIPS
                for c in range(N_CHIPS):
                    o_ref[c] = result[:, c * w:(c + 1) * w].astype(o_ref.dtype)
            else:
                o_ref[...] = result.astype(o_ref.dtype)

        if nk == 1:
            write(product())
            return
        k = pl.program_id(2)
        acc = scratch[0] if use_acc else o_ref

        @pl.when(k == 0)
        def _():
            acc[...] = jnp.zeros_like(acc)

        acc[...] += product()

        if use_acc:
            @pl.when(k == nk - 1)
            def _():
                write(acc[...])

    specs = [a_spec, b_spec, o_spec]
    grid = (nj, ni, nk)
    if rows_outer:
        specs = [pl.BlockSpec(s.block_shape, lambda i, j, k, f=s.index_map: f(j, i, k)) for s in specs]
        grid = (ni, nj, nk)
    return pl.pallas_call(
        body, name=name, grid=grid, in_specs=specs[:2] + [_ANY] * nd, out_specs=specs[2],
        out_shape=o_shape, scratch_shapes=[pltpu.VMEM((tm, tn), F32)] if use_acc else [],
        compiler_params=_params(("parallel", "parallel", "arbitrary")),
    )(a, b, *deps)


def _mm_chip_block(a, b4, blk, prev, *, name, tm=512, out_dtype=F32, deps=()):
    M, K = a.shape
    nchip, _, C = b4.shape
    tm = min(tm, M)
    extra = ([] if prev is None else [prev]) + list(deps)

    def body(blk_ref, a_ref, b_ref, *rest):
        rest[-1][...] = jnp.dot(a_ref[...].astype(_MXU), b_ref[...].astype(_MXU),
                                preferred_element_type=F32).astype(rest[-1].dtype)

    return pl.pallas_call(
        body, name=name,
        grid_spec=pltpu.PrefetchScalarGridSpec(
            num_scalar_prefetch=1, grid=(M // tm,),
            in_specs=[pl.BlockSpec((tm, K), lambda i, c: (i, 0)), pl.BlockSpec((None, K, C), lambda i, c: (c[0], 0, 0))]
            + [_ANY] * len(extra),
            out_specs=pl.BlockSpec((tm, C), lambda i, c: (i, c[0]))),
        out_shape=jax.ShapeDtypeStruct((M, nchip * C), out_dtype),
        input_output_aliases={} if prev is None else {3: 0},
        compiler_params=_params(("arbitrary",)),
    )(blk, a, b4, *extra)


def _mm_merge_out_loss(proj, y_a, y_s, w_out, x, target, *, tm=256):
    rows, d = x.shape
    ncol = d // CW

    def body(*refs):
        ga_refs, gs_refs = refs[:ncol], refs[ncol:2 * ncol]
        ya_ref, ys_ref, w_ref, x_ref, t_ref, m_ref, d_ref, db_ref, sq_ref = refs[2 * ncol:]
        for j in range(ncol):
            cols = slice(j * CW, (j + 1) * CW)
            m_ref[:, cols] = (_sigmoid(ga_refs[j][...].astype(F32)) * ya_ref[:, cols].astype(F32)
                              + _sigmoid(gs_refs[j][...].astype(F32)) * ys_ref[:, cols].astype(F32)).astype(m_ref.dtype)
        mo = jnp.dot(m_ref[...], w_ref[...].astype(_MXU), preferred_element_type=F32)
        err = (x_ref[...] + mo) - t_ref[...]
        dout = err * (1.0 / d)
        d_ref[...] = dout
        db_ref[...] = dout.astype(db_ref.dtype)
        part = _colsum(err * err)
        i = pl.program_id(0)

        @pl.when(i == 0)
        def _():
            sq_ref[...] = part

        @pl.when(i > 0)
        def _():
            sq_ref[...] += part

    tile = pl.BlockSpec((tm, d), lambda i: (i, 0))
    gate = [pl.BlockSpec((tm, CW), lambda i, c=off + j: (i, c)) for off in (OFF_GA, OFF_GS) for j in range(ncol)]
    return pl.pallas_call(
        body, name="mm_merge_out_loss", grid=(rows // tm,),
        in_specs=gate + [tile, tile, pl.BlockSpec((d, d), lambda i: (0, 0), pipeline_mode=pl.Buffered(1)), tile, tile],
        out_specs=[tile, tile, tile, pl.BlockSpec((1, d), lambda i: (0, 0))],
        out_shape=[jax.ShapeDtypeStruct((rows, d), _MXU), jax.ShapeDtypeStruct((rows, d), F32),
                   jax.ShapeDtypeStruct((rows, d), _MXU), jax.ShapeDtypeStruct((1, d), F32)],
        compiler_params=_params(("arbitrary",)),
    )(*([proj] * (2 * ncol)), y_a, y_s, w_out, x, target)


def _mm_merge_bwd(dout_b, w_out, proj, y_a, y_s, *, tm=512):
    rows, d = y_a.shape
    ncol = d // CW

    def body(do_ref, w_ref, *refs):
        ga_refs, gs_refs = refs[:ncol], refs[ncol:2 * ncol]
        ya_ref, ys_ref, dya_ref, dys_ref, dg_ref = refs[2 * ncol:]
        dm = lax.dot_general(do_ref[...].astype(_MXU), w_ref[...].astype(_MXU), _NT, preferred_element_type=F32)
        for j in range(ncol):
            cols = slice(j * CW, (j + 1) * CW)
            dmj = dm[:, cols]
            sa, ss = _sigmoid(ga_refs[j][...].astype(F32)), _sigmoid(gs_refs[j][...].astype(F32))
            dya_ref[:, cols] = (sa * dmj).astype(dya_ref.dtype)
            dys_ref[:, cols] = (ss * dmj).astype(dys_ref.dtype)
            dg_ref[:, cols] = (dmj * ya_ref[:, cols].astype(F32) * sa * (1.0 - sa)).astype(dg_ref.dtype)
            dg_ref[:, d + j * CW:d + (j + 1) * CW] = (dmj * ys_ref[:, cols].astype(F32) * ss
                                                      * (1.0 - ss)).astype(dg_ref.dtype)

    tile = pl.BlockSpec((tm, d), lambda i: (i, 0))
    gate = [pl.BlockSpec((tm, CW), lambda i, c=off + j: (i, c)) for off in (OFF_GA, OFF_GS) for j in range(ncol)]
    both = pl.BlockSpec((pl.Element(tm), pl.Element(2 * d)), lambda i: (i * tm, OFF_GA * CW))
    return pl.pallas_call(
        body, name="mm_merge_bwd", grid=(rows // tm,),
        in_specs=[tile, pl.BlockSpec((d, d), lambda i: (0, 0), pipeline_mode=pl.Buffered(1))] + gate + [tile, tile],
        out_specs=[tile, tile, both],
        out_shape=[jax.ShapeDtypeStruct((rows, d), _MXU)] * 2 + [jax.ShapeDtypeStruct((rows, IN_W), _MXU)],
        compiler_params=_params(("arbitrary",)),
    )(dout_b, w_out, *([proj] * (2 * ncol)), y_a, y_s)


def _mm_glu_gate(yg, w_glu4, b_glu, proj, *, tm=1024):
    rows, k = yg.shape
    nj, _, tn = w_glu4.shape
    w = nj * tn // 2
    tm = min(tm, rows)

    def body(a_ref, w_ref, ba_ref, bb_ref, z0_ref, z1_ref, glu_ref, ys_ref):
        j = pl.program_id(1)
        for c in range(nj):
            @pl.when(j == c)
            def _(c=c):
                glu_ref[:, c * tn:(c + 1) * tn] = jnp.dot(a_ref[...].astype(_MXU), w_ref[...].astype(_MXU),
                                                          preferred_element_type=F32).astype(glu_ref.dtype)

        @pl.when(j == nj - 1)
        def _():
            z = jnp.concatenate([z0_ref[...], z1_ref[...]], axis=1).astype(F32)
            ys_ref[...] = ((glu_ref[:, :w].astype(F32) + ba_ref[...]) * _sigmoid(glu_ref[:, w:].astype(F32) + bb_ref[...])
                           * (z * _sigmoid(z))).astype(ys_ref.dtype)

    bias = lambda c: pl.BlockSpec((1, w), lambda i, j: (0, c))
    zcol = lambda c: pl.BlockSpec((tm, CW), lambda i, j: (i, OFF_Z + c))
    return pl.pallas_call(
        body, name="mm_glu_gate", grid=(rows // tm, nj),
        in_specs=[pl.BlockSpec((tm, k), lambda i, j: (i, 0)), pl.BlockSpec((None, k, tn), lambda i, j: (j, 0, 0)),
                  bias(0), bias(1), zcol(0), zcol(1)],
        out_specs=[pl.BlockSpec((tm, nj * tn), lambda i, j: (i, 0)), pl.BlockSpec((tm, w), lambda i, j: (i, 0))],
        out_shape=[jax.ShapeDtypeStruct((rows, nj * tn), _MXU), jax.ShapeDtypeStruct((rows, w), _MXU)],
        compiler_params=_params(("arbitrary", "arbitrary")),
    )(yg, w_glu4, b_glu, b_glu, proj, proj)


def _mm_ssm_gate_bwd(d_ys, w_sp4, glu, b_glu, proj, d_proj, *, tm=1024):
    rows, w = glu.shape[0], glu.shape[1] // 2
    nk, tk = w_sp4.shape[0], w_sp4.shape[2]
    tm = min(tm, rows)

    def body(dy_ref, w_ref, ga_ref, gb_ref, ba_ref, bb_ref, z0_ref, z1_ref, buf_ref, dg_ref, dz_ref, db_ref, acc):
        i, k = pl.program_id(0), pl.program_id(1)

        @pl.when(k == 0)
        def _():
            acc[...] = jnp.zeros_like(acc)

        acc[...] += lax.dot_general(dy_ref[...].astype(_MXU), w_ref[...].astype(_MXU), _NT, preferred_element_type=F32)

        @pl.when(k == nk - 1)
        def _():
            dv = acc[...]
            a, sb = ga_ref[...].astype(F32) + ba_ref[...], _sigmoid(gb_ref[...].astype(F32) + bb_ref[...])
            f, df = _silu_and_grad(jnp.concatenate([z0_ref[...], z1_ref[...]], axis=1).astype(F32))
            dga = dv * sb * f
            dgb = dv * a * f * sb * (1.0 - sb)
            dg_ref[:, :w] = dga.astype(dg_ref.dtype)
            dg_ref[:, w:] = dgb.astype(dg_ref.dtype)
            dz_ref[...] = (dv * a * sb * df).astype(dz_ref.dtype)
            part = jnp.concatenate([_colsum(dga), _colsum(dgb)], axis=1)

            @pl.when(i == 0)
            def _():
                db_ref[...] = part

            @pl.when(i > 0)
            def _():
                db_ref[...] += part

    half = lambda c: pl.BlockSpec((tm, w), lambda i, k: (i, c))
    bias = lambda c: pl.BlockSpec((1, w), lambda i, k: (0, c))
    zcol = lambda c: pl.BlockSpec((tm, CW), lambda i, k: (i, OFF_Z + c))
    return pl.pallas_call(
        body, name="mm_ssm_gate_bwd", grid=(rows // tm, nk),
        in_specs=[pl.BlockSpec((tm, tk), lambda i, k: (i, k)), pl.BlockSpec((None, w, tk), lambda i, k: (k, 0, 0)),
                  half(0), half(1), bias(0), bias(1), zcol(0), zcol(1), _ANY],
        out_specs=[pl.BlockSpec((tm, 2 * w), lambda i, k: (i, 0)),
                   pl.BlockSpec((pl.Element(tm), pl.Element(w)), lambda i, k: (i * tm, OFF_Z * CW)),
                   pl.BlockSpec((1, 2 * w), lambda i, k: (0, 0))],
        out_shape=[jax.ShapeDtypeStruct((rows, 2 * w), _MXU), jax.ShapeDtypeStruct(d_proj.shape, d_proj.dtype),
                   jax.ShapeDtypeStruct((1, 2 * w), F32)],
        input_output_aliases={8: 1},
        scratch_shapes=[pltpu.VMEM((tm, w), F32)],
        compiler_params=_params(("arbitrary", "arbitrary")),
    )(d_ys, w_sp4, glu, glu, b_glu, b_glu, proj, proj, d_proj)


def _colsum(v):
    return jnp.sum(v, axis=0, keepdims=True)


def _sigmoid(v):
    return jax.nn.sigmoid(v)


def _silu_and_grad(v):
    s = _sigmoid(v)
    return v * s, s * (1.0 + v * (1.0 - s))


def _rms_fwd(x, w, *, tm=512, deps=()):
    rows, d = x.shape
    nd = len(deps)

    def body(x_ref, w_ref, *rest):
        h_ref, r_ref = rest[nd:]
        xv = x_ref[...]
        r = lax.rsqrt(jnp.mean(xv * xv, axis=-1, keepdims=True) + NORM_EPS)
        h_ref[...] = (xv * r * w_ref[...]).astype(h_ref.dtype)
        r_ref[...] = r

    return pl.pallas_call(
        body, name="rms_fwd", grid=(rows // tm,),
        in_specs=[pl.BlockSpec((tm, d), lambda i: (i, 0)), pl.BlockSpec((1, d), lambda i: (0, 0))] + [_ANY] * nd,
        out_specs=[pl.BlockSpec((tm, d), lambda i: (i, 0)), pl.BlockSpec((tm, 1), lambda i: (i, 0))],
        out_shape=[jax.ShapeDtypeStruct((rows, d), _MXU), jax.ShapeDtypeStruct((rows, 1), F32)],
        compiler_params=_params(("arbitrary",)),
    )(x, w, *deps)


def _rms_bwd(dh, x, rstd, w, dout, *, tm=256):
    rows, d = x.shape

    def body(dh_ref, x_ref, r_ref, w_ref, do_ref, gx_ref, gw_ref):
        dhv, xv, r, wv = dh_ref[...], x_ref[...], r_ref[...], w_ref[...]
        xr = xv * r
        t = jnp.mean(dhv * wv * xr, axis=-1, keepdims=True)
        gx_ref[...] = do_ref[...] + r * (wv * dhv - xr * t)
        part = _colsum(dhv * xr)
        i = pl.program_id(0)

        @pl.when(i == 0)
        def _():
            gw_ref[...] = part

        @pl.when(i > 0)
        def _():
            gw_ref[...] += part

    return pl.pallas_call(
        body, name="rms_bwd", grid=(rows // tm,),
        in_specs=[pl.BlockSpec((tm, d), lambda i: (i, 0)), pl.BlockSpec((tm, d), lambda i: (i, 0)),
                  pl.BlockSpec((tm, 1), lambda i: (i, 0)), pl.BlockSpec((1, d), lambda i: (0, 0)),
                  pl.BlockSpec((tm, d), lambda i: (i, 0))],
        out_specs=[pl.BlockSpec((tm, d), lambda i: (i, 0)), pl.BlockSpec((1, d), lambda i: (0, 0))],
        out_shape=[jax.ShapeDtypeStruct((rows, d), F32), jax.ShapeDtypeStruct((1, d), F32)],
        compiler_params=_params(("arbitrary",)),
    )(dh, x, rstd, w, dout)


_NT = (((1,), (1,)), ((), ()))
_TN = (((0,), (0,)), ((), ()))


QKV_W = ATTN_W + 2 * KV_W
HEADS_PER_TILE = LANES // HEAD_DIM


def _low_half(rows):
    return lax.broadcasted_iota(jnp.int32, (rows, LANES), 1) < HEAD_DIM


def _pair_mean(t, low):
    m_lo = jnp.sum(jnp.where(low, t, 0.0), axis=-1, keepdims=True)
    m_hi = jnp.sum(jnp.where(low, 0.0, t), axis=-1, keepdims=True)
    return jnp.where(low, m_lo, m_hi) * (1.0 / HEAD_DIM)


def _pair_rstd(t, low):
    return lax.rsqrt(_pair_mean(t * t, low) + NORM_EPS)


def _dup_half(t, hi, low):
    swapped = pltpu.roll(t, HEAD_DIM, 1)
    return jnp.where(low, swapped, t) if hi else jnp.where(low, t, swapped)


def _fold_halves(t):
    return t + pltpu.roll(t, HEAD_DIM, 1)


def _split_heads(t, low):
    return [jnp.where(low, t, 0.0), jnp.where(low, 0.0, t)]


def _stacked_band_mask(n):
    rows = Q_PER_KV * WINDOW
    qi = lax.broadcasted_iota(jnp.int32, (rows, 2 * WINDOW), 0) % WINDOW + WINDOW
    kj = lax.broadcasted_iota(jnp.int32, (rows, 2 * WINDOW), 1)
    diff = qi - kj
    first_key = jnp.where(n > 0, 0, WINDOW)
    return (diff >= 0) & (diff < WINDOW) & (kj >= first_key)


def _stacked_sinks(sink_ref, g):
    blk = lax.broadcasted_iota(jnp.int32, (Q_PER_KV * WINDOW, 1), 0) // WINDOW
    col = jnp.full((Q_PER_KV * WINDOW, 1), sink_ref[Q_PER_KV * g], F32)
    for r in range(1, Q_PER_KV):
        col = jnp.where(blk == r, sink_ref[Q_PER_KV * g + r], col)
    return col


def _attn_in_specs(nblk, rev):
    def cur(n):
        return (nblk - 1 - n) if rev else n

    q_spec = pl.BlockSpec((WINDOW, ATTN_W), lambda n: (cur(n), 0))
    kvc_spec = pl.BlockSpec((WINDOW, 2 * KV_W), lambda n: (cur(n), ATTN_W // (2 * KV_W)))
    kvp_spec = pl.BlockSpec((WINDOW, 2 * KV_W), lambda n: (jnp.maximum(cur(n) - 1, 0), ATTN_W // (2 * KV_W)))
    w_spec = pl.BlockSpec((1, LANES), lambda n: (0, 0))
    l_spec = pl.BlockSpec((WINDOW, N_Q_HEADS), lambda n: (cur(n), 0))
    gate_specs = [pl.BlockSpec((WINDOW, CW), lambda n, col=OFF_AGATE + j: (cur(n), col)) for j in range(ATTN_W // CW)]
    return q_spec, kvc_spec, kvp_spec, w_spec, l_spec, gate_specs


def _attn2_fwd(proj, qw2, kw2, sinks, deps=()):
    seq = proj.shape[0]
    nblk = seq // WINDOW
    scale = 1.0 / math.sqrt(HEAD_DIM)
    q_spec, kvc_spec, kvp_spec, w_spec, l_spec, gate_specs = _attn_in_specs(nblk, False)
    nd, ng = len(deps), len(gate_specs)

    def body(sink_ref, q_ref, kvc_ref, kvp_ref, qw_ref, kw_ref, *rest):
        gate_refs = rest[:ng]
        o_ref, lse_ref, ya_ref = rest[ng + nd:]
        n = pl.program_id(0)
        low, low2 = _low_half(WINDOW), _low_half(2 * WINDOW)
        valid = _stacked_band_mask(n)
        head_lane = lax.broadcasted_iota(jnp.int32, (WINDOW, N_Q_HEADS), 1)
        kv = jnp.concatenate([kvp_ref[...], kvc_ref[...]], axis=0).astype(F32)
        qwv, kwv = qw_ref[...], kw_ref[...]
        lse_blk = jnp.zeros((WINDOW, N_Q_HEADS), F32)
        for t in range(N_KV_HEADS // HEADS_PER_TILE):
            kt = kv[:, t * LANES:(t + 1) * LANES]
            vt = kv[:, KV_W + t * LANES:KV_W + (t + 1) * LANES]
            kn = kt * _pair_rstd(kt, low2) * kwv
            for hi in range(HEADS_PER_TILE):
                g = HEADS_PER_TILE * t + hi
                kdup = _dup_half(kn, hi, low2).astype(_MXU)
                vdup = _dup_half(vt, hi, low2).astype(_MXU)
                stack = []
                for tq in (2 * g, 2 * g + 1):
                    qt = q_ref[:, tq * LANES:(tq + 1) * LANES].astype(F32)
                    stack += _split_heads(qt * _pair_rstd(qt, low) * qwv, low)
                qs = jnp.concatenate(stack, axis=0).astype(_MXU)
                s = lax.dot_general(qs, kdup, _NT, preferred_element_type=F32) * scale
                s = jnp.where(valid, s, -1e30)
                sink = _stacked_sinks(sink_ref, g)
                m = jnp.maximum(jnp.max(s, axis=-1, keepdims=True), sink)
                e = jnp.exp(s - m)
                z = jnp.sum(e, axis=-1, keepdims=True) + jnp.exp(sink - m)
                o = jnp.dot((e / z).astype(_MXU), vdup, preferred_element_type=F32)
                for i, tq in enumerate((2 * g, 2 * g + 1)):
                    tile = slice(tq * LANES, (tq + 1) * LANES)
                    out = jnp.where(low, o[2 * i * WINDOW:(2 * i + 1) * WINDOW],
                                    o[(2 * i + 1) * WINDOW:(2 * i + 2) * WINDOW])
                    gate = gate_refs[tq * LANES // CW][:, tq * LANES % CW:tq * LANES % CW + LANES].astype(F32)
                    o_ref[:, tile] = out.astype(o_ref.dtype)
                    ya_ref[:, tile] = (out * (gate * _sigmoid(gate))).astype(ya_ref.dtype)
                lse = m + jnp.log(z)
                for r in range(Q_PER_KV):
                    lse_blk = jnp.where(head_lane == Q_PER_KV * g + r, lse[r * WINDOW:(r + 1) * WINDOW], lse_blk)
        lse_ref[...] = lse_blk

    return pl.pallas_call(
        body, name="attn_fwd", grid=(nblk,),
        in_specs=[pl.BlockSpec(memory_space=pltpu.SMEM), q_spec, kvc_spec, kvp_spec, w_spec, w_spec] + gate_specs
        + [_ANY] * nd,
        out_specs=[q_spec, l_spec, q_spec],
        out_shape=[jax.ShapeDtypeStruct((seq, ATTN_W), _MXU), jax.ShapeDtypeStruct((seq, N_Q_HEADS), F32),
                   jax.ShapeDtypeStruct((seq, ATTN_W), _MXU)],
        compiler_params=_params(("arbitrary",)),
    )(sinks, proj, proj, proj, qw2, kw2, *([proj] * ng), *deps)


def _attn2_bwd(proj, qw2, kw2, sinks, lse, attn, dya, d_proj, deps=()):
    seq = proj.shape[0]
    nblk = seq // WINDOW
    scale = 1.0 / math.sqrt(HEAD_DIM)
    q_spec, kvc_spec, kvp_spec, w_spec, l_spec, gate_specs = _attn_in_specs(nblk, True)
    s_spec = pl.BlockSpec((1, N_Q_HEADS), lambda n: (0, 0))
    d_spec = pl.BlockSpec((WINDOW, QKV_W + ATTN_W), lambda n: (nblk - 1 - n, 0))
    deps = list(deps) + [d_proj]
    nd, ng = len(deps), len(gate_specs)

    def body(sink_ref, q_ref, kvc_ref, kvp_ref, qw_ref, kw_ref, lse_ref, attn_ref, dya_ref, *rest):
        gate_refs = rest[:ng]
        d_ref, dqw_ref, dkw_ref, dsk_ref, carry, do_ref = rest[ng + nd:]
        step = pl.program_id(0)
        n = nblk - 1 - step

        @pl.when(step == 0)
        def _():
            carry[...] = jnp.zeros_like(carry)
            dqw_ref[...] = jnp.zeros_like(dqw_ref)
            dkw_ref[...] = jnp.zeros_like(dkw_ref)
            dsk_ref[...] = jnp.zeros_like(dsk_ref)

        for j, g_ref in enumerate(gate_refs):
            cols = slice(j * CW, (j + 1) * CW)
            f, df = _silu_and_grad(g_ref[...].astype(F32))
            dv = dya_ref[:, cols]
            do_ref[:, cols] = dv * f
            d_ref[:, QKV_W + j * CW:QKV_W + (j + 1) * CW] = (dv * attn_ref[:, cols].astype(F32) * df).astype(d_ref.dtype)

        low, low2 = _low_half(WINDOW), _low_half(2 * WINDOW)
        valid = _stacked_band_mask(n)
        head_lane = lax.broadcasted_iota(jnp.int32, (WINDOW, N_Q_HEADS), 1)
        sink_lane = lax.broadcasted_iota(jnp.int32, (1, N_Q_HEADS), 1)
        kv = jnp.concatenate([kvp_ref[...], kvc_ref[...]], axis=0).astype(F32)
        qwv, kwv = qw_ref[...], kw_ref[...]
        lse_blk = lse_ref[...]
        dqw = jnp.zeros((1, LANES), F32)
        dkw = jnp.zeros((1, LANES), F32)
        dsk = jnp.zeros((1, N_Q_HEADS), F32)
        for t in range(N_KV_HEADS // HEADS_PER_TILE):
            kt = kv[:, t * LANES:(t + 1) * LANES]
            vt = kv[:, KV_W + t * LANES:KV_W + (t + 1) * LANES]
            rk = _pair_rstd(kt, low2)
            kn = kt * rk * kwv
            dkn_t = jnp.zeros((2 * WINDOW, LANES), F32)
            dv_t = jnp.zeros((2 * WINDOW, LANES), F32)
            for hi in range(HEADS_PER_TILE):
                g = HEADS_PER_TILE * t + hi
                kdup = _dup_half(kn, hi, low2).astype(_MXU)
                vdup = _dup_half(vt, hi, low2).astype(_MXU)
                tiles = (2 * g, 2 * g + 1)
                qx, rq, stack, dstack, lse_rows = [], [], [], [], []
                for tq in tiles:
                    qt = q_ref[:, tq * LANES:(tq + 1) * LANES].astype(F32)
                    r = _pair_rstd(qt, low)
                    rq.append(r)
                    qx.append(qt * r)
                    stack += _split_heads(qx[-1] * qwv, low)
                    dstack += _split_heads(do_ref[:, tq * LANES:(tq + 1) * LANES], low)
                for r in range(Q_PER_KV):
                    lse_rows.append(jnp.sum(jnp.where(head_lane == Q_PER_KV * g + r, lse_blk, 0.0), axis=-1, keepdims=True))
                qs = jnp.concatenate(stack, axis=0).astype(_MXU)
                dos = jnp.concatenate(dstack, axis=0).astype(_MXU)
                lse_col = jnp.concatenate(lse_rows, axis=0)
                s = lax.dot_general(qs, kdup, _NT, preferred_element_type=F32) * scale
                s = jnp.where(valid, s, -1e30)
                p = jnp.exp(s - lse_col)
                dp = lax.dot_general(dos, vdup, _NT, preferred_element_type=F32)
                dsum = jnp.sum(p * dp, axis=-1, keepdims=True)
                ds = (p * (dp - dsum) * scale).astype(_MXU)
                dsink = -jnp.exp(_stacked_sinks(sink_ref, g) - lse_col) * dsum
                for r in range(Q_PER_KV):
                    dsk = dsk + jnp.where(sink_lane == Q_PER_KV * g + r, _colsum(dsink[r * WINDOW:(r + 1) * WINDOW]), 0.0)
                dv_g = _fold_halves(lax.dot_general(p.astype(_MXU), dos, _TN, preferred_element_type=F32))
                dkn_g = _fold_halves(lax.dot_general(ds, qs, _TN, preferred_element_type=F32))
                dv_t = jnp.where(low2, dv_t, dv_g) if hi else jnp.where(low2, dv_g, dv_t)
                dkn_t = jnp.where(low2, dkn_t, dkn_g) if hi else jnp.where(low2, dkn_g, dkn_t)
                dqn = jnp.dot(ds, kdup, preferred_element_type=F32)
                for i, tq in enumerate(tiles):
                    dqn_t = jnp.where(low, dqn[2 * i * WINDOW:(2 * i + 1) * WINDOW],
                                      dqn[(2 * i + 1) * WINDOW:(2 * i + 2) * WINDOW])
                    dq = rq[i] * (qwv * dqn_t - qx[i] * _pair_mean(dqn_t * qwv * qx[i], low))
                    d_ref[:, tq * LANES:(tq + 1) * LANES] = dq.astype(d_ref.dtype)
                    dqw = dqw + _colsum(dqn_t * qx[i])
            k_cols = slice(t * LANES, (t + 1) * LANES)
            v_cols = slice(KV_W + t * LANES, KV_W + (t + 1) * LANES)
            dkn_c = dkn_t[WINDOW:] + carry[:, k_cols]
            rc = rk[WINDOW:]
            kx = kt[WINDOW:] * rc
            dk = rc * (kwv * dkn_c - kx * _pair_mean(dkn_c * kwv * kx, low))
            d_ref[:, ATTN_W + t * LANES:ATTN_W + (t + 1) * LANES] = dk.astype(d_ref.dtype)
            d_ref[:, ATTN_W + KV_W + t * LANES:ATTN_W + KV_W + (t + 1) * LANES] = (
                dv_t[WINDOW:] + carry[:, v_cols]).astype(d_ref.dtype)
            carry[:, k_cols] = dkn_t[:WINDOW]
            carry[:, v_cols] = dv_t[:WINDOW]
            dkw = dkw + _colsum(dkn_c * kx)
        dqw_ref[...] += dqw
        dkw_ref[...] += dkw
        dsk_ref[...] += dsk

    return pl.pallas_call(
        body, name="attn_bwd", grid=(nblk,),
        in_specs=[pl.BlockSpec(memory_space=pltpu.SMEM), q_spec, kvc_spec, kvp_spec, w_spec, w_spec, l_spec, q_spec,
                  q_spec] + gate_specs + [_ANY] * nd,
        out_specs=[d_spec, w_spec, w_spec, s_spec],
        out_shape=[jax.ShapeDtypeStruct(d_proj.shape, d_proj.dtype), jax.ShapeDtypeStruct((1, LANES), F32),
                   jax.ShapeDtypeStruct((1, LANES), F32), jax.ShapeDtypeStruct((1, N_Q_HEADS), F32)],
        input_output_aliases={9 + ng + nd - 1: 0},
        scratch_shapes=[pltpu.VMEM((WINDOW, 2 * KV_W), F32), pltpu.VMEM((WINDOW, ATTN_W), F32)],
        compiler_params=_params(("arbitrary",)),
    )(sinks, proj, proj, proj, qw2, kw2, lse, attn, dya, *([proj] * ng), *deps)


def _ssm_discretise(a_re, a_im, log_dt):
    dt = jnp.exp(log_dt)
    mag = jnp.exp(dt * a_re)
    ab_re = mag * jnp.cos(dt * a_im)
    ab_im = mag * jnp.sin(dt * a_im)
    num_re = ab_re - 1.0
    num_im = ab_im
    den = a_re * a_re + a_im * a_im
    cf_re = (num_re * a_re + num_im * a_im) / den
    cf_im = (num_im * a_re - num_re * a_im) / den
    return ab_re, ab_im, cf_re, cf_im


def _ssm_params_fwd(a_re, a_im, log_dt):
    shp = jax.ShapeDtypeStruct(a_re.shape, F32)

    def body(are_ref, aim_ref, ldt_ref, abr_ref, abi_ref, cfr_ref, cfi_ref, alr_ref, ali_ref):
        abr, abi, cfr, cfi = _ssm_discretise(are_ref[...], aim_ref[...], ldt_ref[...])
        abr_ref[...], abi_ref[...], cfr_ref[...], cfi_ref[...] = abr, abi, cfr, cfi
        pr, pi = abr, abi
        for _ in range(int(math.log2(SSM_L))):
            pr, pi = pr * pr - pi * pi, 2.0 * pr * pi
        alr_ref[...], ali_ref[...] = pr, pi

    return pl.pallas_call(body, name="ssm_params_fwd", out_shape=[shp] * 6)(a_re, a_im, log_dt)


def _ssm_params_bwd(a_re, a_im, log_dt, d_abr, d_abi, d_cfr, d_cfi):
    def body(are_ref, aim_ref, ldt_ref, g0, g1, g2, g3, dare_ref, daim_ref, dldt_ref):
        _, vjp = jax.vjp(_ssm_discretise, are_ref[...], aim_ref[...], ldt_ref[...])
        dare_ref[...], daim_ref[...], dldt_ref[...] = vjp((g0[...], g1[...], g2[...], g3[...]))

    return pl.pallas_call(
        body, name="ssm_params_bwd",
        out_shape=[jax.ShapeDtypeStruct(a_re.shape, F32), jax.ShapeDtypeStruct(a_im.shape, F32),
                   jax.ShapeDtypeStruct(log_dt.shape, F32)],
    )(a_re, a_im, log_dt, d_abr, d_abi, d_cfr, d_cfi)


def _scan_cols(j):
    return pl.ds(j * SSM_SB, SSM_SB)


def _rows8(r):
    return pl.ds(pl.multiple_of(r * SUBLANES, SUBLANES), SUBLANES)


def _bcast8(row):
    return jnp.broadcast_to(row, (SUBLANES, row.shape[-1]))


def _token_order_pick():
    tok = lax.broadcasted_iota(jnp.int32, (SSM_T, SSM_T), 0)
    row = lax.broadcasted_iota(jnp.int32, (SSM_T, SSM_T), 1)
    return (row == SUBLANES * (tok % SSM_L) + tok // SSM_L).astype(_MXU)


SCAN_UNROLL = 16


def _scan_loop(n, step, init):
    def trip(o, carry):
        for i in range(SCAN_UNROLL):
            carry = step(o * SCAN_UNROLL + i, carry)
        return carry

    return lax.fori_loop(0, n // SCAN_UNROLL, trip, init)


def _ssm_fwd(u, b_re, b_im, c_re, c_im, d_skip, coef):
    seq = u.shape[0]
    nc = seq // SSM_T
    T, L = SSM_T, SSM_L

    def body(u_ref, bre_ref, bim_ref, cre_ref, cim_ref, d_ref, are_ref, aim_ref, cfr_ref, cfi_ref, alr_ref, ali_ref,
             y_ref, yg_ref, sre_ref, sim_ref, ire_ref, iim_ref, car_re, car_im, end_re, end_im, yg_scan):
        c = pl.program_id(0)

        @pl.when(c == 0)
        def _():
            car_re[...] = jnp.zeros_like(car_re)
            car_im[...] = jnp.zeros_like(car_im)

        for j in range(SSM_JB):
            ub = u_ref[:, j * LANES:(j + 1) * LANES].astype(_MXU)
            bur = jnp.dot(ub, bre_ref[j], preferred_element_type=F32)
            bui = jnp.dot(ub, bim_ref[j], preferred_element_type=F32)
            cfr, cfi = cfr_ref[:, _scan_cols(j)], cfi_ref[:, _scan_cols(j)]
            sre_ref[:, _scan_cols(j)] = cfr * bur - cfi * bui
            sim_ref[:, _scan_cols(j)] = cfr * bui + cfi * bur

        for j in range(SSM_JB):
            cols = _scan_cols(j)
            ar, ai = _bcast8(are_ref[:, cols]), _bcast8(aim_ref[:, cols])

            def step1(r, s, cols=cols, ar=ar, ai=ai):
                sr, si = s
                rows = _rows8(r)
                return (ar * sr - ai * si + sre_ref[rows, cols], ar * si + ai * sr + sim_ref[rows, cols])

            zero = jnp.zeros((SUBLANES, SSM_SB), F32)
            er, ei = _scan_loop(L, step1, (zero, zero))
            end_re[:, cols] = er
            end_im[:, cols] = ei

        alr, ali = alr_ref[...], ali_ref[...]
        cr, ci = car_re[...], car_im[...]
        ire_ref[0:1, :] = cr
        iim_ref[0:1, :] = ci
        for i in range(1, SUBLANES):
            er, ei = end_re[i - 1:i, :], end_im[i - 1:i, :]
            cr, ci = alr * cr - ali * ci + er, alr * ci + ali * cr + ei
            ire_ref[i:i + 1, :] = cr
            iim_ref[i:i + 1, :] = ci

        for j in range(SSM_JB):
            cols = _scan_cols(j)
            ar, ai = _bcast8(are_ref[:, cols]), _bcast8(aim_ref[:, cols])

            def step2(r, s, cols=cols, ar=ar, ai=ai):
                sr, si = s
                rows = _rows8(r)
                nr = ar * sr - ai * si + sre_ref[rows, cols]
                ni = ar * si + ai * sr + sim_ref[rows, cols]
                sre_ref[rows, cols] = nr
                sim_ref[rows, cols] = ni
                return nr, ni

            _scan_loop(L, step2, (ire_ref[:, cols], iim_ref[:, cols]))

        car_re[...] = sre_ref[T - 1:T, :]
        car_im[...] = sim_ref[T - 1:T, :]

        for j in range(SSM_JB):
            cols = _scan_cols(j)
            ch = slice(j * LANES, (j + 1) * LANES)
            y = (jnp.dot(sre_ref[:, cols].astype(_MXU), cre_ref[j], preferred_element_type=F32)
                 - jnp.dot(sim_ref[:, cols].astype(_MXU), cim_ref[j], preferred_element_type=F32))
            y = y + d_ref[:, ch] * u_ref[:, ch].astype(F32)
            y_ref[:, ch] = y
            yg_scan[:, ch] = jax.nn.gelu(y).astype(yg_scan.dtype)
        yg_ref[...] = jnp.dot(_token_order_pick(), yg_scan[...], preferred_element_type=F32).astype(yg_ref.dtype)

    tok = pl.BlockSpec((T, SSM_W), lambda c: (c, 0))
    st = pl.BlockSpec((T, N_STATES), lambda c: (c, 0))
    ini = pl.BlockSpec((None, SUBLANES, N_STATES), lambda c: (c, 0, 0))
    bsp = pl.BlockSpec((SSM_JB, LANES, SSM_SB), lambda c: (0, 0, 0))
    csp = pl.BlockSpec((SSM_JB, SSM_SB, LANES), lambda c: (0, 0, 0))
    row_w = pl.BlockSpec((1, SSM_W), lambda c: (0, 0))
    row_s = pl.BlockSpec((1, N_STATES), lambda c: (0, 0))
    return pl.pallas_call(
        body, name="ssm_fwd", grid=(nc,),
        in_specs=[tok, bsp, bsp, csp, csp, row_w] + [row_s] * 6,
        out_specs=[tok, tok, st, st, ini, ini],
        out_shape=[jax.ShapeDtypeStruct((seq, SSM_W), F32), jax.ShapeDtypeStruct((seq, SSM_W), _MXU),
                   jax.ShapeDtypeStruct((seq, N_STATES), F32), jax.ShapeDtypeStruct((seq, N_STATES), F32),
                   jax.ShapeDtypeStruct((nc, SUBLANES, N_STATES), F32),
                   jax.ShapeDtypeStruct((nc, SUBLANES, N_STATES), F32)],
        scratch_shapes=[pltpu.VMEM((1, N_STATES), F32), pltpu.VMEM((1, N_STATES), F32),
                        pltpu.VMEM((SUBLANES, N_STATES), F32), pltpu.VMEM((SUBLANES, N_STATES), F32),
                        pltpu.VMEM((T, SSM_W), _MXU)],
        compiler_params=_params(("arbitrary",)),
    )(u, b_re, b_im, c_re, c_im, d_skip, *coef)


def _ssm_bwd(dyg, y, u, s_re, s_im, i_re, i_im, b_re, b_im, c_re, c_im, d_skip, coef, d_proj, deps=()):
    seq = u.shape[0]
    nc = seq // SSM_T
    T, L = SSM_T, SSM_L
    deps = list(deps) + [d_proj]

    def body(dyg_ref, y_ref, u_ref, sre_ref, sim_ref, ire_ref, iim_ref, bre_ref, bim_ref, cre_ref, cim_ref, d_ref,
             are_ref, aim_ref, cfr_ref, cfi_ref, alr_ref, ali_ref, *rest):
        (du_ref, dbre_out, dbim_out, dcre_out, dcim_out, dd_ref, dar_ref, dai_ref, dcfr_ref, dcfi_ref,
         lre, lim, car_re, car_im, end_re, end_im, ini_re, ini_im, dbre_ref, dbim_ref, dcre_ref, dcim_ref,
         dy_ref, du_scan) = rest[len(deps):]
        step = pl.program_id(0)
        dy_ref[...] = jax.vjp(jax.nn.gelu, y_ref[...])[1](dyg_ref[...])[0]

        @pl.when(step == 0)
        def _():
            car_re[...] = jnp.zeros_like(car_re)
            car_im[...] = jnp.zeros_like(car_im)
            for ref in (dbre_ref, dbim_ref, dcre_ref, dcim_ref, dd_ref, dar_ref, dai_ref, dcfr_ref, dcfi_ref):
                ref[...] = jnp.zeros_like(ref)

        for j in range(SSM_JB):
            dyb = dy_ref[:, j * LANES:(j + 1) * LANES].astype(_MXU)
            lre[:, _scan_cols(j)] = lax.dot_general(dyb, cre_ref[j], _NT, preferred_element_type=F32)
            lim[:, _scan_cols(j)] = -lax.dot_general(dyb, cim_ref[j], _NT, preferred_element_type=F32)

        for j in range(SSM_JB):
            cols = _scan_cols(j)
            ar, ai = _bcast8(are_ref[:, cols]), _bcast8(aim_ref[:, cols])

            def step1(t, s, cols=cols, ar=ar, ai=ai):
                sr, si = s
                rows = _rows8(L - 1 - t)
                return (ar * sr + ai * si + lre[rows, cols], ar * si - ai * sr + lim[rows, cols])

            zero = jnp.zeros((SUBLANES, SSM_SB), F32)
            er, ei = _scan_loop(L, step1, (zero, zero))
            end_re[:, cols] = er
            end_im[:, cols] = ei

        alr, ali = alr_ref[...], ali_ref[...]
        cr, ci = car_re[...], car_im[...]
        ini_re[SUBLANES - 1:SUBLANES, :] = cr
        ini_im[SUBLANES - 1:SUBLANES, :] = ci
        for i in range(SUBLANES - 2, -1, -1):
            er, ei = end_re[i + 1:i + 2, :], end_im[i + 1:i + 2, :]
            cr, ci = alr * cr + ali * ci + er, alr * ci - ali * cr + ei
            ini_re[i:i + 1, :] = cr
            ini_im[i:i + 1, :] = ci

        for j in range(SSM_JB):
            cols = _scan_cols(j)
            ar, ai = _bcast8(are_ref[:, cols]), _bcast8(aim_ref[:, cols])

            def step2(t, s, cols=cols, ar=ar, ai=ai):
                sr, si = s
                rows = _rows8(L - 1 - t)
                nr = ar * sr + ai * si + lre[rows, cols]
                ni = ar * si - ai * sr + lim[rows, cols]
                lre[rows, cols] = nr
                lim[rows, cols] = ni
                return nr, ni

            _scan_loop(L, step2, (ini_re[:, cols], ini_im[:, cols]))

        car_re[...] = lre[0:1, :]
        car_im[...] = lim[0:1, :]

        head, tail, body_rows = slice(0, SUBLANES), slice(SUBLANES, T), slice(0, T - SUBLANES)
        for j in range(SSM_JB):
            cols = _scan_cols(j)
            ch = slice(j * LANES, (j + 1) * LANES)
            lr, li = lre[:, cols], lim[:, cols]
            lt_r, lt_i, sp_r, sp_i = lre[tail, cols], lim[tail, cols], sre_ref[body_rows, cols], sim_ref[body_rows, cols]
            lh_r, lh_i, si_r, si_i = lre[head, cols], lim[head, cols], ire_ref[:, cols], iim_ref[:, cols]
            dar_ref[:, cols] += _colsum(lt_r * sp_r + lt_i * sp_i) + _colsum(lh_r * si_r + lh_i * si_i)
            dai_ref[:, cols] += _colsum(lt_i * sp_r - lt_r * sp_i) + _colsum(lh_i * si_r - lh_r * si_i)
            ub = u_ref[:, ch].astype(_MXU)
            uf = ub.astype(F32)
            bur = jnp.dot(ub, bre_ref[j], preferred_element_type=F32)
            bui = jnp.dot(ub, bim_ref[j], preferred_element_type=F32)
            dcfr_ref[:, cols] += _colsum(lr * bur + li * bui)
            dcfi_ref[:, cols] += _colsum(li * bur - lr * bui)
            cfr, cfi = cfr_ref[:, cols], cfi_ref[:, cols]
            dbur = (cfr * lr + cfi * li).astype(_MXU)
            dbui = (cfr * li - cfi * lr).astype(_MXU)
            dyf = dy_ref[:, ch]
            dyb = dyf.astype(_MXU)
            du = (lax.dot_general(dbur, bre_ref[j], _NT, preferred_element_type=F32)
                  + lax.dot_general(dbui, bim_ref[j], _NT, preferred_element_type=F32) + d_ref[:, ch] * dyf)
            du_scan[:, ch] = du.astype(du_scan.dtype)
            dbre_ref[j] += lax.dot_general(ub, dbur, _TN, preferred_element_type=F32)
            dbim_ref[j] += lax.dot_general(ub, dbui, _TN, preferred_element_type=F32)
            dcre_ref[j] += lax.dot_general(sre_ref[:, cols].astype(_MXU), dyb, _TN, preferred_element_type=F32)
            dcim_ref[j] -= lax.dot_general(sim_ref[:, cols].astype(_MXU), dyb, _TN, preferred_element_type=F32)
            dd_ref[:, ch] += _colsum(dyf * uf)
        du_ref[...] = jnp.dot(_token_order_pick(), du_scan[...], preferred_element_type=F32).astype(du_ref.dtype)

        @pl.when(step == nc - 1)
        def _():
            for acc, out in ((dbre_ref, dbre_out), (dbim_ref, dbim_out), (dcre_ref, dcre_out), (dcim_ref, dcim_out)):
                pltpu.sync_copy(acc, out)

    tok = pl.BlockSpec((T, SSM_W), lambda c: (nc - 1 - c, 0))
    st = pl.BlockSpec((T, N_STATES), lambda c: (nc - 1 - c, 0))
    ini = pl.BlockSpec((None, SUBLANES, N_STATES), lambda c: (nc - 1 - c, 0, 0))
    bsp = pl.BlockSpec((SSM_JB, LANES, SSM_SB), lambda c: (0, 0, 0))
    csp = pl.BlockSpec((SSM_JB, SSM_SB, LANES), lambda c: (0, 0, 0))
    row_w = pl.BlockSpec((1, SSM_W), lambda c: (0, 0))
    row_s = pl.BlockSpec((1, N_STATES), lambda c: (0, 0))
    big = pltpu.VMEM((T, N_STATES), F32)
    one = pltpu.VMEM((1, N_STATES), F32)
    eight = pltpu.VMEM((SUBLANES, N_STATES), F32)
    return pl.pallas_call(
        body, name="ssm_bwd", grid=(nc,),
        in_specs=[tok, tok, tok, st, st, ini, ini, bsp, bsp, csp, csp, row_w] + [row_s] * 6 + [_ANY] * len(deps),
        out_specs=[pl.BlockSpec((pl.Element(T), pl.Element(SSM_W)), lambda c: ((nc - 1 - c) * T, OFF_U * CW)),
                   _ANY, _ANY, _ANY, _ANY, row_w, row_s, row_s, row_s, row_s],
        input_output_aliases={18 + len(deps) - 1: 0},
        out_shape=[jax.ShapeDtypeStruct(d_proj.shape, d_proj.dtype),
                   jax.ShapeDtypeStruct((SSM_JB, LANES, SSM_SB), F32), jax.ShapeDtypeStruct((SSM_JB, LANES, SSM_SB), F32),
                   jax.ShapeDtypeStruct((SSM_JB, SSM_SB, LANES), F32), jax.ShapeDtypeStruct((SSM_JB, SSM_SB, LANES), F32),
                   jax.ShapeDtypeStruct((1, SSM_W), F32)] + [jax.ShapeDtypeStruct((1, N_STATES), F32)] * 4,
        scratch_shapes=[big, big, one, one, eight, eight, eight, eight,
                        pltpu.VMEM((SSM_JB, LANES, SSM_SB), F32), pltpu.VMEM((SSM_JB, LANES, SSM_SB), F32),
                        pltpu.VMEM((SSM_JB, SSM_SB, LANES), F32), pltpu.VMEM((SSM_JB, SSM_SB, LANES), F32),
                        pltpu.VMEM((T, SSM_W), F32), pltpu.VMEM((T, SSM_W), _MXU)],
        compiler_params=_params(("arbitrary",)),
    )(dyg, y, u, s_re, s_im, i_re, i_im, b_re, b_im, c_re, c_im, d_skip, *coef, *deps)


def _block_diag_b(b):
    t = b.reshape(SSM_JB, 8, STATE, GROUP).transpose(0, 1, 3, 2)
    eye = jnp.eye(8, dtype=b.dtype)
    return (t[:, :, :, None, :] * eye[None, :, None, :, None]).reshape(SSM_JB, LANES, SSM_SB)


def _block_diag_c(c):
    t = c.reshape(SSM_JB, 8, GROUP, STATE).transpose(0, 1, 3, 2)
    eye = jnp.eye(8, dtype=c.dtype)
    return (t[:, :, :, None, :] * eye[None, :, None, :, None]).reshape(SSM_JB, SSM_SB, LANES)


def _diag_of_b(blk):
    t = blk.reshape(SSM_JB, 8, GROUP, 8, STATE)
    d = jnp.sum(t * jnp.eye(8, dtype=blk.dtype)[None, :, None, :, None], axis=3)
    return d.transpose(0, 1, 3, 2).reshape(N_GROUPS, STATE, GROUP)


def _diag_of_c(blk):
    t = blk.reshape(SSM_JB, 8, STATE, 8, GROUP)
    d = jnp.sum(t * jnp.eye(8, dtype=blk.dtype)[None, :, None, :, None], axis=3)
    return d.transpose(0, 1, 3, 2).reshape(N_GROUPS, GROUP, STATE)


def _to_scan_order(v):
    seq, w = v.shape
    return v.reshape(seq // SSM_T, SUBLANES, SSM_L, w).transpose(0, 2, 1, 3).reshape(seq, w)


def _adamw_math(w, g, m, v):
    nm = ADAM_B1 * m + (1.0 - ADAM_B1) * g
    nv = ADAM_B2 * v + (1.0 - ADAM_B2) * jnp.square(g)
    m_hat = nm / (1.0 - ADAM_B1 ** ADAM_STEP)
    v_hat = nv / (1.0 - ADAM_B2 ** ADAM_STEP)
    return -ADAM_LR * (m_hat / (jnp.sqrt(v_hat) + ADAM_EPS) + ADAM_WD * w), nm, nv


def _adamw(w, g, m, v, *, name, tm, deps=()):
    rows, cols = w.shape
    nd = len(deps)

    def body(w_ref, g_ref, m_ref, v_ref, *rest):
        d_ref, nm_ref, nv_ref = rest[nd:]
        d_ref[...], nm_ref[...], nv_ref[...] = _adamw_math(w_ref[...], g_ref[...], m_ref[...], v_ref[...])

    spec = pl.BlockSpec((tm, cols), lambda i: (i, 0))
    shp = jax.ShapeDtypeStruct((rows, cols), F32)
    return pl.pallas_call(body, name=name, grid=(rows // tm,), in_specs=[spec] * 4 + [_ANY] * nd,
                          out_specs=[spec] * 3, out_shape=[shp] * 3,
                          compiler_params=_params(("arbitrary",)))(w, g, m, v, *deps)


def _place():
    x, y, c = lax.axis_index("x"), lax.axis_index("y"), lax.axis_index("c")
    chips = [(1 - x, y), (x, 1 - y), (1 - x, 1 - y)]
    return x, y, c, chips


def _remote(src, dst, send_sem, recv_sem, dev):
    return pltpu.make_async_remote_copy(src_ref=src, dst_ref=dst, send_sem=send_sem, recv_sem=recv_sem,
                                        device_id=dev, device_id_type=MESH)


def _place_shard(w, mine_arr, *, name, tm=256, deps=()):
    rows, cols = w.shape

    def body(m_ref, w_ref, *rest):
        rest[-1][...] = w_ref[...].astype(rest[-1].dtype)

    return pl.pallas_call(
        body, name=name,
        grid_spec=pltpu.PrefetchScalarGridSpec(
            num_scalar_prefetch=1, grid=(rows // tm,),
            in_specs=[pl.BlockSpec((tm, cols), lambda i, m: (i, 0))] + [_ANY] * len(deps),
            out_specs=pl.BlockSpec((None, tm, cols), lambda i, m: (m[0], i, 0))),
        out_shape=jax.ShapeDtypeStruct((N_CHIPS, rows, cols), _WIRE),
        compiler_params=_params(("arbitrary",)),
    )(mine_arr, w, *deps)


_HBM = pl.BlockSpec(memory_space=pltpu.HBM)
_SEM = pl.BlockSpec(memory_space=pltpu.SEMAPHORE)
_EFFECT = pltpu.SideEffectType.DATAFLOW_SIDE_EFFECTING


def _copies_start(name, bufs, plan, count, after=()):
    nb, na = len(bufs), len(after)

    def body(*refs):
        send_sems, recv_sems, token = refs[nb + na], refs[nb + na + 1], refs[-1]
        copies = plan(refs[:nb])
        assert len(copies) == count
        for i, (src, dst, dev, _) in enumerate(copies):
            _remote(src, dst, send_sems.at[i], recv_sems.at[i], dev).start()
        token[...] = jnp.zeros_like(token)

    res = pl.pallas_call(
        body, name=name, in_specs=[_HBM] * nb + [_ANY] * na,
        out_specs=(_SEM, _SEM, *[_HBM] * nb, pl.BlockSpec(memory_space=pltpu.VMEM)),
        out_shape=(pltpu.SemaphoreType.DMA((count,)), pltpu.SemaphoreType.DMA((count,)),
                   *[pltpu.HBM(b.shape, b.dtype) for b in bufs], jax.ShapeDtypeStruct((SUBLANES, LANES), F32)),
        input_output_aliases={i: 2 + i for i in range(nb)},
        compiler_params=pltpu.CompilerParams(has_side_effects=_EFFECT),
    )(*[pltpu.with_memory_space_constraint(b, pltpu.HBM) for b in bufs], *after)
    return (res[0], res[1]), list(res[2:2 + nb]), res[-1]


def _copies_wait(name, bufs, sems, plan, after=(), which=None):
    nb, na = len(bufs), len(after)

    def body(*refs):
        send_sems, recv_sems = refs[nb], refs[nb + 1]
        for i, (src, _, dev, land) in enumerate(plan(refs[:nb])):
            if which is not None and i not in which:
                continue
            cp = _remote(src, land, send_sems.at[i], recv_sems.at[i], dev)
            cp.wait_send()
            cp.wait_recv()

    res = pl.pallas_call(
        body, name=name, in_specs=[_HBM] * nb + [_SEM, _SEM] + [_ANY] * na, out_specs=[_HBM] * nb,
        out_shape=[pltpu.HBM(b.shape, b.dtype) for b in bufs],
        input_output_aliases={i: i for i in range(nb)},
        compiler_params=pltpu.CompilerParams(has_side_effects=_EFFECT),
    )(*bufs, *sems, *after)
    return list(res)


def _plan_gather_ici(fulls, which=(0, 1, 2)):
    x, y, c, chips = _place()
    copies = []
    for f in fulls:
        half = pl.ds(c * (f.shape[1] // 2), f.shape[1] // 2)
        own = f.at[2 * x + y, half]
        for chip in [chips[k] for k in which]:
            copies.append((own, own, (*chip, c), f.at[2 * chip[0] + chip[1], half]))
    return copies


def _plan_gather_d2d(fulls, which=(0, 1, 2)):
    x, y, c, chips = _place()
    copies = []
    for f in fulls:
        r2 = f.shape[1] // 2
        for chip in [chips[k] for k in which]:
            blk = 2 * chip[0] + chip[1]
            landed = f.at[blk, pl.ds(c * r2, r2)]
            copies.append((landed, landed, (x, y, 1 - c), f.at[blk, pl.ds((1 - c) * r2, r2)]))
    return copies


def _plan_relay_direct(fulls):
    (f,) = fulls
    x, y, c, chips = _place()
    half = pl.ds(c * (f.shape[1] // 2), f.shape[1] // 2)
    own = f.at[2 * x + y, half]
    return [(own, own, (*chip, c), f.at[2 * chip[0] + chip[1], half]) for chip in chips[:2]]


def _plan_relay_forward(fulls, k):
    (f,) = fulls
    x, y, c, chips = _place()
    r2 = f.shape[1] // 2
    half, other = pl.ds(c * r2, r2), pl.ds((1 - c) * r2, r2)
    quarter = pl.ds(c * r2 + k * (r2 // 2), r2 // 2)
    blk, far = 2 * chips[k][0] + chips[k][1], 2 * chips[2][0] + chips[2][1]
    passed, landed = f.at[blk, quarter], f.at[blk, half]
    return [(passed, passed, (*chips[1 - k], c), f.at[far, quarter]), (landed, landed, (x, y, 1 - c), f.at[blk, other])]


def _plan_relay_last(fulls):
    (f,) = fulls
    x, y, c, chips = _place()
    r2 = f.shape[1] // 2
    far = 2 * chips[2][0] + chips[2][1]
    landed = f.at[far, pl.ds(c * r2, r2)]
    return [(landed, landed, (x, y, 1 - c), f.at[far, pl.ds((1 - c) * r2, r2)])]


def _plan_swap_halves(refs):
    x, y, c, _ = _place()
    n = len(refs) // 2
    copies = []
    for g, land in zip(refs[:n], refs[n:]):
        r2 = g.shape[1] // 2
        copies.append((g.at[:, pl.ds((1 - c) * r2, r2), :], land, (x, y, 1 - c), land))
    return copies


def _plan_scatter_chips(refs):
    x, y, c, chips = _place()
    n = len(refs) // 2
    copies = []
    for h, land in zip(refs[:n], refs[n:]):
        for k, chip in enumerate(chips):
            copies.append((h.at[2 * chip[0] + chip[1]], land.at[k], (*chip, c), land.at[k]))
    return copies


def _plan_join_halves(totals):
    x, y, c, _ = _place()
    copies = []
    for t in totals:
        r2 = t.shape[0] // 2
        mine = t.at[pl.ds(c * r2, r2)]
        copies.append((mine, mine, (x, y, 1 - c), t.at[pl.ds((1 - c) * r2, r2)]))
    return copies


def _add_sibling_half(g, got, c_arr, *, name, tm):
    _, rows, cols = g.shape
    r2 = rows // 2
    nb = r2 // tm

    def body(c_ref, g_ref, r_ref, o_ref):
        o_ref[...] = (g_ref[...].astype(F32) + r_ref[...].astype(F32)).astype(o_ref.dtype)

    return pl.pallas_call(
        body, name=name,
        grid_spec=pltpu.PrefetchScalarGridSpec(
            num_scalar_prefetch=1, grid=(N_CHIPS, nb),
            in_specs=[pl.BlockSpec((None, tm, cols), lambda b, i, c: (b, c[0] * nb + i, 0)),
                      pl.BlockSpec((None, tm, cols), lambda b, i, c: (b, i, 0))],
            out_specs=pl.BlockSpec((None, tm, cols), lambda b, i, c: (b, i, 0))),
        out_shape=jax.ShapeDtypeStruct((N_CHIPS, r2, cols), _WIRE),
        compiler_params=_params(("arbitrary", "arbitrary")),
    )(c_arr, g, got)


def _add_chips(h, got, place_arr, *, name, tm):
    _, r2, cols = h.shape
    nb = r2 // tm

    def body(p_ref, h_ref, r_ref, o_ref):
        o_ref[...] = ((h_ref[...].astype(F32) + r_ref[0].astype(F32)) + r_ref[1].astype(F32)) + r_ref[2].astype(F32)

    return pl.pallas_call(
        body, name=name,
        grid_spec=pltpu.PrefetchScalarGridSpec(
            num_scalar_prefetch=1, grid=(nb,),
            in_specs=[pl.BlockSpec((None, tm, cols), lambda i, p: (p[0], i, 0)),
                      pl.BlockSpec((3, tm, cols), lambda i, p: (0, i, 0))],
            out_specs=pl.BlockSpec((tm, cols), lambda i, p: (p[1] * nb + i, 0))),
        out_shape=jax.ShapeDtypeStruct((2 * r2, cols), F32),
        compiler_params=_params(("arbitrary",)),
    )(place_arr, h, got)


class _ReduceScatter:
    def __init__(self, tag, names, grads):
        self.tag, self.names, self.n = tag, names, len(names)
        core = lax.axis_index("c").astype(jnp.int32)
        chip = (2 * lax.axis_index("x") + lax.axis_index("y")).astype(jnp.int32)
        self.c_arr, self.place_arr = core.reshape(1), jnp.stack([chip, core])
        self.bufs = list(grads)

    def _start(self, step, bufs, plan, count, after):
        self.plan = plan
        self.step = f"grad_{step}_{self.tag}"
        self.sems, self.bufs, token = _copies_start(self.step + "_start", bufs, plan, count, after)
        return [token]

    def _wait(self, after):
        self.bufs = _copies_wait(self.step + "_wait", self.bufs, self.sems, self.plan, after)
        return self.bufs

    def start_swap(self, after=()):
        lands = [lax.empty((N_CHIPS, g.shape[1] // 2, g.shape[2]), g.dtype) for g in self.bufs]
        return self._start("swap", self.bufs + lands, _plan_swap_halves, self.n, after)

    def start_scatter(self, after):
        bufs = self._wait(after)
        pair = [_add_sibling_half(g, r, self.c_arr, name=f"grad_add_sibling_{nm}", tm=min(256, g.shape[1] // 2))
                for nm, g, r in zip(self.names, bufs[:self.n], bufs[self.n:])]
        lands = [lax.empty((3,) + h.shape[1:], h.dtype) for h in pair]
        return self._start("scatter", pair + lands, _plan_scatter_chips, 3 * self.n, ())

    def start_join(self, after):
        bufs = self._wait(after)
        total = [_add_chips(h, r, self.place_arr, name=f"grad_add_chips_{nm}", tm=min(256, h.shape[1]))
                 for nm, h, r in zip(self.names, bufs[:self.n], bufs[self.n:])]
        return self._start("join", total, _plan_join_halves, self.n, ())

    def finish(self, after):
        return dict(zip(self.names, self._wait(after)))


def _all_gather_small(v):
    m_per, n = v.shape

    def body(x_ref, out_ref, send_sems, recv_sems, local_sem):
        x, y, c, chips = _place()
        me, sibling = (x, y, c), (x, y, 1 - c)

        def rows(px, py, pc):
            return out_ref.at[4 * px + 2 * py + pc]

        def copy(k, block, to, src=None):
            return _remote(rows(*block) if src is None else src, rows(*block), send_sems.at[k], recv_sems.at[k], to)

        mine = pltpu.make_async_copy(x_ref, rows(*me), local_sem)
        mine.start()
        first = [copy(0, me, sibling, src=x_ref)]
        first += [copy(1 + j, me, (*chip, c), src=x_ref) for j, chip in enumerate(chips)]
        for cp in first:
            cp.start()
        passed = [copy(4 + j, (*chip, c), sibling) for j, chip in enumerate(chips)]
        for j, chip in enumerate(chips):
            copy(1 + j, (*chip, c), me).wait_recv()
            passed[j].start()
        copy(0, sibling, me).wait_recv()
        for j, chip in enumerate(chips):
            copy(4 + j, (*chip, 1 - c), me).wait_recv()
        for cp in first + passed:
            cp.wait_send()
        mine.wait()

    return pl.pallas_call(
        body, name="gather_small_grads",
        out_shape=jax.ShapeDtypeStruct((8, m_per, n), v.dtype),
        in_specs=[pl.BlockSpec(memory_space=pltpu.VMEM)], out_specs=pl.BlockSpec(memory_space=pltpu.VMEM),
        scratch_shapes=[pltpu.SemaphoreType.DMA((7,)), pltpu.SemaphoreType.DMA((7,)), pltpu.SemaphoreType.DMA],
        compiler_params=pltpu.CompilerParams(vmem_limit_bytes=VMEM_LIMIT),
    )(v)


def _sum8(v, *, name):
    _, m, n = v.shape

    def body(v_ref, o_ref):
        acc = v_ref[0]
        for d in range(1, 8):
            acc = acc + v_ref[d]
        o_ref[...] = acc

    return pl.pallas_call(body, name=name, out_shape=jax.ShapeDtypeStruct((m, n), F32),
                          compiler_params=pltpu.CompilerParams(vmem_limit_bytes=VMEM_LIMIT))(v)


def _local_step(x, target, norm_w, q_norm_w, k_norm_w, sinks, a_re, a_im, log_dt, b_re, b_im, c_re, c_im, d_skip,
                b_glu, io):
    seq = x.shape[0]
    qw2 = jnp.tile(q_norm_w.reshape(1, HEAD_DIM), (1, HEADS_PER_TILE))
    kw2 = jnp.tile(k_norm_w.reshape(1, HEAD_DIM), (1, HEADS_PER_TILE))
    nw, bg = norm_w.reshape(1, D_MODEL), b_glu.reshape(1, D_MODEL)
    dsk = d_skip.reshape(1, SSM_W)

    h, rstd = _rms_fwd(x, nw, deps=io.begin())
    proj, w_in4 = io.projection(h)
    attn, lse, ya_in = _attn2_fwd(proj, qw2, kw2, sinks, deps=io.after_proj(proj))
    w_ap4 = io.weight("w_attn_proj", ya_in)
    w_glu4, w_sp4, w_out = io.weight("w_glu", ya_in), io.weight("w_ssm_proj", ya_in), io.weight("w_out", ya_in)
    y_a = _mm(ya_in, w_ap4, mode="nn", name="mm_attn_proj", tm=2048, tn=512, tk=ATTN_W, b_blocked=True,
              rows_outer=True, out_dtype=_MXU)

    flat_a = (a_re.reshape(1, N_STATES), a_im.reshape(1, N_STATES), jnp.repeat(log_dt, STATE).reshape(1, N_STATES))
    coef = _ssm_params_fwd(*flat_a)
    bre_blk, bim_blk = _block_diag_b(b_re).astype(_MXU), _block_diag_b(b_im).astype(_MXU)
    cre_blk, cim_blk = _block_diag_c(c_re).astype(_MXU), _block_diag_c(c_im).astype(_MXU)
    u_scan = _to_scan_order(proj[:, OFF_U * CW:OFF_U * CW + SSM_W])
    y_scan, yg, s_re, s_im, i_re, i_im = _ssm_fwd(u_scan, bre_blk, bim_blk, cre_blk, cim_blk, dsk, coef)
    glu, ys_in = _mm_glu_gate(yg, w_glu4, bg, proj)
    y_s = _mm(ys_in, w_sp4, mode="nn", name="mm_ssm_proj", tm=2048, tn=512, tk=SSM_W, b_blocked=True,
              rows_outer=True, out_dtype=_MXU)

    merged, dout, dout_b, sq = _mm_merge_out_loss(proj, y_a, y_s, w_out, x, target)
    loss = 0.5 * jnp.sum(sq) / D_MODEL

    d_ya, d_ys, d_proj = _mm_merge_bwd(dout_b, w_out, proj, y_a, y_s)
    g_w_out = _mm(merged, dout_b, mode="tn", name="mm_g_w_out", tm=1024, tn=D_MODEL, tk=1024, out_dtype=_WIRE)

    d_ya_in = _mm(d_ya, w_ap4, mode="nt", name="mm_d_attn_gate", tm=2048, tn=ATTN_W, tk=512, b_blocked=True)
    g_w_ap = _mm(ya_in, d_ya, mode="tn", name="mm_g_w_attn_proj", tm=ATTN_W, tn=D_MODEL, tk=2048, out_dtype=_WIRE,
                 out_blocked=True)

    g_w_sp = _mm(ys_in, d_ys, mode="tn", name="mm_g_w_ssm_proj", tm=SSM_W, tn=D_MODEL, tk=2048, out_dtype=_WIRE,
                 out_blocked=True)
    d_glu, d_proj, g_bglu = _mm_ssm_gate_bwd(d_ys, w_sp4, glu, bg, proj, d_proj)
    d_yg = _mm(d_glu, w_glu4, mode="nt", name="mm_d_gelu", tm=2048, tn=SSM_W, tk=512, b_blocked=True)
    g_w_glu = _mm(yg, d_glu, mode="tn", name="mm_g_w_glu", tm=SSM_W, tn=D_MODEL, tk=2048, out_dtype=_WIRE, out_blocked=True)
    dep = io.later_grads(dict(w_attn_proj=g_w_ap, w_glu=g_w_glu, w_ssm_proj=g_w_sp,
                              w_out=g_w_out.reshape(N_CHIPS, D_MODEL // N_CHIPS, D_MODEL)))

    d_proj, g_qw2, g_kw2, g_sk = _attn2_bwd(proj, qw2, kw2, sinks, lse, attn, d_ya_in, d_proj, deps=dep)
    dep = io.before_scan_backward([d_proj])
    (d_proj, g_bre, g_bim, g_cre, g_cim, g_dsk, g_abr, g_abi, g_cfr, g_cfi) = _ssm_bwd(
        _to_scan_order(d_yg), y_scan, u_scan, s_re, s_im, i_re, i_im, bre_blk, bim_blk, cre_blk, cim_blk, dsk, coef,
        d_proj, deps=dep)
    g_are, g_aim, g_ldt = _ssm_params_bwd(*flat_a, g_abr, g_abi, g_cfr, g_cfi)
    g_are, g_aim = g_are.reshape(N_GROUPS, STATE), g_aim.reshape(N_GROUPS, STATE)
    g_ldt = g_ldt.reshape(N_GROUPS, STATE).sum(axis=1)
    dep = io.before_input_projection_grad([d_proj]) + io.small_grads(dict(
        q_norm_w=g_qw2[0, :HEAD_DIM] + g_qw2[0, HEAD_DIM:], k_norm_w=g_kw2[0, :HEAD_DIM] + g_kw2[0, HEAD_DIM:],
        sinks=g_sk.reshape(N_Q_HEADS), A_re=g_are, A_im=g_aim, log_dt=g_ldt,
        B_re=_diag_of_b(g_bre), B_im=_diag_of_b(g_bim), C_re=_diag_of_c(g_cre), C_im=_diag_of_c(g_cim),
        D_skip=g_dsk.reshape(N_GROUPS, GROUP), b_glu=g_bglu.reshape(D_MODEL)))
    g_w_in = _mm(h, d_proj, mode="tn", name="mm_g_w_in", tm=1024, tn=IN_W // 4, tk=1024, out_dtype=_WIRE,
                 out_blocked=True, deps=dep)
    dep = io.input_projection_grad(g_w_in)
    d_h = _mm(d_proj, w_in4, mode="nt", name="mm_d_h", tm=1024, tn=D_MODEL, tk=IN_W // 4, b_blocked=True, deps=dep)
    grad_x, g_nw = _rms_bwd(d_h, x, rstd, nw, dout)
    return loss, grad_x, g_nw.reshape(D_MODEL)


_SMALL = ["norm_w", "q_norm_w", "k_norm_w", "sinks", "A_re", "A_im", "log_dt", "B_re", "B_im", "C_re", "C_im",
          "D_skip", "b_glu"]
_BIG = ["w_in", "w_attn_proj", "w_glu", "w_ssm_proj", "w_out"]
_LATER = _BIG[1:]
_RELATIONS = ("flip_x", "flip_y", "flip_xy")
_ORDER = ["norm_w", "w_in", "q_norm_w", "k_norm_w", "sinks", "w_attn_proj", "A_re", "A_im", "log_dt", "B_re", "B_im",
          "C_re", "C_im", "D_skip", "w_glu", "b_glu", "w_ssm_proj", "w_out"]
_PACK_W = 1024


def _packed_rows(size):
    unit = SUBLANES * _PACK_W
    return -(-size // unit) * SUBLANES


def _pack_small(d, names):
    parts = []
    for n in names:
        flat = d[n].reshape(-1).astype(F32)
        rows = _packed_rows(flat.shape[0])
        parts.append(jnp.pad(flat, (0, rows * _PACK_W - flat.shape[0])).reshape(rows, _PACK_W))
    return jnp.concatenate(parts, axis=0)


def _unpack_small(packed, like, names):
    out, pos = {}, 0
    for n in names:
        rows = _packed_rows(like[n].size)
        out[n] = packed[pos:pos + rows].reshape(-1)[:like[n].size].reshape(like[n].shape)
        pos += rows
    return out


def _place_block(v, index_arr, *, name):
    rows, cols = v.shape

    def body(i_ref, v_ref, o_ref):
        o_ref[...] = v_ref[...]

    return pl.pallas_call(
        body, name=name,
        grid_spec=pltpu.PrefetchScalarGridSpec(
            num_scalar_prefetch=1, grid=(1,),
            in_specs=[pl.BlockSpec((rows, cols), lambda i, d: (0, 0))],
            out_specs=pl.BlockSpec((None, rows, cols), lambda i, d: (d[0], 0, 0))),
        out_shape=jax.ShapeDtypeStruct((8, rows, cols), v.dtype),
        compiler_params=_params(("arbitrary",)),
    )(index_arr, v)


def _plan_all_to_all(refs):
    (land,) = refs
    x, y, c, _ = _place()
    own = land.at[4 * x + 2 * y + c]
    copies = []
    for fx, fy, fc in [(0, 0, 1), (0, 1, 0), (0, 1, 1), (1, 0, 0), (1, 0, 1), (1, 1, 0), (1, 1, 1)]:
        px, py, pc = (1 - x) if fx else x, (1 - y) if fy else y, (1 - c) if fc else c
        copies.append((own, own, (px, py, pc), land.at[4 * px + 2 * py + pc]))
    return copies


def _adamw_whole(w, g, m, v, *, name):
    def body(w_ref, g_ref, m_ref, v_ref, d_ref, nm_ref, nv_ref):
        d_ref[...], nm_ref[...], nv_ref[...] = _adamw_math(w_ref[...], g_ref[...], m_ref[...], v_ref[...])

    return pl.pallas_call(body, name=name, out_shape=[jax.ShapeDtypeStruct(w.shape, F32)] * 3)(w, g, m, v)


class _Exchanges:
    def __init__(self, w, m, v):
        self.w, self.m, self.v = w, m, v
        self.grads, self.delta, self.new_m, self.new_v = {}, {}, {}, {}

    def _adamw(self, names, deps):
        for n in names:
            self.delta[n], self.new_m[n], self.new_v[n] = _adamw(
                self.w[n], self.grads[n], self.m[n], self.v[n], name=f"adamw_{n}", tm=128, deps=deps)

    def begin(self):
        chip = (2 * lax.axis_index("x") + lax.axis_index("y")).astype(jnp.int32).reshape(1)
        w_in = _place_shard(self.w["w_in"], chip, name="place_w_in")
        self.w_in_sems, self.w_in_buf, token = _copies_start("gather_w_in_direct_start", [w_in], _plan_relay_direct, 2)
        self.later_full = [_place_shard(self.w[n], chip, name=f"place_{n}", deps=[token]) for n in _LATER]
        return self.later_full

    def projection(self, h):
        x, y = lax.axis_index("x"), lax.axis_index("y")
        blks = [jnp.asarray(b, jnp.int32).reshape(1)
                for b in (2 * x + y, 2 * (1 - x) + y, 2 * x + (1 - y), 2 * (1 - x) + (1 - y))]
        bufs = self.w_in_buf
        proj = _mm_chip_block(h, bufs[0], blks[0], None, name="mm_proj_own", out_dtype=_MXU)
        relay, token = [], proj
        for k, tag in enumerate(_RELATIONS[:2]):
            bufs = _copies_wait(f"gather_w_in_direct_{tag}_wait", bufs, self.w_in_sems, _plan_relay_direct, [token],
                                which=(k,))
            plan = functools.partial(_plan_relay_forward, k=k)
            sems, bufs, token = _copies_start(f"gather_w_in_relay_{tag}_start", bufs, plan, 2)
            relay.append((sems, plan))
        self.rest = _copies_start("gather_ici_rest_start", self.later_full, _plan_gather_ici, 3 * len(_LATER),
                                  after=[token])
        token = self.rest[2]
        for k, tag in enumerate(_RELATIONS[:2]):
            bufs = _copies_wait(f"gather_w_in_hand_{tag}_wait", bufs, relay[k][0], relay[k][1], [token], which=(1,))
            token = proj = _mm_chip_block(h, bufs[0], blks[1 + k], proj, name=f"mm_proj_{tag}", out_dtype=_MXU)
        for k, tag in enumerate(_RELATIONS[:2]):
            bufs = _copies_wait(f"gather_w_in_relay_{tag}_wait", bufs, relay[k][0], relay[k][1], [token], which=(0,))
        sems, bufs, token = _copies_start("gather_w_in_last_start", bufs, _plan_relay_last, 1)
        bufs = _copies_wait("gather_w_in_last_wait", bufs, sems, _plan_relay_last, [token])
        proj = _mm_chip_block(h, bufs[0], blks[3], proj, name="mm_proj_flip_xy", out_dtype=_MXU)
        return proj, bufs[0]

    def weight(self, name, after):
        if self.rest is not None:
            sems, bufs = self.rest
            later = dict(zip(_LATER, _copies_wait("gather_d2d_rest_wait", bufs, sems, _plan_gather_d2d, [after])))
            later["w_out"] = later["w_out"].reshape(D_MODEL, D_MODEL)
            self.later, self.rest = later, None
        return self.later[name]

    def after_proj(self, proj):
        sems, bufs, _ = self.rest
        bufs = _copies_wait("gather_ici_rest_wait", bufs, sems, _plan_gather_ici, [proj])
        sems, bufs, token = _copies_start("gather_d2d_rest_start", bufs, _plan_gather_d2d, 3 * len(_LATER))
        self.rest = (sems, bufs)
        return [token]

    def later_grads(self, grads):
        self.rs_later = _ReduceScatter("later", _LATER, [grads[n] for n in _LATER])
        return self.rs_later.start_swap()

    def before_scan_backward(self, after):
        return self.rs_later.start_scatter(after)

    def before_input_projection_grad(self, after):
        return self.rs_later.start_join(after)

    def input_projection_grad(self, g_w_in):
        self.grads.update(self.rs_later.finish([g_w_in]))
        self.rs_in = _ReduceScatter("w_in", ["w_in"], [g_w_in])
        self._adamw(_LATER, self.rs_in.start_swap())
        return self.rs_in.start_scatter([self.delta[n] for n in _LATER])

    def _adamw_small(self, names):
        for n in names:
            self.delta[n], self.new_m[n], self.new_v[n] = _adamw_whole(
                self.w[n], self.grads[n], self.m[n], self.v[n], name=f"adamw_{n}")

    def small_grads(self, grads):
        me = (4 * lax.axis_index("x") + 2 * lax.axis_index("y") + lax.axis_index("c")).astype(jnp.int32).reshape(1)
        land = _place_block(_pack_small(grads, _SMALL[1:]), me, name="place_small_grads")
        self.small = _copies_start("gather_small_start", [land], _plan_all_to_all, 7)
        return [self.small[2]]

    def finish(self, g_norm_w, loss, after):
        join = self.rs_in.start_join(after)
        sems, bufs, _ = self.small
        (land,) = _copies_wait("gather_small_wait", bufs, sems, _plan_all_to_all, join)
        self.grads.update(_unpack_small(_sum8(land, name="sum_small_grads"), self.w, _SMALL[1:]))
        self._adamw_small(_SMALL[1:])
        rows = _packed_rows(g_norm_w.size)
        late = jnp.concatenate([_pack_small(dict(norm_w=g_norm_w), _SMALL[:1]),
                                jnp.pad(loss.reshape(1, 1), ((0, SUBLANES - 1), (0, _PACK_W - 1)))], axis=0)
        late = _sum8(_all_gather_small(late), name="sum_norm_w_grad_and_loss")
        self.grads.update(_unpack_small(late[:rows], self.w, _SMALL[:1]))
        self._adamw_small(_SMALL[:1])
        self.grads.update(self.rs_in.finish([self.delta[_SMALL[0]]]))
        self._adamw(["w_in"], ())
        return late[rows, 0]


def kernel(x, norm_w, w_in, q_norm_w, k_norm_w, sinks, w_attn_proj, A_re, A_im, log_dt, B_re, B_im, C_re, C_im, D_skip, w_glu, b_glu, w_ssm_proj, w_out, loss_target, m_norm_w, m_w_in, m_q_norm_w, m_k_norm_w, m_sinks, m_w_attn_proj, m_A_re, m_A_im, m_log_dt, m_B_re, m_B_im, m_C_re, m_C_im, m_D_skip, m_w_glu, m_b_glu, m_w_ssm_proj, m_w_out, v_norm_w, v_w_in, v_q_norm_w, v_k_norm_w, v_sinks, v_w_attn_proj, v_A_re, v_A_im, v_log_dt, v_B_re, v_B_im, v_C_re, v_C_im, v_D_skip, v_w_glu, v_b_glu, v_w_ssm_proj, v_w_out):
    w = dict(norm_w=norm_w, w_in=w_in, q_norm_w=q_norm_w, k_norm_w=k_norm_w, sinks=sinks, w_attn_proj=w_attn_proj,
             A_re=A_re, A_im=A_im, log_dt=log_dt, B_re=B_re, B_im=B_im, C_re=C_re, C_im=C_im, D_skip=D_skip,
             w_glu=w_glu, b_glu=b_glu, w_ssm_proj=w_ssm_proj, w_out=w_out)
    m = dict(norm_w=m_norm_w, w_in=m_w_in, q_norm_w=m_q_norm_w, k_norm_w=m_k_norm_w, sinks=m_sinks,
             w_attn_proj=m_w_attn_proj, A_re=m_A_re, A_im=m_A_im, log_dt=m_log_dt, B_re=m_B_re, B_im=m_B_im,
             C_re=m_C_re, C_im=m_C_im, D_skip=m_D_skip, w_glu=m_w_glu, b_glu=m_b_glu, w_ssm_proj=m_w_ssm_proj,
             w_out=m_w_out)
    v = dict(norm_w=v_norm_w, w_in=v_w_in, q_norm_w=v_q_norm_w, k_norm_w=v_k_norm_w, sinks=v_sinks,
             w_attn_proj=v_w_attn_proj, A_re=v_A_re, A_im=v_A_im, log_dt=v_log_dt, B_re=v_B_re, B_im=v_B_im,
             C_re=v_C_re, C_im=v_C_im, D_skip=v_D_skip, w_glu=v_w_glu, b_glu=v_b_glu, w_ssm_proj=v_w_ssm_proj,
             w_out=v_w_out)

    io = _Exchanges(w, m, v)
    loss, grad_x, g_norm_w = _local_step(x[0], loss_target[0], norm_w, q_norm_w, k_norm_w, sinks, A_re, A_im, log_dt,
                                         B_re, B_im, C_re, C_im, D_skip, b_glu, io)
    loss = io.finish(g_norm_w, loss, [grad_x])
    grads, delta, new_m, new_v = io.grads, io.delta, io.new_m, io.new_v

    return (loss, grad_x[None], *[grads[n] for n in _ORDER], *[delta[n] for n in _ORDER],
            *[new_m[n] for n in _ORDER], *[new_v[n] for n in _ORDER])
```

```python
import functools
import math

import jax
import jax.numpy as jnp
from jax import lax
from jax.experimental import pallas as pl
from jax.experimental.pallas import tpu as pltpu

F32 = jnp.float32
_MXU = jnp.bfloat16
_WIRE = jnp.bfloat16

LANES = 128
SUBLANES = 8
VMEM_LIMIT = 56 * 1024 * 1024

D_MODEL = 2048
HEAD_DIM = 64
N_Q_HEADS = 16
N_KV_HEADS = 4
Q_PER_KV = 4
ATTN_W = 1024
KV_W = 256
WINDOW = 128
SSM_W = 1024
GROUP = 16
N_GROUPS = 64
STATE = 64
N_STATES = N_GROUPS * STATE
IN_W = 8704
NORM_EPS = 1e-6
N_CHIPS = 4
CW = 512
OFF_AGATE, OFF_U, OFF_Z, OFF_GA, OFF_GS = 3, 5, 7, 9, 13

SSM_T = 256
SSM_L = SSM_T // SUBLANES
SSM_JB = 8
SSM_SB = N_STATES // SSM_JB

ADAM_LR, ADAM_B1, ADAM_B2, ADAM_EPS, ADAM_WD, ADAM_STEP = 0.001, 0.9, 0.999, 1e-08, 0.01, 10

MESH = pl.DeviceIdType.MESH
_ANY = pl.BlockSpec(memory_space=pl.ANY)


def _params(sem=None):
    return pltpu.CompilerParams(dimension_semantics=sem, vmem_limit_bytes=VMEM_LIMIT)


def _mm(a, b, *, mode, name, tm, tn, tk, out_dtype=F32, b_blocked=False, out_blocked=False, rows_outer=False,
        deps=()):
    nd = len(deps)
    if mode == "tn":
        K, M = a.shape
    else:
        M, K = a.shape
    if mode == "nn":
        N = b.shape[0] * b.shape[2] if b_blocked else b.shape[1]
    elif mode == "nt":
        N = b.shape[1] if b_blocked else b.shape[0]
    else:
        N = b.shape[1]
    tm, tn, tk = min(tm, M), min(tn, N), min(tk, K)
    nj, ni, nk = N // tn, M // tm, K // tk
    assert nj * tn == N and ni * tm == M and nk * tk == K, (name, M, N, K)
    dims = {"nn": (((1,), (0,)), ((), ())), "nt": (((1,), (1,)), ((), ())), "tn": (((0,), (0,)), ((), ()))}[mode]

    if mode == "tn":
        a_spec = pl.BlockSpec((tk, tm), lambda j, i, k: (k, i))
    else:
        a_spec = pl.BlockSpec((tm, tk), lambda j, i, k: (i, k))
    if mode == "nn":
        if b_blocked:
            assert b.shape[0] == nj and b.shape[2] == tn
            b_spec = pl.BlockSpec((None, tk, tn), lambda j, i, k: (j, k, 0))
        else:
            b_spec = pl.BlockSpec((tk, tn), lambda j, i, k: (k, j))
    elif mode == "nt":
        if b_blocked:
            assert b.shape[0] == nk and b.shape[2] == tk
            b_spec = pl.BlockSpec((None, tn, tk), lambda j, i, k: (k, j, 0))
        else:
            b_spec = pl.BlockSpec((tn, tk), lambda j, i, k: (j, k))
    else:
        b_spec = pl.BlockSpec((tk, tn), lambda j, i, k: (k, j))
    whole_out = out_blocked and nj == 1
    if whole_out:
        assert ni == 1
        o_spec = pl.BlockSpec((N_CHIPS, tm, tn // N_CHIPS), lambda j, i, k: (0, 0, 0))
        o_shape = jax.ShapeDtypeStruct((N_CHIPS, M, tn // N_CHIPS), out_dtype)
    elif out_blocked:
        assert nj == N_CHIPS
        o_spec = pl.BlockSpec((None, tm, tn), lambda j, i, k: (j, i, 0))
        o_shape = jax.ShapeDtypeStruct((nj, M, tn), out_dtype)
    else:
        o_spec = pl.BlockSpec((tm, tn), lambda j, i, k: (i, j))
        o_shape = jax.ShapeDtypeStruct((M, N), out_dtype)
    use_acc = nk > 1 and (out_dtype != F32 or whole_out)

    def body(a_ref, b_ref, *rest):
        o_ref, scratch = rest[nd], rest[nd + 1:]

        def product():
            return lax.dot_general(a_ref[...].astype(_MXU), b_ref[...].astype(_MXU), dims, preferred_element_type=F32)

        def write(result):
            if whole_out:
                w = tn // N_CHIPS
                for c in range(N_CHIPS):
                    o_ref[c] = result[:, c * w:(c + 1) * w].astype(o_ref.dtype)
            else:
                o_ref[...] = result.astype(o_ref.dtype)

        if nk == 1:
            write(product())
            return
        k = pl.program_id(2)
        acc = scratch[0] if use_acc else o_ref

        @pl.when(k == 0)
        def _():
            acc[...] = jnp.zeros_like(acc)

        acc[...] += product()

        if use_acc:
            @pl.when(k == nk - 1)
            def _():
                write(acc[...])

    specs = [a_spec, b_spec, o_spec]
    grid = (nj, ni, nk)
    if rows_outer:
        specs = [pl.BlockSpec(s.block_shape, lambda i, j, k, f=s.index_map: f(j, i, k)) for s in specs]
        grid = (ni, nj, nk)
    return pl.pallas_call(
        body, name=name, grid=grid, in_specs=specs[:2] + [_ANY] * nd, out_specs=specs[2],
        out_shape=o_shape, scratch_shapes=[pltpu.VMEM((tm, tn), F32)] if use_acc else [],
        compiler_params=_params(("parallel", "parallel", "arbitrary")),
    )(a, b, *deps)


def _mm_chip_block(a, b4, blk, prev, *, name, tm=512, out_dtype=F32, deps=()):
    M, K = a.shape
    nchip, _, C = b4.shape
    tm = min(tm, M)
    extra = ([] if prev is None else [prev]) + list(deps)

    def body(blk_ref, a_ref, b_ref, *rest):
        rest[-1][...] = jnp.dot(a_ref[...].astype(_MXU), b_ref[...].astype(_MXU),
                                preferred_element_type=F32).astype(rest[-1].dtype)

    return pl.pallas_call(
        body, name=name,
        grid_spec=pltpu.PrefetchScalarGridSpec(
            num_scalar_prefetch=1, grid=(M // tm,),
            in_specs=[pl.BlockSpec((tm, K), lambda i, c: (i, 0)), pl.BlockSpec((None, K, C), lambda i, c: (c[0], 0, 0))]
            + [_ANY] * len(extra),
            out_specs=pl.BlockSpec((tm, C), lambda i, c: (i, c[0]))),
        out_shape=jax.ShapeDtypeStruct((M, nchip * C), out_dtype),
        input_output_aliases={} if prev is None else {3: 0},
        compiler_params=_params(("arbitrary",)),
    )(blk, a, b4, *extra)


def _mm_merge_out_loss(proj, y_a, y_s, w_out, x, target, *, tm=256):
    rows, d = x.shape
    ncol = d // CW

    def body(*refs):
        ga_refs, gs_refs = refs[:ncol], refs[ncol:2 * ncol]
        ya_ref, ys_ref, w_ref, x_ref, t_ref, m_ref, d_ref, db_ref, sq_ref = refs[2 * ncol:]
        for j in range(ncol):
            cols = slice(j * CW, (j + 1) * CW)
            m_ref[:, cols] = (_sigmoid(ga_refs[j][...].astype(F32)) * ya_ref[:, cols].astype(F32)
                              + _sigmoid(gs_refs[j][...].astype(F32)) * ys_ref[:, cols].astype(F32)).astype(m_ref.dtype)
        mo = jnp.dot(m_ref[...], w_ref[...].astype(_MXU), preferred_element_type=F32)
        err = (x_ref[...] + mo) - t_ref[...]
        dout = err * (1.0 / d)
        d_ref[...] = dout
        db_ref[...] = dout.astype(db_ref.dtype)
        part = _colsum(err * err)
        i = pl.program_id(0)

        @pl.when(i == 0)
        def _():
            sq_ref[...] = part

        @pl.when(i > 0)
        def _():
            sq_ref[...] += part

    tile = pl.BlockSpec((tm, d), lambda i: (i, 0))
    gate = [pl.BlockSpec((tm, CW), lambda i, c=off + j: (i, c)) for off in (OFF_GA, OFF_GS) for j in range(ncol)]
    return pl.pallas_call(
        body, name="mm_merge_out_loss", grid=(rows // tm,),
        in_specs=gate + [tile, tile, pl.BlockSpec((d, d), lambda i: (0, 0), pipeline_mode=pl.Buffered(1)), tile, tile],
        out_specs=[tile, tile, tile, pl.BlockSpec((1, d), lambda i: (0, 0))],
        out_shape=[jax.ShapeDtypeStruct((rows, d), _MXU), jax.ShapeDtypeStruct((rows, d), F32),
                   jax.ShapeDtypeStruct((rows, d), _MXU), jax.ShapeDtypeStruct((1, d), F32)],
        compiler_params=_params(("arbitrary",)),
    )(*([proj] * (2 * ncol)), y_a, y_s, w_out, x, target)


def _mm_merge_bwd(dout_b, w_out, proj, y_a, y_s, *, tm=512):
    rows, d = y_a.shape
    ncol = d // CW

    def body(do_ref, w_ref, *refs):
        ga_refs, gs_refs = refs[:ncol], refs[ncol:2 * ncol]
        ya_ref, ys_ref, dya_ref, dys_ref, dg_ref = refs[2 * ncol:]
        dm = lax.dot_general(do_ref[...].astype(_MXU), w_ref[...].astype(_MXU), _NT, preferred_element_type=F32)
        for j in range(ncol):
            cols = slice(j * CW, (j + 1) * CW)
            dmj = dm[:, cols]
            sa, ss = _sigmoid(ga_refs[j][...].astype(F32)), _sigmoid(gs_refs[j][...].astype(F32))
            dya_ref[:, cols] = (sa * dmj).astype(dya_ref.dtype)
            dys_ref[:, cols] = (ss * dmj).astype(dys_ref.dtype)
            dg_ref[:, cols] = (dmj * ya_ref[:, cols].astype(F32) * sa * (1.0 - sa)).astype(dg_ref.dtype)
            dg_ref[:, d + j * CW:d + (j + 1) * CW] = (dmj * ys_ref[:, cols].astype(F32) * ss
                                                      * (1.0 - ss)).astype(dg_ref.dtype)

    tile = pl.BlockSpec((tm, d), lambda i: (i, 0))
    gate = [pl.BlockSpec((tm, CW), lambda i, c=off + j: (i, c)) for off in (OFF_GA, OFF_GS) for j in range(ncol)]
    both = pl.BlockSpec((pl.Element(tm), pl.Element(2 * d)), lambda i: (i * tm, OFF_GA * CW))
    return pl.pallas_call(
        body, name="mm_merge_bwd", grid=(rows // tm,),
        in_specs=[tile, pl.BlockSpec((d, d), lambda i: (0, 0), pipeline_mode=pl.Buffered(1))] + gate + [tile, tile],
        out_specs=[tile, tile, both],
        out_shape=[jax.ShapeDtypeStruct((rows, d), _MXU)] * 2 + [jax.ShapeDtypeStruct((rows, IN_W), _MXU)],
        compiler_params=_params(("arbitrary",)),
    )(dout_b, w_out, *([proj] * (2 * ncol)), y_a, y_s)


def _mm_glu_gate(yg, w_glu4, b_glu, proj, *, tm=1024):
    rows, k = yg.shape
    nj, _, tn = w_glu4.shape
    w = nj * tn // 2
    tm = min(tm, rows)

    def body(a_ref, w_ref, ba_ref, bb_ref, z0_ref, z1_ref, glu_ref, ys_ref):
        j = pl.program_id(1)
        for c in range(nj):
            @pl.when(j == c)
            def _(c=c):
                glu_ref[:, c * tn:(c + 1) * tn] = jnp.dot(a_ref[...].astype(_MXU), w_ref[...].astype(_MXU),
                                                          preferred_element_type=F32).astype(glu_ref.dtype)

        @pl.when(j == nj - 1)
        def _():
            z = jnp.concatenate([z0_ref[...], z1_ref[...]], axis=1).astype(F32)
            ys_ref[...] = ((glu_ref[:, :w].astype(F32) + ba_ref[...]) * _sigmoid(glu_ref[:, w:].astype(F32) + bb_ref[...])
                           * (z * _sigmoid(z))).astype(ys_ref.dtype)

    bias = lambda c: pl.BlockSpec((1, w), lambda i, j: (0, c))
    zcol = lambda c: pl.BlockSpec((tm, CW), lambda i, j: (i, OFF_Z + c))
    return pl.pallas_call(
        body, name="mm_glu_gate", grid=(rows // tm, nj),
        in_specs=[pl.BlockSpec((tm, k), lambda i, j: (i, 0)), pl.BlockSpec((None, k, tn), lambda i, j: (j, 0, 0)),
                  bias(0), bias(1), zcol(0), zcol(1)],
        out_specs=[pl.BlockSpec((tm, nj * tn), lambda i, j: (i, 0)), pl.BlockSpec((tm, w), lambda i, j: (i, 0))],
        out_shape=[jax.ShapeDtypeStruct((rows, nj * tn), _MXU), jax.ShapeDtypeStruct((rows, w), _MXU)],
        compiler_params=_params(("arbitrary", "arbitrary")),
    )(yg, w_glu4, b_glu, b_glu, proj, proj)


def _mm_ssm_gate_bwd(d_ys, w_sp4, glu, b_glu, proj, d_proj, *, tm=1024):
    rows, w = glu.shape[0], glu.shape[1] // 2
    nk, tk = w_sp4.shape[0], w_sp4.shape[2]
    tm = min(tm, rows)

    def body(dy_ref, w_ref, ga_ref, gb_ref, ba_ref, bb_ref, z0_ref, z1_ref, buf_ref, dg_ref, dz_ref, db_ref, acc):
        i, k = pl.program_id(0), pl.program_id(1)

        @pl.when(k == 0)
        def _():
            acc[...] = jnp.zeros_like(acc)

        acc[...] += lax.dot_general(dy_ref[...].astype(_MXU), w_ref[...].astype(_MXU), _NT, preferred_element_type=F32)

        @pl.when(k == nk - 1)
        def _():
            dv = acc[...]
            a, sb = ga_ref[...].astype(F32) + ba_ref[...], _sigmoid(gb_ref[...].astype(F32) + bb_ref[...])
            f, df = _silu_and_grad(jnp.concatenate([z0_ref[...], z1_ref[...]], axis=1).astype(F32))
            dga = dv * sb * f
            dgb = dv * a * f * sb * (1.0 - sb)
            dg_ref[:, :w] = dga.astype(dg_ref.dtype)
            dg_ref[:, w:] = dgb.astype(dg_ref.dtype)
            dz_ref[...] = (dv * a * sb * df).astype(dz_ref.dtype)
            part = jnp.concatenate([_colsum(dga), _colsum(dgb)], axis=1)

            @pl.when(i == 0)
            def _():
                db_ref[...] = part

            @pl.when(i > 0)
            def _():
                db_ref[...] += part

    half = lambda c: pl.BlockSpec((tm, w), lambda i, k: (i, c))
    bias = lambda c: pl.BlockSpec((1, w), lambda i, k: (0, c))
    zcol = lambda c: pl.BlockSpec((tm, CW), lambda i, k: (i, OFF_Z + c))
    return pl.pallas_call(
        body, name="mm_ssm_gate_bwd", grid=(rows // tm, nk),
        in_specs=[pl.BlockSpec((tm, tk), lambda i, k: (i, k)), pl.BlockSpec((None, w, tk), lambda i, k: (k, 0, 0)),
                  half(0), half(1), bias(0), bias(1), zcol(0), zcol(1), _ANY],
        out_specs=[pl.BlockSpec((tm, 2 * w), lambda i, k: (i, 0)),
                   pl.BlockSpec((pl.Element(tm), pl.Element(w)), lambda i, k: (i * tm, OFF_Z * CW)),
                   pl.BlockSpec((1, 2 * w), lambda i, k: (0, 0))],
        out_shape=[jax.ShapeDtypeStruct((rows, 2 * w), _MXU), jax.ShapeDtypeStruct(d_proj.shape, d_proj.dtype),
                   jax.ShapeDtypeStruct((1, 2 * w), F32)],
        input_output_aliases={8: 1},
        scratch_shapes=[pltpu.VMEM((tm, w), F32)],
        compiler_params=_params(("arbitrary", "arbitrary")),
    )(d_ys, w_sp4, glu, glu, b_glu, b_glu, proj, proj, d_proj)


def _colsum(v):
    return jnp.sum(v, axis=0, keepdims=True)


def _sigmoid(v):
    return jax.nn.sigmoid(v)


def _silu_and_grad(v):
    s = _sigmoid(v)
    return v * s, s * (1.0 + v * (1.0 - s))


def _rms_fwd(x, w, *, tm=512, deps=()):
    rows, d = x.shape
    nd = len(deps)

    def body(x_ref, w_ref, *rest):
        h_ref, r_ref = rest[nd:]
        xv = x_ref[...]
        r = lax.rsqrt(jnp.mean(xv * xv, axis=-1, keepdims=True) + NORM_EPS)
        h_ref[...] = (xv * r * w_ref[...]).astype(h_ref.dtype)
        r_ref[...] = r

    return pl.pallas_call(
        body, name="rms_fwd", grid=(rows // tm,),
        in_specs=[pl.BlockSpec((tm, d), lambda i: (i, 0)), pl.BlockSpec((1, d), lambda i: (0, 0))] + [_ANY] * nd,
        out_specs=[pl.BlockSpec((tm, d), lambda i: (i, 0)), pl.BlockSpec((tm, 1), lambda i: (i, 0))],
        out_shape=[jax.ShapeDtypeStruct((rows, d), _MXU), jax.ShapeDtypeStruct((rows, 1), F32)],
        compiler_params=_params(("arbitrary",)),
    )(x, w, *deps)


def _rms_bwd(dh, x, rstd, w, dout, *, tm=256):
    rows, d = x.shape

    def body(dh_ref, x_ref, r_ref, w_ref, do_ref, gx_ref, gw_ref):
        dhv, xv, r, wv = dh_ref[...], x_ref[...], r_ref[...], w_ref[...]
        xr = xv * r
        t = jnp.mean(dhv * wv * xr, axis=-1, keepdims=True)
        gx_ref[...] = do_ref[...] + r * (wv * dhv - xr * t)
        part = _colsum(dhv * xr)
        i = pl.program_id(0)

        @pl.when(i == 0)
        def _():
            gw_ref[...] = part

        @pl.when(i > 0)
        def _():
            gw_ref[...] += part

    return pl.pallas_call(
        body, name="rms_bwd", grid=(rows // tm,),
        in_specs=[pl.BlockSpec((tm, d), lambda i: (i, 0)), pl.BlockSpec((tm, d), lambda i: (i, 0)),
                  pl.BlockSpec((tm, 1), lambda i: (i, 0)), pl.BlockSpec((1, d), lambda i: (0, 0)),
                  pl.BlockSpec((tm, d), lambda i: (i, 0))],
        out_specs=[pl.BlockSpec((tm, d), lambda i: (i, 0)), pl.BlockSpec((1, d), lambda i: (0, 0))],
        out_shape=[jax.ShapeDtypeStruct((rows, d), F32), jax.ShapeDtypeStruct((1, d), F32)],
        compiler_params=_params(("arbitrary",)),
    )(dh, x, rstd, w, dout)


_NT = (((1,), (1,)), ((), ()))
_TN = (((0,), (0,)), ((), ()))


QKV_W = ATTN_W + 2 * KV_W
HEADS_PER_TILE = LANES // HEAD_DIM


def _low_half(rows):
    return lax.broadcasted_iota(jnp.int32, (rows, LANES), 1) < HEAD_DIM


def _pair_mean(t, low):
    m_lo = jnp.sum(jnp.where(low, t, 0.0), axis=-1, keepdims=True)
    m_hi = jnp.sum(jnp.where(low, 0.0, t), axis=-1, keepdims=True)
    return jnp.where(low, m_lo, m_hi) * (1.0 / HEAD_DIM)


def _pair_rstd(t, low):
    return lax.rsqrt(_pair_mean(t * t, low) + NORM_EPS)


def _dup_half(t, hi, low):
    swapped = pltpu.roll(t, HEAD_DIM, 1)
    return jnp.where(low, swapped, t) if hi else jnp.where(low, t, swapped)


def _fold_halves(t):
    return t + pltpu.roll(t, HEAD_DIM, 1)


def _split_heads(t, low):
    return [jnp.where(low, t, 0.0), jnp.where(low, 0.0, t)]


def _stacked_band_mask(n):
    rows = Q_PER_KV * WINDOW
    qi = lax.broadcasted_iota(jnp.int32, (rows, 2 * WINDOW), 0) % WINDOW + WINDOW
    kj = lax.broadcasted_iota(jnp.int32, (rows, 2 * WINDOW), 1)
    diff = qi - kj
    first_key = jnp.where(n > 0, 0, WINDOW)
    return (diff >= 0) & (diff < WINDOW) & (kj >= first_key)


def _stacked_sinks(sink_ref, g):
    blk = lax.broadcasted_iota(jnp.int32, (Q_PER_KV * WINDOW, 1), 0) // WINDOW
    col = jnp.full((Q_PER_KV * WINDOW, 1), sink_ref[Q_PER_KV * g], F32)
    for r in range(1, Q_PER_KV):
        col = jnp.where(blk == r, sink_ref[Q_PER_KV * g + r], col)
    return col


def _attn_in_specs(nblk, rev):
    def cur(n):
        return (nblk - 1 - n) if rev else n

    q_spec = pl.BlockSpec((WINDOW, ATTN_W), lambda n: (cur(n), 0))
    kvc_spec = pl.BlockSpec((WINDOW, 2 * KV_W), lambda n: (cur(n), ATTN_W // (2 * KV_W)))
    kvp_spec = pl.BlockSpec((WINDOW, 2 * KV_W), lambda n: (jnp.maximum(cur(n) - 1, 0), ATTN_W // (2 * KV_W)))
    w_spec = pl.BlockSpec((1, LANES), lambda n: (0, 0))
    l_spec = pl.BlockSpec((WINDOW, N_Q_HEADS), lambda n: (cur(n), 0))
    gate_specs = [pl.BlockSpec((WINDOW, CW), lambda n, col=OFF_AGATE + j: (cur(n), col)) for j in range(ATTN_W // CW)]
    return q_spec, kvc_spec, kvp_spec, w_spec, l_spec, gate_specs


def _attn2_fwd(proj, qw2, kw2, sinks, deps=()):
    seq = proj.shape[0]
    nblk = seq // WINDOW
    scale = 1.0 / math.sqrt(HEAD_DIM)
    q_spec, kvc_spec, kvp_spec, w_spec, l_spec, gate_specs = _attn_in_specs(nblk, False)
    nd, ng = len(deps), len(gate_specs)

    def body(sink_ref, q_ref, kvc_ref, kvp_ref, qw_ref, kw_ref, *rest):
        gate_refs = rest[:ng]
        o_ref, lse_ref, ya_ref = rest[ng + nd:]
        n = pl.program_id(0)
        low, low2 = _low_half(WINDOW), _low_half(2 * WINDOW)
        valid = _stacked_band_mask(n)
        head_lane = lax.broadcasted_iota(jnp.int32, (WINDOW, N_Q_HEADS), 1)
        kv = jnp.concatenate([kvp_ref[...], kvc_ref[...]], axis=0).astype(F32)
        qwv, kwv = qw_ref[...], kw_ref[...]
        lse_blk = jnp.zeros((WINDOW, N_Q_HEADS), F32)
        for t in range(N_KV_HEADS // HEADS_PER_TILE):
            kt = kv[:, t * LANES:(t + 1) * LANES]
            vt = kv[:, KV_W + t * LANES:KV_W + (t + 1) * LANES]
            kn = kt * _pair_rstd(kt, low2) * kwv
            for hi in range(HEADS_PER_TILE):
                g = HEADS_PER_TILE * t + hi
                kdup = _dup_half(kn, hi, low2).astype(_MXU)
                vdup = _dup_half(vt, hi, low2).astype(_MXU)
                stack = []
                for tq in (2 * g, 2 * g + 1):
                    qt = q_ref[:, tq * LANES:(tq + 1) * LANES].astype(F32)
                    stack += _split_heads(qt * _pair_rstd(qt, low) * qwv, low)
                qs = jnp.concatenate(stack, axis=0).astype(_MXU)
                s = lax.dot_general(qs, kdup, _NT, preferred_element_type=F32) * scale
                s = jnp.where(valid, s, -1e30)
                sink = _stacked_sinks(sink_ref, g)
                m = jnp.maximum(jnp.max(s, axis=-1, keepdims=True), sink)
                e = jnp.exp(s - m)
                z = jnp.sum(e, axis=-1, keepdims=True) + jnp.exp(sink - m)
                o = jnp.dot((e / z).astype(_MXU), vdup, preferred_element_type=F32)
                for i, tq in enumerate((2 * g, 2 * g + 1)):
                    tile = slice(tq * LANES, (tq + 1) * LANES)
                    out = jnp.where(low, o[2 * i * WINDOW:(2 * i + 1) * WINDOW],
                                    o[(2 * i + 1) * WINDOW:(2 * i + 2) * WINDOW])
                    gate = gate_refs[tq * LANES // CW][:, tq * LANES % CW:tq * LANES % CW + LANES].astype(F32)
                    o_ref[:, tile] = out.astype(o_ref.dtype)
                    ya_ref[:, tile] = (out * (gate * _sigmoid(gate))).astype(ya_ref.dtype)
                lse = m + jnp.log(z)
                for r in range(Q_PER_KV):
                    lse_blk = jnp.where(head_lane == Q_PER_KV * g + r, lse[r * WINDOW:(r + 1) * WINDOW], lse_blk)
        lse_ref[...] = lse_blk

    return pl.pallas_call(
        body, name="attn_fwd", grid=(nblk,),
        in_specs=[pl.BlockSpec(memory_space=pltpu.SMEM), q_spec, kvc_spec, kvp_spec, w_spec, w_spec] + gate_specs
        + [_ANY] * nd,
        out_specs=[q_spec, l_spec, q_spec],
        out_shape=[jax.ShapeDtypeStruct((seq, ATTN_W), _MXU), jax.ShapeDtypeStruct((seq, N_Q_HEADS), F32),
                   jax.ShapeDtypeStruct((seq, ATTN_W), _MXU)],
        compiler_params=_params(("arbitrary",)),
    )(sinks, proj, proj, proj, qw2, kw2, *([proj] * ng), *deps)


def _attn2_bwd(proj, qw2, kw2, sinks, lse, attn, dya, d_proj, deps=()):
    seq = proj.shape[0]
    nblk = seq // WINDOW
    scale = 1.0 / math.sqrt(HEAD_DIM)
    q_spec, kvc_spec, kvp_spec, w_spec, l_spec, gate_specs = _attn_in_specs(nblk, True)
    s_spec = pl.BlockSpec((1, N_Q_HEADS), lambda n: (0, 0))
    d_spec = pl.BlockSpec((WINDOW, QKV_W + ATTN_W), lambda n: (nblk - 1 - n, 0))
    deps = list(deps) + [d_proj]
    nd, ng = len(deps), len(gate_specs)

    def body(sink_ref, q_ref, kvc_ref, kvp_ref, qw_ref, kw_ref, lse_ref, attn_ref, dya_ref, *rest):
        gate_refs = rest[:ng]
        d_ref, dqw_ref, dkw_ref, dsk_ref, carry, do_ref = rest[ng + nd:]
        step = pl.program_id(0)
        n = nblk - 1 - step

        @pl.when(step == 0)
        def _():
            carry[...] = jnp.zeros_like(carry)
            dqw_ref[...] = jnp.zeros_like(dqw_ref)
            dkw_ref[...] = jnp.zeros_like(dkw_ref)
            dsk_ref[...] = jnp.zeros_like(dsk_ref)

        for j, g_ref in enumerate(gate_refs):
            cols = slice(j * CW, (j + 1) * CW)
            f, df = _silu_and_grad(g_ref[...].astype(F32))
            dv = dya_ref[:, cols]
            do_ref[:, cols] = dv * f
            d_ref[:, QKV_W + j * CW:QKV_W + (j + 1) * CW] = (dv * attn_ref[:, cols].astype(F32) * df).astype(d_ref.dtype)

        low, low2 = _low_half(WINDOW), _low_half(2 * WINDOW)
        valid = _stacked_band_mask(n)
        head_lane = lax.broadcasted_iota(jnp.int32, (WINDOW, N_Q_HEADS), 1)
        sink_lane = lax.broadcasted_iota(jnp.int32, (1, N_Q_HEADS), 1)
        kv = jnp.concatenate([kvp_ref[...], kvc_ref[...]], axis=0).astype(F32)
        qwv, kwv = qw_ref[...], kw_ref[...]
        lse_blk = lse_ref[...]
        dqw = jnp.zeros((1, LANES), F32)
        dkw = jnp.zeros((1, LANES), F32)
        dsk = jnp.zeros((1, N_Q_HEADS), F32)
        for t in range(N_KV_HEADS // HEADS_PER_TILE):
            kt = kv[:, t * LANES:(t + 1) * LANES]
            vt = kv[:, KV_W + t * LANES:KV_W + (t + 1) * LANES]
            rk = _pair_rstd(kt, low2)
            kn = kt * rk * kwv
            dkn_t = jnp.zeros((2 * WINDOW, LANES), F32)
            dv_t = jnp.zeros((2 * WINDOW, LANES), F32)
            for hi in range(HEADS_PER_TILE):
                g = HEADS_PER_TILE * t + hi
                kdup = _dup_half(kn, hi, low2).astype(_MXU)
                vdup = _dup_half(vt, hi, low2).astype(_MXU)
                tiles = (2 * g, 2 * g + 1)
                qx, rq, stack, dstack, lse_rows = [], [], [], [], []
                for tq in tiles:
                    qt = q_ref[:, tq * LANES:(tq + 1) * LANES].astype(F32)
                    r = _pair_rstd(qt, low)
                    rq.append(r)
                    qx.append(qt * r)
                    stack += _split_heads(qx[-1] * qwv, low)
                    dstack += _split_heads(do_ref[:, tq * LANES:(tq + 1) * LANES], low)
                for r in range(Q_PER_KV):
                    lse_rows.append(jnp.sum(jnp.where(head_lane == Q_PER_KV * g + r, lse_blk, 0.0), axis=-1, keepdims=True))
                qs = jnp.concatenate(stack, axis=0).astype(_MXU)
                dos = jnp.concatenate(dstack, axis=0).astype(_MXU)
                lse_col = jnp.concatenate(lse_rows, axis=0)
                s = lax.dot_general(qs, kdup, _NT, preferred_element_type=F32) * scale
                s = jnp.where(valid, s, -1e30)
                p = jnp.exp(s - lse_col)
                dp = lax.dot_general(dos, vdup, _NT, preferred_element_type=F32)
                dsum = jnp.sum(p * dp, axis=-1, keepdims=True)
                ds = (p * (dp - dsum) * scale).astype(_MXU)
                dsink = -jnp.exp(_stacked_sinks(sink_ref, g) - lse_col) * dsum
                for r in range(Q_PER_KV):
                    dsk = dsk + jnp.where(sink_lane == Q_PER_KV * g + r, _colsum(dsink[r * WINDOW:(r + 1) * WINDOW]), 0.0)
                dv_g = _fold_halves(lax.dot_general(p.astype(_MXU), dos, _TN, preferred_element_type=F32))
                dkn_g = _fold_halves(lax.dot_general(ds, qs, _TN, preferred_element_type=F32))
                dv_t = jnp.where(low2, dv_t, dv_g) if hi else jnp.where(low2, dv_g, dv_t)
                dkn_t = jnp.where(low2, dkn_t, dkn_g) if hi else jnp.where(low2, dkn_g, dkn_t)
                dqn = jnp.dot(ds, kdup, preferred_element_type=F32)
                for i, tq in enumerate(tiles):
                    dqn_t = jnp.where(low, dqn[2 * i * WINDOW:(2 * i + 1) * WINDOW],
                                      dqn[(2 * i + 1) * WINDOW:(2 * i + 2) * WINDOW])
                    dq = rq[i] * (qwv * dqn_t - qx[i] * _pair_mean(dqn_t * qwv * qx[i], low))
                    d_ref[:, tq * LANES:(tq + 1) * LANES] = dq.astype(d_ref.dtype)
                    dqw = dqw + _colsum(dqn_t * qx[i])
            k_cols = slice(t * LANES, (t + 1) * LANES)
            v_cols = slice(KV_W + t * LANES, KV_W + (t + 1) * LANES)
            dkn_c = dkn_t[WINDOW:] + carry[:, k_cols]
            rc = rk[WINDOW:]
            kx = kt[WINDOW:] * rc
            dk = rc * (kwv * dkn_c - kx * _pair_mean(dkn_c * kwv * kx, low))
            d_ref[:, ATTN_W + t * LANES:ATTN_W + (t + 1) * LANES] = dk.astype(d_ref.dtype)
            d_ref[:, ATTN_W + KV_W + t * LANES:ATTN_W + KV_W + (t + 1) * LANES] = (
                dv_t[WINDOW:] + carry[:, v_cols]).astype(d_ref.dtype)
            carry[:, k_cols] = dkn_t[:WINDOW]
            carry[:, v_cols] = dv_t[:WINDOW]
            dkw = dkw + _colsum(dkn_c * kx)
        dqw_ref[...] += dqw
        dkw_ref[...] += dkw
        dsk_ref[...] += dsk

    return pl.pallas_call(
        body, name="attn_bwd", grid=(nblk,),
        in_specs=[pl.BlockSpec(memory_space=pltpu.SMEM), q_spec, kvc_spec, kvp_spec, w_spec, w_spec, l_spec, q_spec,
                  q_spec] + gate_specs + [_ANY] * nd,
        out_specs=[d_spec, w_spec, w_spec, s_spec],
        out_shape=[jax.ShapeDtypeStruct(d_proj.shape, d_proj.dtype), jax.ShapeDtypeStruct((1, LANES), F32),
                   jax.ShapeDtypeStruct((1, LANES), F32), jax.ShapeDtypeStruct((1, N_Q_HEADS), F32)],
        input_output_aliases={9 + ng + nd - 1: 0},
        scratch_shapes=[pltpu.VMEM((WINDOW, 2 * KV_W), F32), pltpu.VMEM((WINDOW, ATTN_W), F32)],
        compiler_params=_params(("arbitrary",)),
    )(sinks, proj, proj, proj, qw2, kw2, lse, attn, dya, *([proj] * ng), *deps)


def _ssm_discretise(a_re, a_im, log_dt):
    dt = jnp.exp(log_dt)
    mag = jnp.exp(dt * a_re)
    ab_re = mag * jnp.cos(dt * a_im)
    ab_im = mag * jnp.sin(dt * a_im)
    num_re = ab_re - 1.0
    num_im = ab_im
    den = a_re * a_re + a_im * a_im
    cf_re = (num_re * a_re + num_im * a_im) / den
    cf_im = (num_im * a_re - num_re * a_im) / den
    return ab_re, ab_im, cf_re, cf_im


def _ssm_params_fwd(a_re, a_im, log_dt):
    shp = jax.ShapeDtypeStruct(a_re.shape, F32)

    def body(are_ref, aim_ref, ldt_ref, abr_ref, abi_ref, cfr_ref, cfi_ref, alr_ref, ali_ref):
        abr, abi, cfr, cfi = _ssm_discretise(are_ref[...], aim_ref[...], ldt_ref[...])
        abr_ref[...], abi_ref[...], cfr_ref[...], cfi_ref[...] = abr, abi, cfr, cfi
        pr, pi = abr, abi
        for _ in range(int(math.log2(SSM_L))):
            pr, pi = pr * pr - pi * pi, 2.0 * pr * pi
        alr_ref[...], ali_ref[...] = pr, pi

    return pl.pallas_call(body, name="ssm_params_fwd", out_shape=[shp] * 6)(a_re, a_im, log_dt)


def _ssm_params_bwd(a_re, a_im, log_dt, d_abr, d_abi, d_cfr, d_cfi):
    def body(are_ref, aim_ref, ldt_ref, g0, g1, g2, g3, dare_ref, daim_ref, dldt_ref):
        _, vjp = jax.vjp(_ssm_discretise, are_ref[...], aim_ref[...], ldt_ref[...])
        dare_ref[...], daim_ref[...], dldt_ref[...] = vjp((g0[...], g1[...], g2[...], g3[...]))

    return pl.pallas_call(
        body, name="ssm_params_bwd",
        out_shape=[jax.ShapeDtypeStruct(a_re.shape, F32), jax.ShapeDtypeStruct(a_im.shape, F32),
                   jax.ShapeDtypeStruct(log_dt.shape, F32)],
    )(a_re, a_im, log_dt, d_abr, d_abi, d_cfr, d_cfi)


def _scan_cols(j):
    return pl.ds(j * SSM_SB, SSM_SB)


def _rows8(r):
    return pl.ds(pl.multiple_of(r * SUBLANES, SUBLANES), SUBLANES)


def _bcast8(row):
    return jnp.broadcast_to(row, (SUBLANES, row.shape[-1]))


def _token_order_pick():
    tok = lax.broadcasted_iota(jnp.int32, (SSM_T, SSM_T), 0)
    row = lax.broadcasted_iota(jnp.int32, (SSM_T, SSM_T), 1)
    return (row == SUBLANES * (tok % SSM_L) + tok // SSM_L).astype(_MXU)


def _scan_order_pick():
    row = lax.broadcasted_iota(jnp.int32, (SSM_T, SSM_T), 0)
    tok = lax.broadcasted_iota(jnp.int32, (SSM_T, SSM_T), 1)
    return (row == SUBLANES * (tok % SSM_L) + tok // SSM_L).astype(_MXU)


def _u_in_scan_order(u0_ref, u1_ref):
    u = jnp.concatenate([u0_ref[...], u1_ref[...]], axis=1).astype(_MXU)
    return jnp.dot(_scan_order_pick(), u, preferred_element_type=F32).astype(_MXU)


SCAN_UNROLL = 16


def _scan_loop(n, step, init):
    def trip(o, carry):
        for i in range(SCAN_UNROLL):
            carry = step(o * SCAN_UNROLL + i, carry)
        return carry

    return lax.fori_loop(0, n // SCAN_UNROLL, trip, init)


def _ssm_fwd(proj, b_re, b_im, c_re, c_im, d_skip, coef):
    seq = proj.shape[0]
    nc = seq // SSM_T
    T, L = SSM_T, SSM_L

    def body(u0_ref, u1_ref, bre_ref, bim_ref, cre_ref, cim_ref, d_ref, are_ref, aim_ref, cfr_ref, cfi_ref, alr_ref,
             ali_ref, y_ref, yg_ref, sre_ref, sim_ref, ire_ref, iim_ref, car_re, car_im, end_re, end_im, yg_scan, u_ref):
        c = pl.program_id(0)
        u_ref[...] = _u_in_scan_order(u0_ref, u1_ref)

        @pl.when(c == 0)
        def _():
            car_re[...] = jnp.zeros_like(car_re)
            car_im[...] = jnp.zeros_like(car_im)

        for j in range(SSM_JB):
            ub = u_ref[:, j * LANES:(j + 1) * LANES].astype(_MXU)
            bur = jnp.dot(ub, bre_ref[j], preferred_element_type=F32)
            bui = jnp.dot(ub, bim_ref[j], preferred_element_type=F32)
            cfr, cfi = cfr_ref[:, _scan_cols(j)], cfi_ref[:, _scan_cols(j)]
            sre_ref[:, _scan_cols(j)] = cfr * bur - cfi * bui
            sim_ref[:, _scan_cols(j)] = cfr * bui + cfi * bur

        for j in range(SSM_JB):
            cols = _scan_cols(j)
            ar, ai = _bcast8(are_ref[:, cols]), _bcast8(aim_ref[:, cols])

            def step1(r, s, cols=cols, ar=ar, ai=ai):
                sr, si = s
                rows = _rows8(r)
                return (ar * sr - ai * si + sre_ref[rows, cols], ar * si + ai * sr + sim_ref[rows, cols])

            zero = jnp.zeros((SUBLANES, SSM_SB), F32)
            er, ei = _scan_loop(L, step1, (zero, zero))
            end_re[:, cols] = er
            end_im[:, cols] = ei

        alr, ali = alr_ref[...], ali_ref[...]
        cr, ci = car_re[...], car_im[...]
        ire_ref[0:1, :] = cr
        iim_ref[0:1, :] = ci
        for i in range(1, SUBLANES):
            er, ei = end_re[i - 1:i, :], end_im[i - 1:i, :]
            cr, ci = alr * cr - ali * ci + er, alr * ci + ali * cr + ei
            ire_ref[i:i + 1, :] = cr
            iim_ref[i:i + 1, :] = ci

        for j in range(SSM_JB):
            cols = _scan_cols(j)
            ar, ai = _bcast8(are_ref[:, cols]), _bcast8(aim_ref[:, cols])

            def step2(r, s, cols=cols, ar=ar, ai=ai):
                sr, si = s
                rows = _rows8(r)
                nr = ar * sr - ai * si + sre_ref[rows, cols]
                ni = ar * si + ai * sr + sim_ref[rows, cols]
                sre_ref[rows, cols] = nr
                sim_ref[rows, cols] = ni
                return nr, ni

            _scan_loop(L, step2, (ire_ref[:, cols], iim_ref[:, cols]))

        car_re[...] = sre_ref[T - 1:T, :]
        car_im[...] = sim_ref[T - 1:T, :]

        for j in range(SSM_JB):
            cols = _scan_cols(j)
            ch = slice(j * LANES, (j + 1) * LANES)
            y = (jnp.dot(sre_ref[:, cols].astype(_MXU), cre_ref[j], preferred_element_type=F32)
                 - jnp.dot(sim_ref[:, cols].astype(_MXU), cim_ref[j], preferred_element_type=F32))
            y = y + d_ref[:, ch] * u_ref[:, ch].astype(F32)
            y_ref[:, ch] = y
            yg_scan[:, ch] = jax.nn.gelu(y).astype(yg_scan.dtype)
        yg_ref[...] = jnp.dot(_token_order_pick(), yg_scan[...], preferred_element_type=F32).astype(yg_ref.dtype)

    tok = pl.BlockSpec((T, SSM_W), lambda c: (c, 0))
    st = pl.BlockSpec((T, N_STATES), lambda c: (c, 0))
    ini = pl.BlockSpec((None, SUBLANES, N_STATES), lambda c: (c, 0, 0))
    bsp = pl.BlockSpec((SSM_JB, LANES, SSM_SB), lambda c: (0, 0, 0))
    csp = pl.BlockSpec((SSM_JB, SSM_SB, LANES), lambda c: (0, 0, 0))
    row_w = pl.BlockSpec((1, SSM_W), lambda c: (0, 0))
    row_s = pl.BlockSpec((1, N_STATES), lambda c: (0, 0))
    ucol = [pl.BlockSpec((T, CW), lambda c, k=k: (c, OFF_U + k)) for k in range(SSM_W // CW)]
    return pl.pallas_call(
        body, name="ssm_fwd", grid=(nc,),
        in_specs=ucol + [bsp, bsp, csp, csp, row_w] + [row_s] * 6,
        out_specs=[tok, tok, st, st, ini, ini],
        out_shape=[jax.ShapeDtypeStruct((seq, SSM_W), F32), jax.ShapeDtypeStruct((seq, SSM_W), _MXU),
                   jax.ShapeDtypeStruct((seq, N_STATES), F32), jax.ShapeDtypeStruct((seq, N_STATES), F32),
                   jax.ShapeDtypeStruct((nc, SUBLANES, N_STATES), F32),
                   jax.ShapeDtypeStruct((nc, SUBLANES, N_STATES), F32)],
        scratch_shapes=[pltpu.VMEM((1, N_STATES), F32), pltpu.VMEM((1, N_STATES), F32),
                        pltpu.VMEM((SUBLANES, N_STATES), F32), pltpu.VMEM((SUBLANES, N_STATES), F32),
                        pltpu.VMEM((T, SSM_W), _MXU), pltpu.VMEM((T, SSM_W), _MXU)],
        compiler_params=_params(("arbitrary",)),
    )(proj, proj, b_re, b_im, c_re, c_im, d_skip, *coef)


def _ssm_bwd(dyg, y, proj, s_re, s_im, i_re, i_im, b_re, b_im, c_re, c_im, d_skip, coef, d_proj, deps=()):
    seq = proj.shape[0]
    nc = seq // SSM_T
    T, L = SSM_T, SSM_L
    deps = list(deps) + [d_proj]

    def body(dyg_ref, y_ref, u0_ref, u1_ref, sre_ref, sim_ref, ire_ref, iim_ref, bre_ref, bim_ref, cre_ref, cim_ref,
             d_ref, are_ref, aim_ref, cfr_ref, cfi_ref, alr_ref, ali_ref, *rest):
        (du_ref, dbre_out, dbim_out, dcre_out, dcim_out, dd_ref, dar_ref, dai_ref, dcfr_ref, dcfi_ref,
         lre, lim, car_re, car_im, end_re, end_im, ini_re, ini_im, dbre_ref, dbim_ref, dcre_ref, dcim_ref,
         dy_ref, du_scan, u_ref) = rest[len(deps):]
        step = pl.program_id(0)
        dy_ref[...] = jax.vjp(jax.nn.gelu, y_ref[...])[1](dyg_ref[...].astype(F32))[0]
        u_ref[...] = _u_in_scan_order(u0_ref, u1_ref)

        @pl.when(step == 0)
        def _():
            car_re[...] = jnp.zeros_like(car_re)
            car_im[...] = jnp.zeros_like(car_im)
            for ref in (dbre_ref, dbim_ref, dcre_ref, dcim_ref, dd_ref, dar_ref, dai_ref, dcfr_ref, dcfi_ref):
                ref[...] = jnp.zeros_like(ref)

        for j in range(SSM_JB):
            dyb = dy_ref[:, j * LANES:(j + 1) * LANES].astype(_MXU)
            lre[:, _scan_cols(j)] = lax.dot_general(dyb, cre_ref[j], _NT, preferred_element_type=F32)
            lim[:, _scan_cols(j)] = -lax.dot_general(dyb, cim_ref[j], _NT, preferred_element_type=F32)

        for j in range(SSM_JB):
            cols = _scan_cols(j)
            ar, ai = _bcast8(are_ref[:, cols]), _bcast8(aim_ref[:, cols])

            def step1(t, s, cols=cols, ar=ar, ai=ai):
                sr, si = s
                rows = _rows8(L - 1 - t)
                return (ar * sr + ai * si + lre[rows, cols], ar * si - ai * sr + lim[rows, cols])

            zero = jnp.zeros((SUBLANES, SSM_SB), F32)
            er, ei = _scan_loop(L, step1, (zero, zero))
            end_re[:, cols] = er
            end_im[:, cols] = ei

        alr, ali = alr_ref[...], ali_ref[...]
        cr, ci = car_re[...], car_im[...]
        ini_re[SUBLANES - 1:SUBLANES, :] = cr
        ini_im[SUBLANES - 1:SUBLANES, :] = ci
        for i in range(SUBLANES - 2, -1, -1):
            er, ei = end_re[i + 1:i + 2, :], end_im[i + 1:i + 2, :]
            cr, ci = alr * cr + ali * ci + er, alr * ci - ali * cr + ei
            ini_re[i:i + 1, :] = cr
            ini_im[i:i + 1, :] = ci

        for j in range(SSM_JB):
            cols = _scan_cols(j)
            ar, ai = _bcast8(are_ref[:, cols]), _bcast8(aim_ref[:, cols])

            def step2(t, s, cols=cols, ar=ar, ai=ai):
                sr, si = s
                rows = _rows8(L - 1 - t)
                nr = ar * sr + ai * si + lre[rows, cols]
                ni = ar * si - ai * sr + lim[rows, cols]
                lre[rows, cols] = nr
                lim[rows, cols] = ni
                return nr, ni

            _scan_loop(L, step2, (ini_re[:, cols], ini_im[:, cols]))

        car_re[...] = lre[0:1, :]
        car_im[...] = lim[0:1, :]

        head, tail, body_rows = slice(0, SUBLANES), slice(SUBLANES, T), slice(0, T - SUBLANES)
        for j in range(SSM_JB):
            cols = _scan_cols(j)
            ch = slice(j * LANES, (j + 1) * LANES)
            lr, li = lre[:, cols], lim[:, cols]
            lt_r, lt_i, sp_r, sp_i = lre[tail, cols], lim[tail, cols], sre_ref[body_rows, cols], sim_ref[body_rows, cols]
            lh_r, lh_i, si_r, si_i = lre[head, cols], lim[head, cols], ire_ref[:, cols], iim_ref[:, cols]
            dar_ref[:, cols] += _colsum(lt_r * sp_r + lt_i * sp_i) + _colsum(lh_r * si_r + lh_i * si_i)
            dai_ref[:, cols] += _colsum(lt_i * sp_r - lt_r * sp_i) + _colsum(lh_i * si_r - lh_r * si_i)
            ub = u_ref[:, ch].astype(_MXU)
            uf = ub.astype(F32)
            bur = jnp.dot(ub, bre_ref[j], preferred_element_type=F32)
            bui = jnp.dot(ub, bim_ref[j], preferred_element_type=F32)
            dcfr_ref[:, cols] += _colsum(lr * bur + li * bui)
            dcfi_ref[:, cols] += _colsum(li * bur - lr * bui)
            cfr, cfi = cfr_ref[:, cols], cfi_ref[:, cols]
            dbur = (cfr * lr + cfi * li).astype(_MXU)
            dbui = (cfr * li - cfi * lr).astype(_MXU)
            dyf = dy_ref[:, ch]
            dyb = dyf.astype(_MXU)
            du = (lax.dot_general(dbur, bre_ref[j], _NT, preferred_element_type=F32)
                  + lax.dot_general(dbui, bim_ref[j], _NT, preferred_element_type=F32) + d_ref[:, ch] * dyf)
            du_scan[:, ch] = du.astype(du_scan.dtype)
            dbre_ref[j] += lax.dot_general(ub, dbur, _TN, preferred_element_type=F32)
            dbim_ref[j] += lax.dot_general(ub, dbui, _TN, preferred_element_type=F32)
            dcre_ref[j] += lax.dot_general(sre_ref[:, cols].astype(_MXU), dyb, _TN, preferred_element_type=F32)
            dcim_ref[j] -= lax.dot_general(sim_ref[:, cols].astype(_MXU), dyb, _TN, preferred_element_type=F32)
            dd_ref[:, ch] += _colsum(dyf * uf)
        du_ref[...] = jnp.dot(_token_order_pick(), du_scan[...], preferred_element_type=F32).astype(du_ref.dtype)

        @pl.when(step == nc - 1)
        def _():
            for acc, out in ((dbre_ref, dbre_out), (dbim_ref, dbim_out), (dcre_ref, dcre_out), (dcim_ref, dcim_out)):
                pltpu.sync_copy(acc, out)

    tok = pl.BlockSpec((T, SSM_W), lambda c: (nc - 1 - c, 0))
    st = pl.BlockSpec((T, N_STATES), lambda c: (nc - 1 - c, 0))
    ini = pl.BlockSpec((None, SUBLANES, N_STATES), lambda c: (nc - 1 - c, 0, 0))
    bsp = pl.BlockSpec((SSM_JB, LANES, SSM_SB), lambda c: (0, 0, 0))
    csp = pl.BlockSpec((SSM_JB, SSM_SB, LANES), lambda c: (0, 0, 0))
    row_w = pl.BlockSpec((1, SSM_W), lambda c: (0, 0))
    row_s = pl.BlockSpec((1, N_STATES), lambda c: (0, 0))
    big = pltpu.VMEM((T, N_STATES), F32)
    one = pltpu.VMEM((1, N_STATES), F32)
    eight = pltpu.VMEM((SUBLANES, N_STATES), F32)
    ucol = [pl.BlockSpec((T, CW), lambda c, k=k: (nc - 1 - c, OFF_U + k)) for k in range(SSM_W // CW)]
    return pl.pallas_call(
        body, name="ssm_bwd", grid=(nc,),
        in_specs=[tok, tok] + ucol + [st, st, ini, ini, bsp, bsp, csp, csp, row_w] + [row_s] * 6 + [_ANY] * len(deps),
        out_specs=[pl.BlockSpec((pl.Element(T), pl.Element(SSM_W)), lambda c: ((nc - 1 - c) * T, OFF_U * CW)),
                   _ANY, _ANY, _ANY, _ANY, row_w, row_s, row_s, row_s, row_s],
        input_output_aliases={19 + len(deps) - 1: 0},
        out_shape=[jax.ShapeDtypeStruct(d_proj.shape, d_proj.dtype),
                   jax.ShapeDtypeStruct((SSM_JB, LANES, SSM_SB), F32), jax.ShapeDtypeStruct((SSM_JB, LANES, SSM_SB), F32),
                   jax.ShapeDtypeStruct((SSM_JB, SSM_SB, LANES), F32), jax.ShapeDtypeStruct((SSM_JB, SSM_SB, LANES), F32),
                   jax.ShapeDtypeStruct((1, SSM_W), F32)] + [jax.ShapeDtypeStruct((1, N_STATES), F32)] * 4,
        scratch_shapes=[big, big, one, one, eight, eight, eight, eight,
                        pltpu.VMEM((SSM_JB, LANES, SSM_SB), F32), pltpu.VMEM((SSM_JB, LANES, SSM_SB), F32),
                        pltpu.VMEM((SSM_JB, SSM_SB, LANES), F32), pltpu.VMEM((SSM_JB, SSM_SB, LANES), F32),
                        pltpu.VMEM((T, SSM_W), F32), pltpu.VMEM((T, SSM_W), _MXU), pltpu.VMEM((T, SSM_W), _MXU)],
        compiler_params=_params(("arbitrary",)),
    )(dyg, y, proj, proj, s_re, s_im, i_re, i_im, b_re, b_im, c_re, c_im, d_skip, *coef, *deps)


def _block_diag_b(b):
    t = b.reshape(SSM_JB, 8, STATE, GROUP).transpose(0, 1, 3, 2)
    eye = jnp.eye(8, dtype=b.dtype)
    return (t[:, :, :, None, :] * eye[None, :, None, :, None]).reshape(SSM_JB, LANES, SSM_SB)


def _block_diag_c(c):
    t = c.reshape(SSM_JB, 8, GROUP, STATE).transpose(0, 1, 3, 2)
    eye = jnp.eye(8, dtype=c.dtype)
    return (t[:, :, :, None, :] * eye[None, :, None, :, None]).reshape(SSM_JB, SSM_SB, LANES)


def _diag_of_b(blk):
    t = blk.reshape(SSM_JB, 8, GROUP, 8, STATE)
    d = jnp.sum(t * jnp.eye(8, dtype=blk.dtype)[None, :, None, :, None], axis=3)
    return d.transpose(0, 1, 3, 2).reshape(N_GROUPS, STATE, GROUP)


def _diag_of_c(blk):
    t = blk.reshape(SSM_JB, 8, STATE, 8, GROUP)
    d = jnp.sum(t * jnp.eye(8, dtype=blk.dtype)[None, :, None, :, None], axis=3)
    return d.transpose(0, 1, 3, 2).reshape(N_GROUPS, GROUP, STATE)


def _to_scan_order(v):
    seq, w = v.shape
    return v.reshape(seq // SSM_T, SUBLANES, SSM_L, w).transpose(0, 2, 1, 3).reshape(seq, w)


def _adamw_math(w, g, m, v):
    nm = ADAM_B1 * m + (1.0 - ADAM_B1) * g
    nv = ADAM_B2 * v + (1.0 - ADAM_B2) * jnp.square(g)
    m_hat = nm / (1.0 - ADAM_B1 ** ADAM_STEP)
    v_hat = nv / (1.0 - ADAM_B2 ** ADAM_STEP)
    return -ADAM_LR * (m_hat / (jnp.sqrt(v_hat) + ADAM_EPS) + ADAM_WD * w), nm, nv


def _adamw(w, g, m, v, *, name, tm, deps=()):
    rows, cols = w.shape
    nd = len(deps)

    def body(w_ref, g_ref, m_ref, v_ref, *rest):
        d_ref, nm_ref, nv_ref = rest[nd:]
        d_ref[...], nm_ref[...], nv_ref[...] = _adamw_math(w_ref[...], g_ref[...], m_ref[...], v_ref[...])

    spec = pl.BlockSpec((tm, cols), lambda i: (i, 0))
    shp = jax.ShapeDtypeStruct((rows, cols), F32)
    return pl.pallas_call(body, name=name, grid=(rows // tm,), in_specs=[spec] * 4 + [_ANY] * nd,
                          out_specs=[spec] * 3, out_shape=[shp] * 3,
                          compiler_params=_params(("arbitrary",)))(w, g, m, v, *deps)


def _place():
    x, y, c = lax.axis_index("x"), lax.axis_index("y"), lax.axis_index("c")
    chips = [(1 - x, y), (x, 1 - y), (1 - x, 1 - y)]
    return x, y, c, chips


def _remote(src, dst, send_sem, recv_sem, dev):
    return pltpu.make_async_remote_copy(src_ref=src, dst_ref=dst, send_sem=send_sem, recv_sem=recv_sem,
                                        device_id=dev, device_id_type=MESH)


def _place_shard(w, mine_arr, *, name, tm=256, deps=()):
    rows, cols = w.shape

    def body(m_ref, w_ref, *rest):
        rest[-1][...] = w_ref[...].astype(rest[-1].dtype)

    return pl.pallas_call(
        body, name=name,
        grid_spec=pltpu.PrefetchScalarGridSpec(
            num_scalar_prefetch=1, grid=(rows // tm,),
            in_specs=[pl.BlockSpec((tm, cols), lambda i, m: (i, 0))] + [_ANY] * len(deps),
            out_specs=pl.BlockSpec((None, tm, cols), lambda i, m: (m[0], i, 0))),
        out_shape=jax.ShapeDtypeStruct((N_CHIPS, rows, cols), _WIRE),
        compiler_params=_params(("arbitrary",)),
    )(mine_arr, w, *deps)


_HBM = pl.BlockSpec(memory_space=pltpu.HBM)
_SEM = pl.BlockSpec(memory_space=pltpu.SEMAPHORE)
_EFFECT = pltpu.SideEffectType.DATAFLOW_SIDE_EFFECTING


def _copies_start(name, bufs, plan, count, after=()):
    nb, na = len(bufs), len(after)

    def body(*refs):
        send_sems, recv_sems, token = refs[nb + na], refs[nb + na + 1], refs[-1]
        copies = plan(refs[:nb])
        assert len(copies) == count
        for i, (src, dst, dev, _) in enumerate(copies):
            _remote(src, dst, send_sems.at[i], recv_sems.at[i], dev).start()
        token[...] = jnp.zeros_like(token)

    res = pl.pallas_call(
        body, name=name, in_specs=[_HBM] * nb + [_ANY] * na,
        out_specs=(_SEM, _SEM, *[_HBM] * nb, pl.BlockSpec(memory_space=pltpu.VMEM)),
        out_shape=(pltpu.SemaphoreType.DMA((count,)), pltpu.SemaphoreType.DMA((count,)),
                   *[pltpu.HBM(b.shape, b.dtype) for b in bufs], jax.ShapeDtypeStruct((SUBLANES, LANES), F32)),
        input_output_aliases={i: 2 + i for i in range(nb)},
        compiler_params=pltpu.CompilerParams(has_side_effects=_EFFECT),
    )(*[pltpu.with_memory_space_constraint(b, pltpu.HBM) for b in bufs], *after)
    return (res[0], res[1]), list(res[2:2 + nb]), res[-1]


def _copies_wait(name, bufs, sems, plan, after=(), which=None):
    nb, na = len(bufs), len(after)

    def body(*refs):
        send_sems, recv_sems = refs[nb], refs[nb + 1]
        for i, (src, _, dev, land) in enumerate(plan(refs[:nb])):
            if which is not None and i not in which:
                continue
            cp = _remote(src, land, send_sems.at[i], recv_sems.at[i], dev)
            cp.wait_send()
            cp.wait_recv()

    res = pl.pallas_call(
        body, name=name, in_specs=[_HBM] * nb + [_SEM, _SEM] + [_ANY] * na, out_specs=[_HBM] * nb,
        out_shape=[pltpu.HBM(b.shape, b.dtype) for b in bufs],
        input_output_aliases={i: i for i in range(nb)},
        compiler_params=pltpu.CompilerParams(has_side_effects=_EFFECT),
    )(*bufs, *sems, *after)
    return list(res)


def _plan_gather_ici(fulls, which=(0, 1, 2)):
    x, y, c, chips = _place()
    copies = []
    for f in fulls:
        half = pl.ds(c * (f.shape[1] // 2), f.shape[1] // 2)
        own = f.at[2 * x + y, half]
        for chip in [chips[k] for k in which]:
            copies.append((own, own, (*chip, c), f.at[2 * chip[0] + chip[1], half]))
    return copies


def _plan_gather_d2d(fulls, which=(0, 1, 2)):
    x, y, c, chips = _place()
    copies = []
    for f in fulls:
        r2 = f.shape[1] // 2
        for chip in [chips[k] for k in which]:
            blk = 2 * chip[0] + chip[1]
            landed = f.at[blk, pl.ds(c * r2, r2)]
            copies.append((landed, landed, (x, y, 1 - c), f.at[blk, pl.ds((1 - c) * r2, r2)]))
    return copies


def _plan_relay_direct(fulls):
    (f,) = fulls
    x, y, c, chips = _place()
    half = pl.ds(c * (f.shape[1] // 2), f.shape[1] // 2)
    own = f.at[2 * x + y, half]
    return [(own, own, (*chip, c), f.at[2 * chip[0] + chip[1], half]) for chip in chips[:2]]


def _plan_relay_forward(fulls, k):
    (f,) = fulls
    x, y, c, chips = _place()
    r2 = f.shape[1] // 2
    half, other = pl.ds(c * r2, r2), pl.ds((1 - c) * r2, r2)
    quarter = pl.ds(c * r2 + k * (r2 // 2), r2 // 2)
    blk, far = 2 * chips[k][0] + chips[k][1], 2 * chips[2][0] + chips[2][1]
    passed, landed = f.at[blk, quarter], f.at[blk, half]
    return [(passed, passed, (*chips[1 - k], c), f.at[far, quarter]), (landed, landed, (x, y, 1 - c), f.at[blk, other])]


def _plan_relay_last(fulls):
    (f,) = fulls
    x, y, c, chips = _place()
    r2 = f.shape[1] // 2
    far = 2 * chips[2][0] + chips[2][1]
    landed = f.at[far, pl.ds(c * r2, r2)]
    return [(landed, landed, (x, y, 1 - c), f.at[far, pl.ds((1 - c) * r2, r2)])]


def _plan_swap_halves(refs):
    x, y, c, _ = _place()
    n = len(refs) // 2
    copies = []
    for g, land in zip(refs[:n], refs[n:]):
        r2 = g.shape[1] // 2
        copies.append((g.at[:, pl.ds((1 - c) * r2, r2), :], land, (x, y, 1 - c), land))
    return copies


def _plan_scatter_chips(refs):
    x, y, c, chips = _place()
    n = len(refs) // 2
    copies = []
    for h, land in zip(refs[:n], refs[n:]):
        for k, chip in enumerate(chips):
            copies.append((h.at[2 * chip[0] + chip[1]], land.at[k], (*chip, c), land.at[k]))
    return copies


def _plan_join_halves(totals):
    x, y, c, _ = _place()
    copies = []
    for t in totals:
        r2 = t.shape[0] // 2
        mine = t.at[pl.ds(c * r2, r2)]
        copies.append((mine, mine, (x, y, 1 - c), t.at[pl.ds((1 - c) * r2, r2)]))
    return copies


def _add_sibling_half(g, got, c_arr, *, name, tm):
    _, rows, cols = g.shape
    r2 = rows // 2
    nb = r2 // tm

    def body(c_ref, g_ref, r_ref, o_ref):
        o_ref[...] = (g_ref[...].astype(F32) + r_ref[...].astype(F32)).astype(o_ref.dtype)

    return pl.pallas_call(
        body, name=name,
        grid_spec=pltpu.PrefetchScalarGridSpec(
            num_scalar_prefetch=1, grid=(N_CHIPS, nb),
            in_specs=[pl.BlockSpec((None, tm, cols), lambda b, i, c: (b, c[0] * nb + i, 0)),
                      pl.BlockSpec((None, tm, cols), lambda b, i, c: (b, i, 0))],
            out_specs=pl.BlockSpec((None, tm, cols), lambda b, i, c: (b, i, 0))),
        out_shape=jax.ShapeDtypeStruct((N_CHIPS, r2, cols), _WIRE),
        compiler_params=_params(("arbitrary", "arbitrary")),
    )(c_arr, g, got)


def _add_chips(h, got, place_arr, *, name, tm):
    _, r2, cols = h.shape
    nb = r2 // tm

    def body(p_ref, h_ref, r_ref, o_ref):
        o_ref[...] = ((h_ref[...].astype(F32) + r_ref[0].astype(F32)) + r_ref[1].astype(F32)) + r_ref[2].astype(F32)

    return pl.pallas_call(
        body, name=name,
        grid_spec=pltpu.PrefetchScalarGridSpec(
            num_scalar_prefetch=1, grid=(nb,),
            in_specs=[pl.BlockSpec((None, tm, cols), lambda i, p: (p[0], i, 0)),
                      pl.BlockSpec((3, tm, cols), lambda i, p: (0, i, 0))],
            out_specs=pl.BlockSpec((tm, cols), lambda i, p: (p[1] * nb + i, 0))),
        out_shape=jax.ShapeDtypeStruct((2 * r2, cols), F32),
        compiler_params=_params(("arbitrary",)),
    )(place_arr, h, got)


class _ReduceScatter:
    def __init__(self, tag, names, grads):
        self.tag, self.names, self.n = tag, names, len(names)
        core = lax.axis_index("c").astype(jnp.int32)
        chip = (2 * lax.axis_index("x") + lax.axis_index("y")).astype(jnp.int32)
        self.c_arr, self.place_arr = core.reshape(1), jnp.stack([chip, core])
        self.bufs = list(grads)

    def _start(self, step, bufs, plan, count, after):
        self.plan = plan
        self.step = f"grad_{step}_{self.tag}"
        self.sems, self.bufs, token = _copies_start(self.step + "_start", bufs, plan, count, after)
        return [token]

    def _wait(self, after):
        self.bufs = _copies_wait(self.step + "_wait", self.bufs, self.sems, self.plan, after)
        return self.bufs

    def start_swap(self, after=()):
        lands = [lax.empty((N_CHIPS, g.shape[1] // 2, g.shape[2]), g.dtype) for g in self.bufs]
        return self._start("swap", self.bufs + lands, _plan_swap_halves, self.n, after)

    def start_scatter(self, after):
        bufs = self._wait(after)
        pair = [_add_sibling_half(g, r, self.c_arr, name=f"grad_add_sibling_{nm}", tm=min(256, g.shape[1] // 2))
                for nm, g, r in zip(self.names, bufs[:self.n], bufs[self.n:])]
        lands = [lax.empty((3,) + h.shape[1:], h.dtype) for h in pair]
        return self._start("scatter", pair + lands, _plan_scatter_chips, 3 * self.n, ())

    def start_join(self, after):
        bufs = self._wait(after)
        total = [_add_chips(h, r, self.place_arr, name=f"grad_add_chips_{nm}", tm=min(256, h.shape[1]))
                 for nm, h, r in zip(self.names, bufs[:self.n], bufs[self.n:])]
        return self._start("join", total, _plan_join_halves, self.n, ())

    def finish(self, after):
        return dict(zip(self.names, self._wait(after)))


def _all_gather_small(v):
    m_per, n = v.shape

    def body(x_ref, out_ref, send_sems, recv_sems, local_sem):
        x, y, c, chips = _place()
        me, sibling = (x, y, c), (x, y, 1 - c)

        def rows(px, py, pc):
            return out_ref.at[4 * px + 2 * py + pc]

        def copy(k, block, to, src=None):
            return _remote(rows(*block) if src is None else src, rows(*block), send_sems.at[k], recv_sems.at[k], to)

        mine = pltpu.make_async_copy(x_ref, rows(*me), local_sem)
        mine.start()
        first = [copy(0, me, sibling, src=x_ref)]
        first += [copy(1 + j, me, (*chip, c), src=x_ref) for j, chip in enumerate(chips)]
        for cp in first:
            cp.start()
        passed = [copy(4 + j, (*chip, c), sibling) for j, chip in enumerate(chips)]
        for j, chip in enumerate(chips):
            copy(1 + j, (*chip, c), me).wait_recv()
            passed[j].start()
        copy(0, sibling, me).wait_recv()
        for j, chip in enumerate(chips):
            copy(4 + j, (*chip, 1 - c), me).wait_recv()
        for cp in first + passed:
            cp.wait_send()
        mine.wait()

    return pl.pallas_call(
        body, name="gather_small_grads",
        out_shape=jax.ShapeDtypeStruct((8, m_per, n), v.dtype),
        in_specs=[pl.BlockSpec(memory_space=pltpu.VMEM)], out_specs=pl.BlockSpec(memory_space=pltpu.VMEM),
        scratch_shapes=[pltpu.SemaphoreType.DMA((7,)), pltpu.SemaphoreType.DMA((7,)), pltpu.SemaphoreType.DMA],
        compiler_params=pltpu.CompilerParams(vmem_limit_bytes=VMEM_LIMIT),
    )(v)


def _sum8(v, *, name):
    _, m, n = v.shape

    def body(v_ref, o_ref):
        acc = v_ref[0]
        for d in range(1, 8):
            acc = acc + v_ref[d]
        o_ref[...] = acc

    return pl.pallas_call(body, name=name, out_shape=jax.ShapeDtypeStruct((m, n), F32),
                          compiler_params=pltpu.CompilerParams(vmem_limit_bytes=VMEM_LIMIT))(v)


def _local_step(x, target, norm_w, q_norm_w, k_norm_w, sinks, a_re, a_im, log_dt, b_re, b_im, c_re, c_im, d_skip,
                b_glu, io):
    seq = x.shape[0]
    qw2 = jnp.tile(q_norm_w.reshape(1, HEAD_DIM), (1, HEADS_PER_TILE))
    kw2 = jnp.tile(k_norm_w.reshape(1, HEAD_DIM), (1, HEADS_PER_TILE))
    nw, bg = norm_w.reshape(1, D_MODEL), b_glu.reshape(1, D_MODEL)
    dsk = d_skip.reshape(1, SSM_W)

    h, rstd = _rms_fwd(x, nw, deps=io.begin())
    proj, w_in4 = io.projection(h)
    attn, lse, ya_in = _attn2_fwd(proj, qw2, kw2, sinks, deps=io.after_proj(proj))
    w_ap4 = io.weight("w_attn_proj", ya_in)
    w_glu4, w_sp4, w_out = io.weight("w_glu", ya_in), io.weight("w_ssm_proj", ya_in), io.weight("w_out", ya_in)
    y_a = _mm(ya_in, w_ap4, mode="nn", name="mm_attn_proj", tm=2048, tn=512, tk=ATTN_W, b_blocked=True,
              rows_outer=True, out_dtype=_MXU)

    flat_a = (a_re.reshape(1, N_STATES), a_im.reshape(1, N_STATES), jnp.repeat(log_dt, STATE).reshape(1, N_STATES))
    coef = _ssm_params_fwd(*flat_a)
    bre_blk, bim_blk = _block_diag_b(b_re).astype(_MXU), _block_diag_b(b_im).astype(_MXU)
    cre_blk, cim_blk = _block_diag_c(c_re).astype(_MXU), _block_diag_c(c_im).astype(_MXU)
    y_scan, yg, s_re, s_im, i_re, i_im = _ssm_fwd(proj, bre_blk, bim_blk, cre_blk, cim_blk, dsk, coef)
    glu, ys_in = _mm_glu_gate(yg, w_glu4, bg, proj)
    y_s = _mm(ys_in, w_sp4, mode="nn", name="mm_ssm_proj", tm=2048, tn=512, tk=SSM_W, b_blocked=True,
              rows_outer=True, out_dtype=_MXU)

    merged, dout, dout_b, sq = _mm_merge_out_loss(proj, y_a, y_s, w_out, x, target)
    loss = 0.5 * jnp.sum(sq) / D_MODEL

    d_ya, d_ys, d_proj = _mm_merge_bwd(dout_b, w_out, proj, y_a, y_s)
    g_w_out = _mm(merged, dout_b, mode="tn", name="mm_g_w_out", tm=1024, tn=D_MODEL, tk=1024, out_dtype=_WIRE)

    d_ya_in = _mm(d_ya, w_ap4, mode="nt", name="mm_d_attn_gate", tm=2048, tn=ATTN_W, tk=512, b_blocked=True)
    g_w_ap = _mm(ya_in, d_ya, mode="tn", name="mm_g_w_attn_proj", tm=ATTN_W, tn=D_MODEL, tk=2048, out_dtype=_WIRE,
                 out_blocked=True)

    g_w_sp = _mm(ys_in, d_ys, mode="tn", name="mm_g_w_ssm_proj", tm=SSM_W, tn=D_MODEL, tk=2048, out_dtype=_WIRE,
                 out_blocked=True)
    d_glu, d_proj, g_bglu = _mm_ssm_gate_bwd(d_ys, w_sp4, glu, bg, proj, d_proj)
    d_yg = _mm(d_glu, w_glu4, mode="nt", name="mm_d_gelu", tm=2048, tn=SSM_W, tk=512, b_blocked=True, out_dtype=_MXU)
    g_w_glu = _mm(yg, d_glu, mode="tn", name="mm_g_w_glu", tm=SSM_W, tn=D_MODEL, tk=2048, out_dtype=_WIRE, out_blocked=True)
    dep = io.later_grads(dict(w_attn_proj=g_w_ap, w_glu=g_w_glu, w_ssm_proj=g_w_sp,
                              w_out=g_w_out.reshape(N_CHIPS, D_MODEL // N_CHIPS, D_MODEL)))

    d_proj, g_qw2, g_kw2, g_sk = _attn2_bwd(proj, qw2, kw2, sinks, lse, attn, d_ya_in, d_proj, deps=dep)
    dep = io.before_scan_backward([d_proj])
    (d_proj, g_bre, g_bim, g_cre, g_cim, g_dsk, g_abr, g_abi, g_cfr, g_cfi) = _ssm_bwd(
        _to_scan_order(d_yg), y_scan, proj, s_re, s_im, i_re, i_im, bre_blk, bim_blk, cre_blk, cim_blk, dsk, coef,
        d_proj, deps=dep)
    g_are, g_aim, g_ldt = _ssm_params_bwd(*flat_a, g_abr, g_abi, g_cfr, g_cfi)
    g_are, g_aim = g_are.reshape(N_GROUPS, STATE), g_aim.reshape(N_GROUPS, STATE)
    g_ldt = g_ldt.reshape(N_GROUPS, STATE).sum(axis=1)
    dep = io.before_input_projection_grad([d_proj]) + io.small_grads(dict(
        q_norm_w=g_qw2[0, :HEAD_DIM] + g_qw2[0, HEAD_DIM:], k_norm_w=g_kw2[0, :HEAD_DIM] + g_kw2[0, HEAD_DIM:],
        sinks=g_sk.reshape(N_Q_HEADS), A_re=g_are, A_im=g_aim, log_dt=g_ldt,
        B_re=_diag_of_b(g_bre), B_im=_diag_of_b(g_bim), C_re=_diag_of_c(g_cre), C_im=_diag_of_c(g_cim),
        D_skip=g_dsk.reshape(N_GROUPS, GROUP), b_glu=g_bglu.reshape(D_MODEL)))
    g_w_in = _mm(h, d_proj, mode="tn", name="mm_g_w_in", tm=1024, tn=IN_W // 4, tk=1024, out_dtype=_WIRE,
                 out_blocked=True, deps=dep)
    dep = io.input_projection_grad(g_w_in)
    d_h = _mm(d_proj, w_in4, mode="nt", name="mm_d_h", tm=1024, tn=D_MODEL, tk=IN_W // 4, b_blocked=True, deps=dep)
    grad_x, g_nw = _rms_bwd(d_h, x, rstd, nw, dout)
    return loss, grad_x, g_nw.reshape(D_MODEL)


_SMALL = ["norm_w", "q_norm_w", "k_norm_w", "sinks", "A_re", "A_im", "log_dt", "B_re", "B_im", "C_re", "C_im",
          "D_skip", "b_glu"]
_BIG = ["w_in", "w_attn_proj", "w_glu", "w_ssm_proj", "w_out"]
_LATER = _BIG[1:]
_RELATIONS = ("flip_x", "flip_y", "flip_xy")
_ORDER = ["norm_w", "w_in", "q_norm_w", "k_norm_w", "sinks", "w_attn_proj", "A_re", "A_im", "log_dt", "B_re", "B_im",
          "C_re", "C_im", "D_skip", "w_glu", "b_glu", "w_ssm_proj", "w_out"]
_PACK_W = 1024


def _packed_rows(size):
    unit = SUBLANES * _PACK_W
    return -(-size // unit) * SUBLANES


def _pack_small(d, names):
    parts = []
    for n in names:
        flat = d[n].reshape(-1).astype(F32)
        rows = _packed_rows(flat.shape[0])
        parts.append(jnp.pad(flat, (0, rows * _PACK_W - flat.shape[0])).reshape(rows, _PACK_W))
    return jnp.concatenate(parts, axis=0)


def _unpack_small(packed, like, names):
    out, pos = {}, 0
    for n in names:
        rows = _packed_rows(like[n].size)
        out[n] = packed[pos:pos + rows].reshape(-1)[:like[n].size].reshape(like[n].shape)
        pos += rows
    return out


def _place_block(v, index_arr, *, name):
    rows, cols = v.shape

    def body(i_ref, v_ref, o_ref):
        o_ref[...] = v_ref[...]

    return pl.pallas_call(
        body, name=name,
        grid_spec=pltpu.PrefetchScalarGridSpec(
            num_scalar_prefetch=1, grid=(1,),
            in_specs=[pl.BlockSpec((rows, cols), lambda i, d: (0, 0))],
            out_specs=pl.BlockSpec((None, rows, cols), lambda i, d: (d[0], 0, 0))),
        out_shape=jax.ShapeDtypeStruct((8, rows, cols), v.dtype),
        compiler_params=_params(("arbitrary",)),
    )(index_arr, v)


def _plan_all_to_all(refs):
    (land,) = refs
    x, y, c, _ = _place()
    own = land.at[4 * x + 2 * y + c]
    copies = []
    for fx, fy, fc in [(0, 0, 1), (0, 1, 0), (0, 1, 1), (1, 0, 0), (1, 0, 1), (1, 1, 0), (1, 1, 1)]:
        px, py, pc = (1 - x) if fx else x, (1 - y) if fy else y, (1 - c) if fc else c
        copies.append((own, own, (px, py, pc), land.at[4 * px + 2 * py + pc]))
    return copies


def _adamw_whole(w, g, m, v, *, name):
    def body(w_ref, g_ref, m_ref, v_ref, d_ref, nm_ref, nv_ref):
        d_ref[...], nm_ref[...], nv_ref[...] = _adamw_math(w_ref[...], g_ref[...], m_ref[...], v_ref[...])

    return pl.pallas_call(body, name=name, out_shape=[jax.ShapeDtypeStruct(w.shape, F32)] * 3)(w, g, m, v)


class _Exchanges:
    def __init__(self, w, m, v):
        self.w, self.m, self.v = w, m, v
        self.grads, self.delta, self.new_m, self.new_v = {}, {}, {}, {}

    def _adamw(self, names, deps):
        for n in names:
            self.delta[n], self.new_m[n], self.new_v[n] = _adamw(
                self.w[n], self.grads[n], self.m[n], self.v[n], name=f"adamw_{n}", tm=128, deps=deps)

    def begin(self):
        chip = (2 * lax.axis_index("x") + lax.axis_index("y")).astype(jnp.int32).reshape(1)
        w_in = _place_shard(self.w["w_in"], chip, name="place_w_in")
        self.w_in_sems, self.w_in_buf, token = _copies_start("gather_w_in_direct_start", [w_in], _plan_relay_direct, 2)
        self.later_full = [_place_shard(self.w[n], chip, name=f"place_{n}", deps=[token]) for n in _LATER]
        return self.later_full

    def projection(self, h):
        x, y = lax.axis_index("x"), lax.axis_index("y")
        blks = [jnp.asarray(b, jnp.int32).reshape(1)
                for b in (2 * x + y, 2 * (1 - x) + y, 2 * x + (1 - y), 2 * (1 - x) + (1 - y))]
        bufs = self.w_in_buf
        proj = _mm_chip_block(h, bufs[0], blks[0], None, name="mm_proj_own", out_dtype=_MXU)
        relay, token = [], proj
        for k, tag in enumerate(_RELATIONS[:2]):
            bufs = _copies_wait(f"gather_w_in_direct_{tag}_wait", bufs, self.w_in_sems, _plan_relay_direct, [token],
                                which=(k,))
            plan = functools.partial(_plan_relay_forward, k=k)
            sems, bufs, token = _copies_start(f"gather_w_in_relay_{tag}_start", bufs, plan, 2)
            relay.append((sems, plan))
        self.rest = _copies_start("gather_ici_rest_start", self.later_full, _plan_gather_ici, 3 * len(_LATER),
                                  after=[token])
        token = self.rest[2]
        for k, tag in enumerate(_RELATIONS[:2]):
            bufs = _copies_wait(f"gather_w_in_hand_{tag}_wait", bufs, relay[k][0], relay[k][1], [token], which=(1,))
            token = proj = _mm_chip_block(h, bufs[0], blks[1 + k], proj, name=f"mm_proj_{tag}", out_dtype=_MXU)
        for k, tag in enumerate(_RELATIONS[:2]):
            bufs = _copies_wait(f"gather_w_in_relay_{tag}_wait", bufs, relay[k][0], relay[k][1], [token], which=(0,))
        sems, bufs, token = _copies_start("gather_w_in_last_start", bufs, _plan_relay_last, 1)
        bufs = _copies_wait("gather_w_in_last_wait", bufs, sems, _plan_relay_last, [token])
        proj = _mm_chip_block(h, bufs[0], blks[3], proj, name="mm_proj_flip_xy", out_dtype=_MXU)
        return proj, bufs[0]

    def weight(self, name, after):
        if self.rest is not None:
            sems, bufs = self.rest
            later = dict(zip(_LATER, _copies_wait("gather_d2d_rest_wait", bufs, sems, _plan_gather_d2d, [after])))
            later["w_out"] = later["w_out"].reshape(D_MODEL, D_MODEL)
            self.later, self.rest = later, None
        return self.later[name]

    def after_proj(self, proj):
        sems, bufs, _ = self.rest
        bufs = _copies_wait("gather_ici_rest_wait", bufs, sems, _plan_gather_ici, [proj])
        sems, bufs, token = _copies_start("gather_d2d_rest_start", bufs, _plan_gather_d2d, 3 * len(_LATER))
        self.rest = (sems, bufs)
        return [token]

    def later_grads(self, grads):
        self.rs_later = _ReduceScatter("later", _LATER, [grads[n] for n in _LATER])
        return self.rs_later.start_swap()

    def before_scan_backward(self, after):
        return self.rs_later.start_scatter(after)

    def before_input_projection_grad(self, after):
        return self.rs_later.start_join(after)

    def input_projection_grad(self, g_w_in):
        self.grads.update(self.rs_later.finish([g_w_in]))
        self.rs_in = _ReduceScatter("w_in", ["w_in"], [g_w_in])
        self._adamw(_LATER, self.rs_in.start_swap())
        return self.rs_in.start_scatter([self.delta[n] for n in _LATER])

    def _adamw_small(self, names):
        for n in names:
            self.delta[n], self.new_m[n], self.new_v[n] = _adamw_whole(
                self.w[n], self.grads[n], self.m[n], self.v[n], name=f"adamw_{n}")

    def small_grads(self, grads):
        me = (4 * lax.axis_index("x") + 2 * lax.axis_index("y") + lax.axis_index("c")).astype(jnp.int32).reshape(1)
        land = _place_block(_pack_small(grads, _SMALL[1:]), me, name="place_small_grads")
        self.small = _copies_start("gather_small_start", [land], _plan_all_to_all, 7)
        return [self.small[2]]

    def finish(self, g_norm_w, loss, after):
        join = self.rs_in.start_join(after)
        sems, bufs, _ = self.small
        (land,) = _copies_wait("gather_small_wait", bufs, sems, _plan_all_to_all, join)
        self.grads.update(_unpack_small(_sum8(land, name="sum_small_grads"), self.w, _SMALL[1:]))
        self._adamw_small(_SMALL[1:])
        rows = _packed_rows(g_norm_w.size)
        late = jnp.concatenate([_pack_small(dict(norm_w=g_norm_w), _SMALL[:1]),
                                jnp.pad(loss.reshape(1, 1), ((0, SUBLANES - 1), (0, _PACK_W - 1)))], axis=0)
        late = _sum8(_all_gather_small(late), name="sum_norm_w_grad_and_loss")
        self.grads.update(_unpack_small(late[:rows], self.w, _SMALL[:1]))
        self._adamw_small(_SMALL[:1])
        self.grads.update(self.rs_in.finish([self.delta[_SMALL[0]]]))
        self._adamw(["w_in"], ())
        return late[rows, 0]


def kernel(x, norm_w, w_in, q_norm_w, k_norm_w, sinks, w_attn_proj, A_re, A_im, log_dt, B_re, B_im, C_re, C_im, D_skip, w_glu, b_glu, w_ssm_proj, w_out, loss_target, m_norm_w, m_w_in, m_q_norm_w, m_k_norm_w, m_sinks, m_w_attn_proj, m_A_re, m_A_im, m_log_dt, m_B_re, m_B_im, m_C_re, m_C_im, m_D_skip, m_w_glu, m_b_glu, m_w_ssm_proj, m_w_out, v_norm_w, v_w_in, v_q_norm_w, v_k_norm_w, v_sinks, v_w_attn_proj, v_A_re, v_A_im, v_log_dt, v_B_re, v_B_im, v_C_re, v_C_im, v_D_skip, v_w_glu, v_b_glu, v_w_ssm_proj, v_w_out):
    w = dict(norm_w=norm_w, w_in=w_in, q_norm_w=q_norm_w, k_norm_w=k_norm_w, sinks=sinks, w_attn_proj=w_attn_proj,
             A_re=A_re, A_im=A_im, log_dt=log_dt, B_re=B_re, B_im=B_im, C_re=C_re, C_im=C_im, D_skip=D_skip,
             w_glu=w_glu, b_glu=b_glu, w_ssm_proj=w_ssm_proj, w_out=w_out)
    m = dict(norm_w=m_norm_w, w_in=m_w_in, q_norm_w=m_q_norm_w, k_norm_w=m_k_norm_w, sinks=m_sinks,
             w_attn_proj=m_w_attn_proj, A_re=m_A_re, A_im=m_A_im, log_dt=m_log_dt, B_re=m_B_re, B_im=m_B_im,
             C_re=m_C_re, C_im=m_C_im, D_skip=m_D_skip, w_glu=m_w_glu, b_glu=m_b_glu, w_ssm_proj=m_w_ssm_proj,
             w_out=m_w_out)
    v = dict(norm_w=v_norm_w, w_in=v_w_in, q_norm_w=v_q_norm_w, k_norm_w=v_k_norm_w, sinks=v_sinks,
             w_attn_proj=v_w_attn_proj, A_re=v_A_re, A_im=v_A_im, log_dt=v_log_dt, B_re=v_B_re, B_im=v_B_im,
             C_re=v_C_re, C_im=v_C_im, D_skip=v_D_skip, w_glu=v_w_glu, b_glu=v_b_glu, w_ssm_proj=v_w_ssm_proj,
             w_out=v_w_out)

    io = _Exchanges(w, m, v)
    loss, grad_x, g_norm_w = _local_step(x[0], loss_target[0], norm_w, q_norm_w, k_norm_w, sinks, A_re, A_im, log_dt,
                                         B_re, B_im, C_re, C_im, D_skip, b_glu, io)
    loss = io.finish(g_norm_w, loss, [grad_x])
    grads, delta, new_m, new_v = io.grads, io.delta, io.new_m, io.new_v

    return (loss, grad_x[None], *[grads[n] for n in _ORDER], *[delta[n] for n in _ORDER],
            *[new_m[n] for n in _ORDER], *[new_v[n] for n in _ORDER])
```

```python
import functools
import math

import jax
import jax.numpy as jnp
from jax import lax
from jax.experimental import pallas as pl
from jax.experimental.pallas import tpu as pltpu

F32 = jnp.float32
_MXU = jnp.bfloat16
_WIRE = jnp.bfloat16

LANES = 128
SUBLANES = 8
VMEM_LIMIT = 56 * 1024 * 1024

D_MODEL = 2048
HEAD_DIM = 64
N_Q_HEADS = 16
N_KV_HEADS = 4
Q_PER_KV = 4
ATTN_W = 1024
KV_W = 256
WINDOW = 128
SSM_W = 1024
GROUP = 16
N_GROUPS = 64
STATE = 64
N_STATES = N_GROUPS * STATE
IN_W = 8704
NORM_EPS = 1e-6
N_CHIPS = 4
CW = 512
OFF_AGATE, OFF_U, OFF_Z, OFF_GA, OFF_GS = 3, 5, 7, 9, 13

SSM_T = 256
SSM_L = SSM_T // SUBLANES
SSM_JB = 8
SSM_SB = N_STATES // SSM_JB

ADAM_LR, ADAM_B1, ADAM_B2, ADAM_EPS, ADAM_WD, ADAM_STEP = 0.001, 0.9, 0.999, 1e-08, 0.01, 10

MESH = pl.DeviceIdType.MESH
_ANY = pl.BlockSpec(memory_space=pl.ANY)


def _params(sem=None):
    return pltpu.CompilerParams(dimension_semantics=sem, vmem_limit_bytes=VMEM_LIMIT)


def _mm(a, b, *, mode, name, tm, tn, tk, out_dtype=F32, b_blocked=False, out_blocked=False, rows_outer=False,
        deps=()):
    nd = len(deps)
    if mode == "tn":
        K, M = a.shape
    else:
        M, K = a.shape
    if mode == "nn":
        N = b.shape[0] * b.shape[2] if b_blocked else b.shape[1]
    elif mode == "nt":
        N = b.shape[1] if b_blocked else b.shape[0]
    else:
        N = b.shape[1]
    tm, tn, tk = min(tm, M), min(tn, N), min(tk, K)
    nj, ni, nk = N // tn, M // tm, K // tk
    assert nj * tn == N and ni * tm == M and nk * tk == K, (name, M, N, K)
    dims = {"nn": (((1,), (0,)), ((), ())), "nt": (((1,), (1,)), ((), ())), "tn": (((0,), (0,)), ((), ()))}[mode]

    if mode == "tn":
        a_spec = pl.BlockSpec((tk, tm), lambda j, i, k: (k, i))
    else:
        a_spec = pl.BlockSpec((tm, tk), lambda j, i, k: (i, k))
    if mode == "nn":
        if b_blocked:
            assert b.shape[0] == nj and b.shape[2] == tn
            b_spec = pl.BlockSpec((None, tk, tn), lambda j, i, k: (j, k, 0))
        else:
            b_spec = pl.BlockSpec((tk, tn), lambda j, i, k: (k, j))
    elif mode == "nt":
        if b_blocked:
            assert b.shape[0] == nk and b.shape[2] == tk
            b_spec = pl.BlockSpec((None, tn, tk), lambda j, i, k: (k, j, 0))
        else:
            b_spec = pl.BlockSpec((tn, tk), lambda j, i, k: (j, k))
    else:
        b_spec = pl.BlockSpec((tk, tn), lambda j, i, k: (k, j))
    whole_out = out_blocked and nj == 1
    if whole_out:
        assert ni == 1
        o_spec = pl.BlockSpec((N_CHIPS, tm, tn // N_CHIPS), lambda j, i, k: (0, 0, 0))
        o_shape = jax.ShapeDtypeStruct((N_CHIPS, M, tn // N_CHIPS), out_dtype)
    elif out_blocked:
        assert nj == N_CHIPS
        o_spec = pl.BlockSpec((None, tm, tn), lambda j, i, k: (j, i, 0))
        o_shape = jax.ShapeDtypeStruct((nj, M, tn), out_dtype)
    else:
        o_spec = pl.BlockSpec((tm, tn), lambda j, i, k: (i, j))
        o_shape = jax.ShapeDtypeStruct((M, N), out_dtype)
    use_acc = nk > 1 and (out_dtype != F32 or whole_out)

    def body(a_ref, b_ref, *rest):
        o_ref, scratch = rest[nd], rest[nd + 1:]

        def product():
            return lax.dot_general(a_ref[...].astype(_MXU), b_ref[...].astype(_MXU), dims, preferred_element_type=F32)

        def write(result):
            if whole_out:
                w = tn // N_CHIPS
                for c in range(N_CHIPS):
                    o_ref[c] = result[:, c * w:(c + 1) * w].astype(o_ref.dtype)
            else:
                o_ref[...] = result.astype(o_ref.dtype)

        if nk == 1:
            write(product())
            return
        k = pl.program_id(2)
        acc = scratch[0] if use_acc else o_ref

        @pl.when(k == 0)
        def _():
            acc[...] = jnp.zeros_like(acc)

        acc[...] += product()

        if use_acc:
            @pl.when(k == nk - 1)
            def _():
                write(acc[...])

    specs = [a_spec, b_spec, o_spec]
    grid = (nj, ni, nk)
    if rows_outer:
        specs = [pl.BlockSpec(s.block_shape, lambda i, j, k, f=s.index_map: f(j, i, k)) for s in specs]
        grid = (ni, nj, nk)
    return pl.pallas_call(
        body, name=name, grid=grid, in_specs=specs[:2] + [_ANY] * nd, out_specs=specs[2],
        out_shape=o_shape, scratch_shapes=[pltpu.VMEM((tm, tn), F32)] if use_acc else [],
        compiler_params=_params(("parallel", "parallel", "arbitrary")),
    )(a, b, *deps)


def _mm_chip_block(a, b4, blk, prev, *, name, tm=1024, out_dtype=F32, deps=()):
    M, K = a.shape
    nchip, _, C = b4.shape
    tm = min(tm, M)
    extra = ([] if prev is None else [prev]) + list(deps)

    def body(blk_ref, a_ref, b_ref, *rest):
        rest[-1][...] = jnp.dot(a_ref[...].astype(_MXU), b_ref[...].astype(_MXU),
                                preferred_element_type=F32).astype(rest[-1].dtype)

    return pl.pallas_call(
        body, name=name,
        grid_spec=pltpu.PrefetchScalarGridSpec(
            num_scalar_prefetch=1, grid=(M // tm,),
            in_specs=[pl.BlockSpec((tm, K), lambda i, c: (i, 0)), pl.BlockSpec((None, K, C), lambda i, c: (c[0], 0, 0))]
            + [_ANY] * len(extra),
            out_specs=pl.BlockSpec((tm, C), lambda i, c: (i, c[0]))),
        out_shape=jax.ShapeDtypeStruct((M, nchip * C), out_dtype),
        input_output_aliases={} if prev is None else {3: 0},
        compiler_params=_params(("arbitrary",)),
    )(blk, a, b4, *extra)


def _mm_merge_out_loss(proj, y_a, y_s, w_out, x, target, *, tm=256):
    rows, d = x.shape
    ncol = d // CW

    def body(*refs):
        ga_refs, gs_refs = refs[:ncol], refs[ncol:2 * ncol]
        ya_ref, ys_ref, w_ref, x_ref, t_ref, m_ref, d_ref, db_ref, sq_ref = refs[2 * ncol:]
        for j in range(ncol):
            cols = slice(j * CW, (j + 1) * CW)
            m_ref[:, cols] = (_sigmoid(ga_refs[j][...].astype(F32)) * ya_ref[:, cols].astype(F32)
                              + _sigmoid(gs_refs[j][...].astype(F32)) * ys_ref[:, cols].astype(F32)).astype(m_ref.dtype)
        mo = jnp.dot(m_ref[...], w_ref[...].astype(_MXU), preferred_element_type=F32)
        err = (x_ref[...] + mo) - t_ref[...]
        dout = err * (1.0 / d)
        d_ref[...] = dout
        db_ref[...] = dout.astype(db_ref.dtype)
        part = _colsum(err * err)
        i = pl.program_id(0)

        @pl.when(i == 0)
        def _():
            sq_ref[...] = part

        @pl.when(i > 0)
        def _():
            sq_ref[...] += part

    tile = pl.BlockSpec((tm, d), lambda i: (i, 0))
    gate = [pl.BlockSpec((tm, CW), lambda i, c=off + j: (i, c)) for off in (OFF_GA, OFF_GS) for j in range(ncol)]
    return pl.pallas_call(
        body, name="mm_merge_out_loss", grid=(rows // tm,),
        in_specs=gate + [tile, tile, pl.BlockSpec((d, d), lambda i: (0, 0), pipeline_mode=pl.Buffered(1)), tile, tile],
        out_specs=[tile, tile, tile, pl.BlockSpec((1, d), lambda i: (0, 0))],
        out_shape=[jax.ShapeDtypeStruct((rows, d), _MXU), jax.ShapeDtypeStruct((rows, d), F32),
                   jax.ShapeDtypeStruct((rows, d), _MXU), jax.ShapeDtypeStruct((1, d), F32)],
        compiler_params=_params(("arbitrary",)),
    )(*([proj] * (2 * ncol)), y_a, y_s, w_out, x, target)


def _mm_merge_bwd(dout_b, w_out, proj, y_a, y_s, *, tm=512):
    rows, d = y_a.shape
    ncol = d // CW

    def body(do_ref, w_ref, *refs):
        ga_refs, gs_refs = refs[:ncol], refs[ncol:2 * ncol]
        ya_ref, ys_ref, dya_ref, dys_ref, dg_ref = refs[2 * ncol:]
        dm = lax.dot_general(do_ref[...].astype(_MXU), w_ref[...].astype(_MXU), _NT, preferred_element_type=F32)
        for j in range(ncol):
            cols = slice(j * CW, (j + 1) * CW)
            dmj = dm[:, cols]
            sa, ss = _sigmoid(ga_refs[j][...].astype(F32)), _sigmoid(gs_refs[j][...].astype(F32))
            dya_ref[:, cols] = (sa * dmj).astype(dya_ref.dtype)
            dys_ref[:, cols] = (ss * dmj).astype(dys_ref.dtype)
            dg_ref[:, cols] = (dmj * ya_ref[:, cols].astype(F32) * sa * (1.0 - sa)).astype(dg_ref.dtype)
            dg_ref[:, d + j * CW:d + (j + 1) * CW] = (dmj * ys_ref[:, cols].astype(F32) * ss
                                                      * (1.0 - ss)).astype(dg_ref.dtype)

    tile = pl.BlockSpec((tm, d), lambda i: (i, 0))
    gate = [pl.BlockSpec((tm, CW), lambda i, c=off + j: (i, c)) for off in (OFF_GA, OFF_GS) for j in range(ncol)]
    both = pl.BlockSpec((pl.Element(tm), pl.Element(2 * d)), lambda i: (i * tm, OFF_GA * CW))
    return pl.pallas_call(
        body, name="mm_merge_bwd", grid=(rows // tm,),
        in_specs=[tile, pl.BlockSpec((d, d), lambda i: (0, 0), pipeline_mode=pl.Buffered(1))] + gate + [tile, tile],
        out_specs=[tile, tile, both],
        out_shape=[jax.ShapeDtypeStruct((rows, d), _MXU)] * 2 + [jax.ShapeDtypeStruct((rows, IN_W), _MXU)],
        compiler_params=_params(("arbitrary",)),
    )(dout_b, w_out, *([proj] * (2 * ncol)), y_a, y_s)


def _mm_glu_gate(yg, w_glu4, b_glu, proj, *, tm=1024):
    rows, k = yg.shape
    nj, _, tn = w_glu4.shape
    w = nj * tn // 2
    tm = min(tm, rows)

    def body(a_ref, w_ref, ba_ref, bb_ref, z0_ref, z1_ref, glu_ref, ys_ref):
        j = pl.program_id(1)
        for c in range(nj):
            @pl.when(j == c)
            def _(c=c):
                glu_ref[:, c * tn:(c + 1) * tn] = jnp.dot(a_ref[...].astype(_MXU), w_ref[...].astype(_MXU),
                                                          preferred_element_type=F32).astype(glu_ref.dtype)

        @pl.when(j == nj - 1)
        def _():
            z = jnp.concatenate([z0_ref[...], z1_ref[...]], axis=1).astype(F32)
            ys_ref[...] = ((glu_ref[:, :w].astype(F32) + ba_ref[...]) * _sigmoid(glu_ref[:, w:].astype(F32) + bb_ref[...])
                           * (z * _sigmoid(z))).astype(ys_ref.dtype)

    bias = lambda c: pl.BlockSpec((1, w), lambda i, j: (0, c))
    zcol = lambda c: pl.BlockSpec((tm, CW), lambda i, j: (i, OFF_Z + c))
    return pl.pallas_call(
        body, name="mm_glu_gate", grid=(rows // tm, nj),
        in_specs=[pl.BlockSpec((tm, k), lambda i, j: (i, 0)), pl.BlockSpec((None, k, tn), lambda i, j: (j, 0, 0)),
                  bias(0), bias(1), zcol(0), zcol(1)],
        out_specs=[pl.BlockSpec((tm, nj * tn), lambda i, j: (i, 0)), pl.BlockSpec((tm, w), lambda i, j: (i, 0))],
        out_shape=[jax.ShapeDtypeStruct((rows, nj * tn), _MXU), jax.ShapeDtypeStruct((rows, w), _MXU)],
        compiler_params=_params(("arbitrary", "arbitrary")),
    )(yg, w_glu4, b_glu, b_glu, proj, proj)


def _mm_ssm_gate_bwd(d_ys, w_sp4, glu, b_glu, proj, d_proj, *, tm=1024):
    rows, w = glu.shape[0], glu.shape[1] // 2
    nk, tk = w_sp4.shape[0], w_sp4.shape[2]
    tm = min(tm, rows)

    def body(dy_ref, w_ref, ga_ref, gb_ref, ba_ref, bb_ref, z0_ref, z1_ref, buf_ref, dg_ref, dz_ref, db_ref, acc):
        i, k = pl.program_id(0), pl.program_id(1)

        @pl.when(k == 0)
        def _():
            acc[...] = jnp.zeros_like(acc)

        acc[...] += lax.dot_general(dy_ref[...].astype(_MXU), w_ref[...].astype(_MXU), _NT, preferred_element_type=F32)

        @pl.when(k == nk - 1)
        def _():
            dv = acc[...]
            a, sb = ga_ref[...].astype(F32) + ba_ref[...], _sigmoid(gb_ref[...].astype(F32) + bb_ref[...])
            f, df = _silu_and_grad(jnp.concatenate([z0_ref[...], z1_ref[...]], axis=1).astype(F32))
            dga = dv * sb * f
            dgb = dv * a * f * sb * (1.0 - sb)
            dg_ref[:, :w] = dga.astype(dg_ref.dtype)
            dg_ref[:, w:] = dgb.astype(dg_ref.dtype)
            dz_ref[...] = (dv * a * sb * df).astype(dz_ref.dtype)
            part = jnp.concatenate([_colsum(dga), _colsum(dgb)], axis=1)

            @pl.when(i == 0)
            def _():
                db_ref[...] = part

            @pl.when(i > 0)
            def _():
                db_ref[...] += part

    half = lambda c: pl.BlockSpec((tm, w), lambda i, k: (i, c))
    bias = lambda c: pl.BlockSpec((1, w), lambda i, k: (0, c))
    zcol = lambda c: pl.BlockSpec((tm, CW), lambda i, k: (i, OFF_Z + c))
    return pl.pallas_call(
        body, name="mm_ssm_gate_bwd", grid=(rows // tm, nk),
        in_specs=[pl.BlockSpec((tm, tk), lambda i, k: (i, k)), pl.BlockSpec((None, w, tk), lambda i, k: (k, 0, 0)),
                  half(0), half(1), bias(0), bias(1), zcol(0), zcol(1), _ANY],
        out_specs=[pl.BlockSpec((tm, 2 * w), lambda i, k: (i, 0)),
                   pl.BlockSpec((pl.Element(tm), pl.Element(w)), lambda i, k: (i * tm, OFF_Z * CW)),
                   pl.BlockSpec((1, 2 * w), lambda i, k: (0, 0))],
        out_shape=[jax.ShapeDtypeStruct((rows, 2 * w), _MXU), jax.ShapeDtypeStruct(d_proj.shape, d_proj.dtype),
                   jax.ShapeDtypeStruct((1, 2 * w), F32)],
        input_output_aliases={8: 1},
        scratch_shapes=[pltpu.VMEM((tm, w), F32)],
        compiler_params=_params(("arbitrary", "arbitrary")),
    )(d_ys, w_sp4, glu, glu, b_glu, b_glu, proj, proj, d_proj)


def _colsum(v):
    return jnp.sum(v, axis=0, keepdims=True)


def _sigmoid(v):
    return jax.nn.sigmoid(v)


def _silu_and_grad(v):
    s = _sigmoid(v)
    return v * s, s * (1.0 + v * (1.0 - s))


def _rms_fwd(x, w, *, tm=512, deps=()):
    rows, d = x.shape
    nd = len(deps)

    def body(x_ref, w_ref, *rest):
        h_ref, r_ref = rest[nd:]
        xv = x_ref[...]
        r = lax.rsqrt(jnp.mean(xv * xv, axis=-1, keepdims=True) + NORM_EPS)
        h_ref[...] = (xv * r * w_ref[...]).astype(h_ref.dtype)
        r_ref[...] = r

    return pl.pallas_call(
        body, name="rms_fwd", grid=(rows // tm,),
        in_specs=[pl.BlockSpec((tm, d), lambda i: (i, 0)), pl.BlockSpec((1, d), lambda i: (0, 0))] + [_ANY] * nd,
        out_specs=[pl.BlockSpec((tm, d), lambda i: (i, 0)), pl.BlockSpec((tm, 1), lambda i: (i, 0))],
        out_shape=[jax.ShapeDtypeStruct((rows, d), _MXU), jax.ShapeDtypeStruct((rows, 1), F32)],
        compiler_params=_params(("arbitrary",)),
    )(x, w, *deps)


def _rms_bwd(dh, x, rstd, w, dout, *, tm=256):
    rows, d = x.shape

    def body(dh_ref, x_ref, r_ref, w_ref, do_ref, gx_ref, gw_ref):
        dhv, xv, r, wv = dh_ref[...], x_ref[...], r_ref[...], w_ref[...]
        xr = xv * r
        t = jnp.mean(dhv * wv * xr, axis=-1, keepdims=True)
        gx_ref[...] = do_ref[...] + r * (wv * dhv - xr * t)
        part = _colsum(dhv * xr)
        i = pl.program_id(0)

        @pl.when(i == 0)
        def _():
            gw_ref[...] = part

        @pl.when(i > 0)
        def _():
            gw_ref[...] += part

    return pl.pallas_call(
        body, name="rms_bwd", grid=(rows // tm,),
        in_specs=[pl.BlockSpec((tm, d), lambda i: (i, 0)), pl.BlockSpec((tm, d), lambda i: (i, 0)),
                  pl.BlockSpec((tm, 1), lambda i: (i, 0)), pl.BlockSpec((1, d), lambda i: (0, 0)),
                  pl.BlockSpec((tm, d), lambda i: (i, 0))],
        out_specs=[pl.BlockSpec((tm, d), lambda i: (i, 0)), pl.BlockSpec((1, d), lambda i: (0, 0))],
        out_shape=[jax.ShapeDtypeStruct((rows, d), F32), jax.ShapeDtypeStruct((1, d), F32)],
        compiler_params=_params(("arbitrary",)),
    )(dh, x, rstd, w, dout)


_NT = (((1,), (1,)), ((), ()))
_TN = (((0,), (0,)), ((), ()))


QKV_W = ATTN_W + 2 * KV_W
HEADS_PER_TILE = LANES // HEAD_DIM


def _low_half(rows):
    return lax.broadcasted_iota(jnp.int32, (rows, LANES), 1) < HEAD_DIM


def _pair_mean(t, low):
    m_lo = jnp.sum(jnp.where(low, t, 0.0), axis=-1, keepdims=True)
    m_hi = jnp.sum(jnp.where(low, 0.0, t), axis=-1, keepdims=True)
    return jnp.where(low, m_lo, m_hi) * (1.0 / HEAD_DIM)


def _pair_rstd(t, low):
    return lax.rsqrt(_pair_mean(t * t, low) + NORM_EPS)


def _dup_half(t, hi, low):
    swapped = pltpu.roll(t, HEAD_DIM, 1)
    return jnp.where(low, swapped, t) if hi else jnp.where(low, t, swapped)


def _fold_halves(t):
    return t + pltpu.roll(t, HEAD_DIM, 1)


def _split_heads(t, low):
    return [jnp.where(low, t, 0.0), jnp.where(low, 0.0, t)]


def _stacked_band_mask(n):
    rows = Q_PER_KV * WINDOW
    qi = lax.broadcasted_iota(jnp.int32, (rows, 2 * WINDOW), 0) % WINDOW + WINDOW
    kj = lax.broadcasted_iota(jnp.int32, (rows, 2 * WINDOW), 1)
    diff = qi - kj
    first_key = jnp.where(n > 0, 0, WINDOW)
    return (diff >= 0) & (diff < WINDOW) & (kj >= first_key)


def _stacked_sinks(sink_ref, g):
    blk = lax.broadcasted_iota(jnp.int32, (Q_PER_KV * WINDOW, 1), 0) // WINDOW
    col = jnp.full((Q_PER_KV * WINDOW, 1), sink_ref[Q_PER_KV * g], F32)
    for r in range(1, Q_PER_KV):
        col = jnp.where(blk == r, sink_ref[Q_PER_KV * g + r], col)
    return col


def _attn_in_specs(nblk, rev):
    def cur(n):
        return (nblk - 1 - n) if rev else n

    q_spec = pl.BlockSpec((WINDOW, ATTN_W), lambda n: (cur(n), 0))
    kvc_spec = pl.BlockSpec((WINDOW, 2 * KV_W), lambda n: (cur(n), ATTN_W // (2 * KV_W)))
    kvp_spec = pl.BlockSpec((WINDOW, 2 * KV_W), lambda n: (jnp.maximum(cur(n) - 1, 0), ATTN_W // (2 * KV_W)))
    w_spec = pl.BlockSpec((1, LANES), lambda n: (0, 0))
    l_spec = pl.BlockSpec((WINDOW, N_Q_HEADS), lambda n: (cur(n), 0))
    gate_specs = [pl.BlockSpec((WINDOW, CW), lambda n, col=OFF_AGATE + j: (cur(n), col)) for j in range(ATTN_W // CW)]
    return q_spec, kvc_spec, kvp_spec, w_spec, l_spec, gate_specs


def _attn2_fwd(proj, qw2, kw2, sinks, deps=()):
    seq = proj.shape[0]
    nblk = seq // WINDOW
    scale = 1.0 / math.sqrt(HEAD_DIM)
    q_spec, kvc_spec, kvp_spec, w_spec, l_spec, gate_specs = _attn_in_specs(nblk, False)
    nd, ng = len(deps), len(gate_specs)

    def body(sink_ref, q_ref, kvc_ref, kvp_ref, qw_ref, kw_ref, *rest):
        gate_refs = rest[:ng]
        o_ref, lse_ref, ya_ref = rest[ng + nd:]
        n = pl.program_id(0)
        low, low2 = _low_half(WINDOW), _low_half(2 * WINDOW)
        valid = _stacked_band_mask(n)
        head_lane = lax.broadcasted_iota(jnp.int32, (WINDOW, N_Q_HEADS), 1)
        kv = jnp.concatenate([kvp_ref[...], kvc_ref[...]], axis=0).astype(F32)
        qwv, kwv = qw_ref[...], kw_ref[...]
        lse_blk = jnp.zeros((WINDOW, N_Q_HEADS), F32)
        for t in range(N_KV_HEADS // HEADS_PER_TILE):
            kt = kv[:, t * LANES:(t + 1) * LANES]
            vt = kv[:, KV_W + t * LANES:KV_W + (t + 1) * LANES]
            kn = kt * _pair_rstd(kt, low2) * kwv
            for hi in range(HEADS_PER_TILE):
                g = HEADS_PER_TILE * t + hi
                kdup = _dup_half(kn, hi, low2).astype(_MXU)
                vdup = _dup_half(vt, hi, low2).astype(_MXU)
                stack = []
                for tq in (2 * g, 2 * g + 1):
                    qt = q_ref[:, tq * LANES:(tq + 1) * LANES].astype(F32)
                    stack += _split_heads(qt * _pair_rstd(qt, low) * qwv, low)
                qs = jnp.concatenate(stack, axis=0).astype(_MXU)
                s = lax.dot_general(qs, kdup, _NT, preferred_element_type=F32) * scale
                s = jnp.where(valid, s, -1e30)
                sink = _stacked_sinks(sink_ref, g)
                m = jnp.maximum(jnp.max(s, axis=-1, keepdims=True), sink)
                e = jnp.exp(s - m)
                z = jnp.sum(e, axis=-1, keepdims=True) + jnp.exp(sink - m)
                o = jnp.dot((e / z).astype(_MXU), vdup, preferred_element_type=F32)
                for i, tq in enumerate((2 * g, 2 * g + 1)):
                    tile = slice(tq * LANES, (tq + 1) * LANES)
                    out = jnp.where(low, o[2 * i * WINDOW:(2 * i + 1) * WINDOW],
                                    o[(2 * i + 1) * WINDOW:(2 * i + 2) * WINDOW])
                    gate = gate_refs[tq * LANES // CW][:, tq * LANES % CW:tq * LANES % CW + LANES].astype(F32)
                    o_ref[:, tile] = out.astype(o_ref.dtype)
                    ya_ref[:, tile] = (out * (gate * _sigmoid(gate))).astype(ya_ref.dtype)
                lse = m + jnp.log(z)
                for r in range(Q_PER_KV):
                    lse_blk = jnp.where(head_lane == Q_PER_KV * g + r, lse[r * WINDOW:(r + 1) * WINDOW], lse_blk)
        lse_ref[...] = lse_blk

    return pl.pallas_call(
        body, name="attn_fwd", grid=(nblk,),
        in_specs=[pl.BlockSpec(memory_space=pltpu.SMEM), q_spec, kvc_spec, kvp_spec, w_spec, w_spec] + gate_specs
        + [_ANY] * nd,
        out_specs=[q_spec, l_spec, q_spec],
        out_shape=[jax.ShapeDtypeStruct((seq, ATTN_W), _MXU), jax.ShapeDtypeStruct((seq, N_Q_HEADS), F32),
                   jax.ShapeDtypeStruct((seq, ATTN_W), _MXU)],
        compiler_params=_params(("arbitrary",)),
    )(sinks, proj, proj, proj, qw2, kw2, *([proj] * ng), *deps)


def _attn2_bwd(proj, qw2, kw2, sinks, lse, attn, dya, d_proj, deps=()):
    seq = proj.shape[0]
    nblk = seq // WINDOW
    scale = 1.0 / math.sqrt(HEAD_DIM)
    q_spec, kvc_spec, kvp_spec, w_spec, l_spec, gate_specs = _attn_in_specs(nblk, True)
    s_spec = pl.BlockSpec((1, N_Q_HEADS), lambda n: (0, 0))
    d_spec = pl.BlockSpec((WINDOW, QKV_W + ATTN_W), lambda n: (nblk - 1 - n, 0))
    deps = list(deps) + [d_proj]
    nd, ng = len(deps), len(gate_specs)

    def body(sink_ref, q_ref, kvc_ref, kvp_ref, qw_ref, kw_ref, lse_ref, attn_ref, dya_ref, *rest):
        gate_refs = rest[:ng]
        d_ref, dqw_ref, dkw_ref, dsk_ref, carry, do_ref = rest[ng + nd:]
        step = pl.program_id(0)
        n = nblk - 1 - step

        @pl.when(step == 0)
        def _():
            carry[...] = jnp.zeros_like(carry)
            dqw_ref[...] = jnp.zeros_like(dqw_ref)
            dkw_ref[...] = jnp.zeros_like(dkw_ref)
            dsk_ref[...] = jnp.zeros_like(dsk_ref)

        for j, g_ref in enumerate(gate_refs):
            cols = slice(j * CW, (j + 1) * CW)
            f, df = _silu_and_grad(g_ref[...].astype(F32))
            dv = dya_ref[:, cols]
            do_ref[:, cols] = dv * f
            d_ref[:, QKV_W + j * CW:QKV_W + (j + 1) * CW] = (dv * attn_ref[:, cols].astype(F32) * df).astype(d_ref.dtype)

        low, low2 = _low_half(WINDOW), _low_half(2 * WINDOW)
        valid = _stacked_band_mask(n)
        head_lane = lax.broadcasted_iota(jnp.int32, (WINDOW, N_Q_HEADS), 1)
        sink_lane = lax.broadcasted_iota(jnp.int32, (1, N_Q_HEADS), 1)
        kv = jnp.concatenate([kvp_ref[...], kvc_ref[...]], axis=0).astype(F32)
        qwv, kwv = qw_ref[...], kw_ref[...]
        lse_blk = lse_ref[...]
        dqw = jnp.zeros((1, LANES), F32)
        dkw = jnp.zeros((1, LANES), F32)
        dsk = jnp.zeros((1, N_Q_HEADS), F32)
        for t in range(N_KV_HEADS // HEADS_PER_TILE):
            kt = kv[:, t * LANES:(t + 1) * LANES]
            vt = kv[:, KV_W + t * LANES:KV_W + (t + 1) * LANES]
            rk = _pair_rstd(kt, low2)
            kn = kt * rk * kwv
            dkn_t = jnp.zeros((2 * WINDOW, LANES), F32)
            dv_t = jnp.zeros((2 * WINDOW, LANES), F32)
            for hi in range(HEADS_PER_TILE):
                g = HEADS_PER_TILE * t + hi
                kdup = _dup_half(kn, hi, low2).astype(_MXU)
                vdup = _dup_half(vt, hi, low2).astype(_MXU)
                tiles = (2 * g, 2 * g + 1)
                qx, rq, stack, dstack, lse_rows = [], [], [], [], []
                for tq in tiles:
                    qt = q_ref[:, tq * LANES:(tq + 1) * LANES].astype(F32)
                    r = _pair_rstd(qt, low)
                    rq.append(r)
                    qx.append(qt * r)
                    stack += _split_heads(qx[-1] * qwv, low)
                    dstack += _split_heads(do_ref[:, tq * LANES:(tq + 1) * LANES], low)
                for r in range(Q_PER_KV):
                    lse_rows.append(jnp.sum(jnp.where(head_lane == Q_PER_KV * g + r, lse_blk, 0.0), axis=-1, keepdims=True))
                qs = jnp.concatenate(stack, axis=0).astype(_MXU)
                dos = jnp.concatenate(dstack, axis=0).astype(_MXU)
                lse_col = jnp.concatenate(lse_rows, axis=0)
                s = lax.dot_general(qs, kdup, _NT, preferred_element_type=F32) * scale
                s = jnp.where(valid, s, -1e30)
                p = jnp.exp(s - lse_col)
                dp = lax.dot_general(dos, vdup, _NT, preferred_element_type=F32)
                dsum = jnp.sum(p * dp, axis=-1, keepdims=True)
                ds = (p * (dp - dsum) * scale).astype(_MXU)
                dsink = -jnp.exp(_stacked_sinks(sink_ref, g) - lse_col) * dsum
                for r in range(Q_PER_KV):
                    dsk = dsk + jnp.where(sink_lane == Q_PER_KV * g + r, _colsum(dsink[r * WINDOW:(r + 1) * WINDOW]), 0.0)
                dv_g = _fold_halves(lax.dot_general(p.astype(_MXU), dos, _TN, preferred_element_type=F32))
                dkn_g = _fold_halves(lax.dot_general(ds, qs, _TN, preferred_element_type=F32))
                dv_t = jnp.where(low2, dv_t, dv_g) if hi else jnp.where(low2, dv_g, dv_t)
                dkn_t = jnp.where(low2, dkn_t, dkn_g) if hi else jnp.where(low2, dkn_g, dkn_t)
                dqn = jnp.dot(ds, kdup, preferred_element_type=F32)
                for i, tq in enumerate(tiles):
                    dqn_t = jnp.where(low, dqn[2 * i * WINDOW:(2 * i + 1) * WINDOW],
                                      dqn[(2 * i + 1) * WINDOW:(2 * i + 2) * WINDOW])
                    dq = rq[i] * (qwv * dqn_t - qx[i] * _pair_mean(dqn_t * qwv * qx[i], low))
                    d_ref[:, tq * LANES:(tq + 1) * LANES] = dq.astype(d_ref.dtype)
                    dqw = dqw + _colsum(dqn_t * qx[i])
            k_cols = slice(t * LANES, (t + 1) * LANES)
            v_cols = slice(KV_W + t * LANES, KV_W + (t + 1) * LANES)
            dkn_c = dkn_t[WINDOW:] + carry[:, k_cols]
            rc = rk[WINDOW:]
            kx = kt[WINDOW:] * rc
            dk = rc * (kwv * dkn_c - kx * _pair_mean(dkn_c * kwv * kx, low))
            d_ref[:, ATTN_W + t * LANES:ATTN_W + (t + 1) * LANES] = dk.astype(d_ref.dtype)
            d_ref[:, ATTN_W + KV_W + t * LANES:ATTN_W + KV_W + (t + 1) * LANES] = (
                dv_t[WINDOW:] + carry[:, v_cols]).astype(d_ref.dtype)
            carry[:, k_cols] = dkn_t[:WINDOW]
            carry[:, v_cols] = dv_t[:WINDOW]
            dkw = dkw + _colsum(dkn_c * kx)
        dqw_ref[...] += dqw
        dkw_ref[...] += dkw
        dsk_ref[...] += dsk

    return pl.pallas_call(
        body, name="attn_bwd", grid=(nblk,),
        in_specs=[pl.BlockSpec(memory_space=pltpu.SMEM), q_spec, kvc_spec, kvp_spec, w_spec, w_spec, l_spec, q_spec,
                  q_spec] + gate_specs + [_ANY] * nd,
        out_specs=[d_spec, w_spec, w_spec, s_spec],
        out_shape=[jax.ShapeDtypeStruct(d_proj.shape, d_proj.dtype), jax.ShapeDtypeStruct((1, LANES), F32),
                   jax.ShapeDtypeStruct((1, LANES), F32), jax.ShapeDtypeStruct((1, N_Q_HEADS), F32)],
        input_output_aliases={9 + ng + nd - 1: 0},
        scratch_shapes=[pltpu.VMEM((WINDOW, 2 * KV_W), F32), pltpu.VMEM((WINDOW, ATTN_W), F32)],
        compiler_params=_params(("arbitrary",)),
    )(sinks, proj, proj, proj, qw2, kw2, lse, attn, dya, *([proj] * ng), *deps)


def _ssm_discretise(a_re, a_im, log_dt):
    dt = jnp.exp(log_dt)
    mag = jnp.exp(dt * a_re)
    ab_re = mag * jnp.cos(dt * a_im)
    ab_im = mag * jnp.sin(dt * a_im)
    num_re = ab_re - 1.0
    num_im = ab_im
    den = a_re * a_re + a_im * a_im
    cf_re = (num_re * a_re + num_im * a_im) / den
    cf_im = (num_im * a_re - num_re * a_im) / den
    return ab_re, ab_im, cf_re, cf_im


def _ssm_params_fwd(a_re, a_im, log_dt):
    shp = jax.ShapeDtypeStruct(a_re.shape, F32)

    def body(are_ref, aim_ref, ldt_ref, abr_ref, abi_ref, cfr_ref, cfi_ref, alr_ref, ali_ref):
        abr, abi, cfr, cfi = _ssm_discretise(are_ref[...], aim_ref[...], ldt_ref[...])
        abr_ref[...], abi_ref[...], cfr_ref[...], cfi_ref[...] = abr, abi, cfr, cfi
        pr, pi = abr, abi
        for _ in range(int(math.log2(SSM_L))):
            pr, pi = pr * pr - pi * pi, 2.0 * pr * pi
        alr_ref[...], ali_ref[...] = pr, pi

    return pl.pallas_call(body, name="ssm_params_fwd", out_shape=[shp] * 6)(a_re, a_im, log_dt)


def _ssm_params_bwd(a_re, a_im, log_dt, d_abr, d_abi, d_cfr, d_cfi):
    def body(are_ref, aim_ref, ldt_ref, g0, g1, g2, g3, dare_ref, daim_ref, dldt_ref):
        _, vjp = jax.vjp(_ssm_discretise, are_ref[...], aim_ref[...], ldt_ref[...])
        dare_ref[...], daim_ref[...], dldt_ref[...] = vjp((g0[...], g1[...], g2[...], g3[...]))

    return pl.pallas_call(
        body, name="ssm_params_bwd",
        out_shape=[jax.ShapeDtypeStruct(a_re.shape, F32), jax.ShapeDtypeStruct(a_im.shape, F32),
                   jax.ShapeDtypeStruct(log_dt.shape, F32)],
    )(a_re, a_im, log_dt, d_abr, d_abi, d_cfr, d_cfi)


def _scan_cols(j):
    return pl.ds(j * SSM_SB, SSM_SB)


def _rows8(r):
    return pl.ds(pl.multiple_of(r * SUBLANES, SUBLANES), SUBLANES)


def _bcast8(row):
    return jnp.broadcast_to(row, (SUBLANES, row.shape[-1]))


def _token_order_pick():
    tok = lax.broadcasted_iota(jnp.int32, (SSM_T, SSM_T), 0)
    row = lax.broadcasted_iota(jnp.int32, (SSM_T, SSM_T), 1)
    return (row == SUBLANES * (tok % SSM_L) + tok // SSM_L).astype(_MXU)


SCAN_UNROLL = 16


def _scan_loop(n, step, init):
    def trip(o, carry):
        for i in range(SCAN_UNROLL):
            carry = step(o * SCAN_UNROLL + i, carry)
        return carry

    return lax.fori_loop(0, n // SCAN_UNROLL, trip, init)


def _ssm_fwd(u, b_re, b_im, c_re, c_im, d_skip, coef):
    seq = u.shape[0]
    nc = seq // SSM_T
    T, L = SSM_T, SSM_L

    def body(u_ref, bre_ref, bim_ref, cre_ref, cim_ref, d_ref, are_ref, aim_ref, cfr_ref, cfi_ref, alr_ref, ali_ref,
             y_ref, yg_ref, sre_ref, sim_ref, ire_ref, iim_ref, car_re, car_im, end_re, end_im, yg_scan):
        c = pl.program_id(0)

        @pl.when(c == 0)
        def _():
            car_re[...] = jnp.zeros_like(car_re)
            car_im[...] = jnp.zeros_like(car_im)

        for j in range(SSM_JB):
            ub = u_ref[:, j * LANES:(j + 1) * LANES].astype(_MXU)
            bur = jnp.dot(ub, bre_ref[j], preferred_element_type=F32)
            bui = jnp.dot(ub, bim_ref[j], preferred_element_type=F32)
            cfr, cfi = cfr_ref[:, _scan_cols(j)], cfi_ref[:, _scan_cols(j)]
            sre_ref[:, _scan_cols(j)] = cfr * bur - cfi * bui
            sim_ref[:, _scan_cols(j)] = cfr * bui + cfi * bur

        for j in range(SSM_JB):
            cols = _scan_cols(j)
            ar, ai = _bcast8(are_ref[:, cols]), _bcast8(aim_ref[:, cols])

            def step1(r, s, cols=cols, ar=ar, ai=ai):
                sr, si = s
                rows = _rows8(r)
                return (ar * sr - ai * si + sre_ref[rows, cols], ar * si + ai * sr + sim_ref[rows, cols])

            zero = jnp.zeros((SUBLANES, SSM_SB), F32)
            er, ei = _scan_loop(L, step1, (zero, zero))
            end_re[:, cols] = er
            end_im[:, cols] = ei

        alr, ali = alr_ref[...], ali_ref[...]
        cr, ci = car_re[...], car_im[...]
        ire_ref[0:1, :] = cr
        iim_ref[0:1, :] = ci
        for i in range(1, SUBLANES):
            er, ei = end_re[i - 1:i, :], end_im[i - 1:i, :]
            cr, ci = alr * cr - ali * ci + er, alr * ci + ali * cr + ei
            ire_ref[i:i + 1, :] = cr
            iim_ref[i:i + 1, :] = ci

        for j in range(SSM_JB):
            cols = _scan_cols(j)
            ar, ai = _bcast8(are_ref[:, cols]), _bcast8(aim_ref[:, cols])

            def step2(r, s, cols=cols, ar=ar, ai=ai):
                sr, si = s
                rows = _rows8(r)
                nr = ar * sr - ai * si + sre_ref[rows, cols]
                ni = ar * si + ai * sr + sim_ref[rows, cols]
                sre_ref[rows, cols] = nr
                sim_ref[rows, cols] = ni
                return nr, ni

            _scan_loop(L, step2, (ire_ref[:, cols], iim_ref[:, cols]))

        car_re[...] = sre_ref[T - 1:T, :]
        car_im[...] = sim_ref[T - 1:T, :]

        for j in range(SSM_JB):
            cols = _scan_cols(j)
            ch = slice(j * LANES, (j + 1) * LANES)
            y = (jnp.dot(sre_ref[:, cols].astype(_MXU), cre_ref[j], preferred_element_type=F32)
                 - jnp.dot(sim_ref[:, cols].astype(_MXU), cim_ref[j], preferred_element_type=F32))
            y = y + d_ref[:, ch] * u_ref[:, ch].astype(F32)
            y_ref[:, ch] = y
            yg_scan[:, ch] = jax.nn.gelu(y).astype(yg_scan.dtype)
        yg_ref[...] = jnp.dot(_token_order_pick(), yg_scan[...], preferred_element_type=F32).astype(yg_ref.dtype)

    tok = pl.BlockSpec((T, SSM_W), lambda c: (c, 0))
    st = pl.BlockSpec((T, N_STATES), lambda c: (c, 0))
    ini = pl.BlockSpec((None, SUBLANES, N_STATES), lambda c: (c, 0, 0))
    bsp = pl.BlockSpec((SSM_JB, LANES, SSM_SB), lambda c: (0, 0, 0))
    csp = pl.BlockSpec((SSM_JB, SSM_SB, LANES), lambda c: (0, 0, 0))
    row_w = pl.BlockSpec((1, SSM_W), lambda c: (0, 0))
    row_s = pl.BlockSpec((1, N_STATES), lambda c: (0, 0))
    return pl.pallas_call(
        body, name="ssm_fwd", grid=(nc,),
        in_specs=[tok, bsp, bsp, csp, csp, row_w] + [row_s] * 6,
        out_specs=[tok, tok, st, st, ini, ini],
        out_shape=[jax.ShapeDtypeStruct((seq, SSM_W), F32), jax.ShapeDtypeStruct((seq, SSM_W), _MXU),
                   jax.ShapeDtypeStruct((seq, N_STATES), F32), jax.ShapeDtypeStruct((seq, N_STATES), F32),
                   jax.ShapeDtypeStruct((nc, SUBLANES, N_STATES), F32),
                   jax.ShapeDtypeStruct((nc, SUBLANES, N_STATES), F32)],
        scratch_shapes=[pltpu.VMEM((1, N_STATES), F32), pltpu.VMEM((1, N_STATES), F32),
                        pltpu.VMEM((SUBLANES, N_STATES), F32), pltpu.VMEM((SUBLANES, N_STATES), F32),
                        pltpu.VMEM((T, SSM_W), _MXU)],
        compiler_params=_params(("arbitrary",)),
    )(u, b_re, b_im, c_re, c_im, d_skip, *coef)


def _ssm_bwd(dyg, y, u, s_re, s_im, i_re, i_im, b_re, b_im, c_re, c_im, d_skip, coef, d_proj, deps=()):
    seq = u.shape[0]
    nc = seq // SSM_T
    T, L = SSM_T, SSM_L
    deps = list(deps) + [d_proj]

    def body(dyg_ref, y_ref, u_ref, sre_ref, sim_ref, ire_ref, iim_ref, bre_ref, bim_ref, cre_ref, cim_ref, d_ref,
             are_ref, aim_ref, cfr_ref, cfi_ref, alr_ref, ali_ref, *rest):
        (du_ref, dbre_out, dbim_out, dcre_out, dcim_out, dd_ref, dar_ref, dai_ref, dcfr_ref, dcfi_ref,
         lre, lim, car_re, car_im, end_re, end_im, ini_re, ini_im, dbre_ref, dbim_ref, dcre_ref, dcim_ref,
         dy_ref, du_scan) = rest[len(deps):]
        step = pl.program_id(0)
        dy_ref[...] = jax.vjp(jax.nn.gelu, y_ref[...])[1](dyg_ref[...])[0]

        @pl.when(step == 0)
        def _():
            car_re[...] = jnp.zeros_like(car_re)
            car_im[...] = jnp.zeros_like(car_im)
            for ref in (dbre_ref, dbim_ref, dcre_ref, dcim_ref, dd_ref, dar_ref, dai_ref, dcfr_ref, dcfi_ref):
                ref[...] = jnp.zeros_like(ref)

        for j in range(SSM_JB):
            dyb = dy_ref[:, j * LANES:(j + 1) * LANES].astype(_MXU)
            lre[:, _scan_cols(j)] = lax.dot_general(dyb, cre_ref[j], _NT, preferred_element_type=F32)
            lim[:, _scan_cols(j)] = -lax.dot_general(dyb, cim_ref[j], _NT, preferred_element_type=F32)

        for j in range(SSM_JB):
            cols = _scan_cols(j)
            ar, ai = _bcast8(are_ref[:, cols]), _bcast8(aim_ref[:, cols])

            def step1(t, s, cols=cols, ar=ar, ai=ai):
                sr, si = s
                rows = _rows8(L - 1 - t)
                return (ar * sr + ai * si + lre[rows, cols], ar * si - ai * sr + lim[rows, cols])

            zero = jnp.zeros((SUBLANES, SSM_SB), F32)
            er, ei = _scan_loop(L, step1, (zero, zero))
            end_re[:, cols] = er
            end_im[:, cols] = ei

        alr, ali = alr_ref[...], ali_ref[...]
        cr, ci = car_re[...], car_im[...]
        ini_re[SUBLANES - 1:SUBLANES, :] = cr
        ini_im[SUBLANES - 1:SUBLANES, :] = ci
        for i in range(SUBLANES - 2, -1, -1):
            er, ei = end_re[i + 1:i + 2, :], end_im[i + 1:i + 2, :]
            cr, ci = alr * cr + ali * ci + er, alr * ci - ali * cr + ei
            ini_re[i:i + 1, :] = cr
            ini_im[i:i + 1, :] = ci

        for j in range(SSM_JB):
            cols = _scan_cols(j)
            ar, ai = _bcast8(are_ref[:, cols]), _bcast8(aim_ref[:, cols])

            def step2(t, s, cols=cols, ar=ar, ai=ai):
                sr, si = s
                rows = _rows8(L - 1 - t)
                nr = ar * sr + ai * si + lre[rows, cols]
                ni = ar * si - ai * sr + lim[rows, cols]
                lre[rows, cols] = nr
                lim[rows, cols] = ni
                return nr, ni

            _scan_loop(L, step2, (ini_re[:, cols], ini_im[:, cols]))

        car_re[...] = lre[0:1, :]
        car_im[...] = lim[0:1, :]

        head, tail, body_rows = slice(0, SUBLANES), slice(SUBLANES, T), slice(0, T - SUBLANES)
        for j in range(SSM_JB):
            cols = _scan_cols(j)
            ch = slice(j * LANES, (j + 1) * LANES)
            lr, li = lre[:, cols], lim[:, cols]
            lt_r, lt_i, sp_r, sp_i = lre[tail, cols], lim[tail, cols], sre_ref[body_rows, cols], sim_ref[body_rows, cols]
            lh_r, lh_i, si_r, si_i = lre[head, cols], lim[head, cols], ire_ref[:, cols], iim_ref[:, cols]
            dar_ref[:, cols] += _colsum(lt_r * sp_r + lt_i * sp_i) + _colsum(lh_r * si_r + lh_i * si_i)
            dai_ref[:, cols] += _colsum(lt_i * sp_r - lt_r * sp_i) + _colsum(lh_i * si_r - lh_r * si_i)
            ub = u_ref[:, ch].astype(_MXU)
            uf = ub.astype(F32)
            bur = jnp.dot(ub, bre_ref[j], preferred_element_type=F32)
            bui = jnp.dot(ub, bim_ref[j], preferred_element_type=F32)
            dcfr_ref[:, cols] += _colsum(lr * bur + li * bui)
            dcfi_ref[:, cols] += _colsum(li * bur - lr * bui)
            cfr, cfi = cfr_ref[:, cols], cfi_ref[:, cols]
            dbur = (cfr * lr + cfi * li).astype(_MXU)
            dbui = (cfr * li - cfi * lr).astype(_MXU)
            dyf = dy_ref[:, ch]
            dyb = dyf.astype(_MXU)
            du = (lax.dot_general(dbur, bre_ref[j], _NT, preferred_element_type=F32)
                  + lax.dot_general(dbui, bim_ref[j], _NT, preferred_element_type=F32) + d_ref[:, ch] * dyf)
            du_scan[:, ch] = du.astype(du_scan.dtype)
            dbre_ref[j] += lax.dot_general(ub, dbur, _TN, preferred_element_type=F32)
            dbim_ref[j] += lax.dot_general(ub, dbui, _TN, preferred_element_type=F32)
            dcre_ref[j] += lax.dot_general(sre_ref[:, cols].astype(_MXU), dyb, _TN, preferred_element_type=F32)
            dcim_ref[j] -= lax.dot_general(sim_ref[:, cols].astype(_MXU), dyb, _TN, preferred_element_type=F32)
            dd_ref[:, ch] += _colsum(dyf * uf)
        du_ref[...] = jnp.dot(_token_order_pick(), du_scan[...], preferred_element_type=F32).astype(du_ref.dtype)

        @pl.when(step == nc - 1)
        def _():
            for acc, out in ((dbre_ref, dbre_out), (dbim_ref, dbim_out), (dcre_ref, dcre_out), (dcim_ref, dcim_out)):
                pltpu.sync_copy(acc, out)

    tok = pl.BlockSpec((T, SSM_W), lambda c: (nc - 1 - c, 0))
    st = pl.BlockSpec((T, N_STATES), lambda c: (nc - 1 - c, 0))
    ini = pl.BlockSpec((None, SUBLANES, N_STATES), lambda c: (nc - 1 - c, 0, 0))
    bsp = pl.BlockSpec((SSM_JB, LANES, SSM_SB), lambda c: (0, 0, 0))
    csp = pl.BlockSpec((SSM_JB, SSM_SB, LANES), lambda c: (0, 0, 0))
    row_w = pl.BlockSpec((1, SSM_W), lambda c: (0, 0))
    row_s = pl.BlockSpec((1, N_STATES), lambda c: (0, 0))
    big = pltpu.VMEM((T, N_STATES), F32)
    one = pltpu.VMEM((1, N_STATES), F32)
    eight = pltpu.VMEM((SUBLANES, N_STATES), F32)
    return pl.pallas_call(
        body, name="ssm_bwd", grid=(nc,),
        in_specs=[tok, tok, tok, st, st, ini, ini, bsp, bsp, csp, csp, row_w] + [row_s] * 6 + [_ANY] * len(deps),
        out_specs=[pl.BlockSpec((pl.Element(T), pl.Element(SSM_W)), lambda c: ((nc - 1 - c) * T, OFF_U * CW)),
                   _ANY, _ANY, _ANY, _ANY, row_w, row_s, row_s, row_s, row_s],
        input_output_aliases={18 + len(deps) - 1: 0},
        out_shape=[jax.ShapeDtypeStruct(d_proj.shape, d_proj.dtype),
                   jax.ShapeDtypeStruct((SSM_JB, LANES, SSM_SB), F32), jax.ShapeDtypeStruct((SSM_JB, LANES, SSM_SB), F32),
                   jax.ShapeDtypeStruct((SSM_JB, SSM_SB, LANES), F32), jax.ShapeDtypeStruct((SSM_JB, SSM_SB, LANES), F32),
                   jax.ShapeDtypeStruct((1, SSM_W), F32)] + [jax.ShapeDtypeStruct((1, N_STATES), F32)] * 4,
        scratch_shapes=[big, big, one, one, eight, eight, eight, eight,
                        pltpu.VMEM((SSM_JB, LANES, SSM_SB), F32), pltpu.VMEM((SSM_JB, LANES, SSM_SB), F32),
                        pltpu.VMEM((SSM_JB, SSM_SB, LANES), F32), pltpu.VMEM((SSM_JB, SSM_SB, LANES), F32),
                        pltpu.VMEM((T, SSM_W), F32), pltpu.VMEM((T, SSM_W), _MXU)],
        compiler_params=_params(("arbitrary",)),
    )(dyg, y, u, s_re, s_im, i_re, i_im, b_re, b_im, c_re, c_im, d_skip, *coef, *deps)


def _block_diag_b(b):
    t = b.reshape(SSM_JB, 8, STATE, GROUP).transpose(0, 1, 3, 2)
    eye = jnp.eye(8, dtype=b.dtype)
    return (t[:, :, :, None, :] * eye[None, :, None, :, None]).reshape(SSM_JB, LANES, SSM_SB)


def _block_diag_c(c):
    t = c.reshape(SSM_JB, 8, GROUP, STATE).transpose(0, 1, 3, 2)
    eye = jnp.eye(8, dtype=c.dtype)
    return (t[:, :, :, None, :] * eye[None, :, None, :, None]).reshape(SSM_JB, SSM_SB, LANES)


def _diag_of_b(blk):
    t = blk.reshape(SSM_JB, 8, GROUP, 8, STATE)
    d = jnp.sum(t * jnp.eye(8, dtype=blk.dtype)[None, :, None, :, None], axis=3)
    return d.transpose(0, 1, 3, 2).reshape(N_GROUPS, STATE, GROUP)


def _diag_of_c(blk):
    t = blk.reshape(SSM_JB, 8, STATE, 8, GROUP)
    d = jnp.sum(t * jnp.eye(8, dtype=blk.dtype)[None, :, None, :, None], axis=3)
    return d.transpose(0, 1, 3, 2).reshape(N_GROUPS, GROUP, STATE)


def _to_scan_order(v):
    seq, w = v.shape
    return v.reshape(seq // SSM_T, SUBLANES, SSM_L, w).transpose(0, 2, 1, 3).reshape(seq, w)


def _adamw_math(w, g, m, v):
    nm = ADAM_B1 * m + (1.0 - ADAM_B1) * g
    nv = ADAM_B2 * v + (1.0 - ADAM_B2) * jnp.square(g)
    m_hat = nm / (1.0 - ADAM_B1 ** ADAM_STEP)
    v_hat = nv / (1.0 - ADAM_B2 ** ADAM_STEP)
    return -ADAM_LR * (m_hat / (jnp.sqrt(v_hat) + ADAM_EPS) + ADAM_WD * w), nm, nv


def _adamw(w, g, m, v, *, name, tm, deps=()):
    rows, cols = w.shape
    nd = len(deps)

    def body(w_ref, g_ref, m_ref, v_ref, *rest):
        d_ref, nm_ref, nv_ref = rest[nd:]
        d_ref[...], nm_ref[...], nv_ref[...] = _adamw_math(w_ref[...], g_ref[...], m_ref[...], v_ref[...])

    spec = pl.BlockSpec((tm, cols), lambda i: (i, 0))
    shp = jax.ShapeDtypeStruct((rows, cols), F32)
    return pl.pallas_call(body, name=name, grid=(rows // tm,), in_specs=[spec] * 4 + [_ANY] * nd,
                          out_specs=[spec] * 3, out_shape=[shp] * 3,
                          compiler_params=_params(("arbitrary",)))(w, g, m, v, *deps)


def _place():
    x, y, c = lax.axis_index("x"), lax.axis_index("y"), lax.axis_index("c")
    chips = [(1 - x, y), (x, 1 - y), (1 - x, 1 - y)]
    return x, y, c, chips


def _remote(src, dst, send_sem, recv_sem, dev):
    return pltpu.make_async_remote_copy(src_ref=src, dst_ref=dst, send_sem=send_sem, recv_sem=recv_sem,
                                        device_id=dev, device_id_type=MESH)


def _place_shard(w, mine_arr, *, name, tm=256, deps=()):
    rows, cols = w.shape

    def body(m_ref, w_ref, *rest):
        rest[-1][...] = w_ref[...].astype(rest[-1].dtype)

    return pl.pallas_call(
        body, name=name,
        grid_spec=pltpu.PrefetchScalarGridSpec(
            num_scalar_prefetch=1, grid=(rows // tm,),
            in_specs=[pl.BlockSpec((tm, cols), lambda i, m: (i, 0))] + [_ANY] * len(deps),
            out_specs=pl.BlockSpec((None, tm, cols), lambda i, m: (m[0], i, 0))),
        out_shape=jax.ShapeDtypeStruct((N_CHIPS, rows, cols), _WIRE),
        compiler_params=_params(("arbitrary",)),
    )(mine_arr, w, *deps)


_HBM = pl.BlockSpec(memory_space=pltpu.HBM)
_SEM = pl.BlockSpec(memory_space=pltpu.SEMAPHORE)
_EFFECT = pltpu.SideEffectType.DATAFLOW_SIDE_EFFECTING


def _copies_start(name, bufs, plan, count, after=()):
    nb, na = len(bufs), len(after)

    def body(*refs):
        send_sems, recv_sems, token = refs[nb + na], refs[nb + na + 1], refs[-1]
        copies = plan(refs[:nb])
        assert len(copies) == count
        for i, (src, dst, dev, _) in enumerate(copies):
            _remote(src, dst, send_sems.at[i], recv_sems.at[i], dev).start()
        token[...] = jnp.zeros_like(token)

    res = pl.pallas_call(
        body, name=name, in_specs=[_HBM] * nb + [_ANY] * na,
        out_specs=(_SEM, _SEM, *[_HBM] * nb, pl.BlockSpec(memory_space=pltpu.VMEM)),
        out_shape=(pltpu.SemaphoreType.DMA((count,)), pltpu.SemaphoreType.DMA((count,)),
                   *[pltpu.HBM(b.shape, b.dtype) for b in bufs], jax.ShapeDtypeStruct((SUBLANES, LANES), F32)),
        input_output_aliases={i: 2 + i for i in range(nb)},
        compiler_params=pltpu.CompilerParams(has_side_effects=_EFFECT),
    )(*[pltpu.with_memory_space_constraint(b, pltpu.HBM) for b in bufs], *after)
    return (res[0], res[1]), list(res[2:2 + nb]), res[-1]


def _copies_wait(name, bufs, sems, plan, after=(), which=None):
    nb, na = len(bufs), len(after)

    def body(*refs):
        send_sems, recv_sems = refs[nb], refs[nb + 1]
        for i, (src, _, dev, land) in enumerate(plan(refs[:nb])):
            if which is not None and i not in which:
                continue
            cp = _remote(src, land, send_sems.at[i], recv_sems.at[i], dev)
            cp.wait_send()
            cp.wait_recv()

    res = pl.pallas_call(
        body, name=name, in_specs=[_HBM] * nb + [_SEM, _SEM] + [_ANY] * na, out_specs=[_HBM] * nb,
        out_shape=[pltpu.HBM(b.shape, b.dtype) for b in bufs],
        input_output_aliases={i: i for i in range(nb)},
        compiler_params=pltpu.CompilerParams(has_side_effects=_EFFECT),
    )(*bufs, *sems, *after)
    return list(res)


def _plan_gather_ici(fulls, which=(0, 1, 2)):
    x, y, c, chips = _place()
    copies = []
    for f in fulls:
        half = pl.ds(c * (f.shape[1] // 2), f.shape[1] // 2)
        own = f.at[2 * x + y, half]
        for chip in [chips[k] for k in which]:
            copies.append((own, own, (*chip, c), f.at[2 * chip[0] + chip[1], half]))
    return copies


def _plan_gather_d2d(fulls, which=(0, 1, 2)):
    x, y, c, chips = _place()
    copies = []
    for f in fulls:
        r2 = f.shape[1] // 2
        for chip in [chips[k] for k in which]:
            blk = 2 * chip[0] + chip[1]
            landed = f.at[blk, pl.ds(c * r2, r2)]
            copies.append((landed, landed, (x, y, 1 - c), f.at[blk, pl.ds((1 - c) * r2, r2)]))
    return copies


def _plan_relay_direct(fulls):
    (f,) = fulls
    x, y, c, chips = _place()
    half = pl.ds(c * (f.shape[1] // 2), f.shape[1] // 2)
    own = f.at[2 * x + y, half]
    return [(own, own, (*chip, c), f.at[2 * chip[0] + chip[1], half]) for chip in chips[:2]]


def _plan_relay_forward(fulls, k):
    (f,) = fulls
    x, y, c, chips = _place()
    r2 = f.shape[1] // 2
    half, other = pl.ds(c * r2, r2), pl.ds((1 - c) * r2, r2)
    quarter = pl.ds(c * r2 + k * (r2 // 2), r2 // 2)
    blk, far = 2 * chips[k][0] + chips[k][1], 2 * chips[2][0] + chips[2][1]
    passed, landed = f.at[blk, quarter], f.at[blk, half]
    return [(passed, passed, (*chips[1 - k], c), f.at[far, quarter]), (landed, landed, (x, y, 1 - c), f.at[blk, other])]


def _plan_relay_last(fulls):
    (f,) = fulls
    x, y, c, chips = _place()
    r2 = f.shape[1] // 2
    far = 2 * chips[2][0] + chips[2][1]
    landed = f.at[far, pl.ds(c * r2, r2)]
    return [(landed, landed, (x, y, 1 - c), f.at[far, pl.ds((1 - c) * r2, r2)])]


def _plan_swap_halves(refs):
    x, y, c, _ = _place()
    n = len(refs) // 2
    copies = []
    for g, land in zip(refs[:n], refs[n:]):
        r2 = g.shape[1] // 2
        copies.append((g.at[:, pl.ds((1 - c) * r2, r2), :], land, (x, y, 1 - c), land))
    return copies


def _plan_scatter_chips(refs):
    x, y, c, chips = _place()
    n = len(refs) // 2
    copies = []
    for h, land in zip(refs[:n], refs[n:]):
        for k, chip in enumerate(chips):
            copies.append((h.at[2 * chip[0] + chip[1]], land.at[k], (*chip, c), land.at[k]))
    return copies


def _plan_join_halves(totals):
    x, y, c, _ = _place()
    copies = []
    for t in totals:
        r2 = t.shape[0] // 2
        mine = t.at[pl.ds(c * r2, r2)]
        copies.append((mine, mine, (x, y, 1 - c), t.at[pl.ds((1 - c) * r2, r2)]))
    return copies


def _add_sibling_half(g, got, c_arr, *, name, tm):
    _, rows, cols = g.shape
    r2 = rows // 2
    nb = r2 // tm

    def body(c_ref, g_ref, r_ref, o_ref):
        o_ref[...] = (g_ref[...].astype(F32) + r_ref[...].astype(F32)).astype(o_ref.dtype)

    return pl.pallas_call(
        body, name=name,
        grid_spec=pltpu.PrefetchScalarGridSpec(
            num_scalar_prefetch=1, grid=(N_CHIPS, nb),
            in_specs=[pl.BlockSpec((None, tm, cols), lambda b, i, c: (b, c[0] * nb + i, 0)),
                      pl.BlockSpec((None, tm, cols), lambda b, i, c: (b, i, 0))],
            out_specs=pl.BlockSpec((None, tm, cols), lambda b, i, c: (b, i, 0))),
        out_shape=jax.ShapeDtypeStruct((N_CHIPS, r2, cols), _WIRE),
        compiler_params=_params(("arbitrary", "arbitrary")),
    )(c_arr, g, got)


def _add_chips(h, got, place_arr, *, name, tm):
    _, r2, cols = h.shape
    nb = r2 // tm

    def body(p_ref, h_ref, r_ref, o_ref):
        o_ref[...] = ((h_ref[...].astype(F32) + r_ref[0].astype(F32)) + r_ref[1].astype(F32)) + r_ref[2].astype(F32)

    return pl.pallas_call(
        body, name=name,
        grid_spec=pltpu.PrefetchScalarGridSpec(
            num_scalar_prefetch=1, grid=(nb,),
            in_specs=[pl.BlockSpec((None, tm, cols), lambda i, p: (p[0], i, 0)),
                      pl.BlockSpec((3, tm, cols), lambda i, p: (0, i, 0))],
            out_specs=pl.BlockSpec((tm, cols), lambda i, p: (p[1] * nb + i, 0))),
        out_shape=jax.ShapeDtypeStruct((2 * r2, cols), F32),
        compiler_params=_params(("arbitrary",)),
    )(place_arr, h, got)


class _ReduceScatter:
    def __init__(self, tag, names, grads):
        self.tag, self.names, self.n = tag, names, len(names)
        core = lax.axis_index("c").astype(jnp.int32)
        chip = (2 * lax.axis_index("x") + lax.axis_index("y")).astype(jnp.int32)
        self.c_arr, self.place_arr = core.reshape(1), jnp.stack([chip, core])
        self.bufs = list(grads)

    def _start(self, step, bufs, plan, count, after):
        self.plan = plan
        self.step = f"grad_{step}_{self.tag}"
        self.sems, self.bufs, token = _copies_start(self.step + "_start", bufs, plan, count, after)
        return [token]

    def _wait(self, after):
        self.bufs = _copies_wait(self.step + "_wait", self.bufs, self.sems, self.plan, after)
        return self.bufs

    def start_swap(self, after=()):
        lands = [lax.empty((N_CHIPS, g.shape[1] // 2, g.shape[2]), g.dtype) for g in self.bufs]
        return self._start("swap", self.bufs + lands, _plan_swap_halves, self.n, after)

    def start_scatter(self, after):
        bufs = self._wait(after)
        pair = [_add_sibling_half(g, r, self.c_arr, name=f"grad_add_sibling_{nm}", tm=min(256, g.shape[1] // 2))
                for nm, g, r in zip(self.names, bufs[:self.n], bufs[self.n:])]
        lands = [lax.empty((3,) + h.shape[1:], h.dtype) for h in pair]
        return self._start("scatter", pair + lands, _plan_scatter_chips, 3 * self.n, ())

    def start_join(self, after):
        bufs = self._wait(after)
        total = [_add_chips(h, r, self.place_arr, name=f"grad_add_chips_{nm}", tm=min(256, h.shape[1]))
                 for nm, h, r in zip(self.names, bufs[:self.n], bufs[self.n:])]
        return self._start("join", total, _plan_join_halves, self.n, ())

    def finish(self, after):
        return dict(zip(self.names, self._wait(after)))


def _all_gather_small(v):
    m_per, n = v.shape

    def body(x_ref, out_ref, send_sems, recv_sems, local_sem):
        x, y, c, chips = _place()
        me, sibling = (x, y, c), (x, y, 1 - c)

        def rows(px, py, pc):
            return out_ref.at[4 * px + 2 * py + pc]

        def copy(k, block, to, src=None):
            return _remote(rows(*block) if src is None else src, rows(*block), send_sems.at[k], recv_sems.at[k], to)

        mine = pltpu.make_async_copy(x_ref, rows(*me), local_sem)
        mine.start()
        first = [copy(0, me, sibling, src=x_ref)]
        first += [copy(1 + j, me, (*chip, c), src=x_ref) for j, chip in enumerate(chips)]
        for cp in first:
            cp.start()
        passed = [copy(4 + j, (*chip, c), sibling) for j, chip in enumerate(chips)]
        for j, chip in enumerate(chips):
            copy(1 + j, (*chip, c), me).wait_recv()
            passed[j].start()
        copy(0, sibling, me).wait_recv()
        for j, chip in enumerate(chips):
            copy(4 + j, (*chip, 1 - c), me).wait_recv()
        for cp in first + passed:
            cp.wait_send()
        mine.wait()

    return pl.pallas_call(
        body, name="gather_small_grads",
        out_shape=jax.ShapeDtypeStruct((8, m_per, n), v.dtype),
        in_specs=[pl.BlockSpec(memory_space=pltpu.VMEM)], out_specs=pl.BlockSpec(memory_space=pltpu.VMEM),
        scratch_shapes=[pltpu.SemaphoreType.DMA((7,)), pltpu.SemaphoreType.DMA((7,)), pltpu.SemaphoreType.DMA],
        compiler_params=pltpu.CompilerParams(vmem_limit_bytes=VMEM_LIMIT),
    )(v)


def _sum8(v, *, name):
    _, m, n = v.shape

    def body(v_ref, o_ref):
        acc = v_ref[0]
        for d in range(1, 8):
            acc = acc + v_ref[d]
        o_ref[...] = acc

    return pl.pallas_call(body, name=name, out_shape=jax.ShapeDtypeStruct((m, n), F32),
                          compiler_params=pltpu.CompilerParams(vmem_limit_bytes=VMEM_LIMIT))(v)


def _local_step(x, target, norm_w, q_norm_w, k_norm_w, sinks, a_re, a_im, log_dt, b_re, b_im, c_re, c_im, d_skip,
                b_glu, io):
    seq = x.shape[0]
    qw2 = jnp.tile(q_norm_w.reshape(1, HEAD_DIM), (1, HEADS_PER_TILE))
    kw2 = jnp.tile(k_norm_w.reshape(1, HEAD_DIM), (1, HEADS_PER_TILE))
    nw, bg = norm_w.reshape(1, D_MODEL), b_glu.reshape(1, D_MODEL)
    dsk = d_skip.reshape(1, SSM_W)

    h, rstd = _rms_fwd(x, nw, deps=io.begin())
    proj, w_in4 = io.projection(h)
    attn, lse, ya_in = _attn2_fwd(proj, qw2, kw2, sinks, deps=io.after_proj(proj))
    w_ap4 = io.weight("w_attn_proj", ya_in)
    w_glu4, w_sp4, w_out = io.weight("w_glu", ya_in), io.weight("w_ssm_proj", ya_in), io.weight("w_out", ya_in)
    y_a = _mm(ya_in, w_ap4, mode="nn", name="mm_attn_proj", tm=2048, tn=512, tk=ATTN_W, b_blocked=True,
              rows_outer=True, out_dtype=_MXU)

    flat_a = (a_re.reshape(1, N_STATES), a_im.reshape(1, N_STATES), jnp.repeat(log_dt, STATE).reshape(1, N_STATES))
    coef = _ssm_params_fwd(*flat_a)
    bre_blk, bim_blk = _block_diag_b(b_re).astype(_MXU), _block_diag_b(b_im).astype(_MXU)
    cre_blk, cim_blk = _block_diag_c(c_re).astype(_MXU), _block_diag_c(c_im).astype(_MXU)
    u_scan = _to_scan_order(proj[:, OFF_U * CW:OFF_U * CW + SSM_W])
    y_scan, yg, s_re, s_im, i_re, i_im = _ssm_fwd(u_scan, bre_blk, bim_blk, cre_blk, cim_blk, dsk, coef)
    glu, ys_in = _mm_glu_gate(yg, w_glu4, bg, proj)
    y_s = _mm(ys_in, w_sp4, mode="nn", name="mm_ssm_proj", tm=2048, tn=512, tk=SSM_W, b_blocked=True,
              rows_outer=True, out_dtype=_MXU)

    merged, dout, dout_b, sq = _mm_merge_out_loss(proj, y_a, y_s, w_out, x, target)
    loss = 0.5 * jnp.sum(sq) / D_MODEL

    d_ya, d_ys, d_proj = _mm_merge_bwd(dout_b, w_out, proj, y_a, y_s)
    g_w_out = _mm(merged, dout_b, mode="tn", name="mm_g_w_out", tm=1024, tn=D_MODEL, tk=1024, out_dtype=_WIRE)

    d_ya_in = _mm(d_ya, w_ap4, mode="nt", name="mm_d_attn_gate", tm=2048, tn=ATTN_W, tk=512, b_blocked=True)
    g_w_ap = _mm(ya_in, d_ya, mode="tn", name="mm_g_w_attn_proj", tm=ATTN_W, tn=D_MODEL, tk=2048, out_dtype=_WIRE,
                 out_blocked=True)

    g_w_sp = _mm(ys_in, d_ys, mode="tn", name="mm_g_w_ssm_proj", tm=SSM_W, tn=D_MODEL, tk=2048, out_dtype=_WIRE,
                 out_blocked=True)
    d_glu, d_proj, g_bglu = _mm_ssm_gate_bwd(d_ys, w_sp4, glu, bg, proj, d_proj)
    d_yg = _mm(d_glu, w_glu4, mode="nt", name="mm_d_gelu", tm=2048, tn=SSM_W, tk=512, b_blocked=True)
    g_w_glu = _mm(yg, d_glu, mode="tn", name="mm_g_w_glu", tm=SSM_W, tn=D_MODEL, tk=2048, out_dtype=_WIRE, out_blocked=True)
    dep = io.later_grads(dict(w_attn_proj=g_w_ap, w_glu=g_w_glu, w_ssm_proj=g_w_sp,
                              w_out=g_w_out.reshape(N_CHIPS, D_MODEL // N_CHIPS, D_MODEL)))

    d_proj, g_qw2, g_kw2, g_sk = _attn2_bwd(proj, qw2, kw2, sinks, lse, attn, d_ya_in, d_proj, deps=dep)
    dep = io.before_scan_backward([d_proj])
    (d_proj, g_bre, g_bim, g_cre, g_cim, g_dsk, g_abr, g_abi, g_cfr, g_cfi) = _ssm_bwd(
        _to_scan_order(d_yg), y_scan, u_scan, s_re, s_im, i_re, i_im, bre_blk, bim_blk, cre_blk, cim_blk, dsk, coef,
        d_proj, deps=dep)
    g_are, g_aim, g_ldt = _ssm_params_bwd(*flat_a, g_abr, g_abi, g_cfr, g_cfi)
    g_are, g_aim = g_are.reshape(N_GROUPS, STATE), g_aim.reshape(N_GROUPS, STATE)
    g_ldt = g_ldt.reshape(N_GROUPS, STATE).sum(axis=1)
    dep = io.before_input_projection_grad([d_proj]) + io.small_grads(dict(
        q_norm_w=g_qw2[0, :HEAD_DIM] + g_qw2[0, HEAD_DIM:], k_norm_w=g_kw2[0, :HEAD_DIM] + g_kw2[0, HEAD_DIM:],
        sinks=g_sk.reshape(N_Q_HEADS), A_re=g_are, A_im=g_aim, log_dt=g_ldt,
        B_re=_diag_of_b(g_bre), B_im=_diag_of_b(g_bim), C_re=_diag_of_c(g_cre), C_im=_diag_of_c(g_cim),
        D_skip=g_dsk.reshape(N_GROUPS, GROUP), b_glu=g_bglu.reshape(D_MODEL)))
    g_w_in = _mm(h, d_proj, mode="tn", name="mm_g_w_in", tm=1024, tn=IN_W // 4, tk=1024, out_dtype=_WIRE,
                 out_blocked=True, deps=dep)
    dep = io.input_projection_grad(g_w_in)
    d_h = _mm(d_proj, w_in4, mode="nt", name="mm_d_h", tm=1024, tn=D_MODEL, tk=IN_W // 4, b_blocked=True, deps=dep)
    grad_x, g_nw = _rms_bwd(d_h, x, rstd, nw, dout)
    return loss, grad_x, g_nw.reshape(D_MODEL)


_SMALL = ["norm_w", "q_norm_w", "k_norm_w", "sinks", "A_re", "A_im", "log_dt", "B_re", "B_im", "C_re", "C_im",
          "D_skip", "b_glu"]
_BIG = ["w_in", "w_attn_proj", "w_glu", "w_ssm_proj", "w_out"]
_LATER = _BIG[1:]
_RELATIONS = ("flip_x", "flip_y", "flip_xy")
_ORDER = ["norm_w", "w_in", "q_norm_w", "k_norm_w", "sinks", "w_attn_proj", "A_re", "A_im", "log_dt", "B_re", "B_im",
          "C_re", "C_im", "D_skip", "w_glu", "b_glu", "w_ssm_proj", "w_out"]
_PACK_W = 1024


def _packed_rows(size):
    unit = SUBLANES * _PACK_W
    return -(-size // unit) * SUBLANES


def _pack_small(d, names):
    parts = []
    for n in names:
        flat = d[n].reshape(-1).astype(F32)
        rows = _packed_rows(flat.shape[0])
        parts.append(jnp.pad(flat, (0, rows * _PACK_W - flat.shape[0])).reshape(rows, _PACK_W))
    return jnp.concatenate(parts, axis=0)


def _unpack_small(packed, like, names):
    out, pos = {}, 0
    for n in names:
        rows = _packed_rows(like[n].size)
        out[n] = packed[pos:pos + rows].reshape(-1)[:like[n].size].reshape(like[n].shape)
        pos += rows
    return out


def _place_block(v, index_arr, *, name):
    rows, cols = v.shape

    def body(i_ref, v_ref, o_ref):
        o_ref[...] = v_ref[...]

    return pl.pallas_call(
        body, name=name,
        grid_spec=pltpu.PrefetchScalarGridSpec(
            num_scalar_prefetch=1, grid=(1,),
            in_specs=[pl.BlockSpec((rows, cols), lambda i, d: (0, 0))],
            out_specs=pl.BlockSpec((None, rows, cols), lambda i, d: (d[0], 0, 0))),
        out_shape=jax.ShapeDtypeStruct((8, rows, cols), v.dtype),
        compiler_params=_params(("arbitrary",)),
    )(index_arr, v)


def _plan_all_to_all(refs):
    (land,) = refs
    x, y, c, _ = _place()
    own = land.at[4 * x + 2 * y + c]
    copies = []
    for fx, fy, fc in [(0, 0, 1), (0, 1, 0), (0, 1, 1), (1, 0, 0), (1, 0, 1), (1, 1, 0), (1, 1, 1)]:
        px, py, pc = (1 - x) if fx else x, (1 - y) if fy else y, (1 - c) if fc else c
        copies.append((own, own, (px, py, pc), land.at[4 * px + 2 * py + pc]))
    return copies


def _adamw_whole(w, g, m, v, *, name):
    def body(w_ref, g_ref, m_ref, v_ref, d_ref, nm_ref, nv_ref):
        d_ref[...], nm_ref[...], nv_ref[...] = _adamw_math(w_ref[...], g_ref[...], m_ref[...], v_ref[...])

    return pl.pallas_call(body, name=name, out_shape=[jax.ShapeDtypeStruct(w.shape, F32)] * 3)(w, g, m, v)


class _Exchanges:
    def __init__(self, w, m, v):
        self.w, self.m, self.v = w, m, v
        self.grads, self.delta, self.new_m, self.new_v = {}, {}, {}, {}

    def _adamw(self, names, deps):
        for n in names:
            self.delta[n], self.new_m[n], self.new_v[n] = _adamw(
                self.w[n], self.grads[n], self.m[n], self.v[n], name=f"adamw_{n}", tm=256, deps=deps)

    def begin(self):
        chip = (2 * lax.axis_index("x") + lax.axis_index("y")).astype(jnp.int32).reshape(1)
        w_in = _place_shard(self.w["w_in"], chip, name="place_w_in")
        self.w_in_sems, self.w_in_buf, token = _copies_start("gather_w_in_direct_start", [w_in], _plan_relay_direct, 2)
        self.later_full = [_place_shard(self.w[n], chip, name=f"place_{n}", deps=[token]) for n in _LATER]
        return self.later_full

    def projection(self, h):
        x, y = lax.axis_index("x"), lax.axis_index("y")
        blks = [jnp.asarray(b, jnp.int32).reshape(1)
                for b in (2 * x + y, 2 * (1 - x) + y, 2 * x + (1 - y), 2 * (1 - x) + (1 - y))]
        bufs = self.w_in_buf
        proj = _mm_chip_block(h, bufs[0], blks[0], None, name="mm_proj_own", out_dtype=_MXU)
        relay, token = [], proj
        for k, tag in enumerate(_RELATIONS[:2]):
            bufs = _copies_wait(f"gather_w_in_direct_{tag}_wait", bufs, self.w_in_sems, _plan_relay_direct, [token],
                                which=(k,))
            plan = functools.partial(_plan_relay_forward, k=k)
            sems, bufs, token = _copies_start(f"gather_w_in_relay_{tag}_start", bufs, plan, 2)
            relay.append((sems, plan))
        self.rest = _copies_start("gather_ici_rest_start", self.later_full, _plan_gather_ici, 3 * len(_LATER),
                                  after=[token])
        token = self.rest[2]
        for k, tag in enumerate(_RELATIONS[:2]):
            bufs = _copies_wait(f"gather_w_in_hand_{tag}_wait", bufs, relay[k][0], relay[k][1], [token], which=(1,))
            token = proj = _mm_chip_block(h, bufs[0], blks[1 + k], proj, name=f"mm_proj_{tag}", out_dtype=_MXU)
        for k, tag in enumerate(_RELATIONS[:2]):
            bufs = _copies_wait(f"gather_w_in_relay_{tag}_wait", bufs, relay[k][0], relay[k][1], [token], which=(0,))
        sems, bufs, token = _copies_start("gather_w_in_last_start", bufs, _plan_relay_last, 1)
        bufs = _copies_wait("gather_w_in_last_wait", bufs, sems, _plan_relay_last, [token])
        proj = _mm_chip_block(h, bufs[0], blks[3], proj, name="mm_proj_flip_xy", out_dtype=_MXU)
        return proj, bufs[0]

    def weight(self, name, after):
        if self.rest is not None:
            sems, bufs = self.rest
            later = dict(zip(_LATER, _copies_wait("gather_d2d_rest_wait", bufs, sems, _plan_gather_d2d, [after])))
            later["w_out"] = later["w_out"].reshape(D_MODEL, D_MODEL)
            self.later, self.rest = later, None
        return self.later[name]

    def after_proj(self, proj):
        sems, bufs, _ = self.rest
        bufs = _copies_wait("gather_ici_rest_wait", bufs, sems, _plan_gather_ici, [proj])
        sems, bufs, token = _copies_start("gather_d2d_rest_start", bufs, _plan_gather_d2d, 3 * len(_LATER))
        self.rest = (sems, bufs)
        return [token]

    def later_grads(self, grads):
        self.rs_later = _ReduceScatter("later", _LATER, [grads[n] for n in _LATER])
        return self.rs_later.start_swap()

    def before_scan_backward(self, after):
        return self.rs_later.start_scatter(after)

    def before_input_projection_grad(self, after):
        return self.rs_later.start_join(after)

    def input_projection_grad(self, g_w_in):
        self.grads.update(self.rs_later.finish([g_w_in]))
        self.rs_in = _ReduceScatter("w_in", ["w_in"], [g_w_in])
        self._adamw(_LATER, self.rs_in.start_swap())
        return self.rs_in.start_scatter([self.delta[n] for n in _LATER])

    def _adamw_small(self, names):
        for n in names:
            self.delta[n], self.new_m[n], self.new_v[n] = _adamw_whole(
                self.w[n], self.grads[n], self.m[n], self.v[n], name=f"adamw_{n}")

    def small_grads(self, grads):
        me = (4 * lax.axis_index("x") + 2 * lax.axis_index("y") + lax.axis_index("c")).astype(jnp.int32).reshape(1)
        land = _place_block(_pack_small(grads, _SMALL[1:]), me, name="place_small_grads")
        self.small = _copies_start("gather_small_start", [land], _plan_all_to_all, 7)
        return [self.small[2]]

    def finish(self, g_norm_w, loss, after):
        join = self.rs_in.start_join(after)
        sems, bufs, _ = self.small
        (land,) = _copies_wait("gather_small_wait", bufs, sems, _plan_all_to_all, join)
        self.grads.update(_unpack_small(_sum8(land, name="sum_small_grads"), self.w, _SMALL[1:]))
        self._adamw_small(_SMALL[1:])
        rows = _packed_rows(g_norm_w.size)
        late = jnp.concatenate([_pack_small(dict(norm_w=g_norm_w), _SMALL[:1]),
                                jnp.pad(loss.reshape(1, 1), ((0, SUBLANES - 1), (0, _PACK_W - 1)))], axis=0)
        late = _sum8(_all_gather_small(late), name="sum_norm_w_grad_and_loss")
        self.grads.update(_unpack_small(late[:rows], self.w, _SMALL[:1]))
        self._adamw_small(_SMALL[:1])
        self.grads.update(self.rs_in.finish([self.delta[_SMALL[0]]]))
        self._adamw(["w_in"], ())
        return late[rows, 0]


def kernel(x, norm_w, w_in, q_norm_w, k_norm_w, sinks, w_attn_proj, A_re, A_im, log_dt, B_re, B_im, C_re, C_im, D_skip, w_glu, b_glu, w_ssm_proj, w_out, loss_target, m_norm_w, m_w_in, m_q_norm_w, m_k_norm_w, m_sinks, m_w_attn_proj, m_A_re, m_A_im, m_log_dt, m_B_re, m_B_im, m_C_re, m_C_im, m_D_skip, m_w_glu, m_b_glu, m_w_ssm_proj, m_w_out, v_norm_w, v_w_in, v_q_norm_w, v_k_norm_w, v_sinks, v_w_attn_proj, v_A_re, v_A_im, v_log_dt, v_B_re, v_B_im, v_C_re, v_C_im, v_D_skip, v_w_glu, v_b_glu, v_w_ssm_proj, v_w_out):
    w = dict(norm_w=norm_w, w_in=w_in, q_norm_w=q_norm_w, k_norm_w=k_norm_w, sinks=sinks, w_attn_proj=w_attn_proj,
             A_re=A_re, A_im=A_im, log_dt=log_dt, B_re=B_re, B_im=B_im, C_re=C_re, C_im=C_im, D_skip=D_skip,
             w_glu=w_glu, b_glu=b_glu, w_ssm_proj=w_ssm_proj, w_out=w_out)
    m = dict(norm_w=m_norm_w, w_in=m_w_in, q_norm_w=m_q_norm_w, k_norm_w=m_k_norm_w, sinks=m_sinks,
             w_attn_proj=m_w_attn_proj, A_re=m_A_re, A_im=m_A_im, log_dt=m_log_dt, B_re=m_B_re, B_im=m_B_im,
             C_re=m_C_re, C_im=m_C_im, D_skip=m_D_skip, w_glu=m_w_glu, b_glu=m_b_glu, w_ssm_proj=m_w_ssm_proj,
             w_out=m_w_out)
    v = dict(norm_w=v_norm_w, w_in=v_w_in, q_norm_w=v_q_norm_w, k_norm_w=v_k_norm_w, sinks=v_sinks,
             w_attn_proj=v_w_attn_proj, A_re=v_A_re, A_im=v_A_im, log_dt=v_log_dt, B_re=v_B_re, B_im=v_B_im,
             C_re=v_C_re, C_im=v_C_im, D_skip=v_D_skip, w_glu=v_w_glu, b_glu=v_b_glu, w_ssm_proj=v_w_ssm_proj,
             w_out=v_w_out)

    io = _Exchanges(w, m, v)
    loss, grad_x, g_norm_w = _local_step(x[0], loss_target[0], norm_w, q_norm_w, k_norm_w, sinks, A_re, A_im, log_dt,
                                         B_re, B_im, C_re, C_im, D_skip, b_glu, io)
    loss = io.finish(g_norm_w, loss, [grad_x])
    grads, delta, new_m, new_v = io.grads, io.delta, io.new_m, io.new_v

    return (loss, grad_x[None], *[grads[n] for n in _ORDER], *[delta[n] for n in _ORDER],
            *[new_m[n] for n in _ORDER], *[new_v[n] for n in _ORDER])
```

```python
import functools
import math

import jax
import jax.numpy as jnp
from jax import lax
from jax.experimental import pallas as pl
from jax.experimental.pallas import tpu as pltpu

F32 = jnp.float32
_MXU = jnp.bfloat16
_WIRE = jnp.bfloat16

LANES = 128
SUBLANES = 8
VMEM_LIMIT = 56 * 1024 * 1024

D_MODEL = 2048
HEAD_DIM = 64
N_Q_HEADS = 16
N_KV_HEADS = 4
Q_PER_KV = 4
ATTN_W = 1024
KV_W = 256
WINDOW = 128
SSM_W = 1024
GROUP = 16
N_GROUPS = 64
STATE = 64
N_STATES = N_GROUPS * STATE
IN_W = 8704
NORM_EPS = 1e-6
N_CHIPS = 4
CW = 512
OFF_AGATE, OFF_U, OFF_Z, OFF_GA, OFF_GS = 3, 5, 7, 9, 13

SSM_T = 256
SSM_L = SSM_T // SUBLANES
SSM_JB = 8
SSM_SB = N_STATES // SSM_JB

ADAM_LR, ADAM_B1, ADAM_B2, ADAM_EPS, ADAM_WD, ADAM_STEP = 0.001, 0.9, 0.999, 1e-08, 0.01, 10

MESH = pl.DeviceIdType.MESH
_ANY = pl.BlockSpec(memory_space=pl.ANY)


def _params(sem=None):
    return pltpu.CompilerParams(dimension_semantics=sem, vmem_limit_bytes=VMEM_LIMIT)


def _mm(a, b, *, mode, name, tm, tn, tk, out_dtype=F32, b_blocked=False, out_blocked=False, rows_outer=False,
        deps=()):
    nd = len(deps)
    if mode == "tn":
        K, M = a.shape
    else:
        M, K = a.shape
    if mode == "nn":
        N = b.shape[0] * b.shape[2] if b_blocked else b.shape[1]
    elif mode == "nt":
        N = b.shape[1] if b_blocked else b.shape[0]
    else:
        N = b.shape[1]
    tm, tn, tk = min(tm, M), min(tn, N), min(tk, K)
    nj, ni, nk = N // tn, M // tm, K // tk
    assert nj * tn == N and ni * tm == M and nk * tk == K, (name, M, N, K)
    dims = {"nn": (((1,), (0,)), ((), ())), "nt": (((1,), (1,)), ((), ())), "tn": (((0,), (0,)), ((), ()))}[mode]

    if mode == "tn":
        a_spec = pl.BlockSpec((tk, tm), lambda j, i, k: (k, i))
    else:
        a_spec = pl.BlockSpec((tm, tk), lambda j, i, k: (i, k))
    if mode == "nn":
        if b_blocked:
            assert b.shape[0] == nj and b.shape[2] == tn
            b_spec = pl.BlockSpec((None, tk, tn), lambda j, i, k: (j, k, 0))
        else:
            b_spec = pl.BlockSpec((tk, tn), lambda j, i, k: (k, j))
    elif mode == "nt":
        if b_blocked:
            assert b.shape[0] == nk and b.shape[2] == tk
            b_spec = pl.BlockSpec((None, tn, tk), lambda j, i, k: (k, j, 0))
        else:
            b_spec = pl.BlockSpec((tn, tk), lambda j, i, k: (j, k))
    else:
        b_spec = pl.BlockSpec((tk, tn), lambda j, i, k: (k, j))
    whole_out = out_blocked and nj == 1
    if whole_out:
        assert ni == 1
        o_spec = pl.BlockSpec((N_CHIPS, tm, tn // N_CHIPS), lambda j, i, k: (0, 0, 0))
        o_shape = jax.ShapeDtypeStruct((N_CHIPS, M, tn // N_CHIPS), out_dtype)
    elif out_blocked:
        assert nj == N_CHIPS
        o_spec = pl.BlockSpec((None, tm, tn), lambda j, i, k: (j, i, 0))
        o_shape = jax.ShapeDtypeStruct((nj, M, tn), out_dtype)
    else:
        o_spec = pl.BlockSpec((tm, tn), lambda j, i, k: (i, j))
        o_shape = jax.ShapeDtypeStruct((M, N), out_dtype)
    use_acc = nk > 1 and (out_dtype != F32 or whole_out)

    def body(a_ref, b_ref, *rest):
        o_ref, scratch = rest[nd], rest[nd + 1:]

        def product():
            return lax.dot_general(a_ref[...].astype(_MXU), b_ref[...].astype(_MXU), dims, preferred_element_type=F32)

        def write(result):
            if whole_out:
                w = tn // N_CHIPS
                for c in range(N_CHIPS):
                    o_ref[c] = result[:, c * w:(c + 1) * w].astype(o_ref.dtype)
            else:
                o_ref[...] = result.astype(o_ref.dtype)

        if nk == 1:
            write(product())
            return
        k = pl.program_id(2)
        acc = scratch[0] if use_acc else o_ref

        @pl.when(k == 0)
        def _():
            acc[...] = jnp.zeros_like(acc)

        acc[...] += product()

        if use_acc:
            @pl.when(k == nk - 1)
            def _():
                write(acc[...])

    specs = [a_spec, b_spec, o_spec]
    grid = (nj, ni, nk)
    if rows_outer:
        specs = [pl.BlockSpec(s.block_shape, lambda i, j, k, f=s.index_map: f(j, i, k)) for s in specs]
        grid = (ni, nj, nk)
    return pl.pallas_call(
        body, name=name, grid=grid, in_specs=specs[:2] + [_ANY] * nd, out_specs=specs[2],
        out_shape=o_shape, scratch_shapes=[pltpu.VMEM((tm, tn), F32)] if use_acc else [],
        compiler_params=_params(("parallel", "parallel", "arbitrary")),
    )(a, b, *deps)


def _mm_chip_block(a, b4, blk, prev, *, name, tm=1024, out_dtype=F32, deps=()):
    M, K = a.shape
    nchip, _, C = b4.shape
    tm = min(tm, M)
    extra = ([] if prev is None else [prev]) + list(deps)

    def body(blk_ref, a_ref, b_ref, *rest):
        rest[-1][...] = jnp.dot(a_ref[...].astype(_MXU), b_ref[...].astype(_MXU),
                                preferred_element_type=F32).astype(rest[-1].dtype)

    return pl.pallas_call(
        body, name=name,
        grid_spec=pltpu.PrefetchScalarGridSpec(
            num_scalar_prefetch=1, grid=(M // tm,),
            in_specs=[pl.BlockSpec((tm, K), lambda i, c: (i, 0)), pl.BlockSpec((None, K, C), lambda i, c: (c[0], 0, 0))]
            + [_ANY] * len(extra),
            out_specs=pl.BlockSpec((tm, C), lambda i, c: (i, c[0]))),
        out_shape=jax.ShapeDtypeStruct((M, nchip * C), out_dtype),
        input_output_aliases={} if prev is None else {3: 0},
        compiler_params=_params(("arbitrary",)),
    )(blk, a, b4, *extra)


def _mm_merge_out_loss(proj, y_a, y_s, w_out, x, target, *, tm=256):
    rows, d = x.shape
    ncol = d // CW

    def body(*refs):
        ga_refs, gs_refs = refs[:ncol], refs[ncol:2 * ncol]
        ya_ref, ys_ref, w_ref, x_ref, t_ref, m_ref, d_ref, db_ref, sq_ref = refs[2 * ncol:]
        for j in range(ncol):
            cols = slice(j * CW, (j + 1) * CW)
            m_ref[:, cols] = (_sigmoid(ga_refs[j][...].astype(F32)) * ya_ref[:, cols].astype(F32)
                              + _sigmoid(gs_refs[j][...].astype(F32)) * ys_ref[:, cols].astype(F32)).astype(m_ref.dtype)
        mo = jnp.dot(m_ref[...], w_ref[...].astype(_MXU), preferred_element_type=F32)
        err = (x_ref[...] + mo) - t_ref[...]
        dout = err * (1.0 / d)
        d_ref[...] = dout
        db_ref[...] = dout.astype(db_ref.dtype)
        part = _colsum(err * err)
        i = pl.program_id(0)

        @pl.when(i == 0)
        def _():
            sq_ref[...] = part

        @pl.when(i > 0)
        def _():
            sq_ref[...] += part

    tile = pl.BlockSpec((tm, d), lambda i: (i, 0))
    gate = [pl.BlockSpec((tm, CW), lambda i, c=off + j: (i, c)) for off in (OFF_GA, OFF_GS) for j in range(ncol)]
    return pl.pallas_call(
        body, name="mm_merge_out_loss", grid=(rows // tm,),
        in_specs=gate + [tile, tile, pl.BlockSpec((d, d), lambda i: (0, 0), pipeline_mode=pl.Buffered(1)), tile, tile],
        out_specs=[tile, tile, tile, pl.BlockSpec((1, d), lambda i: (0, 0))],
        out_shape=[jax.ShapeDtypeStruct((rows, d), _MXU), jax.ShapeDtypeStruct((rows, d), F32),
                   jax.ShapeDtypeStruct((rows, d), _MXU), jax.ShapeDtypeStruct((1, d), F32)],
        compiler_params=_params(("arbitrary",)),
    )(*([proj] * (2 * ncol)), y_a, y_s, w_out, x, target)


def _mm_merge_bwd(dout_b, w_out, proj, y_a, y_s, *, tm=512):
    rows, d = y_a.shape
    ncol = d // CW

    def body(do_ref, w_ref, *refs):
        ga_refs, gs_refs = refs[:ncol], refs[ncol:2 * ncol]
        ya_ref, ys_ref, dya_ref, dys_ref, dg_ref = refs[2 * ncol:]
        dm = lax.dot_general(do_ref[...].astype(_MXU), w_ref[...].astype(_MXU), _NT, preferred_element_type=F32)
        for j in range(ncol):
            cols = slice(j * CW, (j + 1) * CW)
            dmj = dm[:, cols]
            sa, ss = _sigmoid(ga_refs[j][...].astype(F32)), _sigmoid(gs_refs[j][...].astype(F32))
            dya_ref[:, cols] = (sa * dmj).astype(dya_ref.dtype)
            dys_ref[:, cols] = (ss * dmj).astype(dys_ref.dtype)
            dg_ref[:, cols] = (dmj * ya_ref[:, cols].astype(F32) * sa * (1.0 - sa)).astype(dg_ref.dtype)
            dg_ref[:, d + j * CW:d + (j + 1) * CW] = (dmj * ys_ref[:, cols].astype(F32) * ss
                                                      * (1.0 - ss)).astype(dg_ref.dtype)

    tile = pl.BlockSpec((tm, d), lambda i: (i, 0))
    gate = [pl.BlockSpec((tm, CW), lambda i, c=off + j: (i, c)) for off in (OFF_GA, OFF_GS) for j in range(ncol)]
    both = pl.BlockSpec((pl.Element(tm), pl.Element(2 * d)), lambda i: (i * tm, OFF_GA * CW))
    return pl.pallas_call(
        body, name="mm_merge_bwd", grid=(rows // tm,),
        in_specs=[tile, pl.BlockSpec((d, d), lambda i: (0, 0), pipeline_mode=pl.Buffered(1))] + gate + [tile, tile],
        out_specs=[tile, tile, both],
        out_shape=[jax.ShapeDtypeStruct((rows, d), _MXU)] * 2 + [jax.ShapeDtypeStruct((rows, IN_W), _MXU)],
        compiler_params=_params(("arbitrary",)),
    )(dout_b, w_out, *([proj] * (2 * ncol)), y_a, y_s)


def _mm_glu_gate(yg, w_glu4, b_glu, proj, *, tm=1024):
    rows, k = yg.shape
    nj, _, tn = w_glu4.shape
    w = nj * tn // 2
    tm = min(tm, rows)

    def body(a_ref, w_ref, ba_ref, bb_ref, z0_ref, z1_ref, glu_ref, ys_ref):
        j = pl.program_id(1)
        for c in range(nj):
            @pl.when(j == c)
            def _(c=c):
                glu_ref[:, c * tn:(c + 1) * tn] = jnp.dot(a_ref[...].astype(_MXU), w_ref[...].astype(_MXU),
                                                          preferred_element_type=F32).astype(glu_ref.dtype)

        @pl.when(j == nj - 1)
        def _():
            z = jnp.concatenate([z0_ref[...], z1_ref[...]], axis=1).astype(F32)
            ys_ref[...] = ((glu_ref[:, :w].astype(F32) + ba_ref[...]) * _sigmoid(glu_ref[:, w:].astype(F32) + bb_ref[...])
                           * (z * _sigmoid(z))).astype(ys_ref.dtype)

    bias = lambda c: pl.BlockSpec((1, w), lambda i, j: (0, c))
    zcol = lambda c: pl.BlockSpec((tm, CW), lambda i, j: (i, OFF_Z + c))
    return pl.pallas_call(
        body, name="mm_glu_gate", grid=(rows // tm, nj),
        in_specs=[pl.BlockSpec((tm, k), lambda i, j: (i, 0)), pl.BlockSpec((None, k, tn), lambda i, j: (j, 0, 0)),
                  bias(0), bias(1), zcol(0), zcol(1)],
        out_specs=[pl.BlockSpec((tm, nj * tn), lambda i, j: (i, 0)), pl.BlockSpec((tm, w), lambda i, j: (i, 0))],
        out_shape=[jax.ShapeDtypeStruct((rows, nj * tn), _MXU), jax.ShapeDtypeStruct((rows, w), _MXU)],
        compiler_params=_params(("arbitrary", "arbitrary")),
    )(yg, w_glu4, b_glu, b_glu, proj, proj)


def _mm_ssm_gate_bwd(d_ys, w_sp4, glu, b_glu, proj, d_proj, *, tm=1024):
    rows, w = glu.shape[0], glu.shape[1] // 2
    nk, tk = w_sp4.shape[0], w_sp4.shape[2]
    tm = min(tm, rows)

    def body(dy_ref, w_ref, ga_ref, gb_ref, ba_ref, bb_ref, z0_ref, z1_ref, buf_ref, dg_ref, dz_ref, db_ref, acc):
        i, k = pl.program_id(0), pl.program_id(1)

        @pl.when(k == 0)
        def _():
            acc[...] = jnp.zeros_like(acc)

        acc[...] += lax.dot_general(dy_ref[...].astype(_MXU), w_ref[...].astype(_MXU), _NT, preferred_element_type=F32)

        @pl.when(k == nk - 1)
        def _():
            dv = acc[...]
            a, sb = ga_ref[...].astype(F32) + ba_ref[...], _sigmoid(gb_ref[...].astype(F32) + bb_ref[...])
            f, df = _silu_and_grad(jnp.concatenate([z0_ref[...], z1_ref[...]], axis=1).astype(F32))
            dga = dv * sb * f
            dgb = dv * a * f * sb * (1.0 - sb)
            dg_ref[:, :w] = dga.astype(dg_ref.dtype)
            dg_ref[:, w:] = dgb.astype(dg_ref.dtype)
            dz_ref[...] = (dv * a * sb * df).astype(dz_ref.dtype)
            part = jnp.concatenate([_colsum(dga), _colsum(dgb)], axis=1)

            @pl.when(i == 0)
            def _():
                db_ref[...] = part

            @pl.when(i > 0)
            def _():
                db_ref[...] += part

    half = lambda c: pl.BlockSpec((tm, w), lambda i, k: (i, c))
    bias = lambda c: pl.BlockSpec((1, w), lambda i, k: (0, c))
    zcol = lambda c: pl.BlockSpec((tm, CW), lambda i, k: (i, OFF_Z + c))
    return pl.pallas_call(
        body, name="mm_ssm_gate_bwd", grid=(rows // tm, nk),
        in_specs=[pl.BlockSpec((tm, tk), lambda i, k: (i, k)), pl.BlockSpec((None, w, tk), lambda i, k: (k, 0, 0)),
                  half(0), half(1), bias(0), bias(1), zcol(0), zcol(1), _ANY],
        out_specs=[pl.BlockSpec((tm, 2 * w), lambda i, k: (i, 0)),
                   pl.BlockSpec((pl.Element(tm), pl.Element(w)), lambda i, k: (i * tm, OFF_Z * CW)),
                   pl.BlockSpec((1, 2 * w), lambda i, k: (0, 0))],
        out_shape=[jax.ShapeDtypeStruct((rows, 2 * w), _MXU), jax.ShapeDtypeStruct(d_proj.shape, d_proj.dtype),
                   jax.ShapeDtypeStruct((1, 2 * w), F32)],
        input_output_aliases={8: 1},
        scratch_shapes=[pltpu.VMEM((tm, w), F32)],
        compiler_params=_params(("arbitrary", "arbitrary")),
    )(d_ys, w_sp4, glu, glu, b_glu, b_glu, proj, proj, d_proj)


def _colsum(v):
    return jnp.sum(v, axis=0, keepdims=True)


def _sigmoid(v):
    return jax.nn.sigmoid(v)


def _silu_and_grad(v):
    s = _sigmoid(v)
    return v * s, s * (1.0 + v * (1.0 - s))


def _rms_fwd(x, w, *, tm=512, deps=()):
    rows, d = x.shape
    nd = len(deps)

    def body(x_ref, w_ref, *rest):
        h_ref, r_ref = rest[nd:]
        xv = x_ref[...]
        r = lax.rsqrt(jnp.mean(xv * xv, axis=-1, keepdims=True) + NORM_EPS)
        h_ref[...] = (xv * r * w_ref[...]).astype(h_ref.dtype)
        r_ref[...] = r

    return pl.pallas_call(
        body, name="rms_fwd", grid=(rows // tm,),
        in_specs=[pl.BlockSpec((tm, d), lambda i: (i, 0)), pl.BlockSpec((1, d), lambda i: (0, 0))] + [_ANY] * nd,
        out_specs=[pl.BlockSpec((tm, d), lambda i: (i, 0)), pl.BlockSpec((tm, 1), lambda i: (i, 0))],
        out_shape=[jax.ShapeDtypeStruct((rows, d), _MXU), jax.ShapeDtypeStruct((rows, 1), F32)],
        compiler_params=_params(("arbitrary",)),
    )(x, w, *deps)


def _rms_bwd(dh, x, rstd, w, dout, *, tm=512):
    rows, d = x.shape

    def body(dh_ref, x_ref, r_ref, w_ref, do_ref, gx_ref, gw_ref):
        dhv, xv, r, wv = dh_ref[...], x_ref[...], r_ref[...], w_ref[...]
        xr = xv * r
        t = jnp.mean(dhv * wv * xr, axis=-1, keepdims=True)
        gx_ref[...] = do_ref[...] + r * (wv * dhv - xr * t)
        part = _colsum(dhv * xr)
        i = pl.program_id(0)

        @pl.when(i == 0)
        def _():
            gw_ref[...] = part

        @pl.when(i > 0)
        def _():
            gw_ref[...] += part

    return pl.pallas_call(
        body, name="rms_bwd", grid=(rows // tm,),
        in_specs=[pl.BlockSpec((tm, d), lambda i: (i, 0)), pl.BlockSpec((tm, d), lambda i: (i, 0)),
                  pl.BlockSpec((tm, 1), lambda i: (i, 0)), pl.BlockSpec((1, d), lambda i: (0, 0)),
                  pl.BlockSpec((tm, d), lambda i: (i, 0))],
        out_specs=[pl.BlockSpec((tm, d), lambda i: (i, 0)), pl.BlockSpec((1, d), lambda i: (0, 0))],
        out_shape=[jax.ShapeDtypeStruct((rows, d), F32), jax.ShapeDtypeStruct((1, d), F32)],
        compiler_params=_params(("arbitrary",)),
    )(dh, x, rstd, w, dout)


_NT = (((1,), (1,)), ((), ()))
_TN = (((0,), (0,)), ((), ()))


QKV_W = ATTN_W + 2 * KV_W
HEADS_PER_TILE = LANES // HEAD_DIM


def _low_half(rows):
    return lax.broadcasted_iota(jnp.int32, (rows, LANES), 1) < HEAD_DIM


def _pair_mean(t, low):
    m_lo = jnp.sum(jnp.where(low, t, 0.0), axis=-1, keepdims=True)
    m_hi = jnp.sum(jnp.where(low, 0.0, t), axis=-1, keepdims=True)
    return jnp.where(low, m_lo, m_hi) * (1.0 / HEAD_DIM)


def _pair_rstd(t, low):
    return lax.rsqrt(_pair_mean(t * t, low) + NORM_EPS)


def _dup_half(t, hi, low):
    swapped = pltpu.roll(t, HEAD_DIM, 1)
    return jnp.where(low, swapped, t) if hi else jnp.where(low, t, swapped)


def _fold_halves(t):
    return t + pltpu.roll(t, HEAD_DIM, 1)


def _split_heads(t, low):
    return [jnp.where(low, t, 0.0), jnp.where(low, 0.0, t)]


def _stacked_band_mask(n):
    rows = Q_PER_KV * WINDOW
    qi = lax.broadcasted_iota(jnp.int32, (rows, 2 * WINDOW), 0) % WINDOW + WINDOW
    kj = lax.broadcasted_iota(jnp.int32, (rows, 2 * WINDOW), 1)
    diff = qi - kj
    first_key = jnp.where(n > 0, 0, WINDOW)
    return (diff >= 0) & (diff < WINDOW) & (kj >= first_key)


def _stacked_sinks(sink_ref, g):
    blk = lax.broadcasted_iota(jnp.int32, (Q_PER_KV * WINDOW, 1), 0) // WINDOW
    col = jnp.full((Q_PER_KV * WINDOW, 1), sink_ref[Q_PER_KV * g], F32)
    for r in range(1, Q_PER_KV):
        col = jnp.where(blk == r, sink_ref[Q_PER_KV * g + r], col)
    return col


def _attn_in_specs(nblk, rev):
    def cur(n):
        return (nblk - 1 - n) if rev else n

    q_spec = pl.BlockSpec((WINDOW, ATTN_W), lambda n: (cur(n), 0))
    kvc_spec = pl.BlockSpec((WINDOW, 2 * KV_W), lambda n: (cur(n), ATTN_W // (2 * KV_W)))
    kvp_spec = pl.BlockSpec((WINDOW, 2 * KV_W), lambda n: (jnp.maximum(cur(n) - 1, 0), ATTN_W // (2 * KV_W)))
    w_spec = pl.BlockSpec((1, LANES), lambda n: (0, 0))
    l_spec = pl.BlockSpec((WINDOW, N_Q_HEADS), lambda n: (cur(n), 0))
    gate_specs = [pl.BlockSpec((WINDOW, CW), lambda n, col=OFF_AGATE + j: (cur(n), col)) for j in range(ATTN_W // CW)]
    return q_spec, kvc_spec, kvp_spec, w_spec, l_spec, gate_specs


def _attn2_fwd(proj, qw2, kw2, sinks, deps=()):
    seq = proj.shape[0]
    nblk = seq // WINDOW
    scale = 1.0 / math.sqrt(HEAD_DIM)
    q_spec, kvc_spec, kvp_spec, w_spec, l_spec, gate_specs = _attn_in_specs(nblk, False)
    nd, ng = len(deps), len(gate_specs)

    def body(sink_ref, q_ref, kvc_ref, kvp_ref, qw_ref, kw_ref, *rest):
        gate_refs = rest[:ng]
        o_ref, lse_ref, ya_ref = rest[ng + nd:]
        n = pl.program_id(0)
        low, low2 = _low_half(WINDOW), _low_half(2 * WINDOW)
        valid = _stacked_band_mask(n)
        head_lane = lax.broadcasted_iota(jnp.int32, (WINDOW, N_Q_HEADS), 1)
        kv = jnp.concatenate([kvp_ref[...], kvc_ref[...]], axis=0).astype(F32)
        qwv, kwv = qw_ref[...], kw_ref[...]
        lse_blk = jnp.zeros((WINDOW, N_Q_HEADS), F32)
        for t in range(N_KV_HEADS // HEADS_PER_TILE):
            kt = kv[:, t * LANES:(t + 1) * LANES]
            vt = kv[:, KV_W + t * LANES:KV_W + (t + 1) * LANES]
            kn = kt * _pair_rstd(kt, low2) * kwv
            for hi in range(HEADS_PER_TILE):
                g = HEADS_PER_TILE * t + hi
                kdup = _dup_half(kn, hi, low2).astype(_MXU)
                vdup = _dup_half(vt, hi, low2).astype(_MXU)
                stack = []
                for tq in (2 * g, 2 * g + 1):
                    qt = q_ref[:, tq * LANES:(tq + 1) * LANES].astype(F32)
                    stack += _split_heads(qt * _pair_rstd(qt, low) * qwv, low)
                qs = jnp.concatenate(stack, axis=0).astype(_MXU)
                s = lax.dot_general(qs, kdup, _NT, preferred_element_type=F32) * scale
                s = jnp.where(valid, s, -1e30)
                sink = _stacked_sinks(sink_ref, g)
                m = jnp.maximum(jnp.max(s, axis=-1, keepdims=True), sink)
                e = jnp.exp(s - m)
                z = jnp.sum(e, axis=-1, keepdims=True) + jnp.exp(sink - m)
                o = jnp.dot((e / z).astype(_MXU), vdup, preferred_element_type=F32)
                for i, tq in enumerate((2 * g, 2 * g + 1)):
                    tile = slice(tq * LANES, (tq + 1) * LANES)
                    out = jnp.where(low, o[2 * i * WINDOW:(2 * i + 1) * WINDOW],
                                    o[(2 * i + 1) * WINDOW:(2 * i + 2) * WINDOW])
                    gate = gate_refs[tq * LANES // CW][:, tq * LANES % CW:tq * LANES % CW + LANES].astype(F32)
                    o_ref[:, tile] = out.astype(o_ref.dtype)
                    ya_ref[:, tile] = (out * (gate * _sigmoid(gate))).astype(ya_ref.dtype)
                lse = m + jnp.log(z)
                for r in range(Q_PER_KV):
                    lse_blk = jnp.where(head_lane == Q_PER_KV * g + r, lse[r * WINDOW:(r + 1) * WINDOW], lse_blk)
        lse_ref[...] = lse_blk

    return pl.pallas_call(
        body, name="attn_fwd", grid=(nblk,),
        in_specs=[pl.BlockSpec(memory_space=pltpu.SMEM), q_spec, kvc_spec, kvp_spec, w_spec, w_spec] + gate_specs
        + [_ANY] * nd,
        out_specs=[q_spec, l_spec, q_spec],
        out_shape=[jax.ShapeDtypeStruct((seq, ATTN_W), _MXU), jax.ShapeDtypeStruct((seq, N_Q_HEADS), F32),
                   jax.ShapeDtypeStruct((seq, ATTN_W), _MXU)],
        compiler_params=_params(("arbitrary",)),
    )(sinks, proj, proj, proj, qw2, kw2, *([proj] * ng), *deps)


def _attn2_bwd(proj, qw2, kw2, sinks, lse, attn, dya, d_proj, deps=()):
    seq = proj.shape[0]
    nblk = seq // WINDOW
    scale = 1.0 / math.sqrt(HEAD_DIM)
    q_spec, kvc_spec, kvp_spec, w_spec, l_spec, gate_specs = _attn_in_specs(nblk, True)
    s_spec = pl.BlockSpec((1, N_Q_HEADS), lambda n: (0, 0))
    d_spec = pl.BlockSpec((WINDOW, QKV_W + ATTN_W), lambda n: (nblk - 1 - n, 0))
    deps = list(deps) + [d_proj]
    nd, ng = len(deps), len(gate_specs)

    def body(sink_ref, q_ref, kvc_ref, kvp_ref, qw_ref, kw_ref, lse_ref, attn_ref, dya_ref, *rest):
        gate_refs = rest[:ng]
        d_ref, dqw_ref, dkw_ref, dsk_ref, carry, do_ref = rest[ng + nd:]
        step = pl.program_id(0)
        n = nblk - 1 - step

        @pl.when(step == 0)
        def _():
            carry[...] = jnp.zeros_like(carry)
            dqw_ref[...] = jnp.zeros_like(dqw_ref)
            dkw_ref[...] = jnp.zeros_like(dkw_ref)
            dsk_ref[...] = jnp.zeros_like(dsk_ref)

        for j, g_ref in enumerate(gate_refs):
            cols = slice(j * CW, (j + 1) * CW)
            f, df = _silu_and_grad(g_ref[...].astype(F32))
            dv = dya_ref[:, cols]
            do_ref[:, cols] = dv * f
            d_ref[:, QKV_W + j * CW:QKV_W + (j + 1) * CW] = (dv * attn_ref[:, cols].astype(F32) * df).astype(d_ref.dtype)

        low, low2 = _low_half(WINDOW), _low_half(2 * WINDOW)
        valid = _stacked_band_mask(n)
        head_lane = lax.broadcasted_iota(jnp.int32, (WINDOW, N_Q_HEADS), 1)
        sink_lane = lax.broadcasted_iota(jnp.int32, (1, N_Q_HEADS), 1)
        kv = jnp.concatenate([kvp_ref[...], kvc_ref[...]], axis=0).astype(F32)
        qwv, kwv = qw_ref[...], kw_ref[...]
        lse_blk = lse_ref[...]
        dqw = jnp.zeros((1, LANES), F32)
        dkw = jnp.zeros((1, LANES), F32)
        dsk = jnp.zeros((1, N_Q_HEADS), F32)
        for t in range(N_KV_HEADS // HEADS_PER_TILE):
            kt = kv[:, t * LANES:(t + 1) * LANES]
            vt = kv[:, KV_W + t * LANES:KV_W + (t + 1) * LANES]
            rk = _pair_rstd(kt, low2)
            kn = kt * rk * kwv
            dkn_t = jnp.zeros((2 * WINDOW, LANES), F32)
            dv_t = jnp.zeros((2 * WINDOW, LANES), F32)
            for hi in range(HEADS_PER_TILE):
                g = HEADS_PER_TILE * t + hi
                kdup = _dup_half(kn, hi, low2).astype(_MXU)
                vdup = _dup_half(vt, hi, low2).astype(_MXU)
                tiles = (2 * g, 2 * g + 1)
                qx, rq, stack, dstack, lse_rows = [], [], [], [], []
                for tq in tiles:
                    qt = q_ref[:, tq * LANES:(tq + 1) * LANES].astype(F32)
                    r = _pair_rstd(qt, low)
                    rq.append(r)
                    qx.append(qt * r)
                    stack += _split_heads(qx[-1] * qwv, low)
                    dstack += _split_heads(do_ref[:, tq * LANES:(tq + 1) * LANES], low)
                for r in range(Q_PER_KV):
                    lse_rows.append(jnp.sum(jnp.where(head_lane == Q_PER_KV * g + r, lse_blk, 0.0), axis=-1, keepdims=True))
                qs = jnp.concatenate(stack, axis=0).astype(_MXU)
                dos = jnp.concatenate(dstack, axis=0).astype(_MXU)
                lse_col = jnp.concatenate(lse_rows, axis=0)
                s = lax.dot_general(qs, kdup, _NT, preferred_element_type=F32) * scale
                s = jnp.where(valid, s, -1e30)
                p = jnp.exp(s - lse_col)
                dp = lax.dot_general(dos, vdup, _NT, preferred_element_type=F32)
                dsum = jnp.sum(p * dp, axis=-1, keepdims=True)
                ds = (p * (dp - dsum) * scale).astype(_MXU)
                dsink = -jnp.exp(_stacked_sinks(sink_ref, g) - lse_col) * dsum
                for r in range(Q_PER_KV):
                    dsk = dsk + jnp.where(sink_lane == Q_PER_KV * g + r, _colsum(dsink[r * WINDOW:(r + 1) * WINDOW]), 0.0)
                dv_g = _fold_halves(lax.dot_general(p.astype(_MXU), dos, _TN, preferred_element_type=F32))
                dkn_g = _fold_halves(lax.dot_general(ds, qs, _TN, preferred_element_type=F32))
                dv_t = jnp.where(low2, dv_t, dv_g) if hi else jnp.where(low2, dv_g, dv_t)
                dkn_t = jnp.where(low2, dkn_t, dkn_g) if hi else jnp.where(low2, dkn_g, dkn_t)
                dqn = jnp.dot(ds, kdup, preferred_element_type=F32)
                for i, tq in enumerate(tiles):
                    dqn_t = jnp.where(low, dqn[2 * i * WINDOW:(2 * i + 1) * WINDOW],
                                      dqn[(2 * i + 1) * WINDOW:(2 * i + 2) * WINDOW])
                    dq = rq[i] * (qwv * dqn_t - qx[i] * _pair_mean(dqn_t * qwv * qx[i], low))
                    d_ref[:, tq * LANES:(tq + 1) * LANES] = dq.astype(d_ref.dtype)
                    dqw = dqw + _colsum(dqn_t * qx[i])
            k_cols = slice(t * LANES, (t + 1) * LANES)
            v_cols = slice(KV_W + t * LANES, KV_W + (t + 1) * LANES)
            dkn_c = dkn_t[WINDOW:] + carry[:, k_cols]
            rc = rk[WINDOW:]
            kx = kt[WINDOW:] * rc
            dk = rc * (kwv * dkn_c - kx * _pair_mean(dkn_c * kwv * kx, low))
            d_ref[:, ATTN_W + t * LANES:ATTN_W + (t + 1) * LANES] = dk.astype(d_ref.dtype)
            d_ref[:, ATTN_W + KV_W + t * LANES:ATTN_W + KV_W + (t + 1) * LANES] = (
                dv_t[WINDOW:] + carry[:, v_cols]).astype(d_ref.dtype)
            carry[:, k_cols] = dkn_t[:WINDOW]
            carry[:, v_cols] = dv_t[:WINDOW]
            dkw = dkw + _colsum(dkn_c * kx)
        dqw_ref[...] += dqw
        dkw_ref[...] += dkw
        dsk_ref[...] += dsk

    return pl.pallas_call(
        body, name="attn_bwd", grid=(nblk,),
        in_specs=[pl.BlockSpec(memory_space=pltpu.SMEM), q_spec, kvc_spec, kvp_spec, w_spec, w_spec, l_spec, q_spec,
                  q_spec] + gate_specs + [_ANY] * nd,
        out_specs=[d_spec, w_spec, w_spec, s_spec],
        out_shape=[jax.ShapeDtypeStruct(d_proj.shape, d_proj.dtype), jax.ShapeDtypeStruct((1, LANES), F32),
                   jax.ShapeDtypeStruct((1, LANES), F32), jax.ShapeDtypeStruct((1, N_Q_HEADS), F32)],
        input_output_aliases={9 + ng + nd - 1: 0},
        scratch_shapes=[pltpu.VMEM((WINDOW, 2 * KV_W), F32), pltpu.VMEM((WINDOW, ATTN_W), F32)],
        compiler_params=_params(("arbitrary",)),
    )(sinks, proj, proj, proj, qw2, kw2, lse, attn, dya, *([proj] * ng), *deps)


def _ssm_discretise(a_re, a_im, log_dt):
    dt = jnp.exp(log_dt)
    mag = jnp.exp(dt * a_re)
    ab_re = mag * jnp.cos(dt * a_im)
    ab_im = mag * jnp.sin(dt * a_im)
    num_re = ab_re - 1.0
    num_im = ab_im
    den = a_re * a_re + a_im * a_im
    cf_re = (num_re * a_re + num_im * a_im) / den
    cf_im = (num_im * a_re - num_re * a_im) / den
    return ab_re, ab_im, cf_re, cf_im


def _ssm_params_fwd(a_re, a_im, log_dt):
    shp = jax.ShapeDtypeStruct(a_re.shape, F32)

    def body(are_ref, aim_ref, ldt_ref, abr_ref, abi_ref, cfr_ref, cfi_ref, alr_ref, ali_ref):
        abr, abi, cfr, cfi = _ssm_discretise(are_ref[...], aim_ref[...], ldt_ref[...])
        abr_ref[...], abi_ref[...], cfr_ref[...], cfi_ref[...] = abr, abi, cfr, cfi
        pr, pi = abr, abi
        for _ in range(int(math.log2(SSM_L))):
            pr, pi = pr * pr - pi * pi, 2.0 * pr * pi
        alr_ref[...], ali_ref[...] = pr, pi

    return pl.pallas_call(body, name="ssm_params_fwd", out_shape=[shp] * 6)(a_re, a_im, log_dt)


def _ssm_params_bwd(a_re, a_im, log_dt, d_abr, d_abi, d_cfr, d_cfi):
    def body(are_ref, aim_ref, ldt_ref, g0, g1, g2, g3, dare_ref, daim_ref, dldt_ref):
        _, vjp = jax.vjp(_ssm_discretise, are_ref[...], aim_ref[...], ldt_ref[...])
        dare_ref[...], daim_ref[...], dldt_ref[...] = vjp((g0[...], g1[...], g2[...], g3[...]))

    return pl.pallas_call(
        body, name="ssm_params_bwd",
        out_shape=[jax.ShapeDtypeStruct(a_re.shape, F32), jax.ShapeDtypeStruct(a_im.shape, F32),
                   jax.ShapeDtypeStruct(log_dt.shape, F32)],
    )(a_re, a_im, log_dt, d_abr, d_abi, d_cfr, d_cfi)


def _scan_cols(j):
    return pl.ds(j * SSM_SB, SSM_SB)


def _rows8(r):
    return pl.ds(pl.multiple_of(r * SUBLANES, SUBLANES), SUBLANES)


def _bcast8(row):
    return jnp.broadcast_to(row, (SUBLANES, row.shape[-1]))


def _token_order_pick():
    tok = lax.broadcasted_iota(jnp.int32, (SSM_T, SSM_T), 0)
    row = lax.broadcasted_iota(jnp.int32, (SSM_T, SSM_T), 1)
    return (row == SUBLANES * (tok % SSM_L) + tok // SSM_L).astype(_MXU)


SCAN_UNROLL = 16


def _scan_loop(n, step, init):
    def trip(o, carry):
        for i in range(SCAN_UNROLL):
            carry = step(o * SCAN_UNROLL + i, carry)
        return carry

    return lax.fori_loop(0, n // SCAN_UNROLL, trip, init)


def _ssm_fwd(u, b_re, b_im, c_re, c_im, d_skip, coef):
    seq = u.shape[0]
    nc = seq // SSM_T
    T, L = SSM_T, SSM_L

    def body(u_ref, bre_ref, bim_ref, cre_ref, cim_ref, d_ref, are_ref, aim_ref, cfr_ref, cfi_ref, alr_ref, ali_ref,
             y_ref, yg_ref, sre_ref, sim_ref, ire_ref, iim_ref, car_re, car_im, end_re, end_im, yg_scan):
        c = pl.program_id(0)

        @pl.when(c == 0)
        def _():
            car_re[...] = jnp.zeros_like(car_re)
            car_im[...] = jnp.zeros_like(car_im)

        for j in range(SSM_JB):
            ub = u_ref[:, j * LANES:(j + 1) * LANES].astype(_MXU)
            bur = jnp.dot(ub, bre_ref[j], preferred_element_type=F32)
            bui = jnp.dot(ub, bim_ref[j], preferred_element_type=F32)
            cfr, cfi = cfr_ref[:, _scan_cols(j)], cfi_ref[:, _scan_cols(j)]
            sre_ref[:, _scan_cols(j)] = cfr * bur - cfi * bui
            sim_ref[:, _scan_cols(j)] = cfr * bui + cfi * bur

        for j in range(SSM_JB):
            cols = _scan_cols(j)
            ar, ai = _bcast8(are_ref[:, cols]), _bcast8(aim_ref[:, cols])

            def step1(r, s, cols=cols, ar=ar, ai=ai):
                sr, si = s
                rows = _rows8(r)
                return (ar * sr - ai * si + sre_ref[rows, cols], ar * si + ai * sr + sim_ref[rows, cols])

            zero = jnp.zeros((SUBLANES, SSM_SB), F32)
            er, ei = _scan_loop(L, step1, (zero, zero))
            end_re[:, cols] = er
            end_im[:, cols] = ei

        alr, ali = alr_ref[...], ali_ref[...]
        cr, ci = car_re[...], car_im[...]
        ire_ref[0:1, :] = cr
        iim_ref[0:1, :] = ci
        for i in range(1, SUBLANES):
            er, ei = end_re[i - 1:i, :], end_im[i - 1:i, :]
            cr, ci = alr * cr - ali * ci + er, alr * ci + ali * cr + ei
            ire_ref[i:i + 1, :] = cr
            iim_ref[i:i + 1, :] = ci

        for j in range(SSM_JB):
            cols = _scan_cols(j)
            ar, ai = _bcast8(are_ref[:, cols]), _bcast8(aim_ref[:, cols])

            def step2(r, s, cols=cols, ar=ar, ai=ai):
                sr, si = s
                rows = _rows8(r)
                nr = ar * sr - ai * si + sre_ref[rows, cols]
                ni = ar * si + ai * sr + sim_ref[rows, cols]
                sre_ref[rows, cols] = nr
                sim_ref[rows, cols] = ni
                return nr, ni

            _scan_loop(L, step2, (ire_ref[:, cols], iim_ref[:, cols]))

        car_re[...] = sre_ref[T - 1:T, :]
        car_im[...] = sim_ref[T - 1:T, :]

        for j in range(SSM_JB):
            cols = _scan_cols(j)
            ch = slice(j * LANES, (j + 1) * LANES)
            y = (jnp.dot(sre_ref[:, cols].astype(_MXU), cre_ref[j], preferred_element_type=F32)
                 - jnp.dot(sim_ref[:, cols].astype(_MXU), cim_ref[j], preferred_element_type=F32))
            y = y + d_ref[:, ch] * u_ref[:, ch].astype(F32)
            y_ref[:, ch] = y
            yg_scan[:, ch] = jax.nn.gelu(y).astype(yg_scan.dtype)
        yg_ref[...] = jnp.dot(_token_order_pick(), yg_scan[...], preferred_element_type=F32).astype(yg_ref.dtype)

    tok = pl.BlockSpec((T, SSM_W), lambda c: (c, 0))
    st = pl.BlockSpec((T, N_STATES), lambda c: (c, 0))
    ini = pl.BlockSpec((None, SUBLANES, N_STATES), lambda c: (c, 0, 0))
    bsp = pl.BlockSpec((SSM_JB, LANES, SSM_SB), lambda c: (0, 0, 0))
    csp = pl.BlockSpec((SSM_JB, SSM_SB, LANES), lambda c: (0, 0, 0))
    row_w = pl.BlockSpec((1, SSM_W), lambda c: (0, 0))
    row_s = pl.BlockSpec((1, N_STATES), lambda c: (0, 0))
    return pl.pallas_call(
        body, name="ssm_fwd", grid=(nc,),
        in_specs=[tok, bsp, bsp, csp, csp, row_w] + [row_s] * 6,
        out_specs=[tok, tok, st, st, ini, ini],
        out_shape=[jax.ShapeDtypeStruct((seq, SSM_W), F32), jax.ShapeDtypeStruct((seq, SSM_W), _MXU),
                   jax.ShapeDtypeStruct((seq, N_STATES), F32), jax.ShapeDtypeStruct((seq, N_STATES), F32),
                   jax.ShapeDtypeStruct((nc, SUBLANES, N_STATES), F32),
                   jax.ShapeDtypeStruct((nc, SUBLANES, N_STATES), F32)],
        scratch_shapes=[pltpu.VMEM((1, N_STATES), F32), pltpu.VMEM((1, N_STATES), F32),
                        pltpu.VMEM((SUBLANES, N_STATES), F32), pltpu.VMEM((SUBLANES, N_STATES), F32),
                        pltpu.VMEM((T, SSM_W), _MXU)],
        compiler_params=_params(("arbitrary",)),
    )(u, b_re, b_im, c_re, c_im, d_skip, *coef)


def _ssm_bwd(dyg, y, u, s_re, s_im, i_re, i_im, b_re, b_im, c_re, c_im, d_skip, coef, d_proj, deps=()):
    seq = u.shape[0]
    nc = seq // SSM_T
    T, L = SSM_T, SSM_L
    deps = list(deps) + [d_proj]

    def body(dyg_ref, y_ref, u_ref, sre_ref, sim_ref, ire_ref, iim_ref, bre_ref, bim_ref, cre_ref, cim_ref, d_ref,
             are_ref, aim_ref, cfr_ref, cfi_ref, alr_ref, ali_ref, *rest):
        (du_ref, dbre_out, dbim_out, dcre_out, dcim_out, dd_ref, dar_ref, dai_ref, dcfr_ref, dcfi_ref,
         lre, lim, car_re, car_im, end_re, end_im, ini_re, ini_im, dbre_ref, dbim_ref, dcre_ref, dcim_ref,
         dy_ref, du_scan) = rest[len(deps):]
        step = pl.program_id(0)
        dy_ref[...] = jax.vjp(jax.nn.gelu, y_ref[...])[1](dyg_ref[...])[0]

        @pl.when(step == 0)
        def _():
            car_re[...] = jnp.zeros_like(car_re)
            car_im[...] = jnp.zeros_like(car_im)
            for ref in (dbre_ref, dbim_ref, dcre_ref, dcim_ref, dd_ref, dar_ref, dai_ref, dcfr_ref, dcfi_ref):
                ref[...] = jnp.zeros_like(ref)

        for j in range(SSM_JB):
            dyb = dy_ref[:, j * LANES:(j + 1) * LANES].astype(_MXU)
            lre[:, _scan_cols(j)] = lax.dot_general(dyb, cre_ref[j], _NT, preferred_element_type=F32)
            lim[:, _scan_cols(j)] = -lax.dot_general(dyb, cim_ref[j], _NT, preferred_element_type=F32)

        for j in range(SSM_JB):
            cols = _scan_cols(j)
            ar, ai = _bcast8(are_ref[:, cols]), _bcast8(aim_ref[:, cols])

            def step1(t, s, cols=cols, ar=ar, ai=ai):
                sr, si = s
                rows = _rows8(L - 1 - t)
                return (ar * sr + ai * si + lre[rows, cols], ar * si - ai * sr + lim[rows, cols])

            zero = jnp.zeros((SUBLANES, SSM_SB), F32)
            er, ei = _scan_loop(L, step1, (zero, zero))
            end_re[:, cols] = er
            end_im[:, cols] = ei

        alr, ali = alr_ref[...], ali_ref[...]
        cr, ci = car_re[...], car_im[...]
        ini_re[SUBLANES - 1:SUBLANES, :] = cr
        ini_im[SUBLANES - 1:SUBLANES, :] = ci
        for i in range(SUBLANES - 2, -1, -1):
            er, ei = end_re[i + 1:i + 2, :], end_im[i + 1:i + 2, :]
            cr, ci = alr * cr + ali * ci + er, alr * ci - ali * cr + ei
            ini_re[i:i + 1, :] = cr
            ini_im[i:i + 1, :] = ci

        for j in range(SSM_JB):
            cols = _scan_cols(j)
            ar, ai = _bcast8(are_ref[:, cols]), _bcast8(aim_ref[:, cols])

            def step2(t, s, cols=cols, ar=ar, ai=ai):
                sr, si = s
                rows = _rows8(L - 1 - t)
                nr = ar * sr + ai * si + lre[rows, cols]
                ni = ar * si - ai * sr + lim[rows, cols]
                lre[rows, cols] = nr
                lim[rows, cols] = ni
                return nr, ni

            _scan_loop(L, step2, (ini_re[:, cols], ini_im[:, cols]))

        car_re[...] = lre[0:1, :]
        car_im[...] = lim[0:1, :]

        head, tail, body_rows = slice(0, SUBLANES), slice(SUBLANES, T), slice(0, T - SUBLANES)
        for j in range(SSM_JB):
            cols = _scan_cols(j)
            ch = slice(j * LANES, (j + 1) * LANES)
            lr, li = lre[:, cols], lim[:, cols]
            lt_r, lt_i, sp_r, sp_i = lre[tail, cols], lim[tail, cols], sre_ref[body_rows, cols], sim_ref[body_rows, cols]
            lh_r, lh_i, si_r, si_i = lre[head, cols], lim[head, cols], ire_ref[:, cols], iim_ref[:, cols]
            dar_ref[:, cols] += _colsum(lt_r * sp_r + lt_i * sp_i) + _colsum(lh_r * si_r + lh_i * si_i)
            dai_ref[:, cols] += _colsum(lt_i * sp_r - lt_r * sp_i) + _colsum(lh_i * si_r - lh_r * si_i)
            ub = u_ref[:, ch].astype(_MXU)
            uf = ub.astype(F32)
            bur = jnp.dot(ub, bre_ref[j], preferred_element_type=F32)
            bui = jnp.dot(ub, bim_ref[j], preferred_element_type=F32)
            dcfr_ref[:, cols] += _colsum(lr * bur + li * bui)
            dcfi_ref[:, cols] += _colsum(li * bur - lr * bui)
            cfr, cfi = cfr_ref[:, cols], cfi_ref[:, cols]
            dbur = (cfr * lr + cfi * li).astype(_MXU)
            dbui = (cfr * li - cfi * lr).astype(_MXU)
            dyf = dy_ref[:, ch]
            dyb = dyf.astype(_MXU)
            du = (lax.dot_general(dbur, bre_ref[j], _NT, preferred_element_type=F32)
                  + lax.dot_general(dbui, bim_ref[j], _NT, preferred_element_type=F32) + d_ref[:, ch] * dyf)
            du_scan[:, ch] = du.astype(du_scan.dtype)
            dbre_ref[j] += lax.dot_general(ub, dbur, _TN, preferred_element_type=F32)
            dbim_ref[j] += lax.dot_general(ub, dbui, _TN, preferred_element_type=F32)
            dcre_ref[j] += lax.dot_general(sre_ref[:, cols].astype(_MXU), dyb, _TN, preferred_element_type=F32)
            dcim_ref[j] -= lax.dot_general(sim_ref[:, cols].astype(_MXU), dyb, _TN, preferred_element_type=F32)
            dd_ref[:, ch] += _colsum(dyf * uf)
        du_ref[...] = jnp.dot(_token_order_pick(), du_scan[...], preferred_element_type=F32).astype(du_ref.dtype)

        @pl.when(step == nc - 1)
        def _():
            for acc, out in ((dbre_ref, dbre_out), (dbim_ref, dbim_out), (dcre_ref, dcre_out), (dcim_ref, dcim_out)):
                pltpu.sync_copy(acc, out)

    tok = pl.BlockSpec((T, SSM_W), lambda c: (nc - 1 - c, 0))
    st = pl.BlockSpec((T, N_STATES), lambda c: (nc - 1 - c, 0))
    ini = pl.BlockSpec((None, SUBLANES, N_STATES), lambda c: (nc - 1 - c, 0, 0))
    bsp = pl.BlockSpec((SSM_JB, LANES, SSM_SB), lambda c: (0, 0, 0))
    csp = pl.BlockSpec((SSM_JB, SSM_SB, LANES), lambda c: (0, 0, 0))
    row_w = pl.BlockSpec((1, SSM_W), lambda c: (0, 0))
    row_s = pl.BlockSpec((1, N_STATES), lambda c: (0, 0))
    big = pltpu.VMEM((T, N_STATES), F32)
    one = pltpu.VMEM((1, N_STATES), F32)
    eight = pltpu.VMEM((SUBLANES, N_STATES), F32)
    return pl.pallas_call(
        body, name="ssm_bwd", grid=(nc,),
        in_specs=[tok, tok, tok, st, st, ini, ini, bsp, bsp, csp, csp, row_w] + [row_s] * 6 + [_ANY] * len(deps),
        out_specs=[pl.BlockSpec((pl.Element(T), pl.Element(SSM_W)), lambda c: ((nc - 1 - c) * T, OFF_U * CW)),
                   _ANY, _ANY, _ANY, _ANY, row_w, row_s, row_s, row_s, row_s],
        input_output_aliases={18 + len(deps) - 1: 0},
        out_shape=[jax.ShapeDtypeStruct(d_proj.shape, d_proj.dtype),
                   jax.ShapeDtypeStruct((SSM_JB, LANES, SSM_SB), F32), jax.ShapeDtypeStruct((SSM_JB, LANES, SSM_SB), F32),
                   jax.ShapeDtypeStruct((SSM_JB, SSM_SB, LANES), F32), jax.ShapeDtypeStruct((SSM_JB, SSM_SB, LANES), F32),
                   jax.ShapeDtypeStruct((1, SSM_W), F32)] + [jax.ShapeDtypeStruct((1, N_STATES), F32)] * 4,
        scratch_shapes=[big, big, one, one, eight, eight, eight, eight,
                        pltpu.VMEM((SSM_JB, LANES, SSM_SB), F32), pltpu.VMEM((SSM_JB, LANES, SSM_SB), F32),
                        pltpu.VMEM((SSM_JB, SSM_SB, LANES), F32), pltpu.VMEM((SSM_JB, SSM_SB, LANES), F32),
                        pltpu.VMEM((T, SSM_W), F32), pltpu.VMEM((T, SSM_W), _MXU)],
        compiler_params=_params(("arbitrary",)),
    )(dyg, y, u, s_re, s_im, i_re, i_im, b_re, b_im, c_re, c_im, d_skip, *coef, *deps)


def _block_diag_b(b):
    t = b.reshape(SSM_JB, 8, STATE, GROUP).transpose(0, 1, 3, 2)
    eye = jnp.eye(8, dtype=b.dtype)
    return (t[:, :, :, None, :] * eye[None, :, None, :, None]).reshape(SSM_JB, LANES, SSM_SB)


def _block_diag_c(c):
    t = c.reshape(SSM_JB, 8, GROUP, STATE).transpose(0, 1, 3, 2)
    eye = jnp.eye(8, dtype=c.dtype)
    return (t[:, :, :, None, :] * eye[None, :, None, :, None]).reshape(SSM_JB, SSM_SB, LANES)


def _diag_of_b(blk):
    t = blk.reshape(SSM_JB, 8, GROUP, 8, STATE)
    d = jnp.sum(t * jnp.eye(8, dtype=blk.dtype)[None, :, None, :, None], axis=3)
    return d.transpose(0, 1, 3, 2).reshape(N_GROUPS, STATE, GROUP)


def _diag_of_c(blk):
    t = blk.reshape(SSM_JB, 8, STATE, 8, GROUP)
    d = jnp.sum(t * jnp.eye(8, dtype=blk.dtype)[None, :, None, :, None], axis=3)
    return d.transpose(0, 1, 3, 2).reshape(N_GROUPS, GROUP, STATE)


def _to_scan_order(v):
    seq, w = v.shape
    return v.reshape(seq // SSM_T, SUBLANES, SSM_L, w).transpose(0, 2, 1, 3).reshape(seq, w)


def _adamw_math(w, g, m, v):
    nm = ADAM_B1 * m + (1.0 - ADAM_B1) * g
    nv = ADAM_B2 * v + (1.0 - ADAM_B2) * jnp.square(g)
    m_hat = nm / (1.0 - ADAM_B1 ** ADAM_STEP)
    v_hat = nv / (1.0 - ADAM_B2 ** ADAM_STEP)
    return -ADAM_LR * (m_hat / (jnp.sqrt(v_hat) + ADAM_EPS) + ADAM_WD * w), nm, nv


def _adamw(w, g, m, v, *, name, tm, deps=()):
    rows, cols = w.shape
    nd = len(deps)

    def body(w_ref, g_ref, m_ref, v_ref, *rest):
        d_ref, nm_ref, nv_ref = rest[nd:]
        d_ref[...], nm_ref[...], nv_ref[...] = _adamw_math(w_ref[...], g_ref[...], m_ref[...], v_ref[...])

    spec = pl.BlockSpec((tm, cols), lambda i: (i, 0))
    shp = jax.ShapeDtypeStruct((rows, cols), F32)
    return pl.pallas_call(body, name=name, grid=(rows // tm,), in_specs=[spec] * 4 + [_ANY] * nd,
                          out_specs=[spec] * 3, out_shape=[shp] * 3,
                          compiler_params=_params(("arbitrary",)))(w, g, m, v, *deps)


def _place():
    x, y, c = lax.axis_index("x"), lax.axis_index("y"), lax.axis_index("c")
    chips = [(1 - x, y), (x, 1 - y), (1 - x, 1 - y)]
    return x, y, c, chips


def _remote(src, dst, send_sem, recv_sem, dev):
    return pltpu.make_async_remote_copy(src_ref=src, dst_ref=dst, send_sem=send_sem, recv_sem=recv_sem,
                                        device_id=dev, device_id_type=MESH)


def _place_shard(w, mine_arr, *, name, tm=256, deps=()):
    rows, cols = w.shape

    def body(m_ref, w_ref, *rest):
        rest[-1][...] = w_ref[...].astype(rest[-1].dtype)

    return pl.pallas_call(
        body, name=name,
        grid_spec=pltpu.PrefetchScalarGridSpec(
            num_scalar_prefetch=1, grid=(rows // tm,),
            in_specs=[pl.BlockSpec((tm, cols), lambda i, m: (i, 0))] + [_ANY] * len(deps),
            out_specs=pl.BlockSpec((None, tm, cols), lambda i, m: (m[0], i, 0))),
        out_shape=jax.ShapeDtypeStruct((N_CHIPS, rows, cols), _WIRE),
        compiler_params=_params(("arbitrary",)),
    )(mine_arr, w, *deps)


_HBM = pl.BlockSpec(memory_space=pltpu.HBM)
_SEM = pl.BlockSpec(memory_space=pltpu.SEMAPHORE)
_EFFECT = pltpu.SideEffectType.DATAFLOW_SIDE_EFFECTING


def _copies_start(name, bufs, plan, count, after=()):
    nb, na = len(bufs), len(after)

    def body(*refs):
        send_sems, recv_sems, token = refs[nb + na], refs[nb + na + 1], refs[-1]
        copies = plan(refs[:nb])
        assert len(copies) == count
        for i, (src, dst, dev, _) in enumerate(copies):
            _remote(src, dst, send_sems.at[i], recv_sems.at[i], dev).start()
        token[...] = jnp.zeros_like(token)

    res = pl.pallas_call(
        body, name=name, in_specs=[_HBM] * nb + [_ANY] * na,
        out_specs=(_SEM, _SEM, *[_HBM] * nb, pl.BlockSpec(memory_space=pltpu.VMEM)),
        out_shape=(pltpu.SemaphoreType.DMA((count,)), pltpu.SemaphoreType.DMA((count,)),
                   *[pltpu.HBM(b.shape, b.dtype) for b in bufs], jax.ShapeDtypeStruct((SUBLANES, LANES), F32)),
        input_output_aliases={i: 2 + i for i in range(nb)},
        compiler_params=pltpu.CompilerParams(has_side_effects=_EFFECT),
    )(*[pltpu.with_memory_space_constraint(b, pltpu.HBM) for b in bufs], *after)
    return (res[0], res[1]), list(res[2:2 + nb]), res[-1]


def _copies_wait(name, bufs, sems, plan, after=(), which=None):
    nb, na = len(bufs), len(after)

    def body(*refs):
        send_sems, recv_sems = refs[nb], refs[nb + 1]
        for i, (src, _, dev, land) in enumerate(plan(refs[:nb])):
            if which is not None and i not in which:
                continue
            cp = _remote(src, land, send_sems.at[i], recv_sems.at[i], dev)
            cp.wait_send()
            cp.wait_recv()

    res = pl.pallas_call(
        body, name=name, in_specs=[_HBM] * nb + [_SEM, _SEM] + [_ANY] * na, out_specs=[_HBM] * nb,
        out_shape=[pltpu.HBM(b.shape, b.dtype) for b in bufs],
        input_output_aliases={i: i for i in range(nb)},
        compiler_params=pltpu.CompilerParams(has_side_effects=_EFFECT),
    )(*bufs, *sems, *after)
    return list(res)


def _plan_gather_ici(fulls, which=(0, 1, 2)):
    x, y, c, chips = _place()
    copies = []
    for f in fulls:
        half = pl.ds(c * (f.shape[1] // 2), f.shape[1] // 2)
        own = f.at[2 * x + y, half]
        for chip in [chips[k] for k in which]:
            copies.append((own, own, (*chip, c), f.at[2 * chip[0] + chip[1], half]))
    return copies


def _plan_gather_d2d(fulls, which=(0, 1, 2)):
    x, y, c, chips = _place()
    copies = []
    for f in fulls:
        r2 = f.shape[1] // 2
        for chip in [chips[k] for k in which]:
            blk = 2 * chip[0] + chip[1]
            landed = f.at[blk, pl.ds(c * r2, r2)]
            copies.append((landed, landed, (x, y, 1 - c), f.at[blk, pl.ds((1 - c) * r2, r2)]))
    return copies


def _plan_relay_direct(fulls):
    (f,) = fulls
    x, y, c, chips = _place()
    half = pl.ds(c * (f.shape[1] // 2), f.shape[1] // 2)
    own = f.at[2 * x + y, half]
    return [(own, own, (*chip, c), f.at[2 * chip[0] + chip[1], half]) for chip in chips[:2]]


def _plan_relay_forward(fulls, k):
    (f,) = fulls
    x, y, c, chips = _place()
    r2 = f.shape[1] // 2
    half, other = pl.ds(c * r2, r2), pl.ds((1 - c) * r2, r2)
    quarter = pl.ds(c * r2 + k * (r2 // 2), r2 // 2)
    blk, far = 2 * chips[k][0] + chips[k][1], 2 * chips[2][0] + chips[2][1]
    passed, landed = f.at[blk, quarter], f.at[blk, half]
    return [(passed, passed, (*chips[1 - k], c), f.at[far, quarter]), (landed, landed, (x, y, 1 - c), f.at[blk, other])]


def _plan_relay_last(fulls):
    (f,) = fulls
    x, y, c, chips = _place()
    r2 = f.shape[1] // 2
    far = 2 * chips[2][0] + chips[2][1]
    landed = f.at[far, pl.ds(c * r2, r2)]
    return [(landed, landed, (x, y, 1 - c), f.at[far, pl.ds((1 - c) * r2, r2)])]


def _plan_swap_halves(refs):
    x, y, c, _ = _place()
    n = len(refs) // 2
    copies = []
    for g, land in zip(refs[:n], refs[n:]):
        r2 = g.shape[1] // 2
        copies.append((g.at[:, pl.ds((1 - c) * r2, r2), :], land, (x, y, 1 - c), land))
    return copies


def _plan_scatter_chips(refs):
    x, y, c, chips = _place()
    n = len(refs) // 2
    copies = []
    for h, land in zip(refs[:n], refs[n:]):
        for k, chip in enumerate(chips):
            copies.append((h.at[2 * chip[0] + chip[1]], land.at[k], (*chip, c), land.at[k]))
    return copies


def _plan_join_halves(totals):
    x, y, c, _ = _place()
    copies = []
    for t in totals:
        r2 = t.shape[0] // 2
        mine = t.at[pl.ds(c * r2, r2)]
        copies.append((mine, mine, (x, y, 1 - c), t.at[pl.ds((1 - c) * r2, r2)]))
    return copies


def _add_sibling_half(g, got, c_arr, *, name, tm):
    _, rows, cols = g.shape
    r2 = rows // 2
    nb = r2 // tm

    def body(c_ref, g_ref, r_ref, o_ref):
        o_ref[...] = (g_ref[...].astype(F32) + r_ref[...].astype(F32)).astype(o_ref.dtype)

    return pl.pallas_call(
        body, name=name,
        grid_spec=pltpu.PrefetchScalarGridSpec(
            num_scalar_prefetch=1, grid=(N_CHIPS, nb),
            in_specs=[pl.BlockSpec((None, tm, cols), lambda b, i, c: (b, c[0] * nb + i, 0)),
                      pl.BlockSpec((None, tm, cols), lambda b, i, c: (b, i, 0))],
            out_specs=pl.BlockSpec((None, tm, cols), lambda b, i, c: (b, i, 0))),
        out_shape=jax.ShapeDtypeStruct((N_CHIPS, r2, cols), _WIRE),
        compiler_params=_params(("arbitrary", "arbitrary")),
    )(c_arr, g, got)


def _add_chips(h, got, place_arr, *, name, tm):
    _, r2, cols = h.shape
    nb = r2 // tm

    def body(p_ref, h_ref, r_ref, o_ref):
        o_ref[...] = ((h_ref[...].astype(F32) + r_ref[0].astype(F32)) + r_ref[1].astype(F32)) + r_ref[2].astype(F32)

    return pl.pallas_call(
        body, name=name,
        grid_spec=pltpu.PrefetchScalarGridSpec(
            num_scalar_prefetch=1, grid=(nb,),
            in_specs=[pl.BlockSpec((None, tm, cols), lambda i, p: (p[0], i, 0)),
                      pl.BlockSpec((3, tm, cols), lambda i, p: (0, i, 0))],
            out_specs=pl.BlockSpec((tm, cols), lambda i, p: (p[1] * nb + i, 0))),
        out_shape=jax.ShapeDtypeStruct((2 * r2, cols), F32),
        compiler_params=_params(("arbitrary",)),
    )(place_arr, h, got)


class _ReduceScatter:
    def __init__(self, tag, names, grads):
        self.tag, self.names, self.n = tag, names, len(names)
        core = lax.axis_index("c").astype(jnp.int32)
        chip = (2 * lax.axis_index("x") + lax.axis_index("y")).astype(jnp.int32)
        self.c_arr, self.place_arr = core.reshape(1), jnp.stack([chip, core])
        self.bufs = list(grads)

    def _start(self, step, bufs, plan, count, after):
        self.plan = plan
        self.step = f"grad_{step}_{self.tag}"
        self.sems, self.bufs, token = _copies_start(self.step + "_start", bufs, plan, count, after)
        return [token]

    def _wait(self, after):
        self.bufs = _copies_wait(self.step + "_wait", self.bufs, self.sems, self.plan, after)
        return self.bufs

    def start_swap(self, after=()):
        lands = [lax.empty((N_CHIPS, g.shape[1] // 2, g.shape[2]), g.dtype) for g in self.bufs]
        return self._start("swap", self.bufs + lands, _plan_swap_halves, self.n, after)

    def start_scatter(self, after):
        bufs = self._wait(after)
        pair = [_add_sibling_half(g, r, self.c_arr, name=f"grad_add_sibling_{nm}", tm=min(256, g.shape[1] // 2))
                for nm, g, r in zip(self.names, bufs[:self.n], bufs[self.n:])]
        lands = [lax.empty((3,) + h.shape[1:], h.dtype) for h in pair]
        return self._start("scatter", pair + lands, _plan_scatter_chips, 3 * self.n, ())

    def start_join(self, after):
        bufs = self._wait(after)
        total = [_add_chips(h, r, self.place_arr, name=f"grad_add_chips_{nm}", tm=min(256, h.shape[1]))
                 for nm, h, r in zip(self.names, bufs[:self.n], bufs[self.n:])]
        return self._start("join", total, _plan_join_halves, self.n, ())

    def finish(self, after):
        return dict(zip(self.names, self._wait(after)))


def _all_gather_small(v):
    m_per, n = v.shape

    def body(x_ref, out_ref, send_sems, recv_sems, local_sem):
        x, y, c, chips = _place()
        me, sibling = (x, y, c), (x, y, 1 - c)

        def rows(px, py, pc):
            return out_ref.at[4 * px + 2 * py + pc]

        def copy(k, block, to, src=None):
            return _remote(rows(*block) if src is None else src, rows(*block), send_sems.at[k], recv_sems.at[k], to)

        mine = pltpu.make_async_copy(x_ref, rows(*me), local_sem)
        mine.start()
        first = [copy(0, me, sibling, src=x_ref)]
        first += [copy(1 + j, me, (*chip, c), src=x_ref) for j, chip in enumerate(chips)]
        for cp in first:
            cp.start()
        passed = [copy(4 + j, (*chip, c), sibling) for j, chip in enumerate(chips)]
        for j, chip in enumerate(chips):
            copy(1 + j, (*chip, c), me).wait_recv()
            passed[j].start()
        copy(0, sibling, me).wait_recv()
        for j, chip in enumerate(chips):
            copy(4 + j, (*chip, 1 - c), me).wait_recv()
        for cp in first + passed:
            cp.wait_send()
        mine.wait()

    return pl.pallas_call(
        body, name="gather_small_grads",
        out_shape=jax.ShapeDtypeStruct((8, m_per, n), v.dtype),
        in_specs=[pl.BlockSpec(memory_space=pltpu.VMEM)], out_specs=pl.BlockSpec(memory_space=pltpu.VMEM),
        scratch_shapes=[pltpu.SemaphoreType.DMA((7,)), pltpu.SemaphoreType.DMA((7,)), pltpu.SemaphoreType.DMA],
        compiler_params=pltpu.CompilerParams(vmem_limit_bytes=VMEM_LIMIT),
    )(v)


def _sum8(v, *, name):
    _, m, n = v.shape

    def body(v_ref, o_ref):
        acc = v_ref[0]
        for d in range(1, 8):
            acc = acc + v_ref[d]
        o_ref[...] = acc

    return pl.pallas_call(body, name=name, out_shape=jax.ShapeDtypeStruct((m, n), F32),
                          compiler_params=pltpu.CompilerParams(vmem_limit_bytes=VMEM_LIMIT))(v)


def _local_step(x, target, norm_w, q_norm_w, k_norm_w, sinks, a_re, a_im, log_dt, b_re, b_im, c_re, c_im, d_skip,
                b_glu, io):
    seq = x.shape[0]
    qw2 = jnp.tile(q_norm_w.reshape(1, HEAD_DIM), (1, HEADS_PER_TILE))
    kw2 = jnp.tile(k_norm_w.reshape(1, HEAD_DIM), (1, HEADS_PER_TILE))
    nw, bg = norm_w.reshape(1, D_MODEL), b_glu.reshape(1, D_MODEL)
    dsk = d_skip.reshape(1, SSM_W)

    h, rstd = _rms_fwd(x, nw, deps=io.begin())
    proj, w_in4 = io.projection(h)
    attn, lse, ya_in = _attn2_fwd(proj, qw2, kw2, sinks, deps=io.after_proj(proj))
    w_ap4 = io.weight("w_attn_proj", ya_in)
    w_glu4, w_sp4, w_out = io.weight("w_glu", ya_in), io.weight("w_ssm_proj", ya_in), io.weight("w_out", ya_in)
    y_a = _mm(ya_in, w_ap4, mode="nn", name="mm_attn_proj", tm=2048, tn=512, tk=ATTN_W, b_blocked=True,
              rows_outer=True, out_dtype=_MXU)

    flat_a = (a_re.reshape(1, N_STATES), a_im.reshape(1, N_STATES), jnp.repeat(log_dt, STATE).reshape(1, N_STATES))
    coef = _ssm_params_fwd(*flat_a)
    bre_blk, bim_blk = _block_diag_b(b_re).astype(_MXU), _block_diag_b(b_im).astype(_MXU)
    cre_blk, cim_blk = _block_diag_c(c_re).astype(_MXU), _block_diag_c(c_im).astype(_MXU)
    u_scan = _to_scan_order(proj[:, OFF_U * CW:OFF_U * CW + SSM_W])
    y_scan, yg, s_re, s_im, i_re, i_im = _ssm_fwd(u_scan, bre_blk, bim_blk, cre_blk, cim_blk, dsk, coef)
    glu, ys_in = _mm_glu_gate(yg, w_glu4, bg, proj)
    y_s = _mm(ys_in, w_sp4, mode="nn", name="mm_ssm_proj", tm=2048, tn=512, tk=SSM_W, b_blocked=True,
              rows_outer=True, out_dtype=_MXU)

    merged, dout, dout_b, sq = _mm_merge_out_loss(proj, y_a, y_s, w_out, x, target)
    loss = 0.5 * jnp.sum(sq) / D_MODEL

    d_ya, d_ys, d_proj = _mm_merge_bwd(dout_b, w_out, proj, y_a, y_s)
    g_w_out = _mm(merged, dout_b, mode="tn", name="mm_g_w_out", tm=1024, tn=D_MODEL, tk=2048, out_dtype=_WIRE)

    d_ya_in = _mm(d_ya, w_ap4, mode="nt", name="mm_d_attn_gate", tm=2048, tn=ATTN_W, tk=512, b_blocked=True)
    g_w_ap = _mm(ya_in, d_ya, mode="tn", name="mm_g_w_attn_proj", tm=ATTN_W, tn=D_MODEL, tk=2048, out_dtype=_WIRE,
                 out_blocked=True)

    g_w_sp = _mm(ys_in, d_ys, mode="tn", name="mm_g_w_ssm_proj", tm=SSM_W, tn=D_MODEL, tk=2048, out_dtype=_WIRE,
                 out_blocked=True)
    d_glu, d_proj, g_bglu = _mm_ssm_gate_bwd(d_ys, w_sp4, glu, bg, proj, d_proj)
    d_yg = _mm(d_glu, w_glu4, mode="nt", name="mm_d_gelu", tm=2048, tn=SSM_W, tk=512, b_blocked=True)
    g_w_glu = _mm(yg, d_glu, mode="tn", name="mm_g_w_glu", tm=SSM_W, tn=D_MODEL, tk=2048, out_dtype=_WIRE, out_blocked=True)
    dep = io.later_grads(dict(w_attn_proj=g_w_ap, w_glu=g_w_glu, w_ssm_proj=g_w_sp,
                              w_out=g_w_out.reshape(N_CHIPS, D_MODEL // N_CHIPS, D_MODEL)))

    d_proj, g_qw2, g_kw2, g_sk = _attn2_bwd(proj, qw2, kw2, sinks, lse, attn, d_ya_in, d_proj, deps=dep)
    dep = io.before_scan_backward([d_proj])
    (d_proj, g_bre, g_bim, g_cre, g_cim, g_dsk, g_abr, g_abi, g_cfr, g_cfi) = _ssm_bwd(
        _to_scan_order(d_yg), y_scan, u_scan, s_re, s_im, i_re, i_im, bre_blk, bim_blk, cre_blk, cim_blk, dsk, coef,
        d_proj, deps=dep)
    g_are, g_aim, g_ldt = _ssm_params_bwd(*flat_a, g_abr, g_abi, g_cfr, g_cfi)
    g_are, g_aim = g_are.reshape(N_GROUPS, STATE), g_aim.reshape(N_GROUPS, STATE)
    g_ldt = g_ldt.reshape(N_GROUPS, STATE).sum(axis=1)
    dep = io.before_input_projection_grad([d_proj]) + io.small_grads(dict(
        q_norm_w=g_qw2[0, :HEAD_DIM] + g_qw2[0, HEAD_DIM:], k_norm_w=g_kw2[0, :HEAD_DIM] + g_kw2[0, HEAD_DIM:],
        sinks=g_sk.reshape(N_Q_HEADS), A_re=g_are, A_im=g_aim, log_dt=g_ldt,
        B_re=_diag_of_b(g_bre), B_im=_diag_of_b(g_bim), C_re=_diag_of_c(g_cre), C_im=_diag_of_c(g_cim),
        D_skip=g_dsk.reshape(N_GROUPS, GROUP), b_glu=g_bglu.reshape(D_MODEL)))
    g_w_in = _mm(h, d_proj, mode="tn", name="mm_g_w_in", tm=1024, tn=IN_W // 4, tk=2048, out_dtype=_WIRE,
                 out_blocked=True, deps=dep)
    dep = io.input_projection_grad(g_w_in)
    d_h = _mm(d_proj, w_in4, mode="nt", name="mm_d_h", tm=1024, tn=D_MODEL, tk=IN_W // 4, b_blocked=True, deps=dep)
    grad_x, g_nw = _rms_bwd(d_h, x, rstd, nw, dout)
    return loss, grad_x, g_nw.reshape(D_MODEL)


_SMALL = ["norm_w", "q_norm_w", "k_norm_w", "sinks", "A_re", "A_im", "log_dt", "B_re", "B_im", "C_re", "C_im",
          "D_skip", "b_glu"]
_BIG = ["w_in", "w_attn_proj", "w_glu", "w_ssm_proj", "w_out"]
_LATER = _BIG[1:]
_RELATIONS = ("flip_x", "flip_y", "flip_xy")
_ORDER = ["norm_w", "w_in", "q_norm_w", "k_norm_w", "sinks", "w_attn_proj", "A_re", "A_im", "log_dt", "B_re", "B_im",
          "C_re", "C_im", "D_skip", "w_glu", "b_glu", "w_ssm_proj", "w_out"]
_PACK_W = 1024


def _packed_rows(size):
    unit = SUBLANES * _PACK_W
    return -(-size // unit) * SUBLANES


def _pack_small(d, names):
    parts = []
    for n in names:
        flat = d[n].reshape(-1).astype(F32)
        rows = _packed_rows(flat.shape[0])
        parts.append(jnp.pad(flat, (0, rows * _PACK_W - flat.shape[0])).reshape(rows, _PACK_W))
    return jnp.concatenate(parts, axis=0)


def _unpack_small(packed, like, names):
    out, pos = {}, 0
    for n in names:
        rows = _packed_rows(like[n].size)
        out[n] = packed[pos:pos + rows].reshape(-1)[:like[n].size].reshape(like[n].shape)
        pos += rows
    return out


def _place_block(v, index_arr, *, name):
    rows, cols = v.shape

    def body(i_ref, v_ref, o_ref):
        o_ref[...] = v_ref[...]

    return pl.pallas_call(
        body, name=name,
        grid_spec=pltpu.PrefetchScalarGridSpec(
            num_scalar_prefetch=1, grid=(1,),
            in_specs=[pl.BlockSpec((rows, cols), lambda i, d: (0, 0))],
            out_specs=pl.BlockSpec((None, rows, cols), lambda i, d: (d[0], 0, 0))),
        out_shape=jax.ShapeDtypeStruct((8, rows, cols), v.dtype),
        compiler_params=_params(("arbitrary",)),
    )(index_arr, v)


def _plan_all_to_all(refs):
    (land,) = refs
    x, y, c, _ = _place()
    own = land.at[4 * x + 2 * y + c]
    copies = []
    for fx, fy, fc in [(0, 0, 1), (0, 1, 0), (0, 1, 1), (1, 0, 0), (1, 0, 1), (1, 1, 0), (1, 1, 1)]:
        px, py, pc = (1 - x) if fx else x, (1 - y) if fy else y, (1 - c) if fc else c
        copies.append((own, own, (px, py, pc), land.at[4 * px + 2 * py + pc]))
    return copies


def _adamw_whole(w, g, m, v, *, name):
    def body(w_ref, g_ref, m_ref, v_ref, d_ref, nm_ref, nv_ref):
        d_ref[...], nm_ref[...], nv_ref[...] = _adamw_math(w_ref[...], g_ref[...], m_ref[...], v_ref[...])

    return pl.pallas_call(body, name=name, out_shape=[jax.ShapeDtypeStruct(w.shape, F32)] * 3)(w, g, m, v)


class _Exchanges:
    def __init__(self, w, m, v):
        self.w, self.m, self.v = w, m, v
        self.grads, self.delta, self.new_m, self.new_v = {}, {}, {}, {}

    def _adamw(self, names, deps):
        for n in names:
            self.delta[n], self.new_m[n], self.new_v[n] = _adamw(
                self.w[n], self.grads[n], self.m[n], self.v[n], name=f"adamw_{n}", tm=256, deps=deps)

    def begin(self):
        chip = (2 * lax.axis_index("x") + lax.axis_index("y")).astype(jnp.int32).reshape(1)
        w_in = _place_shard(self.w["w_in"], chip, name="place_w_in")
        self.w_in_sems, self.w_in_buf, token = _copies_start("gather_w_in_direct_start", [w_in], _plan_relay_direct, 2)
        self.later_full = [_place_shard(self.w[n], chip, name=f"place_{n}", deps=[token]) for n in _LATER]
        return self.later_full

    def projection(self, h):
        x, y = lax.axis_index("x"), lax.axis_index("y")
        blks = [jnp.asarray(b, jnp.int32).reshape(1)
                for b in (2 * x + y, 2 * (1 - x) + y, 2 * x + (1 - y), 2 * (1 - x) + (1 - y))]
        bufs = self.w_in_buf
        proj = _mm_chip_block(h, bufs[0], blks[0], None, name="mm_proj_own", out_dtype=_MXU)
        relay, token = [], proj
        for k, tag in enumerate(_RELATIONS[:2]):
            bufs = _copies_wait(f"gather_w_in_direct_{tag}_wait", bufs, self.w_in_sems, _plan_relay_direct, [token],
                                which=(k,))
            plan = functools.partial(_plan_relay_forward, k=k)
            sems, bufs, token = _copies_start(f"gather_w_in_relay_{tag}_start", bufs, plan, 2)
            relay.append((sems, plan))
        self.rest = _copies_start("gather_ici_rest_start", self.later_full, _plan_gather_ici, 3 * len(_LATER),
                                  after=[token])
        token = self.rest[2]
        for k, tag in enumerate(_RELATIONS[:2]):
            bufs = _copies_wait(f"gather_w_in_hand_{tag}_wait", bufs, relay[k][0], relay[k][1], [token], which=(1,))
            token = proj = _mm_chip_block(h, bufs[0], blks[1 + k], proj, name=f"mm_proj_{tag}", out_dtype=_MXU)
        for k, tag in enumerate(_RELATIONS[:2]):
            bufs = _copies_wait(f"gather_w_in_relay_{tag}_wait", bufs, relay[k][0], relay[k][1], [token], which=(0,))
        sems, bufs, token = _copies_start("gather_w_in_last_start", bufs, _plan_relay_last, 1)
        bufs = _copies_wait("gather_w_in_last_wait", bufs, sems, _plan_relay_last, [token])
        proj = _mm_chip_block(h, bufs[0], blks[3], proj, name="mm_proj_flip_xy", out_dtype=_MXU)
        return proj, bufs[0]

    def weight(self, name, after):
        if self.rest is not None:
            sems, bufs = self.rest
            later = dict(zip(_LATER, _copies_wait("gather_d2d_rest_wait", bufs, sems, _plan_gather_d2d, [after])))
            later["w_out"] = later["w_out"].reshape(D_MODEL, D_MODEL)
            self.later, self.rest = later, None
        return self.later[name]

    def after_proj(self, proj):
        sems, bufs, _ = self.rest
        bufs = _copies_wait("gather_ici_rest_wait", bufs, sems, _plan_gather_ici, [proj])
        sems, bufs, token = _copies_start("gather_d2d_rest_start", bufs, _plan_gather_d2d, 3 * len(_LATER))
        self.rest = (sems, bufs)
        return [token]

    def later_grads(self, grads):
        self.rs_later = _ReduceScatter("later", _LATER, [grads[n] for n in _LATER])
        return self.rs_later.start_swap()

    def before_scan_backward(self, after):
        return self.rs_later.start_scatter(after)

    def before_input_projection_grad(self, after):
        return self.rs_later.start_join(after)

    def input_projection_grad(self, g_w_in):
        self.grads.update(self.rs_later.finish([g_w_in]))
        self.rs_in = _ReduceScatter("w_in", ["w_in"], [g_w_in])
        self._adamw(_LATER, self.rs_in.start_swap())
        return self.rs_in.start_scatter([self.delta[n] for n in _LATER])

    def _adamw_small(self, names):
        for n in names:
            self.delta[n], self.new_m[n], self.new_v[n] = _adamw_whole(
                self.w[n], self.grads[n], self.m[n], self.v[n], name=f"adamw_{n}")

    def small_grads(self, grads):
        me = (4 * lax.axis_index("x") + 2 * lax.axis_index("y") + lax.axis_index("c")).astype(jnp.int32).reshape(1)
        land = _place_block(_pack_small(grads, _SMALL[1:]), me, name="place_small_grads")
        self.small = _copies_start("gather_small_start", [land], _plan_all_to_all, 7)
        return [self.small[2]]

    def finish(self, g_norm_w, loss, after):
        join = self.rs_in.start_join(after)
        sems, bufs, _ = self.small
        (land,) = _copies_wait("gather_small_wait", bufs, sems, _plan_all_to_all, join)
        self.grads.update(_unpack_small(_sum8(land, name="sum_small_grads"), self.w, _SMALL[1:]))
        self._adamw_small(_SMALL[1:])
        rows = _packed_rows(g_norm_w.size)
        late = jnp.concatenate([_pack_small(dict(norm_w=g_norm_w), _SMALL[:1]),
                                jnp.pad(loss.reshape(1, 1), ((0, SUBLANES - 1), (0, _PACK_W - 1)))], axis=0)
        late = _sum8(_all_gather_small(late), name="sum_norm_w_grad_and_loss")
        self.grads.update(_unpack_small(late[:rows], self.w, _SMALL[:1]))
        self._adamw_small(_SMALL[:1])
        self.grads.update(self.rs_in.finish([self.delta[_SMALL[0]]]))
        self._adamw(["w_in"], ())
        return late[rows, 0]


def kernel(x, norm_w, w_in, q_norm_w, k_norm_w, sinks, w_attn_proj, A_re, A_im, log_dt, B_re, B_im, C_re, C_im, D_skip, w_glu, b_glu, w_ssm_proj, w_out, loss_target, m_norm_w, m_w_in, m_q_norm_w, m_k_norm_w, m_sinks, m_w_attn_proj, m_A_re, m_A_im, m_log_dt, m_B_re, m_B_im, m_C_re, m_C_im, m_D_skip, m_w_glu, m_b_glu, m_w_ssm_proj, m_w_out, v_norm_w, v_w_in, v_q_norm_w, v_k_norm_w, v_sinks, v_w_attn_proj, v_A_re, v_A_im, v_log_dt, v_B_re, v_B_im, v_C_re, v_C_im, v_D_skip, v_w_glu, v_b_glu, v_w_ssm_proj, v_w_out):
    w = dict(norm_w=norm_w, w_in=w_in, q_norm_w=q_norm_w, k_norm_w=k_norm_w, sinks=sinks, w_attn_proj=w_attn_proj,
             A_re=A_re, A_im=A_im, log_dt=log_dt, B_re=B_re, B_im=B_im, C_re=C_re, C_im=C_im, D_skip=D_skip,
             w_glu=w_glu, b_glu=b_glu, w_ssm_proj=w_ssm_proj, w_out=w_out)
    m = dict(norm_w=m_norm_w, w_in=m_w_in, q_norm_w=m_q_norm_w, k_norm_w=m_k_norm_w, sinks=m_sinks,
             w_attn_proj=m_w_attn_proj, A_re=m_A_re, A_im=m_A_im, log_dt=m_log_dt, B_re=m_B_re, B_im=m_B_im,
             C_re=m_C_re, C_im=m_C_im, D_skip=m_D_skip, w_glu=m_w_glu, b_glu=m_b_glu, w_ssm_proj=m_w_ssm_proj,
             w_out=m_w_out)
    v = dict(norm_w=v_norm_w, w_in=v_w_in, q_norm_w=v_q_norm_w, k_norm_w=v_k_norm_w, sinks=v_sinks,
             w_attn_proj=v_w_attn_proj, A_re=v_A_re, A_im=v_A_im, log_dt=v_log_dt, B_re=v_B_re, B_im=v_B_im,
             C_re=v_C_re, C_im=v_C_im, D_skip=v_D_skip, w_glu=v_w_glu, b_glu=v_b_glu, w_ssm_proj=v_w_ssm_proj,
             w_out=v_w_out)

    io = _Exchanges(w, m, v)
    loss, grad_x, g_norm_w = _local_step(x[0], loss_target[0], norm_w, q_norm_w, k_norm_w, sinks, A_re, A_im, log_dt,
                                         B_re, B_im, C_re, C_im, D_skip, b_glu, io)
    loss = io.finish(g_norm_w, loss, [grad_x])
    grads, delta, new_m, new_v = io.grads, io.delta, io.new_m, io.new_v

    return (loss, grad_x[None], *[grads[n] for n in _ORDER], *[delta[n] for n in _ORDER],
            *[new_m[n] for n in _ORDER], *[new_v[n] for n in _ORDER])
```

```python
import functools
import math

import jax
import jax.numpy as jnp
from jax import lax
from jax.experimental import pallas as pl
from jax.experimental.pallas import tpu as pltpu

F32 = jnp.float32
_MXU = jnp.bfloat16
_WIRE = jnp.bfloat16

LANES = 128
SUBLANES = 8
VMEM_LIMIT = 56 * 1024 * 1024

D_MODEL = 2048
HEAD_DIM = 64
N_Q_HEADS = 16
N_KV_HEADS = 4
Q_PER_KV = 4
ATTN_W = 1024
KV_W = 256
WINDOW = 128
SSM_W = 1024
GROUP = 16
N_GROUPS = 64
STATE = 64
N_STATES = N_GROUPS * STATE
IN_W = 8704
NORM_EPS = 1e-6
N_CHIPS = 4
CW = 512
OFF_AGATE, OFF_U, OFF_Z, OFF_GA, OFF_GS = 3, 5, 7, 9, 13

SSM_T = 256
SSM_L = SSM_T // SUBLANES
SSM_JB = 8
SSM_SB = N_STATES // SSM_JB

ADAM_LR, ADAM_B1, ADAM_B2, ADAM_EPS, ADAM_WD, ADAM_STEP = 0.001, 0.9, 0.999, 1e-08, 0.01, 10

MESH = pl.DeviceIdType.MESH
_ANY = pl.BlockSpec(memory_space=pl.ANY)


def _params(sem=None):
    return pltpu.CompilerParams(dimension_semantics=sem, vmem_limit_bytes=VMEM_LIMIT)


def _mm(a, b, *, mode, name, tm, tn, tk, out_dtype=F32, b_blocked=False, out_blocked=False, rows_outer=False,
        deps=()):
    nd = len(deps)
    if mode == "tn":
        K, M = a.shape
    else:
        M, K = a.shape
    if mode == "nn":
        N = b.shape[0] * b.shape[2] if b_blocked else b.shape[1]
    elif mode == "nt":
        N = b.shape[1] if b_blocked else b.shape[0]
    else:
        N = b.shape[1]
    tm, tn, tk = min(tm, M), min(tn, N), min(tk, K)
    nj, ni, nk = N // tn, M // tm, K // tk
    assert nj * tn == N and ni * tm == M and nk * tk == K, (name, M, N, K)
    dims = {"nn": (((1,), (0,)), ((), ())), "nt": (((1,), (1,)), ((), ())), "tn": (((0,), (0,)), ((), ()))}[mode]

    if mode == "tn":
        a_spec = pl.BlockSpec((tk, tm), lambda j, i, k: (k, i))
    else:
        a_spec = pl.BlockSpec((tm, tk), lambda j, i, k: (i, k))
    if mode == "nn":
        if b_blocked:
            assert b.shape[0] == nj and b.shape[2] == tn
            b_spec = pl.BlockSpec((None, tk, tn), lambda j, i, k: (j, k, 0))
        else:
            b_spec = pl.BlockSpec((tk, tn), lambda j, i, k: (k, j))
    elif mode == "nt":
        if b_blocked:
            assert b.shape[0] == nk and b.shape[2] == tk
            b_spec = pl.BlockSpec((None, tn, tk), lambda j, i, k: (k, j, 0))
        else:
            b_spec = pl.BlockSpec((tn, tk), lambda j, i, k: (j, k))
    else:
        b_spec = pl.BlockSpec((tk, tn), lambda j, i, k: (k, j))
    whole_out = out_blocked and nj == 1
    if whole_out:
        assert ni == 1
        o_spec = pl.BlockSpec((N_CHIPS, tm, tn // N_CHIPS), lambda j, i, k: (0, 0, 0))
        o_shape = jax.ShapeDtypeStruct((N_CHIPS, M, tn // N_CHIPS), out_dtype)
    elif out_blocked:
        assert nj == N_CHIPS
        o_spec = pl.BlockSpec((None, tm, tn), lambda j, i, k: (j, i, 0))
        o_shape = jax.ShapeDtypeStruct((nj, M, tn), out_dtype)
    else:
        o_spec = pl.BlockSpec((tm, tn), lambda j, i, k: (i, j))
        o_shape = jax.ShapeDtypeStruct((M, N), out_dtype)
    use_acc = nk > 1 and (out_dtype != F32 or whole_out)

    def body(a_ref, b_ref, *rest):
        o_ref, scratch = rest[nd], rest[nd + 1:]

        def product():
            return lax.dot_general(a_ref[...].astype(_MXU), b_ref[...].astype(_MXU), dims, preferred_element_type=F32)

        def write(result):
            if whole_out:
                w = tn // N_CHIPS
                for c in range(N_CHIPS):
                    o_ref[c] = result[:, c * w:(c + 1) * w].astype(o_ref.dtype)
            else:
                o_ref[...] = result.astype(o_ref.dtype)

        if nk == 1:
            write(product())
            return
        k = pl.program_id(2)
        acc = scratch[0] if use_acc else o_ref

        @pl.when(k == 0)
        def _():
            acc[...] = jnp.zeros_like(acc)

        acc[...] += product()

        if use_acc:
            @pl.when(k == nk - 1)
            def _():
                write(acc[...])

    specs = [a_spec, b_spec, o_spec]
    grid = (nj, ni, nk)
    if rows_outer:
        specs = [pl.BlockSpec(s.block_shape, lambda i, j, k, f=s.index_map: f(j, i, k)) for s in specs]
        grid = (ni, nj, nk)
    return pl.pallas_call(
        body, name=name, grid=grid, in_specs=specs[:2] + [_ANY] * nd, out_specs=specs[2],
        out_shape=o_shape, scratch_shapes=[pltpu.VMEM((tm, tn), F32)] if use_acc else [],
        compiler_params=_params(("parallel", "parallel", "arbitrary")),
    )(a, b, *deps)


def _mm_chip_block(a, b4, blk, prev, *, name, tm=1024, out_dtype=F32, deps=()):
    M, K = a.shape
    nchip, _, C = b4.shape
    tm = min(tm, M)
    extra = ([] if prev is None else [prev]) + list(deps)

    def body(blk_ref, a_ref, b_ref, *rest):
        rest[-1][...] = jnp.dot(a_ref[...].astype(_MXU), b_ref[...].astype(_MXU),
                                preferred_element_type=F32).astype(rest[-1].dtype)

    return pl.pallas_call(
        body, name=name,
        grid_spec=pltpu.PrefetchScalarGridSpec(
            num_scalar_prefetch=1, grid=(M // tm,),
            in_specs=[pl.BlockSpec((tm, K), lambda i, c: (i, 0)), pl.BlockSpec((None, K, C), lambda i, c: (c[0], 0, 0))]
            + [_ANY] * len(extra),
            out_specs=pl.BlockSpec((tm, C), lambda i, c: (i, c[0]))),
        out_shape=jax.ShapeDtypeStruct((M, nchip * C), out_dtype),
        input_output_aliases={} if prev is None else {3: 0},
        compiler_params=_params(("arbitrary",)),
    )(blk, a, b4, *extra)


def _mm_merge_out_loss(proj, y_a, y_s, w_out, x, target, *, tm=256):
    rows, d = x.shape
    ncol = d // CW

    def body(*refs):
        ga_refs, gs_refs = refs[:ncol], refs[ncol:2 * ncol]
        ya_ref, ys_ref, w_ref, x_ref, t_ref, m_ref, d_ref, db_ref, sq_ref = refs[2 * ncol:]
        for j in range(ncol):
            cols = slice(j * CW, (j + 1) * CW)
            m_ref[:, cols] = (_sigmoid(ga_refs[j][...].astype(F32)) * ya_ref[:, cols].astype(F32)
                              + _sigmoid(gs_refs[j][...].astype(F32)) * ys_ref[:, cols].astype(F32)).astype(m_ref.dtype)
        mo = jnp.dot(m_ref[...], w_ref[...].astype(_MXU), preferred_element_type=F32)
        err = (x_ref[...] + mo) - t_ref[...]
        dout = err * (1.0 / d)
        d_ref[...] = dout
        db_ref[...] = dout.astype(db_ref.dtype)
        part = _colsum(err * err)
        i = pl.program_id(0)

        @pl.when(i == 0)
        def _():
            sq_ref[...] = part

        @pl.when(i > 0)
        def _():
            sq_ref[...] += part

    tile = pl.BlockSpec((tm, d), lambda i: (i, 0))
    gate = [pl.BlockSpec((tm, CW), lambda i, c=off + j: (i, c)) for off in (OFF_GA, OFF_GS) for j in range(ncol)]
    return pl.pallas_call(
        body, name="mm_merge_out_loss", grid=(rows // tm,),
        in_specs=gate + [tile, tile, pl.BlockSpec((d, d), lambda i: (0, 0), pipeline_mode=pl.Buffered(1)), tile, tile],
        out_specs=[tile, tile, tile, pl.BlockSpec((1, d), lambda i: (0, 0))],
        out_shape=[jax.ShapeDtypeStruct((rows, d), _MXU), jax.ShapeDtypeStruct((rows, d), F32),
                   jax.ShapeDtypeStruct((rows, d), _MXU), jax.ShapeDtypeStruct((1, d), F32)],
        compiler_params=_params(("arbitrary",)),
    )(*([proj] * (2 * ncol)), y_a, y_s, w_out, x, target)


def _mm_merge_bwd(dout_b, w_out, proj, y_a, y_s, *, tm=512):
    rows, d = y_a.shape
    ncol = d // CW

    def body(do_ref, w_ref, *refs):
        ga_refs, gs_refs = refs[:ncol], refs[ncol:2 * ncol]
        ya_ref, ys_ref, dya_ref, dys_ref, dg_ref = refs[2 * ncol:]
        dm = lax.dot_general(do_ref[...].astype(_MXU), w_ref[...].astype(_MXU), _NT, preferred_element_type=F32)
        for j in range(ncol):
            cols = slice(j * CW, (j + 1) * CW)
            dmj = dm[:, cols]
            sa, ss = _sigmoid(ga_refs[j][...].astype(F32)), _sigmoid(gs_refs[j][...].astype(F32))
            dya_ref[:, cols] = (sa * dmj).astype(dya_ref.dtype)
            dys_ref[:, cols] = (ss * dmj).astype(dys_ref.dtype)
            dg_ref[:, cols] = (dmj * ya_ref[:, cols].astype(F32) * sa * (1.0 - sa)).astype(dg_ref.dtype)
            dg_ref[:, d + j * CW:d + (j + 1) * CW] = (dmj * ys_ref[:, cols].astype(F32) * ss
                                                      * (1.0 - ss)).astype(dg_ref.dtype)

    tile = pl.BlockSpec((tm, d), lambda i: (i, 0))
    gate = [pl.BlockSpec((tm, CW), lambda i, c=off + j: (i, c)) for off in (OFF_GA, OFF_GS) for j in range(ncol)]
    both = pl.BlockSpec((pl.Element(tm), pl.Element(2 * d)), lambda i: (i * tm, OFF_GA * CW))
    return pl.pallas_call(
        body, name="mm_merge_bwd", grid=(rows // tm,),
        in_specs=[tile, pl.BlockSpec((d, d), lambda i: (0, 0), pipeline_mode=pl.Buffered(1))] + gate + [tile, tile],
        out_specs=[tile, tile, both],
        out_shape=[jax.ShapeDtypeStruct((rows, d), _MXU)] * 2 + [jax.ShapeDtypeStruct((rows, IN_W), _MXU)],
        compiler_params=_params(("arbitrary",)),
    )(dout_b, w_out, *([proj] * (2 * ncol)), y_a, y_s)


def _mm_glu_gate(yg, w_glu4, b_glu, proj, *, tm=1024):
    rows, k = yg.shape
    nj, _, tn = w_glu4.shape
    w = nj * tn // 2
    tm = min(tm, rows)

    def body(a_ref, w_ref, ba_ref, bb_ref, z0_ref, z1_ref, glu_ref, ys_ref):
        j = pl.program_id(1)
        for c in range(nj):
            @pl.when(j == c)
            def _(c=c):
                glu_ref[:, c * tn:(c + 1) * tn] = jnp.dot(a_ref[...].astype(_MXU), w_ref[...].astype(_MXU),
                                                          preferred_element_type=F32).astype(glu_ref.dtype)

        @pl.when(j == nj - 1)
        def _():
            z = jnp.concatenate([z0_ref[...], z1_ref[...]], axis=1).astype(F32)
            ys_ref[...] = ((glu_ref[:, :w].astype(F32) + ba_ref[...]) * _sigmoid(glu_ref[:, w:].astype(F32) + bb_ref[...])
                           * (z * _sigmoid(z))).astype(ys_ref.dtype)

    bias = lambda c: pl.BlockSpec((1, w), lambda i, j: (0, c))
    zcol = lambda c: pl.BlockSpec((tm, CW), lambda i, j: (i, OFF_Z + c))
    return pl.pallas_call(
        body, name="mm_glu_gate", grid=(rows // tm, nj),
        in_specs=[pl.BlockSpec((tm, k), lambda i, j: (i, 0)), pl.BlockSpec((None, k, tn), lambda i, j: (j, 0, 0)),
                  bias(0), bias(1), zcol(0), zcol(1)],
        out_specs=[pl.BlockSpec((tm, nj * tn), lambda i, j: (i, 0)), pl.BlockSpec((tm, w), lambda i, j: (i, 0))],
        out_shape=[jax.ShapeDtypeStruct((rows, nj * tn), _MXU), jax.ShapeDtypeStruct((rows, w), _MXU)],
        compiler_params=_params(("arbitrary", "arbitrary")),
    )(yg, w_glu4, b_glu, b_glu, proj, proj)


def _mm_ssm_gate_bwd(d_ys, w_sp4, glu, b_glu, proj, d_proj, *, tm=1024):
    rows, w = glu.shape[0], glu.shape[1] // 2
    nk, tk = w_sp4.shape[0], w_sp4.shape[2]
    tm = min(tm, rows)

    def body(dy_ref, w_ref, ga_ref, gb_ref, ba_ref, bb_ref, z0_ref, z1_ref, buf_ref, dg_ref, dz_ref, db_ref, acc):
        i, k = pl.program_id(0), pl.program_id(1)

        @pl.when(k == 0)
        def _():
            acc[...] = jnp.zeros_like(acc)

        acc[...] += lax.dot_general(dy_ref[...].astype(_MXU), w_ref[...].astype(_MXU), _NT, preferred_element_type=F32)

        @pl.when(k == nk - 1)
        def _():
            dv = acc[...]
            a, sb = ga_ref[...].astype(F32) + ba_ref[...], _sigmoid(gb_ref[...].astype(F32) + bb_ref[...])
            f, df = _silu_and_grad(jnp.concatenate([z0_ref[...], z1_ref[...]], axis=1).astype(F32))
            dga = dv * sb * f
            dgb = dv * a * f * sb * (1.0 - sb)
            dg_ref[:, :w] = dga.astype(dg_ref.dtype)
            dg_ref[:, w:] = dgb.astype(dg_ref.dtype)
            dz_ref[...] = (dv * a * sb * df).astype(dz_ref.dtype)
            part = jnp.concatenate([_colsum(dga), _colsum(dgb)], axis=1)

            @pl.when(i == 0)
            def _():
                db_ref[...] = part

            @pl.when(i > 0)
            def _():
                db_ref[...] += part

    half = lambda c: pl.BlockSpec((tm, w), lambda i, k: (i, c))
    bias = lambda c: pl.BlockSpec((1, w), lambda i, k: (0, c))
    zcol = lambda c: pl.BlockSpec((tm, CW), lambda i, k: (i, OFF_Z + c))
    return pl.pallas_call(
        body, name="mm_ssm_gate_bwd", grid=(rows // tm, nk),
        in_specs=[pl.BlockSpec((tm, tk), lambda i, k: (i, k)), pl.BlockSpec((None, w, tk), lambda i, k: (k, 0, 0)),
                  half(0), half(1), bias(0), bias(1), zcol(0), zcol(1), _ANY],
        out_specs=[pl.BlockSpec((tm, 2 * w), lambda i, k: (i, 0)),
                   pl.BlockSpec((pl.Element(tm), pl.Element(w)), lambda i, k: (i * tm, OFF_Z * CW)),
                   pl.BlockSpec((1, 2 * w), lambda i, k: (0, 0))],
        out_shape=[jax.ShapeDtypeStruct((rows, 2 * w), _MXU), jax.ShapeDtypeStruct(d_proj.shape, d_proj.dtype),
                   jax.ShapeDtypeStruct((1, 2 * w), F32)],
        input_output_aliases={8: 1},
        scratch_shapes=[pltpu.VMEM((tm, w), F32)],
        compiler_params=_params(("arbitrary", "arbitrary")),
    )(d_ys, w_sp4, glu, glu, b_glu, b_glu, proj, proj, d_proj)


def _colsum(v):
    return jnp.sum(v, axis=0, keepdims=True)


def _sigmoid(v):
    return jax.nn.sigmoid(v)


def _silu_and_grad(v):
    s = _sigmoid(v)
    return v * s, s * (1.0 + v * (1.0 - s))


def _rms_fwd(x, w, *, tm=512, deps=()):
    rows, d = x.shape
    nd = len(deps)

    def body(x_ref, w_ref, *rest):
        h_ref, r_ref = rest[nd:]
        xv = x_ref[...]
        r = lax.rsqrt(jnp.mean(xv * xv, axis=-1, keepdims=True) + NORM_EPS)
        h_ref[...] = (xv * r * w_ref[...]).astype(h_ref.dtype)
        r_ref[...] = r

    return pl.pallas_call(
        body, name="rms_fwd", grid=(rows // tm,),
        in_specs=[pl.BlockSpec((tm, d), lambda i: (i, 0)), pl.BlockSpec((1, d), lambda i: (0, 0))] + [_ANY] * nd,
        out_specs=[pl.BlockSpec((tm, d), lambda i: (i, 0)), pl.BlockSpec((tm, 1), lambda i: (i, 0))],
        out_shape=[jax.ShapeDtypeStruct((rows, d), _MXU), jax.ShapeDtypeStruct((rows, 1), F32)],
        compiler_params=_params(("arbitrary",)),
    )(x, w, *deps)


def _rms_bwd(dh, x, rstd, w, dout, *, tm=512):
    rows, d = x.shape

    def body(dh_ref, x_ref, r_ref, w_ref, do_ref, gx_ref, gw_ref):
        dhv, xv, r, wv = dh_ref[...], x_ref[...], r_ref[...], w_ref[...]
        xr = xv * r
        t = jnp.mean(dhv * wv * xr, axis=-1, keepdims=True)
        gx_ref[...] = do_ref[...] + r * (wv * dhv - xr * t)
        part = _colsum(dhv * xr)
        i = pl.program_id(0)

        @pl.when(i == 0)
        def _():
            gw_ref[...] = part

        @pl.when(i > 0)
        def _():
            gw_ref[...] += part

    return pl.pallas_call(
        body, name="rms_bwd", grid=(rows // tm,),
        in_specs=[pl.BlockSpec((tm, d), lambda i: (i, 0)), pl.BlockSpec((tm, d), lambda i: (i, 0)),
                  pl.BlockSpec((tm, 1), lambda i: (i, 0)), pl.BlockSpec((1, d), lambda i: (0, 0)),
                  pl.BlockSpec((tm, d), lambda i: (i, 0))],
        out_specs=[pl.BlockSpec((tm, d), lambda i: (i, 0)), pl.BlockSpec((1, d), lambda i: (0, 0))],
        out_shape=[jax.ShapeDtypeStruct((rows, d), F32), jax.ShapeDtypeStruct((1, d), F32)],
        compiler_params=_params(("arbitrary",)),
    )(dh, x, rstd, w, dout)


_NT = (((1,), (1,)), ((), ()))
_TN = (((0,), (0,)), ((), ()))


QKV_W = ATTN_W + 2 * KV_W
HEADS_PER_TILE = LANES // HEAD_DIM


def _low_half(rows):
    return lax.broadcasted_iota(jnp.int32, (rows, LANES), 1) < HEAD_DIM


def _pair_mean(t, low):
    m_lo = jnp.sum(jnp.where(low, t, 0.0), axis=-1, keepdims=True)
    m_hi = jnp.sum(jnp.where(low, 0.0, t), axis=-1, keepdims=True)
    return jnp.where(low, m_lo, m_hi) * (1.0 / HEAD_DIM)


def _pair_rstd(t, low):
    return lax.rsqrt(_pair_mean(t * t, low) + NORM_EPS)


def _dup_half(t, hi, low):
    swapped = pltpu.roll(t, HEAD_DIM, 1)
    return jnp.where(low, swapped, t) if hi else jnp.where(low, t, swapped)


def _fold_halves(t):
    return t + pltpu.roll(t, HEAD_DIM, 1)


def _split_heads(t, low):
    return [jnp.where(low, t, 0.0), jnp.where(low, 0.0, t)]


def _stacked_band_mask(n):
    rows = Q_PER_KV * WINDOW
    qi = lax.broadcasted_iota(jnp.int32, (rows, 2 * WINDOW), 0) % WINDOW + WINDOW
    kj = lax.broadcasted_iota(jnp.int32, (rows, 2 * WINDOW), 1)
    diff = qi - kj
    first_key = jnp.where(n > 0, 0, WINDOW)
    return (diff >= 0) & (diff < WINDOW) & (kj >= first_key)


def _stacked_sinks(sink_ref, g):
    blk = lax.broadcasted_iota(jnp.int32, (Q_PER_KV * WINDOW, 1), 0) // WINDOW
    col = jnp.full((Q_PER_KV * WINDOW, 1), sink_ref[Q_PER_KV * g], F32)
    for r in range(1, Q_PER_KV):
        col = jnp.where(blk == r, sink_ref[Q_PER_KV * g + r], col)
    return col


def _attn_in_specs(nblk, rev):
    def cur(n):
        return (nblk - 1 - n) if rev else n

    q_spec = pl.BlockSpec((WINDOW, ATTN_W), lambda n: (cur(n), 0))
    kvc_spec = pl.BlockSpec((WINDOW, 2 * KV_W), lambda n: (cur(n), ATTN_W // (2 * KV_W)))
    kvp_spec = pl.BlockSpec((WINDOW, 2 * KV_W), lambda n: (jnp.maximum(cur(n) - 1, 0), ATTN_W // (2 * KV_W)))
    w_spec = pl.BlockSpec((1, LANES), lambda n: (0, 0))
    l_spec = pl.BlockSpec((WINDOW, N_Q_HEADS), lambda n: (cur(n), 0))
    gate_specs = [pl.BlockSpec((WINDOW, CW), lambda n, col=OFF_AGATE + j: (cur(n), col)) for j in range(ATTN_W // CW)]
    return q_spec, kvc_spec, kvp_spec, w_spec, l_spec, gate_specs


def _attn2_fwd(proj, qw2, kw2, sinks, deps=()):
    seq = proj.shape[0]
    nblk = seq // WINDOW
    scale = 1.0 / math.sqrt(HEAD_DIM)
    q_spec, kvc_spec, kvp_spec, w_spec, l_spec, gate_specs = _attn_in_specs(nblk, False)
    nd, ng = len(deps), len(gate_specs)

    def body(sink_ref, q_ref, kvc_ref, kvp_ref, qw_ref, kw_ref, *rest):
        gate_refs = rest[:ng]
        o_ref, lse_ref, ya_ref = rest[ng + nd:]
        n = pl.program_id(0)
        low, low2 = _low_half(WINDOW), _low_half(2 * WINDOW)
        valid = _stacked_band_mask(n)
        head_lane = lax.broadcasted_iota(jnp.int32, (WINDOW, N_Q_HEADS), 1)
        kv = jnp.concatenate([kvp_ref[...], kvc_ref[...]], axis=0).astype(F32)
        qwv, kwv = qw_ref[...], kw_ref[...]
        lse_blk = jnp.zeros((WINDOW, N_Q_HEADS), F32)
        for t in range(N_KV_HEADS // HEADS_PER_TILE):
            kt = kv[:, t * LANES:(t + 1) * LANES]
            vt = kv[:, KV_W + t * LANES:KV_W + (t + 1) * LANES]
            kn = kt * _pair_rstd(kt, low2) * kwv
            for hi in range(HEADS_PER_TILE):
                g = HEADS_PER_TILE * t + hi
                kdup = _dup_half(kn, hi, low2).astype(_MXU)
                vdup = _dup_half(vt, hi, low2).astype(_MXU)
                stack = []
                for tq in (2 * g, 2 * g + 1):
                    qt = q_ref[:, tq * LANES:(tq + 1) * LANES].astype(F32)
                    stack += _split_heads(qt * _pair_rstd(qt, low) * qwv, low)
                qs = jnp.concatenate(stack, axis=0).astype(_MXU)
                s = lax.dot_general(qs, kdup, _NT, preferred_element_type=F32) * scale
                s = jnp.where(valid, s, -1e30)
                sink = _stacked_sinks(sink_ref, g)
                m = jnp.maximum(jnp.max(s, axis=-1, keepdims=True), sink)
                e = jnp.exp(s - m)
                z = jnp.sum(e, axis=-1, keepdims=True) + jnp.exp(sink - m)
                o = jnp.dot((e / z).astype(_MXU), vdup, preferred_element_type=F32)
                for i, tq in enumerate((2 * g, 2 * g + 1)):
                    tile = slice(tq * LANES, (tq + 1) * LANES)
                    out = jnp.where(low, o[2 * i * WINDOW:(2 * i + 1) * WINDOW],
                                    o[(2 * i + 1) * WINDOW:(2 * i + 2) * WINDOW])
                    gate = gate_refs[tq * LANES // CW][:, tq * LANES % CW:tq * LANES % CW + LANES].astype(F32)
                    o_ref[:, tile] = out.astype(o_ref.dtype)
                    ya_ref[:, tile] = (out * (gate * _sigmoid(gate))).astype(ya_ref.dtype)
                lse = m + jnp.log(z)
                for r in range(Q_PER_KV):
                    lse_blk = jnp.where(head_lane == Q_PER_KV * g + r, lse[r * WINDOW:(r + 1) * WINDOW], lse_blk)
        lse_ref[...] = lse_blk

    return pl.pallas_call(
        body, name="attn_fwd", grid=(nblk,),
        in_specs=[pl.BlockSpec(memory_space=pltpu.SMEM), q_spec, kvc_spec, kvp_spec, w_spec, w_spec] + gate_specs
        + [_ANY] * nd,
        out_specs=[q_spec, l_spec, q_spec],
        out_shape=[jax.ShapeDtypeStruct((seq, ATTN_W), _MXU), jax.ShapeDtypeStruct((seq, N_Q_HEADS), F32),
                   jax.ShapeDtypeStruct((seq, ATTN_W), _MXU)],
        compiler_params=_params(("arbitrary",)),
    )(sinks, proj, proj, proj, qw2, kw2, *([proj] * ng), *deps)


def _attn2_bwd(proj, qw2, kw2, sinks, lse, attn, dya, d_proj, deps=()):
    seq = proj.shape[0]
    nblk = seq // WINDOW
    scale = 1.0 / math.sqrt(HEAD_DIM)
    q_spec, kvc_spec, kvp_spec, w_spec, l_spec, gate_specs = _attn_in_specs(nblk, True)
    s_spec = pl.BlockSpec((1, N_Q_HEADS), lambda n: (0, 0))
    d_spec = pl.BlockSpec((WINDOW, QKV_W + ATTN_W), lambda n: (nblk - 1 - n, 0))
    deps = list(deps) + [d_proj]
    nd, ng = len(deps), len(gate_specs)

    def body(sink_ref, q_ref, kvc_ref, kvp_ref, qw_ref, kw_ref, lse_ref, attn_ref, dya_ref, *rest):
        gate_refs = rest[:ng]
        d_ref, dqw_ref, dkw_ref, dsk_ref, carry, do_ref = rest[ng + nd:]
        step = pl.program_id(0)
        n = nblk - 1 - step

        @pl.when(step == 0)
        def _():
            carry[...] = jnp.zeros_like(carry)
            dqw_ref[...] = jnp.zeros_like(dqw_ref)
            dkw_ref[...] = jnp.zeros_like(dkw_ref)
            dsk_ref[...] = jnp.zeros_like(dsk_ref)

        for j, g_ref in enumerate(gate_refs):
            cols = slice(j * CW, (j + 1) * CW)
            f, df = _silu_and_grad(g_ref[...].astype(F32))
            dv = dya_ref[:, cols]
            do_ref[:, cols] = dv * f
            d_ref[:, QKV_W + j * CW:QKV_W + (j + 1) * CW] = (dv * attn_ref[:, cols].astype(F32) * df).astype(d_ref.dtype)

        low, low2 = _low_half(WINDOW), _low_half(2 * WINDOW)
        valid = _stacked_band_mask(n)
        head_lane = lax.broadcasted_iota(jnp.int32, (WINDOW, N_Q_HEADS), 1)
        sink_lane = lax.broadcasted_iota(jnp.int32, (1, N_Q_HEADS), 1)
        kv = jnp.concatenate([kvp_ref[...], kvc_ref[...]], axis=0).astype(F32)
        qwv, kwv = qw_ref[...], kw_ref[...]
        lse_blk = lse_ref[...]
        dqw = jnp.zeros((1, LANES), F32)
        dkw = jnp.zeros((1, LANES), F32)
        dsk = jnp.zeros((1, N_Q_HEADS), F32)
        for t in range(N_KV_HEADS // HEADS_PER_TILE):
            kt = kv[:, t * LANES:(t + 1) * LANES]
            vt = kv[:, KV_W + t * LANES:KV_W + (t + 1) * LANES]
            rk = _pair_rstd(kt, low2)
            kn = kt * rk * kwv
            dkn_t = jnp.zeros((2 * WINDOW, LANES), F32)
            dv_t = jnp.zeros((2 * WINDOW, LANES), F32)
            for hi in range(HEADS_PER_TILE):
                g = HEADS_PER_TILE * t + hi
                kdup = _dup_half(kn, hi, low2).astype(_MXU)
                vdup = _dup_half(vt, hi, low2).astype(_MXU)
                tiles = (2 * g, 2 * g + 1)
                qx, rq, stack, dstack, lse_rows = [], [], [], [], []
                for tq in tiles:
                    qt = q_ref[:, tq * LANES:(tq + 1) * LANES].astype(F32)
                    r = _pair_rstd(qt, low)
                    rq.append(r)
                    qx.append(qt * r)
                    stack += _split_heads(qx[-1] * qwv, low)
                    dstack += _split_heads(do_ref[:, tq * LANES:(tq + 1) * LANES], low)
                for r in range(Q_PER_KV):
                    lse_rows.append(jnp.sum(jnp.where(head_lane == Q_PER_KV * g + r, lse_blk, 0.0), axis=-1, keepdims=True))
                qs = jnp.concatenate(stack, axis=0).astype(_MXU)
                dos = jnp.concatenate(dstack, axis=0).astype(_MXU)
                lse_col = jnp.concatenate(lse_rows, axis=0)
                s = lax.dot_general(qs, kdup, _NT, preferred_element_type=F32) * scale
                s = jnp.where(valid, s, -1e30)
                p = jnp.exp(s - lse_col)
                dp = lax.dot_general(dos, vdup, _NT, preferred_element_type=F32)
                dsum = jnp.sum(p * dp, axis=-1, keepdims=True)
                ds = (p * (dp - dsum) * scale).astype(_MXU)
                dsink = -jnp.exp(_stacked_sinks(sink_ref, g) - lse_col) * dsum
                for r in range(Q_PER_KV):
                    dsk = dsk + jnp.where(sink_lane == Q_PER_KV * g + r, _colsum(dsink[r * WINDOW:(r + 1) * WINDOW]), 0.0)
                dv_g = _fold_halves(lax.dot_general(p.astype(_MXU), dos, _TN, preferred_element_type=F32))
                dkn_g = _fold_halves(lax.dot_general(ds, qs, _TN, preferred_element_type=F32))
                dv_t = jnp.where(low2, dv_t, dv_g) if hi else jnp.where(low2, dv_g, dv_t)
                dkn_t = jnp.where(low2, dkn_t, dkn_g) if hi else jnp.where(low2, dkn_g, dkn_t)
                dqn = jnp.dot(ds, kdup, preferred_element_type=F32)
                for i, tq in enumerate(tiles):
                    dqn_t = jnp.where(low, dqn[2 * i * WINDOW:(2 * i + 1) * WINDOW],
                                      dqn[(2 * i + 1) * WINDOW:(2 * i + 2) * WINDOW])
                    dq = rq[i] * (qwv * dqn_t - qx[i] * _pair_mean(dqn_t * qwv * qx[i], low))
                    d_ref[:, tq * LANES:(tq + 1) * LANES] = dq.astype(d_ref.dtype)
                    dqw = dqw + _colsum(dqn_t * qx[i])
            k_cols = slice(t * LANES, (t + 1) * LANES)
            v_cols = slice(KV_W + t * LANES, KV_W + (t + 1) * LANES)
            dkn_c = dkn_t[WINDOW:] + carry[:, k_cols]
            rc = rk[WINDOW:]
            kx = kt[WINDOW:] * rc
            dk = rc * (kwv * dkn_c - kx * _pair_mean(dkn_c * kwv * kx, low))
            d_ref[:, ATTN_W + t * LANES:ATTN_W + (t + 1) * LANES] = dk.astype(d_ref.dtype)
            d_ref[:, ATTN_W + KV_W + t * LANES:ATTN_W + KV_W + (t + 1) * LANES] = (
                dv_t[WINDOW:] + carry[:, v_cols]).astype(d_ref.dtype)
            carry[:, k_cols] = dkn_t[:WINDOW]
            carry[:, v_cols] = dv_t[:WINDOW]
            dkw = dkw + _colsum(dkn_c * kx)
        dqw_ref[...] += dqw
        dkw_ref[...] += dkw
        dsk_ref[...] += dsk

    return pl.pallas_call(
        body, name="attn_bwd", grid=(nblk,),
        in_specs=[pl.BlockSpec(memory_space=pltpu.SMEM), q_spec, kvc_spec, kvp_spec, w_spec, w_spec, l_spec, q_spec,
                  q_spec] + gate_specs + [_ANY] * nd,
        out_specs=[d_spec, w_spec, w_spec, s_spec],
        out_shape=[jax.ShapeDtypeStruct(d_proj.shape, d_proj.dtype), jax.ShapeDtypeStruct((1, LANES), F32),
                   jax.ShapeDtypeStruct((1, LANES), F32), jax.ShapeDtypeStruct((1, N_Q_HEADS), F32)],
        input_output_aliases={9 + ng + nd - 1: 0},
        scratch_shapes=[pltpu.VMEM((WINDOW, 2 * KV_W), F32), pltpu.VMEM((WINDOW, ATTN_W), F32)],
        compiler_params=_params(("arbitrary",)),
    )(sinks, proj, proj, proj, qw2, kw2, lse, attn, dya, *([proj] * ng), *deps)


def _ssm_discretise(a_re, a_im, log_dt):
    dt = jnp.exp(log_dt)
    mag = jnp.exp(dt * a_re)
    ab_re = mag * jnp.cos(dt * a_im)
    ab_im = mag * jnp.sin(dt * a_im)
    num_re = ab_re - 1.0
    num_im = ab_im
    den = a_re * a_re + a_im * a_im
    cf_re = (num_re * a_re + num_im * a_im) / den
    cf_im = (num_im * a_re - num_re * a_im) / den
    return ab_re, ab_im, cf_re, cf_im


def _ssm_params_fwd(a_re, a_im, log_dt):
    shp = jax.ShapeDtypeStruct(a_re.shape, F32)

    def body(are_ref, aim_ref, ldt_ref, abr_ref, abi_ref, cfr_ref, cfi_ref, alr_ref, ali_ref):
        abr, abi, cfr, cfi = _ssm_discretise(are_ref[...], aim_ref[...], ldt_ref[...])
        abr_ref[...], abi_ref[...], cfr_ref[...], cfi_ref[...] = abr, abi, cfr, cfi
        pr, pi = abr, abi
        for _ in range(int(math.log2(SSM_L))):
            pr, pi = pr * pr - pi * pi, 2.0 * pr * pi
        alr_ref[...], ali_ref[...] = pr, pi

    return pl.pallas_call(body, name="ssm_params_fwd", out_shape=[shp] * 6)(a_re, a_im, log_dt)


def _ssm_params_bwd(a_re, a_im, log_dt, d_abr, d_abi, d_cfr, d_cfi):
    def body(are_ref, aim_ref, ldt_ref, g0, g1, g2, g3, dare_ref, daim_ref, dldt_ref):
        _, vjp = jax.vjp(_ssm_discretise, are_ref[...], aim_ref[...], ldt_ref[...])
        dare_ref[...], daim_ref[...], dldt_ref[...] = vjp((g0[...], g1[...], g2[...], g3[...]))

    return pl.pallas_call(
        body, name="ssm_params_bwd",
        out_shape=[jax.ShapeDtypeStruct(a_re.shape, F32), jax.ShapeDtypeStruct(a_im.shape, F32),
                   jax.ShapeDtypeStruct(log_dt.shape, F32)],
    )(a_re, a_im, log_dt, d_abr, d_abi, d_cfr, d_cfi)


def _scan_cols(j):
    return pl.ds(j * SSM_SB, SSM_SB)


def _rows8(r):
    return pl.ds(pl.multiple_of(r * SUBLANES, SUBLANES), SUBLANES)


def _bcast8(row):
    return jnp.broadcast_to(row, (SUBLANES, row.shape[-1]))


def _token_order_pick():
    tok = lax.broadcasted_iota(jnp.int32, (SSM_T, SSM_T), 0)
    row = lax.broadcasted_iota(jnp.int32, (SSM_T, SSM_T), 1)
    return (row == SUBLANES * (tok % SSM_L) + tok // SSM_L).astype(_MXU)


SCAN_UNROLL = 16


def _scan_loop(n, step, init):
    def trip(o, carry):
        for i in range(SCAN_UNROLL):
            carry = step(o * SCAN_UNROLL + i, carry)
        return carry

    return lax.fori_loop(0, n // SCAN_UNROLL, trip, init)


def _ssm_fwd(u, b_re, b_im, c_re, c_im, d_skip, coef):
    seq = u.shape[0]
    nc = seq // SSM_T
    T, L = SSM_T, SSM_L

    def body(u_ref, bre_ref, bim_ref, cre_ref, cim_ref, d_ref, are_ref, aim_ref, cfr_ref, cfi_ref, alr_ref, ali_ref,
             y_ref, yg_ref, sre_ref, sim_ref, ire_ref, iim_ref, car_re, car_im, end_re, end_im, yg_scan):
        c = pl.program_id(0)

        @pl.when(c == 0)
        def _():
            car_re[...] = jnp.zeros_like(car_re)
            car_im[...] = jnp.zeros_like(car_im)

        for j in range(SSM_JB):
            ub = u_ref[:, j * LANES:(j + 1) * LANES].astype(_MXU)
            bur = jnp.dot(ub, bre_ref[j], preferred_element_type=F32)
            bui = jnp.dot(ub, bim_ref[j], preferred_element_type=F32)
            cfr, cfi = cfr_ref[:, _scan_cols(j)], cfi_ref[:, _scan_cols(j)]
            sre_ref[:, _scan_cols(j)] = cfr * bur - cfi * bui
            sim_ref[:, _scan_cols(j)] = cfr * bui + cfi * bur

        for j in range(SSM_JB):
            cols = _scan_cols(j)
            ar, ai = _bcast8(are_ref[:, cols]), _bcast8(aim_ref[:, cols])

            def step1(r, s, cols=cols, ar=ar, ai=ai):
                sr, si = s
                rows = _rows8(r)
                return (ar * sr - ai * si + sre_ref[rows, cols], ar * si + ai * sr + sim_ref[rows, cols])

            zero = jnp.zeros((SUBLANES, SSM_SB), F32)
            er, ei = _scan_loop(L, step1, (zero, zero))
            end_re[:, cols] = er
            end_im[:, cols] = ei

        alr, ali = alr_ref[...], ali_ref[...]
        cr, ci = car_re[...], car_im[...]
        ire_ref[0:1, :] = cr
        iim_ref[0:1, :] = ci
        for i in range(1, SUBLANES):
            er, ei = end_re[i - 1:i, :], end_im[i - 1:i, :]
            cr, ci = alr * cr - ali * ci + er, alr * ci + ali * cr + ei
            ire_ref[i:i + 1, :] = cr
            iim_ref[i:i + 1, :] = ci

        for j in range(SSM_JB):
            cols = _scan_cols(j)
            ar, ai = _bcast8(are_ref[:, cols]), _bcast8(aim_ref[:, cols])

            def step2(r, s, cols=cols, ar=ar, ai=ai):
                sr, si = s
                rows = _rows8(r)
                nr = ar * sr - ai * si + sre_ref[rows, cols]
                ni = ar * si + ai * sr + sim_ref[rows, cols]
                sre_ref[rows, cols] = nr
                sim_ref[rows, cols] = ni
                return nr, ni

            _scan_loop(L, step2, (ire_ref[:, cols], iim_ref[:, cols]))

        car_re[...] = sre_ref[T - 1:T, :]
        car_im[...] = sim_ref[T - 1:T, :]

        for j in range(SSM_JB):
            cols = _scan_cols(j)
            ch = slice(j * LANES, (j + 1) * LANES)
            y = (jnp.dot(sre_ref[:, cols].astype(_MXU), cre_ref[j], preferred_element_type=F32)
                 - jnp.dot(sim_ref[:, cols].astype(_MXU), cim_ref[j], preferred_element_type=F32))
            y = y + d_ref[:, ch] * u_ref[:, ch].astype(F32)
            y_ref[:, ch] = y
            yg_scan[:, ch] = jax.nn.gelu(y).astype(yg_scan.dtype)
        yg_ref[...] = jnp.dot(_token_order_pick(), yg_scan[...], preferred_element_type=F32).astype(yg_ref.dtype)

    tok = pl.BlockSpec((T, SSM_W), lambda c: (c, 0))
    st = pl.BlockSpec((T, N_STATES), lambda c: (c, 0))
    ini = pl.BlockSpec((None, SUBLANES, N_STATES), lambda c: (c, 0, 0))
    bsp = pl.BlockSpec((SSM_JB, LANES, SSM_SB), lambda c: (0, 0, 0))
    csp = pl.BlockSpec((SSM_JB, SSM_SB, LANES), lambda c: (0, 0, 0))
    row_w = pl.BlockSpec((1, SSM_W), lambda c: (0, 0))
    row_s = pl.BlockSpec((1, N_STATES), lambda c: (0, 0))
    return pl.pallas_call(
        body, name="ssm_fwd", grid=(nc,),
        in_specs=[tok, bsp, bsp, csp, csp, row_w] + [row_s] * 6,
        out_specs=[tok, tok, st, st, ini, ini],
        out_shape=[jax.ShapeDtypeStruct((seq, SSM_W), F32), jax.ShapeDtypeStruct((seq, SSM_W), _MXU),
                   jax.ShapeDtypeStruct((seq, N_STATES), F32), jax.ShapeDtypeStruct((seq, N_STATES), F32),
                   jax.ShapeDtypeStruct((nc, SUBLANES, N_STATES), F32),
                   jax.ShapeDtypeStruct((nc, SUBLANES, N_STATES), F32)],
        scratch_shapes=[pltpu.VMEM((1, N_STATES), F32), pltpu.VMEM((1, N_STATES), F32),
                        pltpu.VMEM((SUBLANES, N_STATES), F32), pltpu.VMEM((SUBLANES, N_STATES), F32),
                        pltpu.VMEM((T, SSM_W), _MXU)],
        compiler_params=_params(("arbitrary",)),
    )(u, b_re, b_im, c_re, c_im, d_skip, *coef)


def _ssm_bwd(dyg, y, u, s_re, s_im, i_re, i_im, b_re, b_im, c_re, c_im, d_skip, coef, d_proj, deps=()):
    seq = u.shape[0]
    nc = seq // SSM_T
    T, L = SSM_T, SSM_L
    deps = list(deps) + [d_proj]

    def body(dyg_ref, y_ref, u_ref, sre_ref, sim_ref, ire_ref, iim_ref, bre_ref, bim_ref, cre_ref, cim_ref, d_ref,
             are_ref, aim_ref, cfr_ref, cfi_ref, alr_ref, ali_ref, *rest):
        (du_ref, dbre_out, dbim_out, dcre_out, dcim_out, dd_ref, dar_ref, dai_ref, dcfr_ref, dcfi_ref,
         lre, lim, car_re, car_im, end_re, end_im, ini_re, ini_im, dbre_ref, dbim_ref, dcre_ref, dcim_ref,
         dy_ref, du_scan) = rest[len(deps):]
        step = pl.program_id(0)
        dy_ref[...] = jax.vjp(jax.nn.gelu, y_ref[...])[1](dyg_ref[...])[0]

        @pl.when(step == 0)
        def _():
            car_re[...] = jnp.zeros_like(car_re)
            car_im[...] = jnp.zeros_like(car_im)
            for ref in (dbre_ref, dbim_ref, dcre_ref, dcim_ref, dd_ref, dar_ref, dai_ref, dcfr_ref, dcfi_ref):
                ref[...] = jnp.zeros_like(ref)

        for j in range(SSM_JB):
            dyb = dy_ref[:, j * LANES:(j + 1) * LANES].astype(_MXU)
            lre[:, _scan_cols(j)] = lax.dot_general(dyb, cre_ref[j], _NT, preferred_element_type=F32)
            lim[:, _scan_cols(j)] = -lax.dot_general(dyb, cim_ref[j], _NT, preferred_element_type=F32)

        for j in range(SSM_JB):
            cols = _scan_cols(j)
            ar, ai = _bcast8(are_ref[:, cols]), _bcast8(aim_ref[:, cols])

            def step1(t, s, cols=cols, ar=ar, ai=ai):
                sr, si = s
                rows = _rows8(L - 1 - t)
                return (ar * sr + ai * si + lre[rows, cols], ar * si - ai * sr + lim[rows, cols])

            zero = jnp.zeros((SUBLANES, SSM_SB), F32)
            er, ei = _scan_loop(L, step1, (zero, zero))
            end_re[:, cols] = er
            end_im[:, cols] = ei

        alr, ali = alr_ref[...], ali_ref[...]
        cr, ci = car_re[...], car_im[...]
        ini_re[SUBLANES - 1:SUBLANES, :] = cr
        ini_im[SUBLANES - 1:SUBLANES, :] = ci
        for i in range(SUBLANES - 2, -1, -1):
            er, ei = end_re[i + 1:i + 2, :], end_im[i + 1:i + 2, :]
            cr, ci = alr * cr + ali * ci + er, alr * ci - ali * cr + ei
            ini_re[i:i + 1, :] = cr
            ini_im[i:i + 1, :] = ci

        for j in range(SSM_JB):
            cols = _scan_cols(j)
            ar, ai = _bcast8(are_ref[:, cols]), _bcast8(aim_ref[:, cols])

            def step2(t, s, cols=cols, ar=ar, ai=ai):
                sr, si = s
                rows = _rows8(L - 1 - t)
                nr = ar * sr + ai * si + lre[rows, cols]
                ni = ar * si - ai * sr + lim[rows, cols]
                lre[rows, cols] = nr
                lim[rows, cols] = ni
                return nr, ni

            _scan_loop(L, step2, (ini_re[:, cols], ini_im[:, cols]))

        car_re[...] = lre[0:1, :]
        car_im[...] = lim[0:1, :]

        head, tail, body_rows = slice(0, SUBLANES), slice(SUBLANES, T), slice(0, T - SUBLANES)
        for j in range(SSM_JB):
            cols = _scan_cols(j)
            ch = slice(j * LANES, (j + 1) * LANES)
            lr, li = lre[:, cols], lim[:, cols]
            lt_r, lt_i, sp_r, sp_i = lre[tail, cols], lim[tail, cols], sre_ref[body_rows, cols], sim_ref[body_rows, cols]
            lh_r, lh_i, si_r, si_i = lre[head, cols], lim[head, cols], ire_ref[:, cols], iim_ref[:, cols]
            dar_ref[:, cols] += _colsum(lt_r * sp_r + lt_i * sp_i) + _colsum(lh_r * si_r + lh_i * si_i)
            dai_ref[:, cols] += _colsum(lt_i * sp_r - lt_r * sp_i) + _colsum(lh_i * si_r - lh_r * si_i)
            ub = u_ref[:, ch].astype(_MXU)
            uf = ub.astype(F32)
            bur = jnp.dot(ub, bre_ref[j], preferred_element_type=F32)
            bui = jnp.dot(ub, bim_ref[j], preferred_element_type=F32)
            dcfr_ref[:, cols] += _colsum(lr * bur + li * bui)
            dcfi_ref[:, cols] += _colsum(li * bur - lr * bui)
            cfr, cfi = cfr_ref[:, cols], cfi_ref[:, cols]
            dbur = (cfr * lr + cfi * li).astype(_MXU)
            dbui = (cfr * li - cfi * lr).astype(_MXU)
            dyf = dy_ref[:, ch]
            dyb = dyf.astype(_MXU)
            du = (lax.dot_general(dbur, bre_ref[j], _NT, preferred_element_type=F32)
                  + lax.dot_general(dbui, bim_ref[j], _NT, preferred_element_type=F32) + d_ref[:, ch] * dyf)
            du_scan[:, ch] = du.astype(du_scan.dtype)
            dbre_ref[j] += lax.dot_general(ub, dbur, _TN, preferred_element_type=F32)
            dbim_ref[j] += lax.dot_general(ub, dbui, _TN, preferred_element_type=F32)
            dcre_ref[j] += lax.dot_general(sre_ref[:, cols].astype(_MXU), dyb, _TN, preferred_element_type=F32)
            dcim_ref[j] -= lax.dot_general(sim_ref[:, cols].astype(_MXU), dyb, _TN, preferred_element_type=F32)
            dd_ref[:, ch] += _colsum(dyf * uf)
        du_ref[...] = jnp.dot(_token_order_pick(), du_scan[...], preferred_element_type=F32).astype(du_ref.dtype)

        @pl.when(step == nc - 1)
        def _():
            for acc, out in ((dbre_ref, dbre_out), (dbim_ref, dbim_out), (dcre_ref, dcre_out), (dcim_ref, dcim_out)):
                pltpu.sync_copy(acc, out)

    tok = pl.BlockSpec((T, SSM_W), lambda c: (nc - 1 - c, 0))
    st = pl.BlockSpec((T, N_STATES), lambda c: (nc - 1 - c, 0))
    ini = pl.BlockSpec((None, SUBLANES, N_STATES), lambda c: (nc - 1 - c, 0, 0))
    bsp = pl.BlockSpec((SSM_JB, LANES, SSM_SB), lambda c: (0, 0, 0))
    csp = pl.BlockSpec((SSM_JB, SSM_SB, LANES), lambda c: (0, 0, 0))
    row_w = pl.BlockSpec((1, SSM_W), lambda c: (0, 0))
    row_s = pl.BlockSpec((1, N_STATES), lambda c: (0, 0))
    big = pltpu.VMEM((T, N_STATES), F32)
    one = pltpu.VMEM((1, N_STATES), F32)
    eight = pltpu.VMEM((SUBLANES, N_STATES), F32)
    return pl.pallas_call(
        body, name="ssm_bwd", grid=(nc,),
        in_specs=[tok, tok, tok, st, st, ini, ini, bsp, bsp, csp, csp, row_w] + [row_s] * 6 + [_ANY] * len(deps),
        out_specs=[pl.BlockSpec((pl.Element(T), pl.Element(SSM_W)), lambda c: ((nc - 1 - c) * T, OFF_U * CW)),
                   _ANY, _ANY, _ANY, _ANY, row_w, row_s, row_s, row_s, row_s],
        input_output_aliases={18 + len(deps) - 1: 0},
        out_shape=[jax.ShapeDtypeStruct(d_proj.shape, d_proj.dtype),
                   jax.ShapeDtypeStruct((SSM_JB, LANES, SSM_SB), F32), jax.ShapeDtypeStruct((SSM_JB, LANES, SSM_SB), F32),
                   jax.ShapeDtypeStruct((SSM_JB, SSM_SB, LANES), F32), jax.ShapeDtypeStruct((SSM_JB, SSM_SB, LANES), F32),
                   jax.ShapeDtypeStruct((1, SSM_W), F32)] + [jax.ShapeDtypeStruct((1, N_STATES), F32)] * 4,
        scratch_shapes=[big, big, one, one, eight, eight, eight, eight,
                        pltpu.VMEM((SSM_JB, LANES, SSM_SB), F32), pltpu.VMEM((SSM_JB, LANES, SSM_SB), F32),
                        pltpu.VMEM((SSM_JB, SSM_SB, LANES), F32), pltpu.VMEM((SSM_JB, SSM_SB, LANES), F32),
                        pltpu.VMEM((T, SSM_W), F32), pltpu.VMEM((T, SSM_W), _MXU)],
        compiler_params=_params(("arbitrary",)),
    )(dyg, y, u, s_re, s_im, i_re, i_im, b_re, b_im, c_re, c_im, d_skip, *coef, *deps)


def _block_diag_b(b):
    t = b.reshape(SSM_JB, 8, STATE, GROUP).transpose(0, 1, 3, 2)
    eye = jnp.eye(8, dtype=b.dtype)
    return (t[:, :, :, None, :] * eye[None, :, None, :, None]).reshape(SSM_JB, LANES, SSM_SB)


def _block_diag_c(c):
    t = c.reshape(SSM_JB, 8, GROUP, STATE).transpose(0, 1, 3, 2)
    eye = jnp.eye(8, dtype=c.dtype)
    return (t[:, :, :, None, :] * eye[None, :, None, :, None]).reshape(SSM_JB, SSM_SB, LANES)


def _diag_of_b(blk):
    t = blk.reshape(SSM_JB, 8, GROUP, 8, STATE)
    d = jnp.sum(t * jnp.eye(8, dtype=blk.dtype)[None, :, None, :, None], axis=3)
    return d.transpose(0, 1, 3, 2).reshape(N_GROUPS, STATE, GROUP)


def _diag_of_c(blk):
    t = blk.reshape(SSM_JB, 8, STATE, 8, GROUP)
    d = jnp.sum(t * jnp.eye(8, dtype=blk.dtype)[None, :, None, :, None], axis=3)
    return d.transpose(0, 1, 3, 2).reshape(N_GROUPS, GROUP, STATE)


def _to_scan_order(v):
    seq, w = v.shape
    return v.reshape(seq // SSM_T, SUBLANES, SSM_L, w).transpose(0, 2, 1, 3).reshape(seq, w)


def _adamw_math(w, g, m, v):
    nm = ADAM_B1 * m + (1.0 - ADAM_B1) * g
    nv = ADAM_B2 * v + (1.0 - ADAM_B2) * jnp.square(g)
    m_hat = nm / (1.0 - ADAM_B1 ** ADAM_STEP)
    v_hat = nv / (1.0 - ADAM_B2 ** ADAM_STEP)
    return -ADAM_LR * (m_hat / (jnp.sqrt(v_hat) + ADAM_EPS) + ADAM_WD * w), nm, nv


def _adamw(w, g, m, v, *, name, tm, deps=()):
    rows, cols = w.shape
    nd = len(deps)

    def body(w_ref, g_ref, m_ref, v_ref, *rest):
        d_ref, nm_ref, nv_ref = rest[nd:]
        d_ref[...], nm_ref[...], nv_ref[...] = _adamw_math(w_ref[...], g_ref[...], m_ref[...], v_ref[...])

    spec = pl.BlockSpec((tm, cols), lambda i: (i, 0))
    shp = jax.ShapeDtypeStruct((rows, cols), F32)
    return pl.pallas_call(body, name=name, grid=(rows // tm,), in_specs=[spec] * 4 + [_ANY] * nd,
                          out_specs=[spec] * 3, out_shape=[shp] * 3,
                          compiler_params=_params(("arbitrary",)))(w, g, m, v, *deps)


def _place():
    x, y, c = lax.axis_index("x"), lax.axis_index("y"), lax.axis_index("c")
    chips = [(1 - x, y), (x, 1 - y), (1 - x, 1 - y)]
    return x, y, c, chips


def _remote(src, dst, send_sem, recv_sem, dev):
    return pltpu.make_async_remote_copy(src_ref=src, dst_ref=dst, send_sem=send_sem, recv_sem=recv_sem,
                                        device_id=dev, device_id_type=MESH)


def _place_shard(w, mine_arr, *, name, tm=256, deps=()):
    rows, cols = w.shape

    def body(m_ref, w_ref, *rest):
        rest[-1][...] = w_ref[...].astype(rest[-1].dtype)

    return pl.pallas_call(
        body, name=name,
        grid_spec=pltpu.PrefetchScalarGridSpec(
            num_scalar_prefetch=1, grid=(rows // tm,),
            in_specs=[pl.BlockSpec((tm, cols), lambda i, m: (i, 0))] + [_ANY] * len(deps),
            out_specs=pl.BlockSpec((None, tm, cols), lambda i, m: (m[0], i, 0))),
        out_shape=jax.ShapeDtypeStruct((N_CHIPS, rows, cols), _WIRE),
        compiler_params=_params(("arbitrary",)),
    )(mine_arr, w, *deps)


_HBM = pl.BlockSpec(memory_space=pltpu.HBM)
_SEM = pl.BlockSpec(memory_space=pltpu.SEMAPHORE)
_EFFECT = pltpu.SideEffectType.DATAFLOW_SIDE_EFFECTING


def _copies_start(name, bufs, plan, count, after=()):
    nb, na = len(bufs), len(after)

    def body(*refs):
        send_sems, recv_sems, token = refs[nb + na], refs[nb + na + 1], refs[-1]
        copies = plan(refs[:nb])
        assert len(copies) == count
        for i, (src, dst, dev, _) in enumerate(copies):
            _remote(src, dst, send_sems.at[i], recv_sems.at[i], dev).start()
        token[...] = jnp.zeros_like(token)

    res = pl.pallas_call(
        body, name=name, in_specs=[_HBM] * nb + [_ANY] * na,
        out_specs=(_SEM, _SEM, *[_HBM] * nb, pl.BlockSpec(memory_space=pltpu.VMEM)),
        out_shape=(pltpu.SemaphoreType.DMA((count,)), pltpu.SemaphoreType.DMA((count,)),
                   *[pltpu.HBM(b.shape, b.dtype) for b in bufs], jax.ShapeDtypeStruct((SUBLANES, LANES), F32)),
        input_output_aliases={i: 2 + i for i in range(nb)},
        compiler_params=pltpu.CompilerParams(has_side_effects=_EFFECT),
    )(*[pltpu.with_memory_space_constraint(b, pltpu.HBM) for b in bufs], *after)
    return (res[0], res[1]), list(res[2:2 + nb]), res[-1]


def _copies_wait(name, bufs, sems, plan, after=(), which=None):
    nb, na = len(bufs), len(after)

    def body(*refs):
        send_sems, recv_sems = refs[nb], refs[nb + 1]
        for i, (src, _, dev, land) in enumerate(plan(refs[:nb])):
            if which is not None and i not in which:
                continue
            cp = _remote(src, land, send_sems.at[i], recv_sems.at[i], dev)
            cp.wait_send()
            cp.wait_recv()

    res = pl.pallas_call(
        body, name=name, in_specs=[_HBM] * nb + [_SEM, _SEM] + [_ANY] * na, out_specs=[_HBM] * nb,
        out_shape=[pltpu.HBM(b.shape, b.dtype) for b in bufs],
        input_output_aliases={i: i for i in range(nb)},
        compiler_params=pltpu.CompilerParams(has_side_effects=_EFFECT),
    )(*bufs, *sems, *after)
    return list(res)


def _plan_gather_ici(fulls, which=(0, 1, 2)):
    x, y, c, chips = _place()
    copies = []
    for f in fulls:
        half = pl.ds(c * (f.shape[1] // 2), f.shape[1] // 2)
        own = f.at[2 * x + y, half]
        for chip in [chips[k] for k in which]:
            copies.append((own, own, (*chip, c), f.at[2 * chip[0] + chip[1], half]))
    return copies


def _plan_gather_d2d(fulls, which=(0, 1, 2)):
    x, y, c, chips = _place()
    copies = []
    for f in fulls:
        r2 = f.shape[1] // 2
        for chip in [chips[k] for k in which]:
            blk = 2 * chip[0] + chip[1]
            landed = f.at[blk, pl.ds(c * r2, r2)]
            copies.append((landed, landed, (x, y, 1 - c), f.at[blk, pl.ds((1 - c) * r2, r2)]))
    return copies


def _plan_relay_direct(fulls):
    (f,) = fulls
    x, y, c, chips = _place()
    half = pl.ds(c * (f.shape[1] // 2), f.shape[1] // 2)
    own = f.at[2 * x + y, half]
    return [(own, own, (*chip, c), f.at[2 * chip[0] + chip[1], half]) for chip in chips[:2]]


def _plan_relay_forward(fulls, k):
    (f,) = fulls
    x, y, c, chips = _place()
    r2 = f.shape[1] // 2
    half, other = pl.ds(c * r2, r2), pl.ds((1 - c) * r2, r2)
    quarter = pl.ds(c * r2 + k * (r2 // 2), r2 // 2)
    blk, far = 2 * chips[k][0] + chips[k][1], 2 * chips[2][0] + chips[2][1]
    passed, landed = f.at[blk, quarter], f.at[blk, half]
    return [(passed, passed, (*chips[1 - k], c), f.at[far, quarter]), (landed, landed, (x, y, 1 - c), f.at[blk, other])]


def _plan_relay_last(fulls):
    (f,) = fulls
    x, y, c, chips = _place()
    r2 = f.shape[1] // 2
    far = 2 * chips[2][0] + chips[2][1]
    landed = f.at[far, pl.ds(c * r2, r2)]
    return [(landed, landed, (x, y, 1 - c), f.at[far, pl.ds((1 - c) * r2, r2)])]


def _plan_swap_halves(refs):
    x, y, c, _ = _place()
    n = len(refs) // 2
    copies = []
    for g, land in zip(refs[:n], refs[n:]):
        r2 = g.shape[1] // 2
        copies.append((g.at[:, pl.ds((1 - c) * r2, r2), :], land, (x, y, 1 - c), land))
    return copies


def _plan_scatter_chips(refs):
    x, y, c, chips = _place()
    n = len(refs) // 2
    copies = []
    for h, land in zip(refs[:n], refs[n:]):
        for k, chip in enumerate(chips):
            copies.append((h.at[2 * chip[0] + chip[1]], land.at[k], (*chip, c), land.at[k]))
    return copies


def _plan_join_halves(totals):
    x, y, c, _ = _place()
    copies = []
    for t in totals:
        r2 = t.shape[0] // 2
        mine = t.at[pl.ds(c * r2, r2)]
        copies.append((mine, mine, (x, y, 1 - c), t.at[pl.ds((1 - c) * r2, r2)]))
    return copies


def _add_sibling_half(g, got, c_arr, *, name, tm):
    _, rows, cols = g.shape
    r2 = rows // 2
    nb = r2 // tm

    def body(c_ref, g_ref, r_ref, o_ref):
        o_ref[...] = (g_ref[...].astype(F32) + r_ref[...].astype(F32)).astype(o_ref.dtype)

    return pl.pallas_call(
        body, name=name,
        grid_spec=pltpu.PrefetchScalarGridSpec(
            num_scalar_prefetch=1, grid=(N_CHIPS, nb),
            in_specs=[pl.BlockSpec((None, tm, cols), lambda b, i, c: (b, c[0] * nb + i, 0)),
                      pl.BlockSpec((None, tm, cols), lambda b, i, c: (b, i, 0))],
            out_specs=pl.BlockSpec((None, tm, cols), lambda b, i, c: (b, i, 0))),
        out_shape=jax.ShapeDtypeStruct((N_CHIPS, r2, cols), _WIRE),
        compiler_params=_params(("arbitrary", "arbitrary")),
    )(c_arr, g, got)


def _add_chips(h, got, place_arr, *, name, tm):
    _, r2, cols = h.shape
    nb = r2 // tm

    def body(p_ref, h_ref, r_ref, o_ref):
        o_ref[...] = ((h_ref[...].astype(F32) + r_ref[0].astype(F32)) + r_ref[1].astype(F32)) + r_ref[2].astype(F32)

    return pl.pallas_call(
        body, name=name,
        grid_spec=pltpu.PrefetchScalarGridSpec(
            num_scalar_prefetch=1, grid=(nb,),
            in_specs=[pl.BlockSpec((None, tm, cols), lambda i, p: (p[0], i, 0)),
                      pl.BlockSpec((3, tm, cols), lambda i, p: (0, i, 0))],
            out_specs=pl.BlockSpec((tm, cols), lambda i, p: (p[1] * nb + i, 0))),
        out_shape=jax.ShapeDtypeStruct((2 * r2, cols), F32),
        compiler_params=_params(("arbitrary",)),
    )(place_arr, h, got)


class _ReduceScatter:
    def __init__(self, tag, names, grads):
        self.tag, self.names, self.n = tag, names, len(names)
        core = lax.axis_index("c").astype(jnp.int32)
        chip = (2 * lax.axis_index("x") + lax.axis_index("y")).astype(jnp.int32)
        self.c_arr, self.place_arr = core.reshape(1), jnp.stack([chip, core])
        self.bufs = list(grads)

    def _start(self, step, bufs, plan, count, after):
        self.plan = plan
        self.step = f"grad_{step}_{self.tag}"
        self.sems, self.bufs, token = _copies_start(self.step + "_start", bufs, plan, count, after)
        return [token]

    def _wait(self, after):
        self.bufs = _copies_wait(self.step + "_wait", self.bufs, self.sems, self.plan, after)
        return self.bufs

    def start_swap(self, after=()):
        lands = [lax.empty((N_CHIPS, g.shape[1] // 2, g.shape[2]), g.dtype) for g in self.bufs]
        return self._start("swap", self.bufs + lands, _plan_swap_halves, self.n, after)

    def start_scatter(self, after):
        bufs = self._wait(after)
        pair = [_add_sibling_half(g, r, self.c_arr, name=f"grad_add_sibling_{nm}", tm=min(256, g.shape[1] // 2))
                for nm, g, r in zip(self.names, bufs[:self.n], bufs[self.n:])]
        lands = [lax.empty((3,) + h.shape[1:], h.dtype) for h in pair]
        return self._start("scatter", pair + lands, _plan_scatter_chips, 3 * self.n, ())

    def start_join(self, after):
        bufs = self._wait(after)
        total = [_add_chips(h, r, self.place_arr, name=f"grad_add_chips_{nm}", tm=min(256, h.shape[1]))
                 for nm, h, r in zip(self.names, bufs[:self.n], bufs[self.n:])]
        return self._start("join", total, _plan_join_halves, self.n, ())

    def finish(self, after):
        return dict(zip(self.names, self._wait(after)))


def _all_gather_small(v):
    m_per, n = v.shape

    def body(x_ref, out_ref, send_sems, recv_sems, local_sem):
        x, y, c, chips = _place()
        me, sibling = (x, y, c), (x, y, 1 - c)

        def rows(px, py, pc):
            return out_ref.at[4 * px + 2 * py + pc]

        def copy(k, block, to, src=None):
            return _remote(rows(*block) if src is None else src, rows(*block), send_sems.at[k], recv_sems.at[k], to)

        mine = pltpu.make_async_copy(x_ref, rows(*me), local_sem)
        mine.start()
        first = [copy(0, me, sibling, src=x_ref)]
        first += [copy(1 + j, me, (*chip, c), src=x_ref) for j, chip in enumerate(chips)]
        for cp in first:
            cp.start()
        passed = [copy(4 + j, (*chip, c), sibling) for j, chip in enumerate(chips)]
        for j, chip in enumerate(chips):
            copy(1 + j, (*chip, c), me).wait_recv()
            passed[j].start()
        copy(0, sibling, me).wait_recv()
        for j, chip in enumerate(chips):
            copy(4 + j, (*chip, 1 - c), me).wait_recv()
        for cp in first + passed:
            cp.wait_send()
        mine.wait()

    return pl.pallas_call(
        body, name="gather_small_grads",
        out_shape=jax.ShapeDtypeStruct((8, m_per, n), v.dtype),
        in_specs=[pl.BlockSpec(memory_space=pltpu.VMEM)], out_specs=pl.BlockSpec(memory_space=pltpu.VMEM),
        scratch_shapes=[pltpu.SemaphoreType.DMA((7,)), pltpu.SemaphoreType.DMA((7,)), pltpu.SemaphoreType.DMA],
        compiler_params=pltpu.CompilerParams(vmem_limit_bytes=VMEM_LIMIT),
    )(v)


def _sum8(v, *, name):
    _, m, n = v.shape

    def body(v_ref, o_ref):
        acc = v_ref[0]
        for d in range(1, 8):
            acc = acc + v_ref[d]
        o_ref[...] = acc

    return pl.pallas_call(body, name=name, out_shape=jax.ShapeDtypeStruct((m, n), F32),
                          compiler_params=pltpu.CompilerParams(vmem_limit_bytes=VMEM_LIMIT))(v)


def _local_step(x, target, norm_w, q_norm_w, k_norm_w, sinks, a_re, a_im, log_dt, b_re, b_im, c_re, c_im, d_skip,
                b_glu, io):
    seq = x.shape[0]
    qw2 = jnp.tile(q_norm_w.reshape(1, HEAD_DIM), (1, HEADS_PER_TILE))
    kw2 = jnp.tile(k_norm_w.reshape(1, HEAD_DIM), (1, HEADS_PER_TILE))
    nw, bg = norm_w.reshape(1, D_MODEL), b_glu.reshape(1, D_MODEL)
    dsk = d_skip.reshape(1, SSM_W)

    h, rstd = _rms_fwd(x, nw, deps=io.begin())
    proj, w_in4 = io.projection(h)
    attn, lse, ya_in = _attn2_fwd(proj, qw2, kw2, sinks, deps=io.after_proj(proj))
    w_ap4 = io.weight("w_attn_proj", ya_in)
    w_glu4, w_sp4, w_out = io.weight("w_glu", ya_in), io.weight("w_ssm_proj", ya_in), io.weight("w_out", ya_in)
    y_a = _mm(ya_in, w_ap4, mode="nn", name="mm_attn_proj", tm=2048, tn=512, tk=ATTN_W, b_blocked=True,
              rows_outer=True, out_dtype=_MXU)

    flat_a = (a_re.reshape(1, N_STATES), a_im.reshape(1, N_STATES), jnp.repeat(log_dt, STATE).reshape(1, N_STATES))
    coef = _ssm_params_fwd(*flat_a)
    bre_blk, bim_blk = _block_diag_b(b_re).astype(_MXU), _block_diag_b(b_im).astype(_MXU)
    cre_blk, cim_blk = _block_diag_c(c_re).astype(_MXU), _block_diag_c(c_im).astype(_MXU)
    u_scan = _to_scan_order(proj[:, OFF_U * CW:OFF_U * CW + SSM_W])
    y_scan, yg, s_re, s_im, i_re, i_im = _ssm_fwd(u_scan, bre_blk, bim_blk, cre_blk, cim_blk, dsk, coef)
    glu, ys_in = _mm_glu_gate(yg, w_glu4, bg, proj)
    y_s = _mm(ys_in, w_sp4, mode="nn", name="mm_ssm_proj", tm=2048, tn=512, tk=SSM_W, b_blocked=True,
              rows_outer=True, out_dtype=_MXU)

    merged, dout, dout_b, sq = _mm_merge_out_loss(proj, y_a, y_s, w_out, x, target)
    loss = 0.5 * jnp.sum(sq) / D_MODEL

    d_ya, d_ys, d_proj = _mm_merge_bwd(dout_b, w_out, proj, y_a, y_s)
    g_w_out = _mm(merged, dout_b, mode="tn", name="mm_g_w_out", tm=1024, tn=D_MODEL, tk=2048, out_dtype=_WIRE)

    d_ya_in = _mm(d_ya, w_ap4, mode="nt", name="mm_d_attn_gate", tm=2048, tn=ATTN_W, tk=512, b_blocked=True)
    g_w_ap = _mm(ya_in, d_ya, mode="tn", name="mm_g_w_attn_proj", tm=ATTN_W, tn=D_MODEL, tk=2048, out_dtype=_WIRE,
                 out_blocked=True)

    g_w_sp = _mm(ys_in, d_ys, mode="tn", name="mm_g_w_ssm_proj", tm=SSM_W, tn=D_MODEL, tk=2048, out_dtype=_WIRE,
                 out_blocked=True)
    d_glu, d_proj, g_bglu = _mm_ssm_gate_bwd(d_ys, w_sp4, glu, bg, proj, d_proj)
    d_yg = _mm(d_glu, w_glu4, mode="nt", name="mm_d_gelu", tm=2048, tn=SSM_W, tk=512, b_blocked=True)
    g_w_glu = _mm(yg, d_glu, mode="tn", name="mm_g_w_glu", tm=SSM_W, tn=D_MODEL, tk=2048, out_dtype=_WIRE, out_blocked=True)
    dep = io.later_grads(dict(w_attn_proj=g_w_ap, w_glu=g_w_glu, w_ssm_proj=g_w_sp,
                              w_out=g_w_out.reshape(N_CHIPS, D_MODEL // N_CHIPS, D_MODEL)))

    d_proj, g_qw2, g_kw2, g_sk = _attn2_bwd(proj, qw2, kw2, sinks, lse, attn, d_ya_in, d_proj, deps=dep)
    dep = io.before_scan_backward([d_proj])
    (d_proj, g_bre, g_bim, g_cre, g_cim, g_dsk, g_abr, g_abi, g_cfr, g_cfi) = _ssm_bwd(
        _to_scan_order(d_yg), y_scan, u_scan, s_re, s_im, i_re, i_im, bre_blk, bim_blk, cre_blk, cim_blk, dsk, coef,
        d_proj, deps=dep)
    g_are, g_aim, g_ldt = _ssm_params_bwd(*flat_a, g_abr, g_abi, g_cfr, g_cfi)
    g_are, g_aim = g_are.reshape(N_GROUPS, STATE), g_aim.reshape(N_GROUPS, STATE)
    g_ldt = g_ldt.reshape(N_GROUPS, STATE).sum(axis=1)
    dep = io.before_input_projection_grad([d_proj]) + io.small_grads(dict(
        q_norm_w=g_qw2[0, :HEAD_DIM] + g_qw2[0, HEAD_DIM:], k_norm_w=g_kw2[0, :HEAD_DIM] + g_kw2[0, HEAD_DIM:],
        sinks=g_sk.reshape(N_Q_HEADS), A_re=g_are, A_im=g_aim, log_dt=g_ldt,
        B_re=_diag_of_b(g_bre), B_im=_diag_of_b(g_bim), C_re=_diag_of_c(g_cre), C_im=_diag_of_c(g_cim),
        D_skip=g_dsk.reshape(N_GROUPS, GROUP), b_glu=g_bglu.reshape(D_MODEL)))
    g_w_in = _mm(h, d_proj, mode="tn", name="mm_g_w_in", tm=1024, tn=IN_W // 4, tk=2048, out_dtype=_WIRE,
                 out_blocked=True, deps=dep)
    dep = io.input_projection_grad(g_w_in)
    d_h = _mm(d_proj, w_in4, mode="nt", name="mm_d_h", tm=1024, tn=D_MODEL, tk=IN_W // 4, b_blocked=True, deps=dep)
    grad_x, g_nw = _rms_bwd(d_h, x, rstd, nw, dout)
    return loss, grad_x, g_nw.reshape(D_MODEL)


_SMALL = ["norm_w", "q_norm_w", "k_norm_w", "sinks", "A_re", "A_im", "log_dt", "B_re", "B_im", "C_re", "C_im",
          "D_skip", "b_glu"]
_BIG = ["w_in", "w_attn_proj", "w_glu", "w_ssm_proj", "w_out"]
_LATER = _BIG[1:]
_RELATIONS = ("flip_x", "flip_y", "flip_xy")
_ORDER = ["norm_w", "w_in", "q_norm_w", "k_norm_w", "sinks", "w_attn_proj", "A_re", "A_im", "log_dt", "B_re", "B_im",
          "C_re", "C_im", "D_skip", "w_glu", "b_glu", "w_ssm_proj", "w_out"]
_PACK_W = 1024
_MINOR_SWAPPED = ("B_re", "B_im")


def _swap_minor(a):
    return jnp.swapaxes(a, 1, 2)


def _packed_rows(size):
    unit = SUBLANES * _PACK_W
    return -(-size // unit) * SUBLANES


def _pack_small(d, names):
    parts = []
    for n in names:
        flat = d[n].reshape(-1).astype(F32)
        rows = _packed_rows(flat.shape[0])
        parts.append(jnp.pad(flat, (0, rows * _PACK_W - flat.shape[0])).reshape(rows, _PACK_W))
    return jnp.concatenate(parts, axis=0)


def _unpack_small(packed, like, names):
    out, pos = {}, 0
    for n in names:
        rows = _packed_rows(like[n].size)
        out[n] = packed[pos:pos + rows].reshape(-1)[:like[n].size].reshape(like[n].shape)
        pos += rows
    return out


def _place_block(v, index_arr, *, name):
    rows, cols = v.shape

    def body(i_ref, v_ref, o_ref):
        o_ref[...] = v_ref[...]

    return pl.pallas_call(
        body, name=name,
        grid_spec=pltpu.PrefetchScalarGridSpec(
            num_scalar_prefetch=1, grid=(1,),
            in_specs=[pl.BlockSpec((rows, cols), lambda i, d: (0, 0))],
            out_specs=pl.BlockSpec((None, rows, cols), lambda i, d: (d[0], 0, 0))),
        out_shape=jax.ShapeDtypeStruct((8, rows, cols), v.dtype),
        compiler_params=_params(("arbitrary",)),
    )(index_arr, v)


def _plan_all_to_all(refs):
    (land,) = refs
    x, y, c, _ = _place()
    own = land.at[4 * x + 2 * y + c]
    copies = []
    for fx, fy, fc in [(0, 0, 1), (0, 1, 0), (0, 1, 1), (1, 0, 0), (1, 0, 1), (1, 1, 0), (1, 1, 1)]:
        px, py, pc = (1 - x) if fx else x, (1 - y) if fy else y, (1 - c) if fc else c
        copies.append((own, own, (px, py, pc), land.at[4 * px + 2 * py + pc]))
    return copies


def _adamw_whole(w, g, m, v, *, name):
    def body(w_ref, g_ref, m_ref, v_ref, d_ref, nm_ref, nv_ref):
        d_ref[...], nm_ref[...], nv_ref[...] = _adamw_math(w_ref[...], g_ref[...], m_ref[...], v_ref[...])

    return pl.pallas_call(body, name=name, out_shape=[jax.ShapeDtypeStruct(w.shape, F32)] * 3)(w, g, m, v)


class _Exchanges:
    def __init__(self, w, m, v):
        self.w, self.m, self.v = w, m, v
        self.grads, self.delta, self.new_m, self.new_v = {}, {}, {}, {}

    def _adamw(self, names, deps):
        for n in names:
            self.delta[n], self.new_m[n], self.new_v[n] = _adamw(
                self.w[n], self.grads[n], self.m[n], self.v[n], name=f"adamw_{n}", tm=256, deps=deps)

    def begin(self):
        chip = (2 * lax.axis_index("x") + lax.axis_index("y")).astype(jnp.int32).reshape(1)
        w_in = _place_shard(self.w["w_in"], chip, name="place_w_in")
        self.w_in_sems, self.w_in_buf, token = _copies_start("gather_w_in_direct_start", [w_in], _plan_relay_direct, 2)
        self.later_full = [_place_shard(self.w[n], chip, name=f"place_{n}", deps=[token]) for n in _LATER]
        return self.later_full

    def projection(self, h):
        x, y = lax.axis_index("x"), lax.axis_index("y")
        blks = [jnp.asarray(b, jnp.int32).reshape(1)
                for b in (2 * x + y, 2 * (1 - x) + y, 2 * x + (1 - y), 2 * (1 - x) + (1 - y))]
        bufs = self.w_in_buf
        proj = _mm_chip_block(h, bufs[0], blks[0], None, name="mm_proj_own", out_dtype=_MXU)
        relay, token = [], proj
        for k, tag in enumerate(_RELATIONS[:2]):
            bufs = _copies_wait(f"gather_w_in_direct_{tag}_wait", bufs, self.w_in_sems, _plan_relay_direct, [token],
                                which=(k,))
            plan = functools.partial(_plan_relay_forward, k=k)
            sems, bufs, token = _copies_start(f"gather_w_in_relay_{tag}_start", bufs, plan, 2)
            relay.append((sems, plan))
        self.rest = _copies_start("gather_ici_rest_start", self.later_full, _plan_gather_ici, 3 * len(_LATER),
                                  after=[token])
        token = self.rest[2]
        for k, tag in enumerate(_RELATIONS[:2]):
            bufs = _copies_wait(f"gather_w_in_hand_{tag}_wait", bufs, relay[k][0], relay[k][1], [token], which=(1,))
            token = proj = _mm_chip_block(h, bufs[0], blks[1 + k], proj, name=f"mm_proj_{tag}", out_dtype=_MXU)
        for k, tag in enumerate(_RELATIONS[:2]):
            bufs = _copies_wait(f"gather_w_in_relay_{tag}_wait", bufs, relay[k][0], relay[k][1], [token], which=(0,))
        sems, bufs, token = _copies_start("gather_w_in_last_start", bufs, _plan_relay_last, 1)
        bufs = _copies_wait("gather_w_in_last_wait", bufs, sems, _plan_relay_last, [token])
        proj = _mm_chip_block(h, bufs[0], blks[3], proj, name="mm_proj_flip_xy", out_dtype=_MXU)
        return proj, bufs[0]

    def weight(self, name, after):
        if self.rest is not None:
            sems, bufs = self.rest
            later = dict(zip(_LATER, _copies_wait("gather_d2d_rest_wait", bufs, sems, _plan_gather_d2d, [after])))
            later["w_out"] = later["w_out"].reshape(D_MODEL, D_MODEL)
            self.later, self.rest = later, None
        return self.later[name]

    def after_proj(self, proj):
        sems, bufs, _ = self.rest
        bufs = _copies_wait("gather_ici_rest_wait", bufs, sems, _plan_gather_ici, [proj])
        sems, bufs, token = _copies_start("gather_d2d_rest_start", bufs, _plan_gather_d2d, 3 * len(_LATER))
        self.rest = (sems, bufs)
        return [token]

    def later_grads(self, grads):
        self.rs_later = _ReduceScatter("later", _LATER, [grads[n] for n in _LATER])
        return self.rs_later.start_swap()

    def before_scan_backward(self, after):
        return self.rs_later.start_scatter(after)

    def before_input_projection_grad(self, after):
        return self.rs_later.start_join(after)

    def input_projection_grad(self, g_w_in):
        self.grads.update(self.rs_later.finish([g_w_in]))
        self.rs_in = _ReduceScatter("w_in", ["w_in"], [g_w_in])
        self._adamw(_LATER, self.rs_in.start_swap())
        return self.rs_in.start_scatter([self.delta[n] for n in _LATER])

    def _adamw_small(self, names):
        for n in names:
            swap = _swap_minor if n in _MINOR_SWAPPED else (lambda a: a)
            out = _adamw_whole(swap(self.w[n]), self.grads[n], swap(self.m[n]), swap(self.v[n]), name=f"adamw_{n}")
            self.delta[n], self.new_m[n], self.new_v[n] = [swap(o) for o in out]
            self.grads[n] = swap(self.grads[n])

    def small_grads(self, grads):
        me = (4 * lax.axis_index("x") + 2 * lax.axis_index("y") + lax.axis_index("c")).astype(jnp.int32).reshape(1)
        grads = {n: _swap_minor(g) if n in _MINOR_SWAPPED else g for n, g in grads.items()}
        land = _place_block(_pack_small(grads, _SMALL[1:]), me, name="place_small_grads")
        self.small = _copies_start("gather_small_start", [land], _plan_all_to_all, 7)
        return [self.small[2]]

    def finish(self, g_norm_w, loss, after):
        join = self.rs_in.start_join(after)
        sems, bufs, _ = self.small
        (land,) = _copies_wait("gather_small_wait", bufs, sems, _plan_all_to_all, join)
        like = {n: _swap_minor(self.w[n]) if n in _MINOR_SWAPPED else self.w[n] for n in _SMALL[1:]}
        self.grads.update(_unpack_small(_sum8(land, name="sum_small_grads"), like, _SMALL[1:]))
        self._adamw_small(_SMALL[1:])
        rows = _packed_rows(g_norm_w.size)
        late = jnp.concatenate([_pack_small(dict(norm_w=g_norm_w), _SMALL[:1]),
                                jnp.pad(loss.reshape(1, 1), ((0, SUBLANES - 1), (0, _PACK_W - 1)))], axis=0)
        late = _sum8(_all_gather_small(late), name="sum_norm_w_grad_and_loss")
        self.grads.update(_unpack_small(late[:rows], self.w, _SMALL[:1]))
        self._adamw_small(_SMALL[:1])
        self.grads.update(self.rs_in.finish([self.delta[_SMALL[0]]]))
        self._adamw(["w_in"], ())
        return late[rows, 0]


def kernel(x, norm_w, w_in, q_norm_w, k_norm_w, sinks, w_attn_proj, A_re, A_im, log_dt, B_re, B_im, C_re, C_im, D_skip, w_glu, b_glu, w_ssm_proj, w_out, loss_target, m_norm_w, m_w_in, m_q_norm_w, m_k_norm_w, m_sinks, m_w_attn_proj, m_A_re, m_A_im, m_log_dt, m_B_re, m_B_im, m_C_re, m_C_im, m_D_skip, m_w_glu, m_b_glu, m_w_ssm_proj, m_w_out, v_norm_w, v_w_in, v_q_norm_w, v_k_norm_w, v_sinks, v_w_attn_proj, v_A_re, v_A_im, v_log_dt, v_B_re, v_B_im, v_C_re, v_C_im, v_D_skip, v_w_glu, v_b_glu, v_w_ssm_proj, v_w_out):
    w = dict(norm_w=norm_w, w_in=w_in, q_norm_w=q_norm_w, k_norm_w=k_norm_w, sinks=sinks, w_attn_proj=w_attn_proj,
             A_re=A_re, A_im=A_im, log_dt=log_dt, B_re=B_re, B_im=B_im, C_re=C_re, C_im=C_im, D_skip=D_skip,
             w_glu=w_glu, b_glu=b_glu, w_ssm_proj=w_ssm_proj, w_out=w_out)
    m = dict(norm_w=m_norm_w, w_in=m_w_in, q_norm_w=m_q_norm_w, k_norm_w=m_k_norm_w, sinks=m_sinks,
             w_attn_proj=m_w_attn_proj, A_re=m_A_re, A_im=m_A_im, log_dt=m_log_dt, B_re=m_B_re, B_im=m_B_im,
             C_re=m_C_re, C_im=m_C_im, D_skip=m_D_skip, w_glu=m_w_glu, b_glu=m_b_glu, w_ssm_proj=m_w_ssm_proj,
             w_out=m_w_out)
    v = dict(norm_w=v_norm_w, w_in=v_w_in, q_norm_w=v_q_norm_w, k_norm_w=v_k_norm_w, sinks=v_sinks,
             w_attn_proj=v_w_attn_proj, A_re=v_A_re, A_im=v_A_im, log_dt=v_log_dt, B_re=v_B_re, B_im=v_B_im,
             C_re=v_C_re, C_im=v_C_im, D_skip=v_D_skip, w_glu=v_w_glu, b_glu=v_b_glu, w_ssm_proj=v_w_ssm_proj,
             w_out=v_w_out)

    io = _Exchanges(w, m, v)
    loss, grad_x, g_norm_w = _local_step(x[0], loss_target[0], norm_w, q_norm_w, k_norm_w, sinks, A_re, A_im, log_dt,
                                         B_re, B_im, C_re, C_im, D_skip, b_glu, io)
    loss = io.finish(g_norm_w, loss, [grad_x])
    grads, delta, new_m, new_v = io.grads, io.delta, io.new_m, io.new_v

    return (loss, grad_x[None], *[grads[n] for n in _ORDER], *[delta[n] for n in _ORDER],
            *[new_m[n] for n in _ORDER], *[new_v[n] for n in _ORDER])
```

```python
import functools
import math

import jax
import jax.numpy as jnp
from jax import lax
from jax.experimental import pallas as pl
from jax.experimental.pallas import tpu as pltpu

F32 = jnp.float32
_MXU = jnp.bfloat16
_WIRE = jnp.bfloat16

LANES = 128
SUBLANES = 8
VMEM_LIMIT = 56 * 1024 * 1024

D_MODEL = 2048
HEAD_DIM = 64
N_Q_HEADS = 16
N_KV_HEADS = 4
Q_PER_KV = 4
ATTN_W = 1024
KV_W = 256
WINDOW = 128
SSM_W = 1024
GROUP = 16
N_GROUPS = 64
STATE = 64
N_STATES = N_GROUPS * STATE
IN_W = 8704
NORM_EPS = 1e-6
N_CHIPS = 4
CW = 512
OFF_AGATE, OFF_U, OFF_Z, OFF_GA, OFF_GS = 3, 5, 7, 9, 13

SSM_T = 256
SSM_L = SSM_T // SUBLANES
SSM_JB = 8
SSM_SB = N_STATES // SSM_JB

ADAM_LR, ADAM_B1, ADAM_B2, ADAM_EPS, ADAM_WD, ADAM_STEP = 0.001, 0.9, 0.999, 1e-08, 0.01, 10

MESH = pl.DeviceIdType.MESH
_ANY = pl.BlockSpec(memory_space=pl.ANY)


def _params(sem=None):
    return pltpu.CompilerParams(dimension_semantics=sem, vmem_limit_bytes=VMEM_LIMIT)


def _mm(a, b, *, mode, name, tm, tn, tk, out_dtype=F32, b_blocked=False, out_blocked=False, rows_outer=False,
        deps=()):
    nd = len(deps)
    if mode == "tn":
        K, M = a.shape
    else:
        M, K = a.shape
    if mode == "nn":
        N = b.shape[0] * b.shape[2] if b_blocked else b.shape[1]
    elif mode == "nt":
        N = b.shape[1] if b_blocked else b.shape[0]
    else:
        N = b.shape[1]
    tm, tn, tk = min(tm, M), min(tn, N), min(tk, K)
    nj, ni, nk = N // tn, M // tm, K // tk
    assert nj * tn == N and ni * tm == M and nk * tk == K, (name, M, N, K)
    dims = {"nn": (((1,), (0,)), ((), ())), "nt": (((1,), (1,)), ((), ())), "tn": (((0,), (0,)), ((), ()))}[mode]

    if mode == "tn":
        a_spec = pl.BlockSpec((tk, tm), lambda j, i, k: (k, i))
    else:
        a_spec = pl.BlockSpec((tm, tk), lambda j, i, k: (i, k))
    if mode == "nn":
        if b_blocked:
            assert b.shape[0] == nj and b.shape[2] == tn
            b_spec = pl.BlockSpec((None, tk, tn), lambda j, i, k: (j, k, 0))
        else:
            b_spec = pl.BlockSpec((tk, tn), lambda j, i, k: (k, j))
    elif mode == "nt":
        if b_blocked:
            assert b.shape[0] == nk and b.shape[2] == tk
            b_spec = pl.BlockSpec((None, tn, tk), lambda j, i, k: (k, j, 0))
        else:
            b_spec = pl.BlockSpec((tn, tk), lambda j, i, k: (j, k))
    else:
        b_spec = pl.BlockSpec((tk, tn), lambda j, i, k: (k, j))
    whole_out = out_blocked and nj == 1
    if whole_out:
        assert ni == 1
        o_spec = pl.BlockSpec((N_CHIPS, tm, tn // N_CHIPS), lambda j, i, k: (0, 0, 0))
        o_shape = jax.ShapeDtypeStruct((N_CHIPS, M, tn // N_CHIPS), out_dtype)
    elif out_blocked:
        assert nj == N_CHIPS
        o_spec = pl.BlockSpec((None, tm, tn), lambda j, i, k: (j, i, 0))
        o_shape = jax.ShapeDtypeStruct((nj, M, tn), out_dtype)
    else:
        o_spec = pl.BlockSpec((tm, tn), lambda j, i, k: (i, j))
        o_shape = jax.ShapeDtypeStruct((M, N), out_dtype)
    use_acc = nk > 1 and (out_dtype != F32 or whole_out)

    def body(a_ref, b_ref, *rest):
        o_ref, scratch = rest[nd], rest[nd + 1:]

        def product():
            return lax.dot_general(a_ref[...].astype(_MXU), b_ref[...].astype(_MXU), dims, preferred_element_type=F32)

        def write(result):
            if whole_out:
                w = tn // N_CHIPS
                for c in range(N_CHIPS):
                    o_ref[c] = result[:, c * w:(c + 1) * w].astype(o_ref.dtype)
            else:
                o_ref[...] = result.astype(o_ref.dtype)

        if nk == 1:
            write(product())
            return
        k = pl.program_id(2)
        acc = scratch[0] if use_acc else o_ref

        @pl.when(k == 0)
        def _():
            acc[...] = jnp.zeros_like(acc)

        acc[...] += product()

        if use_acc:
            @pl.when(k == nk - 1)
            def _():
                write(acc[...])

    specs = [a_spec, b_spec, o_spec]
    grid = (nj, ni, nk)
    if rows_outer:
        specs = [pl.BlockSpec(s.block_shape, lambda i, j, k, f=s.index_map: f(j, i, k)) for s in specs]
        grid = (ni, nj, nk)
    return pl.pallas_call(
        body, name=name, grid=grid, in_specs=specs[:2] + [_ANY] * nd, out_specs=specs[2],
        out_shape=o_shape, scratch_shapes=[pltpu.VMEM((tm, tn), F32)] if use_acc else [],
        compiler_params=_params(("parallel", "parallel", "arbitrary")),
    )(a, b, *deps)


def _mm_chip_block(a, b4, blk, prev, *, name, tm=1024, out_dtype=F32, deps=()):
    M, K = a.shape
    nchip, _, C = b4.shape
    tm = min(tm, M)
    extra = ([] if prev is None else [prev]) + list(deps)

    def body(blk_ref, a_ref, b_ref, *rest):
        rest[-1][...] = jnp.dot(a_ref[...].astype(_MXU), b_ref[...].astype(_MXU),
                                preferred_element_type=F32).astype(rest[-1].dtype)

    return pl.pallas_call(
        body, name=name,
        grid_spec=pltpu.PrefetchScalarGridSpec(
            num_scalar_prefetch=1, grid=(M // tm,),
            in_specs=[pl.BlockSpec((tm, K), lambda i, c: (i, 0)), pl.BlockSpec((None, K, C), lambda i, c: (c[0], 0, 0))]
            + [_ANY] * len(extra),
            out_specs=pl.BlockSpec((tm, C), lambda i, c: (i, c[0]))),
        out_shape=jax.ShapeDtypeStruct((M, nchip * C), out_dtype),
        input_output_aliases={} if prev is None else {3: 0},
        compiler_params=_params(("arbitrary",)),
    )(blk, a, b4, *extra)


def _mm_merge_out_loss(proj, y_a, y_s, w_out, x, target, *, tm=256):
    rows, d = x.shape
    ncol = d // CW

    def body(*refs):
        ga_refs, gs_refs = refs[:ncol], refs[ncol:2 * ncol]
        ya_ref, ys_ref, w_ref, x_ref, t_ref, m_ref, d_ref, db_ref, sq_ref = refs[2 * ncol:]
        for j in range(ncol):
            cols = slice(j * CW, (j + 1) * CW)
            m_ref[:, cols] = (_sigmoid(ga_refs[j][...].astype(F32)) * ya_ref[:, cols].astype(F32)
                              + _sigmoid(gs_refs[j][...].astype(F32)) * ys_ref[:, cols].astype(F32)).astype(m_ref.dtype)
        mo = jnp.dot(m_ref[...], w_ref[...].astype(_MXU), preferred_element_type=F32)
        err = (x_ref[...] + mo) - t_ref[...]
        dout = err * (1.0 / d)
        d_ref[...] = dout
        db_ref[...] = dout.astype(db_ref.dtype)
        part = _colsum(err * err)
        i = pl.program_id(0)

        @pl.when(i == 0)
        def _():
            sq_ref[...] = part

        @pl.when(i > 0)
        def _():
            sq_ref[...] += part

    tile = pl.BlockSpec((tm, d), lambda i: (i, 0))
    gate = [pl.BlockSpec((tm, CW), lambda i, c=off + j: (i, c)) for off in (OFF_GA, OFF_GS) for j in range(ncol)]
    return pl.pallas_call(
        body, name="mm_merge_out_loss", grid=(rows // tm,),
        in_specs=gate + [tile, tile, pl.BlockSpec((d, d), lambda i: (0, 0), pipeline_mode=pl.Buffered(1)), tile, tile],
        out_specs=[tile, tile, tile, pl.BlockSpec((1, d), lambda i: (0, 0))],
        out_shape=[jax.ShapeDtypeStruct((rows, d), _MXU), jax.ShapeDtypeStruct((rows, d), F32),
                   jax.ShapeDtypeStruct((rows, d), _MXU), jax.ShapeDtypeStruct((1, d), F32)],
        compiler_params=_params(("arbitrary",)),
    )(*([proj] * (2 * ncol)), y_a, y_s, w_out, x, target)


def _mm_merge_bwd(dout_b, w_out, proj, y_a, y_s, *, tm=512):
    rows, d = y_a.shape
    ncol = d // CW

    def body(do_ref, w_ref, *refs):
        ga_refs, gs_refs = refs[:ncol], refs[ncol:2 * ncol]
        ya_ref, ys_ref, dya_ref, dys_ref, dg_ref = refs[2 * ncol:]
        dm = lax.dot_general(do_ref[...].astype(_MXU), w_ref[...].astype(_MXU), _NT, preferred_element_type=F32)
        for j in range(ncol):
            cols = slice(j * CW, (j + 1) * CW)
            dmj = dm[:, cols]
            sa, ss = _sigmoid(ga_refs[j][...].astype(F32)), _sigmoid(gs_refs[j][...].astype(F32))
            dya_ref[:, cols] = (sa * dmj).astype(dya_ref.dtype)
            dys_ref[:, cols] = (ss * dmj).astype(dys_ref.dtype)
            dg_ref[:, cols] = (dmj * ya_ref[:, cols].astype(F32) * sa * (1.0 - sa)).astype(dg_ref.dtype)
            dg_ref[:, d + j * CW:d + (j + 1) * CW] = (dmj * ys_ref[:, cols].astype(F32) * ss
                                                      * (1.0 - ss)).astype(dg_ref.dtype)

    tile = pl.BlockSpec((tm, d), lambda i: (i, 0))
    gate = [pl.BlockSpec((tm, CW), lambda i, c=off + j: (i, c)) for off in (OFF_GA, OFF_GS) for j in range(ncol)]
    both = pl.BlockSpec((pl.Element(tm), pl.Element(2 * d)), lambda i: (i * tm, OFF_GA * CW))
    return pl.pallas_call(
        body, name="mm_merge_bwd", grid=(rows // tm,),
        in_specs=[tile, pl.BlockSpec((d, d), lambda i: (0, 0), pipeline_mode=pl.Buffered(1))] + gate + [tile, tile],
        out_specs=[tile, tile, both],
        out_shape=[jax.ShapeDtypeStruct((rows, d), _MXU)] * 2 + [jax.ShapeDtypeStruct((rows, IN_W), _MXU)],
        compiler_params=_params(("arbitrary",)),
    )(dout_b, w_out, *([proj] * (2 * ncol)), y_a, y_s)


def _mm_glu_gate(yg, w_glu4, b_glu, proj, *, tm=1024):
    rows, k = yg.shape
    nj, _, tn = w_glu4.shape
    w = nj * tn // 2
    tm = min(tm, rows)

    def body(a_ref, w_ref, ba_ref, bb_ref, z0_ref, z1_ref, glu_ref, ys_ref):
        j = pl.program_id(1)
        for c in range(nj):
            @pl.when(j == c)
            def _(c=c):
                glu_ref[:, c * tn:(c + 1) * tn] = jnp.dot(a_ref[...].astype(_MXU), w_ref[...].astype(_MXU),
                                                          preferred_element_type=F32).astype(glu_ref.dtype)

        @pl.when(j == nj - 1)
        def _():
            z = jnp.concatenate([z0_ref[...], z1_ref[...]], axis=1).astype(F32)
            ys_ref[...] = ((glu_ref[:, :w].astype(F32) + ba_ref[...]) * _sigmoid(glu_ref[:, w:].astype(F32) + bb_ref[...])
                           * (z * _sigmoid(z))).astype(ys_ref.dtype)

    bias = lambda c: pl.BlockSpec((1, w), lambda i, j: (0, c))
    zcol = lambda c: pl.BlockSpec((tm, CW), lambda i, j: (i, OFF_Z + c))
    return pl.pallas_call(
        body, name="mm_glu_gate", grid=(rows // tm, nj),
        in_specs=[pl.BlockSpec((tm, k), lambda i, j: (i, 0)), pl.BlockSpec((None, k, tn), lambda i, j: (j, 0, 0)),
                  bias(0), bias(1), zcol(0), zcol(1)],
        out_specs=[pl.BlockSpec((tm, nj * tn), lambda i, j: (i, 0)), pl.BlockSpec((tm, w), lambda i, j: (i, 0))],
        out_shape=[jax.ShapeDtypeStruct((rows, nj * tn), _MXU), jax.ShapeDtypeStruct((rows, w), _MXU)],
        compiler_params=_params(("arbitrary", "arbitrary")),
    )(yg, w_glu4, b_glu, b_glu, proj, proj)


def _mm_ssm_gate_bwd(d_ys, w_sp4, glu, b_glu, proj, d_proj, *, tm=1024):
    rows, w = glu.shape[0], glu.shape[1] // 2
    nk, tk = w_sp4.shape[0], w_sp4.shape[2]
    tm = min(tm, rows)

    def body(dy_ref, w_ref, ga_ref, gb_ref, ba_ref, bb_ref, z0_ref, z1_ref, buf_ref, dg_ref, dz_ref, db_ref, acc):
        i, k = pl.program_id(0), pl.program_id(1)

        @pl.when(k == 0)
        def _():
            acc[...] = jnp.zeros_like(acc)

        acc[...] += lax.dot_general(dy_ref[...].astype(_MXU), w_ref[...].astype(_MXU), _NT, preferred_element_type=F32)

        @pl.when(k == nk - 1)
        def _():
            dv = acc[...]
            a, sb = ga_ref[...].astype(F32) + ba_ref[...], _sigmoid(gb_ref[...].astype(F32) + bb_ref[...])
            f, df = _silu_and_grad(jnp.concatenate([z0_ref[...], z1_ref[...]], axis=1).astype(F32))
            dga = dv * sb * f
            dgb = dv * a * f * sb * (1.0 - sb)
            dg_ref[:, :w] = dga.astype(dg_ref.dtype)
            dg_ref[:, w:] = dgb.astype(dg_ref.dtype)
            dz_ref[...] = (dv * a * sb * df).astype(dz_ref.dtype)
            part = jnp.concatenate([_colsum(dga), _colsum(dgb)], axis=1)

            @pl.when(i == 0)
            def _():
                db_ref[...] = part

            @pl.when(i > 0)
            def _():
                db_ref[...] += part

    half = lambda c: pl.BlockSpec((tm, w), lambda i, k: (i, c))
    bias = lambda c: pl.BlockSpec((1, w), lambda i, k: (0, c))
    zcol = lambda c: pl.BlockSpec((tm, CW), lambda i, k: (i, OFF_Z + c))
    return pl.pallas_call(
        body, name="mm_ssm_gate_bwd", grid=(rows // tm, nk),
        in_specs=[pl.BlockSpec((tm, tk), lambda i, k: (i, k)), pl.BlockSpec((None, w, tk), lambda i, k: (k, 0, 0)),
                  half(0), half(1), bias(0), bias(1), zcol(0), zcol(1), _ANY],
        out_specs=[pl.BlockSpec((tm, 2 * w), lambda i, k: (i, 0)),
                   pl.BlockSpec((pl.Element(tm), pl.Element(w)), lambda i, k: (i * tm, OFF_Z * CW)),
                   pl.BlockSpec((1, 2 * w), lambda i, k: (0, 0))],
        out_shape=[jax.ShapeDtypeStruct((rows, 2 * w), _MXU), jax.ShapeDtypeStruct(d_proj.shape, d_proj.dtype),
                   jax.ShapeDtypeStruct((1, 2 * w), F32)],
        input_output_aliases={8: 1},
        scratch_shapes=[pltpu.VMEM((tm, w), F32)],
        compiler_params=_params(("arbitrary", "arbitrary")),
    )(d_ys, w_sp4, glu, glu, b_glu, b_glu, proj, proj, d_proj)


def _colsum(v):
    return jnp.sum(v, axis=0, keepdims=True)


def _sigmoid(v):
    return jax.nn.sigmoid(v)


def _silu_and_grad(v):
    s = _sigmoid(v)
    return v * s, s * (1.0 + v * (1.0 - s))


def _rms_fwd(x, w, *, tm=512, deps=()):
    rows, d = x.shape
    nd = len(deps)

    def body(x_ref, w_ref, *rest):
        h_ref, r_ref = rest[nd:]
        xv = x_ref[...]
        r = lax.rsqrt(jnp.mean(xv * xv, axis=-1, keepdims=True) + NORM_EPS)
        h_ref[...] = (xv * r * w_ref[...]).astype(h_ref.dtype)
        r_ref[...] = r

    return pl.pallas_call(
        body, name="rms_fwd", grid=(rows // tm,),
        in_specs=[pl.BlockSpec((tm, d), lambda i: (i, 0)), pl.BlockSpec((1, d), lambda i: (0, 0))] + [_ANY] * nd,
        out_specs=[pl.BlockSpec((tm, d), lambda i: (i, 0)), pl.BlockSpec((tm, 1), lambda i: (i, 0))],
        out_shape=[jax.ShapeDtypeStruct((rows, d), _MXU), jax.ShapeDtypeStruct((rows, 1), F32)],
        compiler_params=_params(("arbitrary",)),
    )(x, w, *deps)


def _rms_bwd(dh, x, rstd, w, dout, *, tm=512):
    rows, d = x.shape

    def body(dh_ref, x_ref, r_ref, w_ref, do_ref, gx_ref, gw_ref):
        dhv, xv, r, wv = dh_ref[...], x_ref[...], r_ref[...], w_ref[...]
        xr = xv * r
        t = jnp.mean(dhv * wv * xr, axis=-1, keepdims=True)
        gx_ref[...] = do_ref[...] + r * (wv * dhv - xr * t)
        part = _colsum(dhv * xr)
        i = pl.program_id(0)

        @pl.when(i == 0)
        def _():
            gw_ref[...] = part

        @pl.when(i > 0)
        def _():
            gw_ref[...] += part

    return pl.pallas_call(
        body, name="rms_bwd", grid=(rows // tm,),
        in_specs=[pl.BlockSpec((tm, d), lambda i: (i, 0)), pl.BlockSpec((tm, d), lambda i: (i, 0)),
                  pl.BlockSpec((tm, 1), lambda i: (i, 0)), pl.BlockSpec((1, d), lambda i: (0, 0)),
                  pl.BlockSpec((tm, d), lambda i: (i, 0))],
        out_specs=[pl.BlockSpec((tm, d), lambda i: (i, 0)), pl.BlockSpec((1, d), lambda i: (0, 0))],
        out_shape=[jax.ShapeDtypeStruct((rows, d), F32), jax.ShapeDtypeStruct((1, d), F32)],
        compiler_params=_params(("arbitrary",)),
    )(dh, x, rstd, w, dout)


_NT = (((1,), (1,)), ((), ()))
_TN = (((0,), (0,)), ((), ()))


QKV_W = ATTN_W + 2 * KV_W
HEADS_PER_TILE = LANES // HEAD_DIM


def _low_half(rows):
    return lax.broadcasted_iota(jnp.int32, (rows, LANES), 1) < HEAD_DIM


def _pair_mean(t, low):
    m_lo = jnp.sum(jnp.where(low, t, 0.0), axis=-1, keepdims=True)
    m_hi = jnp.sum(jnp.where(low, 0.0, t), axis=-1, keepdims=True)
    return jnp.where(low, m_lo, m_hi) * (1.0 / HEAD_DIM)


def _pair_rstd(t, low):
    return lax.rsqrt(_pair_mean(t * t, low) + NORM_EPS)


def _dup_half(t, hi, low):
    swapped = pltpu.roll(t, HEAD_DIM, 1)
    return jnp.where(low, swapped, t) if hi else jnp.where(low, t, swapped)


def _fold_halves(t):
    return t + pltpu.roll(t, HEAD_DIM, 1)


def _split_heads(t, low):
    return [jnp.where(low, t, 0.0), jnp.where(low, 0.0, t)]


def _stacked_band_mask(n):
    rows = Q_PER_KV * WINDOW
    qi = lax.broadcasted_iota(jnp.int32, (rows, 2 * WINDOW), 0) % WINDOW + WINDOW
    kj = lax.broadcasted_iota(jnp.int32, (rows, 2 * WINDOW), 1)
    diff = qi - kj
    first_key = jnp.where(n > 0, 0, WINDOW)
    return (diff >= 0) & (diff < WINDOW) & (kj >= first_key)


def _stacked_sinks(sink_ref, g):
    blk = lax.broadcasted_iota(jnp.int32, (Q_PER_KV * WINDOW, 1), 0) // WINDOW
    col = jnp.full((Q_PER_KV * WINDOW, 1), sink_ref[Q_PER_KV * g], F32)
    for r in range(1, Q_PER_KV):
        col = jnp.where(blk == r, sink_ref[Q_PER_KV * g + r], col)
    return col


def _attn_in_specs(nblk, rev):
    def cur(n):
        return (nblk - 1 - n) if rev else n

    q_spec = pl.BlockSpec((WINDOW, ATTN_W), lambda n: (cur(n), 0))
    kvc_spec = pl.BlockSpec((WINDOW, 2 * KV_W), lambda n: (cur(n), ATTN_W // (2 * KV_W)))
    kvp_spec = pl.BlockSpec((WINDOW, 2 * KV_W), lambda n: (jnp.maximum(cur(n) - 1, 0), ATTN_W // (2 * KV_W)))
    w_spec = pl.BlockSpec((1, LANES), lambda n: (0, 0))
    l_spec = pl.BlockSpec((WINDOW, N_Q_HEADS), lambda n: (cur(n), 0))
    gate_specs = [pl.BlockSpec((WINDOW, CW), lambda n, col=OFF_AGATE + j: (cur(n), col)) for j in range(ATTN_W // CW)]
    return q_spec, kvc_spec, kvp_spec, w_spec, l_spec, gate_specs


def _attn2_fwd(proj, qw2, kw2, sinks, deps=()):
    seq = proj.shape[0]
    nblk = seq // WINDOW
    scale = 1.0 / math.sqrt(HEAD_DIM)
    q_spec, kvc_spec, kvp_spec, w_spec, l_spec, gate_specs = _attn_in_specs(nblk, False)
    nd, ng = len(deps), len(gate_specs)

    def body(sink_ref, q_ref, kvc_ref, kvp_ref, qw_ref, kw_ref, *rest):
        gate_refs = rest[:ng]
        o_ref, lse_ref, ya_ref = rest[ng + nd:]
        n = pl.program_id(0)
        low, low2 = _low_half(WINDOW), _low_half(2 * WINDOW)
        valid = _stacked_band_mask(n)
        head_lane = lax.broadcasted_iota(jnp.int32, (WINDOW, N_Q_HEADS), 1)
        kv = jnp.concatenate([kvp_ref[...], kvc_ref[...]], axis=0).astype(F32)
        qwv, kwv = qw_ref[...], kw_ref[...]
        lse_blk = jnp.zeros((WINDOW, N_Q_HEADS), F32)
        for t in range(N_KV_HEADS // HEADS_PER_TILE):
            kt = kv[:, t * LANES:(t + 1) * LANES]
            vt = kv[:, KV_W + t * LANES:KV_W + (t + 1) * LANES]
            kn = kt * _pair_rstd(kt, low2) * kwv
            for hi in range(HEADS_PER_TILE):
                g = HEADS_PER_TILE * t + hi
                kdup = _dup_half(kn, hi, low2).astype(_MXU)
                vdup = _dup_half(vt, hi, low2).astype(_MXU)
                stack = []
                for tq in (2 * g, 2 * g + 1):
                    qt = q_ref[:, tq * LANES:(tq + 1) * LANES].astype(F32)
                    stack += _split_heads(qt * _pair_rstd(qt, low) * qwv, low)
                qs = jnp.concatenate(stack, axis=0).astype(_MXU)
                s = lax.dot_general(qs, kdup, _NT, preferred_element_type=F32) * scale
                s = jnp.where(valid, s, -1e30)
                sink = _stacked_sinks(sink_ref, g)
                m = jnp.maximum(jnp.max(s, axis=-1, keepdims=True), sink)
                e = jnp.exp(s - m)
                z = jnp.sum(e, axis=-1, keepdims=True) + jnp.exp(sink - m)
                o = jnp.dot((e / z).astype(_MXU), vdup, preferred_element_type=F32)
                for i, tq in enumerate((2 * g, 2 * g + 1)):
                    tile = slice(tq * LANES, (tq + 1) * LANES)
                    out = jnp.where(low, o[2 * i * WINDOW:(2 * i + 1) * WINDOW],
                                    o[(2 * i + 1) * WINDOW:(2 * i + 2) * WINDOW])
                    gate = gate_refs[tq * LANES // CW][:, tq * LANES % CW:tq * LANES % CW + LANES].astype(F32)
                    o_ref[:, tile] = out.astype(o_ref.dtype)
                    ya_ref[:, tile] = (out * (gate * _sigmoid(gate))).astype(ya_ref.dtype)
                lse = m + jnp.log(z)
                for r in range(Q_PER_KV):
                    lse_blk = jnp.where(head_lane == Q_PER_KV * g + r, lse[r * WINDOW:(r + 1) * WINDOW], lse_blk)
        lse_ref[...] = lse_blk

    return pl.pallas_call(
        body, name="attn_fwd", grid=(nblk,),
        in_specs=[pl.BlockSpec(memory_space=pltpu.SMEM), q_spec, kvc_spec, kvp_spec, w_spec, w_spec] + gate_specs
        + [_ANY] * nd,
        out_specs=[q_spec, l_spec, q_spec],
        out_shape=[jax.ShapeDtypeStruct((seq, ATTN_W), _MXU), jax.ShapeDtypeStruct((seq, N_Q_HEADS), F32),
                   jax.ShapeDtypeStruct((seq, ATTN_W), _MXU)],
        compiler_params=_params(("arbitrary",)),
    )(sinks, proj, proj, proj, qw2, kw2, *([proj] * ng), *deps)


def _attn2_bwd(proj, qw2, kw2, sinks, lse, attn, dya, d_proj, deps=()):
    seq = proj.shape[0]
    nblk = seq // WINDOW
    scale = 1.0 / math.sqrt(HEAD_DIM)
    q_spec, kvc_spec, kvp_spec, w_spec, l_spec, gate_specs = _attn_in_specs(nblk, True)
    s_spec = pl.BlockSpec((1, N_Q_HEADS), lambda n: (0, 0))
    d_spec = pl.BlockSpec((WINDOW, QKV_W + ATTN_W), lambda n: (nblk - 1 - n, 0))
    deps = list(deps) + [d_proj]
    nd, ng = len(deps), len(gate_specs)

    def body(sink_ref, q_ref, kvc_ref, kvp_ref, qw_ref, kw_ref, lse_ref, attn_ref, dya_ref, *rest):
        gate_refs = rest[:ng]
        d_ref, dqw_ref, dkw_ref, dsk_ref, carry, do_ref = rest[ng + nd:]
        step = pl.program_id(0)
        n = nblk - 1 - step

        @pl.when(step == 0)
        def _():
            carry[...] = jnp.zeros_like(carry)
            dqw_ref[...] = jnp.zeros_like(dqw_ref)
            dkw_ref[...] = jnp.zeros_like(dkw_ref)
            dsk_ref[...] = jnp.zeros_like(dsk_ref)

        for j, g_ref in enumerate(gate_refs):
            cols = slice(j * CW, (j + 1) * CW)
            f, df = _silu_and_grad(g_ref[...].astype(F32))
            dv = dya_ref[:, cols]
            do_ref[:, cols] = dv * f
            d_ref[:, QKV_W + j * CW:QKV_W + (j + 1) * CW] = (dv * attn_ref[:, cols].astype(F32) * df).astype(d_ref.dtype)

        low, low2 = _low_half(WINDOW), _low_half(2 * WINDOW)
        valid = _stacked_band_mask(n)
        head_lane = lax.broadcasted_iota(jnp.int32, (WINDOW, N_Q_HEADS), 1)
        sink_lane = lax.broadcasted_iota(jnp.int32, (1, N_Q_HEADS), 1)
        kv = jnp.concatenate([kvp_ref[...], kvc_ref[...]], axis=0).astype(F32)
        qwv, kwv = qw_ref[...], kw_ref[...]
        lse_blk = lse_ref[...]
        dqw = jnp.zeros((1, LANES), F32)
        dkw = jnp.zeros((1, LANES), F32)
        dsk = jnp.zeros((1, N_Q_HEADS), F32)
        for t in range(N_KV_HEADS // HEADS_PER_TILE):
            kt = kv[:, t * LANES:(t + 1) * LANES]
            vt = kv[:, KV_W + t * LANES:KV_W + (t + 1) * LANES]
            rk = _pair_rstd(kt, low2)
            kn = kt * rk * kwv
            dkn_t = jnp.zeros((2 * WINDOW, LANES), F32)
            dv_t = jnp.zeros((2 * WINDOW, LANES), F32)
            for hi in range(HEADS_PER_TILE):
                g = HEADS_PER_TILE * t + hi
                kdup = _dup_half(kn, hi, low2).astype(_MXU)
                vdup = _dup_half(vt, hi, low2).astype(_MXU)
                tiles = (2 * g, 2 * g + 1)
                qx, rq, stack, dstack, lse_rows = [], [], [], [], []
                for tq in tiles:
                    qt = q_ref[:, tq * LANES:(tq + 1) * LANES].astype(F32)
                    r = _pair_rstd(qt, low)
                    rq.append(r)
                    qx.append(qt * r)
                    stack += _split_heads(qx[-1] * qwv, low)
                    dstack += _split_heads(do_ref[:, tq * LANES:(tq + 1) * LANES], low)
                for r in range(Q_PER_KV):
                    lse_rows.append(jnp.sum(jnp.where(head_lane == Q_PER_KV * g + r, lse_blk, 0.0), axis=-1, keepdims=True))
                qs = jnp.concatenate(stack, axis=0).astype(_MXU)
                dos = jnp.concatenate(dstack, axis=0).astype(_MXU)
                lse_col = jnp.concatenate(lse_rows, axis=0)
                s = lax.dot_general(qs, kdup, _NT, preferred_element_type=F32) * scale
                s = jnp.where(valid, s, -1e30)
                p = jnp.exp(s - lse_col)
                dp = lax.dot_general(dos, vdup, _NT, preferred_element_type=F32)
                dsum = jnp.sum(p * dp, axis=-1, keepdims=True)
                ds = (p * (dp - dsum) * scale).astype(_MXU)
                dsink = -jnp.exp(_stacked_sinks(sink_ref, g) - lse_col) * dsum
                for r in range(Q_PER_KV):
                    dsk = dsk + jnp.where(sink_lane == Q_PER_KV * g + r, _colsum(dsink[r * WINDOW:(r + 1) * WINDOW]), 0.0)
                dv_g = _fold_halves(lax.dot_general(p.astype(_MXU), dos, _TN, preferred_element_type=F32))
                dkn_g = _fold_halves(lax.dot_general(ds, qs, _TN, preferred_element_type=F32))
                dv_t = jnp.where(low2, dv_t, dv_g) if hi else jnp.where(low2, dv_g, dv_t)
                dkn_t = jnp.where(low2, dkn_t, dkn_g) if hi else jnp.where(low2, dkn_g, dkn_t)
                dqn = jnp.dot(ds, kdup, preferred_element_type=F32)
                for i, tq in enumerate(tiles):
                    dqn_t = jnp.where(low, dqn[2 * i * WINDOW:(2 * i + 1) * WINDOW],
                                      dqn[(2 * i + 1) * WINDOW:(2 * i + 2) * WINDOW])
                    dq = rq[i] * (qwv * dqn_t - qx[i] * _pair_mean(dqn_t * qwv * qx[i], low))
                    d_ref[:, tq * LANES:(tq + 1) * LANES] = dq.astype(d_ref.dtype)
                    dqw = dqw + _colsum(dqn_t * qx[i])
            k_cols = slice(t * LANES, (t + 1) * LANES)
            v_cols = slice(KV_W + t * LANES, KV_W + (t + 1) * LANES)
            dkn_c = dkn_t[WINDOW:] + carry[:, k_cols]
            rc = rk[WINDOW:]
            kx = kt[WINDOW:] * rc
            dk = rc * (kwv * dkn_c - kx * _pair_mean(dkn_c * kwv * kx, low))
            d_ref[:, ATTN_W + t * LANES:ATTN_W + (t + 1) * LANES] = dk.astype(d_ref.dtype)
            d_ref[:, ATTN_W + KV_W + t * LANES:ATTN_W + KV_W + (t + 1) * LANES] = (
                dv_t[WINDOW:] + carry[:, v_cols]).astype(d_ref.dtype)
            carry[:, k_cols] = dkn_t[:WINDOW]
            carry[:, v_cols] = dv_t[:WINDOW]
            dkw = dkw + _colsum(dkn_c * kx)
        dqw_ref[...] += dqw
        dkw_ref[...] += dkw
        dsk_ref[...] += dsk

    return pl.pallas_call(
        body, name="attn_bwd", grid=(nblk,),
        in_specs=[pl.BlockSpec(memory_space=pltpu.SMEM), q_spec, kvc_spec, kvp_spec, w_spec, w_spec, l_spec, q_spec,
                  q_spec] + gate_specs + [_ANY] * nd,
        out_specs=[d_spec, w_spec, w_spec, s_spec],
        out_shape=[jax.ShapeDtypeStruct(d_proj.shape, d_proj.dtype), jax.ShapeDtypeStruct((1, LANES), F32),
                   jax.ShapeDtypeStruct((1, LANES), F32), jax.ShapeDtypeStruct((1, N_Q_HEADS), F32)],
        input_output_aliases={9 + ng + nd - 1: 0},
        scratch_shapes=[pltpu.VMEM((WINDOW, 2 * KV_W), F32), pltpu.VMEM((WINDOW, ATTN_W), F32)],
        compiler_params=_params(("arbitrary",)),
    )(sinks, proj, proj, proj, qw2, kw2, lse, attn, dya, *([proj] * ng), *deps)


def _ssm_discretise(a_re, a_im, log_dt):
    dt = jnp.exp(log_dt)
    mag = jnp.exp(dt * a_re)
    ab_re = mag * jnp.cos(dt * a_im)
    ab_im = mag * jnp.sin(dt * a_im)
    num_re = ab_re - 1.0
    num_im = ab_im
    den = a_re * a_re + a_im * a_im
    cf_re = (num_re * a_re + num_im * a_im) / den
    cf_im = (num_im * a_re - num_re * a_im) / den
    return ab_re, ab_im, cf_re, cf_im


def _ssm_params_fwd(a_re, a_im, log_dt):
    shp = jax.ShapeDtypeStruct(a_re.shape, F32)

    def body(are_ref, aim_ref, ldt_ref, abr_ref, abi_ref, cfr_ref, cfi_ref, alr_ref, ali_ref):
        abr, abi, cfr, cfi = _ssm_discretise(are_ref[...], aim_ref[...], ldt_ref[...])
        abr_ref[...], abi_ref[...], cfr_ref[...], cfi_ref[...] = abr, abi, cfr, cfi
        pr, pi = abr, abi
        for _ in range(int(math.log2(SSM_L))):
            pr, pi = pr * pr - pi * pi, 2.0 * pr * pi
        alr_ref[...], ali_ref[...] = pr, pi

    return pl.pallas_call(body, name="ssm_params_fwd", out_shape=[shp] * 6)(a_re, a_im, log_dt)


def _ssm_params_bwd(a_re, a_im, log_dt, d_abr, d_abi, d_cfr, d_cfi):
    def body(are_ref, aim_ref, ldt_ref, g0, g1, g2, g3, dare_ref, daim_ref, dldt_ref):
        _, vjp = jax.vjp(_ssm_discretise, are_ref[...], aim_ref[...], ldt_ref[...])
        dare_ref[...], daim_ref[...], dldt_ref[...] = vjp((g0[...], g1[...], g2[...], g3[...]))

    return pl.pallas_call(
        body, name="ssm_params_bwd",
        out_shape=[jax.ShapeDtypeStruct(a_re.shape, F32), jax.ShapeDtypeStruct(a_im.shape, F32),
                   jax.ShapeDtypeStruct(log_dt.shape, F32)],
    )(a_re, a_im, log_dt, d_abr, d_abi, d_cfr, d_cfi)


def _scan_cols(j):
    return pl.ds(j * SSM_SB, SSM_SB)


def _rows8(r):
    return pl.ds(pl.multiple_of(r * SUBLANES, SUBLANES), SUBLANES)


def _bcast8(row):
    return jnp.broadcast_to(row, (SUBLANES, row.shape[-1]))


def _token_order_pick():
    tok = lax.broadcasted_iota(jnp.int32, (SSM_T, SSM_T), 0)
    row = lax.broadcasted_iota(jnp.int32, (SSM_T, SSM_T), 1)
    return (row == SUBLANES * (tok % SSM_L) + tok // SSM_L).astype(_MXU)


SCAN_UNROLL = 16


def _scan_loop(n, step, init):
    def trip(o, carry):
        for i in range(SCAN_UNROLL):
            carry = step(o * SCAN_UNROLL + i, carry)
        return carry

    return lax.fori_loop(0, n // SCAN_UNROLL, trip, init)


def _ssm_fwd(u, b_re, b_im, c_re, c_im, d_skip, coef):
    seq = u.shape[0]
    nc = seq // SSM_T
    T, L = SSM_T, SSM_L

    def body(u_ref, bre_ref, bim_ref, cre_ref, cim_ref, d_ref, are_ref, aim_ref, cfr_ref, cfi_ref, alr_ref, ali_ref,
             y_ref, yg_ref, sre_ref, sim_ref, ire_ref, iim_ref, car_re, car_im, end_re, end_im, yg_scan):
        c = pl.program_id(0)

        @pl.when(c == 0)
        def _():
            car_re[...] = jnp.zeros_like(car_re)
            car_im[...] = jnp.zeros_like(car_im)

        for j in range(SSM_JB):
            ub = u_ref[:, j * LANES:(j + 1) * LANES].astype(_MXU)
            bur = jnp.dot(ub, bre_ref[j], preferred_element_type=F32)
            bui = jnp.dot(ub, bim_ref[j], preferred_element_type=F32)
            cfr, cfi = cfr_ref[:, _scan_cols(j)], cfi_ref[:, _scan_cols(j)]
            sre_ref[:, _scan_cols(j)] = cfr * bur - cfi * bui
            sim_ref[:, _scan_cols(j)] = cfr * bui + cfi * bur

        for j in range(SSM_JB):
            cols = _scan_cols(j)
            ar, ai = _bcast8(are_ref[:, cols]), _bcast8(aim_ref[:, cols])

            def step1(r, s, cols=cols, ar=ar, ai=ai):
                sr, si = s
                rows = _rows8(r)
                return (ar * sr - ai * si + sre_ref[rows, cols], ar * si + ai * sr + sim_ref[rows, cols])

            zero = jnp.zeros((SUBLANES, SSM_SB), F32)
            er, ei = _scan_loop(L, step1, (zero, zero))
            end_re[:, cols] = er
            end_im[:, cols] = ei

        alr, ali = alr_ref[...], ali_ref[...]
        cr, ci = car_re[...], car_im[...]
        ire_ref[0:1, :] = cr
        iim_ref[0:1, :] = ci
        for i in range(1, SUBLANES):
            er, ei = end_re[i - 1:i, :], end_im[i - 1:i, :]
            cr, ci = alr * cr - ali * ci + er, alr * ci + ali * cr + ei
            ire_ref[i:i + 1, :] = cr
            iim_ref[i:i + 1, :] = ci

        for j in range(SSM_JB):
            cols = _scan_cols(j)
            ar, ai = _bcast8(are_ref[:, cols]), _bcast8(aim_ref[:, cols])

            def step2(r, s, cols=cols, ar=ar, ai=ai):
                sr, si = s
                rows = _rows8(r)
                nr = ar * sr - ai * si + sre_ref[rows, cols]
                ni = ar * si + ai * sr + sim_ref[rows, cols]
                sre_ref[rows, cols] = nr
                sim_ref[rows, cols] = ni
                return nr, ni

            _scan_loop(L, step2, (ire_ref[:, cols], iim_ref[:, cols]))

        car_re[...] = sre_ref[T - 1:T, :]
        car_im[...] = sim_ref[T - 1:T, :]

        for j in range(SSM_JB):
            cols = _scan_cols(j)
            ch = slice(j * LANES, (j + 1) * LANES)
            y = (jnp.dot(sre_ref[:, cols].astype(_MXU), cre_ref[j], preferred_element_type=F32)
                 - jnp.dot(sim_ref[:, cols].astype(_MXU), cim_ref[j], preferred_element_type=F32))
            y = y + d_ref[:, ch] * u_ref[:, ch].astype(F32)
            y_ref[:, ch] = y
            yg_scan[:, ch] = jax.nn.gelu(y).astype(yg_scan.dtype)
        yg_ref[...] = jnp.dot(_token_order_pick(), yg_scan[...], preferred_element_type=F32).astype(yg_ref.dtype)

    tok = pl.BlockSpec((T, SSM_W), lambda c: (c, 0))
    st = pl.BlockSpec((T, N_STATES), lambda c: (c, 0))
    ini = pl.BlockSpec((None, SUBLANES, N_STATES), lambda c: (c, 0, 0))
    bsp = pl.BlockSpec((SSM_JB, LANES, SSM_SB), lambda c: (0, 0, 0))
    csp = pl.BlockSpec((SSM_JB, SSM_SB, LANES), lambda c: (0, 0, 0))
    row_w = pl.BlockSpec((1, SSM_W), lambda c: (0, 0))
    row_s = pl.BlockSpec((1, N_STATES), lambda c: (0, 0))
    return pl.pallas_call(
        body, name="ssm_fwd", grid=(nc,),
        in_specs=[tok, bsp, bsp, csp, csp, row_w] + [row_s] * 6,
        out_specs=[tok, tok, st, st, ini, ini],
        out_shape=[jax.ShapeDtypeStruct((seq, SSM_W), F32), jax.ShapeDtypeStruct((seq, SSM_W), _MXU),
                   jax.ShapeDtypeStruct((seq, N_STATES), F32), jax.ShapeDtypeStruct((seq, N_STATES), F32),
                   jax.ShapeDtypeStruct((nc, SUBLANES, N_STATES), F32),
                   jax.ShapeDtypeStruct((nc, SUBLANES, N_STATES), F32)],
        scratch_shapes=[pltpu.VMEM((1, N_STATES), F32), pltpu.VMEM((1, N_STATES), F32),
                        pltpu.VMEM((SUBLANES, N_STATES), F32), pltpu.VMEM((SUBLANES, N_STATES), F32),
                        pltpu.VMEM((T, SSM_W), _MXU)],
        compiler_params=_params(("arbitrary",)),
    )(u, b_re, b_im, c_re, c_im, d_skip, *coef)


def _ssm_bwd(dyg, y, u, s_re, s_im, i_re, i_im, b_re, b_im, c_re, c_im, d_skip, coef, d_proj, deps=()):
    seq = u.shape[0]
    nc = seq // SSM_T
    T, L = SSM_T, SSM_L
    deps = list(deps) + [d_proj]

    def body(dyg_ref, y_ref, u_ref, sre_ref, sim_ref, ire_ref, iim_ref, bre_ref, bim_ref, cre_ref, cim_ref, d_ref,
             are_ref, aim_ref, cfr_ref, cfi_ref, alr_ref, ali_ref, *rest):
        (du_ref, dbre_out, dbim_out, dcre_out, dcim_out, dd_ref, dar_ref, dai_ref, dcfr_ref, dcfi_ref,
         lre, lim, car_re, car_im, end_re, end_im, ini_re, ini_im, dbre_ref, dbim_ref, dcre_ref, dcim_ref,
         dy_ref, du_scan) = rest[len(deps):]
        step = pl.program_id(0)
        dy_ref[...] = jax.vjp(jax.nn.gelu, y_ref[...])[1](dyg_ref[...])[0]

        @pl.when(step == 0)
        def _():
            car_re[...] = jnp.zeros_like(car_re)
            car_im[...] = jnp.zeros_like(car_im)
            for ref in (dbre_ref, dbim_ref, dcre_ref, dcim_ref, dd_ref, dar_ref, dai_ref, dcfr_ref, dcfi_ref):
                ref[...] = jnp.zeros_like(ref)

        for j in range(SSM_JB):
            dyb = dy_ref[:, j * LANES:(j + 1) * LANES].astype(_MXU)
            lre[:, _scan_cols(j)] = lax.dot_general(dyb, cre_ref[j], _NT, preferred_element_type=F32)
            lim[:, _scan_cols(j)] = -lax.dot_general(dyb, cim_ref[j], _NT, preferred_element_type=F32)

        for j in range(SSM_JB):
            cols = _scan_cols(j)
            ar, ai = _bcast8(are_ref[:, cols]), _bcast8(aim_ref[:, cols])

            def step1(t, s, cols=cols, ar=ar, ai=ai):
                sr, si = s
                rows = _rows8(L - 1 - t)
                return (ar * sr + ai * si + lre[rows, cols], ar * si - ai * sr + lim[rows, cols])

            zero = jnp.zeros((SUBLANES, SSM_SB), F32)
            er, ei = _scan_loop(L, step1, (zero, zero))
            end_re[:, cols] = er
            end_im[:, cols] = ei

        alr, ali = alr_ref[...], ali_ref[...]
        cr, ci = car_re[...], car_im[...]
        ini_re[SUBLANES - 1:SUBLANES, :] = cr
        ini_im[SUBLANES - 1:SUBLANES, :] = ci
        for i in range(SUBLANES - 2, -1, -1):
            er, ei = end_re[i + 1:i + 2, :], end_im[i + 1:i + 2, :]
            cr, ci = alr * cr + ali * ci + er, alr * ci - ali * cr + ei
            ini_re[i:i + 1, :] = cr
            ini_im[i:i + 1, :] = ci

        for j in range(SSM_JB):
            cols = _scan_cols(j)
            ar, ai = _bcast8(are_ref[:, cols]), _bcast8(aim_ref[:, cols])

            def step2(t, s, cols=cols, ar=ar, ai=ai):
                sr, si = s
                rows = _rows8(L - 1 - t)
                nr = ar * sr + ai * si + lre[rows, cols]
                ni = ar * si - ai * sr + lim[rows, cols]
                lre[rows, cols] = nr
                lim[rows, cols] = ni
                return nr, ni

            _scan_loop(L, step2, (ini_re[:, cols], ini_im[:, cols]))

        car_re[...] = lre[0:1, :]
        car_im[...] = lim[0:1, :]

        head, tail, body_rows = slice(0, SUBLANES), slice(SUBLANES, T), slice(0, T - SUBLANES)
        for j in range(SSM_JB):
            cols = _scan_cols(j)
            ch = slice(j * LANES, (j + 1) * LANES)
            lr, li = lre[:, cols], lim[:, cols]
            lt_r, lt_i, sp_r, sp_i = lre[tail, cols], lim[tail, cols], sre_ref[body_rows, cols], sim_ref[body_rows, cols]
            lh_r, lh_i, si_r, si_i = lre[head, cols], lim[head, cols], ire_ref[:, cols], iim_ref[:, cols]
            dar_ref[:, cols] += _colsum(lt_r * sp_r + lt_i * sp_i) + _colsum(lh_r * si_r + lh_i * si_i)
            dai_ref[:, cols] += _colsum(lt_i * sp_r - lt_r * sp_i) + _colsum(lh_i * si_r - lh_r * si_i)
            ub = u_ref[:, ch].astype(_MXU)
            uf = ub.astype(F32)
            bur = jnp.dot(ub, bre_ref[j], preferred_element_type=F32)
            bui = jnp.dot(ub, bim_ref[j], preferred_element_type=F32)
            dcfr_ref[:, cols] += _colsum(lr * bur + li * bui)
            dcfi_ref[:, cols] += _colsum(li * bur - lr * bui)
            cfr, cfi = cfr_ref[:, cols], cfi_ref[:, cols]
            dbur = (cfr * lr + cfi * li).astype(_MXU)
            dbui = (cfr * li - cfi * lr).astype(_MXU)
            dyf = dy_ref[:, ch]
            dyb = dyf.astype(_MXU)
            du = (lax.dot_general(dbur, bre_ref[j], _NT, preferred_element_type=F32)
                  + lax.dot_general(dbui, bim_ref[j], _NT, preferred_element_type=F32) + d_ref[:, ch] * dyf)
            du_scan[:, ch] = du.astype(du_scan.dtype)
            dbre_ref[j] += lax.dot_general(ub, dbur, _TN, preferred_element_type=F32)
            dbim_ref[j] += lax.dot_general(ub, dbui, _TN, preferred_element_type=F32)
            dcre_ref[j] += lax.dot_general(sre_ref[:, cols].astype(_MXU), dyb, _TN, preferred_element_type=F32)
            dcim_ref[j] -= lax.dot_general(sim_ref[:, cols].astype(_MXU), dyb, _TN, preferred_element_type=F32)
            dd_ref[:, ch] += _colsum(dyf * uf)
        du_ref[...] = jnp.dot(_token_order_pick(), du_scan[...], preferred_element_type=F32).astype(du_ref.dtype)

        @pl.when(step == nc - 1)
        def _():
            for acc, out in ((dbre_ref, dbre_out), (dbim_ref, dbim_out), (dcre_ref, dcre_out), (dcim_ref, dcim_out)):
                pltpu.sync_copy(acc, out)

    tok = pl.BlockSpec((T, SSM_W), lambda c: (nc - 1 - c, 0))
    st = pl.BlockSpec((T, N_STATES), lambda c: (nc - 1 - c, 0))
    ini = pl.BlockSpec((None, SUBLANES, N_STATES), lambda c: (nc - 1 - c, 0, 0))
    bsp = pl.BlockSpec((SSM_JB, LANES, SSM_SB), lambda c: (0, 0, 0))
    csp = pl.BlockSpec((SSM_JB, SSM_SB, LANES), lambda c: (0, 0, 0))
    row_w = pl.BlockSpec((1, SSM_W), lambda c: (0, 0))
    row_s = pl.BlockSpec((1, N_STATES), lambda c: (0, 0))
    big = pltpu.VMEM((T, N_STATES), F32)
    one = pltpu.VMEM((1, N_STATES), F32)
    eight = pltpu.VMEM((SUBLANES, N_STATES), F32)
    return pl.pallas_call(
        body, name="ssm_bwd", grid=(nc,),
        in_specs=[tok, tok, tok, st, st, ini, ini, bsp, bsp, csp, csp, row_w] + [row_s] * 6 + [_ANY] * len(deps),
        out_specs=[pl.BlockSpec((pl.Element(T), pl.Element(SSM_W)), lambda c: ((nc - 1 - c) * T, OFF_U * CW)),
                   _ANY, _ANY, _ANY, _ANY, row_w, row_s, row_s, row_s, row_s],
        input_output_aliases={18 + len(deps) - 1: 0},
        out_shape=[jax.ShapeDtypeStruct(d_proj.shape, d_proj.dtype),
                   jax.ShapeDtypeStruct((SSM_JB, LANES, SSM_SB), F32), jax.ShapeDtypeStruct((SSM_JB, LANES, SSM_SB), F32),
                   jax.ShapeDtypeStruct((SSM_JB, SSM_SB, LANES), F32), jax.ShapeDtypeStruct((SSM_JB, SSM_SB, LANES), F32),
                   jax.ShapeDtypeStruct((1, SSM_W), F32)] + [jax.ShapeDtypeStruct((1, N_STATES), F32)] * 4,
        scratch_shapes=[big, big, one, one, eight, eight, eight, eight,
                        pltpu.VMEM((SSM_JB, LANES, SSM_SB), F32), pltpu.VMEM((SSM_JB, LANES, SSM_SB), F32),
                        pltpu.VMEM((SSM_JB, SSM_SB, LANES), F32), pltpu.VMEM((SSM_JB, SSM_SB, LANES), F32),
                        pltpu.VMEM((T, SSM_W), F32), pltpu.VMEM((T, SSM_W), _MXU)],
        compiler_params=_params(("arbitrary",)),
    )(dyg, y, u, s_re, s_im, i_re, i_im, b_re, b_im, c_re, c_im, d_skip, *coef, *deps)


def _block_diag_b(b):
    t = b.reshape(SSM_JB, 8, STATE, GROUP).transpose(0, 1, 3, 2)
    eye = jnp.eye(8, dtype=b.dtype)
    return (t[:, :, :, None, :] * eye[None, :, None, :, None]).reshape(SSM_JB, LANES, SSM_SB)


def _block_diag_c(c):
    t = c.reshape(SSM_JB, 8, GROUP, STATE).transpose(0, 1, 3, 2)
    eye = jnp.eye(8, dtype=c.dtype)
    return (t[:, :, :, None, :] * eye[None, :, None, :, None]).reshape(SSM_JB, SSM_SB, LANES)


def _diag_of_b(blk):
    t = blk.reshape(SSM_JB, 8, GROUP, 8, STATE)
    d = jnp.sum(t * jnp.eye(8, dtype=blk.dtype)[None, :, None, :, None], axis=3)
    return d.transpose(0, 1, 3, 2).reshape(N_GROUPS, STATE, GROUP)


def _diag_of_c(blk):
    t = blk.reshape(SSM_JB, 8, STATE, 8, GROUP)
    d = jnp.sum(t * jnp.eye(8, dtype=blk.dtype)[None, :, None, :, None], axis=3)
    return d.transpose(0, 1, 3, 2).reshape(N_GROUPS, GROUP, STATE)


def _to_scan_order(v):
    seq, w = v.shape
    return v.reshape(seq // SSM_T, SUBLANES, SSM_L, w).transpose(0, 2, 1, 3).reshape(seq, w)


def _adamw_math(w, g, m, v):
    nm = ADAM_B1 * m + (1.0 - ADAM_B1) * g
    nv = ADAM_B2 * v + (1.0 - ADAM_B2) * jnp.square(g)
    m_hat = nm / (1.0 - ADAM_B1 ** ADAM_STEP)
    v_hat = nv / (1.0 - ADAM_B2 ** ADAM_STEP)
    return -ADAM_LR * (m_hat / (jnp.sqrt(v_hat) + ADAM_EPS) + ADAM_WD * w), nm, nv


def _adamw(w, g, m, v, *, name, tm, deps=()):
    rows, cols = w.shape
    nd = len(deps)

    def body(w_ref, g_ref, m_ref, v_ref, *rest):
        d_ref, nm_ref, nv_ref, g_out_ref = rest[nd:]
        g = g_ref[...]
        d_ref[...], nm_ref[...], nv_ref[...] = _adamw_math(w_ref[...], g, m_ref[...], v_ref[...])
        g_out_ref[...] = g

    spec = pl.BlockSpec((tm, cols), lambda i: (i, 0))
    shp = jax.ShapeDtypeStruct((rows, cols), F32)
    return pl.pallas_call(body, name=name, grid=(rows // tm,), in_specs=[spec] * 4 + [_ANY] * nd,
                          out_specs=[spec] * 4, out_shape=[shp] * 4,
                          compiler_params=_params(("arbitrary",)))(w, g, m, v, *deps)


def _place():
    x, y, c = lax.axis_index("x"), lax.axis_index("y"), lax.axis_index("c")
    chips = [(1 - x, y), (x, 1 - y), (1 - x, 1 - y)]
    return x, y, c, chips


def _remote(src, dst, send_sem, recv_sem, dev):
    return pltpu.make_async_remote_copy(src_ref=src, dst_ref=dst, send_sem=send_sem, recv_sem=recv_sem,
                                        device_id=dev, device_id_type=MESH)


def _place_shard(w, mine_arr, *, name, tm=256, deps=()):
    rows, cols = w.shape

    def body(m_ref, w_ref, *rest):
        rest[-1][...] = w_ref[...].astype(rest[-1].dtype)

    return pl.pallas_call(
        body, name=name,
        grid_spec=pltpu.PrefetchScalarGridSpec(
            num_scalar_prefetch=1, grid=(rows // tm,),
            in_specs=[pl.BlockSpec((tm, cols), lambda i, m: (i, 0))] + [_ANY] * len(deps),
            out_specs=pl.BlockSpec((None, tm, cols), lambda i, m: (m[0], i, 0))),
        out_shape=jax.ShapeDtypeStruct((N_CHIPS, rows, cols), _WIRE),
        compiler_params=_params(("arbitrary",)),
    )(mine_arr, w, *deps)


_HBM = pl.BlockSpec(memory_space=pltpu.HBM)
_SEM = pl.BlockSpec(memory_space=pltpu.SEMAPHORE)
_EFFECT = pltpu.SideEffectType.DATAFLOW_SIDE_EFFECTING


def _copies_start(name, bufs, plan, count, after=()):
    nb, na = len(bufs), len(after)

    def body(*refs):
        send_sems, recv_sems, token = refs[nb + na], refs[nb + na + 1], refs[-1]
        copies = plan(refs[:nb])
        assert len(copies) == count
        for i, (src, dst, dev, _) in enumerate(copies):
            _remote(src, dst, send_sems.at[i], recv_sems.at[i], dev).start()
        token[...] = jnp.zeros_like(token)

    res = pl.pallas_call(
        body, name=name, in_specs=[_HBM] * nb + [_ANY] * na,
        out_specs=(_SEM, _SEM, *[_HBM] * nb, pl.BlockSpec(memory_space=pltpu.VMEM)),
        out_shape=(pltpu.SemaphoreType.DMA((count,)), pltpu.SemaphoreType.DMA((count,)),
                   *[pltpu.HBM(b.shape, b.dtype) for b in bufs], jax.ShapeDtypeStruct((SUBLANES, LANES), F32)),
        input_output_aliases={i: 2 + i for i in range(nb)},
        compiler_params=pltpu.CompilerParams(has_side_effects=_EFFECT),
    )(*[pltpu.with_memory_space_constraint(b, pltpu.HBM) for b in bufs], *after)
    return (res[0], res[1]), list(res[2:2 + nb]), res[-1]


def _copies_wait(name, bufs, sems, plan, after=(), which=None):
    nb, na = len(bufs), len(after)

    def body(*refs):
        send_sems, recv_sems = refs[nb], refs[nb + 1]
        for i, (src, _, dev, land) in enumerate(plan(refs[:nb])):
            if which is not None and i not in which:
                continue
            cp = _remote(src, land, send_sems.at[i], recv_sems.at[i], dev)
            cp.wait_send()
            cp.wait_recv()

    res = pl.pallas_call(
        body, name=name, in_specs=[_HBM] * nb + [_SEM, _SEM] + [_ANY] * na, out_specs=[_HBM] * nb,
        out_shape=[pltpu.HBM(b.shape, b.dtype) for b in bufs],
        input_output_aliases={i: i for i in range(nb)},
        compiler_params=pltpu.CompilerParams(has_side_effects=_EFFECT),
    )(*bufs, *sems, *after)
    return list(res)


def _plan_gather_ici(fulls, which=(0, 1, 2)):
    x, y, c, chips = _place()
    copies = []
    for f in fulls:
        half = pl.ds(c * (f.shape[1] // 2), f.shape[1] // 2)
        own = f.at[2 * x + y, half]
        for chip in [chips[k] for k in which]:
            copies.append((own, own, (*chip, c), f.at[2 * chip[0] + chip[1], half]))
    return copies


def _plan_gather_d2d(fulls, which=(0, 1, 2)):
    x, y, c, chips = _place()
    copies = []
    for f in fulls:
        r2 = f.shape[1] // 2
        for chip in [chips[k] for k in which]:
            blk = 2 * chip[0] + chip[1]
            landed = f.at[blk, pl.ds(c * r2, r2)]
            copies.append((landed, landed, (x, y, 1 - c), f.at[blk, pl.ds((1 - c) * r2, r2)]))
    return copies


def _plan_relay_direct(fulls):
    (f,) = fulls
    x, y, c, chips = _place()
    half = pl.ds(c * (f.shape[1] // 2), f.shape[1] // 2)
    own = f.at[2 * x + y, half]
    return [(own, own, (*chip, c), f.at[2 * chip[0] + chip[1], half]) for chip in chips[:2]]


def _plan_relay_forward(fulls, k):
    (f,) = fulls
    x, y, c, chips = _place()
    r2 = f.shape[1] // 2
    half, other = pl.ds(c * r2, r2), pl.ds((1 - c) * r2, r2)
    quarter = pl.ds(c * r2 + k * (r2 // 2), r2 // 2)
    blk, far = 2 * chips[k][0] + chips[k][1], 2 * chips[2][0] + chips[2][1]
    passed, landed = f.at[blk, quarter], f.at[blk, half]
    return [(passed, passed, (*chips[1 - k], c), f.at[far, quarter]), (landed, landed, (x, y, 1 - c), f.at[blk, other])]


def _plan_relay_last(fulls):
    (f,) = fulls
    x, y, c, chips = _place()
    r2 = f.shape[1] // 2
    far = 2 * chips[2][0] + chips[2][1]
    landed = f.at[far, pl.ds(c * r2, r2)]
    return [(landed, landed, (x, y, 1 - c), f.at[far, pl.ds((1 - c) * r2, r2)])]


def _plan_swap_halves(refs):
    x, y, c, _ = _place()
    n = len(refs) // 2
    copies = []
    for g, land in zip(refs[:n], refs[n:]):
        r2 = g.shape[1] // 2
        copies.append((g.at[:, pl.ds((1 - c) * r2, r2), :], land, (x, y, 1 - c), land))
    return copies


def _plan_scatter_chips(refs):
    x, y, c, chips = _place()
    n = len(refs) // 2
    copies = []
    for h, land in zip(refs[:n], refs[n:]):
        for k, chip in enumerate(chips):
            copies.append((h.at[2 * chip[0] + chip[1]], land.at[k], (*chip, c), land.at[k]))
    return copies


def _plan_join_halves(totals):
    x, y, c, _ = _place()
    copies = []
    for t in totals:
        r2 = t.shape[0] // 2
        mine = t.at[pl.ds(c * r2, r2)]
        copies.append((mine, mine, (x, y, 1 - c), t.at[pl.ds((1 - c) * r2, r2)]))
    return copies


def _add_sibling_half(g, got, c_arr, *, name, tm):
    _, rows, cols = g.shape
    r2 = rows // 2
    nb = r2 // tm

    def body(c_ref, g_ref, r_ref, o_ref):
        o_ref[...] = (g_ref[...].astype(F32) + r_ref[...].astype(F32)).astype(o_ref.dtype)

    return pl.pallas_call(
        body, name=name,
        grid_spec=pltpu.PrefetchScalarGridSpec(
            num_scalar_prefetch=1, grid=(N_CHIPS, nb),
            in_specs=[pl.BlockSpec((None, tm, cols), lambda b, i, c: (b, c[0] * nb + i, 0)),
                      pl.BlockSpec((None, tm, cols), lambda b, i, c: (b, i, 0))],
            out_specs=pl.BlockSpec((None, tm, cols), lambda b, i, c: (b, i, 0))),
        out_shape=jax.ShapeDtypeStruct((N_CHIPS, r2, cols), _WIRE),
        compiler_params=_params(("arbitrary", "arbitrary")),
    )(c_arr, g, got)


def _add_chips(h, got, place_arr, *, name, tm):
    _, r2, cols = h.shape
    nb = r2 // tm

    def body(p_ref, h_ref, r_ref, o_ref):
        o_ref[...] = ((h_ref[...].astype(F32) + r_ref[0].astype(F32)) + r_ref[1].astype(F32)) + r_ref[2].astype(F32)

    return pl.pallas_call(
        body, name=name,
        grid_spec=pltpu.PrefetchScalarGridSpec(
            num_scalar_prefetch=1, grid=(nb,),
            in_specs=[pl.BlockSpec((None, tm, cols), lambda i, p: (p[0], i, 0)),
                      pl.BlockSpec((3, tm, cols), lambda i, p: (0, i, 0))],
            out_specs=pl.BlockSpec((tm, cols), lambda i, p: (p[1] * nb + i, 0))),
        out_shape=jax.ShapeDtypeStruct((2 * r2, cols), F32),
        compiler_params=_params(("arbitrary",)),
    )(place_arr, h, got)


class _ReduceScatter:
    def __init__(self, tag, names, grads):
        self.tag, self.names, self.n = tag, names, len(names)
        core = lax.axis_index("c").astype(jnp.int32)
        chip = (2 * lax.axis_index("x") + lax.axis_index("y")).astype(jnp.int32)
        self.c_arr, self.place_arr = core.reshape(1), jnp.stack([chip, core])
        self.bufs = list(grads)

    def _start(self, step, bufs, plan, count, after):
        self.plan = plan
        self.step = f"grad_{step}_{self.tag}"
        self.sems, self.bufs, token = _copies_start(self.step + "_start", bufs, plan, count, after)
        return [token]

    def _wait(self, after):
        self.bufs = _copies_wait(self.step + "_wait", self.bufs, self.sems, self.plan, after)
        return self.bufs

    def start_swap(self, after=()):
        lands = [lax.empty((N_CHIPS, g.shape[1] // 2, g.shape[2]), g.dtype) for g in self.bufs]
        return self._start("swap", self.bufs + lands, _plan_swap_halves, self.n, after)

    def start_scatter(self, after):
        bufs = self._wait(after)
        pair = [_add_sibling_half(g, r, self.c_arr, name=f"grad_add_sibling_{nm}", tm=min(256, g.shape[1] // 2))
                for nm, g, r in zip(self.names, bufs[:self.n], bufs[self.n:])]
        lands = [lax.empty((3,) + h.shape[1:], h.dtype) for h in pair]
        return self._start("scatter", pair + lands, _plan_scatter_chips, 3 * self.n, ())

    def start_join(self, after):
        bufs = self._wait(after)
        total = [_add_chips(h, r, self.place_arr, name=f"grad_add_chips_{nm}", tm=min(256, h.shape[1]))
                 for nm, h, r in zip(self.names, bufs[:self.n], bufs[self.n:])]
        return self._start("join", total, _plan_join_halves, self.n, ())

    def finish(self, after):
        return dict(zip(self.names, self._wait(after)))


def _all_gather_small(v):
    m_per, n = v.shape

    def body(x_ref, out_ref, send_sems, recv_sems, local_sem):
        x, y, c, chips = _place()
        me, sibling = (x, y, c), (x, y, 1 - c)

        def rows(px, py, pc):
            return out_ref.at[4 * px + 2 * py + pc]

        def copy(k, block, to, src=None):
            return _remote(rows(*block) if src is None else src, rows(*block), send_sems.at[k], recv_sems.at[k], to)

        mine = pltpu.make_async_copy(x_ref, rows(*me), local_sem)
        mine.start()
        first = [copy(0, me, sibling, src=x_ref)]
        first += [copy(1 + j, me, (*chip, c), src=x_ref) for j, chip in enumerate(chips)]
        for cp in first:
            cp.start()
        passed = [copy(4 + j, (*chip, c), sibling) for j, chip in enumerate(chips)]
        for j, chip in enumerate(chips):
            copy(1 + j, (*chip, c), me).wait_recv()
            passed[j].start()
        copy(0, sibling, me).wait_recv()
        for j, chip in enumerate(chips):
            copy(4 + j, (*chip, 1 - c), me).wait_recv()
        for cp in first + passed:
            cp.wait_send()
        mine.wait()

    return pl.pallas_call(
        body, name="gather_small_grads",
        out_shape=jax.ShapeDtypeStruct((8, m_per, n), v.dtype),
        in_specs=[pl.BlockSpec(memory_space=pltpu.VMEM)], out_specs=pl.BlockSpec(memory_space=pltpu.VMEM),
        scratch_shapes=[pltpu.SemaphoreType.DMA((7,)), pltpu.SemaphoreType.DMA((7,)), pltpu.SemaphoreType.DMA],
        compiler_params=pltpu.CompilerParams(vmem_limit_bytes=VMEM_LIMIT),
    )(v)


def _sum8(v, *, name):
    _, m, n = v.shape

    def body(v_ref, o_ref):
        acc = v_ref[0]
        for d in range(1, 8):
            acc = acc + v_ref[d]
        o_ref[...] = acc

    return pl.pallas_call(body, name=name, out_shape=jax.ShapeDtypeStruct((m, n), F32),
                          compiler_params=pltpu.CompilerParams(vmem_limit_bytes=VMEM_LIMIT))(v)


def _local_step(x, target, norm_w, q_norm_w, k_norm_w, sinks, a_re, a_im, log_dt, b_re, b_im, c_re, c_im, d_skip,
                b_glu, io):
    seq = x.shape[0]
    qw2 = jnp.tile(q_norm_w.reshape(1, HEAD_DIM), (1, HEADS_PER_TILE))
    kw2 = jnp.tile(k_norm_w.reshape(1, HEAD_DIM), (1, HEADS_PER_TILE))
    nw, bg = norm_w.reshape(1, D_MODEL), b_glu.reshape(1, D_MODEL)
    dsk = d_skip.reshape(1, SSM_W)

    h, rstd = _rms_fwd(x, nw, deps=io.begin())
    proj, w_in4 = io.projection(h)
    attn, lse, ya_in = _attn2_fwd(proj, qw2, kw2, sinks, deps=io.after_proj(proj))
    w_ap4 = io.weight("w_attn_proj", ya_in)
    w_glu4, w_sp4, w_out = io.weight("w_glu", ya_in), io.weight("w_ssm_proj", ya_in), io.weight("w_out", ya_in)
    y_a = _mm(ya_in, w_ap4, mode="nn", name="mm_attn_proj", tm=2048, tn=512, tk=ATTN_W, b_blocked=True,
              rows_outer=True, out_dtype=_MXU)

    flat_a = (a_re.reshape(1, N_STATES), a_im.reshape(1, N_STATES), jnp.repeat(log_dt, STATE).reshape(1, N_STATES))
    coef = _ssm_params_fwd(*flat_a)
    bre_blk, bim_blk = _block_diag_b(b_re).astype(_MXU), _block_diag_b(b_im).astype(_MXU)
    cre_blk, cim_blk = _block_diag_c(c_re).astype(_MXU), _block_diag_c(c_im).astype(_MXU)
    u_scan = _to_scan_order(proj[:, OFF_U * CW:OFF_U * CW + SSM_W])
    y_scan, yg, s_re, s_im, i_re, i_im = _ssm_fwd(u_scan, bre_blk, bim_blk, cre_blk, cim_blk, dsk, coef)
    glu, ys_in = _mm_glu_gate(yg, w_glu4, bg, proj)
    y_s = _mm(ys_in, w_sp4, mode="nn", name="mm_ssm_proj", tm=2048, tn=512, tk=SSM_W, b_blocked=True,
              rows_outer=True, out_dtype=_MXU)

    merged, dout, dout_b, sq = _mm_merge_out_loss(proj, y_a, y_s, w_out, x, target)
    loss = 0.5 * jnp.sum(sq) / D_MODEL

    d_ya, d_ys, d_proj = _mm_merge_bwd(dout_b, w_out, proj, y_a, y_s)
    g_w_out = _mm(merged, dout_b, mode="tn", name="mm_g_w_out", tm=1024, tn=D_MODEL, tk=2048, out_dtype=_WIRE)

    d_ya_in = _mm(d_ya, w_ap4, mode="nt", name="mm_d_attn_gate", tm=2048, tn=ATTN_W, tk=512, b_blocked=True)
    g_w_ap = _mm(ya_in, d_ya, mode="tn", name="mm_g_w_attn_proj", tm=ATTN_W, tn=D_MODEL, tk=2048, out_dtype=_WIRE,
                 out_blocked=True)

    g_w_sp = _mm(ys_in, d_ys, mode="tn", name="mm_g_w_ssm_proj", tm=SSM_W, tn=D_MODEL, tk=2048, out_dtype=_WIRE,
                 out_blocked=True)
    d_glu, d_proj, g_bglu = _mm_ssm_gate_bwd(d_ys, w_sp4, glu, bg, proj, d_proj)
    d_yg = _mm(d_glu, w_glu4, mode="nt", name="mm_d_gelu", tm=2048, tn=SSM_W, tk=512, b_blocked=True)
    g_w_glu = _mm(yg, d_glu, mode="tn", name="mm_g_w_glu", tm=SSM_W, tn=D_MODEL, tk=2048, out_dtype=_WIRE, out_blocked=True)
    dep = io.later_grads(dict(w_attn_proj=g_w_ap, w_glu=g_w_glu, w_ssm_proj=g_w_sp,
                              w_out=g_w_out.reshape(N_CHIPS, D_MODEL // N_CHIPS, D_MODEL)))

    d_proj, g_qw2, g_kw2, g_sk = _attn2_bwd(proj, qw2, kw2, sinks, lse, attn, d_ya_in, d_proj, deps=dep)
    dep = io.before_scan_backward([d_proj])
    (d_proj, g_bre, g_bim, g_cre, g_cim, g_dsk, g_abr, g_abi, g_cfr, g_cfi) = _ssm_bwd(
        _to_scan_order(d_yg), y_scan, u_scan, s_re, s_im, i_re, i_im, bre_blk, bim_blk, cre_blk, cim_blk, dsk, coef,
        d_proj, deps=dep)
    g_are, g_aim, g_ldt = _ssm_params_bwd(*flat_a, g_abr, g_abi, g_cfr, g_cfi)
    g_are, g_aim = g_are.reshape(N_GROUPS, STATE), g_aim.reshape(N_GROUPS, STATE)
    g_ldt = g_ldt.reshape(N_GROUPS, STATE).sum(axis=1)
    dep = io.before_input_projection_grad([d_proj]) + io.small_grads(dict(
        q_norm_w=g_qw2[0, :HEAD_DIM] + g_qw2[0, HEAD_DIM:], k_norm_w=g_kw2[0, :HEAD_DIM] + g_kw2[0, HEAD_DIM:],
        sinks=g_sk.reshape(N_Q_HEADS), A_re=g_are, A_im=g_aim, log_dt=g_ldt,
        B_re=_diag_of_b(g_bre), B_im=_diag_of_b(g_bim), C_re=_diag_of_c(g_cre), C_im=_diag_of_c(g_cim),
        D_skip=g_dsk.reshape(N_GROUPS, GROUP), b_glu=g_bglu.reshape(D_MODEL)))
    g_w_in = _mm(h, d_proj, mode="tn", name="mm_g_w_in", tm=1024, tn=IN_W // 4, tk=2048, out_dtype=_WIRE,
                 out_blocked=True, deps=dep)
    dep = io.input_projection_grad(g_w_in)
    d_h = _mm(d_proj, w_in4, mode="nt", name="mm_d_h", tm=1024, tn=D_MODEL, tk=IN_W // 4, b_blocked=True, deps=dep)
    grad_x, g_nw = _rms_bwd(d_h, x, rstd, nw, dout)
    return loss, grad_x, g_nw.reshape(D_MODEL)


_SMALL = ["norm_w", "q_norm_w", "k_norm_w", "sinks", "A_re", "A_im", "log_dt", "B_re", "B_im", "C_re", "C_im",
          "D_skip", "b_glu"]
_BIG = ["w_in", "w_attn_proj", "w_glu", "w_ssm_proj", "w_out"]
_LATER = _BIG[1:]
_RELATIONS = ("flip_x", "flip_y", "flip_xy")
_ORDER = ["norm_w", "w_in", "q_norm_w", "k_norm_w", "sinks", "w_attn_proj", "A_re", "A_im", "log_dt", "B_re", "B_im",
          "C_re", "C_im", "D_skip", "w_glu", "b_glu", "w_ssm_proj", "w_out"]
_PACK_W = 1024
_MINOR_SWAPPED = ("B_re", "B_im")


def _swap_minor(a):
    return jnp.swapaxes(a, 1, 2)


def _packed_rows(size):
    unit = SUBLANES * _PACK_W
    return -(-size // unit) * SUBLANES


def _pack_small(d, names):
    parts = []
    for n in names:
        flat = d[n].reshape(-1).astype(F32)
        rows = _packed_rows(flat.shape[0])
        parts.append(jnp.pad(flat, (0, rows * _PACK_W - flat.shape[0])).reshape(rows, _PACK_W))
    return jnp.concatenate(parts, axis=0)


def _unpack_small(packed, like, names):
    out, pos = {}, 0
    for n in names:
        rows = _packed_rows(like[n].size)
        out[n] = packed[pos:pos + rows].reshape(-1)[:like[n].size].reshape(like[n].shape)
        pos += rows
    return out


def _place_block(v, index_arr, *, name):
    rows, cols = v.shape

    def body(i_ref, v_ref, o_ref):
        o_ref[...] = v_ref[...]

    return pl.pallas_call(
        body, name=name,
        grid_spec=pltpu.PrefetchScalarGridSpec(
            num_scalar_prefetch=1, grid=(1,),
            in_specs=[pl.BlockSpec((rows, cols), lambda i, d: (0, 0))],
            out_specs=pl.BlockSpec((None, rows, cols), lambda i, d: (d[0], 0, 0))),
        out_shape=jax.ShapeDtypeStruct((8, rows, cols), v.dtype),
        compiler_params=_params(("arbitrary",)),
    )(index_arr, v)


def _plan_all_to_all(refs):
    (land,) = refs
    x, y, c, _ = _place()
    own = land.at[4 * x + 2 * y + c]
    copies = []
    for fx, fy, fc in [(0, 0, 1), (0, 1, 0), (0, 1, 1), (1, 0, 0), (1, 0, 1), (1, 1, 0), (1, 1, 1)]:
        px, py, pc = (1 - x) if fx else x, (1 - y) if fy else y, (1 - c) if fc else c
        copies.append((own, own, (px, py, pc), land.at[4 * px + 2 * py + pc]))
    return copies


def _adamw_whole(w, g, m, v, *, name):
    def body(w_ref, g_ref, m_ref, v_ref, d_ref, nm_ref, nv_ref):
        d_ref[...], nm_ref[...], nv_ref[...] = _adamw_math(w_ref[...], g_ref[...], m_ref[...], v_ref[...])

    return pl.pallas_call(body, name=name, out_shape=[jax.ShapeDtypeStruct(w.shape, F32)] * 3)(w, g, m, v)


class _Exchanges:
    def __init__(self, w, m, v):
        self.w, self.m, self.v = w, m, v
        self.grads, self.delta, self.new_m, self.new_v = {}, {}, {}, {}

    def _adamw(self, names, deps):
        for n in names:
            self.delta[n], self.new_m[n], self.new_v[n], self.grads[n] = _adamw(
                self.w[n], self.grads[n], self.m[n], self.v[n], name=f"adamw_{n}", tm=256, deps=deps)

    def begin(self):
        chip = (2 * lax.axis_index("x") + lax.axis_index("y")).astype(jnp.int32).reshape(1)
        w_in = _place_shard(self.w["w_in"], chip, name="place_w_in")
        self.w_in_sems, self.w_in_buf, token = _copies_start("gather_w_in_direct_start", [w_in], _plan_relay_direct, 2)
        self.later_full = [_place_shard(self.w[n], chip, name=f"place_{n}", deps=[token]) for n in _LATER]
        return self.later_full

    def projection(self, h):
        x, y = lax.axis_index("x"), lax.axis_index("y")
        blks = [jnp.asarray(b, jnp.int32).reshape(1)
                for b in (2 * x + y, 2 * (1 - x) + y, 2 * x + (1 - y), 2 * (1 - x) + (1 - y))]
        bufs = self.w_in_buf
        proj = _mm_chip_block(h, bufs[0], blks[0], None, name="mm_proj_own", out_dtype=_MXU)
        relay, token = [], proj
        for k, tag in enumerate(_RELATIONS[:2]):
            bufs = _copies_wait(f"gather_w_in_direct_{tag}_wait", bufs, self.w_in_sems, _plan_relay_direct, [token],
                                which=(k,))
            plan = functools.partial(_plan_relay_forward, k=k)
            sems, bufs, token = _copies_start(f"gather_w_in_relay_{tag}_start", bufs, plan, 2)
            relay.append((sems, plan))
        self.rest = _copies_start("gather_ici_rest_start", self.later_full, _plan_gather_ici, 3 * len(_LATER),
                                  after=[token])
        token = self.rest[2]
        for k, tag in enumerate(_RELATIONS[:2]):
            bufs = _copies_wait(f"gather_w_in_hand_{tag}_wait", bufs, relay[k][0], relay[k][1], [token], which=(1,))
            token = proj = _mm_chip_block(h, bufs[0], blks[1 + k], proj, name=f"mm_proj_{tag}", out_dtype=_MXU)
        for k, tag in enumerate(_RELATIONS[:2]):
            bufs = _copies_wait(f"gather_w_in_relay_{tag}_wait", bufs, relay[k][0], relay[k][1], [token], which=(0,))
        sems, bufs, token = _copies_start("gather_w_in_last_start", bufs, _plan_relay_last, 1)
        bufs = _copies_wait("gather_w_in_last_wait", bufs, sems, _plan_relay_last, [token])
        proj = _mm_chip_block(h, bufs[0], blks[3], proj, name="mm_proj_flip_xy", out_dtype=_MXU)
        return proj, bufs[0]

    def weight(self, name, after):
        if self.rest is not None:
            sems, bufs = self.rest
            later = dict(zip(_LATER, _copies_wait("gather_d2d_rest_wait", bufs, sems, _plan_gather_d2d, [after])))
            later["w_out"] = later["w_out"].reshape(D_MODEL, D_MODEL)
            self.later, self.rest = later, None
        return self.later[name]

    def after_proj(self, proj):
        sems, bufs, _ = self.rest
        bufs = _copies_wait("gather_ici_rest_wait", bufs, sems, _plan_gather_ici, [proj])
        sems, bufs, token = _copies_start("gather_d2d_rest_start", bufs, _plan_gather_d2d, 3 * len(_LATER))
        self.rest = (sems, bufs)
        return [token]

    def later_grads(self, grads):
        self.rs_later = _ReduceScatter("later", _LATER, [grads[n] for n in _LATER])
        return self.rs_later.start_swap()

    def before_scan_backward(self, after):
        return self.rs_later.start_scatter(after)

    def before_input_projection_grad(self, after):
        return self.rs_later.start_join(after)

    def input_projection_grad(self, g_w_in):
        self.grads.update(self.rs_later.finish([g_w_in]))
        self.rs_in = _ReduceScatter("w_in", ["w_in"], [g_w_in])
        self._adamw(_LATER, self.rs_in.start_swap())
        return self.rs_in.start_scatter([self.delta[n] for n in _LATER])

    def _adamw_small(self, names):
        for n in names:
            swap = _swap_minor if n in _MINOR_SWAPPED else (lambda a: a)
            out = _adamw_whole(swap(self.w[n]), self.grads[n], swap(self.m[n]), swap(self.v[n]), name=f"adamw_{n}")
            self.delta[n], self.new_m[n], self.new_v[n] = [swap(o) for o in out]
            self.grads[n] = swap(self.grads[n])

    def small_grads(self, grads):
        me = (4 * lax.axis_index("x") + 2 * lax.axis_index("y") + lax.axis_index("c")).astype(jnp.int32).reshape(1)
        grads = {n: _swap_minor(g) if n in _MINOR_SWAPPED else g for n, g in grads.items()}
        land = _place_block(_pack_small(grads, _SMALL[1:]), me, name="place_small_grads")
        self.small = _copies_start("gather_small_start", [land], _plan_all_to_all, 7)
        return [self.small[2]]

    def finish(self, g_norm_w, loss, after):
        join = self.rs_in.start_join(after)
        sems, bufs, _ = self.small
        (land,) = _copies_wait("gather_small_wait", bufs, sems, _plan_all_to_all, join)
        like = {n: _swap_minor(self.w[n]) if n in _MINOR_SWAPPED else self.w[n] for n in _SMALL[1:]}
        self.grads.update(_unpack_small(_sum8(land, name="sum_small_grads"), like, _SMALL[1:]))
        self._adamw_small(_SMALL[1:])
        rows = _packed_rows(g_norm_w.size)
        late = jnp.concatenate([_pack_small(dict(norm_w=g_norm_w), _SMALL[:1]),
                                jnp.pad(loss.reshape(1, 1), ((0, SUBLANES - 1), (0, _PACK_W - 1)))], axis=0)
        late = _sum8(_all_gather_small(late), name="sum_norm_w_grad_and_loss")
        self.grads.update(_unpack_small(late[:rows], self.w, _SMALL[:1]))
        self._adamw_small(_SMALL[:1])
        self.grads.update(self.rs_in.finish([self.delta[_SMALL[0]]]))
        self._adamw(["w_in"], ())
        return late[rows, 0]


def kernel(x, norm_w, w_in, q_norm_w, k_norm_w, sinks, w_attn_proj, A_re, A_im, log_dt, B_re, B_im, C_re, C_im, D_skip, w_glu, b_glu, w_ssm_proj, w_out, loss_target, m_norm_w, m_w_in, m_q_norm_w, m_k_norm_w, m_sinks, m_w_attn_proj, m_A_re, m_A_im, m_log_dt, m_B_re, m_B_im, m_C_re, m_C_im, m_D_skip, m_w_glu, m_b_glu, m_w_ssm_proj, m_w_out, v_norm_w, v_w_in, v_q_norm_w, v_k_norm_w, v_sinks, v_w_attn_proj, v_A_re, v_A_im, v_log_dt, v_B_re, v_B_im, v_C_re, v_C_im, v_D_skip, v_w_glu, v_b_glu, v_w_ssm_proj, v_w_out):
    w = dict(norm_w=norm_w, w_in=w_in, q_norm_w=q_norm_w, k_norm_w=k_norm_w, sinks=sinks, w_attn_proj=w_attn_proj,
             A_re=A_re, A_im=A_im, log_dt=log_dt, B_re=B_re, B_im=B_im, C_re=C_re, C_im=C_im, D_skip=D_skip,
             w_glu=w_glu, b_glu=b_glu, w_ssm_proj=w_ssm_proj, w_out=w_out)
    m = dict(norm_w=m_norm_w, w_in=m_w_in, q_norm_w=m_q_norm_w, k_norm_w=m_k_norm_w, sinks=m_sinks,
             w_attn_proj=m_w_attn_proj, A_re=m_A_re, A_im=m_A_im, log_dt=m_log_dt, B_re=m_B_re, B_im=m_B_im,
             C_re=m_C_re, C_im=m_C_im, D_skip=m_D_skip, w_glu=m_w_glu, b_glu=m_b_glu, w_ssm_proj=m_w_ssm_proj,
             w_out=m_w_out)
    v = dict(norm_w=v_norm_w, w_in=v_w_in, q_norm_w=v_q_norm_w, k_norm_w=v_k_norm_w, sinks=v_sinks,
             w_attn_proj=v_w_attn_proj, A_re=v_A_re, A_im=v_A_im, log_dt=v_log_dt, B_re=v_B_re, B_im=v_B_im,
             C_re=v_C_re, C_im=v_C_im, D_skip=v_D_skip, w_glu=v_w_glu, b_glu=v_b_glu, w_ssm_proj=v_w_ssm_proj,
             w_out=v_w_out)

    io = _Exchanges(w, m, v)
    loss, grad_x, g_norm_w = _local_step(x[0], loss_target[0], norm_w, q_norm_w, k_norm_w, sinks, A_re, A_im, log_dt,
                                         B_re, B_im, C_re, C_im, D_skip, b_glu, io)
    loss = io.finish(g_norm_w, loss, [grad_x])
    grads, delta, new_m, new_v = io.grads, io.delta, io.new_m, io.new_v

    return (loss, grad_x[None], *[grads[n] for n in _ORDER], *[delta[n] for n in _ORDER],
            *[new_m[n] for n in _ORDER], *[new_v[n] for n in _ORDER])
```

```python
import functools
import math

import jax
import jax.numpy as jnp
from jax import lax
from jax.experimental import pallas as pl
from jax.experimental.pallas import tpu as pltpu

F32 = jnp.float32
_MXU = jnp.bfloat16
_WIRE = jnp.bfloat16

LANES = 128
SUBLANES = 8
VMEM_LIMIT = 56 * 1024 * 1024

D_MODEL = 2048
HEAD_DIM = 64
N_Q_HEADS = 16
N_KV_HEADS = 4
Q_PER_KV = 4
ATTN_W = 1024
KV_W = 256
WINDOW = 128
SSM_W = 1024
GROUP = 16
N_GROUPS = 64
STATE = 64
N_STATES = N_GROUPS * STATE
IN_W = 8704
NORM_EPS = 1e-6
N_CHIPS = 4
CW = 512
OFF_AGATE, OFF_U, OFF_Z, OFF_GA, OFF_GS = 3, 5, 7, 9, 13

SSM_T = 256
SSM_L = SSM_T // SUBLANES
SSM_JB = 8
SSM_SB = N_STATES // SSM_JB

ADAM_LR, ADAM_B1, ADAM_B2, ADAM_EPS, ADAM_WD, ADAM_STEP = 0.001, 0.9, 0.999, 1e-08, 0.01, 10

MESH = pl.DeviceIdType.MESH
_ANY = pl.BlockSpec(memory_space=pl.ANY)


def _params(sem=None):
    return pltpu.CompilerParams(dimension_semantics=sem, vmem_limit_bytes=VMEM_LIMIT)


def _mm(a, b, *, mode, name, tm, tn, tk, out_dtype=F32, b_blocked=False, out_blocked=False, rows_outer=False,
        deps=()):
    nd = len(deps)
    if mode == "tn":
        K, M = a.shape
    else:
        M, K = a.shape
    if mode == "nn":
        N = b.shape[0] * b.shape[2] if b_blocked else b.shape[1]
    elif mode == "nt":
        N = b.shape[1] if b_blocked else b.shape[0]
    else:
        N = b.shape[1]
    tm, tn, tk = min(tm, M), min(tn, N), min(tk, K)
    nj, ni, nk = N // tn, M // tm, K // tk
    assert nj * tn == N and ni * tm == M and nk * tk == K, (name, M, N, K)
    dims = {"nn": (((1,), (0,)), ((), ())), "nt": (((1,), (1,)), ((), ())), "tn": (((0,), (0,)), ((), ()))}[mode]

    if mode == "tn":
        a_spec = pl.BlockSpec((tk, tm), lambda j, i, k: (k, i))
    else:
        a_spec = pl.BlockSpec((tm, tk), lambda j, i, k: (i, k))
    if mode == "nn":
        if b_blocked:
            assert b.shape[0] == nj and b.shape[2] == tn
            b_spec = pl.BlockSpec((None, tk, tn), lambda j, i, k: (j, k, 0))
        else:
            b_spec = pl.BlockSpec((tk, tn), lambda j, i, k: (k, j))
    elif mode == "nt":
        if b_blocked:
            assert b.shape[0] == nk and b.shape[2] == tk
            b_spec = pl.BlockSpec((None, tn, tk), lambda j, i, k: (k, j, 0))
        else:
            b_spec = pl.BlockSpec((tn, tk), lambda j, i, k: (j, k))
    else:
        b_spec = pl.BlockSpec((tk, tn), lambda j, i, k: (k, j))
    whole_out = out_blocked and nj == 1
    if whole_out:
        assert ni == 1
        o_spec = pl.BlockSpec((N_CHIPS, tm, tn // N_CHIPS), lambda j, i, k: (0, 0, 0))
        o_shape = jax.ShapeDtypeStruct((N_CHIPS, M, tn // N_CHIPS), out_dtype)
    elif out_blocked:
        assert nj == N_CHIPS
        o_spec = pl.BlockSpec((None, tm, tn), lambda j, i, k: (j, i, 0))
        o_shape = jax.ShapeDtypeStruct((nj, M, tn), out_dtype)
    else:
        o_spec = pl.BlockSpec((tm, tn), lambda j, i, k: (i, j))
        o_shape = jax.ShapeDtypeStruct((M, N), out_dtype)
    use_acc = nk > 1 and (out_dtype != F32 or whole_out)

    def body(a_ref, b_ref, *rest):
        o_ref, scratch = rest[nd], rest[nd + 1:]

        def product():
            return lax.dot_general(a_ref[...].astype(_MXU), b_ref[...].astype(_MXU), dims, preferred_element_type=F32)

        def write(result):
            if whole_out:
                w = tn // N_CHIPS
                for c in range(N_CHIPS):
                    o_ref[c] = result[:, c * w:(c + 1) * w].astype(o_ref.dtype)
            else:
                o_ref[...] = result.astype(o_ref.dtype)

        if nk == 1:
            write(product())
            return
        k = pl.program_id(2)
        acc = scratch[0] if use_acc else o_ref

        @pl.when(k == 0)
        def _():
            acc[...] = jnp.zeros_like(acc)

        acc[...] += product()

        if use_acc:
            @pl.when(k == nk - 1)
            def _():
                write(acc[...])

    specs = [a_spec, b_spec, o_spec]
    grid = (nj, ni, nk)
    if rows_outer:
        specs = [pl.BlockSpec(s.block_shape, lambda i, j, k, f=s.index_map: f(j, i, k)) for s in specs]
        grid = (ni, nj, nk)
    return pl.pallas_call(
        body, name=name, grid=grid, in_specs=specs[:2] + [_ANY] * nd, out_specs=specs[2],
        out_shape=o_shape, scratch_shapes=[pltpu.VMEM((tm, tn), F32)] if use_acc else [],
        compiler_params=_params(("parallel", "parallel", "arbitrary")),
    )(a, b, *deps)


def _mm_chip_block(a, b4, blk, prev, *, name, tm=1024, out_dtype=F32, deps=()):
    M, K = a.shape
    nchip, _, C = b4.shape
    tm = min(tm, M)
    extra = ([] if prev is None else [prev]) + list(deps)

    def body(blk_ref, a_ref, b_ref, *rest):
        rest[-1][...] = jnp.dot(a_ref[...].astype(_MXU), b_ref[...].astype(_MXU),
                                preferred_element_type=F32).astype(rest[-1].dtype)

    return pl.pallas_call(
        body, name=name,
        grid_spec=pltpu.PrefetchScalarGridSpec(
            num_scalar_prefetch=1, grid=(M // tm,),
            in_specs=[pl.BlockSpec((tm, K), lambda i, c: (i, 0)), pl.BlockSpec((None, K, C), lambda i, c: (c[0], 0, 0))]
            + [_ANY] * len(extra),
            out_specs=pl.BlockSpec((tm, C), lambda i, c: (i, c[0]))),
        out_shape=jax.ShapeDtypeStruct((M, nchip * C), out_dtype),
        input_output_aliases={} if prev is None else {3: 0},
        compiler_params=_params(("arbitrary",)),
    )(blk, a, b4, *extra)


def _mm_merge_out_loss(proj, y_a, y_s, w_out, x, target, *, tm=256):
    rows, d = x.shape
    ncol = d // CW

    def body(*refs):
        ga_refs, gs_refs = refs[:ncol], refs[ncol:2 * ncol]
        ya_ref, ys_ref, w_ref, x_ref, t_ref, m_ref, d_ref, db_ref, sq_ref = refs[2 * ncol:]
        for j in range(ncol):
            cols = slice(j * CW, (j + 1) * CW)
            m_ref[:, cols] = (_sigmoid(ga_refs[j][...].astype(F32)) * ya_ref[:, cols].astype(F32)
                              + _sigmoid(gs_refs[j][...].astype(F32)) * ys_ref[:, cols].astype(F32)).astype(m_ref.dtype)
        mo = jnp.dot(m_ref[...], w_ref[...].astype(_MXU), preferred_element_type=F32)
        err = (x_ref[...] + mo) - t_ref[...]
        dout = err * (1.0 / d)
        d_ref[...] = dout
        db_ref[...] = dout.astype(db_ref.dtype)
        part = _colsum(err * err)
        i = pl.program_id(0)

        @pl.when(i == 0)
        def _():
            sq_ref[...] = part

        @pl.when(i > 0)
        def _():
            sq_ref[...] += part

    tile = pl.BlockSpec((tm, d), lambda i: (i, 0))
    gate = [pl.BlockSpec((tm, CW), lambda i, c=off + j: (i, c)) for off in (OFF_GA, OFF_GS) for j in range(ncol)]
    return pl.pallas_call(
        body, name="mm_merge_out_loss", grid=(rows // tm,),
        in_specs=gate + [tile, tile, pl.BlockSpec((d, d), lambda i: (0, 0), pipeline_mode=pl.Buffered(1)), tile, tile],
        out_specs=[tile, tile, tile, pl.BlockSpec((1, d), lambda i: (0, 0))],
        out_shape=[jax.ShapeDtypeStruct((rows, d), _MXU), jax.ShapeDtypeStruct((rows, d), F32),
                   jax.ShapeDtypeStruct((rows, d), _MXU), jax.ShapeDtypeStruct((1, d), F32)],
        compiler_params=_params(("arbitrary",)),
    )(*([proj] * (2 * ncol)), y_a, y_s, w_out, x, target)


def _mm_merge_bwd(dout_b, w_out, proj, y_a, y_s, *, tm=512):
    rows, d = y_a.shape
    ncol = d // CW

    def body(do_ref, w_ref, *refs):
        ga_refs, gs_refs = refs[:ncol], refs[ncol:2 * ncol]
        ya_ref, ys_ref, dya_ref, dys_ref, dg_ref = refs[2 * ncol:]
        dm = lax.dot_general(do_ref[...].astype(_MXU), w_ref[...].astype(_MXU), _NT, preferred_element_type=F32)
        for j in range(ncol):
            cols = slice(j * CW, (j + 1) * CW)
            dmj = dm[:, cols]
            sa, ss = _sigmoid(ga_refs[j][...].astype(F32)), _sigmoid(gs_refs[j][...].astype(F32))
            dya_ref[:, cols] = (sa * dmj).astype(dya_ref.dtype)
            dys_ref[:, cols] = (ss * dmj).astype(dys_ref.dtype)
            dg_ref[:, cols] = (dmj * ya_ref[:, cols].astype(F32) * sa * (1.0 - sa)).astype(dg_ref.dtype)
            dg_ref[:, d + j * CW:d + (j + 1) * CW] = (dmj * ys_ref[:, cols].astype(F32) * ss
                                                      * (1.0 - ss)).astype(dg_ref.dtype)

    tile = pl.BlockSpec((tm, d), lambda i: (i, 0))
    gate = [pl.BlockSpec((tm, CW), lambda i, c=off + j: (i, c)) for off in (OFF_GA, OFF_GS) for j in range(ncol)]
    both = pl.BlockSpec((pl.Element(tm), pl.Element(2 * d)), lambda i: (i * tm, OFF_GA * CW))
    return pl.pallas_call(
        body, name="mm_merge_bwd", grid=(rows // tm,),
        in_specs=[tile, pl.BlockSpec((d, d), lambda i: (0, 0), pipeline_mode=pl.Buffered(1))] + gate + [tile, tile],
        out_specs=[tile, tile, both],
        out_shape=[jax.ShapeDtypeStruct((rows, d), _MXU)] * 2 + [jax.ShapeDtypeStruct((rows, IN_W), _MXU)],
        compiler_params=_params(("arbitrary",)),
    )(dout_b, w_out, *([proj] * (2 * ncol)), y_a, y_s)


def _mm_glu_gate(yg, w_glu4, b_glu, proj, *, tm=1024):
    rows, k = yg.shape
    nj, _, tn = w_glu4.shape
    w = nj * tn // 2
    tm = min(tm, rows)

    def body(a_ref, w_ref, ba_ref, bb_ref, z0_ref, z1_ref, glu_ref, ys_ref):
        j = pl.program_id(1)
        for c in range(nj):
            @pl.when(j == c)
            def _(c=c):
                glu_ref[:, c * tn:(c + 1) * tn] = jnp.dot(a_ref[...].astype(_MXU), w_ref[...].astype(_MXU),
                                                          preferred_element_type=F32).astype(glu_ref.dtype)

        @pl.when(j == nj - 1)
        def _():
            z = jnp.concatenate([z0_ref[...], z1_ref[...]], axis=1).astype(F32)
            ys_ref[...] = ((glu_ref[:, :w].astype(F32) + ba_ref[...]) * _sigmoid(glu_ref[:, w:].astype(F32) + bb_ref[...])
                           * (z * _sigmoid(z))).astype(ys_ref.dtype)

    bias = lambda c: pl.BlockSpec((1, w), lambda i, j: (0, c))
    zcol = lambda c: pl.BlockSpec((tm, CW), lambda i, j: (i, OFF_Z + c))
    return pl.pallas_call(
        body, name="mm_glu_gate", grid=(rows // tm, nj),
        in_specs=[pl.BlockSpec((tm, k), lambda i, j: (i, 0)), pl.BlockSpec((None, k, tn), lambda i, j: (j, 0, 0)),
                  bias(0), bias(1), zcol(0), zcol(1)],
        out_specs=[pl.BlockSpec((tm, nj * tn), lambda i, j: (i, 0)), pl.BlockSpec((tm, w), lambda i, j: (i, 0))],
        out_shape=[jax.ShapeDtypeStruct((rows, nj * tn), _MXU), jax.ShapeDtypeStruct((rows, w), _MXU)],
        compiler_params=_params(("arbitrary", "arbitrary")),
    )(yg, w_glu4, b_glu, b_glu, proj, proj)


def _mm_ssm_gate_bwd(d_ys, w_sp4, glu, b_glu, proj, d_proj, *, tm=1024):
    rows, w = glu.shape[0], glu.shape[1] // 2
    nk, tk = w_sp4.shape[0], w_sp4.shape[2]
    tm = min(tm, rows)

    def body(dy_ref, w_ref, ga_ref, gb_ref, ba_ref, bb_ref, z0_ref, z1_ref, buf_ref, dg_ref, dz_ref, db_ref, acc):
        i, k = pl.program_id(0), pl.program_id(1)

        @pl.when(k == 0)
        def _():
            acc[...] = jnp.zeros_like(acc)

        acc[...] += lax.dot_general(dy_ref[...].astype(_MXU), w_ref[...].astype(_MXU), _NT, preferred_element_type=F32)

        @pl.when(k == nk - 1)
        def _():
            dv = acc[...]
            a, sb = ga_ref[...].astype(F32) + ba_ref[...], _sigmoid(gb_ref[...].astype(F32) + bb_ref[...])
            f, df = _silu_and_grad(jnp.concatenate([z0_ref[...], z1_ref[...]], axis=1).astype(F32))
            dga = dv * sb * f
            dgb = dv * a * f * sb * (1.0 - sb)
            dg_ref[:, :w] = dga.astype(dg_ref.dtype)
            dg_ref[:, w:] = dgb.astype(dg_ref.dtype)
            dz_ref[...] = (dv * a * sb * df).astype(dz_ref.dtype)
            part = jnp.concatenate([_colsum(dga), _colsum(dgb)], axis=1)

            @pl.when(i == 0)
            def _():
                db_ref[...] = part

            @pl.when(i > 0)
            def _():
                db_ref[...] += part

    half = lambda c: pl.BlockSpec((tm, w), lambda i, k: (i, c))
    bias = lambda c: pl.BlockSpec((1, w), lambda i, k: (0, c))
    zcol = lambda c: pl.BlockSpec((tm, CW), lambda i, k: (i, OFF_Z + c))
    return pl.pallas_call(
        body, name="mm_ssm_gate_bwd", grid=(rows // tm, nk),
        in_specs=[pl.BlockSpec((tm, tk), lambda i, k: (i, k)), pl.BlockSpec((None, w, tk), lambda i, k: (k, 0, 0)),
                  half(0), half(1), bias(0), bias(1), zcol(0), zcol(1), _ANY],
        out_specs=[pl.BlockSpec((tm, 2 * w), lambda i, k: (i, 0)),
                   pl.BlockSpec((pl.Element(tm), pl.Element(w)), lambda i, k: (i * tm, OFF_Z * CW)),
                   pl.BlockSpec((1, 2 * w), lambda i, k: (0, 0))],
        out_shape=[jax.ShapeDtypeStruct((rows, 2 * w), _MXU), jax.ShapeDtypeStruct(d_proj.shape, d_proj.dtype),
                   jax.ShapeDtypeStruct((1, 2 * w), F32)],
        input_output_aliases={8: 1},
        scratch_shapes=[pltpu.VMEM((tm, w), F32)],
        compiler_params=_params(("arbitrary", "arbitrary")),
    )(d_ys, w_sp4, glu, glu, b_glu, b_glu, proj, proj, d_proj)


def _colsum(v):
    return jnp.sum(v, axis=0, keepdims=True)


def _sigmoid(v):
    return jax.nn.sigmoid(v)


def _silu_and_grad(v):
    s = _sigmoid(v)
    return v * s, s * (1.0 + v * (1.0 - s))


def _rms_fwd(x, w, *, tm=512, deps=()):
    rows, d = x.shape
    nd = len(deps)

    def body(x_ref, w_ref, *rest):
        h_ref, r_ref = rest[nd:]
        xv = x_ref[...]
        r = lax.rsqrt(jnp.mean(xv * xv, axis=-1, keepdims=True) + NORM_EPS)
        h_ref[...] = (xv * r * w_ref[...]).astype(h_ref.dtype)
        r_ref[...] = r

    return pl.pallas_call(
        body, name="rms_fwd", grid=(rows // tm,),
        in_specs=[pl.BlockSpec((tm, d), lambda i: (i, 0)), pl.BlockSpec((1, d), lambda i: (0, 0))] + [_ANY] * nd,
        out_specs=[pl.BlockSpec((tm, d), lambda i: (i, 0)), pl.BlockSpec((tm, 1), lambda i: (i, 0))],
        out_shape=[jax.ShapeDtypeStruct((rows, d), _MXU), jax.ShapeDtypeStruct((rows, 1), F32)],
        compiler_params=_params(("arbitrary",)),
    )(x, w, *deps)


def _rms_bwd(dh, x, rstd, w, dout, *, tm=512):
    rows, d = x.shape

    def body(dh_ref, x_ref, r_ref, w_ref, do_ref, gx_ref, gw_ref):
        dhv, xv, r, wv = dh_ref[...], x_ref[...], r_ref[...], w_ref[...]
        xr = xv * r
        t = jnp.mean(dhv * wv * xr, axis=-1, keepdims=True)
        gx_ref[...] = do_ref[...] + r * (wv * dhv - xr * t)
        part = _colsum(dhv * xr)
        i = pl.program_id(0)

        @pl.when(i == 0)
        def _():
            gw_ref[...] = part

        @pl.when(i > 0)
        def _():
            gw_ref[...] += part

    return pl.pallas_call(
        body, name="rms_bwd", grid=(rows // tm,),
        in_specs=[pl.BlockSpec((tm, d), lambda i: (i, 0)), pl.BlockSpec((tm, d), lambda i: (i, 0)),
                  pl.BlockSpec((tm, 1), lambda i: (i, 0)), pl.BlockSpec((1, d), lambda i: (0, 0)),
                  pl.BlockSpec((tm, d), lambda i: (i, 0))],
        out_specs=[pl.BlockSpec((tm, d), lambda i: (i, 0)), pl.BlockSpec((1, d), lambda i: (0, 0))],
        out_shape=[jax.ShapeDtypeStruct((rows, d), F32), jax.ShapeDtypeStruct((1, d), F32)],
        compiler_params=_params(("arbitrary",)),
    )(dh, x, rstd, w, dout)


_NT = (((1,), (1,)), ((), ()))
_TN = (((0,), (0,)), ((), ()))


QKV_W = ATTN_W + 2 * KV_W
HEADS_PER_TILE = LANES // HEAD_DIM


def _low_half(rows):
    return lax.broadcasted_iota(jnp.int32, (rows, LANES), 1) < HEAD_DIM


def _pair_mean(t, low):
    m_lo = jnp.sum(jnp.where(low, t, 0.0), axis=-1, keepdims=True)
    m_hi = jnp.sum(jnp.where(low, 0.0, t), axis=-1, keepdims=True)
    return jnp.where(low, m_lo, m_hi) * (1.0 / HEAD_DIM)


def _pair_rstd(t, low):
    return lax.rsqrt(_pair_mean(t * t, low) + NORM_EPS)


def _dup_half(t, hi, low):
    swapped = pltpu.roll(t, HEAD_DIM, 1)
    return jnp.where(low, swapped, t) if hi else jnp.where(low, t, swapped)


def _fold_halves(t):
    return t + pltpu.roll(t, HEAD_DIM, 1)


def _split_heads(t, low):
    return [jnp.where(low, t, 0.0), jnp.where(low, 0.0, t)]


def _stacked_band_mask(n):
    rows = Q_PER_KV * WINDOW
    qi = lax.broadcasted_iota(jnp.int32, (rows, 2 * WINDOW), 0) % WINDOW + WINDOW
    kj = lax.broadcasted_iota(jnp.int32, (rows, 2 * WINDOW), 1)
    diff = qi - kj
    first_key = jnp.where(n > 0, 0, WINDOW)
    return (diff >= 0) & (diff < WINDOW) & (kj >= first_key)


def _stacked_sinks(sink_ref, g):
    blk = lax.broadcasted_iota(jnp.int32, (Q_PER_KV * WINDOW, 1), 0) // WINDOW
    col = jnp.full((Q_PER_KV * WINDOW, 1), sink_ref[Q_PER_KV * g], F32)
    for r in range(1, Q_PER_KV):
        col = jnp.where(blk == r, sink_ref[Q_PER_KV * g + r], col)
    return col


def _attn_in_specs(nblk, rev):
    def cur(n):
        return (nblk - 1 - n) if rev else n

    q_spec = pl.BlockSpec((WINDOW, ATTN_W), lambda n: (cur(n), 0))
    kvc_spec = pl.BlockSpec((WINDOW, 2 * KV_W), lambda n: (cur(n), ATTN_W // (2 * KV_W)))
    kvp_spec = pl.BlockSpec((WINDOW, 2 * KV_W), lambda n: (jnp.maximum(cur(n) - 1, 0), ATTN_W // (2 * KV_W)))
    w_spec = pl.BlockSpec((1, LANES), lambda n: (0, 0))
    l_spec = pl.BlockSpec((WINDOW, N_Q_HEADS), lambda n: (cur(n), 0))
    gate_specs = [pl.BlockSpec((WINDOW, CW), lambda n, col=OFF_AGATE + j: (cur(n), col)) for j in range(ATTN_W // CW)]
    return q_spec, kvc_spec, kvp_spec, w_spec, l_spec, gate_specs


def _attn2_fwd(proj, qw2, kw2, sinks, deps=()):
    seq = proj.shape[0]
    nblk = seq // WINDOW
    scale = 1.0 / math.sqrt(HEAD_DIM)
    q_spec, kvc_spec, kvp_spec, w_spec, l_spec, gate_specs = _attn_in_specs(nblk, False)
    nd, ng = len(deps), len(gate_specs)

    def body(sink_ref, q_ref, kvc_ref, kvp_ref, qw_ref, kw_ref, *rest):
        gate_refs = rest[:ng]
        o_ref, lse_ref, ya_ref = rest[ng + nd:]
        n = pl.program_id(0)
        low, low2 = _low_half(WINDOW), _low_half(2 * WINDOW)
        valid = _stacked_band_mask(n)
        head_lane = lax.broadcasted_iota(jnp.int32, (WINDOW, N_Q_HEADS), 1)
        kv = jnp.concatenate([kvp_ref[...], kvc_ref[...]], axis=0).astype(F32)
        qwv, kwv = qw_ref[...], kw_ref[...]
        lse_blk = jnp.zeros((WINDOW, N_Q_HEADS), F32)
        for t in range(N_KV_HEADS // HEADS_PER_TILE):
            kt = kv[:, t * LANES:(t + 1) * LANES]
            vt = kv[:, KV_W + t * LANES:KV_W + (t + 1) * LANES]
            kn = kt * _pair_rstd(kt, low2) * kwv
            for hi in range(HEADS_PER_TILE):
                g = HEADS_PER_TILE * t + hi
                kdup = _dup_half(kn, hi, low2).astype(_MXU)
                vdup = _dup_half(vt, hi, low2).astype(_MXU)
                stack = []
                for tq in (2 * g, 2 * g + 1):
                    qt = q_ref[:, tq * LANES:(tq + 1) * LANES].astype(F32)
                    stack += _split_heads(qt * _pair_rstd(qt, low) * qwv, low)
                qs = jnp.concatenate(stack, axis=0).astype(_MXU)
                s = lax.dot_general(qs, kdup, _NT, preferred_element_type=F32) * scale
                s = jnp.where(valid, s, -1e30)
                sink = _stacked_sinks(sink_ref, g)
                m = jnp.maximum(jnp.max(s, axis=-1, keepdims=True), sink)
                e = jnp.exp(s - m)
                z = jnp.sum(e, axis=-1, keepdims=True) + jnp.exp(sink - m)
                o = jnp.dot((e / z).astype(_MXU), vdup, preferred_element_type=F32)
                for i, tq in enumerate((2 * g, 2 * g + 1)):
                    tile = slice(tq * LANES, (tq + 1) * LANES)
                    out = jnp.where(low, o[2 * i * WINDOW:(2 * i + 1) * WINDOW],
                                    o[(2 * i + 1) * WINDOW:(2 * i + 2) * WINDOW])
                    gate = gate_refs[tq * LANES // CW][:, tq * LANES % CW:tq * LANES % CW + LANES].astype(F32)
                    o_ref[:, tile] = out.astype(o_ref.dtype)
                    ya_ref[:, tile] = (out * (gate * _sigmoid(gate))).astype(ya_ref.dtype)
                lse = m + jnp.log(z)
                for r in range(Q_PER_KV):
                    lse_blk = jnp.where(head_lane == Q_PER_KV * g + r, lse[r * WINDOW:(r + 1) * WINDOW], lse_blk)
        lse_ref[...] = lse_blk

    return pl.pallas_call(
        body, name="attn_fwd", grid=(nblk,),
        in_specs=[pl.BlockSpec(memory_space=pltpu.SMEM), q_spec, kvc_spec, kvp_spec, w_spec, w_spec] + gate_specs
        + [_ANY] * nd,
        out_specs=[q_spec, l_spec, q_spec],
        out_shape=[jax.ShapeDtypeStruct((seq, ATTN_W), _MXU), jax.ShapeDtypeStruct((seq, N_Q_HEADS), F32),
                   jax.ShapeDtypeStruct((seq, ATTN_W), _MXU)],
        compiler_params=_params(("arbitrary",)),
    )(sinks, proj, proj, proj, qw2, kw2, *([proj] * ng), *deps)


def _attn2_bwd(proj, qw2, kw2, sinks, lse, attn, dya, d_proj, deps=()):
    seq = proj.shape[0]
    nblk = seq // WINDOW
    scale = 1.0 / math.sqrt(HEAD_DIM)
    q_spec, kvc_spec, kvp_spec, w_spec, l_spec, gate_specs = _attn_in_specs(nblk, True)
    s_spec = pl.BlockSpec((1, N_Q_HEADS), lambda n: (0, 0))
    d_spec = pl.BlockSpec((WINDOW, QKV_W + ATTN_W), lambda n: (nblk - 1 - n, 0))
    deps = list(deps) + [d_proj]
    nd, ng = len(deps), len(gate_specs)

    def body(sink_ref, q_ref, kvc_ref, kvp_ref, qw_ref, kw_ref, lse_ref, attn_ref, dya_ref, *rest):
        gate_refs = rest[:ng]
        d_ref, dqw_ref, dkw_ref, dsk_ref, carry, do_ref = rest[ng + nd:]
        step = pl.program_id(0)
        n = nblk - 1 - step

        @pl.when(step == 0)
        def _():
            carry[...] = jnp.zeros_like(carry)
            dqw_ref[...] = jnp.zeros_like(dqw_ref)
            dkw_ref[...] = jnp.zeros_like(dkw_ref)
            dsk_ref[...] = jnp.zeros_like(dsk_ref)

        for j, g_ref in enumerate(gate_refs):
            cols = slice(j * CW, (j + 1) * CW)
            f, df = _silu_and_grad(g_ref[...].astype(F32))
            dv = dya_ref[:, cols]
            do_ref[:, cols] = dv * f
            d_ref[:, QKV_W + j * CW:QKV_W + (j + 1) * CW] = (dv * attn_ref[:, cols].astype(F32) * df).astype(d_ref.dtype)

        low, low2 = _low_half(WINDOW), _low_half(2 * WINDOW)
        valid = _stacked_band_mask(n)
        head_lane = lax.broadcasted_iota(jnp.int32, (WINDOW, N_Q_HEADS), 1)
        sink_lane = lax.broadcasted_iota(jnp.int32, (1, N_Q_HEADS), 1)
        kv = jnp.concatenate([kvp_ref[...], kvc_ref[...]], axis=0).astype(F32)
        qwv, kwv = qw_ref[...], kw_ref[...]
        lse_blk = lse_ref[...]
        dqw = jnp.zeros((1, LANES), F32)
        dkw = jnp.zeros((1, LANES), F32)
        dsk = jnp.zeros((1, N_Q_HEADS), F32)
        for t in range(N_KV_HEADS // HEADS_PER_TILE):
            kt = kv[:, t * LANES:(t + 1) * LANES]
            vt = kv[:, KV_W + t * LANES:KV_W + (t + 1) * LANES]
            rk = _pair_rstd(kt, low2)
            kn = kt * rk * kwv
            dkn_t = jnp.zeros((2 * WINDOW, LANES), F32)
            dv_t = jnp.zeros((2 * WINDOW, LANES), F32)
            for hi in range(HEADS_PER_TILE):
                g = HEADS_PER_TILE * t + hi
                kdup = _dup_half(kn, hi, low2).astype(_MXU)
                vdup = _dup_half(vt, hi, low2).astype(_MXU)
                tiles = (2 * g, 2 * g + 1)
                qx, rq, stack, dstack, lse_rows = [], [], [], [], []
                for tq in tiles:
                    qt = q_ref[:, tq * LANES:(tq + 1) * LANES].astype(F32)
                    r = _pair_rstd(qt, low)
                    rq.append(r)
                    qx.append(qt * r)
                    stack += _split_heads(qx[-1] * qwv, low)
                    dstack += _split_heads(do_ref[:, tq * LANES:(tq + 1) * LANES], low)
                for r in range(Q_PER_KV):
                    lse_rows.append(jnp.sum(jnp.where(head_lane == Q_PER_KV * g + r, lse_blk, 0.0), axis=-1, keepdims=True))
                qs = jnp.concatenate(stack, axis=0).astype(_MXU)
                dos = jnp.concatenate(dstack, axis=0).astype(_MXU)
                lse_col = jnp.concatenate(lse_rows, axis=0)
                s = lax.dot_general(qs, kdup, _NT, preferred_element_type=F32) * scale
                s = jnp.where(valid, s, -1e30)
                p = jnp.exp(s - lse_col)
                dp = lax.dot_general(dos, vdup, _NT, preferred_element_type=F32)
                dsum = jnp.sum(p * dp, axis=-1, keepdims=True)
                ds = (p * (dp - dsum) * scale).astype(_MXU)
                dsink = -jnp.exp(_stacked_sinks(sink_ref, g) - lse_col) * dsum
                for r in range(Q_PER_KV):
                    dsk = dsk + jnp.where(sink_lane == Q_PER_KV * g + r, _colsum(dsink[r * WINDOW:(r + 1) * WINDOW]), 0.0)
                dv_g = _fold_halves(lax.dot_general(p.astype(_MXU), dos, _TN, preferred_element_type=F32))
                dkn_g = _fold_halves(lax.dot_general(ds, qs, _TN, preferred_element_type=F32))
                dv_t = jnp.where(low2, dv_t, dv_g) if hi else jnp.where(low2, dv_g, dv_t)
                dkn_t = jnp.where(low2, dkn_t, dkn_g) if hi else jnp.where(low2, dkn_g, dkn_t)
                dqn = jnp.dot(ds, kdup, preferred_element_type=F32)
                for i, tq in enumerate(tiles):
                    dqn_t = jnp.where(low, dqn[2 * i * WINDOW:(2 * i + 1) * WINDOW],
                                      dqn[(2 * i + 1) * WINDOW:(2 * i + 2) * WINDOW])
                    dq = rq[i] * (qwv * dqn_t - qx[i] * _pair_mean(dqn_t * qwv * qx[i], low))
                    d_ref[:, tq * LANES:(tq + 1) * LANES] = dq.astype(d_ref.dtype)
                    dqw = dqw + _colsum(dqn_t * qx[i])
            k_cols = slice(t * LANES, (t + 1) * LANES)
            v_cols = slice(KV_W + t * LANES, KV_W + (t + 1) * LANES)
            dkn_c = dkn_t[WINDOW:] + carry[:, k_cols]
            rc = rk[WINDOW:]
            kx = kt[WINDOW:] * rc
            dk = rc * (kwv * dkn_c - kx * _pair_mean(dkn_c * kwv * kx, low))
            d_ref[:, ATTN_W + t * LANES:ATTN_W + (t + 1) * LANES] = dk.astype(d_ref.dtype)
            d_ref[:, ATTN_W + KV_W + t * LANES:ATTN_W + KV_W + (t + 1) * LANES] = (
                dv_t[WINDOW:] + carry[:, v_cols]).astype(d_ref.dtype)
            carry[:, k_cols] = dkn_t[:WINDOW]
            carry[:, v_cols] = dv_t[:WINDOW]
            dkw = dkw + _colsum(dkn_c * kx)
        dqw_ref[...] += dqw
        dkw_ref[...] += dkw
        dsk_ref[...] += dsk

    return pl.pallas_call(
        body, name="attn_bwd", grid=(nblk,),
        in_specs=[pl.BlockSpec(memory_space=pltpu.SMEM), q_spec, kvc_spec, kvp_spec, w_spec, w_spec, l_spec, q_spec,
                  q_spec] + gate_specs + [_ANY] * nd,
        out_specs=[d_spec, w_spec, w_spec, s_spec],
        out_shape=[jax.ShapeDtypeStruct(d_proj.shape, d_proj.dtype), jax.ShapeDtypeStruct((1, LANES), F32),
                   jax.ShapeDtypeStruct((1, LANES), F32), jax.ShapeDtypeStruct((1, N_Q_HEADS), F32)],
        input_output_aliases={9 + ng + nd - 1: 0},
        scratch_shapes=[pltpu.VMEM((WINDOW, 2 * KV_W), F32), pltpu.VMEM((WINDOW, ATTN_W), F32)],
        compiler_params=_params(("arbitrary",)),
    )(sinks, proj, proj, proj, qw2, kw2, lse, attn, dya, *([proj] * ng), *deps)


def _ssm_discretise(a_re, a_im, log_dt):
    dt = jnp.exp(log_dt)
    mag = jnp.exp(dt * a_re)
    ab_re = mag * jnp.cos(dt * a_im)
    ab_im = mag * jnp.sin(dt * a_im)
    num_re = ab_re - 1.0
    num_im = ab_im
    den = a_re * a_re + a_im * a_im
    cf_re = (num_re * a_re + num_im * a_im) / den
    cf_im = (num_im * a_re - num_re * a_im) / den
    return ab_re, ab_im, cf_re, cf_im


def _ssm_params_fwd(a_re, a_im, log_dt):
    shp = jax.ShapeDtypeStruct(a_re.shape, F32)

    def body(are_ref, aim_ref, ldt_ref, abr_ref, abi_ref, cfr_ref, cfi_ref, alr_ref, ali_ref):
        abr, abi, cfr, cfi = _ssm_discretise(are_ref[...], aim_ref[...], ldt_ref[...])
        abr_ref[...], abi_ref[...], cfr_ref[...], cfi_ref[...] = abr, abi, cfr, cfi
        pr, pi = abr, abi
        for _ in range(int(math.log2(SSM_L))):
            pr, pi = pr * pr - pi * pi, 2.0 * pr * pi
        alr_ref[...], ali_ref[...] = pr, pi

    return pl.pallas_call(body, name="ssm_params_fwd", out_shape=[shp] * 6)(a_re, a_im, log_dt)


def _ssm_params_bwd(a_re, a_im, log_dt, d_abr, d_abi, d_cfr, d_cfi):
    def body(are_ref, aim_ref, ldt_ref, g0, g1, g2, g3, dare_ref, daim_ref, dldt_ref):
        _, vjp = jax.vjp(_ssm_discretise, are_ref[...], aim_ref[...], ldt_ref[...])
        dare_ref[...], daim_ref[...], dldt_ref[...] = vjp((g0[...], g1[...], g2[...], g3[...]))

    return pl.pallas_call(
        body, name="ssm_params_bwd",
        out_shape=[jax.ShapeDtypeStruct(a_re.shape, F32), jax.ShapeDtypeStruct(a_im.shape, F32),
                   jax.ShapeDtypeStruct(log_dt.shape, F32)],
    )(a_re, a_im, log_dt, d_abr, d_abi, d_cfr, d_cfi)


def _scan_cols(j):
    return pl.ds(j * SSM_SB, SSM_SB)


def _rows8(r):
    return pl.ds(pl.multiple_of(r * SUBLANES, SUBLANES), SUBLANES)


def _bcast8(row):
    return jnp.broadcast_to(row, (SUBLANES, row.shape[-1]))


def _token_order_pick():
    tok = lax.broadcasted_iota(jnp.int32, (SSM_T, SSM_T), 0)
    row = lax.broadcasted_iota(jnp.int32, (SSM_T, SSM_T), 1)
    return (row == SUBLANES * (tok % SSM_L) + tok // SSM_L).astype(_MXU)


SCAN_UNROLL = 16


def _scan_loop(n, step, init):
    def trip(o, carry):
        for i in range(SCAN_UNROLL):
            carry = step(o * SCAN_UNROLL + i, carry)
        return carry

    return lax.fori_loop(0, n // SCAN_UNROLL, trip, init)


def _ssm_fwd(u, b_re, b_im, c_re, c_im, d_skip, coef):
    seq = u.shape[0]
    nc = seq // SSM_T
    T, L = SSM_T, SSM_L

    def body(u_ref, bre_ref, bim_ref, cre_ref, cim_ref, d_ref, are_ref, aim_ref, cfr_ref, cfi_ref, alr_ref, ali_ref,
             y_ref, yg_ref, sre_ref, sim_ref, ire_ref, iim_ref, car_re, car_im, end_re, end_im, yg_scan):
        c = pl.program_id(0)

        @pl.when(c == 0)
        def _():
            car_re[...] = jnp.zeros_like(car_re)
            car_im[...] = jnp.zeros_like(car_im)

        for j in range(SSM_JB):
            ub = u_ref[:, j * LANES:(j + 1) * LANES].astype(_MXU)
            bur = jnp.dot(ub, bre_ref[j], preferred_element_type=F32)
            bui = jnp.dot(ub, bim_ref[j], preferred_element_type=F32)
            cfr, cfi = cfr_ref[:, _scan_cols(j)], cfi_ref[:, _scan_cols(j)]
            sre_ref[:, _scan_cols(j)] = cfr * bur - cfi * bui
            sim_ref[:, _scan_cols(j)] = cfr * bui + cfi * bur

        for j in range(SSM_JB):
            cols = _scan_cols(j)
            ar, ai = _bcast8(are_ref[:, cols]), _bcast8(aim_ref[:, cols])

            def step1(r, s, cols=cols, ar=ar, ai=ai):
                sr, si = s
                rows = _rows8(r)
                return (ar * sr - ai * si + sre_ref[rows, cols], ar * si + ai * sr + sim_ref[rows, cols])

            zero = jnp.zeros((SUBLANES, SSM_SB), F32)
            er, ei = _scan_loop(L, step1, (zero, zero))
            end_re[:, cols] = er
            end_im[:, cols] = ei

        alr, ali = alr_ref[...], ali_ref[...]
        cr, ci = car_re[...], car_im[...]
        ire_ref[0:1, :] = cr
        iim_ref[0:1, :] = ci
        for i in range(1, SUBLANES):
            er, ei = end_re[i - 1:i, :], end_im[i - 1:i, :]
            cr, ci = alr * cr - ali * ci + er, alr * ci + ali * cr + ei
            ire_ref[i:i + 1, :] = cr
            iim_ref[i:i + 1, :] = ci

        for j in range(SSM_JB):
            cols = _scan_cols(j)
            ar, ai = _bcast8(are_ref[:, cols]), _bcast8(aim_ref[:, cols])

            def step2(r, s, cols=cols, ar=ar, ai=ai):
                sr, si = s
                rows = _rows8(r)
                nr = ar * sr - ai * si + sre_ref[rows, cols]
                ni = ar * si + ai * sr + sim_ref[rows, cols]
                sre_ref[rows, cols] = nr
                sim_ref[rows, cols] = ni
                return nr, ni

            _scan_loop(L, step2, (ire_ref[:, cols], iim_ref[:, cols]))

        car_re[...] = sre_ref[T - 1:T, :]
        car_im[...] = sim_ref[T - 1:T, :]

        for j in range(SSM_JB):
            cols = _scan_cols(j)
            ch = slice(j * LANES, (j + 1) * LANES)
            y = (jnp.dot(sre_ref[:, cols].astype(_MXU), cre_ref[j], preferred_element_type=F32)
                 - jnp.dot(sim_ref[:, cols].astype(_MXU), cim_ref[j], preferred_element_type=F32))
            y = y + d_ref[:, ch] * u_ref[:, ch].astype(F32)
            y_ref[:, ch] = y
            yg_scan[:, ch] = jax.nn.gelu(y).astype(yg_scan.dtype)
        yg_ref[...] = jnp.dot(_token_order_pick(), yg_scan[...], preferred_element_type=F32).astype(yg_ref.dtype)

    tok = pl.BlockSpec((T, SSM_W), lambda c: (c, 0))
    st = pl.BlockSpec((T, N_STATES), lambda c: (c, 0))
    ini = pl.BlockSpec((None, SUBLANES, N_STATES), lambda c: (c, 0, 0))
    bsp = pl.BlockSpec((SSM_JB, LANES, SSM_SB), lambda c: (0, 0, 0))
    csp = pl.BlockSpec((SSM_JB, SSM_SB, LANES), lambda c: (0, 0, 0))
    row_w = pl.BlockSpec((1, SSM_W), lambda c: (0, 0))
    row_s = pl.BlockSpec((1, N_STATES), lambda c: (0, 0))
    return pl.pallas_call(
        body, name="ssm_fwd", grid=(nc,),
        in_specs=[tok, bsp, bsp, csp, csp, row_w] + [row_s] * 6,
        out_specs=[tok, tok, st, st, ini, ini],
        out_shape=[jax.ShapeDtypeStruct((seq, SSM_W), F32), jax.ShapeDtypeStruct((seq, SSM_W), _MXU),
                   jax.ShapeDtypeStruct((seq, N_STATES), F32), jax.ShapeDtypeStruct((seq, N_STATES), F32),
                   jax.ShapeDtypeStruct((nc, SUBLANES, N_STATES), F32),
                   jax.ShapeDtypeStruct((nc, SUBLANES, N_STATES), F32)],
        scratch_shapes=[pltpu.VMEM((1, N_STATES), F32), pltpu.VMEM((1, N_STATES), F32),
                        pltpu.VMEM((SUBLANES, N_STATES), F32), pltpu.VMEM((SUBLANES, N_STATES), F32),
                        pltpu.VMEM((T, SSM_W), _MXU)],
        compiler_params=_params(("arbitrary",)),
    )(u, b_re, b_im, c_re, c_im, d_skip, *coef)


def _ssm_bwd(dyg, y, u, s_re, s_im, i_re, i_im, b_re, b_im, c_re, c_im, d_skip, coef, d_proj, deps=()):
    seq = u.shape[0]
    nc = seq // SSM_T
    T, L = SSM_T, SSM_L
    deps = list(deps) + [d_proj]

    def body(dyg_ref, y_ref, u_ref, sre_ref, sim_ref, ire_ref, iim_ref, bre_ref, bim_ref, cre_ref, cim_ref, d_ref,
             are_ref, aim_ref, cfr_ref, cfi_ref, alr_ref, ali_ref, *rest):
        (du_ref, dbre_out, dbim_out, dcre_out, dcim_out, dd_ref, dar_ref, dai_ref, dcfr_ref, dcfi_ref,
         lre, lim, car_re, car_im, end_re, end_im, ini_re, ini_im, dbre_ref, dbim_ref, dcre_ref, dcim_ref,
         dy_ref, du_scan) = rest[len(deps):]
        step = pl.program_id(0)
        dy_ref[...] = jax.vjp(jax.nn.gelu, y_ref[...])[1](dyg_ref[...])[0]

        @pl.when(step == 0)
        def _():
            car_re[...] = jnp.zeros_like(car_re)
            car_im[...] = jnp.zeros_like(car_im)
            for ref in (dbre_ref, dbim_ref, dcre_ref, dcim_ref, dd_ref, dar_ref, dai_ref, dcfr_ref, dcfi_ref):
                ref[...] = jnp.zeros_like(ref)

        for j in range(SSM_JB):
            dyb = dy_ref[:, j * LANES:(j + 1) * LANES].astype(_MXU)
            lre[:, _scan_cols(j)] = lax.dot_general(dyb, cre_ref[j], _NT, preferred_element_type=F32)
            lim[:, _scan_cols(j)] = -lax.dot_general(dyb, cim_ref[j], _NT, preferred_element_type=F32)

        for j in range(SSM_JB):
            cols = _scan_cols(j)
            ar, ai = _bcast8(are_ref[:, cols]), _bcast8(aim_ref[:, cols])

            def step1(t, s, cols=cols, ar=ar, ai=ai):
                sr, si = s
                rows = _rows8(L - 1 - t)
                return (ar * sr + ai * si + lre[rows, cols], ar * si - ai * sr + lim[rows, cols])

            zero = jnp.zeros((SUBLANES, SSM_SB), F32)
            er, ei = _scan_loop(L, step1, (zero, zero))
            end_re[:, cols] = er
            end_im[:, cols] = ei

        alr, ali = alr_ref[...], ali_ref[...]
        cr, ci = car_re[...], car_im[...]
        ini_re[SUBLANES - 1:SUBLANES, :] = cr
        ini_im[SUBLANES - 1:SUBLANES, :] = ci
        for i in range(SUBLANES - 2, -1, -1):
            er, ei = end_re[i + 1:i + 2, :], end_im[i + 1:i + 2, :]
            cr, ci = alr * cr + ali * ci + er, alr * ci - ali * cr + ei
            ini_re[i:i + 1, :] = cr
            ini_im[i:i + 1, :] = ci

        for j in range(SSM_JB):
            cols = _scan_cols(j)
            ar, ai = _bcast8(are_ref[:, cols]), _bcast8(aim_ref[:, cols])

            def step2(t, s, cols=cols, ar=ar, ai=ai):
                sr, si = s
                rows = _rows8(L - 1 - t)
                nr = ar * sr + ai * si + lre[rows, cols]
                ni = ar * si - ai * sr + lim[rows, cols]
                lre[rows, cols] = nr
                lim[rows, cols] = ni
                return nr, ni

            _scan_loop(L, step2, (ini_re[:, cols], ini_im[:, cols]))

        car_re[...] = lre[0:1, :]
        car_im[...] = lim[0:1, :]

        head, tail, body_rows = slice(0, SUBLANES), slice(SUBLANES, T), slice(0, T - SUBLANES)
        for j in range(SSM_JB):
            cols = _scan_cols(j)
            ch = slice(j * LANES, (j + 1) * LANES)
            lr, li = lre[:, cols], lim[:, cols]
            lt_r, lt_i, sp_r, sp_i = lre[tail, cols], lim[tail, cols], sre_ref[body_rows, cols], sim_ref[body_rows, cols]
            lh_r, lh_i, si_r, si_i = lre[head, cols], lim[head, cols], ire_ref[:, cols], iim_ref[:, cols]
            dar_ref[:, cols] += _colsum(lt_r * sp_r + lt_i * sp_i) + _colsum(lh_r * si_r + lh_i * si_i)
            dai_ref[:, cols] += _colsum(lt_i * sp_r - lt_r * sp_i) + _colsum(lh_i * si_r - lh_r * si_i)
            ub = u_ref[:, ch].astype(_MXU)
            uf = ub.astype(F32)
            bur = jnp.dot(ub, bre_ref[j], preferred_element_type=F32)
            bui = jnp.dot(ub, bim_ref[j], preferred_element_type=F32)
            dcfr_ref[:, cols] += _colsum(lr * bur + li * bui)
            dcfi_ref[:, cols] += _colsum(li * bur - lr * bui)
            cfr, cfi = cfr_ref[:, cols], cfi_ref[:, cols]
            dbur = (cfr * lr + cfi * li).astype(_MXU)
            dbui = (cfr * li - cfi * lr).astype(_MXU)
            dyf = dy_ref[:, ch]
            dyb = dyf.astype(_MXU)
            du = (lax.dot_general(dbur, bre_ref[j], _NT, preferred_element_type=F32)
                  + lax.dot_general(dbui, bim_ref[j], _NT, preferred_element_type=F32) + d_ref[:, ch] * dyf)
            du_scan[:, ch] = du.astype(du_scan.dtype)
            dbre_ref[j] += lax.dot_general(ub, dbur, _TN, preferred_element_type=F32)
            dbim_ref[j] += lax.dot_general(ub, dbui, _TN, preferred_element_type=F32)
            dcre_ref[j] += lax.dot_general(sre_ref[:, cols].astype(_MXU), dyb, _TN, preferred_element_type=F32)
            dcim_ref[j] -= lax.dot_general(sim_ref[:, cols].astype(_MXU), dyb, _TN, preferred_element_type=F32)
            dd_ref[:, ch] += _colsum(dyf * uf)
        du_ref[...] = jnp.dot(_token_order_pick(), du_scan[...], preferred_element_type=F32).astype(du_ref.dtype)

        @pl.when(step == nc - 1)
        def _():
            for acc, out in ((dbre_ref, dbre_out), (dbim_ref, dbim_out), (dcre_ref, dcre_out), (dcim_ref, dcim_out)):
                pltpu.sync_copy(acc, out)

    tok = pl.BlockSpec((T, SSM_W), lambda c: (nc - 1 - c, 0))
    st = pl.BlockSpec((T, N_STATES), lambda c: (nc - 1 - c, 0))
    ini = pl.BlockSpec((None, SUBLANES, N_STATES), lambda c: (nc - 1 - c, 0, 0))
    bsp = pl.BlockSpec((SSM_JB, LANES, SSM_SB), lambda c: (0, 0, 0))
    csp = pl.BlockSpec((SSM_JB, SSM_SB, LANES), lambda c: (0, 0, 0))
    row_w = pl.BlockSpec((1, SSM_W), lambda c: (0, 0))
    row_s = pl.BlockSpec((1, N_STATES), lambda c: (0, 0))
    big = pltpu.VMEM((T, N_STATES), F32)
    one = pltpu.VMEM((1, N_STATES), F32)
    eight = pltpu.VMEM((SUBLANES, N_STATES), F32)
    return pl.pallas_call(
        body, name="ssm_bwd", grid=(nc,),
        in_specs=[tok, tok, tok, st, st, ini, ini, bsp, bsp, csp, csp, row_w] + [row_s] * 6 + [_ANY] * len(deps),
        out_specs=[pl.BlockSpec((pl.Element(T), pl.Element(SSM_W)), lambda c: ((nc - 1 - c) * T, OFF_U * CW)),
                   _ANY, _ANY, _ANY, _ANY, row_w, row_s, row_s, row_s, row_s],
        input_output_aliases={18 + len(deps) - 1: 0},
        out_shape=[jax.ShapeDtypeStruct(d_proj.shape, d_proj.dtype),
                   jax.ShapeDtypeStruct((SSM_JB, LANES, SSM_SB), F32), jax.ShapeDtypeStruct((SSM_JB, LANES, SSM_SB), F32),
                   jax.ShapeDtypeStruct((SSM_JB, SSM_SB, LANES), F32), jax.ShapeDtypeStruct((SSM_JB, SSM_SB, LANES), F32),
                   jax.ShapeDtypeStruct((1, SSM_W), F32)] + [jax.ShapeDtypeStruct((1, N_STATES), F32)] * 4,
        scratch_shapes=[big, big, one, one, eight, eight, eight, eight,
                        pltpu.VMEM((SSM_JB, LANES, SSM_SB), F32), pltpu.VMEM((SSM_JB, LANES, SSM_SB), F32),
                        pltpu.VMEM((SSM_JB, SSM_SB, LANES), F32), pltpu.VMEM((SSM_JB, SSM_SB, LANES), F32),
                        pltpu.VMEM((T, SSM_W), F32), pltpu.VMEM((T, SSM_W), _MXU)],
        compiler_params=_params(("arbitrary",)),
    )(dyg, y, u, s_re, s_im, i_re, i_im, b_re, b_im, c_re, c_im, d_skip, *coef, *deps)


def _block_diag_b(b):
    t = b.reshape(SSM_JB, 8, STATE, GROUP).transpose(0, 1, 3, 2)
    eye = jnp.eye(8, dtype=b.dtype)
    return (t[:, :, :, None, :] * eye[None, :, None, :, None]).reshape(SSM_JB, LANES, SSM_SB)


def _block_diag_c(c):
    t = c.reshape(SSM_JB, 8, GROUP, STATE).transpose(0, 1, 3, 2)
    eye = jnp.eye(8, dtype=c.dtype)
    return (t[:, :, :, None, :] * eye[None, :, None, :, None]).reshape(SSM_JB, SSM_SB, LANES)


def _diag_of_b(blk):
    t = blk.reshape(SSM_JB, 8, GROUP, 8, STATE)
    d = jnp.sum(t * jnp.eye(8, dtype=blk.dtype)[None, :, None, :, None], axis=3)
    return d.transpose(0, 1, 3, 2).reshape(N_GROUPS, STATE, GROUP)


def _diag_of_c(blk):
    t = blk.reshape(SSM_JB, 8, STATE, 8, GROUP)
    d = jnp.sum(t * jnp.eye(8, dtype=blk.dtype)[None, :, None, :, None], axis=3)
    return d.transpose(0, 1, 3, 2).reshape(N_GROUPS, GROUP, STATE)


def _to_scan_order(v):
    seq, w = v.shape
    return v.reshape(seq // SSM_T, SUBLANES, SSM_L, w).transpose(0, 2, 1, 3).reshape(seq, w)


def _adamw_math(w, g, m, v):
    nm = ADAM_B1 * m + (1.0 - ADAM_B1) * g
    nv = ADAM_B2 * v + (1.0 - ADAM_B2) * jnp.square(g)
    m_hat = nm / (1.0 - ADAM_B1 ** ADAM_STEP)
    v_hat = nv / (1.0 - ADAM_B2 ** ADAM_STEP)
    return -ADAM_LR * (m_hat / (jnp.sqrt(v_hat) + ADAM_EPS) + ADAM_WD * w), nm, nv


def _adamw(w, g, m, v, *, name, tm, deps=()):
    rows, cols = w.shape
    nd = len(deps)

    def body(w_ref, g_ref, m_ref, v_ref, *rest):
        d_ref, nm_ref, nv_ref, g_out_ref = rest[nd:]
        g = g_ref[...]
        d_ref[...], nm_ref[...], nv_ref[...] = _adamw_math(w_ref[...], g, m_ref[...], v_ref[...])
        g_out_ref[...] = g

    spec = pl.BlockSpec((tm, cols), lambda i: (i, 0))
    shp = jax.ShapeDtypeStruct((rows, cols), F32)
    return pl.pallas_call(body, name=name, grid=(rows // tm,), in_specs=[spec] * 4 + [_ANY] * nd,
                          out_specs=[spec] * 4, out_shape=[shp] * 4,
                          compiler_params=_params(("arbitrary",)))(w, g, m, v, *deps)


def _place():
    x, y, c = lax.axis_index("x"), lax.axis_index("y"), lax.axis_index("c")
    chips = [(1 - x, y), (x, 1 - y), (1 - x, 1 - y)]
    return x, y, c, chips


def _remote(src, dst, send_sem, recv_sem, dev):
    return pltpu.make_async_remote_copy(src_ref=src, dst_ref=dst, send_sem=send_sem, recv_sem=recv_sem,
                                        device_id=dev, device_id_type=MESH)


def _place_shard(w, mine_arr, *, name, tm=256, deps=()):
    rows, cols = w.shape

    def body(m_ref, w_ref, *rest):
        rest[-1][...] = w_ref[...].astype(rest[-1].dtype)

    return pl.pallas_call(
        body, name=name,
        grid_spec=pltpu.PrefetchScalarGridSpec(
            num_scalar_prefetch=1, grid=(rows // tm,),
            in_specs=[pl.BlockSpec((tm, cols), lambda i, m: (i, 0))] + [_ANY] * len(deps),
            out_specs=pl.BlockSpec((None, tm, cols), lambda i, m: (m[0], i, 0))),
        out_shape=jax.ShapeDtypeStruct((N_CHIPS, rows, cols), _WIRE),
        compiler_params=_params(("arbitrary",)),
    )(mine_arr, w, *deps)


_HBM = pl.BlockSpec(memory_space=pltpu.HBM)
_SEM = pl.BlockSpec(memory_space=pltpu.SEMAPHORE)
_EFFECT = pltpu.SideEffectType.DATAFLOW_SIDE_EFFECTING


def _copies_start(name, bufs, plan, count, after=()):
    nb, na = len(bufs), len(after)

    def body(*refs):
        send_sems, recv_sems, token = refs[nb + na], refs[nb + na + 1], refs[-1]
        copies = plan(refs[:nb])
        assert len(copies) == count
        for i, (src, dst, dev, _) in enumerate(copies):
            _remote(src, dst, send_sems.at[i], recv_sems.at[i], dev).start()
        token[...] = jnp.zeros_like(token)

    res = pl.pallas_call(
        body, name=name, in_specs=[_HBM] * nb + [_ANY] * na,
        out_specs=(_SEM, _SEM, *[_HBM] * nb, pl.BlockSpec(memory_space=pltpu.VMEM)),
        out_shape=(pltpu.SemaphoreType.DMA((count,)), pltpu.SemaphoreType.DMA((count,)),
                   *[pltpu.HBM(b.shape, b.dtype) for b in bufs], jax.ShapeDtypeStruct((SUBLANES, LANES), F32)),
        input_output_aliases={i: 2 + i for i in range(nb)},
        compiler_params=pltpu.CompilerParams(has_side_effects=_EFFECT),
    )(*[pltpu.with_memory_space_constraint(b, pltpu.HBM) for b in bufs], *after)
    return (res[0], res[1]), list(res[2:2 + nb]), res[-1]


def _copies_wait(name, bufs, sems, plan, after=(), which=None):
    nb, na = len(bufs), len(after)

    def body(*refs):
        send_sems, recv_sems = refs[nb], refs[nb + 1]
        for i, (src, _, dev, land) in enumerate(plan(refs[:nb])):
            if which is not None and i not in which:
                continue
            cp = _remote(src, land, send_sems.at[i], recv_sems.at[i], dev)
            cp.wait_send()
            cp.wait_recv()

    res = pl.pallas_call(
        body, name=name, in_specs=[_HBM] * nb + [_SEM, _SEM] + [_ANY] * na, out_specs=[_HBM] * nb,
        out_shape=[pltpu.HBM(b.shape, b.dtype) for b in bufs],
        input_output_aliases={i: i for i in range(nb)},
        compiler_params=pltpu.CompilerParams(has_side_effects=_EFFECT),
    )(*bufs, *sems, *after)
    return list(res)


def _plan_gather_ici(fulls, which=(0, 1, 2)):
    x, y, c, chips = _place()
    copies = []
    for f in fulls:
        half = pl.ds(c * (f.shape[1] // 2), f.shape[1] // 2)
        own = f.at[2 * x + y, half]
        for chip in [chips[k] for k in which]:
            copies.append((own, own, (*chip, c), f.at[2 * chip[0] + chip[1], half]))
    return copies


def _plan_gather_d2d(fulls, which=(0, 1, 2)):
    x, y, c, chips = _place()
    copies = []
    for f in fulls:
        r2 = f.shape[1] // 2
        for chip in [chips[k] for k in which]:
            blk = 2 * chip[0] + chip[1]
            landed = f.at[blk, pl.ds(c * r2, r2)]
            copies.append((landed, landed, (x, y, 1 - c), f.at[blk, pl.ds((1 - c) * r2, r2)]))
    return copies


def _plan_relay_direct(fulls):
    (f,) = fulls
    x, y, c, chips = _place()
    half = pl.ds(c * (f.shape[1] // 2), f.shape[1] // 2)
    own = f.at[2 * x + y, half]
    return [(own, own, (*chip, c), f.at[2 * chip[0] + chip[1], half]) for chip in chips[:2]]


def _plan_relay_forward(fulls, k):
    (f,) = fulls
    x, y, c, chips = _place()
    r2 = f.shape[1] // 2
    half, other = pl.ds(c * r2, r2), pl.ds((1 - c) * r2, r2)
    quarter = pl.ds(c * r2 + k * (r2 // 2), r2 // 2)
    blk, far = 2 * chips[k][0] + chips[k][1], 2 * chips[2][0] + chips[2][1]
    passed, landed = f.at[blk, quarter], f.at[blk, half]
    return [(passed, passed, (*chips[1 - k], c), f.at[far, quarter]), (landed, landed, (x, y, 1 - c), f.at[blk, other])]


def _plan_relay_last(fulls):
    (f,) = fulls
    x, y, c, chips = _place()
    r2 = f.shape[1] // 2
    far = 2 * chips[2][0] + chips[2][1]
    landed = f.at[far, pl.ds(c * r2, r2)]
    return [(landed, landed, (x, y, 1 - c), f.at[far, pl.ds((1 - c) * r2, r2)])]


def _plan_swap_halves(refs):
    x, y, c, _ = _place()
    n = len(refs) // 2
    copies = []
    for g, land in zip(refs[:n], refs[n:]):
        r2 = g.shape[1] // 2
        copies.append((g.at[:, pl.ds((1 - c) * r2, r2), :], land, (x, y, 1 - c), land))
    return copies


def _plan_scatter_chips(refs):
    x, y, c, chips = _place()
    n = len(refs) // 2
    copies = []
    for h, land in zip(refs[:n], refs[n:]):
        for k, chip in enumerate(chips):
            copies.append((h.at[2 * chip[0] + chip[1]], land.at[k], (*chip, c), land.at[k]))
    return copies


def _plan_join_halves(totals):
    x, y, c, _ = _place()
    copies = []
    for t in totals:
        r2 = t.shape[0] // 2
        mine = t.at[pl.ds(c * r2, r2)]
        copies.append((mine, mine, (x, y, 1 - c), t.at[pl.ds((1 - c) * r2, r2)]))
    return copies


def _add_sibling_half(g, got, c_arr, *, name, tm):
    _, rows, cols = g.shape
    r2 = rows // 2
    nb = r2 // tm

    def body(c_ref, g_ref, r_ref, o_ref):
        o_ref[...] = (g_ref[...].astype(F32) + r_ref[...].astype(F32)).astype(o_ref.dtype)

    return pl.pallas_call(
        body, name=name,
        grid_spec=pltpu.PrefetchScalarGridSpec(
            num_scalar_prefetch=1, grid=(N_CHIPS, nb),
            in_specs=[pl.BlockSpec((None, tm, cols), lambda b, i, c: (b, c[0] * nb + i, 0)),
                      pl.BlockSpec((None, tm, cols), lambda b, i, c: (b, i, 0))],
            out_specs=pl.BlockSpec((None, tm, cols), lambda b, i, c: (b, i, 0))),
        out_shape=jax.ShapeDtypeStruct((N_CHIPS, r2, cols), _WIRE),
        compiler_params=_params(("arbitrary", "arbitrary")),
    )(c_arr, g, got)


def _add_chips(h, got, place_arr, *, name, tm):
    _, r2, cols = h.shape
    nb = r2 // tm

    def body(p_ref, h_ref, r_ref, o_ref):
        o_ref[...] = ((h_ref[...].astype(F32) + r_ref[0].astype(F32)) + r_ref[1].astype(F32)) + r_ref[2].astype(F32)

    return pl.pallas_call(
        body, name=name,
        grid_spec=pltpu.PrefetchScalarGridSpec(
            num_scalar_prefetch=1, grid=(nb,),
            in_specs=[pl.BlockSpec((None, tm, cols), lambda i, p: (p[0], i, 0)),
                      pl.BlockSpec((3, tm, cols), lambda i, p: (0, i, 0))],
            out_specs=pl.BlockSpec((tm, cols), lambda i, p: (p[1] * nb + i, 0))),
        out_shape=jax.ShapeDtypeStruct((2 * r2, cols), F32),
        compiler_params=_params(("arbitrary",)),
    )(place_arr, h, got)


class _ReduceScatter:
    def __init__(self, tag, names, grads):
        self.tag, self.names, self.n = tag, names, len(names)
        core = lax.axis_index("c").astype(jnp.int32)
        chip = (2 * lax.axis_index("x") + lax.axis_index("y")).astype(jnp.int32)
        self.c_arr, self.place_arr = core.reshape(1), jnp.stack([chip, core])
        self.bufs = list(grads)

    def _start(self, step, bufs, plan, count, after):
        self.plan = plan
        self.step = f"grad_{step}_{self.tag}"
        self.sems, self.bufs, token = _copies_start(self.step + "_start", bufs, plan, count, after)
        return [token]

    def _wait(self, after):
        self.bufs = _copies_wait(self.step + "_wait", self.bufs, self.sems, self.plan, after)
        return self.bufs

    def start_swap(self, after=()):
        lands = [lax.empty((N_CHIPS, g.shape[1] // 2, g.shape[2]), g.dtype) for g in self.bufs]
        return self._start("swap", self.bufs + lands, _plan_swap_halves, self.n, after)

    def start_scatter(self, after):
        bufs = self._wait(after)
        pair = [_add_sibling_half(g, r, self.c_arr, name=f"grad_add_sibling_{nm}", tm=min(256, g.shape[1] // 2))
                for nm, g, r in zip(self.names, bufs[:self.n], bufs[self.n:])]
        lands = [lax.empty((3,) + h.shape[1:], h.dtype) for h in pair]
        return self._start("scatter", pair + lands, _plan_scatter_chips, 3 * self.n, ())

    def start_join(self, after):
        bufs = self._wait(after)
        total = [_add_chips(h, r, self.place_arr, name=f"grad_add_chips_{nm}", tm=min(256, h.shape[1]))
                 for nm, h, r in zip(self.names, bufs[:self.n], bufs[self.n:])]
        return self._start("join", total, _plan_join_halves, self.n, ())

    def finish(self, after):
        return dict(zip(self.names, self._wait(after)))


def _all_gather_small(v):
    m_per, n = v.shape

    def body(x_ref, out_ref, send_sems, recv_sems, local_sem):
        x, y, c, chips = _place()
        me, sibling = (x, y, c), (x, y, 1 - c)

        def rows(px, py, pc):
            return out_ref.at[4 * px + 2 * py + pc]

        def copy(k, block, to, src=None):
            return _remote(rows(*block) if src is None else src, rows(*block), send_sems.at[k], recv_sems.at[k], to)

        mine = pltpu.make_async_copy(x_ref, rows(*me), local_sem)
        mine.start()
        first = [copy(0, me, sibling, src=x_ref)]
        first += [copy(1 + j, me, (*chip, c), src=x_ref) for j, chip in enumerate(chips)]
        for cp in first:
            cp.start()
        passed = [copy(4 + j, (*chip, c), sibling) for j, chip in enumerate(chips)]
        for j, chip in enumerate(chips):
            copy(1 + j, (*chip, c), me).wait_recv()
            passed[j].start()
        copy(0, sibling, me).wait_recv()
        for j, chip in enumerate(chips):
            copy(4 + j, (*chip, 1 - c), me).wait_recv()
        for cp in first + passed:
            cp.wait_send()
        mine.wait()

    return pl.pallas_call(
        body, name="gather_small_grads",
        out_shape=jax.ShapeDtypeStruct((8, m_per, n), v.dtype),
        in_specs=[pl.BlockSpec(memory_space=pltpu.VMEM)], out_specs=pl.BlockSpec(memory_space=pltpu.VMEM),
        scratch_shapes=[pltpu.SemaphoreType.DMA((7,)), pltpu.SemaphoreType.DMA((7,)), pltpu.SemaphoreType.DMA],
        compiler_params=pltpu.CompilerParams(vmem_limit_bytes=VMEM_LIMIT),
    )(v)


def _sum8(v, *, name):
    _, m, n = v.shape

    def body(v_ref, o_ref):
        acc = v_ref[0]
        for d in range(1, 8):
            acc = acc + v_ref[d]
        o_ref[...] = acc

    return pl.pallas_call(body, name=name, out_shape=jax.ShapeDtypeStruct((m, n), F32),
                          compiler_params=pltpu.CompilerParams(vmem_limit_bytes=VMEM_LIMIT))(v)


def _local_step(x, target, norm_w, q_norm_w, k_norm_w, sinks, a_re, a_im, log_dt, b_re, b_im, c_re, c_im, d_skip,
                b_glu, io):
    seq = x.shape[0]
    qw2 = jnp.tile(q_norm_w.reshape(1, HEAD_DIM), (1, HEADS_PER_TILE))
    kw2 = jnp.tile(k_norm_w.reshape(1, HEAD_DIM), (1, HEADS_PER_TILE))
    nw, bg = norm_w.reshape(1, D_MODEL), b_glu.reshape(1, D_MODEL)
    dsk = d_skip.reshape(1, SSM_W)

    h, rstd = _rms_fwd(x, nw, deps=io.begin())
    proj, w_in4 = io.projection(h)
    attn, lse, ya_in = _attn2_fwd(proj, qw2, kw2, sinks, deps=io.after_proj(proj))
    w_ap4 = io.weight("w_attn_proj", ya_in)
    w_glu4, w_sp4, w_out = io.weight("w_glu", ya_in), io.weight("w_ssm_proj", ya_in), io.weight("w_out", ya_in)
    y_a = _mm(ya_in, w_ap4, mode="nn", name="mm_attn_proj", tm=2048, tn=512, tk=ATTN_W, b_blocked=True,
              rows_outer=True, out_dtype=_MXU)

    flat_a = (a_re.reshape(1, N_STATES), a_im.reshape(1, N_STATES), jnp.repeat(log_dt, STATE).reshape(1, N_STATES))
    coef = _ssm_params_fwd(*flat_a)
    bre_blk, bim_blk = _block_diag_b(b_re).astype(_MXU), _block_diag_b(b_im).astype(_MXU)
    cre_blk, cim_blk = _block_diag_c(c_re).astype(_MXU), _block_diag_c(c_im).astype(_MXU)
    u_scan = _to_scan_order(proj[:, OFF_U * CW:OFF_U * CW + SSM_W])
    y_scan, yg, s_re, s_im, i_re, i_im = _ssm_fwd(u_scan, bre_blk, bim_blk, cre_blk, cim_blk, dsk, coef)
    glu, ys_in = _mm_glu_gate(yg, w_glu4, bg, proj)
    y_s = _mm(ys_in, w_sp4, mode="nn", name="mm_ssm_proj", tm=2048, tn=512, tk=SSM_W, b_blocked=True,
              rows_outer=True, out_dtype=_MXU)

    merged, dout, dout_b, sq = _mm_merge_out_loss(proj, y_a, y_s, w_out, x, target)
    loss = 0.5 * jnp.sum(sq) / D_MODEL

    d_ya, d_ys, d_proj = _mm_merge_bwd(dout_b, w_out, proj, y_a, y_s)
    g_w_out = _mm(merged, dout_b, mode="tn", name="mm_g_w_out", tm=1024, tn=D_MODEL, tk=2048, out_dtype=_WIRE)

    d_ya_in = _mm(d_ya, w_ap4, mode="nt", name="mm_d_attn_gate", tm=2048, tn=ATTN_W, tk=512, b_blocked=True)
    g_w_ap = _mm(ya_in, d_ya, mode="tn", name="mm_g_w_attn_proj", tm=ATTN_W, tn=D_MODEL, tk=2048, out_dtype=_WIRE,
                 out_blocked=True)

    g_w_sp = _mm(ys_in, d_ys, mode="tn", name="mm_g_w_ssm_proj", tm=SSM_W, tn=D_MODEL, tk=2048, out_dtype=_WIRE,
                 out_blocked=True)
    d_glu, d_proj, g_bglu = _mm_ssm_gate_bwd(d_ys, w_sp4, glu, bg, proj, d_proj)
    d_yg = _mm(d_glu, w_glu4, mode="nt", name="mm_d_gelu", tm=2048, tn=SSM_W, tk=512, b_blocked=True)
    g_w_glu = _mm(yg, d_glu, mode="tn", name="mm_g_w_glu", tm=SSM_W, tn=D_MODEL, tk=2048, out_dtype=_WIRE, out_blocked=True)
    dep = io.later_grads(dict(w_attn_proj=g_w_ap, w_glu=g_w_glu, w_ssm_proj=g_w_sp,
                              w_out=g_w_out.reshape(N_CHIPS, D_MODEL // N_CHIPS, D_MODEL)))

    d_proj, g_qw2, g_kw2, g_sk = _attn2_bwd(proj, qw2, kw2, sinks, lse, attn, d_ya_in, d_proj, deps=dep)
    dep = io.before_scan_backward([d_proj])
    (d_proj, g_bre, g_bim, g_cre, g_cim, g_dsk, g_abr, g_abi, g_cfr, g_cfi) = _ssm_bwd(
        _to_scan_order(d_yg), y_scan, u_scan, s_re, s_im, i_re, i_im, bre_blk, bim_blk, cre_blk, cim_blk, dsk, coef,
        d_proj, deps=dep)
    g_are, g_aim, g_ldt = _ssm_params_bwd(*flat_a, g_abr, g_abi, g_cfr, g_cfi)
    g_are, g_aim = g_are.reshape(N_GROUPS, STATE), g_aim.reshape(N_GROUPS, STATE)
    g_ldt = g_ldt.reshape(N_GROUPS, STATE).sum(axis=1)
    dep = io.before_input_projection_grad([d_proj]) + io.small_grads(dict(
        q_norm_w=g_qw2[0, :HEAD_DIM] + g_qw2[0, HEAD_DIM:], k_norm_w=g_kw2[0, :HEAD_DIM] + g_kw2[0, HEAD_DIM:],
        sinks=g_sk.reshape(N_Q_HEADS), A_re=g_are, A_im=g_aim, log_dt=g_ldt,
        B_re=_diag_of_b(g_bre), B_im=_diag_of_b(g_bim), C_re=_diag_of_c(g_cre), C_im=_diag_of_c(g_cim),
        D_skip=g_dsk.reshape(N_GROUPS, GROUP), b_glu=g_bglu.reshape(D_MODEL)))
    g_w_in = _mm(h, d_proj, mode="tn", name="mm_g_w_in", tm=1024, tn=IN_W // 4, tk=2048, out_dtype=_WIRE,
                 out_blocked=True, deps=dep)
    dep = io.input_projection_grad(g_w_in)
    d_h = _mm(d_proj, w_in4, mode="nt", name="mm_d_h", tm=1024, tn=D_MODEL, tk=IN_W // 4, b_blocked=True, deps=dep)
    grad_x, g_nw = _rms_bwd(d_h, x, rstd, nw, dout)
    return loss, grad_x, g_nw.reshape(D_MODEL)


_SMALL = ["norm_w", "q_norm_w", "k_norm_w", "sinks", "A_re", "A_im", "log_dt", "B_re", "B_im", "C_re", "C_im",
          "D_skip", "b_glu"]
_BIG = ["w_in", "w_attn_proj", "w_glu", "w_ssm_proj", "w_out"]
_LATER = _BIG[1:]
_RELATIONS = ("flip_x", "flip_y", "flip_xy")
_ORDER = ["norm_w", "w_in", "q_norm_w", "k_norm_w", "sinks", "w_attn_proj", "A_re", "A_im", "log_dt", "B_re", "B_im",
          "C_re", "C_im", "D_skip", "w_glu", "b_glu", "w_ssm_proj", "w_out"]
_PACK_W = 1024
_MINOR_SWAPPED = ("B_re", "B_im")


def _swap_minor(a):
    return jnp.swapaxes(a, 1, 2)


def _packed_rows(size):
    unit = SUBLANES * _PACK_W
    return -(-size // unit) * SUBLANES


def _pack_small(d, names):
    parts = []
    for n in names:
        flat = d[n].reshape(-1).astype(F32)
        rows = _packed_rows(flat.shape[0])
        parts.append(jnp.pad(flat, (0, rows * _PACK_W - flat.shape[0])).reshape(rows, _PACK_W))
    return jnp.concatenate(parts, axis=0)


def _unpack_small(packed, like, names):
    out, pos = {}, 0
    for n in names:
        rows = _packed_rows(like[n].size)
        out[n] = packed[pos:pos + rows].reshape(-1)[:like[n].size].reshape(like[n].shape)
        pos += rows
    return out


def _place_block(v, index_arr, *, name):
    rows, cols = v.shape

    def body(i_ref, v_ref, o_ref):
        o_ref[...] = v_ref[...]

    return pl.pallas_call(
        body, name=name,
        grid_spec=pltpu.PrefetchScalarGridSpec(
            num_scalar_prefetch=1, grid=(1,),
            in_specs=[pl.BlockSpec((rows, cols), lambda i, d: (0, 0))],
            out_specs=pl.BlockSpec((None, rows, cols), lambda i, d: (d[0], 0, 0))),
        out_shape=jax.ShapeDtypeStruct((8, rows, cols), v.dtype),
        compiler_params=_params(("arbitrary",)),
    )(index_arr, v)


def _plan_all_to_all(refs):
    (land,) = refs
    x, y, c, _ = _place()
    own = land.at[4 * x + 2 * y + c]
    copies = []
    for fx, fy, fc in [(0, 0, 1), (0, 1, 0), (0, 1, 1), (1, 0, 0), (1, 0, 1), (1, 1, 0), (1, 1, 1)]:
        px, py, pc = (1 - x) if fx else x, (1 - y) if fy else y, (1 - c) if fc else c
        copies.append((own, own, (px, py, pc), land.at[4 * px + 2 * py + pc]))
    return copies


def _adamw_whole(ws, gs, ms, vs, *, name):
    n = len(ws)

    def body(*refs):
        ins, outs = refs[:4 * n], refs[4 * n:]
        for i in range(n):
            w_ref, g_ref, m_ref, v_ref = ins[i], ins[n + i], ins[2 * n + i], ins[3 * n + i]
            outs[3 * i][...], outs[3 * i + 1][...], outs[3 * i + 2][...] = _adamw_math(
                w_ref[...], g_ref[...], m_ref[...], v_ref[...])

    out = pl.pallas_call(body, name=name, out_shape=[jax.ShapeDtypeStruct(w.shape, F32) for w in ws for _ in range(3)],
                         compiler_params=_params())(*ws, *gs, *ms, *vs)
    return [tuple(out[3 * i:3 * i + 3]) for i in range(n)]


class _Exchanges:
    def __init__(self, w, m, v):
        self.w, self.m, self.v = w, m, v
        self.grads, self.delta, self.new_m, self.new_v = {}, {}, {}, {}

    def _adamw(self, names, deps):
        for n in names:
            self.delta[n], self.new_m[n], self.new_v[n], self.grads[n] = _adamw(
                self.w[n], self.grads[n], self.m[n], self.v[n], name=f"adamw_{n}", tm=256, deps=deps)

    def begin(self):
        chip = (2 * lax.axis_index("x") + lax.axis_index("y")).astype(jnp.int32).reshape(1)
        w_in = _place_shard(self.w["w_in"], chip, name="place_w_in")
        self.w_in_sems, self.w_in_buf, token = _copies_start("gather_w_in_direct_start", [w_in], _plan_relay_direct, 2)
        self.later_full = [_place_shard(self.w[n], chip, name=f"place_{n}", deps=[token]) for n in _LATER]
        return self.later_full

    def projection(self, h):
        x, y = lax.axis_index("x"), lax.axis_index("y")
        blks = [jnp.asarray(b, jnp.int32).reshape(1)
                for b in (2 * x + y, 2 * (1 - x) + y, 2 * x + (1 - y), 2 * (1 - x) + (1 - y))]
        bufs = self.w_in_buf
        proj = _mm_chip_block(h, bufs[0], blks[0], None, name="mm_proj_own", out_dtype=_MXU)
        relay, token = [], proj
        for k, tag in enumerate(_RELATIONS[:2]):
            bufs = _copies_wait(f"gather_w_in_direct_{tag}_wait", bufs, self.w_in_sems, _plan_relay_direct, [token],
                                which=(k,))
            plan = functools.partial(_plan_relay_forward, k=k)
            sems, bufs, token = _copies_start(f"gather_w_in_relay_{tag}_start", bufs, plan, 2)
            relay.append((sems, plan))
        self.rest = _copies_start("gather_ici_rest_start", self.later_full, _plan_gather_ici, 3 * len(_LATER),
                                  after=[token])
        token = self.rest[2]
        for k, tag in enumerate(_RELATIONS[:2]):
            bufs = _copies_wait(f"gather_w_in_hand_{tag}_wait", bufs, relay[k][0], relay[k][1], [token], which=(1,))
            token = proj = _mm_chip_block(h, bufs[0], blks[1 + k], proj, name=f"mm_proj_{tag}", out_dtype=_MXU)
        for k, tag in enumerate(_RELATIONS[:2]):
            bufs = _copies_wait(f"gather_w_in_relay_{tag}_wait", bufs, relay[k][0], relay[k][1], [token], which=(0,))
        sems, bufs, token = _copies_start("gather_w_in_last_start", bufs, _plan_relay_last, 1)
        bufs = _copies_wait("gather_w_in_last_wait", bufs, sems, _plan_relay_last, [token])
        proj = _mm_chip_block(h, bufs[0], blks[3], proj, name="mm_proj_flip_xy", out_dtype=_MXU)
        return proj, bufs[0]

    def weight(self, name, after):
        if self.rest is not None:
            sems, bufs = self.rest
            later = dict(zip(_LATER, _copies_wait("gather_d2d_rest_wait", bufs, sems, _plan_gather_d2d, [after])))
            later["w_out"] = later["w_out"].reshape(D_MODEL, D_MODEL)
            self.later, self.rest = later, None
        return self.later[name]

    def after_proj(self, proj):
        sems, bufs, _ = self.rest
        bufs = _copies_wait("gather_ici_rest_wait", bufs, sems, _plan_gather_ici, [proj])
        sems, bufs, token = _copies_start("gather_d2d_rest_start", bufs, _plan_gather_d2d, 3 * len(_LATER))
        self.rest = (sems, bufs)
        return [token]

    def later_grads(self, grads):
        self.rs_later = _ReduceScatter("later", _LATER, [grads[n] for n in _LATER])
        return self.rs_later.start_swap()

    def before_scan_backward(self, after):
        return self.rs_later.start_scatter(after)

    def before_input_projection_grad(self, after):
        return self.rs_later.start_join(after)

    def input_projection_grad(self, g_w_in):
        self.grads.update(self.rs_later.finish([g_w_in]))
        self.rs_in = _ReduceScatter("w_in", ["w_in"], [g_w_in])
        self._adamw(_LATER, self.rs_in.start_swap())
        return self.rs_in.start_scatter([self.delta[n] for n in _LATER])

    def _adamw_small(self, names):
        swaps = [_swap_minor if n in _MINOR_SWAPPED else (lambda a: a) for n in names]
        ws, ms, vs = [[swap(d[n]) for n, swap in zip(names, swaps)] for d in (self.w, self.m, self.v)]
        outs = _adamw_whole(ws, [self.grads[n] for n in names], ms, vs,
                            name="adamw_small" if len(names) > 1 else f"adamw_{names[0]}")
        for n, swap, out in zip(names, swaps, outs):
            self.delta[n], self.new_m[n], self.new_v[n] = [swap(o) for o in out]
            self.grads[n] = swap(self.grads[n])

    def small_grads(self, grads):
        me = (4 * lax.axis_index("x") + 2 * lax.axis_index("y") + lax.axis_index("c")).astype(jnp.int32).reshape(1)
        grads = {n: _swap_minor(g) if n in _MINOR_SWAPPED else g for n, g in grads.items()}
        land = _place_block(_pack_small(grads, _SMALL[1:]), me, name="place_small_grads")
        self.small = _copies_start("gather_small_start", [land], _plan_all_to_all, 7)
        return [self.small[2]]

    def finish(self, g_norm_w, loss, after):
        join = self.rs_in.start_join(after)
        sems, bufs, _ = self.small
        (land,) = _copies_wait("gather_small_wait", bufs, sems, _plan_all_to_all, join)
        like = {n: _swap_minor(self.w[n]) if n in _MINOR_SWAPPED else self.w[n] for n in _SMALL[1:]}
        self.grads.update(_unpack_small(_sum8(land, name="sum_small_grads"), like, _SMALL[1:]))
        self._adamw_small(_SMALL[1:])
        rows = _packed_rows(g_norm_w.size)
        late = jnp.concatenate([_pack_small(dict(norm_w=g_norm_w), _SMALL[:1]),
                                jnp.pad(loss.reshape(1, 1), ((0, SUBLANES - 1), (0, _PACK_W - 1)))], axis=0)
        late = _sum8(_all_gather_small(late), name="sum_norm_w_grad_and_loss")
        self.grads.update(_unpack_small(late[:rows], self.w, _SMALL[:1]))
        self._adamw_small(_SMALL[:1])
        self.grads.update(self.rs_in.finish([self.delta[_SMALL[0]]]))
        self._adamw(["w_in"], ())
        return late[rows, 0]


def kernel(x, norm_w, w_in, q_norm_w, k_norm_w, sinks, w_attn_proj, A_re, A_im, log_dt, B_re, B_im, C_re, C_im, D_skip, w_glu, b_glu, w_ssm_proj, w_out, loss_target, m_norm_w, m_w_in, m_q_norm_w, m_k_norm_w, m_sinks, m_w_attn_proj, m_A_re, m_A_im, m_log_dt, m_B_re, m_B_im, m_C_re, m_C_im, m_D_skip, m_w_glu, m_b_glu, m_w_ssm_proj, m_w_out, v_norm_w, v_w_in, v_q_norm_w, v_k_norm_w, v_sinks, v_w_attn_proj, v_A_re, v_A_im, v_log_dt, v_B_re, v_B_im, v_C_re, v_C_im, v_D_skip, v_w_glu, v_b_glu, v_w_ssm_proj, v_w_out):
    w = dict(norm_w=norm_w, w_in=w_in, q_norm_w=q_norm_w, k_norm_w=k_norm_w, sinks=sinks, w_attn_proj=w_attn_proj,
             A_re=A_re, A_im=A_im, log_dt=log_dt, B_re=B_re, B_im=B_im, C_re=C_re, C_im=C_im, D_skip=D_skip,
             w_glu=w_glu, b_glu=b_glu, w_ssm_proj=w_ssm_proj, w_out=w_out)
    m = dict(norm_w=m_norm_w, w_in=m_w_in, q_norm_w=m_q_norm_w, k_norm_w=m_k_norm_w, sinks=m_sinks,
             w_attn_proj=m_w_attn_proj, A_re=m_A_re, A_im=m_A_im, log_dt=m_log_dt, B_re=m_B_re, B_im=m_B_im,
             C_re=m_C_re, C_im=m_C_im, D_skip=m_D_skip, w_glu=m_w_glu, b_glu=m_b_glu, w_ssm_proj=m_w_ssm_proj,
             w_out=m_w_out)
    v = dict(norm_w=v_norm_w, w_in=v_w_in, q_norm_w=v_q_norm_w, k_norm_w=v_k_norm_w, sinks=v_sinks,
             w_attn_proj=v_w_attn_proj, A_re=v_A_re, A_im=v_A_im, log_dt=v_log_dt, B_re=v_B_re, B_im=v_B_im,
             C_re=v_C_re, C_im=v_C_im, D_skip=v_D_skip, w_glu=v_w_glu, b_glu=v_b_glu, w_ssm_proj=v_w_ssm_proj,
             w_out=v_w_out)

    io = _Exchanges(w, m, v)
    loss, grad_x, g_norm_w = _local_step(x[0], loss_target[0], norm_w, q_norm_w, k_norm_w, sinks, A_re, A_im, log_dt,
                                         B_re, B_im, C_re, C_im, D_skip, b_glu, io)
    loss = io.finish(g_norm_w, loss, [grad_x])
    grads, delta, new_m, new_v = io.grads, io.delta, io.new_m, io.new_v

    return (loss, grad_x[None], *[grads[n] for n in _ORDER], *[delta[n] for n in _ORDER],
            *[new_m[n] for n in _ORDER], *[new_v[n] for n in _ORDER])
```

```python
import functools
import math

import jax
import jax.numpy as jnp
from jax import lax
from jax.experimental import pallas as pl
from jax.experimental.pallas import tpu as pltpu

F32 = jnp.float32
_MXU = jnp.bfloat16
_WIRE = jnp.bfloat16

LANES = 128
SUBLANES = 8
VMEM_LIMIT = 56 * 1024 * 1024

D_MODEL = 2048
HEAD_DIM = 64
N_Q_HEADS = 16
N_KV_HEADS = 4
Q_PER_KV = 4
ATTN_W = 1024
KV_W = 256
WINDOW = 128
SSM_W = 1024
GROUP = 16
N_GROUPS = 64
STATE = 64
N_STATES = N_GROUPS * STATE
IN_W = 8704
NORM_EPS = 1e-6
N_CHIPS = 4
CW = 512
OFF_AGATE, OFF_U, OFF_Z, OFF_GA, OFF_GS = 3, 5, 7, 9, 13

SSM_T = 256
SSM_L = SSM_T // SUBLANES
SSM_JB = 8
SSM_SB = N_STATES // SSM_JB

ADAM_LR, ADAM_B1, ADAM_B2, ADAM_EPS, ADAM_WD, ADAM_STEP = 0.001, 0.9, 0.999, 1e-08, 0.01, 10

MESH = pl.DeviceIdType.MESH
_ANY = pl.BlockSpec(memory_space=pl.ANY)


def _params(sem=None):
    return pltpu.CompilerParams(dimension_semantics=sem, vmem_limit_bytes=VMEM_LIMIT)


def _mm(a, b, *, mode, name, tm, tn, tk, out_dtype=F32, b_blocked=False, out_blocked=False, rows_outer=False,
        deps=()):
    nd = len(deps)
    if mode == "tn":
        K, M = a.shape
    else:
        M, K = a.shape
    if mode == "nn":
        N = b.shape[0] * b.shape[2] if b_blocked else b.shape[1]
    elif mode == "nt":
        N = b.shape[1] if b_blocked else b.shape[0]
    else:
        N = b.shape[1]
    tm, tn, tk = min(tm, M), min(tn, N), min(tk, K)
    nj, ni, nk = N // tn, M // tm, K // tk
    assert nj * tn == N and ni * tm == M and nk * tk == K, (name, M, N, K)
    dims = {"nn": (((1,), (0,)), ((), ())), "nt": (((1,), (1,)), ((), ())), "tn": (((0,), (0,)), ((), ()))}[mode]

    if mode == "tn":
        a_spec = pl.BlockSpec((tk, tm), lambda j, i, k: (k, i))
    else:
        a_spec = pl.BlockSpec((tm, tk), lambda j, i, k: (i, k))
    if mode == "nn":
        if b_blocked:
            assert b.shape[0] == nj and b.shape[2] == tn
            b_spec = pl.BlockSpec((None, tk, tn), lambda j, i, k: (j, k, 0))
        else:
            b_spec = pl.BlockSpec((tk, tn), lambda j, i, k: (k, j))
    elif mode == "nt":
        if b_blocked:
            assert b.shape[0] == nk and b.shape[2] == tk
            b_spec = pl.BlockSpec((None, tn, tk), lambda j, i, k: (k, j, 0))
        else:
            b_spec = pl.BlockSpec((tn, tk), lambda j, i, k: (j, k))
    else:
        b_spec = pl.BlockSpec((tk, tn), lambda j, i, k: (k, j))
    whole_out = out_blocked and nj == 1
    if whole_out:
        assert ni == 1
        o_spec = pl.BlockSpec((N_CHIPS, tm, tn // N_CHIPS), lambda j, i, k: (0, 0, 0))
        o_shape = jax.ShapeDtypeStruct((N_CHIPS, M, tn // N_CHIPS), out_dtype)
    elif out_blocked:
        assert nj == N_CHIPS
        o_spec = pl.BlockSpec((None, tm, tn), lambda j, i, k: (j, i, 0))
        o_shape = jax.ShapeDtypeStruct((nj, M, tn), out_dtype)
    else:
        o_spec = pl.BlockSpec((tm, tn), lambda j, i, k: (i, j))
        o_shape = jax.ShapeDtypeStruct((M, N), out_dtype)
    use_acc = nk > 1 and (out_dtype != F32 or whole_out)

    def body(a_ref, b_ref, *rest):
        o_ref, scratch = rest[nd], rest[nd + 1:]

        def product():
            return lax.dot_general(a_ref[...].astype(_MXU), b_ref[...].astype(_MXU), dims, preferred_element_type=F32)

        def write(result):
            if whole_out:
                w = tn // N_CHIPS
                for c in range(N_CHIPS):
                    o_ref[c] = result[:, c * w:(c + 1) * w].astype(o_ref.dtype)
            else:
                o_ref[...] = result.astype(o_ref.dtype)

        if nk == 1:
            write(product())
            return
        k = pl.program_id(2)
        acc = scratch[0] if use_acc else o_ref

        @pl.when(k == 0)
        def _():
            acc[...] = jnp.zeros_like(acc)

        acc[...] += product()

        if use_acc:
            @pl.when(k == nk - 1)
            def _():
                write(acc[...])

    specs = [a_spec, b_spec, o_spec]
    grid = (nj, ni, nk)
    if rows_outer:
        specs = [pl.BlockSpec(s.block_shape, lambda i, j, k, f=s.index_map: f(j, i, k)) for s in specs]
        grid = (ni, nj, nk)
    return pl.pallas_call(
        body, name=name, grid=grid, in_specs=specs[:2] + [_ANY] * nd, out_specs=specs[2],
        out_shape=o_shape, scratch_shapes=[pltpu.VMEM((tm, tn), F32)] if use_acc else [],
        compiler_params=_params(("parallel", "parallel", "arbitrary")),
    )(a, b, *deps)


def _mm_chip_block(a, b4, blk, prev, *, name, tm=1024, out_dtype=F32, deps=()):
    M, K = a.shape
    nchip, _, C = b4.shape
    tm = min(tm, M)
    extra = ([] if prev is None else [prev]) + list(deps)

    def body(blk_ref, a_ref, b_ref, *rest):
        rest[-1][...] = jnp.dot(a_ref[...].astype(_MXU), b_ref[...].astype(_MXU),
                                preferred_element_type=F32).astype(rest[-1].dtype)

    return pl.pallas_call(
        body, name=name,
        grid_spec=pltpu.PrefetchScalarGridSpec(
            num_scalar_prefetch=1, grid=(M // tm,),
            in_specs=[pl.BlockSpec((tm, K), lambda i, c: (i, 0)), pl.BlockSpec((None, K, C), lambda i, c: (c[0], 0, 0))]
            + [_ANY] * len(extra),
            out_specs=pl.BlockSpec((tm, C), lambda i, c: (i, c[0]))),
        out_shape=jax.ShapeDtypeStruct((M, nchip * C), out_dtype),
        input_output_aliases={} if prev is None else {3: 0},
        compiler_params=_params(("arbitrary",)),
    )(blk, a, b4, *extra)


def _mm_merge_out_loss(proj, y_a, y_s, w_out, x, target, *, tm=256):
    rows, d = x.shape
    ncol = d // CW

    def body(*refs):
        ga_refs, gs_refs = refs[:ncol], refs[ncol:2 * ncol]
        ya_ref, ys_ref, w_ref, x_ref, t_ref, m_ref, d_ref, db_ref, sq_ref = refs[2 * ncol:]
        for j in range(ncol):
            cols = slice(j * CW, (j + 1) * CW)
            m_ref[:, cols] = (_sigmoid(ga_refs[j][...].astype(F32)) * ya_ref[:, cols].astype(F32)
                              + _sigmoid(gs_refs[j][...].astype(F32)) * ys_ref[:, cols].astype(F32)).astype(m_ref.dtype)
        mo = jnp.dot(m_ref[...], w_ref[...].astype(_MXU), preferred_element_type=F32)
        err = (x_ref[...] + mo) - t_ref[...]
        dout = err * (1.0 / d)
        d_ref[...] = dout
        db_ref[...] = dout.astype(db_ref.dtype)
        part = _colsum(err * err)
        i = pl.program_id(0)

        @pl.when(i == 0)
        def _():
            sq_ref[...] = part

        @pl.when(i > 0)
        def _():
            sq_ref[...] += part

    tile = pl.BlockSpec((tm, d), lambda i: (i, 0))
    gate = [pl.BlockSpec((tm, CW), lambda i, c=off + j: (i, c)) for off in (OFF_GA, OFF_GS) for j in range(ncol)]
    return pl.pallas_call(
        body, name="mm_merge_out_loss", grid=(rows // tm,),
        in_specs=gate + [tile, tile, pl.BlockSpec((d, d), lambda i: (0, 0), pipeline_mode=pl.Buffered(1)), tile, tile],
        out_specs=[tile, tile, tile, pl.BlockSpec((1, d), lambda i: (0, 0))],
        out_shape=[jax.ShapeDtypeStruct((rows, d), _MXU), jax.ShapeDtypeStruct((rows, d), F32),
                   jax.ShapeDtypeStruct((rows, d), _MXU), jax.ShapeDtypeStruct((1, d), F32)],
        compiler_params=_params(("arbitrary",)),
    )(*([proj] * (2 * ncol)), y_a, y_s, w_out, x, target)


def _mm_merge_bwd(dout_b, w_out, proj, y_a, y_s, *, tm=512):
    rows, d = y_a.shape
    ncol = d // CW

    def body(do_ref, w_ref, *refs):
        ga_refs, gs_refs = refs[:ncol], refs[ncol:2 * ncol]
        ya_ref, ys_ref, dya_ref, dys_ref, dg_ref = refs[2 * ncol:]
        dm = lax.dot_general(do_ref[...].astype(_MXU), w_ref[...].astype(_MXU), _NT, preferred_element_type=F32)
        for j in range(ncol):
            cols = slice(j * CW, (j + 1) * CW)
            dmj = dm[:, cols]
            sa, ss = _sigmoid(ga_refs[j][...].astype(F32)), _sigmoid(gs_refs[j][...].astype(F32))
            dya_ref[:, cols] = (sa * dmj).astype(dya_ref.dtype)
            dys_ref[:, cols] = (ss * dmj).astype(dys_ref.dtype)
            dg_ref[:, cols] = (dmj * ya_ref[:, cols].astype(F32) * sa * (1.0 - sa)).astype(dg_ref.dtype)
            dg_ref[:, d + j * CW:d + (j + 1) * CW] = (dmj * ys_ref[:, cols].astype(F32) * ss
                                                      * (1.0 - ss)).astype(dg_ref.dtype)

    tile = pl.BlockSpec((tm, d), lambda i: (i, 0))
    gate = [pl.BlockSpec((tm, CW), lambda i, c=off + j: (i, c)) for off in (OFF_GA, OFF_GS) for j in range(ncol)]
    both = pl.BlockSpec((pl.Element(tm), pl.Element(2 * d)), lambda i: (i * tm, OFF_GA * CW))
    return pl.pallas_call(
        body, name="mm_merge_bwd", grid=(rows // tm,),
        in_specs=[tile, pl.BlockSpec((d, d), lambda i: (0, 0), pipeline_mode=pl.Buffered(1))] + gate + [tile, tile],
        out_specs=[tile, tile, both],
        out_shape=[jax.ShapeDtypeStruct((rows, d), _MXU)] * 2 + [jax.ShapeDtypeStruct((rows, IN_W), _MXU)],
        compiler_params=_params(("arbitrary",)),
    )(dout_b, w_out, *([proj] * (2 * ncol)), y_a, y_s)


def _mm_glu_gate(yg, w_glu4, b_glu, proj, *, tm=1024):
    rows, k = yg.shape
    nj, _, tn = w_glu4.shape
    w = nj * tn // 2
    tm = min(tm, rows)

    def body(a_ref, w_ref, ba_ref, bb_ref, z0_ref, z1_ref, glu_ref, ys_ref):
        j = pl.program_id(1)
        for c in range(nj):
            @pl.when(j == c)
            def _(c=c):
                glu_ref[:, c * tn:(c + 1) * tn] = jnp.dot(a_ref[...].astype(_MXU), w_ref[...].astype(_MXU),
                                                          preferred_element_type=F32).astype(glu_ref.dtype)

        @pl.when(j == nj - 1)
        def _():
            z = jnp.concatenate([z0_ref[...], z1_ref[...]], axis=1).astype(F32)
            ys_ref[...] = ((glu_ref[:, :w].astype(F32) + ba_ref[...]) * _sigmoid(glu_ref[:, w:].astype(F32) + bb_ref[...])
                           * (z * _sigmoid(z))).astype(ys_ref.dtype)

    bias = lambda c: pl.BlockSpec((1, w), lambda i, j: (0, c))
    zcol = lambda c: pl.BlockSpec((tm, CW), lambda i, j: (i, OFF_Z + c))
    return pl.pallas_call(
        body, name="mm_glu_gate", grid=(rows // tm, nj),
        in_specs=[pl.BlockSpec((tm, k), lambda i, j: (i, 0)), pl.BlockSpec((None, k, tn), lambda i, j: (j, 0, 0)),
                  bias(0), bias(1), zcol(0), zcol(1)],
        out_specs=[pl.BlockSpec((tm, nj * tn), lambda i, j: (i, 0)), pl.BlockSpec((tm, w), lambda i, j: (i, 0))],
        out_shape=[jax.ShapeDtypeStruct((rows, nj * tn), _MXU), jax.ShapeDtypeStruct((rows, w), _MXU)],
        compiler_params=_params(("arbitrary", "arbitrary")),
    )(yg, w_glu4, b_glu, b_glu, proj, proj)


def _mm_ssm_gate_bwd(d_ys, w_sp4, glu, b_glu, proj, d_proj, *, tm=1024):
    rows, w = glu.shape[0], glu.shape[1] // 2
    nk, tk = w_sp4.shape[0], w_sp4.shape[2]
    tm = min(tm, rows)

    def body(dy_ref, w_ref, ga_ref, gb_ref, ba_ref, bb_ref, z0_ref, z1_ref, buf_ref, dg_ref, dz_ref, db_ref, acc):
        i, k = pl.program_id(0), pl.program_id(1)

        @pl.when(k == 0)
        def _():
            acc[...] = jnp.zeros_like(acc)

        acc[...] += lax.dot_general(dy_ref[...].astype(_MXU), w_ref[...].astype(_MXU), _NT, preferred_element_type=F32)

        @pl.when(k == nk - 1)
        def _():
            dv = acc[...]
            a, sb = ga_ref[...].astype(F32) + ba_ref[...], _sigmoid(gb_ref[...].astype(F32) + bb_ref[...])
            f, df = _silu_and_grad(jnp.concatenate([z0_ref[...], z1_ref[...]], axis=1).astype(F32))
            dga = dv * sb * f
            dgb = dv * a * f * sb * (1.0 - sb)
            dg_ref[:, :w] = dga.astype(dg_ref.dtype)
            dg_ref[:, w:] = dgb.astype(dg_ref.dtype)
            dz_ref[...] = (dv * a * sb * df).astype(dz_ref.dtype)
            part = jnp.concatenate([_colsum(dga), _colsum(dgb)], axis=1)

            @pl.when(i == 0)
            def _():
                db_ref[...] = part

            @pl.when(i > 0)
            def _():
                db_ref[...] += part

    half = lambda c: pl.BlockSpec((tm, w), lambda i, k: (i, c))
    bias = lambda c: pl.BlockSpec((1, w), lambda i, k: (0, c))
    zcol = lambda c: pl.BlockSpec((tm, CW), lambda i, k: (i, OFF_Z + c))
    return pl.pallas_call(
        body, name="mm_ssm_gate_bwd", grid=(rows // tm, nk),
        in_specs=[pl.BlockSpec((tm, tk), lambda i, k: (i, k)), pl.BlockSpec((None, w, tk), lambda i, k: (k, 0, 0)),
                  half(0), half(1), bias(0), bias(1), zcol(0), zcol(1), _ANY],
        out_specs=[pl.BlockSpec((tm, 2 * w), lambda i, k: (i, 0)),
                   pl.BlockSpec((pl.Element(tm), pl.Element(w)), lambda i, k: (i * tm, OFF_Z * CW)),
                   pl.BlockSpec((1, 2 * w), lambda i, k: (0, 0))],
        out_shape=[jax.ShapeDtypeStruct((rows, 2 * w), _MXU), jax.ShapeDtypeStruct(d_proj.shape, d_proj.dtype),
                   jax.ShapeDtypeStruct((1, 2 * w), F32)],
        input_output_aliases={8: 1},
        scratch_shapes=[pltpu.VMEM((tm, w), F32)],
        compiler_params=_params(("arbitrary", "arbitrary")),
    )(d_ys, w_sp4, glu, glu, b_glu, b_glu, proj, proj, d_proj)


def _colsum(v):
    return jnp.sum(v, axis=0, keepdims=True)


def _sigmoid(v):
    return jax.nn.sigmoid(v)


def _silu_and_grad(v):
    s = _sigmoid(v)
    return v * s, s * (1.0 + v * (1.0 - s))


def _rms_fwd(x, w, *, tm=512, deps=()):
    rows, d = x.shape
    nd = len(deps)

    def body(x_ref, w_ref, *rest):
        h_ref, r_ref = rest[nd:]
        xv = x_ref[...]
        r = lax.rsqrt(jnp.mean(xv * xv, axis=-1, keepdims=True) + NORM_EPS)
        h_ref[...] = (xv * r * w_ref[...]).astype(h_ref.dtype)
        r_ref[...] = r

    return pl.pallas_call(
        body, name="rms_fwd", grid=(rows // tm,),
        in_specs=[pl.BlockSpec((tm, d), lambda i: (i, 0)), pl.BlockSpec((1, d), lambda i: (0, 0))] + [_ANY] * nd,
        out_specs=[pl.BlockSpec((tm, d), lambda i: (i, 0)), pl.BlockSpec((tm, 1), lambda i: (i, 0))],
        out_shape=[jax.ShapeDtypeStruct((rows, d), _MXU), jax.ShapeDtypeStruct((rows, 1), F32)],
        compiler_params=_params(("arbitrary",)),
    )(x, w, *deps)


def _rms_bwd(dh, x, rstd, w, dout, *, tm=512):
    rows, d = x.shape

    def body(dh_ref, x_ref, r_ref, w_ref, do_ref, gx_ref, gw_ref):
        dhv, xv, r, wv = dh_ref[...], x_ref[...], r_ref[...], w_ref[...]
        xr = xv * r
        t = jnp.mean(dhv * wv * xr, axis=-1, keepdims=True)
        gx_ref[...] = do_ref[...] + r * (wv * dhv - xr * t)
        part = _colsum(dhv * xr)
        i = pl.program_id(0)

        @pl.when(i == 0)
        def _():
            gw_ref[...] = part

        @pl.when(i > 0)
        def _():
            gw_ref[...] += part

    return pl.pallas_call(
        body, name="rms_bwd", grid=(rows // tm,),
        in_specs=[pl.BlockSpec((tm, d), lambda i: (i, 0)), pl.BlockSpec((tm, d), lambda i: (i, 0)),
                  pl.BlockSpec((tm, 1), lambda i: (i, 0)), pl.BlockSpec((1, d), lambda i: (0, 0)),
                  pl.BlockSpec((tm, d), lambda i: (i, 0))],
        out_specs=[pl.BlockSpec((tm, d), lambda i: (i, 0)), pl.BlockSpec((1, d), lambda i: (0, 0))],
        out_shape=[jax.ShapeDtypeStruct((rows, d), F32), jax.ShapeDtypeStruct((1, d), F32)],
        compiler_params=_params(("arbitrary",)),
    )(dh, x, rstd, w, dout)


_NT = (((1,), (1,)), ((), ()))
_TN = (((0,), (0,)), ((), ()))


QKV_W = ATTN_W + 2 * KV_W
HEADS_PER_TILE = LANES // HEAD_DIM


def _low_half(rows):
    return lax.broadcasted_iota(jnp.int32, (rows, LANES), 1) < HEAD_DIM


def _pair_mean(t, low):
    m_lo = jnp.sum(jnp.where(low, t, 0.0), axis=-1, keepdims=True)
    m_hi = jnp.sum(jnp.where(low, 0.0, t), axis=-1, keepdims=True)
    return jnp.where(low, m_lo, m_hi) * (1.0 / HEAD_DIM)


def _pair_rstd(t, low):
    return lax.rsqrt(_pair_mean(t * t, low) + NORM_EPS)


def _dup_half(t, hi, low):
    swapped = pltpu.roll(t, HEAD_DIM, 1)
    return jnp.where(low, swapped, t) if hi else jnp.where(low, t, swapped)


def _fold_halves(t):
    return t + pltpu.roll(t, HEAD_DIM, 1)


def _split_heads(t, low):
    return [jnp.where(low, t, 0.0), jnp.where(low, 0.0, t)]


def _stacked_band_mask(n):
    rows = Q_PER_KV * WINDOW
    qi = lax.broadcasted_iota(jnp.int32, (rows, 2 * WINDOW), 0) % WINDOW + WINDOW
    kj = lax.broadcasted_iota(jnp.int32, (rows, 2 * WINDOW), 1)
    diff = qi - kj
    first_key = jnp.where(n > 0, 0, WINDOW)
    return (diff >= 0) & (diff < WINDOW) & (kj >= first_key)


def _stacked_sinks(sink_ref, g):
    blk = lax.broadcasted_iota(jnp.int32, (Q_PER_KV * WINDOW, 1), 0) // WINDOW
    col = jnp.full((Q_PER_KV * WINDOW, 1), sink_ref[Q_PER_KV * g], F32)
    for r in range(1, Q_PER_KV):
        col = jnp.where(blk == r, sink_ref[Q_PER_KV * g + r], col)
    return col


def _attn_in_specs(nblk, rev):
    def cur(n):
        return (nblk - 1 - n) if rev else n

    q_spec = pl.BlockSpec((WINDOW, ATTN_W), lambda n: (cur(n), 0))
    kvc_spec = pl.BlockSpec((WINDOW, 2 * KV_W), lambda n: (cur(n), ATTN_W // (2 * KV_W)))
    kvp_spec = pl.BlockSpec((WINDOW, 2 * KV_W), lambda n: (jnp.maximum(cur(n) - 1, 0), ATTN_W // (2 * KV_W)))
    w_spec = pl.BlockSpec((1, LANES), lambda n: (0, 0))
    l_spec = pl.BlockSpec((WINDOW, N_Q_HEADS), lambda n: (cur(n), 0))
    gate_specs = [pl.BlockSpec((WINDOW, CW), lambda n, col=OFF_AGATE + j: (cur(n), col)) for j in range(ATTN_W // CW)]
    return q_spec, kvc_spec, kvp_spec, w_spec, l_spec, gate_specs


def _attn2_fwd(proj, qw2, kw2, sinks, deps=()):
    seq = proj.shape[0]
    nblk = seq // WINDOW
    scale = 1.0 / math.sqrt(HEAD_DIM)
    q_spec, kvc_spec, kvp_spec, w_spec, l_spec, gate_specs = _attn_in_specs(nblk, False)
    nd, ng = len(deps), len(gate_specs)

    def body(sink_ref, q_ref, kvc_ref, kvp_ref, qw_ref, kw_ref, *rest):
        gate_refs = rest[:ng]
        o_ref, lse_ref, ya_ref = rest[ng + nd:]
        n = pl.program_id(0)
        low, low2 = _low_half(WINDOW), _low_half(2 * WINDOW)
        valid = _stacked_band_mask(n)
        head_lane = lax.broadcasted_iota(jnp.int32, (WINDOW, N_Q_HEADS), 1)
        kv = jnp.concatenate([kvp_ref[...], kvc_ref[...]], axis=0).astype(F32)
        qwv, kwv = qw_ref[...], kw_ref[...]
        lse_blk = jnp.zeros((WINDOW, N_Q_HEADS), F32)
        for t in range(N_KV_HEADS // HEADS_PER_TILE):
            kt = kv[:, t * LANES:(t + 1) * LANES]
            vt = kv[:, KV_W + t * LANES:KV_W + (t + 1) * LANES]
            kn = kt * _pair_rstd(kt, low2) * kwv
            for hi in range(HEADS_PER_TILE):
                g = HEADS_PER_TILE * t + hi
                kdup = _dup_half(kn, hi, low2).astype(_MXU)
                vdup = _dup_half(vt, hi, low2).astype(_MXU)
                stack = []
                for tq in (2 * g, 2 * g + 1):
                    qt = q_ref[:, tq * LANES:(tq + 1) * LANES].astype(F32)
                    stack += _split_heads(qt * _pair_rstd(qt, low) * qwv, low)
                qs = jnp.concatenate(stack, axis=0).astype(_MXU)
                s = lax.dot_general(qs, kdup, _NT, preferred_element_type=F32) * scale
                s = jnp.where(valid, s, -1e30)
                sink = _stacked_sinks(sink_ref, g)
                m = jnp.maximum(jnp.max(s, axis=-1, keepdims=True), sink)
                e = jnp.exp(s - m)
                z = jnp.sum(e, axis=-1, keepdims=True) + jnp.exp(sink - m)
                o = jnp.dot((e / z).astype(_MXU), vdup, preferred_element_type=F32)
                for i, tq in enumerate((2 * g, 2 * g + 1)):
                    tile = slice(tq * LANES, (tq + 1) * LANES)
                    out = jnp.where(low, o[2 * i * WINDOW:(2 * i + 1) * WINDOW],
                                    o[(2 * i + 1) * WINDOW:(2 * i + 2) * WINDOW])
                    gate = gate_refs[tq * LANES // CW][:, tq * LANES % CW:tq * LANES % CW + LANES].astype(F32)
                    o_ref[:, tile] = out.astype(o_ref.dtype)
                    ya_ref[:, tile] = (out * (gate * _sigmoid(gate))).astype(ya_ref.dtype)
                lse = m + jnp.log(z)
                for r in range(Q_PER_KV):
                    lse_blk = jnp.where(head_lane == Q_PER_KV * g + r, lse[r * WINDOW:(r + 1) * WINDOW], lse_blk)
        lse_ref[...] = lse_blk

    return pl.pallas_call(
        body, name="attn_fwd", grid=(nblk,),
        in_specs=[pl.BlockSpec(memory_space=pltpu.SMEM), q_spec, kvc_spec, kvp_spec, w_spec, w_spec] + gate_specs
        + [_ANY] * nd,
        out_specs=[q_spec, l_spec, q_spec],
        out_shape=[jax.ShapeDtypeStruct((seq, ATTN_W), _MXU), jax.ShapeDtypeStruct((seq, N_Q_HEADS), F32),
                   jax.ShapeDtypeStruct((seq, ATTN_W), _MXU)],
        compiler_params=_params(("arbitrary",)),
    )(sinks, proj, proj, proj, qw2, kw2, *([proj] * ng), *deps)


def _attn2_bwd(proj, qw2, kw2, sinks, lse, attn, dya, d_proj, deps=()):
    seq = proj.shape[0]
    nblk = seq // WINDOW
    scale = 1.0 / math.sqrt(HEAD_DIM)
    q_spec, kvc_spec, kvp_spec, w_spec, l_spec, gate_specs = _attn_in_specs(nblk, True)
    s_spec = pl.BlockSpec((1, N_Q_HEADS), lambda n: (0, 0))
    d_spec = pl.BlockSpec((WINDOW, QKV_W + ATTN_W), lambda n: (nblk - 1 - n, 0))
    deps = list(deps) + [d_proj]
    nd, ng = len(deps), len(gate_specs)

    def body(sink_ref, q_ref, kvc_ref, kvp_ref, qw_ref, kw_ref, lse_ref, attn_ref, dya_ref, *rest):
        gate_refs = rest[:ng]
        d_ref, dqw_ref, dkw_ref, dsk_ref, carry, do_ref = rest[ng + nd:]
        step = pl.program_id(0)
        n = nblk - 1 - step

        @pl.when(step == 0)
        def _():
            carry[...] = jnp.zeros_like(carry)
            dqw_ref[...] = jnp.zeros_like(dqw_ref)
            dkw_ref[...] = jnp.zeros_like(dkw_ref)
            dsk_ref[...] = jnp.zeros_like(dsk_ref)

        for j, g_ref in enumerate(gate_refs):
            cols = slice(j * CW, (j + 1) * CW)
            f, df = _silu_and_grad(g_ref[...].astype(F32))
            dv = dya_ref[:, cols]
            do_ref[:, cols] = dv * f
            d_ref[:, QKV_W + j * CW:QKV_W + (j + 1) * CW] = (dv * attn_ref[:, cols].astype(F32) * df).astype(d_ref.dtype)

        low, low2 = _low_half(WINDOW), _low_half(2 * WINDOW)
        valid = _stacked_band_mask(n)
        head_lane = lax.broadcasted_iota(jnp.int32, (WINDOW, N_Q_HEADS), 1)
        sink_lane = lax.broadcasted_iota(jnp.int32, (1, N_Q_HEADS), 1)
        kv = jnp.concatenate([kvp_ref[...], kvc_ref[...]], axis=0).astype(F32)
        qwv, kwv = qw_ref[...], kw_ref[...]
        lse_blk = lse_ref[...]
        dqw = jnp.zeros((1, LANES), F32)
        dkw = jnp.zeros((1, LANES), F32)
        dsk = jnp.zeros((1, N_Q_HEADS), F32)
        for t in range(N_KV_HEADS // HEADS_PER_TILE):
            kt = kv[:, t * LANES:(t + 1) * LANES]
            vt = kv[:, KV_W + t * LANES:KV_W + (t + 1) * LANES]
            rk = _pair_rstd(kt, low2)
            kn = kt * rk * kwv
            dkn_t = jnp.zeros((2 * WINDOW, LANES), F32)
            dv_t = jnp.zeros((2 * WINDOW, LANES), F32)
            for hi in range(HEADS_PER_TILE):
                g = HEADS_PER_TILE * t + hi
                kdup = _dup_half(kn, hi, low2).astype(_MXU)
                vdup = _dup_half(vt, hi, low2).astype(_MXU)
                tiles = (2 * g, 2 * g + 1)
                qx, rq, stack, dstack, lse_rows = [], [], [], [], []
                for tq in tiles:
                    qt = q_ref[:, tq * LANES:(tq + 1) * LANES].astype(F32)
                    r = _pair_rstd(qt, low)
                    rq.append(r)
                    qx.append(qt * r)
                    stack += _split_heads(qx[-1] * qwv, low)
                    dstack += _split_heads(do_ref[:, tq * LANES:(tq + 1) * LANES], low)
                for r in range(Q_PER_KV):
                    lse_rows.append(jnp.sum(jnp.where(head_lane == Q_PER_KV * g + r, lse_blk, 0.0), axis=-1, keepdims=True))
                qs = jnp.concatenate(stack, axis=0).astype(_MXU)
                dos = jnp.concatenate(dstack, axis=0).astype(_MXU)
                lse_col = jnp.concatenate(lse_rows, axis=0)
                s = lax.dot_general(qs, kdup, _NT, preferred_element_type=F32) * scale
                s = jnp.where(valid, s, -1e30)
                p = jnp.exp(s - lse_col)
                dp = lax.dot_general(dos, vdup, _NT, preferred_element_type=F32)
                dsum = jnp.sum(p * dp, axis=-1, keepdims=True)
                ds = (p * (dp - dsum) * scale).astype(_MXU)
                dsink = -jnp.exp(_stacked_sinks(sink_ref, g) - lse_col) * dsum
                for r in range(Q_PER_KV):
                    dsk = dsk + jnp.where(sink_lane == Q_PER_KV * g + r, _colsum(dsink[r * WINDOW:(r + 1) * WINDOW]), 0.0)
                dv_g = _fold_halves(lax.dot_general(p.astype(_MXU), dos, _TN, preferred_element_type=F32))
                dkn_g = _fold_halves(lax.dot_general(ds, qs, _TN, preferred_element_type=F32))
                dv_t = jnp.where(low2, dv_t, dv_g) if hi else jnp.where(low2, dv_g, dv_t)
                dkn_t = jnp.where(low2, dkn_t, dkn_g) if hi else jnp.where(low2, dkn_g, dkn_t)
                dqn = jnp.dot(ds, kdup, preferred_element_type=F32)
                for i, tq in enumerate(tiles):
                    dqn_t = jnp.where(low, dqn[2 * i * WINDOW:(2 * i + 1) * WINDOW],
                                      dqn[(2 * i + 1) * WINDOW:(2 * i + 2) * WINDOW])
                    dq = rq[i] * (qwv * dqn_t - qx[i] * _pair_mean(dqn_t * qwv * qx[i], low))
                    d_ref[:, tq * LANES:(tq + 1) * LANES] = dq.astype(d_ref.dtype)
                    dqw = dqw + _colsum(dqn_t * qx[i])
            k_cols = slice(t * LANES, (t + 1) * LANES)
            v_cols = slice(KV_W + t * LANES, KV_W + (t + 1) * LANES)
            dkn_c = dkn_t[WINDOW:] + carry[:, k_cols]
            rc = rk[WINDOW:]
            kx = kt[WINDOW:] * rc
            dk = rc * (kwv * dkn_c - kx * _pair_mean(dkn_c * kwv * kx, low))
            d_ref[:, ATTN_W + t * LANES:ATTN_W + (t + 1) * LANES] = dk.astype(d_ref.dtype)
            d_ref[:, ATTN_W + KV_W + t * LANES:ATTN_W + KV_W + (t + 1) * LANES] = (
                dv_t[WINDOW:] + carry[:, v_cols]).astype(d_ref.dtype)
            carry[:, k_cols] = dkn_t[:WINDOW]
            carry[:, v_cols] = dv_t[:WINDOW]
            dkw = dkw + _colsum(dkn_c * kx)
        dqw_ref[...] += dqw
        dkw_ref[...] += dkw
        dsk_ref[...] += dsk

    return pl.pallas_call(
        body, name="attn_bwd", grid=(nblk,),
        in_specs=[pl.BlockSpec(memory_space=pltpu.SMEM), q_spec, kvc_spec, kvp_spec, w_spec, w_spec, l_spec, q_spec,
                  q_spec] + gate_specs + [_ANY] * nd,
        out_specs=[d_spec, w_spec, w_spec, s_spec],
        out_shape=[jax.ShapeDtypeStruct(d_proj.shape, d_proj.dtype), jax.ShapeDtypeStruct((1, LANES), F32),
                   jax.ShapeDtypeStruct((1, LANES), F32), jax.ShapeDtypeStruct((1, N_Q_HEADS), F32)],
        input_output_aliases={9 + ng + nd - 1: 0},
        scratch_shapes=[pltpu.VMEM((WINDOW, 2 * KV_W), F32), pltpu.VMEM((WINDOW, ATTN_W), F32)],
        compiler_params=_params(("arbitrary",)),
    )(sinks, proj, proj, proj, qw2, kw2, lse, attn, dya, *([proj] * ng), *deps)


def _ssm_discretise(a_re, a_im, log_dt):
    dt = jnp.exp(log_dt)
    mag = jnp.exp(dt * a_re)
    ab_re = mag * jnp.cos(dt * a_im)
    ab_im = mag * jnp.sin(dt * a_im)
    num_re = ab_re - 1.0
    num_im = ab_im
    den = a_re * a_re + a_im * a_im
    cf_re = (num_re * a_re + num_im * a_im) / den
    cf_im = (num_im * a_re - num_re * a_im) / den
    return ab_re, ab_im, cf_re, cf_im


def _ssm_params_fwd(a_re, a_im, log_dt):
    shp = jax.ShapeDtypeStruct(a_re.shape, F32)

    def body(are_ref, aim_ref, ldt_ref, abr_ref, abi_ref, cfr_ref, cfi_ref, alr_ref, ali_ref):
        abr, abi, cfr, cfi = _ssm_discretise(are_ref[...], aim_ref[...], ldt_ref[...])
        abr_ref[...], abi_ref[...], cfr_ref[...], cfi_ref[...] = abr, abi, cfr, cfi
        pr, pi = abr, abi
        for _ in range(int(math.log2(SSM_L))):
            pr, pi = pr * pr - pi * pi, 2.0 * pr * pi
        alr_ref[...], ali_ref[...] = pr, pi

    return pl.pallas_call(body, name="ssm_params_fwd", out_shape=[shp] * 6)(a_re, a_im, log_dt)


def _ssm_params_bwd(a_re, a_im, log_dt, d_abr, d_abi, d_cfr, d_cfi):
    def body(are_ref, aim_ref, ldt_ref, g0, g1, g2, g3, dare_ref, daim_ref, dldt_ref):
        _, vjp = jax.vjp(_ssm_discretise, are_ref[...], aim_ref[...], ldt_ref[...])
        dare_ref[...], daim_ref[...], dldt_ref[...] = vjp((g0[...], g1[...], g2[...], g3[...]))

    return pl.pallas_call(
        body, name="ssm_params_bwd",
        out_shape=[jax.ShapeDtypeStruct(a_re.shape, F32), jax.ShapeDtypeStruct(a_im.shape, F32),
                   jax.ShapeDtypeStruct(log_dt.shape, F32)],
    )(a_re, a_im, log_dt, d_abr, d_abi, d_cfr, d_cfi)


def _scan_cols(j):
    return pl.ds(j * SSM_SB, SSM_SB)


def _rows8(r):
    return pl.ds(pl.multiple_of(r * SUBLANES, SUBLANES), SUBLANES)


def _bcast8(row):
    return jnp.broadcast_to(row, (SUBLANES, row.shape[-1]))


def _token_order_pick():
    tok = lax.broadcasted_iota(jnp.int32, (SSM_T, SSM_T), 0)
    row = lax.broadcasted_iota(jnp.int32, (SSM_T, SSM_T), 1)
    return (row == SUBLANES * (tok % SSM_L) + tok // SSM_L).astype(_MXU)


SCAN_UNROLL = 16


def _scan_loop(n, step, init):
    def trip(o, carry):
        for i in range(SCAN_UNROLL):
            carry = step(o * SCAN_UNROLL + i, carry)
        return carry

    return lax.fori_loop(0, n // SCAN_UNROLL, trip, init)


def _ssm_fwd(u, b_re, b_im, c_re, c_im, d_skip, coef):
    seq = u.shape[0]
    nc = seq // SSM_T
    T, L = SSM_T, SSM_L

    def body(u_ref, bre_ref, bim_ref, cre_ref, cim_ref, d_ref, are_ref, aim_ref, cfr_ref, cfi_ref, alr_ref, ali_ref,
             y_ref, yg_ref, sre_ref, sim_ref, ire_ref, iim_ref, car_re, car_im, end_re, end_im, yg_scan):
        c = pl.program_id(0)

        @pl.when(c == 0)
        def _():
            car_re[...] = jnp.zeros_like(car_re)
            car_im[...] = jnp.zeros_like(car_im)

        for j in range(SSM_JB):
            ub = u_ref[:, j * LANES:(j + 1) * LANES].astype(_MXU)
            bur = jnp.dot(ub, bre_ref[j], preferred_element_type=F32)
            bui = jnp.dot(ub, bim_ref[j], preferred_element_type=F32)
            cfr, cfi = cfr_ref[:, _scan_cols(j)], cfi_ref[:, _scan_cols(j)]
            sre_ref[:, _scan_cols(j)] = cfr * bur - cfi * bui
            sim_ref[:, _scan_cols(j)] = cfr * bui + cfi * bur

        for j in range(SSM_JB):
            cols = _scan_cols(j)
            ar, ai = _bcast8(are_ref[:, cols]), _bcast8(aim_ref[:, cols])

            def step1(r, s, cols=cols, ar=ar, ai=ai):
                sr, si = s
                rows = _rows8(r)
                return (ar * sr - ai * si + sre_ref[rows, cols], ar * si + ai * sr + sim_ref[rows, cols])

            zero = jnp.zeros((SUBLANES, SSM_SB), F32)
            er, ei = _scan_loop(L, step1, (zero, zero))
            end_re[:, cols] = er
            end_im[:, cols] = ei

        alr, ali = alr_ref[...], ali_ref[...]
        cr, ci = car_re[...], car_im[...]
        ire_ref[0:1, :] = cr
        iim_ref[0:1, :] = ci
        for i in range(1, SUBLANES):
            er, ei = end_re[i - 1:i, :], end_im[i - 1:i, :]
            cr, ci = alr * cr - ali * ci + er, alr * ci + ali * cr + ei
            ire_ref[i:i + 1, :] = cr
            iim_ref[i:i + 1, :] = ci

        for j in range(SSM_JB):
            cols = _scan_cols(j)
            ar, ai = _bcast8(are_ref[:, cols]), _bcast8(aim_ref[:, cols])

            def step2(r, s, cols=cols, ar=ar, ai=ai):
                sr, si = s
                rows = _rows8(r)
                nr = ar * sr - ai * si + sre_ref[rows, cols]
                ni = ar * si + ai * sr + sim_ref[rows, cols]
                sre_ref[rows, cols] = nr
                sim_ref[rows, cols] = ni
                return nr, ni

            _scan_loop(L, step2, (ire_ref[:, cols], iim_ref[:, cols]))

        car_re[...] = sre_ref[T - 1:T, :]
        car_im[...] = sim_ref[T - 1:T, :]

        for j in range(SSM_JB):
            cols = _scan_cols(j)
            ch = slice(j * LANES, (j + 1) * LANES)
            y = (jnp.dot(sre_ref[:, cols].astype(_MXU), cre_ref[j], preferred_element_type=F32)
                 - jnp.dot(sim_ref[:, cols].astype(_MXU), cim_ref[j], preferred_element_type=F32))
            y = y + d_ref[:, ch] * u_ref[:, ch].astype(F32)
            y_ref[:, ch] = y
            yg_scan[:, ch] = jax.nn.gelu(y).astype(yg_scan.dtype)
        yg_ref[...] = jnp.dot(_token_order_pick(), yg_scan[...], preferred_element_type=F32).astype(yg_ref.dtype)

    tok = pl.BlockSpec((T, SSM_W), lambda c: (c, 0))
    st = pl.BlockSpec((T, N_STATES), lambda c: (c, 0))
    ini = pl.BlockSpec((None, SUBLANES, N_STATES), lambda c: (c, 0, 0))
    bsp = pl.BlockSpec((SSM_JB, LANES, SSM_SB), lambda c: (0, 0, 0))
    csp = pl.BlockSpec((SSM_JB, SSM_SB, LANES), lambda c: (0, 0, 0))
    row_w = pl.BlockSpec((1, SSM_W), lambda c: (0, 0))
    row_s = pl.BlockSpec((1, N_STATES), lambda c: (0, 0))
    return pl.pallas_call(
        body, name="ssm_fwd", grid=(nc,),
        in_specs=[tok, bsp, bsp, csp, csp, row_w] + [row_s] * 6,
        out_specs=[tok, tok, st, st, ini, ini],
        out_shape=[jax.ShapeDtypeStruct((seq, SSM_W), F32), jax.ShapeDtypeStruct((seq, SSM_W), _MXU),
                   jax.ShapeDtypeStruct((seq, N_STATES), F32), jax.ShapeDtypeStruct((seq, N_STATES), F32),
                   jax.ShapeDtypeStruct((nc, SUBLANES, N_STATES), F32),
                   jax.ShapeDtypeStruct((nc, SUBLANES, N_STATES), F32)],
        scratch_shapes=[pltpu.VMEM((1, N_STATES), F32), pltpu.VMEM((1, N_STATES), F32),
                        pltpu.VMEM((SUBLANES, N_STATES), F32), pltpu.VMEM((SUBLANES, N_STATES), F32),
                        pltpu.VMEM((T, SSM_W), _MXU)],
        compiler_params=_params(("arbitrary",)),
    )(u, b_re, b_im, c_re, c_im, d_skip, *coef)


def _ssm_bwd(dyg, y, u, s_re, s_im, i_re, i_im, b_re, b_im, c_re, c_im, d_skip, coef, d_proj, deps=()):
    seq = u.shape[0]
    nc = seq // SSM_T
    T, L = SSM_T, SSM_L
    deps = list(deps) + [d_proj]

    def body(dyg_ref, y_ref, u_ref, sre_ref, sim_ref, ire_ref, iim_ref, bre_ref, bim_ref, cre_ref, cim_ref, d_ref,
             are_ref, aim_ref, cfr_ref, cfi_ref, alr_ref, ali_ref, *rest):
        (du_ref, dbre_out, dbim_out, dcre_out, dcim_out, dd_ref, dar_ref, dai_ref, dcfr_ref, dcfi_ref,
         lre, lim, car_re, car_im, end_re, end_im, ini_re, ini_im, dbre_ref, dbim_ref, dcre_ref, dcim_ref,
         dy_ref, du_scan) = rest[len(deps):]
        step = pl.program_id(0)
        dy_ref[...] = jax.vjp(jax.nn.gelu, y_ref[...])[1](dyg_ref[...])[0]

        @pl.when(step == 0)
        def _():
            car_re[...] = jnp.zeros_like(car_re)
            car_im[...] = jnp.zeros_like(car_im)
            for ref in (dbre_ref, dbim_ref, dcre_ref, dcim_ref, dd_ref, dar_ref, dai_ref, dcfr_ref, dcfi_ref):
                ref[...] = jnp.zeros_like(ref)

        for j in range(SSM_JB):
            dyb = dy_ref[:, j * LANES:(j + 1) * LANES].astype(_MXU)
            lre[:, _scan_cols(j)] = lax.dot_general(dyb, cre_ref[j], _NT, preferred_element_type=F32)
            lim[:, _scan_cols(j)] = -lax.dot_general(dyb, cim_ref[j], _NT, preferred_element_type=F32)

        for j in range(SSM_JB):
            cols = _scan_cols(j)
            ar, ai = _bcast8(are_ref[:, cols]), _bcast8(aim_ref[:, cols])

            def step1(t, s, cols=cols, ar=ar, ai=ai):
                sr, si = s
                rows = _rows8(L - 1 - t)
                return (ar * sr + ai * si + lre[rows, cols], ar * si - ai * sr + lim[rows, cols])

            zero = jnp.zeros((SUBLANES, SSM_SB), F32)
            er, ei = _scan_loop(L, step1, (zero, zero))
            end_re[:, cols] = er
            end_im[:, cols] = ei

        alr, ali = alr_ref[...], ali_ref[...]
        cr, ci = car_re[...], car_im[...]
        ini_re[SUBLANES - 1:SUBLANES, :] = cr
        ini_im[SUBLANES - 1:SUBLANES, :] = ci
        for i in range(SUBLANES - 2, -1, -1):
            er, ei = end_re[i + 1:i + 2, :], end_im[i + 1:i + 2, :]
            cr, ci = alr * cr + ali * ci + er, alr * ci - ali * cr + ei
            ini_re[i:i + 1, :] = cr
            ini_im[i:i + 1, :] = ci

        for j in range(SSM_JB):
            cols = _scan_cols(j)
            ar, ai = _bcast8(are_ref[:, cols]), _bcast8(aim_ref[:, cols])

            def step2(t, s, cols=cols, ar=ar, ai=ai):
                sr, si = s
                rows = _rows8(L - 1 - t)
                nr = ar * sr + ai * si + lre[rows, cols]
                ni = ar * si - ai * sr + lim[rows, cols]
                lre[rows, cols] = nr
                lim[rows, cols] = ni
                return nr, ni

            _scan_loop(L, step2, (ini_re[:, cols], ini_im[:, cols]))

        car_re[...] = lre[0:1, :]
        car_im[...] = lim[0:1, :]

        head, tail, body_rows = slice(0, SUBLANES), slice(SUBLANES, T), slice(0, T - SUBLANES)
        for j in range(SSM_JB):
            cols = _scan_cols(j)
            ch = slice(j * LANES, (j + 1) * LANES)
            lr, li = lre[:, cols], lim[:, cols]
            lt_r, lt_i, sp_r, sp_i = lre[tail, cols], lim[tail, cols], sre_ref[body_rows, cols], sim_ref[body_rows, cols]
            lh_r, lh_i, si_r, si_i = lre[head, cols], lim[head, cols], ire_ref[:, cols], iim_ref[:, cols]
            dar_ref[:, cols] += _colsum(lt_r * sp_r + lt_i * sp_i) + _colsum(lh_r * si_r + lh_i * si_i)
            dai_ref[:, cols] += _colsum(lt_i * sp_r - lt_r * sp_i) + _colsum(lh_i * si_r - lh_r * si_i)
            ub = u_ref[:, ch].astype(_MXU)
            uf = ub.astype(F32)
            bur = jnp.dot(ub, bre_ref[j], preferred_element_type=F32)
            bui = jnp.dot(ub, bim_ref[j], preferred_element_type=F32)
            dcfr_ref[:, cols] += _colsum(lr * bur + li * bui)
            dcfi_ref[:, cols] += _colsum(li * bur - lr * bui)
            cfr, cfi = cfr_ref[:, cols], cfi_ref[:, cols]
            dbur = (cfr * lr + cfi * li).astype(_MXU)
            dbui = (cfr * li - cfi * lr).astype(_MXU)
            dyf = dy_ref[:, ch]
            dyb = dyf.astype(_MXU)
            du = (lax.dot_general(dbur, bre_ref[j], _NT, preferred_element_type=F32)
                  + lax.dot_general(dbui, bim_ref[j], _NT, preferred_element_type=F32) + d_ref[:, ch] * dyf)
            du_scan[:, ch] = du.astype(du_scan.dtype)
            dbre_ref[j] += lax.dot_general(ub, dbur, _TN, preferred_element_type=F32)
            dbim_ref[j] += lax.dot_general(ub, dbui, _TN, preferred_element_type=F32)
            dcre_ref[j] += lax.dot_general(sre_ref[:, cols].astype(_MXU), dyb, _TN, preferred_element_type=F32)
            dcim_ref[j] -= lax.dot_general(sim_ref[:, cols].astype(_MXU), dyb, _TN, preferred_element_type=F32)
            dd_ref[:, ch] += _colsum(dyf * uf)
        du_ref[...] = jnp.dot(_token_order_pick(), du_scan[...], preferred_element_type=F32).astype(du_ref.dtype)

        @pl.when(step == nc - 1)
        def _():
            for acc, out in ((dbre_ref, dbre_out), (dbim_ref, dbim_out), (dcre_ref, dcre_out), (dcim_ref, dcim_out)):
                pltpu.sync_copy(acc, out)

    tok = pl.BlockSpec((T, SSM_W), lambda c: (nc - 1 - c, 0))
    st = pl.BlockSpec((T, N_STATES), lambda c: (nc - 1 - c, 0))
    ini = pl.BlockSpec((None, SUBLANES, N_STATES), lambda c: (nc - 1 - c, 0, 0))
    bsp = pl.BlockSpec((SSM_JB, LANES, SSM_SB), lambda c: (0, 0, 0))
    csp = pl.BlockSpec((SSM_JB, SSM_SB, LANES), lambda c: (0, 0, 0))
    row_w = pl.BlockSpec((1, SSM_W), lambda c: (0, 0))
    row_s = pl.BlockSpec((1, N_STATES), lambda c: (0, 0))
    big = pltpu.VMEM((T, N_STATES), F32)
    one = pltpu.VMEM((1, N_STATES), F32)
    eight = pltpu.VMEM((SUBLANES, N_STATES), F32)
    return pl.pallas_call(
        body, name="ssm_bwd", grid=(nc,),
        in_specs=[tok, tok, tok, st, st, ini, ini, bsp, bsp, csp, csp, row_w] + [row_s] * 6 + [_ANY] * len(deps),
        out_specs=[pl.BlockSpec((pl.Element(T), pl.Element(SSM_W)), lambda c: ((nc - 1 - c) * T, OFF_U * CW)),
                   _ANY, _ANY, _ANY, _ANY, row_w, row_s, row_s, row_s, row_s],
        input_output_aliases={18 + len(deps) - 1: 0},
        out_shape=[jax.ShapeDtypeStruct(d_proj.shape, d_proj.dtype),
                   jax.ShapeDtypeStruct((SSM_JB, LANES, SSM_SB), F32), jax.ShapeDtypeStruct((SSM_JB, LANES, SSM_SB), F32),
                   jax.ShapeDtypeStruct((SSM_JB, SSM_SB, LANES), F32), jax.ShapeDtypeStruct((SSM_JB, SSM_SB, LANES), F32),
                   jax.ShapeDtypeStruct((1, SSM_W), F32)] + [jax.ShapeDtypeStruct((1, N_STATES), F32)] * 4,
        scratch_shapes=[big, big, one, one, eight, eight, eight, eight,
                        pltpu.VMEM((SSM_JB, LANES, SSM_SB), F32), pltpu.VMEM((SSM_JB, LANES, SSM_SB), F32),
                        pltpu.VMEM((SSM_JB, SSM_SB, LANES), F32), pltpu.VMEM((SSM_JB, SSM_SB, LANES), F32),
                        pltpu.VMEM((T, SSM_W), F32), pltpu.VMEM((T, SSM_W), _MXU)],
        compiler_params=_params(("arbitrary",)),
    )(dyg, y, u, s_re, s_im, i_re, i_im, b_re, b_im, c_re, c_im, d_skip, *coef, *deps)


def _block_diag_b(b):
    t = b.reshape(SSM_JB, 8, STATE, GROUP).transpose(0, 1, 3, 2)
    eye = jnp.eye(8, dtype=b.dtype)
    return (t[:, :, :, None, :] * eye[None, :, None, :, None]).reshape(SSM_JB, LANES, SSM_SB)


def _block_diag_c(c):
    t = c.reshape(SSM_JB, 8, GROUP, STATE).transpose(0, 1, 3, 2)
    eye = jnp.eye(8, dtype=c.dtype)
    return (t[:, :, :, None, :] * eye[None, :, None, :, None]).reshape(SSM_JB, SSM_SB, LANES)


def _diag_of_b(blk):
    t = blk.reshape(SSM_JB, 8, GROUP, 8, STATE)
    d = jnp.sum(t * jnp.eye(8, dtype=blk.dtype)[None, :, None, :, None], axis=3)
    return d.transpose(0, 1, 3, 2).reshape(N_GROUPS, STATE, GROUP)


def _diag_of_c(blk):
    t = blk.reshape(SSM_JB, 8, STATE, 8, GROUP)
    d = jnp.sum(t * jnp.eye(8, dtype=blk.dtype)[None, :, None, :, None], axis=3)
    return d.transpose(0, 1, 3, 2).reshape(N_GROUPS, GROUP, STATE)


def _to_scan_order(v):
    seq, w = v.shape
    return v.reshape(seq // SSM_T, SUBLANES, SSM_L, w).transpose(0, 2, 1, 3).reshape(seq, w)


def _adamw_math(w, g, m, v):
    nm = ADAM_B1 * m + (1.0 - ADAM_B1) * g
    nv = ADAM_B2 * v + (1.0 - ADAM_B2) * jnp.square(g)
    m_hat = nm / (1.0 - ADAM_B1 ** ADAM_STEP)
    v_hat = nv / (1.0 - ADAM_B2 ** ADAM_STEP)
    return -ADAM_LR * (m_hat / (jnp.sqrt(v_hat) + ADAM_EPS) + ADAM_WD * w), nm, nv


def _adamw(w, g, m, v, *, name, tm, deps=()):
    rows, cols = w.shape
    nd = len(deps)

    def body(w_ref, g_ref, m_ref, v_ref, *rest):
        d_ref, nm_ref, nv_ref, g_out_ref = rest[nd:]
        g = g_ref[...]
        d_ref[...], nm_ref[...], nv_ref[...] = _adamw_math(w_ref[...], g, m_ref[...], v_ref[...])
        g_out_ref[...] = g

    spec = pl.BlockSpec((tm, cols), lambda i: (i, 0))
    shp = jax.ShapeDtypeStruct((rows, cols), F32)
    return pl.pallas_call(body, name=name, grid=(rows // tm,), in_specs=[spec] * 4 + [_ANY] * nd,
                          out_specs=[spec] * 4, out_shape=[shp] * 4,
                          compiler_params=_params(("arbitrary",)))(w, g, m, v, *deps)


def _place():
    x, y, c = lax.axis_index("x"), lax.axis_index("y"), lax.axis_index("c")
    chips = [(1 - x, y), (x, 1 - y), (1 - x, 1 - y)]
    return x, y, c, chips


def _remote(src, dst, send_sem, recv_sem, dev):
    return pltpu.make_async_remote_copy(src_ref=src, dst_ref=dst, send_sem=send_sem, recv_sem=recv_sem,
                                        device_id=dev, device_id_type=MESH)


def _place_shard(w, mine_arr, *, name, tm=256, deps=()):
    rows, cols = w.shape

    def body(m_ref, w_ref, *rest):
        rest[-1][...] = w_ref[...].astype(rest[-1].dtype)

    return pl.pallas_call(
        body, name=name,
        grid_spec=pltpu.PrefetchScalarGridSpec(
            num_scalar_prefetch=1, grid=(rows // tm,),
            in_specs=[pl.BlockSpec((tm, cols), lambda i, m: (i, 0))] + [_ANY] * len(deps),
            out_specs=pl.BlockSpec((None, tm, cols), lambda i, m: (m[0], i, 0))),
        out_shape=jax.ShapeDtypeStruct((N_CHIPS, rows, cols), _WIRE),
        compiler_params=_params(("arbitrary",)),
    )(mine_arr, w, *deps)


_HBM = pl.BlockSpec(memory_space=pltpu.HBM)
_SEM = pl.BlockSpec(memory_space=pltpu.SEMAPHORE)
_EFFECT = pltpu.SideEffectType.DATAFLOW_SIDE_EFFECTING


def _copies_start(name, bufs, plan, count, after=()):
    nb, na = len(bufs), len(after)

    def body(*refs):
        send_sems, recv_sems, token = refs[nb + na], refs[nb + na + 1], refs[-1]
        copies = plan(refs[:nb])
        assert len(copies) == count
        for i, (src, dst, dev, _) in enumerate(copies):
            _remote(src, dst, send_sems.at[i], recv_sems.at[i], dev).start()
        token[...] = jnp.zeros_like(token)

    res = pl.pallas_call(
        body, name=name, in_specs=[_HBM] * nb + [_ANY] * na,
        out_specs=(_SEM, _SEM, *[_HBM] * nb, pl.BlockSpec(memory_space=pltpu.VMEM)),
        out_shape=(pltpu.SemaphoreType.DMA((count,)), pltpu.SemaphoreType.DMA((count,)),
                   *[pltpu.HBM(b.shape, b.dtype) for b in bufs], jax.ShapeDtypeStruct((SUBLANES, LANES), F32)),
        input_output_aliases={i: 2 + i for i in range(nb)},
        compiler_params=pltpu.CompilerParams(has_side_effects=_EFFECT),
    )(*[pltpu.with_memory_space_constraint(b, pltpu.HBM) for b in bufs], *after)
    return (res[0], res[1]), list(res[2:2 + nb]), res[-1]


def _copies_wait(name, bufs, sems, plan, after=(), which=None):
    nb, na = len(bufs), len(after)

    def body(*refs):
        send_sems, recv_sems = refs[nb], refs[nb + 1]
        for i, (src, _, dev, land) in enumerate(plan(refs[:nb])):
            if which is not None and i not in which:
                continue
            cp = _remote(src, land, send_sems.at[i], recv_sems.at[i], dev)
            cp.wait_send()
            cp.wait_recv()

    res = pl.pallas_call(
        body, name=name, in_specs=[_HBM] * nb + [_SEM, _SEM] + [_ANY] * na, out_specs=[_HBM] * nb,
        out_shape=[pltpu.HBM(b.shape, b.dtype) for b in bufs],
        input_output_aliases={i: i for i in range(nb)},
        compiler_params=pltpu.CompilerParams(has_side_effects=_EFFECT),
    )(*bufs, *sems, *after)
    return list(res)


def _plan_gather_ici(fulls, which=(0, 1, 2)):
    x, y, c, chips = _place()
    copies = []
    for f in fulls:
        half = pl.ds(c * (f.shape[1] // 2), f.shape[1] // 2)
        own = f.at[2 * x + y, half]
        for chip in [chips[k] for k in which]:
            copies.append((own, own, (*chip, c), f.at[2 * chip[0] + chip[1], half]))
    return copies


def _plan_gather_d2d(fulls, which=(0, 1, 2)):
    x, y, c, chips = _place()
    copies = []
    for f in fulls:
        r2 = f.shape[1] // 2
        for chip in [chips[k] for k in which]:
            blk = 2 * chip[0] + chip[1]
            landed = f.at[blk, pl.ds(c * r2, r2)]
            copies.append((landed, landed, (x, y, 1 - c), f.at[blk, pl.ds((1 - c) * r2, r2)]))
    return copies


def _plan_relay_direct(fulls):
    (f,) = fulls
    x, y, c, chips = _place()
    half = pl.ds(c * (f.shape[1] // 2), f.shape[1] // 2)
    own = f.at[2 * x + y, half]
    return [(own, own, (*chip, c), f.at[2 * chip[0] + chip[1], half]) for chip in chips[:2]]


def _plan_relay_forward(fulls, k):
    (f,) = fulls
    x, y, c, chips = _place()
    r2 = f.shape[1] // 2
    half, other = pl.ds(c * r2, r2), pl.ds((1 - c) * r2, r2)
    quarter = pl.ds(c * r2 + k * (r2 // 2), r2 // 2)
    blk, far = 2 * chips[k][0] + chips[k][1], 2 * chips[2][0] + chips[2][1]
    passed, landed = f.at[blk, quarter], f.at[blk, half]
    return [(passed, passed, (*chips[1 - k], c), f.at[far, quarter]), (landed, landed, (x, y, 1 - c), f.at[blk, other])]


def _plan_relay_last(fulls):
    (f,) = fulls
    x, y, c, chips = _place()
    r2 = f.shape[1] // 2
    far = 2 * chips[2][0] + chips[2][1]
    landed = f.at[far, pl.ds(c * r2, r2)]
    return [(landed, landed, (x, y, 1 - c), f.at[far, pl.ds((1 - c) * r2, r2)])]


def _plan_swap_halves(refs):
    x, y, c, _ = _place()
    n = len(refs) // 2
    copies = []
    for g, land in zip(refs[:n], refs[n:]):
        r2 = g.shape[1] // 2
        copies.append((g.at[:, pl.ds((1 - c) * r2, r2), :], land, (x, y, 1 - c), land))
    return copies


def _plan_scatter_chips(refs):
    x, y, c, chips = _place()
    n = len(refs) // 2
    copies = []
    for h, land in zip(refs[:n], refs[n:]):
        for k, chip in enumerate(chips):
            copies.append((h.at[2 * chip[0] + chip[1]], land.at[k], (*chip, c), land.at[k]))
    return copies


def _plan_join_halves(totals):
    x, y, c, _ = _place()
    copies = []
    for t in totals:
        r2 = t.shape[0] // 2
        mine = t.at[pl.ds(c * r2, r2)]
        copies.append((mine, mine, (x, y, 1 - c), t.at[pl.ds((1 - c) * r2, r2)]))
    return copies


def _add_sibling_half(g, got, c_arr, *, name, tm):
    _, rows, cols = g.shape
    r2 = rows // 2
    nb = r2 // tm

    def body(c_ref, g_ref, r_ref, o_ref):
        o_ref[...] = (g_ref[...].astype(F32) + r_ref[...].astype(F32)).astype(o_ref.dtype)

    return pl.pallas_call(
        body, name=name,
        grid_spec=pltpu.PrefetchScalarGridSpec(
            num_scalar_prefetch=1, grid=(N_CHIPS, nb),
            in_specs=[pl.BlockSpec((None, tm, cols), lambda b, i, c: (b, c[0] * nb + i, 0)),
                      pl.BlockSpec((None, tm, cols), lambda b, i, c: (b, i, 0))],
            out_specs=pl.BlockSpec((None, tm, cols), lambda b, i, c: (b, i, 0))),
        out_shape=jax.ShapeDtypeStruct((N_CHIPS, r2, cols), _WIRE),
        compiler_params=_params(("arbitrary", "arbitrary")),
    )(c_arr, g, got)


def _add_chips(h, got, place_arr, *, name, tm):
    _, r2, cols = h.shape
    nb = r2 // tm

    def body(p_ref, h_ref, r_ref, o_ref):
        o_ref[...] = ((h_ref[...].astype(F32) + r_ref[0].astype(F32)) + r_ref[1].astype(F32)) + r_ref[2].astype(F32)

    return pl.pallas_call(
        body, name=name,
        grid_spec=pltpu.PrefetchScalarGridSpec(
            num_scalar_prefetch=1, grid=(nb,),
            in_specs=[pl.BlockSpec((None, tm, cols), lambda i, p: (p[0], i, 0)),
                      pl.BlockSpec((3, tm, cols), lambda i, p: (0, i, 0))],
            out_specs=pl.BlockSpec((tm, cols), lambda i, p: (p[1] * nb + i, 0))),
        out_shape=jax.ShapeDtypeStruct((2 * r2, cols), F32),
        compiler_params=_params(("arbitrary",)),
    )(place_arr, h, got)


class _ReduceScatter:
    def __init__(self, tag, names, grads):
        self.tag, self.names, self.n = tag, names, len(names)
        core = lax.axis_index("c").astype(jnp.int32)
        chip = (2 * lax.axis_index("x") + lax.axis_index("y")).astype(jnp.int32)
        self.c_arr, self.place_arr = core.reshape(1), jnp.stack([chip, core])
        self.bufs = list(grads)

    def _start(self, step, bufs, plan, count, after):
        self.plan = plan
        self.step = f"grad_{step}_{self.tag}"
        self.sems, self.bufs, token = _copies_start(self.step + "_start", bufs, plan, count, after)
        return [token]

    def _wait(self, after):
        self.bufs = _copies_wait(self.step + "_wait", self.bufs, self.sems, self.plan, after)
        return self.bufs

    def start_swap(self, after=()):
        lands = [lax.empty((N_CHIPS, g.shape[1] // 2, g.shape[2]), g.dtype) for g in self.bufs]
        return self._start("swap", self.bufs + lands, _plan_swap_halves, self.n, after)

    def start_scatter(self, after):
        bufs = self._wait(after)
        pair = [_add_sibling_half(g, r, self.c_arr, name=f"grad_add_sibling_{nm}", tm=min(256, g.shape[1] // 2))
                for nm, g, r in zip(self.names, bufs[:self.n], bufs[self.n:])]
        lands = [lax.empty((3,) + h.shape[1:], h.dtype) for h in pair]
        return self._start("scatter", pair + lands, _plan_scatter_chips, 3 * self.n, ())

    def start_join(self, after):
        bufs = self._wait(after)
        total = [_add_chips(h, r, self.place_arr, name=f"grad_add_chips_{nm}", tm=min(256, h.shape[1]))
                 for nm, h, r in zip(self.names, bufs[:self.n], bufs[self.n:])]
        return self._start("join", total, _plan_join_halves, self.n, ())

    def finish(self, after):
        return dict(zip(self.names, self._wait(after)))


def _all_gather_small(v):
    m_per, n = v.shape

    def body(x_ref, out_ref, send_sems, recv_sems, local_sem):
        x, y, c, chips = _place()
        me, sibling = (x, y, c), (x, y, 1 - c)

        def rows(px, py, pc):
            return out_ref.at[4 * px + 2 * py + pc]

        def copy(k, block, to, src=None):
            return _remote(rows(*block) if src is None else src, rows(*block), send_sems.at[k], recv_sems.at[k], to)

        mine = pltpu.make_async_copy(x_ref, rows(*me), local_sem)
        mine.start()
        first = [copy(0, me, sibling, src=x_ref)]
        first += [copy(1 + j, me, (*chip, c), src=x_ref) for j, chip in enumerate(chips)]
        for cp in first:
            cp.start()
        passed = [copy(4 + j, (*chip, c), sibling) for j, chip in enumerate(chips)]
        for j, chip in enumerate(chips):
            copy(1 + j, (*chip, c), me).wait_recv()
            passed[j].start()
        copy(0, sibling, me).wait_recv()
        for j, chip in enumerate(chips):
            copy(4 + j, (*chip, 1 - c), me).wait_recv()
        for cp in first + passed:
            cp.wait_send()
        mine.wait()

    return pl.pallas_call(
        body, name="gather_small_grads",
        out_shape=jax.ShapeDtypeStruct((8, m_per, n), v.dtype),
        in_specs=[pl.BlockSpec(memory_space=pltpu.VMEM)], out_specs=pl.BlockSpec(memory_space=pltpu.VMEM),
        scratch_shapes=[pltpu.SemaphoreType.DMA((7,)), pltpu.SemaphoreType.DMA((7,)), pltpu.SemaphoreType.DMA],
        compiler_params=pltpu.CompilerParams(vmem_limit_bytes=VMEM_LIMIT),
    )(v)


def _sum8(v, *, name):
    _, m, n = v.shape

    def body(v_ref, o_ref):
        acc = v_ref[0]
        for d in range(1, 8):
            acc = acc + v_ref[d]
        o_ref[...] = acc

    return pl.pallas_call(body, name=name, out_shape=jax.ShapeDtypeStruct((m, n), F32),
                          compiler_params=pltpu.CompilerParams(vmem_limit_bytes=VMEM_LIMIT))(v)


def _local_step(x, target, norm_w, q_norm_w, k_norm_w, sinks, a_re, a_im, log_dt, b_re, b_im, c_re, c_im, d_skip,
                b_glu, io):
    seq = x.shape[0]
    qw2 = jnp.tile(q_norm_w.reshape(1, HEAD_DIM), (1, HEADS_PER_TILE))
    kw2 = jnp.tile(k_norm_w.reshape(1, HEAD_DIM), (1, HEADS_PER_TILE))
    nw, bg = norm_w.reshape(1, D_MODEL), b_glu.reshape(1, D_MODEL)
    dsk = d_skip.reshape(1, SSM_W)

    h, rstd = _rms_fwd(x, nw, deps=io.begin())
    proj, w_in4 = io.projection(h)
    attn, lse, ya_in = _attn2_fwd(proj, qw2, kw2, sinks, deps=io.after_proj(proj))
    w_ap4 = io.weight("w_attn_proj", ya_in)
    w_glu4, w_sp4, w_out = io.weight("w_glu", ya_in), io.weight("w_ssm_proj", ya_in), io.weight("w_out", ya_in)
    y_a = _mm(ya_in, w_ap4, mode="nn", name="mm_attn_proj", tm=2048, tn=512, tk=ATTN_W, b_blocked=True,
              rows_outer=True, out_dtype=_MXU)

    flat_a = (a_re.reshape(1, N_STATES), a_im.reshape(1, N_STATES), jnp.repeat(log_dt, STATE).reshape(1, N_STATES))
    coef = _ssm_params_fwd(*flat_a)
    bre_blk, bim_blk = _block_diag_b(b_re).astype(_MXU), _block_diag_b(b_im).astype(_MXU)
    cre_blk, cim_blk = _block_diag_c(c_re).astype(_MXU), _block_diag_c(c_im).astype(_MXU)
    u_scan = _to_scan_order(proj[:, OFF_U * CW:OFF_U * CW + SSM_W])
    y_scan, yg, s_re, s_im, i_re, i_im = _ssm_fwd(u_scan, bre_blk, bim_blk, cre_blk, cim_blk, dsk, coef)
    glu, ys_in = _mm_glu_gate(yg, w_glu4, bg, proj)
    y_s = _mm(ys_in, w_sp4, mode="nn", name="mm_ssm_proj", tm=2048, tn=512, tk=SSM_W, b_blocked=True,
              rows_outer=True, out_dtype=_MXU)

    merged, dout, dout_b, sq = _mm_merge_out_loss(proj, y_a, y_s, w_out, x, target)
    loss = 0.5 * jnp.sum(sq) / D_MODEL

    d_ya, d_ys, d_proj = _mm_merge_bwd(dout_b, w_out, proj, y_a, y_s)
    g_w_out = _mm(merged, dout_b, mode="tn", name="mm_g_w_out", tm=1024, tn=D_MODEL, tk=2048, out_dtype=_WIRE)

    d_ya_in = _mm(d_ya, w_ap4, mode="nt", name="mm_d_attn_gate", tm=2048, tn=ATTN_W, tk=512, b_blocked=True)
    g_w_ap = _mm(ya_in, d_ya, mode="tn", name="mm_g_w_attn_proj", tm=ATTN_W, tn=D_MODEL, tk=2048, out_dtype=_WIRE,
                 out_blocked=True)

    g_w_sp = _mm(ys_in, d_ys, mode="tn", name="mm_g_w_ssm_proj", tm=SSM_W, tn=D_MODEL, tk=2048, out_dtype=_WIRE,
                 out_blocked=True)
    d_glu, d_proj, g_bglu = _mm_ssm_gate_bwd(d_ys, w_sp4, glu, bg, proj, d_proj)
    d_yg = _mm(d_glu, w_glu4, mode="nt", name="mm_d_gelu", tm=2048, tn=SSM_W, tk=512, b_blocked=True)
    g_w_glu = _mm(yg, d_glu, mode="tn", name="mm_g_w_glu", tm=SSM_W, tn=D_MODEL, tk=2048, out_dtype=_WIRE, out_blocked=True)
    dep = io.later_grads(dict(w_attn_proj=g_w_ap, w_glu=g_w_glu, w_ssm_proj=g_w_sp,
                              w_out=g_w_out.reshape(N_CHIPS, D_MODEL // N_CHIPS, D_MODEL)))

    d_proj, g_qw2, g_kw2, g_sk = _attn2_bwd(proj, qw2, kw2, sinks, lse, attn, d_ya_in, d_proj, deps=dep)
    dep = io.before_scan_backward([d_proj])
    (d_proj, g_bre, g_bim, g_cre, g_cim, g_dsk, g_abr, g_abi, g_cfr, g_cfi) = _ssm_bwd(
        _to_scan_order(d_yg), y_scan, u_scan, s_re, s_im, i_re, i_im, bre_blk, bim_blk, cre_blk, cim_blk, dsk, coef,
        d_proj, deps=dep)
    g_are, g_aim, g_ldt = _ssm_params_bwd(*flat_a, g_abr, g_abi, g_cfr, g_cfi)
    g_are, g_aim = g_are.reshape(N_GROUPS, STATE), g_aim.reshape(N_GROUPS, STATE)
    g_ldt = g_ldt.reshape(N_GROUPS, STATE).sum(axis=1)
    dep = io.before_input_projection_grad([d_proj]) + io.small_grads(dict(
        q_norm_w=g_qw2[0, :HEAD_DIM] + g_qw2[0, HEAD_DIM:], k_norm_w=g_kw2[0, :HEAD_DIM] + g_kw2[0, HEAD_DIM:],
        sinks=g_sk.reshape(N_Q_HEADS), A_re=g_are, A_im=g_aim, log_dt=g_ldt,
        B_re=_diag_of_b(g_bre), B_im=_diag_of_b(g_bim), C_re=_diag_of_c(g_cre), C_im=_diag_of_c(g_cim),
        D_skip=g_dsk.reshape(N_GROUPS, GROUP), b_glu=g_bglu.reshape(D_MODEL)))
    g_w_in = _mm(h, d_proj, mode="tn", name="mm_g_w_in", tm=1024, tn=IN_W // 4, tk=2048, out_dtype=_WIRE,
                 out_blocked=True, deps=dep)
    dep = io.input_projection_grad(g_w_in)
    d_h = _mm(d_proj, w_in4, mode="nt", name="mm_d_h", tm=1024, tn=D_MODEL, tk=IN_W // 4, b_blocked=True, deps=dep)
    grad_x, g_nw = _rms_bwd(d_h, x, rstd, nw, dout)
    return loss, grad_x, g_nw.reshape(D_MODEL)


_SMALL = ["norm_w", "q_norm_w", "k_norm_w", "sinks", "A_re", "A_im", "log_dt", "B_re", "B_im", "C_re", "C_im",
          "D_skip", "b_glu"]
_BIG = ["w_in", "w_attn_proj", "w_glu", "w_ssm_proj", "w_out"]
_LATER = _BIG[1:]
_RELATIONS = ("flip_x", "flip_y", "flip_xy")
_ORDER = ["norm_w", "w_in", "q_norm_w", "k_norm_w", "sinks", "w_attn_proj", "A_re", "A_im", "log_dt", "B_re", "B_im",
          "C_re", "C_im", "D_skip", "w_glu", "b_glu", "w_ssm_proj", "w_out"]
_PACK_W = 1024
_MINOR_SWAPPED = ("B_re", "B_im")


def _swap_minor(a):
    return jnp.swapaxes(a, 1, 2)


def _packed_rows(size):
    unit = SUBLANES * _PACK_W
    return -(-size // unit) * SUBLANES


def _pack_small(d, names):
    parts = []
    for n in names:
        flat = d[n].reshape(-1).astype(F32)
        rows = _packed_rows(flat.shape[0])
        parts.append(jnp.pad(flat, (0, rows * _PACK_W - flat.shape[0])).reshape(rows, _PACK_W))
    return jnp.concatenate(parts, axis=0)


def _unpack_small(packed, like, names):
    out, pos = {}, 0
    for n in names:
        rows = _packed_rows(like[n].size)
        out[n] = packed[pos:pos + rows].reshape(-1)[:like[n].size].reshape(like[n].shape)
        pos += rows
    return out


def _place_block(v, index_arr, *, name):
    rows, cols = v.shape

    def body(i_ref, v_ref, o_ref):
        o_ref[...] = v_ref[...]

    return pl.pallas_call(
        body, name=name,
        grid_spec=pltpu.PrefetchScalarGridSpec(
            num_scalar_prefetch=1, grid=(1,),
            in_specs=[pl.BlockSpec((rows, cols), lambda i, d: (0, 0))],
            out_specs=pl.BlockSpec((None, rows, cols), lambda i, d: (d[0], 0, 0))),
        out_shape=jax.ShapeDtypeStruct((8, rows, cols), v.dtype),
        compiler_params=_params(("arbitrary",)),
    )(index_arr, v)


def _plan_all_to_all(refs):
    (land,) = refs
    x, y, c, _ = _place()
    own = land.at[4 * x + 2 * y + c]
    copies = []
    for fx, fy, fc in [(0, 0, 1), (0, 1, 0), (0, 1, 1), (1, 0, 0), (1, 0, 1), (1, 1, 0), (1, 1, 1)]:
        px, py, pc = (1 - x) if fx else x, (1 - y) if fy else y, (1 - c) if fc else c
        copies.append((own, own, (px, py, pc), land.at[4 * px + 2 * py + pc]))
    return copies


def _adamw_whole(ws, gs, ms, vs, *, name):
    n = len(ws)

    def body(*refs):
        ins, outs = refs[:4 * n], refs[4 * n:]
        for i in range(n):
            w_ref, g_ref, m_ref, v_ref = ins[i], ins[n + i], ins[2 * n + i], ins[3 * n + i]
            outs[3 * i][...], outs[3 * i + 1][...], outs[3 * i + 2][...] = _adamw_math(
                w_ref[...], g_ref[...], m_ref[...], v_ref[...])

    out = pl.pallas_call(body, name=name, out_shape=[jax.ShapeDtypeStruct(w.shape, F32) for w in ws for _ in range(3)],
                         compiler_params=_params())(*ws, *gs, *ms, *vs)
    return [tuple(out[3 * i:3 * i + 3]) for i in range(n)]


class _Exchanges:
    def __init__(self, w, m, v):
        self.w, self.m, self.v = w, m, v
        self.grads, self.delta, self.new_m, self.new_v = {}, {}, {}, {}

    def _adamw(self, names, deps):
        for n in names:
            self.delta[n], self.new_m[n], self.new_v[n], self.grads[n] = _adamw(
                self.w[n], self.grads[n], self.m[n], self.v[n], name=f"adamw_{n}", tm=256, deps=deps)

    def begin(self):
        chip = (2 * lax.axis_index("x") + lax.axis_index("y")).astype(jnp.int32).reshape(1)
        w_in = _place_shard(self.w["w_in"], chip, name="place_w_in")
        self.w_in_sems, self.w_in_buf, token = _copies_start("gather_w_in_direct_start", [w_in], _plan_relay_direct, 2)
        self.later_full = [_place_shard(self.w[n], chip, name=f"place_{n}", deps=[token]) for n in _LATER]
        return self.later_full

    def projection(self, h):
        x, y = lax.axis_index("x"), lax.axis_index("y")
        blks = [jnp.asarray(b, jnp.int32).reshape(1)
                for b in (2 * x + y, 2 * (1 - x) + y, 2 * x + (1 - y), 2 * (1 - x) + (1 - y))]
        bufs = self.w_in_buf
        proj = _mm_chip_block(h, bufs[0], blks[0], None, name="mm_proj_own", out_dtype=_MXU)
        relay, token = [], proj
        for k, tag in enumerate(_RELATIONS[:2]):
            bufs = _copies_wait(f"gather_w_in_direct_{tag}_wait", bufs, self.w_in_sems, _plan_relay_direct, [token],
                                which=(k,))
            plan = functools.partial(_plan_relay_forward, k=k)
            sems, bufs, token = _copies_start(f"gather_w_in_relay_{tag}_start", bufs, plan, 2)
            relay.append((sems, plan))
        self.rest = _copies_start("gather_ici_rest_start", self.later_full, _plan_gather_ici, 3 * len(_LATER),
                                  after=[token])
        token = self.rest[2]
        for k, tag in enumerate(_RELATIONS[:2]):
            bufs = _copies_wait(f"gather_w_in_hand_{tag}_wait", bufs, relay[k][0], relay[k][1], [token], which=(1,))
            token = proj = _mm_chip_block(h, bufs[0], blks[1 + k], proj, name=f"mm_proj_{tag}", out_dtype=_MXU)
        for k, tag in enumerate(_RELATIONS[:2]):
            bufs = _copies_wait(f"gather_w_in_relay_{tag}_wait", bufs, relay[k][0], relay[k][1], [token], which=(0,))
        sems, bufs, token = _copies_start("gather_w_in_last_start", bufs, _plan_relay_last, 1)
        bufs = _copies_wait("gather_w_in_last_wait", bufs, sems, _plan_relay_last, [token])
        proj = _mm_chip_block(h, bufs[0], blks[3], proj, name="mm_proj_flip_xy", out_dtype=_MXU)
        return proj, bufs[0]

    def weight(self, name, after):
        if self.rest is not None:
            sems, bufs = self.rest
            later = dict(zip(_LATER, _copies_wait("gather_d2d_rest_wait", bufs, sems, _plan_gather_d2d, [after])))
            later["w_out"] = later["w_out"].reshape(D_MODEL, D_MODEL)
            self.later, self.rest = later, None
        return self.later[name]

    def after_proj(self, proj):
        sems, bufs, _ = self.rest
        bufs = _copies_wait("gather_ici_rest_wait", bufs, sems, _plan_gather_ici, [proj])
        sems, bufs, token = _copies_start("gather_d2d_rest_start", bufs, _plan_gather_d2d, 3 * len(_LATER))
        self.rest = (sems, bufs)
        return [token]

    def later_grads(self, grads):
        self.rs_later = _ReduceScatter("later", _LATER, [grads[n] for n in _LATER])
        return self.rs_later.start_swap()

    def before_scan_backward(self, after):
        return self.rs_later.start_scatter(after)

    def before_input_projection_grad(self, after):
        return self.rs_later.start_join(after)

    def input_projection_grad(self, g_w_in):
        self.grads.update(self.rs_later.finish([g_w_in]))
        self.rs_in = _ReduceScatter("w_in", ["w_in"], [g_w_in])
        self._adamw(_LATER, self.rs_in.start_swap())
        return self.rs_in.start_scatter([self.delta[n] for n in _LATER])

    def _adamw_small(self, names):
        swaps = [_swap_minor if n in _MINOR_SWAPPED else (lambda a: a) for n in names]
        ws, ms, vs = [[swap(d[n]) for n, swap in zip(names, swaps)] for d in (self.w, self.m, self.v)]
        outs = _adamw_whole(ws, [self.grads[n] for n in names], ms, vs,
                            name="adamw_small" if len(names) > 1 else f"adamw_{names[0]}")
        for n, swap, out in zip(names, swaps, outs):
            self.delta[n], self.new_m[n], self.new_v[n] = [swap(o) for o in out]
            self.grads[n] = swap(self.grads[n])

    def small_grads(self, grads):
        me = (4 * lax.axis_index("x") + 2 * lax.axis_index("y") + lax.axis_index("c")).astype(jnp.int32).reshape(1)
        grads = {n: _swap_minor(g) if n in _MINOR_SWAPPED else g for n, g in grads.items()}
        self.me = me
        land = _place_block(_pack_small(grads, _SMALL[1:]), me, name="place_small_grads")
        self.small = _copies_start("gather_small_start", [land], _plan_all_to_all, 7)
        return [self.small[2]]

    def finish(self, g_norm_w, loss, after):
        join = self.rs_in.start_join(after)
        rows = _packed_rows(g_norm_w.size)
        late = jnp.concatenate([_pack_small(dict(norm_w=g_norm_w), _SMALL[:1]),
                                jnp.pad(loss.reshape(1, 1), ((0, SUBLANES - 1), (0, _PACK_W - 1)))], axis=0)
        late_sems, late_bufs, late_token = _copies_start(
            "gather_late_start", [_place_block(late, self.me, name="place_late")], _plan_all_to_all, 7, join)
        sems, bufs, _ = self.small
        (land,) = _copies_wait("gather_small_wait", bufs, sems, _plan_all_to_all, [late_token])
        like = {n: _swap_minor(self.w[n]) if n in _MINOR_SWAPPED else self.w[n] for n in _SMALL[1:]}
        self.grads.update(_unpack_small(_sum8(land, name="sum_small_grads"), like, _SMALL[1:]))
        self._adamw_small(_SMALL[1:])
        (late_land,) = _copies_wait("gather_late_wait", late_bufs, late_sems, _plan_all_to_all, [self.delta[_SMALL[-1]]])
        late = _sum8(late_land, name="sum_norm_w_grad_and_loss")
        self.grads.update(_unpack_small(late[:rows], self.w, _SMALL[:1]))
        self._adamw_small(_SMALL[:1])
        self.grads.update(self.rs_in.finish([self.delta[_SMALL[0]]]))
        self._adamw(["w_in"], ())
        return late[rows, 0]


def kernel(x, norm_w, w_in, q_norm_w, k_norm_w, sinks, w_attn_proj, A_re, A_im, log_dt, B_re, B_im, C_re, C_im, D_skip, w_glu, b_glu, w_ssm_proj, w_out, loss_target, m_norm_w, m_w_in, m_q_norm_w, m_k_norm_w, m_sinks, m_w_attn_proj, m_A_re, m_A_im, m_log_dt, m_B_re, m_B_im, m_C_re, m_C_im, m_D_skip, m_w_glu, m_b_glu, m_w_ssm_proj, m_w_out, v_norm_w, v_w_in, v_q_norm_w, v_k_norm_w, v_sinks, v_w_attn_proj, v_A_re, v_A_im, v_log_dt, v_B_re, v_B_im, v_C_re, v_C_im, v_D_skip, v_w_glu, v_b_glu, v_w_ssm_proj, v_w_out):
    w = dict(norm_w=norm_w, w_in=w_in, q_norm_w=q_norm_w, k_norm_w=k_norm_w, sinks=sinks, w_attn_proj=w_attn_proj,
             A_re=A_re, A_im=A_im, log_dt=log_dt, B_re=B_re, B_im=B_im, C_re=C_re, C_im=C_im, D_skip=D_skip,
             w_glu=w_glu, b_glu=b_glu, w_ssm_proj=w_ssm_proj, w_out=w_out)
    m = dict(norm_w=m_norm_w, w_in=m_w_in, q_norm_w=m_q_norm_w, k_norm_w=m_k_norm_w, sinks=m_sinks,
             w_attn_proj=m_w_attn_proj, A_re=m_A_re, A_im=m_A_im, log_dt=m_log_dt, B_re=m_B_re, B_im=m_B_im,
             C_re=m_C_re, C_im=m_C_im, D_skip=m_D_skip, w_glu=m_w_glu, b_glu=m_b_glu, w_ssm_proj=m_w_ssm_proj,
             w_out=m_w_out)
    v = dict(norm_w=v_norm_w, w_in=v_w_in, q_norm_w=v_q_norm_w, k_norm_w=v_k_norm_w, sinks=v_sinks,
             w_attn_proj=v_w_attn_proj, A_re=v_A_re, A_im=v_A_im, log_dt=v_log_dt, B_re=v_B_re, B_im=v_B_im,
             C_re=v_C_re, C_im=v_C_im, D_skip=v_D_skip, w_glu=v_w_glu, b_glu=v_b_glu, w_ssm_proj=v_w_ssm_proj,
             w_out=v_w_out)

    io = _Exchanges(w, m, v)
    loss, grad_x, g_norm_w = _local_step(x[0], loss_target[0], norm_w, q_norm_w, k_norm_w, sinks, A_re, A_im, log_dt,
                                         B_re, B_im, C_re, C_im, D_skip, b_glu, io)
    loss = io.finish(g_norm_w, loss, [grad_x])
    grads, delta, new_m, new_v = io.grads, io.delta, io.new_m, io.new_v

    return (loss, grad_x[None], *[grads[n] for n in _ORDER], *[delta[n] for n in _ORDER],
            *[new_m[n] for n in _ORDER], *[new_v[n] for n in _ORDER])
```
